```python
import math
import jax, jax.numpy as jnp
from jax import lax
import numpy as np

D_MODEL = 1024
BATCH = 16
SEQ = 2048
DEPTH = 4

N_MEM = 256
HEAD_DIM = 64
N_MEM_HEADS = 4
MEM_W = N_MEM_HEADS * HEAD_DIM
MAIN_W = D_MODEL - MEM_W
MIX_W = MAIN_W + MEM_W
POOL_WINDOWS = (2, 4, 8, 16)
POOL_GROUP = MAIN_W // len(POOL_WINDOWS)
DIL_PATTERNS = ((128, 1), (512, 4), (2048, 16))
N_GROUPS = len(DIL_PATTERNS)
HEADS_PER_GROUP = MAIN_W // (N_GROUPS * HEAD_DIM)
N_DIL_HEADS = N_GROUPS * HEADS_PER_GROUP
N_A_LAYERS = DEPTH // 2
N_B_LAYERS = DEPTH - N_A_LAYERS
D_FF = ((8 * D_MODEL + 3 * 256 - 1) // (3 * 256)) * 256
ROPE_THETA = 10000.0
EPS = 1e-6
NEG = -1e30

kernel_name = "yoco_pool_dilated_hybrid"


def rms_norm(x, g):
    xf = x.astype(jnp.float32)
    y = xf * lax.rsqrt(jnp.mean(xf * xf, axis=-1, keepdims=True) + EPS)
    return (y * g.astype(jnp.float32)).astype(x.dtype)


def rope(t, positions):
    half = HEAD_DIM // 2
    freqs = ROPE_THETA ** (-jnp.arange(half, dtype=jnp.float32) / half)
    ang = positions.astype(jnp.float32)[..., None] * freqs
    cos = jnp.cos(ang)[:, :, None, :]
    sin = jnp.sin(ang)[:, :, None, :]
    tf = t.astype(jnp.float32)
    t1, t2 = tf[..., :half], tf[..., half:]
    return jnp.concatenate([t1 * cos - t2 * sin, t1 * sin + t2 * cos], axis=-1).astype(t.dtype)


def pool_mixer(u, w_pool, scale):
    S = u.shape[1]
    uf = u.astype(jnp.float32)
    c = jnp.cumsum(uf, axis=1)
    t = jnp.arange(S)
    outs = []
    for gi, w in enumerate(POOL_WINDOWS):
        sl = slice(gi * POOL_GROUP, (gi + 1) * POOL_GROUP)
        cg = c[..., sl]
        shifted = jnp.pad(cg, ((0, 0), (w, 0), (0, 0)))[:, :S]
        cnt = jnp.minimum(t + 1, w).astype(jnp.float32)[None, :, None]
        outs.append((cg - shifted) / cnt - uf[..., sl])
    p = jnp.stack(outs, axis=2).astype(u.dtype)
    y = jnp.einsum('bsgc,gce->bsge', p, w_pool).reshape(u.shape)
    return y * scale


def to_strided(t, d):
    B, S = t.shape[:2]
    return t.reshape((B, S // d, d) + t.shape[2:]).swapaxes(1, 2)


def kv_blocks(t, d, steps):
    s = to_strided(t, d)
    B, _, L, H, hd = s.shape
    nb = -(-L // steps)
    Lp = nb * steps
    s = jnp.pad(s, ((0, 0), (0, 0), (steps, Lp - L), (0, 0), (0, 0)))
    s = s.reshape(B, d, nb + 1, steps, H, hd)
    return jnp.concatenate([s[:, :, :-1], s[:, :, 1:]], axis=3)


def band_mask(nb, steps):
    i = jnp.arange(steps)[:, None] + steps
    j = jnp.arange(2 * steps)[None, :]
    rel = i - j
    m = (rel >= 0) & (rel <= steps)
    n = jnp.arange(nb)[:, None, None]
    return m[None] & ((n > 0) | (j >= steps)[None])


def dilated_group_attn(q, kk, vv, d, steps):
    B, S, H, hd = q.shape
    L = S // d
    nb = kk.shape[2]
    Lp = nb * steps
    qs = jnp.pad(to_strided(q, d), ((0, 0), (0, 0), (0, Lp - L), (0, 0), (0, 0)))
    qs = qs.reshape(B, d, nb, steps, H, hd)
    s = jnp.einsum('bdnqhe,bdnkhe->bdnhqk', qs, kk).astype(jnp.float32) * (HEAD_DIM ** -0.5)
    s = jnp.where(band_mask(nb, steps)[None, None, :, None], s, NEG)
    lse = jax.nn.logsumexp(s, axis=-1)
    p = jnp.exp(s - lse[..., None]).astype(vv.dtype)
    o = jnp.einsum('bdnhqk,bdnkhe->bdnqhe', p, vv)
    o = o.reshape(B, d, Lp, H, hd)[:, :, :L].swapaxes(1, 2).reshape(B, S, H, hd)
    lse = lse.transpose(0, 1, 2, 4, 3).reshape(B, d, Lp, H)[:, :, :L].swapaxes(1, 2).reshape(B, S, H)
    return o, lse


def build_shared_kv(x, positions, kv_norm, w_kv):
    B, S, _ = x.shape
    kv = rms_norm(x, kv_norm) @ w_kv
    k = rope(kv[..., :MAIN_W].reshape(B, S, N_DIL_HEADS, HEAD_DIM), positions)
    v = kv[..., MAIN_W:].reshape(B, S, N_DIL_HEADS, HEAD_DIM)
    k = k.reshape(B, S, N_GROUPS, HEADS_PER_GROUP, HEAD_DIM)
    v = v.reshape(B, S, N_GROUPS, HEADS_PER_GROUP, HEAD_DIM)
    shared = []
    for g, (win, d) in enumerate(DIL_PATTERNS):
        steps = win // d
        shared.append((kv_blocks(k[:, :, g], d, steps), kv_blocks(v[:, :, g], d, steps)))
    return shared


def dilated_mixer(zq, positions, shared):
    B, S, _ = zq.shape
    q = rope(zq.reshape(B, S, N_DIL_HEADS, HEAD_DIM), positions)
    q = q.reshape(B, S, N_GROUPS, HEADS_PER_GROUP, HEAD_DIM)
    outs, lses = [], []
    for g, (win, d) in enumerate(DIL_PATTERNS):
        kk, vv = shared[g]
        o, l = dilated_group_attn(q[:, :, g], kk, vv, d, win // d)
        outs.append(o)
        lses.append(l)
    o = jnp.stack(outs, axis=2)
    alpha = jax.nn.softmax(jnp.stack(lses, axis=2), axis=2)
    return (o * alpha[..., None].astype(o.dtype)).reshape(B, S, MAIN_W)


def memory_attn(zm, mem, g, w_mkv):
    B, S, _ = zm.shape
    M = mem.shape[1]
    q = zm.reshape(B, S, N_MEM_HEADS, HEAD_DIM)
    kv = rms_norm(mem, g) @ w_mkv
    km = kv[..., :MEM_W].reshape(B, M, N_MEM_HEADS, HEAD_DIM)
    vm = kv[..., MEM_W:].reshape(B, M, N_MEM_HEADS, HEAD_DIM)
    s = jnp.einsum('bshe,bmhe->bhsm', q, km).astype(jnp.float32) * (HEAD_DIM ** -0.5)
    p = jax.nn.softmax(s, axis=-1).astype(vm.dtype)
    return jnp.einsum('bhsm,bmhe->bshe', p, vm).reshape(B, S, MEM_W)


def _fwd_setup_inputs(seed: int = 0) -> dict:
    key = jax.random.key(seed)
    ks = jax.random.split(key, 14)
    f32 = jnp.float32
    nrm = lambda k, shape, fan: jax.random.normal(k, shape, f32) * (fan ** -0.5)
    return {
        "x": jax.random.normal(ks[0], (BATCH, SEQ, D_MODEL), f32),
        "mem": jax.random.normal(ks[1], (BATCH, N_MEM, D_MODEL), f32),
        "positions": jnp.broadcast_to(jnp.arange(SEQ, dtype=jnp.int32)[None], (BATCH, SEQ)),
        "norm_gains": 1.0 + 0.05 * jax.random.normal(ks[2], (DEPTH, 4, D_MODEL), f32),
        "mem_norm": 1.0 + 0.05 * jax.random.normal(ks[3], (DEPTH, D_MODEL), f32),
        "w_in": nrm(ks[4], (DEPTH, D_MODEL, MIX_W), D_MODEL),
        "w_mem_kv": nrm(ks[5], (DEPTH, D_MODEL, 2 * MEM_W), D_MODEL),
        "w_out": nrm(ks[6], (DEPTH, MIX_W, D_MODEL), MIX_W),
        "w_pool": nrm(ks[7], (N_A_LAYERS, len(POOL_WINDOWS), POOL_GROUP, POOL_GROUP), POOL_GROUP),
        "pool_scale": 1.0 + 0.1 * jax.random.normal(ks[8], (N_A_LAYERS, MAIN_W), f32),
        "kv_norm": 1.0 + 0.05 * jax.random.normal(ks[9], (D_MODEL,), f32),
        "w_kv": nrm(ks[10], (D_MODEL, 2 * MAIN_W), D_MODEL),
        "w_gate_up": nrm(ks[11], (DEPTH, D_MODEL, 2 * D_FF), D_MODEL),
        "w_down": nrm(ks[12], (DEPTH, D_FF, D_MODEL), D_FF),
    }


def _fwd_reference(x, mem, positions, norm_gains, mem_norm, w_in, w_mem_kv, w_out,
              w_pool, pool_scale, kv_norm, w_kv, w_gate_up, w_down):
    shared = None
    for l in range(DEPTH):
        h = rms_norm(x, norm_gains[l, 0])
        z = h @ w_in[l]
        z_main, z_mem = z[..., :MAIN_W], z[..., MAIN_W:]
        if l < N_A_LAYERS:
            y_main = pool_mixer(z_main, w_pool[l], pool_scale[l])
        else:
            y_main = dilated_mixer(z_main, positions, shared)
        y_mem = memory_attn(z_mem, mem, mem_norm[l], w_mem_kv[l])
        y = jnp.concatenate([y_main, y_mem], axis=-1) @ w_out[l]
        x = x + rms_norm(y, norm_gains[l, 1])
        h = rms_norm(x, norm_gains[l, 2])
        gu = h @ w_gate_up[l]
        y = (jax.nn.silu(gu[..., :D_FF]) * gu[..., D_FF:]) @ w_down[l]
        x = x + rms_norm(y, norm_gains[l, 3])
        if l == N_A_LAYERS - 1:
            shared = build_shared_kv(x, positions, kv_norm, w_kv)
    return x


import jax as _jax
import jax.numpy as _jnp

TWIN_FORMAT = 'train_step'
FWD_PARAMS = ['x', 'mem', 'positions', 'norm_gains', 'mem_norm', 'w_in', 'w_mem_kv', 'w_out', 'w_pool', 'pool_scale', 'kv_norm', 'w_kv', 'w_gate_up', 'w_down']
TWIN_WEIGHTS = ['norm_gains', 'mem_norm', 'w_in', 'w_mem_kv', 'w_out', 'w_pool', 'pool_scale', 'kv_norm', 'w_kv', 'w_gate_up', 'w_down']
TWIN_DIFF_INPUT = 'x'
TWIN_INPUTS = ['x', 'mem', 'positions', 'norm_gains', 'mem_norm', 'w_in', 'w_mem_kv', 'w_out', 'w_pool', 'pool_scale', 'kv_norm', 'w_kv', 'w_gate_up', 'w_down', 'loss_target', 'm_norm_gains', 'm_mem_norm', 'm_w_in', 'm_w_mem_kv', 'm_w_out', 'm_w_pool', 'm_pool_scale', 'm_kv_norm', 'm_w_kv', 'm_w_gate_up', 'm_w_down', 'v_norm_gains', 'v_mem_norm', 'v_w_in', 'v_w_mem_kv', 'v_w_out', 'v_w_pool', 'v_pool_scale', 'v_kv_norm', 'v_w_kv', 'v_w_gate_up', 'v_w_down']
TWIN_OUTPUTS = ['loss', 'grad_x', 'grad_norm_gains', 'grad_mem_norm', 'grad_w_in', 'grad_w_mem_kv', 'grad_w_out', 'grad_w_pool', 'grad_pool_scale', 'grad_kv_norm', 'grad_w_kv', 'grad_w_gate_up', 'grad_w_down', 'delta_norm_gains', 'delta_mem_norm', 'delta_w_in', 'delta_w_mem_kv', 'delta_w_out', 'delta_w_pool', 'delta_pool_scale', 'delta_kv_norm', 'delta_w_kv', 'delta_w_gate_up', 'delta_w_down', 'new_m_norm_gains', 'new_m_mem_norm', 'new_m_w_in', 'new_m_w_mem_kv', 'new_m_w_out', 'new_m_w_pool', 'new_m_pool_scale', 'new_m_kv_norm', 'new_m_w_kv', 'new_m_w_gate_up', 'new_m_w_down', 'new_v_norm_gains', 'new_v_mem_norm', 'new_v_w_in', 'new_v_w_mem_kv', 'new_v_w_out', 'new_v_w_pool', 'new_v_pool_scale', 'new_v_kv_norm', 'new_v_w_kv', 'new_v_w_gate_up', 'new_v_w_down']
TWIN_LEAF_KINDS = {'loss': 'loss', 'grad_x': 'grad_x', 'grad_norm_gains': 'grad_w', 'grad_mem_norm': 'grad_w', 'grad_w_in': 'grad_w', 'grad_w_mem_kv': 'grad_w', 'grad_w_out': 'grad_w', 'grad_w_pool': 'grad_w', 'grad_pool_scale': 'grad_w', 'grad_kv_norm': 'grad_w', 'grad_w_kv': 'grad_w', 'grad_w_gate_up': 'grad_w', 'grad_w_down': 'grad_w', 'delta_norm_gains': 'delta_w', 'delta_mem_norm': 'delta_w', 'delta_w_in': 'delta_w', 'delta_w_mem_kv': 'delta_w', 'delta_w_out': 'delta_w', 'delta_w_pool': 'delta_w', 'delta_pool_scale': 'delta_w', 'delta_kv_norm': 'delta_w', 'delta_w_kv': 'delta_w', 'delta_w_gate_up': 'delta_w', 'delta_w_down': 'delta_w', 'new_m_norm_gains': 'new_m', 'new_m_mem_norm': 'new_m', 'new_m_w_in': 'new_m', 'new_m_w_mem_kv': 'new_m', 'new_m_w_out': 'new_m', 'new_m_w_pool': 'new_m', 'new_m_pool_scale': 'new_m', 'new_m_kv_norm': 'new_m', 'new_m_w_kv': 'new_m', 'new_m_w_gate_up': 'new_m', 'new_m_w_down': 'new_m', 'new_v_norm_gains': 'new_v', 'new_v_mem_norm': 'new_v', 'new_v_w_in': 'new_v', 'new_v_w_mem_kv': 'new_v', 'new_v_w_out': 'new_v', 'new_v_w_pool': 'new_v', 'new_v_pool_scale': 'new_v', 'new_v_kv_norm': 'new_v', 'new_v_w_kv': 'new_v', 'new_v_w_gate_up': 'new_v', 'new_v_w_down': 'new_v'}


def _forward(args):
    return _fwd_reference(*[args[k] for k in FWD_PARAMS])


def _output_shape():
    out = _jax.eval_shape(lambda: _forward(_fwd_setup_inputs(0)))
    return out.shape, out.dtype

N_MICROBATCH = 1
ADAM_LR = 0.001
ADAM_B1 = 0.9
ADAM_B2 = 0.999
ADAM_EPS = 1e-08
ADAM_WD = 0.01
ADAM_STEP = 10
PER_EXAMPLE_BATCH_AXIS = {'x': 0, 'mem': 0, 'positions': 0, 'loss_target': 0}
SHARED_INPUTS = []
_WEIGHT_DTYPES = {'norm_gains': _jnp.float32, 'mem_norm': _jnp.float32, 'w_in': _jnp.float32, 'w_mem_kv': _jnp.float32, 'w_out': _jnp.float32, 'w_pool': _jnp.float32, 'pool_scale': _jnp.float32, 'kv_norm': _jnp.float32, 'w_kv': _jnp.float32, 'w_gate_up': _jnp.float32, 'w_down': _jnp.float32}
MOMENT_SCALE = {'norm_gains': 2.198325e+01, 'mem_norm': 2.718279e+00, 'w_in': 1.522715e+00, 'w_mem_kv': 3.488410e+00, 'w_out': 2.448914e+00, 'w_pool': 2.362278e+00, 'pool_scale': 2.702659e+00, 'kv_norm': 1.427798e+00, 'w_kv': 1.168664e+00, 'w_gate_up': 6.921838e-01, 'w_down': 1.273189e+00}


def _to_microbatches(a, axis):
    t = _jnp.moveaxis(a, axis, 0)
    t = t.reshape((N_MICROBATCH, t.shape[0] // N_MICROBATCH) + t.shape[1:])
    return _jnp.moveaxis(t, 1, axis + 1)


def setup_inputs(seed: int = 0) -> dict:
    inp = _fwd_setup_inputs(seed)
    key = _jax.random.fold_in(_jax.random.key(seed), 7919)
    shape, _ = _output_shape()
    out = dict(inp)
    out["loss_target"] = _jax.random.normal(_jax.random.fold_in(key, 0), shape, _jnp.float32)
    for i, name in enumerate(TWIN_WEIGHTS):
        w = inp[name].astype(_jnp.float32)
        if MOMENT_SCALE is None:
            s = _jnp.sqrt(_jnp.mean(_jnp.square(w)) + 1e-30)
        else:
            s = MOMENT_SCALE[name]
        km, kv = _jax.random.split(_jax.random.fold_in(key, i + 1))
        out[name] = w
        out["m_" + name] = s * _jax.random.normal(km, w.shape, _jnp.float32)
        out["v_" + name] = (s * s) * _jax.random.uniform(kv, w.shape, _jnp.float32, 0.5, 1.5)
    if N_MICROBATCH > 1:
        for name, axis in PER_EXAMPLE_BATCH_AXIS.items():
            out[name] = _to_microbatches(out[name], axis)
    return {'x': out['x'], 'mem': out['mem'], 'positions': out['positions'], 'norm_gains': out['norm_gains'], 'mem_norm': out['mem_norm'], 'w_in': out['w_in'], 'w_mem_kv': out['w_mem_kv'], 'w_out': out['w_out'], 'w_pool': out['w_pool'], 'pool_scale': out['pool_scale'], 'kv_norm': out['kv_norm'], 'w_kv': out['w_kv'], 'w_gate_up': out['w_gate_up'], 'w_down': out['w_down'], 'loss_target': out['loss_target'], 'm_norm_gains': out['m_norm_gains'], 'm_mem_norm': out['m_mem_norm'], 'm_w_in': out['m_w_in'], 'm_w_mem_kv': out['m_w_mem_kv'], 'm_w_out': out['m_w_out'], 'm_w_pool': out['m_w_pool'], 'm_pool_scale': out['m_pool_scale'], 'm_kv_norm': out['m_kv_norm'], 'm_w_kv': out['m_w_kv'], 'm_w_gate_up': out['m_w_gate_up'], 'm_w_down': out['m_w_down'], 'v_norm_gains': out['v_norm_gains'], 'v_mem_norm': out['v_mem_norm'], 'v_w_in': out['v_w_in'], 'v_w_mem_kv': out['v_w_mem_kv'], 'v_w_out': out['v_w_out'], 'v_w_pool': out['v_w_pool'], 'v_pool_scale': out['v_pool_scale'], 'v_kv_norm': out['v_kv_norm'], 'v_w_kv': out['v_w_kv'], 'v_w_gate_up': out['v_w_gate_up'], 'v_w_down': out['v_w_down']}


def _loss(weights, diff, rest, loss_target):
    with _jax.named_scope("forward"):
        args = {**rest, TWIN_DIFF_INPUT: diff, **{k: w.astype(_WEIGHT_DTYPES[k]) for k, w in weights.items()}}
        y = _forward(args)
    with _jax.named_scope("loss_head"):
        err = _jnp.square(y.astype(_jnp.float32) - loss_target)
        return 0.5 * _jnp.sum(_jnp.mean(err, axis=-1)) if err.ndim else 0.5 * err


def _adamw(w, g, m, v):
    m = ADAM_B1 * m + (1.0 - ADAM_B1) * g
    v = ADAM_B2 * v + (1.0 - ADAM_B2) * _jnp.square(g)
    m_hat = m / (1.0 - ADAM_B1 ** ADAM_STEP)
    v_hat = v / (1.0 - ADAM_B2 ** ADAM_STEP)
    delta = -ADAM_LR * (m_hat / (_jnp.sqrt(v_hat) + ADAM_EPS) + ADAM_WD * w)
    return delta, m, v


def reference(x, mem, positions, norm_gains, mem_norm, w_in, w_mem_kv, w_out, w_pool, pool_scale, kv_norm, w_kv, w_gate_up, w_down, loss_target, m_norm_gains, m_mem_norm, m_w_in, m_w_mem_kv, m_w_out, m_w_pool, m_pool_scale, m_kv_norm, m_w_kv, m_w_gate_up, m_w_down, v_norm_gains, v_mem_norm, v_w_in, v_w_mem_kv, v_w_out, v_w_pool, v_pool_scale, v_kv_norm, v_w_kv, v_w_gate_up, v_w_down):
    given = dict(x=x, mem=mem, positions=positions, norm_gains=norm_gains, mem_norm=mem_norm, w_in=w_in, w_mem_kv=w_mem_kv, w_out=w_out, w_pool=w_pool, pool_scale=pool_scale, kv_norm=kv_norm, w_kv=w_kv, w_gate_up=w_gate_up, w_down=w_down, loss_target=loss_target, m_norm_gains=m_norm_gains, m_mem_norm=m_mem_norm, m_w_in=m_w_in, m_w_mem_kv=m_w_mem_kv, m_w_out=m_w_out, m_w_pool=m_w_pool, m_pool_scale=m_pool_scale, m_kv_norm=m_kv_norm, m_w_kv=m_w_kv, m_w_gate_up=m_w_gate_up, m_w_down=m_w_down, v_norm_gains=v_norm_gains, v_mem_norm=v_mem_norm, v_w_in=v_w_in, v_w_mem_kv=v_w_mem_kv, v_w_out=v_w_out, v_w_pool=v_w_pool, v_pool_scale=v_pool_scale, v_kv_norm=v_kv_norm, v_w_kv=v_w_kv, v_w_gate_up=v_w_gate_up, v_w_down=v_w_down)
    weights = {n: given[n] for n in TWIN_WEIGHTS}
    shared = {n: given[n] for n in SHARED_INPUTS}
    per_example = {n: given[n] for n in ['x', 'mem', 'positions']}
    grad_fn = _jax.value_and_grad(_loss, argnums=(0, 1))

    def one_microbatch(ex, loss_target):
        ex = dict(ex)
        diff = ex.pop(TWIN_DIFF_INPUT)
        return grad_fn(weights, diff, {**shared, **ex}, loss_target)

    if N_MICROBATCH == 1:
        loss, (grad_w, grad_x) = one_microbatch(per_example, given["loss_target"])
    else:
        def body(carry, xs):
            loss_sum, grad_sum = carry
            l_k, (gw_k, gx_k) = one_microbatch(xs[0], xs[1])
            with _jax.named_scope("update"):
                return (loss_sum + l_k, _jax.tree.map(_jnp.add, grad_sum, gw_k)), gx_k

        init = (_jnp.zeros((), _jnp.float32), _jax.tree.map(_jnp.zeros_like, weights))
        (loss, grad_w), grad_x = _jax.lax.scan(body, init, (per_example, given["loss_target"]))
    with _jax.named_scope("update"):
        delta_w, new_m, new_v = {}, {}, {}
        for n in TWIN_WEIGHTS:
            delta_w[n], new_m[n], new_v[n] = _adamw(weights[n], grad_w[n], given["m_" + n], given["v_" + n])
    return (loss, grad_x, *[grad_w[n] for n in TWIN_WEIGHTS], *[delta_w[n] for n in TWIN_WEIGHTS],
            *[new_m[n] for n in TWIN_WEIGHTS], *[new_v[n] for n in TWIN_WEIGHTS])
```

```python
import functools

import jax
import jax.numpy as jnp
from jax import lax
from jax.experimental import pallas as pl
from jax.experimental.pallas import tpu as pltpu

F32 = jnp.float32
BF = jnp.bfloat16

D_MODEL = 1024
DEPTH = 4
N_A = 2
HEAD_DIM = 64
MEM_W = 256
MAIN_W = 768
D_FF = 2816
N_MEM = 256
POOL_WINDOWS = (2, 4, 8, 16)
POOL_GROUP = 192
DIL = (1, 4, 16)
STEPS = 128
ROPE_THETA = 10000.0
EPS = 1e-6
SCALE = HEAD_DIM ** -0.5
NEG = -1e30

ADAM_LR = 0.001
ADAM_B1 = 0.9
ADAM_B2 = 0.999
ADAM_EPS = 1e-08
ADAM_WD = 0.01
ADAM_STEP = 10

VMEM_LIMIT = 48 * 1024 * 1024
MESH = pl.DeviceIdType.MESH


def _cp(sem):
    return pltpu.CompilerParams(dimension_semantics=sem, vmem_limit_bytes=VMEM_LIMIT)


def _mm(a, b, *, name, ta=False, tb=False, tm=1024, tn=512, tk=1024, b_layer=None, b_offsets=(0,),
        extras=(), epilogue=None, out_dtypes=(F32,), out_n=None, stack=None):
    a_pair = isinstance(a, (tuple, list))
    b_pair = isinstance(b, (tuple, list))
    a0 = a[0] if a_pair else a
    b0 = b[0] if b_pair else b
    a_rows, a_cols = a0.shape
    if a_pair:
        a_cols *= 2
    b_rows, b_cols = b0.shape[-2:]
    if b_pair:
        b_cols *= 2
    M, K = (a_cols, a_rows) if ta else (a_rows, a_cols)
    N = b_rows if tb else b_cols
    if out_n is not None:
        N = out_n
    tm, tn, tk = min(tm, M), min(tn, N), min(tk, K)
    assert M % tm == 0 and N % tn == 0 and K % tk == 0, (name, M, N, K, tm, tn, tk)
    nk = K // tk
    n_acc = len(b_offsets)

    if a_pair:
        a_half = (a0.shape[1] // (tm if ta else tk))
    if b_pair:
        b_half = (b0.shape[1] // (tk if tb else tn))

    def a_map(sel):
        def f(i, j, k):
            r, c = (k, i) if ta else (i, k)
            if a_pair:
                c = jnp.clip(c - sel * a_half, 0, a_half - 1)
            return (r, c)
        return f

    def b_map(sel, off):
        def f(i, j, k):
            r, c = (j + off, k) if tb else (k, j + off)
            if b_pair:
                c = jnp.clip(c - sel * b_half, 0, b_half - 1)
            if b_layer is not None:
                return (b_layer, r, c)
            return (r, c)
        return f

    a_blk = (tk, tm) if ta else (tm, tk)
    b_blk = (tn, tk) if tb else (tk, tn)
    if b_layer is not None:
        b_blk = (None,) + b_blk
    in_specs, operands = [], []
    for sel in range(2 if a_pair else 1):
        in_specs.append(pl.BlockSpec(a_blk, a_map(sel)))
        operands.append(a[sel] if a_pair else a)
    n_a = len(operands)
    for off in b_offsets:
        for sel in range(2 if b_pair else 1):
            in_specs.append(pl.BlockSpec(b_blk, b_map(sel, off)))
            operands.append(b[sel] if b_pair else b)
    n_b = len(operands) - n_a
    for arr, kind in extras:
        if kind == 'tile':
            in_specs.append(pl.BlockSpec((tm, tn), lambda i, j, k: (i, j)))
        elif kind == 'row':
            in_specs.append(pl.BlockSpec((tm, 1), lambda i, j, k: (i, 0)))
        else:
            in_specs.append(pl.BlockSpec((1, tn), lambda i, j, k: (0, j)))
        operands.append(arr)
    n_e = len(extras)
    n_o = len(out_dtypes)
    dims = (((0,) if ta else (1,), (1,) if tb else (0,)), ((), ()))

    def body(*refs):
        a_refs = refs[:n_a]
        b_refs = refs[n_a:n_a + n_b]
        e_refs = refs[n_a + n_b:n_a + n_b + n_e]
        n_in = n_a + n_b + n_e + (1 if stack is not None else 0)
        o_refs = refs[n_in:n_in + n_o]
        acc_refs = refs[n_in + n_o:]
        i, j, k = pl.program_id(0), pl.program_id(1), pl.program_id(2)
        if a_pair:
            cidx = i if ta else k
            av = jnp.where(cidx < a_half, a_refs[0][...], a_refs[1][...])
        else:
            av = a_refs[0][...]
        av = av.astype(BF)
        prods = []
        for q in range(n_acc):
            if b_pair:
                cidx = (k if tb else j) + b_offsets[q]
                bv = jnp.where(cidx < b_half, b_refs[2 * q][...], b_refs[2 * q + 1][...])
            else:
                bv = b_refs[q][...]
            prods.append(lax.dot_general(av, bv.astype(BF), dims, preferred_element_type=F32))

        def finish(accs):
            outs = epilogue(accs, *[r[...] for r in e_refs]) if epilogue is not None else accs
            for o_ref, o in zip(o_refs, outs):
                o_ref[...] = o.astype(o_ref.dtype)

        if nk == 1:
            finish(prods)
        else:
            @pl.when(k == 0)
            def _():
                for r, p in zip(acc_refs, prods):
                    r[...] = p

            @pl.when(k > 0)
            def _():
                for r, p in zip(acc_refs, prods):
                    r[...] += p

            @pl.when(k == nk - 1)
            def _():
                finish([r[...] for r in acc_refs])

    if stack is not None:
        buf, layer = stack
        assert n_o == 1 and buf.shape[1:] == (M, N)
        return pl.pallas_call(
            body, name=name,
            grid=(M // tm, N // tn, nk),
            in_specs=in_specs + [pl.BlockSpec(memory_space=pl.ANY)],
            out_specs=[pl.BlockSpec((None, tm, tn), lambda i, j, k: (layer, i, j))],
            out_shape=[jax.ShapeDtypeStruct(buf.shape, buf.dtype)],
            scratch_shapes=[pltpu.VMEM((tm, tn), F32) for _ in range(n_acc if nk > 1 else 0)],
            input_output_aliases={len(operands): 0},
            compiler_params=_cp(("parallel", "parallel", "arbitrary")),
        )(*operands, buf)[0]
    return pl.pallas_call(
        body, name=name,
        grid=(M // tm, N // tn, nk),
        in_specs=in_specs,
        out_specs=[pl.BlockSpec((tm, tn), lambda i, j, k: (i, j)) for _ in range(n_o)],
        out_shape=[jax.ShapeDtypeStruct((M, N), dt) for dt in out_dtypes],
        scratch_shapes=[pltpu.VMEM((tm, tn), F32) for _ in range(n_acc if nk > 1 else 0)],
        compiler_params=_cp(("parallel", "parallel", "arbitrary")),
    )(*operands)


def _norm_fwd(x, g, *, name, res=None, out_dtype=F32, tm=512):
    T, Dm = x.shape
    has_res = res is not None

    def body(*refs):
        if has_res:
            x_ref, g_ref, r_ref, y_ref, s_ref = refs
        else:
            x_ref, g_ref, y_ref, s_ref = refs
        xv = x_ref[...]
        rstd = lax.rsqrt(jnp.mean(xv * xv, axis=-1, keepdims=True) + EPS)
        y = xv * rstd * g_ref[...]
        if has_res:
            y = r_ref[...] + y
        y_ref[...] = y.astype(y_ref.dtype)
        s_ref[...] = rstd

    row = pl.BlockSpec((tm, Dm), lambda i: (i, 0))
    in_specs = [row, pl.BlockSpec((1, Dm), lambda i: (0, 0))] + ([row] if has_res else [])
    ops = [x, g.reshape(1, Dm)] + ([res] if has_res else [])
    return pl.pallas_call(
        body, name=name, grid=(T // tm,), in_specs=in_specs,
        out_specs=[row, pl.BlockSpec((tm, 1), lambda i: (i, 0))],
        out_shape=[jax.ShapeDtypeStruct((T, Dm), out_dtype), jax.ShapeDtypeStruct((T, 1), F32)],
        compiler_params=_cp(("parallel",)),
    )(*ops)


def _norm_bwd(dout, x, rstd, g, *, name, add=None, out_dtype=F32, tm=512):
    T, Dm = x.shape
    has_add = add is not None
    nt = T // tm

    def body(*refs):
        if has_add:
            do_ref, x_ref, s_ref, g_ref, a_ref, dx_ref, dg_ref, acc = refs
        else:
            do_ref, x_ref, s_ref, g_ref, dx_ref, dg_ref, acc = refs
        i = pl.program_id(0)
        do = do_ref[...].astype(F32)
        xh = x_ref[...] * s_ref[...]
        gd = do * g_ref[...]
        dx = s_ref[...] * (gd - xh * jnp.mean(gd * xh, axis=-1, keepdims=True))
        if has_add:
            dx = dx + a_ref[...].astype(F32)
        dx_ref[...] = dx.astype(dx_ref.dtype)
        part = jnp.sum((do * xh).reshape(tm // 8, 8, Dm), axis=0)

        @pl.when(i == 0)
        def _():
            acc[...] = part

        @pl.when(i > 0)
        def _():
            acc[...] += part

        @pl.when(i == nt - 1)
        def _():
            dg_ref[...] = jnp.sum(acc[...], axis=0, keepdims=True)

    row = pl.BlockSpec((tm, Dm), lambda i: (i, 0))
    in_specs = [row, row, pl.BlockSpec((tm, 1), lambda i: (i, 0)), pl.BlockSpec((1, Dm), lambda i: (0, 0))]
    ops = [dout, x, rstd, g.reshape(1, Dm)]
    if has_add:
        in_specs.append(row)
        ops.append(add)
    return pl.pallas_call(
        body, name=name, grid=(nt,), in_specs=in_specs,
        out_specs=[row, pl.BlockSpec((1, Dm), lambda i: (0, 0))],
        out_shape=[jax.ShapeDtypeStruct((T, Dm), out_dtype), jax.ShapeDtypeStruct((1, Dm), F32)],
        scratch_shapes=[pltpu.VMEM((8, Dm), F32)],
        compiler_params=_cp(("arbitrary",)),
    )(*ops)


def _swiglu_fwd_epilogue(accs):
    g, u = accs
    return g, u, g * jax.nn.sigmoid(g) * u


def _swiglu_bwd_epilogue(accs, g, u):
    da = accs[0]
    g = g.astype(F32)
    u = u.astype(F32)
    sig = jax.nn.sigmoid(g)
    return da * u * (sig * (1.0 + g * (1.0 - sig))), da * (g * sig)


def _rope_tables(pos, *, name, tm=1024):
    T = pos.shape[0]
    half = HEAD_DIM // 2
    freqs = ROPE_THETA ** (-jnp.arange(half, dtype=F32) / half)
    freqs = jnp.tile(freqs, 4).reshape(1, 128)

    def body(p_ref, f_ref, c_ref, s_ref):
        ang = p_ref[...].astype(F32) * f_ref[...]
        lane = lax.broadcasted_iota(jnp.int32, ang.shape, 1)
        c_ref[...] = jnp.cos(ang)
        s_ref[...] = jnp.where(lane % HEAD_DIM < half, -1.0, 1.0) * jnp.sin(ang)

    tab = pl.BlockSpec((tm, 128), lambda i: (i, 0))
    return pl.pallas_call(
        body, name=name, grid=(T // tm,),
        in_specs=[pl.BlockSpec((tm, 1), lambda i: (i, 0)), pl.BlockSpec((1, 128), lambda i: (0, 0))],
        out_specs=[tab, tab],
        out_shape=[jax.ShapeDtypeStruct((T, 128), F32)] * 2,
        compiler_params=_cp(("parallel",)),
    )(pos, freqs)


def _rot(x, cos, sin, sign):
    W = x.shape[1]
    half = HEAD_DIM // 2
    reps = W // 128
    c = jnp.concatenate([cos] * reps, axis=1) if reps > 1 else cos
    s = jnp.concatenate([sin] * reps, axis=1) if reps > 1 else sin
    lane = lax.broadcasted_iota(jnp.int32, x.shape, 1)
    swapped = jnp.where(lane % HEAD_DIM < half, pltpu.roll(x, W - half, axis=1), pltpu.roll(x, half, axis=1))
    return x * c + (sign * s) * swapped


def _rope_apply(x, cos, sin, *, name, sign=1.0, width=MAIN_W, passthrough=False, out_dtype=BF, alias=None,
                out_cols=None, tm=512):
    T = x.shape[0]

    def body(*refs):
        if passthrough:
            x_ref, v_ref, c_ref, s_ref, o_ref, ov_ref = refs
            ov_ref[...] = v_ref[...].astype(ov_ref.dtype)
        elif alias is not None:
            x_ref, c_ref, s_ref, _, o_ref = refs
        else:
            x_ref, c_ref, s_ref, o_ref = refs
        o_ref[...] = _rot(x_ref[...].astype(F32), c_ref[...], s_ref[...], sign).astype(o_ref.dtype)

    blk0 = pl.BlockSpec((tm, width), lambda i: (i, 0))
    blk1 = pl.BlockSpec((tm, width), lambda i: (i, 1))
    tab = pl.BlockSpec((tm, 128), lambda i: (i, 0))
    if passthrough:
        return pl.pallas_call(
            body, name=name, grid=(T // tm,), in_specs=[blk0, blk1, tab, tab], out_specs=[blk0, blk0],
            out_shape=[jax.ShapeDtypeStruct((T, width), out_dtype)] * 2,
            compiler_params=_cp(("parallel",)),
        )(x, x, cos, sin)
    if alias is not None:
        return pl.pallas_call(
            body, name=name, grid=(T // tm,),
            in_specs=[blk0, tab, tab, pl.BlockSpec(memory_space=pl.ANY)], out_specs=blk0,
            out_shape=jax.ShapeDtypeStruct(alias.shape, alias.dtype),
            input_output_aliases={3: 0},
            compiler_params=_cp(("parallel",)),
        )(x, cos, sin, alias)
    return pl.pallas_call(
        body, name=name, grid=(T // tm,), in_specs=[blk0, tab, tab], out_specs=blk0,
        out_shape=jax.ShapeDtypeStruct((T, width), out_dtype),
        compiler_params=_cp(("parallel",)),
    )(x, cos, sin)


POOL_T = 256
POOL_HALO = 16


def _pool_lane_window(shape):
    lane = lax.broadcasted_iota(jnp.int32, shape, 1)
    w = jnp.full(shape, POOL_WINDOWS[0], jnp.int32)
    for gi in range(1, len(POOL_WINDOWS)):
        w = jnp.where(lane >= gi * POOL_GROUP, POOL_WINDOWS[gi], w)
    return w


def _pool_fwd(z, wbd, scale, B, S, *, name):
    T = z.shape[0]
    nt = S // POOL_T
    hb = POOL_T // POOL_HALO

    def body(z_ref, h_ref, w_ref, sc_ref, y_ref, p_ref, ext):
        i = pl.program_id(1)
        u = z_ref[...]
        ext[pl.ds(POOL_HALO, POOL_T), :] = u
        ext[pl.ds(0, POOL_HALO), :] = jnp.where(i > 0, h_ref[...], 0.0)
        win = _pool_lane_window((POOL_T, MAIN_W))
        acc = u
        for k in range(1, POOL_HALO):
            acc = acc + jnp.where(k < win, ext[pl.ds(POOL_HALO - k, POOL_T), :], 0.0)
        t = i * POOL_T + lax.broadcasted_iota(jnp.int32, (POOL_T, MAIN_W), 0)
        cnt = jnp.minimum(t + 1, win).astype(F32)
        p = (acc / cnt - u).astype(BF)
        p_ref[...] = p
        y = jnp.dot(p, w_ref[...], preferred_element_type=F32) * sc_ref[...]
        y_ref[...] = y.astype(y_ref.dtype)

    return pl.pallas_call(
        body, name=name, grid=(B, nt),
        in_specs=[pl.BlockSpec((POOL_T, MAIN_W), lambda b, i: (b * nt + i, 0)),
                  pl.BlockSpec((POOL_HALO, MAIN_W), lambda b, i: (jnp.maximum((b * nt + i) * hb - 1, 0), 0)),
                  pl.BlockSpec((MAIN_W, MAIN_W), lambda b, i: (0, 0)),
                  pl.BlockSpec((1, MAIN_W), lambda b, i: (0, 0))],
        out_specs=[pl.BlockSpec((POOL_T, MAIN_W), lambda b, i: (b * nt + i, 0)),
                   pl.BlockSpec((POOL_T, MAIN_W), lambda b, i: (b * nt + i, 0))],
        out_shape=[jax.ShapeDtypeStruct((T, D_MODEL), BF), jax.ShapeDtypeStruct((T, MAIN_W), BF)],
        scratch_shapes=[pltpu.VMEM((POOL_T + POOL_HALO, MAIN_W), F32)],
        compiler_params=_cp(("parallel", "parallel")),
    )(z, z, wbd, scale.reshape(1, MAIN_W))


def _pool_bwd(dy, p, wbd, scale, dz_alias, B, S, *, name):
    T = dy.shape[0]
    nt = S // POOL_T
    hb = POOL_T // POOL_HALO
    last_halo = T // POOL_HALO - 1
    R = POOL_T + POOL_HALO

    def body(dy_ref, dyn_ref, p_ref, pn_ref, w_ref, sc_ref, _, dz_ref, dw_ref, ds_ref, ext, dw_acc, ds_acc):
        b, i = pl.program_id(0), pl.program_id(1)
        first = jnp.logical_and(b == 0, i == 0)
        dyv = dy_ref[...]
        pv = p_ref[...]
        sc = sc_ref[...]
        w = w_ref[...]
        pw = jnp.dot(pv, w, preferred_element_type=F32)
        ds_part = jnp.sum((dyv * pw).reshape(POOL_T // 8, 8, MAIN_W), axis=0)
        dpw = (dyv * sc).astype(BF)
        dw_part = lax.dot_general(pv, dpw, (((0,), (0,)), ((), ())), preferred_element_type=F32)

        @pl.when(first)
        def _():
            dw_acc[...] = dw_part
            ds_acc[...] = ds_part

        @pl.when(jnp.logical_not(first))
        def _():
            dw_acc[...] += dw_part
            ds_acc[...] += ds_part

        @pl.when(jnp.logical_and(b == pl.num_programs(0) - 1, i == nt - 1))
        def _():
            dw_ref[...] = dw_acc[...]
            ds_ref[...] = jnp.sum(ds_acc[...], axis=0, keepdims=True)

        dp = lax.dot_general(dpw, w, (((1,), (1,)), ((), ())), preferred_element_type=F32)
        dpn = lax.dot_general((dyn_ref[...] * sc).astype(BF), w, (((1,), (1,)), ((), ())), preferred_element_type=F32)
        win = _pool_lane_window((POOL_T, MAIN_W))
        win_n = _pool_lane_window((POOL_HALO, MAIN_W))
        t = i * POOL_T + lax.broadcasted_iota(jnp.int32, (POOL_T, MAIN_W), 0)
        tn = (i + 1) * POOL_T + lax.broadcasted_iota(jnp.int32, (POOL_HALO, MAIN_W), 0)
        ext[pl.ds(0, POOL_T), :] = dp / jnp.minimum(t + 1, win).astype(F32)
        ext[pl.ds(POOL_T, POOL_HALO), :] = jnp.where(i < nt - 1, dpn / jnp.minimum(tn + 1, win_n).astype(F32), 0.0)
        acc = -dp
        for k in range(POOL_HALO):
            acc = acc + jnp.where(k < win, ext[pl.ds(k, POOL_T), :], 0.0)
        dz_ref[...] = acc.astype(dz_ref.dtype)

    cur = lambda b, i: (b * nt + i, 0)
    nxt = lambda b, i: (jnp.minimum((b * nt + i + 1) * hb, last_halo), 0)
    return pl.pallas_call(
        body, name=name, grid=(B, nt),
        in_specs=[pl.BlockSpec((POOL_T, MAIN_W), cur), pl.BlockSpec((POOL_HALO, MAIN_W), nxt),
                  pl.BlockSpec((POOL_T, MAIN_W), cur), pl.BlockSpec((POOL_HALO, MAIN_W), nxt),
                  pl.BlockSpec((MAIN_W, MAIN_W), lambda b, i: (0, 0)),
                  pl.BlockSpec((1, MAIN_W), lambda b, i: (0, 0)),
                  pl.BlockSpec(memory_space=pl.ANY)],
        out_specs=[pl.BlockSpec((POOL_T, MAIN_W), cur),
                   pl.BlockSpec((MAIN_W, MAIN_W), lambda b, i: (0, 0)),
                   pl.BlockSpec((1, MAIN_W), lambda b, i: (0, 0))],
        out_shape=[jax.ShapeDtypeStruct(dz_alias.shape, dz_alias.dtype),
                   jax.ShapeDtypeStruct((MAIN_W, MAIN_W), F32), jax.ShapeDtypeStruct((1, MAIN_W), F32)],
        scratch_shapes=[pltpu.VMEM((R, MAIN_W), F32), pltpu.VMEM((MAIN_W, MAIN_W), F32), pltpu.VMEM((8, MAIN_W), F32)],
        input_output_aliases={6: 0},
        compiler_params=_cp(("arbitrary", "arbitrary")),
    )(dy, dy, p, p, wbd, scale.reshape(1, MAIN_W), dz_alias)


def _head_masks(shape):
    lane = lax.broadcasted_iota(jnp.int32, shape, 1)
    return [(lane // HEAD_DIM) == h for h in range(shape[1] // HEAD_DIM)]


def _row_of(bcast, mask):
    return jnp.max(jnp.where(mask, bcast, -jnp.inf), axis=-1, keepdims=True)


MEM_TQ = 512


def _memattn_fwd(z, kv, y_alias, B, S, *, name):
    T = z.shape[0]
    nt = S // MEM_TQ

    def body(q_ref, k_ref, v_ref, _, y_ref, l_ref):
        q = q_ref[...]
        k = k_ref[...]
        v = v_ref[...]
        masks = _head_masks(q.shape)
        o = jnp.zeros(q.shape, F32)
        lse_b = jnp.zeros(q.shape, F32)
        for m in masks:
            qm = jnp.where(m, q, 0.0).astype(BF)
            s = lax.dot_general(qm, k, (((1,), (1,)), ((), ())), preferred_element_type=F32) * SCALE
            mx = jnp.max(s, axis=-1, keepdims=True)
            e = jnp.exp(s - mx)
            l = jnp.sum(e, axis=-1, keepdims=True)
            p = (e / l).astype(BF)
            o = o + jnp.where(m, jnp.dot(p, v, preferred_element_type=F32), 0.0)
            lse_b = lse_b + jnp.where(m, mx + jnp.log(l), 0.0)
        y_ref[...] = o.astype(y_ref.dtype)
        l_ref[...] = lse_b

    qblk = pl.BlockSpec((MEM_TQ, MEM_W), lambda b, i: (b * nt + i, 3))
    return pl.pallas_call(
        body, name=name, grid=(B, nt),
        in_specs=[qblk, pl.BlockSpec((N_MEM, MEM_W), lambda b, i: (b, 0)), pl.BlockSpec((N_MEM, MEM_W), lambda b, i: (b, 1)),
                  pl.BlockSpec(memory_space=pl.ANY)],
        out_specs=[qblk, pl.BlockSpec((MEM_TQ, MEM_W), lambda b, i: (b * nt + i, 0))],
        out_shape=[jax.ShapeDtypeStruct(y_alias.shape, y_alias.dtype), jax.ShapeDtypeStruct((T, MEM_W), F32)],
        input_output_aliases={3: 0},
        compiler_params=_cp(("parallel", "parallel")),
    )(z, kv, kv, y_alias)


def _memattn_bwd(dy, z, kv, lse, dz_alias, B, S, *, name):
    nt = S // MEM_TQ

    def body(do_ref, q_ref, k_ref, v_ref, l_ref, _, dz_ref, dk_ref, dv_ref, dk_acc, dv_acc):
        i = pl.program_id(1)
        do = do_ref[...]
        q = q_ref[...]
        k = k_ref[...]
        v = v_ref[...]
        lse_b = l_ref[...]
        masks = _head_masks(q.shape)
        dq = jnp.zeros(q.shape, F32)
        dk = jnp.zeros(k.shape, F32)
        dv = jnp.zeros(v.shape, F32)
        for m in masks:
            qm = jnp.where(m, q, 0.0).astype(BF)
            dom = jnp.where(m, do, 0.0).astype(BF)
            s = lax.dot_general(qm, k, (((1,), (1,)), ((), ())), preferred_element_type=F32) * SCALE
            p = jnp.exp(s - _row_of(lse_b, m))
            dp = lax.dot_general(dom, v, (((1,), (1,)), ((), ())), preferred_element_type=F32)
            delta = jnp.sum(p * dp, axis=-1, keepdims=True)
            ds = (p * (dp - delta) * SCALE).astype(BF)
            pb = p.astype(BF)
            dv = dv + jnp.where(m[:N_MEM], lax.dot_general(pb, dom, (((0,), (0,)), ((), ())), preferred_element_type=F32), 0.0)
            dk = dk + jnp.where(m[:N_MEM], lax.dot_general(ds, qm, (((0,), (0,)), ((), ())), preferred_element_type=F32), 0.0)
            dq = dq + jnp.where(m, jnp.dot(ds, k, preferred_element_type=F32), 0.0)
        dz_ref[...] = dq.astype(dz_ref.dtype)

        @pl.when(i == 0)
        def _():
            dk_acc[...] = dk
            dv_acc[...] = dv

        @pl.when(i > 0)
        def _():
            dk_acc[...] += dk
            dv_acc[...] += dv

        @pl.when(i == nt - 1)
        def _():
            dk_ref[...] = dk_acc[...]
            dv_ref[...] = dv_acc[...]

    qblk = pl.BlockSpec((MEM_TQ, MEM_W), lambda b, i: (b * nt + i, 3))
    kblk = pl.BlockSpec((N_MEM, MEM_W), lambda b, i: (b, 0))
    return pl.pallas_call(
        body, name=name, grid=(B, nt),
        in_specs=[qblk, qblk, kblk, pl.BlockSpec((N_MEM, MEM_W), lambda b, i: (b, 1)),
                  pl.BlockSpec((MEM_TQ, MEM_W), lambda b, i: (b * nt + i, 0)), pl.BlockSpec(memory_space=pl.ANY)],
        out_specs=[qblk, kblk, kblk],
        out_shape=[jax.ShapeDtypeStruct(dz_alias.shape, dz_alias.dtype),
                   jax.ShapeDtypeStruct((B * N_MEM, MEM_W), F32), jax.ShapeDtypeStruct((B * N_MEM, MEM_W), F32)],
        scratch_shapes=[pltpu.VMEM((N_MEM, MEM_W), F32), pltpu.VMEM((N_MEM, MEM_W), F32)],
        input_output_aliases={5: 0},
        compiler_params=_cp(("parallel", "arbitrary")),
    )(dy, z, kv, kv, lse, dz_alias)


def _dil_scores(qm, kp, kc, n):
    qi = lax.broadcasted_iota(jnp.int32, (STEPS, STEPS), 0)
    kj = lax.broadcasted_iota(jnp.int32, (STEPS, STEPS), 1)
    sp = lax.dot_general(qm, kp, (((1,), (1,)), ((), ())), preferred_element_type=F32) * SCALE
    sc = lax.dot_general(qm, kc, (((1,), (1,)), ((), ())), preferred_element_type=F32) * SCALE
    sp = jnp.where(jnp.logical_and(kj >= qi, n > 0), sp, NEG)
    sc = jnp.where(kj <= qi, sc, NEG)
    return sp, sc


def _dil_specs(g, d, nb):
    cur = lambda b, r, n: (b, n, r * 3 + g)
    prev = lambda b, r, n: (b, jnp.maximum(n - 1, 0), r * 3 + g)
    return pl.BlockSpec((None, STEPS, 256), cur), pl.BlockSpec((None, STEPS, 256), prev)


def _dil_fwd_group(g, q, k, v, o_alias, l_alias, B, S, *, name):
    d = DIL[g]
    L = S // d
    nb = L // STEPS
    view = lambda t: t.reshape(B, L, d * MAIN_W)

    def body(q_ref, kp_ref, kc_ref, vp_ref, vc_ref, _, __, o_ref, l_ref):
        n = pl.program_id(2)
        q = q_ref[...]
        kp, kc, vp, vc = kp_ref[...], kc_ref[...], vp_ref[...], vc_ref[...]
        o = jnp.zeros(q.shape, F32)
        lse_b = jnp.zeros(q.shape, F32)
        for m in _head_masks(q.shape):
            qm = jnp.where(m, q, 0).astype(BF)
            sp, sc = _dil_scores(qm, kp, kc, n)
            mx = jnp.maximum(jnp.max(sp, axis=-1, keepdims=True), jnp.max(sc, axis=-1, keepdims=True))
            ep = jnp.exp(sp - mx)
            ec = jnp.exp(sc - mx)
            l = jnp.sum(ep, axis=-1, keepdims=True) + jnp.sum(ec, axis=-1, keepdims=True)
            lse = mx + jnp.log(l)
            pp = jnp.exp(sp - lse).astype(BF)
            pc = jnp.exp(sc - lse).astype(BF)
            oh = jnp.dot(pp, vp, preferred_element_type=F32) + jnp.dot(pc, vc, preferred_element_type=F32)
            o = o + jnp.where(m, oh, 0.0)
            lse_b = lse_b + jnp.where(m, lse, 0.0)
        o_ref[...] = o
        l_ref[...] = lse_b

    cur, prev = _dil_specs(g, d, nb)
    anyspec = pl.BlockSpec(memory_space=pl.ANY)
    o, l = pl.pallas_call(
        body, name=name, grid=(B, d, nb),
        in_specs=[cur, prev, cur, prev, cur, anyspec, anyspec],
        out_specs=[cur, cur],
        out_shape=[jax.ShapeDtypeStruct((B, L, d * MAIN_W), F32)] * 2,
        input_output_aliases={5: 0, 6: 1},
        compiler_params=_cp(("parallel", "parallel", "parallel")),
    )(view(q), view(k), view(k), view(v), view(v), view(o_alias), view(l_alias))
    return o.reshape(B * S, MAIN_W), l.reshape(B * S, MAIN_W)


def _dil_bwd_group(g, q, k, v, do, cb, lse, aliases, B, S, *, name):
    d = DIL[g]
    L = S // d
    nb = L // STEPS
    view = lambda t: t.reshape(B, L, d * MAIN_W)

    def body(q_ref, kp_ref, kc_ref, vp_ref, vc_ref, do_ref, c_ref, l_ref, *rest):
        dq_ref, dkc_ref, dkp_ref, dvc_ref, dvp_ref = rest[5:]
        n = pl.program_id(2)
        q = q_ref[...]
        kp, kc, vp, vc = kp_ref[...], kc_ref[...], vp_ref[...], vc_ref[...]
        do = do_ref[...]
        cbv = c_ref[...]
        lse_b = l_ref[...]
        z = jnp.zeros(q.shape, F32)
        dq, dkc, dkp, dvc, dvp = z, z, z, z, z
        tdot = lambda a, b: lax.dot_general(a, b, (((0,), (0,)), ((), ())), preferred_element_type=F32)
        ndot = lambda a, b: lax.dot_general(a, b, (((1,), (1,)), ((), ())), preferred_element_type=F32)
        for m in _head_masks(q.shape):
            qm = jnp.where(m, q, 0).astype(BF)
            dom = jnp.where(m, do, 0).astype(BF)
            sp, sc = _dil_scores(qm, kp, kc, n)
            lse = _row_of(lse_b, m)
            c = _row_of(cbv, m)
            pp = jnp.exp(sp - lse)
            pc = jnp.exp(sc - lse)
            dsp = (pp * (ndot(dom, vp) + c) * SCALE).astype(BF)
            dsc = (pc * (ndot(dom, vc) + c) * SCALE).astype(BF)
            dq = dq + jnp.where(m, jnp.dot(dsp, kp, preferred_element_type=F32) + jnp.dot(dsc, kc, preferred_element_type=F32), 0.0)
            dkp = dkp + jnp.where(m, tdot(dsp, qm), 0.0)
            dkc = dkc + jnp.where(m, tdot(dsc, qm), 0.0)
            dvp = dvp + jnp.where(m, tdot(pp.astype(BF), dom), 0.0)
            dvc = dvc + jnp.where(m, tdot(pc.astype(BF), dom), 0.0)
        dq_ref[...] = dq
        dkc_ref[...] = dkc
        dkp_ref[...] = dkp
        dvc_ref[...] = dvc
        dvp_ref[...] = dvp

    cur, prev = _dil_specs(g, d, nb)
    anyspec = pl.BlockSpec(memory_space=pl.ANY)
    outs = pl.pallas_call(
        body, name=name, grid=(B, d, nb),
        in_specs=[cur, prev, cur, prev, cur, cur, cur, cur] + [anyspec] * 5,
        out_specs=[cur] * 5,
        out_shape=[jax.ShapeDtypeStruct((B, L, d * MAIN_W), F32)] * 5,
        input_output_aliases={8: 0, 9: 1, 10: 2, 11: 3, 12: 4},
        compiler_params=_cp(("parallel", "parallel", "parallel")),
    )(view(q), view(k), view(k), view(v), view(v), view(do), view(cb), view(lse), *[view(t) for t in aliases])
    return tuple(t.reshape(B * S, MAIN_W) for t in outs)


def _group_softmax(lse):
    l0, l1, l2 = lse[:, 0:256], lse[:, 256:512], lse[:, 512:768]
    mx = jnp.maximum(jnp.maximum(l0, l1), l2)
    e0, e1, e2 = jnp.exp(l0 - mx), jnp.exp(l1 - mx), jnp.exp(l2 - mx)
    tot = e0 + e1 + e2
    return e0 / tot, e1 / tot, e2 / tot


def _dil_combine_fwd(o, lse, y_alias, *, name, tm=512):
    T = o.shape[0]

    def body(o_ref, l_ref, _, y_ref):
        a = jnp.concatenate(_group_softmax(l_ref[...]), axis=1)
        y_ref[...] = (o_ref[...] * a).astype(y_ref.dtype)

    blk = pl.BlockSpec((tm, MAIN_W), lambda i: (i, 0))
    return pl.pallas_call(
        body, name=name, grid=(T // tm,), in_specs=[blk, blk, pl.BlockSpec(memory_space=pl.ANY)], out_specs=blk,
        out_shape=jax.ShapeDtypeStruct(y_alias.shape, y_alias.dtype), input_output_aliases={2: 0},
        compiler_params=_cp(("parallel",)),
    )(o, lse, y_alias)


def _dil_combine_bwd(dy, o, lse, *, name, tm=256):
    T = o.shape[0]
    lane_r = lax.broadcasted_iota(jnp.int32, (256, 256), 0) // HEAD_DIM
    lane_c = lax.broadcasted_iota(jnp.int32, (256, 256), 1) // HEAD_DIM
    ones_bd = (lane_r == lane_c).astype(BF)

    def body(dy_ref, o_ref, l_ref, e_ref, do_ref, c_ref):
        dyv = dy_ref[...]
        alphas = _group_softmax(l_ref[...])
        prod = dyv * o_ref[...]
        e = e_ref[...]
        tot = jnp.zeros((tm, 256), F32)
        for gi in range(3):
            x = prod[:, gi * 256:(gi + 1) * 256]
            hi = x.astype(BF)
            lo = (x - hi.astype(F32)).astype(BF)
            dalpha = jnp.dot(hi, e, preferred_element_type=F32) + jnp.dot(lo, e, preferred_element_type=F32)
            tot = tot + alphas[gi] * dalpha
        a = jnp.concatenate(alphas, axis=1)
        do_ref[...] = (dyv * a).astype(do_ref.dtype)
        c_ref[...] = jnp.concatenate([-al * tot for al in alphas], axis=1)

    blk = pl.BlockSpec((tm, MAIN_W), lambda i: (i, 0))
    return pl.pallas_call(
        body, name=name, grid=(T // tm,),
        in_specs=[blk, blk, blk, pl.BlockSpec((256, 256), lambda i: (0, 0))], out_specs=[blk, blk],
        out_shape=[jax.ShapeDtypeStruct((T, MAIN_W), BF), jax.ShapeDtypeStruct((T, MAIN_W), F32)],
        compiler_params=_cp(("parallel",)),
    )(dy, o, lse, ones_bd)


def _kv_grad(parts, cos, sin, B, S, *, name):
    T = B * S
    tb = S // STEPS
    n_l = len(parts)

    def shifted(g):
        def f(b, t):
            return (b * tb + jnp.minimum(t + DIL[g], tb - 1), g)
        return f

    def body(*refs):
        c_ref, s_ref = refs[0], refs[1]
        ins = refs[2:2 + n_l * 8]
        dk_ref, dv_ref = refs[2 + n_l * 8:]
        t = pl.program_id(1)
        dk = jnp.zeros((STEPS, MAIN_W), F32)
        dv = jnp.zeros((STEPS, MAIN_W), F32)
        for li in range(n_l):
            kc, vc = ins[li * 8], ins[li * 8 + 1]
            dk = dk + kc[...]
            dv = dv + vc[...]
            kparts, vparts = [], []
            for g in range(3):
                ok = t + DIL[g] < tb
                kparts.append(jnp.where(ok, ins[li * 8 + 2 + g][...], 0.0))
                vparts.append(jnp.where(ok, ins[li * 8 + 5 + g][...], 0.0))
            dk = dk + jnp.concatenate(kparts, axis=1)
            dv = dv + jnp.concatenate(vparts, axis=1)
        dk_ref[...] = _rot(dk, c_ref[...], s_ref[...], -1.0).astype(dk_ref.dtype)
        dv_ref[...] = dv.astype(dv_ref.dtype)

    full = pl.BlockSpec((STEPS, MAIN_W), lambda b, t: (b * tb + t, 0))
    tab = pl.BlockSpec((STEPS, 128), lambda b, t: (b * tb + t, 0))
    in_specs, ops = [tab, tab], [cos, sin]
    for (kc, kp, vc, vp) in parts:
        in_specs += [full, full] + [pl.BlockSpec((STEPS, 256), shifted(g)) for g in range(3)] * 2
        ops += [kc, vc, kp, kp, kp, vp, vp, vp]
    return pl.pallas_call(
        body, name=name, grid=(B, tb), in_specs=in_specs, out_specs=[full, full],
        out_shape=[jax.ShapeDtypeStruct((T, MAIN_W), BF)] * 2,
        compiler_params=_cp(("parallel", "parallel")),
    )(*ops)


def _loss(y, target, *, name, tm=512):
    T, Dm = y.shape
    nt = T // tm

    def body(y_ref, t_ref, l_ref, d_ref, acc):
        i = pl.program_id(0)
        err = y_ref[...] - t_ref[...]
        d_ref[...] = err / Dm
        part = jnp.sum(jnp.mean(err * err, axis=-1, keepdims=True).reshape(tm // 8, 8, 1), axis=0)

        @pl.when(i == 0)
        def _():
            acc[...] = part

        @pl.when(i > 0)
        def _():
            acc[...] += part

        @pl.when(i == nt - 1)
        def _():
            l_ref[...] = 0.5 * jnp.sum(acc[...], axis=0, keepdims=True)

    row = pl.BlockSpec((tm, Dm), lambda i: (i, 0))
    return pl.pallas_call(
        body, name=name, grid=(nt,), in_specs=[row, row],
        out_specs=[pl.BlockSpec((1, 1), lambda i: (0, 0)), row],
        out_shape=[jax.ShapeDtypeStruct((1, 1), F32), jax.ShapeDtypeStruct((T, Dm), F32)],
        scratch_shapes=[pltpu.VMEM((8, 1), F32)],
        compiler_params=_cp(("arbitrary",)),
    )(y, target)


def _adamw(w, g, m, v, *, name):
    shape = w.shape
    cols = shape[-1]
    rows = w.size // cols
    tm = rows
    for cand in (512, 352, 256, 128):
        if rows > cand and rows % cand == 0 and cand * cols * 4 <= (1 << 20):
            tm = cand
            break

    def body(w_ref, g_ref, m_ref, v_ref, d_ref, mo_ref, vo_ref):
        gv = g_ref[...]
        mn = ADAM_B1 * m_ref[...] + (1.0 - ADAM_B1) * gv
        vn = ADAM_B2 * v_ref[...] + (1.0 - ADAM_B2) * (gv * gv)
        m_hat = mn / (1.0 - ADAM_B1 ** ADAM_STEP)
        v_hat = vn / (1.0 - ADAM_B2 ** ADAM_STEP)
        d_ref[...] = -ADAM_LR * (m_hat / (jnp.sqrt(v_hat) + ADAM_EPS) + ADAM_WD * w_ref[...])
        mo_ref[...] = mn
        vo_ref[...] = vn

    blk = pl.BlockSpec((tm, cols), lambda i: (i, 0))
    outs = pl.pallas_call(
        body, name=name, grid=(rows // tm,), in_specs=[blk] * 4, out_specs=[blk] * 3,
        out_shape=[jax.ShapeDtypeStruct((rows, cols), F32)] * 3,
        compiler_params=_cp(("parallel",)),
    )(*[t.reshape(rows, cols) for t in (w, g, m, v)])
    return tuple(t.reshape(shape) for t in outs)


BIG = {
    'w_in': ((DEPTH, D_MODEL, D_MODEL), 'row'),
    'w_mem_kv': ((DEPTH, D_MODEL, 2 * MEM_W), 'row'),
    'w_out': ((DEPTH, D_MODEL, D_MODEL), 'row'),
    'w_kv': ((1, D_MODEL, 2 * MAIN_W), 'col'),
    'w_gate_up': ((DEPTH, D_MODEL, 2 * D_FF), 'col'),
    'w_down': ((DEPTH, D_FF, D_MODEL), 'row'),
}
BIG_NAMES = tuple(BIG)
N_CHIPS = 4
HBM_ANY = pl.BlockSpec(memory_space=pl.ANY)


def _geom(name):
    (L, R, C), kind = BIG[name]
    if kind == 'row':
        return L, R, C, kind, R // N_CHIPS, C, R // (2 * N_CHIPS)
    return L, R, C, kind, R, C // N_CHIPS, R // 2


def _shard_shape(name):
    L, R, C, kind, rs, cs, rh = _geom(name)
    return (L, rs, cs)


def _half_shape(name):
    L, R, C, kind, rs, cs, rh = _geom(name)
    return (L, rh, cs)


def _full_win(ref, name, s, h):
    L, R, C, kind, rs, cs, rh = _geom(name)
    if kind == 'row':
        rows = pl.ds(s * rs, rs) if h is None else pl.ds(s * rs + h * rh, rh)
        return ref.at[:, rows, :]
    rows = slice(None) if h is None else pl.ds(h * rh, rh)
    return ref.at[:, rows, pl.ds(s * cs, cs)]


def _shard_half(ref, name, h):
    L, R, C, kind, rs, cs, rh = _geom(name)
    return ref.at[:, pl.ds(h * rh, rh), :]


def _halves_win(ref, name, s):
    L, R, C, kind, rs, cs, rh = _geom(name)
    if kind == 'row':
        return ref.at[:, pl.ds(s * rh, rh), :]
    return ref.at[:, :, pl.ds(s * cs, cs)]


def _halves_shape(name):
    L, R, C, kind, rs, cs, rh = _geom(name)
    return (L, N_CHIPS * rh, cs) if kind == 'row' else (L, rh, C)


def _place():
    x, y, c = lax.axis_index("x"), lax.axis_index("y"), lax.axis_index("c")
    chips = [(1 - x, y), (x, 1 - y), (1 - x, 1 - y)]
    return x, y, c, chips


SMALL_ROWS = 24


def _all_gather(shards, small):
    names = BIG_NAMES
    nw = len(names)

    def body(*refs):
        src = dict(zip(names, refs[:nw]))
        small_ref = refs[nw]
        dst = dict(zip(names, refs[nw + 1:2 * nw + 1]))
        small_out = refs[2 * nw + 1]
        send_sems, recv_sems, local_sems = refs[2 * nw + 2:]
        x, y, c, chips = _place()
        s = 2 * x + y
        sib = (x, y, 1 - c)

        def remote(k, src_ref, dst_ref, to):
            return pltpu.make_async_remote_copy(src_ref=src_ref, dst_ref=dst_ref, send_sem=send_sems.at[k],
                                                recv_sem=recv_sems.at[k], device_id=to, device_id_type=MESH)

        local = []
        for wi, nm in enumerate(names):
            local.append(pltpu.make_async_copy(src[nm], _full_win(dst[nm], nm, s, None), local_sems.at[wi]))
        local.append(pltpu.make_async_copy(small_ref, small_out.at[s], local_sems.at[nw]))
        for cp in local:
            cp.start()
        sends = []
        for j, (px, py) in enumerate(chips):
            for wi, nm in enumerate(names):
                sends.append(remote(wi * 6 + j, _shard_half(src[nm], nm, c), _full_win(dst[nm], nm, s, c), (px, py, c)))
            sends.append(remote(nw * 6 + j, small_ref, small_out.at[s], (px, py, c)))
        for cp in sends:
            cp.start()
        for j, (px, py) in enumerate(chips):
            sp = 2 * px + py
            for wi, nm in enumerate(names):
                w = _full_win(dst[nm], nm, sp, c)
                remote(wi * 6 + j, w, w, sib).wait_recv()
                fwd = remote(wi * 6 + 3 + j, w, w, sib)
                fwd.start()
                sends.append(fwd)
            remote(nw * 6 + j, small_ref, small_out.at[sp], sib).wait_recv()
        for j, (px, py) in enumerate(chips):
            sp = 2 * px + py
            for wi, nm in enumerate(names):
                w = _full_win(dst[nm], nm, sp, 1 - c)
                remote(wi * 6 + 3 + j, w, w, sib).wait_recv()
        for cp in sends:
            cp.wait_send()
        for cp in local:
            cp.wait()

    n_sem = nw * 6 + 3
    outs = pl.pallas_call(
        body, name="all_gather_weights",
        in_specs=[HBM_ANY] * (nw + 1), out_specs=[HBM_ANY] * (nw + 1),
        out_shape=[jax.ShapeDtypeStruct(BIG[nm][0], BF) for nm in names]
        + [jax.ShapeDtypeStruct((N_CHIPS, SMALL_ROWS, 256), F32)],
        scratch_shapes=[pltpu.SemaphoreType.DMA((n_sem,)), pltpu.SemaphoreType.DMA((n_sem,)),
                        pltpu.SemaphoreType.DMA((nw + 1,))],
    )(*[shards[nm] for nm in names], small)
    return dict(zip(names, outs[:nw])), outs[nw]


def _exchange_halves(grads):
    names = BIG_NAMES
    nw = len(names)

    def body(*refs):
        src = dict(zip(names, refs[:nw]))
        dst = dict(zip(names, refs[nw:2 * nw]))
        send_sems, recv_sems = refs[2 * nw:]
        x, y, c, _ = _place()
        sib = (x, y, 1 - c)
        sends, recvs = [], []
        k = 0
        for nm in names:
            kind = BIG[nm][1]
            for sp in range(N_CHIPS if kind == 'row' else 1):
                if kind == 'row':
                    out_w = _full_win(src[nm], nm, sp, 1 - c)
                    in_w = _halves_win(dst[nm], nm, sp)
                else:
                    L, R, C, _, rs, cs, rh = _geom(nm)
                    out_w = src[nm].at[:, pl.ds((1 - c) * rh, rh), :]
                    in_w = dst[nm]
                cp = pltpu.make_async_remote_copy(src_ref=out_w, dst_ref=in_w, send_sem=send_sems.at[k],
                                                  recv_sem=recv_sems.at[k], device_id=sib, device_id_type=MESH)
                cp.start()
                sends.append(cp)
                k += 1
        for cp in sends:
            cp.wait_recv()
        for cp in sends:
            cp.wait_send()

    n_sem = sum(N_CHIPS if BIG[nm][1] == 'row' else 1 for nm in names)
    outs = pl.pallas_call(
        body, name="exchange_grad_halves",
        in_specs=[HBM_ANY] * nw, out_specs=[HBM_ANY] * nw,
        out_shape=[jax.ShapeDtypeStruct(_halves_shape(nm), F32) for nm in names],
        scratch_shapes=[pltpu.SemaphoreType.DMA((n_sem,)), pltpu.SemaphoreType.DMA((n_sem,))],
    )(*[grads[nm] for nm in names])
    return dict(zip(names, outs))


def _add_halves(name, g, r, sc):
    L, R, C, kind, rs, cs, rh = _geom(name)
    tr = rh if kind == 'row' else 256
    nr = rh // tr

    def body(sc_ref, g_ref, r_ref, hb_ref, own_ref):
        sp = pl.program_id(2)
        tot = g_ref[...] + r_ref[...]
        hb_ref[...] = tot.astype(hb_ref.dtype)

        @pl.when(sp == sc_ref[0])
        def _():
            own_ref[...] = tot

    if kind == 'row':
        g_map = lambda l, ri, sp, sc_ref: (l, sp * 2 + sc_ref[1], 0)
        h_map = lambda l, ri, sp, sc_ref: (l, sp, 0)
    else:
        g_map = lambda l, ri, sp, sc_ref: (l, sc_ref[1] * nr + ri, sp)
        h_map = lambda l, ri, sp, sc_ref: (l, ri, sp)
    own_map = lambda l, ri, sp, sc_ref: (l, ri, 0)
    blk = (None, tr, cs)
    return pl.pallas_call(
        body, name="add_halves_" + name,
        grid_spec=pltpu.PrefetchScalarGridSpec(
            num_scalar_prefetch=1, grid=(L, nr, N_CHIPS),
            in_specs=[pl.BlockSpec(blk, g_map), pl.BlockSpec(blk, h_map)],
            out_specs=[pl.BlockSpec(blk, h_map), pl.BlockSpec(blk, own_map)]),
        out_shape=[jax.ShapeDtypeStruct(_halves_shape(name), BF), jax.ShapeDtypeStruct(_half_shape(name), F32)],
        compiler_params=_cp(("parallel", "parallel", "arbitrary")),
    )(sc, g, r)


def _scatter_to_chips(halves):
    names = BIG_NAMES
    nw = len(names)

    def body(*refs):
        src = dict(zip(names, refs[:nw]))
        dst = dict(zip(names, refs[nw:2 * nw]))
        send_sems, recv_sems = refs[2 * nw:]
        x, y, c, chips = _place()
        sends = []
        for j, (px, py) in enumerate(chips):
            sp = 2 * px + py
            for wi, nm in enumerate(names):
                cp = pltpu.make_async_remote_copy(
                    src_ref=_halves_win(src[nm], nm, sp), dst_ref=dst[nm].at[j], send_sem=send_sems.at[wi * 3 + j],
                    recv_sem=recv_sems.at[wi * 3 + j], device_id=(px, py, c), device_id_type=MESH)
                cp.start()
                sends.append(cp)
        for cp in sends:
            cp.wait_recv()
        for cp in sends:
            cp.wait_send()

    outs = pl.pallas_call(
        body, name="scatter_grads_to_chips",
        in_specs=[HBM_ANY] * nw, out_specs=[HBM_ANY] * nw,
        out_shape=[jax.ShapeDtypeStruct((3,) + _half_shape(nm), BF) for nm in names],
        scratch_shapes=[pltpu.SemaphoreType.DMA((nw * 3,)), pltpu.SemaphoreType.DMA((nw * 3,))],
    )(*[halves[nm] for nm in names])
    return dict(zip(names, outs))


def _sum_pieces(name, own, pieces, sc):
    L, R, C, kind, rs, cs, rh = _geom(name)
    tr = rh if kind == 'row' else 256
    nr = rh // tr

    def body(sc_ref, o_ref, p_ref, out_ref):
        out_ref[...] = o_ref[...] + p_ref[0].astype(F32) + p_ref[1].astype(F32) + p_ref[2].astype(F32)

    blk = (None, tr, cs)
    return pl.pallas_call(
        body, name="sum_pieces_" + name,
        grid_spec=pltpu.PrefetchScalarGridSpec(
            num_scalar_prefetch=1, grid=(L, nr),
            in_specs=[pl.BlockSpec(blk, lambda l, ri, sc_ref: (l, ri, 0)),
                      pl.BlockSpec((3, None, tr, cs), lambda l, ri, sc_ref: (0, l, ri, 0))],
            out_specs=pl.BlockSpec(blk, lambda l, ri, sc_ref: (l, sc_ref[1] * nr + ri, 0))),
        out_shape=jax.ShapeDtypeStruct(_shard_shape(name), F32),
        compiler_params=_cp(("parallel", "parallel")),
    )(sc, own, pieces)


def _share_with_sibling(gshards):
    names = BIG_NAMES
    nw = len(names)

    def body(*refs):
        bufs = dict(zip(names, refs[nw:2 * nw]))
        send_sems, recv_sems = refs[2 * nw:]
        x, y, c, _ = _place()
        sib = (x, y, 1 - c)
        sends = []
        for wi, nm in enumerate(names):
            w = _shard_half(bufs[nm], nm, c)
            cp = pltpu.make_async_remote_copy(src_ref=w, dst_ref=w, send_sem=send_sems.at[wi], recv_sem=recv_sems.at[wi],
                                              device_id=sib, device_id_type=MESH)
            cp.start()
            sends.append(cp)
        for wi, nm in enumerate(names):
            w = _shard_half(bufs[nm], nm, 1 - c)
            pltpu.make_async_remote_copy(src_ref=w, dst_ref=w, send_sem=send_sems.at[wi], recv_sem=recv_sems.at[wi],
                                         device_id=sib, device_id_type=MESH).wait_recv()
        for cp in sends:
            cp.wait_send()

    outs = pl.pallas_call(
        body, name="share_grad_shards",
        in_specs=[HBM_ANY] * nw, out_specs=[HBM_ANY] * nw,
        out_shape=[jax.ShapeDtypeStruct(_shard_shape(nm), F32) for nm in names],
        input_output_aliases={i: i for i in range(nw)},
        scratch_shapes=[pltpu.SemaphoreType.DMA((nw,)), pltpu.SemaphoreType.DMA((nw,))],
    )(*[gshards[nm] for nm in names])
    return dict(zip(names, outs))


def _all_gather_small(v):
    rows = v.shape[0]

    def body(v_ref, out_ref, send_sems, recv_sems, local_sem):
        x, y, c, _ = _place()
        me = 4 * x + 2 * y + c
        mine = pltpu.make_async_copy(v_ref, out_ref.at[me], local_sem)
        mine.start()
        sends = []
        flips = [(fx, fy, fc) for fx in (0, 1) for fy in (0, 1) for fc in (0, 1)][1:]
        for k, (fx, fy, fc) in enumerate(flips):
            px, py, pc = (1 - x if fx else x), (1 - y if fy else y), (1 - c if fc else c)
            cp = pltpu.make_async_remote_copy(src_ref=v_ref, dst_ref=out_ref.at[me], send_sem=send_sems.at[k],
                                              recv_sem=recv_sems.at[k], device_id=(px, py, pc), device_id_type=MESH)
            cp.start()
            sends.append((cp, 4 * px + 2 * py + pc))
        for k, (cp, peer) in enumerate(sends):
            pltpu.make_async_remote_copy(src_ref=v_ref, dst_ref=out_ref.at[peer], send_sem=send_sems.at[k],
                                         recv_sem=recv_sems.at[k], device_id=(x, y, c), device_id_type=MESH).wait_recv()
        for cp, _ in sends:
            cp.wait_send()
        mine.wait()

    return pl.pallas_call(
        body, name="all_gather_small_grads",
        in_specs=[HBM_ANY], out_specs=HBM_ANY,
        out_shape=jax.ShapeDtypeStruct((8, rows, 128), F32),
        scratch_shapes=[pltpu.SemaphoreType.DMA((7,)), pltpu.SemaphoreType.DMA((7,)), pltpu.SemaphoreType.DMA],
    )(v)


def _sum8(v8, *, name, tr=336):
    rows = v8.shape[1]
    tr = min(tr, rows)
    assert rows % tr == 0

    def body(v_ref, o_ref):
        tot = v_ref[0]
        for d in range(1, 8):
            tot = tot + v_ref[d]
        o_ref[...] = tot

    return pl.pallas_call(
        body, name=name, grid=(rows // tr,),
        in_specs=[pl.BlockSpec((8, tr, 128), lambda i: (0, i, 0))], out_specs=pl.BlockSpec((tr, 128), lambda i: (i, 0)),
        out_shape=jax.ShapeDtypeStruct((rows, 128), F32),
        compiler_params=_cp(("parallel",)),
    )(v8)


def _block_diag(w_pool_l):
    wbd = jnp.zeros((MAIN_W, MAIN_W), F32)
    for gi in range(len(POOL_WINDOWS)):
        wbd = lax.dynamic_update_slice(wbd, w_pool_l[gi], (gi * POOL_GROUP, gi * POOL_GROUP))
    return wbd.astype(BF)


def _local_step(x, mem, positions, wf, norm_gains, mem_norm, w_pool, pool_scale, kv_norm, target):
    B, S, _ = x.shape
    T = B * S
    xc = x.reshape(T, D_MODEL)
    memf = mem.reshape(B * N_MEM, D_MODEL)
    tgt = target.reshape(T, D_MODEL)
    cos, sin = _rope_tables(positions.reshape(T, 1), name="rope_tables")
    wbd = [_block_diag(w_pool[l]) for l in range(N_A)]
    nbo = D_FF // 256
    fw = []
    rk = rv = None
    kv_saved = None

    for l in range(DEPTH):
        t = f"l{l}_"
        sv = {'x_in': xc}
        h0, sv['r0'] = _norm_fwd(xc, norm_gains[l, 0], name=t + "norm0", out_dtype=BF)
        z, = _mm(h0, wf['w_in'], b_layer=l, name=t + "mm_in")
        memn, sv['rm'] = _norm_fwd(memf, mem_norm[l], name=t + "norm_mem", out_dtype=BF, tm=256)
        kvm, = _mm(memn, wf['w_mem_kv'], b_layer=l, name=t + "mm_memkv", out_dtypes=(BF,))
        if l < N_A:
            ycat, sv['p'] = _pool_fwd(z, wbd[l], pool_scale[l], B, S, name=t + "pool_fwd")
        else:
            rq = _rope_apply(z, cos, sin, name=t + "rope_q")
            o = lax.empty((T, MAIN_W), F32)
            lse = lax.empty((T, MAIN_W), F32)
            for g in range(3):
                o, lse = _dil_fwd_group(g, rq, rk, rv, o, lse, B, S, name=t + f"dil_fwd{g}")
            ycat = _dil_combine_fwd(o, lse, lax.empty((T, D_MODEL), BF), name=t + "dil_combine")
            sv.update(rq=rq, o=o, lse=lse)
        ycat, sv['lse_m'] = _memattn_fwd(z, kvm, ycat, B, S, name=t + "memattn_fwd")
        y1, = _mm(ycat, wf['w_out'], b_layer=l, name=t + "mm_out")
        x1, sv['r1'] = _norm_fwd(y1, norm_gains[l, 1], name=t + "norm1", res=xc)
        h2, sv['r2'] = _norm_fwd(x1, norm_gains[l, 2], name=t + "norm2", out_dtype=BF)
        gg, uu, aa = _mm(h2, wf['w_gate_up'], b_layer=l, b_offsets=(0, nbo), out_n=D_FF, tn=256, name=t + "mm_gate_up",
                         epilogue=_swiglu_fwd_epilogue, out_dtypes=(BF, BF, BF))
        y2, = _mm(aa, wf['w_down'], b_layer=l, tk=D_FF, name=t + "mm_down")
        x2, sv['r3'] = _norm_fwd(y2, norm_gains[l, 3], name=t + "norm3", res=x1)
        sv.update(h0=h0, z=z, memn=memn, kvm=kvm, ycat=ycat, y1=y1, x1=x1, h2=h2, gg=gg, uu=uu, aa=aa, y2=y2)
        fw.append(sv)
        xc = x2
        if l == N_A - 1:
            kvn, rkv = _norm_fwd(xc, kv_norm, name="norm_kv", out_dtype=BF)
            kv, = _mm(kvn, wf['w_kv'], b_layer=0, name="mm_kv")
            rk, rv = _rope_apply(kv, cos, sin, name="rope_k", passthrough=True)
            kv_saved = (xc, kvn, rkv)

    loss, dx = _loss(xc, tgt, name="loss")

    gbig = {nm: lax.empty(BIG[nm][0], F32) for nm in BIG_NAMES}
    d_ng = [[None] * 4 for _ in range(DEPTH)]
    d_memnorm = [None] * DEPTH
    d_wbd = [None] * N_A
    d_pscale = [None] * N_A
    d_kvnorm = None
    kv_parts = []

    for l in reversed(range(DEPTH)):
        t = f"l{l}_b_"
        sv = fw[l]
        dy2, d_ng[l][3] = _norm_bwd(dx, sv['y2'], sv['r3'], norm_gains[l, 3], name=t + "norm3", out_dtype=BF)
        gbig['w_down'] = _mm(sv['aa'], dy2, ta=True, tm=1408, name=t + "dw_down", stack=(gbig['w_down'], l))
        dg, du = _mm(dy2, wf['w_down'], tb=True, b_layer=l, tn=256, name=t + "d_act",
                     extras=((sv['gg'], 'tile'), (sv['uu'], 'tile')), epilogue=_swiglu_bwd_epilogue, out_dtypes=(BF, BF))
        gbig['w_gate_up'] = _mm(sv['h2'], (dg, du), ta=True, tn=1408, tk=512, name=t + "dw_gate_up",
                                stack=(gbig['w_gate_up'], l))
        dh2, = _mm((dg, du), wf['w_gate_up'], tb=True, b_layer=l, tk=1408, name=t + "d_h2", out_dtypes=(BF,))
        dx1, d_ng[l][2] = _norm_bwd(dh2, sv['x1'], sv['r2'], norm_gains[l, 2], name=t + "norm2", add=dx)
        dy1, d_ng[l][1] = _norm_bwd(dx1, sv['y1'], sv['r1'], norm_gains[l, 1], name=t + "norm1", out_dtype=BF)
        gbig['w_out'] = _mm(sv['ycat'], dy1, ta=True, name=t + "dw_out", stack=(gbig['w_out'], l))
        dycat, = _mm(dy1, wf['w_out'], tb=True, b_layer=l, name=t + "d_ycat")
        dz = lax.empty((T, D_MODEL), BF)
        dz, dkm, dvm = _memattn_bwd(dycat, sv['z'], sv['kvm'], sv['lse_m'], dz, B, S, name=t + "memattn")
        if l < N_A:
            dz, d_wbd[l], d_pscale[l] = _pool_bwd(dycat, sv['p'], wbd[l], pool_scale[l], dz, B, S, name=t + "pool")
        else:
            do, cb = _dil_combine_bwd(dycat, sv['o'], sv['lse'], name=t + "dil_combine")
            acc = tuple(lax.empty((T, MAIN_W), F32) for _ in range(5))
            for g in range(3):
                acc = _dil_bwd_group(g, sv['rq'], rk, rv, do, cb, sv['lse'], acc, B, S, name=t + f"dil{g}")
            dz = _rope_apply(acc[0], cos, sin, name=t + "rope_q", sign=-1.0, alias=dz)
            kv_parts.append(acc[1:])
        gbig['w_in'] = _mm(sv['h0'], dz, ta=True, name=t + "dw_in", stack=(gbig['w_in'], l))
        dh0, = _mm(dz, wf['w_in'], tb=True, b_layer=l, name=t + "d_h0", out_dtypes=(BF,))
        dx, d_ng[l][0] = _norm_bwd(dh0, sv['x_in'], sv['r0'], norm_gains[l, 0], name=t + "norm0", add=dx1)
        gbig['w_mem_kv'] = _mm(sv['memn'], (dkm, dvm), ta=True, tn=256, name=t + "dw_memkv", stack=(gbig['w_mem_kv'], l))
        dmemn, = _mm((dkm, dvm), wf['w_mem_kv'], tb=True, b_layer=l, tk=256, name=t + "d_memn", out_dtypes=(BF,))
        _, d_memnorm[l] = _norm_bwd(dmemn, memf, sv['rm'], mem_norm[l], name=t + "norm_mem", out_dtype=BF, tm=256)
        if l == N_A:
            dk, dv = _kv_grad(kv_parts, cos, sin, B, S, name="kv_grad")
            x_kv, kvn, rkv = kv_saved
            gbig['w_kv'] = _mm(kvn, (dk, dv), ta=True, tn=768, name="dw_kv", stack=(gbig['w_kv'], 0))
            dkvn, = _mm((dk, dv), wf['w_kv'], tb=True, b_layer=0, tk=768, name="d_kvn", out_dtypes=(BF,))
            dx, d_kvnorm = _norm_bwd(dkvn, x_kv, rkv, kv_norm, name="norm_kv_b", add=dx)

    small = {
        'norm_gains': jnp.stack([jnp.concatenate(d_ng[l], axis=0) for l in range(DEPTH)]),
        'mem_norm': jnp.concatenate(d_memnorm, axis=0),
        'kv_norm': d_kvnorm.reshape(D_MODEL),
        'pool_scale': jnp.concatenate(d_pscale, axis=0),
        'w_pool': jnp.stack([jnp.stack([d_wbd[l][gi * POOL_GROUP:(gi + 1) * POOL_GROUP, gi * POOL_GROUP:(gi + 1) * POOL_GROUP]
                                        for gi in range(len(POOL_WINDOWS))]) for l in range(N_A)]),
    }
    return loss, dx, gbig, small


SMALL_ORDER = ('norm_gains', 'mem_norm', 'kv_norm', 'pool_scale', 'w_pool')
SMALL_VEC_ROWS = 2560


def kernel(x, mem, positions, norm_gains, mem_norm, w_in, w_mem_kv, w_out, w_pool, pool_scale, kv_norm, w_kv, w_gate_up, w_down, loss_target, m_norm_gains, m_mem_norm, m_w_in, m_w_mem_kv, m_w_out, m_w_pool, m_pool_scale, m_kv_norm, m_w_kv, m_w_gate_up, m_w_down, v_norm_gains, v_mem_norm, v_w_in, v_w_mem_kv, v_w_out, v_w_pool, v_pool_scale, v_kv_norm, v_w_kv, v_w_gate_up, v_w_down):
    xi, yi, ci = lax.axis_index("x"), lax.axis_index("y"), lax.axis_index("c")
    s = 2 * xi + yi
    sc = jnp.stack([s, ci]).astype(jnp.int32)
    weights = dict(norm_gains=norm_gains, mem_norm=mem_norm, w_in=w_in, w_mem_kv=w_mem_kv, w_out=w_out, w_pool=w_pool,
                   pool_scale=pool_scale, kv_norm=kv_norm, w_kv=w_kv, w_gate_up=w_gate_up, w_down=w_down)
    moms = dict(norm_gains=m_norm_gains, mem_norm=m_mem_norm, w_in=m_w_in, w_mem_kv=m_w_mem_kv, w_out=m_w_out,
                w_pool=m_w_pool, pool_scale=m_pool_scale, kv_norm=m_kv_norm, w_kv=m_w_kv, w_gate_up=m_w_gate_up,
                w_down=m_w_down)
    vels = dict(norm_gains=v_norm_gains, mem_norm=v_mem_norm, w_in=v_w_in, w_mem_kv=v_w_mem_kv, w_out=v_w_out,
                w_pool=v_w_pool, pool_scale=v_pool_scale, kv_norm=v_kv_norm, w_kv=v_w_kv, w_gate_up=v_w_gate_up,
                w_down=v_w_down)

    shards = {nm: weights[nm].astype(BF).reshape(_shard_shape(nm)) for nm in BIG_NAMES}
    small_w = jnp.zeros((SMALL_ROWS, 256), F32)
    small_w = lax.dynamic_update_slice(small_w, norm_gains.reshape(16, 256), (0, 0))
    small_w = lax.dynamic_update_slice(small_w, pool_scale, (16, 0))
    wf, small_all = _all_gather(shards, small_w)
    ng_full = small_all[:, :16, :].reshape(N_CHIPS, DEPTH, 4, 256).transpose(1, 2, 0, 3).reshape(DEPTH, 4, D_MODEL)
    ps_full = small_all[:, 16:18, :POOL_GROUP].transpose(1, 0, 2).reshape(N_A, MAIN_W)

    loss, gx, gbig, gsmall = _local_step(x, mem, positions, wf, ng_full, mem_norm, w_pool, ps_full, kv_norm, loss_target)
    loss = lax.psum(loss[0, 0], ("x", "y", "c"))

    from_sib = _exchange_halves(gbig)
    halves, own = {}, {}
    for nm in BIG_NAMES:
        halves[nm], own[nm] = _add_halves(nm, gbig[nm], from_sib[nm], sc)
    pieces = _scatter_to_chips(halves)
    gsh = _share_with_sibling({nm: _sum_pieces(nm, own[nm], pieces[nm], sc) for nm in BIG_NAMES})

    vec = jnp.concatenate([gsmall[nm].reshape(-1) for nm in SMALL_ORDER])
    vec = jnp.pad(vec, (0, SMALL_VEC_ROWS * 128 - vec.shape[0])).reshape(SMALL_VEC_ROWS, 128)
    tot = _sum8(_all_gather_small(vec), name="sum_small_grads", tr=512).reshape(-1)
    grads, off = {}, 0
    for nm in SMALL_ORDER:
        shape = (DEPTH, 4, D_MODEL) if nm == 'norm_gains' else (N_A, MAIN_W) if nm == 'pool_scale' else weights[nm].shape
        n = 1
        for dim in shape:
            n *= dim
        grads[nm] = tot[off:off + n].reshape(shape)
        off += n
    grads['norm_gains'] = lax.dynamic_slice(grads['norm_gains'], (0, 0, s * 256), (DEPTH, 4, 256))
    grads['pool_scale'] = lax.dynamic_slice(grads['pool_scale'], (0, s * POOL_GROUP), (N_A, POOL_GROUP))
    for nm in BIG_NAMES:
        grads[nm] = gsh[nm].reshape(weights[nm].shape)

    order = ('norm_gains', 'mem_norm', 'w_in', 'w_mem_kv', 'w_out', 'w_pool', 'pool_scale', 'kv_norm', 'w_kv',
             'w_gate_up', 'w_down')
    deltas, new_m, new_v = {}, {}, {}
    for nm in order:
        deltas[nm], new_m[nm], new_v[nm] = _adamw(weights[nm], grads[nm], moms[nm], vels[nm], name="adamw_" + nm)
    return (loss, gx.reshape(x.shape), *[grads[nm] for nm in order], *[deltas[nm] for nm in order],
            *[new_m[nm] for nm in order], *[new_v[nm] for nm in order])
```

```python
import functools

import jax
import jax.numpy as jnp
from jax import lax
from jax.experimental import pallas as pl
from jax.experimental.pallas import tpu as pltpu

F32 = jnp.float32
BF = jnp.bfloat16

D_MODEL = 1024
DEPTH = 4
N_A = 2
HEAD_DIM = 64
MEM_W = 256
MAIN_W = 768
D_FF = 2816
N_MEM = 256
POOL_WINDOWS = (2, 4, 8, 16)
POOL_GROUP = 192
DIL = (1, 4, 16)
STEPS = 128
ROPE_THETA = 10000.0
EPS = 1e-6
SCALE = HEAD_DIM ** -0.5
NEG = -1e30

ADAM_LR = 0.001
ADAM_B1 = 0.9
ADAM_B2 = 0.999
ADAM_EPS = 1e-08
ADAM_WD = 0.01
ADAM_STEP = 10

VMEM_LIMIT = 48 * 1024 * 1024
MESH = pl.DeviceIdType.MESH


def _cp(sem):
    return pltpu.CompilerParams(dimension_semantics=sem, vmem_limit_bytes=VMEM_LIMIT)


def _mm(a, b, *, name, ta=False, tb=False, tm=1024, tn=512, tk=1024, b_layer=None, b_offsets=(0,),
        extras=(), epilogue=None, out_dtypes=(F32,), out_n=None, stack=None):
    a_pair = isinstance(a, (tuple, list))
    b_pair = isinstance(b, (tuple, list))
    a0 = a[0] if a_pair else a
    b0 = b[0] if b_pair else b
    a_rows, a_cols = a0.shape
    if a_pair:
        a_cols *= 2
    b_rows, b_cols = b0.shape[-2:]
    if b_pair:
        b_cols *= 2
    M, K = (a_cols, a_rows) if ta else (a_rows, a_cols)
    N = b_rows if tb else b_cols
    if out_n is not None:
        N = out_n
    tm, tn, tk = min(tm, M), min(tn, N), min(tk, K)
    assert M % tm == 0 and N % tn == 0 and K % tk == 0, (name, M, N, K, tm, tn, tk)
    nk = K // tk
    n_acc = len(b_offsets)

    if a_pair:
        a_half = (a0.shape[1] // (tm if ta else tk))
    if b_pair:
        b_half = (b0.shape[1] // (tk if tb else tn))

    def a_map(sel):
        def f(i, j, k):
            r, c = (k, i) if ta else (i, k)
            if a_pair:
                c = jnp.clip(c - sel * a_half, 0, a_half - 1)
            return (r, c)
        return f

    def b_map(sel, off):
        def f(i, j, k):
            r, c = (j + off, k) if tb else (k, j + off)
            if b_pair:
                c = jnp.clip(c - sel * b_half, 0, b_half - 1)
            if b_layer is not None:
                return (b_layer, r, c)
            return (r, c)
        return f

    a_blk = (tk, tm) if ta else (tm, tk)
    b_blk = (tn, tk) if tb else (tk, tn)
    if b_layer is not None:
        b_blk = (None,) + b_blk
    in_specs, operands = [], []
    for sel in range(2 if a_pair else 1):
        in_specs.append(pl.BlockSpec(a_blk, a_map(sel)))
        operands.append(a[sel] if a_pair else a)
    n_a = len(operands)
    for off in b_offsets:
        for sel in range(2 if b_pair else 1):
            in_specs.append(pl.BlockSpec(b_blk, b_map(sel, off)))
            operands.append(b[sel] if b_pair else b)
    n_b = len(operands) - n_a
    for arr, kind in extras:
        if kind == 'tile':
            in_specs.append(pl.BlockSpec((tm, tn), lambda i, j, k: (i, j)))
        elif kind == 'row':
            in_specs.append(pl.BlockSpec((tm, 1), lambda i, j, k: (i, 0)))
        else:
            in_specs.append(pl.BlockSpec((1, tn), lambda i, j, k: (0, j)))
        operands.append(arr)
    n_e = len(extras)
    n_o = len(out_dtypes)
    dims = (((0,) if ta else (1,), (1,) if tb else (0,)), ((), ()))

    def body(*refs):
        a_refs = refs[:n_a]
        b_refs = refs[n_a:n_a + n_b]
        e_refs = refs[n_a + n_b:n_a + n_b + n_e]
        n_in = n_a + n_b + n_e + (1 if stack is not None else 0)
        o_refs = refs[n_in:n_in + n_o]
        acc_refs = refs[n_in + n_o:]
        i, j, k = pl.program_id(0), pl.program_id(1), pl.program_id(2)
        if a_pair:
            cidx = i if ta else k
            av = jnp.where(cidx < a_half, a_refs[0][...], a_refs[1][...])
        else:
            av = a_refs[0][...]
        av = av.astype(BF)
        prods = []
        for q in range(n_acc):
            if b_pair:
                cidx = (k if tb else j) + b_offsets[q]
                bv = jnp.where(cidx < b_half, b_refs[2 * q][...], b_refs[2 * q + 1][...])
            else:
                bv = b_refs[q][...]
            prods.append(lax.dot_general(av, bv.astype(BF), dims, preferred_element_type=F32))

        def finish(accs):
            outs = epilogue(accs, *[r[...] for r in e_refs]) if epilogue is not None else accs
            for o_ref, o in zip(o_refs, outs):
                o_ref[...] = o.astype(o_ref.dtype)

        if nk == 1:
            finish(prods)
        else:
            @pl.when(k == 0)
            def _():
                for r, p in zip(acc_refs, prods):
                    r[...] = p

            @pl.when(k > 0)
            def _():
                for r, p in zip(acc_refs, prods):
                    r[...] += p

            @pl.when(k == nk - 1)
            def _():
                finish([r[...] for r in acc_refs])

    if stack is not None:
        buf, layer = stack
        assert n_o == 1 and buf.shape[1:] == (M, N)
        return pl.pallas_call(
            body, name=name,
            grid=(M // tm, N // tn, nk),
            in_specs=in_specs + [pl.BlockSpec(memory_space=pl.ANY)],
            out_specs=[pl.BlockSpec((None, tm, tn), lambda i, j, k: (layer, i, j))],
            out_shape=[jax.ShapeDtypeStruct(buf.shape, buf.dtype)],
            scratch_shapes=[pltpu.VMEM((tm, tn), F32) for _ in range(n_acc if nk > 1 else 0)],
            input_output_aliases={len(operands): 0},
            compiler_params=_cp(("parallel", "parallel", "arbitrary")),
        )(*operands, buf)[0]
    return pl.pallas_call(
        body, name=name,
        grid=(M // tm, N // tn, nk),
        in_specs=in_specs,
        out_specs=[pl.BlockSpec((tm, tn), lambda i, j, k: (i, j)) for _ in range(n_o)],
        out_shape=[jax.ShapeDtypeStruct((M, N), dt) for dt in out_dtypes],
        scratch_shapes=[pltpu.VMEM((tm, tn), F32) for _ in range(n_acc if nk > 1 else 0)],
        compiler_params=_cp(("parallel", "parallel", "arbitrary")),
    )(*operands)


def _norm_fwd(x, g, *, name, res=None, out_dtype=F32, tm=512):
    T, Dm = x.shape
    has_res = res is not None

    def body(*refs):
        if has_res:
            x_ref, g_ref, r_ref, y_ref, s_ref = refs
        else:
            x_ref, g_ref, y_ref, s_ref = refs
        xv = x_ref[...]
        rstd = lax.rsqrt(jnp.mean(xv * xv, axis=-1, keepdims=True) + EPS)
        y = xv * rstd * g_ref[...]
        if has_res:
            y = r_ref[...] + y
        y_ref[...] = y.astype(y_ref.dtype)
        s_ref[...] = rstd

    row = pl.BlockSpec((tm, Dm), lambda i: (i, 0))
    in_specs = [row, pl.BlockSpec((1, Dm), lambda i: (0, 0))] + ([row] if has_res else [])
    ops = [x, g.reshape(1, Dm)] + ([res] if has_res else [])
    return pl.pallas_call(
        body, name=name, grid=(T // tm,), in_specs=in_specs,
        out_specs=[row, pl.BlockSpec((tm, 1), lambda i: (i, 0))],
        out_shape=[jax.ShapeDtypeStruct((T, Dm), out_dtype), jax.ShapeDtypeStruct((T, 1), F32)],
        compiler_params=_cp(("parallel",)),
    )(*ops)


def _norm_bwd(dout, x, rstd, g, *, name, add=None, out_dtype=F32, tm=512):
    T, Dm = x.shape
    has_add = add is not None
    nt = T // tm

    def body(*refs):
        if has_add:
            do_ref, x_ref, s_ref, g_ref, a_ref, dx_ref, dg_ref, acc = refs
        else:
            do_ref, x_ref, s_ref, g_ref, dx_ref, dg_ref, acc = refs
        i = pl.program_id(0)
        do = do_ref[...].astype(F32)
        xh = x_ref[...] * s_ref[...]
        gd = do * g_ref[...]
        dx = s_ref[...] * (gd - xh * jnp.mean(gd * xh, axis=-1, keepdims=True))
        if has_add:
            dx = dx + a_ref[...].astype(F32)
        dx_ref[...] = dx.astype(dx_ref.dtype)
        part = jnp.sum((do * xh).reshape(tm // 8, 8, Dm), axis=0)

        @pl.when(i == 0)
        def _():
            acc[...] = part

        @pl.when(i > 0)
        def _():
            acc[...] += part

        @pl.when(i == nt - 1)
        def _():
            dg_ref[...] = jnp.sum(acc[...], axis=0, keepdims=True)

    row = pl.BlockSpec((tm, Dm), lambda i: (i, 0))
    in_specs = [row, row, pl.BlockSpec((tm, 1), lambda i: (i, 0)), pl.BlockSpec((1, Dm), lambda i: (0, 0))]
    ops = [dout, x, rstd, g.reshape(1, Dm)]
    if has_add:
        in_specs.append(row)
        ops.append(add)
    return pl.pallas_call(
        body, name=name, grid=(nt,), in_specs=in_specs,
        out_specs=[row, pl.BlockSpec((1, Dm), lambda i: (0, 0))],
        out_shape=[jax.ShapeDtypeStruct((T, Dm), out_dtype), jax.ShapeDtypeStruct((1, Dm), F32)],
        scratch_shapes=[pltpu.VMEM((8, Dm), F32)],
        compiler_params=_cp(("arbitrary",)),
    )(*ops)


def _swiglu_fwd_epilogue(accs):
    g, u = accs
    return g, u, g * jax.nn.sigmoid(g) * u


def _swiglu_bwd_epilogue(accs, g, u):
    da = accs[0]
    g = g.astype(F32)
    u = u.astype(F32)
    sig = jax.nn.sigmoid(g)
    return da * u * (sig * (1.0 + g * (1.0 - sig))), da * (g * sig)


def _rope_tables(pos, *, name, tm=1024):
    T = pos.shape[0]
    half = HEAD_DIM // 2
    freqs = ROPE_THETA ** (-jnp.arange(half, dtype=F32) / half)
    freqs = jnp.tile(freqs, 4).reshape(1, 128)

    def body(p_ref, f_ref, c_ref, s_ref):
        ang = p_ref[...].astype(F32) * f_ref[...]
        lane = lax.broadcasted_iota(jnp.int32, ang.shape, 1)
        c_ref[...] = jnp.cos(ang)
        s_ref[...] = jnp.where(lane % HEAD_DIM < half, -1.0, 1.0) * jnp.sin(ang)

    tab = pl.BlockSpec((tm, 128), lambda i: (i, 0))
    return pl.pallas_call(
        body, name=name, grid=(T // tm,),
        in_specs=[pl.BlockSpec((tm, 1), lambda i: (i, 0)), pl.BlockSpec((1, 128), lambda i: (0, 0))],
        out_specs=[tab, tab],
        out_shape=[jax.ShapeDtypeStruct((T, 128), F32)] * 2,
        compiler_params=_cp(("parallel",)),
    )(pos, freqs)


def _rot(x, cos, sin, sign):
    W = x.shape[1]
    half = HEAD_DIM // 2
    reps = W // 128
    c = jnp.concatenate([cos] * reps, axis=1) if reps > 1 else cos
    s = jnp.concatenate([sin] * reps, axis=1) if reps > 1 else sin
    lane = lax.broadcasted_iota(jnp.int32, x.shape, 1)
    swapped = jnp.where(lane % HEAD_DIM < half, pltpu.roll(x, W - half, axis=1), pltpu.roll(x, half, axis=1))
    return x * c + (sign * s) * swapped


def _rope_apply(x, cos, sin, *, name, sign=1.0, width=MAIN_W, passthrough=False, out_dtype=BF, alias=None,
                out_cols=None, tm=512):
    T = x.shape[0]

    def body(*refs):
        if passthrough:
            x_ref, v_ref, c_ref, s_ref, o_ref, ov_ref = refs
            ov_ref[...] = v_ref[...].astype(ov_ref.dtype)
        elif alias is not None:
            x_ref, c_ref, s_ref, _, o_ref = refs
        else:
            x_ref, c_ref, s_ref, o_ref = refs
        o_ref[...] = _rot(x_ref[...].astype(F32), c_ref[...], s_ref[...], sign).astype(o_ref.dtype)

    blk0 = pl.BlockSpec((tm, width), lambda i: (i, 0))
    blk1 = pl.BlockSpec((tm, width), lambda i: (i, 1))
    tab = pl.BlockSpec((tm, 128), lambda i: (i, 0))
    if passthrough:
        return pl.pallas_call(
            body, name=name, grid=(T // tm,), in_specs=[blk0, blk1, tab, tab], out_specs=[blk0, blk0],
            out_shape=[jax.ShapeDtypeStruct((T, width), out_dtype)] * 2,
            compiler_params=_cp(("parallel",)),
        )(x, x, cos, sin)
    if alias is not None:
        return pl.pallas_call(
            body, name=name, grid=(T // tm,),
            in_specs=[blk0, tab, tab, pl.BlockSpec(memory_space=pl.ANY)], out_specs=blk0,
            out_shape=jax.ShapeDtypeStruct(alias.shape, alias.dtype),
            input_output_aliases={3: 0},
            compiler_params=_cp(("parallel",)),
        )(x, cos, sin, alias)
    return pl.pallas_call(
        body, name=name, grid=(T // tm,), in_specs=[blk0, tab, tab], out_specs=blk0,
        out_shape=jax.ShapeDtypeStruct((T, width), out_dtype),
        compiler_params=_cp(("parallel",)),
    )(x, cos, sin)


POOL_T = 256
POOL_HALO = 16


def _pool_lane_window(shape):
    lane = lax.broadcasted_iota(jnp.int32, shape, 1)
    w = jnp.full(shape, POOL_WINDOWS[0], jnp.int32)
    for gi in range(1, len(POOL_WINDOWS)):
        w = jnp.where(lane >= gi * POOL_GROUP, POOL_WINDOWS[gi], w)
    return w


def _pool_fwd(z, wbd, scale, B, S, *, name):
    T = z.shape[0]
    nt = S // POOL_T
    hb = POOL_T // POOL_HALO

    def body(z_ref, h_ref, w_ref, sc_ref, y_ref, p_ref, ext):
        i = pl.program_id(1)
        u = z_ref[...]
        ext[pl.ds(POOL_HALO, POOL_T), :] = u
        ext[pl.ds(0, POOL_HALO), :] = jnp.where(i > 0, h_ref[...], 0.0)
        win = _pool_lane_window((POOL_T, MAIN_W))
        acc = u
        for k in range(1, POOL_HALO):
            acc = acc + jnp.where(k < win, ext[pl.ds(POOL_HALO - k, POOL_T), :], 0.0)
        t = i * POOL_T + lax.broadcasted_iota(jnp.int32, (POOL_T, MAIN_W), 0)
        cnt = jnp.minimum(t + 1, win).astype(F32)
        p = (acc / cnt - u).astype(BF)
        p_ref[...] = p
        y = jnp.dot(p, w_ref[...], preferred_element_type=F32) * sc_ref[...]
        y_ref[...] = y.astype(y_ref.dtype)

    return pl.pallas_call(
        body, name=name, grid=(B, nt),
        in_specs=[pl.BlockSpec((POOL_T, MAIN_W), lambda b, i: (b * nt + i, 0)),
                  pl.BlockSpec((POOL_HALO, MAIN_W), lambda b, i: (jnp.maximum((b * nt + i) * hb - 1, 0), 0)),
                  pl.BlockSpec((MAIN_W, MAIN_W), lambda b, i: (0, 0)),
                  pl.BlockSpec((1, MAIN_W), lambda b, i: (0, 0))],
        out_specs=[pl.BlockSpec((POOL_T, MAIN_W), lambda b, i: (b * nt + i, 0)),
                   pl.BlockSpec((POOL_T, MAIN_W), lambda b, i: (b * nt + i, 0))],
        out_shape=[jax.ShapeDtypeStruct((T, D_MODEL), BF), jax.ShapeDtypeStruct((T, MAIN_W), BF)],
        scratch_shapes=[pltpu.VMEM((POOL_T + POOL_HALO, MAIN_W), F32)],
        compiler_params=_cp(("parallel", "parallel")),
    )(z, z, wbd, scale.reshape(1, MAIN_W))


def _pool_bwd(dy, p, wbd, scale, dz_alias, B, S, *, name):
    T = dy.shape[0]
    nt = S // POOL_T
    hb = POOL_T // POOL_HALO
    last_halo = T // POOL_HALO - 1
    R = POOL_T + POOL_HALO

    def body(dy_ref, dyn_ref, p_ref, pn_ref, w_ref, sc_ref, _, dz_ref, dw_ref, ds_ref, ext, dw_acc, ds_acc):
        b, i = pl.program_id(0), pl.program_id(1)
        first = jnp.logical_and(b == 0, i == 0)
        dyv = dy_ref[...]
        pv = p_ref[...]
        sc = sc_ref[...]
        w = w_ref[...]
        pw = jnp.dot(pv, w, preferred_element_type=F32)
        ds_part = jnp.sum((dyv * pw).reshape(POOL_T // 8, 8, MAIN_W), axis=0)
        dpw = (dyv * sc).astype(BF)
        dw_part = lax.dot_general(pv, dpw, (((0,), (0,)), ((), ())), preferred_element_type=F32)

        @pl.when(first)
        def _():
            dw_acc[...] = dw_part
            ds_acc[...] = ds_part

        @pl.when(jnp.logical_not(first))
        def _():
            dw_acc[...] += dw_part
            ds_acc[...] += ds_part

        @pl.when(jnp.logical_and(b == pl.num_programs(0) - 1, i == nt - 1))
        def _():
            dw_ref[...] = dw_acc[...]
            ds_ref[...] = jnp.sum(ds_acc[...], axis=0, keepdims=True)

        dp = lax.dot_general(dpw, w, (((1,), (1,)), ((), ())), preferred_element_type=F32)
        dpn = lax.dot_general((dyn_ref[...] * sc).astype(BF), w, (((1,), (1,)), ((), ())), preferred_element_type=F32)
        win = _pool_lane_window((POOL_T, MAIN_W))
        win_n = _pool_lane_window((POOL_HALO, MAIN_W))
        t = i * POOL_T + lax.broadcasted_iota(jnp.int32, (POOL_T, MAIN_W), 0)
        tn = (i + 1) * POOL_T + lax.broadcasted_iota(jnp.int32, (POOL_HALO, MAIN_W), 0)
        ext[pl.ds(0, POOL_T), :] = dp / jnp.minimum(t + 1, win).astype(F32)
        ext[pl.ds(POOL_T, POOL_HALO), :] = jnp.where(i < nt - 1, dpn / jnp.minimum(tn + 1, win_n).astype(F32), 0.0)
        acc = -dp
        for k in range(POOL_HALO):
            acc = acc + jnp.where(k < win, ext[pl.ds(k, POOL_T), :], 0.0)
        dz_ref[...] = acc.astype(dz_ref.dtype)

    cur = lambda b, i: (b * nt + i, 0)
    nxt = lambda b, i: (jnp.minimum((b * nt + i + 1) * hb, last_halo), 0)
    return pl.pallas_call(
        body, name=name, grid=(B, nt),
        in_specs=[pl.BlockSpec((POOL_T, MAIN_W), cur), pl.BlockSpec((POOL_HALO, MAIN_W), nxt),
                  pl.BlockSpec((POOL_T, MAIN_W), cur), pl.BlockSpec((POOL_HALO, MAIN_W), nxt),
                  pl.BlockSpec((MAIN_W, MAIN_W), lambda b, i: (0, 0)),
                  pl.BlockSpec((1, MAIN_W), lambda b, i: (0, 0)),
                  pl.BlockSpec(memory_space=pl.ANY)],
        out_specs=[pl.BlockSpec((POOL_T, MAIN_W), cur),
                   pl.BlockSpec((MAIN_W, MAIN_W), lambda b, i: (0, 0)),
                   pl.BlockSpec((1, MAIN_W), lambda b, i: (0, 0))],
        out_shape=[jax.ShapeDtypeStruct(dz_alias.shape, dz_alias.dtype),
                   jax.ShapeDtypeStruct((MAIN_W, MAIN_W), F32), jax.ShapeDtypeStruct((1, MAIN_W), F32)],
        scratch_shapes=[pltpu.VMEM((R, MAIN_W), F32), pltpu.VMEM((MAIN_W, MAIN_W), F32), pltpu.VMEM((8, MAIN_W), F32)],
        input_output_aliases={6: 0},
        compiler_params=_cp(("arbitrary", "arbitrary")),
    )(dy, dy, p, p, wbd, scale.reshape(1, MAIN_W), dz_alias)


def _head_masks(shape):
    lane = lax.broadcasted_iota(jnp.int32, shape, 1)
    return [(lane // HEAD_DIM) == h for h in range(shape[1] // HEAD_DIM)]


def _row_of(bcast, mask):
    return jnp.max(jnp.where(mask, bcast, -jnp.inf), axis=-1, keepdims=True)


MEM_TQ = 512


def _memattn_fwd(z, kv, y_alias, B, S, *, name):
    T = z.shape[0]
    nt = S // MEM_TQ

    def body(q_ref, k_ref, v_ref, _, y_ref, l_ref):
        q = q_ref[...]
        k = k_ref[...]
        v = v_ref[...]
        masks = _head_masks(q.shape)
        o = jnp.zeros(q.shape, F32)
        lse_b = jnp.zeros(q.shape, F32)
        for m in masks:
            qm = jnp.where(m, q, 0.0).astype(BF)
            s = lax.dot_general(qm, k, (((1,), (1,)), ((), ())), preferred_element_type=F32) * SCALE
            mx = jnp.max(s, axis=-1, keepdims=True)
            e = jnp.exp(s - mx)
            l = jnp.sum(e, axis=-1, keepdims=True)
            p = (e / l).astype(BF)
            o = o + jnp.where(m, jnp.dot(p, v, preferred_element_type=F32), 0.0)
            lse_b = lse_b + jnp.where(m, mx + jnp.log(l), 0.0)
        y_ref[...] = o.astype(y_ref.dtype)
        l_ref[...] = lse_b

    qblk = pl.BlockSpec((MEM_TQ, MEM_W), lambda b, i: (b * nt + i, 3))
    return pl.pallas_call(
        body, name=name, grid=(B, nt),
        in_specs=[qblk, pl.BlockSpec((N_MEM, MEM_W), lambda b, i: (b, 0)), pl.BlockSpec((N_MEM, MEM_W), lambda b, i: (b, 1)),
                  pl.BlockSpec(memory_space=pl.ANY)],
        out_specs=[qblk, pl.BlockSpec((MEM_TQ, MEM_W), lambda b, i: (b * nt + i, 0))],
        out_shape=[jax.ShapeDtypeStruct(y_alias.shape, y_alias.dtype), jax.ShapeDtypeStruct((T, MEM_W), F32)],
        input_output_aliases={3: 0},
        compiler_params=_cp(("parallel", "parallel")),
    )(z, kv, kv, y_alias)


def _memattn_bwd(dy, z, kv, lse, dz_alias, B, S, *, name):
    nt = S // MEM_TQ

    def body(do_ref, q_ref, k_ref, v_ref, l_ref, _, dz_ref, dk_ref, dv_ref, dk_acc, dv_acc):
        i = pl.program_id(1)
        do = do_ref[...]
        q = q_ref[...]
        k = k_ref[...]
        v = v_ref[...]
        lse_b = l_ref[...]
        masks = _head_masks(q.shape)
        dq = jnp.zeros(q.shape, F32)
        dk = jnp.zeros(k.shape, F32)
        dv = jnp.zeros(v.shape, F32)
        for m in masks:
            qm = jnp.where(m, q, 0.0).astype(BF)
            dom = jnp.where(m, do, 0.0).astype(BF)
            s = lax.dot_general(qm, k, (((1,), (1,)), ((), ())), preferred_element_type=F32) * SCALE
            p = jnp.exp(s - _row_of(lse_b, m))
            dp = lax.dot_general(dom, v, (((1,), (1,)), ((), ())), preferred_element_type=F32)
            delta = jnp.sum(p * dp, axis=-1, keepdims=True)
            ds = (p * (dp - delta) * SCALE).astype(BF)
            pb = p.astype(BF)
            dv = dv + jnp.where(m[:N_MEM], lax.dot_general(pb, dom, (((0,), (0,)), ((), ())), preferred_element_type=F32), 0.0)
            dk = dk + jnp.where(m[:N_MEM], lax.dot_general(ds, qm, (((0,), (0,)), ((), ())), preferred_element_type=F32), 0.0)
            dq = dq + jnp.where(m, jnp.dot(ds, k, preferred_element_type=F32), 0.0)
        dz_ref[...] = dq.astype(dz_ref.dtype)

        @pl.when(i == 0)
        def _():
            dk_acc[...] = dk
            dv_acc[...] = dv

        @pl.when(i > 0)
        def _():
            dk_acc[...] += dk
            dv_acc[...] += dv

        @pl.when(i == nt - 1)
        def _():
            dk_ref[...] = dk_acc[...]
            dv_ref[...] = dv_acc[...]

    qblk = pl.BlockSpec((MEM_TQ, MEM_W), lambda b, i: (b * nt + i, 3))
    kblk = pl.BlockSpec((N_MEM, MEM_W), lambda b, i: (b, 0))
    return pl.pallas_call(
        body, name=name, grid=(B, nt),
        in_specs=[qblk, qblk, kblk, pl.BlockSpec((N_MEM, MEM_W), lambda b, i: (b, 1)),
                  pl.BlockSpec((MEM_TQ, MEM_W), lambda b, i: (b * nt + i, 0)), pl.BlockSpec(memory_space=pl.ANY)],
        out_specs=[qblk, kblk, kblk],
        out_shape=[jax.ShapeDtypeStruct(dz_alias.shape, dz_alias.dtype),
                   jax.ShapeDtypeStruct((B * N_MEM, MEM_W), F32), jax.ShapeDtypeStruct((B * N_MEM, MEM_W), F32)],
        scratch_shapes=[pltpu.VMEM((N_MEM, MEM_W), F32), pltpu.VMEM((N_MEM, MEM_W), F32)],
        input_output_aliases={5: 0},
        compiler_params=_cp(("parallel", "arbitrary")),
    )(dy, z, kv, kv, lse, dz_alias)


def _dil_scores(qm, kp, kc, n):
    qi = lax.broadcasted_iota(jnp.int32, (STEPS, STEPS), 0)
    kj = lax.broadcasted_iota(jnp.int32, (STEPS, STEPS), 1)
    sc = lax.dot_general(qm, kc, (((1,), (1,)), ((), ())), preferred_element_type=F32) * SCALE
    sc = jnp.where(kj <= qi, sc, NEG)
    if kp is None:
        return None, sc
    sp = lax.dot_general(qm, kp, (((1,), (1,)), ((), ())), preferred_element_type=F32) * SCALE
    sp = jnp.where(jnp.logical_and(kj >= qi, n > 0), sp, NEG)
    return sp, sc


def _dil_specs(g, d, nb):
    chunk = STEPS * d
    cur = pl.BlockSpec((chunk, 128), lambda b, n, hf: (b * nb + n, g * 2 + hf))
    prev = pl.BlockSpec((chunk, 128), lambda b, n, hf: (b * nb + jnp.maximum(n - 1, 0), g * 2 + hf))
    return cur, prev


def _dil_rows(r, d):
    return pl.ds(r, STEPS, stride=d) if d > 1 else slice(None)


def _dil_loop(d, fn):
    if d <= 4:
        for r in range(d):
            fn(r)
    else:
        lax.fori_loop(0, d, lambda r, carry: (fn(r), carry)[1], 0)


def _dil_fwd_group(g, q, k, v, o_alias, l_alias, B, S, *, name):
    d = DIL[g]
    nb = S // (STEPS * d)
    has_prev = nb > 1

    def body(*refs):
        if has_prev:
            q_ref, kp_ref, kc_ref, vp_ref, vc_ref, _, __, o_ref, l_ref = refs
        else:
            q_ref, kc_ref, vc_ref, _, __, o_ref, l_ref = refs
        n = pl.program_id(1)

        def residue(r):
            rows = _dil_rows(r, d)
            q = q_ref[rows, :]
            kc, vc = kc_ref[rows, :].astype(BF), vc_ref[rows, :].astype(BF)
            kp = kp_ref[rows, :].astype(BF) if has_prev else None
            vp = vp_ref[rows, :].astype(BF) if has_prev else None
            o = jnp.zeros(q.shape, F32)
            lse_b = jnp.zeros(q.shape, F32)
            for m in _head_masks(q.shape):
                qm = jnp.where(m, q, 0.0).astype(BF)
                sp, sc = _dil_scores(qm, kp, kc, n)
                mx = jnp.max(sc, axis=-1, keepdims=True)
                if has_prev:
                    mx = jnp.maximum(mx, jnp.max(sp, axis=-1, keepdims=True))
                l = jnp.sum(jnp.exp(sc - mx), axis=-1, keepdims=True)
                if has_prev:
                    l = l + jnp.sum(jnp.exp(sp - mx), axis=-1, keepdims=True)
                lse = mx + jnp.log(l)
                oh = jnp.dot(jnp.exp(sc - lse).astype(BF), vc, preferred_element_type=F32)
                if has_prev:
                    oh = oh + jnp.dot(jnp.exp(sp - lse).astype(BF), vp, preferred_element_type=F32)
                o = o + jnp.where(m, oh, 0.0)
                lse_b = lse_b + jnp.where(m, lse, 0.0)
            o_ref[rows, :] = o
            l_ref[rows, :] = lse_b

        _dil_loop(d, residue)

    cur, prev = _dil_specs(g, d, nb)
    anyspec = pl.BlockSpec(memory_space=pl.ANY)
    if has_prev:
        in_specs, ops = [cur, prev, cur, prev, cur], [q, k, k, v, v]
    else:
        in_specs, ops = [cur, cur, cur], [q, k, v]
    n_in = len(ops)
    o, l = pl.pallas_call(
        body, name=name, grid=(B, nb, 2),
        in_specs=in_specs + [anyspec, anyspec],
        out_specs=[cur, cur],
        out_shape=[jax.ShapeDtypeStruct(q.shape, F32)] * 2,
        input_output_aliases={n_in: 0, n_in + 1: 1},
        compiler_params=_cp(("parallel", "parallel", "parallel")),
    )(*ops, o_alias, l_alias)
    return o, l


def _dil_bwd_group(g, q, k, v, do, cb, lse, aliases, B, S, *, name):
    d = DIL[g]
    nb = S // (STEPS * d)
    has_prev = nb > 1
    n_out = 5 if has_prev else 3

    def body(*refs):
        if has_prev:
            q_ref, kp_ref, kc_ref, vp_ref, vc_ref, do_ref, c_ref, l_ref = refs[:8]
            dq_ref, dkc_ref, dvc_ref, dkp_ref, dvp_ref = refs[8 + n_out:]
        else:
            q_ref, kc_ref, vc_ref, do_ref, c_ref, l_ref = refs[:6]
            dq_ref, dkc_ref, dvc_ref = refs[6 + n_out:]
        n = pl.program_id(1)
        tdot = lambda a, b: lax.dot_general(a, b, (((0,), (0,)), ((), ())), preferred_element_type=F32)
        ndot = lambda a, b: lax.dot_general(a, b, (((1,), (1,)), ((), ())), preferred_element_type=F32)

        def residue(r):
            rows = _dil_rows(r, d)
            q = q_ref[rows, :]
            kc, vc = kc_ref[rows, :].astype(BF), vc_ref[rows, :].astype(BF)
            kp = kp_ref[rows, :].astype(BF) if has_prev else None
            vp = vp_ref[rows, :].astype(BF) if has_prev else None
            do = do_ref[rows, :]
            cbv = c_ref[rows, :]
            lse_b = l_ref[rows, :]
            z = jnp.zeros(q.shape, F32)
            dq, dkc, dkp, dvc, dvp = z, z, z, z, z
            for m in _head_masks(q.shape):
                qm = jnp.where(m, q, 0.0).astype(BF)
                dom = jnp.where(m, do, 0.0).astype(BF)
                sp, sc = _dil_scores(qm, kp, kc, n)
                lse = _row_of(lse_b, m)
                c = _row_of(cbv, m)
                pc = jnp.exp(sc - lse)
                dsc = (pc * (ndot(dom, vc) + c) * SCALE).astype(BF)
                dqh = jnp.dot(dsc, kc, preferred_element_type=F32)
                dkc = dkc + jnp.where(m, tdot(dsc, qm), 0.0)
                dvc = dvc + jnp.where(m, tdot(pc.astype(BF), dom), 0.0)
                if has_prev:
                    pp = jnp.exp(sp - lse)
                    dsp = (pp * (ndot(dom, vp) + c) * SCALE).astype(BF)
                    dqh = dqh + jnp.dot(dsp, kp, preferred_element_type=F32)
                    dkp = dkp + jnp.where(m, tdot(dsp, qm), 0.0)
                    dvp = dvp + jnp.where(m, tdot(pp.astype(BF), dom), 0.0)
                dq = dq + jnp.where(m, dqh, 0.0)
            dq_ref[rows, :] = dq
            dkc_ref[rows, :] = dkc
            dvc_ref[rows, :] = dvc
            if has_prev:
                dkp_ref[rows, :] = dkp
                dvp_ref[rows, :] = dvp

        _dil_loop(d, residue)

    cur, prev = _dil_specs(g, d, nb)
    anyspec = pl.BlockSpec(memory_space=pl.ANY)
    dq_a, dkc_a, dkp_a, dvc_a, dvp_a = aliases
    if has_prev:
        in_specs, ops = [cur, prev, cur, prev, cur, cur, cur, cur], [q, k, k, v, v, do, cb, lse]
        al = [dq_a, dkc_a, dvc_a, dkp_a, dvp_a]
    else:
        in_specs, ops = [cur, cur, cur, cur, cur, cur], [q, k, v, do, cb, lse]
        al = [dq_a, dkc_a, dvc_a]
    n_in = len(ops)
    outs = pl.pallas_call(
        body, name=name, grid=(B, nb, 2),
        in_specs=in_specs + [anyspec] * n_out,
        out_specs=[cur] * n_out,
        out_shape=[jax.ShapeDtypeStruct(q.shape, F32)] * n_out,
        input_output_aliases={n_in + i: i for i in range(n_out)},
        compiler_params=_cp(("parallel", "parallel", "parallel")),
    )(*ops, *al)
    if has_prev:
        dq_a, dkc_a, dvc_a, dkp_a, dvp_a = outs
    else:
        dq_a, dkc_a, dvc_a = outs
    return dq_a, dkc_a, dkp_a, dvc_a, dvp_a


def _group_softmax(lse):
    l0, l1, l2 = lse[:, 0:256], lse[:, 256:512], lse[:, 512:768]
    mx = jnp.maximum(jnp.maximum(l0, l1), l2)
    e0, e1, e2 = jnp.exp(l0 - mx), jnp.exp(l1 - mx), jnp.exp(l2 - mx)
    tot = e0 + e1 + e2
    return e0 / tot, e1 / tot, e2 / tot


def _dil_combine_fwd(o, lse, y_alias, *, name, tm=512):
    T = o.shape[0]

    def body(o_ref, l_ref, _, y_ref):
        a = jnp.concatenate(_group_softmax(l_ref[...]), axis=1)
        y_ref[...] = (o_ref[...] * a).astype(y_ref.dtype)

    blk = pl.BlockSpec((tm, MAIN_W), lambda i: (i, 0))
    return pl.pallas_call(
        body, name=name, grid=(T // tm,), in_specs=[blk, blk, pl.BlockSpec(memory_space=pl.ANY)], out_specs=blk,
        out_shape=jax.ShapeDtypeStruct(y_alias.shape, y_alias.dtype), input_output_aliases={2: 0},
        compiler_params=_cp(("parallel",)),
    )(o, lse, y_alias)


def _dil_combine_bwd(dy, o, lse, *, name, tm=256):
    T = o.shape[0]
    lane_r = lax.broadcasted_iota(jnp.int32, (256, 256), 0) // HEAD_DIM
    lane_c = lax.broadcasted_iota(jnp.int32, (256, 256), 1) // HEAD_DIM
    ones_bd = (lane_r == lane_c).astype(BF)

    def body(dy_ref, o_ref, l_ref, e_ref, do_ref, c_ref):
        dyv = dy_ref[...]
        alphas = _group_softmax(l_ref[...])
        prod = dyv * o_ref[...]
        e = e_ref[...]
        tot = jnp.zeros((tm, 256), F32)
        for gi in range(3):
            x = prod[:, gi * 256:(gi + 1) * 256]
            hi = x.astype(BF)
            lo = (x - hi.astype(F32)).astype(BF)
            dalpha = jnp.dot(hi, e, preferred_element_type=F32) + jnp.dot(lo, e, preferred_element_type=F32)
            tot = tot + alphas[gi] * dalpha
        a = jnp.concatenate(alphas, axis=1)
        do_ref[...] = (dyv * a).astype(do_ref.dtype)
        c_ref[...] = jnp.concatenate([-al * tot for al in alphas], axis=1)

    blk = pl.BlockSpec((tm, MAIN_W), lambda i: (i, 0))
    return pl.pallas_call(
        body, name=name, grid=(T // tm,),
        in_specs=[blk, blk, blk, pl.BlockSpec((256, 256), lambda i: (0, 0))], out_specs=[blk, blk],
        out_shape=[jax.ShapeDtypeStruct((T, MAIN_W), F32), jax.ShapeDtypeStruct((T, MAIN_W), F32)],
        compiler_params=_cp(("parallel",)),
    )(dy, o, lse, ones_bd)


def _kv_grad(parts, cos, sin, B, S, *, name):
    T = B * S
    tb = S // STEPS
    n_l = len(parts)

    def shifted(g):
        def f(b, t):
            return (b * tb + jnp.minimum(t + DIL[g], tb - 1), g)
        return f

    with_prev = [g for g in range(3) if DIL[g] < tb]
    n_p = len(with_prev)
    per_l = 2 + 2 * n_p

    def body(*refs):
        c_ref, s_ref = refs[0], refs[1]
        ins = refs[2:2 + n_l * per_l]
        dk_ref, dv_ref = refs[2 + n_l * per_l:]
        t = pl.program_id(1)
        dk = jnp.zeros((STEPS, MAIN_W), F32)
        dv = jnp.zeros((STEPS, MAIN_W), F32)
        zero = jnp.zeros((STEPS, 256), F32)
        for li in range(n_l):
            base = li * per_l
            dk = dk + ins[base][...]
            dv = dv + ins[base + 1][...]
            kparts, vparts = [zero] * 3, [zero] * 3
            for pi, g in enumerate(with_prev):
                ok = t + DIL[g] < tb
                kparts[g] = jnp.where(ok, ins[base + 2 + pi][...], 0.0)
                vparts[g] = jnp.where(ok, ins[base + 2 + n_p + pi][...], 0.0)
            dk = dk + jnp.concatenate(kparts, axis=1)
            dv = dv + jnp.concatenate(vparts, axis=1)
        dk_ref[...] = _rot(dk, c_ref[...], s_ref[...], -1.0).astype(dk_ref.dtype)
        dv_ref[...] = dv.astype(dv_ref.dtype)

    full = pl.BlockSpec((STEPS, MAIN_W), lambda b, t: (b * tb + t, 0))
    tab = pl.BlockSpec((STEPS, 128), lambda b, t: (b * tb + t, 0))
    in_specs, ops = [tab, tab], [cos, sin]
    for (kc, kp, vc, vp) in parts:
        in_specs += [full, full] + [pl.BlockSpec((STEPS, 256), shifted(g)) for g in with_prev] * 2
        ops += [kc, vc] + [kp] * n_p + [vp] * n_p
    return pl.pallas_call(
        body, name=name, grid=(B, tb), in_specs=in_specs, out_specs=[full, full],
        out_shape=[jax.ShapeDtypeStruct((T, MAIN_W), BF)] * 2,
        compiler_params=_cp(("parallel", "parallel")),
    )(*ops)


def _loss(y, target, *, name, tm=512):
    T, Dm = y.shape
    nt = T // tm

    def body(y_ref, t_ref, l_ref, d_ref, acc):
        i = pl.program_id(0)
        err = y_ref[...] - t_ref[...]
        d_ref[...] = err / Dm
        part = jnp.sum(jnp.mean(err * err, axis=-1, keepdims=True).reshape(tm // 8, 8, 1), axis=0)

        @pl.when(i == 0)
        def _():
            acc[...] = part

        @pl.when(i > 0)
        def _():
            acc[...] += part

        @pl.when(i == nt - 1)
        def _():
            l_ref[...] = 0.5 * jnp.sum(acc[...], axis=0, keepdims=True)

    row = pl.BlockSpec((tm, Dm), lambda i: (i, 0))
    return pl.pallas_call(
        body, name=name, grid=(nt,), in_specs=[row, row],
        out_specs=[pl.BlockSpec((1, 1), lambda i: (0, 0)), row],
        out_shape=[jax.ShapeDtypeStruct((1, 1), F32), jax.ShapeDtypeStruct((T, Dm), F32)],
        scratch_shapes=[pltpu.VMEM((8, 1), F32)],
        compiler_params=_cp(("arbitrary",)),
    )(y, target)


def _adamw(w, g, m, v, *, name):
    shape = w.shape
    cols = shape[-1]
    rows = w.size // cols
    tm = rows
    for cand in (512, 352, 256, 128):
        if rows > cand and rows % cand == 0 and cand * cols * 4 <= (1 << 20):
            tm = cand
            break

    def body(w_ref, g_ref, m_ref, v_ref, d_ref, mo_ref, vo_ref):
        gv = g_ref[...]
        mn = ADAM_B1 * m_ref[...] + (1.0 - ADAM_B1) * gv
        vn = ADAM_B2 * v_ref[...] + (1.0 - ADAM_B2) * (gv * gv)
        m_hat = mn / (1.0 - ADAM_B1 ** ADAM_STEP)
        v_hat = vn / (1.0 - ADAM_B2 ** ADAM_STEP)
        d_ref[...] = -ADAM_LR * (m_hat / (jnp.sqrt(v_hat) + ADAM_EPS) + ADAM_WD * w_ref[...])
        mo_ref[...] = mn
        vo_ref[...] = vn

    blk = pl.BlockSpec((tm, cols), lambda i: (i, 0))
    outs = pl.pallas_call(
        body, name=name, grid=(rows // tm,), in_specs=[blk] * 4, out_specs=[blk] * 3,
        out_shape=[jax.ShapeDtypeStruct((rows, cols), F32)] * 3,
        compiler_params=_cp(("parallel",)),
    )(*[t.reshape(rows, cols) for t in (w, g, m, v)])
    return tuple(t.reshape(shape) for t in outs)


BIG = {
    'w_in': ((DEPTH, D_MODEL, D_MODEL), 'row'),
    'w_mem_kv': ((DEPTH, D_MODEL, 2 * MEM_W), 'row'),
    'w_out': ((DEPTH, D_MODEL, D_MODEL), 'row'),
    'w_kv': ((1, D_MODEL, 2 * MAIN_W), 'col'),
    'w_gate_up': ((DEPTH, D_MODEL, 2 * D_FF), 'col'),
    'w_down': ((DEPTH, D_FF, D_MODEL), 'row'),
}
BIG_NAMES = tuple(BIG)
N_CHIPS = 4
HBM_ANY = pl.BlockSpec(memory_space=pl.ANY)


def _geom(name):
    (L, R, C), kind = BIG[name]
    if kind == 'row':
        return L, R, C, kind, R // N_CHIPS, C, R // (2 * N_CHIPS)
    return L, R, C, kind, R, C // N_CHIPS, R // 2


def _shard_shape(name):
    L, R, C, kind, rs, cs, rh = _geom(name)
    return (L, rs, cs)


def _half_shape(name):
    L, R, C, kind, rs, cs, rh = _geom(name)
    return (L, rh, cs)


def _full_win(ref, name, s, h):
    L, R, C, kind, rs, cs, rh = _geom(name)
    if kind == 'row':
        rows = pl.ds(s * rs, rs) if h is None else pl.ds(s * rs + h * rh, rh)
        return ref.at[:, rows, :]
    rows = slice(None) if h is None else pl.ds(h * rh, rh)
    return ref.at[:, rows, pl.ds(s * cs, cs)]


def _shard_half(ref, name, h):
    L, R, C, kind, rs, cs, rh = _geom(name)
    return ref.at[:, pl.ds(h * rh, rh), :]


def _halves_win(ref, name, s):
    L, R, C, kind, rs, cs, rh = _geom(name)
    if kind == 'row':
        return ref.at[:, pl.ds(s * rh, rh), :]
    return ref.at[:, :, pl.ds(s * cs, cs)]


def _halves_shape(name):
    L, R, C, kind, rs, cs, rh = _geom(name)
    return (L, N_CHIPS * rh, cs) if kind == 'row' else (L, rh, C)


def _place():
    x, y, c = lax.axis_index("x"), lax.axis_index("y"), lax.axis_index("c")
    chips = [(1 - x, y), (x, 1 - y), (1 - x, 1 - y)]
    return x, y, c, chips


SMALL_ROWS = 24


def _all_gather(shards, small):
    names = BIG_NAMES
    nw = len(names)

    def body(*refs):
        src = dict(zip(names, refs[:nw]))
        small_ref = refs[nw]
        dst = dict(zip(names, refs[nw + 1:2 * nw + 1]))
        small_out = refs[2 * nw + 1]
        send_sems, recv_sems, local_sems = refs[2 * nw + 2:]
        x, y, c, chips = _place()
        s = 2 * x + y
        sib = (x, y, 1 - c)

        def remote(k, src_ref, dst_ref, to):
            return pltpu.make_async_remote_copy(src_ref=src_ref, dst_ref=dst_ref, send_sem=send_sems.at[k],
                                                recv_sem=recv_sems.at[k], device_id=to, device_id_type=MESH)

        local = []
        for wi, nm in enumerate(names):
            local.append(pltpu.make_async_copy(src[nm], _full_win(dst[nm], nm, s, None), local_sems.at[wi]))
        local.append(pltpu.make_async_copy(small_ref, small_out.at[s], local_sems.at[nw]))
        for cp in local:
            cp.start()
        sends = []
        for j, (px, py) in enumerate(chips):
            for wi, nm in enumerate(names):
                sends.append(remote(wi * 6 + j, _shard_half(src[nm], nm, c), _full_win(dst[nm], nm, s, c), (px, py, c)))
            sends.append(remote(nw * 6 + j, small_ref, small_out.at[s], (px, py, c)))
        for cp in sends:
            cp.start()
        for j, (px, py) in enumerate(chips):
            sp = 2 * px + py
            for wi, nm in enumerate(names):
                w = _full_win(dst[nm], nm, sp, c)
                remote(wi * 6 + j, w, w, sib).wait_recv()
                fwd = remote(wi * 6 + 3 + j, w, w, sib)
                fwd.start()
                sends.append(fwd)
            remote(nw * 6 + j, small_ref, small_out.at[sp], sib).wait_recv()
        for j, (px, py) in enumerate(chips):
            sp = 2 * px + py
            for wi, nm in enumerate(names):
                w = _full_win(dst[nm], nm, sp, 1 - c)
                remote(wi * 6 + 3 + j, w, w, sib).wait_recv()
        for cp in sends:
            cp.wait_send()
        for cp in local:
            cp.wait()

    n_sem = nw * 6 + 3
    outs = pl.pallas_call(
        body, name="all_gather_weights",
        in_specs=[HBM_ANY] * (nw + 1), out_specs=[HBM_ANY] * (nw + 1),
        out_shape=[jax.ShapeDtypeStruct(BIG[nm][0], BF) for nm in names]
        + [jax.ShapeDtypeStruct((N_CHIPS, SMALL_ROWS, 256), F32)],
        scratch_shapes=[pltpu.SemaphoreType.DMA((n_sem,)), pltpu.SemaphoreType.DMA((n_sem,)),
                        pltpu.SemaphoreType.DMA((nw + 1,))],
    )(*[shards[nm] for nm in names], small)
    return dict(zip(names, outs[:nw])), outs[nw]


def _exchange_halves(grads):
    names = BIG_NAMES
    nw = len(names)

    def body(*refs):
        src = dict(zip(names, refs[:nw]))
        dst = dict(zip(names, refs[nw:2 * nw]))
        send_sems, recv_sems = refs[2 * nw:]
        x, y, c, _ = _place()
        sib = (x, y, 1 - c)
        sends, recvs = [], []
        k = 0
        for nm in names:
            kind = BIG[nm][1]
            for sp in range(N_CHIPS if kind == 'row' else 1):
                if kind == 'row':
                    out_w = _full_win(src[nm], nm, sp, 1 - c)
                    in_w = _halves_win(dst[nm], nm, sp)
                else:
                    L, R, C, _, rs, cs, rh = _geom(nm)
                    out_w = src[nm].at[:, pl.ds((1 - c) * rh, rh), :]
                    in_w = dst[nm]
                cp = pltpu.make_async_remote_copy(src_ref=out_w, dst_ref=in_w, send_sem=send_sems.at[k],
                                                  recv_sem=recv_sems.at[k], device_id=sib, device_id_type=MESH)
                cp.start()
                sends.append(cp)
                k += 1
        for cp in sends:
            cp.wait_recv()
        for cp in sends:
            cp.wait_send()

    n_sem = sum(N_CHIPS if BIG[nm][1] == 'row' else 1 for nm in names)
    outs = pl.pallas_call(
        body, name="exchange_grad_halves",
        in_specs=[HBM_ANY] * nw, out_specs=[HBM_ANY] * nw,
        out_shape=[jax.ShapeDtypeStruct(_halves_shape(nm), F32) for nm in names],
        scratch_shapes=[pltpu.SemaphoreType.DMA((n_sem,)), pltpu.SemaphoreType.DMA((n_sem,))],
    )(*[grads[nm] for nm in names])
    return dict(zip(names, outs))


def _add_halves(name, g, r, sc):
    L, R, C, kind, rs, cs, rh = _geom(name)
    tr = rh if kind == 'row' else 256
    nr = rh // tr

    def body(sc_ref, g_ref, r_ref, hb_ref, own_ref):
        sp = pl.program_id(2)
        tot = g_ref[...] + r_ref[...]
        hb_ref[...] = tot.astype(hb_ref.dtype)

        @pl.when(sp == sc_ref[0])
        def _():
            own_ref[...] = tot

    if kind == 'row':
        g_map = lambda l, ri, sp, sc_ref: (l, sp * 2 + sc_ref[1], 0)
        h_map = lambda l, ri, sp, sc_ref: (l, sp, 0)
    else:
        g_map = lambda l, ri, sp, sc_ref: (l, sc_ref[1] * nr + ri, sp)
        h_map = lambda l, ri, sp, sc_ref: (l, ri, sp)
    own_map = lambda l, ri, sp, sc_ref: (l, ri, 0)
    blk = (None, tr, cs)
    return pl.pallas_call(
        body, name="add_halves_" + name,
        grid_spec=pltpu.PrefetchScalarGridSpec(
            num_scalar_prefetch=1, grid=(L, nr, N_CHIPS),
            in_specs=[pl.BlockSpec(blk, g_map), pl.BlockSpec(blk, h_map)],
            out_specs=[pl.BlockSpec(blk, h_map), pl.BlockSpec(blk, own_map)]),
        out_shape=[jax.ShapeDtypeStruct(_halves_shape(name), BF), jax.ShapeDtypeStruct(_half_shape(name), F32)],
        compiler_params=_cp(("parallel", "parallel", "arbitrary")),
    )(sc, g, r)


def _scatter_to_chips(halves):
    names = BIG_NAMES
    nw = len(names)

    def body(*refs):
        src = dict(zip(names, refs[:nw]))
        dst = dict(zip(names, refs[nw:2 * nw]))
        send_sems, recv_sems = refs[2 * nw:]
        x, y, c, chips = _place()
        sends = []
        for j, (px, py) in enumerate(chips):
            sp = 2 * px + py
            for wi, nm in enumerate(names):
                cp = pltpu.make_async_remote_copy(
                    src_ref=_halves_win(src[nm], nm, sp), dst_ref=dst[nm].at[j], send_sem=send_sems.at[wi * 3 + j],
                    recv_sem=recv_sems.at[wi * 3 + j], device_id=(px, py, c), device_id_type=MESH)
                cp.start()
                sends.append(cp)
        for cp in sends:
            cp.wait_recv()
        for cp in sends:
            cp.wait_send()

    outs = pl.pallas_call(
        body, name="scatter_grads_to_chips",
        in_specs=[HBM_ANY] * nw, out_specs=[HBM_ANY] * nw,
        out_shape=[jax.ShapeDtypeStruct((3,) + _half_shape(nm), BF) for nm in names],
        scratch_shapes=[pltpu.SemaphoreType.DMA((nw * 3,)), pltpu.SemaphoreType.DMA((nw * 3,))],
    )(*[halves[nm] for nm in names])
    return dict(zip(names, outs))


def _sum_pieces(name, own, pieces, sc):
    L, R, C, kind, rs, cs, rh = _geom(name)
    tr = rh if kind == 'row' else 256
    nr = rh // tr

    def body(sc_ref, o_ref, p_ref, out_ref):
        out_ref[...] = o_ref[...] + p_ref[0].astype(F32) + p_ref[1].astype(F32) + p_ref[2].astype(F32)

    blk = (None, tr, cs)
    return pl.pallas_call(
        body, name="sum_pieces_" + name,
        grid_spec=pltpu.PrefetchScalarGridSpec(
            num_scalar_prefetch=1, grid=(L, nr),
            in_specs=[pl.BlockSpec(blk, lambda l, ri, sc_ref: (l, ri, 0)),
                      pl.BlockSpec((3, None, tr, cs), lambda l, ri, sc_ref: (0, l, ri, 0))],
            out_specs=pl.BlockSpec(blk, lambda l, ri, sc_ref: (l, sc_ref[1] * nr + ri, 0))),
        out_shape=jax.ShapeDtypeStruct(_shard_shape(name), F32),
        compiler_params=_cp(("parallel", "parallel")),
    )(sc, own, pieces)


def _share_with_sibling(gshards):
    names = BIG_NAMES
    nw = len(names)

    def body(*refs):
        bufs = dict(zip(names, refs[nw:2 * nw]))
        send_sems, recv_sems = refs[2 * nw:]
        x, y, c, _ = _place()
        sib = (x, y, 1 - c)
        sends = []
        for wi, nm in enumerate(names):
            w = _shard_half(bufs[nm], nm, c)
            cp = pltpu.make_async_remote_copy(src_ref=w, dst_ref=w, send_sem=send_sems.at[wi], recv_sem=recv_sems.at[wi],
                                              device_id=sib, device_id_type=MESH)
            cp.start()
            sends.append(cp)
        for wi, nm in enumerate(names):
            w = _shard_half(bufs[nm], nm, 1 - c)
            pltpu.make_async_remote_copy(src_ref=w, dst_ref=w, send_sem=send_sems.at[wi], recv_sem=recv_sems.at[wi],
                                         device_id=sib, device_id_type=MESH).wait_recv()
        for cp in sends:
            cp.wait_send()

    outs = pl.pallas_call(
        body, name="share_grad_shards",
        in_specs=[HBM_ANY] * nw, out_specs=[HBM_ANY] * nw,
        out_shape=[jax.ShapeDtypeStruct(_shard_shape(nm), F32) for nm in names],
        input_output_aliases={i: i for i in range(nw)},
        scratch_shapes=[pltpu.SemaphoreType.DMA((nw,)), pltpu.SemaphoreType.DMA((nw,))],
    )(*[gshards[nm] for nm in names])
    return dict(zip(names, outs))


def _all_gather_small(v):
    rows = v.shape[0]

    def body(v_ref, out_ref, send_sems, recv_sems, local_sem):
        x, y, c, _ = _place()
        me = 4 * x + 2 * y + c
        mine = pltpu.make_async_copy(v_ref, out_ref.at[me], local_sem)
        mine.start()
        sends = []
        flips = [(fx, fy, fc) for fx in (0, 1) for fy in (0, 1) for fc in (0, 1)][1:]
        for k, (fx, fy, fc) in enumerate(flips):
            px, py, pc = (1 - x if fx else x), (1 - y if fy else y), (1 - c if fc else c)
            cp = pltpu.make_async_remote_copy(src_ref=v_ref, dst_ref=out_ref.at[me], send_sem=send_sems.at[k],
                                              recv_sem=recv_sems.at[k], device_id=(px, py, pc), device_id_type=MESH)
            cp.start()
            sends.append((cp, 4 * px + 2 * py + pc))
        for k, (cp, peer) in enumerate(sends):
            pltpu.make_async_remote_copy(src_ref=v_ref, dst_ref=out_ref.at[peer], send_sem=send_sems.at[k],
                                         recv_sem=recv_sems.at[k], device_id=(x, y, c), device_id_type=MESH).wait_recv()
        for cp, _ in sends:
            cp.wait_send()
        mine.wait()

    return pl.pallas_call(
        body, name="all_gather_small_grads",
        in_specs=[HBM_ANY], out_specs=HBM_ANY,
        out_shape=jax.ShapeDtypeStruct((8, rows, 128), F32),
        scratch_shapes=[pltpu.SemaphoreType.DMA((7,)), pltpu.SemaphoreType.DMA((7,)), pltpu.SemaphoreType.DMA],
    )(v)


def _sum8(v8, *, name, tr=336):
    rows = v8.shape[1]
    tr = min(tr, rows)
    assert rows % tr == 0

    def body(v_ref, o_ref):
        tot = v_ref[0]
        for d in range(1, 8):
            tot = tot + v_ref[d]
        o_ref[...] = tot

    return pl.pallas_call(
        body, name=name, grid=(rows // tr,),
        in_specs=[pl.BlockSpec((8, tr, 128), lambda i: (0, i, 0))], out_specs=pl.BlockSpec((tr, 128), lambda i: (i, 0)),
        out_shape=jax.ShapeDtypeStruct((rows, 128), F32),
        compiler_params=_cp(("parallel",)),
    )(v8)


def _block_diag(w_pool_l):
    wbd = jnp.zeros((MAIN_W, MAIN_W), F32)
    for gi in range(len(POOL_WINDOWS)):
        wbd = lax.dynamic_update_slice(wbd, w_pool_l[gi], (gi * POOL_GROUP, gi * POOL_GROUP))
    return wbd.astype(BF)


def _local_step(x, mem, positions, wf, norm_gains, mem_norm, w_pool, pool_scale, kv_norm, target):
    B, S, _ = x.shape
    T = B * S
    xc = x.reshape(T, D_MODEL)
    memf = mem.reshape(B * N_MEM, D_MODEL)
    tgt = target.reshape(T, D_MODEL)
    cos, sin = _rope_tables(positions.reshape(T, 1), name="rope_tables")
    wbd = [_block_diag(w_pool[l]) for l in range(N_A)]
    nbo = D_FF // 256
    fw = []
    rk = rv = None
    kv_saved = None

    for l in range(DEPTH):
        t = f"l{l}_"
        sv = {'x_in': xc}
        h0, sv['r0'] = _norm_fwd(xc, norm_gains[l, 0], name=t + "norm0", out_dtype=BF)
        z, = _mm(h0, wf['w_in'], b_layer=l, name=t + "mm_in")
        memn, sv['rm'] = _norm_fwd(memf, mem_norm[l], name=t + "norm_mem", out_dtype=BF, tm=256)
        kvm, = _mm(memn, wf['w_mem_kv'], b_layer=l, name=t + "mm_memkv", out_dtypes=(BF,))
        if l < N_A:
            ycat, sv['p'] = _pool_fwd(z, wbd[l], pool_scale[l], B, S, name=t + "pool_fwd")
        else:
            rq = _rope_apply(z, cos, sin, name=t + "rope_q", out_dtype=F32)
            o = lax.empty((T, MAIN_W), F32)
            lse = lax.empty((T, MAIN_W), F32)
            for g in range(3):
                o, lse = _dil_fwd_group(g, rq, rk, rv, o, lse, B, S, name=t + f"dil_fwd{g}")
            ycat = _dil_combine_fwd(o, lse, lax.empty((T, D_MODEL), BF), name=t + "dil_combine")
            sv.update(rq=rq, o=o, lse=lse)
        ycat, sv['lse_m'] = _memattn_fwd(z, kvm, ycat, B, S, name=t + "memattn_fwd")
        y1, = _mm(ycat, wf['w_out'], b_layer=l, name=t + "mm_out")
        x1, sv['r1'] = _norm_fwd(y1, norm_gains[l, 1], name=t + "norm1", res=xc)
        h2, sv['r2'] = _norm_fwd(x1, norm_gains[l, 2], name=t + "norm2", out_dtype=BF)
        gg, uu, aa = _mm(h2, wf['w_gate_up'], b_layer=l, b_offsets=(0, nbo), out_n=D_FF, tn=256, name=t + "mm_gate_up",
                         epilogue=_swiglu_fwd_epilogue, out_dtypes=(BF, BF, BF))
        y2, = _mm(aa, wf['w_down'], b_layer=l, tk=D_FF, name=t + "mm_down")
        x2, sv['r3'] = _norm_fwd(y2, norm_gains[l, 3], name=t + "norm3", res=x1)
        sv.update(h0=h0, z=z, memn=memn, kvm=kvm, ycat=ycat, y1=y1, x1=x1, h2=h2, gg=gg, uu=uu, aa=aa, y2=y2)
        fw.append(sv)
        xc = x2
        if l == N_A - 1:
            kvn, rkv = _norm_fwd(xc, kv_norm, name="norm_kv", out_dtype=BF)
            kv, = _mm(kvn, wf['w_kv'], b_layer=0, name="mm_kv")
            rk, rv = _rope_apply(kv, cos, sin, name="rope_k", passthrough=True, out_dtype=F32)
            kv_saved = (xc, kvn, rkv)

    loss, dx = _loss(xc, tgt, name="loss")

    gbig = {nm: lax.empty(BIG[nm][0], F32) for nm in BIG_NAMES}
    d_ng = [[None] * 4 for _ in range(DEPTH)]
    d_memnorm = [None] * DEPTH
    d_wbd = [None] * N_A
    d_pscale = [None] * N_A
    d_kvnorm = None
    kv_parts = []

    for l in reversed(range(DEPTH)):
        t = f"l{l}_b_"
        sv = fw[l]
        dy2, d_ng[l][3] = _norm_bwd(dx, sv['y2'], sv['r3'], norm_gains[l, 3], name=t + "norm3", out_dtype=BF)
        gbig['w_down'] = _mm(sv['aa'], dy2, ta=True, tm=1408, name=t + "dw_down", stack=(gbig['w_down'], l))
        dg, du = _mm(dy2, wf['w_down'], tb=True, b_layer=l, tn=256, name=t + "d_act",
                     extras=((sv['gg'], 'tile'), (sv['uu'], 'tile')), epilogue=_swiglu_bwd_epilogue, out_dtypes=(BF, BF))
        gbig['w_gate_up'] = _mm(sv['h2'], (dg, du), ta=True, tn=1408, tk=512, name=t + "dw_gate_up",
                                stack=(gbig['w_gate_up'], l))
        dh2, = _mm((dg, du), wf['w_gate_up'], tb=True, b_layer=l, tk=1408, name=t + "d_h2", out_dtypes=(BF,))
        dx1, d_ng[l][2] = _norm_bwd(dh2, sv['x1'], sv['r2'], norm_gains[l, 2], name=t + "norm2", add=dx)
        dy1, d_ng[l][1] = _norm_bwd(dx1, sv['y1'], sv['r1'], norm_gains[l, 1], name=t + "norm1", out_dtype=BF)
        gbig['w_out'] = _mm(sv['ycat'], dy1, ta=True, name=t + "dw_out", stack=(gbig['w_out'], l))
        dycat, = _mm(dy1, wf['w_out'], tb=True, b_layer=l, name=t + "d_ycat")
        dz = lax.empty((T, D_MODEL), BF)
        dz, dkm, dvm = _memattn_bwd(dycat, sv['z'], sv['kvm'], sv['lse_m'], dz, B, S, name=t + "memattn")
        if l < N_A:
            dz, d_wbd[l], d_pscale[l] = _pool_bwd(dycat, sv['p'], wbd[l], pool_scale[l], dz, B, S, name=t + "pool")
        else:
            do, cb = _dil_combine_bwd(dycat, sv['o'], sv['lse'], name=t + "dil_combine")
            acc = tuple(lax.empty((T, MAIN_W), F32) for _ in range(5))
            for g in range(3):
                acc = _dil_bwd_group(g, sv['rq'], rk, rv, do, cb, sv['lse'], acc, B, S, name=t + f"dil{g}")
            dz = _rope_apply(acc[0], cos, sin, name=t + "rope_q", sign=-1.0, alias=dz)
            kv_parts.append(acc[1:])
        gbig['w_in'] = _mm(sv['h0'], dz, ta=True, name=t + "dw_in", stack=(gbig['w_in'], l))
        dh0, = _mm(dz, wf['w_in'], tb=True, b_layer=l, name=t + "d_h0", out_dtypes=(BF,))
        dx, d_ng[l][0] = _norm_bwd(dh0, sv['x_in'], sv['r0'], norm_gains[l, 0], name=t + "norm0", add=dx1)
        gbig['w_mem_kv'] = _mm(sv['memn'], (dkm, dvm), ta=True, tn=256, name=t + "dw_memkv", stack=(gbig['w_mem_kv'], l))
        dmemn, = _mm((dkm, dvm), wf['w_mem_kv'], tb=True, b_layer=l, tk=256, name=t + "d_memn", out_dtypes=(BF,))
        _, d_memnorm[l] = _norm_bwd(dmemn, memf, sv['rm'], mem_norm[l], name=t + "norm_mem", out_dtype=BF, tm=256)
        if l == N_A:
            dk, dv = _kv_grad(kv_parts, cos, sin, B, S, name="kv_grad")
            x_kv, kvn, rkv = kv_saved
            gbig['w_kv'] = _mm(kvn, (dk, dv), ta=True, tn=768, name="dw_kv", stack=(gbig['w_kv'], 0))
            dkvn, = _mm((dk, dv), wf['w_kv'], tb=True, b_layer=0, tk=768, name="d_kvn", out_dtypes=(BF,))
            dx, d_kvnorm = _norm_bwd(dkvn, x_kv, rkv, kv_norm, name="norm_kv_b", add=dx)

    small = {
        'norm_gains': jnp.stack([jnp.concatenate(d_ng[l], axis=0) for l in range(DEPTH)]),
        'mem_norm': jnp.concatenate(d_memnorm, axis=0),
        'kv_norm': d_kvnorm.reshape(D_MODEL),
        'pool_scale': jnp.concatenate(d_pscale, axis=0),
        'w_pool': jnp.stack([jnp.stack([d_wbd[l][gi * POOL_GROUP:(gi + 1) * POOL_GROUP, gi * POOL_GROUP:(gi + 1) * POOL_GROUP]
                                        for gi in range(len(POOL_WINDOWS))]) for l in range(N_A)]),
    }
    return loss, dx, gbig, small


SMALL_ORDER = ('norm_gains', 'mem_norm', 'kv_norm', 'pool_scale', 'w_pool')
SMALL_VEC_ROWS = 2560


def kernel(x, mem, positions, norm_gains, mem_norm, w_in, w_mem_kv, w_out, w_pool, pool_scale, kv_norm, w_kv, w_gate_up, w_down, loss_target, m_norm_gains, m_mem_norm, m_w_in, m_w_mem_kv, m_w_out, m_w_pool, m_pool_scale, m_kv_norm, m_w_kv, m_w_gate_up, m_w_down, v_norm_gains, v_mem_norm, v_w_in, v_w_mem_kv, v_w_out, v_w_pool, v_pool_scale, v_kv_norm, v_w_kv, v_w_gate_up, v_w_down):
    xi, yi, ci = lax.axis_index("x"), lax.axis_index("y"), lax.axis_index("c")
    s = 2 * xi + yi
    sc = jnp.stack([s, ci]).astype(jnp.int32)
    weights = dict(norm_gains=norm_gains, mem_norm=mem_norm, w_in=w_in, w_mem_kv=w_mem_kv, w_out=w_out, w_pool=w_pool,
                   pool_scale=pool_scale, kv_norm=kv_norm, w_kv=w_kv, w_gate_up=w_gate_up, w_down=w_down)
    moms = dict(norm_gains=m_norm_gains, mem_norm=m_mem_norm, w_in=m_w_in, w_mem_kv=m_w_mem_kv, w_out=m_w_out,
                w_pool=m_w_pool, pool_scale=m_pool_scale, kv_norm=m_kv_norm, w_kv=m_w_kv, w_gate_up=m_w_gate_up,
                w_down=m_w_down)
    vels = dict(norm_gains=v_norm_gains, mem_norm=v_mem_norm, w_in=v_w_in, w_mem_kv=v_w_mem_kv, w_out=v_w_out,
                w_pool=v_w_pool, pool_scale=v_pool_scale, kv_norm=v_kv_norm, w_kv=v_w_kv, w_gate_up=v_w_gate_up,
                w_down=v_w_down)

    shards = {nm: weights[nm].astype(BF).reshape(_shard_shape(nm)) for nm in BIG_NAMES}
    small_w = jnp.zeros((SMALL_ROWS, 256), F32)
    small_w = lax.dynamic_update_slice(small_w, norm_gains.reshape(16, 256), (0, 0))
    small_w = lax.dynamic_update_slice(small_w, pool_scale, (16, 0))
    wf, small_all = _all_gather(shards, small_w)
    ng_full = small_all[:, :16, :].reshape(N_CHIPS, DEPTH, 4, 256).transpose(1, 2, 0, 3).reshape(DEPTH, 4, D_MODEL)
    ps_full = small_all[:, 16:18, :POOL_GROUP].transpose(1, 0, 2).reshape(N_A, MAIN_W)

    loss, gx, gbig, gsmall = _local_step(x, mem, positions, wf, ng_full, mem_norm, w_pool, ps_full, kv_norm, loss_target)
    loss = lax.psum(loss[0, 0], ("x", "y", "c"))

    from_sib = _exchange_halves(gbig)
    halves, own = {}, {}
    for nm in BIG_NAMES:
        halves[nm], own[nm] = _add_halves(nm, gbig[nm], from_sib[nm], sc)
    pieces = _scatter_to_chips(halves)
    gsh = _share_with_sibling({nm: _sum_pieces(nm, own[nm], pieces[nm], sc) for nm in BIG_NAMES})

    vec = jnp.concatenate([gsmall[nm].reshape(-1) for nm in SMALL_ORDER])
    vec = jnp.pad(vec, (0, SMALL_VEC_ROWS * 128 - vec.shape[0])).reshape(SMALL_VEC_ROWS, 128)
    tot = _sum8(_all_gather_small(vec), name="sum_small_grads", tr=512).reshape(-1)
    grads, off = {}, 0
    for nm in SMALL_ORDER:
        shape = (DEPTH, 4, D_MODEL) if nm == 'norm_gains' else (N_A, MAIN_W) if nm == 'pool_scale' else weights[nm].shape
        n = 1
        for dim in shape:
            n *= dim
        grads[nm] = tot[off:off + n].reshape(shape)
        off += n
    grads['norm_gains'] = lax.dynamic_slice(grads['norm_gains'], (0, 0, s * 256), (DEPTH, 4, 256))
    grads['pool_scale'] = lax.dynamic_slice(grads['pool_scale'], (0, s * POOL_GROUP), (N_A, POOL_GROUP))
    for nm in BIG_NAMES:
        grads[nm] = gsh[nm].reshape(weights[nm].shape)

    order = ('norm_gains', 'mem_norm', 'w_in', 'w_mem_kv', 'w_out', 'w_pool', 'pool_scale', 'kv_norm', 'w_kv',
             'w_gate_up', 'w_down')
    deltas, new_m, new_v = {}, {}, {}
    for nm in order:
        deltas[nm], new_m[nm], new_v[nm] = _adamw(weights[nm], grads[nm], moms[nm], vels[nm], name="adamw_" + nm)
    return (loss, gx.reshape(x.shape), *[grads[nm] for nm in order], *[deltas[nm] for nm in order],
            *[new_m[nm] for nm in order], *[new_v[nm] for nm in order])
```

```python
import functools

import jax
import jax.numpy as jnp
from jax import lax
from jax.experimental import pallas as pl
from jax.experimental.pallas import tpu as pltpu

F32 = jnp.float32
BF = jnp.bfloat16

D_MODEL = 1024
DEPTH = 4
N_A = 2
HEAD_DIM = 64
MEM_W = 256
MAIN_W = 768
D_FF = 2816
N_MEM = 256
POOL_WINDOWS = (2, 4, 8, 16)
POOL_GROUP = 192
DIL = (1, 4, 16)
STEPS = 128
ROPE_THETA = 10000.0
EPS = 1e-6
SCALE = HEAD_DIM ** -0.5
NEG = -1e30

ADAM_LR = 0.001
ADAM_B1 = 0.9
ADAM_B2 = 0.999
ADAM_EPS = 1e-08
ADAM_WD = 0.01
ADAM_STEP = 10

VMEM_LIMIT = 48 * 1024 * 1024
MESH = pl.DeviceIdType.MESH


def _cp(sem):
    return pltpu.CompilerParams(dimension_semantics=sem, vmem_limit_bytes=VMEM_LIMIT)


def _mm(a, b, *, name, ta=False, tb=False, tm=1024, tn=512, tk=1024, b_layer=None, b_offsets=(0,),
        extras=(), epilogue=None, out_dtypes=(F32,), out_n=None, stack=None):
    a_pair = isinstance(a, (tuple, list))
    b_pair = isinstance(b, (tuple, list))
    a0 = a[0] if a_pair else a
    b0 = b[0] if b_pair else b
    a_rows, a_cols = a0.shape
    if a_pair:
        a_cols *= 2
    b_rows, b_cols = b0.shape[-2:]
    if b_pair:
        b_cols *= 2
    M, K = (a_cols, a_rows) if ta else (a_rows, a_cols)
    N = b_rows if tb else b_cols
    if out_n is not None:
        N = out_n
    tm, tn, tk = min(tm, M), min(tn, N), min(tk, K)
    assert M % tm == 0 and N % tn == 0 and K % tk == 0, (name, M, N, K, tm, tn, tk)
    nk = K // tk
    n_acc = len(b_offsets)

    if a_pair:
        a_half = (a0.shape[1] // (tm if ta else tk))
    if b_pair:
        b_half = (b0.shape[1] // (tk if tb else tn))

    def a_map(sel):
        def f(i, j, k):
            r, c = (k, i) if ta else (i, k)
            if a_pair:
                c = jnp.clip(c - sel * a_half, 0, a_half - 1)
            return (r, c)
        return f

    def b_map(sel, off):
        def f(i, j, k):
            r, c = (j + off, k) if tb else (k, j + off)
            if b_pair:
                c = jnp.clip(c - sel * b_half, 0, b_half - 1)
            if b_layer is not None:
                return (b_layer, r, c)
            return (r, c)
        return f

    a_blk = (tk, tm) if ta else (tm, tk)
    b_blk = (tn, tk) if tb else (tk, tn)
    if b_layer is not None:
        b_blk = (None,) + b_blk
    in_specs, operands = [], []
    for sel in range(2 if a_pair else 1):
        in_specs.append(pl.BlockSpec(a_blk, a_map(sel)))
        operands.append(a[sel] if a_pair else a)
    n_a = len(operands)
    for off in b_offsets:
        for sel in range(2 if b_pair else 1):
            in_specs.append(pl.BlockSpec(b_blk, b_map(sel, off)))
            operands.append(b[sel] if b_pair else b)
    n_b = len(operands) - n_a
    for arr, kind in extras:
        if kind == 'tile':
            in_specs.append(pl.BlockSpec((tm, tn), lambda i, j, k: (i, j)))
        elif kind == 'row':
            in_specs.append(pl.BlockSpec((tm, 1), lambda i, j, k: (i, 0)))
        else:
            in_specs.append(pl.BlockSpec((1, tn), lambda i, j, k: (0, j)))
        operands.append(arr)
    n_e = len(extras)
    n_o = len(out_dtypes)
    dims = (((0,) if ta else (1,), (1,) if tb else (0,)), ((), ()))

    def body(*refs):
        a_refs = refs[:n_a]
        b_refs = refs[n_a:n_a + n_b]
        e_refs = refs[n_a + n_b:n_a + n_b + n_e]
        n_in = n_a + n_b + n_e + (1 if stack is not None else 0)
        o_refs = refs[n_in:n_in + n_o]
        acc_refs = refs[n_in + n_o:]
        i, j, k = pl.program_id(0), pl.program_id(1), pl.program_id(2)
        if a_pair:
            cidx = i if ta else k
            av = jnp.where(cidx < a_half, a_refs[0][...], a_refs[1][...])
        else:
            av = a_refs[0][...]
        av = av.astype(BF)
        prods = []
        for q in range(n_acc):
            if b_pair:
                cidx = (k if tb else j) + b_offsets[q]
                bv = jnp.where(cidx < b_half, b_refs[2 * q][...], b_refs[2 * q + 1][...])
            else:
                bv = b_refs[q][...]
            prods.append(lax.dot_general(av, bv.astype(BF), dims, preferred_element_type=F32))

        def finish(accs):
            outs = epilogue(accs, *[r[...] for r in e_refs]) if epilogue is not None else accs
            for o_ref, o in zip(o_refs, outs):
                o_ref[...] = o.astype(o_ref.dtype)

        if nk == 1:
            finish(prods)
        else:
            @pl.when(k == 0)
            def _():
                for r, p in zip(acc_refs, prods):
                    r[...] = p

            @pl.when(k > 0)
            def _():
                for r, p in zip(acc_refs, prods):
                    r[...] += p

            @pl.when(k == nk - 1)
            def _():
                finish([r[...] for r in acc_refs])

    if stack is not None:
        buf, layer = stack
        assert n_o == 1 and buf.shape[1:] == (M, N)
        return pl.pallas_call(
            body, name=name,
            grid=(M // tm, N // tn, nk),
            in_specs=in_specs + [pl.BlockSpec(memory_space=pl.ANY)],
            out_specs=[pl.BlockSpec((None, tm, tn), lambda i, j, k: (layer, i, j))],
            out_shape=[jax.ShapeDtypeStruct(buf.shape, buf.dtype)],
            scratch_shapes=[pltpu.VMEM((tm, tn), F32) for _ in range(n_acc if nk > 1 else 0)],
            input_output_aliases={len(operands): 0},
            compiler_params=_cp(("parallel", "parallel", "arbitrary")),
        )(*operands, buf)[0]
    return pl.pallas_call(
        body, name=name,
        grid=(M // tm, N // tn, nk),
        in_specs=in_specs,
        out_specs=[pl.BlockSpec((tm, tn), lambda i, j, k: (i, j)) for _ in range(n_o)],
        out_shape=[jax.ShapeDtypeStruct((M, N), dt) for dt in out_dtypes],
        scratch_shapes=[pltpu.VMEM((tm, tn), F32) for _ in range(n_acc if nk > 1 else 0)],
        compiler_params=_cp(("parallel", "parallel", "arbitrary")),
    )(*operands)


def _norm_fwd(x, g, *, name, res=None, out_dtype=F32, tm=512):
    T, Dm = x.shape
    has_res = res is not None

    def body(*refs):
        if has_res:
            x_ref, g_ref, r_ref, y_ref, s_ref = refs
        else:
            x_ref, g_ref, y_ref, s_ref = refs
        xv = x_ref[...]
        rstd = lax.rsqrt(jnp.mean(xv * xv, axis=-1, keepdims=True) + EPS)
        y = xv * rstd * g_ref[...]
        if has_res:
            y = r_ref[...] + y
        y_ref[...] = y.astype(y_ref.dtype)
        s_ref[...] = rstd

    row = pl.BlockSpec((tm, Dm), lambda i: (i, 0))
    in_specs = [row, pl.BlockSpec((1, Dm), lambda i: (0, 0))] + ([row] if has_res else [])
    ops = [x, g.reshape(1, Dm)] + ([res] if has_res else [])
    return pl.pallas_call(
        body, name=name, grid=(T // tm,), in_specs=in_specs,
        out_specs=[row, pl.BlockSpec((tm, 1), lambda i: (i, 0))],
        out_shape=[jax.ShapeDtypeStruct((T, Dm), out_dtype), jax.ShapeDtypeStruct((T, 1), F32)],
        compiler_params=_cp(("parallel",)),
    )(*ops)


def _norm_bwd(dout, x, rstd, g, *, name, add=None, out_dtype=F32, tm=512):
    T, Dm = x.shape
    has_add = add is not None
    nt = T // tm

    def body(*refs):
        if has_add:
            do_ref, x_ref, s_ref, g_ref, a_ref, dx_ref, dg_ref, acc = refs
        else:
            do_ref, x_ref, s_ref, g_ref, dx_ref, dg_ref, acc = refs
        i = pl.program_id(0)
        do = do_ref[...].astype(F32)
        xh = x_ref[...] * s_ref[...]
        gd = do * g_ref[...]
        dx = s_ref[...] * (gd - xh * jnp.mean(gd * xh, axis=-1, keepdims=True))
        if has_add:
            dx = dx + a_ref[...].astype(F32)
        dx_ref[...] = dx.astype(dx_ref.dtype)
        part = jnp.sum((do * xh).reshape(tm // 8, 8, Dm), axis=0)

        @pl.when(i == 0)
        def _():
            acc[...] = part

        @pl.when(i > 0)
        def _():
            acc[...] += part

        @pl.when(i == nt - 1)
        def _():
            dg_ref[...] = jnp.sum(acc[...], axis=0, keepdims=True)

    row = pl.BlockSpec((tm, Dm), lambda i: (i, 0))
    in_specs = [row, row, pl.BlockSpec((tm, 1), lambda i: (i, 0)), pl.BlockSpec((1, Dm), lambda i: (0, 0))]
    ops = [dout, x, rstd, g.reshape(1, Dm)]
    if has_add:
        in_specs.append(row)
        ops.append(add)
    return pl.pallas_call(
        body, name=name, grid=(nt,), in_specs=in_specs,
        out_specs=[row, pl.BlockSpec((1, Dm), lambda i: (0, 0))],
        out_shape=[jax.ShapeDtypeStruct((T, Dm), out_dtype), jax.ShapeDtypeStruct((1, Dm), F32)],
        scratch_shapes=[pltpu.VMEM((8, Dm), F32)],
        compiler_params=_cp(("arbitrary",)),
    )(*ops)


def _swiglu_fwd_epilogue(accs):
    g, u = accs
    return g, u, g * jax.nn.sigmoid(g) * u


def _swiglu_bwd_epilogue(accs, g, u):
    da = accs[0]
    g = g.astype(F32)
    u = u.astype(F32)
    sig = jax.nn.sigmoid(g)
    return da * u * (sig * (1.0 + g * (1.0 - sig))), da * (g * sig)


def _rope_tables(pos, *, name, tm=1024):
    T = pos.shape[0]
    half = HEAD_DIM // 2
    freqs = ROPE_THETA ** (-jnp.arange(half, dtype=F32) / half)
    freqs = jnp.tile(freqs, 4).reshape(1, 128)

    def body(p_ref, f_ref, c_ref, s_ref):
        ang = p_ref[...].astype(F32) * f_ref[...]
        lane = lax.broadcasted_iota(jnp.int32, ang.shape, 1)
        c_ref[...] = jnp.cos(ang)
        s_ref[...] = jnp.where(lane % HEAD_DIM < half, -1.0, 1.0) * jnp.sin(ang)

    tab = pl.BlockSpec((tm, 128), lambda i: (i, 0))
    return pl.pallas_call(
        body, name=name, grid=(T // tm,),
        in_specs=[pl.BlockSpec((tm, 1), lambda i: (i, 0)), pl.BlockSpec((1, 128), lambda i: (0, 0))],
        out_specs=[tab, tab],
        out_shape=[jax.ShapeDtypeStruct((T, 128), F32)] * 2,
        compiler_params=_cp(("parallel",)),
    )(pos, freqs)


def _rot(x, cos, sin, sign):
    W = x.shape[1]
    half = HEAD_DIM // 2
    reps = W // 128
    c = jnp.concatenate([cos] * reps, axis=1) if reps > 1 else cos
    s = jnp.concatenate([sin] * reps, axis=1) if reps > 1 else sin
    lane = lax.broadcasted_iota(jnp.int32, x.shape, 1)
    swapped = jnp.where(lane % HEAD_DIM < half, pltpu.roll(x, W - half, axis=1), pltpu.roll(x, half, axis=1))
    return x * c + (sign * s) * swapped


def _rope_apply(x, cos, sin, *, name, sign=1.0, width=MAIN_W, passthrough=False, out_dtype=BF, alias=None,
                out_cols=None, tm=512):
    T = x.shape[0]

    def body(*refs):
        if passthrough:
            x_ref, v_ref, c_ref, s_ref, o_ref, ov_ref = refs
            ov_ref[...] = v_ref[...].astype(ov_ref.dtype)
        elif alias is not None:
            x_ref, c_ref, s_ref, _, o_ref = refs
        else:
            x_ref, c_ref, s_ref, o_ref = refs
        o_ref[...] = _rot(x_ref[...].astype(F32), c_ref[...], s_ref[...], sign).astype(o_ref.dtype)

    blk0 = pl.BlockSpec((tm, width), lambda i: (i, 0))
    blk1 = pl.BlockSpec((tm, width), lambda i: (i, 1))
    tab = pl.BlockSpec((tm, 128), lambda i: (i, 0))
    if passthrough:
        return pl.pallas_call(
            body, name=name, grid=(T // tm,), in_specs=[blk0, blk1, tab, tab], out_specs=[blk0, blk0],
            out_shape=[jax.ShapeDtypeStruct((T, width), out_dtype)] * 2,
            compiler_params=_cp(("parallel",)),
        )(x, x, cos, sin)
    if alias is not None:
        return pl.pallas_call(
            body, name=name, grid=(T // tm,),
            in_specs=[blk0, tab, tab, pl.BlockSpec(memory_space=pl.ANY)], out_specs=blk0,
            out_shape=jax.ShapeDtypeStruct(alias.shape, alias.dtype),
            input_output_aliases={3: 0},
            compiler_params=_cp(("parallel",)),
        )(x, cos, sin, alias)
    return pl.pallas_call(
        body, name=name, grid=(T // tm,), in_specs=[blk0, tab, tab], out_specs=blk0,
        out_shape=jax.ShapeDtypeStruct((T, width), out_dtype),
        compiler_params=_cp(("parallel",)),
    )(x, cos, sin)


POOL_T = 256
POOL_HALO = 16


def _pool_lane_window(shape):
    lane = lax.broadcasted_iota(jnp.int32, shape, 1)
    w = jnp.full(shape, POOL_WINDOWS[0], jnp.int32)
    for gi in range(1, len(POOL_WINDOWS)):
        w = jnp.where(lane >= gi * POOL_GROUP, POOL_WINDOWS[gi], w)
    return w


def _pool_fwd(z, wbd, scale, B, S, *, name):
    T = z.shape[0]
    nt = S // POOL_T
    hb = POOL_T // POOL_HALO

    def body(z_ref, h_ref, w_ref, sc_ref, y_ref, p_ref, ext):
        i = pl.program_id(1)
        u = z_ref[...]
        ext[pl.ds(POOL_HALO, POOL_T), :] = u
        ext[pl.ds(0, POOL_HALO), :] = jnp.where(i > 0, h_ref[...], 0.0)
        win = _pool_lane_window((POOL_T, MAIN_W))
        acc = u
        for k in range(1, POOL_HALO):
            acc = acc + jnp.where(k < win, ext[pl.ds(POOL_HALO - k, POOL_T), :], 0.0)
        t = i * POOL_T + lax.broadcasted_iota(jnp.int32, (POOL_T, MAIN_W), 0)
        cnt = jnp.minimum(t + 1, win).astype(F32)
        p = (acc / cnt - u).astype(BF)
        p_ref[...] = p
        y = jnp.dot(p, w_ref[...], preferred_element_type=F32) * sc_ref[...]
        y_ref[...] = y.astype(y_ref.dtype)

    return pl.pallas_call(
        body, name=name, grid=(B, nt),
        in_specs=[pl.BlockSpec((POOL_T, MAIN_W), lambda b, i: (b * nt + i, 0)),
                  pl.BlockSpec((POOL_HALO, MAIN_W), lambda b, i: (jnp.maximum((b * nt + i) * hb - 1, 0), 0)),
                  pl.BlockSpec((MAIN_W, MAIN_W), lambda b, i: (0, 0)),
                  pl.BlockSpec((1, MAIN_W), lambda b, i: (0, 0))],
        out_specs=[pl.BlockSpec((POOL_T, MAIN_W), lambda b, i: (b * nt + i, 0)),
                   pl.BlockSpec((POOL_T, MAIN_W), lambda b, i: (b * nt + i, 0))],
        out_shape=[jax.ShapeDtypeStruct((T, D_MODEL), BF), jax.ShapeDtypeStruct((T, MAIN_W), BF)],
        scratch_shapes=[pltpu.VMEM((POOL_T + POOL_HALO, MAIN_W), F32)],
        compiler_params=_cp(("parallel", "parallel")),
    )(z, z, wbd, scale.reshape(1, MAIN_W))


def _pool_bwd(dy, p, wbd, scale, dz_alias, B, S, *, name):
    T = dy.shape[0]
    nt = S // POOL_T
    hb = POOL_T // POOL_HALO
    last_halo = T // POOL_HALO - 1
    R = POOL_T + POOL_HALO

    def body(dy_ref, dyn_ref, p_ref, pn_ref, w_ref, sc_ref, _, dz_ref, dw_ref, ds_ref, ext, dw_acc, ds_acc):
        b, i = pl.program_id(0), pl.program_id(1)
        first = jnp.logical_and(b == 0, i == 0)
        dyv = dy_ref[...]
        pv = p_ref[...]
        sc = sc_ref[...]
        w = w_ref[...]
        pw = jnp.dot(pv, w, preferred_element_type=F32)
        ds_part = jnp.sum((dyv * pw).reshape(POOL_T // 8, 8, MAIN_W), axis=0)
        dpw = (dyv * sc).astype(BF)
        dw_part = lax.dot_general(pv, dpw, (((0,), (0,)), ((), ())), preferred_element_type=F32)

        @pl.when(first)
        def _():
            dw_acc[...] = dw_part
            ds_acc[...] = ds_part

        @pl.when(jnp.logical_not(first))
        def _():
            dw_acc[...] += dw_part
            ds_acc[...] += ds_part

        @pl.when(jnp.logical_and(b == pl.num_programs(0) - 1, i == nt - 1))
        def _():
            dw_ref[...] = dw_acc[...]
            ds_ref[...] = jnp.sum(ds_acc[...], axis=0, keepdims=True)

        dp = lax.dot_general(dpw, w, (((1,), (1,)), ((), ())), preferred_element_type=F32)
        dpn = lax.dot_general((dyn_ref[...] * sc).astype(BF), w, (((1,), (1,)), ((), ())), preferred_element_type=F32)
        win = _pool_lane_window((POOL_T, MAIN_W))
        win_n = _pool_lane_window((POOL_HALO, MAIN_W))
        t = i * POOL_T + lax.broadcasted_iota(jnp.int32, (POOL_T, MAIN_W), 0)
        tn = (i + 1) * POOL_T + lax.broadcasted_iota(jnp.int32, (POOL_HALO, MAIN_W), 0)
        ext[pl.ds(0, POOL_T), :] = dp / jnp.minimum(t + 1, win).astype(F32)
        ext[pl.ds(POOL_T, POOL_HALO), :] = jnp.where(i < nt - 1, dpn / jnp.minimum(tn + 1, win_n).astype(F32), 0.0)
        acc = -dp
        for k in range(POOL_HALO):
            acc = acc + jnp.where(k < win, ext[pl.ds(k, POOL_T), :], 0.0)
        dz_ref[...] = acc.astype(dz_ref.dtype)

    cur = lambda b, i: (b * nt + i, 0)
    nxt = lambda b, i: (jnp.minimum((b * nt + i + 1) * hb, last_halo), 0)
    return pl.pallas_call(
        body, name=name, grid=(B, nt),
        in_specs=[pl.BlockSpec((POOL_T, MAIN_W), cur), pl.BlockSpec((POOL_HALO, MAIN_W), nxt),
                  pl.BlockSpec((POOL_T, MAIN_W), cur), pl.BlockSpec((POOL_HALO, MAIN_W), nxt),
                  pl.BlockSpec((MAIN_W, MAIN_W), lambda b, i: (0, 0)),
                  pl.BlockSpec((1, MAIN_W), lambda b, i: (0, 0)),
                  pl.BlockSpec(memory_space=pl.ANY)],
        out_specs=[pl.BlockSpec((POOL_T, MAIN_W), cur),
                   pl.BlockSpec((MAIN_W, MAIN_W), lambda b, i: (0, 0)),
                   pl.BlockSpec((1, MAIN_W), lambda b, i: (0, 0))],
        out_shape=[jax.ShapeDtypeStruct(dz_alias.shape, dz_alias.dtype),
                   jax.ShapeDtypeStruct((MAIN_W, MAIN_W), F32), jax.ShapeDtypeStruct((1, MAIN_W), F32)],
        scratch_shapes=[pltpu.VMEM((R, MAIN_W), F32), pltpu.VMEM((MAIN_W, MAIN_W), F32), pltpu.VMEM((8, MAIN_W), F32)],
        input_output_aliases={6: 0},
        compiler_params=_cp(("arbitrary", "arbitrary")),
    )(dy, dy, p, p, wbd, scale.reshape(1, MAIN_W), dz_alias)


def _head_masks(shape):
    lane = lax.broadcasted_iota(jnp.int32, shape, 1)
    return [(lane // HEAD_DIM) == h for h in range(shape[1] // HEAD_DIM)]


def _row_of(bcast, mask):
    return jnp.max(jnp.where(mask, bcast, -jnp.inf), axis=-1, keepdims=True)


MEM_TQ = 512


def _memattn_fwd(z, kv, y_alias, B, S, *, name):
    T = z.shape[0]
    nt = S // MEM_TQ

    def body(q_ref, k_ref, v_ref, _, y_ref, l_ref):
        q = q_ref[...]
        k = k_ref[...]
        v = v_ref[...]
        masks = _head_masks(q.shape)
        o = jnp.zeros(q.shape, F32)
        lse_b = jnp.zeros(q.shape, F32)
        for m in masks:
            qm = jnp.where(m, q, 0.0).astype(BF)
            s = lax.dot_general(qm, k, (((1,), (1,)), ((), ())), preferred_element_type=F32) * SCALE
            mx = jnp.max(s, axis=-1, keepdims=True)
            e = jnp.exp(s - mx)
            l = jnp.sum(e, axis=-1, keepdims=True)
            p = (e / l).astype(BF)
            o = o + jnp.where(m, jnp.dot(p, v, preferred_element_type=F32), 0.0)
            lse_b = lse_b + jnp.where(m, mx + jnp.log(l), 0.0)
        y_ref[...] = o.astype(y_ref.dtype)
        l_ref[...] = lse_b

    qblk = pl.BlockSpec((MEM_TQ, MEM_W), lambda b, i: (b * nt + i, 3))
    return pl.pallas_call(
        body, name=name, grid=(B, nt),
        in_specs=[qblk, pl.BlockSpec((N_MEM, MEM_W), lambda b, i: (b, 0)), pl.BlockSpec((N_MEM, MEM_W), lambda b, i: (b, 1)),
                  pl.BlockSpec(memory_space=pl.ANY)],
        out_specs=[qblk, pl.BlockSpec((MEM_TQ, MEM_W), lambda b, i: (b * nt + i, 0))],
        out_shape=[jax.ShapeDtypeStruct(y_alias.shape, y_alias.dtype), jax.ShapeDtypeStruct((T, MEM_W), F32)],
        input_output_aliases={3: 0},
        compiler_params=_cp(("parallel", "parallel")),
    )(z, kv, kv, y_alias)


def _memattn_bwd(dy, z, kv, lse, dz_alias, B, S, *, name):
    nt = S // MEM_TQ

    def body(do_ref, q_ref, k_ref, v_ref, l_ref, _, dz_ref, dk_ref, dv_ref, dk_acc, dv_acc):
        i = pl.program_id(1)
        do = do_ref[...]
        q = q_ref[...]
        k = k_ref[...]
        v = v_ref[...]
        lse_b = l_ref[...]
        masks = _head_masks(q.shape)
        dq = jnp.zeros(q.shape, F32)
        dk = jnp.zeros(k.shape, F32)
        dv = jnp.zeros(v.shape, F32)
        for m in masks:
            qm = jnp.where(m, q, 0.0).astype(BF)
            dom = jnp.where(m, do, 0.0).astype(BF)
            s = lax.dot_general(qm, k, (((1,), (1,)), ((), ())), preferred_element_type=F32) * SCALE
            p = jnp.exp(s - _row_of(lse_b, m))
            dp = lax.dot_general(dom, v, (((1,), (1,)), ((), ())), preferred_element_type=F32)
            delta = jnp.sum(p * dp, axis=-1, keepdims=True)
            ds = (p * (dp - delta) * SCALE).astype(BF)
            pb = p.astype(BF)
            dv = dv + jnp.where(m[:N_MEM], lax.dot_general(pb, dom, (((0,), (0,)), ((), ())), preferred_element_type=F32), 0.0)
            dk = dk + jnp.where(m[:N_MEM], lax.dot_general(ds, qm, (((0,), (0,)), ((), ())), preferred_element_type=F32), 0.0)
            dq = dq + jnp.where(m, jnp.dot(ds, k, preferred_element_type=F32), 0.0)
        dz_ref[...] = dq.astype(dz_ref.dtype)

        @pl.when(i == 0)
        def _():
            dk_acc[...] = dk
            dv_acc[...] = dv

        @pl.when(i > 0)
        def _():
            dk_acc[...] += dk
            dv_acc[...] += dv

        @pl.when(i == nt - 1)
        def _():
            dk_ref[...] = dk_acc[...]
            dv_ref[...] = dv_acc[...]

    qblk = pl.BlockSpec((MEM_TQ, MEM_W), lambda b, i: (b * nt + i, 3))
    kblk = pl.BlockSpec((N_MEM, MEM_W), lambda b, i: (b, 0))
    return pl.pallas_call(
        body, name=name, grid=(B, nt),
        in_specs=[qblk, qblk, kblk, pl.BlockSpec((N_MEM, MEM_W), lambda b, i: (b, 1)),
                  pl.BlockSpec((MEM_TQ, MEM_W), lambda b, i: (b * nt + i, 0)), pl.BlockSpec(memory_space=pl.ANY)],
        out_specs=[qblk, kblk, kblk],
        out_shape=[jax.ShapeDtypeStruct(dz_alias.shape, dz_alias.dtype),
                   jax.ShapeDtypeStruct((B * N_MEM, MEM_W), F32), jax.ShapeDtypeStruct((B * N_MEM, MEM_W), F32)],
        scratch_shapes=[pltpu.VMEM((N_MEM, MEM_W), F32), pltpu.VMEM((N_MEM, MEM_W), F32)],
        input_output_aliases={5: 0},
        compiler_params=_cp(("parallel", "arbitrary")),
    )(dy, z, kv, kv, lse, dz_alias)


def _dil_scores(qm, kp, kc, n):
    qi = lax.broadcasted_iota(jnp.int32, (STEPS, STEPS), 0)
    kj = lax.broadcasted_iota(jnp.int32, (STEPS, STEPS), 1)
    sc = lax.dot_general(qm, kc, (((1,), (1,)), ((), ())), preferred_element_type=F32) * SCALE
    sc = jnp.where(kj <= qi, sc, NEG)
    if kp is None:
        return None, sc
    sp = lax.dot_general(qm, kp, (((1,), (1,)), ((), ())), preferred_element_type=F32) * SCALE
    sp = jnp.where(jnp.logical_and(kj >= qi, n > 0), sp, NEG)
    return sp, sc


def _dil_specs(g, d, nb):
    chunk = STEPS * d
    cur = pl.BlockSpec((chunk, 128), lambda b, n, hf: (b * nb + n, g * 2 + hf))
    prev = pl.BlockSpec((chunk, 128), lambda b, n, hf: (b * nb + jnp.maximum(n - 1, 0), g * 2 + hf))
    return cur, prev


def _dil_rows(r, d):
    return pl.ds(r, STEPS, stride=d) if d > 1 else slice(None)


def _dil_loop(d, fn):
    if d <= 4:
        for r in range(d):
            fn(r)
    else:
        lax.fori_loop(0, d, lambda r, carry: (fn(r), carry)[1], 0)


def _dil_fwd_group(g, q, k, v, o_alias, l_alias, B, S, *, name):
    d = DIL[g]
    nb = S // (STEPS * d)
    has_prev = nb > 1

    def body(*refs):
        if has_prev:
            q_ref, kp_ref, kc_ref, vp_ref, vc_ref, _, __, o_ref, l_ref = refs
        else:
            q_ref, kc_ref, vc_ref, _, __, o_ref, l_ref = refs
        n = pl.program_id(1)

        def residue(r):
            rows = _dil_rows(r, d)
            q = q_ref[rows, :]
            kc, vc = kc_ref[rows, :].astype(BF), vc_ref[rows, :].astype(BF)
            kp = kp_ref[rows, :].astype(BF) if has_prev else None
            vp = vp_ref[rows, :].astype(BF) if has_prev else None
            o = jnp.zeros(q.shape, F32)
            lse_b = jnp.zeros(q.shape, F32)
            for m in _head_masks(q.shape):
                qm = jnp.where(m, q, 0.0).astype(BF)
                sp, sc = _dil_scores(qm, kp, kc, n)
                mx = jnp.max(sc, axis=-1, keepdims=True)
                if has_prev:
                    mx = jnp.maximum(mx, jnp.max(sp, axis=-1, keepdims=True))
                l = jnp.sum(jnp.exp(sc - mx), axis=-1, keepdims=True)
                if has_prev:
                    l = l + jnp.sum(jnp.exp(sp - mx), axis=-1, keepdims=True)
                lse = mx + jnp.log(l)
                oh = jnp.dot(jnp.exp(sc - lse).astype(BF), vc, preferred_element_type=F32)
                if has_prev:
                    oh = oh + jnp.dot(jnp.exp(sp - lse).astype(BF), vp, preferred_element_type=F32)
                o = o + jnp.where(m, oh, 0.0)
                lse_b = lse_b + jnp.where(m, lse, 0.0)
            o_ref[rows, :] = o
            l_ref[rows, :] = lse_b

        _dil_loop(d, residue)

    cur, prev = _dil_specs(g, d, nb)
    anyspec = pl.BlockSpec(memory_space=pl.ANY)
    if has_prev:
        in_specs, ops = [cur, prev, cur, prev, cur], [q, k, k, v, v]
    else:
        in_specs, ops = [cur, cur, cur], [q, k, v]
    n_in = len(ops)
    o, l = pl.pallas_call(
        body, name=name, grid=(B, nb, 2),
        in_specs=in_specs + [anyspec, anyspec],
        out_specs=[cur, cur],
        out_shape=[jax.ShapeDtypeStruct(q.shape, F32)] * 2,
        input_output_aliases={n_in: 0, n_in + 1: 1},
        compiler_params=_cp(("parallel", "parallel", "parallel")),
    )(*ops, o_alias, l_alias)
    return o, l


def _dil_bwd_group(g, q, k, v, do, cb, lse, aliases, B, S, *, name):
    d = DIL[g]
    nb = S // (STEPS * d)
    has_prev = nb > 1
    n_out = 5 if has_prev else 3

    def body(*refs):
        if has_prev:
            q_ref, kp_ref, kc_ref, vp_ref, vc_ref, do_ref, c_ref, l_ref = refs[:8]
            dq_ref, dkc_ref, dvc_ref, dkp_ref, dvp_ref = refs[8 + n_out:]
        else:
            q_ref, kc_ref, vc_ref, do_ref, c_ref, l_ref = refs[:6]
            dq_ref, dkc_ref, dvc_ref = refs[6 + n_out:]
        n = pl.program_id(1)
        tdot = lambda a, b: lax.dot_general(a, b, (((0,), (0,)), ((), ())), preferred_element_type=F32)
        ndot = lambda a, b: lax.dot_general(a, b, (((1,), (1,)), ((), ())), preferred_element_type=F32)

        def residue(r):
            rows = _dil_rows(r, d)
            q = q_ref[rows, :]
            kc, vc = kc_ref[rows, :].astype(BF), vc_ref[rows, :].astype(BF)
            kp = kp_ref[rows, :].astype(BF) if has_prev else None
            vp = vp_ref[rows, :].astype(BF) if has_prev else None
            do = do_ref[rows, :]
            cbv = c_ref[rows, :]
            lse_b = l_ref[rows, :]
            z = jnp.zeros(q.shape, F32)
            dq, dkc, dkp, dvc, dvp = z, z, z, z, z
            for m in _head_masks(q.shape):
                qm = jnp.where(m, q, 0.0).astype(BF)
                dom = jnp.where(m, do, 0.0).astype(BF)
                sp, sc = _dil_scores(qm, kp, kc, n)
                lse = _row_of(lse_b, m)
                c = _row_of(cbv, m)
                pc = jnp.exp(sc - lse)
                dsc = (pc * (ndot(dom, vc) + c) * SCALE).astype(BF)
                dqh = jnp.dot(dsc, kc, preferred_element_type=F32)
                dkc = dkc + jnp.where(m, tdot(dsc, qm), 0.0)
                dvc = dvc + jnp.where(m, tdot(pc.astype(BF), dom), 0.0)
                if has_prev:
                    pp = jnp.exp(sp - lse)
                    dsp = (pp * (ndot(dom, vp) + c) * SCALE).astype(BF)
                    dqh = dqh + jnp.dot(dsp, kp, preferred_element_type=F32)
                    dkp = dkp + jnp.where(m, tdot(dsp, qm), 0.0)
                    dvp = dvp + jnp.where(m, tdot(pp.astype(BF), dom), 0.0)
                dq = dq + jnp.where(m, dqh, 0.0)
            dq_ref[rows, :] = dq
            dkc_ref[rows, :] = dkc
            dvc_ref[rows, :] = dvc
            if has_prev:
                dkp_ref[rows, :] = dkp
                dvp_ref[rows, :] = dvp

        _dil_loop(d, residue)

    cur, prev = _dil_specs(g, d, nb)
    anyspec = pl.BlockSpec(memory_space=pl.ANY)
    dq_a, dkc_a, dkp_a, dvc_a, dvp_a = aliases
    if has_prev:
        in_specs, ops = [cur, prev, cur, prev, cur, cur, cur, cur], [q, k, k, v, v, do, cb, lse]
        al = [dq_a, dkc_a, dvc_a, dkp_a, dvp_a]
    else:
        in_specs, ops = [cur, cur, cur, cur, cur, cur], [q, k, v, do, cb, lse]
        al = [dq_a, dkc_a, dvc_a]
    n_in = len(ops)
    outs = pl.pallas_call(
        body, name=name, grid=(B, nb, 2),
        in_specs=in_specs + [anyspec] * n_out,
        out_specs=[cur] * n_out,
        out_shape=[jax.ShapeDtypeStruct(q.shape, F32)] * n_out,
        input_output_aliases={n_in + i: i for i in range(n_out)},
        compiler_params=_cp(("parallel", "parallel", "parallel")),
    )(*ops, *al)
    if has_prev:
        dq_a, dkc_a, dvc_a, dkp_a, dvp_a = outs
    else:
        dq_a, dkc_a, dvc_a = outs
    return dq_a, dkc_a, dkp_a, dvc_a, dvp_a


def _group_softmax(lse):
    l0, l1, l2 = lse[:, 0:256], lse[:, 256:512], lse[:, 512:768]
    mx = jnp.maximum(jnp.maximum(l0, l1), l2)
    e0, e1, e2 = jnp.exp(l0 - mx), jnp.exp(l1 - mx), jnp.exp(l2 - mx)
    tot = e0 + e1 + e2
    return e0 / tot, e1 / tot, e2 / tot


def _dil_combine_fwd(o, lse, y_alias, *, name, tm=512):
    T = o.shape[0]

    def body(o_ref, l_ref, _, y_ref):
        a = jnp.concatenate(_group_softmax(l_ref[...]), axis=1)
        y_ref[...] = (o_ref[...] * a).astype(y_ref.dtype)

    blk = pl.BlockSpec((tm, MAIN_W), lambda i: (i, 0))
    return pl.pallas_call(
        body, name=name, grid=(T // tm,), in_specs=[blk, blk, pl.BlockSpec(memory_space=pl.ANY)], out_specs=blk,
        out_shape=jax.ShapeDtypeStruct(y_alias.shape, y_alias.dtype), input_output_aliases={2: 0},
        compiler_params=_cp(("parallel",)),
    )(o, lse, y_alias)


def _dil_combine_bwd(dy, o, lse, *, name, tm=256):
    T = o.shape[0]
    lane_r = lax.broadcasted_iota(jnp.int32, (256, 256), 0) // HEAD_DIM
    lane_c = lax.broadcasted_iota(jnp.int32, (256, 256), 1) // HEAD_DIM
    ones_bd = (lane_r == lane_c).astype(BF)

    def body(dy_ref, o_ref, l_ref, e_ref, do_ref, c_ref):
        dyv = dy_ref[...]
        alphas = _group_softmax(l_ref[...])
        prod = dyv * o_ref[...]
        e = e_ref[...]
        tot = jnp.zeros((tm, 256), F32)
        for gi in range(3):
            x = prod[:, gi * 256:(gi + 1) * 256]
            hi = x.astype(BF)
            lo = (x - hi.astype(F32)).astype(BF)
            dalpha = jnp.dot(hi, e, preferred_element_type=F32) + jnp.dot(lo, e, preferred_element_type=F32)
            tot = tot + alphas[gi] * dalpha
        a = jnp.concatenate(alphas, axis=1)
        do_ref[...] = (dyv * a).astype(do_ref.dtype)
        c_ref[...] = jnp.concatenate([-al * tot for al in alphas], axis=1)

    blk = pl.BlockSpec((tm, MAIN_W), lambda i: (i, 0))
    return pl.pallas_call(
        body, name=name, grid=(T // tm,),
        in_specs=[blk, blk, blk, pl.BlockSpec((256, 256), lambda i: (0, 0))], out_specs=[blk, blk],
        out_shape=[jax.ShapeDtypeStruct((T, MAIN_W), F32), jax.ShapeDtypeStruct((T, MAIN_W), F32)],
        compiler_params=_cp(("parallel",)),
    )(dy, o, lse, ones_bd)


def _kv_grad(parts, cos, sin, B, S, *, name):
    T = B * S
    tb = S // STEPS
    n_l = len(parts)

    def shifted(g):
        def f(b, t):
            return (b * tb + jnp.minimum(t + DIL[g], tb - 1), g)
        return f

    with_prev = [g for g in range(3) if DIL[g] < tb]
    n_p = len(with_prev)
    per_l = 2 + 2 * n_p

    def body(*refs):
        c_ref, s_ref = refs[0], refs[1]
        ins = refs[2:2 + n_l * per_l]
        dk_ref, dv_ref = refs[2 + n_l * per_l:]
        t = pl.program_id(1)
        dk = jnp.zeros((STEPS, MAIN_W), F32)
        dv = jnp.zeros((STEPS, MAIN_W), F32)
        zero = jnp.zeros((STEPS, 256), F32)
        for li in range(n_l):
            base = li * per_l
            dk = dk + ins[base][...]
            dv = dv + ins[base + 1][...]
            kparts, vparts = [zero] * 3, [zero] * 3
            for pi, g in enumerate(with_prev):
                ok = t + DIL[g] < tb
                kparts[g] = jnp.where(ok, ins[base + 2 + pi][...], 0.0)
                vparts[g] = jnp.where(ok, ins[base + 2 + n_p + pi][...], 0.0)
            dk = dk + jnp.concatenate(kparts, axis=1)
            dv = dv + jnp.concatenate(vparts, axis=1)
        dk_ref[...] = _rot(dk, c_ref[...], s_ref[...], -1.0).astype(dk_ref.dtype)
        dv_ref[...] = dv.astype(dv_ref.dtype)

    full = pl.BlockSpec((STEPS, MAIN_W), lambda b, t: (b * tb + t, 0))
    tab = pl.BlockSpec((STEPS, 128), lambda b, t: (b * tb + t, 0))
    in_specs, ops = [tab, tab], [cos, sin]
    for (kc, kp, vc, vp) in parts:
        in_specs += [full, full] + [pl.BlockSpec((STEPS, 256), shifted(g)) for g in with_prev] * 2
        ops += [kc, vc] + [kp] * n_p + [vp] * n_p
    return pl.pallas_call(
        body, name=name, grid=(B, tb), in_specs=in_specs, out_specs=[full, full],
        out_shape=[jax.ShapeDtypeStruct((T, MAIN_W), BF)] * 2,
        compiler_params=_cp(("parallel", "parallel")),
    )(*ops)


def _loss(y, target, *, name, tm=512):
    T, Dm = y.shape
    nt = T // tm

    def body(y_ref, t_ref, l_ref, d_ref, acc):
        i = pl.program_id(0)
        err = y_ref[...] - t_ref[...]
        d_ref[...] = err / Dm
        part = jnp.sum(jnp.mean(err * err, axis=-1, keepdims=True).reshape(tm // 8, 8, 1), axis=0)

        @pl.when(i == 0)
        def _():
            acc[...] = part

        @pl.when(i > 0)
        def _():
            acc[...] += part

        @pl.when(i == nt - 1)
        def _():
            l_ref[...] = 0.5 * jnp.sum(acc[...], axis=0, keepdims=True)

    row = pl.BlockSpec((tm, Dm), lambda i: (i, 0))
    return pl.pallas_call(
        body, name=name, grid=(nt,), in_specs=[row, row],
        out_specs=[pl.BlockSpec((1, 1), lambda i: (0, 0)), row],
        out_shape=[jax.ShapeDtypeStruct((1, 1), F32), jax.ShapeDtypeStruct((T, Dm), F32)],
        scratch_shapes=[pltpu.VMEM((8, 1), F32)],
        compiler_params=_cp(("arbitrary",)),
    )(y, target)


def _adamw(w, g, m, v, *, name):
    shape = w.shape
    cols = shape[-1]
    rows = w.size // cols
    tm = rows
    for cand in (512, 352, 256, 128):
        if rows > cand and rows % cand == 0 and cand * cols * 4 <= (1 << 20):
            tm = cand
            break

    def body(w_ref, g_ref, m_ref, v_ref, d_ref, mo_ref, vo_ref):
        gv = g_ref[...]
        mn = ADAM_B1 * m_ref[...] + (1.0 - ADAM_B1) * gv
        vn = ADAM_B2 * v_ref[...] + (1.0 - ADAM_B2) * (gv * gv)
        m_hat = mn / (1.0 - ADAM_B1 ** ADAM_STEP)
        v_hat = vn / (1.0 - ADAM_B2 ** ADAM_STEP)
        d_ref[...] = -ADAM_LR * (m_hat / (jnp.sqrt(v_hat) + ADAM_EPS) + ADAM_WD * w_ref[...])
        mo_ref[...] = mn
        vo_ref[...] = vn

    blk = pl.BlockSpec((tm, cols), lambda i: (i, 0))
    outs = pl.pallas_call(
        body, name=name, grid=(rows // tm,), in_specs=[blk] * 4, out_specs=[blk] * 3,
        out_shape=[jax.ShapeDtypeStruct((rows, cols), F32)] * 3,
        compiler_params=_cp(("parallel",)),
    )(*[t.reshape(rows, cols) for t in (w, g, m, v)])
    return tuple(t.reshape(shape) for t in outs)


BIG = {
    'w_in': ((DEPTH, D_MODEL, D_MODEL), 'row'),
    'w_mem_kv': ((DEPTH, D_MODEL, 2 * MEM_W), 'row'),
    'w_out': ((DEPTH, D_MODEL, D_MODEL), 'row'),
    'w_kv': ((1, D_MODEL, 2 * MAIN_W), 'col'),
    'w_gate_up': ((DEPTH, D_MODEL, 2 * D_FF), 'col'),
    'w_down': ((DEPTH, D_FF, D_MODEL), 'row'),
}
BIG_NAMES = tuple(BIG)
N_CHIPS = 4
HBM_ANY = pl.BlockSpec(memory_space=pl.ANY)


def _geom(name):
    (L, R, C), kind = BIG[name]
    if kind == 'row':
        return L, R, C, kind, R // N_CHIPS, C, R // (2 * N_CHIPS)
    return L, R, C, kind, R, C // N_CHIPS, R // 2


def _shard_shape(name):
    L, R, C, kind, rs, cs, rh = _geom(name)
    return (L, rs, cs)


def _half_shape(name):
    L, R, C, kind, rs, cs, rh = _geom(name)
    return (L, rh, cs)


def _full_win(ref, name, s, h):
    L, R, C, kind, rs, cs, rh = _geom(name)
    if kind == 'row':
        rows = pl.ds(s * rs, rs) if h is None else pl.ds(s * rs + h * rh, rh)
        return ref.at[:, rows, :]
    rows = slice(None) if h is None else pl.ds(h * rh, rh)
    return ref.at[:, rows, pl.ds(s * cs, cs)]


def _shard_half(ref, name, h):
    L, R, C, kind, rs, cs, rh = _geom(name)
    return ref.at[:, pl.ds(h * rh, rh), :]


def _halves_win(ref, name, s):
    L, R, C, kind, rs, cs, rh = _geom(name)
    if kind == 'row':
        return ref.at[:, pl.ds(s * rh, rh), :]
    return ref.at[:, :, pl.ds(s * cs, cs)]


def _halves_shape(name):
    L, R, C, kind, rs, cs, rh = _geom(name)
    return (L, N_CHIPS * rh, cs) if kind == 'row' else (L, rh, C)


def _place():
    x, y, c = lax.axis_index("x"), lax.axis_index("y"), lax.axis_index("c")
    chips = [(1 - x, y), (x, 1 - y), (1 - x, 1 - y)]
    return x, y, c, chips


SMALL_ROWS = 24


def _all_gather(shards, small):
    names = BIG_NAMES
    nw = len(names)

    def body(*refs):
        src = dict(zip(names, refs[:nw]))
        small_ref = refs[nw]
        dst = dict(zip(names, refs[nw + 1:2 * nw + 1]))
        small_out = refs[2 * nw + 1]
        send_sems, recv_sems, local_sems = refs[2 * nw + 2:]
        x, y, c, chips = _place()
        s = 2 * x + y
        sib = (x, y, 1 - c)

        def remote(k, src_ref, dst_ref, to):
            return pltpu.make_async_remote_copy(src_ref=src_ref, dst_ref=dst_ref, send_sem=send_sems.at[k],
                                                recv_sem=recv_sems.at[k], device_id=to, device_id_type=MESH)

        local = []
        for wi, nm in enumerate(names):
            local.append(pltpu.make_async_copy(src[nm], _full_win(dst[nm], nm, s, None), local_sems.at[wi]))
        local.append(pltpu.make_async_copy(small_ref, small_out.at[s], local_sems.at[nw]))
        for cp in local:
            cp.start()
        sends = []
        for j, (px, py) in enumerate(chips):
            for wi, nm in enumerate(names):
                sends.append(remote(wi * 6 + j, _shard_half(src[nm], nm, c), _full_win(dst[nm], nm, s, c), (px, py, c)))
            sends.append(remote(nw * 6 + j, small_ref, small_out.at[s], (px, py, c)))
        for cp in sends:
            cp.start()
        for j, (px, py) in enumerate(chips):
            sp = 2 * px + py
            for wi, nm in enumerate(names):
                w = _full_win(dst[nm], nm, sp, c)
                remote(wi * 6 + j, w, w, sib).wait_recv()
                fwd = remote(wi * 6 + 3 + j, w, w, sib)
                fwd.start()
                sends.append(fwd)
            remote(nw * 6 + j, small_ref, small_out.at[sp], sib).wait_recv()
        for j, (px, py) in enumerate(chips):
            sp = 2 * px + py
            for wi, nm in enumerate(names):
                w = _full_win(dst[nm], nm, sp, 1 - c)
                remote(wi * 6 + 3 + j, w, w, sib).wait_recv()
        for cp in sends:
            cp.wait_send()
        for cp in local:
            cp.wait()

    n_sem = nw * 6 + 3
    outs = pl.pallas_call(
        body, name="all_gather_weights",
        in_specs=[HBM_ANY] * (nw + 1), out_specs=[HBM_ANY] * (nw + 1),
        out_shape=[jax.ShapeDtypeStruct(BIG[nm][0], BF) for nm in names]
        + [jax.ShapeDtypeStruct((N_CHIPS, SMALL_ROWS, 256), F32)],
        scratch_shapes=[pltpu.SemaphoreType.DMA((n_sem,)), pltpu.SemaphoreType.DMA((n_sem,)),
                        pltpu.SemaphoreType.DMA((nw + 1,))],
    )(*[shards[nm] for nm in names], small)
    return dict(zip(names, outs[:nw])), outs[nw]


SEM_SPEC = pl.BlockSpec(memory_space=pltpu.SEMAPHORE)
HBM_SPEC = pl.BlockSpec(memory_space=pltpu.HBM)
DATAFLOW = pltpu.SideEffectType.DATAFLOW_SIDE_EFFECTING


def _in_hbm(a):
    return pltpu.with_memory_space_constraint(a, pltpu.HBM)


def _remote(src, dst, send_sems, recv_sems, k, to):
    return pltpu.make_async_remote_copy(src_ref=src, dst_ref=dst, send_sem=send_sems.at[k], recv_sem=recv_sems.at[k],
                                        device_id=to, device_id_type=MESH)


def _split_start(name, bufs, n_copies, sends):
    nb = len(bufs)

    def body(*refs):
        in_refs = refs[:nb]
        send_sems, recv_sems = refs[nb], refs[nb + 1]
        token = refs[-1]
        for k, (src, dst, to) in enumerate(sends(in_refs)):
            _remote(src, dst, send_sems, recv_sems, k, to).start()
        token[...] = jnp.zeros_like(token)

    outs = pl.pallas_call(
        body, name=name,
        out_shape=(pltpu.SemaphoreType.DMA((n_copies,)), pltpu.SemaphoreType.DMA((n_copies,)),
                   *[pltpu.HBM(b.shape, b.dtype) for b in bufs], jax.ShapeDtypeStruct((8, 128), F32)),
        in_specs=[HBM_SPEC] * nb,
        out_specs=(SEM_SPEC, SEM_SPEC, *[HBM_SPEC] * nb, pl.BlockSpec(memory_space=pltpu.VMEM)),
        input_output_aliases={i: 2 + i for i in range(nb)},
        compiler_params=pltpu.CompilerParams(has_side_effects=DATAFLOW),
    )(*[_in_hbm(b) for b in bufs])
    return outs[0], outs[1], list(outs[2:2 + nb]), outs[-1]


def _split_wait(name, send_sems, recv_sems, bufs, after, sends, arrivals):
    nb = len(bufs)

    def body(*refs):
        in_refs = refs[:nb]
        s_sems, r_sems = refs[nb], refs[nb + 1]
        me = (lax.axis_index("x"), lax.axis_index("y"), lax.axis_index("c"))
        for k, (src, dst, to) in enumerate(sends(in_refs)):
            _remote(src, dst, s_sems, r_sems, k, to).wait_send()
        for k, win in enumerate(arrivals(in_refs)):
            _remote(win, win, s_sems, r_sems, k, me).wait_recv()

    outs = pl.pallas_call(
        body, name=name,
        out_shape=[pltpu.HBM(b.shape, b.dtype) for b in bufs],
        in_specs=[HBM_SPEC] * nb + [SEM_SPEC, SEM_SPEC, HBM_ANY],
        out_specs=[HBM_SPEC] * nb,
        input_output_aliases={i: i for i in range(nb)},
        compiler_params=pltpu.CompilerParams(has_side_effects=DATAFLOW),
    )(*bufs, send_sems, recv_sems, after)
    return list(outs)


LAYER_W = ('w_in', 'w_mem_kv', 'w_out', 'w_gate_up', 'w_down')


def _place_own(l, names, shards, small):
    nw = len(names)
    has_small = small is not None

    def body(*refs):
        n_in = nw + (1 if has_small else 0)
        srcs, dsts, sems = refs[:n_in], refs[n_in:2 * n_in], refs[2 * n_in]
        x, y, c, _ = _place()
        s = 2 * x + y
        cps = [pltpu.make_async_copy(srcs[wi], _full_win(dsts[wi], nm, s, None), sems.at[wi]) for wi, nm in enumerate(names)]
        if has_small:
            cps.append(pltpu.make_async_copy(srcs[nw], dsts[nw].at[s], sems.at[nw]))
        for cp in cps:
            cp.start()
        for cp in cps:
            cp.wait()

    out_shape = [jax.ShapeDtypeStruct((1,) + BIG[nm][0][1:], BF) for nm in names]
    ops = list(shards)
    if has_small:
        out_shape.append(jax.ShapeDtypeStruct((N_CHIPS, SMALL_ROWS, 256), F32))
        ops.append(small)
    return pl.pallas_call(
        body, name=f"l{l}_place_own_shard",
        in_specs=[HBM_ANY] * len(ops), out_specs=[HBM_ANY] * len(ops), out_shape=out_shape,
        scratch_shapes=[pltpu.SemaphoreType.DMA((len(ops),))],
    )(*ops)


def _gather_start(l, names, shards, small):
    nw = len(names)
    has_small = small is not None
    fulls = _place_own(l, names, shards, small)
    bufs = list(shards) + ([small] if has_small else []) + list(fulls)
    n_src = nw + (1 if has_small else 0)

    def sends(refs):
        x, y, c, chips = _place()
        s = 2 * x + y
        out = []
        for (px, py) in chips:
            for wi, nm in enumerate(names):
                out.append((_shard_half(refs[wi], nm, c), _full_win(refs[n_src + wi], nm, s, c), (px, py, c)))
            if has_small:
                out.append((refs[nw], refs[n_src + nw].at[s], (px, py, c)))
        return out

    def arrivals(refs):
        x, y, c, chips = _place()
        out = []
        for (px, py) in chips:
            sp = 2 * px + py
            for wi, nm in enumerate(names):
                out.append(_full_win(refs[n_src + wi], nm, sp, c))
            if has_small:
                out.append(refs[n_src + nw].at[sp])
        return out

    n_copies = 3 * n_src
    send_sems, recv_sems, bufs, token = _split_start(f"l{l}_gather_ici_start", bufs, n_copies, sends)
    return dict(l=l, names=names, has_small=has_small, sems=(send_sems, recv_sems), bufs=bufs, sends=sends,
                arrivals=arrivals, token=token)


def _gather_forward(st, after):
    l, names = st['l'], st['names']
    nw = len(names)
    n_src = nw + (1 if st['has_small'] else 0)
    bufs = _split_wait(f"l{l}_gather_ici_wait", *st['sems'], st['bufs'], after, st['sends'], st['arrivals'])
    fulls = bufs[n_src:n_src + nw]
    small_all = bufs[n_src + nw] if st['has_small'] else None

    def sends(refs):
        x, y, c, chips = _place()
        out = []
        for (px, py) in chips:
            sp = 2 * px + py
            for wi, nm in enumerate(names):
                w = _full_win(refs[wi], nm, sp, c)
                out.append((w, w, (x, y, 1 - c)))
        return out

    def arrivals(refs):
        x, y, c, chips = _place()
        out = []
        for (px, py) in chips:
            sp = 2 * px + py
            for wi, nm in enumerate(names):
                out.append(_full_win(refs[wi], nm, sp, 1 - c))
        return out

    send_sems, recv_sems, fulls, token = _split_start(f"l{l}_gather_d2d_start", fulls, 3 * nw, sends)
    return dict(l=l, names=names, sems=(send_sems, recv_sems), bufs=fulls, sends=sends, arrivals=arrivals,
                small_all=small_all, token=token)


def _gather_finish(st, after):
    fulls = _split_wait(f"l{st['l']}_gather_d2d_wait", *st['sems'], st['bufs'], after, st['sends'], st['arrivals'])
    return dict(zip(st['names'], fulls)), st['small_all']


def _exchange_halves(grads):
    names = BIG_NAMES
    nw = len(names)

    def body(*refs):
        src = dict(zip(names, refs[:nw]))
        dst = dict(zip(names, refs[nw:2 * nw]))
        send_sems, recv_sems = refs[2 * nw:]
        x, y, c, _ = _place()
        sib = (x, y, 1 - c)
        sends, recvs = [], []
        k = 0
        for nm in names:
            kind = BIG[nm][1]
            for sp in range(N_CHIPS if kind == 'row' else 1):
                if kind == 'row':
                    out_w = _full_win(src[nm], nm, sp, 1 - c)
                    in_w = _halves_win(dst[nm], nm, sp)
                else:
                    L, R, C, _, rs, cs, rh = _geom(nm)
                    out_w = src[nm].at[:, pl.ds((1 - c) * rh, rh), :]
                    in_w = dst[nm]
                cp = pltpu.make_async_remote_copy(src_ref=out_w, dst_ref=in_w, send_sem=send_sems.at[k],
                                                  recv_sem=recv_sems.at[k], device_id=sib, device_id_type=MESH)
                cp.start()
                sends.append(cp)
                k += 1
        for cp in sends:
            cp.wait_recv()
        for cp in sends:
            cp.wait_send()

    n_sem = sum(N_CHIPS if BIG[nm][1] == 'row' else 1 for nm in names)
    outs = pl.pallas_call(
        body, name="exchange_grad_halves",
        in_specs=[HBM_ANY] * nw, out_specs=[HBM_ANY] * nw,
        out_shape=[jax.ShapeDtypeStruct(_halves_shape(nm), F32) for nm in names],
        scratch_shapes=[pltpu.SemaphoreType.DMA((n_sem,)), pltpu.SemaphoreType.DMA((n_sem,))],
    )(*[grads[nm] for nm in names])
    return dict(zip(names, outs))


def _add_halves(name, g, r, sc):
    L, R, C, kind, rs, cs, rh = _geom(name)
    tr = rh if kind == 'row' else 256
    nr = rh // tr

    def body(sc_ref, g_ref, r_ref, hb_ref, own_ref):
        sp = pl.program_id(2)
        tot = g_ref[...] + r_ref[...]
        hb_ref[...] = tot.astype(hb_ref.dtype)

        @pl.when(sp == sc_ref[0])
        def _():
            own_ref[...] = tot

    if kind == 'row':
        g_map = lambda l, ri, sp, sc_ref: (l, sp * 2 + sc_ref[1], 0)
        h_map = lambda l, ri, sp, sc_ref: (l, sp, 0)
    else:
        g_map = lambda l, ri, sp, sc_ref: (l, sc_ref[1] * nr + ri, sp)
        h_map = lambda l, ri, sp, sc_ref: (l, ri, sp)
    own_map = lambda l, ri, sp, sc_ref: (l, ri, 0)
    blk = (None, tr, cs)
    return pl.pallas_call(
        body, name="add_halves_" + name,
        grid_spec=pltpu.PrefetchScalarGridSpec(
            num_scalar_prefetch=1, grid=(L, nr, N_CHIPS),
            in_specs=[pl.BlockSpec(blk, g_map), pl.BlockSpec(blk, h_map)],
            out_specs=[pl.BlockSpec(blk, h_map), pl.BlockSpec(blk, own_map)]),
        out_shape=[jax.ShapeDtypeStruct(_halves_shape(name), BF), jax.ShapeDtypeStruct(_half_shape(name), F32)],
        compiler_params=_cp(("parallel", "parallel", "arbitrary")),
    )(sc, g, r)


def _scatter_to_chips(halves):
    names = BIG_NAMES
    nw = len(names)

    def body(*refs):
        src = dict(zip(names, refs[:nw]))
        dst = dict(zip(names, refs[nw:2 * nw]))
        send_sems, recv_sems = refs[2 * nw:]
        x, y, c, chips = _place()
        sends = []
        for j, (px, py) in enumerate(chips):
            sp = 2 * px + py
            for wi, nm in enumerate(names):
                cp = pltpu.make_async_remote_copy(
                    src_ref=_halves_win(src[nm], nm, sp), dst_ref=dst[nm].at[j], send_sem=send_sems.at[wi * 3 + j],
                    recv_sem=recv_sems.at[wi * 3 + j], device_id=(px, py, c), device_id_type=MESH)
                cp.start()
                sends.append(cp)
        for cp in sends:
            cp.wait_recv()
        for cp in sends:
            cp.wait_send()

    outs = pl.pallas_call(
        body, name="scatter_grads_to_chips",
        in_specs=[HBM_ANY] * nw, out_specs=[HBM_ANY] * nw,
        out_shape=[jax.ShapeDtypeStruct((3,) + _half_shape(nm), BF) for nm in names],
        scratch_shapes=[pltpu.SemaphoreType.DMA((nw * 3,)), pltpu.SemaphoreType.DMA((nw * 3,))],
    )(*[halves[nm] for nm in names])
    return dict(zip(names, outs))


def _sum_pieces(name, own, pieces, sc):
    L, R, C, kind, rs, cs, rh = _geom(name)
    tr = rh if kind == 'row' else 256
    nr = rh // tr

    def body(sc_ref, o_ref, p_ref, out_ref):
        out_ref[...] = o_ref[...] + p_ref[0].astype(F32) + p_ref[1].astype(F32) + p_ref[2].astype(F32)

    blk = (None, tr, cs)
    return pl.pallas_call(
        body, name="sum_pieces_" + name,
        grid_spec=pltpu.PrefetchScalarGridSpec(
            num_scalar_prefetch=1, grid=(L, nr),
            in_specs=[pl.BlockSpec(blk, lambda l, ri, sc_ref: (l, ri, 0)),
                      pl.BlockSpec((3, None, tr, cs), lambda l, ri, sc_ref: (0, l, ri, 0))],
            out_specs=pl.BlockSpec(blk, lambda l, ri, sc_ref: (l, sc_ref[1] * nr + ri, 0))),
        out_shape=jax.ShapeDtypeStruct(_shard_shape(name), F32),
        compiler_params=_cp(("parallel", "parallel")),
    )(sc, own, pieces)


def _share_with_sibling(gshards):
    names = BIG_NAMES
    nw = len(names)

    def body(*refs):
        bufs = dict(zip(names, refs[nw:2 * nw]))
        send_sems, recv_sems = refs[2 * nw:]
        x, y, c, _ = _place()
        sib = (x, y, 1 - c)
        sends = []
        for wi, nm in enumerate(names):
            w = _shard_half(bufs[nm], nm, c)
            cp = pltpu.make_async_remote_copy(src_ref=w, dst_ref=w, send_sem=send_sems.at[wi], recv_sem=recv_sems.at[wi],
                                              device_id=sib, device_id_type=MESH)
            cp.start()
            sends.append(cp)
        for wi, nm in enumerate(names):
            w = _shard_half(bufs[nm], nm, 1 - c)
            pltpu.make_async_remote_copy(src_ref=w, dst_ref=w, send_sem=send_sems.at[wi], recv_sem=recv_sems.at[wi],
                                         device_id=sib, device_id_type=MESH).wait_recv()
        for cp in sends:
            cp.wait_send()

    outs = pl.pallas_call(
        body, name="share_grad_shards",
        in_specs=[HBM_ANY] * nw, out_specs=[HBM_ANY] * nw,
        out_shape=[jax.ShapeDtypeStruct(_shard_shape(nm), F32) for nm in names],
        input_output_aliases={i: i for i in range(nw)},
        scratch_shapes=[pltpu.SemaphoreType.DMA((nw,)), pltpu.SemaphoreType.DMA((nw,))],
    )(*[gshards[nm] for nm in names])
    return dict(zip(names, outs))


def _all_gather_small(v):
    rows = v.shape[0]

    def body(v_ref, out_ref, send_sems, recv_sems, local_sem):
        x, y, c, _ = _place()
        me = 4 * x + 2 * y + c
        mine = pltpu.make_async_copy(v_ref, out_ref.at[me], local_sem)
        mine.start()
        sends = []
        flips = [(fx, fy, fc) for fx in (0, 1) for fy in (0, 1) for fc in (0, 1)][1:]
        for k, (fx, fy, fc) in enumerate(flips):
            px, py, pc = (1 - x if fx else x), (1 - y if fy else y), (1 - c if fc else c)
            cp = pltpu.make_async_remote_copy(src_ref=v_ref, dst_ref=out_ref.at[me], send_sem=send_sems.at[k],
                                              recv_sem=recv_sems.at[k], device_id=(px, py, pc), device_id_type=MESH)
            cp.start()
            sends.append((cp, 4 * px + 2 * py + pc))
        for k, (cp, peer) in enumerate(sends):
            pltpu.make_async_remote_copy(src_ref=v_ref, dst_ref=out_ref.at[peer], send_sem=send_sems.at[k],
                                         recv_sem=recv_sems.at[k], device_id=(x, y, c), device_id_type=MESH).wait_recv()
        for cp, _ in sends:
            cp.wait_send()
        mine.wait()

    return pl.pallas_call(
        body, name="all_gather_small_grads",
        in_specs=[HBM_ANY], out_specs=HBM_ANY,
        out_shape=jax.ShapeDtypeStruct((8, rows, 128), F32),
        scratch_shapes=[pltpu.SemaphoreType.DMA((7,)), pltpu.SemaphoreType.DMA((7,)), pltpu.SemaphoreType.DMA],
    )(v)


def _sum8(v8, *, name, tr=336):
    rows = v8.shape[1]
    tr = min(tr, rows)
    assert rows % tr == 0

    def body(v_ref, o_ref):
        tot = v_ref[0]
        for d in range(1, 8):
            tot = tot + v_ref[d]
        o_ref[...] = tot

    return pl.pallas_call(
        body, name=name, grid=(rows // tr,),
        in_specs=[pl.BlockSpec((8, tr, 128), lambda i: (0, i, 0))], out_specs=pl.BlockSpec((tr, 128), lambda i: (i, 0)),
        out_shape=jax.ShapeDtypeStruct((rows, 128), F32),
        compiler_params=_cp(("parallel",)),
    )(v8)


def _block_diag(w_pool_l):
    wbd = jnp.zeros((MAIN_W, MAIN_W), F32)
    for gi in range(len(POOL_WINDOWS)):
        wbd = lax.dynamic_update_slice(wbd, w_pool_l[gi], (gi * POOL_GROUP, gi * POOL_GROUP))
    return wbd.astype(BF)


def _unpack_small(small_all):
    ng = small_all[:, :16, :].reshape(N_CHIPS, DEPTH, 4, 256).transpose(1, 2, 0, 3).reshape(DEPTH, 4, D_MODEL)
    ps = small_all[:, 16:18, :POOL_GROUP].transpose(1, 0, 2).reshape(N_A, MAIN_W)
    return ng, ps


def _local_step(x, mem, positions, weights_of, mem_norm, w_pool, kv_norm, target):
    B, S, _ = x.shape
    T = B * S
    xc = x.reshape(T, D_MODEL)
    memf = mem.reshape(B * N_MEM, D_MODEL)
    tgt = target.reshape(T, D_MODEL)
    cos, sin = _rope_tables(positions.reshape(T, 1), name="rope_tables")
    wbd = [_block_diag(w_pool[l]) for l in range(N_A)]
    nbo = D_FF // 256
    fw = []
    rk = rv = None
    kv_saved = None
    w0, small_all = weights_of(0, None, None)
    norm_gains, pool_scale = _unpack_small(small_all)
    wts = [w0]
    y1 = y2 = None

    for l in range(DEPTH):
        t = f"l{l}_"
        if l > 0:
            wts.append(weights_of(l, y1, y2)[0])
        sv = {'x_in': xc}
        h0, sv['r0'] = _norm_fwd(xc, norm_gains[l, 0], name=t + "norm0", out_dtype=BF)
        z, = _mm(h0, wts[l]['w_in'], b_layer=0, name=t + "mm_in")
        memn, sv['rm'] = _norm_fwd(memf, mem_norm[l], name=t + "norm_mem", out_dtype=BF, tm=256)
        kvm, = _mm(memn, wts[l]['w_mem_kv'], b_layer=0, name=t + "mm_memkv", out_dtypes=(BF,))
        if l < N_A:
            ycat, sv['p'] = _pool_fwd(z, wbd[l], pool_scale[l], B, S, name=t + "pool_fwd")
        else:
            rq = _rope_apply(z, cos, sin, name=t + "rope_q", out_dtype=F32)
            o = lax.empty((T, MAIN_W), F32)
            lse = lax.empty((T, MAIN_W), F32)
            for g in range(3):
                o, lse = _dil_fwd_group(g, rq, rk, rv, o, lse, B, S, name=t + f"dil_fwd{g}")
            ycat = _dil_combine_fwd(o, lse, lax.empty((T, D_MODEL), BF), name=t + "dil_combine")
            sv.update(rq=rq, o=o, lse=lse)
        ycat, sv['lse_m'] = _memattn_fwd(z, kvm, ycat, B, S, name=t + "memattn_fwd")
        y1, = _mm(ycat, wts[l]['w_out'], b_layer=0, name=t + "mm_out")
        x1, sv['r1'] = _norm_fwd(y1, norm_gains[l, 1], name=t + "norm1", res=xc)
        h2, sv['r2'] = _norm_fwd(x1, norm_gains[l, 2], name=t + "norm2", out_dtype=BF)
        gg, uu, aa = _mm(h2, wts[l]['w_gate_up'], b_layer=0, b_offsets=(0, nbo), out_n=D_FF, tn=256, name=t + "mm_gate_up",
                         epilogue=_swiglu_fwd_epilogue, out_dtypes=(BF, BF, BF))
        y2, = _mm(aa, wts[l]['w_down'], b_layer=0, tk=D_FF, name=t + "mm_down")
        x2, sv['r3'] = _norm_fwd(y2, norm_gains[l, 3], name=t + "norm3", res=x1)
        sv.update(h0=h0, z=z, memn=memn, kvm=kvm, ycat=ycat, y1=y1, x1=x1, h2=h2, gg=gg, uu=uu, aa=aa, y2=y2)
        fw.append(sv)
        xc = x2
        if l == N_A - 1:
            kvn, rkv = _norm_fwd(xc, kv_norm, name="norm_kv", out_dtype=BF)
            kv, = _mm(kvn, wts[N_A - 1]['w_kv'], b_layer=0, name="mm_kv")
            rk, rv = _rope_apply(kv, cos, sin, name="rope_k", passthrough=True, out_dtype=F32)
            kv_saved = (xc, kvn, rkv)

    loss, dx = _loss(xc, tgt, name="loss")

    gbig = {nm: lax.empty(BIG[nm][0], F32) for nm in BIG_NAMES}
    d_ng = [[None] * 4 for _ in range(DEPTH)]
    d_memnorm = [None] * DEPTH
    d_wbd = [None] * N_A
    d_pscale = [None] * N_A
    d_kvnorm = None
    kv_parts = []

    for l in reversed(range(DEPTH)):
        t = f"l{l}_b_"
        sv = fw[l]
        dy2, d_ng[l][3] = _norm_bwd(dx, sv['y2'], sv['r3'], norm_gains[l, 3], name=t + "norm3", out_dtype=BF)
        gbig['w_down'] = _mm(sv['aa'], dy2, ta=True, tm=1408, name=t + "dw_down", stack=(gbig['w_down'], l))
        dg, du = _mm(dy2, wts[l]['w_down'], tb=True, b_layer=0, tn=256, name=t + "d_act",
                     extras=((sv['gg'], 'tile'), (sv['uu'], 'tile')), epilogue=_swiglu_bwd_epilogue, out_dtypes=(BF, BF))
        gbig['w_gate_up'] = _mm(sv['h2'], (dg, du), ta=True, tn=1408, tk=512, name=t + "dw_gate_up",
                                stack=(gbig['w_gate_up'], l))
        dh2, = _mm((dg, du), wts[l]['w_gate_up'], tb=True, b_layer=0, tk=1408, name=t + "d_h2", out_dtypes=(BF,))
        dx1, d_ng[l][2] = _norm_bwd(dh2, sv['x1'], sv['r2'], norm_gains[l, 2], name=t + "norm2", add=dx)
        dy1, d_ng[l][1] = _norm_bwd(dx1, sv['y1'], sv['r1'], norm_gains[l, 1], name=t + "norm1", out_dtype=BF)
        gbig['w_out'] = _mm(sv['ycat'], dy1, ta=True, name=t + "dw_out", stack=(gbig['w_out'], l))
        dycat, = _mm(dy1, wts[l]['w_out'], tb=True, b_layer=0, name=t + "d_ycat")
        dz = lax.empty((T, D_MODEL), BF)
        dz, dkm, dvm = _memattn_bwd(dycat, sv['z'], sv['kvm'], sv['lse_m'], dz, B, S, name=t + "memattn")
        if l < N_A:
            dz, d_wbd[l], d_pscale[l] = _pool_bwd(dycat, sv['p'], wbd[l], pool_scale[l], dz, B, S, name=t + "pool")
        else:
            do, cb = _dil_combine_bwd(dycat, sv['o'], sv['lse'], name=t + "dil_combine")
            acc = tuple(lax.empty((T, MAIN_W), F32) for _ in range(5))
            for g in range(3):
                acc = _dil_bwd_group(g, sv['rq'], rk, rv, do, cb, sv['lse'], acc, B, S, name=t + f"dil{g}")
            dz = _rope_apply(acc[0], cos, sin, name=t + "rope_q", sign=-1.0, alias=dz)
            kv_parts.append(acc[1:])
        gbig['w_in'] = _mm(sv['h0'], dz, ta=True, name=t + "dw_in", stack=(gbig['w_in'], l))
        dh0, = _mm(dz, wts[l]['w_in'], tb=True, b_layer=0, name=t + "d_h0", out_dtypes=(BF,))
        dx, d_ng[l][0] = _norm_bwd(dh0, sv['x_in'], sv['r0'], norm_gains[l, 0], name=t + "norm0", add=dx1)
        gbig['w_mem_kv'] = _mm(sv['memn'], (dkm, dvm), ta=True, tn=256, name=t + "dw_memkv", stack=(gbig['w_mem_kv'], l))
        dmemn, = _mm((dkm, dvm), wts[l]['w_mem_kv'], tb=True, b_layer=0, tk=256, name=t + "d_memn", out_dtypes=(BF,))
        _, d_memnorm[l] = _norm_bwd(dmemn, memf, sv['rm'], mem_norm[l], name=t + "norm_mem", out_dtype=BF, tm=256)
        if l == N_A:
            dk, dv = _kv_grad(kv_parts, cos, sin, B, S, name="kv_grad")
            x_kv, kvn, rkv = kv_saved
            gbig['w_kv'] = _mm(kvn, (dk, dv), ta=True, tn=768, name="dw_kv", stack=(gbig['w_kv'], 0))
            dkvn, = _mm((dk, dv), wts[N_A - 1]['w_kv'], tb=True, b_layer=0, tk=768, name="d_kvn", out_dtypes=(BF,))
            dx, d_kvnorm = _norm_bwd(dkvn, x_kv, rkv, kv_norm, name="norm_kv_b", add=dx)

    small = {
        'norm_gains': jnp.stack([jnp.concatenate(d_ng[l], axis=0) for l in range(DEPTH)]),
        'mem_norm': jnp.concatenate(d_memnorm, axis=0),
        'kv_norm': d_kvnorm.reshape(D_MODEL),
        'pool_scale': jnp.concatenate(d_pscale, axis=0),
        'w_pool': jnp.stack([jnp.stack([d_wbd[l][gi * POOL_GROUP:(gi + 1) * POOL_GROUP, gi * POOL_GROUP:(gi + 1) * POOL_GROUP]
                                        for gi in range(len(POOL_WINDOWS))]) for l in range(N_A)]),
    }
    return loss, dx, gbig, small


SMALL_ORDER = ('norm_gains', 'mem_norm', 'kv_norm', 'pool_scale', 'w_pool')
SMALL_VEC_ROWS = 2560


def kernel(x, mem, positions, norm_gains, mem_norm, w_in, w_mem_kv, w_out, w_pool, pool_scale, kv_norm, w_kv, w_gate_up, w_down, loss_target, m_norm_gains, m_mem_norm, m_w_in, m_w_mem_kv, m_w_out, m_w_pool, m_pool_scale, m_kv_norm, m_w_kv, m_w_gate_up, m_w_down, v_norm_gains, v_mem_norm, v_w_in, v_w_mem_kv, v_w_out, v_w_pool, v_pool_scale, v_kv_norm, v_w_kv, v_w_gate_up, v_w_down):
    xi, yi, ci = lax.axis_index("x"), lax.axis_index("y"), lax.axis_index("c")
    s = 2 * xi + yi
    sc = jnp.stack([s, ci]).astype(jnp.int32)
    weights = dict(norm_gains=norm_gains, mem_norm=mem_norm, w_in=w_in, w_mem_kv=w_mem_kv, w_out=w_out, w_pool=w_pool,
                   pool_scale=pool_scale, kv_norm=kv_norm, w_kv=w_kv, w_gate_up=w_gate_up, w_down=w_down)
    moms = dict(norm_gains=m_norm_gains, mem_norm=m_mem_norm, w_in=m_w_in, w_mem_kv=m_w_mem_kv, w_out=m_w_out,
                w_pool=m_w_pool, pool_scale=m_pool_scale, kv_norm=m_kv_norm, w_kv=m_w_kv, w_gate_up=m_w_gate_up,
                w_down=m_w_down)
    vels = dict(norm_gains=v_norm_gains, mem_norm=v_mem_norm, w_in=v_w_in, w_mem_kv=v_w_mem_kv, w_out=v_w_out,
                w_pool=v_w_pool, pool_scale=v_pool_scale, kv_norm=v_kv_norm, w_kv=v_w_kv, w_gate_up=v_w_gate_up,
                w_down=v_w_down)

    small_w = jnp.zeros((SMALL_ROWS, 256), F32)
    small_w = lax.dynamic_update_slice(small_w, norm_gains.reshape(16, 256), (0, 0))
    small_w = lax.dynamic_update_slice(small_w, pool_scale, (16, 0))
    started = []
    for l in range(DEPTH):
        names = LAYER_W + (('w_kv',) if l == N_A - 1 else ())
        shards = [weights[nm][l:l + 1].astype(BF) for nm in LAYER_W]
        if l == N_A - 1:
            shards.append(w_kv.astype(BF).reshape(_shard_shape('w_kv')))
        started.append(_gather_start(l, names, shards, small_w if l == 0 else None))
    all_started = started[0]['token'] + started[1]['token'] + started[2]['token'] + started[3]['token']

    def weights_of(l, mid, end):
        fwd = _gather_forward(started[l], all_started if mid is None else mid)
        return _gather_finish(fwd, fwd['token'] if end is None else end)

    loss, gx, gbig, gsmall = _local_step(x, mem, positions, weights_of, mem_norm, w_pool, kv_norm, loss_target)
    loss = lax.psum(loss[0, 0], ("x", "y", "c"))

    from_sib = _exchange_halves(gbig)
    halves, own = {}, {}
    for nm in BIG_NAMES:
        halves[nm], own[nm] = _add_halves(nm, gbig[nm], from_sib[nm], sc)
    pieces = _scatter_to_chips(halves)
    gsh = _share_with_sibling({nm: _sum_pieces(nm, own[nm], pieces[nm], sc) for nm in BIG_NAMES})

    vec = jnp.concatenate([gsmall[nm].reshape(-1) for nm in SMALL_ORDER])
    vec = jnp.pad(vec, (0, SMALL_VEC_ROWS * 128 - vec.shape[0])).reshape(SMALL_VEC_ROWS, 128)
    tot = _sum8(_all_gather_small(vec), name="sum_small_grads", tr=512).reshape(-1)
    grads, off = {}, 0
    for nm in SMALL_ORDER:
        shape = (DEPTH, 4, D_MODEL) if nm == 'norm_gains' else (N_A, MAIN_W) if nm == 'pool_scale' else weights[nm].shape
        n = 1
        for dim in shape:
            n *= dim
        grads[nm] = tot[off:off + n].reshape(shape)
        off += n
    grads['norm_gains'] = lax.dynamic_slice(grads['norm_gains'], (0, 0, s * 256), (DEPTH, 4, 256))
    grads['pool_scale'] = lax.dynamic_slice(grads['pool_scale'], (0, s * POOL_GROUP), (N_A, POOL_GROUP))
    for nm in BIG_NAMES:
        grads[nm] = gsh[nm].reshape(weights[nm].shape)

    order = ('norm_gains', 'mem_norm', 'w_in', 'w_mem_kv', 'w_out', 'w_pool', 'pool_scale', 'kv_norm', 'w_kv',
             'w_gate_up', 'w_down')
    deltas, new_m, new_v = {}, {}, {}
    for nm in order:
        deltas[nm], new_m[nm], new_v[nm] = _adamw(weights[nm], grads[nm], moms[nm], vels[nm], name="adamw_" + nm)
    return (loss, gx.reshape(x.shape), *[grads[nm] for nm in order], *[deltas[nm] for nm in order],
            *[new_m[nm] for nm in order], *[new_v[nm] for nm in order])
```

```python
import functools

import jax
import jax.numpy as jnp
from jax import lax
from jax.experimental import pallas as pl
from jax.experimental.pallas import tpu as pltpu

F32 = jnp.float32
BF = jnp.bfloat16

D_MODEL = 1024
DEPTH = 4
N_A = 2
HEAD_DIM = 64
MEM_W = 256
MAIN_W = 768
D_FF = 2816
N_MEM = 256
POOL_WINDOWS = (2, 4, 8, 16)
POOL_GROUP = 192
DIL = (1, 4, 16)
STEPS = 128
ROPE_THETA = 10000.0
EPS = 1e-6
SCALE = HEAD_DIM ** -0.5
NEG = -1e30

ADAM_LR = 0.001
ADAM_B1 = 0.9
ADAM_B2 = 0.999
ADAM_EPS = 1e-08
ADAM_WD = 0.01
ADAM_STEP = 10

VMEM_LIMIT = 48 * 1024 * 1024
MESH = pl.DeviceIdType.MESH


def _cp(sem):
    return pltpu.CompilerParams(dimension_semantics=sem, vmem_limit_bytes=VMEM_LIMIT)


def _mm(a, b, *, name, ta=False, tb=False, tm=1024, tn=512, tk=1024, b_layer=None, b_offsets=(0,),
        extras=(), epilogue=None, out_dtypes=(F32,), out_n=None, stack=None):
    a_pair = isinstance(a, (tuple, list))
    b_pair = isinstance(b, (tuple, list))
    a0 = a[0] if a_pair else a
    b0 = b[0] if b_pair else b
    a_rows, a_cols = a0.shape
    if a_pair:
        a_cols *= 2
    b_rows, b_cols = b0.shape[-2:]
    if b_pair:
        b_cols *= 2
    M, K = (a_cols, a_rows) if ta else (a_rows, a_cols)
    N = b_rows if tb else b_cols
    if out_n is not None:
        N = out_n
    tm, tn, tk = min(tm, M), min(tn, N), min(tk, K)
    assert M % tm == 0 and N % tn == 0 and K % tk == 0, (name, M, N, K, tm, tn, tk)
    nk = K // tk
    n_acc = len(b_offsets)

    if a_pair:
        a_half = (a0.shape[1] // (tm if ta else tk))
    if b_pair:
        b_half = (b0.shape[1] // (tk if tb else tn))

    def a_map(sel):
        def f(i, j, k):
            r, c = (k, i) if ta else (i, k)
            if a_pair:
                c = jnp.clip(c - sel * a_half, 0, a_half - 1)
            return (r, c)
        return f

    def b_map(sel, off):
        def f(i, j, k):
            r, c = (j + off, k) if tb else (k, j + off)
            if b_pair:
                c = jnp.clip(c - sel * b_half, 0, b_half - 1)
            if b_layer is not None:
                return (b_layer, r, c)
            return (r, c)
        return f

    a_blk = (tk, tm) if ta else (tm, tk)
    b_blk = (tn, tk) if tb else (tk, tn)
    if b_layer is not None:
        b_blk = (None,) + b_blk
    in_specs, operands = [], []
    for sel in range(2 if a_pair else 1):
        in_specs.append(pl.BlockSpec(a_blk, a_map(sel)))
        operands.append(a[sel] if a_pair else a)
    n_a = len(operands)
    for off in b_offsets:
        for sel in range(2 if b_pair else 1):
            in_specs.append(pl.BlockSpec(b_blk, b_map(sel, off)))
            operands.append(b[sel] if b_pair else b)
    n_b = len(operands) - n_a
    for arr, kind in extras:
        if kind == 'tile':
            in_specs.append(pl.BlockSpec((tm, tn), lambda i, j, k: (i, j)))
        elif kind == 'row':
            in_specs.append(pl.BlockSpec((tm, 1), lambda i, j, k: (i, 0)))
        else:
            in_specs.append(pl.BlockSpec((1, tn), lambda i, j, k: (0, j)))
        operands.append(arr)
    n_e = len(extras)
    n_o = len(out_dtypes)
    dims = (((0,) if ta else (1,), (1,) if tb else (0,)), ((), ()))

    def body(*refs):
        a_refs = refs[:n_a]
        b_refs = refs[n_a:n_a + n_b]
        e_refs = refs[n_a + n_b:n_a + n_b + n_e]
        n_in = n_a + n_b + n_e + (1 if stack is not None else 0)
        o_refs = refs[n_in:n_in + n_o]
        acc_refs = refs[n_in + n_o:]
        i, j, k = pl.program_id(0), pl.program_id(1), pl.program_id(2)
        if a_pair:
            cidx = i if ta else k
            av = jnp.where(cidx < a_half, a_refs[0][...], a_refs[1][...])
        else:
            av = a_refs[0][...]
        av = av.astype(BF)
        prods = []
        for q in range(n_acc):
            if b_pair:
                cidx = (k if tb else j) + b_offsets[q]
                bv = jnp.where(cidx < b_half, b_refs[2 * q][...], b_refs[2 * q + 1][...])
            else:
                bv = b_refs[q][...]
            prods.append(lax.dot_general(av, bv.astype(BF), dims, preferred_element_type=F32))

        def finish(accs):
            outs = epilogue(accs, *[r[...] for r in e_refs]) if epilogue is not None else accs
            for o_ref, o in zip(o_refs, outs):
                o_ref[...] = o.astype(o_ref.dtype)

        if nk == 1:
            finish(prods)
        else:
            @pl.when(k == 0)
            def _():
                for r, p in zip(acc_refs, prods):
                    r[...] = p

            @pl.when(k > 0)
            def _():
                for r, p in zip(acc_refs, prods):
                    r[...] += p

            @pl.when(k == nk - 1)
            def _():
                finish([r[...] for r in acc_refs])

    if stack is not None:
        buf, layer = stack
        assert n_o == 1 and buf.shape[1:] == (M, N)
        return pl.pallas_call(
            body, name=name,
            grid=(M // tm, N // tn, nk),
            in_specs=in_specs + [pl.BlockSpec(memory_space=pl.ANY)],
            out_specs=[pl.BlockSpec((None, tm, tn), lambda i, j, k: (layer, i, j))],
            out_shape=[jax.ShapeDtypeStruct(buf.shape, buf.dtype)],
            scratch_shapes=[pltpu.VMEM((tm, tn), F32) for _ in range(n_acc if nk > 1 else 0)],
            input_output_aliases={len(operands): 0},
            compiler_params=_cp(("parallel", "parallel", "arbitrary")),
        )(*operands, buf)[0]
    return pl.pallas_call(
        body, name=name,
        grid=(M // tm, N // tn, nk),
        in_specs=in_specs,
        out_specs=[pl.BlockSpec((tm, tn), lambda i, j, k: (i, j)) for _ in range(n_o)],
        out_shape=[jax.ShapeDtypeStruct((M, N), dt) for dt in out_dtypes],
        scratch_shapes=[pltpu.VMEM((tm, tn), F32) for _ in range(n_acc if nk > 1 else 0)],
        compiler_params=_cp(("parallel", "parallel", "arbitrary")),
    )(*operands)


def _norm_fwd(x, g, *, name, res=None, out_dtype=F32, tm=512):
    T, Dm = x.shape
    has_res = res is not None

    def body(*refs):
        if has_res:
            x_ref, g_ref, r_ref, y_ref, s_ref = refs
        else:
            x_ref, g_ref, y_ref, s_ref = refs
        xv = x_ref[...]
        rstd = lax.rsqrt(jnp.mean(xv * xv, axis=-1, keepdims=True) + EPS)
        y = xv * rstd * g_ref[...]
        if has_res:
            y = r_ref[...] + y
        y_ref[...] = y.astype(y_ref.dtype)
        s_ref[...] = rstd

    row = pl.BlockSpec((tm, Dm), lambda i: (i, 0))
    in_specs = [row, pl.BlockSpec((1, Dm), lambda i: (0, 0))] + ([row] if has_res else [])
    ops = [x, g.reshape(1, Dm)] + ([res] if has_res else [])
    return pl.pallas_call(
        body, name=name, grid=(T // tm,), in_specs=in_specs,
        out_specs=[row, pl.BlockSpec((tm, 1), lambda i: (i, 0))],
        out_shape=[jax.ShapeDtypeStruct((T, Dm), out_dtype), jax.ShapeDtypeStruct((T, 1), F32)],
        compiler_params=_cp(("parallel",)),
    )(*ops)


def _norm_bwd(dout, x, rstd, g, *, name, add=None, out_dtype=F32, tm=512):
    T, Dm = x.shape
    has_add = add is not None
    nt = T // tm

    def body(*refs):
        if has_add:
            do_ref, x_ref, s_ref, g_ref, a_ref, dx_ref, dg_ref, acc = refs
        else:
            do_ref, x_ref, s_ref, g_ref, dx_ref, dg_ref, acc = refs
        i = pl.program_id(0)
        do = do_ref[...].astype(F32)
        xh = x_ref[...] * s_ref[...]
        gd = do * g_ref[...]
        dx = s_ref[...] * (gd - xh * jnp.mean(gd * xh, axis=-1, keepdims=True))
        if has_add:
            dx = dx + a_ref[...].astype(F32)
        dx_ref[...] = dx.astype(dx_ref.dtype)
        part = jnp.sum((do * xh).reshape(tm // 8, 8, Dm), axis=0)

        @pl.when(i == 0)
        def _():
            acc[...] = part

        @pl.when(i > 0)
        def _():
            acc[...] += part

        @pl.when(i == nt - 1)
        def _():
            dg_ref[...] = jnp.sum(acc[...], axis=0, keepdims=True)

    row = pl.BlockSpec((tm, Dm), lambda i: (i, 0))
    in_specs = [row, row, pl.BlockSpec((tm, 1), lambda i: (i, 0)), pl.BlockSpec((1, Dm), lambda i: (0, 0))]
    ops = [dout, x, rstd, g.reshape(1, Dm)]
    if has_add:
        in_specs.append(row)
        ops.append(add)
    return pl.pallas_call(
        body, name=name, grid=(nt,), in_specs=in_specs,
        out_specs=[row, pl.BlockSpec((1, Dm), lambda i: (0, 0))],
        out_shape=[jax.ShapeDtypeStruct((T, Dm), out_dtype), jax.ShapeDtypeStruct((1, Dm), F32)],
        scratch_shapes=[pltpu.VMEM((8, Dm), F32)],
        compiler_params=_cp(("arbitrary",)),
    )(*ops)


def _swiglu_fwd_epilogue(accs):
    g, u = accs
    return g, u, g * jax.nn.sigmoid(g) * u


def _swiglu_bwd_epilogue(accs, g, u):
    da = accs[0]
    g = g.astype(F32)
    u = u.astype(F32)
    sig = jax.nn.sigmoid(g)
    return da * u * (sig * (1.0 + g * (1.0 - sig))), da * (g * sig)


def _rope_tables(pos, *, name, tm=1024):
    T = pos.shape[0]
    half = HEAD_DIM // 2
    freqs = ROPE_THETA ** (-jnp.arange(half, dtype=F32) / half)
    freqs = jnp.tile(freqs, 4).reshape(1, 128)

    def body(p_ref, f_ref, c_ref, s_ref):
        ang = p_ref[...].astype(F32) * f_ref[...]
        lane = lax.broadcasted_iota(jnp.int32, ang.shape, 1)
        c_ref[...] = jnp.cos(ang)
        s_ref[...] = jnp.where(lane % HEAD_DIM < half, -1.0, 1.0) * jnp.sin(ang)

    tab = pl.BlockSpec((tm, 128), lambda i: (i, 0))
    return pl.pallas_call(
        body, name=name, grid=(T // tm,),
        in_specs=[pl.BlockSpec((tm, 1), lambda i: (i, 0)), pl.BlockSpec((1, 128), lambda i: (0, 0))],
        out_specs=[tab, tab],
        out_shape=[jax.ShapeDtypeStruct((T, 128), F32)] * 2,
        compiler_params=_cp(("parallel",)),
    )(pos, freqs)


def _rot(x, cos, sin, sign):
    W = x.shape[1]
    half = HEAD_DIM // 2
    reps = W // 128
    c = jnp.concatenate([cos] * reps, axis=1) if reps > 1 else cos
    s = jnp.concatenate([sin] * reps, axis=1) if reps > 1 else sin
    lane = lax.broadcasted_iota(jnp.int32, x.shape, 1)
    swapped = jnp.where(lane % HEAD_DIM < half, pltpu.roll(x, W - half, axis=1), pltpu.roll(x, half, axis=1))
    return x * c + (sign * s) * swapped


def _rope_apply(x, cos, sin, *, name, sign=1.0, width=MAIN_W, passthrough=False, out_dtype=BF, alias=None,
                out_cols=None, tm=512):
    T = x.shape[0]

    def body(*refs):
        if passthrough:
            x_ref, v_ref, c_ref, s_ref, o_ref, ov_ref = refs
            ov_ref[...] = v_ref[...].astype(ov_ref.dtype)
        elif alias is not None:
            x_ref, c_ref, s_ref, _, o_ref = refs
        else:
            x_ref, c_ref, s_ref, o_ref = refs
        o_ref[...] = _rot(x_ref[...].astype(F32), c_ref[...], s_ref[...], sign).astype(o_ref.dtype)

    blk0 = pl.BlockSpec((tm, width), lambda i: (i, 0))
    blk1 = pl.BlockSpec((tm, width), lambda i: (i, 1))
    tab = pl.BlockSpec((tm, 128), lambda i: (i, 0))
    if passthrough:
        return pl.pallas_call(
            body, name=name, grid=(T // tm,), in_specs=[blk0, blk1, tab, tab], out_specs=[blk0, blk0],
            out_shape=[jax.ShapeDtypeStruct((T, width), out_dtype)] * 2,
            compiler_params=_cp(("parallel",)),
        )(x, x, cos, sin)
    if alias is not None:
        return pl.pallas_call(
            body, name=name, grid=(T // tm,),
            in_specs=[blk0, tab, tab, pl.BlockSpec(memory_space=pl.ANY)], out_specs=blk0,
            out_shape=jax.ShapeDtypeStruct(alias.shape, alias.dtype),
            input_output_aliases={3: 0},
            compiler_params=_cp(("parallel",)),
        )(x, cos, sin, alias)
    return pl.pallas_call(
        body, name=name, grid=(T // tm,), in_specs=[blk0, tab, tab], out_specs=blk0,
        out_shape=jax.ShapeDtypeStruct((T, width), out_dtype),
        compiler_params=_cp(("parallel",)),
    )(x, cos, sin)


POOL_T = 256
POOL_HALO = 16


def _pool_lane_window(shape):
    lane = lax.broadcasted_iota(jnp.int32, shape, 1)
    w = jnp.full(shape, POOL_WINDOWS[0], jnp.int32)
    for gi in range(1, len(POOL_WINDOWS)):
        w = jnp.where(lane >= gi * POOL_GROUP, POOL_WINDOWS[gi], w)
    return w


def _pool_fwd(z, wbd, scale, B, S, *, name):
    T = z.shape[0]
    nt = S // POOL_T
    hb = POOL_T // POOL_HALO

    def body(z_ref, h_ref, w_ref, sc_ref, y_ref, p_ref, ext):
        i = pl.program_id(1)
        u = z_ref[...]
        ext[pl.ds(POOL_HALO, POOL_T), :] = u
        ext[pl.ds(0, POOL_HALO), :] = jnp.where(i > 0, h_ref[...], 0.0)
        win = _pool_lane_window((POOL_T, MAIN_W))
        acc = u
        for k in range(1, POOL_HALO):
            acc = acc + jnp.where(k < win, ext[pl.ds(POOL_HALO - k, POOL_T), :], 0.0)
        t = i * POOL_T + lax.broadcasted_iota(jnp.int32, (POOL_T, MAIN_W), 0)
        cnt = jnp.minimum(t + 1, win).astype(F32)
        p = (acc / cnt - u).astype(BF)
        p_ref[...] = p
        y = jnp.dot(p, w_ref[...], preferred_element_type=F32) * sc_ref[...]
        y_ref[...] = y.astype(y_ref.dtype)

    return pl.pallas_call(
        body, name=name, grid=(B, nt),
        in_specs=[pl.BlockSpec((POOL_T, MAIN_W), lambda b, i: (b * nt + i, 0)),
                  pl.BlockSpec((POOL_HALO, MAIN_W), lambda b, i: (jnp.maximum((b * nt + i) * hb - 1, 0), 0)),
                  pl.BlockSpec((MAIN_W, MAIN_W), lambda b, i: (0, 0)),
                  pl.BlockSpec((1, MAIN_W), lambda b, i: (0, 0))],
        out_specs=[pl.BlockSpec((POOL_T, MAIN_W), lambda b, i: (b * nt + i, 0)),
                   pl.BlockSpec((POOL_T, MAIN_W), lambda b, i: (b * nt + i, 0))],
        out_shape=[jax.ShapeDtypeStruct((T, D_MODEL), BF), jax.ShapeDtypeStruct((T, MAIN_W), BF)],
        scratch_shapes=[pltpu.VMEM((POOL_T + POOL_HALO, MAIN_W), F32)],
        compiler_params=_cp(("parallel", "parallel")),
    )(z, z, wbd, scale.reshape(1, MAIN_W))


def _pool_bwd(dy, p, wbd, scale, dz_alias, B, S, *, name):
    T = dy.shape[0]
    nt = S // POOL_T
    hb = POOL_T // POOL_HALO
    last_halo = T // POOL_HALO - 1
    R = POOL_T + POOL_HALO

    def body(dy_ref, dyn_ref, p_ref, pn_ref, w_ref, sc_ref, _, dz_ref, dw_ref, ds_ref, ext, dw_acc, ds_acc):
        b, i = pl.program_id(0), pl.program_id(1)
        first = jnp.logical_and(b == 0, i == 0)
        dyv = dy_ref[...]
        pv = p_ref[...]
        sc = sc_ref[...]
        w = w_ref[...]
        pw = jnp.dot(pv, w, preferred_element_type=F32)
        ds_part = jnp.sum((dyv * pw).reshape(POOL_T // 8, 8, MAIN_W), axis=0)
        dpw = (dyv * sc).astype(BF)
        dw_part = lax.dot_general(pv, dpw, (((0,), (0,)), ((), ())), preferred_element_type=F32)

        @pl.when(first)
        def _():
            dw_acc[...] = dw_part
            ds_acc[...] = ds_part

        @pl.when(jnp.logical_not(first))
        def _():
            dw_acc[...] += dw_part
            ds_acc[...] += ds_part

        @pl.when(jnp.logical_and(b == pl.num_programs(0) - 1, i == nt - 1))
        def _():
            dw_ref[...] = dw_acc[...]
            ds_ref[...] = jnp.sum(ds_acc[...], axis=0, keepdims=True)

        dp = lax.dot_general(dpw, w, (((1,), (1,)), ((), ())), preferred_element_type=F32)
        dpn = lax.dot_general((dyn_ref[...] * sc).astype(BF), w, (((1,), (1,)), ((), ())), preferred_element_type=F32)
        win = _pool_lane_window((POOL_T, MAIN_W))
        win_n = _pool_lane_window((POOL_HALO, MAIN_W))
        t = i * POOL_T + lax.broadcasted_iota(jnp.int32, (POOL_T, MAIN_W), 0)
        tn = (i + 1) * POOL_T + lax.broadcasted_iota(jnp.int32, (POOL_HALO, MAIN_W), 0)
        ext[pl.ds(0, POOL_T), :] = dp / jnp.minimum(t + 1, win).astype(F32)
        ext[pl.ds(POOL_T, POOL_HALO), :] = jnp.where(i < nt - 1, dpn / jnp.minimum(tn + 1, win_n).astype(F32), 0.0)
        acc = -dp
        for k in range(POOL_HALO):
            acc = acc + jnp.where(k < win, ext[pl.ds(k, POOL_T), :], 0.0)
        dz_ref[...] = acc.astype(dz_ref.dtype)

    cur = lambda b, i: (b * nt + i, 0)
    nxt = lambda b, i: (jnp.minimum((b * nt + i + 1) * hb, last_halo), 0)
    return pl.pallas_call(
        body, name=name, grid=(B, nt),
        in_specs=[pl.BlockSpec((POOL_T, MAIN_W), cur), pl.BlockSpec((POOL_HALO, MAIN_W), nxt),
                  pl.BlockSpec((POOL_T, MAIN_W), cur), pl.BlockSpec((POOL_HALO, MAIN_W), nxt),
                  pl.BlockSpec((MAIN_W, MAIN_W), lambda b, i: (0, 0)),
                  pl.BlockSpec((1, MAIN_W), lambda b, i: (0, 0)),
                  pl.BlockSpec(memory_space=pl.ANY)],
        out_specs=[pl.BlockSpec((POOL_T, MAIN_W), cur),
                   pl.BlockSpec((MAIN_W, MAIN_W), lambda b, i: (0, 0)),
                   pl.BlockSpec((1, MAIN_W), lambda b, i: (0, 0))],
        out_shape=[jax.ShapeDtypeStruct(dz_alias.shape, dz_alias.dtype),
                   jax.ShapeDtypeStruct((MAIN_W, MAIN_W), F32), jax.ShapeDtypeStruct((1, MAIN_W), F32)],
        scratch_shapes=[pltpu.VMEM((R, MAIN_W), F32), pltpu.VMEM((MAIN_W, MAIN_W), F32), pltpu.VMEM((8, MAIN_W), F32)],
        input_output_aliases={6: 0},
        compiler_params=_cp(("arbitrary", "arbitrary")),
    )(dy, dy, p, p, wbd, scale.reshape(1, MAIN_W), dz_alias)


def _head_masks(shape):
    lane = lax.broadcasted_iota(jnp.int32, shape, 1)
    return [(lane // HEAD_DIM) == h for h in range(shape[1] // HEAD_DIM)]


def _row_of(bcast, mask):
    return jnp.max(jnp.where(mask, bcast, -jnp.inf), axis=-1, keepdims=True)


MEM_TQ = 512


def _memattn_fwd(z, kv, y_alias, B, S, *, name):
    T = z.shape[0]
    nt = S // MEM_TQ

    def body(q_ref, k_ref, v_ref, _, y_ref, l_ref):
        q = q_ref[...]
        k = k_ref[...]
        v = v_ref[...]
        masks = _head_masks(q.shape)
        o = jnp.zeros(q.shape, F32)
        lse_b = jnp.zeros(q.shape, F32)
        for m in masks:
            qm = jnp.where(m, q, 0.0).astype(BF)
            s = lax.dot_general(qm, k, (((1,), (1,)), ((), ())), preferred_element_type=F32) * SCALE
            mx = jnp.max(s, axis=-1, keepdims=True)
            e = jnp.exp(s - mx)
            l = jnp.sum(e, axis=-1, keepdims=True)
            p = (e / l).astype(BF)
            o = o + jnp.where(m, jnp.dot(p, v, preferred_element_type=F32), 0.0)
            lse_b = lse_b + jnp.where(m, mx + jnp.log(l), 0.0)
        y_ref[...] = o.astype(y_ref.dtype)
        l_ref[...] = lse_b

    qblk = pl.BlockSpec((MEM_TQ, MEM_W), lambda b, i: (b * nt + i, 3))
    return pl.pallas_call(
        body, name=name, grid=(B, nt),
        in_specs=[qblk, pl.BlockSpec((N_MEM, MEM_W), lambda b, i: (b, 0)), pl.BlockSpec((N_MEM, MEM_W), lambda b, i: (b, 1)),
                  pl.BlockSpec(memory_space=pl.ANY)],
        out_specs=[qblk, pl.BlockSpec((MEM_TQ, MEM_W), lambda b, i: (b * nt + i, 0))],
        out_shape=[jax.ShapeDtypeStruct(y_alias.shape, y_alias.dtype), jax.ShapeDtypeStruct((T, MEM_W), F32)],
        input_output_aliases={3: 0},
        compiler_params=_cp(("parallel", "parallel")),
    )(z, kv, kv, y_alias)


def _memattn_bwd(dy, z, kv, lse, dz_alias, B, S, *, name):
    nt = S // MEM_TQ

    def body(do_ref, q_ref, k_ref, v_ref, l_ref, _, dz_ref, dk_ref, dv_ref, dk_acc, dv_acc):
        i = pl.program_id(1)
        do = do_ref[...]
        q = q_ref[...]
        k = k_ref[...]
        v = v_ref[...]
        lse_b = l_ref[...]
        masks = _head_masks(q.shape)
        dq = jnp.zeros(q.shape, F32)
        dk = jnp.zeros(k.shape, F32)
        dv = jnp.zeros(v.shape, F32)
        for m in masks:
            qm = jnp.where(m, q, 0.0).astype(BF)
            dom = jnp.where(m, do, 0.0).astype(BF)
            s = lax.dot_general(qm, k, (((1,), (1,)), ((), ())), preferred_element_type=F32) * SCALE
            p = jnp.exp(s - _row_of(lse_b, m))
            dp = lax.dot_general(dom, v, (((1,), (1,)), ((), ())), preferred_element_type=F32)
            delta = jnp.sum(p * dp, axis=-1, keepdims=True)
            ds = (p * (dp - delta) * SCALE).astype(BF)
            pb = p.astype(BF)
            dv = dv + jnp.where(m[:N_MEM], lax.dot_general(pb, dom, (((0,), (0,)), ((), ())), preferred_element_type=F32), 0.0)
            dk = dk + jnp.where(m[:N_MEM], lax.dot_general(ds, qm, (((0,), (0,)), ((), ())), preferred_element_type=F32), 0.0)
            dq = dq + jnp.where(m, jnp.dot(ds, k, preferred_element_type=F32), 0.0)
        dz_ref[...] = dq.astype(dz_ref.dtype)

        @pl.when(i == 0)
        def _():
            dk_acc[...] = dk
            dv_acc[...] = dv

        @pl.when(i > 0)
        def _():
            dk_acc[...] += dk
            dv_acc[...] += dv

        @pl.when(i == nt - 1)
        def _():
            dk_ref[...] = dk_acc[...]
            dv_ref[...] = dv_acc[...]

    qblk = pl.BlockSpec((MEM_TQ, MEM_W), lambda b, i: (b * nt + i, 3))
    kblk = pl.BlockSpec((N_MEM, MEM_W), lambda b, i: (b, 0))
    return pl.pallas_call(
        body, name=name, grid=(B, nt),
        in_specs=[qblk, qblk, kblk, pl.BlockSpec((N_MEM, MEM_W), lambda b, i: (b, 1)),
                  pl.BlockSpec((MEM_TQ, MEM_W), lambda b, i: (b * nt + i, 0)), pl.BlockSpec(memory_space=pl.ANY)],
        out_specs=[qblk, kblk, kblk],
        out_shape=[jax.ShapeDtypeStruct(dz_alias.shape, dz_alias.dtype),
                   jax.ShapeDtypeStruct((B * N_MEM, MEM_W), F32), jax.ShapeDtypeStruct((B * N_MEM, MEM_W), F32)],
        scratch_shapes=[pltpu.VMEM((N_MEM, MEM_W), F32), pltpu.VMEM((N_MEM, MEM_W), F32)],
        input_output_aliases={5: 0},
        compiler_params=_cp(("parallel", "arbitrary")),
    )(dy, z, kv, kv, lse, dz_alias)


def _dil_scores(qm, kp, kc, n):
    qi = lax.broadcasted_iota(jnp.int32, (STEPS, STEPS), 0)
    kj = lax.broadcasted_iota(jnp.int32, (STEPS, STEPS), 1)
    sc = lax.dot_general(qm, kc, (((1,), (1,)), ((), ())), preferred_element_type=F32) * SCALE
    sc = jnp.where(kj <= qi, sc, NEG)
    if kp is None:
        return None, sc
    sp = lax.dot_general(qm, kp, (((1,), (1,)), ((), ())), preferred_element_type=F32) * SCALE
    sp = jnp.where(jnp.logical_and(kj >= qi, n > 0), sp, NEG)
    return sp, sc


def _dil_specs(g, d, nb):
    chunk = STEPS * d
    cur = pl.BlockSpec((chunk, 128), lambda b, n, hf: (b * nb + n, g * 2 + hf))
    prev = pl.BlockSpec((chunk, 128), lambda b, n, hf: (b * nb + jnp.maximum(n - 1, 0), g * 2 + hf))
    return cur, prev


def _dil_rows(r, d):
    return pl.ds(r, STEPS, stride=d) if d > 1 else slice(None)


def _dil_loop(d, fn):
    if d <= 4:
        for r in range(d):
            fn(r)
    else:
        lax.fori_loop(0, d, lambda r, carry: (fn(r), carry)[1], 0)


def _dil_fwd_group(g, q, k, v, o_alias, l_alias, B, S, *, name):
    d = DIL[g]
    nb = S // (STEPS * d)
    has_prev = nb > 1

    def body(*refs):
        if has_prev:
            q_ref, kp_ref, kc_ref, vp_ref, vc_ref, _, __, o_ref, l_ref = refs
        else:
            q_ref, kc_ref, vc_ref, _, __, o_ref, l_ref = refs
        n = pl.program_id(1)

        def residue(r):
            rows = _dil_rows(r, d)
            q = q_ref[rows, :]
            kc, vc = kc_ref[rows, :].astype(BF), vc_ref[rows, :].astype(BF)
            kp = kp_ref[rows, :].astype(BF) if has_prev else None
            vp = vp_ref[rows, :].astype(BF) if has_prev else None
            o = jnp.zeros(q.shape, F32)
            lse_b = jnp.zeros(q.shape, F32)
            for m in _head_masks(q.shape):
                qm = jnp.where(m, q, 0.0).astype(BF)
                sp, sc = _dil_scores(qm, kp, kc, n)
                mx = jnp.max(sc, axis=-1, keepdims=True)
                if has_prev:
                    mx = jnp.maximum(mx, jnp.max(sp, axis=-1, keepdims=True))
                l = jnp.sum(jnp.exp(sc - mx), axis=-1, keepdims=True)
                if has_prev:
                    l = l + jnp.sum(jnp.exp(sp - mx), axis=-1, keepdims=True)
                lse = mx + jnp.log(l)
                oh = jnp.dot(jnp.exp(sc - lse).astype(BF), vc, preferred_element_type=F32)
                if has_prev:
                    oh = oh + jnp.dot(jnp.exp(sp - lse).astype(BF), vp, preferred_element_type=F32)
                o = o + jnp.where(m, oh, 0.0)
                lse_b = lse_b + jnp.where(m, lse, 0.0)
            o_ref[rows, :] = o
            l_ref[rows, :] = lse_b

        _dil_loop(d, residue)

    cur, prev = _dil_specs(g, d, nb)
    anyspec = pl.BlockSpec(memory_space=pl.ANY)
    if has_prev:
        in_specs, ops = [cur, prev, cur, prev, cur], [q, k, k, v, v]
    else:
        in_specs, ops = [cur, cur, cur], [q, k, v]
    n_in = len(ops)
    o, l = pl.pallas_call(
        body, name=name, grid=(B, nb, 2),
        in_specs=in_specs + [anyspec, anyspec],
        out_specs=[cur, cur],
        out_shape=[jax.ShapeDtypeStruct(q.shape, F32)] * 2,
        input_output_aliases={n_in: 0, n_in + 1: 1},
        compiler_params=_cp(("parallel", "parallel", "parallel")),
    )(*ops, o_alias, l_alias)
    return o, l


def _dil_bwd_group(g, q, k, v, do, cb, lse, aliases, B, S, *, name):
    d = DIL[g]
    nb = S // (STEPS * d)
    has_prev = nb > 1
    n_out = 5 if has_prev else 3

    def body(*refs):
        if has_prev:
            q_ref, kp_ref, kc_ref, vp_ref, vc_ref, do_ref, c_ref, l_ref = refs[:8]
            dq_ref, dkc_ref, dvc_ref, dkp_ref, dvp_ref = refs[8 + n_out:]
        else:
            q_ref, kc_ref, vc_ref, do_ref, c_ref, l_ref = refs[:6]
            dq_ref, dkc_ref, dvc_ref = refs[6 + n_out:]
        n = pl.program_id(1)
        tdot = lambda a, b: lax.dot_general(a, b, (((0,), (0,)), ((), ())), preferred_element_type=F32)
        ndot = lambda a, b: lax.dot_general(a, b, (((1,), (1,)), ((), ())), preferred_element_type=F32)

        def residue(r):
            rows = _dil_rows(r, d)
            q = q_ref[rows, :]
            kc, vc = kc_ref[rows, :].astype(BF), vc_ref[rows, :].astype(BF)
            kp = kp_ref[rows, :].astype(BF) if has_prev else None
            vp = vp_ref[rows, :].astype(BF) if has_prev else None
            do = do_ref[rows, :]
            cbv = c_ref[rows, :]
            lse_b = l_ref[rows, :]
            z = jnp.zeros(q.shape, F32)
            dq, dkc, dkp, dvc, dvp = z, z, z, z, z
            for m in _head_masks(q.shape):
                qm = jnp.where(m, q, 0.0).astype(BF)
                dom = jnp.where(m, do, 0.0).astype(BF)
                sp, sc = _dil_scores(qm, kp, kc, n)
                lse = _row_of(lse_b, m)
                c = _row_of(cbv, m)
                pc = jnp.exp(sc - lse)
                dsc = (pc * (ndot(dom, vc) + c) * SCALE).astype(BF)
                dqh = jnp.dot(dsc, kc, preferred_element_type=F32)
                dkc = dkc + jnp.where(m, tdot(dsc, qm), 0.0)
                dvc = dvc + jnp.where(m, tdot(pc.astype(BF), dom), 0.0)
                if has_prev:
                    pp = jnp.exp(sp - lse)
                    dsp = (pp * (ndot(dom, vp) + c) * SCALE).astype(BF)
                    dqh = dqh + jnp.dot(dsp, kp, preferred_element_type=F32)
                    dkp = dkp + jnp.where(m, tdot(dsp, qm), 0.0)
                    dvp = dvp + jnp.where(m, tdot(pp.astype(BF), dom), 0.0)
                dq = dq + jnp.where(m, dqh, 0.0)
            dq_ref[rows, :] = dq
            dkc_ref[rows, :] = dkc
            dvc_ref[rows, :] = dvc
            if has_prev:
                dkp_ref[rows, :] = dkp
                dvp_ref[rows, :] = dvp

        _dil_loop(d, residue)

    cur, prev = _dil_specs(g, d, nb)
    anyspec = pl.BlockSpec(memory_space=pl.ANY)
    dq_a, dkc_a, dkp_a, dvc_a, dvp_a = aliases
    if has_prev:
        in_specs, ops = [cur, prev, cur, prev, cur, cur, cur, cur], [q, k, k, v, v, do, cb, lse]
        al = [dq_a, dkc_a, dvc_a, dkp_a, dvp_a]
    else:
        in_specs, ops = [cur, cur, cur, cur, cur, cur], [q, k, v, do, cb, lse]
        al = [dq_a, dkc_a, dvc_a]
    n_in = len(ops)
    outs = pl.pallas_call(
        body, name=name, grid=(B, nb, 2),
        in_specs=in_specs + [anyspec] * n_out,
        out_specs=[cur] * n_out,
        out_shape=[jax.ShapeDtypeStruct(q.shape, F32)] * n_out,
        input_output_aliases={n_in + i: i for i in range(n_out)},
        compiler_params=_cp(("parallel", "parallel", "parallel")),
    )(*ops, *al)
    if has_prev:
        dq_a, dkc_a, dvc_a, dkp_a, dvp_a = outs
    else:
        dq_a, dkc_a, dvc_a = outs
    return dq_a, dkc_a, dkp_a, dvc_a, dvp_a


def _group_softmax(lse):
    l0, l1, l2 = lse[:, 0:256], lse[:, 256:512], lse[:, 512:768]
    mx = jnp.maximum(jnp.maximum(l0, l1), l2)
    e0, e1, e2 = jnp.exp(l0 - mx), jnp.exp(l1 - mx), jnp.exp(l2 - mx)
    tot = e0 + e1 + e2
    return e0 / tot, e1 / tot, e2 / tot


def _dil_combine_fwd(o, lse, y_alias, *, name, tm=512):
    T = o.shape[0]

    def body(o_ref, l_ref, _, y_ref):
        a = jnp.concatenate(_group_softmax(l_ref[...]), axis=1)
        y_ref[...] = (o_ref[...] * a).astype(y_ref.dtype)

    blk = pl.BlockSpec((tm, MAIN_W), lambda i: (i, 0))
    return pl.pallas_call(
        body, name=name, grid=(T // tm,), in_specs=[blk, blk, pl.BlockSpec(memory_space=pl.ANY)], out_specs=blk,
        out_shape=jax.ShapeDtypeStruct(y_alias.shape, y_alias.dtype), input_output_aliases={2: 0},
        compiler_params=_cp(("parallel",)),
    )(o, lse, y_alias)


def _dil_combine_bwd(dy, o, lse, *, name, tm=256):
    T = o.shape[0]
    lane_r = lax.broadcasted_iota(jnp.int32, (256, 256), 0) // HEAD_DIM
    lane_c = lax.broadcasted_iota(jnp.int32, (256, 256), 1) // HEAD_DIM
    ones_bd = (lane_r == lane_c).astype(BF)

    def body(dy_ref, o_ref, l_ref, e_ref, do_ref, c_ref):
        dyv = dy_ref[...]
        alphas = _group_softmax(l_ref[...])
        prod = dyv * o_ref[...]
        e = e_ref[...]
        tot = jnp.zeros((tm, 256), F32)
        for gi in range(3):
            x = prod[:, gi * 256:(gi + 1) * 256]
            hi = x.astype(BF)
            lo = (x - hi.astype(F32)).astype(BF)
            dalpha = jnp.dot(hi, e, preferred_element_type=F32) + jnp.dot(lo, e, preferred_element_type=F32)
            tot = tot + alphas[gi] * dalpha
        a = jnp.concatenate(alphas, axis=1)
        do_ref[...] = (dyv * a).astype(do_ref.dtype)
        c_ref[...] = jnp.concatenate([-al * tot for al in alphas], axis=1)

    blk = pl.BlockSpec((tm, MAIN_W), lambda i: (i, 0))
    return pl.pallas_call(
        body, name=name, grid=(T // tm,),
        in_specs=[blk, blk, blk, pl.BlockSpec((256, 256), lambda i: (0, 0))], out_specs=[blk, blk],
        out_shape=[jax.ShapeDtypeStruct((T, MAIN_W), F32), jax.ShapeDtypeStruct((T, MAIN_W), F32)],
        compiler_params=_cp(("parallel",)),
    )(dy, o, lse, ones_bd)


def _kv_grad(parts, cos, sin, B, S, *, name):
    T = B * S
    tb = S // STEPS
    n_l = len(parts)

    def shifted(g):
        def f(b, t):
            return (b * tb + jnp.minimum(t + DIL[g], tb - 1), g)
        return f

    with_prev = [g for g in range(3) if DIL[g] < tb]
    n_p = len(with_prev)
    per_l = 2 + 2 * n_p

    def body(*refs):
        c_ref, s_ref = refs[0], refs[1]
        ins = refs[2:2 + n_l * per_l]
        dk_ref, dv_ref = refs[2 + n_l * per_l:]
        t = pl.program_id(1)
        dk = jnp.zeros((STEPS, MAIN_W), F32)
        dv = jnp.zeros((STEPS, MAIN_W), F32)
        zero = jnp.zeros((STEPS, 256), F32)
        for li in range(n_l):
            base = li * per_l
            dk = dk + ins[base][...]
            dv = dv + ins[base + 1][...]
            kparts, vparts = [zero] * 3, [zero] * 3
            for pi, g in enumerate(with_prev):
                ok = t + DIL[g] < tb
                kparts[g] = jnp.where(ok, ins[base + 2 + pi][...], 0.0)
                vparts[g] = jnp.where(ok, ins[base + 2 + n_p + pi][...], 0.0)
            dk = dk + jnp.concatenate(kparts, axis=1)
            dv = dv + jnp.concatenate(vparts, axis=1)
        dk_ref[...] = _rot(dk, c_ref[...], s_ref[...], -1.0).astype(dk_ref.dtype)
        dv_ref[...] = dv.astype(dv_ref.dtype)

    full = pl.BlockSpec((STEPS, MAIN_W), lambda b, t: (b * tb + t, 0))
    tab = pl.BlockSpec((STEPS, 128), lambda b, t: (b * tb + t, 0))
    in_specs, ops = [tab, tab], [cos, sin]
    for (kc, kp, vc, vp) in parts:
        in_specs += [full, full] + [pl.BlockSpec((STEPS, 256), shifted(g)) for g in with_prev] * 2
        ops += [kc, vc] + [kp] * n_p + [vp] * n_p
    return pl.pallas_call(
        body, name=name, grid=(B, tb), in_specs=in_specs, out_specs=[full, full],
        out_shape=[jax.ShapeDtypeStruct((T, MAIN_W), BF)] * 2,
        compiler_params=_cp(("parallel", "parallel")),
    )(*ops)


def _loss(y, target, *, name, tm=512):
    T, Dm = y.shape
    nt = T // tm

    def body(y_ref, t_ref, l_ref, d_ref, acc):
        i = pl.program_id(0)
        err = y_ref[...] - t_ref[...]
        d_ref[...] = err / Dm
        part = jnp.sum(jnp.mean(err * err, axis=-1, keepdims=True).reshape(tm // 8, 8, 1), axis=0)

        @pl.when(i == 0)
        def _():
            acc[...] = part

        @pl.when(i > 0)
        def _():
            acc[...] += part

        @pl.when(i == nt - 1)
        def _():
            l_ref[...] = 0.5 * jnp.sum(acc[...], axis=0, keepdims=True)

    row = pl.BlockSpec((tm, Dm), lambda i: (i, 0))
    return pl.pallas_call(
        body, name=name, grid=(nt,), in_specs=[row, row],
        out_specs=[pl.BlockSpec((1, 1), lambda i: (0, 0)), row],
        out_shape=[jax.ShapeDtypeStruct((1, 1), F32), jax.ShapeDtypeStruct((T, Dm), F32)],
        scratch_shapes=[pltpu.VMEM((8, 1), F32)],
        compiler_params=_cp(("arbitrary",)),
    )(y, target)


def _adamw(w, g, m, v, *, name):
    shape = w.shape
    cols = shape[-1]
    rows = w.size // cols
    tm = rows
    for cand in (512, 352, 256, 128):
        if rows > cand and rows % cand == 0 and cand * cols * 4 <= (1 << 20):
            tm = cand
            break

    def body(w_ref, g_ref, m_ref, v_ref, d_ref, mo_ref, vo_ref):
        gv = g_ref[...]
        mn = ADAM_B1 * m_ref[...] + (1.0 - ADAM_B1) * gv
        vn = ADAM_B2 * v_ref[...] + (1.0 - ADAM_B2) * (gv * gv)
        m_hat = mn / (1.0 - ADAM_B1 ** ADAM_STEP)
        v_hat = vn / (1.0 - ADAM_B2 ** ADAM_STEP)
        d_ref[...] = -ADAM_LR * (m_hat / (jnp.sqrt(v_hat) + ADAM_EPS) + ADAM_WD * w_ref[...])
        mo_ref[...] = mn
        vo_ref[...] = vn

    blk = pl.BlockSpec((tm, cols), lambda i: (i, 0))
    outs = pl.pallas_call(
        body, name=name, grid=(rows // tm,), in_specs=[blk] * 4, out_specs=[blk] * 3,
        out_shape=[jax.ShapeDtypeStruct((rows, cols), F32)] * 3,
        compiler_params=_cp(("parallel",)),
    )(*[t.reshape(rows, cols) for t in (w, g, m, v)])
    return tuple(t.reshape(shape) for t in outs)


BIG = {
    'w_in': ((DEPTH, D_MODEL, D_MODEL), 'row'),
    'w_mem_kv': ((DEPTH, D_MODEL, 2 * MEM_W), 'row'),
    'w_out': ((DEPTH, D_MODEL, D_MODEL), 'row'),
    'w_kv': ((1, D_MODEL, 2 * MAIN_W), 'col'),
    'w_gate_up': ((DEPTH, D_MODEL, 2 * D_FF), 'col'),
    'w_down': ((DEPTH, D_FF, D_MODEL), 'row'),
}
BIG_NAMES = tuple(BIG)
N_CHIPS = 4
HBM_ANY = pl.BlockSpec(memory_space=pl.ANY)


def _geom(name):
    (L, R, C), kind = BIG[name]
    if kind == 'row':
        return L, R, C, kind, R // N_CHIPS, C, R // (2 * N_CHIPS)
    return L, R, C, kind, R, C // N_CHIPS, R // 2


def _shard_shape(name):
    L, R, C, kind, rs, cs, rh = _geom(name)
    return (L, rs, cs)


def _half_shape(name):
    L, R, C, kind, rs, cs, rh = _geom(name)
    return (L, rh, cs)


def _full_win(ref, name, s, h):
    L, R, C, kind, rs, cs, rh = _geom(name)
    if kind == 'row':
        rows = pl.ds(s * rs, rs) if h is None else pl.ds(s * rs + h * rh, rh)
        return ref.at[:, rows, :]
    rows = slice(None) if h is None else pl.ds(h * rh, rh)
    return ref.at[:, rows, pl.ds(s * cs, cs)]


def _shard_half(ref, name, h):
    L, R, C, kind, rs, cs, rh = _geom(name)
    return ref.at[:, pl.ds(h * rh, rh), :]


def _halves_win(ref, name, s):
    L, R, C, kind, rs, cs, rh = _geom(name)
    if kind == 'row':
        return ref.at[:, pl.ds(s * rh, rh), :]
    return ref.at[:, :, pl.ds(s * cs, cs)]


def _halves_shape(name):
    L, R, C, kind, rs, cs, rh = _geom(name)
    return (L, N_CHIPS * rh, cs) if kind == 'row' else (L, rh, C)


def _place():
    x, y, c = lax.axis_index("x"), lax.axis_index("y"), lax.axis_index("c")
    chips = [(1 - x, y), (x, 1 - y), (1 - x, 1 - y)]
    return x, y, c, chips


SMALL_ROWS = 24


def _all_gather(shards, small):
    names = BIG_NAMES
    nw = len(names)

    def body(*refs):
        src = dict(zip(names, refs[:nw]))
        small_ref = refs[nw]
        dst = dict(zip(names, refs[nw + 1:2 * nw + 1]))
        small_out = refs[2 * nw + 1]
        send_sems, recv_sems, local_sems = refs[2 * nw + 2:]
        x, y, c, chips = _place()
        s = 2 * x + y
        sib = (x, y, 1 - c)

        def remote(k, src_ref, dst_ref, to):
            return pltpu.make_async_remote_copy(src_ref=src_ref, dst_ref=dst_ref, send_sem=send_sems.at[k],
                                                recv_sem=recv_sems.at[k], device_id=to, device_id_type=MESH)

        local = []
        for wi, nm in enumerate(names):
            local.append(pltpu.make_async_copy(src[nm], _full_win(dst[nm], nm, s, None), local_sems.at[wi]))
        local.append(pltpu.make_async_copy(small_ref, small_out.at[s], local_sems.at[nw]))
        for cp in local:
            cp.start()
        sends = []
        for j, (px, py) in enumerate(chips):
            for wi, nm in enumerate(names):
                sends.append(remote(wi * 6 + j, _shard_half(src[nm], nm, c), _full_win(dst[nm], nm, s, c), (px, py, c)))
            sends.append(remote(nw * 6 + j, small_ref, small_out.at[s], (px, py, c)))
        for cp in sends:
            cp.start()
        for j, (px, py) in enumerate(chips):
            sp = 2 * px + py
            for wi, nm in enumerate(names):
                w = _full_win(dst[nm], nm, sp, c)
                remote(wi * 6 + j, w, w, sib).wait_recv()
                fwd = remote(wi * 6 + 3 + j, w, w, sib)
                fwd.start()
                sends.append(fwd)
            remote(nw * 6 + j, small_ref, small_out.at[sp], sib).wait_recv()
        for j, (px, py) in enumerate(chips):
            sp = 2 * px + py
            for wi, nm in enumerate(names):
                w = _full_win(dst[nm], nm, sp, 1 - c)
                remote(wi * 6 + 3 + j, w, w, sib).wait_recv()
        for cp in sends:
            cp.wait_send()
        for cp in local:
            cp.wait()

    n_sem = nw * 6 + 3
    outs = pl.pallas_call(
        body, name="all_gather_weights",
        in_specs=[HBM_ANY] * (nw + 1), out_specs=[HBM_ANY] * (nw + 1),
        out_shape=[jax.ShapeDtypeStruct(BIG[nm][0], BF) for nm in names]
        + [jax.ShapeDtypeStruct((N_CHIPS, SMALL_ROWS, 256), F32)],
        scratch_shapes=[pltpu.SemaphoreType.DMA((n_sem,)), pltpu.SemaphoreType.DMA((n_sem,)),
                        pltpu.SemaphoreType.DMA((nw + 1,))],
    )(*[shards[nm] for nm in names], small)
    return dict(zip(names, outs[:nw])), outs[nw]


SEM_SPEC = pl.BlockSpec(memory_space=pltpu.SEMAPHORE)
HBM_SPEC = pl.BlockSpec(memory_space=pltpu.HBM)
DATAFLOW = pltpu.SideEffectType.DATAFLOW_SIDE_EFFECTING


def _in_hbm(a):
    return pltpu.with_memory_space_constraint(a, pltpu.HBM)


def _remote(src, dst, send_sems, recv_sems, k, to):
    return pltpu.make_async_remote_copy(src_ref=src, dst_ref=dst, send_sem=send_sems.at[k], recv_sem=recv_sems.at[k],
                                        device_id=to, device_id_type=MESH)


def _split_start(name, bufs, n_copies, sends):
    nb = len(bufs)

    def body(*refs):
        in_refs = refs[:nb]
        send_sems, recv_sems = refs[nb], refs[nb + 1]
        token = refs[-1]
        for k, (src, dst, to) in enumerate(sends(in_refs)):
            _remote(src, dst, send_sems, recv_sems, k, to).start()
        token[...] = jnp.zeros_like(token)

    outs = pl.pallas_call(
        body, name=name,
        out_shape=(pltpu.SemaphoreType.DMA((n_copies,)), pltpu.SemaphoreType.DMA((n_copies,)),
                   *[pltpu.HBM(b.shape, b.dtype) for b in bufs], jax.ShapeDtypeStruct((8, 128), F32)),
        in_specs=[HBM_SPEC] * nb,
        out_specs=(SEM_SPEC, SEM_SPEC, *[HBM_SPEC] * nb, pl.BlockSpec(memory_space=pltpu.VMEM)),
        input_output_aliases={i: 2 + i for i in range(nb)},
        compiler_params=pltpu.CompilerParams(has_side_effects=DATAFLOW),
    )(*[_in_hbm(b) for b in bufs])
    return outs[0], outs[1], list(outs[2:2 + nb]), outs[-1]


def _split_wait(name, send_sems, recv_sems, bufs, after, sends, arrivals):
    nb = len(bufs)

    def body(*refs):
        in_refs = refs[:nb]
        s_sems, r_sems = refs[nb], refs[nb + 1]
        me = (lax.axis_index("x"), lax.axis_index("y"), lax.axis_index("c"))
        for k, (src, dst, to) in enumerate(sends(in_refs)):
            _remote(src, dst, s_sems, r_sems, k, to).wait_send()
        for k, win in enumerate(arrivals(in_refs)):
            _remote(win, win, s_sems, r_sems, k, me).wait_recv()

    outs = pl.pallas_call(
        body, name=name,
        out_shape=[pltpu.HBM(b.shape, b.dtype) for b in bufs],
        in_specs=[HBM_SPEC] * nb + [SEM_SPEC, SEM_SPEC, HBM_ANY],
        out_specs=[HBM_SPEC] * nb,
        input_output_aliases={i: i for i in range(nb)},
        compiler_params=pltpu.CompilerParams(has_side_effects=DATAFLOW),
    )(*bufs, send_sems, recv_sems, after)
    return list(outs)


LAYER_W = ('w_in', 'w_mem_kv', 'w_out', 'w_gate_up', 'w_down')


def _place_own(l, names, shards, small, sc):
    nw = len(names)
    has_small = small is not None
    n_ops = nw + (1 if has_small else 0)

    def body(sc_ref, *refs):
        for src, dst in zip(refs[:n_ops], refs[n_ops:]):
            dst[...] = src[...]

    in_specs, out_specs, out_shape, ops = [], [], [], list(shards)
    for nm in names:
        L, R, C, kind, rs, cs, rh = _geom(nm)
        in_specs.append(pl.BlockSpec((1, rs, cs), lambda i, sc_ref: (0, 0, 0)))
        if kind == 'row':
            out_specs.append(pl.BlockSpec((1, rs, cs), lambda i, sc_ref: (0, sc_ref[0], 0)))
        else:
            out_specs.append(pl.BlockSpec((1, rs, cs), lambda i, sc_ref: (0, 0, sc_ref[0])))
        out_shape.append(jax.ShapeDtypeStruct((1, R, C), BF))
    if has_small:
        in_specs.append(pl.BlockSpec((SMALL_ROWS, 256), lambda i, sc_ref: (0, 0)))
        out_specs.append(pl.BlockSpec((None, SMALL_ROWS, 256), lambda i, sc_ref: (sc_ref[0], 0, 0)))
        out_shape.append(jax.ShapeDtypeStruct((N_CHIPS, SMALL_ROWS, 256), F32))
        ops.append(small)
    return pl.pallas_call(
        body, name=f"l{l}_place_own_shard",
        grid_spec=pltpu.PrefetchScalarGridSpec(num_scalar_prefetch=1, grid=(1,), in_specs=in_specs, out_specs=out_specs),
        out_shape=out_shape,
        compiler_params=_cp(("arbitrary",)),
    )(sc, *ops)


def _gather_start(l, names, shards, small, sc):
    nw = len(names)
    has_small = small is not None
    fulls = _place_own(l, names, shards, small, sc)
    bufs = list(shards) + ([small] if has_small else []) + list(fulls)
    n_src = nw + (1 if has_small else 0)

    def sends(refs):
        x, y, c, chips = _place()
        s = 2 * x + y
        out = []
        for (px, py) in chips:
            for wi, nm in enumerate(names):
                out.append((_shard_half(refs[wi], nm, c), _full_win(refs[n_src + wi], nm, s, c), (px, py, c)))
            if has_small:
                out.append((refs[nw], refs[n_src + nw].at[s], (px, py, c)))
        return out

    def arrivals(refs):
        x, y, c, chips = _place()
        out = []
        for (px, py) in chips:
            sp = 2 * px + py
            for wi, nm in enumerate(names):
                out.append(_full_win(refs[n_src + wi], nm, sp, c))
            if has_small:
                out.append(refs[n_src + nw].at[sp])
        return out

    n_copies = 3 * n_src
    send_sems, recv_sems, bufs, token = _split_start(f"l{l}_gather_ici_start", bufs, n_copies, sends)
    return dict(l=l, names=names, has_small=has_small, sems=(send_sems, recv_sems), bufs=bufs, sends=sends,
                arrivals=arrivals, token=token)


def _gather_forward(st, after):
    l, names = st['l'], st['names']
    nw = len(names)
    n_src = nw + (1 if st['has_small'] else 0)
    bufs = _split_wait(f"l{l}_gather_ici_wait", *st['sems'], st['bufs'], after, st['sends'], st['arrivals'])
    fulls = bufs[n_src:n_src + nw]
    small_all = bufs[n_src + nw] if st['has_small'] else None

    def sends(refs):
        x, y, c, chips = _place()
        out = []
        for (px, py) in chips:
            sp = 2 * px + py
            for wi, nm in enumerate(names):
                w = _full_win(refs[wi], nm, sp, c)
                out.append((w, w, (x, y, 1 - c)))
        return out

    def arrivals(refs):
        x, y, c, chips = _place()
        out = []
        for (px, py) in chips:
            sp = 2 * px + py
            for wi, nm in enumerate(names):
                out.append(_full_win(refs[wi], nm, sp, 1 - c))
        return out

    send_sems, recv_sems, fulls, token = _split_start(f"l{l}_gather_d2d_start", fulls, 3 * nw, sends)
    return dict(l=l, names=names, sems=(send_sems, recv_sems), bufs=fulls, sends=sends, arrivals=arrivals,
                small_all=small_all, token=token)


def _gather_finish(st, after):
    fulls = _split_wait(f"l{st['l']}_gather_d2d_wait", *st['sems'], st['bufs'], after, st['sends'], st['arrivals'])
    return dict(zip(st['names'], fulls)), st['small_all']


def _exchange_halves(grads):
    names = BIG_NAMES
    nw = len(names)

    def body(*refs):
        src = dict(zip(names, refs[:nw]))
        dst = dict(zip(names, refs[nw:2 * nw]))
        send_sems, recv_sems = refs[2 * nw:]
        x, y, c, _ = _place()
        sib = (x, y, 1 - c)
        sends, recvs = [], []
        k = 0
        for nm in names:
            kind = BIG[nm][1]
            for sp in range(N_CHIPS if kind == 'row' else 1):
                if kind == 'row':
                    out_w = _full_win(src[nm], nm, sp, 1 - c)
                    in_w = _halves_win(dst[nm], nm, sp)
                else:
                    L, R, C, _, rs, cs, rh = _geom(nm)
                    out_w = src[nm].at[:, pl.ds((1 - c) * rh, rh), :]
                    in_w = dst[nm]
                cp = pltpu.make_async_remote_copy(src_ref=out_w, dst_ref=in_w, send_sem=send_sems.at[k],
                                                  recv_sem=recv_sems.at[k], device_id=sib, device_id_type=MESH)
                cp.start()
                sends.append(cp)
                k += 1
        for cp in sends:
            cp.wait_recv()
        for cp in sends:
            cp.wait_send()

    n_sem = sum(N_CHIPS if BIG[nm][1] == 'row' else 1 for nm in names)
    outs = pl.pallas_call(
        body, name="exchange_grad_halves",
        in_specs=[HBM_ANY] * nw, out_specs=[HBM_ANY] * nw,
        out_shape=[jax.ShapeDtypeStruct(_halves_shape(nm), F32) for nm in names],
        scratch_shapes=[pltpu.SemaphoreType.DMA((n_sem,)), pltpu.SemaphoreType.DMA((n_sem,))],
    )(*[grads[nm] for nm in names])
    return dict(zip(names, outs))


def _add_halves(name, g, r, sc):
    L, R, C, kind, rs, cs, rh = _geom(name)
    tr = rh if kind == 'row' else 256
    nr = rh // tr

    def body(sc_ref, g_ref, r_ref, hb_ref, own_ref):
        sp = pl.program_id(2)
        tot = g_ref[...] + r_ref[...]
        hb_ref[...] = tot.astype(hb_ref.dtype)

        @pl.when(sp == sc_ref[0])
        def _():
            own_ref[...] = tot

    if kind == 'row':
        g_map = lambda l, ri, sp, sc_ref: (l, sp * 2 + sc_ref[1], 0)
        h_map = lambda l, ri, sp, sc_ref: (l, sp, 0)
    else:
        g_map = lambda l, ri, sp, sc_ref: (l, sc_ref[1] * nr + ri, sp)
        h_map = lambda l, ri, sp, sc_ref: (l, ri, sp)
    own_map = lambda l, ri, sp, sc_ref: (l, ri, 0)
    blk = (None, tr, cs)
    return pl.pallas_call(
        body, name="add_halves_" + name,
        grid_spec=pltpu.PrefetchScalarGridSpec(
            num_scalar_prefetch=1, grid=(L, nr, N_CHIPS),
            in_specs=[pl.BlockSpec(blk, g_map), pl.BlockSpec(blk, h_map)],
            out_specs=[pl.BlockSpec(blk, h_map), pl.BlockSpec(blk, own_map)]),
        out_shape=[jax.ShapeDtypeStruct(_halves_shape(name), BF), jax.ShapeDtypeStruct(_half_shape(name), F32)],
        compiler_params=_cp(("parallel", "parallel", "arbitrary")),
    )(sc, g, r)


def _scatter_to_chips(halves):
    names = BIG_NAMES
    nw = len(names)

    def body(*refs):
        src = dict(zip(names, refs[:nw]))
        dst = dict(zip(names, refs[nw:2 * nw]))
        send_sems, recv_sems = refs[2 * nw:]
        x, y, c, chips = _place()
        sends = []
        for j, (px, py) in enumerate(chips):
            sp = 2 * px + py
            for wi, nm in enumerate(names):
                cp = pltpu.make_async_remote_copy(
                    src_ref=_halves_win(src[nm], nm, sp), dst_ref=dst[nm].at[j], send_sem=send_sems.at[wi * 3 + j],
                    recv_sem=recv_sems.at[wi * 3 + j], device_id=(px, py, c), device_id_type=MESH)
                cp.start()
                sends.append(cp)
        for cp in sends:
            cp.wait_recv()
        for cp in sends:
            cp.wait_send()

    outs = pl.pallas_call(
        body, name="scatter_grads_to_chips",
        in_specs=[HBM_ANY] * nw, out_specs=[HBM_ANY] * nw,
        out_shape=[jax.ShapeDtypeStruct((3,) + _half_shape(nm), BF) for nm in names],
        scratch_shapes=[pltpu.SemaphoreType.DMA((nw * 3,)), pltpu.SemaphoreType.DMA((nw * 3,))],
    )(*[halves[nm] for nm in names])
    return dict(zip(names, outs))


def _sum_pieces(name, own, pieces, sc):
    L, R, C, kind, rs, cs, rh = _geom(name)
    tr = rh if kind == 'row' else 256
    nr = rh // tr

    def body(sc_ref, o_ref, p_ref, out_ref):
        out_ref[...] = o_ref[...] + p_ref[0].astype(F32) + p_ref[1].astype(F32) + p_ref[2].astype(F32)

    blk = (None, tr, cs)
    return pl.pallas_call(
        body, name="sum_pieces_" + name,
        grid_spec=pltpu.PrefetchScalarGridSpec(
            num_scalar_prefetch=1, grid=(L, nr),
            in_specs=[pl.BlockSpec(blk, lambda l, ri, sc_ref: (l, ri, 0)),
                      pl.BlockSpec((3, None, tr, cs), lambda l, ri, sc_ref: (0, l, ri, 0))],
            out_specs=pl.BlockSpec(blk, lambda l, ri, sc_ref: (l, sc_ref[1] * nr + ri, 0))),
        out_shape=jax.ShapeDtypeStruct(_shard_shape(name), F32),
        compiler_params=_cp(("parallel", "parallel")),
    )(sc, own, pieces)


def _share_with_sibling(gshards):
    names = BIG_NAMES
    nw = len(names)

    def body(*refs):
        bufs = dict(zip(names, refs[nw:2 * nw]))
        send_sems, recv_sems = refs[2 * nw:]
        x, y, c, _ = _place()
        sib = (x, y, 1 - c)
        sends = []
        for wi, nm in enumerate(names):
            w = _shard_half(bufs[nm], nm, c)
            cp = pltpu.make_async_remote_copy(src_ref=w, dst_ref=w, send_sem=send_sems.at[wi], recv_sem=recv_sems.at[wi],
                                              device_id=sib, device_id_type=MESH)
            cp.start()
            sends.append(cp)
        for wi, nm in enumerate(names):
            w = _shard_half(bufs[nm], nm, 1 - c)
            pltpu.make_async_remote_copy(src_ref=w, dst_ref=w, send_sem=send_sems.at[wi], recv_sem=recv_sems.at[wi],
                                         device_id=sib, device_id_type=MESH).wait_recv()
        for cp in sends:
            cp.wait_send()

    outs = pl.pallas_call(
        body, name="share_grad_shards",
        in_specs=[HBM_ANY] * nw, out_specs=[HBM_ANY] * nw,
        out_shape=[jax.ShapeDtypeStruct(_shard_shape(nm), F32) for nm in names],
        input_output_aliases={i: i for i in range(nw)},
        scratch_shapes=[pltpu.SemaphoreType.DMA((nw,)), pltpu.SemaphoreType.DMA((nw,))],
    )(*[gshards[nm] for nm in names])
    return dict(zip(names, outs))


def _all_gather_small(v):
    rows = v.shape[0]

    def body(v_ref, out_ref, send_sems, recv_sems, local_sem):
        x, y, c, _ = _place()
        me = 4 * x + 2 * y + c
        mine = pltpu.make_async_copy(v_ref, out_ref.at[me], local_sem)
        mine.start()
        sends = []
        flips = [(fx, fy, fc) for fx in (0, 1) for fy in (0, 1) for fc in (0, 1)][1:]
        for k, (fx, fy, fc) in enumerate(flips):
            px, py, pc = (1 - x if fx else x), (1 - y if fy else y), (1 - c if fc else c)
            cp = pltpu.make_async_remote_copy(src_ref=v_ref, dst_ref=out_ref.at[me], send_sem=send_sems.at[k],
                                              recv_sem=recv_sems.at[k], device_id=(px, py, pc), device_id_type=MESH)
            cp.start()
            sends.append((cp, 4 * px + 2 * py + pc))
        for k, (cp, peer) in enumerate(sends):
            pltpu.make_async_remote_copy(src_ref=v_ref, dst_ref=out_ref.at[peer], send_sem=send_sems.at[k],
                                         recv_sem=recv_sems.at[k], device_id=(x, y, c), device_id_type=MESH).wait_recv()
        for cp, _ in sends:
            cp.wait_send()
        mine.wait()

    return pl.pallas_call(
        body, name="all_gather_small_grads",
        in_specs=[HBM_ANY], out_specs=HBM_ANY,
        out_shape=jax.ShapeDtypeStruct((8, rows, 128), F32),
        scratch_shapes=[pltpu.SemaphoreType.DMA((7,)), pltpu.SemaphoreType.DMA((7,)), pltpu.SemaphoreType.DMA],
    )(v)


def _sum8(v8, *, name, tr=336):
    rows = v8.shape[1]
    tr = min(tr, rows)
    assert rows % tr == 0

    def body(v_ref, o_ref):
        tot = v_ref[0]
        for d in range(1, 8):
            tot = tot + v_ref[d]
        o_ref[...] = tot

    return pl.pallas_call(
        body, name=name, grid=(rows // tr,),
        in_specs=[pl.BlockSpec((8, tr, 128), lambda i: (0, i, 0))], out_specs=pl.BlockSpec((tr, 128), lambda i: (i, 0)),
        out_shape=jax.ShapeDtypeStruct((rows, 128), F32),
        compiler_params=_cp(("parallel",)),
    )(v8)


def _block_diag(w_pool_l):
    wbd = jnp.zeros((MAIN_W, MAIN_W), F32)
    for gi in range(len(POOL_WINDOWS)):
        wbd = lax.dynamic_update_slice(wbd, w_pool_l[gi], (gi * POOL_GROUP, gi * POOL_GROUP))
    return wbd.astype(BF)


def _unpack_small(small_all):
    ng = small_all[:, :16, :].reshape(N_CHIPS, DEPTH, 4, 256).transpose(1, 2, 0, 3).reshape(DEPTH, 4, D_MODEL)
    ps = small_all[:, 16:18, :POOL_GROUP].transpose(1, 0, 2).reshape(N_A, MAIN_W)
    return ng, ps


def _local_step(x, mem, positions, weights_of, mem_norm, w_pool, kv_norm, target):
    B, S, _ = x.shape
    T = B * S
    xc = x.reshape(T, D_MODEL)
    memf = mem.reshape(B * N_MEM, D_MODEL)
    tgt = target.reshape(T, D_MODEL)
    cos, sin = _rope_tables(positions.reshape(T, 1), name="rope_tables")
    wbd = [_block_diag(w_pool[l]) for l in range(N_A)]
    nbo = D_FF // 256
    fw = []
    rk = rv = None
    kv_saved = None
    w0, small_all = weights_of(0, None, None)
    norm_gains, pool_scale = _unpack_small(small_all)
    wts = [w0]
    y1 = y2 = None

    for l in range(DEPTH):
        t = f"l{l}_"
        if l > 0:
            wts.append(weights_of(l, y1, y2)[0])
        sv = {'x_in': xc}
        h0, sv['r0'] = _norm_fwd(xc, norm_gains[l, 0], name=t + "norm0", out_dtype=BF)
        z, = _mm(h0, wts[l]['w_in'], b_layer=0, name=t + "mm_in")
        memn, sv['rm'] = _norm_fwd(memf, mem_norm[l], name=t + "norm_mem", out_dtype=BF, tm=256)
        kvm, = _mm(memn, wts[l]['w_mem_kv'], b_layer=0, name=t + "mm_memkv", out_dtypes=(BF,))
        if l < N_A:
            ycat, sv['p'] = _pool_fwd(z, wbd[l], pool_scale[l], B, S, name=t + "pool_fwd")
        else:
            rq = _rope_apply(z, cos, sin, name=t + "rope_q", out_dtype=F32)
            o = lax.empty((T, MAIN_W), F32)
            lse = lax.empty((T, MAIN_W), F32)
            for g in range(3):
                o, lse = _dil_fwd_group(g, rq, rk, rv, o, lse, B, S, name=t + f"dil_fwd{g}")
            ycat = _dil_combine_fwd(o, lse, lax.empty((T, D_MODEL), BF), name=t + "dil_combine")
            sv.update(rq=rq, o=o, lse=lse)
        ycat, sv['lse_m'] = _memattn_fwd(z, kvm, ycat, B, S, name=t + "memattn_fwd")
        y1, = _mm(ycat, wts[l]['w_out'], b_layer=0, name=t + "mm_out")
        x1, sv['r1'] = _norm_fwd(y1, norm_gains[l, 1], name=t + "norm1", res=xc)
        h2, sv['r2'] = _norm_fwd(x1, norm_gains[l, 2], name=t + "norm2", out_dtype=BF)
        gg, uu, aa = _mm(h2, wts[l]['w_gate_up'], b_layer=0, b_offsets=(0, nbo), out_n=D_FF, tn=256, name=t + "mm_gate_up",
                         epilogue=_swiglu_fwd_epilogue, out_dtypes=(BF, BF, BF))
        y2, = _mm(aa, wts[l]['w_down'], b_layer=0, tk=D_FF, name=t + "mm_down")
        x2, sv['r3'] = _norm_fwd(y2, norm_gains[l, 3], name=t + "norm3", res=x1)
        sv.update(h0=h0, z=z, memn=memn, kvm=kvm, ycat=ycat, y1=y1, x1=x1, h2=h2, gg=gg, uu=uu, aa=aa, y2=y2)
        fw.append(sv)
        xc = x2
        if l == N_A - 1:
            kvn, rkv = _norm_fwd(xc, kv_norm, name="norm_kv", out_dtype=BF)
            kv, = _mm(kvn, wts[N_A - 1]['w_kv'], b_layer=0, name="mm_kv")
            rk, rv = _rope_apply(kv, cos, sin, name="rope_k", passthrough=True, out_dtype=F32)
            kv_saved = (xc, kvn, rkv)

    loss, dx = _loss(xc, tgt, name="loss")

    gbig = {nm: lax.empty(BIG[nm][0], F32) for nm in BIG_NAMES}
    d_ng = [[None] * 4 for _ in range(DEPTH)]
    d_memnorm = [None] * DEPTH
    d_wbd = [None] * N_A
    d_pscale = [None] * N_A
    d_kvnorm = None
    kv_parts = []

    for l in reversed(range(DEPTH)):
        t = f"l{l}_b_"
        sv = fw[l]
        dy2, d_ng[l][3] = _norm_bwd(dx, sv['y2'], sv['r3'], norm_gains[l, 3], name=t + "norm3", out_dtype=BF)
        gbig['w_down'] = _mm(sv['aa'], dy2, ta=True, tm=1408, name=t + "dw_down", stack=(gbig['w_down'], l))
        dg, du = _mm(dy2, wts[l]['w_down'], tb=True, b_layer=0, tn=256, name=t + "d_act",
                     extras=((sv['gg'], 'tile'), (sv['uu'], 'tile')), epilogue=_swiglu_bwd_epilogue, out_dtypes=(BF, BF))
        gbig['w_gate_up'] = _mm(sv['h2'], (dg, du), ta=True, tn=1408, tk=512, name=t + "dw_gate_up",
                                stack=(gbig['w_gate_up'], l))
        dh2, = _mm((dg, du), wts[l]['w_gate_up'], tb=True, b_layer=0, tk=1408, name=t + "d_h2", out_dtypes=(BF,))
        dx1, d_ng[l][2] = _norm_bwd(dh2, sv['x1'], sv['r2'], norm_gains[l, 2], name=t + "norm2", add=dx)
        dy1, d_ng[l][1] = _norm_bwd(dx1, sv['y1'], sv['r1'], norm_gains[l, 1], name=t + "norm1", out_dtype=BF)
        gbig['w_out'] = _mm(sv['ycat'], dy1, ta=True, name=t + "dw_out", stack=(gbig['w_out'], l))
        dycat, = _mm(dy1, wts[l]['w_out'], tb=True, b_layer=0, name=t + "d_ycat")
        dz = lax.empty((T, D_MODEL), BF)
        dz, dkm, dvm = _memattn_bwd(dycat, sv['z'], sv['kvm'], sv['lse_m'], dz, B, S, name=t + "memattn")
        if l < N_A:
            dz, d_wbd[l], d_pscale[l] = _pool_bwd(dycat, sv['p'], wbd[l], pool_scale[l], dz, B, S, name=t + "pool")
        else:
            do, cb = _dil_combine_bwd(dycat, sv['o'], sv['lse'], name=t + "dil_combine")
            acc = tuple(lax.empty((T, MAIN_W), F32) for _ in range(5))
            for g in range(3):
                acc = _dil_bwd_group(g, sv['rq'], rk, rv, do, cb, sv['lse'], acc, B, S, name=t + f"dil{g}")
            dz = _rope_apply(acc[0], cos, sin, name=t + "rope_q", sign=-1.0, alias=dz)
            kv_parts.append(acc[1:])
        gbig['w_in'] = _mm(sv['h0'], dz, ta=True, name=t + "dw_in", stack=(gbig['w_in'], l))
        dh0, = _mm(dz, wts[l]['w_in'], tb=True, b_layer=0, name=t + "d_h0", out_dtypes=(BF,))
        dx, d_ng[l][0] = _norm_bwd(dh0, sv['x_in'], sv['r0'], norm_gains[l, 0], name=t + "norm0", add=dx1)
        gbig['w_mem_kv'] = _mm(sv['memn'], (dkm, dvm), ta=True, tn=256, name=t + "dw_memkv", stack=(gbig['w_mem_kv'], l))
        dmemn, = _mm((dkm, dvm), wts[l]['w_mem_kv'], tb=True, b_layer=0, tk=256, name=t + "d_memn", out_dtypes=(BF,))
        _, d_memnorm[l] = _norm_bwd(dmemn, memf, sv['rm'], mem_norm[l], name=t + "norm_mem", out_dtype=BF, tm=256)
        if l == N_A:
            dk, dv = _kv_grad(kv_parts, cos, sin, B, S, name="kv_grad")
            x_kv, kvn, rkv = kv_saved
            gbig['w_kv'] = _mm(kvn, (dk, dv), ta=True, tn=768, name="dw_kv", stack=(gbig['w_kv'], 0))
            dkvn, = _mm((dk, dv), wts[N_A - 1]['w_kv'], tb=True, b_layer=0, tk=768, name="d_kvn", out_dtypes=(BF,))
            dx, d_kvnorm = _norm_bwd(dkvn, x_kv, rkv, kv_norm, name="norm_kv_b", add=dx)

    small = {
        'norm_gains': jnp.stack([jnp.concatenate(d_ng[l], axis=0) for l in range(DEPTH)]),
        'mem_norm': jnp.concatenate(d_memnorm, axis=0),
        'kv_norm': d_kvnorm.reshape(D_MODEL),
        'pool_scale': jnp.concatenate(d_pscale, axis=0),
        'w_pool': jnp.stack([jnp.stack([d_wbd[l][gi * POOL_GROUP:(gi + 1) * POOL_GROUP, gi * POOL_GROUP:(gi + 1) * POOL_GROUP]
                                        for gi in range(len(POOL_WINDOWS))]) for l in range(N_A)]),
    }
    return loss, dx, gbig, small


SMALL_ORDER = ('norm_gains', 'mem_norm', 'kv_norm', 'pool_scale', 'w_pool')
SMALL_VEC_ROWS = 2560


def kernel(x, mem, positions, norm_gains, mem_norm, w_in, w_mem_kv, w_out, w_pool, pool_scale, kv_norm, w_kv, w_gate_up, w_down, loss_target, m_norm_gains, m_mem_norm, m_w_in, m_w_mem_kv, m_w_out, m_w_pool, m_pool_scale, m_kv_norm, m_w_kv, m_w_gate_up, m_w_down, v_norm_gains, v_mem_norm, v_w_in, v_w_mem_kv, v_w_out, v_w_pool, v_pool_scale, v_kv_norm, v_w_kv, v_w_gate_up, v_w_down):
    xi, yi, ci = lax.axis_index("x"), lax.axis_index("y"), lax.axis_index("c")
    s = 2 * xi + yi
    sc = jnp.stack([s, ci]).astype(jnp.int32)
    weights = dict(norm_gains=norm_gains, mem_norm=mem_norm, w_in=w_in, w_mem_kv=w_mem_kv, w_out=w_out, w_pool=w_pool,
                   pool_scale=pool_scale, kv_norm=kv_norm, w_kv=w_kv, w_gate_up=w_gate_up, w_down=w_down)
    moms = dict(norm_gains=m_norm_gains, mem_norm=m_mem_norm, w_in=m_w_in, w_mem_kv=m_w_mem_kv, w_out=m_w_out,
                w_pool=m_w_pool, pool_scale=m_pool_scale, kv_norm=m_kv_norm, w_kv=m_w_kv, w_gate_up=m_w_gate_up,
                w_down=m_w_down)
    vels = dict(norm_gains=v_norm_gains, mem_norm=v_mem_norm, w_in=v_w_in, w_mem_kv=v_w_mem_kv, w_out=v_w_out,
                w_pool=v_w_pool, pool_scale=v_pool_scale, kv_norm=v_kv_norm, w_kv=v_w_kv, w_gate_up=v_w_gate_up,
                w_down=v_w_down)

    small_w = jnp.zeros((SMALL_ROWS, 256), F32)
    small_w = lax.dynamic_update_slice(small_w, norm_gains.reshape(16, 256), (0, 0))
    small_w = lax.dynamic_update_slice(small_w, pool_scale, (16, 0))
    started = []
    for l in range(DEPTH):
        names = LAYER_W + (('w_kv',) if l == N_A - 1 else ())
        shards = [weights[nm][l:l + 1].astype(BF) for nm in LAYER_W]
        if l == N_A - 1:
            shards.append(w_kv.astype(BF).reshape(_shard_shape('w_kv')))
        started.append(_gather_start(l, names, shards, small_w if l == 0 else None, sc))
    all_started = started[0]['token'] + started[1]['token'] + started[2]['token'] + started[3]['token']

    def weights_of(l, mid, end):
        fwd = _gather_forward(started[l], all_started if mid is None else mid)
        return _gather_finish(fwd, fwd['token'] if end is None else end)

    loss, gx, gbig, gsmall = _local_step(x, mem, positions, weights_of, mem_norm, w_pool, kv_norm, loss_target)
    loss = lax.psum(loss[0, 0], ("x", "y", "c"))

    from_sib = _exchange_halves(gbig)
    halves, own = {}, {}
    for nm in BIG_NAMES:
        halves[nm], own[nm] = _add_halves(nm, gbig[nm], from_sib[nm], sc)
    pieces = _scatter_to_chips(halves)
    gsh = _share_with_sibling({nm: _sum_pieces(nm, own[nm], pieces[nm], sc) for nm in BIG_NAMES})

    vec = jnp.concatenate([gsmall[nm].reshape(-1) for nm in SMALL_ORDER])
    vec = jnp.pad(vec, (0, SMALL_VEC_ROWS * 128 - vec.shape[0])).reshape(SMALL_VEC_ROWS, 128)
    tot = _sum8(_all_gather_small(vec), name="sum_small_grads", tr=512).reshape(-1)
    grads, off = {}, 0
    for nm in SMALL_ORDER:
        shape = (DEPTH, 4, D_MODEL) if nm == 'norm_gains' else (N_A, MAIN_W) if nm == 'pool_scale' else weights[nm].shape
        n = 1
        for dim in shape:
            n *= dim
        grads[nm] = tot[off:off + n].reshape(shape)
        off += n
    grads['norm_gains'] = lax.dynamic_slice(grads['norm_gains'], (0, 0, s * 256), (DEPTH, 4, 256))
    grads['pool_scale'] = lax.dynamic_slice(grads['pool_scale'], (0, s * POOL_GROUP), (N_A, POOL_GROUP))
    for nm in BIG_NAMES:
        grads[nm] = gsh[nm].reshape(weights[nm].shape)

    order = ('norm_gains', 'mem_norm', 'w_in', 'w_mem_kv', 'w_out', 'w_pool', 'pool_scale', 'kv_norm', 'w_kv',
             'w_gate_up', 'w_down')
    deltas, new_m, new_v = {}, {}, {}
    for nm in order:
        deltas[nm], new_m[nm], new_v[nm] = _adamw(weights[nm], grads[nm], moms[nm], vels[nm], name="adamw_" + nm)
    return (loss, gx.reshape(x.shape), *[grads[nm] for nm in order], *[deltas[nm] for nm in order],
            *[new_m[nm] for nm in order], *[new_v[nm] for nm in order])
```

```python
import functools

import jax
import jax.numpy as jnp
from jax import lax
from jax.experimental import pallas as pl
from jax.experimental.pallas import tpu as pltpu

F32 = jnp.float32
BF = jnp.bfloat16

D_MODEL = 1024
DEPTH = 4
N_A = 2
HEAD_DIM = 64
MEM_W = 256
MAIN_W = 768
D_FF = 2816
N_MEM = 256
POOL_WINDOWS = (2, 4, 8, 16)
POOL_GROUP = 192
DIL = (1, 4, 16)
STEPS = 128
ROPE_THETA = 10000.0
EPS = 1e-6
SCALE = HEAD_DIM ** -0.5
NEG = -1e30

ADAM_LR = 0.001
ADAM_B1 = 0.9
ADAM_B2 = 0.999
ADAM_EPS = 1e-08
ADAM_WD = 0.01
ADAM_STEP = 10

VMEM_LIMIT = 48 * 1024 * 1024
MESH = pl.DeviceIdType.MESH


def _cp(sem):
    return pltpu.CompilerParams(dimension_semantics=sem, vmem_limit_bytes=VMEM_LIMIT)


def _mm(a, b, *, name, ta=False, tb=False, tm=1024, tn=512, tk=1024, b_layer=None, b_offsets=(0,),
        extras=(), epilogue=None, out_dtypes=(F32,), out_n=None, stack=None):
    a_pair = isinstance(a, (tuple, list))
    b_pair = isinstance(b, (tuple, list))
    a0 = a[0] if a_pair else a
    b0 = b[0] if b_pair else b
    a_rows, a_cols = a0.shape
    if a_pair:
        a_cols *= 2
    b_rows, b_cols = b0.shape[-2:]
    if b_pair:
        b_cols *= 2
    M, K = (a_cols, a_rows) if ta else (a_rows, a_cols)
    N = b_rows if tb else b_cols
    if out_n is not None:
        N = out_n
    tm, tn, tk = min(tm, M), min(tn, N), min(tk, K)
    assert M % tm == 0 and N % tn == 0 and K % tk == 0, (name, M, N, K, tm, tn, tk)
    nk = K // tk
    n_acc = len(b_offsets)

    if a_pair:
        a_half = (a0.shape[1] // (tm if ta else tk))
    if b_pair:
        b_half = (b0.shape[1] // (tk if tb else tn))

    def a_map(sel):
        def f(i, j, k):
            r, c = (k, i) if ta else (i, k)
            if a_pair:
                c = jnp.clip(c - sel * a_half, 0, a_half - 1)
            return (r, c)
        return f

    def b_map(sel, off):
        def f(i, j, k):
            r, c = (j + off, k) if tb else (k, j + off)
            if b_pair:
                c = jnp.clip(c - sel * b_half, 0, b_half - 1)
            if b_layer is not None:
                return (b_layer, r, c)
            return (r, c)
        return f

    a_blk = (tk, tm) if ta else (tm, tk)
    b_blk = (tn, tk) if tb else (tk, tn)
    if b_layer is not None:
        b_blk = (None,) + b_blk
    in_specs, operands = [], []
    for sel in range(2 if a_pair else 1):
        in_specs.append(pl.BlockSpec(a_blk, a_map(sel)))
        operands.append(a[sel] if a_pair else a)
    n_a = len(operands)
    for off in b_offsets:
        for sel in range(2 if b_pair else 1):
            in_specs.append(pl.BlockSpec(b_blk, b_map(sel, off)))
            operands.append(b[sel] if b_pair else b)
    n_b = len(operands) - n_a
    for arr, kind in extras:
        if kind == 'tile':
            in_specs.append(pl.BlockSpec((tm, tn), lambda i, j, k: (i, j)))
        elif kind == 'row':
            in_specs.append(pl.BlockSpec((tm, 1), lambda i, j, k: (i, 0)))
        else:
            in_specs.append(pl.BlockSpec((1, tn), lambda i, j, k: (0, j)))
        operands.append(arr)
    n_e = len(extras)
    n_o = len(out_dtypes)
    dims = (((0,) if ta else (1,), (1,) if tb else (0,)), ((), ()))

    def body(*refs):
        a_refs = refs[:n_a]
        b_refs = refs[n_a:n_a + n_b]
        e_refs = refs[n_a + n_b:n_a + n_b + n_e]
        n_in = n_a + n_b + n_e + (1 if stack is not None else 0)
        o_refs = refs[n_in:n_in + n_o]
        acc_refs = refs[n_in + n_o:]
        i, j, k = pl.program_id(0), pl.program_id(1), pl.program_id(2)
        if a_pair:
            cidx = i if ta else k
            av = jnp.where(cidx < a_half, a_refs[0][...], a_refs[1][...])
        else:
            av = a_refs[0][...]
        av = av.astype(BF)
        prods = []
        for q in range(n_acc):
            if b_pair:
                cidx = (k if tb else j) + b_offsets[q]
                bv = jnp.where(cidx < b_half, b_refs[2 * q][...], b_refs[2 * q + 1][...])
            else:
                bv = b_refs[q][...]
            prods.append(lax.dot_general(av, bv.astype(BF), dims, preferred_element_type=F32))

        def finish(accs):
            outs = epilogue(accs, *[r[...] for r in e_refs]) if epilogue is not None else accs
            for o_ref, o in zip(o_refs, outs):
                o_ref[...] = o.astype(o_ref.dtype)

        if nk == 1:
            finish(prods)
        else:
            @pl.when(k == 0)
            def _():
                for r, p in zip(acc_refs, prods):
                    r[...] = p

            @pl.when(k > 0)
            def _():
                for r, p in zip(acc_refs, prods):
                    r[...] += p

            @pl.when(k == nk - 1)
            def _():
                finish([r[...] for r in acc_refs])

    if stack is not None:
        buf, layer = stack
        assert n_o == 1 and buf.shape[1:] == (M, N)
        return pl.pallas_call(
            body, name=name,
            grid=(M // tm, N // tn, nk),
            in_specs=in_specs + [pl.BlockSpec(memory_space=pl.ANY)],
            out_specs=[pl.BlockSpec((None, tm, tn), lambda i, j, k: (layer, i, j))],
            out_shape=[jax.ShapeDtypeStruct(buf.shape, buf.dtype)],
            scratch_shapes=[pltpu.VMEM((tm, tn), F32) for _ in range(n_acc if nk > 1 else 0)],
            input_output_aliases={len(operands): 0},
            compiler_params=_cp(("parallel", "parallel", "arbitrary")),
        )(*operands, buf)[0]
    return pl.pallas_call(
        body, name=name,
        grid=(M // tm, N // tn, nk),
        in_specs=in_specs,
        out_specs=[pl.BlockSpec((tm, tn), lambda i, j, k: (i, j)) for _ in range(n_o)],
        out_shape=[jax.ShapeDtypeStruct((M, N), dt) for dt in out_dtypes],
        scratch_shapes=[pltpu.VMEM((tm, tn), F32) for _ in range(n_acc if nk > 1 else 0)],
        compiler_params=_cp(("parallel", "parallel", "arbitrary")),
    )(*operands)


def _norm_fwd(x, g, *, name, res=None, out_dtype=F32, tm=512):
    T, Dm = x.shape
    has_res = res is not None

    def body(*refs):
        if has_res:
            x_ref, g_ref, r_ref, y_ref, s_ref = refs
        else:
            x_ref, g_ref, y_ref, s_ref = refs
        xv = x_ref[...]
        rstd = lax.rsqrt(jnp.mean(xv * xv, axis=-1, keepdims=True) + EPS)
        y = xv * rstd * g_ref[...]
        if has_res:
            y = r_ref[...] + y
        y_ref[...] = y.astype(y_ref.dtype)
        s_ref[...] = rstd

    row = pl.BlockSpec((tm, Dm), lambda i: (i, 0))
    in_specs = [row, pl.BlockSpec((1, Dm), lambda i: (0, 0))] + ([row] if has_res else [])
    ops = [x, g.reshape(1, Dm)] + ([res] if has_res else [])
    return pl.pallas_call(
        body, name=name, grid=(T // tm,), in_specs=in_specs,
        out_specs=[row, pl.BlockSpec((tm, 1), lambda i: (i, 0))],
        out_shape=[jax.ShapeDtypeStruct((T, Dm), out_dtype), jax.ShapeDtypeStruct((T, 1), F32)],
        compiler_params=_cp(("parallel",)),
    )(*ops)


def _norm_bwd(dout, x, rstd, g, *, name, add=None, out_dtype=F32, tm=512):
    T, Dm = x.shape
    has_add = add is not None
    nt = T // tm

    def body(*refs):
        if has_add:
            do_ref, x_ref, s_ref, g_ref, a_ref, dx_ref, dg_ref, acc = refs
        else:
            do_ref, x_ref, s_ref, g_ref, dx_ref, dg_ref, acc = refs
        i = pl.program_id(0)
        do = do_ref[...].astype(F32)
        xh = x_ref[...] * s_ref[...]
        gd = do * g_ref[...]
        dx = s_ref[...] * (gd - xh * jnp.mean(gd * xh, axis=-1, keepdims=True))
        if has_add:
            dx = dx + a_ref[...].astype(F32)
        dx_ref[...] = dx.astype(dx_ref.dtype)
        part = jnp.sum((do * xh).reshape(tm // 8, 8, Dm), axis=0)

        @pl.when(i == 0)
        def _():
            acc[...] = part

        @pl.when(i > 0)
        def _():
            acc[...] += part

        @pl.when(i == nt - 1)
        def _():
            dg_ref[...] = jnp.sum(acc[...], axis=0, keepdims=True)

    row = pl.BlockSpec((tm, Dm), lambda i: (i, 0))
    in_specs = [row, row, pl.BlockSpec((tm, 1), lambda i: (i, 0)), pl.BlockSpec((1, Dm), lambda i: (0, 0))]
    ops = [dout, x, rstd, g.reshape(1, Dm)]
    if has_add:
        in_specs.append(row)
        ops.append(add)
    return pl.pallas_call(
        body, name=name, grid=(nt,), in_specs=in_specs,
        out_specs=[row, pl.BlockSpec((1, Dm), lambda i: (0, 0))],
        out_shape=[jax.ShapeDtypeStruct((T, Dm), out_dtype), jax.ShapeDtypeStruct((1, Dm), F32)],
        scratch_shapes=[pltpu.VMEM((8, Dm), F32)],
        compiler_params=_cp(("arbitrary",)),
    )(*ops)


def _swiglu_fwd_epilogue(accs):
    g, u = accs
    return g, u, g * jax.nn.sigmoid(g) * u


def _swiglu_bwd_epilogue(accs, g, u):
    da = accs[0]
    g = g.astype(F32)
    u = u.astype(F32)
    sig = jax.nn.sigmoid(g)
    return da * u * (sig * (1.0 + g * (1.0 - sig))), da * (g * sig)


def _rope_tables(pos, *, name, tm=1024):
    T = pos.shape[0]
    half = HEAD_DIM // 2
    freqs = ROPE_THETA ** (-jnp.arange(half, dtype=F32) / half)
    freqs = jnp.tile(freqs, 4).reshape(1, 128)

    def body(p_ref, f_ref, c_ref, s_ref):
        ang = p_ref[...].astype(F32) * f_ref[...]
        lane = lax.broadcasted_iota(jnp.int32, ang.shape, 1)
        c_ref[...] = jnp.cos(ang)
        s_ref[...] = jnp.where(lane % HEAD_DIM < half, -1.0, 1.0) * jnp.sin(ang)

    tab = pl.BlockSpec((tm, 128), lambda i: (i, 0))
    return pl.pallas_call(
        body, name=name, grid=(T // tm,),
        in_specs=[pl.BlockSpec((tm, 1), lambda i: (i, 0)), pl.BlockSpec((1, 128), lambda i: (0, 0))],
        out_specs=[tab, tab],
        out_shape=[jax.ShapeDtypeStruct((T, 128), F32)] * 2,
        compiler_params=_cp(("parallel",)),
    )(pos, freqs)


def _rot(x, cos, sin, sign):
    W = x.shape[1]
    half = HEAD_DIM // 2
    reps = W // 128
    c = jnp.concatenate([cos] * reps, axis=1) if reps > 1 else cos
    s = jnp.concatenate([sin] * reps, axis=1) if reps > 1 else sin
    lane = lax.broadcasted_iota(jnp.int32, x.shape, 1)
    swapped = jnp.where(lane % HEAD_DIM < half, pltpu.roll(x, W - half, axis=1), pltpu.roll(x, half, axis=1))
    return x * c + (sign * s) * swapped


def _rope_apply(x, cos, sin, *, name, sign=1.0, width=MAIN_W, passthrough=False, out_dtype=BF, alias=None,
                out_cols=None, tm=512):
    T = x.shape[0]

    def body(*refs):
        if passthrough:
            x_ref, v_ref, c_ref, s_ref, o_ref, ov_ref = refs
            ov_ref[...] = v_ref[...].astype(ov_ref.dtype)
        elif alias is not None:
            x_ref, c_ref, s_ref, _, o_ref = refs
        else:
            x_ref, c_ref, s_ref, o_ref = refs
        o_ref[...] = _rot(x_ref[...].astype(F32), c_ref[...], s_ref[...], sign).astype(o_ref.dtype)

    blk0 = pl.BlockSpec((tm, width), lambda i: (i, 0))
    blk1 = pl.BlockSpec((tm, width), lambda i: (i, 1))
    tab = pl.BlockSpec((tm, 128), lambda i: (i, 0))
    if passthrough:
        return pl.pallas_call(
            body, name=name, grid=(T // tm,), in_specs=[blk0, blk1, tab, tab], out_specs=[blk0, blk0],
            out_shape=[jax.ShapeDtypeStruct((T, width), out_dtype)] * 2,
            compiler_params=_cp(("parallel",)),
        )(x, x, cos, sin)
    if alias is not None:
        return pl.pallas_call(
            body, name=name, grid=(T // tm,),
            in_specs=[blk0, tab, tab, pl.BlockSpec(memory_space=pl.ANY)], out_specs=blk0,
            out_shape=jax.ShapeDtypeStruct(alias.shape, alias.dtype),
            input_output_aliases={3: 0},
            compiler_params=_cp(("parallel",)),
        )(x, cos, sin, alias)
    return pl.pallas_call(
        body, name=name, grid=(T // tm,), in_specs=[blk0, tab, tab], out_specs=blk0,
        out_shape=jax.ShapeDtypeStruct((T, width), out_dtype),
        compiler_params=_cp(("parallel",)),
    )(x, cos, sin)


POOL_T = 256
POOL_HALO = 16


def _pool_lane_window(shape):
    lane = lax.broadcasted_iota(jnp.int32, shape, 1)
    w = jnp.full(shape, POOL_WINDOWS[0], jnp.int32)
    for gi in range(1, len(POOL_WINDOWS)):
        w = jnp.where(lane >= gi * POOL_GROUP, POOL_WINDOWS[gi], w)
    return w


def _pool_fwd(z, wbd, scale, B, S, *, name):
    T = z.shape[0]
    nt = S // POOL_T
    hb = POOL_T // POOL_HALO

    def body(z_ref, h_ref, w_ref, sc_ref, y_ref, p_ref, ext):
        i = pl.program_id(1)
        u = z_ref[...]
        ext[pl.ds(POOL_HALO, POOL_T), :] = u
        ext[pl.ds(0, POOL_HALO), :] = jnp.where(i > 0, h_ref[...], 0.0)
        win = _pool_lane_window((POOL_T, MAIN_W))
        acc = u
        for k in range(1, POOL_HALO):
            acc = acc + jnp.where(k < win, ext[pl.ds(POOL_HALO - k, POOL_T), :], 0.0)
        t = i * POOL_T + lax.broadcasted_iota(jnp.int32, (POOL_T, MAIN_W), 0)
        cnt = jnp.minimum(t + 1, win).astype(F32)
        p = (acc / cnt - u).astype(BF)
        p_ref[...] = p
        y = jnp.dot(p, w_ref[...], preferred_element_type=F32) * sc_ref[...]
        y_ref[...] = y.astype(y_ref.dtype)

    return pl.pallas_call(
        body, name=name, grid=(B, nt),
        in_specs=[pl.BlockSpec((POOL_T, MAIN_W), lambda b, i: (b * nt + i, 0)),
                  pl.BlockSpec((POOL_HALO, MAIN_W), lambda b, i: (jnp.maximum((b * nt + i) * hb - 1, 0), 0)),
                  pl.BlockSpec((MAIN_W, MAIN_W), lambda b, i: (0, 0)),
                  pl.BlockSpec((1, MAIN_W), lambda b, i: (0, 0))],
        out_specs=[pl.BlockSpec((POOL_T, MAIN_W), lambda b, i: (b * nt + i, 0)),
                   pl.BlockSpec((POOL_T, MAIN_W), lambda b, i: (b * nt + i, 0))],
        out_shape=[jax.ShapeDtypeStruct((T, D_MODEL), BF), jax.ShapeDtypeStruct((T, MAIN_W), BF)],
        scratch_shapes=[pltpu.VMEM((POOL_T + POOL_HALO, MAIN_W), F32)],
        compiler_params=_cp(("parallel", "parallel")),
    )(z, z, wbd, scale.reshape(1, MAIN_W))


def _pool_bwd(dy, p, wbd, scale, dz_alias, B, S, *, name):
    T = dy.shape[0]
    nt = S // POOL_T
    hb = POOL_T // POOL_HALO
    last_halo = T // POOL_HALO - 1
    R = POOL_T + POOL_HALO

    def body(dy_ref, dyn_ref, p_ref, pn_ref, w_ref, sc_ref, _, dz_ref, dw_ref, ds_ref, ext, dw_acc, ds_acc):
        b, i = pl.program_id(0), pl.program_id(1)
        first = jnp.logical_and(b == 0, i == 0)
        dyv = dy_ref[...]
        pv = p_ref[...]
        sc = sc_ref[...]
        w = w_ref[...]
        pw = jnp.dot(pv, w, preferred_element_type=F32)
        ds_part = jnp.sum((dyv * pw).reshape(POOL_T // 8, 8, MAIN_W), axis=0)
        dpw = (dyv * sc).astype(BF)
        dw_part = lax.dot_general(pv, dpw, (((0,), (0,)), ((), ())), preferred_element_type=F32)

        @pl.when(first)
        def _():
            dw_acc[...] = dw_part
            ds_acc[...] = ds_part

        @pl.when(jnp.logical_not(first))
        def _():
            dw_acc[...] += dw_part
            ds_acc[...] += ds_part

        @pl.when(jnp.logical_and(b == pl.num_programs(0) - 1, i == nt - 1))
        def _():
            dw_ref[...] = dw_acc[...]
            ds_ref[...] = jnp.sum(ds_acc[...], axis=0, keepdims=True)

        dp = lax.dot_general(dpw, w, (((1,), (1,)), ((), ())), preferred_element_type=F32)
        dpn = lax.dot_general((dyn_ref[...] * sc).astype(BF), w, (((1,), (1,)), ((), ())), preferred_element_type=F32)
        win = _pool_lane_window((POOL_T, MAIN_W))
        win_n = _pool_lane_window((POOL_HALO, MAIN_W))
        t = i * POOL_T + lax.broadcasted_iota(jnp.int32, (POOL_T, MAIN_W), 0)
        tn = (i + 1) * POOL_T + lax.broadcasted_iota(jnp.int32, (POOL_HALO, MAIN_W), 0)
        ext[pl.ds(0, POOL_T), :] = dp / jnp.minimum(t + 1, win).astype(F32)
        ext[pl.ds(POOL_T, POOL_HALO), :] = jnp.where(i < nt - 1, dpn / jnp.minimum(tn + 1, win_n).astype(F32), 0.0)
        acc = -dp
        for k in range(POOL_HALO):
            acc = acc + jnp.where(k < win, ext[pl.ds(k, POOL_T), :], 0.0)
        dz_ref[...] = acc.astype(dz_ref.dtype)

    cur = lambda b, i: (b * nt + i, 0)
    nxt = lambda b, i: (jnp.minimum((b * nt + i + 1) * hb, last_halo), 0)
    return pl.pallas_call(
        body, name=name, grid=(B, nt),
        in_specs=[pl.BlockSpec((POOL_T, MAIN_W), cur), pl.BlockSpec((POOL_HALO, MAIN_W), nxt),
                  pl.BlockSpec((POOL_T, MAIN_W), cur), pl.BlockSpec((POOL_HALO, MAIN_W), nxt),
                  pl.BlockSpec((MAIN_W, MAIN_W), lambda b, i: (0, 0)),
                  pl.BlockSpec((1, MAIN_W), lambda b, i: (0, 0)),
                  pl.BlockSpec(memory_space=pl.ANY)],
        out_specs=[pl.BlockSpec((POOL_T, MAIN_W), cur),
                   pl.BlockSpec((MAIN_W, MAIN_W), lambda b, i: (0, 0)),
                   pl.BlockSpec((1, MAIN_W), lambda b, i: (0, 0))],
        out_shape=[jax.ShapeDtypeStruct(dz_alias.shape, dz_alias.dtype),
                   jax.ShapeDtypeStruct((MAIN_W, MAIN_W), F32), jax.ShapeDtypeStruct((1, MAIN_W), F32)],
        scratch_shapes=[pltpu.VMEM((R, MAIN_W), F32), pltpu.VMEM((MAIN_W, MAIN_W), F32), pltpu.VMEM((8, MAIN_W), F32)],
        input_output_aliases={6: 0},
        compiler_params=_cp(("arbitrary", "arbitrary")),
    )(dy, dy, p, p, wbd, scale.reshape(1, MAIN_W), dz_alias)


def _head_masks(shape):
    lane = lax.broadcasted_iota(jnp.int32, shape, 1)
    return [(lane // HEAD_DIM) == h for h in range(shape[1] // HEAD_DIM)]


def _row_of(bcast, mask):
    return jnp.max(jnp.where(mask, bcast, -jnp.inf), axis=-1, keepdims=True)


MEM_TQ = 512


def _memattn_fwd(z, kv, y_alias, B, S, *, name):
    T = z.shape[0]
    nt = S // MEM_TQ

    def body(q_ref, k_ref, v_ref, _, y_ref, l_ref):
        q = q_ref[...]
        k = k_ref[...]
        v = v_ref[...]
        masks = _head_masks(q.shape)
        o = jnp.zeros(q.shape, F32)
        lse_b = jnp.zeros(q.shape, F32)
        for m in masks:
            qm = jnp.where(m, q, 0.0).astype(BF)
            s = lax.dot_general(qm, k, (((1,), (1,)), ((), ())), preferred_element_type=F32) * SCALE
            mx = jnp.max(s, axis=-1, keepdims=True)
            e = jnp.exp(s - mx)
            l = jnp.sum(e, axis=-1, keepdims=True)
            p = (e / l).astype(BF)
            o = o + jnp.where(m, jnp.dot(p, v, preferred_element_type=F32), 0.0)
            lse_b = lse_b + jnp.where(m, mx + jnp.log(l), 0.0)
        y_ref[...] = o.astype(y_ref.dtype)
        l_ref[...] = lse_b

    qblk = pl.BlockSpec((MEM_TQ, MEM_W), lambda b, i: (b * nt + i, 3))
    return pl.pallas_call(
        body, name=name, grid=(B, nt),
        in_specs=[qblk, pl.BlockSpec((N_MEM, MEM_W), lambda b, i: (b, 0)), pl.BlockSpec((N_MEM, MEM_W), lambda b, i: (b, 1)),
                  pl.BlockSpec(memory_space=pl.ANY)],
        out_specs=[qblk, pl.BlockSpec((MEM_TQ, MEM_W), lambda b, i: (b * nt + i, 0))],
        out_shape=[jax.ShapeDtypeStruct(y_alias.shape, y_alias.dtype), jax.ShapeDtypeStruct((T, MEM_W), F32)],
        input_output_aliases={3: 0},
        compiler_params=_cp(("parallel", "parallel")),
    )(z, kv, kv, y_alias)


def _memattn_bwd(dy, z, kv, lse, dz_alias, B, S, *, name):
    nt = S // MEM_TQ

    def body(do_ref, q_ref, k_ref, v_ref, l_ref, _, dz_ref, dk_ref, dv_ref, dk_acc, dv_acc):
        i = pl.program_id(1)
        do = do_ref[...]
        q = q_ref[...]
        k = k_ref[...]
        v = v_ref[...]
        lse_b = l_ref[...]
        masks = _head_masks(q.shape)
        dq = jnp.zeros(q.shape, F32)
        dk = jnp.zeros(k.shape, F32)
        dv = jnp.zeros(v.shape, F32)
        for m in masks:
            qm = jnp.where(m, q, 0.0).astype(BF)
            dom = jnp.where(m, do, 0.0).astype(BF)
            s = lax.dot_general(qm, k, (((1,), (1,)), ((), ())), preferred_element_type=F32) * SCALE
            p = jnp.exp(s - _row_of(lse_b, m))
            dp = lax.dot_general(dom, v, (((1,), (1,)), ((), ())), preferred_element_type=F32)
            delta = jnp.sum(p * dp, axis=-1, keepdims=True)
            ds = (p * (dp - delta) * SCALE).astype(BF)
            pb = p.astype(BF)
            dv = dv + jnp.where(m[:N_MEM], lax.dot_general(pb, dom, (((0,), (0,)), ((), ())), preferred_element_type=F32), 0.0)
            dk = dk + jnp.where(m[:N_MEM], lax.dot_general(ds, qm, (((0,), (0,)), ((), ())), preferred_element_type=F32), 0.0)
            dq = dq + jnp.where(m, jnp.dot(ds, k, preferred_element_type=F32), 0.0)
        dz_ref[...] = dq.astype(dz_ref.dtype)

        @pl.when(i == 0)
        def _():
            dk_acc[...] = dk
            dv_acc[...] = dv

        @pl.when(i > 0)
        def _():
            dk_acc[...] += dk
            dv_acc[...] += dv

        @pl.when(i == nt - 1)
        def _():
            dk_ref[...] = dk_acc[...]
            dv_ref[...] = dv_acc[...]

    qblk = pl.BlockSpec((MEM_TQ, MEM_W), lambda b, i: (b * nt + i, 3))
    kblk = pl.BlockSpec((N_MEM, MEM_W), lambda b, i: (b, 0))
    return pl.pallas_call(
        body, name=name, grid=(B, nt),
        in_specs=[qblk, qblk, kblk, pl.BlockSpec((N_MEM, MEM_W), lambda b, i: (b, 1)),
                  pl.BlockSpec((MEM_TQ, MEM_W), lambda b, i: (b * nt + i, 0)), pl.BlockSpec(memory_space=pl.ANY)],
        out_specs=[qblk, kblk, kblk],
        out_shape=[jax.ShapeDtypeStruct(dz_alias.shape, dz_alias.dtype),
                   jax.ShapeDtypeStruct((B * N_MEM, MEM_W), F32), jax.ShapeDtypeStruct((B * N_MEM, MEM_W), F32)],
        scratch_shapes=[pltpu.VMEM((N_MEM, MEM_W), F32), pltpu.VMEM((N_MEM, MEM_W), F32)],
        input_output_aliases={5: 0},
        compiler_params=_cp(("parallel", "arbitrary")),
    )(dy, z, kv, kv, lse, dz_alias)


def _dil_scores(qm, kp, kc, n):
    qi = lax.broadcasted_iota(jnp.int32, (STEPS, STEPS), 0)
    kj = lax.broadcasted_iota(jnp.int32, (STEPS, STEPS), 1)
    sc = lax.dot_general(qm, kc, (((1,), (1,)), ((), ())), preferred_element_type=F32) * SCALE
    sc = jnp.where(kj <= qi, sc, NEG)
    if kp is None:
        return None, sc
    sp = lax.dot_general(qm, kp, (((1,), (1,)), ((), ())), preferred_element_type=F32) * SCALE
    sp = jnp.where(jnp.logical_and(kj >= qi, n > 0), sp, NEG)
    return sp, sc


def _dil_specs(g, d, nb):
    chunk = STEPS * d
    cur = pl.BlockSpec((chunk, 128), lambda b, n, hf: (b * nb + n, g * 2 + hf))
    prev = pl.BlockSpec((chunk, 128), lambda b, n, hf: (b * nb + jnp.maximum(n - 1, 0), g * 2 + hf))
    return cur, prev


def _dil_rows(r, d):
    return pl.ds(r, STEPS, stride=d) if d > 1 else slice(None)


def _dil_loop(d, fn):
    if d <= 4:
        for r in range(d):
            fn(r)
    else:
        lax.fori_loop(0, d, lambda r, carry: (fn(r), carry)[1], 0)


def _dil_fwd_group(g, q, k, v, o_alias, l_alias, B, S, *, name):
    d = DIL[g]
    nb = S // (STEPS * d)
    has_prev = nb > 1

    def body(*refs):
        if has_prev:
            q_ref, kp_ref, kc_ref, vp_ref, vc_ref, _, __, o_ref, l_ref = refs
        else:
            q_ref, kc_ref, vc_ref, _, __, o_ref, l_ref = refs
        n = pl.program_id(1)

        def residue(r):
            rows = _dil_rows(r, d)
            q = q_ref[rows, :]
            kc, vc = kc_ref[rows, :].astype(BF), vc_ref[rows, :].astype(BF)
            kp = kp_ref[rows, :].astype(BF) if has_prev else None
            vp = vp_ref[rows, :].astype(BF) if has_prev else None
            o = jnp.zeros(q.shape, F32)
            lse_b = jnp.zeros(q.shape, F32)
            for m in _head_masks(q.shape):
                qm = jnp.where(m, q, 0.0).astype(BF)
                sp, sc = _dil_scores(qm, kp, kc, n)
                mx = jnp.max(sc, axis=-1, keepdims=True)
                if has_prev:
                    mx = jnp.maximum(mx, jnp.max(sp, axis=-1, keepdims=True))
                l = jnp.sum(jnp.exp(sc - mx), axis=-1, keepdims=True)
                if has_prev:
                    l = l + jnp.sum(jnp.exp(sp - mx), axis=-1, keepdims=True)
                lse = mx + jnp.log(l)
                oh = jnp.dot(jnp.exp(sc - lse).astype(BF), vc, preferred_element_type=F32)
                if has_prev:
                    oh = oh + jnp.dot(jnp.exp(sp - lse).astype(BF), vp, preferred_element_type=F32)
                o = o + jnp.where(m, oh, 0.0)
                lse_b = lse_b + jnp.where(m, lse, 0.0)
            o_ref[rows, :] = o
            l_ref[rows, :] = lse_b

        _dil_loop(d, residue)

    cur, prev = _dil_specs(g, d, nb)
    anyspec = pl.BlockSpec(memory_space=pl.ANY)
    if has_prev:
        in_specs, ops = [cur, prev, cur, prev, cur], [q, k, k, v, v]
    else:
        in_specs, ops = [cur, cur, cur], [q, k, v]
    n_in = len(ops)
    o, l = pl.pallas_call(
        body, name=name, grid=(B, nb, 2),
        in_specs=in_specs + [anyspec, anyspec],
        out_specs=[cur, cur],
        out_shape=[jax.ShapeDtypeStruct(q.shape, F32)] * 2,
        input_output_aliases={n_in: 0, n_in + 1: 1},
        compiler_params=_cp(("parallel", "parallel", "parallel")),
    )(*ops, o_alias, l_alias)
    return o, l


def _dil_bwd_group(g, q, k, v, do, cb, lse, aliases, B, S, *, name):
    d = DIL[g]
    nb = S // (STEPS * d)
    has_prev = nb > 1
    n_out = 5 if has_prev else 3

    def body(*refs):
        if has_prev:
            q_ref, kp_ref, kc_ref, vp_ref, vc_ref, do_ref, c_ref, l_ref = refs[:8]
            dq_ref, dkc_ref, dvc_ref, dkp_ref, dvp_ref = refs[8 + n_out:]
        else:
            q_ref, kc_ref, vc_ref, do_ref, c_ref, l_ref = refs[:6]
            dq_ref, dkc_ref, dvc_ref = refs[6 + n_out:]
        n = pl.program_id(1)
        tdot = lambda a, b: lax.dot_general(a, b, (((0,), (0,)), ((), ())), preferred_element_type=F32)
        ndot = lambda a, b: lax.dot_general(a, b, (((1,), (1,)), ((), ())), preferred_element_type=F32)

        def residue(r):
            rows = _dil_rows(r, d)
            q = q_ref[rows, :]
            kc, vc = kc_ref[rows, :].astype(BF), vc_ref[rows, :].astype(BF)
            kp = kp_ref[rows, :].astype(BF) if has_prev else None
            vp = vp_ref[rows, :].astype(BF) if has_prev else None
            do = do_ref[rows, :]
            cbv = c_ref[rows, :]
            lse_b = l_ref[rows, :]
            z = jnp.zeros(q.shape, F32)
            dq, dkc, dkp, dvc, dvp = z, z, z, z, z
            for m in _head_masks(q.shape):
                qm = jnp.where(m, q, 0.0).astype(BF)
                dom = jnp.where(m, do, 0.0).astype(BF)
                sp, sc = _dil_scores(qm, kp, kc, n)
                lse = _row_of(lse_b, m)
                c = _row_of(cbv, m)
                pc = jnp.exp(sc - lse)
                dsc = (pc * (ndot(dom, vc) + c) * SCALE).astype(BF)
                dqh = jnp.dot(dsc, kc, preferred_element_type=F32)
                dkc = dkc + jnp.where(m, tdot(dsc, qm), 0.0)
                dvc = dvc + jnp.where(m, tdot(pc.astype(BF), dom), 0.0)
                if has_prev:
                    pp = jnp.exp(sp - lse)
                    dsp = (pp * (ndot(dom, vp) + c) * SCALE).astype(BF)
                    dqh = dqh + jnp.dot(dsp, kp, preferred_element_type=F32)
                    dkp = dkp + jnp.where(m, tdot(dsp, qm), 0.0)
                    dvp = dvp + jnp.where(m, tdot(pp.astype(BF), dom), 0.0)
                dq = dq + jnp.where(m, dqh, 0.0)
            dq_ref[rows, :] = dq
            dkc_ref[rows, :] = dkc
            dvc_ref[rows, :] = dvc
            if has_prev:
                dkp_ref[rows, :] = dkp
                dvp_ref[rows, :] = dvp

        _dil_loop(d, residue)

    cur, prev = _dil_specs(g, d, nb)
    anyspec = pl.BlockSpec(memory_space=pl.ANY)
    dq_a, dkc_a, dkp_a, dvc_a, dvp_a = aliases
    if has_prev:
        in_specs, ops = [cur, prev, cur, prev, cur, cur, cur, cur], [q, k, k, v, v, do, cb, lse]
        al = [dq_a, dkc_a, dvc_a, dkp_a, dvp_a]
    else:
        in_specs, ops = [cur, cur, cur, cur, cur, cur], [q, k, v, do, cb, lse]
        al = [dq_a, dkc_a, dvc_a]
    n_in = len(ops)
    outs = pl.pallas_call(
        body, name=name, grid=(B, nb, 2),
        in_specs=in_specs + [anyspec] * n_out,
        out_specs=[cur] * n_out,
        out_shape=[jax.ShapeDtypeStruct(q.shape, F32)] * n_out,
        input_output_aliases={n_in + i: i for i in range(n_out)},
        compiler_params=_cp(("parallel", "parallel", "parallel")),
    )(*ops, *al)
    if has_prev:
        dq_a, dkc_a, dvc_a, dkp_a, dvp_a = outs
    else:
        dq_a, dkc_a, dvc_a = outs
    return dq_a, dkc_a, dkp_a, dvc_a, dvp_a


def _group_softmax(lse):
    l0, l1, l2 = lse[:, 0:256], lse[:, 256:512], lse[:, 512:768]
    mx = jnp.maximum(jnp.maximum(l0, l1), l2)
    e0, e1, e2 = jnp.exp(l0 - mx), jnp.exp(l1 - mx), jnp.exp(l2 - mx)
    tot = e0 + e1 + e2
    return e0 / tot, e1 / tot, e2 / tot


def _dil_combine_fwd(o, lse, y_alias, *, name, tm=512):
    T = o.shape[0]

    def body(o_ref, l_ref, _, y_ref):
        a = jnp.concatenate(_group_softmax(l_ref[...]), axis=1)
        y_ref[...] = (o_ref[...] * a).astype(y_ref.dtype)

    blk = pl.BlockSpec((tm, MAIN_W), lambda i: (i, 0))
    return pl.pallas_call(
        body, name=name, grid=(T // tm,), in_specs=[blk, blk, pl.BlockSpec(memory_space=pl.ANY)], out_specs=blk,
        out_shape=jax.ShapeDtypeStruct(y_alias.shape, y_alias.dtype), input_output_aliases={2: 0},
        compiler_params=_cp(("parallel",)),
    )(o, lse, y_alias)


def _dil_combine_bwd(dy, o, lse, *, name, tm=256):
    T = o.shape[0]
    lane_r = lax.broadcasted_iota(jnp.int32, (256, 256), 0) // HEAD_DIM
    lane_c = lax.broadcasted_iota(jnp.int32, (256, 256), 1) // HEAD_DIM
    ones_bd = (lane_r == lane_c).astype(BF)

    def body(dy_ref, o_ref, l_ref, e_ref, do_ref, c_ref):
        dyv = dy_ref[...]
        alphas = _group_softmax(l_ref[...])
        prod = dyv * o_ref[...]
        e = e_ref[...]
        tot = jnp.zeros((tm, 256), F32)
        for gi in range(3):
            x = prod[:, gi * 256:(gi + 1) * 256]
            hi = x.astype(BF)
            lo = (x - hi.astype(F32)).astype(BF)
            dalpha = jnp.dot(hi, e, preferred_element_type=F32) + jnp.dot(lo, e, preferred_element_type=F32)
            tot = tot + alphas[gi] * dalpha
        a = jnp.concatenate(alphas, axis=1)
        do_ref[...] = (dyv * a).astype(do_ref.dtype)
        c_ref[...] = jnp.concatenate([-al * tot for al in alphas], axis=1)

    blk = pl.BlockSpec((tm, MAIN_W), lambda i: (i, 0))
    return pl.pallas_call(
        body, name=name, grid=(T // tm,),
        in_specs=[blk, blk, blk, pl.BlockSpec((256, 256), lambda i: (0, 0))], out_specs=[blk, blk],
        out_shape=[jax.ShapeDtypeStruct((T, MAIN_W), F32), jax.ShapeDtypeStruct((T, MAIN_W), F32)],
        compiler_params=_cp(("parallel",)),
    )(dy, o, lse, ones_bd)


def _kv_grad(parts, cos, sin, B, S, *, name):
    T = B * S
    tb = S // STEPS
    n_l = len(parts)

    def shifted(g):
        def f(b, t):
            return (b * tb + jnp.minimum(t + DIL[g], tb - 1), g)
        return f

    with_prev = [g for g in range(3) if DIL[g] < tb]
    n_p = len(with_prev)
    per_l = 2 + 2 * n_p

    def body(*refs):
        c_ref, s_ref = refs[0], refs[1]
        ins = refs[2:2 + n_l * per_l]
        dk_ref, dv_ref = refs[2 + n_l * per_l:]
        t = pl.program_id(1)
        dk = jnp.zeros((STEPS, MAIN_W), F32)
        dv = jnp.zeros((STEPS, MAIN_W), F32)
        zero = jnp.zeros((STEPS, 256), F32)
        for li in range(n_l):
            base = li * per_l
            dk = dk + ins[base][...]
            dv = dv + ins[base + 1][...]
            kparts, vparts = [zero] * 3, [zero] * 3
            for pi, g in enumerate(with_prev):
                ok = t + DIL[g] < tb
                kparts[g] = jnp.where(ok, ins[base + 2 + pi][...], 0.0)
                vparts[g] = jnp.where(ok, ins[base + 2 + n_p + pi][...], 0.0)
            dk = dk + jnp.concatenate(kparts, axis=1)
            dv = dv + jnp.concatenate(vparts, axis=1)
        dk_ref[...] = _rot(dk, c_ref[...], s_ref[...], -1.0).astype(dk_ref.dtype)
        dv_ref[...] = dv.astype(dv_ref.dtype)

    full = pl.BlockSpec((STEPS, MAIN_W), lambda b, t: (b * tb + t, 0))
    tab = pl.BlockSpec((STEPS, 128), lambda b, t: (b * tb + t, 0))
    in_specs, ops = [tab, tab], [cos, sin]
    for (kc, kp, vc, vp) in parts:
        in_specs += [full, full] + [pl.BlockSpec((STEPS, 256), shifted(g)) for g in with_prev] * 2
        ops += [kc, vc] + [kp] * n_p + [vp] * n_p
    return pl.pallas_call(
        body, name=name, grid=(B, tb), in_specs=in_specs, out_specs=[full, full],
        out_shape=[jax.ShapeDtypeStruct((T, MAIN_W), BF)] * 2,
        compiler_params=_cp(("parallel", "parallel")),
    )(*ops)


def _loss(y, target, *, name, tm=512):
    T, Dm = y.shape
    nt = T // tm

    def body(y_ref, t_ref, l_ref, d_ref, acc):
        i = pl.program_id(0)
        err = y_ref[...] - t_ref[...]
        d_ref[...] = err / Dm
        part = jnp.sum(jnp.mean(err * err, axis=-1, keepdims=True).reshape(tm // 8, 8, 1), axis=0)

        @pl.when(i == 0)
        def _():
            acc[...] = part

        @pl.when(i > 0)
        def _():
            acc[...] += part

        @pl.when(i == nt - 1)
        def _():
            l_ref[...] = 0.5 * jnp.sum(acc[...], axis=0, keepdims=True)

    row = pl.BlockSpec((tm, Dm), lambda i: (i, 0))
    return pl.pallas_call(
        body, name=name, grid=(nt,), in_specs=[row, row],
        out_specs=[pl.BlockSpec((1, 1), lambda i: (0, 0)), row],
        out_shape=[jax.ShapeDtypeStruct((1, 1), F32), jax.ShapeDtypeStruct((T, Dm), F32)],
        scratch_shapes=[pltpu.VMEM((8, 1), F32)],
        compiler_params=_cp(("arbitrary",)),
    )(y, target)


def _adamw(w, g, m, v, *, name):
    shape = w.shape
    cols = shape[-1]
    rows = w.size // cols
    tm = rows
    for cand in (512, 352, 256, 128):
        if rows > cand and rows % cand == 0 and cand * cols * 4 <= (1 << 20):
            tm = cand
            break

    def body(w_ref, g_ref, m_ref, v_ref, d_ref, mo_ref, vo_ref):
        gv = g_ref[...]
        mn = ADAM_B1 * m_ref[...] + (1.0 - ADAM_B1) * gv
        vn = ADAM_B2 * v_ref[...] + (1.0 - ADAM_B2) * (gv * gv)
        m_hat = mn / (1.0 - ADAM_B1 ** ADAM_STEP)
        v_hat = vn / (1.0 - ADAM_B2 ** ADAM_STEP)
        d_ref[...] = -ADAM_LR * (m_hat / (jnp.sqrt(v_hat) + ADAM_EPS) + ADAM_WD * w_ref[...])
        mo_ref[...] = mn
        vo_ref[...] = vn

    blk = pl.BlockSpec((tm, cols), lambda i: (i, 0))
    outs = pl.pallas_call(
        body, name=name, grid=(rows // tm,), in_specs=[blk] * 4, out_specs=[blk] * 3,
        out_shape=[jax.ShapeDtypeStruct((rows, cols), F32)] * 3,
        compiler_params=_cp(("parallel",)),
    )(*[t.reshape(rows, cols) for t in (w, g, m, v)])
    return tuple(t.reshape(shape) for t in outs)


def _adamw_layer(name, l, w, g, m, v, prev):
    L, rows, cols = w.shape
    tm = rows
    for cand in (512, 352, 256, 176, 128, 64):
        if rows % cand == 0 and cand * cols * 4 <= (1 << 20):
            tm = cand
            break
    if prev is None:
        prev = tuple(lax.empty(w.shape, F32) for _ in range(4))

    def body(w_ref, g_ref, m_ref, v_ref, *rest):
        d_ref, mo_ref, vo_ref, go_ref = rest[4:]
        gv = g_ref[...]
        mn = ADAM_B1 * m_ref[...] + (1.0 - ADAM_B1) * gv
        vn = ADAM_B2 * v_ref[...] + (1.0 - ADAM_B2) * (gv * gv)
        m_hat = mn / (1.0 - ADAM_B1 ** ADAM_STEP)
        v_hat = vn / (1.0 - ADAM_B2 ** ADAM_STEP)
        d_ref[...] = -ADAM_LR * (m_hat / (jnp.sqrt(v_hat) + ADAM_EPS) + ADAM_WD * w_ref[...])
        mo_ref[...] = mn
        vo_ref[...] = vn
        go_ref[...] = gv

    lay = pl.BlockSpec((None, tm, cols), lambda i: (l, i, 0))
    one = pl.BlockSpec((None, tm, cols), lambda i: (0, i, 0))
    return tuple(pl.pallas_call(
        body, name=f"l{l}_adamw_{name}", grid=(rows // tm,),
        in_specs=[lay, one, lay, lay] + [pl.BlockSpec(memory_space=pl.ANY)] * 4, out_specs=[lay] * 4,
        out_shape=[jax.ShapeDtypeStruct(w.shape, F32)] * 4,
        input_output_aliases={4 + i: i for i in range(4)},
        compiler_params=_cp(("parallel",)),
    )(w, g, m, v, *prev))


BIG = {
    'w_in': ((DEPTH, D_MODEL, D_MODEL), 'row'),
    'w_mem_kv': ((DEPTH, D_MODEL, 2 * MEM_W), 'row'),
    'w_out': ((DEPTH, D_MODEL, D_MODEL), 'row'),
    'w_kv': ((1, D_MODEL, 2 * MAIN_W), 'col'),
    'w_gate_up': ((DEPTH, D_MODEL, 2 * D_FF), 'col'),
    'w_down': ((DEPTH, D_FF, D_MODEL), 'row'),
}
BIG_NAMES = tuple(BIG)
N_CHIPS = 4
HBM_ANY = pl.BlockSpec(memory_space=pl.ANY)


def _geom(name):
    (L, R, C), kind = BIG[name]
    if kind == 'row':
        return L, R, C, kind, R // N_CHIPS, C, R // (2 * N_CHIPS)
    return L, R, C, kind, R, C // N_CHIPS, R // 2


def _shard_shape(name):
    L, R, C, kind, rs, cs, rh = _geom(name)
    return (L, rs, cs)


def _half_shape(name):
    L, R, C, kind, rs, cs, rh = _geom(name)
    return (L, rh, cs)


def _full_win(ref, name, s, h):
    L, R, C, kind, rs, cs, rh = _geom(name)
    if kind == 'row':
        rows = pl.ds(s * rs, rs) if h is None else pl.ds(s * rs + h * rh, rh)
        return ref.at[:, rows, :]
    rows = slice(None) if h is None else pl.ds(h * rh, rh)
    return ref.at[:, rows, pl.ds(s * cs, cs)]


def _shard_half(ref, name, h):
    L, R, C, kind, rs, cs, rh = _geom(name)
    return ref.at[:, pl.ds(h * rh, rh), :]


def _halves_win(ref, name, s):
    L, R, C, kind, rs, cs, rh = _geom(name)
    if kind == 'row':
        return ref.at[:, pl.ds(s * rh, rh), :]
    return ref.at[:, :, pl.ds(s * cs, cs)]


def _halves_shape(name):
    L, R, C, kind, rs, cs, rh = _geom(name)
    return (L, N_CHIPS * rh, cs) if kind == 'row' else (L, rh, C)


def _place():
    x, y, c = lax.axis_index("x"), lax.axis_index("y"), lax.axis_index("c")
    chips = [(1 - x, y), (x, 1 - y), (1 - x, 1 - y)]
    return x, y, c, chips


SMALL_ROWS = 24


def _all_gather(shards, small):
    names = BIG_NAMES
    nw = len(names)

    def body(*refs):
        src = dict(zip(names, refs[:nw]))
        small_ref = refs[nw]
        dst = dict(zip(names, refs[nw + 1:2 * nw + 1]))
        small_out = refs[2 * nw + 1]
        send_sems, recv_sems, local_sems = refs[2 * nw + 2:]
        x, y, c, chips = _place()
        s = 2 * x + y
        sib = (x, y, 1 - c)

        def remote(k, src_ref, dst_ref, to):
            return pltpu.make_async_remote_copy(src_ref=src_ref, dst_ref=dst_ref, send_sem=send_sems.at[k],
                                                recv_sem=recv_sems.at[k], device_id=to, device_id_type=MESH)

        local = []
        for wi, nm in enumerate(names):
            local.append(pltpu.make_async_copy(src[nm], _full_win(dst[nm], nm, s, None), local_sems.at[wi]))
        local.append(pltpu.make_async_copy(small_ref, small_out.at[s], local_sems.at[nw]))
        for cp in local:
            cp.start()
        sends = []
        for j, (px, py) in enumerate(chips):
            for wi, nm in enumerate(names):
                sends.append(remote(wi * 6 + j, _shard_half(src[nm], nm, c), _full_win(dst[nm], nm, s, c), (px, py, c)))
            sends.append(remote(nw * 6 + j, small_ref, small_out.at[s], (px, py, c)))
        for cp in sends:
            cp.start()
        for j, (px, py) in enumerate(chips):
            sp = 2 * px + py
            for wi, nm in enumerate(names):
                w = _full_win(dst[nm], nm, sp, c)
                remote(wi * 6 + j, w, w, sib).wait_recv()
                fwd = remote(wi * 6 + 3 + j, w, w, sib)
                fwd.start()
                sends.append(fwd)
            remote(nw * 6 + j, small_ref, small_out.at[sp], sib).wait_recv()
        for j, (px, py) in enumerate(chips):
            sp = 2 * px + py
            for wi, nm in enumerate(names):
                w = _full_win(dst[nm], nm, sp, 1 - c)
                remote(wi * 6 + 3 + j, w, w, sib).wait_recv()
        for cp in sends:
            cp.wait_send()
        for cp in local:
            cp.wait()

    n_sem = nw * 6 + 3
    outs = pl.pallas_call(
        body, name="all_gather_weights",
        in_specs=[HBM_ANY] * (nw + 1), out_specs=[HBM_ANY] * (nw + 1),
        out_shape=[jax.ShapeDtypeStruct(BIG[nm][0], BF) for nm in names]
        + [jax.ShapeDtypeStruct((N_CHIPS, SMALL_ROWS, 256), F32)],
        scratch_shapes=[pltpu.SemaphoreType.DMA((n_sem,)), pltpu.SemaphoreType.DMA((n_sem,)),
                        pltpu.SemaphoreType.DMA((nw + 1,))],
    )(*[shards[nm] for nm in names], small)
    return dict(zip(names, outs[:nw])), outs[nw]


SEM_SPEC = pl.BlockSpec(memory_space=pltpu.SEMAPHORE)
HBM_SPEC = pl.BlockSpec(memory_space=pltpu.HBM)
DATAFLOW = pltpu.SideEffectType.DATAFLOW_SIDE_EFFECTING


def _in_hbm(a):
    return pltpu.with_memory_space_constraint(a, pltpu.HBM)


def _remote(src, dst, send_sems, recv_sems, k, to):
    return pltpu.make_async_remote_copy(src_ref=src, dst_ref=dst, send_sem=send_sems.at[k], recv_sem=recv_sems.at[k],
                                        device_id=to, device_id_type=MESH)


def _split_start(name, bufs, n_copies, sends):
    nb = len(bufs)

    def body(*refs):
        in_refs = refs[:nb]
        send_sems, recv_sems = refs[nb], refs[nb + 1]
        token = refs[-1]
        for k, (src, dst, to) in enumerate(sends(in_refs)):
            _remote(src, dst, send_sems, recv_sems, k, to).start()
        token[...] = jnp.zeros_like(token)

    outs = pl.pallas_call(
        body, name=name,
        out_shape=(pltpu.SemaphoreType.DMA((n_copies,)), pltpu.SemaphoreType.DMA((n_copies,)),
                   *[pltpu.HBM(b.shape, b.dtype) for b in bufs], jax.ShapeDtypeStruct((8, 128), F32)),
        in_specs=[HBM_SPEC] * nb,
        out_specs=(SEM_SPEC, SEM_SPEC, *[HBM_SPEC] * nb, pl.BlockSpec(memory_space=pltpu.VMEM)),
        input_output_aliases={i: 2 + i for i in range(nb)},
        compiler_params=pltpu.CompilerParams(has_side_effects=DATAFLOW),
    )(*[_in_hbm(b) for b in bufs])
    return outs[0], outs[1], list(outs[2:2 + nb]), outs[-1]


def _split_wait(name, send_sems, recv_sems, bufs, after, sends, arrivals):
    nb = len(bufs)

    def body(*refs):
        in_refs = refs[:nb]
        s_sems, r_sems = refs[nb], refs[nb + 1]
        me = (lax.axis_index("x"), lax.axis_index("y"), lax.axis_index("c"))
        for k, (src, dst, to) in enumerate(sends(in_refs)):
            _remote(src, dst, s_sems, r_sems, k, to).wait_send()
        for k, win in enumerate(arrivals(in_refs)):
            _remote(win, win, s_sems, r_sems, k, me).wait_recv()

    outs = pl.pallas_call(
        body, name=name,
        out_shape=[pltpu.HBM(b.shape, b.dtype) for b in bufs],
        in_specs=[HBM_SPEC] * nb + [SEM_SPEC, SEM_SPEC, HBM_ANY],
        out_specs=[HBM_SPEC] * nb,
        input_output_aliases={i: i for i in range(nb)},
        compiler_params=pltpu.CompilerParams(has_side_effects=DATAFLOW),
    )(*bufs, send_sems, recv_sems, after)
    return list(outs)


LAYER_W = ('w_in', 'w_mem_kv', 'w_out', 'w_gate_up', 'w_down')


def _place_own(l, names, shards, small, sc):
    nw = len(names)
    has_small = small is not None
    n_ops = nw + (1 if has_small else 0)

    def body(sc_ref, *refs):
        for src, dst in zip(refs[:n_ops], refs[n_ops:]):
            dst[...] = src[...]

    in_specs, out_specs, out_shape, ops = [], [], [], list(shards)
    for nm in names:
        L, R, C, kind, rs, cs, rh = _geom(nm)
        in_specs.append(pl.BlockSpec((1, rs, cs), lambda i, sc_ref: (0, 0, 0)))
        if kind == 'row':
            out_specs.append(pl.BlockSpec((1, rs, cs), lambda i, sc_ref: (0, sc_ref[0], 0)))
        else:
            out_specs.append(pl.BlockSpec((1, rs, cs), lambda i, sc_ref: (0, 0, sc_ref[0])))
        out_shape.append(jax.ShapeDtypeStruct((1, R, C), BF))
    if has_small:
        in_specs.append(pl.BlockSpec((SMALL_ROWS, 256), lambda i, sc_ref: (0, 0)))
        out_specs.append(pl.BlockSpec((None, SMALL_ROWS, 256), lambda i, sc_ref: (sc_ref[0], 0, 0)))
        out_shape.append(jax.ShapeDtypeStruct((N_CHIPS, SMALL_ROWS, 256), F32))
        ops.append(small)
    return pl.pallas_call(
        body, name=f"l{l}_place_own_shard",
        grid_spec=pltpu.PrefetchScalarGridSpec(num_scalar_prefetch=1, grid=(1,), in_specs=in_specs, out_specs=out_specs),
        out_shape=out_shape,
        compiler_params=_cp(("arbitrary",)),
    )(sc, *ops)


def _gather_start(l, names, shards, small, sc):
    nw = len(names)
    has_small = small is not None
    fulls = _place_own(l, names, shards, small, sc)
    bufs = list(shards) + ([small] if has_small else []) + list(fulls)
    n_src = nw + (1 if has_small else 0)

    def sends(refs):
        x, y, c, chips = _place()
        s = 2 * x + y
        out = []
        for (px, py) in chips:
            for wi, nm in enumerate(names):
                out.append((_shard_half(refs[wi], nm, c), _full_win(refs[n_src + wi], nm, s, c), (px, py, c)))
            if has_small:
                out.append((refs[nw], refs[n_src + nw].at[s], (px, py, c)))
        return out

    def arrivals(refs):
        x, y, c, chips = _place()
        out = []
        for (px, py) in chips:
            sp = 2 * px + py
            for wi, nm in enumerate(names):
                out.append(_full_win(refs[n_src + wi], nm, sp, c))
            if has_small:
                out.append(refs[n_src + nw].at[sp])
        return out

    n_copies = 3 * n_src
    send_sems, recv_sems, bufs, token = _split_start(f"l{l}_gather_ici_start", bufs, n_copies, sends)
    return dict(l=l, names=names, has_small=has_small, sems=(send_sems, recv_sems), bufs=bufs, sends=sends,
                arrivals=arrivals, token=token)


def _gather_forward(st, after):
    l, names = st['l'], st['names']
    nw = len(names)
    n_src = nw + (1 if st['has_small'] else 0)
    bufs = _split_wait(f"l{l}_gather_ici_wait", *st['sems'], st['bufs'], after, st['sends'], st['arrivals'])
    fulls = bufs[n_src:n_src + nw]
    small_all = bufs[n_src + nw] if st['has_small'] else None

    def sends(refs):
        x, y, c, chips = _place()
        out = []
        for (px, py) in chips:
            sp = 2 * px + py
            for wi, nm in enumerate(names):
                w = _full_win(refs[wi], nm, sp, c)
                out.append((w, w, (x, y, 1 - c)))
        return out

    def arrivals(refs):
        x, y, c, chips = _place()
        out = []
        for (px, py) in chips:
            sp = 2 * px + py
            for wi, nm in enumerate(names):
                out.append(_full_win(refs[wi], nm, sp, 1 - c))
        return out

    send_sems, recv_sems, fulls, token = _split_start(f"l{l}_gather_d2d_start", fulls, 3 * nw, sends)
    return dict(l=l, names=names, sems=(send_sems, recv_sems), bufs=fulls, sends=sends, arrivals=arrivals,
                small_all=small_all, token=token)


def _gather_finish(st, after):
    fulls = _split_wait(f"l{st['l']}_gather_d2d_wait", *st['sems'], st['bufs'], after, st['sends'], st['arrivals'])
    return dict(zip(st['names'], fulls)), st['small_all']


def _reduce_start(tag, names, grads):
    nw = len(names)
    recv = [lax.empty((1,) + _halves_shape(nm)[1:], F32) for nm in names]
    bufs = [grads[nm] for nm in names] + recv

    def windows(refs, half_of):
        x, y, c, _ = _place()
        h = half_of(c)
        out = []
        for wi, nm in enumerate(names):
            L, R, C, kind, rs, cs, rh = _geom(nm)
            if kind == 'row':
                for sp in range(N_CHIPS):
                    out.append((_full_win(refs[wi], nm, sp, h), _halves_win(refs[nw + wi], nm, sp)))
            else:
                out.append((refs[wi].at[:, pl.ds(h * rh, rh), :], refs[nw + wi]))
        return out

    def sends(refs):
        x, y, c, _ = _place()
        return [(src, dst, (x, y, 1 - c)) for src, dst in windows(refs, lambda c: 1 - c)]

    def arrivals(refs):
        return [dst for _, dst in windows(refs, lambda c: c)]

    n_copies = sum(N_CHIPS if BIG[nm][1] == 'row' else 1 for nm in names)
    send_sems, recv_sems, bufs, token = _split_start(tag + "_halves_start", bufs, n_copies, sends)
    return dict(tag=tag, names=names, sems=(send_sems, recv_sems), bufs=bufs, sends=sends, arrivals=arrivals, token=token)


def _reduce_mid(st, after, sc):
    tag, names = st['tag'], st['names']
    nw = len(names)
    bufs = _split_wait(tag + "_halves_wait", *st['sems'], st['bufs'], after, st['sends'], st['arrivals'])
    halves, own = [], []
    for wi, nm in enumerate(names):
        hb, ow = _add_halves(nm, bufs[wi], bufs[nw + wi], sc, tag)
        halves.append(hb)
        own.append(ow)
    pieces = [lax.empty((3, 1) + _half_shape(nm)[1:], BF) for nm in names]

    def sends(refs):
        x, y, c, chips = _place()
        out = []
        for j, (px, py) in enumerate(chips):
            for wi, nm in enumerate(names):
                out.append((_halves_win(refs[wi], nm, 2 * px + py), refs[nw + wi].at[j], (px, py, c)))
        return out

    def arrivals(refs):
        return [refs[nw + wi].at[j] for j in range(3) for wi in range(nw)]

    send_sems, recv_sems, bufs, token = _split_start(tag + "_pieces_start", halves + pieces, 3 * nw, sends)
    return dict(tag=tag, names=names, sems=(send_sems, recv_sems), bufs=bufs, sends=sends, arrivals=arrivals, own=own,
                token=token)


def _reduce_late(st, after, sc):
    tag, names = st['tag'], st['names']
    nw = len(names)
    bufs = _split_wait(tag + "_pieces_wait", *st['sems'], st['bufs'], after, st['sends'], st['arrivals'])
    gsh = [_sum_pieces(nm, st['own'][wi], bufs[nw + wi], sc, tag) for wi, nm in enumerate(names)]

    def sends(refs):
        x, y, c, _ = _place()
        return [(_shard_half(refs[wi], nm, c), _shard_half(refs[wi], nm, c), (x, y, 1 - c)) for wi, nm in enumerate(names)]

    def arrivals(refs):
        x, y, c, _ = _place()
        return [_shard_half(refs[wi], nm, 1 - c) for wi, nm in enumerate(names)]

    send_sems, recv_sems, bufs, token = _split_start(tag + "_share_start", gsh, nw, sends)
    return dict(tag=tag, names=names, sems=(send_sems, recv_sems), bufs=bufs, sends=sends, arrivals=arrivals, token=token)


def _reduce_finish(st, after):
    gsh = _split_wait(st['tag'] + "_share_wait", *st['sems'], st['bufs'], after, st['sends'], st['arrivals'])
    return dict(zip(st['names'], gsh))


def _add_halves(name, g, r, sc, tag):
    _, R, C, kind, rs, cs, rh = _geom(name)
    L = g.shape[0]
    tr = rh if kind == 'row' else 256
    nr = rh // tr

    def body(sc_ref, g_ref, r_ref, hb_ref, own_ref):
        sp = pl.program_id(2)
        tot = g_ref[...] + r_ref[...]
        hb_ref[...] = tot.astype(hb_ref.dtype)

        @pl.when(sp == sc_ref[0])
        def _():
            own_ref[...] = tot

    if kind == 'row':
        g_map = lambda l, ri, sp, sc_ref: (l, sp * 2 + sc_ref[1], 0)
        h_map = lambda l, ri, sp, sc_ref: (l, sp, 0)
    else:
        g_map = lambda l, ri, sp, sc_ref: (l, sc_ref[1] * nr + ri, sp)
        h_map = lambda l, ri, sp, sc_ref: (l, ri, sp)
    own_map = lambda l, ri, sp, sc_ref: (l, ri, 0)
    blk = (None, tr, cs)
    return pl.pallas_call(
        body, name=tag + "_add_halves_" + name,
        grid_spec=pltpu.PrefetchScalarGridSpec(
            num_scalar_prefetch=1, grid=(L, nr, N_CHIPS),
            in_specs=[pl.BlockSpec(blk, g_map), pl.BlockSpec(blk, h_map)],
            out_specs=[pl.BlockSpec(blk, h_map), pl.BlockSpec(blk, own_map)]),
        out_shape=[jax.ShapeDtypeStruct((L,) + _halves_shape(name)[1:], BF),
                   jax.ShapeDtypeStruct((L,) + _half_shape(name)[1:], F32)],
        compiler_params=_cp(("parallel", "parallel", "arbitrary")),
    )(sc, g, r)


def _sum_pieces(name, own, pieces, sc, tag):
    _, R, C, kind, rs, cs, rh = _geom(name)
    L = own.shape[0]
    tr = rh if kind == 'row' else 256
    nr = rh // tr

    def body(sc_ref, o_ref, p_ref, out_ref):
        out_ref[...] = o_ref[...] + p_ref[0].astype(F32) + p_ref[1].astype(F32) + p_ref[2].astype(F32)

    blk = (None, tr, cs)
    return pl.pallas_call(
        body, name=tag + "_sum_pieces_" + name,
        grid_spec=pltpu.PrefetchScalarGridSpec(
            num_scalar_prefetch=1, grid=(L, nr),
            in_specs=[pl.BlockSpec(blk, lambda l, ri, sc_ref: (l, ri, 0)),
                      pl.BlockSpec((3, None, tr, cs), lambda l, ri, sc_ref: (0, l, ri, 0))],
            out_specs=pl.BlockSpec(blk, lambda l, ri, sc_ref: (l, sc_ref[1] * nr + ri, 0))),
        out_shape=jax.ShapeDtypeStruct((L,) + _shard_shape(name)[1:], F32),
        compiler_params=_cp(("parallel", "parallel")),
    )(sc, own, pieces)


def _all_gather_small(v):
    rows = v.shape[0]

    def body(v_ref, out_ref, send_sems, recv_sems, local_sem):
        x, y, c, _ = _place()
        me = 4 * x + 2 * y + c
        mine = pltpu.make_async_copy(v_ref, out_ref.at[me], local_sem)
        mine.start()
        sends = []
        flips = [(fx, fy, fc) for fx in (0, 1) for fy in (0, 1) for fc in (0, 1)][1:]
        for k, (fx, fy, fc) in enumerate(flips):
            px, py, pc = (1 - x if fx else x), (1 - y if fy else y), (1 - c if fc else c)
            cp = pltpu.make_async_remote_copy(src_ref=v_ref, dst_ref=out_ref.at[me], send_sem=send_sems.at[k],
                                              recv_sem=recv_sems.at[k], device_id=(px, py, pc), device_id_type=MESH)
            cp.start()
            sends.append((cp, 4 * px + 2 * py + pc))
        for k, (cp, peer) in enumerate(sends):
            pltpu.make_async_remote_copy(src_ref=v_ref, dst_ref=out_ref.at[peer], send_sem=send_sems.at[k],
                                         recv_sem=recv_sems.at[k], device_id=(x, y, c), device_id_type=MESH).wait_recv()
        for cp, _ in sends:
            cp.wait_send()
        mine.wait()

    return pl.pallas_call(
        body, name="all_gather_small_grads",
        in_specs=[HBM_ANY], out_specs=HBM_ANY,
        out_shape=jax.ShapeDtypeStruct((8, rows, 128), F32),
        scratch_shapes=[pltpu.SemaphoreType.DMA((7,)), pltpu.SemaphoreType.DMA((7,)), pltpu.SemaphoreType.DMA],
    )(v)


def _sum8(v8, *, name, tr=336):
    rows = v8.shape[1]
    tr = min(tr, rows)
    assert rows % tr == 0

    def body(v_ref, o_ref):
        tot = v_ref[0]
        for d in range(1, 8):
            tot = tot + v_ref[d]
        o_ref[...] = tot

    return pl.pallas_call(
        body, name=name, grid=(rows // tr,),
        in_specs=[pl.BlockSpec((8, tr, 128), lambda i: (0, i, 0))], out_specs=pl.BlockSpec((tr, 128), lambda i: (i, 0)),
        out_shape=jax.ShapeDtypeStruct((rows, 128), F32),
        compiler_params=_cp(("parallel",)),
    )(v8)


def _block_diag(w_pool_l):
    wbd = jnp.zeros((MAIN_W, MAIN_W), F32)
    for gi in range(len(POOL_WINDOWS)):
        wbd = lax.dynamic_update_slice(wbd, w_pool_l[gi], (gi * POOL_GROUP, gi * POOL_GROUP))
    return wbd.astype(BF)


def _unpack_small(small_all):
    ng = small_all[:, :16, :].reshape(N_CHIPS, DEPTH, 4, 256).transpose(1, 2, 0, 3).reshape(DEPTH, 4, D_MODEL)
    ps = small_all[:, 16:18, :POOL_GROUP].transpose(1, 0, 2).reshape(N_A, MAIN_W)
    return ng, ps


def _local_step(x, mem, positions, weights_of, on_backward, mem_norm, w_pool, kv_norm, target):
    B, S, _ = x.shape
    T = B * S
    xc = x.reshape(T, D_MODEL)
    memf = mem.reshape(B * N_MEM, D_MODEL)
    tgt = target.reshape(T, D_MODEL)
    cos, sin = _rope_tables(positions.reshape(T, 1), name="rope_tables")
    wbd = [_block_diag(w_pool[l]) for l in range(N_A)]
    nbo = D_FF // 256
    fw = []
    rk = rv = None
    kv_saved = None
    w0, small_all = weights_of(0, None, None)
    norm_gains, pool_scale = _unpack_small(small_all)
    wts = [w0]
    y1 = y2 = None

    for l in range(DEPTH):
        t = f"l{l}_"
        if l > 0:
            wts.append(weights_of(l, y1, y2)[0])
        sv = {'x_in': xc}
        h0, sv['r0'] = _norm_fwd(xc, norm_gains[l, 0], name=t + "norm0", out_dtype=BF)
        z, = _mm(h0, wts[l]['w_in'], b_layer=0, name=t + "mm_in")
        memn, sv['rm'] = _norm_fwd(memf, mem_norm[l], name=t + "norm_mem", out_dtype=BF, tm=256)
        kvm, = _mm(memn, wts[l]['w_mem_kv'], b_layer=0, name=t + "mm_memkv", out_dtypes=(BF,))
        if l < N_A:
            ycat, sv['p'] = _pool_fwd(z, wbd[l], pool_scale[l], B, S, name=t + "pool_fwd")
        else:
            rq = _rope_apply(z, cos, sin, name=t + "rope_q", out_dtype=F32)
            o = lax.empty((T, MAIN_W), F32)
            lse = lax.empty((T, MAIN_W), F32)
            for g in range(3):
                o, lse = _dil_fwd_group(g, rq, rk, rv, o, lse, B, S, name=t + f"dil_fwd{g}")
            ycat = _dil_combine_fwd(o, lse, lax.empty((T, D_MODEL), BF), name=t + "dil_combine")
            sv.update(rq=rq, o=o, lse=lse)
        ycat, sv['lse_m'] = _memattn_fwd(z, kvm, ycat, B, S, name=t + "memattn_fwd")
        y1, = _mm(ycat, wts[l]['w_out'], b_layer=0, name=t + "mm_out")
        x1, sv['r1'] = _norm_fwd(y1, norm_gains[l, 1], name=t + "norm1", res=xc)
        h2, sv['r2'] = _norm_fwd(x1, norm_gains[l, 2], name=t + "norm2", out_dtype=BF)
        gg, uu, aa = _mm(h2, wts[l]['w_gate_up'], b_layer=0, b_offsets=(0, nbo), out_n=D_FF, tn=256, name=t + "mm_gate_up",
                         epilogue=_swiglu_fwd_epilogue, out_dtypes=(BF, BF, BF))
        y2, = _mm(aa, wts[l]['w_down'], b_layer=0, tk=D_FF, name=t + "mm_down")
        x2, sv['r3'] = _norm_fwd(y2, norm_gains[l, 3], name=t + "norm3", res=x1)
        sv.update(h0=h0, z=z, memn=memn, kvm=kvm, ycat=ycat, y1=y1, x1=x1, h2=h2, gg=gg, uu=uu, aa=aa, y2=y2)
        fw.append(sv)
        xc = x2
        if l == N_A - 1:
            kvn, rkv = _norm_fwd(xc, kv_norm, name="norm_kv", out_dtype=BF)
            kv, = _mm(kvn, wts[N_A - 1]['w_kv'], b_layer=0, name="mm_kv")
            rk, rv = _rope_apply(kv, cos, sin, name="rope_k", passthrough=True, out_dtype=F32)
            kv_saved = (xc, kvn, rkv)

    loss, dx = _loss(xc, tgt, name="loss")

    d_ng = [[None] * 4 for _ in range(DEPTH)]
    d_memnorm = [None] * DEPTH
    d_wbd = [None] * N_A
    d_pscale = [None] * N_A
    d_kvnorm = None
    kv_parts = []
    tok = None

    def tied(vec, tok):
        return vec if tok is None else vec + tok[0, 0]

    for l in reversed(range(DEPTH)):
        t = f"l{l}_b_"
        sv = fw[l]
        gl = {}
        dy2, d_ng[l][3] = _norm_bwd(dx, sv['y2'], sv['r3'], tied(norm_gains[l, 3], tok), name=t + "norm3", out_dtype=BF)
        gl['w_down'], = _mm(sv['aa'], dy2, ta=True, tm=1408, name=t + "dw_down")
        dg, du = _mm(dy2, wts[l]['w_down'], tb=True, b_layer=0, tn=256, name=t + "d_act",
                     extras=((sv['gg'], 'tile'), (sv['uu'], 'tile')), epilogue=_swiglu_bwd_epilogue, out_dtypes=(BF, BF))
        gl['w_gate_up'], = _mm(sv['h2'], (dg, du), ta=True, tn=1408, tk=512, name=t + "dw_gate_up")
        dh2, = _mm((dg, du), wts[l]['w_gate_up'], tb=True, b_layer=0, tk=1408, name=t + "d_h2", out_dtypes=(BF,))
        dx1, d_ng[l][2] = _norm_bwd(dh2, sv['x1'], sv['r2'], norm_gains[l, 2], name=t + "norm2", add=dx)
        tok = on_backward('ffn', l, dx1, None)
        dy1, d_ng[l][1] = _norm_bwd(dx1, sv['y1'], sv['r1'], tied(norm_gains[l, 1], tok), name=t + "norm1", out_dtype=BF)
        gl['w_out'], = _mm(sv['ycat'], dy1, ta=True, name=t + "dw_out")
        dycat, = _mm(dy1, wts[l]['w_out'], tb=True, b_layer=0, name=t + "d_ycat")
        dz = lax.empty((T, D_MODEL), BF)
        dz, dkm, dvm = _memattn_bwd(dycat, sv['z'], sv['kvm'], sv['lse_m'], dz, B, S, name=t + "memattn")
        if l < N_A:
            dz, d_wbd[l], d_pscale[l] = _pool_bwd(dycat, sv['p'], wbd[l], pool_scale[l], dz, B, S, name=t + "pool")
        else:
            do, cb = _dil_combine_bwd(dycat, sv['o'], sv['lse'], name=t + "dil_combine")
            acc = tuple(lax.empty((T, MAIN_W), F32) for _ in range(5))
            for g in range(3):
                acc = _dil_bwd_group(g, sv['rq'], rk, rv, do, cb, sv['lse'], acc, B, S, name=t + f"dil{g}")
            dz = _rope_apply(acc[0], cos, sin, name=t + "rope_q", sign=-1.0, alias=dz)
            kv_parts.append(acc[1:])
        tok = on_backward('mix', l, dz, None)
        gl['w_in'], = _mm(sv['h0'], dz, ta=True, name=t + "dw_in")
        dh0, = _mm(dz, wts[l]['w_in'], tb=True, b_layer=0, name=t + "d_h0", out_dtypes=(BF,))
        dx, d_ng[l][0] = _norm_bwd(dh0, sv['x_in'], sv['r0'], tied(norm_gains[l, 0], tok), name=t + "norm0", add=dx1)
        gl['w_mem_kv'], = _mm(sv['memn'], (dkm, dvm), ta=True, tn=256, name=t + "dw_memkv")
        dmemn, = _mm((dkm, dvm), wts[l]['w_mem_kv'], tb=True, b_layer=0, tk=256, name=t + "d_memn", out_dtypes=(BF,))
        _, d_memnorm[l] = _norm_bwd(dmemn, memf, sv['rm'], mem_norm[l], name=t + "norm_mem", out_dtype=BF, tm=256)
        if l == N_A:
            dk, dv = _kv_grad(kv_parts, cos, sin, B, S, name="kv_grad")
            x_kv, kvn, rkv = kv_saved
            gl['w_kv'], = _mm(kvn, (dk, dv), ta=True, tn=768, name="dw_kv")
            dkvn, = _mm((dk, dv), wts[N_A - 1]['w_kv'], tb=True, b_layer=0, tk=768, name="d_kvn", out_dtypes=(BF,))
            dx, d_kvnorm = _norm_bwd(dkvn, x_kv, rkv, kv_norm, name="norm_kv_b", add=dx)
        tok = on_backward('end', l, dx, {nm: g.reshape((1,) + g.shape) for nm, g in gl.items()})

    small = {
        'norm_gains': jnp.stack([jnp.concatenate(d_ng[l], axis=0) for l in range(DEPTH)]),
        'mem_norm': jnp.concatenate(d_memnorm, axis=0),
        'kv_norm': d_kvnorm.reshape(D_MODEL),
        'pool_scale': jnp.concatenate(d_pscale, axis=0),
        'w_pool': jnp.stack([jnp.stack([d_wbd[l][gi * POOL_GROUP:(gi + 1) * POOL_GROUP, gi * POOL_GROUP:(gi + 1) * POOL_GROUP]
                                        for gi in range(len(POOL_WINDOWS))]) for l in range(N_A)]),
    }
    return loss, dx, small


SMALL_ORDER = ('norm_gains', 'mem_norm', 'kv_norm', 'pool_scale', 'w_pool')
SMALL_VEC_ROWS = 2560


def kernel(x, mem, positions, norm_gains, mem_norm, w_in, w_mem_kv, w_out, w_pool, pool_scale, kv_norm, w_kv, w_gate_up, w_down, loss_target, m_norm_gains, m_mem_norm, m_w_in, m_w_mem_kv, m_w_out, m_w_pool, m_pool_scale, m_kv_norm, m_w_kv, m_w_gate_up, m_w_down, v_norm_gains, v_mem_norm, v_w_in, v_w_mem_kv, v_w_out, v_w_pool, v_pool_scale, v_kv_norm, v_w_kv, v_w_gate_up, v_w_down):
    xi, yi, ci = lax.axis_index("x"), lax.axis_index("y"), lax.axis_index("c")
    s = 2 * xi + yi
    sc = jnp.stack([s, ci]).astype(jnp.int32)
    weights = dict(norm_gains=norm_gains, mem_norm=mem_norm, w_in=w_in, w_mem_kv=w_mem_kv, w_out=w_out, w_pool=w_pool,
                   pool_scale=pool_scale, kv_norm=kv_norm, w_kv=w_kv, w_gate_up=w_gate_up, w_down=w_down)
    moms = dict(norm_gains=m_norm_gains, mem_norm=m_mem_norm, w_in=m_w_in, w_mem_kv=m_w_mem_kv, w_out=m_w_out,
                w_pool=m_w_pool, pool_scale=m_pool_scale, kv_norm=m_kv_norm, w_kv=m_w_kv, w_gate_up=m_w_gate_up,
                w_down=m_w_down)
    vels = dict(norm_gains=v_norm_gains, mem_norm=v_mem_norm, w_in=v_w_in, w_mem_kv=v_w_mem_kv, w_out=v_w_out,
                w_pool=v_w_pool, pool_scale=v_pool_scale, kv_norm=v_kv_norm, w_kv=v_w_kv, w_gate_up=v_w_gate_up,
                w_down=v_w_down)

    small_w = jnp.zeros((SMALL_ROWS, 256), F32)
    small_w = lax.dynamic_update_slice(small_w, norm_gains.reshape(16, 256), (0, 0))
    small_w = lax.dynamic_update_slice(small_w, pool_scale, (16, 0))
    started = []
    for l in range(DEPTH):
        names = LAYER_W + (('w_kv',) if l == N_A - 1 else ())
        shards = [weights[nm][l:l + 1].astype(BF) for nm in LAYER_W]
        if l == N_A - 1:
            shards.append(w_kv.astype(BF).reshape(_shard_shape('w_kv')))
        started.append(_gather_start(l, names, shards, small_w if l == 0 else None, sc))
    all_started = started[0]['token'] + started[1]['token'] + started[2]['token'] + started[3]['token']

    def weights_of(l, mid, end):
        fwd = _gather_forward(started[l], all_started if mid is None else mid)
        return _gather_finish(fwd, fwd['token'] if end is None else end)

    pending, reduced = {}, {}

    def on_backward(where, l, after, grads):
        prev = l + 1
        if where == 'ffn' and prev in pending:
            pending[prev] = _reduce_mid(pending[prev], after, sc)
            return pending[prev]['token']
        if where == 'mix' and prev in pending:
            pending[prev] = _reduce_late(pending[prev], after, sc)
            return pending[prev]['token']
        if where == 'end':
            if prev in pending:
                reduced[prev] = _reduce_finish(pending.pop(prev), after)
            pending[l] = _reduce_start(f"l{l}_grads", tuple(grads), grads)
            return pending[l]['token']
        return None

    loss, gx, gsmall = _local_step(x, mem, positions, weights_of, on_backward, mem_norm, w_pool, kv_norm, loss_target)
    loss = lax.psum(loss[0, 0], ("x", "y", "c"))

    outs = {nm: None for nm in LAYER_W}

    def adamw_layers(layers):
        for l in layers:
            for nm in LAYER_W:
                outs[nm] = _adamw_layer(nm, l, weights[nm], reduced[l][nm], moms[nm], vels[nm], outs[nm])

    st = pending.pop(0)
    st = _reduce_mid(st, st['token'], sc)
    adamw_layers(range(DEPTH - 1, 0, -1))
    st = _reduce_late(st, outs[LAYER_W[-1]][0], sc)

    vec = jnp.concatenate([gsmall[nm].reshape(-1) for nm in SMALL_ORDER])
    vec = jnp.pad(vec, (0, SMALL_VEC_ROWS * 128 - vec.shape[0])).reshape(SMALL_VEC_ROWS, 128)
    tot = _sum8(_all_gather_small(vec), name="sum_small_grads", tr=512)
    reduced[0] = _reduce_finish(st, tot)
    adamw_layers([0])
    tot = tot.reshape(-1)
    grads, off = {}, 0
    for nm in SMALL_ORDER:
        shape = (DEPTH, 4, D_MODEL) if nm == 'norm_gains' else (N_A, MAIN_W) if nm == 'pool_scale' else weights[nm].shape
        n = 1
        for dim in shape:
            n *= dim
        grads[nm] = tot[off:off + n].reshape(shape)
        off += n
    grads['norm_gains'] = lax.dynamic_slice(grads['norm_gains'], (0, 0, s * 256), (DEPTH, 4, 256))
    grads['pool_scale'] = lax.dynamic_slice(grads['pool_scale'], (0, s * POOL_GROUP), (N_A, POOL_GROUP))
    grads['w_kv'] = reduced[N_A]['w_kv'].reshape(w_kv.shape)

    order = ('norm_gains', 'mem_norm', 'w_in', 'w_mem_kv', 'w_out', 'w_pool', 'pool_scale', 'kv_norm', 'w_kv',
             'w_gate_up', 'w_down')
    deltas, new_m, new_v = {}, {}, {}
    for nm in order:
        if nm in LAYER_W:
            deltas[nm], new_m[nm], new_v[nm], grads[nm] = outs[nm]
        else:
            deltas[nm], new_m[nm], new_v[nm] = _adamw(weights[nm], grads[nm], moms[nm], vels[nm], name="adamw_" + nm)
    return (loss, gx.reshape(x.shape), *[grads[nm] for nm in order], *[deltas[nm] for nm in order],
            *[new_m[nm] for nm in order], *[new_v[nm] for nm in order])
```

```python
import functools

import jax
import jax.numpy as jnp
from jax import lax
from jax.experimental import pallas as pl
from jax.experimental.pallas import tpu as pltpu

F32 = jnp.float32
BF = jnp.bfloat16

D_MODEL = 1024
DEPTH = 4
N_A = 2
HEAD_DIM = 64
MEM_W = 256
MAIN_W = 768
D_FF = 2816
N_MEM = 256
POOL_WINDOWS = (2, 4, 8, 16)
POOL_GROUP = 192
DIL = (1, 4, 16)
STEPS = 128
ROPE_THETA = 10000.0
EPS = 1e-6
SCALE = HEAD_DIM ** -0.5
NEG = -1e30

ADAM_LR = 0.001
ADAM_B1 = 0.9
ADAM_B2 = 0.999
ADAM_EPS = 1e-08
ADAM_WD = 0.01
ADAM_STEP = 10

VMEM_LIMIT = 48 * 1024 * 1024
MESH = pl.DeviceIdType.MESH


def _cp(sem):
    return pltpu.CompilerParams(dimension_semantics=sem, vmem_limit_bytes=VMEM_LIMIT)


def _mm(a, b, *, name, ta=False, tb=False, tm=1024, tn=512, tk=1024, b_layer=None, b_offsets=(0,),
        extras=(), epilogue=None, out_dtypes=(F32,), out_n=None, stack=None):
    a_pair = isinstance(a, (tuple, list))
    b_pair = isinstance(b, (tuple, list))
    a0 = a[0] if a_pair else a
    b0 = b[0] if b_pair else b
    a_rows, a_cols = a0.shape
    if a_pair:
        a_cols *= 2
    b_rows, b_cols = b0.shape[-2:]
    if b_pair:
        b_cols *= 2
    M, K = (a_cols, a_rows) if ta else (a_rows, a_cols)
    N = b_rows if tb else b_cols
    if out_n is not None:
        N = out_n
    tm, tn, tk = min(tm, M), min(tn, N), min(tk, K)
    assert M % tm == 0 and N % tn == 0 and K % tk == 0, (name, M, N, K, tm, tn, tk)
    nk = K // tk
    n_acc = len(b_offsets)

    if a_pair:
        a_half = (a0.shape[1] // (tm if ta else tk))
    if b_pair:
        b_half = (b0.shape[1] // (tk if tb else tn))

    def a_map(sel):
        def f(i, j, k):
            r, c = (k, i) if ta else (i, k)
            if a_pair:
                c = jnp.clip(c - sel * a_half, 0, a_half - 1)
            return (r, c)
        return f

    def b_map(sel, off):
        def f(i, j, k):
            r, c = (j + off, k) if tb else (k, j + off)
            if b_pair:
                c = jnp.clip(c - sel * b_half, 0, b_half - 1)
            if b_layer is not None:
                return (b_layer, r, c)
            return (r, c)
        return f

    a_blk = (tk, tm) if ta else (tm, tk)
    b_blk = (tn, tk) if tb else (tk, tn)
    if b_layer is not None:
        b_blk = (None,) + b_blk
    in_specs, operands = [], []
    for sel in range(2 if a_pair else 1):
        in_specs.append(pl.BlockSpec(a_blk, a_map(sel)))
        operands.append(a[sel] if a_pair else a)
    n_a = len(operands)
    for off in b_offsets:
        for sel in range(2 if b_pair else 1):
            in_specs.append(pl.BlockSpec(b_blk, b_map(sel, off)))
            operands.append(b[sel] if b_pair else b)
    n_b = len(operands) - n_a
    for arr, kind in extras:
        if kind == 'tile':
            in_specs.append(pl.BlockSpec((tm, tn), lambda i, j, k: (i, j)))
        elif kind == 'row':
            in_specs.append(pl.BlockSpec((tm, 1), lambda i, j, k: (i, 0)))
        else:
            in_specs.append(pl.BlockSpec((1, tn), lambda i, j, k: (0, j)))
        operands.append(arr)
    n_e = len(extras)
    n_o = len(out_dtypes)
    dims = (((0,) if ta else (1,), (1,) if tb else (0,)), ((), ()))

    def body(*refs):
        a_refs = refs[:n_a]
        b_refs = refs[n_a:n_a + n_b]
        e_refs = refs[n_a + n_b:n_a + n_b + n_e]
        n_in = n_a + n_b + n_e + (1 if stack is not None else 0)
        o_refs = refs[n_in:n_in + n_o]
        acc_refs = refs[n_in + n_o:]
        i, j, k = pl.program_id(0), pl.program_id(1), pl.program_id(2)
        if a_pair:
            cidx = i if ta else k
            av = jnp.where(cidx < a_half, a_refs[0][...], a_refs[1][...])
        else:
            av = a_refs[0][...]
        av = av.astype(BF)
        prods = []
        for q in range(n_acc):
            if b_pair:
                cidx = (k if tb else j) + b_offsets[q]
                bv = jnp.where(cidx < b_half, b_refs[2 * q][...], b_refs[2 * q + 1][...])
            else:
                bv = b_refs[q][...]
            prods.append(lax.dot_general(av, bv.astype(BF), dims, preferred_element_type=F32))

        def finish(accs):
            outs = epilogue(accs, *[r[...] for r in e_refs]) if epilogue is not None else accs
            for o_ref, o in zip(o_refs, outs):
                o_ref[...] = o.astype(o_ref.dtype)

        if nk == 1:
            finish(prods)
        else:
            @pl.when(k == 0)
            def _():
                for r, p in zip(acc_refs, prods):
                    r[...] = p

            @pl.when(k > 0)
            def _():
                for r, p in zip(acc_refs, prods):
                    r[...] += p

            @pl.when(k == nk - 1)
            def _():
                finish([r[...] for r in acc_refs])

    if stack is not None:
        buf, layer = stack
        assert n_o == 1 and buf.shape[1:] == (M, N)
        return pl.pallas_call(
            body, name=name,
            grid=(M // tm, N // tn, nk),
            in_specs=in_specs + [pl.BlockSpec(memory_space=pl.ANY)],
            out_specs=[pl.BlockSpec((None, tm, tn), lambda i, j, k: (layer, i, j))],
            out_shape=[jax.ShapeDtypeStruct(buf.shape, buf.dtype)],
            scratch_shapes=[pltpu.VMEM((tm, tn), F32) for _ in range(n_acc if nk > 1 else 0)],
            input_output_aliases={len(operands): 0},
            compiler_params=_cp(("parallel", "parallel", "arbitrary")),
        )(*operands, buf)[0]
    return pl.pallas_call(
        body, name=name,
        grid=(M // tm, N // tn, nk),
        in_specs=in_specs,
        out_specs=[pl.BlockSpec((tm, tn), lambda i, j, k: (i, j)) for _ in range(n_o)],
        out_shape=[jax.ShapeDtypeStruct((M, N), dt) for dt in out_dtypes],
        scratch_shapes=[pltpu.VMEM((tm, tn), F32) for _ in range(n_acc if nk > 1 else 0)],
        compiler_params=_cp(("parallel", "parallel", "arbitrary")),
    )(*operands)


def _norm_fwd(x, g, *, name, res=None, out_dtype=F32, tm=512):
    T, Dm = x.shape
    has_res = res is not None

    def body(*refs):
        if has_res:
            x_ref, g_ref, r_ref, y_ref, s_ref = refs
        else:
            x_ref, g_ref, y_ref, s_ref = refs
        xv = x_ref[...]
        rstd = lax.rsqrt(jnp.mean(xv * xv, axis=-1, keepdims=True) + EPS)
        y = xv * rstd * g_ref[...]
        if has_res:
            y = r_ref[...] + y
        y_ref[...] = y.astype(y_ref.dtype)
        s_ref[...] = rstd

    row = pl.BlockSpec((tm, Dm), lambda i: (i, 0))
    in_specs = [row, pl.BlockSpec((1, Dm), lambda i: (0, 0))] + ([row] if has_res else [])
    ops = [x, g.reshape(1, Dm)] + ([res] if has_res else [])
    return pl.pallas_call(
        body, name=name, grid=(T // tm,), in_specs=in_specs,
        out_specs=[row, pl.BlockSpec((tm, 1), lambda i: (i, 0))],
        out_shape=[jax.ShapeDtypeStruct((T, Dm), out_dtype), jax.ShapeDtypeStruct((T, 1), F32)],
        compiler_params=_cp(("parallel",)),
    )(*ops)


def _norm_bwd(dout, x, rstd, g, *, name, add=None, out_dtype=F32, tm=512):
    T, Dm = x.shape
    has_add = add is not None
    nt = T // tm

    def body(*refs):
        if has_add:
            do_ref, x_ref, s_ref, g_ref, a_ref, dx_ref, dg_ref, acc = refs
        else:
            do_ref, x_ref, s_ref, g_ref, dx_ref, dg_ref, acc = refs
        i = pl.program_id(0)
        do = do_ref[...].astype(F32)
        xh = x_ref[...] * s_ref[...]
        gd = do * g_ref[...]
        dx = s_ref[...] * (gd - xh * jnp.mean(gd * xh, axis=-1, keepdims=True))
        if has_add:
            dx = dx + a_ref[...].astype(F32)
        dx_ref[...] = dx.astype(dx_ref.dtype)
        part = jnp.sum((do * xh).reshape(tm // 8, 8, Dm), axis=0)

        @pl.when(i == 0)
        def _():
            acc[...] = part

        @pl.when(i > 0)
        def _():
            acc[...] += part

        @pl.when(i == nt - 1)
        def _():
            dg_ref[...] = jnp.sum(acc[...], axis=0, keepdims=True)

    row = pl.BlockSpec((tm, Dm), lambda i: (i, 0))
    in_specs = [row, row, pl.BlockSpec((tm, 1), lambda i: (i, 0)), pl.BlockSpec((1, Dm), lambda i: (0, 0))]
    ops = [dout, x, rstd, g.reshape(1, Dm)]
    if has_add:
        in_specs.append(row)
        ops.append(add)
    return pl.pallas_call(
        body, name=name, grid=(nt,), in_specs=in_specs,
        out_specs=[row, pl.BlockSpec((1, Dm), lambda i: (0, 0))],
        out_shape=[jax.ShapeDtypeStruct((T, Dm), out_dtype), jax.ShapeDtypeStruct((1, Dm), F32)],
        scratch_shapes=[pltpu.VMEM((8, Dm), F32)],
        compiler_params=_cp(("arbitrary",)),
    )(*ops)


def _swiglu_fwd_epilogue(accs):
    g, u = accs
    return g, u, g * jax.nn.sigmoid(g) * u


def _swiglu_bwd_epilogue(accs, g, u):
    da = accs[0]
    g = g.astype(F32)
    u = u.astype(F32)
    sig = jax.nn.sigmoid(g)
    return da * u * (sig * (1.0 + g * (1.0 - sig))), da * (g * sig)


def _rope_tables(pos, *, name, tm=1024):
    T = pos.shape[0]
    half = HEAD_DIM // 2
    freqs = ROPE_THETA ** (-jnp.arange(half, dtype=F32) / half)
    freqs = jnp.tile(freqs, 4).reshape(1, 128)

    def body(p_ref, f_ref, c_ref, s_ref):
        ang = p_ref[...].astype(F32) * f_ref[...]
        lane = lax.broadcasted_iota(jnp.int32, ang.shape, 1)
        c_ref[...] = jnp.cos(ang)
        s_ref[...] = jnp.where(lane % HEAD_DIM < half, -1.0, 1.0) * jnp.sin(ang)

    tab = pl.BlockSpec((tm, 128), lambda i: (i, 0))
    return pl.pallas_call(
        body, name=name, grid=(T // tm,),
        in_specs=[pl.BlockSpec((tm, 1), lambda i: (i, 0)), pl.BlockSpec((1, 128), lambda i: (0, 0))],
        out_specs=[tab, tab],
        out_shape=[jax.ShapeDtypeStruct((T, 128), F32)] * 2,
        compiler_params=_cp(("parallel",)),
    )(pos, freqs)


def _rot(x, cos, sin, sign):
    W = x.shape[1]
    half = HEAD_DIM // 2
    reps = W // 128
    c = jnp.concatenate([cos] * reps, axis=1) if reps > 1 else cos
    s = jnp.concatenate([sin] * reps, axis=1) if reps > 1 else sin
    lane = lax.broadcasted_iota(jnp.int32, x.shape, 1)
    swapped = jnp.where(lane % HEAD_DIM < half, pltpu.roll(x, W - half, axis=1), pltpu.roll(x, half, axis=1))
    return x * c + (sign * s) * swapped


def _rope_apply(x, cos, sin, *, name, sign=1.0, width=MAIN_W, passthrough=False, out_dtype=BF, alias=None,
                out_cols=None, tm=512):
    T = x.shape[0]

    def body(*refs):
        if passthrough:
            x_ref, v_ref, c_ref, s_ref, o_ref, ov_ref = refs
            ov_ref[...] = v_ref[...].astype(ov_ref.dtype)
        elif alias is not None:
            x_ref, c_ref, s_ref, _, o_ref = refs
        else:
            x_ref, c_ref, s_ref, o_ref = refs
        o_ref[...] = _rot(x_ref[...].astype(F32), c_ref[...], s_ref[...], sign).astype(o_ref.dtype)

    blk0 = pl.BlockSpec((tm, width), lambda i: (i, 0))
    blk1 = pl.BlockSpec((tm, width), lambda i: (i, 1))
    tab = pl.BlockSpec((tm, 128), lambda i: (i, 0))
    if passthrough:
        return pl.pallas_call(
            body, name=name, grid=(T // tm,), in_specs=[blk0, blk1, tab, tab], out_specs=[blk0, blk0],
            out_shape=[jax.ShapeDtypeStruct((T, width), out_dtype)] * 2,
            compiler_params=_cp(("parallel",)),
        )(x, x, cos, sin)
    if alias is not None:
        return pl.pallas_call(
            body, name=name, grid=(T // tm,),
            in_specs=[blk0, tab, tab, pl.BlockSpec(memory_space=pl.ANY)], out_specs=blk0,
            out_shape=jax.ShapeDtypeStruct(alias.shape, alias.dtype),
            input_output_aliases={3: 0},
            compiler_params=_cp(("parallel",)),
        )(x, cos, sin, alias)
    return pl.pallas_call(
        body, name=name, grid=(T // tm,), in_specs=[blk0, tab, tab], out_specs=blk0,
        out_shape=jax.ShapeDtypeStruct((T, width), out_dtype),
        compiler_params=_cp(("parallel",)),
    )(x, cos, sin)


POOL_T = 256
POOL_HALO = 16


def _pool_lane_window(shape):
    lane = lax.broadcasted_iota(jnp.int32, shape, 1)
    w = jnp.full(shape, POOL_WINDOWS[0], jnp.int32)
    for gi in range(1, len(POOL_WINDOWS)):
        w = jnp.where(lane >= gi * POOL_GROUP, POOL_WINDOWS[gi], w)
    return w


def _pool_fwd(z, wbd, scale, B, S, *, name):
    T = z.shape[0]
    nt = S // POOL_T
    hb = POOL_T // POOL_HALO

    def body(z_ref, h_ref, w_ref, sc_ref, y_ref, p_ref, ext):
        i = pl.program_id(1)
        u = z_ref[...]
        ext[pl.ds(POOL_HALO, POOL_T), :] = u
        ext[pl.ds(0, POOL_HALO), :] = jnp.where(i > 0, h_ref[...], 0.0)
        win = _pool_lane_window((POOL_T, MAIN_W))
        acc = u
        for k in range(1, POOL_HALO):
            acc = acc + jnp.where(k < win, ext[pl.ds(POOL_HALO - k, POOL_T), :], 0.0)
        t = i * POOL_T + lax.broadcasted_iota(jnp.int32, (POOL_T, MAIN_W), 0)
        cnt = jnp.minimum(t + 1, win).astype(F32)
        p = (acc / cnt - u).astype(BF)
        p_ref[...] = p
        y = jnp.dot(p, w_ref[...], preferred_element_type=F32) * sc_ref[...]
        y_ref[...] = y.astype(y_ref.dtype)

    return pl.pallas_call(
        body, name=name, grid=(B, nt),
        in_specs=[pl.BlockSpec((POOL_T, MAIN_W), lambda b, i: (b * nt + i, 0)),
                  pl.BlockSpec((POOL_HALO, MAIN_W), lambda b, i: (jnp.maximum((b * nt + i) * hb - 1, 0), 0)),
                  pl.BlockSpec((MAIN_W, MAIN_W), lambda b, i: (0, 0)),
                  pl.BlockSpec((1, MAIN_W), lambda b, i: (0, 0))],
        out_specs=[pl.BlockSpec((POOL_T, MAIN_W), lambda b, i: (b * nt + i, 0)),
                   pl.BlockSpec((POOL_T, MAIN_W), lambda b, i: (b * nt + i, 0))],
        out_shape=[jax.ShapeDtypeStruct((T, D_MODEL), BF), jax.ShapeDtypeStruct((T, MAIN_W), BF)],
        scratch_shapes=[pltpu.VMEM((POOL_T + POOL_HALO, MAIN_W), F32)],
        compiler_params=_cp(("parallel", "parallel")),
    )(z, z, wbd, scale.reshape(1, MAIN_W))


def _pool_bwd(dy, p, wbd, scale, dz_alias, B, S, *, name):
    T = dy.shape[0]
    nt = S // POOL_T
    hb = POOL_T // POOL_HALO
    last_halo = T // POOL_HALO - 1
    R = POOL_T + POOL_HALO

    def body(dy_ref, dyn_ref, p_ref, pn_ref, w_ref, sc_ref, _, dz_ref, dw_ref, ds_ref, ext, dw_acc, ds_acc):
        b, i = pl.program_id(0), pl.program_id(1)
        first = jnp.logical_and(b == 0, i == 0)
        dyv = dy_ref[...]
        pv = p_ref[...]
        sc = sc_ref[...]
        w = w_ref[...]
        pw = jnp.dot(pv, w, preferred_element_type=F32)
        ds_part = jnp.sum((dyv * pw).reshape(POOL_T // 8, 8, MAIN_W), axis=0)
        dpw = (dyv * sc).astype(BF)
        dw_part = lax.dot_general(pv, dpw, (((0,), (0,)), ((), ())), preferred_element_type=F32)

        @pl.when(first)
        def _():
            dw_acc[...] = dw_part
            ds_acc[...] = ds_part

        @pl.when(jnp.logical_not(first))
        def _():
            dw_acc[...] += dw_part
            ds_acc[...] += ds_part

        @pl.when(jnp.logical_and(b == pl.num_programs(0) - 1, i == nt - 1))
        def _():
            dw_ref[...] = dw_acc[...]
            ds_ref[...] = jnp.sum(ds_acc[...], axis=0, keepdims=True)

        dp = lax.dot_general(dpw, w, (((1,), (1,)), ((), ())), preferred_element_type=F32)
        dpn = lax.dot_general((dyn_ref[...] * sc).astype(BF), w, (((1,), (1,)), ((), ())), preferred_element_type=F32)
        win = _pool_lane_window((POOL_T, MAIN_W))
        win_n = _pool_lane_window((POOL_HALO, MAIN_W))
        t = i * POOL_T + lax.broadcasted_iota(jnp.int32, (POOL_T, MAIN_W), 0)
        tn = (i + 1) * POOL_T + lax.broadcasted_iota(jnp.int32, (POOL_HALO, MAIN_W), 0)
        ext[pl.ds(0, POOL_T), :] = dp / jnp.minimum(t + 1, win).astype(F32)
        ext[pl.ds(POOL_T, POOL_HALO), :] = jnp.where(i < nt - 1, dpn / jnp.minimum(tn + 1, win_n).astype(F32), 0.0)
        acc = -dp
        for k in range(POOL_HALO):
            acc = acc + jnp.where(k < win, ext[pl.ds(k, POOL_T), :], 0.0)
        dz_ref[...] = acc.astype(dz_ref.dtype)

    cur = lambda b, i: (b * nt + i, 0)
    nxt = lambda b, i: (jnp.minimum((b * nt + i + 1) * hb, last_halo), 0)
    return pl.pallas_call(
        body, name=name, grid=(B, nt),
        in_specs=[pl.BlockSpec((POOL_T, MAIN_W), cur), pl.BlockSpec((POOL_HALO, MAIN_W), nxt),
                  pl.BlockSpec((POOL_T, MAIN_W), cur), pl.BlockSpec((POOL_HALO, MAIN_W), nxt),
                  pl.BlockSpec((MAIN_W, MAIN_W), lambda b, i: (0, 0)),
                  pl.BlockSpec((1, MAIN_W), lambda b, i: (0, 0)),
                  pl.BlockSpec(memory_space=pl.ANY)],
        out_specs=[pl.BlockSpec((POOL_T, MAIN_W), cur),
                   pl.BlockSpec((MAIN_W, MAIN_W), lambda b, i: (0, 0)),
                   pl.BlockSpec((1, MAIN_W), lambda b, i: (0, 0))],
        out_shape=[jax.ShapeDtypeStruct(dz_alias.shape, dz_alias.dtype),
                   jax.ShapeDtypeStruct((MAIN_W, MAIN_W), F32), jax.ShapeDtypeStruct((1, MAIN_W), F32)],
        scratch_shapes=[pltpu.VMEM((R, MAIN_W), F32), pltpu.VMEM((MAIN_W, MAIN_W), F32), pltpu.VMEM((8, MAIN_W), F32)],
        input_output_aliases={6: 0},
        compiler_params=_cp(("arbitrary", "arbitrary")),
    )(dy, dy, p, p, wbd, scale.reshape(1, MAIN_W), dz_alias)


def _head_masks(shape):
    lane = lax.broadcasted_iota(jnp.int32, shape, 1)
    return [(lane // HEAD_DIM) == h for h in range(shape[1] // HEAD_DIM)]


def _row_of(bcast, mask):
    return jnp.max(jnp.where(mask, bcast, -jnp.inf), axis=-1, keepdims=True)


MEM_TQ = 512


def _memattn_fwd(z, kv, y_alias, B, S, *, name):
    T = z.shape[0]
    nt = S // MEM_TQ

    def body(q_ref, k_ref, v_ref, _, y_ref, l_ref):
        q = q_ref[...]
        k = k_ref[...]
        v = v_ref[...]
        masks = _head_masks(q.shape)
        o = jnp.zeros(q.shape, F32)
        lse_b = jnp.zeros(q.shape, F32)
        for m in masks:
            qm = jnp.where(m, q, 0.0).astype(BF)
            s = lax.dot_general(qm, k, (((1,), (1,)), ((), ())), preferred_element_type=F32) * SCALE
            mx = jnp.max(s, axis=-1, keepdims=True)
            e = jnp.exp(s - mx)
            l = jnp.sum(e, axis=-1, keepdims=True)
            p = (e / l).astype(BF)
            o = o + jnp.where(m, jnp.dot(p, v, preferred_element_type=F32), 0.0)
            lse_b = lse_b + jnp.where(m, mx + jnp.log(l), 0.0)
        y_ref[...] = o.astype(y_ref.dtype)
        l_ref[...] = lse_b

    qblk = pl.BlockSpec((MEM_TQ, MEM_W), lambda b, i: (b * nt + i, 3))
    return pl.pallas_call(
        body, name=name, grid=(B, nt),
        in_specs=[qblk, pl.BlockSpec((N_MEM, MEM_W), lambda b, i: (b, 0)), pl.BlockSpec((N_MEM, MEM_W), lambda b, i: (b, 1)),
                  pl.BlockSpec(memory_space=pl.ANY)],
        out_specs=[qblk, pl.BlockSpec((MEM_TQ, MEM_W), lambda b, i: (b * nt + i, 0))],
        out_shape=[jax.ShapeDtypeStruct(y_alias.shape, y_alias.dtype), jax.ShapeDtypeStruct((T, MEM_W), F32)],
        input_output_aliases={3: 0},
        compiler_params=_cp(("parallel", "parallel")),
    )(z, kv, kv, y_alias)


def _memattn_bwd(dy, z, kv, lse, dz_alias, B, S, *, name):
    nt = S // MEM_TQ

    def body(do_ref, q_ref, k_ref, v_ref, l_ref, _, dz_ref, dk_ref, dv_ref, dk_acc, dv_acc):
        i = pl.program_id(1)
        do = do_ref[...]
        q = q_ref[...]
        k = k_ref[...]
        v = v_ref[...]
        lse_b = l_ref[...]
        masks = _head_masks(q.shape)
        dq = jnp.zeros(q.shape, F32)
        dk = jnp.zeros(k.shape, F32)
        dv = jnp.zeros(v.shape, F32)
        for m in masks:
            qm = jnp.where(m, q, 0.0).astype(BF)
            dom = jnp.where(m, do, 0.0).astype(BF)
            s = lax.dot_general(qm, k, (((1,), (1,)), ((), ())), preferred_element_type=F32) * SCALE
            p = jnp.exp(s - _row_of(lse_b, m))
            dp = lax.dot_general(dom, v, (((1,), (1,)), ((), ())), preferred_element_type=F32)
            delta = jnp.sum(p * dp, axis=-1, keepdims=True)
            ds = (p * (dp - delta) * SCALE).astype(BF)
            pb = p.astype(BF)
            dv = dv + jnp.where(m[:N_MEM], lax.dot_general(pb, dom, (((0,), (0,)), ((), ())), preferred_element_type=F32), 0.0)
            dk = dk + jnp.where(m[:N_MEM], lax.dot_general(ds, qm, (((0,), (0,)), ((), ())), preferred_element_type=F32), 0.0)
            dq = dq + jnp.where(m, jnp.dot(ds, k, preferred_element_type=F32), 0.0)
        dz_ref[...] = dq.astype(dz_ref.dtype)

        @pl.when(i == 0)
        def _():
            dk_acc[...] = dk
            dv_acc[...] = dv

        @pl.when(i > 0)
        def _():
            dk_acc[...] += dk
            dv_acc[...] += dv

        @pl.when(i == nt - 1)
        def _():
            dk_ref[...] = dk_acc[...]
            dv_ref[...] = dv_acc[...]

    qblk = pl.BlockSpec((MEM_TQ, MEM_W), lambda b, i: (b * nt + i, 3))
    kblk = pl.BlockSpec((N_MEM, MEM_W), lambda b, i: (b, 0))
    return pl.pallas_call(
        body, name=name, grid=(B, nt),
        in_specs=[qblk, qblk, kblk, pl.BlockSpec((N_MEM, MEM_W), lambda b, i: (b, 1)),
                  pl.BlockSpec((MEM_TQ, MEM_W), lambda b, i: (b * nt + i, 0)), pl.BlockSpec(memory_space=pl.ANY)],
        out_specs=[qblk, kblk, kblk],
        out_shape=[jax.ShapeDtypeStruct(dz_alias.shape, dz_alias.dtype),
                   jax.ShapeDtypeStruct((B * N_MEM, MEM_W), F32), jax.ShapeDtypeStruct((B * N_MEM, MEM_W), F32)],
        scratch_shapes=[pltpu.VMEM((N_MEM, MEM_W), F32), pltpu.VMEM((N_MEM, MEM_W), F32)],
        input_output_aliases={5: 0},
        compiler_params=_cp(("parallel", "arbitrary")),
    )(dy, z, kv, kv, lse, dz_alias)


def _dil_scores(qm, kp, kc, n):
    qi = lax.broadcasted_iota(jnp.int32, (STEPS, STEPS), 0)
    kj = lax.broadcasted_iota(jnp.int32, (STEPS, STEPS), 1)
    sc = lax.dot_general(qm, kc, (((1,), (1,)), ((), ())), preferred_element_type=F32) * SCALE
    sc = jnp.where(kj <= qi, sc, NEG)
    if kp is None:
        return None, sc
    sp = lax.dot_general(qm, kp, (((1,), (1,)), ((), ())), preferred_element_type=F32) * SCALE
    sp = jnp.where(jnp.logical_and(kj >= qi, n > 0), sp, NEG)
    return sp, sc


def _dil_specs(g, d, nb):
    chunk = STEPS * d
    cur = pl.BlockSpec((chunk, 128), lambda b, n, hf: (b * nb + n, g * 2 + hf))
    prev = pl.BlockSpec((chunk, 128), lambda b, n, hf: (b * nb + jnp.maximum(n - 1, 0), g * 2 + hf))
    return cur, prev


def _dil_rows(r, d):
    return pl.ds(r, STEPS, stride=d) if d > 1 else slice(None)


def _dil_loop(d, fn):
    if d <= 4:
        for r in range(d):
            fn(r)
    else:
        lax.fori_loop(0, d, lambda r, carry: (fn(r), carry)[1], 0)


def _dil_fwd_group(g, q, k, v, o_alias, l_alias, B, S, *, name):
    d = DIL[g]
    nb = S // (STEPS * d)
    has_prev = nb > 1

    def body(*refs):
        if has_prev:
            q_ref, kp_ref, kc_ref, vp_ref, vc_ref, _, __, o_ref, l_ref = refs
        else:
            q_ref, kc_ref, vc_ref, _, __, o_ref, l_ref = refs
        n = pl.program_id(1)

        def residue(r):
            rows = _dil_rows(r, d)
            q = q_ref[rows, :]
            kc, vc = kc_ref[rows, :].astype(BF), vc_ref[rows, :].astype(BF)
            kp = kp_ref[rows, :].astype(BF) if has_prev else None
            vp = vp_ref[rows, :].astype(BF) if has_prev else None
            o = jnp.zeros(q.shape, F32)
            lse_b = jnp.zeros(q.shape, F32)
            for m in _head_masks(q.shape):
                qm = jnp.where(m, q, 0.0).astype(BF)
                sp, sc = _dil_scores(qm, kp, kc, n)
                mx = jnp.max(sc, axis=-1, keepdims=True)
                if has_prev:
                    mx = jnp.maximum(mx, jnp.max(sp, axis=-1, keepdims=True))
                l = jnp.sum(jnp.exp(sc - mx), axis=-1, keepdims=True)
                if has_prev:
                    l = l + jnp.sum(jnp.exp(sp - mx), axis=-1, keepdims=True)
                lse = mx + jnp.log(l)
                oh = jnp.dot(jnp.exp(sc - lse).astype(BF), vc, preferred_element_type=F32)
                if has_prev:
                    oh = oh + jnp.dot(jnp.exp(sp - lse).astype(BF), vp, preferred_element_type=F32)
                o = o + jnp.where(m, oh, 0.0)
                lse_b = lse_b + jnp.where(m, lse, 0.0)
            o_ref[rows, :] = o
            l_ref[rows, :] = lse_b

        _dil_loop(d, residue)

    cur, prev = _dil_specs(g, d, nb)
    anyspec = pl.BlockSpec(memory_space=pl.ANY)
    if has_prev:
        in_specs, ops = [cur, prev, cur, prev, cur], [q, k, k, v, v]
    else:
        in_specs, ops = [cur, cur, cur], [q, k, v]
    n_in = len(ops)
    o, l = pl.pallas_call(
        body, name=name, grid=(B, nb, 2),
        in_specs=in_specs + [anyspec, anyspec],
        out_specs=[cur, cur],
        out_shape=[jax.ShapeDtypeStruct(q.shape, F32)] * 2,
        input_output_aliases={n_in: 0, n_in + 1: 1},
        compiler_params=_cp(("parallel", "parallel", "parallel")),
    )(*ops, o_alias, l_alias)
    return o, l


def _dil_bwd_group(g, q, k, v, do, cb, lse, aliases, B, S, *, name):
    d = DIL[g]
    nb = S // (STEPS * d)
    has_prev = nb > 1
    n_out = 5 if has_prev else 3

    def body(*refs):
        if has_prev:
            q_ref, kp_ref, kc_ref, vp_ref, vc_ref, do_ref, c_ref, l_ref = refs[:8]
            dq_ref, dkc_ref, dvc_ref, dkp_ref, dvp_ref = refs[8 + n_out:]
        else:
            q_ref, kc_ref, vc_ref, do_ref, c_ref, l_ref = refs[:6]
            dq_ref, dkc_ref, dvc_ref = refs[6 + n_out:]
        n = pl.program_id(1)
        tdot = lambda a, b: lax.dot_general(a, b, (((0,), (0,)), ((), ())), preferred_element_type=F32)
        ndot = lambda a, b: lax.dot_general(a, b, (((1,), (1,)), ((), ())), preferred_element_type=F32)

        def residue(r):
            rows = _dil_rows(r, d)
            q = q_ref[rows, :]
            kc, vc = kc_ref[rows, :].astype(BF), vc_ref[rows, :].astype(BF)
            kp = kp_ref[rows, :].astype(BF) if has_prev else None
            vp = vp_ref[rows, :].astype(BF) if has_prev else None
            do = do_ref[rows, :]
            cbv = c_ref[rows, :]
            lse_b = l_ref[rows, :]
            z = jnp.zeros(q.shape, F32)
            dq, dkc, dkp, dvc, dvp = z, z, z, z, z
            for m in _head_masks(q.shape):
                qm = jnp.where(m, q, 0.0).astype(BF)
                dom = jnp.where(m, do, 0.0).astype(BF)
                sp, sc = _dil_scores(qm, kp, kc, n)
                lse = _row_of(lse_b, m)
                c = _row_of(cbv, m)
                pc = jnp.exp(sc - lse)
                dsc = (pc * (ndot(dom, vc) + c) * SCALE).astype(BF)
                dqh = jnp.dot(dsc, kc, preferred_element_type=F32)
                dkc = dkc + jnp.where(m, tdot(dsc, qm), 0.0)
                dvc = dvc + jnp.where(m, tdot(pc.astype(BF), dom), 0.0)
                if has_prev:
                    pp = jnp.exp(sp - lse)
                    dsp = (pp * (ndot(dom, vp) + c) * SCALE).astype(BF)
                    dqh = dqh + jnp.dot(dsp, kp, preferred_element_type=F32)
                    dkp = dkp + jnp.where(m, tdot(dsp, qm), 0.0)
                    dvp = dvp + jnp.where(m, tdot(pp.astype(BF), dom), 0.0)
                dq = dq + jnp.where(m, dqh, 0.0)
            dq_ref[rows, :] = dq
            dkc_ref[rows, :] = dkc
            dvc_ref[rows, :] = dvc
            if has_prev:
                dkp_ref[rows, :] = dkp
                dvp_ref[rows, :] = dvp

        _dil_loop(d, residue)

    cur, prev = _dil_specs(g, d, nb)
    anyspec = pl.BlockSpec(memory_space=pl.ANY)
    dq_a, dkc_a, dkp_a, dvc_a, dvp_a = aliases
    if has_prev:
        in_specs, ops = [cur, prev, cur, prev, cur, cur, cur, cur], [q, k, k, v, v, do, cb, lse]
        al = [dq_a, dkc_a, dvc_a, dkp_a, dvp_a]
    else:
        in_specs, ops = [cur, cur, cur, cur, cur, cur], [q, k, v, do, cb, lse]
        al = [dq_a, dkc_a, dvc_a]
    n_in = len(ops)
    outs = pl.pallas_call(
        body, name=name, grid=(B, nb, 2),
        in_specs=in_specs + [anyspec] * n_out,
        out_specs=[cur] * n_out,
        out_shape=[jax.ShapeDtypeStruct(q.shape, F32)] * n_out,
        input_output_aliases={n_in + i: i for i in range(n_out)},
        compiler_params=_cp(("parallel", "parallel", "parallel")),
    )(*ops, *al)
    if has_prev:
        dq_a, dkc_a, dvc_a, dkp_a, dvp_a = outs
    else:
        dq_a, dkc_a, dvc_a = outs
    return dq_a, dkc_a, dkp_a, dvc_a, dvp_a


def _group_softmax(lse):
    l0, l1, l2 = lse[:, 0:256], lse[:, 256:512], lse[:, 512:768]
    mx = jnp.maximum(jnp.maximum(l0, l1), l2)
    e0, e1, e2 = jnp.exp(l0 - mx), jnp.exp(l1 - mx), jnp.exp(l2 - mx)
    tot = e0 + e1 + e2
    return e0 / tot, e1 / tot, e2 / tot


def _dil_combine_fwd(o, lse, y_alias, *, name, tm=512):
    T = o.shape[0]

    def body(o_ref, l_ref, _, y_ref):
        a = jnp.concatenate(_group_softmax(l_ref[...]), axis=1)
        y_ref[...] = (o_ref[...] * a).astype(y_ref.dtype)

    blk = pl.BlockSpec((tm, MAIN_W), lambda i: (i, 0))
    return pl.pallas_call(
        body, name=name, grid=(T // tm,), in_specs=[blk, blk, pl.BlockSpec(memory_space=pl.ANY)], out_specs=blk,
        out_shape=jax.ShapeDtypeStruct(y_alias.shape, y_alias.dtype), input_output_aliases={2: 0},
        compiler_params=_cp(("parallel",)),
    )(o, lse, y_alias)


def _dil_combine_bwd(dy, o, lse, *, name, tm=256):
    T = o.shape[0]
    lane_r = lax.broadcasted_iota(jnp.int32, (256, 256), 0) // HEAD_DIM
    lane_c = lax.broadcasted_iota(jnp.int32, (256, 256), 1) // HEAD_DIM
    ones_bd = (lane_r == lane_c).astype(BF)

    def body(dy_ref, o_ref, l_ref, e_ref, do_ref, c_ref):
        dyv = dy_ref[...]
        alphas = _group_softmax(l_ref[...])
        prod = dyv * o_ref[...]
        e = e_ref[...]
        tot = jnp.zeros((tm, 256), F32)
        for gi in range(3):
            x = prod[:, gi * 256:(gi + 1) * 256]
            hi = x.astype(BF)
            lo = (x - hi.astype(F32)).astype(BF)
            dalpha = jnp.dot(hi, e, preferred_element_type=F32) + jnp.dot(lo, e, preferred_element_type=F32)
            tot = tot + alphas[gi] * dalpha
        a = jnp.concatenate(alphas, axis=1)
        do_ref[...] = (dyv * a).astype(do_ref.dtype)
        c_ref[...] = jnp.concatenate([-al * tot for al in alphas], axis=1)

    blk = pl.BlockSpec((tm, MAIN_W), lambda i: (i, 0))
    return pl.pallas_call(
        body, name=name, grid=(T // tm,),
        in_specs=[blk, blk, blk, pl.BlockSpec((256, 256), lambda i: (0, 0))], out_specs=[blk, blk],
        out_shape=[jax.ShapeDtypeStruct((T, MAIN_W), F32), jax.ShapeDtypeStruct((T, MAIN_W), F32)],
        compiler_params=_cp(("parallel",)),
    )(dy, o, lse, ones_bd)


def _kv_grad(parts, cos, sin, B, S, *, name):
    T = B * S
    tb = S // STEPS
    n_l = len(parts)

    def shifted(g):
        def f(b, t):
            return (b * tb + jnp.minimum(t + DIL[g], tb - 1), g)
        return f

    with_prev = [g for g in range(3) if DIL[g] < tb]
    n_p = len(with_prev)
    per_l = 2 + 2 * n_p

    def body(*refs):
        c_ref, s_ref = refs[0], refs[1]
        ins = refs[2:2 + n_l * per_l]
        dk_ref, dv_ref = refs[2 + n_l * per_l:]
        t = pl.program_id(1)
        dk = jnp.zeros((STEPS, MAIN_W), F32)
        dv = jnp.zeros((STEPS, MAIN_W), F32)
        zero = jnp.zeros((STEPS, 256), F32)
        for li in range(n_l):
            base = li * per_l
            dk = dk + ins[base][...]
            dv = dv + ins[base + 1][...]
            kparts, vparts = [zero] * 3, [zero] * 3
            for pi, g in enumerate(with_prev):
                ok = t + DIL[g] < tb
                kparts[g] = jnp.where(ok, ins[base + 2 + pi][...], 0.0)
                vparts[g] = jnp.where(ok, ins[base + 2 + n_p + pi][...], 0.0)
            dk = dk + jnp.concatenate(kparts, axis=1)
            dv = dv + jnp.concatenate(vparts, axis=1)
        dk_ref[...] = _rot(dk, c_ref[...], s_ref[...], -1.0).astype(dk_ref.dtype)
        dv_ref[...] = dv.astype(dv_ref.dtype)

    full = pl.BlockSpec((STEPS, MAIN_W), lambda b, t: (b * tb + t, 0))
    tab = pl.BlockSpec((STEPS, 128), lambda b, t: (b * tb + t, 0))
    in_specs, ops = [tab, tab], [cos, sin]
    for (kc, kp, vc, vp) in parts:
        in_specs += [full, full] + [pl.BlockSpec((STEPS, 256), shifted(g)) for g in with_prev] * 2
        ops += [kc, vc] + [kp] * n_p + [vp] * n_p
    return pl.pallas_call(
        body, name=name, grid=(B, tb), in_specs=in_specs, out_specs=[full, full],
        out_shape=[jax.ShapeDtypeStruct((T, MAIN_W), BF)] * 2,
        compiler_params=_cp(("parallel", "parallel")),
    )(*ops)


def _loss(y, target, *, name, tm=512):
    T, Dm = y.shape
    nt = T // tm

    def body(y_ref, t_ref, l_ref, d_ref, acc):
        i = pl.program_id(0)
        err = y_ref[...] - t_ref[...]
        d_ref[...] = err / Dm
        part = jnp.sum(jnp.mean(err * err, axis=-1, keepdims=True).reshape(tm // 8, 8, 1), axis=0)

        @pl.when(i == 0)
        def _():
            acc[...] = part

        @pl.when(i > 0)
        def _():
            acc[...] += part

        @pl.when(i == nt - 1)
        def _():
            l_ref[...] = 0.5 * jnp.sum(acc[...], axis=0, keepdims=True)

    row = pl.BlockSpec((tm, Dm), lambda i: (i, 0))
    return pl.pallas_call(
        body, name=name, grid=(nt,), in_specs=[row, row],
        out_specs=[pl.BlockSpec((1, 1), lambda i: (0, 0)), row],
        out_shape=[jax.ShapeDtypeStruct((1, 1), F32), jax.ShapeDtypeStruct((T, Dm), F32)],
        scratch_shapes=[pltpu.VMEM((8, 1), F32)],
        compiler_params=_cp(("arbitrary",)),
    )(y, target)


def _adamw(w, g, m, v, *, name):
    shape = w.shape
    cols = shape[-1]
    rows = w.size // cols
    tm = rows
    for cand in (512, 352, 256, 128):
        if rows > cand and rows % cand == 0 and cand * cols * 4 <= (1 << 20):
            tm = cand
            break

    def body(w_ref, g_ref, m_ref, v_ref, d_ref, mo_ref, vo_ref):
        gv = g_ref[...]
        mn = ADAM_B1 * m_ref[...] + (1.0 - ADAM_B1) * gv
        vn = ADAM_B2 * v_ref[...] + (1.0 - ADAM_B2) * (gv * gv)
        m_hat = mn / (1.0 - ADAM_B1 ** ADAM_STEP)
        v_hat = vn / (1.0 - ADAM_B2 ** ADAM_STEP)
        d_ref[...] = -ADAM_LR * (m_hat / (jnp.sqrt(v_hat) + ADAM_EPS) + ADAM_WD * w_ref[...])
        mo_ref[...] = mn
        vo_ref[...] = vn

    blk = pl.BlockSpec((tm, cols), lambda i: (i, 0))
    outs = pl.pallas_call(
        body, name=name, grid=(rows // tm,), in_specs=[blk] * 4, out_specs=[blk] * 3,
        out_shape=[jax.ShapeDtypeStruct((rows, cols), F32)] * 3,
        compiler_params=_cp(("parallel",)),
    )(*[t.reshape(rows, cols) for t in (w, g, m, v)])
    return tuple(t.reshape(shape) for t in outs)


def _adamw_layer(name, l, w, g, m, v, prev):
    L, rows, cols = w.shape
    tm = rows
    for cand in (512, 352, 256, 176, 128, 64):
        if rows % cand == 0 and cand * cols * 4 <= (1 << 20):
            tm = cand
            break
    if prev is None:
        prev = tuple(lax.empty(w.shape, F32) for _ in range(4))

    def body(w_ref, g_ref, m_ref, v_ref, *rest):
        d_ref, mo_ref, vo_ref, go_ref = rest[4:]
        gv = g_ref[...]
        mn = ADAM_B1 * m_ref[...] + (1.0 - ADAM_B1) * gv
        vn = ADAM_B2 * v_ref[...] + (1.0 - ADAM_B2) * (gv * gv)
        m_hat = mn / (1.0 - ADAM_B1 ** ADAM_STEP)
        v_hat = vn / (1.0 - ADAM_B2 ** ADAM_STEP)
        d_ref[...] = -ADAM_LR * (m_hat / (jnp.sqrt(v_hat) + ADAM_EPS) + ADAM_WD * w_ref[...])
        mo_ref[...] = mn
        vo_ref[...] = vn
        go_ref[...] = gv

    lay = pl.BlockSpec((None, tm, cols), lambda i: (l, i, 0))
    one = pl.BlockSpec((None, tm, cols), lambda i: (0, i, 0))
    return tuple(pl.pallas_call(
        body, name=f"l{l}_adamw_{name}", grid=(rows // tm,),
        in_specs=[lay, one, lay, lay] + [pl.BlockSpec(memory_space=pl.ANY)] * 4, out_specs=[lay] * 4,
        out_shape=[jax.ShapeDtypeStruct(w.shape, F32)] * 4,
        input_output_aliases={4 + i: i for i in range(4)},
        compiler_params=_cp(("parallel",)),
    )(w, g, m, v, *prev))


BIG = {
    'w_in': ((DEPTH, D_MODEL, D_MODEL), 'row'),
    'w_mem_kv': ((DEPTH, D_MODEL, 2 * MEM_W), 'row'),
    'w_out': ((DEPTH, D_MODEL, D_MODEL), 'row'),
    'w_kv': ((1, D_MODEL, 2 * MAIN_W), 'col'),
    'w_gate_up': ((DEPTH, D_MODEL, 2 * D_FF), 'col'),
    'w_down': ((DEPTH, D_FF, D_MODEL), 'row'),
}
BIG_NAMES = tuple(BIG)
N_CHIPS = 4
HBM_ANY = pl.BlockSpec(memory_space=pl.ANY)


def _geom(name):
    (L, R, C), kind = BIG[name]
    if kind == 'row':
        return L, R, C, kind, R // N_CHIPS, C, R // (2 * N_CHIPS)
    return L, R, C, kind, R, C // N_CHIPS, R // 2


def _shard_shape(name):
    L, R, C, kind, rs, cs, rh = _geom(name)
    return (L, rs, cs)


def _half_shape(name):
    L, R, C, kind, rs, cs, rh = _geom(name)
    return (L, rh, cs)


def _full_win(ref, name, s, h):
    L, R, C, kind, rs, cs, rh = _geom(name)
    if kind == 'row':
        rows = pl.ds(s * rs, rs) if h is None else pl.ds(s * rs + h * rh, rh)
        return ref.at[:, rows, :]
    rows = slice(None) if h is None else pl.ds(h * rh, rh)
    return ref.at[:, rows, pl.ds(s * cs, cs)]


def _shard_half(ref, name, h):
    L, R, C, kind, rs, cs, rh = _geom(name)
    return ref.at[:, pl.ds(h * rh, rh), :]


def _halves_win(ref, name, s):
    L, R, C, kind, rs, cs, rh = _geom(name)
    if kind == 'row':
        return ref.at[:, pl.ds(s * rh, rh), :]
    return ref.at[:, :, pl.ds(s * cs, cs)]


def _halves_shape(name):
    L, R, C, kind, rs, cs, rh = _geom(name)
    return (L, N_CHIPS * rh, cs) if kind == 'row' else (L, rh, C)


def _place():
    x, y, c = lax.axis_index("x"), lax.axis_index("y"), lax.axis_index("c")
    chips = [(1 - x, y), (x, 1 - y), (1 - x, 1 - y)]
    return x, y, c, chips


SMALL_ROWS = 24


def _all_gather(shards, small):
    names = BIG_NAMES
    nw = len(names)

    def body(*refs):
        src = dict(zip(names, refs[:nw]))
        small_ref = refs[nw]
        dst = dict(zip(names, refs[nw + 1:2 * nw + 1]))
        small_out = refs[2 * nw + 1]
        send_sems, recv_sems, local_sems = refs[2 * nw + 2:]
        x, y, c, chips = _place()
        s = 2 * x + y
        sib = (x, y, 1 - c)

        def remote(k, src_ref, dst_ref, to):
            return pltpu.make_async_remote_copy(src_ref=src_ref, dst_ref=dst_ref, send_sem=send_sems.at[k],
                                                recv_sem=recv_sems.at[k], device_id=to, device_id_type=MESH)

        local = []
        for wi, nm in enumerate(names):
            local.append(pltpu.make_async_copy(src[nm], _full_win(dst[nm], nm, s, None), local_sems.at[wi]))
        local.append(pltpu.make_async_copy(small_ref, small_out.at[s], local_sems.at[nw]))
        for cp in local:
            cp.start()
        sends = []
        for j, (px, py) in enumerate(chips):
            for wi, nm in enumerate(names):
                sends.append(remote(wi * 6 + j, _shard_half(src[nm], nm, c), _full_win(dst[nm], nm, s, c), (px, py, c)))
            sends.append(remote(nw * 6 + j, small_ref, small_out.at[s], (px, py, c)))
        for cp in sends:
            cp.start()
        for j, (px, py) in enumerate(chips):
            sp = 2 * px + py
            for wi, nm in enumerate(names):
                w = _full_win(dst[nm], nm, sp, c)
                remote(wi * 6 + j, w, w, sib).wait_recv()
                fwd = remote(wi * 6 + 3 + j, w, w, sib)
                fwd.start()
                sends.append(fwd)
            remote(nw * 6 + j, small_ref, small_out.at[sp], sib).wait_recv()
        for j, (px, py) in enumerate(chips):
            sp = 2 * px + py
            for wi, nm in enumerate(names):
                w = _full_win(dst[nm], nm, sp, 1 - c)
                remote(wi * 6 + 3 + j, w, w, sib).wait_recv()
        for cp in sends:
            cp.wait_send()
        for cp in local:
            cp.wait()

    n_sem = nw * 6 + 3
    outs = pl.pallas_call(
        body, name="all_gather_weights",
        in_specs=[HBM_ANY] * (nw + 1), out_specs=[HBM_ANY] * (nw + 1),
        out_shape=[jax.ShapeDtypeStruct(BIG[nm][0], BF) for nm in names]
        + [jax.ShapeDtypeStruct((N_CHIPS, SMALL_ROWS, 256), F32)],
        scratch_shapes=[pltpu.SemaphoreType.DMA((n_sem,)), pltpu.SemaphoreType.DMA((n_sem,)),
                        pltpu.SemaphoreType.DMA((nw + 1,))],
    )(*[shards[nm] for nm in names], small)
    return dict(zip(names, outs[:nw])), outs[nw]


SEM_SPEC = pl.BlockSpec(memory_space=pltpu.SEMAPHORE)
HBM_SPEC = pl.BlockSpec(memory_space=pltpu.HBM)
DATAFLOW = pltpu.SideEffectType.DATAFLOW_SIDE_EFFECTING


def _in_hbm(a):
    return pltpu.with_memory_space_constraint(a, pltpu.HBM)


def _remote(src, dst, send_sems, recv_sems, k, to):
    return pltpu.make_async_remote_copy(src_ref=src, dst_ref=dst, send_sem=send_sems.at[k], recv_sem=recv_sems.at[k],
                                        device_id=to, device_id_type=MESH)


def _split_start(name, bufs, n_copies, sends):
    nb = len(bufs)

    def body(*refs):
        in_refs = refs[:nb]
        send_sems, recv_sems = refs[nb], refs[nb + 1]
        token = refs[-1]
        for k, (src, dst, to) in enumerate(sends(in_refs)):
            _remote(src, dst, send_sems, recv_sems, k, to).start()
        token[...] = jnp.zeros_like(token)

    outs = pl.pallas_call(
        body, name=name,
        out_shape=(pltpu.SemaphoreType.DMA((n_copies,)), pltpu.SemaphoreType.DMA((n_copies,)),
                   *[pltpu.HBM(b.shape, b.dtype) for b in bufs], jax.ShapeDtypeStruct((8, 128), F32)),
        in_specs=[HBM_SPEC] * nb,
        out_specs=(SEM_SPEC, SEM_SPEC, *[HBM_SPEC] * nb, pl.BlockSpec(memory_space=pltpu.VMEM)),
        input_output_aliases={i: 2 + i for i in range(nb)},
        compiler_params=pltpu.CompilerParams(has_side_effects=DATAFLOW),
    )(*[_in_hbm(b) for b in bufs])
    return outs[0], outs[1], list(outs[2:2 + nb]), outs[-1]


def _split_wait(name, send_sems, recv_sems, bufs, after, sends, arrivals):
    nb = len(bufs)

    def body(*refs):
        in_refs = refs[:nb]
        s_sems, r_sems = refs[nb], refs[nb + 1]
        me = (lax.axis_index("x"), lax.axis_index("y"), lax.axis_index("c"))
        for k, (src, dst, to) in enumerate(sends(in_refs)):
            _remote(src, dst, s_sems, r_sems, k, to).wait_send()
        for k, win in enumerate(arrivals(in_refs)):
            _remote(win, win, s_sems, r_sems, k, me).wait_recv()

    outs = pl.pallas_call(
        body, name=name,
        out_shape=[pltpu.HBM(b.shape, b.dtype) for b in bufs],
        in_specs=[HBM_SPEC] * nb + [SEM_SPEC, SEM_SPEC, HBM_ANY],
        out_specs=[HBM_SPEC] * nb,
        input_output_aliases={i: i for i in range(nb)},
        compiler_params=pltpu.CompilerParams(has_side_effects=DATAFLOW),
    )(*bufs, send_sems, recv_sems, after)
    return list(outs)


MIX_W = ('w_in', 'w_mem_kv', 'w_out')
FFN_W = ('w_gate_up', 'w_down')
LAYER_W = MIX_W + FFN_W


def _place_own(l, names, shards, small, sc):
    nw = len(names)
    has_small = small is not None
    n_ops = nw + (1 if has_small else 0)

    def body(sc_ref, *refs):
        for src, dst in zip(refs[:n_ops], refs[n_ops:]):
            dst[...] = src[...]

    in_specs, out_specs, out_shape, ops = [], [], [], list(shards)
    for nm in names:
        L, R, C, kind, rs, cs, rh = _geom(nm)
        in_specs.append(pl.BlockSpec((1, rs, cs), lambda i, sc_ref: (0, 0, 0)))
        if kind == 'row':
            out_specs.append(pl.BlockSpec((1, rs, cs), lambda i, sc_ref: (0, sc_ref[0], 0)))
        else:
            out_specs.append(pl.BlockSpec((1, rs, cs), lambda i, sc_ref: (0, 0, sc_ref[0])))
        out_shape.append(jax.ShapeDtypeStruct((1, R, C), BF))
    if has_small:
        in_specs.append(pl.BlockSpec((SMALL_ROWS, 256), lambda i, sc_ref: (0, 0)))
        out_specs.append(pl.BlockSpec((None, SMALL_ROWS, 256), lambda i, sc_ref: (sc_ref[0], 0, 0)))
        out_shape.append(jax.ShapeDtypeStruct((N_CHIPS, SMALL_ROWS, 256), F32))
        ops.append(small)
    return pl.pallas_call(
        body, name=f"{l}_place_own_shard",
        grid_spec=pltpu.PrefetchScalarGridSpec(num_scalar_prefetch=1, grid=(1,), in_specs=in_specs, out_specs=out_specs),
        out_shape=out_shape,
        compiler_params=_cp(("arbitrary",)),
    )(sc, *ops)


def _gather_start(l, names, shards, small, sc):
    nw = len(names)
    has_small = small is not None
    fulls = _place_own(l, names, shards, small, sc)
    bufs = list(shards) + ([small] if has_small else []) + list(fulls)
    n_src = nw + (1 if has_small else 0)

    def sends(refs):
        x, y, c, chips = _place()
        s = 2 * x + y
        out = []
        for (px, py) in chips:
            for wi, nm in enumerate(names):
                out.append((_shard_half(refs[wi], nm, c), _full_win(refs[n_src + wi], nm, s, c), (px, py, c)))
            if has_small:
                out.append((refs[nw], refs[n_src + nw].at[s], (px, py, c)))
        return out

    def arrivals(refs):
        x, y, c, chips = _place()
        out = []
        for (px, py) in chips:
            sp = 2 * px + py
            for wi, nm in enumerate(names):
                out.append(_full_win(refs[n_src + wi], nm, sp, c))
            if has_small:
                out.append(refs[n_src + nw].at[sp])
        return out

    n_copies = 3 * n_src
    send_sems, recv_sems, bufs, token = _split_start(f"{l}_gather_ici_start", bufs, n_copies, sends)
    return dict(l=l, names=names, has_small=has_small, sems=(send_sems, recv_sems), bufs=bufs, sends=sends,
                arrivals=arrivals, token=token)


def _gather_forward(st, after):
    l, names = st['l'], st['names']
    nw = len(names)
    n_src = nw + (1 if st['has_small'] else 0)
    bufs = _split_wait(f"{l}_gather_ici_wait", *st['sems'], st['bufs'], after, st['sends'], st['arrivals'])
    fulls = bufs[n_src:n_src + nw]
    small_all = bufs[n_src + nw] if st['has_small'] else None

    def sends(refs):
        x, y, c, chips = _place()
        out = []
        for (px, py) in chips:
            sp = 2 * px + py
            for wi, nm in enumerate(names):
                w = _full_win(refs[wi], nm, sp, c)
                out.append((w, w, (x, y, 1 - c)))
        return out

    def arrivals(refs):
        x, y, c, chips = _place()
        out = []
        for (px, py) in chips:
            sp = 2 * px + py
            for wi, nm in enumerate(names):
                out.append(_full_win(refs[wi], nm, sp, 1 - c))
        return out

    send_sems, recv_sems, fulls, token = _split_start(f"{l}_gather_d2d_start", fulls, 3 * nw, sends)
    return dict(l=l, names=names, sems=(send_sems, recv_sems), bufs=fulls, sends=sends, arrivals=arrivals,
                small_all=small_all, token=token)


def _gather_finish(st, after):
    fulls = _split_wait(f"{st['l']}_gather_d2d_wait", *st['sems'], st['bufs'], after, st['sends'], st['arrivals'])
    return dict(zip(st['names'], fulls)), st['small_all']


def _reduce_start(tag, names, grads):
    nw = len(names)
    recv = [lax.empty((1,) + _halves_shape(nm)[1:], F32) for nm in names]
    bufs = [grads[nm] for nm in names] + recv

    def windows(refs, half_of):
        x, y, c, _ = _place()
        h = half_of(c)
        out = []
        for wi, nm in enumerate(names):
            L, R, C, kind, rs, cs, rh = _geom(nm)
            if kind == 'row':
                for sp in range(N_CHIPS):
                    out.append((_full_win(refs[wi], nm, sp, h), _halves_win(refs[nw + wi], nm, sp)))
            else:
                out.append((refs[wi].at[:, pl.ds(h * rh, rh), :], refs[nw + wi]))
        return out

    def sends(refs):
        x, y, c, _ = _place()
        return [(src, dst, (x, y, 1 - c)) for src, dst in windows(refs, lambda c: 1 - c)]

    def arrivals(refs):
        return [dst for _, dst in windows(refs, lambda c: c)]

    n_copies = sum(N_CHIPS if BIG[nm][1] == 'row' else 1 for nm in names)
    send_sems, recv_sems, bufs, token = _split_start(tag + "_halves_start", bufs, n_copies, sends)
    return dict(tag=tag, names=names, sems=(send_sems, recv_sems), bufs=bufs, sends=sends, arrivals=arrivals, token=token)


def _reduce_mid(st, after, sc):
    tag, names = st['tag'], st['names']
    nw = len(names)
    bufs = _split_wait(tag + "_halves_wait", *st['sems'], st['bufs'], after, st['sends'], st['arrivals'])
    halves, own = [], []
    for wi, nm in enumerate(names):
        hb, ow = _add_halves(nm, bufs[wi], bufs[nw + wi], sc, tag)
        halves.append(hb)
        own.append(ow)
    pieces = [lax.empty((3, 1) + _half_shape(nm)[1:], BF) for nm in names]

    def sends(refs):
        x, y, c, chips = _place()
        out = []
        for j, (px, py) in enumerate(chips):
            for wi, nm in enumerate(names):
                out.append((_halves_win(refs[wi], nm, 2 * px + py), refs[nw + wi].at[j], (px, py, c)))
        return out

    def arrivals(refs):
        return [refs[nw + wi].at[j] for j in range(3) for wi in range(nw)]

    send_sems, recv_sems, bufs, token = _split_start(tag + "_pieces_start", halves + pieces, 3 * nw, sends)
    return dict(tag=tag, names=names, sems=(send_sems, recv_sems), bufs=bufs, sends=sends, arrivals=arrivals, own=own,
                token=token)


def _reduce_late(st, after, sc):
    tag, names = st['tag'], st['names']
    nw = len(names)
    bufs = _split_wait(tag + "_pieces_wait", *st['sems'], st['bufs'], after, st['sends'], st['arrivals'])
    gsh = [_sum_pieces(nm, st['own'][wi], bufs[nw + wi], sc, tag) for wi, nm in enumerate(names)]

    def sends(refs):
        x, y, c, _ = _place()
        return [(_shard_half(refs[wi], nm, c), _shard_half(refs[wi], nm, c), (x, y, 1 - c)) for wi, nm in enumerate(names)]

    def arrivals(refs):
        x, y, c, _ = _place()
        return [_shard_half(refs[wi], nm, 1 - c) for wi, nm in enumerate(names)]

    send_sems, recv_sems, bufs, token = _split_start(tag + "_share_start", gsh, nw, sends)
    return dict(tag=tag, names=names, sems=(send_sems, recv_sems), bufs=bufs, sends=sends, arrivals=arrivals, token=token)


def _reduce_finish(st, after):
    gsh = _split_wait(st['tag'] + "_share_wait", *st['sems'], st['bufs'], after, st['sends'], st['arrivals'])
    return dict(zip(st['names'], gsh))


def _add_halves(name, g, r, sc, tag):
    _, R, C, kind, rs, cs, rh = _geom(name)
    L = g.shape[0]
    tr = rh if kind == 'row' else 256
    nr = rh // tr

    def body(sc_ref, g_ref, r_ref, hb_ref, own_ref):
        sp = pl.program_id(2)
        tot = g_ref[...] + r_ref[...]
        hb_ref[...] = tot.astype(hb_ref.dtype)

        @pl.when(sp == sc_ref[0])
        def _():
            own_ref[...] = tot

    if kind == 'row':
        g_map = lambda l, ri, sp, sc_ref: (l, sp * 2 + sc_ref[1], 0)
        h_map = lambda l, ri, sp, sc_ref: (l, sp, 0)
    else:
        g_map = lambda l, ri, sp, sc_ref: (l, sc_ref[1] * nr + ri, sp)
        h_map = lambda l, ri, sp, sc_ref: (l, ri, sp)
    own_map = lambda l, ri, sp, sc_ref: (l, ri, 0)
    blk = (None, tr, cs)
    return pl.pallas_call(
        body, name=tag + "_add_halves_" + name,
        grid_spec=pltpu.PrefetchScalarGridSpec(
            num_scalar_prefetch=1, grid=(L, nr, N_CHIPS),
            in_specs=[pl.BlockSpec(blk, g_map), pl.BlockSpec(blk, h_map)],
            out_specs=[pl.BlockSpec(blk, h_map), pl.BlockSpec(blk, own_map)]),
        out_shape=[jax.ShapeDtypeStruct((L,) + _halves_shape(name)[1:], BF),
                   jax.ShapeDtypeStruct((L,) + _half_shape(name)[1:], F32)],
        compiler_params=_cp(("parallel", "parallel", "arbitrary")),
    )(sc, g, r)


def _sum_pieces(name, own, pieces, sc, tag):
    _, R, C, kind, rs, cs, rh = _geom(name)
    L = own.shape[0]
    tr = rh if kind == 'row' else 256
    nr = rh // tr

    def body(sc_ref, o_ref, p_ref, out_ref):
        out_ref[...] = o_ref[...] + p_ref[0].astype(F32) + p_ref[1].astype(F32) + p_ref[2].astype(F32)

    blk = (None, tr, cs)
    return pl.pallas_call(
        body, name=tag + "_sum_pieces_" + name,
        grid_spec=pltpu.PrefetchScalarGridSpec(
            num_scalar_prefetch=1, grid=(L, nr),
            in_specs=[pl.BlockSpec(blk, lambda l, ri, sc_ref: (l, ri, 0)),
                      pl.BlockSpec((3, None, tr, cs), lambda l, ri, sc_ref: (0, l, ri, 0))],
            out_specs=pl.BlockSpec(blk, lambda l, ri, sc_ref: (l, sc_ref[1] * nr + ri, 0))),
        out_shape=jax.ShapeDtypeStruct((L,) + _shard_shape(name)[1:], F32),
        compiler_params=_cp(("parallel", "parallel")),
    )(sc, own, pieces)


def _small_gather_start(v, sc):
    rows = v.shape[0]

    def place(sc_ref, v_ref, o_ref):
        o_ref[...] = v_ref[...]

    slots = pl.pallas_call(
        place, name="small_grads_place_own",
        grid_spec=pltpu.PrefetchScalarGridSpec(
            num_scalar_prefetch=1, grid=(1,),
            in_specs=[pl.BlockSpec((rows, 128), lambda i, sc_ref: (0, 0))],
            out_specs=pl.BlockSpec((None, rows, 128), lambda i, sc_ref: (2 * sc_ref[0] + sc_ref[1], 0, 0))),
        out_shape=jax.ShapeDtypeStruct((8, rows, 128), F32),
        compiler_params=_cp(("arbitrary",)),
    )(sc, v)

    def peers():
        x, y, c, _ = _place()
        flips = [(fx, fy, fc) for fx in (0, 1) for fy in (0, 1) for fc in (0, 1)][1:]
        return [((1 - x if fx else x), (1 - y if fy else y), (1 - c if fc else c)) for fx, fy, fc in flips]

    def sends(refs):
        x, y, c, _ = _place()
        return [(refs[0], refs[1].at[4 * x + 2 * y + c], p) for p in peers()]

    def arrivals(refs):
        return [refs[1].at[4 * px + 2 * py + pc] for px, py, pc in peers()]

    send_sems, recv_sems, bufs, token = _split_start("small_grads_gather_start", [v, slots], 7, sends)
    return dict(sems=(send_sems, recv_sems), bufs=bufs, sends=sends, arrivals=arrivals, token=token)


def _small_gather_finish(st, after):
    return _split_wait("small_grads_gather_wait", *st['sems'], st['bufs'], after, st['sends'], st['arrivals'])[1]


def _sum8(v8, *, name, tr=336):
    rows = v8.shape[1]
    tr = min(tr, rows)
    assert rows % tr == 0

    def body(v_ref, o_ref):
        tot = v_ref[0]
        for d in range(1, 8):
            tot = tot + v_ref[d]
        o_ref[...] = tot

    return pl.pallas_call(
        body, name=name, grid=(rows // tr,),
        in_specs=[pl.BlockSpec((8, tr, 128), lambda i: (0, i, 0))], out_specs=pl.BlockSpec((tr, 128), lambda i: (i, 0)),
        out_shape=jax.ShapeDtypeStruct((rows, 128), F32),
        compiler_params=_cp(("parallel",)),
    )(v8)


def _block_diag(w_pool_l):
    wbd = jnp.zeros((MAIN_W, MAIN_W), F32)
    for gi in range(len(POOL_WINDOWS)):
        wbd = lax.dynamic_update_slice(wbd, w_pool_l[gi], (gi * POOL_GROUP, gi * POOL_GROUP))
    return wbd.astype(BF)


def _unpack_small(small_all):
    ng = small_all[:, :16, :].reshape(N_CHIPS, DEPTH, 4, 256).transpose(1, 2, 0, 3).reshape(DEPTH, 4, D_MODEL)
    ps = small_all[:, 16:18, :POOL_GROUP].transpose(1, 0, 2).reshape(N_A, MAIN_W)
    return ng, ps


def _local_step(x, mem, positions, on_forward, on_backward, mem_norm, w_pool, kv_norm, target):
    B, S, _ = x.shape
    T = B * S
    xc = x.reshape(T, D_MODEL)
    memf = mem.reshape(B * N_MEM, D_MODEL)
    tgt = target.reshape(T, D_MODEL)
    cos, sin = _rope_tables(positions.reshape(T, 1), name="rope_tables")
    wbd = [_block_diag(w_pool[l]) for l in range(N_A)]
    nbo = D_FF // 256
    fw = []
    rk = rv = None
    kv_saved = None
    wts = []
    norm_gains = pool_scale = y2 = None

    for l in range(DEPTH):
        t = f"l{l}_"
        got = on_forward('start', l, y2)
        wts.append(dict(got[0]))
        if l == 0:
            norm_gains, pool_scale = _unpack_small(got[1])
        sv = {'x_in': xc}
        h0, sv['r0'] = _norm_fwd(xc, norm_gains[l, 0], name=t + "norm0", out_dtype=BF)
        z, = _mm(h0, wts[l]['w_in'], b_layer=0, name=t + "mm_in")
        on_forward('mid', l, z)
        memn, sv['rm'] = _norm_fwd(memf, mem_norm[l], name=t + "norm_mem", out_dtype=BF, tm=256)
        kvm, = _mm(memn, wts[l]['w_mem_kv'], b_layer=0, name=t + "mm_memkv", out_dtypes=(BF,))
        if l < N_A:
            ycat, sv['p'] = _pool_fwd(z, wbd[l], pool_scale[l], B, S, name=t + "pool_fwd")
        else:
            rq = _rope_apply(z, cos, sin, name=t + "rope_q", out_dtype=F32)
            o = lax.empty((T, MAIN_W), F32)
            lse = lax.empty((T, MAIN_W), F32)
            for g in range(3):
                o, lse = _dil_fwd_group(g, rq, rk, rv, o, lse, B, S, name=t + f"dil_fwd{g}")
            ycat = _dil_combine_fwd(o, lse, lax.empty((T, D_MODEL), BF), name=t + "dil_combine")
            sv.update(rq=rq, o=o, lse=lse)
        ycat, sv['lse_m'] = _memattn_fwd(z, kvm, ycat, B, S, name=t + "memattn_fwd")
        y1, = _mm(ycat, wts[l]['w_out'], b_layer=0, name=t + "mm_out")
        wts[l].update(on_forward('ffn', l, y1)[0])
        x1, sv['r1'] = _norm_fwd(y1, norm_gains[l, 1], name=t + "norm1", res=xc)
        h2, sv['r2'] = _norm_fwd(x1, norm_gains[l, 2], name=t + "norm2", out_dtype=BF)
        gg, uu, aa = _mm(h2, wts[l]['w_gate_up'], b_layer=0, b_offsets=(0, nbo), out_n=D_FF, tn=256, name=t + "mm_gate_up",
                         epilogue=_swiglu_fwd_epilogue, out_dtypes=(BF, BF, BF))
        y2, = _mm(aa, wts[l]['w_down'], b_layer=0, tk=D_FF, name=t + "mm_down")
        x2, sv['r3'] = _norm_fwd(y2, norm_gains[l, 3], name=t + "norm3", res=x1)
        sv.update(h0=h0, z=z, memn=memn, kvm=kvm, ycat=ycat, y1=y1, x1=x1, h2=h2, gg=gg, uu=uu, aa=aa, y2=y2)
        fw.append(sv)
        xc = x2
        if l == N_A - 1:
            kvn, rkv = _norm_fwd(xc, kv_norm, name="norm_kv", out_dtype=BF)
            kv, = _mm(kvn, wts[N_A - 1]['w_kv'], b_layer=0, name="mm_kv")
            rk, rv = _rope_apply(kv, cos, sin, name="rope_k", passthrough=True, out_dtype=F32)
            kv_saved = (xc, kvn, rkv)

    loss, dx = _loss(xc, tgt, name="loss")

    d_ng = [[None] * 4 for _ in range(DEPTH)]
    d_memnorm = [None] * DEPTH
    d_wbd = [None] * N_A
    d_pscale = [None] * N_A
    d_kvnorm = None
    kv_parts = []
    tok = None

    def tied(vec, tok):
        return vec if tok is None else vec + tok

    def as3d(gl):
        return {nm: g.reshape((1,) + g.shape) for nm, g in gl.items()}

    for l in reversed(range(DEPTH)):
        t = f"l{l}_b_"
        sv = fw[l]
        gl = {}
        dy2, d_ng[l][3] = _norm_bwd(dx, sv['y2'], sv['r3'], tied(norm_gains[l, 3], tok), name=t + "norm3", out_dtype=BF)
        gl['w_down'], = _mm(sv['aa'], dy2, ta=True, tm=1408, name=t + "dw_down")
        dg, du = _mm(dy2, wts[l]['w_down'], tb=True, b_layer=0, tn=256, name=t + "d_act",
                     extras=((sv['gg'], 'tile'), (sv['uu'], 'tile')), epilogue=_swiglu_bwd_epilogue, out_dtypes=(BF, BF))
        gl['w_gate_up'], = _mm(sv['h2'], (dg, du), ta=True, tn=1408, tk=512, name=t + "dw_gate_up")
        dh2, = _mm((dg, du), wts[l]['w_gate_up'], tb=True, b_layer=0, tk=1408, name=t + "d_h2", out_dtypes=(BF,))
        dx1, d_ng[l][2] = _norm_bwd(dh2, sv['x1'], sv['r2'], norm_gains[l, 2], name=t + "norm2", add=dx)
        tok = on_backward('ffn', l, dx1, as3d(gl))
        dy1, d_ng[l][1] = _norm_bwd(dx1, sv['y1'], sv['r1'], tied(norm_gains[l, 1], tok), name=t + "norm1", out_dtype=BF)
        gl['w_out'], = _mm(sv['ycat'], dy1, ta=True, name=t + "dw_out")
        dycat, = _mm(dy1, wts[l]['w_out'], tb=True, b_layer=0, name=t + "d_ycat")
        dz = lax.empty((T, D_MODEL), BF)
        dz, dkm, dvm = _memattn_bwd(dycat, sv['z'], sv['kvm'], sv['lse_m'], dz, B, S, name=t + "memattn")
        if l < N_A:
            dz, d_wbd[l], d_pscale[l] = _pool_bwd(dycat, sv['p'], wbd[l], pool_scale[l], dz, B, S, name=t + "pool")
        else:
            do, cb = _dil_combine_bwd(dycat, sv['o'], sv['lse'], name=t + "dil_combine")
            acc = tuple(lax.empty((T, MAIN_W), F32) for _ in range(5))
            for g in range(3):
                acc = _dil_bwd_group(g, sv['rq'], rk, rv, do, cb, sv['lse'], acc, B, S, name=t + f"dil{g}")
            dz = _rope_apply(acc[0], cos, sin, name=t + "rope_q", sign=-1.0, alias=dz)
            kv_parts.append(acc[1:])
        tok = on_backward('mix', l, dz, as3d(gl))
        gl['w_in'], = _mm(sv['h0'], dz, ta=True, name=t + "dw_in")
        dh0, = _mm(dz, wts[l]['w_in'], tb=True, b_layer=0, name=t + "d_h0", out_dtypes=(BF,))
        dx, d_ng[l][0] = _norm_bwd(dh0, sv['x_in'], sv['r0'], tied(norm_gains[l, 0], tok), name=t + "norm0", add=dx1)
        gl['w_mem_kv'], = _mm(sv['memn'], (dkm, dvm), ta=True, tn=256, name=t + "dw_memkv")
        dmemn, = _mm((dkm, dvm), wts[l]['w_mem_kv'], tb=True, b_layer=0, tk=256, name=t + "d_memn", out_dtypes=(BF,))
        _, d_memnorm[l] = _norm_bwd(dmemn, memf, sv['rm'], mem_norm[l], name=t + "norm_mem", out_dtype=BF, tm=256)
        if l == N_A:
            dk, dv = _kv_grad(kv_parts, cos, sin, B, S, name="kv_grad")
            x_kv, kvn, rkv = kv_saved
            gl['w_kv'], = _mm(kvn, (dk, dv), ta=True, tn=768, name="dw_kv")
            dkvn, = _mm((dk, dv), wts[N_A - 1]['w_kv'], tb=True, b_layer=0, tk=768, name="d_kvn", out_dtypes=(BF,))
            dx, d_kvnorm = _norm_bwd(dkvn, x_kv, rkv, kv_norm, name="norm_kv_b", add=dx)
        tok = on_backward('end', l, dx, as3d(gl))

    small = {
        'norm_gains': jnp.stack([jnp.concatenate(d_ng[l], axis=0) for l in range(DEPTH)]),
        'mem_norm': jnp.concatenate(d_memnorm, axis=0),
        'kv_norm': d_kvnorm.reshape(D_MODEL),
        'pool_scale': jnp.concatenate(d_pscale, axis=0),
        'w_pool': jnp.stack([jnp.stack([d_wbd[l][gi * POOL_GROUP:(gi + 1) * POOL_GROUP, gi * POOL_GROUP:(gi + 1) * POOL_GROUP]
                                        for gi in range(len(POOL_WINDOWS))]) for l in range(N_A)]),
    }
    return loss, dx, small


SMALL_ORDER = ('norm_gains', 'mem_norm', 'kv_norm', 'pool_scale', 'w_pool')
SMALL_VEC_ROWS = 2560


def kernel(x, mem, positions, norm_gains, mem_norm, w_in, w_mem_kv, w_out, w_pool, pool_scale, kv_norm, w_kv, w_gate_up, w_down, loss_target, m_norm_gains, m_mem_norm, m_w_in, m_w_mem_kv, m_w_out, m_w_pool, m_pool_scale, m_kv_norm, m_w_kv, m_w_gate_up, m_w_down, v_norm_gains, v_mem_norm, v_w_in, v_w_mem_kv, v_w_out, v_w_pool, v_pool_scale, v_kv_norm, v_w_kv, v_w_gate_up, v_w_down):
    xi, yi, ci = lax.axis_index("x"), lax.axis_index("y"), lax.axis_index("c")
    s = 2 * xi + yi
    sc = jnp.stack([s, ci]).astype(jnp.int32)
    weights = dict(norm_gains=norm_gains, mem_norm=mem_norm, w_in=w_in, w_mem_kv=w_mem_kv, w_out=w_out, w_pool=w_pool,
                   pool_scale=pool_scale, kv_norm=kv_norm, w_kv=w_kv, w_gate_up=w_gate_up, w_down=w_down)
    moms = dict(norm_gains=m_norm_gains, mem_norm=m_mem_norm, w_in=m_w_in, w_mem_kv=m_w_mem_kv, w_out=m_w_out,
                w_pool=m_w_pool, pool_scale=m_pool_scale, kv_norm=m_kv_norm, w_kv=m_w_kv, w_gate_up=m_w_gate_up,
                w_down=m_w_down)
    vels = dict(norm_gains=v_norm_gains, mem_norm=v_mem_norm, w_in=v_w_in, w_mem_kv=v_w_mem_kv, w_out=v_w_out,
                w_pool=v_w_pool, pool_scale=v_pool_scale, kv_norm=v_kv_norm, w_kv=v_w_kv, w_gate_up=v_w_gate_up,
                w_down=v_w_down)

    small_w = jnp.zeros((SMALL_ROWS, 256), F32)
    small_w = lax.dynamic_update_slice(small_w, norm_gains.reshape(16, 256), (0, 0))
    small_w = lax.dynamic_update_slice(small_w, pool_scale, (16, 0))
    def shard_of(nm, l):
        return w_kv.astype(BF).reshape(_shard_shape('w_kv')) if nm == 'w_kv' else weights[nm][l:l + 1].astype(BF)

    groups = [('l0a', 0, MIX_W), ('l0b', 0, FFN_W)] + [(f"l{l}", l, LAYER_W + (('w_kv',) if l == N_A - 1 else ()))
                                                        for l in range(1, DEPTH)]
    started = {tag: _gather_start(tag, names, [shard_of(nm, l) for nm in names], small_w if tag == 'l0a' else None, sc)
               for tag, l, names in groups}
    all_started = sum(st['token'][0, 0] for st in started.values())
    on_d2d, gathered = {}, {}

    def on_forward(where, l, after):
        if where == 'start' and l == 0:
            fwd = _gather_forward(started['l0a'], jnp.full((8, 128), all_started))
            return _gather_finish(fwd, fwd['token'])
        if where == 'mid' and l == 0:
            on_d2d['l0b'] = _gather_forward(started['l0b'], after)
        if where == 'ffn':
            if l + 1 < DEPTH:
                on_d2d[f"l{l + 1}"] = _gather_forward(started[f"l{l + 1}"], after)
            if l == 0:
                return _gather_finish(on_d2d.pop('l0b'), after)
            return {nm: gathered[l][nm] for nm in FFN_W}, None
        if where == 'start':
            gathered[l] = _gather_finish(on_d2d.pop(f"l{l}"), after)[0]
            return {nm: w for nm, w in gathered[l].items() if nm not in FFN_W}, None
        return None

    hook_of = {'ffn': 0, 'mix': 1, 'end': 2}
    active, reduced = [], {l: {} for l in range(DEPTH)}
    advance = {'mid': lambda st, after: _reduce_mid(st, after, sc), 'late': lambda st, after: _reduce_late(st, after, sc)}

    def run_hook(idx, after):
        toks = []
        for grp in list(active):
            while grp['plan'] and grp['plan'][0][1] <= idx:
                step = grp['plan'].pop(0)[0]
                if step == 'finish':
                    reduced[grp['layer']].update(_reduce_finish(grp['st'], after))
                    active.remove(grp)
                else:
                    grp['st'] = advance[step](grp['st'], after)
                    toks.append(grp['st']['token'][0, 0])
        return toks

    def on_backward(where, l, after, grads):
        idx = 3 * (DEPTH - 1 - l) + hook_of[where]
        toks = run_hook(idx, after)
        if where in ('ffn', 'end'):
            names = FFN_W if where == 'ffn' else tuple(nm for nm in grads if nm not in FFN_W)
            st = _reduce_start(f"l{l}_{where}_grads", names, {nm: grads[nm] for nm in names})
            plan = [('mid', idx + 1), ('late', idx + 3), ('finish', idx + 4)] if where == 'ffn' else \
                   [('mid', idx + 1), ('late', idx + 2), ('finish', idx + 3)]
            active.append(dict(layer=l, st=st, plan=plan))
            toks.append(st['token'][0, 0])
        return sum(toks) if toks else None

    loss, gx, gsmall = _local_step(x, mem, positions, on_forward, on_backward, mem_norm, w_pool, kv_norm, loss_target)
    loss = lax.psum(loss[0, 0], ("x", "y", "c"))

    vec = jnp.concatenate([gsmall[nm].reshape(-1) for nm in SMALL_ORDER])
    vec = jnp.pad(vec, (0, SMALL_VEC_ROWS * 128 - vec.shape[0])).reshape(SMALL_VEC_ROWS, 128)
    small_st = _small_gather_start(vec, sc)
    outs = {nm: None for nm in LAYER_W}

    def adamw_layers(layers):
        for l in layers:
            for nm in LAYER_W:
                outs[nm] = _adamw_layer(nm, l, weights[nm], reduced[l][nm], moms[nm], vels[nm], outs[nm])

    last = 3 * DEPTH
    run_hook(last, small_st['token'])
    adamw_layers(range(DEPTH - 1, 0, -1))
    run_hook(last + 1, outs[LAYER_W[-1]][0])
    tot = _sum8(_small_gather_finish(small_st, outs[LAYER_W[0]][0]), name="sum_small_grads", tr=512)
    run_hook(last + 2, tot)
    assert not active
    adamw_layers([0])
    tot = tot.reshape(-1)
    grads, off = {}, 0
    for nm in SMALL_ORDER:
        shape = (DEPTH, 4, D_MODEL) if nm == 'norm_gains' else (N_A, MAIN_W) if nm == 'pool_scale' else weights[nm].shape
        n = 1
        for dim in shape:
            n *= dim
        grads[nm] = tot[off:off + n].reshape(shape)
        off += n
    grads['norm_gains'] = lax.dynamic_slice(grads['norm_gains'], (0, 0, s * 256), (DEPTH, 4, 256))
    grads['pool_scale'] = lax.dynamic_slice(grads['pool_scale'], (0, s * POOL_GROUP), (N_A, POOL_GROUP))
    grads['w_kv'] = reduced[N_A]['w_kv'].reshape(w_kv.shape)

    order = ('norm_gains', 'mem_norm', 'w_in', 'w_mem_kv', 'w_out', 'w_pool', 'pool_scale', 'kv_norm', 'w_kv',
             'w_gate_up', 'w_down')
    deltas, new_m, new_v = {}, {}, {}
    for nm in order:
        if nm in LAYER_W:
            deltas[nm], new_m[nm], new_v[nm], grads[nm] = outs[nm]
        else:
            deltas[nm], new_m[nm], new_v[nm] = _adamw(weights[nm], grads[nm], moms[nm], vels[nm], name="adamw_" + nm)
    return (loss, gx.reshape(x.shape), *[grads[nm] for nm in order], *[deltas[nm] for nm in order],
            *[new_m[nm] for nm in order], *[new_v[nm] for nm in order])
```

```python
import functools

import jax
import jax.numpy as jnp
from jax import lax
from jax.experimental import pallas as pl
from jax.experimental.pallas import tpu as pltpu

F32 = jnp.float32
BF = jnp.bfloat16

D_MODEL = 1024
DEPTH = 4
N_A = 2
HEAD_DIM = 64
MEM_W = 256
MAIN_W = 768
D_FF = 2816
N_MEM = 256
POOL_WINDOWS = (2, 4, 8, 16)
POOL_GROUP = 192
DIL = (1, 4, 16)
STEPS = 128
ROPE_THETA = 10000.0
EPS = 1e-6
SCALE = HEAD_DIM ** -0.5
NEG = -1e30

ADAM_LR = 0.001
ADAM_B1 = 0.9
ADAM_B2 = 0.999
ADAM_EPS = 1e-08
ADAM_WD = 0.01
ADAM_STEP = 10

VMEM_LIMIT = 48 * 1024 * 1024
MESH = pl.DeviceIdType.MESH


def _cp(sem):
    return pltpu.CompilerParams(dimension_semantics=sem, vmem_limit_bytes=VMEM_LIMIT)


def _mm(a, b, *, name, ta=False, tb=False, tm=1024, tn=512, tk=1024, b_layer=None, b_offsets=(0,),
        extras=(), epilogue=None, out_dtypes=(F32,), out_n=None, stack=None):
    a_pair = isinstance(a, (tuple, list))
    b_pair = isinstance(b, (tuple, list))
    a0 = a[0] if a_pair else a
    b0 = b[0] if b_pair else b
    a_rows, a_cols = a0.shape
    if a_pair:
        a_cols *= 2
    b_rows, b_cols = b0.shape[-2:]
    if b_pair:
        b_cols *= 2
    M, K = (a_cols, a_rows) if ta else (a_rows, a_cols)
    N = b_rows if tb else b_cols
    if out_n is not None:
        N = out_n
    tm, tn, tk = min(tm, M), min(tn, N), min(tk, K)
    assert M % tm == 0 and N % tn == 0 and K % tk == 0, (name, M, N, K, tm, tn, tk)
    nk = K // tk
    n_acc = len(b_offsets)

    if a_pair:
        a_half = (a0.shape[1] // (tm if ta else tk))
    if b_pair:
        b_half = (b0.shape[1] // (tk if tb else tn))

    def a_map(sel):
        def f(i, j, k):
            r, c = (k, i) if ta else (i, k)
            if a_pair:
                c = jnp.clip(c - sel * a_half, 0, a_half - 1)
            return (r, c)
        return f

    def b_map(sel, off):
        def f(i, j, k):
            r, c = (j + off, k) if tb else (k, j + off)
            if b_pair:
                c = jnp.clip(c - sel * b_half, 0, b_half - 1)
            if b_layer is not None:
                return (b_layer, r, c)
            return (r, c)
        return f

    a_blk = (tk, tm) if ta else (tm, tk)
    b_blk = (tn, tk) if tb else (tk, tn)
    if b_layer is not None:
        b_blk = (None,) + b_blk
    in_specs, operands = [], []
    for sel in range(2 if a_pair else 1):
        in_specs.append(pl.BlockSpec(a_blk, a_map(sel)))
        operands.append(a[sel] if a_pair else a)
    n_a = len(operands)
    for off in b_offsets:
        for sel in range(2 if b_pair else 1):
            in_specs.append(pl.BlockSpec(b_blk, b_map(sel, off)))
            operands.append(b[sel] if b_pair else b)
    n_b = len(operands) - n_a
    for arr, kind in extras:
        if kind == 'tile':
            in_specs.append(pl.BlockSpec((tm, tn), lambda i, j, k: (i, j)))
        elif kind == 'row':
            in_specs.append(pl.BlockSpec((tm, 1), lambda i, j, k: (i, 0)))
        else:
            in_specs.append(pl.BlockSpec((1, tn), lambda i, j, k: (0, j)))
        operands.append(arr)
    n_e = len(extras)
    n_o = len(out_dtypes)
    dims = (((0,) if ta else (1,), (1,) if tb else (0,)), ((), ()))

    def body(*refs):
        a_refs = refs[:n_a]
        b_refs = refs[n_a:n_a + n_b]
        e_refs = refs[n_a + n_b:n_a + n_b + n_e]
        n_in = n_a + n_b + n_e + (1 if stack is not None else 0)
        o_refs = refs[n_in:n_in + n_o]
        acc_refs = refs[n_in + n_o:]
        i, j, k = pl.program_id(0), pl.program_id(1), pl.program_id(2)
        if a_pair:
            cidx = i if ta else k
            av = jnp.where(cidx < a_half, a_refs[0][...], a_refs[1][...])
        else:
            av = a_refs[0][...]
        av = av.astype(BF)
        prods = []
        for q in range(n_acc):
            if b_pair:
                cidx = (k if tb else j) + b_offsets[q]
                bv = jnp.where(cidx < b_half, b_refs[2 * q][...], b_refs[2 * q + 1][...])
            else:
                bv = b_refs[q][...]
            prods.append(lax.dot_general(av, bv.astype(BF), dims, preferred_element_type=F32))

        def finish(accs):
            outs = epilogue(accs, *[r[...] for r in e_refs]) if epilogue is not None else accs
            for o_ref, o in zip(o_refs, outs):
                o_ref[...] = o.astype(o_ref.dtype)

        if nk == 1:
            finish(prods)
        else:
            @pl.when(k == 0)
            def _():
                for r, p in zip(acc_refs, prods):
                    r[...] = p

            @pl.when(k > 0)
            def _():
                for r, p in zip(acc_refs, prods):
                    r[...] += p

            @pl.when(k == nk - 1)
            def _():
                finish([r[...] for r in acc_refs])

    if stack is not None:
        buf, layer = stack
        assert n_o == 1 and buf.shape[1:] == (M, N)
        return pl.pallas_call(
            body, name=name,
            grid=(M // tm, N // tn, nk),
            in_specs=in_specs + [pl.BlockSpec(memory_space=pl.ANY)],
            out_specs=[pl.BlockSpec((None, tm, tn), lambda i, j, k: (layer, i, j))],
            out_shape=[jax.ShapeDtypeStruct(buf.shape, buf.dtype)],
            scratch_shapes=[pltpu.VMEM((tm, tn), F32) for _ in range(n_acc if nk > 1 else 0)],
            input_output_aliases={len(operands): 0},
            compiler_params=_cp(("parallel", "parallel", "arbitrary")),
        )(*operands, buf)[0]
    return pl.pallas_call(
        body, name=name,
        grid=(M // tm, N // tn, nk),
        in_specs=in_specs,
        out_specs=[pl.BlockSpec((tm, tn), lambda i, j, k: (i, j)) for _ in range(n_o)],
        out_shape=[jax.ShapeDtypeStruct((M, N), dt) for dt in out_dtypes],
        scratch_shapes=[pltpu.VMEM((tm, tn), F32) for _ in range(n_acc if nk > 1 else 0)],
        compiler_params=_cp(("parallel", "parallel", "arbitrary")),
    )(*operands)


def _norm_fwd(x, g, *, name, res=None, out_dtype=F32, tm=512):
    T, Dm = x.shape
    has_res = res is not None

    def body(*refs):
        if has_res:
            x_ref, g_ref, r_ref, y_ref, s_ref = refs
        else:
            x_ref, g_ref, y_ref, s_ref = refs
        xv = x_ref[...]
        rstd = lax.rsqrt(jnp.mean(xv * xv, axis=-1, keepdims=True) + EPS)
        y = xv * rstd * g_ref[...]
        if has_res:
            y = r_ref[...] + y
        y_ref[...] = y.astype(y_ref.dtype)
        s_ref[...] = rstd

    row = pl.BlockSpec((tm, Dm), lambda i: (i, 0))
    in_specs = [row, pl.BlockSpec((1, Dm), lambda i: (0, 0))] + ([row] if has_res else [])
    ops = [x, g.reshape(1, Dm)] + ([res] if has_res else [])
    return pl.pallas_call(
        body, name=name, grid=(T // tm,), in_specs=in_specs,
        out_specs=[row, pl.BlockSpec((tm, 1), lambda i: (i, 0))],
        out_shape=[jax.ShapeDtypeStruct((T, Dm), out_dtype), jax.ShapeDtypeStruct((T, 1), F32)],
        compiler_params=_cp(("parallel",)),
    )(*ops)


def _norm_bwd(dout, x, rstd, g, *, name, add=None, out_dtype=F32, tm=512):
    T, Dm = x.shape
    has_add = add is not None
    nt = T // tm

    def body(*refs):
        if has_add:
            do_ref, x_ref, s_ref, g_ref, a_ref, dx_ref, dg_ref, acc = refs
        else:
            do_ref, x_ref, s_ref, g_ref, dx_ref, dg_ref, acc = refs
        i = pl.program_id(0)
        do = do_ref[...].astype(F32)
        xh = x_ref[...] * s_ref[...]
        gd = do * g_ref[...]
        dx = s_ref[...] * (gd - xh * jnp.mean(gd * xh, axis=-1, keepdims=True))
        if has_add:
            dx = dx + a_ref[...].astype(F32)
        dx_ref[...] = dx.astype(dx_ref.dtype)
        part = jnp.sum((do * xh).reshape(tm // 8, 8, Dm), axis=0)

        @pl.when(i == 0)
        def _():
            acc[...] = part

        @pl.when(i > 0)
        def _():
            acc[...] += part

        @pl.when(i == nt - 1)
        def _():
            dg_ref[...] = jnp.sum(acc[...], axis=0, keepdims=True)

    row = pl.BlockSpec((tm, Dm), lambda i: (i, 0))
    in_specs = [row, row, pl.BlockSpec((tm, 1), lambda i: (i, 0)), pl.BlockSpec((1, Dm), lambda i: (0, 0))]
    ops = [dout, x, rstd, g.reshape(1, Dm)]
    if has_add:
        in_specs.append(row)
        ops.append(add)
    return pl.pallas_call(
        body, name=name, grid=(nt,), in_specs=in_specs,
        out_specs=[row, pl.BlockSpec((1, Dm), lambda i: (0, 0))],
        out_shape=[jax.ShapeDtypeStruct((T, Dm), out_dtype), jax.ShapeDtypeStruct((1, Dm), F32)],
        scratch_shapes=[pltpu.VMEM((8, Dm), F32)],
        compiler_params=_cp(("arbitrary",)),
    )(*ops)


def _swiglu_fwd_epilogue(accs):
    g, u = accs
    return g, u, g * jax.nn.sigmoid(g) * u


def _swiglu_bwd_epilogue(accs, g, u):
    da = accs[0]
    g = g.astype(F32)
    u = u.astype(F32)
    sig = jax.nn.sigmoid(g)
    return da * u * (sig * (1.0 + g * (1.0 - sig))), da * (g * sig)


def _rope_tables(pos, *, name, tm=1024):
    T = pos.shape[0]
    half = HEAD_DIM // 2
    freqs = ROPE_THETA ** (-jnp.arange(half, dtype=F32) / half)
    freqs = jnp.tile(freqs, 4).reshape(1, 128)

    def body(p_ref, f_ref, c_ref, s_ref):
        ang = p_ref[...].astype(F32) * f_ref[...]
        lane = lax.broadcasted_iota(jnp.int32, ang.shape, 1)
        c_ref[...] = jnp.cos(ang)
        s_ref[...] = jnp.where(lane % HEAD_DIM < half, -1.0, 1.0) * jnp.sin(ang)

    tab = pl.BlockSpec((tm, 128), lambda i: (i, 0))
    return pl.pallas_call(
        body, name=name, grid=(T // tm,),
        in_specs=[pl.BlockSpec((tm, 1), lambda i: (i, 0)), pl.BlockSpec((1, 128), lambda i: (0, 0))],
        out_specs=[tab, tab],
        out_shape=[jax.ShapeDtypeStruct((T, 128), F32)] * 2,
        compiler_params=_cp(("parallel",)),
    )(pos, freqs)


def _rot(x, cos, sin, sign):
    W = x.shape[1]
    half = HEAD_DIM // 2
    reps = W // 128
    c = jnp.concatenate([cos] * reps, axis=1) if reps > 1 else cos
    s = jnp.concatenate([sin] * reps, axis=1) if reps > 1 else sin
    lane = lax.broadcasted_iota(jnp.int32, x.shape, 1)
    swapped = jnp.where(lane % HEAD_DIM < half, pltpu.roll(x, W - half, axis=1), pltpu.roll(x, half, axis=1))
    return x * c + (sign * s) * swapped


def _rope_apply(x, cos, sin, *, name, sign=1.0, width=MAIN_W, passthrough=False, out_dtype=BF, alias=None,
                out_cols=None, tm=512):
    T = x.shape[0]

    def body(*refs):
        if passthrough:
            x_ref, v_ref, c_ref, s_ref, o_ref, ov_ref = refs
            ov_ref[...] = v_ref[...].astype(ov_ref.dtype)
        elif alias is not None:
            x_ref, c_ref, s_ref, _, o_ref = refs
        else:
            x_ref, c_ref, s_ref, o_ref = refs
        o_ref[...] = _rot(x_ref[...].astype(F32), c_ref[...], s_ref[...], sign).astype(o_ref.dtype)

    blk0 = pl.BlockSpec((tm, width), lambda i: (i, 0))
    blk1 = pl.BlockSpec((tm, width), lambda i: (i, 1))
    tab = pl.BlockSpec((tm, 128), lambda i: (i, 0))
    if passthrough:
        return pl.pallas_call(
            body, name=name, grid=(T // tm,), in_specs=[blk0, blk1, tab, tab], out_specs=[blk0, blk0],
            out_shape=[jax.ShapeDtypeStruct((T, width), out_dtype)] * 2,
            compiler_params=_cp(("parallel",)),
        )(x, x, cos, sin)
    if alias is not None:
        return pl.pallas_call(
            body, name=name, grid=(T // tm,),
            in_specs=[blk0, tab, tab, pl.BlockSpec(memory_space=pl.ANY)], out_specs=blk0,
            out_shape=jax.ShapeDtypeStruct(alias.shape, alias.dtype),
            input_output_aliases={3: 0},
            compiler_params=_cp(("parallel",)),
        )(x, cos, sin, alias)
    return pl.pallas_call(
        body, name=name, grid=(T // tm,), in_specs=[blk0, tab, tab], out_specs=blk0,
        out_shape=jax.ShapeDtypeStruct((T, width), out_dtype),
        compiler_params=_cp(("parallel",)),
    )(x, cos, sin)


POOL_T = 256
POOL_HALO = 16


def _pool_lane_window(shape):
    lane = lax.broadcasted_iota(jnp.int32, shape, 1)
    w = jnp.full(shape, POOL_WINDOWS[0], jnp.int32)
    for gi in range(1, len(POOL_WINDOWS)):
        w = jnp.where(lane >= gi * POOL_GROUP, POOL_WINDOWS[gi], w)
    return w


def _pool_fwd(z, wbd, scale, B, S, *, name):
    T = z.shape[0]
    nt = S // POOL_T
    hb = POOL_T // POOL_HALO

    def body(z_ref, h_ref, w_ref, sc_ref, y_ref, p_ref, ext):
        i = pl.program_id(1)
        u = z_ref[...]
        ext[pl.ds(POOL_HALO, POOL_T), :] = u
        ext[pl.ds(0, POOL_HALO), :] = jnp.where(i > 0, h_ref[...], 0.0)
        win = _pool_lane_window((POOL_T, MAIN_W))
        acc = u
        for k in range(1, POOL_HALO):
            acc = acc + jnp.where(k < win, ext[pl.ds(POOL_HALO - k, POOL_T), :], 0.0)
        t = i * POOL_T + lax.broadcasted_iota(jnp.int32, (POOL_T, MAIN_W), 0)
        cnt = jnp.minimum(t + 1, win).astype(F32)
        p = (acc / cnt - u).astype(BF)
        p_ref[...] = p
        y = jnp.dot(p, w_ref[...], preferred_element_type=F32) * sc_ref[...]
        y_ref[...] = y.astype(y_ref.dtype)

    return pl.pallas_call(
        body, name=name, grid=(B, nt),
        in_specs=[pl.BlockSpec((POOL_T, MAIN_W), lambda b, i: (b * nt + i, 0)),
                  pl.BlockSpec((POOL_HALO, MAIN_W), lambda b, i: (jnp.maximum((b * nt + i) * hb - 1, 0), 0)),
                  pl.BlockSpec((MAIN_W, MAIN_W), lambda b, i: (0, 0)),
                  pl.BlockSpec((1, MAIN_W), lambda b, i: (0, 0))],
        out_specs=[pl.BlockSpec((POOL_T, MAIN_W), lambda b, i: (b * nt + i, 0)),
                   pl.BlockSpec((POOL_T, MAIN_W), lambda b, i: (b * nt + i, 0))],
        out_shape=[jax.ShapeDtypeStruct((T, D_MODEL), BF), jax.ShapeDtypeStruct((T, MAIN_W), BF)],
        scratch_shapes=[pltpu.VMEM((POOL_T + POOL_HALO, MAIN_W), F32)],
        compiler_params=_cp(("parallel", "parallel")),
    )(z, z, wbd, scale.reshape(1, MAIN_W))


def _pool_bwd(dy, p, wbd, scale, dz_alias, B, S, *, name):
    T = dy.shape[0]
    nt = S // POOL_T
    hb = POOL_T // POOL_HALO
    last_halo = T // POOL_HALO - 1
    R = POOL_T + POOL_HALO

    def body(dy_ref, dyn_ref, p_ref, pn_ref, w_ref, sc_ref, _, dz_ref, dw_ref, ds_ref, ext, dw_acc, ds_acc):
        b, i = pl.program_id(0), pl.program_id(1)
        first = jnp.logical_and(b == 0, i == 0)
        dyv = dy_ref[...]
        pv = p_ref[...]
        sc = sc_ref[...]
        w = w_ref[...]
        pw = jnp.dot(pv, w, preferred_element_type=F32)
        ds_part = jnp.sum((dyv * pw).reshape(POOL_T // 8, 8, MAIN_W), axis=0)
        dpw = (dyv * sc).astype(BF)
        dw_part = lax.dot_general(pv, dpw, (((0,), (0,)), ((), ())), preferred_element_type=F32)

        @pl.when(first)
        def _():
            dw_acc[...] = dw_part
            ds_acc[...] = ds_part

        @pl.when(jnp.logical_not(first))
        def _():
            dw_acc[...] += dw_part
            ds_acc[...] += ds_part

        @pl.when(jnp.logical_and(b == pl.num_programs(0) - 1, i == nt - 1))
        def _():
            dw_ref[...] = dw_acc[...]
            ds_ref[...] = jnp.sum(ds_acc[...], axis=0, keepdims=True)

        dp = lax.dot_general(dpw, w, (((1,), (1,)), ((), ())), preferred_element_type=F32)
        dpn = lax.dot_general((dyn_ref[...] * sc).astype(BF), w, (((1,), (1,)), ((), ())), preferred_element_type=F32)
        win = _pool_lane_window((POOL_T, MAIN_W))
        win_n = _pool_lane_window((POOL_HALO, MAIN_W))
        t = i * POOL_T + lax.broadcasted_iota(jnp.int32, (POOL_T, MAIN_W), 0)
        tn = (i + 1) * POOL_T + lax.broadcasted_iota(jnp.int32, (POOL_HALO, MAIN_W), 0)
        ext[pl.ds(0, POOL_T), :] = dp / jnp.minimum(t + 1, win).astype(F32)
        ext[pl.ds(POOL_T, POOL_HALO), :] = jnp.where(i < nt - 1, dpn / jnp.minimum(tn + 1, win_n).astype(F32), 0.0)
        acc = -dp
        for k in range(POOL_HALO):
            acc = acc + jnp.where(k < win, ext[pl.ds(k, POOL_T), :], 0.0)
        dz_ref[...] = acc.astype(dz_ref.dtype)

    cur = lambda b, i: (b * nt + i, 0)
    nxt = lambda b, i: (jnp.minimum((b * nt + i + 1) * hb, last_halo), 0)
    return pl.pallas_call(
        body, name=name, grid=(B, nt),
        in_specs=[pl.BlockSpec((POOL_T, MAIN_W), cur), pl.BlockSpec((POOL_HALO, MAIN_W), nxt),
                  pl.BlockSpec((POOL_T, MAIN_W), cur), pl.BlockSpec((POOL_HALO, MAIN_W), nxt),
                  pl.BlockSpec((MAIN_W, MAIN_W), lambda b, i: (0, 0)),
                  pl.BlockSpec((1, MAIN_W), lambda b, i: (0, 0)),
                  pl.BlockSpec(memory_space=pl.ANY)],
        out_specs=[pl.BlockSpec((POOL_T, MAIN_W), cur),
                   pl.BlockSpec((MAIN_W, MAIN_W), lambda b, i: (0, 0)),
                   pl.BlockSpec((1, MAIN_W), lambda b, i: (0, 0))],
        out_shape=[jax.ShapeDtypeStruct(dz_alias.shape, dz_alias.dtype),
                   jax.ShapeDtypeStruct((MAIN_W, MAIN_W), F32), jax.ShapeDtypeStruct((1, MAIN_W), F32)],
        scratch_shapes=[pltpu.VMEM((R, MAIN_W), F32), pltpu.VMEM((MAIN_W, MAIN_W), F32), pltpu.VMEM((8, MAIN_W), F32)],
        input_output_aliases={6: 0},
        compiler_params=_cp(("arbitrary", "arbitrary")),
    )(dy, dy, p, p, wbd, scale.reshape(1, MAIN_W), dz_alias)


def _head_masks(shape):
    lane = lax.broadcasted_iota(jnp.int32, shape, 1)
    return [(lane // HEAD_DIM) == h for h in range(shape[1] // HEAD_DIM)]


def _row_of(bcast, mask):
    return jnp.max(jnp.where(mask, bcast, -jnp.inf), axis=-1, keepdims=True)


MEM_TQ = 512


def _memattn_fwd(z, kv, y_alias, B, S, *, name):
    T = z.shape[0]
    nt = S // MEM_TQ

    def body(q_ref, k_ref, v_ref, _, y_ref, l_ref):
        q = q_ref[...]
        k = k_ref[...]
        v = v_ref[...]
        masks = _head_masks(q.shape)
        o = jnp.zeros(q.shape, F32)
        lse_b = jnp.zeros(q.shape, F32)
        for m in masks:
            qm = jnp.where(m, q, 0.0).astype(BF)
            s = lax.dot_general(qm, k, (((1,), (1,)), ((), ())), preferred_element_type=F32) * SCALE
            mx = jnp.max(s, axis=-1, keepdims=True)
            e = jnp.exp(s - mx)
            l = jnp.sum(e, axis=-1, keepdims=True)
            p = (e / l).astype(BF)
            o = o + jnp.where(m, jnp.dot(p, v, preferred_element_type=F32), 0.0)
            lse_b = lse_b + jnp.where(m, mx + jnp.log(l), 0.0)
        y_ref[...] = o.astype(y_ref.dtype)
        l_ref[...] = lse_b

    qblk = pl.BlockSpec((MEM_TQ, MEM_W), lambda b, i: (b * nt + i, 3))
    return pl.pallas_call(
        body, name=name, grid=(B, nt),
        in_specs=[qblk, pl.BlockSpec((N_MEM, MEM_W), lambda b, i: (b, 0)), pl.BlockSpec((N_MEM, MEM_W), lambda b, i: (b, 1)),
                  pl.BlockSpec(memory_space=pl.ANY)],
        out_specs=[qblk, pl.BlockSpec((MEM_TQ, MEM_W), lambda b, i: (b * nt + i, 0))],
        out_shape=[jax.ShapeDtypeStruct(y_alias.shape, y_alias.dtype), jax.ShapeDtypeStruct((T, MEM_W), F32)],
        input_output_aliases={3: 0},
        compiler_params=_cp(("parallel", "parallel")),
    )(z, kv, kv, y_alias)


def _memattn_bwd(dy, z, kv, lse, dz_alias, B, S, *, name):
    nt = S // MEM_TQ

    def body(do_ref, q_ref, k_ref, v_ref, l_ref, _, dz_ref, dk_ref, dv_ref, dk_acc, dv_acc):
        i = pl.program_id(1)
        do = do_ref[...]
        q = q_ref[...]
        k = k_ref[...]
        v = v_ref[...]
        lse_b = l_ref[...]
        masks = _head_masks(q.shape)
        dq = jnp.zeros(q.shape, F32)
        dk = jnp.zeros(k.shape, F32)
        dv = jnp.zeros(v.shape, F32)
        for m in masks:
            qm = jnp.where(m, q, 0.0).astype(BF)
            dom = jnp.where(m, do, 0.0).astype(BF)
            s = lax.dot_general(qm, k, (((1,), (1,)), ((), ())), preferred_element_type=F32) * SCALE
            p = jnp.exp(s - _row_of(lse_b, m))
            dp = lax.dot_general(dom, v, (((1,), (1,)), ((), ())), preferred_element_type=F32)
            delta = jnp.sum(p * dp, axis=-1, keepdims=True)
            ds = (p * (dp - delta) * SCALE).astype(BF)
            pb = p.astype(BF)
            dv = dv + jnp.where(m[:N_MEM], lax.dot_general(pb, dom, (((0,), (0,)), ((), ())), preferred_element_type=F32), 0.0)
            dk = dk + jnp.where(m[:N_MEM], lax.dot_general(ds, qm, (((0,), (0,)), ((), ())), preferred_element_type=F32), 0.0)
            dq = dq + jnp.where(m, jnp.dot(ds, k, preferred_element_type=F32), 0.0)
        dz_ref[...] = dq.astype(dz_ref.dtype)

        @pl.when(i == 0)
        def _():
            dk_acc[...] = dk
            dv_acc[...] = dv

        @pl.when(i > 0)
        def _():
            dk_acc[...] += dk
            dv_acc[...] += dv

        @pl.when(i == nt - 1)
        def _():
            dk_ref[...] = dk_acc[...]
            dv_ref[...] = dv_acc[...]

    qblk = pl.BlockSpec((MEM_TQ, MEM_W), lambda b, i: (b * nt + i, 3))
    kblk = pl.BlockSpec((N_MEM, MEM_W), lambda b, i: (b, 0))
    return pl.pallas_call(
        body, name=name, grid=(B, nt),
        in_specs=[qblk, qblk, kblk, pl.BlockSpec((N_MEM, MEM_W), lambda b, i: (b, 1)),
                  pl.BlockSpec((MEM_TQ, MEM_W), lambda b, i: (b * nt + i, 0)), pl.BlockSpec(memory_space=pl.ANY)],
        out_specs=[qblk, kblk, kblk],
        out_shape=[jax.ShapeDtypeStruct(dz_alias.shape, dz_alias.dtype),
                   jax.ShapeDtypeStruct((B * N_MEM, MEM_W), F32), jax.ShapeDtypeStruct((B * N_MEM, MEM_W), F32)],
        scratch_shapes=[pltpu.VMEM((N_MEM, MEM_W), F32), pltpu.VMEM((N_MEM, MEM_W), F32)],
        input_output_aliases={5: 0},
        compiler_params=_cp(("parallel", "arbitrary")),
    )(dy, z, kv, kv, lse, dz_alias)


def _dil_scores(qm, kp, kc, n):
    qi = lax.broadcasted_iota(jnp.int32, (STEPS, STEPS), 0)
    kj = lax.broadcasted_iota(jnp.int32, (STEPS, STEPS), 1)
    sc = lax.dot_general(qm, kc, (((1,), (1,)), ((), ())), preferred_element_type=F32) * SCALE
    sc = jnp.where(kj <= qi, sc, NEG)
    if kp is None:
        return None, sc
    sp = lax.dot_general(qm, kp, (((1,), (1,)), ((), ())), preferred_element_type=F32) * SCALE
    sp = jnp.where(jnp.logical_and(kj >= qi, n > 0), sp, NEG)
    return sp, sc


def _dil_specs(g, d, nb):
    chunk = STEPS * d
    cur = pl.BlockSpec((chunk, 128), lambda b, n, hf: (b * nb + n, g * 2 + hf))
    prev = pl.BlockSpec((chunk, 128), lambda b, n, hf: (b * nb + jnp.maximum(n - 1, 0), g * 2 + hf))
    return cur, prev


def _dil_rows(r, d):
    return pl.ds(r, STEPS, stride=d) if d > 1 else slice(None)


def _dil_loop(d, fn):
    if d <= 4:
        for r in range(d):
            fn(r)
    else:
        lax.fori_loop(0, d, lambda r, carry: (fn(r), carry)[1], 0)


def _dil_fwd_group(g, q, k, v, o_alias, l_alias, B, S, *, name):
    d = DIL[g]
    nb = S // (STEPS * d)
    has_prev = nb > 1

    def body(*refs):
        if has_prev:
            q_ref, kp_ref, kc_ref, vp_ref, vc_ref, _, __, o_ref, l_ref = refs
        else:
            q_ref, kc_ref, vc_ref, _, __, o_ref, l_ref = refs
        n = pl.program_id(1)

        def residue(r):
            rows = _dil_rows(r, d)
            q = q_ref[rows, :]
            kc, vc = kc_ref[rows, :].astype(BF), vc_ref[rows, :].astype(BF)
            kp = kp_ref[rows, :].astype(BF) if has_prev else None
            vp = vp_ref[rows, :].astype(BF) if has_prev else None
            o = jnp.zeros(q.shape, F32)
            lse_b = jnp.zeros(q.shape, F32)
            for m in _head_masks(q.shape):
                qm = jnp.where(m, q, 0.0).astype(BF)
                sp, sc = _dil_scores(qm, kp, kc, n)
                mx = jnp.max(sc, axis=-1, keepdims=True)
                if has_prev:
                    mx = jnp.maximum(mx, jnp.max(sp, axis=-1, keepdims=True))
                l = jnp.sum(jnp.exp(sc - mx), axis=-1, keepdims=True)
                if has_prev:
                    l = l + jnp.sum(jnp.exp(sp - mx), axis=-1, keepdims=True)
                lse = mx + jnp.log(l)
                oh = jnp.dot(jnp.exp(sc - lse).astype(BF), vc, preferred_element_type=F32)
                if has_prev:
                    oh = oh + jnp.dot(jnp.exp(sp - lse).astype(BF), vp, preferred_element_type=F32)
                o = o + jnp.where(m, oh, 0.0)
                lse_b = lse_b + jnp.where(m, lse, 0.0)
            o_ref[rows, :] = o
            l_ref[rows, :] = lse_b

        _dil_loop(d, residue)

    cur, prev = _dil_specs(g, d, nb)
    anyspec = pl.BlockSpec(memory_space=pl.ANY)
    if has_prev:
        in_specs, ops = [cur, prev, cur, prev, cur], [q, k, k, v, v]
    else:
        in_specs, ops = [cur, cur, cur], [q, k, v]
    n_in = len(ops)
    o, l = pl.pallas_call(
        body, name=name, grid=(B, nb, 2),
        in_specs=in_specs + [anyspec, anyspec],
        out_specs=[cur, cur],
        out_shape=[jax.ShapeDtypeStruct(q.shape, F32)] * 2,
        input_output_aliases={n_in: 0, n_in + 1: 1},
        compiler_params=_cp(("parallel", "parallel", "parallel")),
    )(*ops, o_alias, l_alias)
    return o, l


def _dil_bwd_group(g, q, k, v, do, cb, lse, aliases, B, S, *, name):
    d = DIL[g]
    nb = S // (STEPS * d)
    has_prev = nb > 1
    n_out = 5 if has_prev else 3

    def body(*refs):
        if has_prev:
            q_ref, kp_ref, kc_ref, vp_ref, vc_ref, do_ref, c_ref, l_ref = refs[:8]
            dq_ref, dkc_ref, dvc_ref, dkp_ref, dvp_ref = refs[8 + n_out:]
        else:
            q_ref, kc_ref, vc_ref, do_ref, c_ref, l_ref = refs[:6]
            dq_ref, dkc_ref, dvc_ref = refs[6 + n_out:]
        n = pl.program_id(1)
        tdot = lambda a, b: lax.dot_general(a, b, (((0,), (0,)), ((), ())), preferred_element_type=F32)
        ndot = lambda a, b: lax.dot_general(a, b, (((1,), (1,)), ((), ())), preferred_element_type=F32)

        def residue(r):
            rows = _dil_rows(r, d)
            q = q_ref[rows, :]
            kc, vc = kc_ref[rows, :].astype(BF), vc_ref[rows, :].astype(BF)
            kp = kp_ref[rows, :].astype(BF) if has_prev else None
            vp = vp_ref[rows, :].astype(BF) if has_prev else None
            do = do_ref[rows, :]
            cbv = c_ref[rows, :]
            lse_b = l_ref[rows, :]
            z = jnp.zeros(q.shape, F32)
            dq, dkc, dkp, dvc, dvp = z, z, z, z, z
            for m in _head_masks(q.shape):
                qm = jnp.where(m, q, 0.0).astype(BF)
                dom = jnp.where(m, do, 0.0).astype(BF)
                sp, sc = _dil_scores(qm, kp, kc, n)
                lse = _row_of(lse_b, m)
                c = _row_of(cbv, m)
                pc = jnp.exp(sc - lse)
                dsc = (pc * (ndot(dom, vc) + c) * SCALE).astype(BF)
                dqh = jnp.dot(dsc, kc, preferred_element_type=F32)
                dkc = dkc + jnp.where(m, tdot(dsc, qm), 0.0)
                dvc = dvc + jnp.where(m, tdot(pc.astype(BF), dom), 0.0)
                if has_prev:
                    pp = jnp.exp(sp - lse)
                    dsp = (pp * (ndot(dom, vp) + c) * SCALE).astype(BF)
                    dqh = dqh + jnp.dot(dsp, kp, preferred_element_type=F32)
                    dkp = dkp + jnp.where(m, tdot(dsp, qm), 0.0)
                    dvp = dvp + jnp.where(m, tdot(pp.astype(BF), dom), 0.0)
                dq = dq + jnp.where(m, dqh, 0.0)
            dq_ref[rows, :] = dq
            dkc_ref[rows, :] = dkc
            dvc_ref[rows, :] = dvc
            if has_prev:
                dkp_ref[rows, :] = dkp
                dvp_ref[rows, :] = dvp

        _dil_loop(d, residue)

    cur, prev = _dil_specs(g, d, nb)
    anyspec = pl.BlockSpec(memory_space=pl.ANY)
    dq_a, dkc_a, dkp_a, dvc_a, dvp_a = aliases
    if has_prev:
        in_specs, ops = [cur, prev, cur, prev, cur, cur, cur, cur], [q, k, k, v, v, do, cb, lse]
        al = [dq_a, dkc_a, dvc_a, dkp_a, dvp_a]
    else:
        in_specs, ops = [cur, cur, cur, cur, cur, cur], [q, k, v, do, cb, lse]
        al = [dq_a, dkc_a, dvc_a]
    n_in = len(ops)
    outs = pl.pallas_call(
        body, name=name, grid=(B, nb, 2),
        in_specs=in_specs + [anyspec] * n_out,
        out_specs=[cur] * n_out,
        out_shape=[jax.ShapeDtypeStruct(q.shape, F32)] * n_out,
        input_output_aliases={n_in + i: i for i in range(n_out)},
        compiler_params=_cp(("parallel", "parallel", "parallel")),
    )(*ops, *al)
    if has_prev:
        dq_a, dkc_a, dvc_a, dkp_a, dvp_a = outs
    else:
        dq_a, dkc_a, dvc_a = outs
    return dq_a, dkc_a, dkp_a, dvc_a, dvp_a


def _group_softmax(lse):
    l0, l1, l2 = lse[:, 0:256], lse[:, 256:512], lse[:, 512:768]
    mx = jnp.maximum(jnp.maximum(l0, l1), l2)
    e0, e1, e2 = jnp.exp(l0 - mx), jnp.exp(l1 - mx), jnp.exp(l2 - mx)
    tot = e0 + e1 + e2
    return e0 / tot, e1 / tot, e2 / tot


def _dil_combine_fwd(o, lse, y_alias, *, name, tm=512):
    T = o.shape[0]

    def body(o_ref, l_ref, _, y_ref):
        a = jnp.concatenate(_group_softmax(l_ref[...]), axis=1)
        y_ref[...] = (o_ref[...] * a).astype(y_ref.dtype)

    blk = pl.BlockSpec((tm, MAIN_W), lambda i: (i, 0))
    return pl.pallas_call(
        body, name=name, grid=(T // tm,), in_specs=[blk, blk, pl.BlockSpec(memory_space=pl.ANY)], out_specs=blk,
        out_shape=jax.ShapeDtypeStruct(y_alias.shape, y_alias.dtype), input_output_aliases={2: 0},
        compiler_params=_cp(("parallel",)),
    )(o, lse, y_alias)


def _dil_combine_bwd(dy, o, lse, *, name, tm=256):
    T = o.shape[0]
    lane_r = lax.broadcasted_iota(jnp.int32, (256, 256), 0) // HEAD_DIM
    lane_c = lax.broadcasted_iota(jnp.int32, (256, 256), 1) // HEAD_DIM
    ones_bd = (lane_r == lane_c).astype(BF)

    def body(dy_ref, o_ref, l_ref, e_ref, do_ref, c_ref):
        dyv = dy_ref[...]
        alphas = _group_softmax(l_ref[...])
        prod = dyv * o_ref[...]
        e = e_ref[...]
        tot = jnp.zeros((tm, 256), F32)
        for gi in range(3):
            x = prod[:, gi * 256:(gi + 1) * 256]
            hi = x.astype(BF)
            lo = (x - hi.astype(F32)).astype(BF)
            dalpha = jnp.dot(hi, e, preferred_element_type=F32) + jnp.dot(lo, e, preferred_element_type=F32)
            tot = tot + alphas[gi] * dalpha
        a = jnp.concatenate(alphas, axis=1)
        do_ref[...] = (dyv * a).astype(do_ref.dtype)
        c_ref[...] = jnp.concatenate([-al * tot for al in alphas], axis=1)

    blk = pl.BlockSpec((tm, MAIN_W), lambda i: (i, 0))
    return pl.pallas_call(
        body, name=name, grid=(T // tm,),
        in_specs=[blk, blk, blk, pl.BlockSpec((256, 256), lambda i: (0, 0))], out_specs=[blk, blk],
        out_shape=[jax.ShapeDtypeStruct((T, MAIN_W), F32), jax.ShapeDtypeStruct((T, MAIN_W), F32)],
        compiler_params=_cp(("parallel",)),
    )(dy, o, lse, ones_bd)


def _kv_grad(parts, cos, sin, B, S, *, name):
    T = B * S
    tb = S // STEPS
    n_l = len(parts)

    def shifted(g):
        def f(b, t):
            return (b * tb + jnp.minimum(t + DIL[g], tb - 1), g)
        return f

    with_prev = [g for g in range(3) if DIL[g] < tb]
    n_p = len(with_prev)
    per_l = 2 + 2 * n_p

    def body(*refs):
        c_ref, s_ref = refs[0], refs[1]
        ins = refs[2:2 + n_l * per_l]
        dk_ref, dv_ref = refs[2 + n_l * per_l:]
        t = pl.program_id(1)
        dk = jnp.zeros((STEPS, MAIN_W), F32)
        dv = jnp.zeros((STEPS, MAIN_W), F32)
        zero = jnp.zeros((STEPS, 256), F32)
        for li in range(n_l):
            base = li * per_l
            dk = dk + ins[base][...]
            dv = dv + ins[base + 1][...]
            kparts, vparts = [zero] * 3, [zero] * 3
            for pi, g in enumerate(with_prev):
                ok = t + DIL[g] < tb
                kparts[g] = jnp.where(ok, ins[base + 2 + pi][...], 0.0)
                vparts[g] = jnp.where(ok, ins[base + 2 + n_p + pi][...], 0.0)
            dk = dk + jnp.concatenate(kparts, axis=1)
            dv = dv + jnp.concatenate(vparts, axis=1)
        dk_ref[...] = _rot(dk, c_ref[...], s_ref[...], -1.0).astype(dk_ref.dtype)
        dv_ref[...] = dv.astype(dv_ref.dtype)

    full = pl.BlockSpec((STEPS, MAIN_W), lambda b, t: (b * tb + t, 0))
    tab = pl.BlockSpec((STEPS, 128), lambda b, t: (b * tb + t, 0))
    in_specs, ops = [tab, tab], [cos, sin]
    for (kc, kp, vc, vp) in parts:
        in_specs += [full, full] + [pl.BlockSpec((STEPS, 256), shifted(g)) for g in with_prev] * 2
        ops += [kc, vc] + [kp] * n_p + [vp] * n_p
    return pl.pallas_call(
        body, name=name, grid=(B, tb), in_specs=in_specs, out_specs=[full, full],
        out_shape=[jax.ShapeDtypeStruct((T, MAIN_W), BF)] * 2,
        compiler_params=_cp(("parallel", "parallel")),
    )(*ops)


def _loss(y, target, *, name, tm=512):
    T, Dm = y.shape
    nt = T // tm

    def body(y_ref, t_ref, l_ref, d_ref, acc):
        i = pl.program_id(0)
        err = y_ref[...] - t_ref[...]
        d_ref[...] = err / Dm
        part = jnp.sum(jnp.mean(err * err, axis=-1, keepdims=True).reshape(tm // 8, 8, 1), axis=0)

        @pl.when(i == 0)
        def _():
            acc[...] = part

        @pl.when(i > 0)
        def _():
            acc[...] += part

        @pl.when(i == nt - 1)
        def _():
            l_ref[...] = 0.5 * jnp.sum(acc[...], axis=0, keepdims=True)

    row = pl.BlockSpec((tm, Dm), lambda i: (i, 0))
    return pl.pallas_call(
        body, name=name, grid=(nt,), in_specs=[row, row],
        out_specs=[pl.BlockSpec((1, 1), lambda i: (0, 0)), row],
        out_shape=[jax.ShapeDtypeStruct((1, 1), F32), jax.ShapeDtypeStruct((T, Dm), F32)],
        scratch_shapes=[pltpu.VMEM((8, 1), F32)],
        compiler_params=_cp(("arbitrary",)),
    )(y, target)


def _adamw(w, g, m, v, *, name):
    shape = w.shape
    cols = shape[-1]
    rows = w.size // cols
    tm = rows
    for cand in (512, 352, 256, 128):
        if rows > cand and rows % cand == 0 and cand * cols * 4 <= (1 << 20):
            tm = cand
            break

    def body(w_ref, g_ref, m_ref, v_ref, d_ref, mo_ref, vo_ref):
        gv = g_ref[...]
        mn = ADAM_B1 * m_ref[...] + (1.0 - ADAM_B1) * gv
        vn = ADAM_B2 * v_ref[...] + (1.0 - ADAM_B2) * (gv * gv)
        m_hat = mn / (1.0 - ADAM_B1 ** ADAM_STEP)
        v_hat = vn / (1.0 - ADAM_B2 ** ADAM_STEP)
        d_ref[...] = -ADAM_LR * (m_hat / (jnp.sqrt(v_hat) + ADAM_EPS) + ADAM_WD * w_ref[...])
        mo_ref[...] = mn
        vo_ref[...] = vn

    blk = pl.BlockSpec((tm, cols), lambda i: (i, 0))
    outs = pl.pallas_call(
        body, name=name, grid=(rows // tm,), in_specs=[blk] * 4, out_specs=[blk] * 3,
        out_shape=[jax.ShapeDtypeStruct((rows, cols), F32)] * 3,
        compiler_params=_cp(("parallel",)),
    )(*[t.reshape(rows, cols) for t in (w, g, m, v)])
    return tuple(t.reshape(shape) for t in outs)


def _adamw_layer(name, l, w, g, m, v, prev):
    L, rows, cols = w.shape
    tm = rows
    for cand in (512, 352, 256, 176, 128, 64):
        if rows % cand == 0 and cand * cols * 4 <= (1 << 20):
            tm = cand
            break
    if prev is None:
        prev = tuple(lax.empty(w.shape, F32) for _ in range(4))

    def body(w_ref, g_ref, m_ref, v_ref, *rest):
        d_ref, mo_ref, vo_ref, go_ref = rest[4:]
        gv = g_ref[...]
        mn = ADAM_B1 * m_ref[...] + (1.0 - ADAM_B1) * gv
        vn = ADAM_B2 * v_ref[...] + (1.0 - ADAM_B2) * (gv * gv)
        m_hat = mn / (1.0 - ADAM_B1 ** ADAM_STEP)
        v_hat = vn / (1.0 - ADAM_B2 ** ADAM_STEP)
        d_ref[...] = -ADAM_LR * (m_hat / (jnp.sqrt(v_hat) + ADAM_EPS) + ADAM_WD * w_ref[...])
        mo_ref[...] = mn
        vo_ref[...] = vn
        go_ref[...] = gv

    lay = pl.BlockSpec((None, tm, cols), lambda i: (l, i, 0))
    one = pl.BlockSpec((None, tm, cols), lambda i: (0, i, 0))
    return tuple(pl.pallas_call(
        body, name=f"l{l}_adamw_{name}", grid=(rows // tm,),
        in_specs=[lay, one, lay, lay] + [pl.BlockSpec(memory_space=pl.ANY)] * 4, out_specs=[lay] * 4,
        out_shape=[jax.ShapeDtypeStruct(w.shape, F32)] * 4,
        input_output_aliases={4 + i: i for i in range(4)},
        compiler_params=_cp(("parallel",)),
    )(w, g, m, v, *prev))


BIG = {
    'w_in': ((DEPTH, D_MODEL, D_MODEL), 'row'),
    'w_mem_kv': ((DEPTH, D_MODEL, 2 * MEM_W), 'row'),
    'w_out': ((DEPTH, D_MODEL, D_MODEL), 'row'),
    'w_kv': ((1, D_MODEL, 2 * MAIN_W), 'col'),
    'w_gate_up': ((DEPTH, D_MODEL, 2 * D_FF), 'col'),
    'w_down': ((DEPTH, D_FF, D_MODEL), 'row'),
}
BIG_NAMES = tuple(BIG)
N_CHIPS = 4
HBM_ANY = pl.BlockSpec(memory_space=pl.ANY)


def _geom(name):
    (L, R, C), kind = BIG[name]
    if kind == 'row':
        return L, R, C, kind, R // N_CHIPS, C, R // (2 * N_CHIPS)
    return L, R, C, kind, R, C // N_CHIPS, R // 2


def _shard_shape(name):
    L, R, C, kind, rs, cs, rh = _geom(name)
    return (L, rs, cs)


def _half_shape(name):
    L, R, C, kind, rs, cs, rh = _geom(name)
    return (L, rh, cs)


def _full_win(ref, name, s, h):
    L, R, C, kind, rs, cs, rh = _geom(name)
    if kind == 'row':
        rows = pl.ds(s * rs, rs) if h is None else pl.ds(s * rs + h * rh, rh)
        return ref.at[:, rows, :]
    rows = slice(None) if h is None else pl.ds(h * rh, rh)
    return ref.at[:, rows, pl.ds(s * cs, cs)]


def _shard_half(ref, name, h):
    L, R, C, kind, rs, cs, rh = _geom(name)
    return ref.at[:, pl.ds(h * rh, rh), :]


def _halves_win(ref, name, s):
    L, R, C, kind, rs, cs, rh = _geom(name)
    if kind == 'row':
        return ref.at[:, pl.ds(s * rh, rh), :]
    return ref.at[:, :, pl.ds(s * cs, cs)]


def _halves_shape(name):
    L, R, C, kind, rs, cs, rh = _geom(name)
    return (L, N_CHIPS * rh, cs) if kind == 'row' else (L, rh, C)


def _place():
    x, y, c = lax.axis_index("x"), lax.axis_index("y"), lax.axis_index("c")
    chips = [(1 - x, y), (x, 1 - y), (1 - x, 1 - y)]
    return x, y, c, chips


SMALL_ROWS = 24


def _all_gather(shards, small):
    names = BIG_NAMES
    nw = len(names)

    def body(*refs):
        src = dict(zip(names, refs[:nw]))
        small_ref = refs[nw]
        dst = dict(zip(names, refs[nw + 1:2 * nw + 1]))
        small_out = refs[2 * nw + 1]
        send_sems, recv_sems, local_sems = refs[2 * nw + 2:]
        x, y, c, chips = _place()
        s = 2 * x + y
        sib = (x, y, 1 - c)

        def remote(k, src_ref, dst_ref, to):
            return pltpu.make_async_remote_copy(src_ref=src_ref, dst_ref=dst_ref, send_sem=send_sems.at[k],
                                                recv_sem=recv_sems.at[k], device_id=to, device_id_type=MESH)

        local = []
        for wi, nm in enumerate(names):
            local.append(pltpu.make_async_copy(src[nm], _full_win(dst[nm], nm, s, None), local_sems.at[wi]))
        local.append(pltpu.make_async_copy(small_ref, small_out.at[s], local_sems.at[nw]))
        for cp in local:
            cp.start()
        sends = []
        for j, (px, py) in enumerate(chips):
            for wi, nm in enumerate(names):
                sends.append(remote(wi * 6 + j, _shard_half(src[nm], nm, c), _full_win(dst[nm], nm, s, c), (px, py, c)))
            sends.append(remote(nw * 6 + j, small_ref, small_out.at[s], (px, py, c)))
        for cp in sends:
            cp.start()
        for j, (px, py) in enumerate(chips):
            sp = 2 * px + py
            for wi, nm in enumerate(names):
                w = _full_win(dst[nm], nm, sp, c)
                remote(wi * 6 + j, w, w, sib).wait_recv()
                fwd = remote(wi * 6 + 3 + j, w, w, sib)
                fwd.start()
                sends.append(fwd)
            remote(nw * 6 + j, small_ref, small_out.at[sp], sib).wait_recv()
        for j, (px, py) in enumerate(chips):
            sp = 2 * px + py
            for wi, nm in enumerate(names):
                w = _full_win(dst[nm], nm, sp, 1 - c)
                remote(wi * 6 + 3 + j, w, w, sib).wait_recv()
        for cp in sends:
            cp.wait_send()
        for cp in local:
            cp.wait()

    n_sem = nw * 6 + 3
    outs = pl.pallas_call(
        body, name="all_gather_weights",
        in_specs=[HBM_ANY] * (nw + 1), out_specs=[HBM_ANY] * (nw + 1),
        out_shape=[jax.ShapeDtypeStruct(BIG[nm][0], BF) for nm in names]
        + [jax.ShapeDtypeStruct((N_CHIPS, SMALL_ROWS, 256), F32)],
        scratch_shapes=[pltpu.SemaphoreType.DMA((n_sem,)), pltpu.SemaphoreType.DMA((n_sem,)),
                        pltpu.SemaphoreType.DMA((nw + 1,))],
    )(*[shards[nm] for nm in names], small)
    return dict(zip(names, outs[:nw])), outs[nw]


SEM_SPEC = pl.BlockSpec(memory_space=pltpu.SEMAPHORE)
HBM_SPEC = pl.BlockSpec(memory_space=pltpu.HBM)
DATAFLOW = pltpu.SideEffectType.DATAFLOW_SIDE_EFFECTING


def _in_hbm(a):
    return pltpu.with_memory_space_constraint(a, pltpu.HBM)


def _remote(src, dst, send_sems, recv_sems, k, to):
    return pltpu.make_async_remote_copy(src_ref=src, dst_ref=dst, send_sem=send_sems.at[k], recv_sem=recv_sems.at[k],
                                        device_id=to, device_id_type=MESH)


def _split_start(name, bufs, n_copies, sends, after=None):
    nb = len(bufs)
    n_in = nb + (0 if after is None else 1)

    def body(*refs):
        in_refs = refs[:nb]
        send_sems, recv_sems = refs[n_in], refs[n_in + 1]
        token = refs[-1]
        for k, (src, dst, to) in enumerate(sends(in_refs)):
            _remote(src, dst, send_sems, recv_sems, k, to).start()
        token[...] = jnp.zeros_like(token)

    outs = pl.pallas_call(
        body, name=name,
        out_shape=(pltpu.SemaphoreType.DMA((n_copies,)), pltpu.SemaphoreType.DMA((n_copies,)),
                   *[pltpu.HBM(b.shape, b.dtype) for b in bufs], jax.ShapeDtypeStruct((8, 128), F32)),
        in_specs=[HBM_SPEC] * nb + [HBM_ANY] * (n_in - nb),
        out_specs=(SEM_SPEC, SEM_SPEC, *[HBM_SPEC] * nb, pl.BlockSpec(memory_space=pltpu.VMEM)),
        input_output_aliases={i: 2 + i for i in range(nb)},
        compiler_params=pltpu.CompilerParams(has_side_effects=DATAFLOW),
    )(*[_in_hbm(b) for b in bufs], *([] if after is None else [after]))
    return outs[0], outs[1], list(outs[2:2 + nb]), outs[-1]


def _split_wait(name, send_sems, recv_sems, bufs, after, sends, arrivals):
    nb = len(bufs)

    def body(*refs):
        in_refs = refs[:nb]
        s_sems, r_sems = refs[nb], refs[nb + 1]
        me = (lax.axis_index("x"), lax.axis_index("y"), lax.axis_index("c"))
        for k, (src, dst, to) in enumerate(sends(in_refs)):
            _remote(src, dst, s_sems, r_sems, k, to).wait_send()
        for k, win in enumerate(arrivals(in_refs)):
            _remote(win, win, s_sems, r_sems, k, me).wait_recv()

    outs = pl.pallas_call(
        body, name=name,
        out_shape=[pltpu.HBM(b.shape, b.dtype) for b in bufs],
        in_specs=[HBM_SPEC] * nb + [SEM_SPEC, SEM_SPEC, HBM_ANY],
        out_specs=[HBM_SPEC] * nb,
        input_output_aliases={i: i for i in range(nb)},
        compiler_params=pltpu.CompilerParams(has_side_effects=DATAFLOW),
    )(*bufs, send_sems, recv_sems, after)
    return list(outs)


MIX_W = ('w_in', 'w_mem_kv', 'w_out')
FFN_W = ('w_gate_up', 'w_down')
LAYER_W = MIX_W + FFN_W


def _place_own(l, names, shards, small, sc):
    nw = len(names)
    has_small = small is not None
    n_ops = nw + (1 if has_small else 0)

    def body(sc_ref, *refs):
        for src, dst in zip(refs[:n_ops], refs[n_ops:]):
            dst[...] = src[...]

    in_specs, out_specs, out_shape, ops = [], [], [], list(shards)
    for nm in names:
        L, R, C, kind, rs, cs, rh = _geom(nm)
        in_specs.append(pl.BlockSpec((1, rs, cs), lambda i, sc_ref: (0, 0, 0)))
        if kind == 'row':
            out_specs.append(pl.BlockSpec((1, rs, cs), lambda i, sc_ref: (0, sc_ref[0], 0)))
        else:
            out_specs.append(pl.BlockSpec((1, rs, cs), lambda i, sc_ref: (0, 0, sc_ref[0])))
        out_shape.append(jax.ShapeDtypeStruct((1, R, C), BF))
    if has_small:
        in_specs.append(pl.BlockSpec((SMALL_ROWS, 256), lambda i, sc_ref: (0, 0)))
        out_specs.append(pl.BlockSpec((None, SMALL_ROWS, 256), lambda i, sc_ref: (sc_ref[0], 0, 0)))
        out_shape.append(jax.ShapeDtypeStruct((N_CHIPS, SMALL_ROWS, 256), F32))
        ops.append(small)
    return pl.pallas_call(
        body, name=f"{l}_place_own_shard",
        grid_spec=pltpu.PrefetchScalarGridSpec(num_scalar_prefetch=1, grid=(1,), in_specs=in_specs, out_specs=out_specs),
        out_shape=out_shape,
        compiler_params=_cp(("arbitrary",)),
    )(sc, *ops)


def _gather_start(l, names, shards, small, sc, after=None):
    nw = len(names)
    has_small = small is not None
    fulls = _place_own(l, names, shards, small, sc)
    bufs = list(shards) + ([small] if has_small else []) + list(fulls)
    n_src = nw + (1 if has_small else 0)

    def sends(refs):
        x, y, c, chips = _place()
        s = 2 * x + y
        out = []
        for (px, py) in chips:
            for wi, nm in enumerate(names):
                out.append((_shard_half(refs[wi], nm, c), _full_win(refs[n_src + wi], nm, s, c), (px, py, c)))
            if has_small:
                out.append((refs[nw], refs[n_src + nw].at[s], (px, py, c)))
        return out

    def arrivals(refs):
        x, y, c, chips = _place()
        out = []
        for (px, py) in chips:
            sp = 2 * px + py
            for wi, nm in enumerate(names):
                out.append(_full_win(refs[n_src + wi], nm, sp, c))
            if has_small:
                out.append(refs[n_src + nw].at[sp])
        return out

    n_copies = 3 * n_src
    send_sems, recv_sems, bufs, token = _split_start(f"{l}_gather_ici_start", bufs, n_copies, sends, after)
    return dict(l=l, names=names, has_small=has_small, sems=(send_sems, recv_sems), bufs=bufs, sends=sends,
                arrivals=arrivals, token=token)


def _gather_forward(st, after):
    l, names = st['l'], st['names']
    nw = len(names)
    n_src = nw + (1 if st['has_small'] else 0)
    bufs = _split_wait(f"{l}_gather_ici_wait", *st['sems'], st['bufs'], after, st['sends'], st['arrivals'])
    fulls = bufs[n_src:n_src + nw]
    small_all = bufs[n_src + nw] if st['has_small'] else None

    def sends(refs):
        x, y, c, chips = _place()
        out = []
        for (px, py) in chips:
            sp = 2 * px + py
            for wi, nm in enumerate(names):
                w = _full_win(refs[wi], nm, sp, c)
                out.append((w, w, (x, y, 1 - c)))
        return out

    def arrivals(refs):
        x, y, c, chips = _place()
        out = []
        for (px, py) in chips:
            sp = 2 * px + py
            for wi, nm in enumerate(names):
                out.append(_full_win(refs[wi], nm, sp, 1 - c))
        return out

    send_sems, recv_sems, fulls, token = _split_start(f"{l}_gather_d2d_start", fulls, 3 * nw, sends)
    return dict(l=l, names=names, sems=(send_sems, recv_sems), bufs=fulls, sends=sends, arrivals=arrivals,
                small_all=small_all, token=token)


def _gather_finish(st, after):
    fulls = _split_wait(f"{st['l']}_gather_d2d_wait", *st['sems'], st['bufs'], after, st['sends'], st['arrivals'])
    return dict(zip(st['names'], fulls)), st['small_all']


def _reduce_start(tag, names, grads):
    nw = len(names)
    recv = [lax.empty((1,) + _halves_shape(nm)[1:], F32) for nm in names]
    bufs = [grads[nm] for nm in names] + recv

    def windows(refs, half_of):
        x, y, c, _ = _place()
        h = half_of(c)
        out = []
        for wi, nm in enumerate(names):
            L, R, C, kind, rs, cs, rh = _geom(nm)
            if kind == 'row':
                for sp in range(N_CHIPS):
                    out.append((_full_win(refs[wi], nm, sp, h), _halves_win(refs[nw + wi], nm, sp)))
            else:
                out.append((refs[wi].at[:, pl.ds(h * rh, rh), :], refs[nw + wi]))
        return out

    def sends(refs):
        x, y, c, _ = _place()
        return [(src, dst, (x, y, 1 - c)) for src, dst in windows(refs, lambda c: 1 - c)]

    def arrivals(refs):
        return [dst for _, dst in windows(refs, lambda c: c)]

    n_copies = sum(N_CHIPS if BIG[nm][1] == 'row' else 1 for nm in names)
    send_sems, recv_sems, bufs, token = _split_start(tag + "_halves_start", bufs, n_copies, sends)
    return dict(tag=tag, names=names, sems=(send_sems, recv_sems), bufs=bufs, sends=sends, arrivals=arrivals, token=token)


def _reduce_mid(st, after, sc):
    tag, names = st['tag'], st['names']
    nw = len(names)
    bufs = _split_wait(tag + "_halves_wait", *st['sems'], st['bufs'], after, st['sends'], st['arrivals'])
    halves, own = [], []
    for wi, nm in enumerate(names):
        hb, ow = _add_halves(nm, bufs[wi], bufs[nw + wi], sc, tag)
        halves.append(hb)
        own.append(ow)
    pieces = [lax.empty((3, 1) + _half_shape(nm)[1:], BF) for nm in names]

    def sends(refs):
        x, y, c, chips = _place()
        out = []
        for j, (px, py) in enumerate(chips):
            for wi, nm in enumerate(names):
                out.append((_halves_win(refs[wi], nm, 2 * px + py), refs[nw + wi].at[j], (px, py, c)))
        return out

    def arrivals(refs):
        return [refs[nw + wi].at[j] for j in range(3) for wi in range(nw)]

    send_sems, recv_sems, bufs, token = _split_start(tag + "_pieces_start", halves + pieces, 3 * nw, sends)
    return dict(tag=tag, names=names, sems=(send_sems, recv_sems), bufs=bufs, sends=sends, arrivals=arrivals, own=own,
                token=token)


def _reduce_late(st, after, sc):
    tag, names = st['tag'], st['names']
    nw = len(names)
    bufs = _split_wait(tag + "_pieces_wait", *st['sems'], st['bufs'], after, st['sends'], st['arrivals'])
    gsh = [_sum_pieces(nm, st['own'][wi], bufs[nw + wi], sc, tag) for wi, nm in enumerate(names)]

    def sends(refs):
        x, y, c, _ = _place()
        return [(_shard_half(refs[wi], nm, c), _shard_half(refs[wi], nm, c), (x, y, 1 - c)) for wi, nm in enumerate(names)]

    def arrivals(refs):
        x, y, c, _ = _place()
        return [_shard_half(refs[wi], nm, 1 - c) for wi, nm in enumerate(names)]

    send_sems, recv_sems, bufs, token = _split_start(tag + "_share_start", gsh, nw, sends)
    return dict(tag=tag, names=names, sems=(send_sems, recv_sems), bufs=bufs, sends=sends, arrivals=arrivals, token=token)


def _reduce_finish(st, after):
    gsh = _split_wait(st['tag'] + "_share_wait", *st['sems'], st['bufs'], after, st['sends'], st['arrivals'])
    return dict(zip(st['names'], gsh))


def _add_halves(name, g, r, sc, tag):
    _, R, C, kind, rs, cs, rh = _geom(name)
    L = g.shape[0]
    tr = rh if kind == 'row' else 256
    nr = rh // tr

    def body(sc_ref, g_ref, r_ref, hb_ref, own_ref):
        sp = pl.program_id(2)
        tot = g_ref[...] + r_ref[...]
        hb_ref[...] = tot.astype(hb_ref.dtype)

        @pl.when(sp == sc_ref[0])
        def _():
            own_ref[...] = tot

    if kind == 'row':
        g_map = lambda l, ri, sp, sc_ref: (l, sp * 2 + sc_ref[1], 0)
        h_map = lambda l, ri, sp, sc_ref: (l, sp, 0)
    else:
        g_map = lambda l, ri, sp, sc_ref: (l, sc_ref[1] * nr + ri, sp)
        h_map = lambda l, ri, sp, sc_ref: (l, ri, sp)
    own_map = lambda l, ri, sp, sc_ref: (l, ri, 0)
    blk = (None, tr, cs)
    return pl.pallas_call(
        body, name=tag + "_add_halves_" + name,
        grid_spec=pltpu.PrefetchScalarGridSpec(
            num_scalar_prefetch=1, grid=(L, nr, N_CHIPS),
            in_specs=[pl.BlockSpec(blk, g_map), pl.BlockSpec(blk, h_map)],
            out_specs=[pl.BlockSpec(blk, h_map), pl.BlockSpec(blk, own_map)]),
        out_shape=[jax.ShapeDtypeStruct((L,) + _halves_shape(name)[1:], BF),
                   jax.ShapeDtypeStruct((L,) + _half_shape(name)[1:], F32)],
        compiler_params=_cp(("parallel", "parallel", "arbitrary")),
    )(sc, g, r)


def _sum_pieces(name, own, pieces, sc, tag):
    _, R, C, kind, rs, cs, rh = _geom(name)
    L = own.shape[0]
    tr = rh if kind == 'row' else 256
    nr = rh // tr

    def body(sc_ref, o_ref, p_ref, out_ref):
        out_ref[...] = o_ref[...] + p_ref[0].astype(F32) + p_ref[1].astype(F32) + p_ref[2].astype(F32)

    blk = (None, tr, cs)
    return pl.pallas_call(
        body, name=tag + "_sum_pieces_" + name,
        grid_spec=pltpu.PrefetchScalarGridSpec(
            num_scalar_prefetch=1, grid=(L, nr),
            in_specs=[pl.BlockSpec(blk, lambda l, ri, sc_ref: (l, ri, 0)),
                      pl.BlockSpec((3, None, tr, cs), lambda l, ri, sc_ref: (0, l, ri, 0))],
            out_specs=pl.BlockSpec(blk, lambda l, ri, sc_ref: (l, sc_ref[1] * nr + ri, 0))),
        out_shape=jax.ShapeDtypeStruct((L,) + _shard_shape(name)[1:], F32),
        compiler_params=_cp(("parallel", "parallel")),
    )(sc, own, pieces)


def _small_gather_start(v, sc):
    rows = v.shape[0]

    def place(sc_ref, v_ref, o_ref):
        o_ref[...] = v_ref[...]

    slots = pl.pallas_call(
        place, name="small_grads_place_own",
        grid_spec=pltpu.PrefetchScalarGridSpec(
            num_scalar_prefetch=1, grid=(1,),
            in_specs=[pl.BlockSpec((rows, 128), lambda i, sc_ref: (0, 0))],
            out_specs=pl.BlockSpec((None, rows, 128), lambda i, sc_ref: (2 * sc_ref[0] + sc_ref[1], 0, 0))),
        out_shape=jax.ShapeDtypeStruct((8, rows, 128), F32),
        compiler_params=_cp(("arbitrary",)),
    )(sc, v)

    def peers():
        x, y, c, _ = _place()
        flips = [(fx, fy, fc) for fx in (0, 1) for fy in (0, 1) for fc in (0, 1)][1:]
        return [((1 - x if fx else x), (1 - y if fy else y), (1 - c if fc else c)) for fx, fy, fc in flips]

    def sends(refs):
        x, y, c, _ = _place()
        return [(refs[0], refs[1].at[4 * x + 2 * y + c], p) for p in peers()]

    def arrivals(refs):
        return [refs[1].at[4 * px + 2 * py + pc] for px, py, pc in peers()]

    send_sems, recv_sems, bufs, token = _split_start("small_grads_gather_start", [v, slots], 7, sends)
    return dict(sems=(send_sems, recv_sems), bufs=bufs, sends=sends, arrivals=arrivals, token=token)


def _small_gather_finish(st, after):
    return _split_wait("small_grads_gather_wait", *st['sems'], st['bufs'], after, st['sends'], st['arrivals'])[1]


def _sum8(v8, *, name, tr=336):
    rows = v8.shape[1]
    tr = min(tr, rows)
    assert rows % tr == 0

    def body(v_ref, o_ref):
        tot = v_ref[0]
        for d in range(1, 8):
            tot = tot + v_ref[d]
        o_ref[...] = tot

    return pl.pallas_call(
        body, name=name, grid=(rows // tr,),
        in_specs=[pl.BlockSpec((8, tr, 128), lambda i: (0, i, 0))], out_specs=pl.BlockSpec((tr, 128), lambda i: (i, 0)),
        out_shape=jax.ShapeDtypeStruct((rows, 128), F32),
        compiler_params=_cp(("parallel",)),
    )(v8)


def _block_diag(w_pool_l):
    wbd = jnp.zeros((MAIN_W, MAIN_W), F32)
    for gi in range(len(POOL_WINDOWS)):
        wbd = lax.dynamic_update_slice(wbd, w_pool_l[gi], (gi * POOL_GROUP, gi * POOL_GROUP))
    return wbd.astype(BF)


def _unpack_small(small_all):
    ng = small_all[:, :16, :].reshape(N_CHIPS, DEPTH, 4, 256).transpose(1, 2, 0, 3).reshape(DEPTH, 4, D_MODEL)
    ps = small_all[:, 16:18, :POOL_GROUP].transpose(1, 0, 2).reshape(N_A, MAIN_W)
    return ng, ps


def _local_step(x, mem, positions, on_forward, on_backward, mem_norm, w_pool, kv_norm, target):
    B, S, _ = x.shape
    T = B * S
    xc = x.reshape(T, D_MODEL)
    memf = mem.reshape(B * N_MEM, D_MODEL)
    tgt = target.reshape(T, D_MODEL)
    cos, sin = _rope_tables(positions.reshape(T, 1), name="rope_tables")
    wbd = [_block_diag(w_pool[l]) for l in range(N_A)]
    nbo = D_FF // 256
    fw = []
    rk = rv = None
    kv_saved = None
    wts = []
    norm_gains = pool_scale = y2 = None

    def tied(vec, tok):
        return vec if tok is None else vec + tok

    for l in range(DEPTH):
        t = f"l{l}_"
        got = on_forward('start', l, y2)
        wts.append(dict(got[0]))
        if l == 0:
            norm_gains, pool_scale = _unpack_small(got[1])
        sv = {'x_in': xc}
        h0, sv['r0'] = _norm_fwd(xc, tied(norm_gains[l, 0], got[2]), name=t + "norm0", out_dtype=BF)
        z, = _mm(h0, wts[l]['w_in'], b_layer=0, name=t + "mm_in")
        memn, sv['rm'] = _norm_fwd(memf, mem_norm[l], name=t + "norm_mem", out_dtype=BF, tm=256)
        kvm, = _mm(memn, wts[l]['w_mem_kv'], b_layer=0, name=t + "mm_memkv", out_dtypes=(BF,))
        if l < N_A:
            ycat, sv['p'] = _pool_fwd(z, wbd[l], pool_scale[l], B, S, name=t + "pool_fwd")
        else:
            rq = _rope_apply(z, cos, sin, name=t + "rope_q", out_dtype=F32)
            o = lax.empty((T, MAIN_W), F32)
            lse = lax.empty((T, MAIN_W), F32)
            for g in range(3):
                o, lse = _dil_fwd_group(g, rq, rk, rv, o, lse, B, S, name=t + f"dil_fwd{g}")
            ycat = _dil_combine_fwd(o, lse, lax.empty((T, D_MODEL), BF), name=t + "dil_combine")
            sv.update(rq=rq, o=o, lse=lse)
        ycat, sv['lse_m'] = _memattn_fwd(z, kvm, ycat, B, S, name=t + "memattn_fwd")
        tok = on_forward('mid', l, ycat)
        y1, = _mm(ycat, wts[l]['w_out'], b_layer=0, name=t + "mm_out")
        wts[l].update(on_forward('ffn', l, y1)[0])
        x1, sv['r1'] = _norm_fwd(y1, tied(norm_gains[l, 1], tok), name=t + "norm1", res=xc)
        h2, sv['r2'] = _norm_fwd(x1, norm_gains[l, 2], name=t + "norm2", out_dtype=BF)
        gg, uu, aa = _mm(h2, wts[l]['w_gate_up'], b_layer=0, b_offsets=(0, nbo), out_n=D_FF, tn=256, name=t + "mm_gate_up",
                         epilogue=_swiglu_fwd_epilogue, out_dtypes=(BF, BF, BF))
        on_forward('post', l, gg)
        y2, = _mm(aa, wts[l]['w_down'], b_layer=0, tk=D_FF, name=t + "mm_down")
        x2, sv['r3'] = _norm_fwd(y2, norm_gains[l, 3], name=t + "norm3", res=x1)
        sv.update(h0=h0, z=z, memn=memn, kvm=kvm, ycat=ycat, y1=y1, x1=x1, h2=h2, gg=gg, uu=uu, aa=aa, y2=y2)
        fw.append(sv)
        xc = x2
        if l == N_A - 1:
            kvn, rkv = _norm_fwd(xc, kv_norm, name="norm_kv", out_dtype=BF)
            kv, = _mm(kvn, wts[N_A - 1]['w_kv'], b_layer=0, name="mm_kv")
            rk, rv = _rope_apply(kv, cos, sin, name="rope_k", passthrough=True, out_dtype=F32)
            kv_saved = (xc, kvn, rkv)

    loss, dx = _loss(xc, tgt, name="loss")

    d_ng = [[None] * 4 for _ in range(DEPTH)]
    d_memnorm = [None] * DEPTH
    d_wbd = [None] * N_A
    d_pscale = [None] * N_A
    d_kvnorm = None
    kv_parts = []
    tok = None

    def as3d(gl):
        return {nm: g.reshape((1,) + g.shape) for nm, g in gl.items()}

    for l in reversed(range(DEPTH)):
        t = f"l{l}_b_"
        sv = fw[l]
        gl = {}
        dy2, d_ng[l][3] = _norm_bwd(dx, sv['y2'], sv['r3'], tied(norm_gains[l, 3], tok), name=t + "norm3", out_dtype=BF)
        gl['w_down'], = _mm(sv['aa'], dy2, ta=True, tm=1408, name=t + "dw_down")
        dg, du = _mm(dy2, wts[l]['w_down'], tb=True, b_layer=0, tn=256, name=t + "d_act",
                     extras=((sv['gg'], 'tile'), (sv['uu'], 'tile')), epilogue=_swiglu_bwd_epilogue, out_dtypes=(BF, BF))
        gl['w_gate_up'], = _mm(sv['h2'], (dg, du), ta=True, tn=1408, tk=512, name=t + "dw_gate_up")
        dh2, = _mm((dg, du), wts[l]['w_gate_up'], tb=True, b_layer=0, tk=1408, name=t + "d_h2", out_dtypes=(BF,))
        dx1, d_ng[l][2] = _norm_bwd(dh2, sv['x1'], sv['r2'], norm_gains[l, 2], name=t + "norm2", add=dx)
        tok = on_backward('ffn', l, dx1, as3d(gl))
        dy1, d_ng[l][1] = _norm_bwd(dx1, sv['y1'], sv['r1'], tied(norm_gains[l, 1], tok), name=t + "norm1", out_dtype=BF)
        gl['w_out'], = _mm(sv['ycat'], dy1, ta=True, name=t + "dw_out")
        dycat, = _mm(dy1, wts[l]['w_out'], tb=True, b_layer=0, name=t + "d_ycat")
        dz = lax.empty((T, D_MODEL), BF)
        dz, dkm, dvm = _memattn_bwd(dycat, sv['z'], sv['kvm'], sv['lse_m'], dz, B, S, name=t + "memattn")
        if l < N_A:
            dz, d_wbd[l], d_pscale[l] = _pool_bwd(dycat, sv['p'], wbd[l], pool_scale[l], dz, B, S, name=t + "pool")
        else:
            do, cb = _dil_combine_bwd(dycat, sv['o'], sv['lse'], name=t + "dil_combine")
            acc = tuple(lax.empty((T, MAIN_W), F32) for _ in range(5))
            for g in range(3):
                acc = _dil_bwd_group(g, sv['rq'], rk, rv, do, cb, sv['lse'], acc, B, S, name=t + f"dil{g}")
            dz = _rope_apply(acc[0], cos, sin, name=t + "rope_q", sign=-1.0, alias=dz)
            kv_parts.append(acc[1:])
        tok = on_backward('mix', l, dz, as3d(gl))
        gl['w_in'], = _mm(sv['h0'], dz, ta=True, name=t + "dw_in")
        dh0, = _mm(dz, wts[l]['w_in'], tb=True, b_layer=0, name=t + "d_h0", out_dtypes=(BF,))
        dx, d_ng[l][0] = _norm_bwd(dh0, sv['x_in'], sv['r0'], tied(norm_gains[l, 0], tok), name=t + "norm0", add=dx1)
        gl['w_mem_kv'], = _mm(sv['memn'], (dkm, dvm), ta=True, tn=256, name=t + "dw_memkv")
        dmemn, = _mm((dkm, dvm), wts[l]['w_mem_kv'], tb=True, b_layer=0, tk=256, name=t + "d_memn", out_dtypes=(BF,))
        _, d_memnorm[l] = _norm_bwd(dmemn, memf, sv['rm'], mem_norm[l], name=t + "norm_mem", out_dtype=BF, tm=256)
        if l == N_A:
            dk, dv = _kv_grad(kv_parts, cos, sin, B, S, name="kv_grad")
            x_kv, kvn, rkv = kv_saved
            gl['w_kv'], = _mm(kvn, (dk, dv), ta=True, tn=768, name="dw_kv")
            dkvn, = _mm((dk, dv), wts[N_A - 1]['w_kv'], tb=True, b_layer=0, tk=768, name="d_kvn", out_dtypes=(BF,))
            dx, d_kvnorm = _norm_bwd(dkvn, x_kv, rkv, kv_norm, name="norm_kv_b", add=dx)
        tok = on_backward('end', l, dx, as3d(gl))

    small = {
        'norm_gains': jnp.stack([jnp.concatenate(d_ng[l], axis=0) for l in range(DEPTH)]),
        'mem_norm': jnp.concatenate(d_memnorm, axis=0),
        'kv_norm': d_kvnorm.reshape(D_MODEL),
        'pool_scale': jnp.concatenate(d_pscale, axis=0),
        'w_pool': jnp.stack([jnp.stack([d_wbd[l][gi * POOL_GROUP:(gi + 1) * POOL_GROUP, gi * POOL_GROUP:(gi + 1) * POOL_GROUP]
                                        for gi in range(len(POOL_WINDOWS))]) for l in range(N_A)]),
    }
    return loss, dx, small


SMALL_ORDER = ('norm_gains', 'mem_norm', 'kv_norm', 'pool_scale', 'w_pool')
SMALL_VEC_ROWS = 2560


def kernel(x, mem, positions, norm_gains, mem_norm, w_in, w_mem_kv, w_out, w_pool, pool_scale, kv_norm, w_kv, w_gate_up, w_down, loss_target, m_norm_gains, m_mem_norm, m_w_in, m_w_mem_kv, m_w_out, m_w_pool, m_pool_scale, m_kv_norm, m_w_kv, m_w_gate_up, m_w_down, v_norm_gains, v_mem_norm, v_w_in, v_w_mem_kv, v_w_out, v_w_pool, v_pool_scale, v_kv_norm, v_w_kv, v_w_gate_up, v_w_down):
    xi, yi, ci = lax.axis_index("x"), lax.axis_index("y"), lax.axis_index("c")
    s = 2 * xi + yi
    sc = jnp.stack([s, ci]).astype(jnp.int32)
    weights = dict(norm_gains=norm_gains, mem_norm=mem_norm, w_in=w_in, w_mem_kv=w_mem_kv, w_out=w_out, w_pool=w_pool,
                   pool_scale=pool_scale, kv_norm=kv_norm, w_kv=w_kv, w_gate_up=w_gate_up, w_down=w_down)
    moms = dict(norm_gains=m_norm_gains, mem_norm=m_mem_norm, w_in=m_w_in, w_mem_kv=m_w_mem_kv, w_out=m_w_out,
                w_pool=m_w_pool, pool_scale=m_pool_scale, kv_norm=m_kv_norm, w_kv=m_w_kv, w_gate_up=m_w_gate_up,
                w_down=m_w_down)
    vels = dict(norm_gains=v_norm_gains, mem_norm=v_mem_norm, w_in=v_w_in, w_mem_kv=v_w_mem_kv, w_out=v_w_out,
                w_pool=v_w_pool, pool_scale=v_pool_scale, kv_norm=v_kv_norm, w_kv=v_w_kv, w_gate_up=v_w_gate_up,
                w_down=v_w_down)

    small_w = jnp.zeros((SMALL_ROWS, 256), F32)
    small_w = lax.dynamic_update_slice(small_w, norm_gains.reshape(16, 256), (0, 0))
    small_w = lax.dynamic_update_slice(small_w, pool_scale, (16, 0))
    def shard_of(nm, l):
        return w_kv.astype(BF).reshape(_shard_shape('w_kv')) if nm == 'w_kv' else weights[nm][l:l + 1].astype(BF)

    groups = {'l0a': (0, MIX_W), 'l0b': (0, FFN_W)}
    groups.update({f"l{l}": (l, LAYER_W + (('w_kv',) if l == N_A - 1 else ())) for l in range(1, DEPTH)})
    on_ici, on_d2d, gathered = {}, {}, {}

    def start_group(tag, after):
        l, names = groups[tag]
        on_ici[tag] = _gather_start(tag, names, [shard_of(nm, l) for nm in names], small_w if tag == 'l0a' else None, sc,
                                    after)
        return on_ici[tag]['token'][0, 0]

    def on_forward(where, l, after):
        if where == 'start':
            if l == 0:
                start_group('l0a', None)
                st = on_ici.pop('l0a')
                fwd = _gather_forward(st, st['token'])
                w, small_all = _gather_finish(fwd, fwd['token'])
                return w, small_all, start_group('l0b', w['w_in'])
            gathered[l] = _gather_finish(on_d2d.pop(f"l{l}"), after)[0]
            tok = start_group(f"l{l + 1}", gathered[l]['w_in']) if l + 1 < DEPTH else None
            return {nm: w for nm, w in gathered[l].items() if nm not in FFN_W}, None, tok
        if where == 'mid' and l == 0:
            on_d2d['l0b'] = _gather_forward(on_ici.pop('l0b'), after)
            return start_group('l1', on_d2d['l0b']['token'])
        if where == 'ffn':
            if l == 0:
                return (_gather_finish(on_d2d.pop('l0b'), after)[0],)
            return ({nm: gathered[l][nm] for nm in FFN_W},)
        if where == 'post' and l + 1 < DEPTH:
            on_d2d[f"l{l + 1}"] = _gather_forward(on_ici.pop(f"l{l + 1}"), after)
        return None

    hook_of = {'ffn': 0, 'mix': 1, 'end': 2}
    active, reduced = [], {l: {} for l in range(DEPTH)}
    advance = {'mid': lambda st, after: _reduce_mid(st, after, sc), 'late': lambda st, after: _reduce_late(st, after, sc)}

    def run_hook(idx, after):
        toks = []
        for grp in list(active):
            while grp['plan'] and grp['plan'][0][1] <= idx:
                step = grp['plan'].pop(0)[0]
                if step == 'finish':
                    reduced[grp['layer']].update(_reduce_finish(grp['st'], after))
                    active.remove(grp)
                else:
                    grp['st'] = advance[step](grp['st'], after)
                    toks.append(grp['st']['token'][0, 0])
        return toks

    def on_backward(where, l, after, grads):
        idx = 3 * (DEPTH - 1 - l) + hook_of[where]
        toks = run_hook(idx, after)
        if where in ('ffn', 'end'):
            names = FFN_W if where == 'ffn' else tuple(nm for nm in grads if nm not in FFN_W)
            st = _reduce_start(f"l{l}_{where}_grads", names, {nm: grads[nm] for nm in names})
            plan = [('mid', idx + 1), ('late', idx + 3), ('finish', idx + 4)] if where == 'ffn' else \
                   [('mid', idx + 1), ('late', idx + 2), ('finish', idx + 3)]
            active.append(dict(layer=l, st=st, plan=plan))
            toks.append(st['token'][0, 0])
        return sum(toks) if toks else None

    loss, gx, gsmall = _local_step(x, mem, positions, on_forward, on_backward, mem_norm, w_pool, kv_norm, loss_target)
    loss = lax.psum(loss[0, 0], ("x", "y", "c"))

    vec = jnp.concatenate([gsmall[nm].reshape(-1) for nm in SMALL_ORDER])
    vec = jnp.pad(vec, (0, SMALL_VEC_ROWS * 128 - vec.shape[0])).reshape(SMALL_VEC_ROWS, 128)
    small_st = _small_gather_start(vec, sc)
    outs = {nm: None for nm in LAYER_W}

    def adamw_layers(layers):
        for l in layers:
            for nm in LAYER_W:
                outs[nm] = _adamw_layer(nm, l, weights[nm], reduced[l][nm], moms[nm], vels[nm], outs[nm])

    last = 3 * DEPTH
    run_hook(last, small_st['token'])
    adamw_layers(range(DEPTH - 1, 0, -1))
    run_hook(last + 1, outs[LAYER_W[-1]][0])
    tot = _sum8(_small_gather_finish(small_st, outs[LAYER_W[0]][0]), name="sum_small_grads", tr=512)
    run_hook(last + 2, tot)
    assert not active
    adamw_layers([0])
    tot = tot.reshape(-1)
    grads, off = {}, 0
    for nm in SMALL_ORDER:
        shape = (DEPTH, 4, D_MODEL) if nm == 'norm_gains' else (N_A, MAIN_W) if nm == 'pool_scale' else weights[nm].shape
        n = 1
        for dim in shape:
            n *= dim
        grads[nm] = tot[off:off + n].reshape(shape)
        off += n
    grads['norm_gains'] = lax.dynamic_slice(grads['norm_gains'], (0, 0, s * 256), (DEPTH, 4, 256))
    grads['pool_scale'] = lax.dynamic_slice(grads['pool_scale'], (0, s * POOL_GROUP), (N_A, POOL_GROUP))
    grads['w_kv'] = reduced[N_A]['w_kv'].reshape(w_kv.shape)

    order = ('norm_gains', 'mem_norm', 'w_in', 'w_mem_kv', 'w_out', 'w_pool', 'pool_scale', 'kv_norm', 'w_kv',
             'w_gate_up', 'w_down')
    deltas, new_m, new_v = {}, {}, {}
    for nm in order:
        if nm in LAYER_W:
            deltas[nm], new_m[nm], new_v[nm], grads[nm] = outs[nm]
        else:
            deltas[nm], new_m[nm], new_v[nm] = _adamw(weights[nm], grads[nm], moms[nm], vels[nm], name="adamw_" + nm)
    return (loss, gx.reshape(x.shape), *[grads[nm] for nm in order], *[deltas[nm] for nm in order],
            *[new_m[nm] for nm in order], *[new_v[nm] for nm in order])
```

```python
import functools

import jax
import jax.numpy as jnp
from jax import lax
from jax.experimental import pallas as pl
from jax.experimental.pallas import tpu as pltpu

F32 = jnp.float32
BF = jnp.bfloat16

D_MODEL = 1024
DEPTH = 4
N_A = 2
HEAD_DIM = 64
MEM_W = 256
MAIN_W = 768
D_FF = 2816
N_MEM = 256
POOL_WINDOWS = (2, 4, 8, 16)
POOL_GROUP = 192
DIL = (1, 4, 16)
STEPS = 128
ROPE_THETA = 10000.0
EPS = 1e-6
SCALE = HEAD_DIM ** -0.5
NEG = -1e30

ADAM_LR = 0.001
ADAM_B1 = 0.9
ADAM_B2 = 0.999
ADAM_EPS = 1e-08
ADAM_WD = 0.01
ADAM_STEP = 10

VMEM_LIMIT = 48 * 1024 * 1024
MESH = pl.DeviceIdType.MESH


def _cp(sem):
    return pltpu.CompilerParams(dimension_semantics=sem, vmem_limit_bytes=VMEM_LIMIT)


def _mm(a, b, *, name, ta=False, tb=False, tm=1024, tn=512, tk=1024, b_layer=None, b_offsets=(0,),
        extras=(), epilogue=None, out_dtypes=(F32,), out_n=None, stack=None):
    a_pair = isinstance(a, (tuple, list))
    b_pair = isinstance(b, (tuple, list))
    a0 = a[0] if a_pair else a
    b0 = b[0] if b_pair else b
    a_rows, a_cols = a0.shape
    if a_pair:
        a_cols *= 2
    b_rows, b_cols = b0.shape[-2:]
    if b_pair:
        b_cols *= 2
    M, K = (a_cols, a_rows) if ta else (a_rows, a_cols)
    N = b_rows if tb else b_cols
    if out_n is not None:
        N = out_n
    tm, tn, tk = min(tm, M), min(tn, N), min(tk, K)
    assert M % tm == 0 and N % tn == 0 and K % tk == 0, (name, M, N, K, tm, tn, tk)
    nk = K // tk
    n_acc = len(b_offsets)

    if a_pair:
        a_half = (a0.shape[1] // (tm if ta else tk))
    if b_pair:
        b_half = (b0.shape[1] // (tk if tb else tn))

    def a_map(sel):
        def f(i, j, k):
            r, c = (k, i) if ta else (i, k)
            if a_pair:
                c = jnp.clip(c - sel * a_half, 0, a_half - 1)
            return (r, c)
        return f

    def b_map(sel, off):
        def f(i, j, k):
            r, c = (j + off, k) if tb else (k, j + off)
            if b_pair:
                c = jnp.clip(c - sel * b_half, 0, b_half - 1)
            if b_layer is not None:
                return (b_layer, r, c)
            return (r, c)
        return f

    a_blk = (tk, tm) if ta else (tm, tk)
    b_blk = (tn, tk) if tb else (tk, tn)
    if b_layer is not None:
        b_blk = (None,) + b_blk
    in_specs, operands = [], []
    for sel in range(2 if a_pair else 1):
        in_specs.append(pl.BlockSpec(a_blk, a_map(sel)))
        operands.append(a[sel] if a_pair else a)
    n_a = len(operands)
    for off in b_offsets:
        for sel in range(2 if b_pair else 1):
            in_specs.append(pl.BlockSpec(b_blk, b_map(sel, off)))
            operands.append(b[sel] if b_pair else b)
    n_b = len(operands) - n_a
    for arr, kind in extras:
        if kind == 'tile':
            in_specs.append(pl.BlockSpec((tm, tn), lambda i, j, k: (i, j)))
        elif kind == 'row':
            in_specs.append(pl.BlockSpec((tm, 1), lambda i, j, k: (i, 0)))
        else:
            in_specs.append(pl.BlockSpec((1, tn), lambda i, j, k: (0, j)))
        operands.append(arr)
    n_e = len(extras)
    n_o = len(out_dtypes)
    dims = (((0,) if ta else (1,), (1,) if tb else (0,)), ((), ()))

    def body(*refs):
        a_refs = refs[:n_a]
        b_refs = refs[n_a:n_a + n_b]
        e_refs = refs[n_a + n_b:n_a + n_b + n_e]
        n_in = n_a + n_b + n_e + (1 if stack is not None else 0)
        o_refs = refs[n_in:n_in + n_o]
        acc_refs = refs[n_in + n_o:]
        i, j, k = pl.program_id(0), pl.program_id(1), pl.program_id(2)
        if a_pair:
            cidx = i if ta else k
            av = jnp.where(cidx < a_half, a_refs[0][...], a_refs[1][...])
        else:
            av = a_refs[0][...]
        av = av.astype(BF)
        prods = []
        for q in range(n_acc):
            if b_pair:
                cidx = (k if tb else j) + b_offsets[q]
                bv = jnp.where(cidx < b_half, b_refs[2 * q][...], b_refs[2 * q + 1][...])
            else:
                bv = b_refs[q][...]
            prods.append(lax.dot_general(av, bv.astype(BF), dims, preferred_element_type=F32))

        def finish(accs):
            outs = epilogue(accs, *[r[...] for r in e_refs]) if epilogue is not None else accs
            for o_ref, o in zip(o_refs, outs):
                o_ref[...] = o.astype(o_ref.dtype)

        if nk == 1:
            finish(prods)
        else:
            @pl.when(k == 0)
            def _():
                for r, p in zip(acc_refs, prods):
                    r[...] = p

            @pl.when(k > 0)
            def _():
                for r, p in zip(acc_refs, prods):
                    r[...] += p

            @pl.when(k == nk - 1)
            def _():
                finish([r[...] for r in acc_refs])

    if stack is not None:
        buf, layer = stack
        assert n_o == 1 and buf.shape[1:] == (M, N)
        return pl.pallas_call(
            body, name=name,
            grid=(M // tm, N // tn, nk),
            in_specs=in_specs + [pl.BlockSpec(memory_space=pl.ANY)],
            out_specs=[pl.BlockSpec((None, tm, tn), lambda i, j, k: (layer, i, j))],
            out_shape=[jax.ShapeDtypeStruct(buf.shape, buf.dtype)],
            scratch_shapes=[pltpu.VMEM((tm, tn), F32) for _ in range(n_acc if nk > 1 else 0)],
            input_output_aliases={len(operands): 0},
            compiler_params=_cp(("parallel", "parallel", "arbitrary")),
        )(*operands, buf)[0]
    return pl.pallas_call(
        body, name=name,
        grid=(M // tm, N // tn, nk),
        in_specs=in_specs,
        out_specs=[pl.BlockSpec((tm, tn), lambda i, j, k: (i, j)) for _ in range(n_o)],
        out_shape=[jax.ShapeDtypeStruct((M, N), dt) for dt in out_dtypes],
        scratch_shapes=[pltpu.VMEM((tm, tn), F32) for _ in range(n_acc if nk > 1 else 0)],
        compiler_params=_cp(("parallel", "parallel", "arbitrary")),
    )(*operands)


def _norm_fwd(x, g, *, name, res=None, out_dtype=F32, tm=512):
    T, Dm = x.shape
    has_res = res is not None

    def body(*refs):
        if has_res:
            x_ref, g_ref, r_ref, y_ref, s_ref = refs
        else:
            x_ref, g_ref, y_ref, s_ref = refs
        xv = x_ref[...]
        rstd = lax.rsqrt(jnp.mean(xv * xv, axis=-1, keepdims=True) + EPS)
        y = xv * rstd * g_ref[...]
        if has_res:
            y = r_ref[...] + y
        y_ref[...] = y.astype(y_ref.dtype)
        s_ref[...] = rstd

    row = pl.BlockSpec((tm, Dm), lambda i: (i, 0))
    in_specs = [row, pl.BlockSpec((1, Dm), lambda i: (0, 0))] + ([row] if has_res else [])
    ops = [x, _in_hbm(g.reshape(1, Dm))] + ([res] if has_res else [])
    return pl.pallas_call(
        body, name=name, grid=(T // tm,), in_specs=in_specs,
        out_specs=[row, pl.BlockSpec((tm, 1), lambda i: (i, 0))],
        out_shape=[jax.ShapeDtypeStruct((T, Dm), out_dtype), jax.ShapeDtypeStruct((T, 1), F32)],
        compiler_params=_cp(("parallel",)),
    )(*ops)


def _norm_bwd(dout, x, rstd, g, *, name, add=None, out_dtype=F32, tm=512):
    T, Dm = x.shape
    has_add = add is not None
    nt = T // tm

    def body(*refs):
        if has_add:
            do_ref, x_ref, s_ref, g_ref, a_ref, dx_ref, dg_ref, acc = refs
        else:
            do_ref, x_ref, s_ref, g_ref, dx_ref, dg_ref, acc = refs
        i = pl.program_id(0)
        do = do_ref[...].astype(F32)
        xh = x_ref[...] * s_ref[...]
        gd = do * g_ref[...]
        dx = s_ref[...] * (gd - xh * jnp.mean(gd * xh, axis=-1, keepdims=True))
        if has_add:
            dx = dx + a_ref[...].astype(F32)
        dx_ref[...] = dx.astype(dx_ref.dtype)
        part = jnp.sum((do * xh).reshape(tm // 8, 8, Dm), axis=0)

        @pl.when(i == 0)
        def _():
            acc[...] = part

        @pl.when(i > 0)
        def _():
            acc[...] += part

        @pl.when(i == nt - 1)
        def _():
            dg_ref[...] = jnp.sum(acc[...], axis=0, keepdims=True)

    row = pl.BlockSpec((tm, Dm), lambda i: (i, 0))
    in_specs = [row, row, pl.BlockSpec((tm, 1), lambda i: (i, 0)), pl.BlockSpec((1, Dm), lambda i: (0, 0))]
    ops = [dout, x, _in_hbm(rstd), _in_hbm(g.reshape(1, Dm))]
    if has_add:
        in_specs.append(row)
        ops.append(add)
    return pl.pallas_call(
        body, name=name, grid=(nt,), in_specs=in_specs,
        out_specs=[row, pl.BlockSpec((1, Dm), lambda i: (0, 0))],
        out_shape=[jax.ShapeDtypeStruct((T, Dm), out_dtype), jax.ShapeDtypeStruct((1, Dm), F32)],
        scratch_shapes=[pltpu.VMEM((8, Dm), F32)],
        compiler_params=_cp(("arbitrary",)),
    )(*ops)


def _swiglu_fwd_epilogue(accs):
    g, u = accs
    return g, u, g * jax.nn.sigmoid(g) * u


def _swiglu_bwd_epilogue(accs, g, u):
    da = accs[0]
    g = g.astype(F32)
    u = u.astype(F32)
    sig = jax.nn.sigmoid(g)
    return da * u * (sig * (1.0 + g * (1.0 - sig))), da * (g * sig)


def _rope_tables(pos, *, name, tm=1024):
    T = pos.shape[0]
    half = HEAD_DIM // 2
    freqs = ROPE_THETA ** (-jnp.arange(half, dtype=F32) / half)
    freqs = jnp.tile(freqs, 4).reshape(1, 128)

    def body(p_ref, f_ref, c_ref, s_ref):
        ang = p_ref[...].astype(F32) * f_ref[...]
        lane = lax.broadcasted_iota(jnp.int32, ang.shape, 1)
        c_ref[...] = jnp.cos(ang)
        s_ref[...] = jnp.where(lane % HEAD_DIM < half, -1.0, 1.0) * jnp.sin(ang)

    tab = pl.BlockSpec((tm, 128), lambda i: (i, 0))
    return pl.pallas_call(
        body, name=name, grid=(T // tm,),
        in_specs=[pl.BlockSpec((tm, 1), lambda i: (i, 0)), pl.BlockSpec((1, 128), lambda i: (0, 0))],
        out_specs=[tab, tab],
        out_shape=[jax.ShapeDtypeStruct((T, 128), F32)] * 2,
        compiler_params=_cp(("parallel",)),
    )(pos, freqs)


def _rot(x, cos, sin, sign):
    W = x.shape[1]
    half = HEAD_DIM // 2
    reps = W // 128
    c = jnp.concatenate([cos] * reps, axis=1) if reps > 1 else cos
    s = jnp.concatenate([sin] * reps, axis=1) if reps > 1 else sin
    lane = lax.broadcasted_iota(jnp.int32, x.shape, 1)
    swapped = jnp.where(lane % HEAD_DIM < half, pltpu.roll(x, W - half, axis=1), pltpu.roll(x, half, axis=1))
    return x * c + (sign * s) * swapped


def _rope_apply(x, cos, sin, *, name, sign=1.0, width=MAIN_W, passthrough=False, out_dtype=BF, alias=None,
                out_cols=None, tm=512):
    T = x.shape[0]

    def body(*refs):
        if passthrough:
            x_ref, v_ref, c_ref, s_ref, o_ref, ov_ref = refs
            ov_ref[...] = v_ref[...].astype(ov_ref.dtype)
        elif alias is not None:
            x_ref, c_ref, s_ref, _, o_ref = refs
        else:
            x_ref, c_ref, s_ref, o_ref = refs
        o_ref[...] = _rot(x_ref[...].astype(F32), c_ref[...], s_ref[...], sign).astype(o_ref.dtype)

    blk0 = pl.BlockSpec((tm, width), lambda i: (i, 0))
    blk1 = pl.BlockSpec((tm, width), lambda i: (i, 1))
    tab = pl.BlockSpec((tm, 128), lambda i: (i, 0))
    if passthrough:
        return pl.pallas_call(
            body, name=name, grid=(T // tm,), in_specs=[blk0, blk1, tab, tab], out_specs=[blk0, blk0],
            out_shape=[jax.ShapeDtypeStruct((T, width), out_dtype)] * 2,
            compiler_params=_cp(("parallel",)),
        )(x, x, cos, sin)
    if alias is not None:
        return pl.pallas_call(
            body, name=name, grid=(T // tm,),
            in_specs=[blk0, tab, tab, pl.BlockSpec(memory_space=pl.ANY)], out_specs=blk0,
            out_shape=jax.ShapeDtypeStruct(alias.shape, alias.dtype),
            input_output_aliases={3: 0},
            compiler_params=_cp(("parallel",)),
        )(x, cos, sin, alias)
    return pl.pallas_call(
        body, name=name, grid=(T // tm,), in_specs=[blk0, tab, tab], out_specs=blk0,
        out_shape=jax.ShapeDtypeStruct((T, width), out_dtype),
        compiler_params=_cp(("parallel",)),
    )(x, cos, sin)


POOL_T = 256
POOL_HALO = 16


def _pool_lane_window(shape):
    lane = lax.broadcasted_iota(jnp.int32, shape, 1)
    w = jnp.full(shape, POOL_WINDOWS[0], jnp.int32)
    for gi in range(1, len(POOL_WINDOWS)):
        w = jnp.where(lane >= gi * POOL_GROUP, POOL_WINDOWS[gi], w)
    return w


def _pool_fwd(z, wbd, scale, B, S, *, name):
    T = z.shape[0]
    nt = S // POOL_T
    hb = POOL_T // POOL_HALO

    def body(z_ref, h_ref, w_ref, sc_ref, y_ref, p_ref, ext):
        i = pl.program_id(1)
        u = z_ref[...]
        ext[pl.ds(POOL_HALO, POOL_T), :] = u
        ext[pl.ds(0, POOL_HALO), :] = jnp.where(i > 0, h_ref[...], 0.0)
        win = _pool_lane_window((POOL_T, MAIN_W))
        acc = u
        for k in range(1, POOL_HALO):
            acc = acc + jnp.where(k < win, ext[pl.ds(POOL_HALO - k, POOL_T), :], 0.0)
        t = i * POOL_T + lax.broadcasted_iota(jnp.int32, (POOL_T, MAIN_W), 0)
        cnt = jnp.minimum(t + 1, win).astype(F32)
        p = (acc / cnt - u).astype(BF)
        p_ref[...] = p
        y = jnp.dot(p, w_ref[...], preferred_element_type=F32) * sc_ref[...]
        y_ref[...] = y.astype(y_ref.dtype)

    return pl.pallas_call(
        body, name=name, grid=(B, nt),
        in_specs=[pl.BlockSpec((POOL_T, MAIN_W), lambda b, i: (b * nt + i, 0)),
                  pl.BlockSpec((POOL_HALO, MAIN_W), lambda b, i: (jnp.maximum((b * nt + i) * hb - 1, 0), 0)),
                  pl.BlockSpec((MAIN_W, MAIN_W), lambda b, i: (0, 0)),
                  pl.BlockSpec((1, MAIN_W), lambda b, i: (0, 0))],
        out_specs=[pl.BlockSpec((POOL_T, MAIN_W), lambda b, i: (b * nt + i, 0)),
                   pl.BlockSpec((POOL_T, MAIN_W), lambda b, i: (b * nt + i, 0))],
        out_shape=[jax.ShapeDtypeStruct((T, D_MODEL), BF), jax.ShapeDtypeStruct((T, MAIN_W), BF)],
        scratch_shapes=[pltpu.VMEM((POOL_T + POOL_HALO, MAIN_W), F32)],
        compiler_params=_cp(("parallel", "parallel")),
    )(z, z, wbd, scale.reshape(1, MAIN_W))


def _pool_bwd(dy, p, wbd, scale, dz_alias, B, S, *, name):
    T = dy.shape[0]
    nt = S // POOL_T
    hb = POOL_T // POOL_HALO
    last_halo = T // POOL_HALO - 1
    R = POOL_T + POOL_HALO

    def body(dy_ref, dyn_ref, p_ref, pn_ref, w_ref, sc_ref, _, dz_ref, dw_ref, ds_ref, ext, dw_acc, ds_acc):
        b, i = pl.program_id(0), pl.program_id(1)
        first = jnp.logical_and(b == 0, i == 0)
        dyv = dy_ref[...]
        pv = p_ref[...]
        sc = sc_ref[...]
        w = w_ref[...]
        pw = jnp.dot(pv, w, preferred_element_type=F32)
        ds_part = jnp.sum((dyv * pw).reshape(POOL_T // 8, 8, MAIN_W), axis=0)
        dpw = (dyv * sc).astype(BF)
        dw_part = lax.dot_general(pv, dpw, (((0,), (0,)), ((), ())), preferred_element_type=F32)

        @pl.when(first)
        def _():
            dw_acc[...] = dw_part
            ds_acc[...] = ds_part

        @pl.when(jnp.logical_not(first))
        def _():
            dw_acc[...] += dw_part
            ds_acc[...] += ds_part

        @pl.when(jnp.logical_and(b == pl.num_programs(0) - 1, i == nt - 1))
        def _():
            dw_ref[...] = dw_acc[...]
            ds_ref[...] = jnp.sum(ds_acc[...], axis=0, keepdims=True)

        dp = lax.dot_general(dpw, w, (((1,), (1,)), ((), ())), preferred_element_type=F32)
        dpn = lax.dot_general((dyn_ref[...] * sc).astype(BF), w, (((1,), (1,)), ((), ())), preferred_element_type=F32)
        win = _pool_lane_window((POOL_T, MAIN_W))
        win_n = _pool_lane_window((POOL_HALO, MAIN_W))
        t = i * POOL_T + lax.broadcasted_iota(jnp.int32, (POOL_T, MAIN_W), 0)
        tn = (i + 1) * POOL_T + lax.broadcasted_iota(jnp.int32, (POOL_HALO, MAIN_W), 0)
        ext[pl.ds(0, POOL_T), :] = dp / jnp.minimum(t + 1, win).astype(F32)
        ext[pl.ds(POOL_T, POOL_HALO), :] = jnp.where(i < nt - 1, dpn / jnp.minimum(tn + 1, win_n).astype(F32), 0.0)
        acc = -dp
        for k in range(POOL_HALO):
            acc = acc + jnp.where(k < win, ext[pl.ds(k, POOL_T), :], 0.0)
        dz_ref[...] = acc.astype(dz_ref.dtype)

    cur = lambda b, i: (b * nt + i, 0)
    nxt = lambda b, i: (jnp.minimum((b * nt + i + 1) * hb, last_halo), 0)
    return pl.pallas_call(
        body, name=name, grid=(B, nt),
        in_specs=[pl.BlockSpec((POOL_T, MAIN_W), cur), pl.BlockSpec((POOL_HALO, MAIN_W), nxt),
                  pl.BlockSpec((POOL_T, MAIN_W), cur), pl.BlockSpec((POOL_HALO, MAIN_W), nxt),
                  pl.BlockSpec((MAIN_W, MAIN_W), lambda b, i: (0, 0)),
                  pl.BlockSpec((1, MAIN_W), lambda b, i: (0, 0)),
                  pl.BlockSpec(memory_space=pl.ANY)],
        out_specs=[pl.BlockSpec((POOL_T, MAIN_W), cur),
                   pl.BlockSpec((MAIN_W, MAIN_W), lambda b, i: (0, 0)),
                   pl.BlockSpec((1, MAIN_W), lambda b, i: (0, 0))],
        out_shape=[jax.ShapeDtypeStruct(dz_alias.shape, dz_alias.dtype),
                   jax.ShapeDtypeStruct((MAIN_W, MAIN_W), F32), jax.ShapeDtypeStruct((1, MAIN_W), F32)],
        scratch_shapes=[pltpu.VMEM((R, MAIN_W), F32), pltpu.VMEM((MAIN_W, MAIN_W), F32), pltpu.VMEM((8, MAIN_W), F32)],
        input_output_aliases={6: 0},
        compiler_params=_cp(("arbitrary", "arbitrary")),
    )(dy, dy, p, p, wbd, scale.reshape(1, MAIN_W), dz_alias)


def _head_masks(shape):
    lane = lax.broadcasted_iota(jnp.int32, shape, 1)
    return [(lane // HEAD_DIM) == h for h in range(shape[1] // HEAD_DIM)]


def _row_of(bcast, mask):
    return jnp.max(jnp.where(mask, bcast, -jnp.inf), axis=-1, keepdims=True)


MEM_TQ = 512


def _memattn_fwd(z, kv, y_alias, B, S, *, name):
    T = z.shape[0]
    nt = S // MEM_TQ

    def body(q_ref, k_ref, v_ref, _, y_ref, l_ref):
        q = q_ref[...]
        k = k_ref[...]
        v = v_ref[...]
        masks = _head_masks(q.shape)
        o = jnp.zeros(q.shape, F32)
        lse_b = jnp.zeros(q.shape, F32)
        for m in masks:
            qm = jnp.where(m, q, 0.0).astype(BF)
            s = lax.dot_general(qm, k, (((1,), (1,)), ((), ())), preferred_element_type=F32) * SCALE
            mx = jnp.max(s, axis=-1, keepdims=True)
            e = jnp.exp(s - mx)
            l = jnp.sum(e, axis=-1, keepdims=True)
            p = (e / l).astype(BF)
            o = o + jnp.where(m, jnp.dot(p, v, preferred_element_type=F32), 0.0)
            lse_b = lse_b + jnp.where(m, mx + jnp.log(l), 0.0)
        y_ref[...] = o.astype(y_ref.dtype)
        l_ref[...] = lse_b

    qblk = pl.BlockSpec((MEM_TQ, MEM_W), lambda b, i: (b * nt + i, 3))
    return pl.pallas_call(
        body, name=name, grid=(B, nt),
        in_specs=[qblk, pl.BlockSpec((N_MEM, MEM_W), lambda b, i: (b, 0)), pl.BlockSpec((N_MEM, MEM_W), lambda b, i: (b, 1)),
                  pl.BlockSpec(memory_space=pl.ANY)],
        out_specs=[qblk, pl.BlockSpec((MEM_TQ, MEM_W), lambda b, i: (b * nt + i, 0))],
        out_shape=[jax.ShapeDtypeStruct(y_alias.shape, y_alias.dtype), jax.ShapeDtypeStruct((T, MEM_W), F32)],
        input_output_aliases={3: 0},
        compiler_params=_cp(("parallel", "parallel")),
    )(z, kv, kv, y_alias)


def _memattn_bwd(dy, z, kv, lse, dz_alias, B, S, *, name):
    nt = S // MEM_TQ

    def body(do_ref, q_ref, k_ref, v_ref, l_ref, _, dz_ref, dk_ref, dv_ref, dk_acc, dv_acc):
        i = pl.program_id(1)
        do = do_ref[...]
        q = q_ref[...]
        k = k_ref[...]
        v = v_ref[...]
        lse_b = l_ref[...]
        masks = _head_masks(q.shape)
        dq = jnp.zeros(q.shape, F32)
        dk = jnp.zeros(k.shape, F32)
        dv = jnp.zeros(v.shape, F32)
        for m in masks:
            qm = jnp.where(m, q, 0.0).astype(BF)
            dom = jnp.where(m, do, 0.0).astype(BF)
            s = lax.dot_general(qm, k, (((1,), (1,)), ((), ())), preferred_element_type=F32) * SCALE
            p = jnp.exp(s - _row_of(lse_b, m))
            dp = lax.dot_general(dom, v, (((1,), (1,)), ((), ())), preferred_element_type=F32)
            delta = jnp.sum(p * dp, axis=-1, keepdims=True)
            ds = (p * (dp - delta) * SCALE).astype(BF)
            pb = p.astype(BF)
            dv = dv + jnp.where(m[:N_MEM], lax.dot_general(pb, dom, (((0,), (0,)), ((), ())), preferred_element_type=F32), 0.0)
            dk = dk + jnp.where(m[:N_MEM], lax.dot_general(ds, qm, (((0,), (0,)), ((), ())), preferred_element_type=F32), 0.0)
            dq = dq + jnp.where(m, jnp.dot(ds, k, preferred_element_type=F32), 0.0)
        dz_ref[...] = dq.astype(dz_ref.dtype)

        @pl.when(i == 0)
        def _():
            dk_acc[...] = dk
            dv_acc[...] = dv

        @pl.when(i > 0)
        def _():
            dk_acc[...] += dk
            dv_acc[...] += dv

        @pl.when(i == nt - 1)
        def _():
            dk_ref[...] = dk_acc[...]
            dv_ref[...] = dv_acc[...]

    qblk = pl.BlockSpec((MEM_TQ, MEM_W), lambda b, i: (b * nt + i, 3))
    kblk = pl.BlockSpec((N_MEM, MEM_W), lambda b, i: (b, 0))
    return pl.pallas_call(
        body, name=name, grid=(B, nt),
        in_specs=[qblk, qblk, kblk, pl.BlockSpec((N_MEM, MEM_W), lambda b, i: (b, 1)),
                  pl.BlockSpec((MEM_TQ, MEM_W), lambda b, i: (b * nt + i, 0)), pl.BlockSpec(memory_space=pl.ANY)],
        out_specs=[qblk, kblk, kblk],
        out_shape=[jax.ShapeDtypeStruct(dz_alias.shape, dz_alias.dtype),
                   jax.ShapeDtypeStruct((B * N_MEM, MEM_W), F32), jax.ShapeDtypeStruct((B * N_MEM, MEM_W), F32)],
        scratch_shapes=[pltpu.VMEM((N_MEM, MEM_W), F32), pltpu.VMEM((N_MEM, MEM_W), F32)],
        input_output_aliases={5: 0},
        compiler_params=_cp(("parallel", "arbitrary")),
    )(dy, z, kv, kv, lse, dz_alias)


def _dil_scores(qm, kp, kc, n):
    qi = lax.broadcasted_iota(jnp.int32, (STEPS, STEPS), 0)
    kj = lax.broadcasted_iota(jnp.int32, (STEPS, STEPS), 1)
    sc = lax.dot_general(qm, kc, (((1,), (1,)), ((), ())), preferred_element_type=F32) * SCALE
    sc = jnp.where(kj <= qi, sc, NEG)
    if kp is None:
        return None, sc
    sp = lax.dot_general(qm, kp, (((1,), (1,)), ((), ())), preferred_element_type=F32) * SCALE
    sp = jnp.where(jnp.logical_and(kj >= qi, n > 0), sp, NEG)
    return sp, sc


def _dil_specs(g, d, nb):
    chunk = STEPS * d
    cur = pl.BlockSpec((chunk, 128), lambda b, n, hf: (b * nb + n, g * 2 + hf))
    prev = pl.BlockSpec((chunk, 128), lambda b, n, hf: (b * nb + jnp.maximum(n - 1, 0), g * 2 + hf))
    return cur, prev


def _dil_rows(r, d):
    return pl.ds(r, STEPS, stride=d) if d > 1 else slice(None)


def _dil_loop(d, fn):
    if d <= 4:
        for r in range(d):
            fn(r)
    else:
        lax.fori_loop(0, d, lambda r, carry: (fn(r), carry)[1], 0)


def _dil_fwd_group(g, q, k, v, o_alias, l_alias, B, S, *, name):
    d = DIL[g]
    nb = S // (STEPS * d)
    has_prev = nb > 1

    def body(*refs):
        if has_prev:
            q_ref, kp_ref, kc_ref, vp_ref, vc_ref, _, __, o_ref, l_ref = refs
        else:
            q_ref, kc_ref, vc_ref, _, __, o_ref, l_ref = refs
        n = pl.program_id(1)

        def residue(r):
            rows = _dil_rows(r, d)
            q = q_ref[rows, :]
            kc, vc = kc_ref[rows, :].astype(BF), vc_ref[rows, :].astype(BF)
            kp = kp_ref[rows, :].astype(BF) if has_prev else None
            vp = vp_ref[rows, :].astype(BF) if has_prev else None
            o = jnp.zeros(q.shape, F32)
            lse_b = jnp.zeros(q.shape, F32)
            for m in _head_masks(q.shape):
                qm = jnp.where(m, q, 0.0).astype(BF)
                sp, sc = _dil_scores(qm, kp, kc, n)
                mx = jnp.max(sc, axis=-1, keepdims=True)
                if has_prev:
                    mx = jnp.maximum(mx, jnp.max(sp, axis=-1, keepdims=True))
                l = jnp.sum(jnp.exp(sc - mx), axis=-1, keepdims=True)
                if has_prev:
                    l = l + jnp.sum(jnp.exp(sp - mx), axis=-1, keepdims=True)
                lse = mx + jnp.log(l)
                oh = jnp.dot(jnp.exp(sc - lse).astype(BF), vc, preferred_element_type=F32)
                if has_prev:
                    oh = oh + jnp.dot(jnp.exp(sp - lse).astype(BF), vp, preferred_element_type=F32)
                o = o + jnp.where(m, oh, 0.0)
                lse_b = lse_b + jnp.where(m, lse, 0.0)
            o_ref[rows, :] = o
            l_ref[rows, :] = lse_b

        _dil_loop(d, residue)

    cur, prev = _dil_specs(g, d, nb)
    anyspec = pl.BlockSpec(memory_space=pl.ANY)
    if has_prev:
        in_specs, ops = [cur, prev, cur, prev, cur], [q, k, k, v, v]
    else:
        in_specs, ops = [cur, cur, cur], [q, k, v]
    n_in = len(ops)
    o, l = pl.pallas_call(
        body, name=name, grid=(B, nb, 2),
        in_specs=in_specs + [anyspec, anyspec],
        out_specs=[cur, cur],
        out_shape=[jax.ShapeDtypeStruct(q.shape, F32)] * 2,
        input_output_aliases={n_in: 0, n_in + 1: 1},
        compiler_params=_cp(("parallel", "parallel", "parallel")),
    )(*ops, o_alias, l_alias)
    return o, l


def _dil_bwd_group(g, q, k, v, do, cb, lse, aliases, B, S, *, name):
    d = DIL[g]
    nb = S // (STEPS * d)
    has_prev = nb > 1
    n_out = 5 if has_prev else 3

    def body(*refs):
        if has_prev:
            q_ref, kp_ref, kc_ref, vp_ref, vc_ref, do_ref, c_ref, l_ref = refs[:8]
            dq_ref, dkc_ref, dvc_ref, dkp_ref, dvp_ref = refs[8 + n_out:]
        else:
            q_ref, kc_ref, vc_ref, do_ref, c_ref, l_ref = refs[:6]
            dq_ref, dkc_ref, dvc_ref = refs[6 + n_out:]
        n = pl.program_id(1)
        tdot = lambda a, b: lax.dot_general(a, b, (((0,), (0,)), ((), ())), preferred_element_type=F32)
        ndot = lambda a, b: lax.dot_general(a, b, (((1,), (1,)), ((), ())), preferred_element_type=F32)

        def residue(r):
            rows = _dil_rows(r, d)
            q = q_ref[rows, :]
            kc, vc = kc_ref[rows, :].astype(BF), vc_ref[rows, :].astype(BF)
            kp = kp_ref[rows, :].astype(BF) if has_prev else None
            vp = vp_ref[rows, :].astype(BF) if has_prev else None
            do = do_ref[rows, :]
            cbv = c_ref[rows, :]
            lse_b = l_ref[rows, :]
            z = jnp.zeros(q.shape, F32)
            dq, dkc, dkp, dvc, dvp = z, z, z, z, z
            for m in _head_masks(q.shape):
                qm = jnp.where(m, q, 0.0).astype(BF)
                dom = jnp.where(m, do, 0.0).astype(BF)
                sp, sc = _dil_scores(qm, kp, kc, n)
                lse = _row_of(lse_b, m)
                c = _row_of(cbv, m)
                pc = jnp.exp(sc - lse)
                dsc = (pc * (ndot(dom, vc) + c) * SCALE).astype(BF)
                dqh = jnp.dot(dsc, kc, preferred_element_type=F32)
                dkc = dkc + jnp.where(m, tdot(dsc, qm), 0.0)
                dvc = dvc + jnp.where(m, tdot(pc.astype(BF), dom), 0.0)
                if has_prev:
                    pp = jnp.exp(sp - lse)
                    dsp = (pp * (ndot(dom, vp) + c) * SCALE).astype(BF)
                    dqh = dqh + jnp.dot(dsp, kp, preferred_element_type=F32)
                    dkp = dkp + jnp.where(m, tdot(dsp, qm), 0.0)
                    dvp = dvp + jnp.where(m, tdot(pp.astype(BF), dom), 0.0)
                dq = dq + jnp.where(m, dqh, 0.0)
            dq_ref[rows, :] = dq
            dkc_ref[rows, :] = dkc
            dvc_ref[rows, :] = dvc
            if has_prev:
                dkp_ref[rows, :] = dkp
                dvp_ref[rows, :] = dvp

        _dil_loop(d, residue)

    cur, prev = _dil_specs(g, d, nb)
    anyspec = pl.BlockSpec(memory_space=pl.ANY)
    dq_a, dkc_a, dkp_a, dvc_a, dvp_a = aliases
    if has_prev:
        in_specs, ops = [cur, prev, cur, prev, cur, cur, cur, cur], [q, k, k, v, v, do, cb, lse]
        al = [dq_a, dkc_a, dvc_a, dkp_a, dvp_a]
    else:
        in_specs, ops = [cur, cur, cur, cur, cur, cur], [q, k, v, do, cb, lse]
        al = [dq_a, dkc_a, dvc_a]
    n_in = len(ops)
    outs = pl.pallas_call(
        body, name=name, grid=(B, nb, 2),
        in_specs=in_specs + [anyspec] * n_out,
        out_specs=[cur] * n_out,
        out_shape=[jax.ShapeDtypeStruct(q.shape, F32)] * n_out,
        input_output_aliases={n_in + i: i for i in range(n_out)},
        compiler_params=_cp(("parallel", "parallel", "parallel")),
    )(*ops, *al)
    if has_prev:
        dq_a, dkc_a, dvc_a, dkp_a, dvp_a = outs
    else:
        dq_a, dkc_a, dvc_a = outs
    return dq_a, dkc_a, dkp_a, dvc_a, dvp_a


def _group_softmax(lse):
    l0, l1, l2 = lse[:, 0:256], lse[:, 256:512], lse[:, 512:768]
    mx = jnp.maximum(jnp.maximum(l0, l1), l2)
    e0, e1, e2 = jnp.exp(l0 - mx), jnp.exp(l1 - mx), jnp.exp(l2 - mx)
    tot = e0 + e1 + e2
    return e0 / tot, e1 / tot, e2 / tot


def _dil_combine_fwd(o, lse, y_alias, *, name, tm=512):
    T = o.shape[0]

    def body(o_ref, l_ref, _, y_ref):
        a = jnp.concatenate(_group_softmax(l_ref[...]), axis=1)
        y_ref[...] = (o_ref[...] * a).astype(y_ref.dtype)

    blk = pl.BlockSpec((tm, MAIN_W), lambda i: (i, 0))
    return pl.pallas_call(
        body, name=name, grid=(T // tm,), in_specs=[blk, blk, pl.BlockSpec(memory_space=pl.ANY)], out_specs=blk,
        out_shape=jax.ShapeDtypeStruct(y_alias.shape, y_alias.dtype), input_output_aliases={2: 0},
        compiler_params=_cp(("parallel",)),
    )(o, lse, y_alias)


def _dil_combine_bwd(dy, o, lse, *, name, tm=256):
    T = o.shape[0]
    lane_r = lax.broadcasted_iota(jnp.int32, (256, 256), 0) // HEAD_DIM
    lane_c = lax.broadcasted_iota(jnp.int32, (256, 256), 1) // HEAD_DIM
    ones_bd = (lane_r == lane_c).astype(BF)

    def body(dy_ref, o_ref, l_ref, e_ref, do_ref, c_ref):
        dyv = dy_ref[...]
        alphas = _group_softmax(l_ref[...])
        prod = dyv * o_ref[...]
        e = e_ref[...]
        tot = jnp.zeros((tm, 256), F32)
        for gi in range(3):
            x = prod[:, gi * 256:(gi + 1) * 256]
            hi = x.astype(BF)
            lo = (x - hi.astype(F32)).astype(BF)
            dalpha = jnp.dot(hi, e, preferred_element_type=F32) + jnp.dot(lo, e, preferred_element_type=F32)
            tot = tot + alphas[gi] * dalpha
        a = jnp.concatenate(alphas, axis=1)
        do_ref[...] = (dyv * a).astype(do_ref.dtype)
        c_ref[...] = jnp.concatenate([-al * tot for al in alphas], axis=1)

    blk = pl.BlockSpec((tm, MAIN_W), lambda i: (i, 0))
    return pl.pallas_call(
        body, name=name, grid=(T // tm,),
        in_specs=[blk, blk, blk, pl.BlockSpec((256, 256), lambda i: (0, 0))], out_specs=[blk, blk],
        out_shape=[jax.ShapeDtypeStruct((T, MAIN_W), F32), jax.ShapeDtypeStruct((T, MAIN_W), F32)],
        compiler_params=_cp(("parallel",)),
    )(dy, o, lse, ones_bd)


def _kv_grad(parts, cos, sin, B, S, *, name):
    T = B * S
    tb = S // STEPS
    n_l = len(parts)

    def shifted(g):
        def f(b, t):
            return (b * tb + jnp.minimum(t + DIL[g], tb - 1), g)
        return f

    with_prev = [g for g in range(3) if DIL[g] < tb]
    n_p = len(with_prev)
    per_l = 2 + 2 * n_p

    def body(*refs):
        c_ref, s_ref = refs[0], refs[1]
        ins = refs[2:2 + n_l * per_l]
        dk_ref, dv_ref = refs[2 + n_l * per_l:]
        t = pl.program_id(1)
        dk = jnp.zeros((STEPS, MAIN_W), F32)
        dv = jnp.zeros((STEPS, MAIN_W), F32)
        zero = jnp.zeros((STEPS, 256), F32)
        for li in range(n_l):
            base = li * per_l
            dk = dk + ins[base][...]
            dv = dv + ins[base + 1][...]
            kparts, vparts = [zero] * 3, [zero] * 3
            for pi, g in enumerate(with_prev):
                ok = t + DIL[g] < tb
                kparts[g] = jnp.where(ok, ins[base + 2 + pi][...], 0.0)
                vparts[g] = jnp.where(ok, ins[base + 2 + n_p + pi][...], 0.0)
            dk = dk + jnp.concatenate(kparts, axis=1)
            dv = dv + jnp.concatenate(vparts, axis=1)
        dk_ref[...] = _rot(dk, c_ref[...], s_ref[...], -1.0).astype(dk_ref.dtype)
        dv_ref[...] = dv.astype(dv_ref.dtype)

    full = pl.BlockSpec((STEPS, MAIN_W), lambda b, t: (b * tb + t, 0))
    tab = pl.BlockSpec((STEPS, 128), lambda b, t: (b * tb + t, 0))
    in_specs, ops = [tab, tab], [cos, sin]
    for (kc, kp, vc, vp) in parts:
        in_specs += [full, full] + [pl.BlockSpec((STEPS, 256), shifted(g)) for g in with_prev] * 2
        ops += [kc, vc] + [kp] * n_p + [vp] * n_p
    return pl.pallas_call(
        body, name=name, grid=(B, tb), in_specs=in_specs, out_specs=[full, full],
        out_shape=[jax.ShapeDtypeStruct((T, MAIN_W), BF)] * 2,
        compiler_params=_cp(("parallel", "parallel")),
    )(*ops)


def _loss(y, target, *, name, tm=512):
    T, Dm = y.shape
    nt = T // tm

    def body(y_ref, t_ref, l_ref, d_ref, acc):
        i = pl.program_id(0)
        err = y_ref[...] - t_ref[...]
        d_ref[...] = err / Dm
        part = jnp.sum(jnp.mean(err * err, axis=-1, keepdims=True).reshape(tm // 8, 8, 1), axis=0)

        @pl.when(i == 0)
        def _():
            acc[...] = part

        @pl.when(i > 0)
        def _():
            acc[...] += part

        @pl.when(i == nt - 1)
        def _():
            l_ref[...] = 0.5 * jnp.sum(acc[...], axis=0, keepdims=True)

    row = pl.BlockSpec((tm, Dm), lambda i: (i, 0))
    return pl.pallas_call(
        body, name=name, grid=(nt,), in_specs=[row, row],
        out_specs=[pl.BlockSpec((1, 1), lambda i: (0, 0)), row],
        out_shape=[jax.ShapeDtypeStruct((1, 1), F32), jax.ShapeDtypeStruct((T, Dm), F32)],
        scratch_shapes=[pltpu.VMEM((8, 1), F32)],
        compiler_params=_cp(("arbitrary",)),
    )(y, target)


def _adamw(w, g, m, v, *, name):
    shape = w.shape
    cols = shape[-1]
    rows = w.size // cols
    tm = rows
    for cand in (512, 352, 256, 128):
        if rows > cand and rows % cand == 0 and cand * cols * 4 <= (1 << 20):
            tm = cand
            break

    def body(w_ref, g_ref, m_ref, v_ref, d_ref, mo_ref, vo_ref):
        gv = g_ref[...]
        mn = ADAM_B1 * m_ref[...] + (1.0 - ADAM_B1) * gv
        vn = ADAM_B2 * v_ref[...] + (1.0 - ADAM_B2) * (gv * gv)
        m_hat = mn / (1.0 - ADAM_B1 ** ADAM_STEP)
        v_hat = vn / (1.0 - ADAM_B2 ** ADAM_STEP)
        d_ref[...] = -ADAM_LR * (m_hat / (jnp.sqrt(v_hat) + ADAM_EPS) + ADAM_WD * w_ref[...])
        mo_ref[...] = mn
        vo_ref[...] = vn

    blk = pl.BlockSpec((tm, cols), lambda i: (i, 0))
    outs = pl.pallas_call(
        body, name=name, grid=(rows // tm,), in_specs=[blk] * 4, out_specs=[blk] * 3,
        out_shape=[jax.ShapeDtypeStruct((rows, cols), F32)] * 3,
        compiler_params=_cp(("parallel",)),
    )(*[t.reshape(rows, cols) for t in (w, g, m, v)])
    return tuple(t.reshape(shape) for t in outs)


def _adamw_layer(name, l, w, g, m, v, prev):
    L, rows, cols = w.shape
    tm = rows
    for cand in (512, 352, 256, 176, 128, 64):
        if rows % cand == 0 and cand * cols * 4 <= (1 << 20):
            tm = cand
            break
    if prev is None:
        prev = tuple(lax.empty(w.shape, F32) for _ in range(4))

    def body(w_ref, g_ref, m_ref, v_ref, *rest):
        d_ref, mo_ref, vo_ref, go_ref = rest[4:]
        gv = g_ref[...]
        mn = ADAM_B1 * m_ref[...] + (1.0 - ADAM_B1) * gv
        vn = ADAM_B2 * v_ref[...] + (1.0 - ADAM_B2) * (gv * gv)
        m_hat = mn / (1.0 - ADAM_B1 ** ADAM_STEP)
        v_hat = vn / (1.0 - ADAM_B2 ** ADAM_STEP)
        d_ref[...] = -ADAM_LR * (m_hat / (jnp.sqrt(v_hat) + ADAM_EPS) + ADAM_WD * w_ref[...])
        mo_ref[...] = mn
        vo_ref[...] = vn
        go_ref[...] = gv

    lay = pl.BlockSpec((None, tm, cols), lambda i: (l, i, 0))
    one = pl.BlockSpec((None, tm, cols), lambda i: (0, i, 0))
    return tuple(pl.pallas_call(
        body, name=f"l{l}_adamw_{name}", grid=(rows // tm,),
        in_specs=[lay, one, lay, lay] + [pl.BlockSpec(memory_space=pl.ANY)] * 4, out_specs=[lay] * 4,
        out_shape=[jax.ShapeDtypeStruct(w.shape, F32)] * 4,
        input_output_aliases={4 + i: i for i in range(4)},
        compiler_params=_cp(("parallel",)),
    )(w, g, m, v, *prev))


BIG = {
    'w_in': ((DEPTH, D_MODEL, D_MODEL), 'row'),
    'w_mem_kv': ((DEPTH, D_MODEL, 2 * MEM_W), 'row'),
    'w_out': ((DEPTH, D_MODEL, D_MODEL), 'row'),
    'w_kv': ((1, D_MODEL, 2 * MAIN_W), 'col'),
    'w_gate_up': ((DEPTH, D_MODEL, 2 * D_FF), 'col'),
    'w_down': ((DEPTH, D_FF, D_MODEL), 'row'),
}
BIG_NAMES = tuple(BIG)
N_CHIPS = 4
HBM_ANY = pl.BlockSpec(memory_space=pl.ANY)


def _geom(name):
    (L, R, C), kind = BIG[name]
    if kind == 'row':
        return L, R, C, kind, R // N_CHIPS, C, R // (2 * N_CHIPS)
    return L, R, C, kind, R, C // N_CHIPS, R // 2


def _shard_shape(name):
    L, R, C, kind, rs, cs, rh = _geom(name)
    return (L, rs, cs)


def _half_shape(name):
    L, R, C, kind, rs, cs, rh = _geom(name)
    return (L, rh, cs)


def _full_win(ref, name, s, h):
    L, R, C, kind, rs, cs, rh = _geom(name)
    if kind == 'row':
        rows = pl.ds(s * rs, rs) if h is None else pl.ds(s * rs + h * rh, rh)
        return ref.at[:, rows, :]
    rows = slice(None) if h is None else pl.ds(h * rh, rh)
    return ref.at[:, rows, pl.ds(s * cs, cs)]


def _shard_half(ref, name, h):
    L, R, C, kind, rs, cs, rh = _geom(name)
    return ref.at[:, pl.ds(h * rh, rh), :]


def _halves_win(ref, name, s):
    L, R, C, kind, rs, cs, rh = _geom(name)
    if kind == 'row':
        return ref.at[:, pl.ds(s * rh, rh), :]
    return ref.at[:, :, pl.ds(s * cs, cs)]


def _halves_shape(name):
    L, R, C, kind, rs, cs, rh = _geom(name)
    return (L, N_CHIPS * rh, cs) if kind == 'row' else (L, rh, C)


def _place():
    x, y, c = lax.axis_index("x"), lax.axis_index("y"), lax.axis_index("c")
    chips = [(1 - x, y), (x, 1 - y), (1 - x, 1 - y)]
    return x, y, c, chips


SMALL_ROWS = 24


def _all_gather(shards, small):
    names = BIG_NAMES
    nw = len(names)

    def body(*refs):
        src = dict(zip(names, refs[:nw]))
        small_ref = refs[nw]
        dst = dict(zip(names, refs[nw + 1:2 * nw + 1]))
        small_out = refs[2 * nw + 1]
        send_sems, recv_sems, local_sems = refs[2 * nw + 2:]
        x, y, c, chips = _place()
        s = 2 * x + y
        sib = (x, y, 1 - c)

        def remote(k, src_ref, dst_ref, to):
            return pltpu.make_async_remote_copy(src_ref=src_ref, dst_ref=dst_ref, send_sem=send_sems.at[k],
                                                recv_sem=recv_sems.at[k], device_id=to, device_id_type=MESH)

        local = []
        for wi, nm in enumerate(names):
            local.append(pltpu.make_async_copy(src[nm], _full_win(dst[nm], nm, s, None), local_sems.at[wi]))
        local.append(pltpu.make_async_copy(small_ref, small_out.at[s], local_sems.at[nw]))
        for cp in local:
            cp.start()
        sends = []
        for j, (px, py) in enumerate(chips):
            for wi, nm in enumerate(names):
                sends.append(remote(wi * 6 + j, _shard_half(src[nm], nm, c), _full_win(dst[nm], nm, s, c), (px, py, c)))
            sends.append(remote(nw * 6 + j, small_ref, small_out.at[s], (px, py, c)))
        for cp in sends:
            cp.start()
        for j, (px, py) in enumerate(chips):
            sp = 2 * px + py
            for wi, nm in enumerate(names):
                w = _full_win(dst[nm], nm, sp, c)
                remote(wi * 6 + j, w, w, sib).wait_recv()
                fwd = remote(wi * 6 + 3 + j, w, w, sib)
                fwd.start()
                sends.append(fwd)
            remote(nw * 6 + j, small_ref, small_out.at[sp], sib).wait_recv()
        for j, (px, py) in enumerate(chips):
            sp = 2 * px + py
            for wi, nm in enumerate(names):
                w = _full_win(dst[nm], nm, sp, 1 - c)
                remote(wi * 6 + 3 + j, w, w, sib).wait_recv()
        for cp in sends:
            cp.wait_send()
        for cp in local:
            cp.wait()

    n_sem = nw * 6 + 3
    outs = pl.pallas_call(
        body, name="all_gather_weights",
        in_specs=[HBM_ANY] * (nw + 1), out_specs=[HBM_ANY] * (nw + 1),
        out_shape=[jax.ShapeDtypeStruct(BIG[nm][0], BF) for nm in names]
        + [jax.ShapeDtypeStruct((N_CHIPS, SMALL_ROWS, 256), F32)],
        scratch_shapes=[pltpu.SemaphoreType.DMA((n_sem,)), pltpu.SemaphoreType.DMA((n_sem,)),
                        pltpu.SemaphoreType.DMA((nw + 1,))],
    )(*[shards[nm] for nm in names], small)
    return dict(zip(names, outs[:nw])), outs[nw]


SEM_SPEC = pl.BlockSpec(memory_space=pltpu.SEMAPHORE)
HBM_SPEC = pl.BlockSpec(memory_space=pltpu.HBM)
DATAFLOW = pltpu.SideEffectType.DATAFLOW_SIDE_EFFECTING


def _in_hbm(a):
    return pltpu.with_memory_space_constraint(a, pltpu.HBM)


def _remote(src, dst, send_sems, recv_sems, k, to):
    return pltpu.make_async_remote_copy(src_ref=src, dst_ref=dst, send_sem=send_sems.at[k], recv_sem=recv_sems.at[k],
                                        device_id=to, device_id_type=MESH)


def _split_start(name, bufs, n_copies, sends, after=None):
    nb = len(bufs)
    n_in = nb + (0 if after is None else 1)

    def body(*refs):
        in_refs = refs[:nb]
        send_sems, recv_sems = refs[n_in], refs[n_in + 1]
        token = refs[-1]
        for k, (src, dst, to) in enumerate(sends(in_refs)):
            _remote(src, dst, send_sems, recv_sems, k, to).start()
        token[...] = jnp.zeros_like(token)

    outs = pl.pallas_call(
        body, name=name,
        out_shape=(pltpu.SemaphoreType.DMA((n_copies,)), pltpu.SemaphoreType.DMA((n_copies,)),
                   *[pltpu.HBM(b.shape, b.dtype) for b in bufs], jax.ShapeDtypeStruct((8, 128), F32)),
        in_specs=[HBM_SPEC] * nb + [HBM_ANY] * (n_in - nb),
        out_specs=(SEM_SPEC, SEM_SPEC, *[HBM_SPEC] * nb, pl.BlockSpec(memory_space=pltpu.VMEM)),
        input_output_aliases={i: 2 + i for i in range(nb)},
        compiler_params=pltpu.CompilerParams(has_side_effects=DATAFLOW),
    )(*[_in_hbm(b) for b in bufs], *([] if after is None else [after]))
    return outs[0], outs[1], list(outs[2:2 + nb]), outs[-1]


def _split_wait(name, send_sems, recv_sems, bufs, after, sends, arrivals):
    nb = len(bufs)

    def body(*refs):
        in_refs = refs[:nb]
        s_sems, r_sems = refs[nb], refs[nb + 1]
        me = (lax.axis_index("x"), lax.axis_index("y"), lax.axis_index("c"))
        for k, (src, dst, to) in enumerate(sends(in_refs)):
            _remote(src, dst, s_sems, r_sems, k, to).wait_send()
        for k, win in enumerate(arrivals(in_refs)):
            _remote(win, win, s_sems, r_sems, k, me).wait_recv()

    outs = pl.pallas_call(
        body, name=name,
        out_shape=[pltpu.HBM(b.shape, b.dtype) for b in bufs],
        in_specs=[HBM_SPEC] * nb + [SEM_SPEC, SEM_SPEC, HBM_ANY],
        out_specs=[HBM_SPEC] * nb,
        input_output_aliases={i: i for i in range(nb)},
        compiler_params=pltpu.CompilerParams(has_side_effects=DATAFLOW),
    )(*bufs, send_sems, recv_sems, after)
    return list(outs)


MIX_W = ('w_in', 'w_mem_kv', 'w_out')
FFN_W = ('w_gate_up', 'w_down')
LAYER_W = MIX_W + FFN_W


def _place_own(l, names, shards, small, sc):
    nw = len(names)
    has_small = small is not None
    n_ops = nw + (1 if has_small else 0)

    def body(sc_ref, *refs):
        for src, dst in zip(refs[:n_ops], refs[n_ops:]):
            dst[...] = src[...]

    in_specs, out_specs, out_shape, ops = [], [], [], list(shards)
    for nm in names:
        L, R, C, kind, rs, cs, rh = _geom(nm)
        in_specs.append(pl.BlockSpec((1, rs, cs), lambda i, sc_ref: (0, 0, 0)))
        if kind == 'row':
            out_specs.append(pl.BlockSpec((1, rs, cs), lambda i, sc_ref: (0, sc_ref[0], 0)))
        else:
            out_specs.append(pl.BlockSpec((1, rs, cs), lambda i, sc_ref: (0, 0, sc_ref[0])))
        out_shape.append(jax.ShapeDtypeStruct((1, R, C), BF))
    if has_small:
        in_specs.append(pl.BlockSpec((SMALL_ROWS, 256), lambda i, sc_ref: (0, 0)))
        out_specs.append(pl.BlockSpec((None, SMALL_ROWS, 256), lambda i, sc_ref: (sc_ref[0], 0, 0)))
        out_shape.append(jax.ShapeDtypeStruct((N_CHIPS, SMALL_ROWS, 256), F32))
        ops.append(small)
    return pl.pallas_call(
        body, name=f"{l}_place_own_shard",
        grid_spec=pltpu.PrefetchScalarGridSpec(num_scalar_prefetch=1, grid=(1,), in_specs=in_specs, out_specs=out_specs),
        out_shape=out_shape,
        compiler_params=_cp(("arbitrary",)),
    )(sc, *ops)


def _gather_start(l, names, shards, small, sc, after=None):
    nw = len(names)
    has_small = small is not None
    fulls = _place_own(l, names, shards, small, sc)
    bufs = list(shards) + ([small] if has_small else []) + list(fulls)
    n_src = nw + (1 if has_small else 0)

    def sends(refs):
        x, y, c, chips = _place()
        s = 2 * x + y
        out = []
        for (px, py) in chips:
            for wi, nm in enumerate(names):
                out.append((_shard_half(refs[wi], nm, c), _full_win(refs[n_src + wi], nm, s, c), (px, py, c)))
            if has_small:
                out.append((refs[nw], refs[n_src + nw].at[s], (px, py, c)))
        return out

    def arrivals(refs):
        x, y, c, chips = _place()
        out = []
        for (px, py) in chips:
            sp = 2 * px + py
            for wi, nm in enumerate(names):
                out.append(_full_win(refs[n_src + wi], nm, sp, c))
            if has_small:
                out.append(refs[n_src + nw].at[sp])
        return out

    n_copies = 3 * n_src
    send_sems, recv_sems, bufs, token = _split_start(f"{l}_gather_ici_start", bufs, n_copies, sends, after)
    return dict(l=l, names=names, has_small=has_small, sems=(send_sems, recv_sems), bufs=bufs, sends=sends,
                arrivals=arrivals, token=token)


def _gather_forward(st, after):
    l, names = st['l'], st['names']
    nw = len(names)
    n_src = nw + (1 if st['has_small'] else 0)
    bufs = _split_wait(f"{l}_gather_ici_wait", *st['sems'], st['bufs'], after, st['sends'], st['arrivals'])
    fulls = bufs[n_src:n_src + nw]
    small_all = bufs[n_src + nw] if st['has_small'] else None

    def sends(refs):
        x, y, c, chips = _place()
        out = []
        for (px, py) in chips:
            sp = 2 * px + py
            for wi, nm in enumerate(names):
                w = _full_win(refs[wi], nm, sp, c)
                out.append((w, w, (x, y, 1 - c)))
        return out

    def arrivals(refs):
        x, y, c, chips = _place()
        out = []
        for (px, py) in chips:
            sp = 2 * px + py
            for wi, nm in enumerate(names):
                out.append(_full_win(refs[wi], nm, sp, 1 - c))
        return out

    send_sems, recv_sems, fulls, token = _split_start(f"{l}_gather_d2d_start", fulls, 3 * nw, sends)
    return dict(l=l, names=names, sems=(send_sems, recv_sems), bufs=fulls, sends=sends, arrivals=arrivals,
                small_all=small_all, token=token)


def _gather_finish(st, after):
    fulls = _split_wait(f"{st['l']}_gather_d2d_wait", *st['sems'], st['bufs'], after, st['sends'], st['arrivals'])
    return dict(zip(st['names'], fulls)), st['small_all']


def _reduce_start(tag, names, grads):
    nw = len(names)
    recv = [lax.empty((1,) + _halves_shape(nm)[1:], F32) for nm in names]
    bufs = [grads[nm] for nm in names] + recv

    def windows(refs, half_of):
        x, y, c, _ = _place()
        h = half_of(c)
        out = []
        for wi, nm in enumerate(names):
            L, R, C, kind, rs, cs, rh = _geom(nm)
            if kind == 'row':
                for sp in range(N_CHIPS):
                    out.append((_full_win(refs[wi], nm, sp, h), _halves_win(refs[nw + wi], nm, sp)))
            else:
                out.append((refs[wi].at[:, pl.ds(h * rh, rh), :], refs[nw + wi]))
        return out

    def sends(refs):
        x, y, c, _ = _place()
        return [(src, dst, (x, y, 1 - c)) for src, dst in windows(refs, lambda c: 1 - c)]

    def arrivals(refs):
        return [dst for _, dst in windows(refs, lambda c: c)]

    n_copies = sum(N_CHIPS if BIG[nm][1] == 'row' else 1 for nm in names)
    send_sems, recv_sems, bufs, token = _split_start(tag + "_halves_start", bufs, n_copies, sends)
    return dict(tag=tag, names=names, sems=(send_sems, recv_sems), bufs=bufs, sends=sends, arrivals=arrivals, token=token)


def _reduce_mid(st, after, sc):
    tag, names = st['tag'], st['names']
    nw = len(names)
    bufs = _split_wait(tag + "_halves_wait", *st['sems'], st['bufs'], after, st['sends'], st['arrivals'])
    halves, own = [], []
    for wi, nm in enumerate(names):
        hb, ow = _add_halves(nm, bufs[wi], bufs[nw + wi], sc, tag)
        halves.append(hb)
        own.append(ow)
    pieces = [lax.empty((3, 1) + _half_shape(nm)[1:], BF) for nm in names]

    def sends(refs):
        x, y, c, chips = _place()
        out = []
        for j, (px, py) in enumerate(chips):
            for wi, nm in enumerate(names):
                out.append((_halves_win(refs[wi], nm, 2 * px + py), refs[nw + wi].at[j], (px, py, c)))
        return out

    def arrivals(refs):
        return [refs[nw + wi].at[j] for j in range(3) for wi in range(nw)]

    send_sems, recv_sems, bufs, token = _split_start(tag + "_pieces_start", halves + pieces, 3 * nw, sends)
    return dict(tag=tag, names=names, sems=(send_sems, recv_sems), bufs=bufs, sends=sends, arrivals=arrivals, own=own,
                token=token)


def _reduce_late(st, after, sc):
    tag, names = st['tag'], st['names']
    nw = len(names)
    bufs = _split_wait(tag + "_pieces_wait", *st['sems'], st['bufs'], after, st['sends'], st['arrivals'])
    gsh = [_sum_pieces(nm, st['own'][wi], bufs[nw + wi], sc, tag) for wi, nm in enumerate(names)]

    def sends(refs):
        x, y, c, _ = _place()
        return [(_shard_half(refs[wi], nm, c), _shard_half(refs[wi], nm, c), (x, y, 1 - c)) for wi, nm in enumerate(names)]

    def arrivals(refs):
        x, y, c, _ = _place()
        return [_shard_half(refs[wi], nm, 1 - c) for wi, nm in enumerate(names)]

    send_sems, recv_sems, bufs, token = _split_start(tag + "_share_start", gsh, nw, sends)
    return dict(tag=tag, names=names, sems=(send_sems, recv_sems), bufs=bufs, sends=sends, arrivals=arrivals, token=token)


def _reduce_finish(st, after):
    gsh = _split_wait(st['tag'] + "_share_wait", *st['sems'], st['bufs'], after, st['sends'], st['arrivals'])
    return dict(zip(st['names'], gsh))


def _add_halves(name, g, r, sc, tag):
    _, R, C, kind, rs, cs, rh = _geom(name)
    L = g.shape[0]
    tr = rh if kind == 'row' else 256
    nr = rh // tr

    def body(sc_ref, g_ref, r_ref, hb_ref, own_ref):
        sp = pl.program_id(2)
        tot = g_ref[...] + r_ref[...]
        hb_ref[...] = tot.astype(hb_ref.dtype)

        @pl.when(sp == sc_ref[0])
        def _():
            own_ref[...] = tot

    if kind == 'row':
        g_map = lambda l, ri, sp, sc_ref: (l, sp * 2 + sc_ref[1], 0)
        h_map = lambda l, ri, sp, sc_ref: (l, sp, 0)
    else:
        g_map = lambda l, ri, sp, sc_ref: (l, sc_ref[1] * nr + ri, sp)
        h_map = lambda l, ri, sp, sc_ref: (l, ri, sp)
    own_map = lambda l, ri, sp, sc_ref: (l, ri, 0)
    blk = (None, tr, cs)
    return pl.pallas_call(
        body, name=tag + "_add_halves_" + name,
        grid_spec=pltpu.PrefetchScalarGridSpec(
            num_scalar_prefetch=1, grid=(L, nr, N_CHIPS),
            in_specs=[pl.BlockSpec(blk, g_map), pl.BlockSpec(blk, h_map)],
            out_specs=[pl.BlockSpec(blk, h_map), pl.BlockSpec(blk, own_map)]),
        out_shape=[jax.ShapeDtypeStruct((L,) + _halves_shape(name)[1:], BF),
                   jax.ShapeDtypeStruct((L,) + _half_shape(name)[1:], F32)],
        compiler_params=_cp(("parallel", "parallel", "arbitrary")),
    )(sc, g, r)


def _sum_pieces(name, own, pieces, sc, tag):
    _, R, C, kind, rs, cs, rh = _geom(name)
    L = own.shape[0]
    tr = rh if kind == 'row' else 256
    nr = rh // tr

    def body(sc_ref, o_ref, p_ref, out_ref):
        out_ref[...] = o_ref[...] + p_ref[0].astype(F32) + p_ref[1].astype(F32) + p_ref[2].astype(F32)

    blk = (None, tr, cs)
    return pl.pallas_call(
        body, name=tag + "_sum_pieces_" + name,
        grid_spec=pltpu.PrefetchScalarGridSpec(
            num_scalar_prefetch=1, grid=(L, nr),
            in_specs=[pl.BlockSpec(blk, lambda l, ri, sc_ref: (l, ri, 0)),
                      pl.BlockSpec((3, None, tr, cs), lambda l, ri, sc_ref: (0, l, ri, 0))],
            out_specs=pl.BlockSpec(blk, lambda l, ri, sc_ref: (l, sc_ref[1] * nr + ri, 0))),
        out_shape=jax.ShapeDtypeStruct((L,) + _shard_shape(name)[1:], F32),
        compiler_params=_cp(("parallel", "parallel")),
    )(sc, own, pieces)


def _small_gather_start(v, sc):
    rows = v.shape[0]

    def place(sc_ref, v_ref, o_ref):
        o_ref[...] = v_ref[...]

    slots = pl.pallas_call(
        place, name="small_grads_place_own",
        grid_spec=pltpu.PrefetchScalarGridSpec(
            num_scalar_prefetch=1, grid=(1,),
            in_specs=[pl.BlockSpec((rows, 128), lambda i, sc_ref: (0, 0))],
            out_specs=pl.BlockSpec((None, rows, 128), lambda i, sc_ref: (2 * sc_ref[0] + sc_ref[1], 0, 0))),
        out_shape=jax.ShapeDtypeStruct((8, rows, 128), F32),
        compiler_params=_cp(("arbitrary",)),
    )(sc, v)

    def peers():
        x, y, c, _ = _place()
        flips = [(fx, fy, fc) for fx in (0, 1) for fy in (0, 1) for fc in (0, 1)][1:]
        return [((1 - x if fx else x), (1 - y if fy else y), (1 - c if fc else c)) for fx, fy, fc in flips]

    def sends(refs):
        x, y, c, _ = _place()
        return [(refs[0], refs[1].at[4 * x + 2 * y + c], p) for p in peers()]

    def arrivals(refs):
        return [refs[1].at[4 * px + 2 * py + pc] for px, py, pc in peers()]

    send_sems, recv_sems, bufs, token = _split_start("small_grads_gather_start", [v, slots], 7, sends)
    return dict(sems=(send_sems, recv_sems), bufs=bufs, sends=sends, arrivals=arrivals, token=token)


def _small_gather_finish(st, after):
    return _split_wait("small_grads_gather_wait", *st['sems'], st['bufs'], after, st['sends'], st['arrivals'])[1]


def _sum8(v8, *, name, tr=336):
    rows = v8.shape[1]
    tr = min(tr, rows)
    assert rows % tr == 0

    def body(v_ref, o_ref):
        tot = v_ref[0]
        for d in range(1, 8):
            tot = tot + v_ref[d]
        o_ref[...] = tot

    return pl.pallas_call(
        body, name=name, grid=(rows // tr,),
        in_specs=[pl.BlockSpec((8, tr, 128), lambda i: (0, i, 0))], out_specs=pl.BlockSpec((tr, 128), lambda i: (i, 0)),
        out_shape=jax.ShapeDtypeStruct((rows, 128), F32),
        compiler_params=_cp(("parallel",)),
    )(v8)


def _block_diag(w_pool_l):
    wbd = jnp.zeros((MAIN_W, MAIN_W), F32)
    for gi in range(len(POOL_WINDOWS)):
        wbd = lax.dynamic_update_slice(wbd, w_pool_l[gi], (gi * POOL_GROUP, gi * POOL_GROUP))
    return wbd.astype(BF)


def _unpack_small(small_all):
    ng = small_all[:, :16, :].reshape(N_CHIPS, DEPTH, 4, 256).transpose(1, 2, 0, 3).reshape(DEPTH, 4, D_MODEL)
    ps = small_all[:, 16:18, :POOL_GROUP].transpose(1, 0, 2).reshape(N_A, MAIN_W)
    return ng, ps


def _local_step(x, mem, positions, on_forward, on_backward, mem_norm, w_pool, kv_norm, target):
    B, S, _ = x.shape
    T = B * S
    xc = x.reshape(T, D_MODEL)
    memf = mem.reshape(B * N_MEM, D_MODEL)
    tgt = target.reshape(T, D_MODEL)
    cos, sin = _rope_tables(positions.reshape(T, 1), name="rope_tables")
    wbd = [_block_diag(w_pool[l]) for l in range(N_A)]
    nbo = D_FF // 256
    fw = []
    rk = rv = None
    kv_saved = None
    wts = []
    norm_gains = pool_scale = y2 = None

    def tied(vec, tok):
        return vec if tok is None else vec + tok

    for l in range(DEPTH):
        t = f"l{l}_"
        got = on_forward('start', l, y2)
        wts.append(dict(got[0]))
        if l == 0:
            norm_gains, pool_scale = _unpack_small(got[1])
        sv = {'x_in': xc}
        h0, sv['r0'] = _norm_fwd(xc, tied(norm_gains[l, 0], got[2]), name=t + "norm0", out_dtype=BF)
        z, = _mm(h0, wts[l]['w_in'], b_layer=0, name=t + "mm_in")
        memn, sv['rm'] = _norm_fwd(memf, mem_norm[l], name=t + "norm_mem", out_dtype=BF, tm=256)
        kvm, = _mm(memn, wts[l]['w_mem_kv'], b_layer=0, name=t + "mm_memkv", out_dtypes=(BF,))
        if l < N_A:
            ycat, sv['p'] = _pool_fwd(z, wbd[l], pool_scale[l], B, S, name=t + "pool_fwd")
        else:
            rq = _rope_apply(z, cos, sin, name=t + "rope_q", out_dtype=F32)
            o = lax.empty((T, MAIN_W), F32)
            lse = lax.empty((T, MAIN_W), F32)
            for g in range(3):
                o, lse = _dil_fwd_group(g, rq, rk, rv, o, lse, B, S, name=t + f"dil_fwd{g}")
            ycat = _dil_combine_fwd(o, lse, lax.empty((T, D_MODEL), BF), name=t + "dil_combine")
            sv.update(rq=rq, o=o, lse=lse)
        ycat, sv['lse_m'] = _memattn_fwd(z, kvm, ycat, B, S, name=t + "memattn_fwd")
        tok = on_forward('mid', l, ycat)
        y1, = _mm(ycat, wts[l]['w_out'], b_layer=0, name=t + "mm_out")
        wts[l].update(on_forward('ffn', l, y1)[0])
        x1, sv['r1'] = _norm_fwd(y1, tied(norm_gains[l, 1], tok), name=t + "norm1", res=xc)
        h2, sv['r2'] = _norm_fwd(x1, norm_gains[l, 2], name=t + "norm2", out_dtype=BF)
        gg, uu, aa = _mm(h2, wts[l]['w_gate_up'], b_layer=0, b_offsets=(0, nbo), out_n=D_FF, tn=256, name=t + "mm_gate_up",
                         epilogue=_swiglu_fwd_epilogue, out_dtypes=(BF, BF, BF))
        on_forward('post', l, gg)
        y2, = _mm(aa, wts[l]['w_down'], b_layer=0, tk=D_FF, name=t + "mm_down")
        x2, sv['r3'] = _norm_fwd(y2, norm_gains[l, 3], name=t + "norm3", res=x1)
        sv.update(h0=h0, z=z, memn=memn, kvm=kvm, ycat=ycat, y1=y1, x1=x1, h2=h2, gg=gg, uu=uu, aa=aa, y2=y2)
        fw.append(sv)
        xc = x2
        if l == N_A - 1:
            kvn, rkv = _norm_fwd(xc, kv_norm, name="norm_kv", out_dtype=BF)
            kv, = _mm(kvn, wts[N_A - 1]['w_kv'], b_layer=0, name="mm_kv")
            rk, rv = _rope_apply(kv, cos, sin, name="rope_k", passthrough=True, out_dtype=F32)
            kv_saved = (xc, kvn, rkv)

    loss, dx = _loss(xc, tgt, name="loss")

    d_ng = [[None] * 4 for _ in range(DEPTH)]
    d_memnorm = [None] * DEPTH
    d_wbd = [None] * N_A
    d_pscale = [None] * N_A
    d_kvnorm = None
    kv_parts = []
    tok = None

    def as3d(gl):
        return {nm: g.reshape((1,) + g.shape) for nm, g in gl.items()}

    for l in reversed(range(DEPTH)):
        t = f"l{l}_b_"
        sv = fw[l]
        gl = {}
        dy2, d_ng[l][3] = _norm_bwd(dx, sv['y2'], sv['r3'], tied(norm_gains[l, 3], tok), name=t + "norm3", out_dtype=BF)
        gl['w_down'], = _mm(sv['aa'], dy2, ta=True, tm=1408, tk=2048, name=t + "dw_down")
        dg, du = _mm(dy2, wts[l]['w_down'], tb=True, b_layer=0, tn=256, name=t + "d_act",
                     extras=((sv['gg'], 'tile'), (sv['uu'], 'tile')), epilogue=_swiglu_bwd_epilogue, out_dtypes=(BF, BF))
        gl['w_gate_up'], = _mm(sv['h2'], (dg, du), ta=True, tn=1408, tk=1024, name=t + "dw_gate_up")
        dh2, = _mm((dg, du), wts[l]['w_gate_up'], tb=True, b_layer=0, tn=1024, tk=1408, name=t + "d_h2", out_dtypes=(BF,))
        dx1, d_ng[l][2] = _norm_bwd(dh2, sv['x1'], sv['r2'], norm_gains[l, 2], name=t + "norm2", add=dx)
        tok = on_backward('ffn', l, dx1, as3d(gl))
        dy1, d_ng[l][1] = _norm_bwd(dx1, sv['y1'], sv['r1'], tied(norm_gains[l, 1], tok), name=t + "norm1", out_dtype=BF)
        gl['w_out'], = _mm(sv['ycat'], dy1, ta=True, tk=4096, name=t + "dw_out")
        dycat, = _mm(dy1, wts[l]['w_out'], tb=True, b_layer=0, name=t + "d_ycat")
        dz = lax.empty((T, D_MODEL), BF)
        dz, dkm, dvm = _memattn_bwd(dycat, sv['z'], sv['kvm'], sv['lse_m'], dz, B, S, name=t + "memattn")
        if l < N_A:
            dz, d_wbd[l], d_pscale[l] = _pool_bwd(dycat, sv['p'], wbd[l], pool_scale[l], dz, B, S, name=t + "pool")
        else:
            do, cb = _dil_combine_bwd(dycat, sv['o'], sv['lse'], name=t + "dil_combine")
            acc = tuple(lax.empty((T, MAIN_W), F32) for _ in range(5))
            for g in range(3):
                acc = _dil_bwd_group(g, sv['rq'], rk, rv, do, cb, sv['lse'], acc, B, S, name=t + f"dil{g}")
            dz = _rope_apply(acc[0], cos, sin, name=t + "rope_q", sign=-1.0, alias=dz)
            kv_parts.append(acc[1:])
        tok = on_backward('mix', l, dz, as3d(gl))
        gl['w_in'], = _mm(sv['h0'], dz, ta=True, tk=4096, name=t + "dw_in")
        dh0, = _mm(dz, wts[l]['w_in'], tb=True, b_layer=0, name=t + "d_h0", out_dtypes=(BF,))
        dx, d_ng[l][0] = _norm_bwd(dh0, sv['x_in'], sv['r0'], tied(norm_gains[l, 0], tok), name=t + "norm0", add=dx1)
        gl['w_mem_kv'], = _mm(sv['memn'], (dkm, dvm), ta=True, tn=256, name=t + "dw_memkv")
        dmemn, = _mm((dkm, dvm), wts[l]['w_mem_kv'], tb=True, b_layer=0, tk=256, name=t + "d_memn", out_dtypes=(BF,))
        _, d_memnorm[l] = _norm_bwd(dmemn, memf, sv['rm'], mem_norm[l], name=t + "norm_mem", out_dtype=BF, tm=256)
        if l == N_A:
            dk, dv = _kv_grad(kv_parts, cos, sin, B, S, name="kv_grad")
            x_kv, kvn, rkv = kv_saved
            gl['w_kv'], = _mm(kvn, (dk, dv), ta=True, tn=768, tk=2048, name="dw_kv")
            dkvn, = _mm((dk, dv), wts[N_A - 1]['w_kv'], tb=True, b_layer=0, tn=1024, tk=768, name="d_kvn", out_dtypes=(BF,))
            dx, d_kvnorm = _norm_bwd(dkvn, x_kv, rkv, kv_norm, name="norm_kv_b", add=dx)
        tok = on_backward('end', l, dx, as3d(gl))

    small = {
        'norm_gains': jnp.stack([jnp.concatenate(d_ng[l], axis=0) for l in range(DEPTH)]),
        'mem_norm': jnp.concatenate(d_memnorm, axis=0),
        'kv_norm': d_kvnorm.reshape(D_MODEL),
        'pool_scale': jnp.concatenate(d_pscale, axis=0),
        'w_pool': jnp.stack([jnp.stack([d_wbd[l][gi * POOL_GROUP:(gi + 1) * POOL_GROUP, gi * POOL_GROUP:(gi + 1) * POOL_GROUP]
                                        for gi in range(len(POOL_WINDOWS))]) for l in range(N_A)]),
    }
    return loss, dx, small


SMALL_ORDER = ('norm_gains', 'mem_norm', 'kv_norm', 'pool_scale', 'w_pool')
SMALL_VEC_ROWS = 2560


def kernel(x, mem, positions, norm_gains, mem_norm, w_in, w_mem_kv, w_out, w_pool, pool_scale, kv_norm, w_kv, w_gate_up, w_down, loss_target, m_norm_gains, m_mem_norm, m_w_in, m_w_mem_kv, m_w_out, m_w_pool, m_pool_scale, m_kv_norm, m_w_kv, m_w_gate_up, m_w_down, v_norm_gains, v_mem_norm, v_w_in, v_w_mem_kv, v_w_out, v_w_pool, v_pool_scale, v_kv_norm, v_w_kv, v_w_gate_up, v_w_down):
    xi, yi, ci = lax.axis_index("x"), lax.axis_index("y"), lax.axis_index("c")
    s = 2 * xi + yi
    sc = jnp.stack([s, ci]).astype(jnp.int32)
    weights = dict(norm_gains=norm_gains, mem_norm=mem_norm, w_in=w_in, w_mem_kv=w_mem_kv, w_out=w_out, w_pool=w_pool,
                   pool_scale=pool_scale, kv_norm=kv_norm, w_kv=w_kv, w_gate_up=w_gate_up, w_down=w_down)
    moms = dict(norm_gains=m_norm_gains, mem_norm=m_mem_norm, w_in=m_w_in, w_mem_kv=m_w_mem_kv, w_out=m_w_out,
                w_pool=m_w_pool, pool_scale=m_pool_scale, kv_norm=m_kv_norm, w_kv=m_w_kv, w_gate_up=m_w_gate_up,
                w_down=m_w_down)
    vels = dict(norm_gains=v_norm_gains, mem_norm=v_mem_norm, w_in=v_w_in, w_mem_kv=v_w_mem_kv, w_out=v_w_out,
                w_pool=v_w_pool, pool_scale=v_pool_scale, kv_norm=v_kv_norm, w_kv=v_w_kv, w_gate_up=v_w_gate_up,
                w_down=v_w_down)

    small_w = jnp.zeros((SMALL_ROWS, 256), F32)
    small_w = lax.dynamic_update_slice(small_w, norm_gains.reshape(16, 256), (0, 0))
    small_w = lax.dynamic_update_slice(small_w, pool_scale, (16, 0))
    def shard_of(nm, l):
        return w_kv.astype(BF).reshape(_shard_shape('w_kv')) if nm == 'w_kv' else weights[nm][l:l + 1].astype(BF)

    groups = {'l0a': (0, MIX_W), 'l0b': (0, FFN_W)}
    groups.update({f"l{l}": (l, LAYER_W + (('w_kv',) if l == N_A - 1 else ())) for l in range(1, DEPTH)})
    on_ici, on_d2d, gathered = {}, {}, {}

    def start_group(tag, after):
        l, names = groups[tag]
        on_ici[tag] = _gather_start(tag, names, [shard_of(nm, l) for nm in names], small_w if tag == 'l0a' else None, sc,
                                    after)
        return on_ici[tag]['token'][0, 0]

    def on_forward(where, l, after):
        if where == 'start':
            if l == 0:
                start_group('l0a', None)
                st = on_ici.pop('l0a')
                fwd = _gather_forward(st, st['token'])
                w, small_all = _gather_finish(fwd, fwd['token'])
                return w, small_all, start_group('l0b', w['w_in'])
            gathered[l] = _gather_finish(on_d2d.pop(f"l{l}"), after)[0]
            tok = start_group(f"l{l + 1}", gathered[l]['w_in']) if l + 1 < DEPTH else None
            return {nm: w for nm, w in gathered[l].items() if nm not in FFN_W}, None, tok
        if where == 'mid' and l == 0:
            on_d2d['l0b'] = _gather_forward(on_ici.pop('l0b'), after)
            return start_group('l1', on_d2d['l0b']['token'])
        if where == 'ffn':
            if l == 0:
                return (_gather_finish(on_d2d.pop('l0b'), after)[0],)
            return ({nm: gathered[l][nm] for nm in FFN_W},)
        if where == 'post' and l + 1 < DEPTH:
            on_d2d[f"l{l + 1}"] = _gather_forward(on_ici.pop(f"l{l + 1}"), after)
        return None

    hook_of = {'ffn': 0, 'mix': 1, 'end': 2}
    active, reduced = [], {l: {} for l in range(DEPTH)}
    advance = {'mid': lambda st, after: _reduce_mid(st, after, sc), 'late': lambda st, after: _reduce_late(st, after, sc)}

    def run_hook(idx, after):
        toks = []
        for grp in list(active):
            while grp['plan'] and grp['plan'][0][1] <= idx:
                step = grp['plan'].pop(0)[0]
                if step == 'finish':
                    reduced[grp['layer']].update(_reduce_finish(grp['st'], after))
                    active.remove(grp)
                else:
                    grp['st'] = advance[step](grp['st'], after)
                    toks.append(grp['st']['token'][0, 0])
        return toks

    def on_backward(where, l, after, grads):
        idx = 3 * (DEPTH - 1 - l) + hook_of[where]
        toks = run_hook(idx, after)
        if where in ('ffn', 'end'):
            names = FFN_W if where == 'ffn' else tuple(nm for nm in grads if nm not in FFN_W)
            st = _reduce_start(f"l{l}_{where}_grads", names, {nm: grads[nm] for nm in names})
            plan = [('mid', idx + 1), ('late', idx + 3), ('finish', idx + 4)] if where == 'ffn' else \
                   [('mid', idx + 1), ('late', idx + 2), ('finish', idx + 3)]
            active.append(dict(layer=l, st=st, plan=plan))
            toks.append(st['token'][0, 0])
        return sum(toks) if toks else None

    loss, gx, gsmall = _local_step(x, mem, positions, on_forward, on_backward, mem_norm, w_pool, kv_norm, loss_target)
    loss = lax.psum(loss[0, 0], ("x", "y", "c"))

    vec = jnp.concatenate([gsmall[nm].reshape(-1) for nm in SMALL_ORDER])
    vec = jnp.pad(vec, (0, SMALL_VEC_ROWS * 128 - vec.shape[0])).reshape(SMALL_VEC_ROWS, 128)
    small_st = _small_gather_start(vec, sc)
    outs = {nm: None for nm in LAYER_W}

    def adamw_layers(layers):
        for l in layers:
            for nm in LAYER_W:
                outs[nm] = _adamw_layer(nm, l, weights[nm], reduced[l][nm], moms[nm], vels[nm], outs[nm])

    last = 3 * DEPTH
    run_hook(last, small_st['token'])
    adamw_layers(range(DEPTH - 1, 0, -1))
    run_hook(last + 1, outs[LAYER_W[-1]][0])
    tot = _sum8(_small_gather_finish(small_st, outs[LAYER_W[0]][0]), name="sum_small_grads", tr=512)
    run_hook(last + 2, tot)
    assert not active
    adamw_layers([0])
    tot = tot.reshape(-1)
    grads, off = {}, 0
    for nm in SMALL_ORDER:
        shape = (DEPTH, 4, D_MODEL) if nm == 'norm_gains' else (N_A, MAIN_W) if nm == 'pool_scale' else weights[nm].shape
        n = 1
        for dim in shape:
            n *= dim
        grads[nm] = tot[off:off + n].reshape(shape)
        off += n
    grads['norm_gains'] = lax.dynamic_slice(grads['norm_gains'], (0, 0, s * 256), (DEPTH, 4, 256))
    grads['pool_scale'] = lax.dynamic_slice(grads['pool_scale'], (0, s * POOL_GROUP), (N_A, POOL_GROUP))
    grads['w_kv'] = reduced[N_A]['w_kv'].reshape(w_kv.shape)

    order = ('norm_gains', 'mem_norm', 'w_in', 'w_mem_kv', 'w_out', 'w_pool', 'pool_scale', 'kv_norm', 'w_kv',
             'w_gate_up', 'w_down')
    deltas, new_m, new_v = {}, {}, {}
    for nm in order:
        if nm in LAYER_W:
            deltas[nm], new_m[nm], new_v[nm], grads[nm] = outs[nm]
        else:
            deltas[nm], new_m[nm], new_v[nm] = _adamw(weights[nm], grads[nm], moms[nm], vels[nm], name="adamw_" + nm)
    return (loss, gx.reshape(x.shape), *[grads[nm] for nm in order], *[deltas[nm] for nm in order],
            *[new_m[nm] for nm in order], *[new_v[nm] for nm in order])
```

```python
import functools

import jax
import jax.numpy as jnp
from jax import lax
from jax.experimental import pallas as pl
from jax.experimental.pallas import tpu as pltpu

F32 = jnp.float32
BF = jnp.bfloat16

D_MODEL = 1024
DEPTH = 4
N_A = 2
HEAD_DIM = 64
MEM_W = 256
MAIN_W = 768
D_FF = 2816
N_MEM = 256
POOL_WINDOWS = (2, 4, 8, 16)
POOL_GROUP = 192
DIL = (1, 4, 16)
STEPS = 128
ROPE_THETA = 10000.0
EPS = 1e-6
SCALE = HEAD_DIM ** -0.5
NEG = -1e30

ADAM_LR = 0.001
ADAM_B1 = 0.9
ADAM_B2 = 0.999
ADAM_EPS = 1e-08
ADAM_WD = 0.01
ADAM_STEP = 10

VMEM_LIMIT = 48 * 1024 * 1024
MESH = pl.DeviceIdType.MESH


def _cp(sem):
    return pltpu.CompilerParams(dimension_semantics=sem, vmem_limit_bytes=VMEM_LIMIT)


def _mm(a, b, *, name, ta=False, tb=False, tm=1024, tn=512, tk=1024, b_layer=None, b_offsets=(0,),
        extras=(), epilogue=None, out_dtypes=(F32,), out_n=None, stack=None):
    a_pair = isinstance(a, (tuple, list))
    b_pair = isinstance(b, (tuple, list))
    a0 = a[0] if a_pair else a
    b0 = b[0] if b_pair else b
    a_rows, a_cols = a0.shape
    if a_pair:
        a_cols *= 2
    b_rows, b_cols = b0.shape[-2:]
    if b_pair:
        b_cols *= 2
    M, K = (a_cols, a_rows) if ta else (a_rows, a_cols)
    N = b_rows if tb else b_cols
    if out_n is not None:
        N = out_n
    tm, tn, tk = min(tm, M), min(tn, N), min(tk, K)
    assert M % tm == 0 and N % tn == 0 and K % tk == 0, (name, M, N, K, tm, tn, tk)
    nk = K // tk
    n_acc = len(b_offsets)

    if a_pair:
        a_half = (a0.shape[1] // (tm if ta else tk))
    if b_pair:
        b_half = (b0.shape[1] // (tk if tb else tn))

    def a_map(sel):
        def f(i, j, k):
            r, c = (k, i) if ta else (i, k)
            if a_pair:
                c = jnp.clip(c - sel * a_half, 0, a_half - 1)
            return (r, c)
        return f

    def b_map(sel, off):
        def f(i, j, k):
            r, c = (j + off, k) if tb else (k, j + off)
            if b_pair:
                c = jnp.clip(c - sel * b_half, 0, b_half - 1)
            if b_layer is not None:
                return (b_layer, r, c)
            return (r, c)
        return f

    a_blk = (tk, tm) if ta else (tm, tk)
    b_blk = (tn, tk) if tb else (tk, tn)
    if b_layer is not None:
        b_blk = (None,) + b_blk
    in_specs, operands = [], []
    for sel in range(2 if a_pair else 1):
        in_specs.append(pl.BlockSpec(a_blk, a_map(sel)))
        operands.append(a[sel] if a_pair else a)
    n_a = len(operands)
    for off in b_offsets:
        for sel in range(2 if b_pair else 1):
            in_specs.append(pl.BlockSpec(b_blk, b_map(sel, off)))
            operands.append(b[sel] if b_pair else b)
    n_b = len(operands) - n_a
    for arr, kind in extras:
        if kind == 'tile':
            in_specs.append(pl.BlockSpec((tm, tn), lambda i, j, k: (i, j)))
        elif kind == 'row':
            in_specs.append(pl.BlockSpec((tm, 1), lambda i, j, k: (i, 0)))
        else:
            in_specs.append(pl.BlockSpec((1, tn), lambda i, j, k: (0, j)))
        operands.append(arr)
    n_e = len(extras)
    n_o = len(out_dtypes)
    dims = (((0,) if ta else (1,), (1,) if tb else (0,)), ((), ()))

    def body(*refs):
        a_refs = refs[:n_a]
        b_refs = refs[n_a:n_a + n_b]
        e_refs = refs[n_a + n_b:n_a + n_b + n_e]
        n_in = n_a + n_b + n_e + (1 if stack is not None else 0)
        o_refs = refs[n_in:n_in + n_o]
        acc_refs = refs[n_in + n_o:]
        i, j, k = pl.program_id(0), pl.program_id(1), pl.program_id(2)
        if a_pair:
            cidx = i if ta else k
            av = jnp.where(cidx < a_half, a_refs[0][...], a_refs[1][...])
        else:
            av = a_refs[0][...]
        av = av.astype(BF)
        prods = []
        for q in range(n_acc):
            if b_pair:
                cidx = (k if tb else j) + b_offsets[q]
                bv = jnp.where(cidx < b_half, b_refs[2 * q][...], b_refs[2 * q + 1][...])
            else:
                bv = b_refs[q][...]
            prods.append(lax.dot_general(av, bv.astype(BF), dims, preferred_element_type=F32))

        def finish(accs):
            outs = epilogue(accs, *[r[...] for r in e_refs]) if epilogue is not None else accs
            for o_ref, o in zip(o_refs, outs):
                o_ref[...] = o.astype(o_ref.dtype)

        if nk == 1:
            finish(prods)
        else:
            @pl.when(k == 0)
            def _():
                for r, p in zip(acc_refs, prods):
                    r[...] = p

            @pl.when(k > 0)
            def _():
                for r, p in zip(acc_refs, prods):
                    r[...] += p

            @pl.when(k == nk - 1)
            def _():
                finish([r[...] for r in acc_refs])

    if stack is not None:
        buf, layer = stack
        assert n_o == 1 and buf.shape[1:] == (M, N)
        return pl.pallas_call(
            body, name=name,
            grid=(M // tm, N // tn, nk),
            in_specs=in_specs + [pl.BlockSpec(memory_space=pl.ANY)],
            out_specs=[pl.BlockSpec((None, tm, tn), lambda i, j, k: (layer, i, j))],
            out_shape=[jax.ShapeDtypeStruct(buf.shape, buf.dtype)],
            scratch_shapes=[pltpu.VMEM((tm, tn), F32) for _ in range(n_acc if nk > 1 else 0)],
            input_output_aliases={len(operands): 0},
            compiler_params=_cp(("parallel", "parallel", "arbitrary")),
        )(*operands, buf)[0]
    return pl.pallas_call(
        body, name=name,
        grid=(M // tm, N // tn, nk),
        in_specs=in_specs,
        out_specs=[pl.BlockSpec((tm, tn), lambda i, j, k: (i, j)) for _ in range(n_o)],
        out_shape=[jax.ShapeDtypeStruct((M, N), dt) for dt in out_dtypes],
        scratch_shapes=[pltpu.VMEM((tm, tn), F32) for _ in range(n_acc if nk > 1 else 0)],
        compiler_params=_cp(("parallel", "parallel", "arbitrary")),
    )(*operands)


def _norm_fwd(x, g, *, name, res=None, out_dtype=F32, tm=512):
    T, Dm = x.shape
    has_res = res is not None

    def body(*refs):
        if has_res:
            x_ref, g_ref, r_ref, y_ref, s_ref = refs
        else:
            x_ref, g_ref, y_ref, s_ref = refs
        xv = x_ref[...]
        rstd = lax.rsqrt(jnp.mean(xv * xv, axis=-1, keepdims=True) + EPS)
        y = xv * rstd * g_ref[...]
        if has_res:
            y = r_ref[...] + y
        y_ref[...] = y.astype(y_ref.dtype)
        s_ref[...] = rstd

    row = pl.BlockSpec((tm, Dm), lambda i: (i, 0))
    in_specs = [row, pl.BlockSpec((1, Dm), lambda i: (0, 0))] + ([row] if has_res else [])
    ops = [x, _in_hbm(g.reshape(1, Dm))] + ([res] if has_res else [])
    return pl.pallas_call(
        body, name=name, grid=(T // tm,), in_specs=in_specs,
        out_specs=[row, pl.BlockSpec((tm, 1), lambda i: (i, 0))],
        out_shape=[jax.ShapeDtypeStruct((T, Dm), out_dtype), jax.ShapeDtypeStruct((T, 1), F32)],
        compiler_params=_cp(("parallel",)),
    )(*ops)


def _norm_bwd(dout, x, rstd, g, *, name, add=None, out_dtype=F32, tm=512):
    T, Dm = x.shape
    has_add = add is not None
    nt = T // tm

    def body(*refs):
        if has_add:
            do_ref, x_ref, s_ref, g_ref, a_ref, dx_ref, dg_ref, acc = refs
        else:
            do_ref, x_ref, s_ref, g_ref, dx_ref, dg_ref, acc = refs
        i = pl.program_id(0)
        do = do_ref[...].astype(F32)
        xh = x_ref[...] * s_ref[...]
        gd = do * g_ref[...]
        dx = s_ref[...] * (gd - xh * jnp.mean(gd * xh, axis=-1, keepdims=True))
        if has_add:
            dx = dx + a_ref[...].astype(F32)
        dx_ref[...] = dx.astype(dx_ref.dtype)
        part = jnp.sum((do * xh).reshape(tm // 8, 8, Dm), axis=0)

        @pl.when(i == 0)
        def _():
            acc[...] = part

        @pl.when(i > 0)
        def _():
            acc[...] += part

        @pl.when(i == nt - 1)
        def _():
            dg_ref[...] = jnp.sum(acc[...], axis=0, keepdims=True)

    row = pl.BlockSpec((tm, Dm), lambda i: (i, 0))
    in_specs = [row, row, pl.BlockSpec((tm, 1), lambda i: (i, 0)), pl.BlockSpec((1, Dm), lambda i: (0, 0))]
    ops = [dout, x, _in_hbm(rstd), _in_hbm(g.reshape(1, Dm))]
    if has_add:
        in_specs.append(row)
        ops.append(add)
    return pl.pallas_call(
        body, name=name, grid=(nt,), in_specs=in_specs,
        out_specs=[row, pl.BlockSpec((1, Dm), lambda i: (0, 0))],
        out_shape=[jax.ShapeDtypeStruct((T, Dm), out_dtype), jax.ShapeDtypeStruct((1, Dm), F32)],
        scratch_shapes=[pltpu.VMEM((8, Dm), F32)],
        compiler_params=_cp(("arbitrary",)),
    )(*ops)


def _swiglu_fwd_epilogue(accs):
    g, u = accs
    return g, u, g * jax.nn.sigmoid(g) * u


def _swiglu_bwd_epilogue(accs, g, u):
    da = accs[0]
    g = g.astype(F32)
    u = u.astype(F32)
    sig = jax.nn.sigmoid(g)
    return da * u * (sig * (1.0 + g * (1.0 - sig))), da * (g * sig)


def _rope_tables(pos, *, name, tm=1024):
    T = pos.shape[0]
    half = HEAD_DIM // 2
    freqs = ROPE_THETA ** (-jnp.arange(half, dtype=F32) / half)
    freqs = jnp.tile(freqs, 4).reshape(1, 128)

    def body(p_ref, f_ref, c_ref, s_ref):
        ang = p_ref[...].astype(F32) * f_ref[...]
        lane = lax.broadcasted_iota(jnp.int32, ang.shape, 1)
        c_ref[...] = jnp.cos(ang)
        s_ref[...] = jnp.where(lane % HEAD_DIM < half, -1.0, 1.0) * jnp.sin(ang)

    tab = pl.BlockSpec((tm, 128), lambda i: (i, 0))
    return pl.pallas_call(
        body, name=name, grid=(T // tm,),
        in_specs=[pl.BlockSpec((tm, 1), lambda i: (i, 0)), pl.BlockSpec((1, 128), lambda i: (0, 0))],
        out_specs=[tab, tab],
        out_shape=[jax.ShapeDtypeStruct((T, 128), F32)] * 2,
        compiler_params=_cp(("parallel",)),
    )(pos, freqs)


def _rot(x, cos, sin, sign):
    W = x.shape[1]
    half = HEAD_DIM // 2
    reps = W // 128
    c = jnp.concatenate([cos] * reps, axis=1) if reps > 1 else cos
    s = jnp.concatenate([sin] * reps, axis=1) if reps > 1 else sin
    lane = lax.broadcasted_iota(jnp.int32, x.shape, 1)
    swapped = jnp.where(lane % HEAD_DIM < half, pltpu.roll(x, W - half, axis=1), pltpu.roll(x, half, axis=1))
    return x * c + (sign * s) * swapped


def _rope_apply(x, cos, sin, *, name, sign=1.0, width=MAIN_W, passthrough=False, out_dtype=BF, alias=None,
                out_cols=None, tm=512):
    T = x.shape[0]

    def body(*refs):
        if passthrough:
            x_ref, v_ref, c_ref, s_ref, o_ref, ov_ref = refs
            ov_ref[...] = v_ref[...].astype(ov_ref.dtype)
        elif alias is not None:
            x_ref, c_ref, s_ref, _, o_ref = refs
        else:
            x_ref, c_ref, s_ref, o_ref = refs
        o_ref[...] = _rot(x_ref[...].astype(F32), c_ref[...], s_ref[...], sign).astype(o_ref.dtype)

    blk0 = pl.BlockSpec((tm, width), lambda i: (i, 0))
    blk1 = pl.BlockSpec((tm, width), lambda i: (i, 1))
    tab = pl.BlockSpec((tm, 128), lambda i: (i, 0))
    if passthrough:
        return pl.pallas_call(
            body, name=name, grid=(T // tm,), in_specs=[blk0, blk1, tab, tab], out_specs=[blk0, blk0],
            out_shape=[jax.ShapeDtypeStruct((T, width), out_dtype)] * 2,
            compiler_params=_cp(("parallel",)),
        )(x, x, cos, sin)
    if alias is not None:
        return pl.pallas_call(
            body, name=name, grid=(T // tm,),
            in_specs=[blk0, tab, tab, pl.BlockSpec(memory_space=pl.ANY)], out_specs=blk0,
            out_shape=jax.ShapeDtypeStruct(alias.shape, alias.dtype),
            input_output_aliases={3: 0},
            compiler_params=_cp(("parallel",)),
        )(x, cos, sin, alias)
    return pl.pallas_call(
        body, name=name, grid=(T // tm,), in_specs=[blk0, tab, tab], out_specs=blk0,
        out_shape=jax.ShapeDtypeStruct((T, width), out_dtype),
        compiler_params=_cp(("parallel",)),
    )(x, cos, sin)


POOL_T = 256
POOL_HALO = 16


def _pool_lane_window(shape):
    lane = lax.broadcasted_iota(jnp.int32, shape, 1)
    w = jnp.full(shape, POOL_WINDOWS[0], jnp.int32)
    for gi in range(1, len(POOL_WINDOWS)):
        w = jnp.where(lane >= gi * POOL_GROUP, POOL_WINDOWS[gi], w)
    return w


def _pool_fwd(z, wbd, scale, B, S, *, name):
    T = z.shape[0]
    nt = S // POOL_T
    hb = POOL_T // POOL_HALO

    def body(z_ref, h_ref, w_ref, sc_ref, y_ref, p_ref, ext):
        i = pl.program_id(1)
        u = z_ref[...]
        ext[pl.ds(POOL_HALO, POOL_T), :] = u
        ext[pl.ds(0, POOL_HALO), :] = jnp.where(i > 0, h_ref[...], 0.0)
        win = _pool_lane_window((POOL_T, MAIN_W))
        acc = u
        for k in range(1, POOL_HALO):
            acc = acc + jnp.where(k < win, ext[pl.ds(POOL_HALO - k, POOL_T), :], 0.0)
        t = i * POOL_T + lax.broadcasted_iota(jnp.int32, (POOL_T, MAIN_W), 0)
        cnt = jnp.minimum(t + 1, win).astype(F32)
        p = (acc / cnt - u).astype(BF)
        p_ref[...] = p
        y = jnp.dot(p, w_ref[...], preferred_element_type=F32) * sc_ref[...]
        y_ref[...] = y.astype(y_ref.dtype)

    return pl.pallas_call(
        body, name=name, grid=(B, nt),
        in_specs=[pl.BlockSpec((POOL_T, MAIN_W), lambda b, i: (b * nt + i, 0)),
                  pl.BlockSpec((POOL_HALO, MAIN_W), lambda b, i: (jnp.maximum((b * nt + i) * hb - 1, 0), 0)),
                  pl.BlockSpec((MAIN_W, MAIN_W), lambda b, i: (0, 0)),
                  pl.BlockSpec((1, MAIN_W), lambda b, i: (0, 0))],
        out_specs=[pl.BlockSpec((POOL_T, MAIN_W), lambda b, i: (b * nt + i, 0)),
                   pl.BlockSpec((POOL_T, MAIN_W), lambda b, i: (b * nt + i, 0))],
        out_shape=[jax.ShapeDtypeStruct((T, D_MODEL), BF), jax.ShapeDtypeStruct((T, MAIN_W), BF)],
        scratch_shapes=[pltpu.VMEM((POOL_T + POOL_HALO, MAIN_W), F32)],
        compiler_params=_cp(("parallel", "parallel")),
    )(z, z, wbd, scale.reshape(1, MAIN_W))


def _pool_bwd(dy, p, wbd, scale, dz_alias, B, S, *, name):
    T = dy.shape[0]
    nt = S // POOL_T
    hb = POOL_T // POOL_HALO
    last_halo = T // POOL_HALO - 1
    R = POOL_T + POOL_HALO

    def body(dy_ref, dyn_ref, p_ref, pn_ref, w_ref, sc_ref, _, dz_ref, dw_ref, ds_ref, ext, dw_acc, ds_acc):
        b, i = pl.program_id(0), pl.program_id(1)
        first = jnp.logical_and(b == 0, i == 0)
        dyv = dy_ref[...]
        pv = p_ref[...]
        sc = sc_ref[...]
        w = w_ref[...]
        pw = jnp.dot(pv, w, preferred_element_type=F32)
        ds_part = jnp.sum((dyv * pw).reshape(POOL_T // 8, 8, MAIN_W), axis=0)
        dpw = (dyv * sc).astype(BF)
        dw_part = lax.dot_general(pv, dpw, (((0,), (0,)), ((), ())), preferred_element_type=F32)

        @pl.when(first)
        def _():
            dw_acc[...] = dw_part
            ds_acc[...] = ds_part

        @pl.when(jnp.logical_not(first))
        def _():
            dw_acc[...] += dw_part
            ds_acc[...] += ds_part

        @pl.when(jnp.logical_and(b == pl.num_programs(0) - 1, i == nt - 1))
        def _():
            dw_ref[...] = dw_acc[...]
            ds_ref[...] = jnp.sum(ds_acc[...], axis=0, keepdims=True)

        dp = lax.dot_general(dpw, w, (((1,), (1,)), ((), ())), preferred_element_type=F32)
        dpn = lax.dot_general((dyn_ref[...] * sc).astype(BF), w, (((1,), (1,)), ((), ())), preferred_element_type=F32)
        win = _pool_lane_window((POOL_T, MAIN_W))
        win_n = _pool_lane_window((POOL_HALO, MAIN_W))
        t = i * POOL_T + lax.broadcasted_iota(jnp.int32, (POOL_T, MAIN_W), 0)
        tn = (i + 1) * POOL_T + lax.broadcasted_iota(jnp.int32, (POOL_HALO, MAIN_W), 0)
        ext[pl.ds(0, POOL_T), :] = dp / jnp.minimum(t + 1, win).astype(F32)
        ext[pl.ds(POOL_T, POOL_HALO), :] = jnp.where(i < nt - 1, dpn / jnp.minimum(tn + 1, win_n).astype(F32), 0.0)
        acc = -dp
        for k in range(POOL_HALO):
            acc = acc + jnp.where(k < win, ext[pl.ds(k, POOL_T), :], 0.0)
        dz_ref[...] = acc.astype(dz_ref.dtype)

    cur = lambda b, i: (b * nt + i, 0)
    nxt = lambda b, i: (jnp.minimum((b * nt + i + 1) * hb, last_halo), 0)
    return pl.pallas_call(
        body, name=name, grid=(B, nt),
        in_specs=[pl.BlockSpec((POOL_T, MAIN_W), cur), pl.BlockSpec((POOL_HALO, MAIN_W), nxt),
                  pl.BlockSpec((POOL_T, MAIN_W), cur), pl.BlockSpec((POOL_HALO, MAIN_W), nxt),
                  pl.BlockSpec((MAIN_W, MAIN_W), lambda b, i: (0, 0)),
                  pl.BlockSpec((1, MAIN_W), lambda b, i: (0, 0)),
                  pl.BlockSpec(memory_space=pl.ANY)],
        out_specs=[pl.BlockSpec((POOL_T, MAIN_W), cur),
                   pl.BlockSpec((MAIN_W, MAIN_W), lambda b, i: (0, 0)),
                   pl.BlockSpec((1, MAIN_W), lambda b, i: (0, 0))],
        out_shape=[jax.ShapeDtypeStruct(dz_alias.shape, dz_alias.dtype),
                   jax.ShapeDtypeStruct((MAIN_W, MAIN_W), F32), jax.ShapeDtypeStruct((1, MAIN_W), F32)],
        scratch_shapes=[pltpu.VMEM((R, MAIN_W), F32), pltpu.VMEM((MAIN_W, MAIN_W), F32), pltpu.VMEM((8, MAIN_W), F32)],
        input_output_aliases={6: 0},
        compiler_params=_cp(("arbitrary", "arbitrary")),
    )(dy, dy, p, p, wbd, scale.reshape(1, MAIN_W), dz_alias)


def _head_masks(shape):
    lane = lax.broadcasted_iota(jnp.int32, shape, 1)
    return [(lane // HEAD_DIM) == h for h in range(shape[1] // HEAD_DIM)]


def _row_of(bcast, mask):
    return jnp.max(jnp.where(mask, bcast, -jnp.inf), axis=-1, keepdims=True)


MEM_TQ = 512


def _memattn_fwd(z, kv, y_alias, B, S, *, name):
    T = z.shape[0]
    nt = S // MEM_TQ

    def body(q_ref, k_ref, v_ref, _, y_ref, l_ref):
        q = q_ref[...]
        k = k_ref[...]
        v = v_ref[...]
        masks = _head_masks(q.shape)
        o = jnp.zeros(q.shape, F32)
        lse_b = jnp.zeros(q.shape, F32)
        for m in masks:
            qm = jnp.where(m, q, 0.0).astype(BF)
            s = lax.dot_general(qm, k, (((1,), (1,)), ((), ())), preferred_element_type=F32) * SCALE
            mx = jnp.max(s, axis=-1, keepdims=True)
            e = jnp.exp(s - mx)
            l = jnp.sum(e, axis=-1, keepdims=True)
            p = (e / l).astype(BF)
            o = o + jnp.where(m, jnp.dot(p, v, preferred_element_type=F32), 0.0)
            lse_b = lse_b + jnp.where(m, mx + jnp.log(l), 0.0)
        y_ref[...] = o.astype(y_ref.dtype)
        l_ref[...] = lse_b

    qblk = pl.BlockSpec((MEM_TQ, MEM_W), lambda b, i: (b * nt + i, 3))
    return pl.pallas_call(
        body, name=name, grid=(B, nt),
        in_specs=[qblk, pl.BlockSpec((N_MEM, MEM_W), lambda b, i: (b, 0)), pl.BlockSpec((N_MEM, MEM_W), lambda b, i: (b, 1)),
                  pl.BlockSpec(memory_space=pl.ANY)],
        out_specs=[qblk, pl.BlockSpec((MEM_TQ, MEM_W), lambda b, i: (b * nt + i, 0))],
        out_shape=[jax.ShapeDtypeStruct(y_alias.shape, y_alias.dtype), jax.ShapeDtypeStruct((T, MEM_W), F32)],
        input_output_aliases={3: 0},
        compiler_params=_cp(("parallel", "parallel")),
    )(z, kv, kv, y_alias)


def _memattn_bwd(dy, z, kv, lse, dz_alias, B, S, *, name):
    nt = S // MEM_TQ

    def body(do_ref, q_ref, k_ref, v_ref, l_ref, _, dz_ref, dk_ref, dv_ref, dk_acc, dv_acc):
        i = pl.program_id(1)
        do = do_ref[...]
        q = q_ref[...]
        k = k_ref[...]
        v = v_ref[...]
        lse_b = l_ref[...]
        masks = _head_masks(q.shape)
        dq = jnp.zeros(q.shape, F32)
        dk = jnp.zeros(k.shape, F32)
        dv = jnp.zeros(v.shape, F32)
        for m in masks:
            qm = jnp.where(m, q, 0.0).astype(BF)
            dom = jnp.where(m, do, 0.0).astype(BF)
            s = lax.dot_general(qm, k, (((1,), (1,)), ((), ())), preferred_element_type=F32) * SCALE
            p = jnp.exp(s - _row_of(lse_b, m))
            dp = lax.dot_general(dom, v, (((1,), (1,)), ((), ())), preferred_element_type=F32)
            delta = jnp.sum(p * dp, axis=-1, keepdims=True)
            ds = (p * (dp - delta) * SCALE).astype(BF)
            pb = p.astype(BF)
            dv = dv + jnp.where(m[:N_MEM], lax.dot_general(pb, dom, (((0,), (0,)), ((), ())), preferred_element_type=F32), 0.0)
            dk = dk + jnp.where(m[:N_MEM], lax.dot_general(ds, qm, (((0,), (0,)), ((), ())), preferred_element_type=F32), 0.0)
            dq = dq + jnp.where(m, jnp.dot(ds, k, preferred_element_type=F32), 0.0)
        dz_ref[...] = dq.astype(dz_ref.dtype)

        @pl.when(i == 0)
        def _():
            dk_acc[...] = dk
            dv_acc[...] = dv

        @pl.when(i > 0)
        def _():
            dk_acc[...] += dk
            dv_acc[...] += dv

        @pl.when(i == nt - 1)
        def _():
            dk_ref[...] = dk_acc[...]
            dv_ref[...] = dv_acc[...]

    qblk = pl.BlockSpec((MEM_TQ, MEM_W), lambda b, i: (b * nt + i, 3))
    kblk = pl.BlockSpec((N_MEM, MEM_W), lambda b, i: (b, 0))
    return pl.pallas_call(
        body, name=name, grid=(B, nt),
        in_specs=[qblk, qblk, kblk, pl.BlockSpec((N_MEM, MEM_W), lambda b, i: (b, 1)),
                  pl.BlockSpec((MEM_TQ, MEM_W), lambda b, i: (b * nt + i, 0)), pl.BlockSpec(memory_space=pl.ANY)],
        out_specs=[qblk, kblk, kblk],
        out_shape=[jax.ShapeDtypeStruct(dz_alias.shape, dz_alias.dtype),
                   jax.ShapeDtypeStruct((B * N_MEM, MEM_W), F32), jax.ShapeDtypeStruct((B * N_MEM, MEM_W), F32)],
        scratch_shapes=[pltpu.VMEM((N_MEM, MEM_W), F32), pltpu.VMEM((N_MEM, MEM_W), F32)],
        input_output_aliases={5: 0},
        compiler_params=_cp(("parallel", "arbitrary")),
    )(dy, z, kv, kv, lse, dz_alias)


def _dil_scores(qm, kp, kc, n):
    qi = lax.broadcasted_iota(jnp.int32, (STEPS, STEPS), 0)
    kj = lax.broadcasted_iota(jnp.int32, (STEPS, STEPS), 1)
    sc = lax.dot_general(qm, kc, (((1,), (1,)), ((), ())), preferred_element_type=F32) * SCALE
    sc = jnp.where(kj <= qi, sc, NEG)
    if kp is None:
        return None, sc
    sp = lax.dot_general(qm, kp, (((1,), (1,)), ((), ())), preferred_element_type=F32) * SCALE
    sp = jnp.where(jnp.logical_and(kj >= qi, n > 0), sp, NEG)
    return sp, sc


def _dil_specs(g, d, nb):
    chunk = STEPS * d
    cur = pl.BlockSpec((chunk, 128), lambda b, n, hf: (b * nb + n, g * 2 + hf))
    prev = pl.BlockSpec((chunk, 128), lambda b, n, hf: (b * nb + jnp.maximum(n - 1, 0), g * 2 + hf))
    return cur, prev


def _dil_rows(r, d):
    return pl.ds(r, STEPS, stride=d) if d > 1 else slice(None)


def _dil_loop(d, fn):
    if d <= 4:
        for r in range(d):
            fn(r)
    else:
        lax.fori_loop(0, d, lambda r, carry: (fn(r), carry)[1], 0)


def _dil_fwd_group(g, q, k, v, o_alias, l_alias, B, S, *, name):
    d = DIL[g]
    nb = S // (STEPS * d)
    has_prev = nb > 1

    def body(*refs):
        if has_prev:
            q_ref, kp_ref, kc_ref, vp_ref, vc_ref, _, __, o_ref, l_ref = refs
        else:
            q_ref, kc_ref, vc_ref, _, __, o_ref, l_ref = refs
        n = pl.program_id(1)

        def residue(r):
            rows = _dil_rows(r, d)
            q = q_ref[rows, :]
            kc, vc = kc_ref[rows, :].astype(BF), vc_ref[rows, :].astype(BF)
            kp = kp_ref[rows, :].astype(BF) if has_prev else None
            vp = vp_ref[rows, :].astype(BF) if has_prev else None
            o = jnp.zeros(q.shape, F32)
            lse_b = jnp.zeros(q.shape, F32)
            for m in _head_masks(q.shape):
                qm = jnp.where(m, q, 0.0).astype(BF)
                sp, sc = _dil_scores(qm, kp, kc, n)
                mx = jnp.max(sc, axis=-1, keepdims=True)
                if has_prev:
                    mx = jnp.maximum(mx, jnp.max(sp, axis=-1, keepdims=True))
                l = jnp.sum(jnp.exp(sc - mx), axis=-1, keepdims=True)
                if has_prev:
                    l = l + jnp.sum(jnp.exp(sp - mx), axis=-1, keepdims=True)
                lse = mx + jnp.log(l)
                oh = jnp.dot(jnp.exp(sc - lse).astype(BF), vc, preferred_element_type=F32)
                if has_prev:
                    oh = oh + jnp.dot(jnp.exp(sp - lse).astype(BF), vp, preferred_element_type=F32)
                o = o + jnp.where(m, oh, 0.0)
                lse_b = lse_b + jnp.where(m, lse, 0.0)
            o_ref[rows, :] = o
            l_ref[rows, :] = lse_b

        _dil_loop(d, residue)

    cur, prev = _dil_specs(g, d, nb)
    anyspec = pl.BlockSpec(memory_space=pl.ANY)
    if has_prev:
        in_specs, ops = [cur, prev, cur, prev, cur], [q, k, k, v, v]
    else:
        in_specs, ops = [cur, cur, cur], [q, k, v]
    n_in = len(ops)
    o, l = pl.pallas_call(
        body, name=name, grid=(B, nb, 2),
        in_specs=in_specs + [anyspec, anyspec],
        out_specs=[cur, cur],
        out_shape=[jax.ShapeDtypeStruct(q.shape, F32)] * 2,
        input_output_aliases={n_in: 0, n_in + 1: 1},
        compiler_params=_cp(("parallel", "parallel", "parallel")),
    )(*ops, o_alias, l_alias)
    return o, l


def _dil_bwd_group(g, q, k, v, do, cb, lse, aliases, B, S, *, name):
    d = DIL[g]
    nb = S // (STEPS * d)
    has_prev = nb > 1
    n_out = 5 if has_prev else 3

    def body(*refs):
        if has_prev:
            q_ref, kp_ref, kc_ref, vp_ref, vc_ref, do_ref, c_ref, l_ref = refs[:8]
            dq_ref, dkc_ref, dvc_ref, dkp_ref, dvp_ref = refs[8 + n_out:]
        else:
            q_ref, kc_ref, vc_ref, do_ref, c_ref, l_ref = refs[:6]
            dq_ref, dkc_ref, dvc_ref = refs[6 + n_out:]
        n = pl.program_id(1)
        tdot = lambda a, b: lax.dot_general(a, b, (((0,), (0,)), ((), ())), preferred_element_type=F32)
        ndot = lambda a, b: lax.dot_general(a, b, (((1,), (1,)), ((), ())), preferred_element_type=F32)

        def residue(r):
            rows = _dil_rows(r, d)
            q = q_ref[rows, :]
            kc, vc = kc_ref[rows, :].astype(BF), vc_ref[rows, :].astype(BF)
            kp = kp_ref[rows, :].astype(BF) if has_prev else None
            vp = vp_ref[rows, :].astype(BF) if has_prev else None
            do = do_ref[rows, :]
            cbv = c_ref[rows, :]
            lse_b = l_ref[rows, :]
            z = jnp.zeros(q.shape, F32)
            dq, dkc, dkp, dvc, dvp = z, z, z, z, z
            for m in _head_masks(q.shape):
                qm = jnp.where(m, q, 0.0).astype(BF)
                dom = jnp.where(m, do, 0.0).astype(BF)
                sp, sc = _dil_scores(qm, kp, kc, n)
                lse = _row_of(lse_b, m)
                c = _row_of(cbv, m)
                pc = jnp.exp(sc - lse)
                dsc = (pc * (ndot(dom, vc) + c) * SCALE).astype(BF)
                dqh = jnp.dot(dsc, kc, preferred_element_type=F32)
                dkc = dkc + jnp.where(m, tdot(dsc, qm), 0.0)
                dvc = dvc + jnp.where(m, tdot(pc.astype(BF), dom), 0.0)
                if has_prev:
                    pp = jnp.exp(sp - lse)
                    dsp = (pp * (ndot(dom, vp) + c) * SCALE).astype(BF)
                    dqh = dqh + jnp.dot(dsp, kp, preferred_element_type=F32)
                    dkp = dkp + jnp.where(m, tdot(dsp, qm), 0.0)
                    dvp = dvp + jnp.where(m, tdot(pp.astype(BF), dom), 0.0)
                dq = dq + jnp.where(m, dqh, 0.0)
            dq_ref[rows, :] = dq
            dkc_ref[rows, :] = dkc
            dvc_ref[rows, :] = dvc
            if has_prev:
                dkp_ref[rows, :] = dkp
                dvp_ref[rows, :] = dvp

        _dil_loop(d, residue)

    cur, prev = _dil_specs(g, d, nb)
    anyspec = pl.BlockSpec(memory_space=pl.ANY)
    dq_a, dkc_a, dkp_a, dvc_a, dvp_a = aliases
    if has_prev:
        in_specs, ops = [cur, prev, cur, prev, cur, cur, cur, cur], [q, k, k, v, v, do, cb, lse]
        al = [dq_a, dkc_a, dvc_a, dkp_a, dvp_a]
    else:
        in_specs, ops = [cur, cur, cur, cur, cur, cur], [q, k, v, do, cb, lse]
        al = [dq_a, dkc_a, dvc_a]
    n_in = len(ops)
    outs = pl.pallas_call(
        body, name=name, grid=(B, nb, 2),
        in_specs=in_specs + [anyspec] * n_out,
        out_specs=[cur] * n_out,
        out_shape=[jax.ShapeDtypeStruct(q.shape, F32)] * n_out,
        input_output_aliases={n_in + i: i for i in range(n_out)},
        compiler_params=_cp(("parallel", "parallel", "parallel")),
    )(*ops, *al)
    if has_prev:
        dq_a, dkc_a, dvc_a, dkp_a, dvp_a = outs
    else:
        dq_a, dkc_a, dvc_a = outs
    return dq_a, dkc_a, dkp_a, dvc_a, dvp_a


N_UNITS = 16


def _unit_rows(g):
    d = DIL[g]
    nb = N_UNITS // d
    return [pl.ds(n * STEPS * d + r, STEPS, stride=d) if d > 1 else pl.ds(n * STEPS, STEPS)
            for n in range(nb) for r in range(d)]


def _load_units(ref, g):
    if DIL[g] == 1:
        return ref[...].reshape(N_UNITS, STEPS, 128)
    return jnp.stack([ref[rows, :] for rows in _unit_rows(g)])


def _store_units(ref, val, g):
    if DIL[g] == 1:
        ref[...] = val.reshape(N_UNITS * STEPS, 128)
    else:
        for u, rows in enumerate(_unit_rows(g)):
            ref[rows, :] = val[u]


def _shift_units(x, by):
    z = jnp.zeros((abs(by),) + x.shape[1:], x.dtype)
    return jnp.concatenate([z, x[:N_UNITS - by]], axis=0) if by > 0 else jnp.concatenate([x[-by:], z], axis=0)


def _bdot(a, b, ca, cb):
    return lax.dot_general(a, b, (((ca,), (cb,)), ((0,), (0,))), preferred_element_type=F32)


def _dil_masks(g):
    d = DIL[g]
    has_prev = N_UNITS // d > 1
    qi = lax.broadcasted_iota(jnp.int32, (1, STEPS, STEPS), 1)
    kj = lax.broadcasted_iota(jnp.int32, (1, STEPS, STEPS), 2)
    unit = lax.broadcasted_iota(jnp.int32, (N_UNITS, 1, 1), 0)
    cur = kj <= qi
    prev = jnp.logical_and(kj >= qi, unit >= d) if has_prev else None
    lane = lax.broadcasted_iota(jnp.int32, (1, 1, 128), 2)
    heads = [(lane // HEAD_DIM) == h for h in range(128 // HEAD_DIM)]
    return has_prev, cur, prev, heads


def _dil_fwd(g, q, k, v, o_alias, l_alias, B, S, *, name):
    assert S == N_UNITS * STEPS
    d = DIL[g]

    def body(q_ref, k_ref, v_ref, _, __, o_ref, l_ref):
        has_prev, cur, prev, heads = _dil_masks(g)
        q = _load_units(q_ref, g)
        kc = _load_units(k_ref, g).astype(BF)
        vc = _load_units(v_ref, g).astype(BF)
        if has_prev:
            kp, vp = _shift_units(kc, d), _shift_units(vc, d)
        o = jnp.zeros(q.shape, F32)
        lse_b = jnp.zeros(q.shape, F32)
        for m in heads:
            qm = jnp.where(m, q, 0.0).astype(BF)
            sc = jnp.where(cur, _bdot(qm, kc, 2, 2) * SCALE, NEG)
            mx = jnp.max(sc, axis=-1, keepdims=True)
            if has_prev:
                sp = jnp.where(prev, _bdot(qm, kp, 2, 2) * SCALE, NEG)
                mx = jnp.maximum(mx, jnp.max(sp, axis=-1, keepdims=True))
            l = jnp.sum(jnp.exp(sc - mx), axis=-1, keepdims=True)
            if has_prev:
                l = l + jnp.sum(jnp.exp(sp - mx), axis=-1, keepdims=True)
            lse = mx + jnp.log(l)
            oh = _bdot(jnp.exp(sc - lse).astype(BF), vc, 2, 1)
            if has_prev:
                oh = oh + _bdot(jnp.exp(sp - lse).astype(BF), vp, 2, 1)
            o = o + jnp.where(m, oh, 0.0)
            lse_b = lse_b + jnp.where(m, lse, 0.0)
        _store_units(o_ref, o, g)
        _store_units(l_ref, lse_b, g)

    blk = pl.BlockSpec((S, 128), lambda b, hf: (b, g * 2 + hf))
    anyspec = pl.BlockSpec(memory_space=pl.ANY)
    o, l = pl.pallas_call(
        body, name=name, grid=(B, 2),
        in_specs=[blk, blk, blk, anyspec, anyspec], out_specs=[blk, blk],
        out_shape=[jax.ShapeDtypeStruct(q.shape, F32)] * 2,
        input_output_aliases={3: 0, 4: 1},
        compiler_params=_cp(("parallel", "parallel")),
    )(q, k, v, o_alias, l_alias)
    return o, l


def _dil_bwd(g, q, k, v, do, cb, lse, aliases, B, S, *, name):
    assert S == N_UNITS * STEPS
    d = DIL[g]

    def body(q_ref, k_ref, v_ref, do_ref, c_ref, l_ref, _, __, ___, dq_ref, dk_ref, dv_ref):
        has_prev, cur, prev, heads = _dil_masks(g)
        q = _load_units(q_ref, g)
        kc = _load_units(k_ref, g).astype(BF)
        vc = _load_units(v_ref, g).astype(BF)
        do = _load_units(do_ref, g)
        cbv = _load_units(c_ref, g)
        lse_b = _load_units(l_ref, g)
        if has_prev:
            kp, vp = _shift_units(kc, d), _shift_units(vc, d)
        z = jnp.zeros(q.shape, F32)
        dq, dkc, dkp, dvc, dvp = z, z, z, z, z
        for m in heads:
            qm = jnp.where(m, q, 0.0).astype(BF)
            dom = jnp.where(m, do, 0.0).astype(BF)
            lse = jnp.max(jnp.where(m, lse_b, -jnp.inf), axis=-1, keepdims=True)
            c = jnp.max(jnp.where(m, cbv, -jnp.inf), axis=-1, keepdims=True)
            sc = jnp.where(cur, _bdot(qm, kc, 2, 2) * SCALE, NEG)
            pc = jnp.exp(sc - lse)
            dsc = (pc * (_bdot(dom, vc, 2, 2) + c) * SCALE).astype(BF)
            dqh = _bdot(dsc, kc, 2, 1)
            dkc = dkc + jnp.where(m, _bdot(dsc, qm, 1, 1), 0.0)
            dvc = dvc + jnp.where(m, _bdot(pc.astype(BF), dom, 1, 1), 0.0)
            if has_prev:
                sp = jnp.where(prev, _bdot(qm, kp, 2, 2) * SCALE, NEG)
                pp = jnp.exp(sp - lse)
                dsp = (pp * (_bdot(dom, vp, 2, 2) + c) * SCALE).astype(BF)
                dqh = dqh + _bdot(dsp, kp, 2, 1)
                dkp = dkp + jnp.where(m, _bdot(dsp, qm, 1, 1), 0.0)
                dvp = dvp + jnp.where(m, _bdot(pp.astype(BF), dom, 1, 1), 0.0)
            dq = dq + jnp.where(m, dqh, 0.0)
        if has_prev:
            dkc = dkc + _shift_units(dkp, -d)
            dvc = dvc + _shift_units(dvp, -d)
        _store_units(dq_ref, dq, g)
        _store_units(dk_ref, dkc, g)
        _store_units(dv_ref, dvc, g)

    blk = pl.BlockSpec((S, 128), lambda b, hf: (b, g * 2 + hf))
    anyspec = pl.BlockSpec(memory_space=pl.ANY)
    return tuple(pl.pallas_call(
        body, name=name, grid=(B, 2),
        in_specs=[blk] * 6 + [anyspec] * 3, out_specs=[blk] * 3,
        out_shape=[jax.ShapeDtypeStruct(q.shape, F32)] * 3,
        input_output_aliases={6: 0, 7: 1, 8: 2},
        compiler_params=_cp(("parallel", "parallel")),
    )(q, k, v, do, cb, lse, *aliases))


def _kv_grad_sum(parts, cos, sin, *, name, tm=512):
    T = parts[0][0].shape[0]
    n_l = len(parts)

    def body(*refs):
        c_ref, s_ref = refs[0], refs[1]
        dk_ref, dv_ref = refs[2 + 2 * n_l:]
        dk = refs[2][...]
        dv = refs[3][...]
        for li in range(1, n_l):
            dk = dk + refs[2 + 2 * li][...]
            dv = dv + refs[3 + 2 * li][...]
        dk_ref[...] = _rot(dk, c_ref[...], s_ref[...], -1.0).astype(dk_ref.dtype)
        dv_ref[...] = dv.astype(dv_ref.dtype)

    full = pl.BlockSpec((tm, MAIN_W), lambda i: (i, 0))
    tab = pl.BlockSpec((tm, 128), lambda i: (i, 0))
    ops = [cos, sin] + [t for part in parts for t in part]
    return pl.pallas_call(
        body, name=name, grid=(T // tm,), in_specs=[tab, tab] + [full] * (2 * n_l), out_specs=[full, full],
        out_shape=[jax.ShapeDtypeStruct((T, MAIN_W), BF)] * 2,
        compiler_params=_cp(("parallel",)),
    )(*ops)


def _group_softmax(lse):
    l0, l1, l2 = lse[:, 0:256], lse[:, 256:512], lse[:, 512:768]
    mx = jnp.maximum(jnp.maximum(l0, l1), l2)
    e0, e1, e2 = jnp.exp(l0 - mx), jnp.exp(l1 - mx), jnp.exp(l2 - mx)
    tot = e0 + e1 + e2
    return e0 / tot, e1 / tot, e2 / tot


def _dil_combine_fwd(o, lse, y_alias, *, name, tm=512):
    T = o.shape[0]

    def body(o_ref, l_ref, _, y_ref):
        a = jnp.concatenate(_group_softmax(l_ref[...]), axis=1)
        y_ref[...] = (o_ref[...] * a).astype(y_ref.dtype)

    blk = pl.BlockSpec((tm, MAIN_W), lambda i: (i, 0))
    return pl.pallas_call(
        body, name=name, grid=(T // tm,), in_specs=[blk, blk, pl.BlockSpec(memory_space=pl.ANY)], out_specs=blk,
        out_shape=jax.ShapeDtypeStruct(y_alias.shape, y_alias.dtype), input_output_aliases={2: 0},
        compiler_params=_cp(("parallel",)),
    )(o, lse, y_alias)


def _dil_combine_bwd(dy, o, lse, *, name, tm=256):
    T = o.shape[0]
    lane_r = lax.broadcasted_iota(jnp.int32, (256, 256), 0) // HEAD_DIM
    lane_c = lax.broadcasted_iota(jnp.int32, (256, 256), 1) // HEAD_DIM
    ones_bd = (lane_r == lane_c).astype(BF)

    def body(dy_ref, o_ref, l_ref, e_ref, do_ref, c_ref):
        dyv = dy_ref[...]
        alphas = _group_softmax(l_ref[...])
        prod = dyv * o_ref[...]
        e = e_ref[...]
        tot = jnp.zeros((tm, 256), F32)
        for gi in range(3):
            x = prod[:, gi * 256:(gi + 1) * 256]
            hi = x.astype(BF)
            lo = (x - hi.astype(F32)).astype(BF)
            dalpha = jnp.dot(hi, e, preferred_element_type=F32) + jnp.dot(lo, e, preferred_element_type=F32)
            tot = tot + alphas[gi] * dalpha
        a = jnp.concatenate(alphas, axis=1)
        do_ref[...] = (dyv * a).astype(do_ref.dtype)
        c_ref[...] = jnp.concatenate([-al * tot for al in alphas], axis=1)

    blk = pl.BlockSpec((tm, MAIN_W), lambda i: (i, 0))
    return pl.pallas_call(
        body, name=name, grid=(T // tm,),
        in_specs=[blk, blk, blk, pl.BlockSpec((256, 256), lambda i: (0, 0))], out_specs=[blk, blk],
        out_shape=[jax.ShapeDtypeStruct((T, MAIN_W), F32), jax.ShapeDtypeStruct((T, MAIN_W), F32)],
        compiler_params=_cp(("parallel",)),
    )(dy, o, lse, ones_bd)


def _kv_grad(parts, cos, sin, B, S, *, name):
    T = B * S
    tb = S // STEPS
    n_l = len(parts)

    def shifted(g):
        def f(b, t):
            return (b * tb + jnp.minimum(t + DIL[g], tb - 1), g)
        return f

    with_prev = [g for g in range(3) if DIL[g] < tb]
    n_p = len(with_prev)
    per_l = 2 + 2 * n_p

    def body(*refs):
        c_ref, s_ref = refs[0], refs[1]
        ins = refs[2:2 + n_l * per_l]
        dk_ref, dv_ref = refs[2 + n_l * per_l:]
        t = pl.program_id(1)
        dk = jnp.zeros((STEPS, MAIN_W), F32)
        dv = jnp.zeros((STEPS, MAIN_W), F32)
        zero = jnp.zeros((STEPS, 256), F32)
        for li in range(n_l):
            base = li * per_l
            dk = dk + ins[base][...]
            dv = dv + ins[base + 1][...]
            kparts, vparts = [zero] * 3, [zero] * 3
            for pi, g in enumerate(with_prev):
                ok = t + DIL[g] < tb
                kparts[g] = jnp.where(ok, ins[base + 2 + pi][...], 0.0)
                vparts[g] = jnp.where(ok, ins[base + 2 + n_p + pi][...], 0.0)
            dk = dk + jnp.concatenate(kparts, axis=1)
            dv = dv + jnp.concatenate(vparts, axis=1)
        dk_ref[...] = _rot(dk, c_ref[...], s_ref[...], -1.0).astype(dk_ref.dtype)
        dv_ref[...] = dv.astype(dv_ref.dtype)

    full = pl.BlockSpec((STEPS, MAIN_W), lambda b, t: (b * tb + t, 0))
    tab = pl.BlockSpec((STEPS, 128), lambda b, t: (b * tb + t, 0))
    in_specs, ops = [tab, tab], [cos, sin]
    for (kc, kp, vc, vp) in parts:
        in_specs += [full, full] + [pl.BlockSpec((STEPS, 256), shifted(g)) for g in with_prev] * 2
        ops += [kc, vc] + [kp] * n_p + [vp] * n_p
    return pl.pallas_call(
        body, name=name, grid=(B, tb), in_specs=in_specs, out_specs=[full, full],
        out_shape=[jax.ShapeDtypeStruct((T, MAIN_W), BF)] * 2,
        compiler_params=_cp(("parallel", "parallel")),
    )(*ops)


def _loss(y, target, *, name, tm=512):
    T, Dm = y.shape
    nt = T // tm

    def body(y_ref, t_ref, l_ref, d_ref, acc):
        i = pl.program_id(0)
        err = y_ref[...] - t_ref[...]
        d_ref[...] = err / Dm
        part = jnp.sum(jnp.mean(err * err, axis=-1, keepdims=True).reshape(tm // 8, 8, 1), axis=0)

        @pl.when(i == 0)
        def _():
            acc[...] = part

        @pl.when(i > 0)
        def _():
            acc[...] += part

        @pl.when(i == nt - 1)
        def _():
            l_ref[...] = 0.5 * jnp.sum(acc[...], axis=0, keepdims=True)

    row = pl.BlockSpec((tm, Dm), lambda i: (i, 0))
    return pl.pallas_call(
        body, name=name, grid=(nt,), in_specs=[row, row],
        out_specs=[pl.BlockSpec((1, 1), lambda i: (0, 0)), row],
        out_shape=[jax.ShapeDtypeStruct((1, 1), F32), jax.ShapeDtypeStruct((T, Dm), F32)],
        scratch_shapes=[pltpu.VMEM((8, 1), F32)],
        compiler_params=_cp(("arbitrary",)),
    )(y, target)


def _adamw(w, g, m, v, *, name):
    shape = w.shape
    cols = shape[-1]
    rows = w.size // cols
    tm = rows
    for cand in (512, 352, 256, 128):
        if rows > cand and rows % cand == 0 and cand * cols * 4 <= (1 << 20):
            tm = cand
            break

    def body(w_ref, g_ref, m_ref, v_ref, d_ref, mo_ref, vo_ref):
        gv = g_ref[...]
        mn = ADAM_B1 * m_ref[...] + (1.0 - ADAM_B1) * gv
        vn = ADAM_B2 * v_ref[...] + (1.0 - ADAM_B2) * (gv * gv)
        m_hat = mn / (1.0 - ADAM_B1 ** ADAM_STEP)
        v_hat = vn / (1.0 - ADAM_B2 ** ADAM_STEP)
        d_ref[...] = -ADAM_LR * (m_hat / (jnp.sqrt(v_hat) + ADAM_EPS) + ADAM_WD * w_ref[...])
        mo_ref[...] = mn
        vo_ref[...] = vn

    blk = pl.BlockSpec((tm, cols), lambda i: (i, 0))
    outs = pl.pallas_call(
        body, name=name, grid=(rows // tm,), in_specs=[blk] * 4, out_specs=[blk] * 3,
        out_shape=[jax.ShapeDtypeStruct((rows, cols), F32)] * 3,
        compiler_params=_cp(("parallel",)),
    )(*[t.reshape(rows, cols) for t in (w, g, m, v)])
    return tuple(t.reshape(shape) for t in outs)


def _adamw_layer(name, l, w, g, m, v, prev):
    L, rows, cols = w.shape
    tm = rows
    for cand in (512, 352, 256, 176, 128, 64):
        if rows % cand == 0 and cand * cols * 4 <= (1 << 20):
            tm = cand
            break
    if prev is None:
        prev = tuple(lax.empty(w.shape, F32) for _ in range(4))

    def body(w_ref, g_ref, m_ref, v_ref, *rest):
        d_ref, mo_ref, vo_ref, go_ref = rest[4:]
        gv = g_ref[...]
        mn = ADAM_B1 * m_ref[...] + (1.0 - ADAM_B1) * gv
        vn = ADAM_B2 * v_ref[...] + (1.0 - ADAM_B2) * (gv * gv)
        m_hat = mn / (1.0 - ADAM_B1 ** ADAM_STEP)
        v_hat = vn / (1.0 - ADAM_B2 ** ADAM_STEP)
        d_ref[...] = -ADAM_LR * (m_hat / (jnp.sqrt(v_hat) + ADAM_EPS) + ADAM_WD * w_ref[...])
        mo_ref[...] = mn
        vo_ref[...] = vn
        go_ref[...] = gv

    lay = pl.BlockSpec((None, tm, cols), lambda i: (l, i, 0))
    one = pl.BlockSpec((None, tm, cols), lambda i: (0, i, 0))
    return tuple(pl.pallas_call(
        body, name=f"l{l}_adamw_{name}", grid=(rows // tm,),
        in_specs=[lay, one, lay, lay] + [pl.BlockSpec(memory_space=pl.ANY)] * 4, out_specs=[lay] * 4,
        out_shape=[jax.ShapeDtypeStruct(w.shape, F32)] * 4,
        input_output_aliases={4 + i: i for i in range(4)},
        compiler_params=_cp(("parallel",)),
    )(w, g, m, v, *prev))


BIG = {
    'w_in': ((DEPTH, D_MODEL, D_MODEL), 'row'),
    'w_mem_kv': ((DEPTH, D_MODEL, 2 * MEM_W), 'row'),
    'w_out': ((DEPTH, D_MODEL, D_MODEL), 'row'),
    'w_kv': ((1, D_MODEL, 2 * MAIN_W), 'col'),
    'w_gate_up': ((DEPTH, D_MODEL, 2 * D_FF), 'col'),
    'w_down': ((DEPTH, D_FF, D_MODEL), 'row'),
}
BIG_NAMES = tuple(BIG)
N_CHIPS = 4
HBM_ANY = pl.BlockSpec(memory_space=pl.ANY)


def _geom(name):
    (L, R, C), kind = BIG[name]
    if kind == 'row':
        return L, R, C, kind, R // N_CHIPS, C, R // (2 * N_CHIPS)
    return L, R, C, kind, R, C // N_CHIPS, R // 2


def _shard_shape(name):
    L, R, C, kind, rs, cs, rh = _geom(name)
    return (L, rs, cs)


def _half_shape(name):
    L, R, C, kind, rs, cs, rh = _geom(name)
    return (L, rh, cs)


def _full_win(ref, name, s, h):
    L, R, C, kind, rs, cs, rh = _geom(name)
    if kind == 'row':
        rows = pl.ds(s * rs, rs) if h is None else pl.ds(s * rs + h * rh, rh)
        return ref.at[:, rows, :]
    rows = slice(None) if h is None else pl.ds(h * rh, rh)
    return ref.at[:, rows, pl.ds(s * cs, cs)]


def _shard_half(ref, name, h):
    L, R, C, kind, rs, cs, rh = _geom(name)
    return ref.at[:, pl.ds(h * rh, rh), :]


def _halves_win(ref, name, s):
    L, R, C, kind, rs, cs, rh = _geom(name)
    if kind == 'row':
        return ref.at[:, pl.ds(s * rh, rh), :]
    return ref.at[:, :, pl.ds(s * cs, cs)]


def _halves_shape(name):
    L, R, C, kind, rs, cs, rh = _geom(name)
    return (L, N_CHIPS * rh, cs) if kind == 'row' else (L, rh, C)


def _place():
    x, y, c = lax.axis_index("x"), lax.axis_index("y"), lax.axis_index("c")
    chips = [(1 - x, y), (x, 1 - y), (1 - x, 1 - y)]
    return x, y, c, chips


SMALL_ROWS = 24


def _all_gather(shards, small):
    names = BIG_NAMES
    nw = len(names)

    def body(*refs):
        src = dict(zip(names, refs[:nw]))
        small_ref = refs[nw]
        dst = dict(zip(names, refs[nw + 1:2 * nw + 1]))
        small_out = refs[2 * nw + 1]
        send_sems, recv_sems, local_sems = refs[2 * nw + 2:]
        x, y, c, chips = _place()
        s = 2 * x + y
        sib = (x, y, 1 - c)

        def remote(k, src_ref, dst_ref, to):
            return pltpu.make_async_remote_copy(src_ref=src_ref, dst_ref=dst_ref, send_sem=send_sems.at[k],
                                                recv_sem=recv_sems.at[k], device_id=to, device_id_type=MESH)

        local = []
        for wi, nm in enumerate(names):
            local.append(pltpu.make_async_copy(src[nm], _full_win(dst[nm], nm, s, None), local_sems.at[wi]))
        local.append(pltpu.make_async_copy(small_ref, small_out.at[s], local_sems.at[nw]))
        for cp in local:
            cp.start()
        sends = []
        for j, (px, py) in enumerate(chips):
            for wi, nm in enumerate(names):
                sends.append(remote(wi * 6 + j, _shard_half(src[nm], nm, c), _full_win(dst[nm], nm, s, c), (px, py, c)))
            sends.append(remote(nw * 6 + j, small_ref, small_out.at[s], (px, py, c)))
        for cp in sends:
            cp.start()
        for j, (px, py) in enumerate(chips):
            sp = 2 * px + py
            for wi, nm in enumerate(names):
                w = _full_win(dst[nm], nm, sp, c)
                remote(wi * 6 + j, w, w, sib).wait_recv()
                fwd = remote(wi * 6 + 3 + j, w, w, sib)
                fwd.start()
                sends.append(fwd)
            remote(nw * 6 + j, small_ref, small_out.at[sp], sib).wait_recv()
        for j, (px, py) in enumerate(chips):
            sp = 2 * px + py
            for wi, nm in enumerate(names):
                w = _full_win(dst[nm], nm, sp, 1 - c)
                remote(wi * 6 + 3 + j, w, w, sib).wait_recv()
        for cp in sends:
            cp.wait_send()
        for cp in local:
            cp.wait()

    n_sem = nw * 6 + 3
    outs = pl.pallas_call(
        body, name="all_gather_weights",
        in_specs=[HBM_ANY] * (nw + 1), out_specs=[HBM_ANY] * (nw + 1),
        out_shape=[jax.ShapeDtypeStruct(BIG[nm][0], BF) for nm in names]
        + [jax.ShapeDtypeStruct((N_CHIPS, SMALL_ROWS, 256), F32)],
        scratch_shapes=[pltpu.SemaphoreType.DMA((n_sem,)), pltpu.SemaphoreType.DMA((n_sem,)),
                        pltpu.SemaphoreType.DMA((nw + 1,))],
    )(*[shards[nm] for nm in names], small)
    return dict(zip(names, outs[:nw])), outs[nw]


SEM_SPEC = pl.BlockSpec(memory_space=pltpu.SEMAPHORE)
HBM_SPEC = pl.BlockSpec(memory_space=pltpu.HBM)
DATAFLOW = pltpu.SideEffectType.DATAFLOW_SIDE_EFFECTING


def _in_hbm(a):
    return pltpu.with_memory_space_constraint(a, pltpu.HBM)


def _remote(src, dst, send_sems, recv_sems, k, to):
    return pltpu.make_async_remote_copy(src_ref=src, dst_ref=dst, send_sem=send_sems.at[k], recv_sem=recv_sems.at[k],
                                        device_id=to, device_id_type=MESH)


def _split_start(name, bufs, n_copies, sends, after=None):
    nb = len(bufs)
    n_in = nb + (0 if after is None else 1)

    def body(*refs):
        in_refs = refs[:nb]
        send_sems, recv_sems = refs[n_in], refs[n_in + 1]
        token = refs[-1]
        for k, (src, dst, to) in enumerate(sends(in_refs)):
            _remote(src, dst, send_sems, recv_sems, k, to).start()
        token[...] = jnp.zeros_like(token)

    outs = pl.pallas_call(
        body, name=name,
        out_shape=(pltpu.SemaphoreType.DMA((n_copies,)), pltpu.SemaphoreType.DMA((n_copies,)),
                   *[pltpu.HBM(b.shape, b.dtype) for b in bufs], jax.ShapeDtypeStruct((8, 128), F32)),
        in_specs=[HBM_SPEC] * nb + [HBM_ANY] * (n_in - nb),
        out_specs=(SEM_SPEC, SEM_SPEC, *[HBM_SPEC] * nb, pl.BlockSpec(memory_space=pltpu.VMEM)),
        input_output_aliases={i: 2 + i for i in range(nb)},
        compiler_params=pltpu.CompilerParams(has_side_effects=DATAFLOW),
    )(*[_in_hbm(b) for b in bufs], *([] if after is None else [after]))
    return outs[0], outs[1], list(outs[2:2 + nb]), outs[-1]


def _split_wait(name, send_sems, recv_sems, bufs, after, sends, arrivals):
    nb = len(bufs)

    def body(*refs):
        in_refs = refs[:nb]
        s_sems, r_sems = refs[nb], refs[nb + 1]
        me = (lax.axis_index("x"), lax.axis_index("y"), lax.axis_index("c"))
        for k, (src, dst, to) in enumerate(sends(in_refs)):
            _remote(src, dst, s_sems, r_sems, k, to).wait_send()
        for k, win in enumerate(arrivals(in_refs)):
            _remote(win, win, s_sems, r_sems, k, me).wait_recv()

    outs = pl.pallas_call(
        body, name=name,
        out_shape=[pltpu.HBM(b.shape, b.dtype) for b in bufs],
        in_specs=[HBM_SPEC] * nb + [SEM_SPEC, SEM_SPEC, HBM_ANY],
        out_specs=[HBM_SPEC] * nb,
        input_output_aliases={i: i for i in range(nb)},
        compiler_params=pltpu.CompilerParams(has_side_effects=DATAFLOW),
    )(*bufs, send_sems, recv_sems, after)
    return list(outs)


MIX_W = ('w_in', 'w_mem_kv', 'w_out')
FFN_W = ('w_gate_up', 'w_down')
LAYER_W = MIX_W + FFN_W


def _place_own(l, names, shards, small, sc):
    nw = len(names)
    has_small = small is not None
    n_ops = nw + (1 if has_small else 0)

    def body(sc_ref, *refs):
        for src, dst in zip(refs[:n_ops], refs[n_ops:]):
            dst[...] = src[...]

    in_specs, out_specs, out_shape, ops = [], [], [], list(shards)
    for nm in names:
        L, R, C, kind, rs, cs, rh = _geom(nm)
        in_specs.append(pl.BlockSpec((1, rs, cs), lambda i, sc_ref: (0, 0, 0)))
        if kind == 'row':
            out_specs.append(pl.BlockSpec((1, rs, cs), lambda i, sc_ref: (0, sc_ref[0], 0)))
        else:
            out_specs.append(pl.BlockSpec((1, rs, cs), lambda i, sc_ref: (0, 0, sc_ref[0])))
        out_shape.append(jax.ShapeDtypeStruct((1, R, C), BF))
    if has_small:
        in_specs.append(pl.BlockSpec((SMALL_ROWS, 256), lambda i, sc_ref: (0, 0)))
        out_specs.append(pl.BlockSpec((None, SMALL_ROWS, 256), lambda i, sc_ref: (sc_ref[0], 0, 0)))
        out_shape.append(jax.ShapeDtypeStruct((N_CHIPS, SMALL_ROWS, 256), F32))
        ops.append(small)
    return pl.pallas_call(
        body, name=f"{l}_place_own_shard",
        grid_spec=pltpu.PrefetchScalarGridSpec(num_scalar_prefetch=1, grid=(1,), in_specs=in_specs, out_specs=out_specs),
        out_shape=out_shape,
        compiler_params=_cp(("arbitrary",)),
    )(sc, *ops)


def _gather_start(l, names, shards, small, sc, after=None):
    nw = len(names)
    has_small = small is not None
    fulls = _place_own(l, names, shards, small, sc)
    bufs = list(shards) + ([small] if has_small else []) + list(fulls)
    n_src = nw + (1 if has_small else 0)

    def sends(refs):
        x, y, c, chips = _place()
        s = 2 * x + y
        out = []
        for (px, py) in chips:
            for wi, nm in enumerate(names):
                out.append((_shard_half(refs[wi], nm, c), _full_win(refs[n_src + wi], nm, s, c), (px, py, c)))
            if has_small:
                out.append((refs[nw], refs[n_src + nw].at[s], (px, py, c)))
        return out

    def arrivals(refs):
        x, y, c, chips = _place()
        out = []
        for (px, py) in chips:
            sp = 2 * px + py
            for wi, nm in enumerate(names):
                out.append(_full_win(refs[n_src + wi], nm, sp, c))
            if has_small:
                out.append(refs[n_src + nw].at[sp])
        return out

    n_copies = 3 * n_src
    send_sems, recv_sems, bufs, token = _split_start(f"{l}_gather_ici_start", bufs, n_copies, sends, after)
    return dict(l=l, names=names, has_small=has_small, sems=(send_sems, recv_sems), bufs=bufs, sends=sends,
                arrivals=arrivals, token=token)


def _gather_forward(st, after):
    l, names = st['l'], st['names']
    nw = len(names)
    n_src = nw + (1 if st['has_small'] else 0)
    bufs = _split_wait(f"{l}_gather_ici_wait", *st['sems'], st['bufs'], after, st['sends'], st['arrivals'])
    fulls = bufs[n_src:n_src + nw]
    small_all = bufs[n_src + nw] if st['has_small'] else None

    def sends(refs):
        x, y, c, chips = _place()
        out = []
        for (px, py) in chips:
            sp = 2 * px + py
            for wi, nm in enumerate(names):
                w = _full_win(refs[wi], nm, sp, c)
                out.append((w, w, (x, y, 1 - c)))
        return out

    def arrivals(refs):
        x, y, c, chips = _place()
        out = []
        for (px, py) in chips:
            sp = 2 * px + py
            for wi, nm in enumerate(names):
                out.append(_full_win(refs[wi], nm, sp, 1 - c))
        return out

    send_sems, recv_sems, fulls, token = _split_start(f"{l}_gather_d2d_start", fulls, 3 * nw, sends)
    return dict(l=l, names=names, sems=(send_sems, recv_sems), bufs=fulls, sends=sends, arrivals=arrivals,
                small_all=small_all, token=token)


def _gather_finish(st, after):
    fulls = _split_wait(f"{st['l']}_gather_d2d_wait", *st['sems'], st['bufs'], after, st['sends'], st['arrivals'])
    return dict(zip(st['names'], fulls)), st['small_all']


def _reduce_start(tag, names, grads):
    nw = len(names)
    recv = [lax.empty((1,) + _halves_shape(nm)[1:], F32) for nm in names]
    bufs = [grads[nm] for nm in names] + recv

    def windows(refs, half_of):
        x, y, c, _ = _place()
        h = half_of(c)
        out = []
        for wi, nm in enumerate(names):
            L, R, C, kind, rs, cs, rh = _geom(nm)
            if kind == 'row':
                for sp in range(N_CHIPS):
                    out.append((_full_win(refs[wi], nm, sp, h), _halves_win(refs[nw + wi], nm, sp)))
            else:
                out.append((refs[wi].at[:, pl.ds(h * rh, rh), :], refs[nw + wi]))
        return out

    def sends(refs):
        x, y, c, _ = _place()
        return [(src, dst, (x, y, 1 - c)) for src, dst in windows(refs, lambda c: 1 - c)]

    def arrivals(refs):
        return [dst for _, dst in windows(refs, lambda c: c)]

    n_copies = sum(N_CHIPS if BIG[nm][1] == 'row' else 1 for nm in names)
    send_sems, recv_sems, bufs, token = _split_start(tag + "_halves_start", bufs, n_copies, sends)
    return dict(tag=tag, names=names, sems=(send_sems, recv_sems), bufs=bufs, sends=sends, arrivals=arrivals, token=token)


def _reduce_mid(st, after, sc):
    tag, names = st['tag'], st['names']
    nw = len(names)
    bufs = _split_wait(tag + "_halves_wait", *st['sems'], st['bufs'], after, st['sends'], st['arrivals'])
    halves, own = [], []
    for wi, nm in enumerate(names):
        hb, ow = _add_halves(nm, bufs[wi], bufs[nw + wi], sc, tag)
        halves.append(hb)
        own.append(ow)
    pieces = [lax.empty((3, 1) + _half_shape(nm)[1:], BF) for nm in names]

    def sends(refs):
        x, y, c, chips = _place()
        out = []
        for j, (px, py) in enumerate(chips):
            for wi, nm in enumerate(names):
                out.append((_halves_win(refs[wi], nm, 2 * px + py), refs[nw + wi].at[j], (px, py, c)))
        return out

    def arrivals(refs):
        return [refs[nw + wi].at[j] for j in range(3) for wi in range(nw)]

    send_sems, recv_sems, bufs, token = _split_start(tag + "_pieces_start", halves + pieces, 3 * nw, sends)
    return dict(tag=tag, names=names, sems=(send_sems, recv_sems), bufs=bufs, sends=sends, arrivals=arrivals, own=own,
                token=token)


def _reduce_late(st, after, sc):
    tag, names = st['tag'], st['names']
    nw = len(names)
    bufs = _split_wait(tag + "_pieces_wait", *st['sems'], st['bufs'], after, st['sends'], st['arrivals'])
    gsh = [_sum_pieces(nm, st['own'][wi], bufs[nw + wi], sc, tag) for wi, nm in enumerate(names)]

    def sends(refs):
        x, y, c, _ = _place()
        return [(_shard_half(refs[wi], nm, c), _shard_half(refs[wi], nm, c), (x, y, 1 - c)) for wi, nm in enumerate(names)]

    def arrivals(refs):
        x, y, c, _ = _place()
        return [_shard_half(refs[wi], nm, 1 - c) for wi, nm in enumerate(names)]

    send_sems, recv_sems, bufs, token = _split_start(tag + "_share_start", gsh, nw, sends)
    return dict(tag=tag, names=names, sems=(send_sems, recv_sems), bufs=bufs, sends=sends, arrivals=arrivals, token=token)


def _reduce_finish(st, after):
    gsh = _split_wait(st['tag'] + "_share_wait", *st['sems'], st['bufs'], after, st['sends'], st['arrivals'])
    return dict(zip(st['names'], gsh))


def _add_halves(name, g, r, sc, tag):
    _, R, C, kind, rs, cs, rh = _geom(name)
    L = g.shape[0]
    tr = rh if kind == 'row' else 256
    nr = rh // tr

    def body(sc_ref, g_ref, r_ref, hb_ref, own_ref):
        sp = pl.program_id(2)
        tot = g_ref[...] + r_ref[...]
        hb_ref[...] = tot.astype(hb_ref.dtype)

        @pl.when(sp == sc_ref[0])
        def _():
            own_ref[...] = tot

    if kind == 'row':
        g_map = lambda l, ri, sp, sc_ref: (l, sp * 2 + sc_ref[1], 0)
        h_map = lambda l, ri, sp, sc_ref: (l, sp, 0)
    else:
        g_map = lambda l, ri, sp, sc_ref: (l, sc_ref[1] * nr + ri, sp)
        h_map = lambda l, ri, sp, sc_ref: (l, ri, sp)
    own_map = lambda l, ri, sp, sc_ref: (l, ri, 0)
    blk = (None, tr, cs)
    return pl.pallas_call(
        body, name=tag + "_add_halves_" + name,
        grid_spec=pltpu.PrefetchScalarGridSpec(
            num_scalar_prefetch=1, grid=(L, nr, N_CHIPS),
            in_specs=[pl.BlockSpec(blk, g_map), pl.BlockSpec(blk, h_map)],
            out_specs=[pl.BlockSpec(blk, h_map), pl.BlockSpec(blk, own_map)]),
        out_shape=[jax.ShapeDtypeStruct((L,) + _halves_shape(name)[1:], BF),
                   jax.ShapeDtypeStruct((L,) + _half_shape(name)[1:], F32)],
        compiler_params=_cp(("parallel", "parallel", "arbitrary")),
    )(sc, g, r)


def _sum_pieces(name, own, pieces, sc, tag):
    _, R, C, kind, rs, cs, rh = _geom(name)
    L = own.shape[0]
    tr = rh if kind == 'row' else 256
    nr = rh // tr

    def body(sc_ref, o_ref, p_ref, out_ref):
        out_ref[...] = o_ref[...] + p_ref[0].astype(F32) + p_ref[1].astype(F32) + p_ref[2].astype(F32)

    blk = (None, tr, cs)
    return pl.pallas_call(
        body, name=tag + "_sum_pieces_" + name,
        grid_spec=pltpu.PrefetchScalarGridSpec(
            num_scalar_prefetch=1, grid=(L, nr),
            in_specs=[pl.BlockSpec(blk, lambda l, ri, sc_ref: (l, ri, 0)),
                      pl.BlockSpec((3, None, tr, cs), lambda l, ri, sc_ref: (0, l, ri, 0))],
            out_specs=pl.BlockSpec(blk, lambda l, ri, sc_ref: (l, sc_ref[1] * nr + ri, 0))),
        out_shape=jax.ShapeDtypeStruct((L,) + _shard_shape(name)[1:], F32),
        compiler_params=_cp(("parallel", "parallel")),
    )(sc, own, pieces)


def _small_gather_start(v, sc):
    rows = v.shape[0]

    def place(sc_ref, v_ref, o_ref):
        o_ref[...] = v_ref[...]

    slots = pl.pallas_call(
        place, name="small_grads_place_own",
        grid_spec=pltpu.PrefetchScalarGridSpec(
            num_scalar_prefetch=1, grid=(1,),
            in_specs=[pl.BlockSpec((rows, 128), lambda i, sc_ref: (0, 0))],
            out_specs=pl.BlockSpec((None, rows, 128), lambda i, sc_ref: (2 * sc_ref[0] + sc_ref[1], 0, 0))),
        out_shape=jax.ShapeDtypeStruct((8, rows, 128), F32),
        compiler_params=_cp(("arbitrary",)),
    )(sc, v)

    def peers():
        x, y, c, _ = _place()
        flips = [(fx, fy, fc) for fx in (0, 1) for fy in (0, 1) for fc in (0, 1)][1:]
        return [((1 - x if fx else x), (1 - y if fy else y), (1 - c if fc else c)) for fx, fy, fc in flips]

    def sends(refs):
        x, y, c, _ = _place()
        return [(refs[0], refs[1].at[4 * x + 2 * y + c], p) for p in peers()]

    def arrivals(refs):
        return [refs[1].at[4 * px + 2 * py + pc] for px, py, pc in peers()]

    send_sems, recv_sems, bufs, token = _split_start("small_grads_gather_start", [v, slots], 7, sends)
    return dict(sems=(send_sems, recv_sems), bufs=bufs, sends=sends, arrivals=arrivals, token=token)


def _small_gather_finish(st, after):
    return _split_wait("small_grads_gather_wait", *st['sems'], st['bufs'], after, st['sends'], st['arrivals'])[1]


def _sum8(v8, *, name, tr=336):
    rows = v8.shape[1]
    tr = min(tr, rows)
    assert rows % tr == 0

    def body(v_ref, o_ref):
        tot = v_ref[0]
        for d in range(1, 8):
            tot = tot + v_ref[d]
        o_ref[...] = tot

    return pl.pallas_call(
        body, name=name, grid=(rows // tr,),
        in_specs=[pl.BlockSpec((8, tr, 128), lambda i: (0, i, 0))], out_specs=pl.BlockSpec((tr, 128), lambda i: (i, 0)),
        out_shape=jax.ShapeDtypeStruct((rows, 128), F32),
        compiler_params=_cp(("parallel",)),
    )(v8)


def _block_diag(w_pool_l):
    wbd = jnp.zeros((MAIN_W, MAIN_W), F32)
    for gi in range(len(POOL_WINDOWS)):
        wbd = lax.dynamic_update_slice(wbd, w_pool_l[gi], (gi * POOL_GROUP, gi * POOL_GROUP))
    return wbd.astype(BF)


def _unpack_small(small_all):
    ng = small_all[:, :16, :].reshape(N_CHIPS, DEPTH, 4, 256).transpose(1, 2, 0, 3).reshape(DEPTH, 4, D_MODEL)
    ps = small_all[:, 16:18, :POOL_GROUP].transpose(1, 0, 2).reshape(N_A, MAIN_W)
    return ng, ps


def _local_step(x, mem, positions, on_forward, on_backward, mem_norm, w_pool, kv_norm, target):
    B, S, _ = x.shape
    T = B * S
    xc = x.reshape(T, D_MODEL)
    memf = mem.reshape(B * N_MEM, D_MODEL)
    tgt = target.reshape(T, D_MODEL)
    cos, sin = _rope_tables(positions.reshape(T, 1), name="rope_tables")
    wbd = [_block_diag(w_pool[l]) for l in range(N_A)]
    nbo = D_FF // 256
    fw = []
    rk = rv = None
    kv_saved = None
    wts = []
    norm_gains = pool_scale = y2 = None

    def tied(vec, tok):
        return vec if tok is None else vec + tok

    for l in range(DEPTH):
        t = f"l{l}_"
        got = on_forward('start', l, y2)
        wts.append(dict(got[0]))
        if l == 0:
            norm_gains, pool_scale = _unpack_small(got[1])
        sv = {'x_in': xc}
        h0, sv['r0'] = _norm_fwd(xc, tied(norm_gains[l, 0], got[2]), name=t + "norm0", out_dtype=BF)
        z, = _mm(h0, wts[l]['w_in'], b_layer=0, name=t + "mm_in")
        memn, sv['rm'] = _norm_fwd(memf, mem_norm[l], name=t + "norm_mem", out_dtype=BF, tm=256)
        kvm, = _mm(memn, wts[l]['w_mem_kv'], b_layer=0, name=t + "mm_memkv", out_dtypes=(BF,))
        if l < N_A:
            ycat, sv['p'] = _pool_fwd(z, wbd[l], pool_scale[l], B, S, name=t + "pool_fwd")
        else:
            rq = _rope_apply(z, cos, sin, name=t + "rope_q", out_dtype=F32)
            o = lax.empty((T, MAIN_W), F32)
            lse = lax.empty((T, MAIN_W), F32)
            for g in range(3):
                o, lse = _dil_fwd(g, rq, rk, rv, o, lse, B, S, name=t + f"dil_fwd{g}")
            ycat = _dil_combine_fwd(o, lse, lax.empty((T, D_MODEL), BF), name=t + "dil_combine")
            sv.update(rq=rq, o=o, lse=lse)
        ycat, sv['lse_m'] = _memattn_fwd(z, kvm, ycat, B, S, name=t + "memattn_fwd")
        tok = on_forward('mid', l, ycat)
        y1, = _mm(ycat, wts[l]['w_out'], b_layer=0, name=t + "mm_out")
        wts[l].update(on_forward('ffn', l, y1)[0])
        x1, sv['r1'] = _norm_fwd(y1, tied(norm_gains[l, 1], tok), name=t + "norm1", res=xc)
        h2, sv['r2'] = _norm_fwd(x1, norm_gains[l, 2], name=t + "norm2", out_dtype=BF)
        gg, uu, aa = _mm(h2, wts[l]['w_gate_up'], b_layer=0, b_offsets=(0, nbo), out_n=D_FF, tn=256, name=t + "mm_gate_up",
                         epilogue=_swiglu_fwd_epilogue, out_dtypes=(BF, BF, BF))
        on_forward('post', l, gg)
        y2, = _mm(aa, wts[l]['w_down'], b_layer=0, tk=D_FF, name=t + "mm_down")
        x2, sv['r3'] = _norm_fwd(y2, norm_gains[l, 3], name=t + "norm3", res=x1)
        sv.update(h0=h0, z=z, memn=memn, kvm=kvm, ycat=ycat, y1=y1, x1=x1, h2=h2, gg=gg, uu=uu, aa=aa, y2=y2)
        fw.append(sv)
        xc = x2
        if l == N_A - 1:
            kvn, rkv = _norm_fwd(xc, kv_norm, name="norm_kv", out_dtype=BF)
            kv, = _mm(kvn, wts[N_A - 1]['w_kv'], b_layer=0, name="mm_kv")
            rk, rv = _rope_apply(kv, cos, sin, name="rope_k", passthrough=True, out_dtype=F32)
            kv_saved = (xc, kvn, rkv)

    loss, dx = _loss(xc, tgt, name="loss")

    d_ng = [[None] * 4 for _ in range(DEPTH)]
    d_memnorm = [None] * DEPTH
    d_wbd = [None] * N_A
    d_pscale = [None] * N_A
    d_kvnorm = None
    kv_parts = []
    tok = None

    def as3d(gl):
        return {nm: g.reshape((1,) + g.shape) for nm, g in gl.items()}

    for l in reversed(range(DEPTH)):
        t = f"l{l}_b_"
        sv = fw[l]
        gl = {}
        dy2, d_ng[l][3] = _norm_bwd(dx, sv['y2'], sv['r3'], tied(norm_gains[l, 3], tok), name=t + "norm3", out_dtype=BF)
        gl['w_down'], = _mm(sv['aa'], dy2, ta=True, tm=1408, tk=2048, name=t + "dw_down")
        dg, du = _mm(dy2, wts[l]['w_down'], tb=True, b_layer=0, tn=256, name=t + "d_act",
                     extras=((sv['gg'], 'tile'), (sv['uu'], 'tile')), epilogue=_swiglu_bwd_epilogue, out_dtypes=(BF, BF))
        gl['w_gate_up'], = _mm(sv['h2'], (dg, du), ta=True, tn=1408, tk=1024, name=t + "dw_gate_up")
        dh2, = _mm((dg, du), wts[l]['w_gate_up'], tb=True, b_layer=0, tn=1024, tk=1408, name=t + "d_h2", out_dtypes=(BF,))
        dx1, d_ng[l][2] = _norm_bwd(dh2, sv['x1'], sv['r2'], norm_gains[l, 2], name=t + "norm2", add=dx)
        tok = on_backward('ffn', l, dx1, as3d(gl))
        dy1, d_ng[l][1] = _norm_bwd(dx1, sv['y1'], sv['r1'], tied(norm_gains[l, 1], tok), name=t + "norm1", out_dtype=BF)
        gl['w_out'], = _mm(sv['ycat'], dy1, ta=True, tk=4096, name=t + "dw_out")
        dycat, = _mm(dy1, wts[l]['w_out'], tb=True, b_layer=0, name=t + "d_ycat")
        dz = lax.empty((T, D_MODEL), BF)
        dz, dkm, dvm = _memattn_bwd(dycat, sv['z'], sv['kvm'], sv['lse_m'], dz, B, S, name=t + "memattn")
        if l < N_A:
            dz, d_wbd[l], d_pscale[l] = _pool_bwd(dycat, sv['p'], wbd[l], pool_scale[l], dz, B, S, name=t + "pool")
        else:
            do, cb = _dil_combine_bwd(dycat, sv['o'], sv['lse'], name=t + "dil_combine")
            acc = tuple(lax.empty((T, MAIN_W), F32) for _ in range(3))
            for g in range(3):
                acc = _dil_bwd(g, sv['rq'], rk, rv, do, cb, sv['lse'], acc, B, S, name=t + f"dil{g}")
            dz = _rope_apply(acc[0], cos, sin, name=t + "rope_q", sign=-1.0, alias=dz)
            kv_parts.append(acc[1:])
        tok = on_backward('mix', l, dz, as3d(gl))
        gl['w_in'], = _mm(sv['h0'], dz, ta=True, tk=4096, name=t + "dw_in")
        dh0, = _mm(dz, wts[l]['w_in'], tb=True, b_layer=0, name=t + "d_h0", out_dtypes=(BF,))
        dx, d_ng[l][0] = _norm_bwd(dh0, sv['x_in'], sv['r0'], tied(norm_gains[l, 0], tok), name=t + "norm0", add=dx1)
        gl['w_mem_kv'], = _mm(sv['memn'], (dkm, dvm), ta=True, tn=256, name=t + "dw_memkv")
        dmemn, = _mm((dkm, dvm), wts[l]['w_mem_kv'], tb=True, b_layer=0, tk=256, name=t + "d_memn", out_dtypes=(BF,))
        _, d_memnorm[l] = _norm_bwd(dmemn, memf, sv['rm'], mem_norm[l], name=t + "norm_mem", out_dtype=BF, tm=256)
        if l == N_A:
            dk, dv = _kv_grad_sum(kv_parts, cos, sin, name="kv_grad")
            x_kv, kvn, rkv = kv_saved
            gl['w_kv'], = _mm(kvn, (dk, dv), ta=True, tn=768, tk=2048, name="dw_kv")
            dkvn, = _mm((dk, dv), wts[N_A - 1]['w_kv'], tb=True, b_layer=0, tn=1024, tk=768, name="d_kvn", out_dtypes=(BF,))
            dx, d_kvnorm = _norm_bwd(dkvn, x_kv, rkv, kv_norm, name="norm_kv_b", add=dx)
        tok = on_backward('end', l, dx, as3d(gl))

    small = {
        'norm_gains': jnp.stack([jnp.concatenate(d_ng[l], axis=0) for l in range(DEPTH)]),
        'mem_norm': jnp.concatenate(d_memnorm, axis=0),
        'kv_norm': d_kvnorm.reshape(D_MODEL),
        'pool_scale': jnp.concatenate(d_pscale, axis=0),
        'w_pool': jnp.stack([jnp.stack([d_wbd[l][gi * POOL_GROUP:(gi + 1) * POOL_GROUP, gi * POOL_GROUP:(gi + 1) * POOL_GROUP]
                                        for gi in range(len(POOL_WINDOWS))]) for l in range(N_A)]),
    }
    return loss, dx, small


SMALL_ORDER = ('norm_gains', 'mem_norm', 'kv_norm', 'pool_scale', 'w_pool')
SMALL_VEC_ROWS = 2560


def kernel(x, mem, positions, norm_gains, mem_norm, w_in, w_mem_kv, w_out, w_pool, pool_scale, kv_norm, w_kv, w_gate_up, w_down, loss_target, m_norm_gains, m_mem_norm, m_w_in, m_w_mem_kv, m_w_out, m_w_pool, m_pool_scale, m_kv_norm, m_w_kv, m_w_gate_up, m_w_down, v_norm_gains, v_mem_norm, v_w_in, v_w_mem_kv, v_w_out, v_w_pool, v_pool_scale, v_kv_norm, v_w_kv, v_w_gate_up, v_w_down):
    xi, yi, ci = lax.axis_index("x"), lax.axis_index("y"), lax.axis_index("c")
    s = 2 * xi + yi
    sc = jnp.stack([s, ci]).astype(jnp.int32)
    weights = dict(norm_gains=norm_gains, mem_norm=mem_norm, w_in=w_in, w_mem_kv=w_mem_kv, w_out=w_out, w_pool=w_pool,
                   pool_scale=pool_scale, kv_norm=kv_norm, w_kv=w_kv, w_gate_up=w_gate_up, w_down=w_down)
    moms = dict(norm_gains=m_norm_gains, mem_norm=m_mem_norm, w_in=m_w_in, w_mem_kv=m_w_mem_kv, w_out=m_w_out,
                w_pool=m_w_pool, pool_scale=m_pool_scale, kv_norm=m_kv_norm, w_kv=m_w_kv, w_gate_up=m_w_gate_up,
                w_down=m_w_down)
    vels = dict(norm_gains=v_norm_gains, mem_norm=v_mem_norm, w_in=v_w_in, w_mem_kv=v_w_mem_kv, w_out=v_w_out,
                w_pool=v_w_pool, pool_scale=v_pool_scale, kv_norm=v_kv_norm, w_kv=v_w_kv, w_gate_up=v_w_gate_up,
                w_down=v_w_down)

    small_w = jnp.zeros((SMALL_ROWS, 256), F32)
    small_w = lax.dynamic_update_slice(small_w, norm_gains.reshape(16, 256), (0, 0))
    small_w = lax.dynamic_update_slice(small_w, pool_scale, (16, 0))
    def shard_of(nm, l):
        return w_kv.astype(BF).reshape(_shard_shape('w_kv')) if nm == 'w_kv' else weights[nm][l:l + 1].astype(BF)

    groups = {'l0a': (0, MIX_W), 'l0b': (0, FFN_W)}
    groups.update({f"l{l}": (l, LAYER_W + (('w_kv',) if l == N_A - 1 else ())) for l in range(1, DEPTH)})
    on_ici, on_d2d, gathered = {}, {}, {}

    def start_group(tag, after):
        l, names = groups[tag]
        on_ici[tag] = _gather_start(tag, names, [shard_of(nm, l) for nm in names], small_w if tag == 'l0a' else None, sc,
                                    after)
        return on_ici[tag]['token'][0, 0]

    def on_forward(where, l, after):
        if where == 'start':
            if l == 0:
                start_group('l0a', None)
                st = on_ici.pop('l0a')
                fwd = _gather_forward(st, st['token'])
                w, small_all = _gather_finish(fwd, fwd['token'])
                return w, small_all, start_group('l0b', w['w_in'])
            gathered[l] = _gather_finish(on_d2d.pop(f"l{l}"), after)[0]
            tok = start_group(f"l{l + 1}", gathered[l]['w_in']) if l + 1 < DEPTH else None
            return {nm: w for nm, w in gathered[l].items() if nm not in FFN_W}, None, tok
        if where == 'mid' and l == 0:
            on_d2d['l0b'] = _gather_forward(on_ici.pop('l0b'), after)
            return start_group('l1', on_d2d['l0b']['token'])
        if where == 'ffn':
            if l == 0:
                return (_gather_finish(on_d2d.pop('l0b'), after)[0],)
            return ({nm: gathered[l][nm] for nm in FFN_W},)
        if where == 'post' and l + 1 < DEPTH:
            on_d2d[f"l{l + 1}"] = _gather_forward(on_ici.pop(f"l{l + 1}"), after)
        return None

    hook_of = {'ffn': 0, 'mix': 1, 'end': 2}
    active, reduced = [], {l: {} for l in range(DEPTH)}
    advance = {'mid': lambda st, after: _reduce_mid(st, after, sc), 'late': lambda st, after: _reduce_late(st, after, sc)}

    def run_hook(idx, after):
        toks = []
        for grp in list(active):
            while grp['plan'] and grp['plan'][0][1] <= idx:
                step = grp['plan'].pop(0)[0]
                if step == 'finish':
                    reduced[grp['layer']].update(_reduce_finish(grp['st'], after))
                    active.remove(grp)
                else:
                    grp['st'] = advance[step](grp['st'], after)
                    toks.append(grp['st']['token'][0, 0])
        return toks

    def on_backward(where, l, after, grads):
        idx = 3 * (DEPTH - 1 - l) + hook_of[where]
        toks = run_hook(idx, after)
        if where in ('ffn', 'end'):
            names = FFN_W if where == 'ffn' else tuple(nm for nm in grads if nm not in FFN_W)
            st = _reduce_start(f"l{l}_{where}_grads", names, {nm: grads[nm] for nm in names})
            plan = [('mid', idx + 1), ('late', idx + 3), ('finish', idx + 4)] if where == 'ffn' else \
                   [('mid', idx + 1), ('late', idx + 2), ('finish', idx + 3)]
            active.append(dict(layer=l, st=st, plan=plan))
            toks.append(st['token'][0, 0])
        return sum(toks) if toks else None

    loss, gx, gsmall = _local_step(x, mem, positions, on_forward, on_backward, mem_norm, w_pool, kv_norm, loss_target)
    loss = lax.psum(loss[0, 0], ("x", "y", "c"))

    vec = jnp.concatenate([gsmall[nm].reshape(-1) for nm in SMALL_ORDER])
    vec = jnp.pad(vec, (0, SMALL_VEC_ROWS * 128 - vec.shape[0])).reshape(SMALL_VEC_ROWS, 128)
    small_st = _small_gather_start(vec, sc)
    outs = {nm: None for nm in LAYER_W}

    def adamw_layers(layers):
        for l in layers:
            for nm in LAYER_W:
                outs[nm] = _adamw_layer(nm, l, weights[nm], reduced[l][nm], moms[nm], vels[nm], outs[nm])

    last = 3 * DEPTH
    run_hook(last, small_st['token'])
    adamw_layers(range(DEPTH - 1, 0, -1))
    run_hook(last + 1, outs[LAYER_W[-1]][0])
    tot = _sum8(_small_gather_finish(small_st, outs[LAYER_W[0]][0]), name="sum_small_grads", tr=512)
    run_hook(last + 2, tot)
    assert not active
    adamw_layers([0])
    tot = tot.reshape(-1)
    grads, off = {}, 0
    for nm in SMALL_ORDER:
        shape = (DEPTH, 4, D_MODEL) if nm == 'norm_gains' else (N_A, MAIN_W) if nm == 'pool_scale' else weights[nm].shape
        n = 1
        for dim in shape:
            n *= dim
        grads[nm] = tot[off:off + n].reshape(shape)
        off += n
    grads['norm_gains'] = lax.dynamic_slice(grads['norm_gains'], (0, 0, s * 256), (DEPTH, 4, 256))
    grads['pool_scale'] = lax.dynamic_slice(grads['pool_scale'], (0, s * POOL_GROUP), (N_A, POOL_GROUP))
    grads['w_kv'] = reduced[N_A]['w_kv'].reshape(w_kv.shape)

    order = ('norm_gains', 'mem_norm', 'w_in', 'w_mem_kv', 'w_out', 'w_pool', 'pool_scale', 'kv_norm', 'w_kv',
             'w_gate_up', 'w_down')
    deltas, new_m, new_v = {}, {}, {}
    for nm in order:
        if nm in LAYER_W:
            deltas[nm], new_m[nm], new_v[nm], grads[nm] = outs[nm]
        else:
            deltas[nm], new_m[nm], new_v[nm] = _adamw(weights[nm], grads[nm], moms[nm], vels[nm], name="adamw_" + nm)
    return (loss, gx.reshape(x.shape), *[grads[nm] for nm in order], *[deltas[nm] for nm in order],
            *[new_m[nm] for nm in order], *[new_v[nm] for nm in order])
```

```python
import functools

import jax
import jax.numpy as jnp
from jax import lax
from jax.experimental import pallas as pl
from jax.experimental.pallas import tpu as pltpu

F32 = jnp.float32
BF = jnp.bfloat16

D_MODEL = 1024
DEPTH = 4
N_A = 2
HEAD_DIM = 64
MEM_W = 256
MAIN_W = 768
D_FF = 2816
N_MEM = 256
POOL_WINDOWS = (2, 4, 8, 16)
POOL_GROUP = 192
DIL = (1, 4, 16)
STEPS = 128
ROPE_THETA = 10000.0
EPS = 1e-6
SCALE = HEAD_DIM ** -0.5
NEG = -1e30

ADAM_LR = 0.001
ADAM_B1 = 0.9
ADAM_B2 = 0.999
ADAM_EPS = 1e-08
ADAM_WD = 0.01
ADAM_STEP = 10

VMEM_LIMIT = 48 * 1024 * 1024
MESH = pl.DeviceIdType.MESH


def _cp(sem):
    return pltpu.CompilerParams(dimension_semantics=sem, vmem_limit_bytes=VMEM_LIMIT)


def _mm(a, b, *, name, ta=False, tb=False, tm=1024, tn=512, tk=1024, b_layer=None, b_offsets=(0,),
        extras=(), epilogue=None, out_dtypes=(F32,), out_n=None, stack=None):
    a_pair = isinstance(a, (tuple, list))
    b_pair = isinstance(b, (tuple, list))
    a0 = a[0] if a_pair else a
    b0 = b[0] if b_pair else b
    a_rows, a_cols = a0.shape
    if a_pair:
        a_cols *= 2
    b_rows, b_cols = b0.shape[-2:]
    if b_pair:
        b_cols *= 2
    M, K = (a_cols, a_rows) if ta else (a_rows, a_cols)
    N = b_rows if tb else b_cols
    if out_n is not None:
        N = out_n
    tm, tn, tk = min(tm, M), min(tn, N), min(tk, K)
    assert M % tm == 0 and N % tn == 0 and K % tk == 0, (name, M, N, K, tm, tn, tk)
    nk = K // tk
    n_acc = len(b_offsets)

    if a_pair:
        a_half = (a0.shape[1] // (tm if ta else tk))
    if b_pair:
        b_half = (b0.shape[1] // (tk if tb else tn))

    def a_map(sel):
        def f(i, j, k):
            r, c = (k, i) if ta else (i, k)
            if a_pair:
                c = jnp.clip(c - sel * a_half, 0, a_half - 1)
            return (r, c)
        return f

    def b_map(sel, off):
        def f(i, j, k):
            r, c = (j + off, k) if tb else (k, j + off)
            if b_pair:
                c = jnp.clip(c - sel * b_half, 0, b_half - 1)
            if b_layer is not None:
                return (b_layer, r, c)
            return (r, c)
        return f

    a_blk = (tk, tm) if ta else (tm, tk)
    b_blk = (tn, tk) if tb else (tk, tn)
    if b_layer is not None:
        b_blk = (None,) + b_blk
    in_specs, operands = [], []
    for sel in range(2 if a_pair else 1):
        in_specs.append(pl.BlockSpec(a_blk, a_map(sel)))
        operands.append(a[sel] if a_pair else a)
    n_a = len(operands)
    for off in b_offsets:
        for sel in range(2 if b_pair else 1):
            in_specs.append(pl.BlockSpec(b_blk, b_map(sel, off)))
            operands.append(b[sel] if b_pair else b)
    n_b = len(operands) - n_a
    for arr, kind in extras:
        if kind == 'tile':
            in_specs.append(pl.BlockSpec((tm, tn), lambda i, j, k: (i, j)))
        elif kind == 'row':
            in_specs.append(pl.BlockSpec((tm, 1), lambda i, j, k: (i, 0)))
        else:
            in_specs.append(pl.BlockSpec((1, tn), lambda i, j, k: (0, j)))
        operands.append(arr)
    n_e = len(extras)
    n_o = len(out_dtypes)
    dims = (((0,) if ta else (1,), (1,) if tb else (0,)), ((), ()))

    def body(*refs):
        a_refs = refs[:n_a]
        b_refs = refs[n_a:n_a + n_b]
        e_refs = refs[n_a + n_b:n_a + n_b + n_e]
        n_in = n_a + n_b + n_e + (1 if stack is not None else 0)
        o_refs = refs[n_in:n_in + n_o]
        acc_refs = refs[n_in + n_o:]
        i, j, k = pl.program_id(0), pl.program_id(1), pl.program_id(2)
        if a_pair:
            cidx = i if ta else k
            av = jnp.where(cidx < a_half, a_refs[0][...], a_refs[1][...])
        else:
            av = a_refs[0][...]
        av = av.astype(BF)
        prods = []
        for q in range(n_acc):
            if b_pair:
                cidx = (k if tb else j) + b_offsets[q]
                bv = jnp.where(cidx < b_half, b_refs[2 * q][...], b_refs[2 * q + 1][...])
            else:
                bv = b_refs[q][...]
            prods.append(lax.dot_general(av, bv.astype(BF), dims, preferred_element_type=F32))

        def finish(accs):
            outs = epilogue(accs, *[r[...] for r in e_refs]) if epilogue is not None else accs
            for o_ref, o in zip(o_refs, outs):
                o_ref[...] = o.astype(o_ref.dtype)

        if nk == 1:
            finish(prods)
        else:
            @pl.when(k == 0)
            def _():
                for r, p in zip(acc_refs, prods):
                    r[...] = p

            @pl.when(k > 0)
            def _():
                for r, p in zip(acc_refs, prods):
                    r[...] += p

            @pl.when(k == nk - 1)
            def _():
                finish([r[...] for r in acc_refs])

    if stack is not None:
        buf, layer = stack
        assert n_o == 1 and buf.shape[1:] == (M, N)
        return pl.pallas_call(
            body, name=name,
            grid=(M // tm, N // tn, nk),
            in_specs=in_specs + [pl.BlockSpec(memory_space=pl.ANY)],
            out_specs=[pl.BlockSpec((None, tm, tn), lambda i, j, k: (layer, i, j))],
            out_shape=[jax.ShapeDtypeStruct(buf.shape, buf.dtype)],
            scratch_shapes=[pltpu.VMEM((tm, tn), F32) for _ in range(n_acc if nk > 1 else 0)],
            input_output_aliases={len(operands): 0},
            compiler_params=_cp(("parallel", "parallel", "arbitrary")),
        )(*operands, buf)[0]
    return pl.pallas_call(
        body, name=name,
        grid=(M // tm, N // tn, nk),
        in_specs=in_specs,
        out_specs=[pl.BlockSpec((tm, tn), lambda i, j, k: (i, j)) for _ in range(n_o)],
        out_shape=[jax.ShapeDtypeStruct((M, N), dt) for dt in out_dtypes],
        scratch_shapes=[pltpu.VMEM((tm, tn), F32) for _ in range(n_acc if nk > 1 else 0)],
        compiler_params=_cp(("parallel", "parallel", "arbitrary")),
    )(*operands)


def _norm_fwd(x, g, *, name, res=None, out_dtype=F32, tm=512):
    T, Dm = x.shape
    has_res = res is not None

    def body(*refs):
        if has_res:
            x_ref, g_ref, r_ref, y_ref, s_ref = refs
        else:
            x_ref, g_ref, y_ref, s_ref = refs
        xv = x_ref[...]
        rstd = lax.rsqrt(jnp.mean(xv * xv, axis=-1, keepdims=True) + EPS)
        y = xv * rstd * g_ref[...]
        if has_res:
            y = r_ref[...] + y
        y_ref[...] = y.astype(y_ref.dtype)
        s_ref[...] = rstd

    row = pl.BlockSpec((tm, Dm), lambda i: (i, 0))
    in_specs = [row, pl.BlockSpec((1, Dm), lambda i: (0, 0))] + ([row] if has_res else [])
    ops = [x, _in_hbm(g.reshape(1, Dm))] + ([res] if has_res else [])
    return pl.pallas_call(
        body, name=name, grid=(T // tm,), in_specs=in_specs,
        out_specs=[row, pl.BlockSpec((tm, 1), lambda i: (i, 0))],
        out_shape=[jax.ShapeDtypeStruct((T, Dm), out_dtype), jax.ShapeDtypeStruct((T, 1), F32)],
        compiler_params=_cp(("parallel",)),
    )(*ops)


def _norm_bwd(dout, x, rstd, g, *, name, add=None, out_dtype=F32, tm=512):
    T, Dm = x.shape
    has_add = add is not None
    nt = T // tm

    def body(*refs):
        if has_add:
            do_ref, x_ref, s_ref, g_ref, a_ref, dx_ref, dg_ref, acc = refs
        else:
            do_ref, x_ref, s_ref, g_ref, dx_ref, dg_ref, acc = refs
        i = pl.program_id(0)
        do = do_ref[...].astype(F32)
        xh = x_ref[...] * s_ref[...]
        gd = do * g_ref[...]
        dx = s_ref[...] * (gd - xh * jnp.mean(gd * xh, axis=-1, keepdims=True))
        if has_add:
            dx = dx + a_ref[...].astype(F32)
        dx_ref[...] = dx.astype(dx_ref.dtype)
        part = jnp.sum((do * xh).reshape(tm // 8, 8, Dm), axis=0)

        @pl.when(i == 0)
        def _():
            acc[...] = part

        @pl.when(i > 0)
        def _():
            acc[...] += part

        @pl.when(i == nt - 1)
        def _():
            dg_ref[...] = jnp.sum(acc[...], axis=0, keepdims=True)

    row = pl.BlockSpec((tm, Dm), lambda i: (i, 0))
    in_specs = [row, row, pl.BlockSpec((tm, 1), lambda i: (i, 0)), pl.BlockSpec((1, Dm), lambda i: (0, 0))]
    ops = [dout, x, _in_hbm(rstd), _in_hbm(g.reshape(1, Dm))]
    if has_add:
        in_specs.append(row)
        ops.append(add)
    return pl.pallas_call(
        body, name=name, grid=(nt,), in_specs=in_specs,
        out_specs=[row, pl.BlockSpec((1, Dm), lambda i: (0, 0))],
        out_shape=[jax.ShapeDtypeStruct((T, Dm), out_dtype), jax.ShapeDtypeStruct((1, Dm), F32)],
        scratch_shapes=[pltpu.VMEM((8, Dm), F32)],
        compiler_params=_cp(("arbitrary",)),
    )(*ops)


def _swiglu_fwd_epilogue(accs):
    g, u = accs
    return g, u, g * jax.nn.sigmoid(g) * u


def _swiglu_bwd_epilogue(accs, g, u):
    da = accs[0]
    g = g.astype(F32)
    u = u.astype(F32)
    sig = jax.nn.sigmoid(g)
    return da * u * (sig * (1.0 + g * (1.0 - sig))), da * (g * sig)


def _rope_tables(pos, *, name, tm=1024):
    T = pos.shape[0]
    half = HEAD_DIM // 2
    freqs = ROPE_THETA ** (-jnp.arange(half, dtype=F32) / half)
    freqs = jnp.tile(freqs, 4).reshape(1, 128)

    def body(p_ref, f_ref, c_ref, s_ref):
        ang = p_ref[...].astype(F32) * f_ref[...]
        lane = lax.broadcasted_iota(jnp.int32, ang.shape, 1)
        c_ref[...] = jnp.cos(ang)
        s_ref[...] = jnp.where(lane % HEAD_DIM < half, -1.0, 1.0) * jnp.sin(ang)

    tab = pl.BlockSpec((tm, 128), lambda i: (i, 0))
    return pl.pallas_call(
        body, name=name, grid=(T // tm,),
        in_specs=[pl.BlockSpec((tm, 1), lambda i: (i, 0)), pl.BlockSpec((1, 128), lambda i: (0, 0))],
        out_specs=[tab, tab],
        out_shape=[jax.ShapeDtypeStruct((T, 128), F32)] * 2,
        compiler_params=_cp(("parallel",)),
    )(pos, freqs)


def _rot(x, cos, sin, sign):
    W = x.shape[1]
    half = HEAD_DIM // 2
    reps = W // 128
    c = jnp.concatenate([cos] * reps, axis=1) if reps > 1 else cos
    s = jnp.concatenate([sin] * reps, axis=1) if reps > 1 else sin
    lane = lax.broadcasted_iota(jnp.int32, x.shape, 1)
    swapped = jnp.where(lane % HEAD_DIM < half, pltpu.roll(x, W - half, axis=1), pltpu.roll(x, half, axis=1))
    return x * c + (sign * s) * swapped


def _rope_apply(x, cos, sin, *, name, sign=1.0, width=MAIN_W, passthrough=False, out_dtype=BF, alias=None,
                out_cols=None, tm=512):
    T = x.shape[0]

    def body(*refs):
        if passthrough:
            x_ref, v_ref, c_ref, s_ref, o_ref, ov_ref = refs
            ov_ref[...] = v_ref[...].astype(ov_ref.dtype)
        elif alias is not None:
            x_ref, c_ref, s_ref, _, o_ref = refs
        else:
            x_ref, c_ref, s_ref, o_ref = refs
        o_ref[...] = _rot(x_ref[...].astype(F32), c_ref[...], s_ref[...], sign).astype(o_ref.dtype)

    blk0 = pl.BlockSpec((tm, width), lambda i: (i, 0))
    blk1 = pl.BlockSpec((tm, width), lambda i: (i, 1))
    tab = pl.BlockSpec((tm, 128), lambda i: (i, 0))
    if passthrough:
        return pl.pallas_call(
            body, name=name, grid=(T // tm,), in_specs=[blk0, blk1, tab, tab], out_specs=[blk0, blk0],
            out_shape=[jax.ShapeDtypeStruct((T, width), out_dtype)] * 2,
            compiler_params=_cp(("parallel",)),
        )(x, x, cos, sin)
    if alias is not None:
        return pl.pallas_call(
            body, name=name, grid=(T // tm,),
            in_specs=[blk0, tab, tab, pl.BlockSpec(memory_space=pl.ANY)], out_specs=blk0,
            out_shape=jax.ShapeDtypeStruct(alias.shape, alias.dtype),
            input_output_aliases={3: 0},
            compiler_params=_cp(("parallel",)),
        )(x, cos, sin, alias)
    return pl.pallas_call(
        body, name=name, grid=(T // tm,), in_specs=[blk0, tab, tab], out_specs=blk0,
        out_shape=jax.ShapeDtypeStruct((T, width), out_dtype),
        compiler_params=_cp(("parallel",)),
    )(x, cos, sin)


POOL_T = 256
POOL_HALO = 16


def _pool_lane_window(shape):
    lane = lax.broadcasted_iota(jnp.int32, shape, 1)
    w = jnp.full(shape, POOL_WINDOWS[0], jnp.int32)
    for gi in range(1, len(POOL_WINDOWS)):
        w = jnp.where(lane >= gi * POOL_GROUP, POOL_WINDOWS[gi], w)
    return w


def _pool_fwd(z, wbd, scale, B, S, *, name):
    T = z.shape[0]
    nt = S // POOL_T
    hb = POOL_T // POOL_HALO

    def body(z_ref, h_ref, w_ref, sc_ref, y_ref, p_ref, ext):
        i = pl.program_id(1)
        u = z_ref[...]
        ext[pl.ds(POOL_HALO, POOL_T), :] = u
        ext[pl.ds(0, POOL_HALO), :] = jnp.where(i > 0, h_ref[...], 0.0)
        win = _pool_lane_window((POOL_T, MAIN_W))
        acc = u
        for k in range(1, POOL_HALO):
            acc = acc + jnp.where(k < win, ext[pl.ds(POOL_HALO - k, POOL_T), :], 0.0)
        t = i * POOL_T + lax.broadcasted_iota(jnp.int32, (POOL_T, MAIN_W), 0)
        cnt = jnp.minimum(t + 1, win).astype(F32)
        p = (acc / cnt - u).astype(BF)
        p_ref[...] = p
        y = jnp.dot(p, w_ref[...], preferred_element_type=F32) * sc_ref[...]
        y_ref[...] = y.astype(y_ref.dtype)

    return pl.pallas_call(
        body, name=name, grid=(B, nt),
        in_specs=[pl.BlockSpec((POOL_T, MAIN_W), lambda b, i: (b * nt + i, 0)),
                  pl.BlockSpec((POOL_HALO, MAIN_W), lambda b, i: (jnp.maximum((b * nt + i) * hb - 1, 0), 0)),
                  pl.BlockSpec((MAIN_W, MAIN_W), lambda b, i: (0, 0)),
                  pl.BlockSpec((1, MAIN_W), lambda b, i: (0, 0))],
        out_specs=[pl.BlockSpec((POOL_T, MAIN_W), lambda b, i: (b * nt + i, 0)),
                   pl.BlockSpec((POOL_T, MAIN_W), lambda b, i: (b * nt + i, 0))],
        out_shape=[jax.ShapeDtypeStruct((T, D_MODEL), BF), jax.ShapeDtypeStruct((T, MAIN_W), BF)],
        scratch_shapes=[pltpu.VMEM((POOL_T + POOL_HALO, MAIN_W), F32)],
        compiler_params=_cp(("parallel", "parallel")),
    )(z, z, wbd, scale.reshape(1, MAIN_W))


def _pool_bwd(dy, p, wbd, scale, dz_alias, B, S, *, name):
    T = dy.shape[0]
    nt = S // POOL_T
    hb = POOL_T // POOL_HALO
    last_halo = T // POOL_HALO - 1
    R = POOL_T + POOL_HALO

    def body(dy_ref, dyn_ref, p_ref, pn_ref, w_ref, sc_ref, _, dz_ref, dw_ref, ds_ref, ext, dw_acc, ds_acc):
        b, i = pl.program_id(0), pl.program_id(1)
        first = jnp.logical_and(b == 0, i == 0)
        dyv = dy_ref[...]
        pv = p_ref[...]
        sc = sc_ref[...]
        w = w_ref[...]
        pw = jnp.dot(pv, w, preferred_element_type=F32)
        ds_part = jnp.sum((dyv * pw).reshape(POOL_T // 8, 8, MAIN_W), axis=0)
        dpw = (dyv * sc).astype(BF)
        dw_part = lax.dot_general(pv, dpw, (((0,), (0,)), ((), ())), preferred_element_type=F32)

        @pl.when(first)
        def _():
            dw_acc[...] = dw_part
            ds_acc[...] = ds_part

        @pl.when(jnp.logical_not(first))
        def _():
            dw_acc[...] += dw_part
            ds_acc[...] += ds_part

        @pl.when(jnp.logical_and(b == pl.num_programs(0) - 1, i == nt - 1))
        def _():
            dw_ref[...] = dw_acc[...]
            ds_ref[...] = jnp.sum(ds_acc[...], axis=0, keepdims=True)

        dp = lax.dot_general(dpw, w, (((1,), (1,)), ((), ())), preferred_element_type=F32)
        dpn = lax.dot_general((dyn_ref[...] * sc).astype(BF), w, (((1,), (1,)), ((), ())), preferred_element_type=F32)
        win = _pool_lane_window((POOL_T, MAIN_W))
        win_n = _pool_lane_window((POOL_HALO, MAIN_W))
        t = i * POOL_T + lax.broadcasted_iota(jnp.int32, (POOL_T, MAIN_W), 0)
        tn = (i + 1) * POOL_T + lax.broadcasted_iota(jnp.int32, (POOL_HALO, MAIN_W), 0)
        ext[pl.ds(0, POOL_T), :] = dp / jnp.minimum(t + 1, win).astype(F32)
        ext[pl.ds(POOL_T, POOL_HALO), :] = jnp.where(i < nt - 1, dpn / jnp.minimum(tn + 1, win_n).astype(F32), 0.0)
        acc = -dp
        for k in range(POOL_HALO):
            acc = acc + jnp.where(k < win, ext[pl.ds(k, POOL_T), :], 0.0)
        dz_ref[...] = acc.astype(dz_ref.dtype)

    cur = lambda b, i: (b * nt + i, 0)
    nxt = lambda b, i: (jnp.minimum((b * nt + i + 1) * hb, last_halo), 0)
    return pl.pallas_call(
        body, name=name, grid=(B, nt),
        in_specs=[pl.BlockSpec((POOL_T, MAIN_W), cur), pl.BlockSpec((POOL_HALO, MAIN_W), nxt),
                  pl.BlockSpec((POOL_T, MAIN_W), cur), pl.BlockSpec((POOL_HALO, MAIN_W), nxt),
                  pl.BlockSpec((MAIN_W, MAIN_W), lambda b, i: (0, 0)),
                  pl.BlockSpec((1, MAIN_W), lambda b, i: (0, 0)),
                  pl.BlockSpec(memory_space=pl.ANY)],
        out_specs=[pl.BlockSpec((POOL_T, MAIN_W), cur),
                   pl.BlockSpec((MAIN_W, MAIN_W), lambda b, i: (0, 0)),
                   pl.BlockSpec((1, MAIN_W), lambda b, i: (0, 0))],
        out_shape=[jax.ShapeDtypeStruct(dz_alias.shape, dz_alias.dtype),
                   jax.ShapeDtypeStruct((MAIN_W, MAIN_W), F32), jax.ShapeDtypeStruct((1, MAIN_W), F32)],
        scratch_shapes=[pltpu.VMEM((R, MAIN_W), F32), pltpu.VMEM((MAIN_W, MAIN_W), F32), pltpu.VMEM((8, MAIN_W), F32)],
        input_output_aliases={6: 0},
        compiler_params=_cp(("arbitrary", "arbitrary")),
    )(dy, dy, p, p, wbd, scale.reshape(1, MAIN_W), dz_alias)


def _head_masks(shape):
    lane = lax.broadcasted_iota(jnp.int32, shape, 1)
    return [(lane // HEAD_DIM) == h for h in range(shape[1] // HEAD_DIM)]


def _row_of(bcast, mask):
    return jnp.max(jnp.where(mask, bcast, -jnp.inf), axis=-1, keepdims=True)


MEM_TQ = 512


def _memattn_fwd(z, kv, y_alias, B, S, *, name):
    T = z.shape[0]
    nt = S // MEM_TQ

    def body(q_ref, k_ref, v_ref, _, y_ref, l_ref):
        q = q_ref[...]
        k = k_ref[...]
        v = v_ref[...]
        masks = _head_masks(q.shape)
        o = jnp.zeros(q.shape, F32)
        lse_b = jnp.zeros(q.shape, F32)
        for m in masks:
            qm = jnp.where(m, q, 0.0).astype(BF)
            s = lax.dot_general(qm, k, (((1,), (1,)), ((), ())), preferred_element_type=F32) * SCALE
            mx = jnp.max(s, axis=-1, keepdims=True)
            e = jnp.exp(s - mx)
            l = jnp.sum(e, axis=-1, keepdims=True)
            p = (e / l).astype(BF)
            o = o + jnp.where(m, jnp.dot(p, v, preferred_element_type=F32), 0.0)
            lse_b = lse_b + jnp.where(m, mx + jnp.log(l), 0.0)
        y_ref[...] = o.astype(y_ref.dtype)
        l_ref[...] = lse_b

    qblk = pl.BlockSpec((MEM_TQ, MEM_W), lambda b, i: (b * nt + i, 3))
    return pl.pallas_call(
        body, name=name, grid=(B, nt),
        in_specs=[qblk, pl.BlockSpec((N_MEM, MEM_W), lambda b, i: (b, 0)), pl.BlockSpec((N_MEM, MEM_W), lambda b, i: (b, 1)),
                  pl.BlockSpec(memory_space=pl.ANY)],
        out_specs=[qblk, pl.BlockSpec((MEM_TQ, MEM_W), lambda b, i: (b * nt + i, 0))],
        out_shape=[jax.ShapeDtypeStruct(y_alias.shape, y_alias.dtype), jax.ShapeDtypeStruct((T, MEM_W), F32)],
        input_output_aliases={3: 0},
        compiler_params=_cp(("parallel", "parallel")),
    )(z, kv, kv, y_alias)


def _memattn_bwd(dy, z, kv, lse, dz_alias, B, S, *, name):
    nt = S // MEM_TQ

    def body(do_ref, q_ref, k_ref, v_ref, l_ref, _, dz_ref, dk_ref, dv_ref, dk_acc, dv_acc):
        i = pl.program_id(1)
        do = do_ref[...]
        q = q_ref[...]
        k = k_ref[...]
        v = v_ref[...]
        lse_b = l_ref[...]
        masks = _head_masks(q.shape)
        dq = jnp.zeros(q.shape, F32)
        dk = jnp.zeros(k.shape, F32)
        dv = jnp.zeros(v.shape, F32)
        for m in masks:
            qm = jnp.where(m, q, 0.0).astype(BF)
            dom = jnp.where(m, do, 0.0).astype(BF)
            s = lax.dot_general(qm, k, (((1,), (1,)), ((), ())), preferred_element_type=F32) * SCALE
            p = jnp.exp(s - _row_of(lse_b, m))
            dp = lax.dot_general(dom, v, (((1,), (1,)), ((), ())), preferred_element_type=F32)
            delta = jnp.sum(p * dp, axis=-1, keepdims=True)
            ds = (p * (dp - delta) * SCALE).astype(BF)
            pb = p.astype(BF)
            dv = dv + jnp.where(m[:N_MEM], lax.dot_general(pb, dom, (((0,), (0,)), ((), ())), preferred_element_type=F32), 0.0)
            dk = dk + jnp.where(m[:N_MEM], lax.dot_general(ds, qm, (((0,), (0,)), ((), ())), preferred_element_type=F32), 0.0)
            dq = dq + jnp.where(m, jnp.dot(ds, k, preferred_element_type=F32), 0.0)
        dz_ref[...] = dq.astype(dz_ref.dtype)

        @pl.when(i == 0)
        def _():
            dk_acc[...] = dk
            dv_acc[...] = dv

        @pl.when(i > 0)
        def _():
            dk_acc[...] += dk
            dv_acc[...] += dv

        @pl.when(i == nt - 1)
        def _():
            dk_ref[...] = dk_acc[...]
            dv_ref[...] = dv_acc[...]

    qblk = pl.BlockSpec((MEM_TQ, MEM_W), lambda b, i: (b * nt + i, 3))
    kblk = pl.BlockSpec((N_MEM, MEM_W), lambda b, i: (b, 0))
    return pl.pallas_call(
        body, name=name, grid=(B, nt),
        in_specs=[qblk, qblk, kblk, pl.BlockSpec((N_MEM, MEM_W), lambda b, i: (b, 1)),
                  pl.BlockSpec((MEM_TQ, MEM_W), lambda b, i: (b * nt + i, 0)), pl.BlockSpec(memory_space=pl.ANY)],
        out_specs=[qblk, kblk, kblk],
        out_shape=[jax.ShapeDtypeStruct(dz_alias.shape, dz_alias.dtype),
                   jax.ShapeDtypeStruct((B * N_MEM, MEM_W), F32), jax.ShapeDtypeStruct((B * N_MEM, MEM_W), F32)],
        scratch_shapes=[pltpu.VMEM((N_MEM, MEM_W), F32), pltpu.VMEM((N_MEM, MEM_W), F32)],
        input_output_aliases={5: 0},
        compiler_params=_cp(("parallel", "arbitrary")),
    )(dy, z, kv, kv, lse, dz_alias)


def _dil_scores(qm, kp, kc, n):
    qi = lax.broadcasted_iota(jnp.int32, (STEPS, STEPS), 0)
    kj = lax.broadcasted_iota(jnp.int32, (STEPS, STEPS), 1)
    sc = lax.dot_general(qm, kc, (((1,), (1,)), ((), ())), preferred_element_type=F32) * SCALE
    sc = jnp.where(kj <= qi, sc, NEG)
    if kp is None:
        return None, sc
    sp = lax.dot_general(qm, kp, (((1,), (1,)), ((), ())), preferred_element_type=F32) * SCALE
    sp = jnp.where(jnp.logical_and(kj >= qi, n > 0), sp, NEG)
    return sp, sc


def _dil_specs(g, d, nb):
    chunk = STEPS * d
    cur = pl.BlockSpec((chunk, 128), lambda b, n, hf: (b * nb + n, g * 2 + hf))
    prev = pl.BlockSpec((chunk, 128), lambda b, n, hf: (b * nb + jnp.maximum(n - 1, 0), g * 2 + hf))
    return cur, prev


def _dil_rows(r, d):
    return pl.ds(r, STEPS, stride=d) if d > 1 else slice(None)


def _dil_loop(d, fn):
    if d <= 4:
        for r in range(d):
            fn(r)
    else:
        lax.fori_loop(0, d, lambda r, carry: (fn(r), carry)[1], 0)


def _dil_fwd_group(g, q, k, v, o_alias, l_alias, B, S, *, name):
    d = DIL[g]
    nb = S // (STEPS * d)
    has_prev = nb > 1

    def body(*refs):
        if has_prev:
            q_ref, kp_ref, kc_ref, vp_ref, vc_ref, _, __, o_ref, l_ref = refs
        else:
            q_ref, kc_ref, vc_ref, _, __, o_ref, l_ref = refs
        n = pl.program_id(1)

        def residue(r):
            rows = _dil_rows(r, d)
            q = q_ref[rows, :]
            kc, vc = kc_ref[rows, :].astype(BF), vc_ref[rows, :].astype(BF)
            kp = kp_ref[rows, :].astype(BF) if has_prev else None
            vp = vp_ref[rows, :].astype(BF) if has_prev else None
            o = jnp.zeros(q.shape, F32)
            lse_b = jnp.zeros(q.shape, F32)
            for m in _head_masks(q.shape):
                qm = jnp.where(m, q, 0.0).astype(BF)
                sp, sc = _dil_scores(qm, kp, kc, n)
                mx = jnp.max(sc, axis=-1, keepdims=True)
                if has_prev:
                    mx = jnp.maximum(mx, jnp.max(sp, axis=-1, keepdims=True))
                l = jnp.sum(jnp.exp(sc - mx), axis=-1, keepdims=True)
                if has_prev:
                    l = l + jnp.sum(jnp.exp(sp - mx), axis=-1, keepdims=True)
                lse = mx + jnp.log(l)
                oh = jnp.dot(jnp.exp(sc - lse).astype(BF), vc, preferred_element_type=F32)
                if has_prev:
                    oh = oh + jnp.dot(jnp.exp(sp - lse).astype(BF), vp, preferred_element_type=F32)
                o = o + jnp.where(m, oh, 0.0)
                lse_b = lse_b + jnp.where(m, lse, 0.0)
            o_ref[rows, :] = o
            l_ref[rows, :] = lse_b

        _dil_loop(d, residue)

    cur, prev = _dil_specs(g, d, nb)
    anyspec = pl.BlockSpec(memory_space=pl.ANY)
    if has_prev:
        in_specs, ops = [cur, prev, cur, prev, cur], [q, k, k, v, v]
    else:
        in_specs, ops = [cur, cur, cur], [q, k, v]
    n_in = len(ops)
    o, l = pl.pallas_call(
        body, name=name, grid=(B, nb, 2),
        in_specs=in_specs + [anyspec, anyspec],
        out_specs=[cur, cur],
        out_shape=[jax.ShapeDtypeStruct(q.shape, F32)] * 2,
        input_output_aliases={n_in: 0, n_in + 1: 1},
        compiler_params=_cp(("parallel", "parallel", "parallel")),
    )(*ops, o_alias, l_alias)
    return o, l


def _dil_bwd_group(g, q, k, v, do, cb, lse, aliases, B, S, *, name):
    d = DIL[g]
    nb = S // (STEPS * d)
    has_prev = nb > 1
    n_out = 5 if has_prev else 3

    def body(*refs):
        if has_prev:
            q_ref, kp_ref, kc_ref, vp_ref, vc_ref, do_ref, c_ref, l_ref = refs[:8]
            dq_ref, dkc_ref, dvc_ref, dkp_ref, dvp_ref = refs[8 + n_out:]
        else:
            q_ref, kc_ref, vc_ref, do_ref, c_ref, l_ref = refs[:6]
            dq_ref, dkc_ref, dvc_ref = refs[6 + n_out:]
        n = pl.program_id(1)
        tdot = lambda a, b: lax.dot_general(a, b, (((0,), (0,)), ((), ())), preferred_element_type=F32)
        ndot = lambda a, b: lax.dot_general(a, b, (((1,), (1,)), ((), ())), preferred_element_type=F32)

        def residue(r):
            rows = _dil_rows(r, d)
            q = q_ref[rows, :]
            kc, vc = kc_ref[rows, :].astype(BF), vc_ref[rows, :].astype(BF)
            kp = kp_ref[rows, :].astype(BF) if has_prev else None
            vp = vp_ref[rows, :].astype(BF) if has_prev else None
            do = do_ref[rows, :]
            cbv = c_ref[rows, :]
            lse_b = l_ref[rows, :]
            z = jnp.zeros(q.shape, F32)
            dq, dkc, dkp, dvc, dvp = z, z, z, z, z
            for m in _head_masks(q.shape):
                qm = jnp.where(m, q, 0.0).astype(BF)
                dom = jnp.where(m, do, 0.0).astype(BF)
                sp, sc = _dil_scores(qm, kp, kc, n)
                lse = _row_of(lse_b, m)
                c = _row_of(cbv, m)
                pc = jnp.exp(sc - lse)
                dsc = (pc * (ndot(dom, vc) + c) * SCALE).astype(BF)
                dqh = jnp.dot(dsc, kc, preferred_element_type=F32)
                dkc = dkc + jnp.where(m, tdot(dsc, qm), 0.0)
                dvc = dvc + jnp.where(m, tdot(pc.astype(BF), dom), 0.0)
                if has_prev:
                    pp = jnp.exp(sp - lse)
                    dsp = (pp * (ndot(dom, vp) + c) * SCALE).astype(BF)
                    dqh = dqh + jnp.dot(dsp, kp, preferred_element_type=F32)
                    dkp = dkp + jnp.where(m, tdot(dsp, qm), 0.0)
                    dvp = dvp + jnp.where(m, tdot(pp.astype(BF), dom), 0.0)
                dq = dq + jnp.where(m, dqh, 0.0)
            dq_ref[rows, :] = dq
            dkc_ref[rows, :] = dkc
            dvc_ref[rows, :] = dvc
            if has_prev:
                dkp_ref[rows, :] = dkp
                dvp_ref[rows, :] = dvp

        _dil_loop(d, residue)

    cur, prev = _dil_specs(g, d, nb)
    anyspec = pl.BlockSpec(memory_space=pl.ANY)
    dq_a, dkc_a, dkp_a, dvc_a, dvp_a = aliases
    if has_prev:
        in_specs, ops = [cur, prev, cur, prev, cur, cur, cur, cur], [q, k, k, v, v, do, cb, lse]
        al = [dq_a, dkc_a, dvc_a, dkp_a, dvp_a]
    else:
        in_specs, ops = [cur, cur, cur, cur, cur, cur], [q, k, v, do, cb, lse]
        al = [dq_a, dkc_a, dvc_a]
    n_in = len(ops)
    outs = pl.pallas_call(
        body, name=name, grid=(B, nb, 2),
        in_specs=in_specs + [anyspec] * n_out,
        out_specs=[cur] * n_out,
        out_shape=[jax.ShapeDtypeStruct(q.shape, F32)] * n_out,
        input_output_aliases={n_in + i: i for i in range(n_out)},
        compiler_params=_cp(("parallel", "parallel", "parallel")),
    )(*ops, *al)
    if has_prev:
        dq_a, dkc_a, dvc_a, dkp_a, dvp_a = outs
    else:
        dq_a, dkc_a, dvc_a = outs
    return dq_a, dkc_a, dkp_a, dvc_a, dvp_a


N_UNITS = 16


def _unit_rows(g):
    d = DIL[g]
    nb = N_UNITS // d
    return [pl.ds(n * STEPS * d + r, STEPS, stride=d) if d > 1 else pl.ds(n * STEPS, STEPS)
            for n in range(nb) for r in range(d)]


def _load_units(ref, g):
    if DIL[g] == 1:
        return ref[...].reshape(N_UNITS, STEPS, 128)
    return jnp.stack([ref[rows, :] for rows in _unit_rows(g)])


def _store_units(ref, val, g):
    if DIL[g] == 1:
        ref[...] = val.reshape(N_UNITS * STEPS, 128)
    else:
        for u, rows in enumerate(_unit_rows(g)):
            ref[rows, :] = val[u]


def _shift_units(x, by):
    z = jnp.zeros((abs(by),) + x.shape[1:], x.dtype)
    return jnp.concatenate([z, x[:N_UNITS - by]], axis=0) if by > 0 else jnp.concatenate([x[-by:], z], axis=0)


def _bdot(a, b, ca, cb):
    return lax.dot_general(a, b, (((ca,), (cb,)), ((0,), (0,))), preferred_element_type=F32)


def _dil_masks(g):
    d = DIL[g]
    has_prev = N_UNITS // d > 1
    qi = lax.broadcasted_iota(jnp.int32, (1, STEPS, STEPS), 1)
    kj = lax.broadcasted_iota(jnp.int32, (1, STEPS, STEPS), 2)
    unit = lax.broadcasted_iota(jnp.int32, (N_UNITS, 1, 1), 0)
    cur = kj <= qi
    prev = jnp.logical_and(kj >= qi, unit >= d) if has_prev else None
    lane = lax.broadcasted_iota(jnp.int32, (1, 1, 128), 2)
    heads = [(lane // HEAD_DIM) == h for h in range(128 // HEAD_DIM)]
    return has_prev, cur, prev, heads


def _dil_fwd(g, q, k, v, o_alias, l_alias, B, S, *, name):
    assert S == N_UNITS * STEPS
    d = DIL[g]

    def body(q_ref, k_ref, v_ref, _, __, o_ref, l_ref):
        has_prev, cur, prev, heads = _dil_masks(g)
        q = _load_units(q_ref, g)
        kc = _load_units(k_ref, g).astype(BF)
        vc = _load_units(v_ref, g).astype(BF)
        if has_prev:
            kp, vp = _shift_units(kc, d), _shift_units(vc, d)
        o = jnp.zeros(q.shape, F32)
        lse_b = jnp.zeros(q.shape, F32)
        for m in heads:
            qm = jnp.where(m, q, 0.0).astype(BF)
            sc = jnp.where(cur, _bdot(qm, kc, 2, 2) * SCALE, NEG)
            mx = jnp.max(sc, axis=-1, keepdims=True)
            if has_prev:
                sp = jnp.where(prev, _bdot(qm, kp, 2, 2) * SCALE, NEG)
                mx = jnp.maximum(mx, jnp.max(sp, axis=-1, keepdims=True))
            l = jnp.sum(jnp.exp(sc - mx), axis=-1, keepdims=True)
            if has_prev:
                l = l + jnp.sum(jnp.exp(sp - mx), axis=-1, keepdims=True)
            lse = mx + jnp.log(l)
            oh = _bdot(jnp.exp(sc - lse).astype(BF), vc, 2, 1)
            if has_prev:
                oh = oh + _bdot(jnp.exp(sp - lse).astype(BF), vp, 2, 1)
            o = o + jnp.where(m, oh, 0.0)
            lse_b = lse_b + jnp.where(m, lse, 0.0)
        _store_units(o_ref, o, g)
        _store_units(l_ref, lse_b, g)

    blk = pl.BlockSpec((S, 128), lambda b, hf: (b, g * 2 + hf))
    anyspec = pl.BlockSpec(memory_space=pl.ANY)
    o, l = pl.pallas_call(
        body, name=name, grid=(B, 2),
        in_specs=[blk, blk, blk, anyspec, anyspec], out_specs=[blk, blk],
        out_shape=[jax.ShapeDtypeStruct(q.shape, F32)] * 2,
        input_output_aliases={3: 0, 4: 1},
        compiler_params=_cp(("parallel", "parallel")),
    )(q, k, v, o_alias, l_alias)
    return o, l


def _dil_bwd(g, q, k, v, do, cb, lse, aliases, B, S, *, name):
    assert S == N_UNITS * STEPS
    d = DIL[g]

    def body(q_ref, k_ref, v_ref, do_ref, c_ref, l_ref, _, __, ___, dq_ref, dk_ref, dv_ref):
        has_prev, cur, prev, heads = _dil_masks(g)
        q = _load_units(q_ref, g)
        kc = _load_units(k_ref, g).astype(BF)
        vc = _load_units(v_ref, g).astype(BF)
        do = _load_units(do_ref, g)
        cbv = _load_units(c_ref, g)
        lse_b = _load_units(l_ref, g)
        if has_prev:
            kp, vp = _shift_units(kc, d), _shift_units(vc, d)
        z = jnp.zeros(q.shape, F32)
        dq, dkc, dkp, dvc, dvp = z, z, z, z, z
        for m in heads:
            qm = jnp.where(m, q, 0.0).astype(BF)
            dom = jnp.where(m, do, 0.0).astype(BF)
            lse = jnp.max(jnp.where(m, lse_b, -jnp.inf), axis=-1, keepdims=True)
            c = jnp.max(jnp.where(m, cbv, -jnp.inf), axis=-1, keepdims=True)
            sc = jnp.where(cur, _bdot(qm, kc, 2, 2) * SCALE, NEG)
            pc = jnp.exp(sc - lse)
            dsc = (pc * (_bdot(dom, vc, 2, 2) + c) * SCALE).astype(BF)
            dqh = _bdot(dsc, kc, 2, 1)
            dkc = dkc + jnp.where(m, _bdot(dsc, qm, 1, 1), 0.0)
            dvc = dvc + jnp.where(m, _bdot(pc.astype(BF), dom, 1, 1), 0.0)
            if has_prev:
                sp = jnp.where(prev, _bdot(qm, kp, 2, 2) * SCALE, NEG)
                pp = jnp.exp(sp - lse)
                dsp = (pp * (_bdot(dom, vp, 2, 2) + c) * SCALE).astype(BF)
                dqh = dqh + _bdot(dsp, kp, 2, 1)
                dkp = dkp + jnp.where(m, _bdot(dsp, qm, 1, 1), 0.0)
                dvp = dvp + jnp.where(m, _bdot(pp.astype(BF), dom, 1, 1), 0.0)
            dq = dq + jnp.where(m, dqh, 0.0)
        if has_prev:
            dkc = dkc + _shift_units(dkp, -d)
            dvc = dvc + _shift_units(dvp, -d)
        _store_units(dq_ref, dq, g)
        _store_units(dk_ref, dkc, g)
        _store_units(dv_ref, dvc, g)

    blk = pl.BlockSpec((S, 128), lambda b, hf: (b, g * 2 + hf))
    anyspec = pl.BlockSpec(memory_space=pl.ANY)
    return tuple(pl.pallas_call(
        body, name=name, grid=(B, 2),
        in_specs=[blk] * 6 + [anyspec] * 3, out_specs=[blk] * 3,
        out_shape=[jax.ShapeDtypeStruct(q.shape, F32)] * 3,
        input_output_aliases={6: 0, 7: 1, 8: 2},
        compiler_params=_cp(("parallel", "parallel")),
    )(q, k, v, do, cb, lse, *aliases))


def _kv_grad_sum(parts, cos, sin, *, name, tm=512):
    T = parts[0][0].shape[0]
    n_l = len(parts)

    def body(*refs):
        c_ref, s_ref = refs[0], refs[1]
        dk_ref, dv_ref = refs[2 + 2 * n_l:]
        dk = refs[2][...]
        dv = refs[3][...]
        for li in range(1, n_l):
            dk = dk + refs[2 + 2 * li][...]
            dv = dv + refs[3 + 2 * li][...]
        dk_ref[...] = _rot(dk, c_ref[...], s_ref[...], -1.0).astype(dk_ref.dtype)
        dv_ref[...] = dv.astype(dv_ref.dtype)

    full = pl.BlockSpec((tm, MAIN_W), lambda i: (i, 0))
    tab = pl.BlockSpec((tm, 128), lambda i: (i, 0))
    ops = [cos, sin] + [t for part in parts for t in part]
    return pl.pallas_call(
        body, name=name, grid=(T // tm,), in_specs=[tab, tab] + [full] * (2 * n_l), out_specs=[full, full],
        out_shape=[jax.ShapeDtypeStruct((T, MAIN_W), BF)] * 2,
        compiler_params=_cp(("parallel",)),
    )(*ops)


def _group_softmax(lse):
    l0, l1, l2 = lse[:, 0:256], lse[:, 256:512], lse[:, 512:768]
    mx = jnp.maximum(jnp.maximum(l0, l1), l2)
    e0, e1, e2 = jnp.exp(l0 - mx), jnp.exp(l1 - mx), jnp.exp(l2 - mx)
    tot = e0 + e1 + e2
    return e0 / tot, e1 / tot, e2 / tot


def _dil_combine_fwd(o, lse, y_alias, *, name, tm=512):
    T = o.shape[0]

    def body(o_ref, l_ref, _, y_ref):
        a = jnp.concatenate(_group_softmax(l_ref[...]), axis=1)
        y_ref[...] = (o_ref[...] * a).astype(y_ref.dtype)

    blk = pl.BlockSpec((tm, MAIN_W), lambda i: (i, 0))
    return pl.pallas_call(
        body, name=name, grid=(T // tm,), in_specs=[blk, blk, pl.BlockSpec(memory_space=pl.ANY)], out_specs=blk,
        out_shape=jax.ShapeDtypeStruct(y_alias.shape, y_alias.dtype), input_output_aliases={2: 0},
        compiler_params=_cp(("parallel",)),
    )(o, lse, y_alias)


def _dil_combine_bwd(dy, o, lse, *, name, tm=256):
    T = o.shape[0]
    lane_r = lax.broadcasted_iota(jnp.int32, (256, 256), 0) // HEAD_DIM
    lane_c = lax.broadcasted_iota(jnp.int32, (256, 256), 1) // HEAD_DIM
    ones_bd = (lane_r == lane_c).astype(BF)

    def body(dy_ref, o_ref, l_ref, e_ref, do_ref, c_ref):
        dyv = dy_ref[...]
        alphas = _group_softmax(l_ref[...])
        prod = dyv * o_ref[...]
        e = e_ref[...]
        tot = jnp.zeros((tm, 256), F32)
        for gi in range(3):
            x = prod[:, gi * 256:(gi + 1) * 256]
            hi = x.astype(BF)
            lo = (x - hi.astype(F32)).astype(BF)
            dalpha = jnp.dot(hi, e, preferred_element_type=F32) + jnp.dot(lo, e, preferred_element_type=F32)
            tot = tot + alphas[gi] * dalpha
        a = jnp.concatenate(alphas, axis=1)
        do_ref[...] = (dyv * a).astype(do_ref.dtype)
        c_ref[...] = jnp.concatenate([-al * tot for al in alphas], axis=1)

    blk = pl.BlockSpec((tm, MAIN_W), lambda i: (i, 0))
    return pl.pallas_call(
        body, name=name, grid=(T // tm,),
        in_specs=[blk, blk, blk, pl.BlockSpec((256, 256), lambda i: (0, 0))], out_specs=[blk, blk],
        out_shape=[jax.ShapeDtypeStruct((T, MAIN_W), F32), jax.ShapeDtypeStruct((T, MAIN_W), F32)],
        compiler_params=_cp(("parallel",)),
    )(dy, o, lse, ones_bd)


def _kv_grad(parts, cos, sin, B, S, *, name):
    T = B * S
    tb = S // STEPS
    n_l = len(parts)

    def shifted(g):
        def f(b, t):
            return (b * tb + jnp.minimum(t + DIL[g], tb - 1), g)
        return f

    with_prev = [g for g in range(3) if DIL[g] < tb]
    n_p = len(with_prev)
    per_l = 2 + 2 * n_p

    def body(*refs):
        c_ref, s_ref = refs[0], refs[1]
        ins = refs[2:2 + n_l * per_l]
        dk_ref, dv_ref = refs[2 + n_l * per_l:]
        t = pl.program_id(1)
        dk = jnp.zeros((STEPS, MAIN_W), F32)
        dv = jnp.zeros((STEPS, MAIN_W), F32)
        zero = jnp.zeros((STEPS, 256), F32)
        for li in range(n_l):
            base = li * per_l
            dk = dk + ins[base][...]
            dv = dv + ins[base + 1][...]
            kparts, vparts = [zero] * 3, [zero] * 3
            for pi, g in enumerate(with_prev):
                ok = t + DIL[g] < tb
                kparts[g] = jnp.where(ok, ins[base + 2 + pi][...], 0.0)
                vparts[g] = jnp.where(ok, ins[base + 2 + n_p + pi][...], 0.0)
            dk = dk + jnp.concatenate(kparts, axis=1)
            dv = dv + jnp.concatenate(vparts, axis=1)
        dk_ref[...] = _rot(dk, c_ref[...], s_ref[...], -1.0).astype(dk_ref.dtype)
        dv_ref[...] = dv.astype(dv_ref.dtype)

    full = pl.BlockSpec((STEPS, MAIN_W), lambda b, t: (b * tb + t, 0))
    tab = pl.BlockSpec((STEPS, 128), lambda b, t: (b * tb + t, 0))
    in_specs, ops = [tab, tab], [cos, sin]
    for (kc, kp, vc, vp) in parts:
        in_specs += [full, full] + [pl.BlockSpec((STEPS, 256), shifted(g)) for g in with_prev] * 2
        ops += [kc, vc] + [kp] * n_p + [vp] * n_p
    return pl.pallas_call(
        body, name=name, grid=(B, tb), in_specs=in_specs, out_specs=[full, full],
        out_shape=[jax.ShapeDtypeStruct((T, MAIN_W), BF)] * 2,
        compiler_params=_cp(("parallel", "parallel")),
    )(*ops)


def _loss(y, target, *, name, tm=512):
    T, Dm = y.shape
    nt = T // tm

    def body(y_ref, t_ref, l_ref, d_ref, acc):
        i = pl.program_id(0)
        err = y_ref[...] - t_ref[...]
        d_ref[...] = err / Dm
        part = jnp.sum(jnp.mean(err * err, axis=-1, keepdims=True).reshape(tm // 8, 8, 1), axis=0)

        @pl.when(i == 0)
        def _():
            acc[...] = part

        @pl.when(i > 0)
        def _():
            acc[...] += part

        @pl.when(i == nt - 1)
        def _():
            l_ref[...] = 0.5 * jnp.sum(acc[...], axis=0, keepdims=True)

    row = pl.BlockSpec((tm, Dm), lambda i: (i, 0))
    return pl.pallas_call(
        body, name=name, grid=(nt,), in_specs=[row, row],
        out_specs=[pl.BlockSpec((1, 1), lambda i: (0, 0)), row],
        out_shape=[jax.ShapeDtypeStruct((1, 1), F32), jax.ShapeDtypeStruct((T, Dm), F32)],
        scratch_shapes=[pltpu.VMEM((8, 1), F32)],
        compiler_params=_cp(("arbitrary",)),
    )(y, target)


def _adamw(w, g, m, v, *, name):
    shape = w.shape
    cols = shape[-1]
    rows = w.size // cols
    tm = rows
    for cand in (512, 352, 256, 128):
        if rows > cand and rows % cand == 0 and cand * cols * 4 <= (1 << 20):
            tm = cand
            break

    def body(w_ref, g_ref, m_ref, v_ref, d_ref, mo_ref, vo_ref):
        gv = g_ref[...]
        mn = ADAM_B1 * m_ref[...] + (1.0 - ADAM_B1) * gv
        vn = ADAM_B2 * v_ref[...] + (1.0 - ADAM_B2) * (gv * gv)
        m_hat = mn / (1.0 - ADAM_B1 ** ADAM_STEP)
        v_hat = vn / (1.0 - ADAM_B2 ** ADAM_STEP)
        d_ref[...] = -ADAM_LR * (m_hat / (jnp.sqrt(v_hat) + ADAM_EPS) + ADAM_WD * w_ref[...])
        mo_ref[...] = mn
        vo_ref[...] = vn

    blk = pl.BlockSpec((tm, cols), lambda i: (i, 0))
    outs = pl.pallas_call(
        body, name=name, grid=(rows // tm,), in_specs=[blk] * 4, out_specs=[blk] * 3,
        out_shape=[jax.ShapeDtypeStruct((rows, cols), F32)] * 3,
        compiler_params=_cp(("parallel",)),
    )(*[t.reshape(rows, cols) for t in (w, g, m, v)])
    return tuple(t.reshape(shape) for t in outs)


def _adamw_layer(name, l, w, g, m, v, prev, after=None):
    L, rows, cols = w.shape
    tm = rows
    for cand in (512, 352, 256, 176, 128, 64):
        if rows % cand == 0 and cand * cols * 4 <= (1 << 20):
            tm = cand
            break
    if prev is None:
        prev = tuple(lax.empty(w.shape, F32) for _ in range(4))

    n_after = 0 if after is None else 1

    def body(w_ref, g_ref, m_ref, v_ref, *rest):
        d_ref, mo_ref, vo_ref, go_ref = rest[4 + n_after:]
        gv = g_ref[...]
        mn = ADAM_B1 * m_ref[...] + (1.0 - ADAM_B1) * gv
        vn = ADAM_B2 * v_ref[...] + (1.0 - ADAM_B2) * (gv * gv)
        m_hat = mn / (1.0 - ADAM_B1 ** ADAM_STEP)
        v_hat = vn / (1.0 - ADAM_B2 ** ADAM_STEP)
        d_ref[...] = -ADAM_LR * (m_hat / (jnp.sqrt(v_hat) + ADAM_EPS) + ADAM_WD * w_ref[...])
        mo_ref[...] = mn
        vo_ref[...] = vn
        go_ref[...] = gv

    lay = pl.BlockSpec((None, tm, cols), lambda i: (l, i, 0))
    one = pl.BlockSpec((None, tm, cols), lambda i: (0, i, 0))
    return tuple(pl.pallas_call(
        body, name=f"l{l}_adamw_{name}", grid=(rows // tm,),
        in_specs=[lay, one, lay, lay] + [pl.BlockSpec(memory_space=pl.ANY)] * (4 + n_after), out_specs=[lay] * 4,
        out_shape=[jax.ShapeDtypeStruct(w.shape, F32)] * 4,
        input_output_aliases={4 + i: i for i in range(4)},
        compiler_params=_cp(("parallel",)),
    )(w, g, m, v, *prev, *([] if after is None else [after])))


BIG = {
    'w_in': ((DEPTH, D_MODEL, D_MODEL), 'row'),
    'w_mem_kv': ((DEPTH, D_MODEL, 2 * MEM_W), 'row'),
    'w_out': ((DEPTH, D_MODEL, D_MODEL), 'row'),
    'w_kv': ((1, D_MODEL, 2 * MAIN_W), 'col'),
    'w_gate_up': ((DEPTH, D_MODEL, 2 * D_FF), 'col'),
    'w_down': ((DEPTH, D_FF, D_MODEL), 'row'),
}
BIG_NAMES = tuple(BIG)
N_CHIPS = 4
HBM_ANY = pl.BlockSpec(memory_space=pl.ANY)


def _geom(name):
    (L, R, C), kind = BIG[name]
    if kind == 'row':
        return L, R, C, kind, R // N_CHIPS, C, R // (2 * N_CHIPS)
    return L, R, C, kind, R, C // N_CHIPS, R // 2


def _shard_shape(name):
    L, R, C, kind, rs, cs, rh = _geom(name)
    return (L, rs, cs)


def _half_shape(name):
    L, R, C, kind, rs, cs, rh = _geom(name)
    return (L, rh, cs)


def _full_win(ref, name, s, h):
    L, R, C, kind, rs, cs, rh = _geom(name)
    if kind == 'row':
        rows = pl.ds(s * rs, rs) if h is None else pl.ds(s * rs + h * rh, rh)
        return ref.at[:, rows, :]
    rows = slice(None) if h is None else pl.ds(h * rh, rh)
    return ref.at[:, rows, pl.ds(s * cs, cs)]


def _shard_half(ref, name, h):
    L, R, C, kind, rs, cs, rh = _geom(name)
    return ref.at[:, pl.ds(h * rh, rh), :]


def _halves_win(ref, name, s):
    L, R, C, kind, rs, cs, rh = _geom(name)
    if kind == 'row':
        return ref.at[:, pl.ds(s * rh, rh), :]
    return ref.at[:, :, pl.ds(s * cs, cs)]


def _halves_shape(name):
    L, R, C, kind, rs, cs, rh = _geom(name)
    return (L, N_CHIPS * rh, cs) if kind == 'row' else (L, rh, C)


def _place():
    x, y, c = lax.axis_index("x"), lax.axis_index("y"), lax.axis_index("c")
    chips = [(1 - x, y), (x, 1 - y), (1 - x, 1 - y)]
    return x, y, c, chips


SMALL_ROWS = 24


def _all_gather(shards, small):
    names = BIG_NAMES
    nw = len(names)

    def body(*refs):
        src = dict(zip(names, refs[:nw]))
        small_ref = refs[nw]
        dst = dict(zip(names, refs[nw + 1:2 * nw + 1]))
        small_out = refs[2 * nw + 1]
        send_sems, recv_sems, local_sems = refs[2 * nw + 2:]
        x, y, c, chips = _place()
        s = 2 * x + y
        sib = (x, y, 1 - c)

        def remote(k, src_ref, dst_ref, to):
            return pltpu.make_async_remote_copy(src_ref=src_ref, dst_ref=dst_ref, send_sem=send_sems.at[k],
                                                recv_sem=recv_sems.at[k], device_id=to, device_id_type=MESH)

        local = []
        for wi, nm in enumerate(names):
            local.append(pltpu.make_async_copy(src[nm], _full_win(dst[nm], nm, s, None), local_sems.at[wi]))
        local.append(pltpu.make_async_copy(small_ref, small_out.at[s], local_sems.at[nw]))
        for cp in local:
            cp.start()
        sends = []
        for j, (px, py) in enumerate(chips):
            for wi, nm in enumerate(names):
                sends.append(remote(wi * 6 + j, _shard_half(src[nm], nm, c), _full_win(dst[nm], nm, s, c), (px, py, c)))
            sends.append(remote(nw * 6 + j, small_ref, small_out.at[s], (px, py, c)))
        for cp in sends:
            cp.start()
        for j, (px, py) in enumerate(chips):
            sp = 2 * px + py
            for wi, nm in enumerate(names):
                w = _full_win(dst[nm], nm, sp, c)
                remote(wi * 6 + j, w, w, sib).wait_recv()
                fwd = remote(wi * 6 + 3 + j, w, w, sib)
                fwd.start()
                sends.append(fwd)
            remote(nw * 6 + j, small_ref, small_out.at[sp], sib).wait_recv()
        for j, (px, py) in enumerate(chips):
            sp = 2 * px + py
            for wi, nm in enumerate(names):
                w = _full_win(dst[nm], nm, sp, 1 - c)
                remote(wi * 6 + 3 + j, w, w, sib).wait_recv()
        for cp in sends:
            cp.wait_send()
        for cp in local:
            cp.wait()

    n_sem = nw * 6 + 3
    outs = pl.pallas_call(
        body, name="all_gather_weights",
        in_specs=[HBM_ANY] * (nw + 1), out_specs=[HBM_ANY] * (nw + 1),
        out_shape=[jax.ShapeDtypeStruct(BIG[nm][0], BF) for nm in names]
        + [jax.ShapeDtypeStruct((N_CHIPS, SMALL_ROWS, 256), F32)],
        scratch_shapes=[pltpu.SemaphoreType.DMA((n_sem,)), pltpu.SemaphoreType.DMA((n_sem,)),
                        pltpu.SemaphoreType.DMA((nw + 1,))],
    )(*[shards[nm] for nm in names], small)
    return dict(zip(names, outs[:nw])), outs[nw]


SEM_SPEC = pl.BlockSpec(memory_space=pltpu.SEMAPHORE)
HBM_SPEC = pl.BlockSpec(memory_space=pltpu.HBM)
DATAFLOW = pltpu.SideEffectType.DATAFLOW_SIDE_EFFECTING


def _in_hbm(a):
    return pltpu.with_memory_space_constraint(a, pltpu.HBM)


def _remote(src, dst, send_sems, recv_sems, k, to):
    return pltpu.make_async_remote_copy(src_ref=src, dst_ref=dst, send_sem=send_sems.at[k], recv_sem=recv_sems.at[k],
                                        device_id=to, device_id_type=MESH)


def _split_start(name, bufs, n_copies, sends, after=None):
    nb = len(bufs)
    n_in = nb + (0 if after is None else 1)

    def body(*refs):
        in_refs = refs[:nb]
        send_sems, recv_sems = refs[n_in], refs[n_in + 1]
        token = refs[-1]
        for k, (src, dst, to) in enumerate(sends(in_refs)):
            _remote(src, dst, send_sems, recv_sems, k, to).start()
        token[...] = jnp.zeros_like(token)

    outs = pl.pallas_call(
        body, name=name,
        out_shape=(pltpu.SemaphoreType.DMA((n_copies,)), pltpu.SemaphoreType.DMA((n_copies,)),
                   *[pltpu.HBM(b.shape, b.dtype) for b in bufs], jax.ShapeDtypeStruct((8, 128), F32)),
        in_specs=[HBM_SPEC] * nb + [HBM_ANY] * (n_in - nb),
        out_specs=(SEM_SPEC, SEM_SPEC, *[HBM_SPEC] * nb, pl.BlockSpec(memory_space=pltpu.VMEM)),
        input_output_aliases={i: 2 + i for i in range(nb)},
        compiler_params=pltpu.CompilerParams(has_side_effects=DATAFLOW),
    )(*[_in_hbm(b) for b in bufs], *([] if after is None else [after]))
    return outs[0], outs[1], list(outs[2:2 + nb]), outs[-1]


def _split_wait(name, send_sems, recv_sems, bufs, after, sends, arrivals):
    nb = len(bufs)

    def body(*refs):
        in_refs = refs[:nb]
        s_sems, r_sems = refs[nb], refs[nb + 1]
        me = (lax.axis_index("x"), lax.axis_index("y"), lax.axis_index("c"))
        for k, (src, dst, to) in enumerate(sends(in_refs)):
            _remote(src, dst, s_sems, r_sems, k, to).wait_send()
        for k, win in enumerate(arrivals(in_refs)):
            _remote(win, win, s_sems, r_sems, k, me).wait_recv()

    outs = pl.pallas_call(
        body, name=name,
        out_shape=[pltpu.HBM(b.shape, b.dtype) for b in bufs],
        in_specs=[HBM_SPEC] * nb + [SEM_SPEC, SEM_SPEC, HBM_ANY],
        out_specs=[HBM_SPEC] * nb,
        input_output_aliases={i: i for i in range(nb)},
        compiler_params=pltpu.CompilerParams(has_side_effects=DATAFLOW),
    )(*bufs, send_sems, recv_sems, after)
    return list(outs)


MIX_W = ('w_in', 'w_mem_kv', 'w_out')
FFN_W = ('w_gate_up', 'w_down')
LAYER_W = MIX_W + FFN_W


def _place_own(l, names, shards, small, sc):
    nw = len(names)
    has_small = small is not None
    n_ops = nw + (1 if has_small else 0)

    def body(sc_ref, *refs):
        for src, dst in zip(refs[:n_ops], refs[n_ops:]):
            dst[...] = src[...]

    in_specs, out_specs, out_shape, ops = [], [], [], list(shards)
    for nm in names:
        L, R, C, kind, rs, cs, rh = _geom(nm)
        in_specs.append(pl.BlockSpec((1, rs, cs), lambda i, sc_ref: (0, 0, 0)))
        if kind == 'row':
            out_specs.append(pl.BlockSpec((1, rs, cs), lambda i, sc_ref: (0, sc_ref[0], 0)))
        else:
            out_specs.append(pl.BlockSpec((1, rs, cs), lambda i, sc_ref: (0, 0, sc_ref[0])))
        out_shape.append(jax.ShapeDtypeStruct((1, R, C), BF))
    if has_small:
        in_specs.append(pl.BlockSpec((SMALL_ROWS, 256), lambda i, sc_ref: (0, 0)))
        out_specs.append(pl.BlockSpec((None, SMALL_ROWS, 256), lambda i, sc_ref: (sc_ref[0], 0, 0)))
        out_shape.append(jax.ShapeDtypeStruct((N_CHIPS, SMALL_ROWS, 256), F32))
        ops.append(small)
    return pl.pallas_call(
        body, name=f"{l}_place_own_shard",
        grid_spec=pltpu.PrefetchScalarGridSpec(num_scalar_prefetch=1, grid=(1,), in_specs=in_specs, out_specs=out_specs),
        out_shape=out_shape,
        compiler_params=_cp(("arbitrary",)),
    )(sc, *ops)


def _gather_start(l, names, shards, small, sc, after=None):
    nw = len(names)
    has_small = small is not None
    fulls = _place_own(l, names, shards, small, sc)
    bufs = list(shards) + ([small] if has_small else []) + list(fulls)
    n_src = nw + (1 if has_small else 0)

    def sends(refs):
        x, y, c, chips = _place()
        s = 2 * x + y
        out = []
        for (px, py) in chips:
            for wi, nm in enumerate(names):
                out.append((_shard_half(refs[wi], nm, c), _full_win(refs[n_src + wi], nm, s, c), (px, py, c)))
            if has_small:
                out.append((refs[nw], refs[n_src + nw].at[s], (px, py, c)))
        return out

    def arrivals(refs):
        x, y, c, chips = _place()
        out = []
        for (px, py) in chips:
            sp = 2 * px + py
            for wi, nm in enumerate(names):
                out.append(_full_win(refs[n_src + wi], nm, sp, c))
            if has_small:
                out.append(refs[n_src + nw].at[sp])
        return out

    n_copies = 3 * n_src
    send_sems, recv_sems, bufs, token = _split_start(f"{l}_gather_ici_start", bufs, n_copies, sends, after)
    return dict(l=l, names=names, has_small=has_small, sems=(send_sems, recv_sems), bufs=bufs, sends=sends,
                arrivals=arrivals, token=token)


def _gather_forward(st, after):
    l, names = st['l'], st['names']
    nw = len(names)
    n_src = nw + (1 if st['has_small'] else 0)
    bufs = _split_wait(f"{l}_gather_ici_wait", *st['sems'], st['bufs'], after, st['sends'], st['arrivals'])
    fulls = bufs[n_src:n_src + nw]
    small_all = bufs[n_src + nw] if st['has_small'] else None

    def sends(refs):
        x, y, c, chips = _place()
        out = []
        for (px, py) in chips:
            sp = 2 * px + py
            for wi, nm in enumerate(names):
                w = _full_win(refs[wi], nm, sp, c)
                out.append((w, w, (x, y, 1 - c)))
        return out

    def arrivals(refs):
        x, y, c, chips = _place()
        out = []
        for (px, py) in chips:
            sp = 2 * px + py
            for wi, nm in enumerate(names):
                out.append(_full_win(refs[wi], nm, sp, 1 - c))
        return out

    send_sems, recv_sems, fulls, token = _split_start(f"{l}_gather_d2d_start", fulls, 3 * nw, sends)
    return dict(l=l, names=names, sems=(send_sems, recv_sems), bufs=fulls, sends=sends, arrivals=arrivals,
                small_all=small_all, token=token)


def _gather_finish(st, after):
    fulls = _split_wait(f"{st['l']}_gather_d2d_wait", *st['sems'], st['bufs'], after, st['sends'], st['arrivals'])
    return dict(zip(st['names'], fulls)), st['small_all']


def _reduce_start(tag, names, grads):
    nw = len(names)
    recv = [lax.empty((1,) + _halves_shape(nm)[1:], F32) for nm in names]
    bufs = [grads[nm] for nm in names] + recv

    def windows(refs, half_of):
        x, y, c, _ = _place()
        h = half_of(c)
        out = []
        for wi, nm in enumerate(names):
            L, R, C, kind, rs, cs, rh = _geom(nm)
            if kind == 'row':
                for sp in range(N_CHIPS):
                    out.append((_full_win(refs[wi], nm, sp, h), _halves_win(refs[nw + wi], nm, sp)))
            else:
                out.append((refs[wi].at[:, pl.ds(h * rh, rh), :], refs[nw + wi]))
        return out

    def sends(refs):
        x, y, c, _ = _place()
        return [(src, dst, (x, y, 1 - c)) for src, dst in windows(refs, lambda c: 1 - c)]

    def arrivals(refs):
        return [dst for _, dst in windows(refs, lambda c: c)]

    n_copies = sum(N_CHIPS if BIG[nm][1] == 'row' else 1 for nm in names)
    send_sems, recv_sems, bufs, token = _split_start(tag + "_halves_start", bufs, n_copies, sends)
    return dict(tag=tag, names=names, sems=(send_sems, recv_sems), bufs=bufs, sends=sends, arrivals=arrivals, token=token)


def _reduce_mid(st, after, sc):
    tag, names = st['tag'], st['names']
    nw = len(names)
    bufs = _split_wait(tag + "_halves_wait", *st['sems'], st['bufs'], after, st['sends'], st['arrivals'])
    halves, own = [], []
    for wi, nm in enumerate(names):
        hb, ow = _add_halves(nm, bufs[wi], bufs[nw + wi], sc, tag)
        halves.append(hb)
        own.append(ow)
    pieces = [lax.empty((3, 1) + _half_shape(nm)[1:], BF) for nm in names]

    def sends(refs):
        x, y, c, chips = _place()
        out = []
        for j, (px, py) in enumerate(chips):
            for wi, nm in enumerate(names):
                out.append((_halves_win(refs[wi], nm, 2 * px + py), refs[nw + wi].at[j], (px, py, c)))
        return out

    def arrivals(refs):
        return [refs[nw + wi].at[j] for j in range(3) for wi in range(nw)]

    send_sems, recv_sems, bufs, token = _split_start(tag + "_pieces_start", halves + pieces, 3 * nw, sends)
    return dict(tag=tag, names=names, sems=(send_sems, recv_sems), bufs=bufs, sends=sends, arrivals=arrivals, own=own,
                token=token)


def _reduce_late(st, after, sc):
    tag, names = st['tag'], st['names']
    nw = len(names)
    bufs = _split_wait(tag + "_pieces_wait", *st['sems'], st['bufs'], after, st['sends'], st['arrivals'])
    gsh = [_sum_pieces(nm, st['own'][wi], bufs[nw + wi], sc, tag) for wi, nm in enumerate(names)]

    def sends(refs):
        x, y, c, _ = _place()
        return [(_shard_half(refs[wi], nm, c), _shard_half(refs[wi], nm, c), (x, y, 1 - c)) for wi, nm in enumerate(names)]

    def arrivals(refs):
        x, y, c, _ = _place()
        return [_shard_half(refs[wi], nm, 1 - c) for wi, nm in enumerate(names)]

    send_sems, recv_sems, bufs, token = _split_start(tag + "_share_start", gsh, nw, sends)
    return dict(tag=tag, names=names, sems=(send_sems, recv_sems), bufs=bufs, sends=sends, arrivals=arrivals, token=token)


def _reduce_finish(st, after):
    gsh = _split_wait(st['tag'] + "_share_wait", *st['sems'], st['bufs'], after, st['sends'], st['arrivals'])
    return dict(zip(st['names'], gsh))


def _add_halves(name, g, r, sc, tag):
    _, R, C, kind, rs, cs, rh = _geom(name)
    L = g.shape[0]
    tr = rh if kind == 'row' else 256
    nr = rh // tr

    def body(sc_ref, g_ref, r_ref, hb_ref, own_ref):
        sp = pl.program_id(2)
        tot = g_ref[...] + r_ref[...]
        hb_ref[...] = tot.astype(hb_ref.dtype)

        @pl.when(sp == sc_ref[0])
        def _():
            own_ref[...] = tot

    if kind == 'row':
        g_map = lambda l, ri, sp, sc_ref: (l, sp * 2 + sc_ref[1], 0)
        h_map = lambda l, ri, sp, sc_ref: (l, sp, 0)
    else:
        g_map = lambda l, ri, sp, sc_ref: (l, sc_ref[1] * nr + ri, sp)
        h_map = lambda l, ri, sp, sc_ref: (l, ri, sp)
    own_map = lambda l, ri, sp, sc_ref: (l, ri, 0)
    blk = (None, tr, cs)
    return pl.pallas_call(
        body, name=tag + "_add_halves_" + name,
        grid_spec=pltpu.PrefetchScalarGridSpec(
            num_scalar_prefetch=1, grid=(L, nr, N_CHIPS),
            in_specs=[pl.BlockSpec(blk, g_map), pl.BlockSpec(blk, h_map)],
            out_specs=[pl.BlockSpec(blk, h_map), pl.BlockSpec(blk, own_map)]),
        out_shape=[jax.ShapeDtypeStruct((L,) + _halves_shape(name)[1:], BF),
                   jax.ShapeDtypeStruct((L,) + _half_shape(name)[1:], F32)],
        compiler_params=_cp(("parallel", "parallel", "arbitrary")),
    )(sc, g, r)


def _sum_pieces(name, own, pieces, sc, tag):
    _, R, C, kind, rs, cs, rh = _geom(name)
    L = own.shape[0]
    tr = rh if kind == 'row' else 256
    nr = rh // tr

    def body(sc_ref, o_ref, p_ref, out_ref):
        out_ref[...] = o_ref[...] + p_ref[0].astype(F32) + p_ref[1].astype(F32) + p_ref[2].astype(F32)

    blk = (None, tr, cs)
    return pl.pallas_call(
        body, name=tag + "_sum_pieces_" + name,
        grid_spec=pltpu.PrefetchScalarGridSpec(
            num_scalar_prefetch=1, grid=(L, nr),
            in_specs=[pl.BlockSpec(blk, lambda l, ri, sc_ref: (l, ri, 0)),
                      pl.BlockSpec((3, None, tr, cs), lambda l, ri, sc_ref: (0, l, ri, 0))],
            out_specs=pl.BlockSpec(blk, lambda l, ri, sc_ref: (l, sc_ref[1] * nr + ri, 0))),
        out_shape=jax.ShapeDtypeStruct((L,) + _shard_shape(name)[1:], F32),
        compiler_params=_cp(("parallel", "parallel")),
    )(sc, own, pieces)


def _small_gather_start(v, sc):
    rows = v.shape[0]

    def place(sc_ref, v_ref, o_ref):
        o_ref[...] = v_ref[...]

    slots = pl.pallas_call(
        place, name="small_grads_place_own",
        grid_spec=pltpu.PrefetchScalarGridSpec(
            num_scalar_prefetch=1, grid=(1,),
            in_specs=[pl.BlockSpec((rows, 128), lambda i, sc_ref: (0, 0))],
            out_specs=pl.BlockSpec((None, rows, 128), lambda i, sc_ref: (2 * sc_ref[0] + sc_ref[1], 0, 0))),
        out_shape=jax.ShapeDtypeStruct((8, rows, 128), F32),
        compiler_params=_cp(("arbitrary",)),
    )(sc, v)

    def peers():
        x, y, c, _ = _place()
        flips = [(fx, fy, fc) for fx in (0, 1) for fy in (0, 1) for fc in (0, 1)][1:]
        return [((1 - x if fx else x), (1 - y if fy else y), (1 - c if fc else c)) for fx, fy, fc in flips]

    def sends(refs):
        x, y, c, _ = _place()
        return [(refs[0], refs[1].at[4 * x + 2 * y + c], p) for p in peers()]

    def arrivals(refs):
        return [refs[1].at[4 * px + 2 * py + pc] for px, py, pc in peers()]

    send_sems, recv_sems, bufs, token = _split_start("small_grads_gather_start", [v, slots], 7, sends)
    return dict(sems=(send_sems, recv_sems), bufs=bufs, sends=sends, arrivals=arrivals, token=token)


def _small_gather_finish(st, after):
    return _split_wait("small_grads_gather_wait", *st['sems'], st['bufs'], after, st['sends'], st['arrivals'])[1]


def _sum8(v8, *, name, tr=336):
    rows = v8.shape[1]
    tr = min(tr, rows)
    assert rows % tr == 0

    def body(v_ref, o_ref):
        tot = v_ref[0]
        for d in range(1, 8):
            tot = tot + v_ref[d]
        o_ref[...] = tot

    return pl.pallas_call(
        body, name=name, grid=(rows // tr,),
        in_specs=[pl.BlockSpec((8, tr, 128), lambda i: (0, i, 0))], out_specs=pl.BlockSpec((tr, 128), lambda i: (i, 0)),
        out_shape=jax.ShapeDtypeStruct((rows, 128), F32),
        compiler_params=_cp(("parallel",)),
    )(v8)


def _block_diag(w_pool_l):
    wbd = jnp.zeros((MAIN_W, MAIN_W), F32)
    for gi in range(len(POOL_WINDOWS)):
        wbd = lax.dynamic_update_slice(wbd, w_pool_l[gi], (gi * POOL_GROUP, gi * POOL_GROUP))
    return wbd.astype(BF)


def _unpack_small(small_all):
    ng = small_all[:, :16, :].reshape(N_CHIPS, DEPTH, 4, 256).transpose(1, 2, 0, 3).reshape(DEPTH, 4, D_MODEL)
    ps = small_all[:, 16:18, :POOL_GROUP].transpose(1, 0, 2).reshape(N_A, MAIN_W)
    return ng, ps


def _local_step(x, mem, positions, on_forward, on_backward, mem_norm, w_pool, kv_norm, target):
    B, S, _ = x.shape
    T = B * S
    xc = x.reshape(T, D_MODEL)
    memf = mem.reshape(B * N_MEM, D_MODEL)
    tgt = target.reshape(T, D_MODEL)
    cos, sin = _rope_tables(positions.reshape(T, 1), name="rope_tables")
    wbd = [_block_diag(w_pool[l]) for l in range(N_A)]
    nbo = D_FF // 256
    fw = []
    rk = rv = None
    kv_saved = None
    wts = []
    norm_gains = pool_scale = y2 = None

    def tied(vec, tok):
        return vec if tok is None else vec + tok

    for l in range(DEPTH):
        t = f"l{l}_"
        got = on_forward('start', l, y2)
        wts.append(dict(got[0]))
        if l == 0:
            norm_gains, pool_scale = _unpack_small(got[1])
        sv = {'x_in': xc}
        h0, sv['r0'] = _norm_fwd(xc, tied(norm_gains[l, 0], got[2]), name=t + "norm0", out_dtype=BF)
        z, = _mm(h0, wts[l]['w_in'], b_layer=0, name=t + "mm_in")
        memn, sv['rm'] = _norm_fwd(memf, mem_norm[l], name=t + "norm_mem", out_dtype=BF, tm=256)
        kvm, = _mm(memn, wts[l]['w_mem_kv'], b_layer=0, name=t + "mm_memkv", out_dtypes=(BF,))
        if l < N_A:
            ycat, sv['p'] = _pool_fwd(z, wbd[l], pool_scale[l], B, S, name=t + "pool_fwd")
        else:
            rq = _rope_apply(z, cos, sin, name=t + "rope_q", out_dtype=F32)
            o = lax.empty((T, MAIN_W), F32)
            lse = lax.empty((T, MAIN_W), F32)
            for g in range(3):
                o, lse = _dil_fwd(g, rq, rk, rv, o, lse, B, S, name=t + f"dil_fwd{g}")
            ycat = _dil_combine_fwd(o, lse, lax.empty((T, D_MODEL), BF), name=t + "dil_combine")
            sv.update(rq=rq, o=o, lse=lse)
        ycat, sv['lse_m'] = _memattn_fwd(z, kvm, ycat, B, S, name=t + "memattn_fwd")
        tok = on_forward('mid', l, ycat)
        y1, = _mm(ycat, wts[l]['w_out'], b_layer=0, name=t + "mm_out")
        wts[l].update(on_forward('ffn', l, y1)[0])
        x1, sv['r1'] = _norm_fwd(y1, tied(norm_gains[l, 1], tok), name=t + "norm1", res=xc)
        h2, sv['r2'] = _norm_fwd(x1, norm_gains[l, 2], name=t + "norm2", out_dtype=BF)
        gg, uu, aa = _mm(h2, wts[l]['w_gate_up'], b_layer=0, b_offsets=(0, nbo), out_n=D_FF, tn=256, name=t + "mm_gate_up",
                         epilogue=_swiglu_fwd_epilogue, out_dtypes=(BF, BF, BF))
        on_forward('post', l, gg)
        y2, = _mm(aa, wts[l]['w_down'], b_layer=0, tk=D_FF, name=t + "mm_down")
        x2, sv['r3'] = _norm_fwd(y2, norm_gains[l, 3], name=t + "norm3", res=x1)
        sv.update(h0=h0, z=z, memn=memn, kvm=kvm, ycat=ycat, y1=y1, x1=x1, h2=h2, gg=gg, uu=uu, aa=aa, y2=y2)
        fw.append(sv)
        xc = x2
        if l == N_A - 1:
            kvn, rkv = _norm_fwd(xc, kv_norm, name="norm_kv", out_dtype=BF)
            kv, = _mm(kvn, wts[N_A - 1]['w_kv'], b_layer=0, name="mm_kv")
            rk, rv = _rope_apply(kv, cos, sin, name="rope_k", passthrough=True, out_dtype=F32)
            kv_saved = (xc, kvn, rkv)

    loss, dx = _loss(xc, tgt, name="loss")

    d_ng = [[None] * 4 for _ in range(DEPTH)]
    d_memnorm = [None] * DEPTH
    d_wbd = [None] * N_A
    d_pscale = [None] * N_A
    d_kvnorm = None
    kv_parts = []
    tok = None

    def as3d(gl):
        return {nm: g.reshape((1,) + g.shape) for nm, g in gl.items()}

    for l in reversed(range(DEPTH)):
        t = f"l{l}_b_"
        sv = fw[l]
        gl = {}
        dy2, d_ng[l][3] = _norm_bwd(dx, sv['y2'], sv['r3'], tied(norm_gains[l, 3], tok), name=t + "norm3", out_dtype=BF)
        gl['w_down'], = _mm(sv['aa'], dy2, ta=True, tm=1408, tk=2048, name=t + "dw_down")
        dg, du = _mm(dy2, wts[l]['w_down'], tb=True, b_layer=0, tn=256, name=t + "d_act",
                     extras=((sv['gg'], 'tile'), (sv['uu'], 'tile')), epilogue=_swiglu_bwd_epilogue, out_dtypes=(BF, BF))
        gl['w_gate_up'], = _mm(sv['h2'], (dg, du), ta=True, tn=1408, tk=1024, name=t + "dw_gate_up")
        dh2, = _mm((dg, du), wts[l]['w_gate_up'], tb=True, b_layer=0, tn=1024, tk=1408, name=t + "d_h2", out_dtypes=(BF,))
        dx1, d_ng[l][2] = _norm_bwd(dh2, sv['x1'], sv['r2'], norm_gains[l, 2], name=t + "norm2", add=dx)
        tok = on_backward('ffn', l, dx1, as3d(gl))
        dy1, d_ng[l][1] = _norm_bwd(dx1, sv['y1'], sv['r1'], tied(norm_gains[l, 1], tok), name=t + "norm1", out_dtype=BF)
        gl['w_out'], = _mm(sv['ycat'], dy1, ta=True, tk=4096, name=t + "dw_out")
        dycat, = _mm(dy1, wts[l]['w_out'], tb=True, b_layer=0, name=t + "d_ycat")
        dz = lax.empty((T, D_MODEL), BF)
        dz, dkm, dvm = _memattn_bwd(dycat, sv['z'], sv['kvm'], sv['lse_m'], dz, B, S, name=t + "memattn")
        if l < N_A:
            dz, d_wbd[l], d_pscale[l] = _pool_bwd(dycat, sv['p'], wbd[l], pool_scale[l], dz, B, S, name=t + "pool")
        else:
            do, cb = _dil_combine_bwd(dycat, sv['o'], sv['lse'], name=t + "dil_combine")
            acc = tuple(lax.empty((T, MAIN_W), F32) for _ in range(3))
            for g in range(3):
                acc = _dil_bwd(g, sv['rq'], rk, rv, do, cb, sv['lse'], acc, B, S, name=t + f"dil{g}")
            dz = _rope_apply(acc[0], cos, sin, name=t + "rope_q", sign=-1.0, alias=dz)
            kv_parts.append(acc[1:])
        tok = on_backward('mix', l, dz, as3d(gl))
        gl['w_in'], = _mm(sv['h0'], dz, ta=True, tk=4096, name=t + "dw_in")
        dh0, = _mm(dz, wts[l]['w_in'], tb=True, b_layer=0, name=t + "d_h0", out_dtypes=(BF,))
        dx, d_ng[l][0] = _norm_bwd(dh0, sv['x_in'], sv['r0'], tied(norm_gains[l, 0], tok), name=t + "norm0", add=dx1)
        gl['w_mem_kv'], = _mm(sv['memn'], (dkm, dvm), ta=True, tn=256, name=t + "dw_memkv")
        dmemn, = _mm((dkm, dvm), wts[l]['w_mem_kv'], tb=True, b_layer=0, tk=256, name=t + "d_memn", out_dtypes=(BF,))
        _, d_memnorm[l] = _norm_bwd(dmemn, memf, sv['rm'], mem_norm[l], name=t + "norm_mem", out_dtype=BF, tm=256)
        if l == N_A:
            dk, dv = _kv_grad_sum(kv_parts, cos, sin, name="kv_grad")
            x_kv, kvn, rkv = kv_saved
            gl['w_kv'], = _mm(kvn, (dk, dv), ta=True, tn=768, tk=2048, name="dw_kv")
            dkvn, = _mm((dk, dv), wts[N_A - 1]['w_kv'], tb=True, b_layer=0, tn=1024, tk=768, name="d_kvn", out_dtypes=(BF,))
            dx, d_kvnorm = _norm_bwd(dkvn, x_kv, rkv, kv_norm, name="norm_kv_b", add=dx)
        tok = on_backward('end', l, dx, as3d(gl))

    small = {
        'norm_gains': jnp.stack([jnp.concatenate(d_ng[l], axis=0) for l in range(DEPTH)]),
        'mem_norm': jnp.concatenate(d_memnorm, axis=0),
        'kv_norm': d_kvnorm.reshape(D_MODEL),
        'pool_scale': jnp.concatenate(d_pscale, axis=0),
        'w_pool': jnp.stack([jnp.stack([d_wbd[l][gi * POOL_GROUP:(gi + 1) * POOL_GROUP, gi * POOL_GROUP:(gi + 1) * POOL_GROUP]
                                        for gi in range(len(POOL_WINDOWS))]) for l in range(N_A)]),
    }
    return loss, dx, small


SMALL_ORDER = ('norm_gains', 'mem_norm', 'kv_norm', 'pool_scale', 'w_pool')
SMALL_VEC_ROWS = 2560


def kernel(x, mem, positions, norm_gains, mem_norm, w_in, w_mem_kv, w_out, w_pool, pool_scale, kv_norm, w_kv, w_gate_up, w_down, loss_target, m_norm_gains, m_mem_norm, m_w_in, m_w_mem_kv, m_w_out, m_w_pool, m_pool_scale, m_kv_norm, m_w_kv, m_w_gate_up, m_w_down, v_norm_gains, v_mem_norm, v_w_in, v_w_mem_kv, v_w_out, v_w_pool, v_pool_scale, v_kv_norm, v_w_kv, v_w_gate_up, v_w_down):
    xi, yi, ci = lax.axis_index("x"), lax.axis_index("y"), lax.axis_index("c")
    s = 2 * xi + yi
    sc = jnp.stack([s, ci]).astype(jnp.int32)
    weights = dict(norm_gains=norm_gains, mem_norm=mem_norm, w_in=w_in, w_mem_kv=w_mem_kv, w_out=w_out, w_pool=w_pool,
                   pool_scale=pool_scale, kv_norm=kv_norm, w_kv=w_kv, w_gate_up=w_gate_up, w_down=w_down)
    moms = dict(norm_gains=m_norm_gains, mem_norm=m_mem_norm, w_in=m_w_in, w_mem_kv=m_w_mem_kv, w_out=m_w_out,
                w_pool=m_w_pool, pool_scale=m_pool_scale, kv_norm=m_kv_norm, w_kv=m_w_kv, w_gate_up=m_w_gate_up,
                w_down=m_w_down)
    vels = dict(norm_gains=v_norm_gains, mem_norm=v_mem_norm, w_in=v_w_in, w_mem_kv=v_w_mem_kv, w_out=v_w_out,
                w_pool=v_w_pool, pool_scale=v_pool_scale, kv_norm=v_kv_norm, w_kv=v_w_kv, w_gate_up=v_w_gate_up,
                w_down=v_w_down)

    small_w = jnp.zeros((SMALL_ROWS, 256), F32)
    small_w = lax.dynamic_update_slice(small_w, norm_gains.reshape(16, 256), (0, 0))
    small_w = lax.dynamic_update_slice(small_w, pool_scale, (16, 0))
    def shard_of(nm, l):
        return w_kv.astype(BF).reshape(_shard_shape('w_kv')) if nm == 'w_kv' else weights[nm][l:l + 1].astype(BF)

    groups = {'l0a': (0, MIX_W), 'l0b': (0, FFN_W)}
    groups.update({f"l{l}": (l, LAYER_W + (('w_kv',) if l == N_A - 1 else ())) for l in range(1, DEPTH)})
    on_ici, on_d2d, gathered = {}, {}, {}

    def start_group(tag, after):
        l, names = groups[tag]
        on_ici[tag] = _gather_start(tag, names, [shard_of(nm, l) for nm in names], small_w if tag == 'l0a' else None, sc,
                                    after)
        return on_ici[tag]['token'][0, 0]

    def on_forward(where, l, after):
        if where == 'start':
            if l == 0:
                start_group('l0a', None)
                st = on_ici.pop('l0a')
                fwd = _gather_forward(st, st['token'])
                w, small_all = _gather_finish(fwd, fwd['token'])
                return w, small_all, start_group('l0b', w['w_in'])
            gathered[l] = _gather_finish(on_d2d.pop(f"l{l}"), after)[0]
            tok = start_group(f"l{l + 1}", gathered[l]['w_in']) if l + 1 < DEPTH else None
            return {nm: w for nm, w in gathered[l].items() if nm not in FFN_W}, None, tok
        if where == 'mid' and l == 0:
            on_d2d['l0b'] = _gather_forward(on_ici.pop('l0b'), after)
            return start_group('l1', on_d2d['l0b']['token'])
        if where == 'ffn':
            if l == 0:
                return (_gather_finish(on_d2d.pop('l0b'), after)[0],)
            return ({nm: gathered[l][nm] for nm in FFN_W},)
        if where == 'post' and l + 1 < DEPTH:
            on_d2d[f"l{l + 1}"] = _gather_forward(on_ici.pop(f"l{l + 1}"), after)
        return None

    hook_of = {'ffn': 0, 'mix': 1, 'end': 2}
    active, reduced = [], {l: {} for l in range(DEPTH)}
    advance = {'mid': lambda st, after: _reduce_mid(st, after, sc), 'late': lambda st, after: _reduce_late(st, after, sc)}

    def run_hook(idx, after):
        toks = []
        for grp in list(active):
            while grp['plan'] and grp['plan'][0][1] <= idx:
                step = grp['plan'].pop(0)[0]
                if step == 'finish':
                    reduced[grp['layer']].update(_reduce_finish(grp['st'], after))
                    active.remove(grp)
                else:
                    grp['st'] = advance[step](grp['st'], after)
                    toks.append(grp['st']['token'][0, 0])
        return toks

    def on_backward(where, l, after, grads):
        idx = 3 * (DEPTH - 1 - l) + hook_of[where]
        toks = run_hook(idx, after)
        if where in ('ffn', 'end'):
            names = FFN_W if where == 'ffn' else tuple(nm for nm in grads if nm not in FFN_W)
            st = _reduce_start(f"l{l}_{where}_grads", names, {nm: grads[nm] for nm in names})
            plan = [('mid', idx + 1), ('late', idx + 3), ('finish', idx + 4)] if where == 'ffn' else \
                   [('mid', idx + 1), ('late', idx + 2), ('finish', idx + 3)]
            active.append(dict(layer=l, st=st, plan=plan))
            toks.append(st['token'][0, 0])
        return sum(toks) if toks else None

    loss, gx, gsmall = _local_step(x, mem, positions, on_forward, on_backward, mem_norm, w_pool, kv_norm, loss_target)
    loss = lax.psum(loss[0, 0], ("x", "y", "c"))

    vec = jnp.concatenate([gsmall[nm].reshape(-1) for nm in SMALL_ORDER])
    vec = jnp.pad(vec, (0, SMALL_VEC_ROWS * 128 - vec.shape[0])).reshape(SMALL_VEC_ROWS, 128)
    vec = vec + sum(grp['st']['token'][0, 0] for grp in active)
    small_st = _small_gather_start(vec, sc)
    outs = {nm: None for nm in LAYER_W}

    def adamw_layers(layers, names, after):
        for l in layers:
            for nm in names:
                outs[nm] = _adamw_layer(nm, l, weights[nm], reduced[l][nm], moms[nm], vels[nm], outs[nm], after)
        return outs[names[-1]][0]

    def zero_of(toks, st):
        return jnp.full((8, 128), sum(toks)) if toks else st['token']

    last = 3 * DEPTH
    toks = run_hook(last, small_st['token'])
    done = adamw_layers(range(DEPTH - 1, 0, -1), LAYER_W, zero_of(toks, small_st))
    toks = run_hook(last + 1, done)
    done = adamw_layers([0], FFN_W, zero_of(toks, small_st))
    tot = _sum8(_small_gather_finish(small_st, done), name="sum_small_grads", tr=512)
    run_hook(last + 2, tot)
    assert not active
    adamw_layers([0], MIX_W, None)
    tot = tot.reshape(-1)
    grads, off = {}, 0
    for nm in SMALL_ORDER:
        shape = (DEPTH, 4, D_MODEL) if nm == 'norm_gains' else (N_A, MAIN_W) if nm == 'pool_scale' else weights[nm].shape
        n = 1
        for dim in shape:
            n *= dim
        grads[nm] = tot[off:off + n].reshape(shape)
        off += n
    grads['norm_gains'] = lax.dynamic_slice(grads['norm_gains'], (0, 0, s * 256), (DEPTH, 4, 256))
    grads['pool_scale'] = lax.dynamic_slice(grads['pool_scale'], (0, s * POOL_GROUP), (N_A, POOL_GROUP))
    grads['w_kv'] = reduced[N_A]['w_kv'].reshape(w_kv.shape)

    order = ('norm_gains', 'mem_norm', 'w_in', 'w_mem_kv', 'w_out', 'w_pool', 'pool_scale', 'kv_norm', 'w_kv',
             'w_gate_up', 'w_down')
    deltas, new_m, new_v = {}, {}, {}
    for nm in order:
        if nm in LAYER_W:
            deltas[nm], new_m[nm], new_v[nm], grads[nm] = outs[nm]
        else:
            deltas[nm], new_m[nm], new_v[nm] = _adamw(weights[nm], grads[nm], moms[nm], vels[nm], name="adamw_" + nm)
    return (loss, gx.reshape(x.shape), *[grads[nm] for nm in order], *[deltas[nm] for nm in order],
            *[new_m[nm] for nm in order], *[new_v[nm] for nm in order])
```

```python
import functools

import jax
import jax.numpy as jnp
from jax import lax
from jax.experimental import pallas as pl
from jax.experimental.pallas import tpu as pltpu

F32 = jnp.float32
BF = jnp.bfloat16

D_MODEL = 1024
DEPTH = 4
N_A = 2
HEAD_DIM = 64
MEM_W = 256
MAIN_W = 768
D_FF = 2816
N_MEM = 256
POOL_WINDOWS = (2, 4, 8, 16)
POOL_GROUP = 192
DIL = (1, 4, 16)
STEPS = 128
ROPE_THETA = 10000.0
EPS = 1e-6
SCALE = HEAD_DIM ** -0.5
NEG = -1e30

ADAM_LR = 0.001
ADAM_B1 = 0.9
ADAM_B2 = 0.999
ADAM_EPS = 1e-08
ADAM_WD = 0.01
ADAM_STEP = 10

VMEM_LIMIT = 48 * 1024 * 1024
MESH = pl.DeviceIdType.MESH


def _cp(sem):
    return pltpu.CompilerParams(dimension_semantics=sem, vmem_limit_bytes=VMEM_LIMIT)


def _mm(a, b, *, name, ta=False, tb=False, tm=1024, tn=512, tk=1024, b_layer=None, b_offsets=(0,),
        extras=(), epilogue=None, out_dtypes=(F32,), out_n=None, stack=None):
    a_pair = isinstance(a, (tuple, list))
    b_pair = isinstance(b, (tuple, list))
    a0 = a[0] if a_pair else a
    b0 = b[0] if b_pair else b
    a_rows, a_cols = a0.shape
    if a_pair:
        a_cols *= 2
    b_rows, b_cols = b0.shape[-2:]
    if b_pair:
        b_cols *= 2
    M, K = (a_cols, a_rows) if ta else (a_rows, a_cols)
    N = b_rows if tb else b_cols
    if out_n is not None:
        N = out_n
    tm, tn, tk = min(tm, M), min(tn, N), min(tk, K)
    assert M % tm == 0 and N % tn == 0 and K % tk == 0, (name, M, N, K, tm, tn, tk)
    nk = K // tk
    n_acc = len(b_offsets)

    if a_pair:
        a_half = (a0.shape[1] // (tm if ta else tk))
    if b_pair:
        b_half = (b0.shape[1] // (tk if tb else tn))

    def a_map(sel):
        def f(i, j, k):
            r, c = (k, i) if ta else (i, k)
            if a_pair:
                c = jnp.clip(c - sel * a_half, 0, a_half - 1)
            return (r, c)
        return f

    def b_map(sel, off):
        def f(i, j, k):
            r, c = (j + off, k) if tb else (k, j + off)
            if b_pair:
                c = jnp.clip(c - sel * b_half, 0, b_half - 1)
            if b_layer is not None:
                return (b_layer, r, c)
            return (r, c)
        return f

    a_blk = (tk, tm) if ta else (tm, tk)
    b_blk = (tn, tk) if tb else (tk, tn)
    if b_layer is not None:
        b_blk = (None,) + b_blk
    in_specs, operands = [], []
    for sel in range(2 if a_pair else 1):
        in_specs.append(pl.BlockSpec(a_blk, a_map(sel)))
        operands.append(a[sel] if a_pair else a)
    n_a = len(operands)
    for off in b_offsets:
        for sel in range(2 if b_pair else 1):
            in_specs.append(pl.BlockSpec(b_blk, b_map(sel, off)))
            operands.append(b[sel] if b_pair else b)
    n_b = len(operands) - n_a
    for arr, kind in extras:
        if kind == 'tile':
            in_specs.append(pl.BlockSpec((tm, tn), lambda i, j, k: (i, j)))
        elif kind == 'row':
            in_specs.append(pl.BlockSpec((tm, 1), lambda i, j, k: (i, 0)))
        else:
            in_specs.append(pl.BlockSpec((1, tn), lambda i, j, k: (0, j)))
        operands.append(arr)
    n_e = len(extras)
    n_o = len(out_dtypes)
    dims = (((0,) if ta else (1,), (1,) if tb else (0,)), ((), ()))

    def body(*refs):
        a_refs = refs[:n_a]
        b_refs = refs[n_a:n_a + n_b]
        e_refs = refs[n_a + n_b:n_a + n_b + n_e]
        n_in = n_a + n_b + n_e + (1 if stack is not None else 0)
        o_refs = refs[n_in:n_in + n_o]
        acc_refs = refs[n_in + n_o:]
        i, j, k = pl.program_id(0), pl.program_id(1), pl.program_id(2)
        if a_pair:
            cidx = i if ta else k
            av = jnp.where(cidx < a_half, a_refs[0][...], a_refs[1][...])
        else:
            av = a_refs[0][...]
        av = av.astype(BF)
        prods = []
        for q in range(n_acc):
            if b_pair:
                cidx = (k if tb else j) + b_offsets[q]
                bv = jnp.where(cidx < b_half, b_refs[2 * q][...], b_refs[2 * q + 1][...])
            else:
                bv = b_refs[q][...]
            prods.append(lax.dot_general(av, bv.astype(BF), dims, preferred_element_type=F32))

        def finish(accs):
            outs = epilogue(accs, *[r[...] for r in e_refs]) if epilogue is not None else accs
            for o_ref, o in zip(o_refs, outs):
                o_ref[...] = o.astype(o_ref.dtype)

        if nk == 1:
            finish(prods)
        else:
            @pl.when(k == 0)
            def _():
                for r, p in zip(acc_refs, prods):
                    r[...] = p

            @pl.when(k > 0)
            def _():
                for r, p in zip(acc_refs, prods):
                    r[...] += p

            @pl.when(k == nk - 1)
            def _():
                finish([r[...] for r in acc_refs])

    if stack is not None:
        buf, layer = stack
        assert n_o == 1 and buf.shape[1:] == (M, N)
        return pl.pallas_call(
            body, name=name,
            grid=(M // tm, N // tn, nk),
            in_specs=in_specs + [pl.BlockSpec(memory_space=pl.ANY)],
            out_specs=[pl.BlockSpec((None, tm, tn), lambda i, j, k: (layer, i, j))],
            out_shape=[jax.ShapeDtypeStruct(buf.shape, buf.dtype)],
            scratch_shapes=[pltpu.VMEM((tm, tn), F32) for _ in range(n_acc if nk > 1 else 0)],
            input_output_aliases={len(operands): 0},
            compiler_params=_cp(("parallel", "parallel", "arbitrary")),
        )(*operands, buf)[0]
    return pl.pallas_call(
        body, name=name,
        grid=(M // tm, N // tn, nk),
        in_specs=in_specs,
        out_specs=[pl.BlockSpec((tm, tn), lambda i, j, k: (i, j)) for _ in range(n_o)],
        out_shape=[jax.ShapeDtypeStruct((M, N), dt) for dt in out_dtypes],
        scratch_shapes=[pltpu.VMEM((tm, tn), F32) for _ in range(n_acc if nk > 1 else 0)],
        compiler_params=_cp(("parallel", "parallel", "arbitrary")),
    )(*operands)


def _norm_fwd(x, g, *, name, res=None, out_dtype=F32, tm=512):
    T, Dm = x.shape
    has_res = res is not None

    def body(*refs):
        if has_res:
            x_ref, g_ref, r_ref, y_ref, s_ref = refs
        else:
            x_ref, g_ref, y_ref, s_ref = refs
        xv = x_ref[...]
        rstd = lax.rsqrt(jnp.mean(xv * xv, axis=-1, keepdims=True) + EPS)
        y = xv * rstd * g_ref[...]
        if has_res:
            y = r_ref[...] + y
        y_ref[...] = y.astype(y_ref.dtype)
        s_ref[...] = rstd

    row = pl.BlockSpec((tm, Dm), lambda i: (i, 0))
    in_specs = [row, pl.BlockSpec((1, Dm), lambda i: (0, 0))] + ([row] if has_res else [])
    ops = [x, g.reshape(1, Dm)] + ([res] if has_res else [])
    return pl.pallas_call(
        body, name=name, grid=(T // tm,), in_specs=in_specs,
        out_specs=[row, pl.BlockSpec((tm, 1), lambda i: (i, 0))],
        out_shape=[jax.ShapeDtypeStruct((T, Dm), out_dtype), jax.ShapeDtypeStruct((T, 1), F32)],
        compiler_params=_cp(("parallel",)),
    )(*ops)


def _norm_bwd(dout, x, rstd, g, *, name, add=None, out_dtype=F32, tm=512):
    T, Dm = x.shape
    has_add = add is not None
    nt = T // tm

    def body(*refs):
        if has_add:
            do_ref, x_ref, s_ref, g_ref, a_ref, dx_ref, dg_ref, acc = refs
        else:
            do_ref, x_ref, s_ref, g_ref, dx_ref, dg_ref, acc = refs
        i = pl.program_id(0)
        do = do_ref[...].astype(F32)
        xh = x_ref[...] * s_ref[...]
        gd = do * g_ref[...]
        dx = s_ref[...] * (gd - xh * jnp.mean(gd * xh, axis=-1, keepdims=True))
        if has_add:
            dx = dx + a_ref[...].astype(F32)
        dx_ref[...] = dx.astype(dx_ref.dtype)
        part = jnp.sum((do * xh).reshape(tm // 8, 8, Dm), axis=0)

        @pl.when(i == 0)
        def _():
            acc[...] = part

        @pl.when(i > 0)
        def _():
            acc[...] += part

        @pl.when(i == nt - 1)
        def _():
            dg_ref[...] = jnp.sum(acc[...], axis=0, keepdims=True)

    row = pl.BlockSpec((tm, Dm), lambda i: (i, 0))
    in_specs = [row, row, pl.BlockSpec((tm, 1), lambda i: (i, 0)), pl.BlockSpec((1, Dm), lambda i: (0, 0))]
    ops = [dout, x, rstd, g.reshape(1, Dm)]
    if has_add:
        in_specs.append(row)
        ops.append(add)
    return pl.pallas_call(
        body, name=name, grid=(nt,), in_specs=in_specs,
        out_specs=[row, pl.BlockSpec((1, Dm), lambda i: (0, 0))],
        out_shape=[jax.ShapeDtypeStruct((T, Dm), out_dtype), jax.ShapeDtypeStruct((1, Dm), F32)],
        scratch_shapes=[pltpu.VMEM((8, Dm), F32)],
        compiler_params=_cp(("arbitrary",)),
    )(*ops)


def _swiglu_fwd_epilogue(accs):
    g, u = accs
    return g, u, g * jax.nn.sigmoid(g) * u


def _swiglu_bwd_epilogue(accs, g, u):
    da = accs[0]
    g = g.astype(F32)
    u = u.astype(F32)
    sig = jax.nn.sigmoid(g)
    return da * u * (sig * (1.0 + g * (1.0 - sig))), da * (g * sig)


def _rope_tables(pos, *, name, tm=1024):
    T = pos.shape[0]
    half = HEAD_DIM // 2
    freqs = ROPE_THETA ** (-jnp.arange(half, dtype=F32) / half)
    freqs = jnp.tile(freqs, 4).reshape(1, 128)

    def body(p_ref, f_ref, c_ref, s_ref):
        ang = p_ref[...].astype(F32) * f_ref[...]
        lane = lax.broadcasted_iota(jnp.int32, ang.shape, 1)
        c_ref[...] = jnp.cos(ang)
        s_ref[...] = jnp.where(lane % HEAD_DIM < half, -1.0, 1.0) * jnp.sin(ang)

    tab = pl.BlockSpec((tm, 128), lambda i: (i, 0))
    return pl.pallas_call(
        body, name=name, grid=(T // tm,),
        in_specs=[pl.BlockSpec((tm, 1), lambda i: (i, 0)), pl.BlockSpec((1, 128), lambda i: (0, 0))],
        out_specs=[tab, tab],
        out_shape=[jax.ShapeDtypeStruct((T, 128), F32)] * 2,
        compiler_params=_cp(("parallel",)),
    )(pos, freqs)


def _rot(x, cos, sin, sign):
    W = x.shape[1]
    half = HEAD_DIM // 2
    reps = W // 128
    c = jnp.concatenate([cos] * reps, axis=1) if reps > 1 else cos
    s = jnp.concatenate([sin] * reps, axis=1) if reps > 1 else sin
    lane = lax.broadcasted_iota(jnp.int32, x.shape, 1)
    swapped = jnp.where(lane % HEAD_DIM < half, pltpu.roll(x, W - half, axis=1), pltpu.roll(x, half, axis=1))
    return x * c + (sign * s) * swapped


def _rope_apply(x, cos, sin, *, name, sign=1.0, width=MAIN_W, passthrough=False, out_dtype=BF, alias=None,
                out_cols=None, tm=512):
    T = x.shape[0]

    def body(*refs):
        if passthrough:
            x_ref, v_ref, c_ref, s_ref, o_ref, ov_ref = refs
            ov_ref[...] = v_ref[...].astype(ov_ref.dtype)
        elif alias is not None:
            x_ref, c_ref, s_ref, _, o_ref = refs
        else:
            x_ref, c_ref, s_ref, o_ref = refs
        o_ref[...] = _rot(x_ref[...].astype(F32), c_ref[...], s_ref[...], sign).astype(o_ref.dtype)

    blk0 = pl.BlockSpec((tm, width), lambda i: (i, 0))
    blk1 = pl.BlockSpec((tm, width), lambda i: (i, 1))
    tab = pl.BlockSpec((tm, 128), lambda i: (i, 0))
    if passthrough:
        return pl.pallas_call(
            body, name=name, grid=(T // tm,), in_specs=[blk0, blk1, tab, tab], out_specs=[blk0, blk0],
            out_shape=[jax.ShapeDtypeStruct((T, width), out_dtype)] * 2,
            compiler_params=_cp(("parallel",)),
        )(x, x, cos, sin)
    if alias is not None:
        return pl.pallas_call(
            body, name=name, grid=(T // tm,),
            in_specs=[blk0, tab, tab, pl.BlockSpec(memory_space=pl.ANY)], out_specs=blk0,
            out_shape=jax.ShapeDtypeStruct(alias.shape, alias.dtype),
            input_output_aliases={3: 0},
            compiler_params=_cp(("parallel",)),
        )(x, cos, sin, alias)
    return pl.pallas_call(
        body, name=name, grid=(T // tm,), in_specs=[blk0, tab, tab], out_specs=blk0,
        out_shape=jax.ShapeDtypeStruct((T, width), out_dtype),
        compiler_params=_cp(("parallel",)),
    )(x, cos, sin)


POOL_T = 256
POOL_HALO = 16


def _pool_lane_window(shape):
    lane = lax.broadcasted_iota(jnp.int32, shape, 1)
    w = jnp.full(shape, POOL_WINDOWS[0], jnp.int32)
    for gi in range(1, len(POOL_WINDOWS)):
        w = jnp.where(lane >= gi * POOL_GROUP, POOL_WINDOWS[gi], w)
    return w


def _pool_fwd(z, wbd, scale, B, S, *, name):
    T = z.shape[0]
    nt = S // POOL_T
    hb = POOL_T // POOL_HALO

    def body(z_ref, h_ref, w_ref, sc_ref, y_ref, p_ref, ext):
        i = pl.program_id(1)
        u = z_ref[...]
        ext[pl.ds(POOL_HALO, POOL_T), :] = u
        ext[pl.ds(0, POOL_HALO), :] = jnp.where(i > 0, h_ref[...], 0.0)
        win = _pool_lane_window((POOL_T, MAIN_W))
        acc = u
        for k in range(1, POOL_HALO):
            acc = acc + jnp.where(k < win, ext[pl.ds(POOL_HALO - k, POOL_T), :], 0.0)
        t = i * POOL_T + lax.broadcasted_iota(jnp.int32, (POOL_T, MAIN_W), 0)
        cnt = jnp.minimum(t + 1, win).astype(F32)
        p = (acc / cnt - u).astype(BF)
        p_ref[...] = p
        y = jnp.dot(p, w_ref[...], preferred_element_type=F32) * sc_ref[...]
        y_ref[...] = y.astype(y_ref.dtype)

    return pl.pallas_call(
        body, name=name, grid=(B, nt),
        in_specs=[pl.BlockSpec((POOL_T, MAIN_W), lambda b, i: (b * nt + i, 0)),
                  pl.BlockSpec((POOL_HALO, MAIN_W), lambda b, i: (jnp.maximum((b * nt + i) * hb - 1, 0), 0)),
                  pl.BlockSpec((MAIN_W, MAIN_W), lambda b, i: (0, 0)),
                  pl.BlockSpec((1, MAIN_W), lambda b, i: (0, 0))],
        out_specs=[pl.BlockSpec((POOL_T, MAIN_W), lambda b, i: (b * nt + i, 0)),
                   pl.BlockSpec((POOL_T, MAIN_W), lambda b, i: (b * nt + i, 0))],
        out_shape=[jax.ShapeDtypeStruct((T, D_MODEL), BF), jax.ShapeDtypeStruct((T, MAIN_W), BF)],
        scratch_shapes=[pltpu.VMEM((POOL_T + POOL_HALO, MAIN_W), F32)],
        compiler_params=_cp(("parallel", "parallel")),
    )(z, z, wbd, scale.reshape(1, MAIN_W))


def _pool_bwd(dy, p, wbd, scale, dz_alias, B, S, *, name):
    T = dy.shape[0]
    nt = S // POOL_T
    hb = POOL_T // POOL_HALO
    last_halo = T // POOL_HALO - 1
    R = POOL_T + POOL_HALO

    def body(dy_ref, dyn_ref, p_ref, pn_ref, w_ref, sc_ref, _, dz_ref, dw_ref, ds_ref, ext, dw_acc, ds_acc):
        b, i = pl.program_id(0), pl.program_id(1)
        first = jnp.logical_and(b == 0, i == 0)
        dyv = dy_ref[...]
        pv = p_ref[...]
        sc = sc_ref[...]
        w = w_ref[...]
        pw = jnp.dot(pv, w, preferred_element_type=F32)
        ds_part = jnp.sum((dyv * pw).reshape(POOL_T // 8, 8, MAIN_W), axis=0)
        dpw = (dyv * sc).astype(BF)
        dw_part = lax.dot_general(pv, dpw, (((0,), (0,)), ((), ())), preferred_element_type=F32)

        @pl.when(first)
        def _():
            dw_acc[...] = dw_part
            ds_acc[...] = ds_part

        @pl.when(jnp.logical_not(first))
        def _():
            dw_acc[...] += dw_part
            ds_acc[...] += ds_part

        @pl.when(jnp.logical_and(b == pl.num_programs(0) - 1, i == nt - 1))
        def _():
            dw_ref[...] = dw_acc[...]
            ds_ref[...] = jnp.sum(ds_acc[...], axis=0, keepdims=True)

        dp = lax.dot_general(dpw, w, (((1,), (1,)), ((), ())), preferred_element_type=F32)
        dpn = lax.dot_general((dyn_ref[...] * sc).astype(BF), w, (((1,), (1,)), ((), ())), preferred_element_type=F32)
        win = _pool_lane_window((POOL_T, MAIN_W))
        win_n = _pool_lane_window((POOL_HALO, MAIN_W))
        t = i * POOL_T + lax.broadcasted_iota(jnp.int32, (POOL_T, MAIN_W), 0)
        tn = (i + 1) * POOL_T + lax.broadcasted_iota(jnp.int32, (POOL_HALO, MAIN_W), 0)
        ext[pl.ds(0, POOL_T), :] = dp / jnp.minimum(t + 1, win).astype(F32)
        ext[pl.ds(POOL_T, POOL_HALO), :] = jnp.where(i < nt - 1, dpn / jnp.minimum(tn + 1, win_n).astype(F32), 0.0)
        acc = -dp
        for k in range(POOL_HALO):
            acc = acc + jnp.where(k < win, ext[pl.ds(k, POOL_T), :], 0.0)
        dz_ref[...] = acc.astype(dz_ref.dtype)

    cur = lambda b, i: (b * nt + i, 0)
    nxt = lambda b, i: (jnp.minimum((b * nt + i + 1) * hb, last_halo), 0)
    return pl.pallas_call(
        body, name=name, grid=(B, nt),
        in_specs=[pl.BlockSpec((POOL_T, MAIN_W), cur), pl.BlockSpec((POOL_HALO, MAIN_W), nxt),
                  pl.BlockSpec((POOL_T, MAIN_W), cur), pl.BlockSpec((POOL_HALO, MAIN_W), nxt),
                  pl.BlockSpec((MAIN_W, MAIN_W), lambda b, i: (0, 0)),
                  pl.BlockSpec((1, MAIN_W), lambda b, i: (0, 0)),
                  pl.BlockSpec(memory_space=pl.ANY)],
        out_specs=[pl.BlockSpec((POOL_T, MAIN_W), cur),
                   pl.BlockSpec((MAIN_W, MAIN_W), lambda b, i: (0, 0)),
                   pl.BlockSpec((1, MAIN_W), lambda b, i: (0, 0))],
        out_shape=[jax.ShapeDtypeStruct(dz_alias.shape, dz_alias.dtype),
                   jax.ShapeDtypeStruct((MAIN_W, MAIN_W), F32), jax.ShapeDtypeStruct((1, MAIN_W), F32)],
        scratch_shapes=[pltpu.VMEM((R, MAIN_W), F32), pltpu.VMEM((MAIN_W, MAIN_W), F32), pltpu.VMEM((8, MAIN_W), F32)],
        input_output_aliases={6: 0},
        compiler_params=_cp(("arbitrary", "arbitrary")),
    )(dy, dy, p, p, wbd, scale.reshape(1, MAIN_W), dz_alias)


def _head_masks(shape):
    lane = lax.broadcasted_iota(jnp.int32, shape, 1)
    return [(lane // HEAD_DIM) == h for h in range(shape[1] // HEAD_DIM)]


def _row_of(bcast, mask):
    return jnp.max(jnp.where(mask, bcast, -jnp.inf), axis=-1, keepdims=True)


MEM_TQ = 512


def _memattn_fwd(z, kv, y_alias, B, S, *, name):
    T = z.shape[0]
    nt = S // MEM_TQ

    def body(q_ref, k_ref, v_ref, _, y_ref, l_ref):
        q = q_ref[...]
        k = k_ref[...]
        v = v_ref[...]
        masks = _head_masks(q.shape)
        o = jnp.zeros(q.shape, F32)
        lse_b = jnp.zeros(q.shape, F32)
        for m in masks:
            qm = jnp.where(m, q, 0.0).astype(BF)
            s = lax.dot_general(qm, k, (((1,), (1,)), ((), ())), preferred_element_type=F32) * SCALE
            mx = jnp.max(s, axis=-1, keepdims=True)
            e = jnp.exp(s - mx)
            l = jnp.sum(e, axis=-1, keepdims=True)
            p = (e / l).astype(BF)
            o = o + jnp.where(m, jnp.dot(p, v, preferred_element_type=F32), 0.0)
            lse_b = lse_b + jnp.where(m, mx + jnp.log(l), 0.0)
        y_ref[...] = o.astype(y_ref.dtype)
        l_ref[...] = lse_b

    qblk = pl.BlockSpec((MEM_TQ, MEM_W), lambda b, i: (b * nt + i, 3))
    return pl.pallas_call(
        body, name=name, grid=(B, nt),
        in_specs=[qblk, pl.BlockSpec((N_MEM, MEM_W), lambda b, i: (b, 0)), pl.BlockSpec((N_MEM, MEM_W), lambda b, i: (b, 1)),
                  pl.BlockSpec(memory_space=pl.ANY)],
        out_specs=[qblk, pl.BlockSpec((MEM_TQ, MEM_W), lambda b, i: (b * nt + i, 0))],
        out_shape=[jax.ShapeDtypeStruct(y_alias.shape, y_alias.dtype), jax.ShapeDtypeStruct((T, MEM_W), F32)],
        input_output_aliases={3: 0},
        compiler_params=_cp(("parallel", "parallel")),
    )(z, kv, kv, y_alias)


def _memattn_bwd(dy, z, kv, lse, dz_alias, B, S, *, name):
    nt = S // MEM_TQ

    def body(do_ref, q_ref, k_ref, v_ref, l_ref, _, dz_ref, dk_ref, dv_ref, dk_acc, dv_acc):
        i = pl.program_id(1)
        do = do_ref[...]
        q = q_ref[...]
        k = k_ref[...]
        v = v_ref[...]
        lse_b = l_ref[...]
        masks = _head_masks(q.shape)
        dq = jnp.zeros(q.shape, F32)
        dk = jnp.zeros(k.shape, F32)
        dv = jnp.zeros(v.shape, F32)
        for m in masks:
            qm = jnp.where(m, q, 0.0).astype(BF)
            dom = jnp.where(m, do, 0.0).astype(BF)
            s = lax.dot_general(qm, k, (((1,), (1,)), ((), ())), preferred_element_type=F32) * SCALE
            p = jnp.exp(s - _row_of(lse_b, m))
            dp = lax.dot_general(dom, v, (((1,), (1,)), ((), ())), preferred_element_type=F32)
            delta = jnp.sum(p * dp, axis=-1, keepdims=True)
            ds = (p * (dp - delta) * SCALE).astype(BF)
            pb = p.astype(BF)
            dv = dv + jnp.where(m[:N_MEM], lax.dot_general(pb, dom, (((0,), (0,)), ((), ())), preferred_element_type=F32), 0.0)
            dk = dk + jnp.where(m[:N_MEM], lax.dot_general(ds, qm, (((0,), (0,)), ((), ())), preferred_element_type=F32), 0.0)
            dq = dq + jnp.where(m, jnp.dot(ds, k, preferred_element_type=F32), 0.0)
        dz_ref[...] = dq.astype(dz_ref.dtype)

        @pl.when(i == 0)
        def _():
            dk_acc[...] = dk
            dv_acc[...] = dv

        @pl.when(i > 0)
        def _():
            dk_acc[...] += dk
            dv_acc[...] += dv

        @pl.when(i == nt - 1)
        def _():
            dk_ref[...] = dk_acc[...]
            dv_ref[...] = dv_acc[...]

    qblk = pl.BlockSpec((MEM_TQ, MEM_W), lambda b, i: (b * nt + i, 3))
    kblk = pl.BlockSpec((N_MEM, MEM_W), lambda b, i: (b, 0))
    return pl.pallas_call(
        body, name=name, grid=(B, nt),
        in_specs=[qblk, qblk, kblk, pl.BlockSpec((N_MEM, MEM_W), lambda b, i: (b, 1)),
                  pl.BlockSpec((MEM_TQ, MEM_W), lambda b, i: (b * nt + i, 0)), pl.BlockSpec(memory_space=pl.ANY)],
        out_specs=[qblk, kblk, kblk],
        out_shape=[jax.ShapeDtypeStruct(dz_alias.shape, dz_alias.dtype),
                   jax.ShapeDtypeStruct((B * N_MEM, MEM_W), F32), jax.ShapeDtypeStruct((B * N_MEM, MEM_W), F32)],
        scratch_shapes=[pltpu.VMEM((N_MEM, MEM_W), F32), pltpu.VMEM((N_MEM, MEM_W), F32)],
        input_output_aliases={5: 0},
        compiler_params=_cp(("parallel", "arbitrary")),
    )(dy, z, kv, kv, lse, dz_alias)


def _dil_scores(qm, kp, kc, n):
    qi = lax.broadcasted_iota(jnp.int32, (STEPS, STEPS), 0)
    kj = lax.broadcasted_iota(jnp.int32, (STEPS, STEPS), 1)
    sc = lax.dot_general(qm, kc, (((1,), (1,)), ((), ())), preferred_element_type=F32) * SCALE
    sc = jnp.where(kj <= qi, sc, NEG)
    if kp is None:
        return None, sc
    sp = lax.dot_general(qm, kp, (((1,), (1,)), ((), ())), preferred_element_type=F32) * SCALE
    sp = jnp.where(jnp.logical_and(kj >= qi, n > 0), sp, NEG)
    return sp, sc


def _dil_specs(g, d, nb):
    chunk = STEPS * d
    cur = pl.BlockSpec((chunk, 128), lambda b, n, hf: (b * nb + n, g * 2 + hf))
    prev = pl.BlockSpec((chunk, 128), lambda b, n, hf: (b * nb + jnp.maximum(n - 1, 0), g * 2 + hf))
    return cur, prev


def _dil_rows(r, d):
    return pl.ds(r, STEPS, stride=d) if d > 1 else slice(None)


def _dil_loop(d, fn):
    if d <= 4:
        for r in range(d):
            fn(r)
    else:
        lax.fori_loop(0, d, lambda r, carry: (fn(r), carry)[1], 0)


def _dil_fwd_group(g, q, k, v, o_alias, l_alias, B, S, *, name):
    d = DIL[g]
    nb = S // (STEPS * d)
    has_prev = nb > 1

    def body(*refs):
        if has_prev:
            q_ref, kp_ref, kc_ref, vp_ref, vc_ref, _, __, o_ref, l_ref = refs
        else:
            q_ref, kc_ref, vc_ref, _, __, o_ref, l_ref = refs
        n = pl.program_id(1)

        def residue(r):
            rows = _dil_rows(r, d)
            q = q_ref[rows, :]
            kc, vc = kc_ref[rows, :].astype(BF), vc_ref[rows, :].astype(BF)
            kp = kp_ref[rows, :].astype(BF) if has_prev else None
            vp = vp_ref[rows, :].astype(BF) if has_prev else None
            o = jnp.zeros(q.shape, F32)
            lse_b = jnp.zeros(q.shape, F32)
            for m in _head_masks(q.shape):
                qm = jnp.where(m, q, 0.0).astype(BF)
                sp, sc = _dil_scores(qm, kp, kc, n)
                mx = jnp.max(sc, axis=-1, keepdims=True)
                if has_prev:
                    mx = jnp.maximum(mx, jnp.max(sp, axis=-1, keepdims=True))
                l = jnp.sum(jnp.exp(sc - mx), axis=-1, keepdims=True)
                if has_prev:
                    l = l + jnp.sum(jnp.exp(sp - mx), axis=-1, keepdims=True)
                lse = mx + jnp.log(l)
                oh = jnp.dot(jnp.exp(sc - lse).astype(BF), vc, preferred_element_type=F32)
                if has_prev:
                    oh = oh + jnp.dot(jnp.exp(sp - lse).astype(BF), vp, preferred_element_type=F32)
                o = o + jnp.where(m, oh, 0.0)
                lse_b = lse_b + jnp.where(m, lse, 0.0)
            o_ref[rows, :] = o
            l_ref[rows, :] = lse_b

        _dil_loop(d, residue)

    cur, prev = _dil_specs(g, d, nb)
    anyspec = pl.BlockSpec(memory_space=pl.ANY)
    if has_prev:
        in_specs, ops = [cur, prev, cur, prev, cur], [q, k, k, v, v]
    else:
        in_specs, ops = [cur, cur, cur], [q, k, v]
    n_in = len(ops)
    o, l = pl.pallas_call(
        body, name=name, grid=(B, nb, 2),
        in_specs=in_specs + [anyspec, anyspec],
        out_specs=[cur, cur],
        out_shape=[jax.ShapeDtypeStruct(q.shape, F32)] * 2,
        input_output_aliases={n_in: 0, n_in + 1: 1},
        compiler_params=_cp(("parallel", "parallel", "parallel")),
    )(*ops, o_alias, l_alias)
    return o, l


def _dil_bwd_group(g, q, k, v, do, cb, lse, aliases, B, S, *, name):
    d = DIL[g]
    nb = S // (STEPS * d)
    has_prev = nb > 1
    n_out = 5 if has_prev else 3

    def body(*refs):
        if has_prev:
            q_ref, kp_ref, kc_ref, vp_ref, vc_ref, do_ref, c_ref, l_ref = refs[:8]
            dq_ref, dkc_ref, dvc_ref, dkp_ref, dvp_ref = refs[8 + n_out:]
        else:
            q_ref, kc_ref, vc_ref, do_ref, c_ref, l_ref = refs[:6]
            dq_ref, dkc_ref, dvc_ref = refs[6 + n_out:]
        n = pl.program_id(1)
        tdot = lambda a, b: lax.dot_general(a, b, (((0,), (0,)), ((), ())), preferred_element_type=F32)
        ndot = lambda a, b: lax.dot_general(a, b, (((1,), (1,)), ((), ())), preferred_element_type=F32)

        def residue(r):
            rows = _dil_rows(r, d)
            q = q_ref[rows, :]
            kc, vc = kc_ref[rows, :].astype(BF), vc_ref[rows, :].astype(BF)
            kp = kp_ref[rows, :].astype(BF) if has_prev else None
            vp = vp_ref[rows, :].astype(BF) if has_prev else None
            do = do_ref[rows, :]
            cbv = c_ref[rows, :]
            lse_b = l_ref[rows, :]
            z = jnp.zeros(q.shape, F32)
            dq, dkc, dkp, dvc, dvp = z, z, z, z, z
            for m in _head_masks(q.shape):
                qm = jnp.where(m, q, 0.0).astype(BF)
                dom = jnp.where(m, do, 0.0).astype(BF)
                sp, sc = _dil_scores(qm, kp, kc, n)
                lse = _row_of(lse_b, m)
                c = _row_of(cbv, m)
                pc = jnp.exp(sc - lse)
                dsc = (pc * (ndot(dom, vc) + c) * SCALE).astype(BF)
                dqh = jnp.dot(dsc, kc, preferred_element_type=F32)
                dkc = dkc + jnp.where(m, tdot(dsc, qm), 0.0)
                dvc = dvc + jnp.where(m, tdot(pc.astype(BF), dom), 0.0)
                if has_prev:
                    pp = jnp.exp(sp - lse)
                    dsp = (pp * (ndot(dom, vp) + c) * SCALE).astype(BF)
                    dqh = dqh + jnp.dot(dsp, kp, preferred_element_type=F32)
                    dkp = dkp + jnp.where(m, tdot(dsp, qm), 0.0)
                    dvp = dvp + jnp.where(m, tdot(pp.astype(BF), dom), 0.0)
                dq = dq + jnp.where(m, dqh, 0.0)
            dq_ref[rows, :] = dq
            dkc_ref[rows, :] = dkc
            dvc_ref[rows, :] = dvc
            if has_prev:
                dkp_ref[rows, :] = dkp
                dvp_ref[rows, :] = dvp

        _dil_loop(d, residue)

    cur, prev = _dil_specs(g, d, nb)
    anyspec = pl.BlockSpec(memory_space=pl.ANY)
    dq_a, dkc_a, dkp_a, dvc_a, dvp_a = aliases
    if has_prev:
        in_specs, ops = [cur, prev, cur, prev, cur, cur, cur, cur], [q, k, k, v, v, do, cb, lse]
        al = [dq_a, dkc_a, dvc_a, dkp_a, dvp_a]
    else:
        in_specs, ops = [cur, cur, cur, cur, cur, cur], [q, k, v, do, cb, lse]
        al = [dq_a, dkc_a, dvc_a]
    n_in = len(ops)
    outs = pl.pallas_call(
        body, name=name, grid=(B, nb, 2),
        in_specs=in_specs + [anyspec] * n_out,
        out_specs=[cur] * n_out,
        out_shape=[jax.ShapeDtypeStruct(q.shape, F32)] * n_out,
        input_output_aliases={n_in + i: i for i in range(n_out)},
        compiler_params=_cp(("parallel", "parallel", "parallel")),
    )(*ops, *al)
    if has_prev:
        dq_a, dkc_a, dvc_a, dkp_a, dvp_a = outs
    else:
        dq_a, dkc_a, dvc_a = outs
    return dq_a, dkc_a, dkp_a, dvc_a, dvp_a


N_UNITS = 16


def _unit_rows(g):
    d = DIL[g]
    nb = N_UNITS // d
    return [pl.ds(n * STEPS * d + r, STEPS, stride=d) if d > 1 else pl.ds(n * STEPS, STEPS)
            for n in range(nb) for r in range(d)]


def _load_units(ref, g):
    if DIL[g] == 1:
        return ref[...].reshape(N_UNITS, STEPS, 128)
    return jnp.stack([ref[rows, :] for rows in _unit_rows(g)])


def _store_units(ref, val, g):
    if DIL[g] == 1:
        ref[...] = val.reshape(N_UNITS * STEPS, 128)
    else:
        for u, rows in enumerate(_unit_rows(g)):
            ref[rows, :] = val[u]


def _shift_units(x, by):
    z = jnp.zeros((abs(by),) + x.shape[1:], x.dtype)
    return jnp.concatenate([z, x[:N_UNITS - by]], axis=0) if by > 0 else jnp.concatenate([x[-by:], z], axis=0)


def _bdot(a, b, ca, cb):
    return lax.dot_general(a, b, (((ca,), (cb,)), ((0,), (0,))), preferred_element_type=F32)


def _dil_masks(g):
    d = DIL[g]
    has_prev = N_UNITS // d > 1
    qi = lax.broadcasted_iota(jnp.int32, (1, STEPS, STEPS), 1)
    kj = lax.broadcasted_iota(jnp.int32, (1, STEPS, STEPS), 2)
    unit = lax.broadcasted_iota(jnp.int32, (N_UNITS, 1, 1), 0)
    cur = kj <= qi
    prev = jnp.logical_and(kj >= qi, unit >= d) if has_prev else None
    lane = lax.broadcasted_iota(jnp.int32, (1, 1, 128), 2)
    heads = [(lane // HEAD_DIM) == h for h in range(128 // HEAD_DIM)]
    return has_prev, cur, prev, heads


def _dil_fwd(g, q, k, v, o_alias, l_alias, B, S, *, name):
    assert S == N_UNITS * STEPS
    d = DIL[g]

    def body(q_ref, k_ref, v_ref, _, __, o_ref, l_ref):
        has_prev, cur, prev, heads = _dil_masks(g)
        q = _load_units(q_ref, g)
        kc = _load_units(k_ref, g).astype(BF)
        vc = _load_units(v_ref, g).astype(BF)
        if has_prev:
            kp, vp = _shift_units(kc, d), _shift_units(vc, d)
        o = jnp.zeros(q.shape, F32)
        lse_b = jnp.zeros(q.shape, F32)
        for m in heads:
            qm = jnp.where(m, q, 0.0).astype(BF)
            sc = jnp.where(cur, _bdot(qm, kc, 2, 2) * SCALE, NEG)
            mx = jnp.max(sc, axis=-1, keepdims=True)
            if has_prev:
                sp = jnp.where(prev, _bdot(qm, kp, 2, 2) * SCALE, NEG)
                mx = jnp.maximum(mx, jnp.max(sp, axis=-1, keepdims=True))
            l = jnp.sum(jnp.exp(sc - mx), axis=-1, keepdims=True)
            if has_prev:
                l = l + jnp.sum(jnp.exp(sp - mx), axis=-1, keepdims=True)
            lse = mx + jnp.log(l)
            oh = _bdot(jnp.exp(sc - lse).astype(BF), vc, 2, 1)
            if has_prev:
                oh = oh + _bdot(jnp.exp(sp - lse).astype(BF), vp, 2, 1)
            o = o + jnp.where(m, oh, 0.0)
            lse_b = lse_b + jnp.where(m, lse, 0.0)
        _store_units(o_ref, o, g)
        _store_units(l_ref, lse_b, g)

    blk = pl.BlockSpec((S, 128), lambda b, hf: (b, g * 2 + hf))
    anyspec = pl.BlockSpec(memory_space=pl.ANY)
    o, l = pl.pallas_call(
        body, name=name, grid=(B, 2),
        in_specs=[blk, blk, blk, anyspec, anyspec], out_specs=[blk, blk],
        out_shape=[jax.ShapeDtypeStruct(q.shape, F32)] * 2,
        input_output_aliases={3: 0, 4: 1},
        compiler_params=_cp(("parallel", "parallel")),
    )(q, k, v, o_alias, l_alias)
    return o, l


def _dil_bwd(g, q, k, v, do, cb, lse, aliases, B, S, *, name):
    assert S == N_UNITS * STEPS
    d = DIL[g]

    def body(q_ref, k_ref, v_ref, do_ref, c_ref, l_ref, _, __, ___, dq_ref, dk_ref, dv_ref):
        has_prev, cur, prev, heads = _dil_masks(g)
        q = _load_units(q_ref, g)
        kc = _load_units(k_ref, g).astype(BF)
        vc = _load_units(v_ref, g).astype(BF)
        do = _load_units(do_ref, g)
        cbv = _load_units(c_ref, g)
        lse_b = _load_units(l_ref, g)
        if has_prev:
            kp, vp = _shift_units(kc, d), _shift_units(vc, d)
        z = jnp.zeros(q.shape, F32)
        dq, dkc, dkp, dvc, dvp = z, z, z, z, z
        for m in heads:
            qm = jnp.where(m, q, 0.0).astype(BF)
            dom = jnp.where(m, do, 0.0).astype(BF)
            lse = jnp.max(jnp.where(m, lse_b, -jnp.inf), axis=-1, keepdims=True)
            c = jnp.max(jnp.where(m, cbv, -jnp.inf), axis=-1, keepdims=True)
            sc = jnp.where(cur, _bdot(qm, kc, 2, 2) * SCALE, NEG)
            pc = jnp.exp(sc - lse)
            dsc = (pc * (_bdot(dom, vc, 2, 2) + c) * SCALE).astype(BF)
            dqh = _bdot(dsc, kc, 2, 1)
            dkc = dkc + jnp.where(m, _bdot(dsc, qm, 1, 1), 0.0)
            dvc = dvc + jnp.where(m, _bdot(pc.astype(BF), dom, 1, 1), 0.0)
            if has_prev:
                sp = jnp.where(prev, _bdot(qm, kp, 2, 2) * SCALE, NEG)
                pp = jnp.exp(sp - lse)
                dsp = (pp * (_bdot(dom, vp, 2, 2) + c) * SCALE).astype(BF)
                dqh = dqh + _bdot(dsp, kp, 2, 1)
                dkp = dkp + jnp.where(m, _bdot(dsp, qm, 1, 1), 0.0)
                dvp = dvp + jnp.where(m, _bdot(pp.astype(BF), dom, 1, 1), 0.0)
            dq = dq + jnp.where(m, dqh, 0.0)
        if has_prev:
            dkc = dkc + _shift_units(dkp, -d)
            dvc = dvc + _shift_units(dvp, -d)
        _store_units(dq_ref, dq, g)
        _store_units(dk_ref, dkc, g)
        _store_units(dv_ref, dvc, g)

    blk = pl.BlockSpec((S, 128), lambda b, hf: (b, g * 2 + hf))
    anyspec = pl.BlockSpec(memory_space=pl.ANY)
    return tuple(pl.pallas_call(
        body, name=name, grid=(B, 2),
        in_specs=[blk] * 6 + [anyspec] * 3, out_specs=[blk] * 3,
        out_shape=[jax.ShapeDtypeStruct(q.shape, F32)] * 3,
        input_output_aliases={6: 0, 7: 1, 8: 2},
        compiler_params=_cp(("parallel", "parallel")),
    )(q, k, v, do, cb, lse, *aliases))


def _kv_grad_sum(parts, cos, sin, *, name, tm=512):
    T = parts[0][0].shape[0]
    n_l = len(parts)

    def body(*refs):
        c_ref, s_ref = refs[0], refs[1]
        dk_ref, dv_ref = refs[2 + 2 * n_l:]
        dk = refs[2][...]
        dv = refs[3][...]
        for li in range(1, n_l):
            dk = dk + refs[2 + 2 * li][...]
            dv = dv + refs[3 + 2 * li][...]
        dk_ref[...] = _rot(dk, c_ref[...], s_ref[...], -1.0).astype(dk_ref.dtype)
        dv_ref[...] = dv.astype(dv_ref.dtype)

    full = pl.BlockSpec((tm, MAIN_W), lambda i: (i, 0))
    tab = pl.BlockSpec((tm, 128), lambda i: (i, 0))
    ops = [cos, sin] + [t for part in parts for t in part]
    return pl.pallas_call(
        body, name=name, grid=(T // tm,), in_specs=[tab, tab] + [full] * (2 * n_l), out_specs=[full, full],
        out_shape=[jax.ShapeDtypeStruct((T, MAIN_W), BF)] * 2,
        compiler_params=_cp(("parallel",)),
    )(*ops)


def _group_softmax(lse):
    l0, l1, l2 = lse[:, 0:256], lse[:, 256:512], lse[:, 512:768]
    mx = jnp.maximum(jnp.maximum(l0, l1), l2)
    e0, e1, e2 = jnp.exp(l0 - mx), jnp.exp(l1 - mx), jnp.exp(l2 - mx)
    tot = e0 + e1 + e2
    return e0 / tot, e1 / tot, e2 / tot


def _dil_combine_fwd(o, lse, y_alias, *, name, tm=512):
    T = o.shape[0]

    def body(o_ref, l_ref, _, y_ref):
        a = jnp.concatenate(_group_softmax(l_ref[...]), axis=1)
        y_ref[...] = (o_ref[...] * a).astype(y_ref.dtype)

    blk = pl.BlockSpec((tm, MAIN_W), lambda i: (i, 0))
    return pl.pallas_call(
        body, name=name, grid=(T // tm,), in_specs=[blk, blk, pl.BlockSpec(memory_space=pl.ANY)], out_specs=blk,
        out_shape=jax.ShapeDtypeStruct(y_alias.shape, y_alias.dtype), input_output_aliases={2: 0},
        compiler_params=_cp(("parallel",)),
    )(o, lse, y_alias)


def _dil_combine_bwd(dy, o, lse, *, name, tm=256):
    T = o.shape[0]
    lane_r = lax.broadcasted_iota(jnp.int32, (256, 256), 0) // HEAD_DIM
    lane_c = lax.broadcasted_iota(jnp.int32, (256, 256), 1) // HEAD_DIM
    ones_bd = (lane_r == lane_c).astype(BF)

    def body(dy_ref, o_ref, l_ref, e_ref, do_ref, c_ref):
        dyv = dy_ref[...]
        alphas = _group_softmax(l_ref[...])
        prod = dyv * o_ref[...]
        e = e_ref[...]
        tot = jnp.zeros((tm, 256), F32)
        for gi in range(3):
            x = prod[:, gi * 256:(gi + 1) * 256]
            hi = x.astype(BF)
            lo = (x - hi.astype(F32)).astype(BF)
            dalpha = jnp.dot(hi, e, preferred_element_type=F32) + jnp.dot(lo, e, preferred_element_type=F32)
            tot = tot + alphas[gi] * dalpha
        a = jnp.concatenate(alphas, axis=1)
        do_ref[...] = (dyv * a).astype(do_ref.dtype)
        c_ref[...] = jnp.concatenate([-al * tot for al in alphas], axis=1)

    blk = pl.BlockSpec((tm, MAIN_W), lambda i: (i, 0))
    return pl.pallas_call(
        body, name=name, grid=(T // tm,),
        in_specs=[blk, blk, blk, pl.BlockSpec((256, 256), lambda i: (0, 0))], out_specs=[blk, blk],
        out_shape=[jax.ShapeDtypeStruct((T, MAIN_W), F32), jax.ShapeDtypeStruct((T, MAIN_W), F32)],
        compiler_params=_cp(("parallel",)),
    )(dy, o, lse, ones_bd)


def _kv_grad(parts, cos, sin, B, S, *, name):
    T = B * S
    tb = S // STEPS
    n_l = len(parts)

    def shifted(g):
        def f(b, t):
            return (b * tb + jnp.minimum(t + DIL[g], tb - 1), g)
        return f

    with_prev = [g for g in range(3) if DIL[g] < tb]
    n_p = len(with_prev)
    per_l = 2 + 2 * n_p

    def body(*refs):
        c_ref, s_ref = refs[0], refs[1]
        ins = refs[2:2 + n_l * per_l]
        dk_ref, dv_ref = refs[2 + n_l * per_l:]
        t = pl.program_id(1)
        dk = jnp.zeros((STEPS, MAIN_W), F32)
        dv = jnp.zeros((STEPS, MAIN_W), F32)
        zero = jnp.zeros((STEPS, 256), F32)
        for li in range(n_l):
            base = li * per_l
            dk = dk + ins[base][...]
            dv = dv + ins[base + 1][...]
            kparts, vparts = [zero] * 3, [zero] * 3
            for pi, g in enumerate(with_prev):
                ok = t + DIL[g] < tb
                kparts[g] = jnp.where(ok, ins[base + 2 + pi][...], 0.0)
                vparts[g] = jnp.where(ok, ins[base + 2 + n_p + pi][...], 0.0)
            dk = dk + jnp.concatenate(kparts, axis=1)
            dv = dv + jnp.concatenate(vparts, axis=1)
        dk_ref[...] = _rot(dk, c_ref[...], s_ref[...], -1.0).astype(dk_ref.dtype)
        dv_ref[...] = dv.astype(dv_ref.dtype)

    full = pl.BlockSpec((STEPS, MAIN_W), lambda b, t: (b * tb + t, 0))
    tab = pl.BlockSpec((STEPS, 128), lambda b, t: (b * tb + t, 0))
    in_specs, ops = [tab, tab], [cos, sin]
    for (kc, kp, vc, vp) in parts:
        in_specs += [full, full] + [pl.BlockSpec((STEPS, 256), shifted(g)) for g in with_prev] * 2
        ops += [kc, vc] + [kp] * n_p + [vp] * n_p
    return pl.pallas_call(
        body, name=name, grid=(B, tb), in_specs=in_specs, out_specs=[full, full],
        out_shape=[jax.ShapeDtypeStruct((T, MAIN_W), BF)] * 2,
        compiler_params=_cp(("parallel", "parallel")),
    )(*ops)


def _loss(y, target, *, name, tm=512):
    T, Dm = y.shape
    nt = T // tm

    def body(y_ref, t_ref, l_ref, d_ref, acc):
        i = pl.program_id(0)
        err = y_ref[...] - t_ref[...]
        d_ref[...] = err / Dm
        part = jnp.sum(jnp.mean(err * err, axis=-1, keepdims=True).reshape(tm // 8, 8, 1), axis=0)

        @pl.when(i == 0)
        def _():
            acc[...] = part

        @pl.when(i > 0)
        def _():
            acc[...] += part

        @pl.when(i == nt - 1)
        def _():
            l_ref[...] = 0.5 * jnp.sum(acc[...], axis=0, keepdims=True)

    row = pl.BlockSpec((tm, Dm), lambda i: (i, 0))
    return pl.pallas_call(
        body, name=name, grid=(nt,), in_specs=[row, row],
        out_specs=[pl.BlockSpec((1, 1), lambda i: (0, 0)), row],
        out_shape=[jax.ShapeDtypeStruct((1, 1), F32), jax.ShapeDtypeStruct((T, Dm), F32)],
        scratch_shapes=[pltpu.VMEM((8, 1), F32)],
        compiler_params=_cp(("arbitrary",)),
    )(y, target)


def _adamw(w, g, m, v, *, name):
    shape = w.shape
    cols = shape[-1]
    rows = w.size // cols
    tm = rows
    for cand in (512, 352, 256, 128):
        if rows > cand and rows % cand == 0 and cand * cols * 4 <= (1 << 20):
            tm = cand
            break

    def body(w_ref, g_ref, m_ref, v_ref, d_ref, mo_ref, vo_ref):
        gv = g_ref[...]
        mn = ADAM_B1 * m_ref[...] + (1.0 - ADAM_B1) * gv
        vn = ADAM_B2 * v_ref[...] + (1.0 - ADAM_B2) * (gv * gv)
        m_hat = mn / (1.0 - ADAM_B1 ** ADAM_STEP)
        v_hat = vn / (1.0 - ADAM_B2 ** ADAM_STEP)
        d_ref[...] = -ADAM_LR * (m_hat / (jnp.sqrt(v_hat) + ADAM_EPS) + ADAM_WD * w_ref[...])
        mo_ref[...] = mn
        vo_ref[...] = vn

    blk = pl.BlockSpec((tm, cols), lambda i: (i, 0))
    outs = pl.pallas_call(
        body, name=name, grid=(rows // tm,), in_specs=[blk] * 4, out_specs=[blk] * 3,
        out_shape=[jax.ShapeDtypeStruct((rows, cols), F32)] * 3,
        compiler_params=_cp(("parallel",)),
    )(*[t.reshape(rows, cols) for t in (w, g, m, v)])
    return tuple(t.reshape(shape) for t in outs)


def _adamw_layer(name, l, w, g, m, v, prev, after=None):
    L, rows, cols = w.shape
    tm = rows
    for cand in (512, 352, 256, 176, 128, 64):
        if rows % cand == 0 and cand * cols * 4 <= (1 << 20):
            tm = cand
            break
    if prev is None:
        prev = tuple(lax.empty(w.shape, F32) for _ in range(4))

    n_after = 0 if after is None else 1

    def body(w_ref, g_ref, m_ref, v_ref, *rest):
        d_ref, mo_ref, vo_ref, go_ref = rest[4 + n_after:]
        gv = g_ref[...]
        mn = ADAM_B1 * m_ref[...] + (1.0 - ADAM_B1) * gv
        vn = ADAM_B2 * v_ref[...] + (1.0 - ADAM_B2) * (gv * gv)
        m_hat = mn / (1.0 - ADAM_B1 ** ADAM_STEP)
        v_hat = vn / (1.0 - ADAM_B2 ** ADAM_STEP)
        d_ref[...] = -ADAM_LR * (m_hat / (jnp.sqrt(v_hat) + ADAM_EPS) + ADAM_WD * w_ref[...])
        mo_ref[...] = mn
        vo_ref[...] = vn
        go_ref[...] = gv

    lay = pl.BlockSpec((None, tm, cols), lambda i: (l, i, 0))
    one = pl.BlockSpec((None, tm, cols), lambda i: (0, i, 0))
    return tuple(pl.pallas_call(
        body, name=f"l{l}_adamw_{name}", grid=(rows // tm,),
        in_specs=[lay, one, lay, lay] + [pl.BlockSpec(memory_space=pl.ANY)] * (4 + n_after), out_specs=[lay] * 4,
        out_shape=[jax.ShapeDtypeStruct(w.shape, F32)] * 4,
        input_output_aliases={4 + i: i for i in range(4)},
        compiler_params=_cp(("parallel",)),
    )(w, g, m, v, *prev, *([] if after is None else [after])))


BIG = {
    'w_in': ((DEPTH, D_MODEL, D_MODEL), 'row'),
    'w_mem_kv': ((DEPTH, D_MODEL, 2 * MEM_W), 'row'),
    'w_out': ((DEPTH, D_MODEL, D_MODEL), 'row'),
    'w_kv': ((1, D_MODEL, 2 * MAIN_W), 'col'),
    'w_gate_up': ((DEPTH, D_MODEL, 2 * D_FF), 'col'),
    'w_down': ((DEPTH, D_FF, D_MODEL), 'row'),
}
BIG_NAMES = tuple(BIG)
N_CHIPS = 4
HBM_ANY = pl.BlockSpec(memory_space=pl.ANY)


def _geom(name):
    (L, R, C), kind = BIG[name]
    if kind == 'row':
        return L, R, C, kind, R // N_CHIPS, C, R // (2 * N_CHIPS)
    return L, R, C, kind, R, C // N_CHIPS, R // 2


def _shard_shape(name):
    L, R, C, kind, rs, cs, rh = _geom(name)
    return (L, rs, cs)


def _half_shape(name):
    L, R, C, kind, rs, cs, rh = _geom(name)
    return (L, rh, cs)


def _full_win(ref, name, s, h):
    L, R, C, kind, rs, cs, rh = _geom(name)
    if kind == 'row':
        rows = pl.ds(s * rs, rs) if h is None else pl.ds(s * rs + h * rh, rh)
        return ref.at[:, rows, :]
    rows = slice(None) if h is None else pl.ds(h * rh, rh)
    return ref.at[:, rows, pl.ds(s * cs, cs)]


def _shard_half(ref, name, h):
    L, R, C, kind, rs, cs, rh = _geom(name)
    return ref.at[:, pl.ds(h * rh, rh), :]


def _halves_win(ref, name, s):
    L, R, C, kind, rs, cs, rh = _geom(name)
    if kind == 'row':
        return ref.at[:, pl.ds(s * rh, rh), :]
    return ref.at[:, :, pl.ds(s * cs, cs)]


def _halves_shape(name):
    L, R, C, kind, rs, cs, rh = _geom(name)
    return (L, N_CHIPS * rh, cs) if kind == 'row' else (L, rh, C)


def _place():
    x, y, c = lax.axis_index("x"), lax.axis_index("y"), lax.axis_index("c")
    chips = [(1 - x, y), (x, 1 - y), (1 - x, 1 - y)]
    return x, y, c, chips


SMALL_ROWS = 24


def _all_gather(shards, small):
    names = BIG_NAMES
    nw = len(names)

    def body(*refs):
        src = dict(zip(names, refs[:nw]))
        small_ref = refs[nw]
        dst = dict(zip(names, refs[nw + 1:2 * nw + 1]))
        small_out = refs[2 * nw + 1]
        send_sems, recv_sems, local_sems = refs[2 * nw + 2:]
        x, y, c, chips = _place()
        s = 2 * x + y
        sib = (x, y, 1 - c)

        def remote(k, src_ref, dst_ref, to):
            return pltpu.make_async_remote_copy(src_ref=src_ref, dst_ref=dst_ref, send_sem=send_sems.at[k],
                                                recv_sem=recv_sems.at[k], device_id=to, device_id_type=MESH)

        local = []
        for wi, nm in enumerate(names):
            local.append(pltpu.make_async_copy(src[nm], _full_win(dst[nm], nm, s, None), local_sems.at[wi]))
        local.append(pltpu.make_async_copy(small_ref, small_out.at[s], local_sems.at[nw]))
        for cp in local:
            cp.start()
        sends = []
        for j, (px, py) in enumerate(chips):
            for wi, nm in enumerate(names):
                sends.append(remote(wi * 6 + j, _shard_half(src[nm], nm, c), _full_win(dst[nm], nm, s, c), (px, py, c)))
            sends.append(remote(nw * 6 + j, small_ref, small_out.at[s], (px, py, c)))
        for cp in sends:
            cp.start()
        for j, (px, py) in enumerate(chips):
            sp = 2 * px + py
            for wi, nm in enumerate(names):
                w = _full_win(dst[nm], nm, sp, c)
                remote(wi * 6 + j, w, w, sib).wait_recv()
                fwd = remote(wi * 6 + 3 + j, w, w, sib)
                fwd.start()
                sends.append(fwd)
            remote(nw * 6 + j, small_ref, small_out.at[sp], sib).wait_recv()
        for j, (px, py) in enumerate(chips):
            sp = 2 * px + py
            for wi, nm in enumerate(names):
                w = _full_win(dst[nm], nm, sp, 1 - c)
                remote(wi * 6 + 3 + j, w, w, sib).wait_recv()
        for cp in sends:
            cp.wait_send()
        for cp in local:
            cp.wait()

    n_sem = nw * 6 + 3
    outs = pl.pallas_call(
        body, name="all_gather_weights",
        in_specs=[HBM_ANY] * (nw + 1), out_specs=[HBM_ANY] * (nw + 1),
        out_shape=[jax.ShapeDtypeStruct(BIG[nm][0], BF) for nm in names]
        + [jax.ShapeDtypeStruct((N_CHIPS, SMALL_ROWS, 256), F32)],
        scratch_shapes=[pltpu.SemaphoreType.DMA((n_sem,)), pltpu.SemaphoreType.DMA((n_sem,)),
                        pltpu.SemaphoreType.DMA((nw + 1,))],
    )(*[shards[nm] for nm in names], small)
    return dict(zip(names, outs[:nw])), outs[nw]


SEM_SPEC = pl.BlockSpec(memory_space=pltpu.SEMAPHORE)
HBM_SPEC = pl.BlockSpec(memory_space=pltpu.HBM)
DATAFLOW = pltpu.SideEffectType.DATAFLOW_SIDE_EFFECTING


def _in_hbm(a):
    return pltpu.with_memory_space_constraint(a, pltpu.HBM)


def _remote(src, dst, send_sems, recv_sems, k, to):
    return pltpu.make_async_remote_copy(src_ref=src, dst_ref=dst, send_sem=send_sems.at[k], recv_sem=recv_sems.at[k],
                                        device_id=to, device_id_type=MESH)


def _split_start(name, bufs, n_copies, sends, after=None):
    nb = len(bufs)
    n_in = nb + (0 if after is None else 1)

    def body(*refs):
        in_refs = refs[:nb]
        send_sems, recv_sems = refs[n_in], refs[n_in + 1]
        token = refs[-1]
        for k, (src, dst, to) in enumerate(sends(in_refs)):
            _remote(src, dst, send_sems, recv_sems, k, to).start()
        token[...] = jnp.zeros_like(token)

    outs = pl.pallas_call(
        body, name=name,
        out_shape=(pltpu.SemaphoreType.DMA((n_copies,)), pltpu.SemaphoreType.DMA((n_copies,)),
                   *[pltpu.HBM(b.shape, b.dtype) for b in bufs], jax.ShapeDtypeStruct((8, 128), F32)),
        in_specs=[HBM_SPEC] * nb + [HBM_ANY] * (n_in - nb),
        out_specs=(SEM_SPEC, SEM_SPEC, *[HBM_SPEC] * nb, pl.BlockSpec(memory_space=pltpu.VMEM)),
        input_output_aliases={i: 2 + i for i in range(nb)},
        compiler_params=pltpu.CompilerParams(has_side_effects=DATAFLOW),
    )(*[_in_hbm(b) for b in bufs], *([] if after is None else [after]))
    return outs[0], outs[1], list(outs[2:2 + nb]), outs[-1]


def _split_wait(name, send_sems, recv_sems, bufs, after, sends, arrivals):
    nb = len(bufs)

    def body(*refs):
        in_refs = refs[:nb]
        s_sems, r_sems = refs[nb], refs[nb + 1]
        me = (lax.axis_index("x"), lax.axis_index("y"), lax.axis_index("c"))
        for k, (src, dst, to) in enumerate(sends(in_refs)):
            _remote(src, dst, s_sems, r_sems, k, to).wait_send()
        for k, win in enumerate(arrivals(in_refs)):
            _remote(win, win, s_sems, r_sems, k, me).wait_recv()

    outs = pl.pallas_call(
        body, name=name,
        out_shape=[pltpu.HBM(b.shape, b.dtype) for b in bufs],
        in_specs=[HBM_SPEC] * nb + [SEM_SPEC, SEM_SPEC, HBM_ANY],
        out_specs=[HBM_SPEC] * nb,
        input_output_aliases={i: i for i in range(nb)},
        compiler_params=pltpu.CompilerParams(has_side_effects=DATAFLOW),
    )(*bufs, send_sems, recv_sems, after)
    return list(outs)


MIX_W = ('w_in', 'w_mem_kv', 'w_out')
FFN_W = ('w_gate_up', 'w_down')
LAYER_W = MIX_W + FFN_W


def _place_own(l, names, shards, small, sc):
    nw = len(names)
    has_small = small is not None
    n_ops = nw + (1 if has_small else 0)

    def body(sc_ref, *refs):
        for src, dst in zip(refs[:n_ops], refs[n_ops:]):
            dst[...] = src[...]

    in_specs, out_specs, out_shape, ops = [], [], [], list(shards)
    for nm in names:
        L, R, C, kind, rs, cs, rh = _geom(nm)
        in_specs.append(pl.BlockSpec((1, rs, cs), lambda i, sc_ref: (0, 0, 0)))
        if kind == 'row':
            out_specs.append(pl.BlockSpec((1, rs, cs), lambda i, sc_ref: (0, sc_ref[0], 0)))
        else:
            out_specs.append(pl.BlockSpec((1, rs, cs), lambda i, sc_ref: (0, 0, sc_ref[0])))
        out_shape.append(jax.ShapeDtypeStruct((1, R, C), BF))
    if has_small:
        in_specs.append(pl.BlockSpec((SMALL_ROWS, 256), lambda i, sc_ref: (0, 0)))
        out_specs.append(pl.BlockSpec((None, SMALL_ROWS, 256), lambda i, sc_ref: (sc_ref[0], 0, 0)))
        out_shape.append(jax.ShapeDtypeStruct((N_CHIPS, SMALL_ROWS, 256), F32))
        ops.append(small)
    return pl.pallas_call(
        body, name=f"{l}_place_own_shard",
        grid_spec=pltpu.PrefetchScalarGridSpec(num_scalar_prefetch=1, grid=(1,), in_specs=in_specs, out_specs=out_specs),
        out_shape=out_shape,
        compiler_params=_cp(("arbitrary",)),
    )(sc, *ops)


def _gather_start(l, names, shards, small, sc, after=None):
    nw = len(names)
    has_small = small is not None
    fulls = _place_own(l, names, shards, small, sc)
    bufs = list(shards) + ([small] if has_small else []) + list(fulls)
    n_src = nw + (1 if has_small else 0)

    def sends(refs):
        x, y, c, chips = _place()
        s = 2 * x + y
        out = []
        for (px, py) in chips:
            for wi, nm in enumerate(names):
                out.append((_shard_half(refs[wi], nm, c), _full_win(refs[n_src + wi], nm, s, c), (px, py, c)))
            if has_small:
                out.append((refs[nw], refs[n_src + nw].at[s], (px, py, c)))
        return out

    def arrivals(refs):
        x, y, c, chips = _place()
        out = []
        for (px, py) in chips:
            sp = 2 * px + py
            for wi, nm in enumerate(names):
                out.append(_full_win(refs[n_src + wi], nm, sp, c))
            if has_small:
                out.append(refs[n_src + nw].at[sp])
        return out

    n_copies = 3 * n_src
    send_sems, recv_sems, bufs, token = _split_start(f"{l}_gather_ici_start", bufs, n_copies, sends, after)
    return dict(l=l, names=names, has_small=has_small, sems=(send_sems, recv_sems), bufs=bufs, sends=sends,
                arrivals=arrivals, token=token)


def _gather_forward(st, after):
    l, names = st['l'], st['names']
    nw = len(names)
    n_src = nw + (1 if st['has_small'] else 0)
    bufs = _split_wait(f"{l}_gather_ici_wait", *st['sems'], st['bufs'], after, st['sends'], st['arrivals'])
    fulls = bufs[n_src:n_src + nw]
    small_all = bufs[n_src + nw] if st['has_small'] else None

    def sends(refs):
        x, y, c, chips = _place()
        out = []
        for (px, py) in chips:
            sp = 2 * px + py
            for wi, nm in enumerate(names):
                w = _full_win(refs[wi], nm, sp, c)
                out.append((w, w, (x, y, 1 - c)))
        return out

    def arrivals(refs):
        x, y, c, chips = _place()
        out = []
        for (px, py) in chips:
            sp = 2 * px + py
            for wi, nm in enumerate(names):
                out.append(_full_win(refs[wi], nm, sp, 1 - c))
        return out

    send_sems, recv_sems, fulls, token = _split_start(f"{l}_gather_d2d_start", fulls, 3 * nw, sends)
    return dict(l=l, names=names, sems=(send_sems, recv_sems), bufs=fulls, sends=sends, arrivals=arrivals,
                small_all=small_all, token=token)


def _gather_finish(st, after):
    fulls = _split_wait(f"{st['l']}_gather_d2d_wait", *st['sems'], st['bufs'], after, st['sends'], st['arrivals'])
    return dict(zip(st['names'], fulls)), st['small_all']


def _reduce_start(tag, names, grads):
    nw = len(names)
    recv = [lax.empty((1,) + _halves_shape(nm)[1:], F32) for nm in names]
    bufs = [grads[nm] for nm in names] + recv

    def windows(refs, half_of):
        x, y, c, _ = _place()
        h = half_of(c)
        out = []
        for wi, nm in enumerate(names):
            L, R, C, kind, rs, cs, rh = _geom(nm)
            if kind == 'row':
                for sp in range(N_CHIPS):
                    out.append((_full_win(refs[wi], nm, sp, h), _halves_win(refs[nw + wi], nm, sp)))
            else:
                out.append((refs[wi].at[:, pl.ds(h * rh, rh), :], refs[nw + wi]))
        return out

    def sends(refs):
        x, y, c, _ = _place()
        return [(src, dst, (x, y, 1 - c)) for src, dst in windows(refs, lambda c: 1 - c)]

    def arrivals(refs):
        return [dst for _, dst in windows(refs, lambda c: c)]

    n_copies = sum(N_CHIPS if BIG[nm][1] == 'row' else 1 for nm in names)
    send_sems, recv_sems, bufs, token = _split_start(tag + "_halves_start", bufs, n_copies, sends)
    return dict(tag=tag, names=names, sems=(send_sems, recv_sems), bufs=bufs, sends=sends, arrivals=arrivals, token=token)


def _reduce_mid(st, after, sc):
    tag, names = st['tag'], st['names']
    nw = len(names)
    bufs = _split_wait(tag + "_halves_wait", *st['sems'], st['bufs'], after, st['sends'], st['arrivals'])
    halves, own = [], []
    for wi, nm in enumerate(names):
        hb, ow = _add_halves(nm, bufs[wi], bufs[nw + wi], sc, tag)
        halves.append(hb)
        own.append(ow)
    pieces = [lax.empty((3, 1) + _half_shape(nm)[1:], BF) for nm in names]

    def sends(refs):
        x, y, c, chips = _place()
        out = []
        for j, (px, py) in enumerate(chips):
            for wi, nm in enumerate(names):
                out.append((_halves_win(refs[wi], nm, 2 * px + py), refs[nw + wi].at[j], (px, py, c)))
        return out

    def arrivals(refs):
        return [refs[nw + wi].at[j] for j in range(3) for wi in range(nw)]

    send_sems, recv_sems, bufs, token = _split_start(tag + "_pieces_start", halves + pieces, 3 * nw, sends)
    return dict(tag=tag, names=names, sems=(send_sems, recv_sems), bufs=bufs, sends=sends, arrivals=arrivals, own=own,
                token=token)


def _reduce_late(st, after, sc):
    tag, names = st['tag'], st['names']
    nw = len(names)
    bufs = _split_wait(tag + "_pieces_wait", *st['sems'], st['bufs'], after, st['sends'], st['arrivals'])
    gsh = [_sum_pieces(nm, st['own'][wi], bufs[nw + wi], sc, tag) for wi, nm in enumerate(names)]

    def sends(refs):
        x, y, c, _ = _place()
        return [(_shard_half(refs[wi], nm, c), _shard_half(refs[wi], nm, c), (x, y, 1 - c)) for wi, nm in enumerate(names)]

    def arrivals(refs):
        x, y, c, _ = _place()
        return [_shard_half(refs[wi], nm, 1 - c) for wi, nm in enumerate(names)]

    send_sems, recv_sems, bufs, token = _split_start(tag + "_share_start", gsh, nw, sends)
    return dict(tag=tag, names=names, sems=(send_sems, recv_sems), bufs=bufs, sends=sends, arrivals=arrivals, token=token)


def _reduce_finish(st, after):
    gsh = _split_wait(st['tag'] + "_share_wait", *st['sems'], st['bufs'], after, st['sends'], st['arrivals'])
    return dict(zip(st['names'], gsh))


def _add_halves(name, g, r, sc, tag):
    _, R, C, kind, rs, cs, rh = _geom(name)
    L = g.shape[0]
    tr = rh if kind == 'row' else 256
    nr = rh // tr

    def body(sc_ref, g_ref, r_ref, hb_ref, own_ref):
        sp = pl.program_id(2)
        tot = g_ref[...] + r_ref[...]
        hb_ref[...] = tot.astype(hb_ref.dtype)

        @pl.when(sp == sc_ref[0])
        def _():
            own_ref[...] = tot

    if kind == 'row':
        g_map = lambda l, ri, sp, sc_ref: (l, sp * 2 + sc_ref[1], 0)
        h_map = lambda l, ri, sp, sc_ref: (l, sp, 0)
    else:
        g_map = lambda l, ri, sp, sc_ref: (l, sc_ref[1] * nr + ri, sp)
        h_map = lambda l, ri, sp, sc_ref: (l, ri, sp)
    own_map = lambda l, ri, sp, sc_ref: (l, ri, 0)
    blk = (None, tr, cs)
    return pl.pallas_call(
        body, name=tag + "_add_halves_" + name,
        grid_spec=pltpu.PrefetchScalarGridSpec(
            num_scalar_prefetch=1, grid=(L, nr, N_CHIPS),
            in_specs=[pl.BlockSpec(blk, g_map), pl.BlockSpec(blk, h_map)],
            out_specs=[pl.BlockSpec(blk, h_map), pl.BlockSpec(blk, own_map)]),
        out_shape=[jax.ShapeDtypeStruct((L,) + _halves_shape(name)[1:], BF),
                   jax.ShapeDtypeStruct((L,) + _half_shape(name)[1:], F32)],
        compiler_params=_cp(("parallel", "parallel", "arbitrary")),
    )(sc, g, r)


def _sum_pieces(name, own, pieces, sc, tag):
    _, R, C, kind, rs, cs, rh = _geom(name)
    L = own.shape[0]
    tr = rh if kind == 'row' else 256
    nr = rh // tr

    def body(sc_ref, o_ref, p_ref, out_ref):
        out_ref[...] = o_ref[...] + p_ref[0].astype(F32) + p_ref[1].astype(F32) + p_ref[2].astype(F32)

    blk = (None, tr, cs)
    return pl.pallas_call(
        body, name=tag + "_sum_pieces_" + name,
        grid_spec=pltpu.PrefetchScalarGridSpec(
            num_scalar_prefetch=1, grid=(L, nr),
            in_specs=[pl.BlockSpec(blk, lambda l, ri, sc_ref: (l, ri, 0)),
                      pl.BlockSpec((3, None, tr, cs), lambda l, ri, sc_ref: (0, l, ri, 0))],
            out_specs=pl.BlockSpec(blk, lambda l, ri, sc_ref: (l, sc_ref[1] * nr + ri, 0))),
        out_shape=jax.ShapeDtypeStruct((L,) + _shard_shape(name)[1:], F32),
        compiler_params=_cp(("parallel", "parallel")),
    )(sc, own, pieces)


def _small_gather_start(v, sc):
    rows = v.shape[0]

    def place(sc_ref, v_ref, o_ref):
        o_ref[...] = v_ref[...]

    slots = pl.pallas_call(
        place, name="small_grads_place_own",
        grid_spec=pltpu.PrefetchScalarGridSpec(
            num_scalar_prefetch=1, grid=(1,),
            in_specs=[pl.BlockSpec((rows, 128), lambda i, sc_ref: (0, 0))],
            out_specs=pl.BlockSpec((None, rows, 128), lambda i, sc_ref: (2 * sc_ref[0] + sc_ref[1], 0, 0))),
        out_shape=jax.ShapeDtypeStruct((8, rows, 128), v.dtype),
        compiler_params=_cp(("arbitrary",)),
    )(sc, v)

    def peers():
        x, y, c, _ = _place()
        flips = [(fx, fy, fc) for fx in (0, 1) for fy in (0, 1) for fc in (0, 1)][1:]
        return [((1 - x if fx else x), (1 - y if fy else y), (1 - c if fc else c)) for fx, fy, fc in flips]

    def sends(refs):
        x, y, c, _ = _place()
        return [(refs[0], refs[1].at[4 * x + 2 * y + c], p) for p in peers()]

    def arrivals(refs):
        return [refs[1].at[4 * px + 2 * py + pc] for px, py, pc in peers()]

    send_sems, recv_sems, bufs, token = _split_start("small_grads_gather_start", [v, slots], 7, sends)
    return dict(sems=(send_sems, recv_sems), bufs=bufs, sends=sends, arrivals=arrivals, token=token)


def _small_gather_finish(st, after):
    return _split_wait("small_grads_gather_wait", *st['sems'], st['bufs'], after, st['sends'], st['arrivals'])[1]


def _sum8(v8, *, name, tr=336):
    rows = v8.shape[1]
    tr = min(tr, rows)
    assert rows % tr == 0

    def body(v_ref, o_ref):
        tot = v_ref[0].astype(F32)
        for d in range(1, 8):
            tot = tot + v_ref[d].astype(F32)
        o_ref[...] = tot

    return pl.pallas_call(
        body, name=name, grid=(rows // tr,),
        in_specs=[pl.BlockSpec((8, tr, 128), lambda i: (0, i, 0))], out_specs=pl.BlockSpec((tr, 128), lambda i: (i, 0)),
        out_shape=jax.ShapeDtypeStruct((rows, 128), F32),
        compiler_params=_cp(("parallel",)),
    )(v8)


def _block_diag(w_pool_l):
    wbd = jnp.zeros((MAIN_W, MAIN_W), F32)
    for gi in range(len(POOL_WINDOWS)):
        wbd = lax.dynamic_update_slice(wbd, w_pool_l[gi], (gi * POOL_GROUP, gi * POOL_GROUP))
    return wbd.astype(BF)


def _unpack_small(small_all):
    ng = small_all[:, :16, :].reshape(N_CHIPS, DEPTH, 4, 256).transpose(1, 2, 0, 3).reshape(DEPTH, 4, D_MODEL)
    ps = small_all[:, 16:18, :POOL_GROUP].transpose(1, 0, 2).reshape(N_A, MAIN_W)
    return ng, ps


def _local_step(x, mem, positions, on_forward, on_backward, mem_norm, w_pool, kv_norm, target):
    B, S, _ = x.shape
    T = B * S
    xc = x.reshape(T, D_MODEL)
    memf = mem.reshape(B * N_MEM, D_MODEL)
    tgt = target.reshape(T, D_MODEL)
    cos, sin = _rope_tables(positions.reshape(T, 1), name="rope_tables")
    wbd = [_block_diag(w_pool[l]) for l in range(N_A)]
    nbo = D_FF // 256
    fw = []
    rk = rv = None
    kv_saved = None
    wts = []
    norm_gains = pool_scale = y2 = None

    def tied(vec, tok):
        return vec if tok is None else vec + tok

    for l in range(DEPTH):
        t = f"l{l}_"
        got = on_forward('start', l, y2)
        wts.append(dict(got[0]))
        if l == 0:
            norm_gains, pool_scale = _unpack_small(got[1])
        sv = {'x_in': xc}
        h0, sv['r0'] = _norm_fwd(xc, tied(norm_gains[l, 0], got[2]), name=t + "norm0", out_dtype=BF)
        z, = _mm(h0, wts[l]['w_in'], b_layer=0, name=t + "mm_in")
        memn, sv['rm'] = _norm_fwd(memf, mem_norm[l], name=t + "norm_mem", out_dtype=BF, tm=256)
        kvm, = _mm(memn, wts[l]['w_mem_kv'], b_layer=0, name=t + "mm_memkv", out_dtypes=(BF,))
        if l < N_A:
            ycat, sv['p'] = _pool_fwd(z, wbd[l], pool_scale[l], B, S, name=t + "pool_fwd")
        else:
            rq = _rope_apply(z, cos, sin, name=t + "rope_q", out_dtype=F32)
            o = lax.empty((T, MAIN_W), F32)
            lse = lax.empty((T, MAIN_W), F32)
            for g in range(3):
                o, lse = _dil_fwd(g, rq, rk, rv, o, lse, B, S, name=t + f"dil_fwd{g}")
            ycat = _dil_combine_fwd(o, lse, lax.empty((T, D_MODEL), BF), name=t + "dil_combine")
            sv.update(rq=rq, o=o, lse=lse)
        ycat, sv['lse_m'] = _memattn_fwd(z, kvm, ycat, B, S, name=t + "memattn_fwd")
        tok = on_forward('mid', l, ycat)
        y1, = _mm(ycat, wts[l]['w_out'], b_layer=0, name=t + "mm_out")
        wts[l].update(on_forward('ffn', l, y1)[0])
        x1, sv['r1'] = _norm_fwd(y1, tied(norm_gains[l, 1], tok), name=t + "norm1", res=xc)
        h2, sv['r2'] = _norm_fwd(x1, norm_gains[l, 2], name=t + "norm2", out_dtype=BF)
        gg, uu, aa = _mm(h2, wts[l]['w_gate_up'], b_layer=0, b_offsets=(0, nbo), out_n=D_FF, tm=2048, tn=256, name=t + "mm_gate_up",
                         epilogue=_swiglu_fwd_epilogue, out_dtypes=(BF, BF, BF))
        on_forward('post', l, gg)
        y2, = _mm(aa, wts[l]['w_down'], b_layer=0, tk=D_FF, name=t + "mm_down")
        x2, sv['r3'] = _norm_fwd(y2, norm_gains[l, 3], name=t + "norm3", res=x1)
        sv.update(h0=h0, z=z, memn=memn, kvm=kvm, ycat=ycat, y1=y1, x1=x1, h2=h2, gg=gg, uu=uu, aa=aa, y2=y2)
        fw.append(sv)
        xc = x2
        if l == N_A - 1:
            kvn, rkv = _norm_fwd(xc, kv_norm, name="norm_kv", out_dtype=BF)
            kv, = _mm(kvn, wts[N_A - 1]['w_kv'], b_layer=0, name="mm_kv")
            rk, rv = _rope_apply(kv, cos, sin, name="rope_k", passthrough=True, out_dtype=F32)
            kv_saved = (xc, kvn, rkv)

    loss, dx = _loss(xc, tgt, name="loss")

    d_ng = [[None] * 4 for _ in range(DEPTH)]
    d_memnorm = [None] * DEPTH
    d_wbd = [None] * N_A
    d_pscale = [None] * N_A
    d_kvnorm = None
    kv_parts = []
    tok = None

    def as3d(gl):
        return {nm: g.reshape((1,) + g.shape) for nm, g in gl.items()}

    for l in reversed(range(DEPTH)):
        t = f"l{l}_b_"
        sv = fw[l]
        gl = {}
        dy2, d_ng[l][3] = _norm_bwd(dx, sv['y2'], sv['r3'], tied(norm_gains[l, 3], tok), name=t + "norm3", out_dtype=BF)
        gl['w_down'], = _mm(sv['aa'], dy2, ta=True, tm=1408, tk=2048, name=t + "dw_down")
        dg, du = _mm(dy2, wts[l]['w_down'], tb=True, b_layer=0, tm=2048, tn=256, name=t + "d_act",
                     extras=((sv['gg'], 'tile'), (sv['uu'], 'tile')), epilogue=_swiglu_bwd_epilogue, out_dtypes=(BF, BF))
        gl['w_gate_up'], = _mm(sv['h2'], (dg, du), ta=True, tn=1408, tk=1024, name=t + "dw_gate_up")
        dh2, = _mm((dg, du), wts[l]['w_gate_up'], tb=True, b_layer=0, tn=1024, tk=1408, name=t + "d_h2", out_dtypes=(BF,))
        dx1, d_ng[l][2] = _norm_bwd(dh2, sv['x1'], sv['r2'], norm_gains[l, 2], name=t + "norm2", add=dx)
        tok = on_backward('ffn', l, dx1, as3d(gl))
        dy1, d_ng[l][1] = _norm_bwd(dx1, sv['y1'], sv['r1'], tied(norm_gains[l, 1], tok), name=t + "norm1", out_dtype=BF)
        gl['w_out'], = _mm(sv['ycat'], dy1, ta=True, tk=4096, name=t + "dw_out")
        dycat, = _mm(dy1, wts[l]['w_out'], tb=True, b_layer=0, name=t + "d_ycat")
        dz = lax.empty((T, D_MODEL), BF)
        dz, dkm, dvm = _memattn_bwd(dycat, sv['z'], sv['kvm'], sv['lse_m'], dz, B, S, name=t + "memattn")
        if l < N_A:
            dz, d_wbd[l], d_pscale[l] = _pool_bwd(dycat, sv['p'], wbd[l], pool_scale[l], dz, B, S, name=t + "pool")
        else:
            do, cb = _dil_combine_bwd(dycat, sv['o'], sv['lse'], name=t + "dil_combine")
            acc = tuple(lax.empty((T, MAIN_W), F32) for _ in range(3))
            for g in range(3):
                acc = _dil_bwd(g, sv['rq'], rk, rv, do, cb, sv['lse'], acc, B, S, name=t + f"dil{g}")
            dz = _rope_apply(acc[0], cos, sin, name=t + "rope_q", sign=-1.0, alias=dz)
            kv_parts.append(acc[1:])
        tok = on_backward('mix', l, dz, as3d(gl))
        gl['w_in'], = _mm(sv['h0'], dz, ta=True, tk=4096, name=t + "dw_in")
        dh0, = _mm(dz, wts[l]['w_in'], tb=True, b_layer=0, name=t + "d_h0", out_dtypes=(BF,))
        dx, d_ng[l][0] = _norm_bwd(dh0, sv['x_in'], sv['r0'], tied(norm_gains[l, 0], tok), name=t + "norm0", add=dx1)
        gl['w_mem_kv'], = _mm(sv['memn'], (dkm, dvm), ta=True, tn=256, name=t + "dw_memkv")
        dmemn, = _mm((dkm, dvm), wts[l]['w_mem_kv'], tb=True, b_layer=0, tk=256, name=t + "d_memn", out_dtypes=(BF,))
        _, d_memnorm[l] = _norm_bwd(dmemn, memf, sv['rm'], mem_norm[l], name=t + "norm_mem", out_dtype=BF, tm=256)
        if l == N_A:
            dk, dv = _kv_grad_sum(kv_parts, cos, sin, name="kv_grad")
            x_kv, kvn, rkv = kv_saved
            gl['w_kv'], = _mm(kvn, (dk, dv), ta=True, tn=768, tk=2048, name="dw_kv")
            dkvn, = _mm((dk, dv), wts[N_A - 1]['w_kv'], tb=True, b_layer=0, tn=1024, tk=768, name="d_kvn", out_dtypes=(BF,))
            dx, d_kvnorm = _norm_bwd(dkvn, x_kv, rkv, kv_norm, name="norm_kv_b", add=dx)
        tok = on_backward('end', l, dx, as3d(gl))

    small = {
        'norm_gains': jnp.stack([jnp.concatenate(d_ng[l], axis=0) for l in range(DEPTH)]),
        'mem_norm': jnp.concatenate(d_memnorm, axis=0),
        'kv_norm': d_kvnorm.reshape(D_MODEL),
        'pool_scale': jnp.concatenate(d_pscale, axis=0),
        'w_pool': jnp.stack([jnp.stack([d_wbd[l][gi * POOL_GROUP:(gi + 1) * POOL_GROUP, gi * POOL_GROUP:(gi + 1) * POOL_GROUP]
                                        for gi in range(len(POOL_WINDOWS))]) for l in range(N_A)]),
    }
    return loss, dx, small


SMALL_ORDER = ('norm_gains', 'mem_norm', 'kv_norm', 'pool_scale', 'w_pool')
SMALL_VEC_ROWS = 2560


def kernel(x, mem, positions, norm_gains, mem_norm, w_in, w_mem_kv, w_out, w_pool, pool_scale, kv_norm, w_kv, w_gate_up, w_down, loss_target, m_norm_gains, m_mem_norm, m_w_in, m_w_mem_kv, m_w_out, m_w_pool, m_pool_scale, m_kv_norm, m_w_kv, m_w_gate_up, m_w_down, v_norm_gains, v_mem_norm, v_w_in, v_w_mem_kv, v_w_out, v_w_pool, v_pool_scale, v_kv_norm, v_w_kv, v_w_gate_up, v_w_down):
    xi, yi, ci = lax.axis_index("x"), lax.axis_index("y"), lax.axis_index("c")
    s = 2 * xi + yi
    sc = jnp.stack([s, ci]).astype(jnp.int32)
    weights = dict(norm_gains=norm_gains, mem_norm=mem_norm, w_in=w_in, w_mem_kv=w_mem_kv, w_out=w_out, w_pool=w_pool,
                   pool_scale=pool_scale, kv_norm=kv_norm, w_kv=w_kv, w_gate_up=w_gate_up, w_down=w_down)
    moms = dict(norm_gains=m_norm_gains, mem_norm=m_mem_norm, w_in=m_w_in, w_mem_kv=m_w_mem_kv, w_out=m_w_out,
                w_pool=m_w_pool, pool_scale=m_pool_scale, kv_norm=m_kv_norm, w_kv=m_w_kv, w_gate_up=m_w_gate_up,
                w_down=m_w_down)
    vels = dict(norm_gains=v_norm_gains, mem_norm=v_mem_norm, w_in=v_w_in, w_mem_kv=v_w_mem_kv, w_out=v_w_out,
                w_pool=v_w_pool, pool_scale=v_pool_scale, kv_norm=v_kv_norm, w_kv=v_w_kv, w_gate_up=v_w_gate_up,
                w_down=v_w_down)

    small_w = jnp.zeros((SMALL_ROWS, 256), F32)
    small_w = lax.dynamic_update_slice(small_w, norm_gains.reshape(16, 256), (0, 0))
    small_w = lax.dynamic_update_slice(small_w, pool_scale, (16, 0))
    def shard_of(nm, l):
        return w_kv.astype(BF).reshape(_shard_shape('w_kv')) if nm == 'w_kv' else weights[nm][l:l + 1].astype(BF)

    groups = {'l0a': (0, MIX_W), 'l0b': (0, FFN_W)}
    groups.update({f"l{l}": (l, LAYER_W + (('w_kv',) if l == N_A - 1 else ())) for l in range(1, DEPTH)})
    on_ici, on_d2d, gathered = {}, {}, {}

    def start_group(tag, after):
        l, names = groups[tag]
        on_ici[tag] = _gather_start(tag, names, [shard_of(nm, l) for nm in names], small_w if tag == 'l0a' else None, sc,
                                    after)
        return on_ici[tag]['token'][0, 0]

    def on_forward(where, l, after):
        if where == 'start':
            if l == 0:
                start_group('l0a', None)
                st = on_ici.pop('l0a')
                fwd = _gather_forward(st, st['token'])
                w, small_all = _gather_finish(fwd, fwd['token'])
                return w, small_all, start_group('l0b', w['w_in'])
            gathered[l] = _gather_finish(on_d2d.pop(f"l{l}"), after)[0]
            tok = start_group(f"l{l + 1}", gathered[l]['w_in']) if l + 1 < DEPTH else None
            return {nm: w for nm, w in gathered[l].items() if nm not in FFN_W}, None, tok
        if where == 'mid' and l == 0:
            on_d2d['l0b'] = _gather_forward(on_ici.pop('l0b'), after)
            return start_group('l1', on_d2d['l0b']['token'])
        if where == 'ffn':
            if l == 0:
                return (_gather_finish(on_d2d.pop('l0b'), after)[0],)
            return ({nm: gathered[l][nm] for nm in FFN_W},)
        if where == 'post' and l + 1 < DEPTH:
            on_d2d[f"l{l + 1}"] = _gather_forward(on_ici.pop(f"l{l + 1}"), after)
        return None

    hook_of = {'ffn': 0, 'mix': 1, 'end': 2}
    active, reduced = [], {l: {} for l in range(DEPTH)}
    advance = {'mid': lambda st, after: _reduce_mid(st, after, sc), 'late': lambda st, after: _reduce_late(st, after, sc)}

    def run_hook(idx, after):
        toks = []
        for grp in list(active):
            while grp['plan'] and grp['plan'][0][1] <= idx:
                step = grp['plan'].pop(0)[0]
                if step == 'finish':
                    reduced[grp['layer']].update(_reduce_finish(grp['st'], after))
                    active.remove(grp)
                else:
                    grp['st'] = advance[step](grp['st'], after)
                    toks.append(grp['st']['token'][0, 0])
        return toks

    def on_backward(where, l, after, grads):
        idx = 3 * (DEPTH - 1 - l) + hook_of[where]
        toks = run_hook(idx, after)
        if where in ('ffn', 'end'):
            names = FFN_W if where == 'ffn' else tuple(nm for nm in grads if nm not in FFN_W)
            st = _reduce_start(f"l{l}_{where}_grads", names, {nm: grads[nm] for nm in names})
            plan = [('mid', idx + 1), ('late', idx + 3), ('finish', idx + 4)] if where == 'ffn' else \
                   [('mid', idx + 1), ('late', idx + 2), ('finish', idx + 3)]
            active.append(dict(layer=l, st=st, plan=plan))
            toks.append(st['token'][0, 0])
        return sum(toks) if toks else None

    loss, gx, gsmall = _local_step(x, mem, positions, on_forward, on_backward, mem_norm, w_pool, kv_norm, loss_target)
    loss = lax.psum(loss[0, 0], ("x", "y", "c"))

    vec = jnp.concatenate([gsmall[nm].reshape(-1) for nm in SMALL_ORDER])
    vec = jnp.pad(vec, (0, SMALL_VEC_ROWS * 128 - vec.shape[0])).reshape(SMALL_VEC_ROWS, 128)
    vec = vec + sum(grp['st']['token'][0, 0] for grp in active)
    small_st = _small_gather_start(vec.astype(BF), sc)
    outs = {nm: None for nm in LAYER_W}

    def adamw_layers(layers, names, after):
        for l in layers:
            for nm in names:
                outs[nm] = _adamw_layer(nm, l, weights[nm], reduced[l][nm], moms[nm], vels[nm], outs[nm], after)
                after = outs[nm][0]
        return after

    def zero_of(toks, st):
        return jnp.full((8, 128), sum(toks)) if toks else st['token']

    last = 3 * DEPTH
    toks = run_hook(last, small_st['token'])
    done = adamw_layers(range(DEPTH - 1, 0, -1), LAYER_W, zero_of(toks, small_st))
    toks = run_hook(last + 1, done)
    done = adamw_layers([0], FFN_W, zero_of(toks, small_st))
    tot = _sum8(_small_gather_finish(small_st, done), name="sum_small_grads", tr=512)
    run_hook(last + 2, tot)
    assert not active
    adamw_layers([0], MIX_W, None)
    tot = tot.reshape(-1)
    grads, off = {}, 0
    for nm in SMALL_ORDER:
        shape = (DEPTH, 4, D_MODEL) if nm == 'norm_gains' else (N_A, MAIN_W) if nm == 'pool_scale' else weights[nm].shape
        n = 1
        for dim in shape:
            n *= dim
        grads[nm] = tot[off:off + n].reshape(shape)
        off += n
    grads['norm_gains'] = lax.dynamic_slice(grads['norm_gains'], (0, 0, s * 256), (DEPTH, 4, 256))
    grads['pool_scale'] = lax.dynamic_slice(grads['pool_scale'], (0, s * POOL_GROUP), (N_A, POOL_GROUP))
    grads['w_kv'] = reduced[N_A]['w_kv'].reshape(w_kv.shape)

    order = ('norm_gains', 'mem_norm', 'w_in', 'w_mem_kv', 'w_out', 'w_pool', 'pool_scale', 'kv_norm', 'w_kv',
             'w_gate_up', 'w_down')
    deltas, new_m, new_v = {}, {}, {}
    for nm in order:
        if nm in LAYER_W:
            deltas[nm], new_m[nm], new_v[nm], grads[nm] = outs[nm]
        else:
            deltas[nm], new_m[nm], new_v[nm] = _adamw(weights[nm], grads[nm], moms[nm], vels[nm], name="adamw_" + nm)
    return (loss, gx.reshape(x.shape), *[grads[nm] for nm in order], *[deltas[nm] for nm in order],
            *[new_m[nm] for nm in order], *[new_v[nm] for nm in order])
```

```python
import functools

import jax
import jax.numpy as jnp
from jax import lax
from jax.experimental import pallas as pl
from jax.experimental.pallas import tpu as pltpu

F32 = jnp.float32
BF = jnp.bfloat16

D_MODEL = 1024
DEPTH = 4
N_A = 2
HEAD_DIM = 64
MEM_W = 256
MAIN_W = 768
D_FF = 2816
N_MEM = 256
POOL_WINDOWS = (2, 4, 8, 16)
POOL_GROUP = 192
DIL = (1, 4, 16)
STEPS = 128
ROPE_THETA = 10000.0
EPS = 1e-6
SCALE = HEAD_DIM ** -0.5
NEG = -1e30

ADAM_LR = 0.001
ADAM_B1 = 0.9
ADAM_B2 = 0.999
ADAM_EPS = 1e-08
ADAM_WD = 0.01
ADAM_STEP = 10

VMEM_LIMIT = 48 * 1024 * 1024
MESH = pl.DeviceIdType.MESH


def _cp(sem, **kw):
    return pltpu.CompilerParams(dimension_semantics=sem, vmem_limit_bytes=VMEM_LIMIT, **kw)


def _mm(a, b, *, name, ta=False, tb=False, tm=1024, tn=512, tk=1024, b_layer=None, b_offsets=(0,),
        extras=(), epilogue=None, out_dtypes=(F32,), out_n=None, stack=None):
    a_pair = isinstance(a, (tuple, list))
    b_pair = isinstance(b, (tuple, list))
    a0 = a[0] if a_pair else a
    b0 = b[0] if b_pair else b
    a_rows, a_cols = a0.shape
    if a_pair:
        a_cols *= 2
    b_rows, b_cols = b0.shape[-2:]
    if b_pair:
        b_cols *= 2
    M, K = (a_cols, a_rows) if ta else (a_rows, a_cols)
    N = b_rows if tb else b_cols
    if out_n is not None:
        N = out_n
    tm, tn, tk = min(tm, M), min(tn, N), min(tk, K)
    assert M % tm == 0 and N % tn == 0 and K % tk == 0, (name, M, N, K, tm, tn, tk)
    nk = K // tk
    n_acc = len(b_offsets)

    if a_pair:
        a_half = (a0.shape[1] // (tm if ta else tk))
    if b_pair:
        b_half = (b0.shape[1] // (tk if tb else tn))

    def a_map(sel):
        def f(i, j, k):
            r, c = (k, i) if ta else (i, k)
            if a_pair:
                c = jnp.clip(c - sel * a_half, 0, a_half - 1)
            return (r, c)
        return f

    def b_map(sel, off):
        def f(i, j, k):
            r, c = (j + off, k) if tb else (k, j + off)
            if b_pair:
                c = jnp.clip(c - sel * b_half, 0, b_half - 1)
            if b_layer is not None:
                return (b_layer, r, c)
            return (r, c)
        return f

    a_blk = (tk, tm) if ta else (tm, tk)
    b_blk = (tn, tk) if tb else (tk, tn)
    if b_layer is not None:
        b_blk = (None,) + b_blk
    in_specs, operands = [], []
    for sel in range(2 if a_pair else 1):
        in_specs.append(pl.BlockSpec(a_blk, a_map(sel)))
        operands.append(a[sel] if a_pair else a)
    n_a = len(operands)
    for off in b_offsets:
        for sel in range(2 if b_pair else 1):
            in_specs.append(pl.BlockSpec(b_blk, b_map(sel, off)))
            operands.append(b[sel] if b_pair else (_in_hbm(b) if b_layer is not None else b))
    n_b = len(operands) - n_a
    for arr, kind in extras:
        if kind == 'tile':
            in_specs.append(pl.BlockSpec((tm, tn), lambda i, j, k: (i, j)))
        elif kind == 'row':
            in_specs.append(pl.BlockSpec((tm, 1), lambda i, j, k: (i, 0)))
        else:
            in_specs.append(pl.BlockSpec((1, tn), lambda i, j, k: (0, j)))
        operands.append(arr)
    n_e = len(extras)
    n_o = len(out_dtypes)
    dims = (((0,) if ta else (1,), (1,) if tb else (0,)), ((), ()))

    def body(*refs):
        a_refs = refs[:n_a]
        b_refs = refs[n_a:n_a + n_b]
        e_refs = refs[n_a + n_b:n_a + n_b + n_e]
        n_in = n_a + n_b + n_e + (1 if stack is not None else 0)
        o_refs = refs[n_in:n_in + n_o]
        acc_refs = refs[n_in + n_o:]
        i, j, k = pl.program_id(0), pl.program_id(1), pl.program_id(2)
        if a_pair:
            cidx = i if ta else k
            av = jnp.where(cidx < a_half, a_refs[0][...], a_refs[1][...])
        else:
            av = a_refs[0][...]
        av = av.astype(BF)
        prods = []
        for q in range(n_acc):
            if b_pair:
                cidx = (k if tb else j) + b_offsets[q]
                bv = jnp.where(cidx < b_half, b_refs[2 * q][...], b_refs[2 * q + 1][...])
            else:
                bv = b_refs[q][...]
            prods.append(lax.dot_general(av, bv.astype(BF), dims, preferred_element_type=F32))

        def finish(accs):
            outs = epilogue(accs, *[r[...] for r in e_refs]) if epilogue is not None else accs
            for o_ref, o in zip(o_refs, outs):
                o_ref[...] = o.astype(o_ref.dtype)

        if nk == 1:
            finish(prods)
        else:
            @pl.when(k == 0)
            def _():
                for r, p in zip(acc_refs, prods):
                    r[...] = p

            @pl.when(k > 0)
            def _():
                for r, p in zip(acc_refs, prods):
                    r[...] += p

            @pl.when(k == nk - 1)
            def _():
                finish([r[...] for r in acc_refs])

    if stack is not None:
        buf, layer = stack
        assert n_o == 1 and buf.shape[1:] == (M, N)
        return pl.pallas_call(
            body, name=name,
            grid=(M // tm, N // tn, nk),
            in_specs=in_specs + [pl.BlockSpec(memory_space=pl.ANY)],
            out_specs=[pl.BlockSpec((None, tm, tn), lambda i, j, k: (layer, i, j))],
            out_shape=[jax.ShapeDtypeStruct(buf.shape, buf.dtype)],
            scratch_shapes=[pltpu.VMEM((tm, tn), F32) for _ in range(n_acc if nk > 1 else 0)],
            input_output_aliases={len(operands): 0},
            compiler_params=_cp(("parallel", "parallel", "arbitrary")),
        )(*operands, buf)[0]
    return pl.pallas_call(
        body, name=name,
        grid=(M // tm, N // tn, nk),
        in_specs=in_specs,
        out_specs=[pl.BlockSpec((tm, tn), lambda i, j, k: (i, j)) for _ in range(n_o)],
        out_shape=[jax.ShapeDtypeStruct((M, N), dt) for dt in out_dtypes],
        scratch_shapes=[pltpu.VMEM((tm, tn), F32) for _ in range(n_acc if nk > 1 else 0)],
        compiler_params=_cp(("parallel", "parallel", "arbitrary")),
    )(*operands)


def _norm_fwd(x, g, *, name, res=None, out_dtype=F32, tm=512):
    T, Dm = x.shape
    has_res = res is not None

    def body(*refs):
        if has_res:
            x_ref, g_ref, r_ref, y_ref, s_ref = refs
        else:
            x_ref, g_ref, y_ref, s_ref = refs
        xv = x_ref[...]
        rstd = lax.rsqrt(jnp.mean(xv * xv, axis=-1, keepdims=True) + EPS)
        y = xv * rstd * g_ref[...]
        if has_res:
            y = r_ref[...] + y
        y_ref[...] = y.astype(y_ref.dtype)
        s_ref[...] = rstd

    row = pl.BlockSpec((tm, Dm), lambda i: (i, 0))
    in_specs = [row, pl.BlockSpec((1, Dm), lambda i: (0, 0))] + ([row] if has_res else [])
    ops = [x, g.reshape(1, Dm)] + ([res] if has_res else [])
    return pl.pallas_call(
        body, name=name, grid=(T // tm,), in_specs=in_specs,
        out_specs=[row, pl.BlockSpec((tm, 1), lambda i: (i, 0))],
        out_shape=[jax.ShapeDtypeStruct((T, Dm), out_dtype), jax.ShapeDtypeStruct((T, 1), F32)],
        compiler_params=_cp(("parallel",)),
    )(*ops)


def _norm_bwd(dout, x, rstd, g, *, name, add=None, out_dtype=F32, tm=512):
    T, Dm = x.shape
    has_add = add is not None
    nt = T // tm

    def body(*refs):
        if has_add:
            do_ref, x_ref, s_ref, g_ref, a_ref, dx_ref, dg_ref, acc = refs
        else:
            do_ref, x_ref, s_ref, g_ref, dx_ref, dg_ref, acc = refs
        i = pl.program_id(0)
        do = do_ref[...].astype(F32)
        xh = x_ref[...] * s_ref[...]
        gd = do * g_ref[...]
        dx = s_ref[...] * (gd - xh * jnp.mean(gd * xh, axis=-1, keepdims=True))
        if has_add:
            dx = dx + a_ref[...].astype(F32)
        dx_ref[...] = dx.astype(dx_ref.dtype)
        part = jnp.sum((do * xh).reshape(tm // 8, 8, Dm), axis=0)

        @pl.when(i == 0)
        def _():
            acc[...] = part

        @pl.when(i > 0)
        def _():
            acc[...] += part

        @pl.when(i == nt - 1)
        def _():
            dg_ref[...] = jnp.sum(acc[...], axis=0, keepdims=True)

    row = pl.BlockSpec((tm, Dm), lambda i: (i, 0))
    in_specs = [row, row, pl.BlockSpec((tm, 1), lambda i: (i, 0)), pl.BlockSpec((1, Dm), lambda i: (0, 0))]
    ops = [dout, x, rstd, g.reshape(1, Dm)]
    if has_add:
        in_specs.append(row)
        ops.append(add)
    return pl.pallas_call(
        body, name=name, grid=(nt,), in_specs=in_specs,
        out_specs=[row, pl.BlockSpec((1, Dm), lambda i: (0, 0))],
        out_shape=[jax.ShapeDtypeStruct((T, Dm), out_dtype), jax.ShapeDtypeStruct((1, Dm), F32)],
        scratch_shapes=[pltpu.VMEM((8, Dm), F32)],
        compiler_params=_cp(("arbitrary",)),
    )(*ops)


def _swiglu_fwd_epilogue(accs):
    g, u = accs
    return g, u, g * jax.nn.sigmoid(g) * u


def _swiglu_bwd_epilogue(accs, g, u):
    da = accs[0]
    g = g.astype(F32)
    u = u.astype(F32)
    sig = jax.nn.sigmoid(g)
    return da * u * (sig * (1.0 + g * (1.0 - sig))), da * (g * sig)


def _rope_tables(pos, *, name, tm=1024):
    T = pos.shape[0]
    half = HEAD_DIM // 2
    freqs = ROPE_THETA ** (-jnp.arange(half, dtype=F32) / half)
    freqs = jnp.tile(freqs, 4).reshape(1, 128)

    def body(p_ref, f_ref, c_ref, s_ref):
        ang = p_ref[...].astype(F32) * f_ref[...]
        lane = lax.broadcasted_iota(jnp.int32, ang.shape, 1)
        c_ref[...] = jnp.cos(ang)
        s_ref[...] = jnp.where(lane % HEAD_DIM < half, -1.0, 1.0) * jnp.sin(ang)

    tab = pl.BlockSpec((tm, 128), lambda i: (i, 0))
    return pl.pallas_call(
        body, name=name, grid=(T // tm,),
        in_specs=[pl.BlockSpec((tm, 1), lambda i: (i, 0)), pl.BlockSpec((1, 128), lambda i: (0, 0))],
        out_specs=[tab, tab],
        out_shape=[jax.ShapeDtypeStruct((T, 128), F32)] * 2,
        compiler_params=_cp(("parallel",)),
    )(pos, freqs)


def _rot(x, cos, sin, sign):
    W = x.shape[1]
    half = HEAD_DIM // 2
    reps = W // 128
    c = jnp.concatenate([cos] * reps, axis=1) if reps > 1 else cos
    s = jnp.concatenate([sin] * reps, axis=1) if reps > 1 else sin
    lane = lax.broadcasted_iota(jnp.int32, x.shape, 1)
    swapped = jnp.where(lane % HEAD_DIM < half, pltpu.roll(x, W - half, axis=1), pltpu.roll(x, half, axis=1))
    return x * c + (sign * s) * swapped


def _rope_apply(x, cos, sin, *, name, sign=1.0, width=MAIN_W, passthrough=False, out_dtype=BF, alias=None,
                out_cols=None, tm=512):
    T = x.shape[0]

    def body(*refs):
        if passthrough:
            x_ref, v_ref, c_ref, s_ref, o_ref, ov_ref = refs
            ov_ref[...] = v_ref[...].astype(ov_ref.dtype)
        elif alias is not None:
            x_ref, c_ref, s_ref, _, o_ref = refs
        else:
            x_ref, c_ref, s_ref, o_ref = refs
        o_ref[...] = _rot(x_ref[...].astype(F32), c_ref[...], s_ref[...], sign).astype(o_ref.dtype)

    blk0 = pl.BlockSpec((tm, width), lambda i: (i, 0))
    blk1 = pl.BlockSpec((tm, width), lambda i: (i, 1))
    tab = pl.BlockSpec((tm, 128), lambda i: (i, 0))
    if passthrough:
        return pl.pallas_call(
            body, name=name, grid=(T // tm,), in_specs=[blk0, blk1, tab, tab], out_specs=[blk0, blk0],
            out_shape=[jax.ShapeDtypeStruct((T, width), out_dtype)] * 2,
            compiler_params=_cp(("parallel",)),
        )(x, x, cos, sin)
    if alias is not None:
        return pl.pallas_call(
            body, name=name, grid=(T // tm,),
            in_specs=[blk0, tab, tab, pl.BlockSpec(memory_space=pl.ANY)], out_specs=blk0,
            out_shape=jax.ShapeDtypeStruct(alias.shape, alias.dtype),
            input_output_aliases={3: 0},
            compiler_params=_cp(("parallel",)),
        )(x, cos, sin, alias)
    return pl.pallas_call(
        body, name=name, grid=(T // tm,), in_specs=[blk0, tab, tab], out_specs=blk0,
        out_shape=jax.ShapeDtypeStruct((T, width), out_dtype),
        compiler_params=_cp(("parallel",)),
    )(x, cos, sin)


POOL_T = 256
POOL_HALO = 16


def _pool_lane_window(shape):
    lane = lax.broadcasted_iota(jnp.int32, shape, 1)
    w = jnp.full(shape, POOL_WINDOWS[0], jnp.int32)
    for gi in range(1, len(POOL_WINDOWS)):
        w = jnp.where(lane >= gi * POOL_GROUP, POOL_WINDOWS[gi], w)
    return w


def _pool_fwd(z, wbd, scale, B, S, *, name):
    T = z.shape[0]
    nt = S // POOL_T
    hb = POOL_T // POOL_HALO

    def body(z_ref, h_ref, w_ref, sc_ref, y_ref, p_ref, ext):
        i = pl.program_id(1)
        u = z_ref[...]
        ext[pl.ds(POOL_HALO, POOL_T), :] = u
        ext[pl.ds(0, POOL_HALO), :] = jnp.where(i > 0, h_ref[...], 0.0)
        win = _pool_lane_window((POOL_T, MAIN_W))
        acc = u
        for k in range(1, POOL_HALO):
            acc = acc + jnp.where(k < win, ext[pl.ds(POOL_HALO - k, POOL_T), :], 0.0)
        t = i * POOL_T + lax.broadcasted_iota(jnp.int32, (POOL_T, MAIN_W), 0)
        cnt = jnp.minimum(t + 1, win).astype(F32)
        p = (acc / cnt - u).astype(BF)
        p_ref[...] = p
        y = jnp.dot(p, w_ref[...], preferred_element_type=F32) * sc_ref[...]
        y_ref[...] = y.astype(y_ref.dtype)

    return pl.pallas_call(
        body, name=name, grid=(B, nt),
        in_specs=[pl.BlockSpec((POOL_T, MAIN_W), lambda b, i: (b * nt + i, 0)),
                  pl.BlockSpec((POOL_HALO, MAIN_W), lambda b, i: (jnp.maximum((b * nt + i) * hb - 1, 0), 0)),
                  pl.BlockSpec((MAIN_W, MAIN_W), lambda b, i: (0, 0)),
                  pl.BlockSpec((1, MAIN_W), lambda b, i: (0, 0))],
        out_specs=[pl.BlockSpec((POOL_T, MAIN_W), lambda b, i: (b * nt + i, 0)),
                   pl.BlockSpec((POOL_T, MAIN_W), lambda b, i: (b * nt + i, 0))],
        out_shape=[jax.ShapeDtypeStruct((T, D_MODEL), BF), jax.ShapeDtypeStruct((T, MAIN_W), BF)],
        scratch_shapes=[pltpu.VMEM((POOL_T + POOL_HALO, MAIN_W), F32)],
        compiler_params=_cp(("parallel", "parallel")),
    )(z, z, wbd, scale.reshape(1, MAIN_W))


def _pool_bwd(dy, p, wbd, scale, dz_alias, B, S, *, name):
    T = dy.shape[0]
    nt = S // POOL_T
    hb = POOL_T // POOL_HALO
    last_halo = T // POOL_HALO - 1
    R = POOL_T + POOL_HALO

    def body(dy_ref, dyn_ref, p_ref, pn_ref, w_ref, sc_ref, _, dz_ref, dw_ref, ds_ref, ext, dw_acc, ds_acc):
        b, i = pl.program_id(0), pl.program_id(1)
        first = jnp.logical_and(b == 0, i == 0)
        dyv = dy_ref[...]
        pv = p_ref[...]
        sc = sc_ref[...]
        w = w_ref[...]
        pw = jnp.dot(pv, w, preferred_element_type=F32)
        ds_part = jnp.sum((dyv * pw).reshape(POOL_T // 8, 8, MAIN_W), axis=0)
        dpw = (dyv * sc).astype(BF)
        dw_part = lax.dot_general(pv, dpw, (((0,), (0,)), ((), ())), preferred_element_type=F32)

        @pl.when(first)
        def _():
            dw_acc[...] = dw_part
            ds_acc[...] = ds_part

        @pl.when(jnp.logical_not(first))
        def _():
            dw_acc[...] += dw_part
            ds_acc[...] += ds_part

        @pl.when(jnp.logical_and(b == pl.num_programs(0) - 1, i == nt - 1))
        def _():
            dw_ref[...] = dw_acc[...]
            ds_ref[...] = jnp.sum(ds_acc[...], axis=0, keepdims=True)

        dp = lax.dot_general(dpw, w, (((1,), (1,)), ((), ())), preferred_element_type=F32)
        dpn = lax.dot_general((dyn_ref[...] * sc).astype(BF), w, (((1,), (1,)), ((), ())), preferred_element_type=F32)
        win = _pool_lane_window((POOL_T, MAIN_W))
        win_n = _pool_lane_window((POOL_HALO, MAIN_W))
        t = i * POOL_T + lax.broadcasted_iota(jnp.int32, (POOL_T, MAIN_W), 0)
        tn = (i + 1) * POOL_T + lax.broadcasted_iota(jnp.int32, (POOL_HALO, MAIN_W), 0)
        ext[pl.ds(0, POOL_T), :] = dp / jnp.minimum(t + 1, win).astype(F32)
        ext[pl.ds(POOL_T, POOL_HALO), :] = jnp.where(i < nt - 1, dpn / jnp.minimum(tn + 1, win_n).astype(F32), 0.0)
        acc = -dp
        for k in range(POOL_HALO):
            acc = acc + jnp.where(k < win, ext[pl.ds(k, POOL_T), :], 0.0)
        dz_ref[...] = acc.astype(dz_ref.dtype)

    cur = lambda b, i: (b * nt + i, 0)
    nxt = lambda b, i: (jnp.minimum((b * nt + i + 1) * hb, last_halo), 0)
    return pl.pallas_call(
        body, name=name, grid=(B, nt),
        in_specs=[pl.BlockSpec((POOL_T, MAIN_W), cur), pl.BlockSpec((POOL_HALO, MAIN_W), nxt),
                  pl.BlockSpec((POOL_T, MAIN_W), cur), pl.BlockSpec((POOL_HALO, MAIN_W), nxt),
                  pl.BlockSpec((MAIN_W, MAIN_W), lambda b, i: (0, 0)),
                  pl.BlockSpec((1, MAIN_W), lambda b, i: (0, 0)),
                  pl.BlockSpec(memory_space=pl.ANY)],
        out_specs=[pl.BlockSpec((POOL_T, MAIN_W), cur),
                   pl.BlockSpec((MAIN_W, MAIN_W), lambda b, i: (0, 0)),
                   pl.BlockSpec((1, MAIN_W), lambda b, i: (0, 0))],
        out_shape=[jax.ShapeDtypeStruct(dz_alias.shape, dz_alias.dtype),
                   jax.ShapeDtypeStruct((MAIN_W, MAIN_W), F32), jax.ShapeDtypeStruct((1, MAIN_W), F32)],
        scratch_shapes=[pltpu.VMEM((R, MAIN_W), F32), pltpu.VMEM((MAIN_W, MAIN_W), F32), pltpu.VMEM((8, MAIN_W), F32)],
        input_output_aliases={6: 0},
        compiler_params=_cp(("arbitrary", "arbitrary")),
    )(dy, dy, p, p, wbd, scale.reshape(1, MAIN_W), dz_alias)


def _head_masks(shape):
    lane = lax.broadcasted_iota(jnp.int32, shape, 1)
    return [(lane // HEAD_DIM) == h for h in range(shape[1] // HEAD_DIM)]


def _row_of(bcast, mask):
    return jnp.max(jnp.where(mask, bcast, -jnp.inf), axis=-1, keepdims=True)


MEM_TQ = 512


def _memattn_fwd(z, kv, y_alias, B, S, *, name):
    T = z.shape[0]
    nt = S // MEM_TQ

    def body(q_ref, k_ref, v_ref, _, y_ref, l_ref):
        q = q_ref[...]
        k = k_ref[...]
        v = v_ref[...]
        masks = _head_masks(q.shape)
        o = jnp.zeros(q.shape, F32)
        lse_b = jnp.zeros(q.shape, F32)
        for m in masks:
            qm = jnp.where(m, q, 0.0).astype(BF)
            s = lax.dot_general(qm, k, (((1,), (1,)), ((), ())), preferred_element_type=F32) * SCALE
            mx = jnp.max(s, axis=-1, keepdims=True)
            e = jnp.exp(s - mx)
            l = jnp.sum(e, axis=-1, keepdims=True)
            p = (e / l).astype(BF)
            o = o + jnp.where(m, jnp.dot(p, v, preferred_element_type=F32), 0.0)
            lse_b = lse_b + jnp.where(m, mx + jnp.log(l), 0.0)
        y_ref[...] = o.astype(y_ref.dtype)
        l_ref[...] = lse_b

    qblk = pl.BlockSpec((MEM_TQ, MEM_W), lambda b, i: (b * nt + i, 3))
    return pl.pallas_call(
        body, name=name, grid=(B, nt),
        in_specs=[qblk, pl.BlockSpec((N_MEM, MEM_W), lambda b, i: (b, 0)), pl.BlockSpec((N_MEM, MEM_W), lambda b, i: (b, 1)),
                  pl.BlockSpec(memory_space=pl.ANY)],
        out_specs=[qblk, pl.BlockSpec((MEM_TQ, MEM_W), lambda b, i: (b * nt + i, 0))],
        out_shape=[jax.ShapeDtypeStruct(y_alias.shape, y_alias.dtype), jax.ShapeDtypeStruct((T, MEM_W), F32)],
        input_output_aliases={3: 0},
        compiler_params=_cp(("parallel", "parallel")),
    )(z, kv, kv, y_alias)


def _memattn_bwd(dy, z, kv, lse, dz_alias, B, S, *, name):
    nt = S // MEM_TQ

    def body(do_ref, q_ref, k_ref, v_ref, l_ref, _, dz_ref, dk_ref, dv_ref, dk_acc, dv_acc):
        i = pl.program_id(1)
        do = do_ref[...]
        q = q_ref[...]
        k = k_ref[...]
        v = v_ref[...]
        lse_b = l_ref[...]
        masks = _head_masks(q.shape)
        dq = jnp.zeros(q.shape, F32)
        dk = jnp.zeros(k.shape, F32)
        dv = jnp.zeros(v.shape, F32)
        for m in masks:
            qm = jnp.where(m, q, 0.0).astype(BF)
            dom = jnp.where(m, do, 0.0).astype(BF)
            s = lax.dot_general(qm, k, (((1,), (1,)), ((), ())), preferred_element_type=F32) * SCALE
            p = jnp.exp(s - _row_of(lse_b, m))
            dp = lax.dot_general(dom, v, (((1,), (1,)), ((), ())), preferred_element_type=F32)
            delta = jnp.sum(p * dp, axis=-1, keepdims=True)
            ds = (p * (dp - delta) * SCALE).astype(BF)
            pb = p.astype(BF)
            dv = dv + jnp.where(m[:N_MEM], lax.dot_general(pb, dom, (((0,), (0,)), ((), ())), preferred_element_type=F32), 0.0)
            dk = dk + jnp.where(m[:N_MEM], lax.dot_general(ds, qm, (((0,), (0,)), ((), ())), preferred_element_type=F32), 0.0)
            dq = dq + jnp.where(m, jnp.dot(ds, k, preferred_element_type=F32), 0.0)
        dz_ref[...] = dq.astype(dz_ref.dtype)

        @pl.when(i == 0)
        def _():
            dk_acc[...] = dk
            dv_acc[...] = dv

        @pl.when(i > 0)
        def _():
            dk_acc[...] += dk
            dv_acc[...] += dv

        @pl.when(i == nt - 1)
        def _():
            dk_ref[...] = dk_acc[...]
            dv_ref[...] = dv_acc[...]

    qblk = pl.BlockSpec((MEM_TQ, MEM_W), lambda b, i: (b * nt + i, 3))
    kblk = pl.BlockSpec((N_MEM, MEM_W), lambda b, i: (b, 0))
    return pl.pallas_call(
        body, name=name, grid=(B, nt),
        in_specs=[qblk, qblk, kblk, pl.BlockSpec((N_MEM, MEM_W), lambda b, i: (b, 1)),
                  pl.BlockSpec((MEM_TQ, MEM_W), lambda b, i: (b * nt + i, 0)), pl.BlockSpec(memory_space=pl.ANY)],
        out_specs=[qblk, kblk, kblk],
        out_shape=[jax.ShapeDtypeStruct(dz_alias.shape, dz_alias.dtype),
                   jax.ShapeDtypeStruct((B * N_MEM, MEM_W), F32), jax.ShapeDtypeStruct((B * N_MEM, MEM_W), F32)],
        scratch_shapes=[pltpu.VMEM((N_MEM, MEM_W), F32), pltpu.VMEM((N_MEM, MEM_W), F32)],
        input_output_aliases={5: 0},
        compiler_params=_cp(("parallel", "arbitrary")),
    )(dy, z, kv, kv, lse, dz_alias)


def _dil_scores(qm, kp, kc, n):
    qi = lax.broadcasted_iota(jnp.int32, (STEPS, STEPS), 0)
    kj = lax.broadcasted_iota(jnp.int32, (STEPS, STEPS), 1)
    sc = lax.dot_general(qm, kc, (((1,), (1,)), ((), ())), preferred_element_type=F32) * SCALE
    sc = jnp.where(kj <= qi, sc, NEG)
    if kp is None:
        return None, sc
    sp = lax.dot_general(qm, kp, (((1,), (1,)), ((), ())), preferred_element_type=F32) * SCALE
    sp = jnp.where(jnp.logical_and(kj >= qi, n > 0), sp, NEG)
    return sp, sc


def _dil_specs(g, d, nb):
    chunk = STEPS * d
    cur = pl.BlockSpec((chunk, 128), lambda b, n, hf: (b * nb + n, g * 2 + hf))
    prev = pl.BlockSpec((chunk, 128), lambda b, n, hf: (b * nb + jnp.maximum(n - 1, 0), g * 2 + hf))
    return cur, prev


def _dil_rows(r, d):
    return pl.ds(r, STEPS, stride=d) if d > 1 else slice(None)


def _dil_loop(d, fn):
    if d <= 4:
        for r in range(d):
            fn(r)
    else:
        lax.fori_loop(0, d, lambda r, carry: (fn(r), carry)[1], 0)


def _dil_fwd_group(g, q, k, v, o_alias, l_alias, B, S, *, name):
    d = DIL[g]
    nb = S // (STEPS * d)
    has_prev = nb > 1

    def body(*refs):
        if has_prev:
            q_ref, kp_ref, kc_ref, vp_ref, vc_ref, _, __, o_ref, l_ref = refs
        else:
            q_ref, kc_ref, vc_ref, _, __, o_ref, l_ref = refs
        n = pl.program_id(1)

        def residue(r):
            rows = _dil_rows(r, d)
            q = q_ref[rows, :]
            kc, vc = kc_ref[rows, :].astype(BF), vc_ref[rows, :].astype(BF)
            kp = kp_ref[rows, :].astype(BF) if has_prev else None
            vp = vp_ref[rows, :].astype(BF) if has_prev else None
            o = jnp.zeros(q.shape, F32)
            lse_b = jnp.zeros(q.shape, F32)
            for m in _head_masks(q.shape):
                qm = jnp.where(m, q, 0.0).astype(BF)
                sp, sc = _dil_scores(qm, kp, kc, n)
                mx = jnp.max(sc, axis=-1, keepdims=True)
                if has_prev:
                    mx = jnp.maximum(mx, jnp.max(sp, axis=-1, keepdims=True))
                l = jnp.sum(jnp.exp(sc - mx), axis=-1, keepdims=True)
                if has_prev:
                    l = l + jnp.sum(jnp.exp(sp - mx), axis=-1, keepdims=True)
                lse = mx + jnp.log(l)
                oh = jnp.dot(jnp.exp(sc - lse).astype(BF), vc, preferred_element_type=F32)
                if has_prev:
                    oh = oh + jnp.dot(jnp.exp(sp - lse).astype(BF), vp, preferred_element_type=F32)
                o = o + jnp.where(m, oh, 0.0)
                lse_b = lse_b + jnp.where(m, lse, 0.0)
            o_ref[rows, :] = o
            l_ref[rows, :] = lse_b

        _dil_loop(d, residue)

    cur, prev = _dil_specs(g, d, nb)
    anyspec = pl.BlockSpec(memory_space=pl.ANY)
    if has_prev:
        in_specs, ops = [cur, prev, cur, prev, cur], [q, k, k, v, v]
    else:
        in_specs, ops = [cur, cur, cur], [q, k, v]
    n_in = len(ops)
    o, l = pl.pallas_call(
        body, name=name, grid=(B, nb, 2),
        in_specs=in_specs + [anyspec, anyspec],
        out_specs=[cur, cur],
        out_shape=[jax.ShapeDtypeStruct(q.shape, F32)] * 2,
        input_output_aliases={n_in: 0, n_in + 1: 1},
        compiler_params=_cp(("parallel", "parallel", "parallel")),
    )(*ops, o_alias, l_alias)
    return o, l


def _dil_bwd_group(g, q, k, v, do, cb, lse, aliases, B, S, *, name):
    d = DIL[g]
    nb = S // (STEPS * d)
    has_prev = nb > 1
    n_out = 5 if has_prev else 3

    def body(*refs):
        if has_prev:
            q_ref, kp_ref, kc_ref, vp_ref, vc_ref, do_ref, c_ref, l_ref = refs[:8]
            dq_ref, dkc_ref, dvc_ref, dkp_ref, dvp_ref = refs[8 + n_out:]
        else:
            q_ref, kc_ref, vc_ref, do_ref, c_ref, l_ref = refs[:6]
            dq_ref, dkc_ref, dvc_ref = refs[6 + n_out:]
        n = pl.program_id(1)
        tdot = lambda a, b: lax.dot_general(a, b, (((0,), (0,)), ((), ())), preferred_element_type=F32)
        ndot = lambda a, b: lax.dot_general(a, b, (((1,), (1,)), ((), ())), preferred_element_type=F32)

        def residue(r):
            rows = _dil_rows(r, d)
            q = q_ref[rows, :]
            kc, vc = kc_ref[rows, :].astype(BF), vc_ref[rows, :].astype(BF)
            kp = kp_ref[rows, :].astype(BF) if has_prev else None
            vp = vp_ref[rows, :].astype(BF) if has_prev else None
            do = do_ref[rows, :]
            cbv = c_ref[rows, :]
            lse_b = l_ref[rows, :]
            z = jnp.zeros(q.shape, F32)
            dq, dkc, dkp, dvc, dvp = z, z, z, z, z
            for m in _head_masks(q.shape):
                qm = jnp.where(m, q, 0.0).astype(BF)
                dom = jnp.where(m, do, 0.0).astype(BF)
                sp, sc = _dil_scores(qm, kp, kc, n)
                lse = _row_of(lse_b, m)
                c = _row_of(cbv, m)
                pc = jnp.exp(sc - lse)
                dsc = (pc * (ndot(dom, vc) + c) * SCALE).astype(BF)
                dqh = jnp.dot(dsc, kc, preferred_element_type=F32)
                dkc = dkc + jnp.where(m, tdot(dsc, qm), 0.0)
                dvc = dvc + jnp.where(m, tdot(pc.astype(BF), dom), 0.0)
                if has_prev:
                    pp = jnp.exp(sp - lse)
                    dsp = (pp * (ndot(dom, vp) + c) * SCALE).astype(BF)
                    dqh = dqh + jnp.dot(dsp, kp, preferred_element_type=F32)
                    dkp = dkp + jnp.where(m, tdot(dsp, qm), 0.0)
                    dvp = dvp + jnp.where(m, tdot(pp.astype(BF), dom), 0.0)
                dq = dq + jnp.where(m, dqh, 0.0)
            dq_ref[rows, :] = dq
            dkc_ref[rows, :] = dkc
            dvc_ref[rows, :] = dvc
            if has_prev:
                dkp_ref[rows, :] = dkp
                dvp_ref[rows, :] = dvp

        _dil_loop(d, residue)

    cur, prev = _dil_specs(g, d, nb)
    anyspec = pl.BlockSpec(memory_space=pl.ANY)
    dq_a, dkc_a, dkp_a, dvc_a, dvp_a = aliases
    if has_prev:
        in_specs, ops = [cur, prev, cur, prev, cur, cur, cur, cur], [q, k, k, v, v, do, cb, lse]
        al = [dq_a, dkc_a, dvc_a, dkp_a, dvp_a]
    else:
        in_specs, ops = [cur, cur, cur, cur, cur, cur], [q, k, v, do, cb, lse]
        al = [dq_a, dkc_a, dvc_a]
    n_in = len(ops)
    outs = pl.pallas_call(
        body, name=name, grid=(B, nb, 2),
        in_specs=in_specs + [anyspec] * n_out,
        out_specs=[cur] * n_out,
        out_shape=[jax.ShapeDtypeStruct(q.shape, F32)] * n_out,
        input_output_aliases={n_in + i: i for i in range(n_out)},
        compiler_params=_cp(("parallel", "parallel", "parallel")),
    )(*ops, *al)
    if has_prev:
        dq_a, dkc_a, dvc_a, dkp_a, dvp_a = outs
    else:
        dq_a, dkc_a, dvc_a = outs
    return dq_a, dkc_a, dkp_a, dvc_a, dvp_a


N_UNITS = 16


def _unit_rows(g):
    d = DIL[g]
    nb = N_UNITS // d
    return [pl.ds(n * STEPS * d + r, STEPS, stride=d) if d > 1 else pl.ds(n * STEPS, STEPS)
            for n in range(nb) for r in range(d)]


def _load_units(ref, g):
    if DIL[g] == 1:
        return ref[...].reshape(N_UNITS, STEPS, 128)
    return jnp.stack([ref[rows, :] for rows in _unit_rows(g)])


def _store_units(ref, val, g):
    if DIL[g] == 1:
        ref[...] = val.reshape(N_UNITS * STEPS, 128)
    else:
        for u, rows in enumerate(_unit_rows(g)):
            ref[rows, :] = val[u]


def _shift_units(x, by):
    z = jnp.zeros((abs(by),) + x.shape[1:], x.dtype)
    return jnp.concatenate([z, x[:N_UNITS - by]], axis=0) if by > 0 else jnp.concatenate([x[-by:], z], axis=0)


def _bdot(a, b, ca, cb):
    return lax.dot_general(a, b, (((ca,), (cb,)), ((0,), (0,))), preferred_element_type=F32)


def _dil_masks(g):
    d = DIL[g]
    has_prev = N_UNITS // d > 1
    qi = lax.broadcasted_iota(jnp.int32, (1, STEPS, STEPS), 1)
    kj = lax.broadcasted_iota(jnp.int32, (1, STEPS, STEPS), 2)
    unit = lax.broadcasted_iota(jnp.int32, (N_UNITS, 1, 1), 0)
    cur = kj <= qi
    prev = jnp.logical_and(kj >= qi, unit >= d) if has_prev else None
    lane = lax.broadcasted_iota(jnp.int32, (1, 1, 128), 2)
    heads = [(lane // HEAD_DIM) == h for h in range(128 // HEAD_DIM)]
    return has_prev, cur, prev, heads


def _dil_fwd(g, q, k, v, o_alias, l_alias, B, S, *, name):
    assert S == N_UNITS * STEPS
    d = DIL[g]

    def body(q_ref, k_ref, v_ref, _, __, o_ref, l_ref):
        has_prev, cur, prev, heads = _dil_masks(g)
        q = _load_units(q_ref, g)
        kc = _load_units(k_ref, g).astype(BF)
        vc = _load_units(v_ref, g).astype(BF)
        if has_prev:
            kp, vp = _shift_units(kc, d), _shift_units(vc, d)
        o = jnp.zeros(q.shape, F32)
        lse_b = jnp.zeros(q.shape, F32)
        for m in heads:
            qm = jnp.where(m, q, 0.0).astype(BF)
            sc = jnp.where(cur, _bdot(qm, kc, 2, 2) * SCALE, NEG)
            mx = jnp.max(sc, axis=-1, keepdims=True)
            if has_prev:
                sp = jnp.where(prev, _bdot(qm, kp, 2, 2) * SCALE, NEG)
                mx = jnp.maximum(mx, jnp.max(sp, axis=-1, keepdims=True))
            l = jnp.sum(jnp.exp(sc - mx), axis=-1, keepdims=True)
            if has_prev:
                l = l + jnp.sum(jnp.exp(sp - mx), axis=-1, keepdims=True)
            lse = mx + jnp.log(l)
            oh = _bdot(jnp.exp(sc - lse).astype(BF), vc, 2, 1)
            if has_prev:
                oh = oh + _bdot(jnp.exp(sp - lse).astype(BF), vp, 2, 1)
            o = o + jnp.where(m, oh, 0.0)
            lse_b = lse_b + jnp.where(m, lse, 0.0)
        _store_units(o_ref, o, g)
        _store_units(l_ref, lse_b, g)

    blk = pl.BlockSpec((S, 128), lambda b, hf: (b, g * 2 + hf))
    anyspec = pl.BlockSpec(memory_space=pl.ANY)
    o, l = pl.pallas_call(
        body, name=name, grid=(B, 2),
        in_specs=[blk, blk, blk, anyspec, anyspec], out_specs=[blk, blk],
        out_shape=[jax.ShapeDtypeStruct(q.shape, F32)] * 2,
        input_output_aliases={3: 0, 4: 1},
        compiler_params=_cp(("parallel", "parallel")),
    )(q, k, v, o_alias, l_alias)
    return o, l


def _dil_bwd(g, q, k, v, do, cb, lse, aliases, B, S, *, name):
    assert S == N_UNITS * STEPS
    d = DIL[g]

    def body(q_ref, k_ref, v_ref, do_ref, c_ref, l_ref, _, __, ___, dq_ref, dk_ref, dv_ref):
        has_prev, cur, prev, heads = _dil_masks(g)
        q = _load_units(q_ref, g)
        kc = _load_units(k_ref, g).astype(BF)
        vc = _load_units(v_ref, g).astype(BF)
        do = _load_units(do_ref, g)
        cbv = _load_units(c_ref, g)
        lse_b = _load_units(l_ref, g)
        if has_prev:
            kp, vp = _shift_units(kc, d), _shift_units(vc, d)
        z = jnp.zeros(q.shape, F32)
        dq, dkc, dkp, dvc, dvp = z, z, z, z, z
        for m in heads:
            qm = jnp.where(m, q, 0.0).astype(BF)
            dom = jnp.where(m, do, 0.0).astype(BF)
            lse = jnp.max(jnp.where(m, lse_b, -jnp.inf), axis=-1, keepdims=True)
            c = jnp.max(jnp.where(m, cbv, -jnp.inf), axis=-1, keepdims=True)
            sc = jnp.where(cur, _bdot(qm, kc, 2, 2) * SCALE, NEG)
            pc = jnp.exp(sc - lse)
            dsc = (pc * (_bdot(dom, vc, 2, 2) + c) * SCALE).astype(BF)
            dqh = _bdot(dsc, kc, 2, 1)
            dkc = dkc + jnp.where(m, _bdot(dsc, qm, 1, 1), 0.0)
            dvc = dvc + jnp.where(m, _bdot(pc.astype(BF), dom, 1, 1), 0.0)
            if has_prev:
                sp = jnp.where(prev, _bdot(qm, kp, 2, 2) * SCALE, NEG)
                pp = jnp.exp(sp - lse)
                dsp = (pp * (_bdot(dom, vp, 2, 2) + c) * SCALE).astype(BF)
                dqh = dqh + _bdot(dsp, kp, 2, 1)
                dkp = dkp + jnp.where(m, _bdot(dsp, qm, 1, 1), 0.0)
                dvp = dvp + jnp.where(m, _bdot(pp.astype(BF), dom, 1, 1), 0.0)
            dq = dq + jnp.where(m, dqh, 0.0)
        if has_prev:
            dkc = dkc + _shift_units(dkp, -d)
            dvc = dvc + _shift_units(dvp, -d)
        _store_units(dq_ref, dq, g)
        _store_units(dk_ref, dkc, g)
        _store_units(dv_ref, dvc, g)

    blk = pl.BlockSpec((S, 128), lambda b, hf: (b, g * 2 + hf))
    anyspec = pl.BlockSpec(memory_space=pl.ANY)
    return tuple(pl.pallas_call(
        body, name=name, grid=(B, 2),
        in_specs=[blk] * 6 + [anyspec] * 3, out_specs=[blk] * 3,
        out_shape=[jax.ShapeDtypeStruct(q.shape, F32)] * 3,
        input_output_aliases={6: 0, 7: 1, 8: 2},
        compiler_params=_cp(("parallel", "parallel")),
    )(q, k, v, do, cb, lse, *aliases))


def _kv_grad_sum(parts, cos, sin, *, name, tm=512):
    T = parts[0][0].shape[0]
    n_l = len(parts)

    def body(*refs):
        c_ref, s_ref = refs[0], refs[1]
        dk_ref, dv_ref = refs[2 + 2 * n_l:]
        dk = refs[2][...]
        dv = refs[3][...]
        for li in range(1, n_l):
            dk = dk + refs[2 + 2 * li][...]
            dv = dv + refs[3 + 2 * li][...]
        dk_ref[...] = _rot(dk, c_ref[...], s_ref[...], -1.0).astype(dk_ref.dtype)
        dv_ref[...] = dv.astype(dv_ref.dtype)

    full = pl.BlockSpec((tm, MAIN_W), lambda i: (i, 0))
    tab = pl.BlockSpec((tm, 128), lambda i: (i, 0))
    ops = [cos, sin] + [t for part in parts for t in part]
    return pl.pallas_call(
        body, name=name, grid=(T // tm,), in_specs=[tab, tab] + [full] * (2 * n_l), out_specs=[full, full],
        out_shape=[jax.ShapeDtypeStruct((T, MAIN_W), BF)] * 2,
        compiler_params=_cp(("parallel",)),
    )(*ops)


def _group_softmax(lse):
    l0, l1, l2 = lse[:, 0:256], lse[:, 256:512], lse[:, 512:768]
    mx = jnp.maximum(jnp.maximum(l0, l1), l2)
    e0, e1, e2 = jnp.exp(l0 - mx), jnp.exp(l1 - mx), jnp.exp(l2 - mx)
    tot = e0 + e1 + e2
    return e0 / tot, e1 / tot, e2 / tot


def _dil_combine_fwd(o, lse, y_alias, *, name, tm=512):
    T = o.shape[0]

    def body(o_ref, l_ref, _, y_ref):
        a = jnp.concatenate(_group_softmax(l_ref[...]), axis=1)
        y_ref[...] = (o_ref[...] * a).astype(y_ref.dtype)

    blk = pl.BlockSpec((tm, MAIN_W), lambda i: (i, 0))
    return pl.pallas_call(
        body, name=name, grid=(T // tm,), in_specs=[blk, blk, pl.BlockSpec(memory_space=pl.ANY)], out_specs=blk,
        out_shape=jax.ShapeDtypeStruct(y_alias.shape, y_alias.dtype), input_output_aliases={2: 0},
        compiler_params=_cp(("parallel",)),
    )(o, lse, y_alias)


def _dil_combine_bwd(dy, o, lse, *, name, tm=256):
    T = o.shape[0]
    lane_r = lax.broadcasted_iota(jnp.int32, (256, 256), 0) // HEAD_DIM
    lane_c = lax.broadcasted_iota(jnp.int32, (256, 256), 1) // HEAD_DIM
    ones_bd = (lane_r == lane_c).astype(BF)

    def body(dy_ref, o_ref, l_ref, e_ref, do_ref, c_ref):
        dyv = dy_ref[...]
        alphas = _group_softmax(l_ref[...])
        prod = dyv * o_ref[...]
        e = e_ref[...]
        tot = jnp.zeros((tm, 256), F32)
        for gi in range(3):
            x = prod[:, gi * 256:(gi + 1) * 256]
            hi = x.astype(BF)
            lo = (x - hi.astype(F32)).astype(BF)
            dalpha = jnp.dot(hi, e, preferred_element_type=F32) + jnp.dot(lo, e, preferred_element_type=F32)
            tot = tot + alphas[gi] * dalpha
        a = jnp.concatenate(alphas, axis=1)
        do_ref[...] = (dyv * a).astype(do_ref.dtype)
        c_ref[...] = jnp.concatenate([-al * tot for al in alphas], axis=1)

    blk = pl.BlockSpec((tm, MAIN_W), lambda i: (i, 0))
    return pl.pallas_call(
        body, name=name, grid=(T // tm,),
        in_specs=[blk, blk, blk, pl.BlockSpec((256, 256), lambda i: (0, 0))], out_specs=[blk, blk],
        out_shape=[jax.ShapeDtypeStruct((T, MAIN_W), F32), jax.ShapeDtypeStruct((T, MAIN_W), F32)],
        compiler_params=_cp(("parallel",)),
    )(dy, o, lse, ones_bd)


def _kv_grad(parts, cos, sin, B, S, *, name):
    T = B * S
    tb = S // STEPS
    n_l = len(parts)

    def shifted(g):
        def f(b, t):
            return (b * tb + jnp.minimum(t + DIL[g], tb - 1), g)
        return f

    with_prev = [g for g in range(3) if DIL[g] < tb]
    n_p = len(with_prev)
    per_l = 2 + 2 * n_p

    def body(*refs):
        c_ref, s_ref = refs[0], refs[1]
        ins = refs[2:2 + n_l * per_l]
        dk_ref, dv_ref = refs[2 + n_l * per_l:]
        t = pl.program_id(1)
        dk = jnp.zeros((STEPS, MAIN_W), F32)
        dv = jnp.zeros((STEPS, MAIN_W), F32)
        zero = jnp.zeros((STEPS, 256), F32)
        for li in range(n_l):
            base = li * per_l
            dk = dk + ins[base][...]
            dv = dv + ins[base + 1][...]
            kparts, vparts = [zero] * 3, [zero] * 3
            for pi, g in enumerate(with_prev):
                ok = t + DIL[g] < tb
                kparts[g] = jnp.where(ok, ins[base + 2 + pi][...], 0.0)
                vparts[g] = jnp.where(ok, ins[base + 2 + n_p + pi][...], 0.0)
            dk = dk + jnp.concatenate(kparts, axis=1)
            dv = dv + jnp.concatenate(vparts, axis=1)
        dk_ref[...] = _rot(dk, c_ref[...], s_ref[...], -1.0).astype(dk_ref.dtype)
        dv_ref[...] = dv.astype(dv_ref.dtype)

    full = pl.BlockSpec((STEPS, MAIN_W), lambda b, t: (b * tb + t, 0))
    tab = pl.BlockSpec((STEPS, 128), lambda b, t: (b * tb + t, 0))
    in_specs, ops = [tab, tab], [cos, sin]
    for (kc, kp, vc, vp) in parts:
        in_specs += [full, full] + [pl.BlockSpec((STEPS, 256), shifted(g)) for g in with_prev] * 2
        ops += [kc, vc] + [kp] * n_p + [vp] * n_p
    return pl.pallas_call(
        body, name=name, grid=(B, tb), in_specs=in_specs, out_specs=[full, full],
        out_shape=[jax.ShapeDtypeStruct((T, MAIN_W), BF)] * 2,
        compiler_params=_cp(("parallel", "parallel")),
    )(*ops)


def _loss(y, target, *, name, tm=512):
    T, Dm = y.shape
    nt = T // tm

    def body(y_ref, t_ref, l_ref, d_ref, acc):
        i = pl.program_id(0)
        err = y_ref[...] - t_ref[...]
        d_ref[...] = err / Dm
        part = jnp.sum(jnp.mean(err * err, axis=-1, keepdims=True).reshape(tm // 8, 8, 1), axis=0)

        @pl.when(i == 0)
        def _():
            acc[...] = part

        @pl.when(i > 0)
        def _():
            acc[...] += part

        @pl.when(i == nt - 1)
        def _():
            l_ref[...] = 0.5 * jnp.sum(acc[...], axis=0, keepdims=True)

    row = pl.BlockSpec((tm, Dm), lambda i: (i, 0))
    return pl.pallas_call(
        body, name=name, grid=(nt,), in_specs=[row, row],
        out_specs=[pl.BlockSpec((1, 1), lambda i: (0, 0)), row],
        out_shape=[jax.ShapeDtypeStruct((1, 1), F32), jax.ShapeDtypeStruct((T, Dm), F32)],
        scratch_shapes=[pltpu.VMEM((8, 1), F32)],
        compiler_params=_cp(("arbitrary",)),
    )(y, target)


def _adamw(w, g, m, v, *, name):
    shape = w.shape
    cols = shape[-1]
    rows = w.size // cols
    tm = rows
    for cand in (512, 352, 256, 128):
        if rows > cand and rows % cand == 0 and cand * cols * 4 <= (1 << 20):
            tm = cand
            break

    def body(w_ref, g_ref, m_ref, v_ref, d_ref, mo_ref, vo_ref):
        gv = g_ref[...]
        mn = ADAM_B1 * m_ref[...] + (1.0 - ADAM_B1) * gv
        vn = ADAM_B2 * v_ref[...] + (1.0 - ADAM_B2) * (gv * gv)
        m_hat = mn / (1.0 - ADAM_B1 ** ADAM_STEP)
        v_hat = vn / (1.0 - ADAM_B2 ** ADAM_STEP)
        d_ref[...] = -ADAM_LR * (m_hat / (jnp.sqrt(v_hat) + ADAM_EPS) + ADAM_WD * w_ref[...])
        mo_ref[...] = mn
        vo_ref[...] = vn

    blk = pl.BlockSpec((tm, cols), lambda i: (i, 0))
    outs = pl.pallas_call(
        body, name=name, grid=(rows // tm,), in_specs=[blk] * 4, out_specs=[blk] * 3,
        out_shape=[jax.ShapeDtypeStruct((rows, cols), F32)] * 3,
        compiler_params=_cp(("parallel",)),
    )(*[t.reshape(rows, cols) for t in (w, g, m, v)])
    return tuple(t.reshape(shape) for t in outs)


def _adamw_layer(name, l, w, g, m, v, prev, after=None):
    L, rows, cols = w.shape
    tm = rows
    for cand in (512, 352, 256, 176, 128, 64):
        if rows % cand == 0 and cand * cols * 4 <= (1 << 20):
            tm = cand
            break
    if prev is None:
        prev = tuple(lax.empty(w.shape, F32) for _ in range(4))

    n_after = 0 if after is None else 1

    def body(w_ref, g_ref, m_ref, v_ref, *rest):
        d_ref, mo_ref, vo_ref, go_ref = rest[4 + n_after:]
        gv = g_ref[...]
        mn = ADAM_B1 * m_ref[...] + (1.0 - ADAM_B1) * gv
        vn = ADAM_B2 * v_ref[...] + (1.0 - ADAM_B2) * (gv * gv)
        m_hat = mn / (1.0 - ADAM_B1 ** ADAM_STEP)
        v_hat = vn / (1.0 - ADAM_B2 ** ADAM_STEP)
        d_ref[...] = -ADAM_LR * (m_hat / (jnp.sqrt(v_hat) + ADAM_EPS) + ADAM_WD * w_ref[...])
        mo_ref[...] = mn
        vo_ref[...] = vn
        go_ref[...] = gv

    lay = pl.BlockSpec((None, tm, cols), lambda i: (l, i, 0))
    one = pl.BlockSpec((None, tm, cols), lambda i: (0, i, 0))
    return tuple(pl.pallas_call(
        body, name=f"l{l}_adamw_{name}", grid=(rows // tm,),
        in_specs=[lay, one, lay, lay] + [pl.BlockSpec(memory_space=pl.ANY)] * (4 + n_after), out_specs=[lay] * 4,
        out_shape=[jax.ShapeDtypeStruct(w.shape, F32)] * 4,
        input_output_aliases={4 + i: i for i in range(4)},
        compiler_params=_cp(("parallel",)),
    )(_in_hbm(w), _in_hbm(g), _in_hbm(m), _in_hbm(v), *prev, *([] if after is None else [after])))


BIG = {
    'w_in': ((DEPTH, D_MODEL, D_MODEL), 'row'),
    'w_mem_kv': ((DEPTH, D_MODEL, 2 * MEM_W), 'row'),
    'w_out': ((DEPTH, D_MODEL, D_MODEL), 'row'),
    'w_kv': ((1, D_MODEL, 2 * MAIN_W), 'col'),
    'w_gate_up': ((DEPTH, D_MODEL, 2 * D_FF), 'col'),
    'w_down': ((DEPTH, D_FF, D_MODEL), 'row'),
}
BIG_NAMES = tuple(BIG)
N_CHIPS = 4
HBM_ANY = pl.BlockSpec(memory_space=pl.ANY)


def _geom(name):
    (L, R, C), kind = BIG[name]
    if kind == 'row':
        return L, R, C, kind, R // N_CHIPS, C, R // (2 * N_CHIPS)
    return L, R, C, kind, R, C // N_CHIPS, R // 2


def _shard_shape(name):
    L, R, C, kind, rs, cs, rh = _geom(name)
    return (L, rs, cs)


def _half_shape(name):
    L, R, C, kind, rs, cs, rh = _geom(name)
    return (L, rh, cs)


def _full_win(ref, name, s, h):
    L, R, C, kind, rs, cs, rh = _geom(name)
    if kind == 'row':
        rows = pl.ds(s * rs, rs) if h is None else pl.ds(s * rs + h * rh, rh)
        return ref.at[:, rows, :]
    rows = slice(None) if h is None else pl.ds(h * rh, rh)
    return ref.at[:, rows, pl.ds(s * cs, cs)]


def _shard_half(ref, name, h):
    L, R, C, kind, rs, cs, rh = _geom(name)
    return ref.at[:, pl.ds(h * rh, rh), :]


def _halves_win(ref, name, s):
    L, R, C, kind, rs, cs, rh = _geom(name)
    if kind == 'row':
        return ref.at[:, pl.ds(s * rh, rh), :]
    return ref.at[:, :, pl.ds(s * cs, cs)]


def _halves_shape(name):
    L, R, C, kind, rs, cs, rh = _geom(name)
    return (L, N_CHIPS * rh, cs) if kind == 'row' else (L, rh, C)


def _place():
    x, y, c = lax.axis_index("x"), lax.axis_index("y"), lax.axis_index("c")
    chips = [(1 - x, y), (x, 1 - y), (1 - x, 1 - y)]
    return x, y, c, chips


SMALL_ROWS = 24


def _all_gather(shards, small):
    names = BIG_NAMES
    nw = len(names)

    def body(*refs):
        src = dict(zip(names, refs[:nw]))
        small_ref = refs[nw]
        dst = dict(zip(names, refs[nw + 1:2 * nw + 1]))
        small_out = refs[2 * nw + 1]
        send_sems, recv_sems, local_sems = refs[2 * nw + 2:]
        x, y, c, chips = _place()
        s = 2 * x + y
        sib = (x, y, 1 - c)

        def remote(k, src_ref, dst_ref, to):
            return pltpu.make_async_remote_copy(src_ref=src_ref, dst_ref=dst_ref, send_sem=send_sems.at[k],
                                                recv_sem=recv_sems.at[k], device_id=to, device_id_type=MESH)

        local = []
        for wi, nm in enumerate(names):
            local.append(pltpu.make_async_copy(src[nm], _full_win(dst[nm], nm, s, None), local_sems.at[wi]))
        local.append(pltpu.make_async_copy(small_ref, small_out.at[s], local_sems.at[nw]))
        for cp in local:
            cp.start()
        sends = []
        for j, (px, py) in enumerate(chips):
            for wi, nm in enumerate(names):
                sends.append(remote(wi * 6 + j, _shard_half(src[nm], nm, c), _full_win(dst[nm], nm, s, c), (px, py, c)))
            sends.append(remote(nw * 6 + j, small_ref, small_out.at[s], (px, py, c)))
        for cp in sends:
            cp.start()
        for j, (px, py) in enumerate(chips):
            sp = 2 * px + py
            for wi, nm in enumerate(names):
                w = _full_win(dst[nm], nm, sp, c)
                remote(wi * 6 + j, w, w, sib).wait_recv()
                fwd = remote(wi * 6 + 3 + j, w, w, sib)
                fwd.start()
                sends.append(fwd)
            remote(nw * 6 + j, small_ref, small_out.at[sp], sib).wait_recv()
        for j, (px, py) in enumerate(chips):
            sp = 2 * px + py
            for wi, nm in enumerate(names):
                w = _full_win(dst[nm], nm, sp, 1 - c)
                remote(wi * 6 + 3 + j, w, w, sib).wait_recv()
        for cp in sends:
            cp.wait_send()
        for cp in local:
            cp.wait()

    n_sem = nw * 6 + 3
    outs = pl.pallas_call(
        body, name="all_gather_weights",
        in_specs=[HBM_ANY] * (nw + 1), out_specs=[HBM_ANY] * (nw + 1),
        out_shape=[jax.ShapeDtypeStruct(BIG[nm][0], BF) for nm in names]
        + [jax.ShapeDtypeStruct((N_CHIPS, SMALL_ROWS, 256), F32)],
        scratch_shapes=[pltpu.SemaphoreType.DMA((n_sem,)), pltpu.SemaphoreType.DMA((n_sem,)),
                        pltpu.SemaphoreType.DMA((nw + 1,))],
    )(*[shards[nm] for nm in names], small)
    return dict(zip(names, outs[:nw])), outs[nw]


SEM_SPEC = pl.BlockSpec(memory_space=pltpu.SEMAPHORE)
HBM_SPEC = pl.BlockSpec(memory_space=pltpu.HBM)
DATAFLOW = pltpu.SideEffectType.DATAFLOW_SIDE_EFFECTING


def _in_hbm(a):
    return pltpu.with_memory_space_constraint(a, pltpu.HBM)


def _remote(src, dst, send_sems, recv_sems, k, to):
    return pltpu.make_async_remote_copy(src_ref=src, dst_ref=dst, send_sem=send_sems.at[k], recv_sem=recv_sems.at[k],
                                        device_id=to, device_id_type=MESH)


def _split_start(name, bufs, n_copies, sends, after=None):
    nb = len(bufs)
    n_in = nb + (0 if after is None else 1)

    def body(*refs):
        in_refs = refs[:nb]
        send_sems, recv_sems = refs[n_in], refs[n_in + 1]
        token = refs[-1]
        for k, (src, dst, to) in enumerate(sends(in_refs)):
            _remote(src, dst, send_sems, recv_sems, k, to).start()
        token[...] = jnp.zeros_like(token)

    outs = pl.pallas_call(
        body, name=name,
        out_shape=(pltpu.SemaphoreType.DMA((n_copies,)), pltpu.SemaphoreType.DMA((n_copies,)),
                   *[pltpu.HBM(b.shape, b.dtype) for b in bufs], jax.ShapeDtypeStruct((8, 128), F32)),
        in_specs=[HBM_SPEC] * nb + [HBM_ANY] * (n_in - nb),
        out_specs=(SEM_SPEC, SEM_SPEC, *[HBM_SPEC] * nb, pl.BlockSpec(memory_space=pltpu.VMEM)),
        input_output_aliases={i: 2 + i for i in range(nb)},
        compiler_params=pltpu.CompilerParams(has_side_effects=DATAFLOW),
    )(*[_in_hbm(b) for b in bufs], *([] if after is None else [after]))
    return outs[0], outs[1], list(outs[2:2 + nb]), outs[-1]


def _split_wait(name, send_sems, recv_sems, bufs, after, sends, arrivals):
    nb = len(bufs)

    def body(*refs):
        in_refs = refs[:nb]
        s_sems, r_sems = refs[nb], refs[nb + 1]
        me = (lax.axis_index("x"), lax.axis_index("y"), lax.axis_index("c"))
        for k, (src, dst, to) in enumerate(sends(in_refs)):
            _remote(src, dst, s_sems, r_sems, k, to).wait_send()
        for k, win in enumerate(arrivals(in_refs)):
            _remote(win, win, s_sems, r_sems, k, me).wait_recv()

    outs = pl.pallas_call(
        body, name=name,
        out_shape=[pltpu.HBM(b.shape, b.dtype) for b in bufs],
        in_specs=[HBM_SPEC] * nb + [SEM_SPEC, SEM_SPEC, HBM_ANY],
        out_specs=[HBM_SPEC] * nb,
        input_output_aliases={i: i for i in range(nb)},
        compiler_params=pltpu.CompilerParams(has_side_effects=DATAFLOW),
    )(*bufs, send_sems, recv_sems, after)
    return list(outs)


MIX_W = ('w_in', 'w_mem_kv', 'w_out')
FFN_W = ('w_gate_up', 'w_down')
LAYER_W = MIX_W + FFN_W


def _place_own(l, names, shards, small, sc):
    nw = len(names)
    has_small = small is not None
    n_ops = nw + (1 if has_small else 0)

    def body(sc_ref, *refs):
        for src, dst in zip(refs[:n_ops], refs[n_ops:]):
            dst[...] = src[...]

    in_specs, out_specs, out_shape, ops = [], [], [], list(shards)
    for nm in names:
        L, R, C, kind, rs, cs, rh = _geom(nm)
        in_specs.append(pl.BlockSpec((1, rs, cs), lambda i, sc_ref: (0, 0, 0)))
        if kind == 'row':
            out_specs.append(pl.BlockSpec((1, rs, cs), lambda i, sc_ref: (0, sc_ref[0], 0)))
        else:
            out_specs.append(pl.BlockSpec((1, rs, cs), lambda i, sc_ref: (0, 0, sc_ref[0])))
        out_shape.append(jax.ShapeDtypeStruct((1, R, C), BF))
    if has_small:
        in_specs.append(pl.BlockSpec((SMALL_ROWS, 256), lambda i, sc_ref: (0, 0)))
        out_specs.append(pl.BlockSpec((None, SMALL_ROWS, 256), lambda i, sc_ref: (sc_ref[0], 0, 0)))
        out_shape.append(jax.ShapeDtypeStruct((N_CHIPS, SMALL_ROWS, 256), F32))
        ops.append(small)
    return pl.pallas_call(
        body, name=f"{l}_place_own_shard",
        grid_spec=pltpu.PrefetchScalarGridSpec(num_scalar_prefetch=1, grid=(1,), in_specs=in_specs, out_specs=out_specs),
        out_shape=out_shape,
        compiler_params=_cp(("arbitrary",)),
    )(sc, *ops)


def _gather_start(l, names, shards, small, sc, after=None):
    nw = len(names)
    has_small = small is not None
    fulls = _place_own(l, names, shards, small, sc)
    bufs = list(shards) + ([small] if has_small else []) + list(fulls)
    n_src = nw + (1 if has_small else 0)

    def sends(refs):
        x, y, c, chips = _place()
        s = 2 * x + y
        out = []
        for (px, py) in chips:
            for wi, nm in enumerate(names):
                out.append((_shard_half(refs[wi], nm, c), _full_win(refs[n_src + wi], nm, s, c), (px, py, c)))
            if has_small:
                out.append((refs[nw], refs[n_src + nw].at[s], (px, py, c)))
        return out

    def arrivals(refs):
        x, y, c, chips = _place()
        out = []
        for (px, py) in chips:
            sp = 2 * px + py
            for wi, nm in enumerate(names):
                out.append(_full_win(refs[n_src + wi], nm, sp, c))
            if has_small:
                out.append(refs[n_src + nw].at[sp])
        return out

    n_copies = 3 * n_src
    send_sems, recv_sems, bufs, token = _split_start(f"{l}_gather_ici_start", bufs, n_copies, sends, after)
    return dict(l=l, names=names, has_small=has_small, sems=(send_sems, recv_sems), bufs=bufs, sends=sends,
                arrivals=arrivals, token=token)


def _gather_forward(st, after):
    l, names = st['l'], st['names']
    nw = len(names)
    n_src = nw + (1 if st['has_small'] else 0)
    bufs = _split_wait(f"{l}_gather_ici_wait", *st['sems'], st['bufs'], after, st['sends'], st['arrivals'])
    fulls = bufs[n_src:n_src + nw]
    small_all = bufs[n_src + nw] if st['has_small'] else None

    def sends(refs):
        x, y, c, chips = _place()
        out = []
        for (px, py) in chips:
            sp = 2 * px + py
            for wi, nm in enumerate(names):
                w = _full_win(refs[wi], nm, sp, c)
                out.append((w, w, (x, y, 1 - c)))
        return out

    def arrivals(refs):
        x, y, c, chips = _place()
        out = []
        for (px, py) in chips:
            sp = 2 * px + py
            for wi, nm in enumerate(names):
                out.append(_full_win(refs[wi], nm, sp, 1 - c))
        return out

    send_sems, recv_sems, fulls, token = _split_start(f"{l}_gather_d2d_start", fulls, 3 * nw, sends)
    return dict(l=l, names=names, sems=(send_sems, recv_sems), bufs=fulls, sends=sends, arrivals=arrivals,
                small_all=small_all, token=token)


def _gather_finish(st, after):
    fulls = _split_wait(f"{st['l']}_gather_d2d_wait", *st['sems'], st['bufs'], after, st['sends'], st['arrivals'])
    return dict(zip(st['names'], fulls)), st['small_all']


def _reduce_start(tag, names, grads):
    nw = len(names)
    recv = [lax.empty((1,) + _halves_shape(nm)[1:], F32) for nm in names]
    bufs = [grads[nm] for nm in names] + recv

    def windows(refs, half_of):
        x, y, c, _ = _place()
        h = half_of(c)
        out = []
        for wi, nm in enumerate(names):
            L, R, C, kind, rs, cs, rh = _geom(nm)
            if kind == 'row':
                for sp in range(N_CHIPS):
                    out.append((_full_win(refs[wi], nm, sp, h), _halves_win(refs[nw + wi], nm, sp)))
            else:
                out.append((refs[wi].at[:, pl.ds(h * rh, rh), :], refs[nw + wi]))
        return out

    def sends(refs):
        x, y, c, _ = _place()
        return [(src, dst, (x, y, 1 - c)) for src, dst in windows(refs, lambda c: 1 - c)]

    def arrivals(refs):
        return [dst for _, dst in windows(refs, lambda c: c)]

    n_copies = sum(N_CHIPS if BIG[nm][1] == 'row' else 1 for nm in names)
    send_sems, recv_sems, bufs, token = _split_start(tag + "_halves_start", bufs, n_copies, sends)
    return dict(tag=tag, names=names, sems=(send_sems, recv_sems), bufs=bufs, sends=sends, arrivals=arrivals, token=token)


def _reduce_mid(st, after, sc):
    tag, names = st['tag'], st['names']
    nw = len(names)
    bufs = _split_wait(tag + "_halves_wait", *st['sems'], st['bufs'], after, st['sends'], st['arrivals'])
    halves, own = [], []
    for wi, nm in enumerate(names):
        hb, ow = _add_halves(nm, bufs[wi], bufs[nw + wi], sc, tag)
        halves.append(hb)
        own.append(ow)
    pieces = [lax.empty((3, 1) + _half_shape(nm)[1:], BF) for nm in names]

    def sends(refs):
        x, y, c, chips = _place()
        out = []
        for j, (px, py) in enumerate(chips):
            for wi, nm in enumerate(names):
                out.append((_halves_win(refs[wi], nm, 2 * px + py), refs[nw + wi].at[j], (px, py, c)))
        return out

    def arrivals(refs):
        return [refs[nw + wi].at[j] for j in range(3) for wi in range(nw)]

    send_sems, recv_sems, bufs, token = _split_start(tag + "_pieces_start", halves + pieces, 3 * nw, sends)
    return dict(tag=tag, names=names, sems=(send_sems, recv_sems), bufs=bufs, sends=sends, arrivals=arrivals, own=own,
                token=token)


def _reduce_late(st, after, sc):
    tag, names = st['tag'], st['names']
    nw = len(names)
    bufs = _split_wait(tag + "_pieces_wait", *st['sems'], st['bufs'], after, st['sends'], st['arrivals'])
    gsh = [_sum_pieces(nm, st['own'][wi], bufs[nw + wi], sc, tag) for wi, nm in enumerate(names)]

    def sends(refs):
        x, y, c, _ = _place()
        return [(_shard_half(refs[wi], nm, c), _shard_half(refs[wi], nm, c), (x, y, 1 - c)) for wi, nm in enumerate(names)]

    def arrivals(refs):
        x, y, c, _ = _place()
        return [_shard_half(refs[wi], nm, 1 - c) for wi, nm in enumerate(names)]

    send_sems, recv_sems, bufs, token = _split_start(tag + "_share_start", gsh, nw, sends)
    return dict(tag=tag, names=names, sems=(send_sems, recv_sems), bufs=bufs, sends=sends, arrivals=arrivals, token=token)


def _reduce_finish(st, after):
    gsh = _split_wait(st['tag'] + "_share_wait", *st['sems'], st['bufs'], after, st['sends'], st['arrivals'])
    return dict(zip(st['names'], gsh))


def _add_halves(name, g, r, sc, tag):
    _, R, C, kind, rs, cs, rh = _geom(name)
    L = g.shape[0]
    tr = rh if kind == 'row' else 256
    nr = rh // tr

    def body(sc_ref, g_ref, r_ref, hb_ref, own_ref):
        sp = pl.program_id(2)
        tot = g_ref[...] + r_ref[...]
        hb_ref[...] = tot.astype(hb_ref.dtype)

        @pl.when(sp == sc_ref[0])
        def _():
            own_ref[...] = tot

    if kind == 'row':
        g_map = lambda l, ri, sp, sc_ref: (l, sp * 2 + sc_ref[1], 0)
        h_map = lambda l, ri, sp, sc_ref: (l, sp, 0)
    else:
        g_map = lambda l, ri, sp, sc_ref: (l, sc_ref[1] * nr + ri, sp)
        h_map = lambda l, ri, sp, sc_ref: (l, ri, sp)
    own_map = lambda l, ri, sp, sc_ref: (l, ri, 0)
    blk = (None, tr, cs)
    return pl.pallas_call(
        body, name=tag + "_add_halves_" + name,
        grid_spec=pltpu.PrefetchScalarGridSpec(
            num_scalar_prefetch=1, grid=(L, nr, N_CHIPS),
            in_specs=[pl.BlockSpec(blk, g_map), pl.BlockSpec(blk, h_map)],
            out_specs=[pl.BlockSpec(blk, h_map), pl.BlockSpec(blk, own_map)]),
        out_shape=[jax.ShapeDtypeStruct((L,) + _halves_shape(name)[1:], BF),
                   jax.ShapeDtypeStruct((L,) + _half_shape(name)[1:], F32)],
        compiler_params=_cp(("parallel", "parallel", "arbitrary")),
    )(sc, _in_hbm(g), _in_hbm(r))


def _sum_pieces(name, own, pieces, sc, tag):
    _, R, C, kind, rs, cs, rh = _geom(name)
    L = own.shape[0]
    tr = rh if kind == 'row' else 256
    nr = rh // tr

    def body(sc_ref, o_ref, p_ref, out_ref):
        out_ref[...] = o_ref[...] + p_ref[0].astype(F32) + p_ref[1].astype(F32) + p_ref[2].astype(F32)

    blk = (None, tr, cs)
    return pl.pallas_call(
        body, name=tag + "_sum_pieces_" + name,
        grid_spec=pltpu.PrefetchScalarGridSpec(
            num_scalar_prefetch=1, grid=(L, nr),
            in_specs=[pl.BlockSpec(blk, lambda l, ri, sc_ref: (l, ri, 0)),
                      pl.BlockSpec((3, None, tr, cs), lambda l, ri, sc_ref: (0, l, ri, 0))],
            out_specs=pl.BlockSpec(blk, lambda l, ri, sc_ref: (l, sc_ref[1] * nr + ri, 0))),
        out_shape=jax.ShapeDtypeStruct((L,) + _shard_shape(name)[1:], F32),
        compiler_params=_cp(("parallel", "parallel")),
    )(sc, _in_hbm(own), _in_hbm(pieces))


def _small_gather_start(v, sc):
    rows = v.shape[0]

    def place(sc_ref, v_ref, o_ref):
        o_ref[...] = v_ref[...]

    slots = pl.pallas_call(
        place, name="small_grads_place_own",
        grid_spec=pltpu.PrefetchScalarGridSpec(
            num_scalar_prefetch=1, grid=(1,),
            in_specs=[pl.BlockSpec((rows, 128), lambda i, sc_ref: (0, 0))],
            out_specs=pl.BlockSpec((None, rows, 128), lambda i, sc_ref: (2 * sc_ref[0] + sc_ref[1], 0, 0))),
        out_shape=jax.ShapeDtypeStruct((8, rows, 128), v.dtype),
        compiler_params=_cp(("arbitrary",)),
    )(sc, v)

    def peers():
        x, y, c, _ = _place()
        flips = [(fx, fy, fc) for fx in (0, 1) for fy in (0, 1) for fc in (0, 1)][1:]
        return [((1 - x if fx else x), (1 - y if fy else y), (1 - c if fc else c)) for fx, fy, fc in flips]

    def sends(refs):
        x, y, c, _ = _place()
        return [(refs[0], refs[1].at[4 * x + 2 * y + c], p) for p in peers()]

    def arrivals(refs):
        return [refs[1].at[4 * px + 2 * py + pc] for px, py, pc in peers()]

    send_sems, recv_sems, bufs, token = _split_start("small_grads_gather_start", [v, slots], 7, sends)
    return dict(sems=(send_sems, recv_sems), bufs=bufs, sends=sends, arrivals=arrivals, token=token)


def _small_gather_finish(st, after):
    return _split_wait("small_grads_gather_wait", *st['sems'], st['bufs'], after, st['sends'], st['arrivals'])[1]


def _sum8(v8, *, name, tr=336):
    rows = v8.shape[1]
    tr = min(tr, rows)
    assert rows % tr == 0

    def body(v_ref, o_ref):
        tot = v_ref[0].astype(F32)
        for d in range(1, 8):
            tot = tot + v_ref[d].astype(F32)
        o_ref[...] = tot

    return pl.pallas_call(
        body, name=name, grid=(rows // tr,),
        in_specs=[pl.BlockSpec((8, tr, 128), lambda i: (0, i, 0))], out_specs=pl.BlockSpec((tr, 128), lambda i: (i, 0)),
        out_shape=jax.ShapeDtypeStruct((rows, 128), F32),
        compiler_params=_cp(("parallel",)),
    )(v8)


def _block_diag(w_pool_l):
    wbd = jnp.zeros((MAIN_W, MAIN_W), F32)
    for gi in range(len(POOL_WINDOWS)):
        wbd = lax.dynamic_update_slice(wbd, w_pool_l[gi], (gi * POOL_GROUP, gi * POOL_GROUP))
    return wbd.astype(BF)


def _unpack_small(small_all):
    ng = small_all[:, :16, :].reshape(N_CHIPS, DEPTH, 4, 256).transpose(1, 2, 0, 3).reshape(DEPTH, 4, D_MODEL)
    ps = small_all[:, 16:18, :POOL_GROUP].transpose(1, 0, 2).reshape(N_A, MAIN_W)
    return ng, ps


def _local_step(x, mem, positions, on_forward, on_backward, mem_norm, w_pool, kv_norm, target):
    B, S, _ = x.shape
    T = B * S
    xc = x.reshape(T, D_MODEL)
    memf = mem.reshape(B * N_MEM, D_MODEL)
    tgt = target.reshape(T, D_MODEL)
    cos, sin = _rope_tables(positions.reshape(T, 1), name="rope_tables")
    wbd = [_block_diag(w_pool[l]) for l in range(N_A)]
    nbo = D_FF // 256
    fw = []
    rk = rv = None
    kv_saved = None
    wts = []
    norm_gains = pool_scale = y2 = None

    def tied(vec, tok):
        return vec if tok is None else vec + tok

    for l in range(DEPTH):
        t = f"l{l}_"
        got = on_forward('start', l, y2)
        wts.append(dict(got[0]))
        if l == 0:
            norm_gains, pool_scale = _unpack_small(got[1])
        sv = {'x_in': xc}
        h0, sv['r0'] = _norm_fwd(xc, tied(norm_gains[l, 0], got[2]), name=t + "norm0", out_dtype=BF)
        z, = _mm(h0, wts[l]['w_in'], b_layer=0, name=t + "mm_in")
        memn, sv['rm'] = _norm_fwd(memf, mem_norm[l], name=t + "norm_mem", out_dtype=BF, tm=256)
        kvm, = _mm(memn, wts[l]['w_mem_kv'], b_layer=0, name=t + "mm_memkv", out_dtypes=(BF,))
        if l < N_A:
            ycat, sv['p'] = _pool_fwd(z, wbd[l], pool_scale[l], B, S, name=t + "pool_fwd")
        else:
            rq = _rope_apply(z, cos, sin, name=t + "rope_q", out_dtype=F32)
            o = lax.empty((T, MAIN_W), F32)
            lse = lax.empty((T, MAIN_W), F32)
            for g in range(3):
                o, lse = _dil_fwd(g, rq, rk, rv, o, lse, B, S, name=t + f"dil_fwd{g}")
            ycat = _dil_combine_fwd(o, lse, lax.empty((T, D_MODEL), BF), name=t + "dil_combine")
            sv.update(rq=rq, o=o, lse=lse)
        ycat, sv['lse_m'] = _memattn_fwd(z, kvm, ycat, B, S, name=t + "memattn_fwd")
        tok = on_forward('mid', l, ycat)
        y1, = _mm(ycat, wts[l]['w_out'], b_layer=0, name=t + "mm_out")
        wts[l].update(on_forward('ffn', l, y1)[0])
        x1, sv['r1'] = _norm_fwd(y1, tied(norm_gains[l, 1], tok), name=t + "norm1", res=xc)
        h2, sv['r2'] = _norm_fwd(x1, norm_gains[l, 2], name=t + "norm2", out_dtype=BF)
        gg, uu, aa = _mm(h2, wts[l]['w_gate_up'], b_layer=0, b_offsets=(0, nbo), out_n=D_FF, tm=2048, tn=256, name=t + "mm_gate_up",
                         epilogue=_swiglu_fwd_epilogue, out_dtypes=(BF, BF, BF))
        on_forward('post', l, gg)
        y2, = _mm(aa, wts[l]['w_down'], b_layer=0, tk=D_FF, name=t + "mm_down")
        x2, sv['r3'] = _norm_fwd(y2, norm_gains[l, 3], name=t + "norm3", res=x1)
        sv.update(h0=h0, z=z, memn=memn, kvm=kvm, ycat=ycat, y1=y1, x1=x1, h2=h2, gg=gg, uu=uu, aa=aa, y2=y2)
        fw.append(sv)
        xc = x2
        if l == N_A - 1:
            kvn, rkv = _norm_fwd(xc, kv_norm, name="norm_kv", out_dtype=BF)
            kv, = _mm(kvn, wts[N_A - 1]['w_kv'], b_layer=0, name="mm_kv")
            rk, rv = _rope_apply(kv, cos, sin, name="rope_k", passthrough=True, out_dtype=F32)
            kv_saved = (xc, kvn, rkv)

    loss, dx = _loss(xc, tgt, name="loss")

    d_ng = [[None] * 4 for _ in range(DEPTH)]
    d_memnorm = [None] * DEPTH
    d_wbd = [None] * N_A
    d_pscale = [None] * N_A
    d_kvnorm = None
    kv_parts = []
    tok = None

    def as3d(gl):
        return {nm: g.reshape((1,) + g.shape) for nm, g in gl.items()}

    for l in reversed(range(DEPTH)):
        t = f"l{l}_b_"
        sv = fw[l]
        gl = {}
        dy2, d_ng[l][3] = _norm_bwd(dx, sv['y2'], sv['r3'], tied(norm_gains[l, 3], tok), name=t + "norm3", out_dtype=BF)
        gl['w_down'], = _mm(sv['aa'], dy2, ta=True, tm=1408, tk=2048, name=t + "dw_down")
        dg, du = _mm(dy2, wts[l]['w_down'], tb=True, b_layer=0, tm=2048, tn=256, name=t + "d_act",
                     extras=((sv['gg'], 'tile'), (sv['uu'], 'tile')), epilogue=_swiglu_bwd_epilogue, out_dtypes=(BF, BF))
        gl['w_gate_up'], = _mm(sv['h2'], (dg, du), ta=True, tn=1408, tk=1024, name=t + "dw_gate_up")
        dh2, = _mm((dg, du), wts[l]['w_gate_up'], tb=True, b_layer=0, tn=1024, tk=1408, name=t + "d_h2", out_dtypes=(BF,))
        dx1, d_ng[l][2] = _norm_bwd(dh2, sv['x1'], sv['r2'], norm_gains[l, 2], name=t + "norm2", add=dx)
        tok = on_backward('ffn', l, dx1, as3d(gl))
        dy1, d_ng[l][1] = _norm_bwd(dx1, sv['y1'], sv['r1'], tied(norm_gains[l, 1], tok), name=t + "norm1", out_dtype=BF)
        gl['w_out'], = _mm(sv['ycat'], dy1, ta=True, tk=4096, name=t + "dw_out")
        dycat, = _mm(dy1, wts[l]['w_out'], tb=True, b_layer=0, name=t + "d_ycat")
        dz = lax.empty((T, D_MODEL), BF)
        dz, dkm, dvm = _memattn_bwd(dycat, sv['z'], sv['kvm'], sv['lse_m'], dz, B, S, name=t + "memattn")
        if l < N_A:
            dz, d_wbd[l], d_pscale[l] = _pool_bwd(dycat, sv['p'], wbd[l], pool_scale[l], dz, B, S, name=t + "pool")
        else:
            do, cb = _dil_combine_bwd(dycat, sv['o'], sv['lse'], name=t + "dil_combine")
            acc = tuple(lax.empty((T, MAIN_W), F32) for _ in range(3))
            for g in range(3):
                acc = _dil_bwd(g, sv['rq'], rk, rv, do, cb, sv['lse'], acc, B, S, name=t + f"dil{g}")
            dz = _rope_apply(acc[0], cos, sin, name=t + "rope_q", sign=-1.0, alias=dz)
            kv_parts.append(acc[1:])
        tok = on_backward('mix', l, dz, as3d(gl))
        gl['w_in'], = _mm(sv['h0'], dz, ta=True, tk=4096, name=t + "dw_in")
        dh0, = _mm(dz, wts[l]['w_in'], tb=True, b_layer=0, name=t + "d_h0", out_dtypes=(BF,))
        dx, d_ng[l][0] = _norm_bwd(dh0, sv['x_in'], sv['r0'], tied(norm_gains[l, 0], tok), name=t + "norm0", add=dx1)
        gl['w_mem_kv'], = _mm(sv['memn'], (dkm, dvm), ta=True, tn=256, name=t + "dw_memkv")
        dmemn, = _mm((dkm, dvm), wts[l]['w_mem_kv'], tb=True, b_layer=0, tk=256, name=t + "d_memn", out_dtypes=(BF,))
        _, d_memnorm[l] = _norm_bwd(dmemn, memf, sv['rm'], mem_norm[l], name=t + "norm_mem", out_dtype=BF, tm=256)
        if l == N_A:
            dk, dv = _kv_grad_sum(kv_parts, cos, sin, name="kv_grad")
            x_kv, kvn, rkv = kv_saved
            gl['w_kv'], = _mm(kvn, (dk, dv), ta=True, tn=768, tk=2048, name="dw_kv")
            dkvn, = _mm((dk, dv), wts[N_A - 1]['w_kv'], tb=True, b_layer=0, tn=1024, tk=768, name="d_kvn", out_dtypes=(BF,))
            dx, d_kvnorm = _norm_bwd(dkvn, x_kv, rkv, kv_norm, name="norm_kv_b", add=dx)
        tok = on_backward('end', l, dx, as3d(gl))

    small = {
        'norm_gains': jnp.stack([jnp.concatenate(d_ng[l], axis=0) for l in range(DEPTH)]),
        'mem_norm': jnp.concatenate(d_memnorm, axis=0),
        'kv_norm': d_kvnorm.reshape(D_MODEL),
        'pool_scale': jnp.concatenate(d_pscale, axis=0),
        'w_pool': jnp.stack([jnp.stack([d_wbd[l][gi * POOL_GROUP:(gi + 1) * POOL_GROUP, gi * POOL_GROUP:(gi + 1) * POOL_GROUP]
                                        for gi in range(len(POOL_WINDOWS))]) for l in range(N_A)]),
    }
    return loss, dx, small


SMALL_ORDER = ('norm_gains', 'mem_norm', 'kv_norm', 'pool_scale', 'w_pool')
SMALL_VEC_ROWS = 2560


def kernel(x, mem, positions, norm_gains, mem_norm, w_in, w_mem_kv, w_out, w_pool, pool_scale, kv_norm, w_kv, w_gate_up, w_down, loss_target, m_norm_gains, m_mem_norm, m_w_in, m_w_mem_kv, m_w_out, m_w_pool, m_pool_scale, m_kv_norm, m_w_kv, m_w_gate_up, m_w_down, v_norm_gains, v_mem_norm, v_w_in, v_w_mem_kv, v_w_out, v_w_pool, v_pool_scale, v_kv_norm, v_w_kv, v_w_gate_up, v_w_down):
    xi, yi, ci = lax.axis_index("x"), lax.axis_index("y"), lax.axis_index("c")
    s = 2 * xi + yi
    sc = jnp.stack([s, ci]).astype(jnp.int32)
    weights = dict(norm_gains=norm_gains, mem_norm=mem_norm, w_in=w_in, w_mem_kv=w_mem_kv, w_out=w_out, w_pool=w_pool,
                   pool_scale=pool_scale, kv_norm=kv_norm, w_kv=w_kv, w_gate_up=w_gate_up, w_down=w_down)
    moms = dict(norm_gains=m_norm_gains, mem_norm=m_mem_norm, w_in=m_w_in, w_mem_kv=m_w_mem_kv, w_out=m_w_out,
                w_pool=m_w_pool, pool_scale=m_pool_scale, kv_norm=m_kv_norm, w_kv=m_w_kv, w_gate_up=m_w_gate_up,
                w_down=m_w_down)
    vels = dict(norm_gains=v_norm_gains, mem_norm=v_mem_norm, w_in=v_w_in, w_mem_kv=v_w_mem_kv, w_out=v_w_out,
                w_pool=v_w_pool, pool_scale=v_pool_scale, kv_norm=v_kv_norm, w_kv=v_w_kv, w_gate_up=v_w_gate_up,
                w_down=v_w_down)

    small_w = jnp.zeros((SMALL_ROWS, 256), F32)
    small_w = lax.dynamic_update_slice(small_w, norm_gains.reshape(16, 256), (0, 0))
    small_w = lax.dynamic_update_slice(small_w, pool_scale, (16, 0))
    def shard_of(nm, l):
        return w_kv.astype(BF).reshape(_shard_shape('w_kv')) if nm == 'w_kv' else weights[nm][l:l + 1].astype(BF)

    groups = {'l0a': (0, MIX_W), 'l0b': (0, FFN_W)}
    groups.update({f"l{l}": (l, LAYER_W + (('w_kv',) if l == N_A - 1 else ())) for l in range(1, DEPTH)})
    on_ici, on_d2d, gathered = {}, {}, {}

    def start_group(tag, after):
        l, names = groups[tag]
        on_ici[tag] = _gather_start(tag, names, [shard_of(nm, l) for nm in names], small_w if tag == 'l0a' else None, sc,
                                    after)
        return on_ici[tag]['token'][0, 0]

    def on_forward(where, l, after):
        if where == 'start':
            if l == 0:
                start_group('l0a', None)
                st = on_ici.pop('l0a')
                fwd = _gather_forward(st, st['token'])
                w, small_all = _gather_finish(fwd, fwd['token'])
                return w, small_all, start_group('l0b', w['w_in'])
            gathered[l] = _gather_finish(on_d2d.pop(f"l{l}"), after)[0]
            tok = start_group(f"l{l + 1}", gathered[l]['w_in']) if l + 1 < DEPTH else None
            return {nm: w for nm, w in gathered[l].items() if nm not in FFN_W}, None, tok
        if where == 'mid' and l == 0:
            on_d2d['l0b'] = _gather_forward(on_ici.pop('l0b'), after)
            return start_group('l1', on_d2d['l0b']['token'])
        if where == 'ffn':
            if l == 0:
                return (_gather_finish(on_d2d.pop('l0b'), after)[0],)
            return ({nm: gathered[l][nm] for nm in FFN_W},)
        if where == 'post' and l + 1 < DEPTH:
            on_d2d[f"l{l + 1}"] = _gather_forward(on_ici.pop(f"l{l + 1}"), after)
        return None

    hook_of = {'ffn': 0, 'mix': 1, 'end': 2}
    active, reduced = [], {l: {} for l in range(DEPTH)}
    advance = {'mid': lambda st, after: _reduce_mid(st, after, sc), 'late': lambda st, after: _reduce_late(st, after, sc)}

    def run_hook(idx, after):
        toks = []
        for grp in list(active):
            while grp['plan'] and grp['plan'][0][1] <= idx:
                step = grp['plan'].pop(0)[0]
                if step == 'finish':
                    reduced[grp['layer']].update(_reduce_finish(grp['st'], after))
                    active.remove(grp)
                else:
                    grp['st'] = advance[step](grp['st'], after)
                    toks.append(grp['st']['token'][0, 0])
        return toks

    def on_backward(where, l, after, grads):
        idx = 3 * (DEPTH - 1 - l) + hook_of[where]
        toks = run_hook(idx, after)
        if where in ('ffn', 'end'):
            names = FFN_W if where == 'ffn' else tuple(nm for nm in grads if nm not in FFN_W)
            st = _reduce_start(f"l{l}_{where}_grads", names, {nm: grads[nm] for nm in names})
            plan = [('mid', idx + 1), ('late', idx + 3), ('finish', idx + 4)] if where == 'ffn' else \
                   [('mid', idx + 1), ('late', idx + 2), ('finish', idx + 3)]
            active.append(dict(layer=l, st=st, plan=plan))
            toks.append(st['token'][0, 0])
        return sum(toks) if toks else None

    loss, gx, gsmall = _local_step(x, mem, positions, on_forward, on_backward, mem_norm, w_pool, kv_norm, loss_target)
    loss = lax.psum(loss[0, 0], ("x", "y", "c"))

    vec = jnp.concatenate([gsmall[nm].reshape(-1) for nm in SMALL_ORDER])
    vec = jnp.pad(vec, (0, SMALL_VEC_ROWS * 128 - vec.shape[0])).reshape(SMALL_VEC_ROWS, 128)
    vec = vec + sum(grp['st']['token'][0, 0] for grp in active)
    small_st = _small_gather_start(vec.astype(BF), sc)
    outs = {nm: None for nm in LAYER_W}

    def adamw_layers(layers, names, after):
        for l in layers:
            for nm in names:
                outs[nm] = _adamw_layer(nm, l, weights[nm], reduced[l][nm], moms[nm], vels[nm], outs[nm], after)
                after = outs[nm][0]
        return after

    def zero_of(toks, st):
        return jnp.full((8, 128), sum(toks)) if toks else st['token']

    last = 3 * DEPTH
    toks = run_hook(last, small_st['token'])
    done = adamw_layers(range(DEPTH - 1, 0, -1), LAYER_W, zero_of(toks, small_st))
    toks = run_hook(last + 1, done)
    done = adamw_layers([0], FFN_W, zero_of(toks, small_st))
    tot = _sum8(_small_gather_finish(small_st, done), name="sum_small_grads", tr=512)
    run_hook(last + 2, tot)
    assert not active
    adamw_layers([0], MIX_W, None)
    tot = tot.reshape(-1)
    grads, off = {}, 0
    for nm in SMALL_ORDER:
        shape = (DEPTH, 4, D_MODEL) if nm == 'norm_gains' else (N_A, MAIN_W) if nm == 'pool_scale' else weights[nm].shape
        n = 1
        for dim in shape:
            n *= dim
        grads[nm] = tot[off:off + n].reshape(shape)
        off += n
    grads['norm_gains'] = lax.dynamic_slice(grads['norm_gains'], (0, 0, s * 256), (DEPTH, 4, 256))
    grads['pool_scale'] = lax.dynamic_slice(grads['pool_scale'], (0, s * POOL_GROUP), (N_A, POOL_GROUP))
    grads['w_kv'] = reduced[N_A]['w_kv'].reshape(w_kv.shape)

    order = ('norm_gains', 'mem_norm', 'w_in', 'w_mem_kv', 'w_out', 'w_pool', 'pool_scale', 'kv_norm', 'w_kv',
             'w_gate_up', 'w_down')
    deltas, new_m, new_v = {}, {}, {}
    for nm in order:
        if nm in LAYER_W:
            deltas[nm], new_m[nm], new_v[nm], grads[nm] = outs[nm]
        else:
            deltas[nm], new_m[nm], new_v[nm] = _adamw(weights[nm], grads[nm], moms[nm], vels[nm], name="adamw_" + nm)
    return (loss, gx.reshape(x.shape), *[grads[nm] for nm in order], *[deltas[nm] for nm in order],
            *[new_m[nm] for nm in order], *[new_v[nm] for nm in order])
```

```python
import functools

import jax
import jax.numpy as jnp
from jax import lax
from jax.experimental import pallas as pl
from jax.experimental.pallas import tpu as pltpu

F32 = jnp.float32
BF = jnp.bfloat16

D_MODEL = 1024
DEPTH = 4
N_A = 2
HEAD_DIM = 64
MEM_W = 256
MAIN_W = 768
D_FF = 2816
N_MEM = 256
POOL_WINDOWS = (2, 4, 8, 16)
POOL_GROUP = 192
DIL = (1, 4, 16)
STEPS = 128
ROPE_THETA = 10000.0
EPS = 1e-6
SCALE = HEAD_DIM ** -0.5
NEG = -1e30

ADAM_LR = 0.001
ADAM_B1 = 0.9
ADAM_B2 = 0.999
ADAM_EPS = 1e-08
ADAM_WD = 0.01
ADAM_STEP = 10

VMEM_LIMIT = 48 * 1024 * 1024
MESH = pl.DeviceIdType.MESH


def _cp(sem):
    return pltpu.CompilerParams(dimension_semantics=sem, vmem_limit_bytes=VMEM_LIMIT)


def _mm(a, b, *, name, ta=False, tb=False, tm=1024, tn=512, tk=1024, b_layer=None, b_offsets=(0,),
        extras=(), epilogue=None, out_dtypes=(F32,), out_n=None, stack=None):
    a_pair = isinstance(a, (tuple, list))
    b_pair = isinstance(b, (tuple, list))
    a0 = a[0] if a_pair else a
    b0 = b[0] if b_pair else b
    a_rows, a_cols = a0.shape
    if a_pair:
        a_cols *= 2
    b_rows, b_cols = b0.shape[-2:]
    if b_pair:
        b_cols *= 2
    M, K = (a_cols, a_rows) if ta else (a_rows, a_cols)
    N = b_rows if tb else b_cols
    if out_n is not None:
        N = out_n
    tm, tn, tk = min(tm, M), min(tn, N), min(tk, K)
    assert M % tm == 0 and N % tn == 0 and K % tk == 0, (name, M, N, K, tm, tn, tk)
    nk = K // tk
    n_acc = len(b_offsets)

    if a_pair:
        a_half = (a0.shape[1] // (tm if ta else tk))
    if b_pair:
        b_half = (b0.shape[1] // (tk if tb else tn))

    def a_map(sel):
        def f(i, j, k):
            r, c = (k, i) if ta else (i, k)
            if a_pair:
                c = jnp.clip(c - sel * a_half, 0, a_half - 1)
            return (r, c)
        return f

    def b_map(sel, off):
        def f(i, j, k):
            r, c = (j + off, k) if tb else (k, j + off)
            if b_pair:
                c = jnp.clip(c - sel * b_half, 0, b_half - 1)
            if b_layer is not None:
                return (b_layer, r, c)
            return (r, c)
        return f

    a_blk = (tk, tm) if ta else (tm, tk)
    b_blk = (tn, tk) if tb else (tk, tn)
    if b_layer is not None:
        b_blk = (None,) + b_blk
    in_specs, operands = [], []
    for sel in range(2 if a_pair else 1):
        in_specs.append(pl.BlockSpec(a_blk, a_map(sel)))
        operands.append(a[sel] if a_pair else a)
    n_a = len(operands)
    for off in b_offsets:
        for sel in range(2 if b_pair else 1):
            in_specs.append(pl.BlockSpec(b_blk, b_map(sel, off)))
            operands.append(b[sel] if b_pair else b)
    n_b = len(operands) - n_a
    for arr, kind in extras:
        if kind == 'tile':
            in_specs.append(pl.BlockSpec((tm, tn), lambda i, j, k: (i, j)))
        elif kind == 'row':
            in_specs.append(pl.BlockSpec((tm, 1), lambda i, j, k: (i, 0)))
        else:
            in_specs.append(pl.BlockSpec((1, tn), lambda i, j, k: (0, j)))
        operands.append(arr)
    n_e = len(extras)
    n_o = len(out_dtypes)
    dims = (((0,) if ta else (1,), (1,) if tb else (0,)), ((), ()))

    def body(*refs):
        a_refs = refs[:n_a]
        b_refs = refs[n_a:n_a + n_b]
        e_refs = refs[n_a + n_b:n_a + n_b + n_e]
        n_in = n_a + n_b + n_e + (1 if stack is not None else 0)
        o_refs = refs[n_in:n_in + n_o]
        acc_refs = refs[n_in + n_o:]
        i, j, k = pl.program_id(0), pl.program_id(1), pl.program_id(2)
        if a_pair:
            cidx = i if ta else k
            av = jnp.where(cidx < a_half, a_refs[0][...], a_refs[1][...])
        else:
            av = a_refs[0][...]
        av = av.astype(BF)
        prods = []
        for q in range(n_acc):
            if b_pair:
                cidx = (k if tb else j) + b_offsets[q]
                bv = jnp.where(cidx < b_half, b_refs[2 * q][...], b_refs[2 * q + 1][...])
            else:
                bv = b_refs[q][...]
            prods.append(lax.dot_general(av, bv.astype(BF), dims, preferred_element_type=F32))

        def finish(accs):
            outs = epilogue(accs, *[r[...] for r in e_refs]) if epilogue is not None else accs
            for o_ref, o in zip(o_refs, outs):
                o_ref[...] = o.astype(o_ref.dtype)

        if nk == 1:
            finish(prods)
        else:
            @pl.when(k == 0)
            def _():
                for r, p in zip(acc_refs, prods):
                    r[...] = p

            @pl.when(k > 0)
            def _():
                for r, p in zip(acc_refs, prods):
                    r[...] += p

            @pl.when(k == nk - 1)
            def _():
                finish([r[...] for r in acc_refs])

    if stack is not None:
        buf, layer = stack
        assert n_o == 1 and buf.shape[1:] == (M, N)
        return pl.pallas_call(
            body, name=name,
            grid=(M // tm, N // tn, nk),
            in_specs=in_specs + [pl.BlockSpec(memory_space=pl.ANY)],
            out_specs=[pl.BlockSpec((None, tm, tn), lambda i, j, k: (layer, i, j))],
            out_shape=[jax.ShapeDtypeStruct(buf.shape, buf.dtype)],
            scratch_shapes=[pltpu.VMEM((tm, tn), F32) for _ in range(n_acc if nk > 1 else 0)],
            input_output_aliases={len(operands): 0},
            compiler_params=_cp(("parallel", "parallel", "arbitrary")),
        )(*operands, buf)[0]
    return pl.pallas_call(
        body, name=name,
        grid=(M // tm, N // tn, nk),
        in_specs=in_specs,
        out_specs=[pl.BlockSpec((tm, tn), lambda i, j, k: (i, j)) for _ in range(n_o)],
        out_shape=[jax.ShapeDtypeStruct((M, N), dt) for dt in out_dtypes],
        scratch_shapes=[pltpu.VMEM((tm, tn), F32) for _ in range(n_acc if nk > 1 else 0)],
        compiler_params=_cp(("parallel", "parallel", "arbitrary")),
    )(*operands)


def _norm_fwd(x, g, *, name, res=None, out_dtype=F32, tm=512):
    T, Dm = x.shape
    has_res = res is not None

    def body(*refs):
        if has_res:
            x_ref, g_ref, r_ref, y_ref, s_ref = refs
        else:
            x_ref, g_ref, y_ref, s_ref = refs
        xv = x_ref[...]
        rstd = lax.rsqrt(jnp.mean(xv * xv, axis=-1, keepdims=True) + EPS)
        y = xv * rstd * g_ref[...]
        if has_res:
            y = r_ref[...] + y
        y_ref[...] = y.astype(y_ref.dtype)
        s_ref[...] = rstd

    row = pl.BlockSpec((tm, Dm), lambda i: (i, 0))
    in_specs = [row, pl.BlockSpec((1, Dm), lambda i: (0, 0))] + ([row] if has_res else [])
    ops = [x, g.reshape(1, Dm)] + ([res] if has_res else [])
    return pl.pallas_call(
        body, name=name, grid=(T // tm,), in_specs=in_specs,
        out_specs=[row, pl.BlockSpec((tm, 1), lambda i: (i, 0))],
        out_shape=[jax.ShapeDtypeStruct((T, Dm), out_dtype), jax.ShapeDtypeStruct((T, 1), F32)],
        compiler_params=_cp(("parallel",)),
    )(*ops)


def _norm_bwd(dout, x, rstd, g, *, name, add=None, out_dtype=F32, tm=512):
    T, Dm = x.shape
    has_add = add is not None
    nt = T // tm

    def body(*refs):
        if has_add:
            do_ref, x_ref, s_ref, g_ref, a_ref, dx_ref, dg_ref, acc = refs
        else:
            do_ref, x_ref, s_ref, g_ref, dx_ref, dg_ref, acc = refs
        i = pl.program_id(0)
        do = do_ref[...].astype(F32)
        xh = x_ref[...] * s_ref[...]
        gd = do * g_ref[...]
        dx = s_ref[...] * (gd - xh * jnp.mean(gd * xh, axis=-1, keepdims=True))
        if has_add:
            dx = dx + a_ref[...].astype(F32)
        dx_ref[...] = dx.astype(dx_ref.dtype)
        part = jnp.sum((do * xh).reshape(tm // 8, 8, Dm), axis=0)

        @pl.when(i == 0)
        def _():
            acc[...] = part

        @pl.when(i > 0)
        def _():
            acc[...] += part

        @pl.when(i == nt - 1)
        def _():
            dg_ref[...] = jnp.sum(acc[...], axis=0, keepdims=True)

    row = pl.BlockSpec((tm, Dm), lambda i: (i, 0))
    in_specs = [row, row, pl.BlockSpec((tm, 1), lambda i: (i, 0)), pl.BlockSpec((1, Dm), lambda i: (0, 0))]
    ops = [dout, x, rstd, g.reshape(1, Dm)]
    if has_add:
        in_specs.append(row)
        ops.append(add)
    return pl.pallas_call(
        body, name=name, grid=(nt,), in_specs=in_specs,
        out_specs=[row, pl.BlockSpec((1, Dm), lambda i: (0, 0))],
        out_shape=[jax.ShapeDtypeStruct((T, Dm), out_dtype), jax.ShapeDtypeStruct((1, Dm), F32)],
        scratch_shapes=[pltpu.VMEM((8, Dm), F32)],
        compiler_params=_cp(("arbitrary",)),
    )(*ops)


def _swiglu_fwd_epilogue(accs):
    g, u = accs
    return g, u, g * jax.nn.sigmoid(g) * u


def _swiglu_bwd_epilogue(accs, g, u):
    da = accs[0]
    g = g.astype(F32)
    u = u.astype(F32)
    sig = jax.nn.sigmoid(g)
    return da * u * (sig * (1.0 + g * (1.0 - sig))), da * (g * sig)


def _rope_tables(pos, *, name, tm=1024):
    T = pos.shape[0]
    half = HEAD_DIM // 2
    freqs = ROPE_THETA ** (-jnp.arange(half, dtype=F32) / half)
    freqs = jnp.tile(freqs, 4).reshape(1, 128)

    def body(p_ref, f_ref, c_ref, s_ref):
        ang = p_ref[...].astype(F32) * f_ref[...]
        lane = lax.broadcasted_iota(jnp.int32, ang.shape, 1)
        c_ref[...] = jnp.cos(ang)
        s_ref[...] = jnp.where(lane % HEAD_DIM < half, -1.0, 1.0) * jnp.sin(ang)

    tab = pl.BlockSpec((tm, 128), lambda i: (i, 0))
    return pl.pallas_call(
        body, name=name, grid=(T // tm,),
        in_specs=[pl.BlockSpec((tm, 1), lambda i: (i, 0)), pl.BlockSpec((1, 128), lambda i: (0, 0))],
        out_specs=[tab, tab],
        out_shape=[jax.ShapeDtypeStruct((T, 128), F32)] * 2,
        compiler_params=_cp(("parallel",)),
    )(pos, freqs)


def _rot(x, cos, sin, sign):
    W = x.shape[1]
    half = HEAD_DIM // 2
    reps = W // 128
    c = jnp.concatenate([cos] * reps, axis=1) if reps > 1 else cos
    s = jnp.concatenate([sin] * reps, axis=1) if reps > 1 else sin
    lane = lax.broadcasted_iota(jnp.int32, x.shape, 1)
    swapped = jnp.where(lane % HEAD_DIM < half, pltpu.roll(x, W - half, axis=1), pltpu.roll(x, half, axis=1))
    return x * c + (sign * s) * swapped


def _rope_apply(x, cos, sin, *, name, sign=1.0, width=MAIN_W, passthrough=False, out_dtype=BF, alias=None,
                out_cols=None, tm=512):
    T = x.shape[0]

    def body(*refs):
        if passthrough:
            x_ref, v_ref, c_ref, s_ref, o_ref, ov_ref = refs
            ov_ref[...] = v_ref[...].astype(ov_ref.dtype)
        elif alias is not None:
            x_ref, c_ref, s_ref, _, o_ref = refs
        else:
            x_ref, c_ref, s_ref, o_ref = refs
        o_ref[...] = _rot(x_ref[...].astype(F32), c_ref[...], s_ref[...], sign).astype(o_ref.dtype)

    blk0 = pl.BlockSpec((tm, width), lambda i: (i, 0))
    blk1 = pl.BlockSpec((tm, width), lambda i: (i, 1))
    tab = pl.BlockSpec((tm, 128), lambda i: (i, 0))
    if passthrough:
        return pl.pallas_call(
            body, name=name, grid=(T // tm,), in_specs=[blk0, blk1, tab, tab], out_specs=[blk0, blk0],
            out_shape=[jax.ShapeDtypeStruct((T, width), out_dtype)] * 2,
            compiler_params=_cp(("parallel",)),
        )(x, x, cos, sin)
    if alias is not None:
        return pl.pallas_call(
            body, name=name, grid=(T // tm,),
            in_specs=[blk0, tab, tab, pl.BlockSpec(memory_space=pl.ANY)], out_specs=blk0,
            out_shape=jax.ShapeDtypeStruct(alias.shape, alias.dtype),
            input_output_aliases={3: 0},
            compiler_params=_cp(("parallel",)),
        )(x, cos, sin, alias)
    return pl.pallas_call(
        body, name=name, grid=(T // tm,), in_specs=[blk0, tab, tab], out_specs=blk0,
        out_shape=jax.ShapeDtypeStruct((T, width), out_dtype),
        compiler_params=_cp(("parallel",)),
    )(x, cos, sin)


POOL_T = 256
POOL_HALO = 16


def _pool_lane_window(shape):
    lane = lax.broadcasted_iota(jnp.int32, shape, 1)
    w = jnp.full(shape, POOL_WINDOWS[0], jnp.int32)
    for gi in range(1, len(POOL_WINDOWS)):
        w = jnp.where(lane >= gi * POOL_GROUP, POOL_WINDOWS[gi], w)
    return w


def _pool_fwd(z, wbd, scale, B, S, *, name):
    T = z.shape[0]
    nt = S // POOL_T
    hb = POOL_T // POOL_HALO

    def body(z_ref, h_ref, w_ref, sc_ref, y_ref, p_ref, ext):
        i = pl.program_id(1)
        u = z_ref[...]
        ext[pl.ds(POOL_HALO, POOL_T), :] = u
        ext[pl.ds(0, POOL_HALO), :] = jnp.where(i > 0, h_ref[...], 0.0)
        win = _pool_lane_window((POOL_T, MAIN_W))
        acc = u
        for k in range(1, POOL_HALO):
            acc = acc + jnp.where(k < win, ext[pl.ds(POOL_HALO - k, POOL_T), :], 0.0)
        t = i * POOL_T + lax.broadcasted_iota(jnp.int32, (POOL_T, MAIN_W), 0)
        cnt = jnp.minimum(t + 1, win).astype(F32)
        p = (acc / cnt - u).astype(BF)
        p_ref[...] = p
        y = jnp.dot(p, w_ref[...], preferred_element_type=F32) * sc_ref[...]
        y_ref[...] = y.astype(y_ref.dtype)

    return pl.pallas_call(
        body, name=name, grid=(B, nt),
        in_specs=[pl.BlockSpec((POOL_T, MAIN_W), lambda b, i: (b * nt + i, 0)),
                  pl.BlockSpec((POOL_HALO, MAIN_W), lambda b, i: (jnp.maximum((b * nt + i) * hb - 1, 0), 0)),
                  pl.BlockSpec((MAIN_W, MAIN_W), lambda b, i: (0, 0)),
                  pl.BlockSpec((1, MAIN_W), lambda b, i: (0, 0))],
        out_specs=[pl.BlockSpec((POOL_T, MAIN_W), lambda b, i: (b * nt + i, 0)),
                   pl.BlockSpec((POOL_T, MAIN_W), lambda b, i: (b * nt + i, 0))],
        out_shape=[jax.ShapeDtypeStruct((T, D_MODEL), BF), jax.ShapeDtypeStruct((T, MAIN_W), BF)],
        scratch_shapes=[pltpu.VMEM((POOL_T + POOL_HALO, MAIN_W), F32)],
        compiler_params=_cp(("parallel", "parallel")),
    )(z, z, wbd, scale.reshape(1, MAIN_W))


def _pool_bwd(dy, p, wbd, scale, dz_alias, B, S, *, name):
    T = dy.shape[0]
    nt = S // POOL_T
    hb = POOL_T // POOL_HALO
    last_halo = T // POOL_HALO - 1
    R = POOL_T + POOL_HALO

    def body(dy_ref, dyn_ref, p_ref, pn_ref, w_ref, sc_ref, _, dz_ref, dw_ref, ds_ref, ext, dw_acc, ds_acc):
        b, i = pl.program_id(0), pl.program_id(1)
        first = jnp.logical_and(b == 0, i == 0)
        dyv = dy_ref[...]
        pv = p_ref[...]
        sc = sc_ref[...]
        w = w_ref[...]
        pw = jnp.dot(pv, w, preferred_element_type=F32)
        ds_part = jnp.sum((dyv * pw).reshape(POOL_T // 8, 8, MAIN_W), axis=0)
        dpw = (dyv * sc).astype(BF)
        dw_part = lax.dot_general(pv, dpw, (((0,), (0,)), ((), ())), preferred_element_type=F32)

        @pl.when(first)
        def _():
            dw_acc[...] = dw_part
            ds_acc[...] = ds_part

        @pl.when(jnp.logical_not(first))
        def _():
            dw_acc[...] += dw_part
            ds_acc[...] += ds_part

        @pl.when(jnp.logical_and(b == pl.num_programs(0) - 1, i == nt - 1))
        def _():
            dw_ref[...] = dw_acc[...]
            ds_ref[...] = jnp.sum(ds_acc[...], axis=0, keepdims=True)

        dp = lax.dot_general(dpw, w, (((1,), (1,)), ((), ())), preferred_element_type=F32)
        dpn = lax.dot_general((dyn_ref[...] * sc).astype(BF), w, (((1,), (1,)), ((), ())), preferred_element_type=F32)
        win = _pool_lane_window((POOL_T, MAIN_W))
        win_n = _pool_lane_window((POOL_HALO, MAIN_W))
        t = i * POOL_T + lax.broadcasted_iota(jnp.int32, (POOL_T, MAIN_W), 0)
        tn = (i + 1) * POOL_T + lax.broadcasted_iota(jnp.int32, (POOL_HALO, MAIN_W), 0)
        ext[pl.ds(0, POOL_T), :] = dp / jnp.minimum(t + 1, win).astype(F32)
        ext[pl.ds(POOL_T, POOL_HALO), :] = jnp.where(i < nt - 1, dpn / jnp.minimum(tn + 1, win_n).astype(F32), 0.0)
        acc = -dp
        for k in range(POOL_HALO):
            acc = acc + jnp.where(k < win, ext[pl.ds(k, POOL_T), :], 0.0)
        dz_ref[...] = acc.astype(dz_ref.dtype)

    cur = lambda b, i: (b * nt + i, 0)
    nxt = lambda b, i: (jnp.minimum((b * nt + i + 1) * hb, last_halo), 0)
    return pl.pallas_call(
        body, name=name, grid=(B, nt),
        in_specs=[pl.BlockSpec((POOL_T, MAIN_W), cur), pl.BlockSpec((POOL_HALO, MAIN_W), nxt),
                  pl.BlockSpec((POOL_T, MAIN_W), cur), pl.BlockSpec((POOL_HALO, MAIN_W), nxt),
                  pl.BlockSpec((MAIN_W, MAIN_W), lambda b, i: (0, 0)),
                  pl.BlockSpec((1, MAIN_W), lambda b, i: (0, 0)),
                  pl.BlockSpec(memory_space=pl.ANY)],
        out_specs=[pl.BlockSpec((POOL_T, MAIN_W), cur),
                   pl.BlockSpec((MAIN_W, MAIN_W), lambda b, i: (0, 0)),
                   pl.BlockSpec((1, MAIN_W), lambda b, i: (0, 0))],
        out_shape=[jax.ShapeDtypeStruct(dz_alias.shape, dz_alias.dtype),
                   jax.ShapeDtypeStruct((MAIN_W, MAIN_W), F32), jax.ShapeDtypeStruct((1, MAIN_W), F32)],
        scratch_shapes=[pltpu.VMEM((R, MAIN_W), F32), pltpu.VMEM((MAIN_W, MAIN_W), F32), pltpu.VMEM((8, MAIN_W), F32)],
        input_output_aliases={6: 0},
        compiler_params=_cp(("arbitrary", "arbitrary")),
    )(dy, dy, p, p, wbd, scale.reshape(1, MAIN_W), dz_alias)


def _head_masks(shape):
    lane = lax.broadcasted_iota(jnp.int32, shape, 1)
    return [(lane // HEAD_DIM) == h for h in range(shape[1] // HEAD_DIM)]


def _row_of(bcast, mask):
    return jnp.max(jnp.where(mask, bcast, -jnp.inf), axis=-1, keepdims=True)


MEM_TQ = 512


def _memattn_fwd(z, kv, y_alias, B, S, *, name):
    T = z.shape[0]
    nt = S // MEM_TQ

    def body(q_ref, k_ref, v_ref, _, y_ref, l_ref):
        q = q_ref[...]
        k = k_ref[...]
        v = v_ref[...]
        masks = _head_masks(q.shape)
        o = jnp.zeros(q.shape, F32)
        lse_b = jnp.zeros(q.shape, F32)
        for m in masks:
            qm = jnp.where(m, q, 0.0).astype(BF)
            s = lax.dot_general(qm, k, (((1,), (1,)), ((), ())), preferred_element_type=F32) * SCALE
            mx = jnp.max(s, axis=-1, keepdims=True)
            e = jnp.exp(s - mx)
            l = jnp.sum(e, axis=-1, keepdims=True)
            p = (e / l).astype(BF)
            o = o + jnp.where(m, jnp.dot(p, v, preferred_element_type=F32), 0.0)
            lse_b = lse_b + jnp.where(m, mx + jnp.log(l), 0.0)
        y_ref[...] = o.astype(y_ref.dtype)
        l_ref[...] = lse_b

    qblk = pl.BlockSpec((MEM_TQ, MEM_W), lambda b, i: (b * nt + i, 3))
    return pl.pallas_call(
        body, name=name, grid=(B, nt),
        in_specs=[qblk, pl.BlockSpec((N_MEM, MEM_W), lambda b, i: (b, 0)), pl.BlockSpec((N_MEM, MEM_W), lambda b, i: (b, 1)),
                  pl.BlockSpec(memory_space=pl.ANY)],
        out_specs=[qblk, pl.BlockSpec((MEM_TQ, MEM_W), lambda b, i: (b * nt + i, 0))],
        out_shape=[jax.ShapeDtypeStruct(y_alias.shape, y_alias.dtype), jax.ShapeDtypeStruct((T, MEM_W), F32)],
        input_output_aliases={3: 0},
        compiler_params=_cp(("parallel", "parallel")),
    )(z, kv, kv, y_alias)


def _memattn_bwd(dy, z, kv, lse, dz_alias, B, S, *, name):
    nt = S // MEM_TQ

    def body(do_ref, q_ref, k_ref, v_ref, l_ref, _, dz_ref, dk_ref, dv_ref, dk_acc, dv_acc):
        i = pl.program_id(1)
        do = do_ref[...]
        q = q_ref[...]
        k = k_ref[...]
        v = v_ref[...]
        lse_b = l_ref[...]
        masks = _head_masks(q.shape)
        dq = jnp.zeros(q.shape, F32)
        dk = jnp.zeros(k.shape, F32)
        dv = jnp.zeros(v.shape, F32)
        for m in masks:
            qm = jnp.where(m, q, 0.0).astype(BF)
            dom = jnp.where(m, do, 0.0).astype(BF)
            s = lax.dot_general(qm, k, (((1,), (1,)), ((), ())), preferred_element_type=F32) * SCALE
            p = jnp.exp(s - _row_of(lse_b, m))
            dp = lax.dot_general(dom, v, (((1,), (1,)), ((), ())), preferred_element_type=F32)
            delta = jnp.sum(p * dp, axis=-1, keepdims=True)
            ds = (p * (dp - delta) * SCALE).astype(BF)
            pb = p.astype(BF)
            dv = dv + jnp.where(m[:N_MEM], lax.dot_general(pb, dom, (((0,), (0,)), ((), ())), preferred_element_type=F32), 0.0)
            dk = dk + jnp.where(m[:N_MEM], lax.dot_general(ds, qm, (((0,), (0,)), ((), ())), preferred_element_type=F32), 0.0)
            dq = dq + jnp.where(m, jnp.dot(ds, k, preferred_element_type=F32), 0.0)
        dz_ref[...] = dq.astype(dz_ref.dtype)

        @pl.when(i == 0)
        def _():
            dk_acc[...] = dk
            dv_acc[...] = dv

        @pl.when(i > 0)
        def _():
            dk_acc[...] += dk
            dv_acc[...] += dv

        @pl.when(i == nt - 1)
        def _():
            dk_ref[...] = dk_acc[...]
            dv_ref[...] = dv_acc[...]

    qblk = pl.BlockSpec((MEM_TQ, MEM_W), lambda b, i: (b * nt + i, 3))
    kblk = pl.BlockSpec((N_MEM, MEM_W), lambda b, i: (b, 0))
    return pl.pallas_call(
        body, name=name, grid=(B, nt),
        in_specs=[qblk, qblk, kblk, pl.BlockSpec((N_MEM, MEM_W), lambda b, i: (b, 1)),
                  pl.BlockSpec((MEM_TQ, MEM_W), lambda b, i: (b * nt + i, 0)), pl.BlockSpec(memory_space=pl.ANY)],
        out_specs=[qblk, kblk, kblk],
        out_shape=[jax.ShapeDtypeStruct(dz_alias.shape, dz_alias.dtype),
                   jax.ShapeDtypeStruct((B * N_MEM, MEM_W), F32), jax.ShapeDtypeStruct((B * N_MEM, MEM_W), F32)],
        scratch_shapes=[pltpu.VMEM((N_MEM, MEM_W), F32), pltpu.VMEM((N_MEM, MEM_W), F32)],
        input_output_aliases={5: 0},
        compiler_params=_cp(("parallel", "arbitrary")),
    )(dy, z, kv, kv, lse, dz_alias)


def _dil_scores(qm, kp, kc, n):
    qi = lax.broadcasted_iota(jnp.int32, (STEPS, STEPS), 0)
    kj = lax.broadcasted_iota(jnp.int32, (STEPS, STEPS), 1)
    sc = lax.dot_general(qm, kc, (((1,), (1,)), ((), ())), preferred_element_type=F32) * SCALE
    sc = jnp.where(kj <= qi, sc, NEG)
    if kp is None:
        return None, sc
    sp = lax.dot_general(qm, kp, (((1,), (1,)), ((), ())), preferred_element_type=F32) * SCALE
    sp = jnp.where(jnp.logical_and(kj >= qi, n > 0), sp, NEG)
    return sp, sc


def _dil_specs(g, d, nb):
    chunk = STEPS * d
    cur = pl.BlockSpec((chunk, 128), lambda b, n, hf: (b * nb + n, g * 2 + hf))
    prev = pl.BlockSpec((chunk, 128), lambda b, n, hf: (b * nb + jnp.maximum(n - 1, 0), g * 2 + hf))
    return cur, prev


def _dil_rows(r, d):
    return pl.ds(r, STEPS, stride=d) if d > 1 else slice(None)


def _dil_loop(d, fn):
    if d <= 4:
        for r in range(d):
            fn(r)
    else:
        lax.fori_loop(0, d, lambda r, carry: (fn(r), carry)[1], 0)


def _dil_fwd_group(g, q, k, v, o_alias, l_alias, B, S, *, name):
    d = DIL[g]
    nb = S // (STEPS * d)
    has_prev = nb > 1

    def body(*refs):
        if has_prev:
            q_ref, kp_ref, kc_ref, vp_ref, vc_ref, _, __, o_ref, l_ref = refs
        else:
            q_ref, kc_ref, vc_ref, _, __, o_ref, l_ref = refs
        n = pl.program_id(1)

        def residue(r):
            rows = _dil_rows(r, d)
            q = q_ref[rows, :]
            kc, vc = kc_ref[rows, :].astype(BF), vc_ref[rows, :].astype(BF)
            kp = kp_ref[rows, :].astype(BF) if has_prev else None
            vp = vp_ref[rows, :].astype(BF) if has_prev else None
            o = jnp.zeros(q.shape, F32)
            lse_b = jnp.zeros(q.shape, F32)
            for m in _head_masks(q.shape):
                qm = jnp.where(m, q, 0.0).astype(BF)
                sp, sc = _dil_scores(qm, kp, kc, n)
                mx = jnp.max(sc, axis=-1, keepdims=True)
                if has_prev:
                    mx = jnp.maximum(mx, jnp.max(sp, axis=-1, keepdims=True))
                l = jnp.sum(jnp.exp(sc - mx), axis=-1, keepdims=True)
                if has_prev:
                    l = l + jnp.sum(jnp.exp(sp - mx), axis=-1, keepdims=True)
                lse = mx + jnp.log(l)
                oh = jnp.dot(jnp.exp(sc - lse).astype(BF), vc, preferred_element_type=F32)
                if has_prev:
                    oh = oh + jnp.dot(jnp.exp(sp - lse).astype(BF), vp, preferred_element_type=F32)
                o = o + jnp.where(m, oh, 0.0)
                lse_b = lse_b + jnp.where(m, lse, 0.0)
            o_ref[rows, :] = o
            l_ref[rows, :] = lse_b

        _dil_loop(d, residue)

    cur, prev = _dil_specs(g, d, nb)
    anyspec = pl.BlockSpec(memory_space=pl.ANY)
    if has_prev:
        in_specs, ops = [cur, prev, cur, prev, cur], [q, k, k, v, v]
    else:
        in_specs, ops = [cur, cur, cur], [q, k, v]
    n_in = len(ops)
    o, l = pl.pallas_call(
        body, name=name, grid=(B, nb, 2),
        in_specs=in_specs + [anyspec, anyspec],
        out_specs=[cur, cur],
        out_shape=[jax.ShapeDtypeStruct(q.shape, F32)] * 2,
        input_output_aliases={n_in: 0, n_in + 1: 1},
        compiler_params=_cp(("parallel", "parallel", "parallel")),
    )(*ops, o_alias, l_alias)
    return o, l


def _dil_bwd_group(g, q, k, v, do, cb, lse, aliases, B, S, *, name):
    d = DIL[g]
    nb = S // (STEPS * d)
    has_prev = nb > 1
    n_out = 5 if has_prev else 3

    def body(*refs):
        if has_prev:
            q_ref, kp_ref, kc_ref, vp_ref, vc_ref, do_ref, c_ref, l_ref = refs[:8]
            dq_ref, dkc_ref, dvc_ref, dkp_ref, dvp_ref = refs[8 + n_out:]
        else:
            q_ref, kc_ref, vc_ref, do_ref, c_ref, l_ref = refs[:6]
            dq_ref, dkc_ref, dvc_ref = refs[6 + n_out:]
        n = pl.program_id(1)
        tdot = lambda a, b: lax.dot_general(a, b, (((0,), (0,)), ((), ())), preferred_element_type=F32)
        ndot = lambda a, b: lax.dot_general(a, b, (((1,), (1,)), ((), ())), preferred_element_type=F32)

        def residue(r):
            rows = _dil_rows(r, d)
            q = q_ref[rows, :]
            kc, vc = kc_ref[rows, :].astype(BF), vc_ref[rows, :].astype(BF)
            kp = kp_ref[rows, :].astype(BF) if has_prev else None
            vp = vp_ref[rows, :].astype(BF) if has_prev else None
            do = do_ref[rows, :]
            cbv = c_ref[rows, :]
            lse_b = l_ref[rows, :]
            z = jnp.zeros(q.shape, F32)
            dq, dkc, dkp, dvc, dvp = z, z, z, z, z
            for m in _head_masks(q.shape):
                qm = jnp.where(m, q, 0.0).astype(BF)
                dom = jnp.where(m, do, 0.0).astype(BF)
                sp, sc = _dil_scores(qm, kp, kc, n)
                lse = _row_of(lse_b, m)
                c = _row_of(cbv, m)
                pc = jnp.exp(sc - lse)
                dsc = (pc * (ndot(dom, vc) + c) * SCALE).astype(BF)
                dqh = jnp.dot(dsc, kc, preferred_element_type=F32)
                dkc = dkc + jnp.where(m, tdot(dsc, qm), 0.0)
                dvc = dvc + jnp.where(m, tdot(pc.astype(BF), dom), 0.0)
                if has_prev:
                    pp = jnp.exp(sp - lse)
                    dsp = (pp * (ndot(dom, vp) + c) * SCALE).astype(BF)
                    dqh = dqh + jnp.dot(dsp, kp, preferred_element_type=F32)
                    dkp = dkp + jnp.where(m, tdot(dsp, qm), 0.0)
                    dvp = dvp + jnp.where(m, tdot(pp.astype(BF), dom), 0.0)
                dq = dq + jnp.where(m, dqh, 0.0)
            dq_ref[rows, :] = dq
            dkc_ref[rows, :] = dkc
            dvc_ref[rows, :] = dvc
            if has_prev:
                dkp_ref[rows, :] = dkp
                dvp_ref[rows, :] = dvp

        _dil_loop(d, residue)

    cur, prev = _dil_specs(g, d, nb)
    anyspec = pl.BlockSpec(memory_space=pl.ANY)
    dq_a, dkc_a, dkp_a, dvc_a, dvp_a = aliases
    if has_prev:
        in_specs, ops = [cur, prev, cur, prev, cur, cur, cur, cur], [q, k, k, v, v, do, cb, lse]
        al = [dq_a, dkc_a, dvc_a, dkp_a, dvp_a]
    else:
        in_specs, ops = [cur, cur, cur, cur, cur, cur], [q, k, v, do, cb, lse]
        al = [dq_a, dkc_a, dvc_a]
    n_in = len(ops)
    outs = pl.pallas_call(
        body, name=name, grid=(B, nb, 2),
        in_specs=in_specs + [anyspec] * n_out,
        out_specs=[cur] * n_out,
        out_shape=[jax.ShapeDtypeStruct(q.shape, F32)] * n_out,
        input_output_aliases={n_in + i: i for i in range(n_out)},
        compiler_params=_cp(("parallel", "parallel", "parallel")),
    )(*ops, *al)
    if has_prev:
        dq_a, dkc_a, dvc_a, dkp_a, dvp_a = outs
    else:
        dq_a, dkc_a, dvc_a = outs
    return dq_a, dkc_a, dkp_a, dvc_a, dvp_a


N_UNITS = 16


def _unit_rows(g):
    d = DIL[g]
    nb = N_UNITS // d
    return [pl.ds(n * STEPS * d + r, STEPS, stride=d) if d > 1 else pl.ds(n * STEPS, STEPS)
            for n in range(nb) for r in range(d)]


def _load_units(ref, g):
    if DIL[g] == 1:
        return ref[...].reshape(N_UNITS, STEPS, 128)
    return jnp.stack([ref[rows, :] for rows in _unit_rows(g)])


def _store_units(ref, val, g):
    if DIL[g] == 1:
        ref[...] = val.reshape(N_UNITS * STEPS, 128)
    else:
        for u, rows in enumerate(_unit_rows(g)):
            ref[rows, :] = val[u]


def _shift_units(x, by):
    z = jnp.zeros((abs(by),) + x.shape[1:], x.dtype)
    return jnp.concatenate([z, x[:N_UNITS - by]], axis=0) if by > 0 else jnp.concatenate([x[-by:], z], axis=0)


def _bdot(a, b, ca, cb):
    return lax.dot_general(a, b, (((ca,), (cb,)), ((0,), (0,))), preferred_element_type=F32)


def _dil_masks(g):
    d = DIL[g]
    has_prev = N_UNITS // d > 1
    qi = lax.broadcasted_iota(jnp.int32, (1, STEPS, STEPS), 1)
    kj = lax.broadcasted_iota(jnp.int32, (1, STEPS, STEPS), 2)
    unit = lax.broadcasted_iota(jnp.int32, (N_UNITS, 1, 1), 0)
    cur = kj <= qi
    prev = jnp.logical_and(kj >= qi, unit >= d) if has_prev else None
    lane = lax.broadcasted_iota(jnp.int32, (1, 1, 128), 2)
    heads = [(lane // HEAD_DIM) == h for h in range(128 // HEAD_DIM)]
    return has_prev, cur, prev, heads


def _dil_fwd(g, q, k, v, o_alias, l_alias, B, S, *, name):
    assert S == N_UNITS * STEPS
    d = DIL[g]

    def body(q_ref, k_ref, v_ref, _, __, o_ref, l_ref):
        has_prev, cur, prev, heads = _dil_masks(g)
        q = _load_units(q_ref, g)
        kc = _load_units(k_ref, g).astype(BF)
        vc = _load_units(v_ref, g).astype(BF)
        if has_prev:
            kp, vp = _shift_units(kc, d), _shift_units(vc, d)
        o = jnp.zeros(q.shape, F32)
        lse_b = jnp.zeros(q.shape, F32)
        for m in heads:
            qm = jnp.where(m, q, 0.0).astype(BF)
            sc = jnp.where(cur, _bdot(qm, kc, 2, 2) * SCALE, NEG)
            mx = jnp.max(sc, axis=-1, keepdims=True)
            if has_prev:
                sp = jnp.where(prev, _bdot(qm, kp, 2, 2) * SCALE, NEG)
                mx = jnp.maximum(mx, jnp.max(sp, axis=-1, keepdims=True))
            l = jnp.sum(jnp.exp(sc - mx), axis=-1, keepdims=True)
            if has_prev:
                l = l + jnp.sum(jnp.exp(sp - mx), axis=-1, keepdims=True)
            lse = mx + jnp.log(l)
            oh = _bdot(jnp.exp(sc - lse).astype(BF), vc, 2, 1)
            if has_prev:
                oh = oh + _bdot(jnp.exp(sp - lse).astype(BF), vp, 2, 1)
            o = o + jnp.where(m, oh, 0.0)
            lse_b = lse_b + jnp.where(m, lse, 0.0)
        _store_units(o_ref, o, g)
        _store_units(l_ref, lse_b, g)

    blk = pl.BlockSpec((S, 128), lambda b, hf: (b, g * 2 + hf))
    anyspec = pl.BlockSpec(memory_space=pl.ANY)
    o, l = pl.pallas_call(
        body, name=name, grid=(B, 2),
        in_specs=[blk, blk, blk, anyspec, anyspec], out_specs=[blk, blk],
        out_shape=[jax.ShapeDtypeStruct(q.shape, F32)] * 2,
        input_output_aliases={3: 0, 4: 1},
        compiler_params=_cp(("parallel", "parallel")),
    )(q, k, v, o_alias, l_alias)
    return o, l


def _dil_bwd(g, q, k, v, do, cb, lse, aliases, B, S, *, name):
    assert S == N_UNITS * STEPS
    d = DIL[g]

    def body(q_ref, k_ref, v_ref, do_ref, c_ref, l_ref, _, __, ___, dq_ref, dk_ref, dv_ref):
        has_prev, cur, prev, heads = _dil_masks(g)
        q = _load_units(q_ref, g)
        kc = _load_units(k_ref, g).astype(BF)
        vc = _load_units(v_ref, g).astype(BF)
        do = _load_units(do_ref, g)
        cbv = _load_units(c_ref, g)
        lse_b = _load_units(l_ref, g)
        if has_prev:
            kp, vp = _shift_units(kc, d), _shift_units(vc, d)
        z = jnp.zeros(q.shape, F32)
        dq, dkc, dkp, dvc, dvp = z, z, z, z, z
        for m in heads:
            qm = jnp.where(m, q, 0.0).astype(BF)
            dom = jnp.where(m, do, 0.0).astype(BF)
            lse = jnp.max(jnp.where(m, lse_b, -jnp.inf), axis=-1, keepdims=True)
            c = jnp.max(jnp.where(m, cbv, -jnp.inf), axis=-1, keepdims=True)
            sc = jnp.where(cur, _bdot(qm, kc, 2, 2) * SCALE, NEG)
            pc = jnp.exp(sc - lse)
            dsc = (pc * (_bdot(dom, vc, 2, 2) + c) * SCALE).astype(BF)
            dqh = _bdot(dsc, kc, 2, 1)
            dkc = dkc + jnp.where(m, _bdot(dsc, qm, 1, 1), 0.0)
            dvc = dvc + jnp.where(m, _bdot(pc.astype(BF), dom, 1, 1), 0.0)
            if has_prev:
                sp = jnp.where(prev, _bdot(qm, kp, 2, 2) * SCALE, NEG)
                pp = jnp.exp(sp - lse)
                dsp = (pp * (_bdot(dom, vp, 2, 2) + c) * SCALE).astype(BF)
                dqh = dqh + _bdot(dsp, kp, 2, 1)
                dkp = dkp + jnp.where(m, _bdot(dsp, qm, 1, 1), 0.0)
                dvp = dvp + jnp.where(m, _bdot(pp.astype(BF), dom, 1, 1), 0.0)
            dq = dq + jnp.where(m, dqh, 0.0)
        if has_prev:
            dkc = dkc + _shift_units(dkp, -d)
            dvc = dvc + _shift_units(dvp, -d)
        _store_units(dq_ref, dq, g)
        _store_units(dk_ref, dkc, g)
        _store_units(dv_ref, dvc, g)

    blk = pl.BlockSpec((S, 128), lambda b, hf: (b, g * 2 + hf))
    anyspec = pl.BlockSpec(memory_space=pl.ANY)
    return tuple(pl.pallas_call(
        body, name=name, grid=(B, 2),
        in_specs=[blk] * 6 + [anyspec] * 3, out_specs=[blk] * 3,
        out_shape=[jax.ShapeDtypeStruct(q.shape, F32)] * 3,
        input_output_aliases={6: 0, 7: 1, 8: 2},
        compiler_params=_cp(("parallel", "parallel")),
    )(q, k, v, do, cb, lse, *aliases))


def _kv_grad_sum(parts, cos, sin, *, name, tm=512):
    T = parts[0][0].shape[0]
    n_l = len(parts)

    def body(*refs):
        c_ref, s_ref = refs[0], refs[1]
        dk_ref, dv_ref = refs[2 + 2 * n_l:]
        dk = refs[2][...]
        dv = refs[3][...]
        for li in range(1, n_l):
            dk = dk + refs[2 + 2 * li][...]
            dv = dv + refs[3 + 2 * li][...]
        dk_ref[...] = _rot(dk, c_ref[...], s_ref[...], -1.0).astype(dk_ref.dtype)
        dv_ref[...] = dv.astype(dv_ref.dtype)

    full = pl.BlockSpec((tm, MAIN_W), lambda i: (i, 0))
    tab = pl.BlockSpec((tm, 128), lambda i: (i, 0))
    ops = [cos, sin] + [t for part in parts for t in part]
    return pl.pallas_call(
        body, name=name, grid=(T // tm,), in_specs=[tab, tab] + [full] * (2 * n_l), out_specs=[full, full],
        out_shape=[jax.ShapeDtypeStruct((T, MAIN_W), BF)] * 2,
        compiler_params=_cp(("parallel",)),
    )(*ops)


def _group_softmax(lse):
    l0, l1, l2 = lse[:, 0:256], lse[:, 256:512], lse[:, 512:768]
    mx = jnp.maximum(jnp.maximum(l0, l1), l2)
    e0, e1, e2 = jnp.exp(l0 - mx), jnp.exp(l1 - mx), jnp.exp(l2 - mx)
    tot = e0 + e1 + e2
    return e0 / tot, e1 / tot, e2 / tot


def _dil_combine_fwd(o, lse, y_alias, *, name, tm=512):
    T = o.shape[0]

    def body(o_ref, l_ref, _, y_ref):
        a = jnp.concatenate(_group_softmax(l_ref[...]), axis=1)
        y_ref[...] = (o_ref[...] * a).astype(y_ref.dtype)

    blk = pl.BlockSpec((tm, MAIN_W), lambda i: (i, 0))
    return pl.pallas_call(
        body, name=name, grid=(T // tm,), in_specs=[blk, blk, pl.BlockSpec(memory_space=pl.ANY)], out_specs=blk,
        out_shape=jax.ShapeDtypeStruct(y_alias.shape, y_alias.dtype), input_output_aliases={2: 0},
        compiler_params=_cp(("parallel",)),
    )(o, lse, y_alias)


def _dil_combine_bwd(dy, o, lse, *, name, tm=256):
    T = o.shape[0]
    lane_r = lax.broadcasted_iota(jnp.int32, (256, 256), 0) // HEAD_DIM
    lane_c = lax.broadcasted_iota(jnp.int32, (256, 256), 1) // HEAD_DIM
    ones_bd = (lane_r == lane_c).astype(BF)

    def body(dy_ref, o_ref, l_ref, e_ref, do_ref, c_ref):
        dyv = dy_ref[...]
        alphas = _group_softmax(l_ref[...])
        prod = dyv * o_ref[...]
        e = e_ref[...]
        tot = jnp.zeros((tm, 256), F32)
        for gi in range(3):
            x = prod[:, gi * 256:(gi + 1) * 256]
            hi = x.astype(BF)
            lo = (x - hi.astype(F32)).astype(BF)
            dalpha = jnp.dot(hi, e, preferred_element_type=F32) + jnp.dot(lo, e, preferred_element_type=F32)
            tot = tot + alphas[gi] * dalpha
        a = jnp.concatenate(alphas, axis=1)
        do_ref[...] = (dyv * a).astype(do_ref.dtype)
        c_ref[...] = jnp.concatenate([-al * tot for al in alphas], axis=1)

    blk = pl.BlockSpec((tm, MAIN_W), lambda i: (i, 0))
    return pl.pallas_call(
        body, name=name, grid=(T // tm,),
        in_specs=[blk, blk, blk, pl.BlockSpec((256, 256), lambda i: (0, 0))], out_specs=[blk, blk],
        out_shape=[jax.ShapeDtypeStruct((T, MAIN_W), F32), jax.ShapeDtypeStruct((T, MAIN_W), F32)],
        compiler_params=_cp(("parallel",)),
    )(dy, o, lse, ones_bd)


def _kv_grad(parts, cos, sin, B, S, *, name):
    T = B * S
    tb = S // STEPS
    n_l = len(parts)

    def shifted(g):
        def f(b, t):
            return (b * tb + jnp.minimum(t + DIL[g], tb - 1), g)
        return f

    with_prev = [g for g in range(3) if DIL[g] < tb]
    n_p = len(with_prev)
    per_l = 2 + 2 * n_p

    def body(*refs):
        c_ref, s_ref = refs[0], refs[1]
        ins = refs[2:2 + n_l * per_l]
        dk_ref, dv_ref = refs[2 + n_l * per_l:]
        t = pl.program_id(1)
        dk = jnp.zeros((STEPS, MAIN_W), F32)
        dv = jnp.zeros((STEPS, MAIN_W), F32)
        zero = jnp.zeros((STEPS, 256), F32)
        for li in range(n_l):
            base = li * per_l
            dk = dk + ins[base][...]
            dv = dv + ins[base + 1][...]
            kparts, vparts = [zero] * 3, [zero] * 3
            for pi, g in enumerate(with_prev):
                ok = t + DIL[g] < tb
                kparts[g] = jnp.where(ok, ins[base + 2 + pi][...], 0.0)
                vparts[g] = jnp.where(ok, ins[base + 2 + n_p + pi][...], 0.0)
            dk = dk + jnp.concatenate(kparts, axis=1)
            dv = dv + jnp.concatenate(vparts, axis=1)
        dk_ref[...] = _rot(dk, c_ref[...], s_ref[...], -1.0).astype(dk_ref.dtype)
        dv_ref[...] = dv.astype(dv_ref.dtype)

    full = pl.BlockSpec((STEPS, MAIN_W), lambda b, t: (b * tb + t, 0))
    tab = pl.BlockSpec((STEPS, 128), lambda b, t: (b * tb + t, 0))
    in_specs, ops = [tab, tab], [cos, sin]
    for (kc, kp, vc, vp) in parts:
        in_specs += [full, full] + [pl.BlockSpec((STEPS, 256), shifted(g)) for g in with_prev] * 2
        ops += [kc, vc] + [kp] * n_p + [vp] * n_p
    return pl.pallas_call(
        body, name=name, grid=(B, tb), in_specs=in_specs, out_specs=[full, full],
        out_shape=[jax.ShapeDtypeStruct((T, MAIN_W), BF)] * 2,
        compiler_params=_cp(("parallel", "parallel")),
    )(*ops)


def _loss(y, target, *, name, tm=512):
    T, Dm = y.shape
    nt = T // tm

    def body(y_ref, t_ref, l_ref, d_ref, acc):
        i = pl.program_id(0)
        err = y_ref[...] - t_ref[...]
        d_ref[...] = err / Dm
        part = jnp.sum(jnp.mean(err * err, axis=-1, keepdims=True).reshape(tm // 8, 8, 1), axis=0)

        @pl.when(i == 0)
        def _():
            acc[...] = part

        @pl.when(i > 0)
        def _():
            acc[...] += part

        @pl.when(i == nt - 1)
        def _():
            l_ref[...] = 0.5 * jnp.sum(acc[...], axis=0, keepdims=True)

    row = pl.BlockSpec((tm, Dm), lambda i: (i, 0))
    return pl.pallas_call(
        body, name=name, grid=(nt,), in_specs=[row, row],
        out_specs=[pl.BlockSpec((1, 1), lambda i: (0, 0)), row],
        out_shape=[jax.ShapeDtypeStruct((1, 1), F32), jax.ShapeDtypeStruct((T, Dm), F32)],
        scratch_shapes=[pltpu.VMEM((8, 1), F32)],
        compiler_params=_cp(("arbitrary",)),
    )(y, target)


def _adamw(w, g, m, v, *, name):
    shape = w.shape
    cols = shape[-1]
    rows = w.size // cols
    tm = rows
    for cand in (512, 352, 256, 128):
        if rows > cand and rows % cand == 0 and cand * cols * 4 <= (1 << 20):
            tm = cand
            break

    def body(w_ref, g_ref, m_ref, v_ref, d_ref, mo_ref, vo_ref):
        gv = g_ref[...]
        mn = ADAM_B1 * m_ref[...] + (1.0 - ADAM_B1) * gv
        vn = ADAM_B2 * v_ref[...] + (1.0 - ADAM_B2) * (gv * gv)
        m_hat = mn / (1.0 - ADAM_B1 ** ADAM_STEP)
        v_hat = vn / (1.0 - ADAM_B2 ** ADAM_STEP)
        d_ref[...] = -ADAM_LR * (m_hat / (jnp.sqrt(v_hat) + ADAM_EPS) + ADAM_WD * w_ref[...])
        mo_ref[...] = mn
        vo_ref[...] = vn

    blk = pl.BlockSpec((tm, cols), lambda i: (i, 0))
    outs = pl.pallas_call(
        body, name=name, grid=(rows // tm,), in_specs=[blk] * 4, out_specs=[blk] * 3,
        out_shape=[jax.ShapeDtypeStruct((rows, cols), F32)] * 3,
        compiler_params=_cp(("parallel",)),
    )(*[t.reshape(rows, cols) for t in (w, g, m, v)])
    return tuple(t.reshape(shape) for t in outs)


def _adamw_layer(name, l, w, g, m, v, prev, after=None):
    L, rows, cols = w.shape
    tm = rows
    for cand in (512, 352, 256, 176, 128, 64):
        if rows % cand == 0 and cand * cols * 4 <= (1 << 20):
            tm = cand
            break
    if prev is None:
        prev = tuple(lax.empty(w.shape, F32) for _ in range(4))

    n_after = 0 if after is None else 1

    def body(w_ref, g_ref, m_ref, v_ref, *rest):
        d_ref, mo_ref, vo_ref, go_ref = rest[4 + n_after:]
        gv = g_ref[...]
        mn = ADAM_B1 * m_ref[...] + (1.0 - ADAM_B1) * gv
        vn = ADAM_B2 * v_ref[...] + (1.0 - ADAM_B2) * (gv * gv)
        m_hat = mn / (1.0 - ADAM_B1 ** ADAM_STEP)
        v_hat = vn / (1.0 - ADAM_B2 ** ADAM_STEP)
        d_ref[...] = -ADAM_LR * (m_hat / (jnp.sqrt(v_hat) + ADAM_EPS) + ADAM_WD * w_ref[...])
        mo_ref[...] = mn
        vo_ref[...] = vn
        go_ref[...] = gv

    lay = pl.BlockSpec((None, tm, cols), lambda i: (l, i, 0))
    one = pl.BlockSpec((None, tm, cols), lambda i: (0, i, 0))
    return tuple(pl.pallas_call(
        body, name=f"l{l}_adamw_{name}", grid=(rows // tm,),
        in_specs=[lay, one, lay, lay] + [pl.BlockSpec(memory_space=pl.ANY)] * (4 + n_after), out_specs=[lay] * 4,
        out_shape=[jax.ShapeDtypeStruct(w.shape, F32)] * 4,
        input_output_aliases={4 + i: i for i in range(4)},
        compiler_params=_cp(("parallel",)),
    )(w, g, m, v, *prev, *([] if after is None else [after])))


BIG = {
    'w_in': ((DEPTH, D_MODEL, D_MODEL), 'row'),
    'w_mem_kv': ((DEPTH, D_MODEL, 2 * MEM_W), 'row'),
    'w_out': ((DEPTH, D_MODEL, D_MODEL), 'row'),
    'w_kv': ((1, D_MODEL, 2 * MAIN_W), 'col'),
    'w_gate_up': ((DEPTH, D_MODEL, 2 * D_FF), 'col'),
    'w_down': ((DEPTH, D_FF, D_MODEL), 'row'),
}
BIG_NAMES = tuple(BIG)
N_CHIPS = 4
HBM_ANY = pl.BlockSpec(memory_space=pl.ANY)


def _geom(name):
    (L, R, C), kind = BIG[name]
    if kind == 'row':
        return L, R, C, kind, R // N_CHIPS, C, R // (2 * N_CHIPS)
    return L, R, C, kind, R, C // N_CHIPS, R // 2


def _shard_shape(name):
    L, R, C, kind, rs, cs, rh = _geom(name)
    return (L, rs, cs)


def _half_shape(name):
    L, R, C, kind, rs, cs, rh = _geom(name)
    return (L, rh, cs)


def _full_win(ref, name, s, h):
    L, R, C, kind, rs, cs, rh = _geom(name)
    if kind == 'row':
        rows = pl.ds(s * rs, rs) if h is None else pl.ds(s * rs + h * rh, rh)
        return ref.at[:, rows, :]
    rows = slice(None) if h is None else pl.ds(h * rh, rh)
    return ref.at[:, rows, pl.ds(s * cs, cs)]


def _shard_half(ref, name, h):
    L, R, C, kind, rs, cs, rh = _geom(name)
    return ref.at[:, pl.ds(h * rh, rh), :]


def _halves_win(ref, name, s):
    L, R, C, kind, rs, cs, rh = _geom(name)
    if kind == 'row':
        return ref.at[:, pl.ds(s * rh, rh), :]
    return ref.at[:, :, pl.ds(s * cs, cs)]


def _halves_shape(name):
    L, R, C, kind, rs, cs, rh = _geom(name)
    return (L, N_CHIPS * rh, cs) if kind == 'row' else (L, rh, C)


def _place():
    x, y, c = lax.axis_index("x"), lax.axis_index("y"), lax.axis_index("c")
    chips = [(1 - x, y), (x, 1 - y), (1 - x, 1 - y)]
    return x, y, c, chips


SMALL_ROWS = 24


def _all_gather(shards, small):
    names = BIG_NAMES
    nw = len(names)

    def body(*refs):
        src = dict(zip(names, refs[:nw]))
        small_ref = refs[nw]
        dst = dict(zip(names, refs[nw + 1:2 * nw + 1]))
        small_out = refs[2 * nw + 1]
        send_sems, recv_sems, local_sems = refs[2 * nw + 2:]
        x, y, c, chips = _place()
        s = 2 * x + y
        sib = (x, y, 1 - c)

        def remote(k, src_ref, dst_ref, to):
            return pltpu.make_async_remote_copy(src_ref=src_ref, dst_ref=dst_ref, send_sem=send_sems.at[k],
                                                recv_sem=recv_sems.at[k], device_id=to, device_id_type=MESH)

        local = []
        for wi, nm in enumerate(names):
            local.append(pltpu.make_async_copy(src[nm], _full_win(dst[nm], nm, s, None), local_sems.at[wi]))
        local.append(pltpu.make_async_copy(small_ref, small_out.at[s], local_sems.at[nw]))
        for cp in local:
            cp.start()
        sends = []
        for j, (px, py) in enumerate(chips):
            for wi, nm in enumerate(names):
                sends.append(remote(wi * 6 + j, _shard_half(src[nm], nm, c), _full_win(dst[nm], nm, s, c), (px, py, c)))
            sends.append(remote(nw * 6 + j, small_ref, small_out.at[s], (px, py, c)))
        for cp in sends:
            cp.start()
        for j, (px, py) in enumerate(chips):
            sp = 2 * px + py
            for wi, nm in enumerate(names):
                w = _full_win(dst[nm], nm, sp, c)
                remote(wi * 6 + j, w, w, sib).wait_recv()
                fwd = remote(wi * 6 + 3 + j, w, w, sib)
                fwd.start()
                sends.append(fwd)
            remote(nw * 6 + j, small_ref, small_out.at[sp], sib).wait_recv()
        for j, (px, py) in enumerate(chips):
            sp = 2 * px + py
            for wi, nm in enumerate(names):
                w = _full_win(dst[nm], nm, sp, 1 - c)
                remote(wi * 6 + 3 + j, w, w, sib).wait_recv()
        for cp in sends:
            cp.wait_send()
        for cp in local:
            cp.wait()

    n_sem = nw * 6 + 3
    outs = pl.pallas_call(
        body, name="all_gather_weights",
        in_specs=[HBM_ANY] * (nw + 1), out_specs=[HBM_ANY] * (nw + 1),
        out_shape=[jax.ShapeDtypeStruct(BIG[nm][0], BF) for nm in names]
        + [jax.ShapeDtypeStruct((N_CHIPS, SMALL_ROWS, 256), F32)],
        scratch_shapes=[pltpu.SemaphoreType.DMA((n_sem,)), pltpu.SemaphoreType.DMA((n_sem,)),
                        pltpu.SemaphoreType.DMA((nw + 1,))],
    )(*[shards[nm] for nm in names], small)
    return dict(zip(names, outs[:nw])), outs[nw]


SEM_SPEC = pl.BlockSpec(memory_space=pltpu.SEMAPHORE)
HBM_SPEC = pl.BlockSpec(memory_space=pltpu.HBM)
DATAFLOW = pltpu.SideEffectType.DATAFLOW_SIDE_EFFECTING


def _in_hbm(a):
    return pltpu.with_memory_space_constraint(a, pltpu.HBM)


def _remote(src, dst, send_sems, recv_sems, k, to):
    return pltpu.make_async_remote_copy(src_ref=src, dst_ref=dst, send_sem=send_sems.at[k], recv_sem=recv_sems.at[k],
                                        device_id=to, device_id_type=MESH)


def _split_start(name, bufs, n_copies, sends, after=None):
    nb = len(bufs)
    n_in = nb + (0 if after is None else 1)

    def body(*refs):
        in_refs = refs[:nb]
        send_sems, recv_sems = refs[n_in], refs[n_in + 1]
        token = refs[-1]
        for k, (src, dst, to) in enumerate(sends(in_refs)):
            _remote(src, dst, send_sems, recv_sems, k, to).start()
        token[...] = jnp.zeros_like(token)

    outs = pl.pallas_call(
        body, name=name,
        out_shape=(pltpu.SemaphoreType.DMA((n_copies,)), pltpu.SemaphoreType.DMA((n_copies,)),
                   *[pltpu.HBM(b.shape, b.dtype) for b in bufs], jax.ShapeDtypeStruct((8, 128), F32)),
        in_specs=[HBM_SPEC] * nb + [HBM_ANY] * (n_in - nb),
        out_specs=(SEM_SPEC, SEM_SPEC, *[HBM_SPEC] * nb, pl.BlockSpec(memory_space=pltpu.VMEM)),
        input_output_aliases={i: 2 + i for i in range(nb)},
        compiler_params=pltpu.CompilerParams(has_side_effects=DATAFLOW),
    )(*[_in_hbm(b) for b in bufs], *([] if after is None else [after]))
    return outs[0], outs[1], list(outs[2:2 + nb]), outs[-1]


def _split_wait(name, send_sems, recv_sems, bufs, after, sends, arrivals):
    nb = len(bufs)

    def body(*refs):
        in_refs = refs[:nb]
        s_sems, r_sems = refs[nb], refs[nb + 1]
        me = (lax.axis_index("x"), lax.axis_index("y"), lax.axis_index("c"))
        for k, (src, dst, to) in enumerate(sends(in_refs)):
            _remote(src, dst, s_sems, r_sems, k, to).wait_send()
        for k, win in enumerate(arrivals(in_refs)):
            _remote(win, win, s_sems, r_sems, k, me).wait_recv()

    outs = pl.pallas_call(
        body, name=name,
        out_shape=[pltpu.HBM(b.shape, b.dtype) for b in bufs],
        in_specs=[HBM_SPEC] * nb + [SEM_SPEC, SEM_SPEC, HBM_ANY],
        out_specs=[HBM_SPEC] * nb,
        input_output_aliases={i: i for i in range(nb)},
        compiler_params=pltpu.CompilerParams(has_side_effects=DATAFLOW),
    )(*bufs, send_sems, recv_sems, after)
    return list(outs)


MIX_W = ('w_in', 'w_mem_kv', 'w_out')
FFN_W = ('w_gate_up', 'w_down')
LAYER_W = MIX_W + FFN_W


def _place_own(l, names, shards, small, sc):
    nw = len(names)
    has_small = small is not None
    n_ops = nw + (1 if has_small else 0)

    def body(sc_ref, *refs):
        for src, dst in zip(refs[:n_ops], refs[n_ops:]):
            dst[...] = src[...]

    in_specs, out_specs, out_shape, ops = [], [], [], list(shards)
    for nm in names:
        L, R, C, kind, rs, cs, rh = _geom(nm)
        in_specs.append(pl.BlockSpec((1, rs, cs), lambda i, sc_ref: (0, 0, 0)))
        if kind == 'row':
            out_specs.append(pl.BlockSpec((1, rs, cs), lambda i, sc_ref: (0, sc_ref[0], 0)))
        else:
            out_specs.append(pl.BlockSpec((1, rs, cs), lambda i, sc_ref: (0, 0, sc_ref[0])))
        out_shape.append(jax.ShapeDtypeStruct((1, R, C), BF))
    if has_small:
        in_specs.append(pl.BlockSpec((SMALL_ROWS, 256), lambda i, sc_ref: (0, 0)))
        out_specs.append(pl.BlockSpec((None, SMALL_ROWS, 256), lambda i, sc_ref: (sc_ref[0], 0, 0)))
        out_shape.append(jax.ShapeDtypeStruct((N_CHIPS, SMALL_ROWS, 256), F32))
        ops.append(small)
    return pl.pallas_call(
        body, name=f"{l}_place_own_shard",
        grid_spec=pltpu.PrefetchScalarGridSpec(num_scalar_prefetch=1, grid=(1,), in_specs=in_specs, out_specs=out_specs),
        out_shape=out_shape,
        compiler_params=_cp(("arbitrary",)),
    )(sc, *ops)


def _gather_start(l, names, shards, small, sc, after=None):
    nw = len(names)
    has_small = small is not None
    fulls = _place_own(l, names, shards, small, sc)
    bufs = list(shards) + ([small] if has_small else []) + list(fulls)
    n_src = nw + (1 if has_small else 0)

    def sends(refs):
        x, y, c, chips = _place()
        s = 2 * x + y
        out = []
        for (px, py) in chips:
            for wi, nm in enumerate(names):
                out.append((_shard_half(refs[wi], nm, c), _full_win(refs[n_src + wi], nm, s, c), (px, py, c)))
            if has_small:
                out.append((refs[nw], refs[n_src + nw].at[s], (px, py, c)))
        return out

    def arrivals(refs):
        x, y, c, chips = _place()
        out = []
        for (px, py) in chips:
            sp = 2 * px + py
            for wi, nm in enumerate(names):
                out.append(_full_win(refs[n_src + wi], nm, sp, c))
            if has_small:
                out.append(refs[n_src + nw].at[sp])
        return out

    n_copies = 3 * n_src
    send_sems, recv_sems, bufs, token = _split_start(f"{l}_gather_ici_start", bufs, n_copies, sends, after)
    return dict(l=l, names=names, has_small=has_small, sems=(send_sems, recv_sems), bufs=bufs, sends=sends,
                arrivals=arrivals, token=token)


def _gather_forward(st, after):
    l, names = st['l'], st['names']
    nw = len(names)
    n_src = nw + (1 if st['has_small'] else 0)
    bufs = _split_wait(f"{l}_gather_ici_wait", *st['sems'], st['bufs'], after, st['sends'], st['arrivals'])
    fulls = bufs[n_src:n_src + nw]
    small_all = bufs[n_src + nw] if st['has_small'] else None

    def sends(refs):
        x, y, c, chips = _place()
        out = []
        for (px, py) in chips:
            sp = 2 * px + py
            for wi, nm in enumerate(names):
                w = _full_win(refs[wi], nm, sp, c)
                out.append((w, w, (x, y, 1 - c)))
        return out

    def arrivals(refs):
        x, y, c, chips = _place()
        out = []
        for (px, py) in chips:
            sp = 2 * px + py
            for wi, nm in enumerate(names):
                out.append(_full_win(refs[wi], nm, sp, 1 - c))
        return out

    send_sems, recv_sems, fulls, token = _split_start(f"{l}_gather_d2d_start", fulls, 3 * nw, sends)
    return dict(l=l, names=names, sems=(send_sems, recv_sems), bufs=fulls, sends=sends, arrivals=arrivals,
                small_all=small_all, token=token)


def _gather_finish(st, after):
    fulls = _split_wait(f"{st['l']}_gather_d2d_wait", *st['sems'], st['bufs'], after, st['sends'], st['arrivals'])
    return dict(zip(st['names'], fulls)), st['small_all']


def _reduce_start(tag, names, grads):
    nw = len(names)
    recv = [lax.empty((1,) + _halves_shape(nm)[1:], F32) for nm in names]
    bufs = [grads[nm] for nm in names] + recv

    def windows(refs, half_of):
        x, y, c, _ = _place()
        h = half_of(c)
        out = []
        for wi, nm in enumerate(names):
            L, R, C, kind, rs, cs, rh = _geom(nm)
            if kind == 'row':
                for sp in range(N_CHIPS):
                    out.append((_full_win(refs[wi], nm, sp, h), _halves_win(refs[nw + wi], nm, sp)))
            else:
                out.append((refs[wi].at[:, pl.ds(h * rh, rh), :], refs[nw + wi]))
        return out

    def sends(refs):
        x, y, c, _ = _place()
        return [(src, dst, (x, y, 1 - c)) for src, dst in windows(refs, lambda c: 1 - c)]

    def arrivals(refs):
        return [dst for _, dst in windows(refs, lambda c: c)]

    n_copies = sum(N_CHIPS if BIG[nm][1] == 'row' else 1 for nm in names)
    send_sems, recv_sems, bufs, token = _split_start(tag + "_halves_start", bufs, n_copies, sends)
    return dict(tag=tag, names=names, sems=(send_sems, recv_sems), bufs=bufs, sends=sends, arrivals=arrivals, token=token)


def _reduce_mid(st, after, sc):
    tag, names = st['tag'], st['names']
    nw = len(names)
    bufs = _split_wait(tag + "_halves_wait", *st['sems'], st['bufs'], after, st['sends'], st['arrivals'])
    halves, own = [], []
    for wi, nm in enumerate(names):
        hb, ow = _add_halves(nm, bufs[wi], bufs[nw + wi], sc, tag)
        halves.append(hb)
        own.append(ow)
    pieces = [lax.empty((3, 1) + _half_shape(nm)[1:], BF) for nm in names]

    def sends(refs):
        x, y, c, chips = _place()
        out = []
        for j, (px, py) in enumerate(chips):
            for wi, nm in enumerate(names):
                out.append((_halves_win(refs[wi], nm, 2 * px + py), refs[nw + wi].at[j], (px, py, c)))
        return out

    def arrivals(refs):
        return [refs[nw + wi].at[j] for j in range(3) for wi in range(nw)]

    send_sems, recv_sems, bufs, token = _split_start(tag + "_pieces_start", halves + pieces, 3 * nw, sends)
    return dict(tag=tag, names=names, sems=(send_sems, recv_sems), bufs=bufs, sends=sends, arrivals=arrivals, own=own,
                token=token)


def _reduce_late(st, after, sc):
    tag, names = st['tag'], st['names']
    nw = len(names)
    bufs = _split_wait(tag + "_pieces_wait", *st['sems'], st['bufs'], after, st['sends'], st['arrivals'])
    gsh = [_sum_pieces(nm, st['own'][wi], bufs[nw + wi], sc, tag) for wi, nm in enumerate(names)]

    def sends(refs):
        x, y, c, _ = _place()
        return [(_shard_half(refs[wi], nm, c), _shard_half(refs[wi], nm, c), (x, y, 1 - c)) for wi, nm in enumerate(names)]

    def arrivals(refs):
        x, y, c, _ = _place()
        return [_shard_half(refs[wi], nm, 1 - c) for wi, nm in enumerate(names)]

    send_sems, recv_sems, bufs, token = _split_start(tag + "_share_start", gsh, nw, sends)
    return dict(tag=tag, names=names, sems=(send_sems, recv_sems), bufs=bufs, sends=sends, arrivals=arrivals, token=token)


def _reduce_finish(st, after):
    gsh = _split_wait(st['tag'] + "_share_wait", *st['sems'], st['bufs'], after, st['sends'], st['arrivals'])
    return dict(zip(st['names'], gsh))


def _add_halves(name, g, r, sc, tag):
    _, R, C, kind, rs, cs, rh = _geom(name)
    L = g.shape[0]
    tr = rh if kind == 'row' else 256
    nr = rh // tr

    def body(sc_ref, g_ref, r_ref, hb_ref, own_ref):
        sp = pl.program_id(2)
        tot = g_ref[...] + r_ref[...]
        hb_ref[...] = tot.astype(hb_ref.dtype)

        @pl.when(sp == sc_ref[0])
        def _():
            own_ref[...] = tot

    if kind == 'row':
        g_map = lambda l, ri, sp, sc_ref: (l, sp * 2 + sc_ref[1], 0)
        h_map = lambda l, ri, sp, sc_ref: (l, sp, 0)
    else:
        g_map = lambda l, ri, sp, sc_ref: (l, sc_ref[1] * nr + ri, sp)
        h_map = lambda l, ri, sp, sc_ref: (l, ri, sp)
    own_map = lambda l, ri, sp, sc_ref: (l, ri, 0)
    blk = (None, tr, cs)
    return pl.pallas_call(
        body, name=tag + "_add_halves_" + name,
        grid_spec=pltpu.PrefetchScalarGridSpec(
            num_scalar_prefetch=1, grid=(L, nr, N_CHIPS),
            in_specs=[pl.BlockSpec(blk, g_map), pl.BlockSpec(blk, h_map)],
            out_specs=[pl.BlockSpec(blk, h_map), pl.BlockSpec(blk, own_map)]),
        out_shape=[jax.ShapeDtypeStruct((L,) + _halves_shape(name)[1:], BF),
                   jax.ShapeDtypeStruct((L,) + _half_shape(name)[1:], F32)],
        compiler_params=_cp(("parallel", "parallel", "arbitrary")),
    )(sc, g, r)


def _sum_pieces(name, own, pieces, sc, tag):
    _, R, C, kind, rs, cs, rh = _geom(name)
    L = own.shape[0]
    tr = rh if kind == 'row' else 256
    nr = rh // tr

    def body(sc_ref, o_ref, p_ref, out_ref):
        out_ref[...] = o_ref[...] + p_ref[0].astype(F32) + p_ref[1].astype(F32) + p_ref[2].astype(F32)

    blk = (None, tr, cs)
    return pl.pallas_call(
        body, name=tag + "_sum_pieces_" + name,
        grid_spec=pltpu.PrefetchScalarGridSpec(
            num_scalar_prefetch=1, grid=(L, nr),
            in_specs=[pl.BlockSpec(blk, lambda l, ri, sc_ref: (l, ri, 0)),
                      pl.BlockSpec((3, None, tr, cs), lambda l, ri, sc_ref: (0, l, ri, 0))],
            out_specs=pl.BlockSpec(blk, lambda l, ri, sc_ref: (l, sc_ref[1] * nr + ri, 0))),
        out_shape=jax.ShapeDtypeStruct((L,) + _shard_shape(name)[1:], F32),
        compiler_params=_cp(("parallel", "parallel")),
    )(sc, own, pieces)


def _small_gather_start(v, sc):
    rows = v.shape[0]

    def place(sc_ref, v_ref, o_ref):
        o_ref[...] = v_ref[...]

    slots = pl.pallas_call(
        place, name="small_grads_place_own",
        grid_spec=pltpu.PrefetchScalarGridSpec(
            num_scalar_prefetch=1, grid=(1,),
            in_specs=[pl.BlockSpec((rows, 128), lambda i, sc_ref: (0, 0))],
            out_specs=pl.BlockSpec((None, rows, 128), lambda i, sc_ref: (2 * sc_ref[0] + sc_ref[1], 0, 0))),
        out_shape=jax.ShapeDtypeStruct((8, rows, 128), v.dtype),
        compiler_params=_cp(("arbitrary",)),
    )(sc, v)

    def peers():
        x, y, c, _ = _place()
        flips = [(fx, fy, fc) for fx in (0, 1) for fy in (0, 1) for fc in (0, 1)][1:]
        return [((1 - x if fx else x), (1 - y if fy else y), (1 - c if fc else c)) for fx, fy, fc in flips]

    def sends(refs):
        x, y, c, _ = _place()
        return [(refs[0], refs[1].at[4 * x + 2 * y + c], p) for p in peers()]

    def arrivals(refs):
        return [refs[1].at[4 * px + 2 * py + pc] for px, py, pc in peers()]

    send_sems, recv_sems, bufs, token = _split_start("small_grads_gather_start", [v, slots], 7, sends)
    return dict(sems=(send_sems, recv_sems), bufs=bufs, sends=sends, arrivals=arrivals, token=token)


def _small_gather_finish(st, after):
    return _split_wait("small_grads_gather_wait", *st['sems'], st['bufs'], after, st['sends'], st['arrivals'])[1]


def _sum8(v8, *, name, tr=336):
    rows = v8.shape[1]
    tr = min(tr, rows)
    assert rows % tr == 0

    def body(v_ref, o_ref):
        tot = v_ref[0].astype(F32)
        for d in range(1, 8):
            tot = tot + v_ref[d].astype(F32)
        o_ref[...] = tot

    return pl.pallas_call(
        body, name=name, grid=(rows // tr,),
        in_specs=[pl.BlockSpec((8, tr, 128), lambda i: (0, i, 0))], out_specs=pl.BlockSpec((tr, 128), lambda i: (i, 0)),
        out_shape=jax.ShapeDtypeStruct((rows, 128), F32),
        compiler_params=_cp(("parallel",)),
    )(v8)


def _block_diag(w_pool_l):
    wbd = jnp.zeros((MAIN_W, MAIN_W), F32)
    for gi in range(len(POOL_WINDOWS)):
        wbd = lax.dynamic_update_slice(wbd, w_pool_l[gi], (gi * POOL_GROUP, gi * POOL_GROUP))
    return wbd.astype(BF)


def _unpack_small(small_all):
    ng = small_all[:, :16, :].reshape(N_CHIPS, DEPTH, 4, 256).transpose(1, 2, 0, 3).reshape(DEPTH, 4, D_MODEL)
    ps = small_all[:, 16:18, :POOL_GROUP].transpose(1, 0, 2).reshape(N_A, MAIN_W)
    return ng, ps


def _local_step(x, mem, positions, on_forward, on_backward, mem_norm, w_pool, kv_norm, target):
    B, S, _ = x.shape
    T = B * S
    xc = x.reshape(T, D_MODEL)
    memf = mem.reshape(B * N_MEM, D_MODEL)
    tgt = target.reshape(T, D_MODEL)
    cos, sin = _rope_tables(positions.reshape(T, 1), name="rope_tables")
    wbd = [_block_diag(w_pool[l]) for l in range(N_A)]
    nbo = D_FF // 256
    fw = []
    rk = rv = None
    kv_saved = None
    wts = []
    norm_gains = pool_scale = y2 = None

    def tied(vec, tok):
        return vec if tok is None else vec + tok

    for l in range(DEPTH):
        t = f"l{l}_"
        got = on_forward('start', l, y2)
        wts.append(dict(got[0]))
        if l == 0:
            norm_gains, pool_scale = _unpack_small(got[1])
        sv = {'x_in': xc}
        h0, sv['r0'] = _norm_fwd(xc, tied(norm_gains[l, 0], got[2]), name=t + "norm0", out_dtype=BF, tm=1024)
        z, = _mm(h0, wts[l]['w_in'], b_layer=0, name=t + "mm_in", tm=1024, tn=1024)
        memn, sv['rm'] = _norm_fwd(memf, mem_norm[l], name=t + "norm_mem", out_dtype=BF, tm=256)
        kvm, = _mm(memn, wts[l]['w_mem_kv'], b_layer=0, name=t + "mm_memkv", out_dtypes=(BF,))
        if l < N_A:
            ycat, sv['p'] = _pool_fwd(z, wbd[l], pool_scale[l], B, S, name=t + "pool_fwd")
        else:
            rq = _rope_apply(z, cos, sin, name=t + "rope_q", out_dtype=F32)
            o = lax.empty((T, MAIN_W), F32)
            lse = lax.empty((T, MAIN_W), F32)
            for g in range(3):
                o, lse = _dil_fwd(g, rq, rk, rv, o, lse, B, S, name=t + f"dil_fwd{g}")
            ycat = _dil_combine_fwd(o, lse, lax.empty((T, D_MODEL), BF), name=t + "dil_combine")
            sv.update(rq=rq, o=o, lse=lse)
        ycat, sv['lse_m'] = _memattn_fwd(z, kvm, ycat, B, S, name=t + "memattn_fwd")
        tok = on_forward('mid', l, ycat)
        y1, = _mm(ycat, wts[l]['w_out'], b_layer=0, name=t + "mm_out", tm=1024, tn=1024)
        wts[l].update(on_forward('ffn', l, y1)[0])
        x1, sv['r1'] = _norm_fwd(y1, tied(norm_gains[l, 1], tok), name=t + "norm1", res=xc)
        h2, sv['r2'] = _norm_fwd(x1, norm_gains[l, 2], name=t + "norm2", out_dtype=BF, tm=1024)
        gg, uu, aa = _mm(h2, wts[l]['w_gate_up'], b_layer=0, b_offsets=(0, nbo), out_n=D_FF, tm=4096, tn=256, name=t + "mm_gate_up",
                         epilogue=_swiglu_fwd_epilogue, out_dtypes=(BF, BF, BF))
        on_forward('post', l, gg)
        y2, = _mm(aa, wts[l]['w_down'], b_layer=0, tk=D_FF, name=t + "mm_down")
        x2, sv['r3'] = _norm_fwd(y2, norm_gains[l, 3], name=t + "norm3", res=x1)
        sv.update(h0=h0, z=z, memn=memn, kvm=kvm, ycat=ycat, y1=y1, x1=x1, h2=h2, gg=gg, uu=uu, aa=aa, y2=y2)
        fw.append(sv)
        xc = x2
        if l == N_A - 1:
            kvn, rkv = _norm_fwd(xc, kv_norm, name="norm_kv", out_dtype=BF)
            kv, = _mm(kvn, wts[N_A - 1]['w_kv'], b_layer=0, name="mm_kv")
            rk, rv = _rope_apply(kv, cos, sin, name="rope_k", passthrough=True, out_dtype=F32)
            kv_saved = (xc, kvn, rkv)

    loss, dx = _loss(xc, tgt, name="loss")

    d_ng = [[None] * 4 for _ in range(DEPTH)]
    d_memnorm = [None] * DEPTH
    d_wbd = [None] * N_A
    d_pscale = [None] * N_A
    d_kvnorm = None
    kv_parts = []
    tok = None

    def as3d(gl):
        return {nm: g.reshape((1,) + g.shape) for nm, g in gl.items()}

    for l in reversed(range(DEPTH)):
        t = f"l{l}_b_"
        sv = fw[l]
        gl = {}
        dy2, d_ng[l][3] = _norm_bwd(dx, sv['y2'], sv['r3'], tied(norm_gains[l, 3], tok), name=t + "norm3", out_dtype=BF, tm=1024)
        gl['w_down'], = _mm(sv['aa'], dy2, ta=True, tm=1408, tn=512, tk=4096, name=t + "dw_down")
        dg, du = _mm(dy2, wts[l]['w_down'], tb=True, b_layer=0, tm=1024, tn=1408, name=t + "d_act",
                     extras=((sv['gg'], 'tile'), (sv['uu'], 'tile')), epilogue=_swiglu_bwd_epilogue, out_dtypes=(BF, BF))
        gl['w_gate_up'], = _mm(sv['h2'], (dg, du), ta=True, tn=1408, tk=1024, name=t + "dw_gate_up")
        dh2, = _mm((dg, du), wts[l]['w_gate_up'], tb=True, b_layer=0, tn=1024, tk=1408, name=t + "d_h2", out_dtypes=(BF,))
        dx1, d_ng[l][2] = _norm_bwd(dh2, sv['x1'], sv['r2'], norm_gains[l, 2], name=t + "norm2", add=dx, tm=1024)
        tok = on_backward('ffn', l, dx1, as3d(gl))
        dy1, d_ng[l][1] = _norm_bwd(dx1, sv['y1'], sv['r1'], tied(norm_gains[l, 1], tok), name=t + "norm1", out_dtype=BF, tm=1024)
        gl['w_out'], = _mm(sv['ycat'], dy1, ta=True, name=t + "dw_out", tk=4096)
        dycat, = _mm(dy1, wts[l]['w_out'], tb=True, b_layer=0, name=t + "d_ycat", tm=1024, tn=1024)
        dz = lax.empty((T, D_MODEL), BF)
        dz, dkm, dvm = _memattn_bwd(dycat, sv['z'], sv['kvm'], sv['lse_m'], dz, B, S, name=t + "memattn")
        if l < N_A:
            dz, d_wbd[l], d_pscale[l] = _pool_bwd(dycat, sv['p'], wbd[l], pool_scale[l], dz, B, S, name=t + "pool")
        else:
            do, cb = _dil_combine_bwd(dycat, sv['o'], sv['lse'], name=t + "dil_combine")
            acc = tuple(lax.empty((T, MAIN_W), F32) for _ in range(3))
            for g in range(3):
                acc = _dil_bwd(g, sv['rq'], rk, rv, do, cb, sv['lse'], acc, B, S, name=t + f"dil{g}")
            dz = _rope_apply(acc[0], cos, sin, name=t + "rope_q", sign=-1.0, alias=dz)
            kv_parts.append(acc[1:])
        tok = on_backward('mix', l, dz, as3d(gl))
        gl['w_in'], = _mm(sv['h0'], dz, ta=True, name=t + "dw_in", tk=4096)
        dh0, = _mm(dz, wts[l]['w_in'], tb=True, b_layer=0, name=t + "d_h0", out_dtypes=(BF,), tm=1024, tn=1024)
        dx, d_ng[l][0] = _norm_bwd(dh0, sv['x_in'], sv['r0'], tied(norm_gains[l, 0], tok), name=t + "norm0", add=dx1, tm=1024)
        gl['w_mem_kv'], = _mm(sv['memn'], (dkm, dvm), ta=True, tn=256, name=t + "dw_memkv")
        dmemn, = _mm((dkm, dvm), wts[l]['w_mem_kv'], tb=True, b_layer=0, tk=256, name=t + "d_memn", out_dtypes=(BF,))
        _, d_memnorm[l] = _norm_bwd(dmemn, memf, sv['rm'], mem_norm[l], name=t + "norm_mem", out_dtype=BF, tm=256)
        if l == N_A:
            dk, dv = _kv_grad_sum(kv_parts, cos, sin, name="kv_grad")
            x_kv, kvn, rkv = kv_saved
            gl['w_kv'], = _mm(kvn, (dk, dv), ta=True, tn=768, tk=2048, name="dw_kv")
            dkvn, = _mm((dk, dv), wts[N_A - 1]['w_kv'], tb=True, b_layer=0, tn=1024, tk=768, name="d_kvn", out_dtypes=(BF,))
            dx, d_kvnorm = _norm_bwd(dkvn, x_kv, rkv, kv_norm, name="norm_kv_b", add=dx)
        tok = on_backward('end', l, dx, as3d(gl))

    small = {
        'norm_gains': jnp.stack([jnp.concatenate(d_ng[l], axis=0) for l in range(DEPTH)]),
        'mem_norm': jnp.concatenate(d_memnorm, axis=0),
        'kv_norm': d_kvnorm.reshape(D_MODEL),
        'pool_scale': jnp.concatenate(d_pscale, axis=0),
        'w_pool': jnp.stack([jnp.stack([d_wbd[l][gi * POOL_GROUP:(gi + 1) * POOL_GROUP, gi * POOL_GROUP:(gi + 1) * POOL_GROUP]
                                        for gi in range(len(POOL_WINDOWS))]) for l in range(N_A)]),
    }
    return loss, dx, small


SMALL_ORDER = ('norm_gains', 'mem_norm', 'kv_norm', 'pool_scale', 'w_pool')
SMALL_VEC_ROWS = 2560


def kernel(x, mem, positions, norm_gains, mem_norm, w_in, w_mem_kv, w_out, w_pool, pool_scale, kv_norm, w_kv, w_gate_up, w_down, loss_target, m_norm_gains, m_mem_norm, m_w_in, m_w_mem_kv, m_w_out, m_w_pool, m_pool_scale, m_kv_norm, m_w_kv, m_w_gate_up, m_w_down, v_norm_gains, v_mem_norm, v_w_in, v_w_mem_kv, v_w_out, v_w_pool, v_pool_scale, v_kv_norm, v_w_kv, v_w_gate_up, v_w_down):
    xi, yi, ci = lax.axis_index("x"), lax.axis_index("y"), lax.axis_index("c")
    s = 2 * xi + yi
    sc = jnp.stack([s, ci]).astype(jnp.int32)
    weights = dict(norm_gains=norm_gains, mem_norm=mem_norm, w_in=w_in, w_mem_kv=w_mem_kv, w_out=w_out, w_pool=w_pool,
                   pool_scale=pool_scale, kv_norm=kv_norm, w_kv=w_kv, w_gate_up=w_gate_up, w_down=w_down)
    moms = dict(norm_gains=m_norm_gains, mem_norm=m_mem_norm, w_in=m_w_in, w_mem_kv=m_w_mem_kv, w_out=m_w_out,
                w_pool=m_w_pool, pool_scale=m_pool_scale, kv_norm=m_kv_norm, w_kv=m_w_kv, w_gate_up=m_w_gate_up,
                w_down=m_w_down)
    vels = dict(norm_gains=v_norm_gains, mem_norm=v_mem_norm, w_in=v_w_in, w_mem_kv=v_w_mem_kv, w_out=v_w_out,
                w_pool=v_w_pool, pool_scale=v_pool_scale, kv_norm=v_kv_norm, w_kv=v_w_kv, w_gate_up=v_w_gate_up,
                w_down=v_w_down)

    small_w = jnp.zeros((SMALL_ROWS, 256), F32)
    small_w = lax.dynamic_update_slice(small_w, norm_gains.reshape(16, 256), (0, 0))
    small_w = lax.dynamic_update_slice(small_w, pool_scale, (16, 0))
    def shard_of(nm, l):
        return w_kv.astype(BF).reshape(_shard_shape('w_kv')) if nm == 'w_kv' else weights[nm][l:l + 1].astype(BF)

    groups = {'l0a': (0, MIX_W), 'l0b': (0, FFN_W)}
    groups.update({f"l{l}": (l, LAYER_W + (('w_kv',) if l == N_A - 1 else ())) for l in range(1, DEPTH)})
    on_ici, on_d2d, gathered = {}, {}, {}

    def start_group(tag, after):
        l, names = groups[tag]
        on_ici[tag] = _gather_start(tag, names, [shard_of(nm, l) for nm in names], small_w if tag == 'l0a' else None, sc,
                                    after)
        return on_ici[tag]['token'][0, 0]

    def on_forward(where, l, after):
        if where == 'start':
            if l == 0:
                start_group('l0a', None)
                st = on_ici.pop('l0a')
                fwd = _gather_forward(st, st['token'])
                w, small_all = _gather_finish(fwd, fwd['token'])
                return w, small_all, start_group('l0b', w['w_in'])
            gathered[l] = _gather_finish(on_d2d.pop(f"l{l}"), after)[0]
            tok = start_group(f"l{l + 1}", gathered[l]['w_in']) if l + 1 < DEPTH else None
            return {nm: w for nm, w in gathered[l].items() if nm not in FFN_W}, None, tok
        if where == 'mid' and l == 0:
            on_d2d['l0b'] = _gather_forward(on_ici.pop('l0b'), after)
            return start_group('l1', on_d2d['l0b']['token'])
        if where == 'ffn':
            if l == 0:
                return (_gather_finish(on_d2d.pop('l0b'), after)[0],)
            return ({nm: gathered[l][nm] for nm in FFN_W},)
        if where == 'post' and l + 1 < DEPTH:
            on_d2d[f"l{l + 1}"] = _gather_forward(on_ici.pop(f"l{l + 1}"), after)
        return None

    hook_of = {'ffn': 0, 'mix': 1, 'end': 2}
    active, reduced = [], {l: {} for l in range(DEPTH)}
    advance = {'mid': lambda st, after: _reduce_mid(st, after, sc), 'late': lambda st, after: _reduce_late(st, after, sc)}

    def run_hook(idx, after):
        toks = []
        for grp in list(active):
            while grp['plan'] and grp['plan'][0][1] <= idx:
                step = grp['plan'].pop(0)[0]
                if step == 'finish':
                    reduced[grp['layer']].update(_reduce_finish(grp['st'], after))
                    active.remove(grp)
                else:
                    grp['st'] = advance[step](grp['st'], after)
                    toks.append(grp['st']['token'][0, 0])
        return toks

    def on_backward(where, l, after, grads):
        idx = 3 * (DEPTH - 1 - l) + hook_of[where]
        toks = run_hook(idx, after)
        if where == 'end' or (where == 'ffn' and l == 0):
            names = FFN_W if where == 'ffn' else tuple(nm for nm in grads if l > 0 or nm not in FFN_W)
            st = _reduce_start(f"l{l}_{where}_grads", names, {nm: grads[nm] for nm in names})
            plan = [('mid', idx + 1), ('late', idx + 3), ('finish', idx + 4)] if where == 'ffn' else \
                   [('mid', idx + 1), ('late', idx + 2), ('finish', idx + 3)]
            active.append(dict(layer=l, st=st, plan=plan))
            toks.append(st['token'][0, 0])
        return sum(toks) if toks else None

    loss, gx, gsmall = _local_step(x, mem, positions, on_forward, on_backward, mem_norm, w_pool, kv_norm, loss_target)
    loss = lax.psum(loss[0, 0], ("x", "y", "c"))

    vec = jnp.concatenate([gsmall[nm].reshape(-1) for nm in SMALL_ORDER])
    vec = jnp.pad(vec, (0, SMALL_VEC_ROWS * 128 - vec.shape[0])).reshape(SMALL_VEC_ROWS, 128)
    vec = vec + sum(grp['st']['token'][0, 0] for grp in active)
    small_st = _small_gather_start(vec.astype(BF), sc)
    outs = {nm: None for nm in LAYER_W}

    def adamw_layers(layers, names, after):
        for l in layers:
            for nm in names:
                outs[nm] = _adamw_layer(nm, l, weights[nm], reduced[l][nm], moms[nm], vels[nm], outs[nm], after)
                after = outs[nm][0]
        return after

    def zero_of(toks, st):
        return jnp.full((8, 128), sum(toks)) if toks else st['token']

    last = 3 * DEPTH
    toks = run_hook(last, small_st['token'])
    done = adamw_layers(range(DEPTH - 1, 0, -1), LAYER_W, zero_of(toks, small_st))
    toks = run_hook(last + 1, done)
    done = adamw_layers([0], FFN_W, zero_of(toks, small_st))
    tot = _sum8(_small_gather_finish(small_st, done), name="sum_small_grads", tr=512)
    run_hook(last + 2, tot)
    assert not active
    adamw_layers([0], MIX_W, None)
    tot = tot.reshape(-1)
    grads, off = {}, 0
    for nm in SMALL_ORDER:
        shape = (DEPTH, 4, D_MODEL) if nm == 'norm_gains' else (N_A, MAIN_W) if nm == 'pool_scale' else weights[nm].shape
        n = 1
        for dim in shape:
            n *= dim
        grads[nm] = tot[off:off + n].reshape(shape)
        off += n
    grads['norm_gains'] = lax.dynamic_slice(grads['norm_gains'], (0, 0, s * 256), (DEPTH, 4, 256))
    grads['pool_scale'] = lax.dynamic_slice(grads['pool_scale'], (0, s * POOL_GROUP), (N_A, POOL_GROUP))
    grads['w_kv'] = reduced[N_A]['w_kv'].reshape(w_kv.shape)

    order = ('norm_gains', 'mem_norm', 'w_in', 'w_mem_kv', 'w_out', 'w_pool', 'pool_scale', 'kv_norm', 'w_kv',
             'w_gate_up', 'w_down')
    deltas, new_m, new_v = {}, {}, {}
    for nm in order:
        if nm in LAYER_W:
            deltas[nm], new_m[nm], new_v[nm], grads[nm] = outs[nm]
        else:
            deltas[nm], new_m[nm], new_v[nm] = _adamw(weights[nm], grads[nm], moms[nm], vels[nm], name="adamw_" + nm)
    return (loss, gx.reshape(x.shape), *[grads[nm] for nm in order], *[deltas[nm] for nm in order],
            *[new_m[nm] for nm in order], *[new_v[nm] for nm in order])
```

```python
import functools

import jax
import jax.numpy as jnp
from jax import lax
from jax.experimental import pallas as pl
from jax.experimental.pallas import tpu as pltpu

F32 = jnp.float32
BF = jnp.bfloat16

D_MODEL = 1024
DEPTH = 4
N_A = 2
HEAD_DIM = 64
MEM_W = 256
MAIN_W = 768
D_FF = 2816
N_MEM = 256
POOL_WINDOWS = (2, 4, 8, 16)
POOL_GROUP = 192
DIL = (1, 4, 16)
STEPS = 128
ROPE_THETA = 10000.0
EPS = 1e-6
SCALE = HEAD_DIM ** -0.5
NEG = -1e30

ADAM_LR = 0.001
ADAM_B1 = 0.9
ADAM_B2 = 0.999
ADAM_EPS = 1e-08
ADAM_WD = 0.01
ADAM_STEP = 10

VMEM_LIMIT = 48 * 1024 * 1024
MESH = pl.DeviceIdType.MESH


def _cp(sem):
    return pltpu.CompilerParams(dimension_semantics=sem, vmem_limit_bytes=VMEM_LIMIT)


def _mm(a, b, *, name, ta=False, tb=False, tm=1024, tn=512, tk=1024, b_layer=None, b_offsets=(0,),
        extras=(), epilogue=None, out_dtypes=(F32,), out_n=None, stack=None):
    a_pair = isinstance(a, (tuple, list))
    b_pair = isinstance(b, (tuple, list))
    a0 = a[0] if a_pair else a
    b0 = b[0] if b_pair else b
    a_rows, a_cols = a0.shape
    if a_pair:
        a_cols *= 2
    b_rows, b_cols = b0.shape[-2:]
    if b_pair:
        b_cols *= 2
    M, K = (a_cols, a_rows) if ta else (a_rows, a_cols)
    N = b_rows if tb else b_cols
    if out_n is not None:
        N = out_n
    tm, tn, tk = min(tm, M), min(tn, N), min(tk, K)
    assert M % tm == 0 and N % tn == 0 and K % tk == 0, (name, M, N, K, tm, tn, tk)
    nk = K // tk
    n_acc = len(b_offsets)

    if a_pair:
        a_half = (a0.shape[1] // (tm if ta else tk))
    if b_pair:
        b_half = (b0.shape[1] // (tk if tb else tn))

    def a_map(sel):
        def f(i, j, k):
            r, c = (k, i) if ta else (i, k)
            if a_pair:
                c = jnp.clip(c - sel * a_half, 0, a_half - 1)
            return (r, c)
        return f

    def b_map(sel, off):
        def f(i, j, k):
            r, c = (j + off, k) if tb else (k, j + off)
            if b_pair:
                c = jnp.clip(c - sel * b_half, 0, b_half - 1)
            if b_layer is not None:
                return (b_layer, r, c)
            return (r, c)
        return f

    a_blk = (tk, tm) if ta else (tm, tk)
    b_blk = (tn, tk) if tb else (tk, tn)
    if b_layer is not None:
        b_blk = (None,) + b_blk
    in_specs, operands = [], []
    for sel in range(2 if a_pair else 1):
        in_specs.append(pl.BlockSpec(a_blk, a_map(sel)))
        operands.append(a[sel] if a_pair else a)
    n_a = len(operands)
    for off in b_offsets:
        for sel in range(2 if b_pair else 1):
            in_specs.append(pl.BlockSpec(b_blk, b_map(sel, off)))
            operands.append(b[sel] if b_pair else b)
    n_b = len(operands) - n_a
    for arr, kind in extras:
        if kind == 'tile':
            in_specs.append(pl.BlockSpec((tm, tn), lambda i, j, k: (i, j)))
        elif kind == 'row':
            in_specs.append(pl.BlockSpec((tm, 1), lambda i, j, k: (i, 0)))
        else:
            in_specs.append(pl.BlockSpec((1, tn), lambda i, j, k: (0, j)))
        operands.append(arr)
    n_e = len(extras)
    n_o = len(out_dtypes)
    dims = (((0,) if ta else (1,), (1,) if tb else (0,)), ((), ()))

    def body(*refs):
        a_refs = refs[:n_a]
        b_refs = refs[n_a:n_a + n_b]
        e_refs = refs[n_a + n_b:n_a + n_b + n_e]
        n_in = n_a + n_b + n_e + (1 if stack is not None else 0)
        o_refs = refs[n_in:n_in + n_o]
        acc_refs = refs[n_in + n_o:]
        i, j, k = pl.program_id(0), pl.program_id(1), pl.program_id(2)
        if a_pair:
            cidx = i if ta else k
            av = jnp.where(cidx < a_half, a_refs[0][...], a_refs[1][...])
        else:
            av = a_refs[0][...]
        av = av.astype(BF)
        prods = []
        for q in range(n_acc):
            if b_pair:
                cidx = (k if tb else j) + b_offsets[q]
                bv = jnp.where(cidx < b_half, b_refs[2 * q][...], b_refs[2 * q + 1][...])
            else:
                bv = b_refs[q][...]
            prods.append(lax.dot_general(av, bv.astype(BF), dims, preferred_element_type=F32))

        def finish(accs):
            outs = epilogue(accs, *[r[...] for r in e_refs]) if epilogue is not None else accs
            for o_ref, o in zip(o_refs, outs):
                o_ref[...] = o.astype(o_ref.dtype)

        if nk == 1:
            finish(prods)
        else:
            @pl.when(k == 0)
            def _():
                for r, p in zip(acc_refs, prods):
                    r[...] = p

            @pl.when(k > 0)
            def _():
                for r, p in zip(acc_refs, prods):
                    r[...] += p

            @pl.when(k == nk - 1)
            def _():
                finish([r[...] for r in acc_refs])

    if stack is not None:
        buf, layer = stack
        assert n_o == 1 and buf.shape[1:] == (M, N)
        return pl.pallas_call(
            body, name=name,
            grid=(M // tm, N // tn, nk),
            in_specs=in_specs + [pl.BlockSpec(memory_space=pl.ANY)],
            out_specs=[pl.BlockSpec((None, tm, tn), lambda i, j, k: (layer, i, j))],
            out_shape=[jax.ShapeDtypeStruct(buf.shape, buf.dtype)],
            scratch_shapes=[pltpu.VMEM((tm, tn), F32) for _ in range(n_acc if nk > 1 else 0)],
            input_output_aliases={len(operands): 0},
            compiler_params=_cp(("parallel", "parallel", "arbitrary")),
        )(*operands, buf)[0]
    return pl.pallas_call(
        body, name=name,
        grid=(M // tm, N // tn, nk),
        in_specs=in_specs,
        out_specs=[pl.BlockSpec((tm, tn), lambda i, j, k: (i, j)) for _ in range(n_o)],
        out_shape=[jax.ShapeDtypeStruct((M, N), dt) for dt in out_dtypes],
        scratch_shapes=[pltpu.VMEM((tm, tn), F32) for _ in range(n_acc if nk > 1 else 0)],
        compiler_params=_cp(("parallel", "parallel", "arbitrary")),
    )(*operands)


def _norm_fwd(x, g, *, name, res=None, out_dtype=F32, tm=512):
    T, Dm = x.shape
    has_res = res is not None

    def body(*refs):
        if has_res:
            x_ref, g_ref, r_ref, y_ref, s_ref = refs
        else:
            x_ref, g_ref, y_ref, s_ref = refs
        xv = x_ref[...]
        rstd = lax.rsqrt(jnp.mean(xv * xv, axis=-1, keepdims=True) + EPS)
        y = xv * rstd * g_ref[...]
        if has_res:
            y = r_ref[...] + y
        y_ref[...] = y.astype(y_ref.dtype)
        s_ref[...] = rstd

    row = pl.BlockSpec((tm, Dm), lambda i: (i, 0))
    in_specs = [row, pl.BlockSpec((1, Dm), lambda i: (0, 0))] + ([row] if has_res else [])
    ops = [x, g.reshape(1, Dm)] + ([res] if has_res else [])
    return pl.pallas_call(
        body, name=name, grid=(T // tm,), in_specs=in_specs,
        out_specs=[row, pl.BlockSpec((tm, 1), lambda i: (i, 0))],
        out_shape=[jax.ShapeDtypeStruct((T, Dm), out_dtype), jax.ShapeDtypeStruct((T, 1), F32)],
        compiler_params=_cp(("parallel",)),
    )(*ops)


def _norm_bwd(dout, x, rstd, g, *, name, add=None, out_dtype=F32, tm=512):
    T, Dm = x.shape
    has_add = add is not None
    nt = T // tm

    def body(*refs):
        if has_add:
            do_ref, x_ref, s_ref, g_ref, a_ref, dx_ref, dg_ref, acc = refs
        else:
            do_ref, x_ref, s_ref, g_ref, dx_ref, dg_ref, acc = refs
        i = pl.program_id(0)
        do = do_ref[...].astype(F32)
        xh = x_ref[...] * s_ref[...]
        gd = do * g_ref[...]
        dx = s_ref[...] * (gd - xh * jnp.mean(gd * xh, axis=-1, keepdims=True))
        if has_add:
            dx = dx + a_ref[...].astype(F32)
        dx_ref[...] = dx.astype(dx_ref.dtype)
        part = jnp.sum((do * xh).reshape(tm // 8, 8, Dm), axis=0)

        @pl.when(i == 0)
        def _():
            acc[...] = part

        @pl.when(i > 0)
        def _():
            acc[...] += part

        @pl.when(i == nt - 1)
        def _():
            dg_ref[...] = jnp.sum(acc[...], axis=0, keepdims=True)

    row = pl.BlockSpec((tm, Dm), lambda i: (i, 0))
    in_specs = [row, row, pl.BlockSpec((tm, 1), lambda i: (i, 0)), pl.BlockSpec((1, Dm), lambda i: (0, 0))]
    ops = [dout, x, rstd, g.reshape(1, Dm)]
    if has_add:
        in_specs.append(row)
        ops.append(add)
    return pl.pallas_call(
        body, name=name, grid=(nt,), in_specs=in_specs,
        out_specs=[row, pl.BlockSpec((1, Dm), lambda i: (0, 0))],
        out_shape=[jax.ShapeDtypeStruct((T, Dm), out_dtype), jax.ShapeDtypeStruct((1, Dm), F32)],
        scratch_shapes=[pltpu.VMEM((8, Dm), F32)],
        compiler_params=_cp(("arbitrary",)),
    )(*ops)


def _swiglu_fwd_epilogue(accs):
    g, u = accs
    return g, u, g * jax.nn.sigmoid(g) * u


def _swiglu_bwd_epilogue(accs, g, u):
    da = accs[0]
    g = g.astype(F32)
    u = u.astype(F32)
    sig = jax.nn.sigmoid(g)
    return da * u * (sig * (1.0 + g * (1.0 - sig))), da * (g * sig)


def _rope_tables(pos, *, name, tm=1024):
    T = pos.shape[0]
    half = HEAD_DIM // 2
    freqs = ROPE_THETA ** (-jnp.arange(half, dtype=F32) / half)
    freqs = jnp.tile(freqs, 4).reshape(1, 128)

    def body(p_ref, f_ref, c_ref, s_ref):
        ang = p_ref[...].astype(F32) * f_ref[...]
        lane = lax.broadcasted_iota(jnp.int32, ang.shape, 1)
        c_ref[...] = jnp.cos(ang)
        s_ref[...] = jnp.where(lane % HEAD_DIM < half, -1.0, 1.0) * jnp.sin(ang)

    tab = pl.BlockSpec((tm, 128), lambda i: (i, 0))
    return pl.pallas_call(
        body, name=name, grid=(T // tm,),
        in_specs=[pl.BlockSpec((tm, 1), lambda i: (i, 0)), pl.BlockSpec((1, 128), lambda i: (0, 0))],
        out_specs=[tab, tab],
        out_shape=[jax.ShapeDtypeStruct((T, 128), F32)] * 2,
        compiler_params=_cp(("parallel",)),
    )(pos, freqs)


def _rot(x, cos, sin, sign):
    W = x.shape[1]
    half = HEAD_DIM // 2
    reps = W // 128
    c = jnp.concatenate([cos] * reps, axis=1) if reps > 1 else cos
    s = jnp.concatenate([sin] * reps, axis=1) if reps > 1 else sin
    lane = lax.broadcasted_iota(jnp.int32, x.shape, 1)
    swapped = jnp.where(lane % HEAD_DIM < half, pltpu.roll(x, W - half, axis=1), pltpu.roll(x, half, axis=1))
    return x * c + (sign * s) * swapped


def _rope_apply(x, cos, sin, *, name, sign=1.0, width=MAIN_W, passthrough=False, out_dtype=BF, alias=None,
                out_cols=None, tm=512):
    T = x.shape[0]

    def body(*refs):
        if passthrough:
            x_ref, v_ref, c_ref, s_ref, o_ref, ov_ref = refs
            ov_ref[...] = v_ref[...].astype(ov_ref.dtype)
        elif alias is not None:
            x_ref, c_ref, s_ref, _, o_ref = refs
        else:
            x_ref, c_ref, s_ref, o_ref = refs
        o_ref[...] = _rot(x_ref[...].astype(F32), c_ref[...], s_ref[...], sign).astype(o_ref.dtype)

    blk0 = pl.BlockSpec((tm, width), lambda i: (i, 0))
    blk1 = pl.BlockSpec((tm, width), lambda i: (i, 1))
    tab = pl.BlockSpec((tm, 128), lambda i: (i, 0))
    if passthrough:
        return pl.pallas_call(
            body, name=name, grid=(T // tm,), in_specs=[blk0, blk1, tab, tab], out_specs=[blk0, blk0],
            out_shape=[jax.ShapeDtypeStruct((T, width), out_dtype)] * 2,
            compiler_params=_cp(("parallel",)),
        )(x, x, cos, sin)
    if alias is not None:
        return pl.pallas_call(
            body, name=name, grid=(T // tm,),
            in_specs=[blk0, tab, tab, pl.BlockSpec(memory_space=pl.ANY)], out_specs=blk0,
            out_shape=jax.ShapeDtypeStruct(alias.shape, alias.dtype),
            input_output_aliases={3: 0},
            compiler_params=_cp(("parallel",)),
        )(x, cos, sin, alias)
    return pl.pallas_call(
        body, name=name, grid=(T // tm,), in_specs=[blk0, tab, tab], out_specs=blk0,
        out_shape=jax.ShapeDtypeStruct((T, width), out_dtype),
        compiler_params=_cp(("parallel",)),
    )(x, cos, sin)


POOL_T = 256
POOL_HALO = 16


def _pool_lane_window(shape):
    lane = lax.broadcasted_iota(jnp.int32, shape, 1)
    w = jnp.full(shape, POOL_WINDOWS[0], jnp.int32)
    for gi in range(1, len(POOL_WINDOWS)):
        w = jnp.where(lane >= gi * POOL_GROUP, POOL_WINDOWS[gi], w)
    return w


POOL_PAD = 32
POOL_R = POOL_T + POOL_PAD


def _pool_window_sums(buf, tmp_a, tmp_b, win, back):
    src, dst, acc = buf, tmp_a, None
    for j, (w, sh) in enumerate(zip(POOL_WINDOWS, (1, 2, 4, 8)), start=1):
        n = POOL_R - 8 * j
        if back:
            dst[pl.ds(8 * j, n), :] = src[pl.ds(8 * j, n), :] + src[pl.ds(8 * j - sh, n), :]
            cur = dst[pl.ds(POOL_PAD, POOL_T), :]
        else:
            dst[pl.ds(0, n), :] = src[pl.ds(0, n), :] + src[pl.ds(sh, n), :]
            cur = dst[pl.ds(0, POOL_T), :]
        acc = cur if acc is None else jnp.where(win >= w, cur, acc)
        src, dst = dst, (tmp_b if dst is tmp_a else tmp_a)
    return acc


def _pool_fwd(z, wbd, scale, B, S, *, name):
    T = z.shape[0]
    nt = S // POOL_T
    hb = POOL_T // POOL_PAD

    def body(z_ref, h_ref, w_ref, sc_ref, y_ref, p_ref, ext, tmp_a, tmp_b):
        i = pl.program_id(1)
        u = z_ref[...]
        ext[pl.ds(POOL_PAD, POOL_T), :] = u
        ext[pl.ds(0, POOL_PAD), :] = jnp.where(i > 0, h_ref[...], 0.0)
        win = _pool_lane_window((POOL_T, MAIN_W))
        acc = _pool_window_sums(ext, tmp_a, tmp_b, win, True)
        t = i * POOL_T + lax.broadcasted_iota(jnp.int32, (POOL_T, MAIN_W), 0)
        cnt = jnp.minimum(t + 1, win).astype(F32)
        p = (acc / cnt - u).astype(BF)
        p_ref[...] = p
        y = jnp.dot(p, w_ref[...], preferred_element_type=F32) * sc_ref[...]
        y_ref[...] = y.astype(y_ref.dtype)

    return pl.pallas_call(
        body, name=name, grid=(B, nt),
        in_specs=[pl.BlockSpec((POOL_T, MAIN_W), lambda b, i: (b * nt + i, 0)),
                  pl.BlockSpec((POOL_PAD, MAIN_W), lambda b, i: (jnp.maximum((b * nt + i) * hb - 1, 0), 0)),
                  pl.BlockSpec((MAIN_W, MAIN_W), lambda b, i: (0, 0)),
                  pl.BlockSpec((1, MAIN_W), lambda b, i: (0, 0))],
        out_specs=[pl.BlockSpec((POOL_T, MAIN_W), lambda b, i: (b * nt + i, 0)),
                   pl.BlockSpec((POOL_T, MAIN_W), lambda b, i: (b * nt + i, 0))],
        out_shape=[jax.ShapeDtypeStruct((T, D_MODEL), BF), jax.ShapeDtypeStruct((T, MAIN_W), BF)],
        scratch_shapes=[pltpu.VMEM((POOL_R, MAIN_W), F32)] * 3,
        compiler_params=_cp(("parallel", "parallel")),
    )(z, z, wbd, scale.reshape(1, MAIN_W))


def _pool_bwd(dy, p, wbd, scale, dz_alias, B, S, *, name):
    T = dy.shape[0]
    nt = S // POOL_T
    hb = POOL_T // POOL_HALO
    last_halo = T // POOL_HALO - 1

    def body(dy_ref, dyn_ref, p_ref, pn_ref, w_ref, sc_ref, _, dz_ref, dw_ref, ds_ref, ext, tmp_a, tmp_b, dw_acc, ds_acc):
        b, i = pl.program_id(0), pl.program_id(1)
        first = jnp.logical_and(b == 0, i == 0)
        dyv = dy_ref[...]
        pv = p_ref[...]
        sc = sc_ref[...]
        w = w_ref[...]
        pw = jnp.dot(pv, w, preferred_element_type=F32)
        ds_part = jnp.sum((dyv * pw).reshape(POOL_T // 8, 8, MAIN_W), axis=0)
        dpw = (dyv * sc).astype(BF)
        dw_part = lax.dot_general(pv, dpw, (((0,), (0,)), ((), ())), preferred_element_type=F32)

        @pl.when(first)
        def _():
            dw_acc[...] = dw_part
            ds_acc[...] = ds_part

        @pl.when(jnp.logical_not(first))
        def _():
            dw_acc[...] += dw_part
            ds_acc[...] += ds_part

        @pl.when(jnp.logical_and(b == pl.num_programs(0) - 1, i == nt - 1))
        def _():
            dw_ref[...] = dw_acc[...]
            ds_ref[...] = jnp.sum(ds_acc[...], axis=0, keepdims=True)

        dp = lax.dot_general(dpw, w, (((1,), (1,)), ((), ())), preferred_element_type=F32)
        dpn = lax.dot_general((dyn_ref[...] * sc).astype(BF), w, (((1,), (1,)), ((), ())), preferred_element_type=F32)
        win = _pool_lane_window((POOL_T, MAIN_W))
        win_n = _pool_lane_window((POOL_HALO, MAIN_W))
        t = i * POOL_T + lax.broadcasted_iota(jnp.int32, (POOL_T, MAIN_W), 0)
        tn = (i + 1) * POOL_T + lax.broadcasted_iota(jnp.int32, (POOL_HALO, MAIN_W), 0)
        ext[pl.ds(0, POOL_T), :] = dp / jnp.minimum(t + 1, win).astype(F32)
        ext[pl.ds(POOL_T, POOL_HALO), :] = jnp.where(i < nt - 1, dpn / jnp.minimum(tn + 1, win_n).astype(F32), 0.0)
        ext[pl.ds(POOL_T + POOL_HALO, POOL_PAD - POOL_HALO), :] = jnp.zeros((POOL_PAD - POOL_HALO, MAIN_W), F32)
        acc = _pool_window_sums(ext, tmp_a, tmp_b, win, False) - dp
        dz_ref[...] = acc.astype(dz_ref.dtype)

    cur = lambda b, i: (b * nt + i, 0)
    nxt = lambda b, i: (jnp.minimum((b * nt + i + 1) * hb, last_halo), 0)
    return pl.pallas_call(
        body, name=name, grid=(B, nt),
        in_specs=[pl.BlockSpec((POOL_T, MAIN_W), cur), pl.BlockSpec((POOL_HALO, MAIN_W), nxt),
                  pl.BlockSpec((POOL_T, MAIN_W), cur), pl.BlockSpec((POOL_HALO, MAIN_W), nxt),
                  pl.BlockSpec((MAIN_W, MAIN_W), lambda b, i: (0, 0)),
                  pl.BlockSpec((1, MAIN_W), lambda b, i: (0, 0)),
                  pl.BlockSpec(memory_space=pl.ANY)],
        out_specs=[pl.BlockSpec((POOL_T, MAIN_W), cur),
                   pl.BlockSpec((MAIN_W, MAIN_W), lambda b, i: (0, 0)),
                   pl.BlockSpec((1, MAIN_W), lambda b, i: (0, 0))],
        out_shape=[jax.ShapeDtypeStruct(dz_alias.shape, dz_alias.dtype),
                   jax.ShapeDtypeStruct((MAIN_W, MAIN_W), F32), jax.ShapeDtypeStruct((1, MAIN_W), F32)],
        scratch_shapes=[pltpu.VMEM((POOL_R, MAIN_W), F32)] * 3 + [pltpu.VMEM((MAIN_W, MAIN_W), F32), pltpu.VMEM((8, MAIN_W), F32)],
        input_output_aliases={6: 0},
        compiler_params=_cp(("arbitrary", "arbitrary")),
    )(dy, dy, p, p, wbd, scale.reshape(1, MAIN_W), dz_alias)


def _head_masks(shape):
    lane = lax.broadcasted_iota(jnp.int32, shape, 1)
    return [(lane // HEAD_DIM) == h for h in range(shape[1] // HEAD_DIM)]


def _row_of(bcast, mask):
    return jnp.max(jnp.where(mask, bcast, -jnp.inf), axis=-1, keepdims=True)


MEM_TQ = 512


def _memattn_fwd(z, kv, y_alias, B, S, *, name):
    T = z.shape[0]
    nt = S // MEM_TQ

    def body(q_ref, k_ref, v_ref, _, y_ref, l_ref):
        q = q_ref[...]
        k = k_ref[...]
        v = v_ref[...]
        masks = _head_masks(q.shape)
        o = jnp.zeros(q.shape, F32)
        lse_b = jnp.zeros(q.shape, F32)
        for m in masks:
            qm = jnp.where(m, q, 0.0).astype(BF)
            s = lax.dot_general(qm, k, (((1,), (1,)), ((), ())), preferred_element_type=F32) * SCALE
            mx = jnp.max(s, axis=-1, keepdims=True)
            e = jnp.exp(s - mx)
            l = jnp.sum(e, axis=-1, keepdims=True)
            p = (e / l).astype(BF)
            o = o + jnp.where(m, jnp.dot(p, v, preferred_element_type=F32), 0.0)
            lse_b = lse_b + jnp.where(m, mx + jnp.log(l), 0.0)
        y_ref[...] = o.astype(y_ref.dtype)
        l_ref[...] = lse_b

    qblk = pl.BlockSpec((MEM_TQ, MEM_W), lambda b, i: (b * nt + i, 3))
    return pl.pallas_call(
        body, name=name, grid=(B, nt),
        in_specs=[qblk, pl.BlockSpec((N_MEM, MEM_W), lambda b, i: (b, 0)), pl.BlockSpec((N_MEM, MEM_W), lambda b, i: (b, 1)),
                  pl.BlockSpec(memory_space=pl.ANY)],
        out_specs=[qblk, pl.BlockSpec((MEM_TQ, MEM_W), lambda b, i: (b * nt + i, 0))],
        out_shape=[jax.ShapeDtypeStruct(y_alias.shape, y_alias.dtype), jax.ShapeDtypeStruct((T, MEM_W), F32)],
        input_output_aliases={3: 0},
        compiler_params=_cp(("parallel", "parallel")),
    )(z, kv, kv, y_alias)


def _memattn_bwd(dy, z, kv, lse, dz_alias, B, S, *, name):
    nt = S // MEM_TQ

    def body(do_ref, q_ref, k_ref, v_ref, l_ref, _, dz_ref, dk_ref, dv_ref, dk_acc, dv_acc):
        i = pl.program_id(1)
        do = do_ref[...]
        q = q_ref[...]
        k = k_ref[...]
        v = v_ref[...]
        lse_b = l_ref[...]
        masks = _head_masks(q.shape)
        dq = jnp.zeros(q.shape, F32)
        dk = jnp.zeros(k.shape, F32)
        dv = jnp.zeros(v.shape, F32)
        for m in masks:
            qm = jnp.where(m, q, 0.0).astype(BF)
            dom = jnp.where(m, do, 0.0).astype(BF)
            s = lax.dot_general(qm, k, (((1,), (1,)), ((), ())), preferred_element_type=F32) * SCALE
            p = jnp.exp(s - _row_of(lse_b, m))
            dp = lax.dot_general(dom, v, (((1,), (1,)), ((), ())), preferred_element_type=F32)
            delta = jnp.sum(p * dp, axis=-1, keepdims=True)
            ds = (p * (dp - delta) * SCALE).astype(BF)
            pb = p.astype(BF)
            dv = dv + jnp.where(m[:N_MEM], lax.dot_general(pb, dom, (((0,), (0,)), ((), ())), preferred_element_type=F32), 0.0)
            dk = dk + jnp.where(m[:N_MEM], lax.dot_general(ds, qm, (((0,), (0,)), ((), ())), preferred_element_type=F32), 0.0)
            dq = dq + jnp.where(m, jnp.dot(ds, k, preferred_element_type=F32), 0.0)
        dz_ref[...] = dq.astype(dz_ref.dtype)

        @pl.when(i == 0)
        def _():
            dk_acc[...] = dk
            dv_acc[...] = dv

        @pl.when(i > 0)
        def _():
            dk_acc[...] += dk
            dv_acc[...] += dv

        @pl.when(i == nt - 1)
        def _():
            dk_ref[...] = dk_acc[...]
            dv_ref[...] = dv_acc[...]

    qblk = pl.BlockSpec((MEM_TQ, MEM_W), lambda b, i: (b * nt + i, 3))
    kblk = pl.BlockSpec((N_MEM, MEM_W), lambda b, i: (b, 0))
    return pl.pallas_call(
        body, name=name, grid=(B, nt),
        in_specs=[qblk, qblk, kblk, pl.BlockSpec((N_MEM, MEM_W), lambda b, i: (b, 1)),
                  pl.BlockSpec((MEM_TQ, MEM_W), lambda b, i: (b * nt + i, 0)), pl.BlockSpec(memory_space=pl.ANY)],
        out_specs=[qblk, kblk, kblk],
        out_shape=[jax.ShapeDtypeStruct(dz_alias.shape, dz_alias.dtype),
                   jax.ShapeDtypeStruct((B * N_MEM, MEM_W), F32), jax.ShapeDtypeStruct((B * N_MEM, MEM_W), F32)],
        scratch_shapes=[pltpu.VMEM((N_MEM, MEM_W), F32), pltpu.VMEM((N_MEM, MEM_W), F32)],
        input_output_aliases={5: 0},
        compiler_params=_cp(("parallel", "arbitrary")),
    )(dy, z, kv, kv, lse, dz_alias)


def _dil_scores(qm, kp, kc, n):
    qi = lax.broadcasted_iota(jnp.int32, (STEPS, STEPS), 0)
    kj = lax.broadcasted_iota(jnp.int32, (STEPS, STEPS), 1)
    sc = lax.dot_general(qm, kc, (((1,), (1,)), ((), ())), preferred_element_type=F32) * SCALE
    sc = jnp.where(kj <= qi, sc, NEG)
    if kp is None:
        return None, sc
    sp = lax.dot_general(qm, kp, (((1,), (1,)), ((), ())), preferred_element_type=F32) * SCALE
    sp = jnp.where(jnp.logical_and(kj >= qi, n > 0), sp, NEG)
    return sp, sc


def _dil_specs(g, d, nb):
    chunk = STEPS * d
    cur = pl.BlockSpec((chunk, 128), lambda b, n, hf: (b * nb + n, g * 2 + hf))
    prev = pl.BlockSpec((chunk, 128), lambda b, n, hf: (b * nb + jnp.maximum(n - 1, 0), g * 2 + hf))
    return cur, prev


def _dil_rows(r, d):
    return pl.ds(r, STEPS, stride=d) if d > 1 else slice(None)


def _dil_loop(d, fn):
    if d <= 4:
        for r in range(d):
            fn(r)
    else:
        lax.fori_loop(0, d, lambda r, carry: (fn(r), carry)[1], 0)


def _dil_fwd_group(g, q, k, v, o_alias, l_alias, B, S, *, name):
    d = DIL[g]
    nb = S // (STEPS * d)
    has_prev = nb > 1

    def body(*refs):
        if has_prev:
            q_ref, kp_ref, kc_ref, vp_ref, vc_ref, _, __, o_ref, l_ref = refs
        else:
            q_ref, kc_ref, vc_ref, _, __, o_ref, l_ref = refs
        n = pl.program_id(1)

        def residue(r):
            rows = _dil_rows(r, d)
            q = q_ref[rows, :]
            kc, vc = kc_ref[rows, :].astype(BF), vc_ref[rows, :].astype(BF)
            kp = kp_ref[rows, :].astype(BF) if has_prev else None
            vp = vp_ref[rows, :].astype(BF) if has_prev else None
            o = jnp.zeros(q.shape, F32)
            lse_b = jnp.zeros(q.shape, F32)
            for m in _head_masks(q.shape):
                qm = jnp.where(m, q, 0.0).astype(BF)
                sp, sc = _dil_scores(qm, kp, kc, n)
                mx = jnp.max(sc, axis=-1, keepdims=True)
                if has_prev:
                    mx = jnp.maximum(mx, jnp.max(sp, axis=-1, keepdims=True))
                l = jnp.sum(jnp.exp(sc - mx), axis=-1, keepdims=True)
                if has_prev:
                    l = l + jnp.sum(jnp.exp(sp - mx), axis=-1, keepdims=True)
                lse = mx + jnp.log(l)
                oh = jnp.dot(jnp.exp(sc - lse).astype(BF), vc, preferred_element_type=F32)
                if has_prev:
                    oh = oh + jnp.dot(jnp.exp(sp - lse).astype(BF), vp, preferred_element_type=F32)
                o = o + jnp.where(m, oh, 0.0)
                lse_b = lse_b + jnp.where(m, lse, 0.0)
            o_ref[rows, :] = o
            l_ref[rows, :] = lse_b

        _dil_loop(d, residue)

    cur, prev = _dil_specs(g, d, nb)
    anyspec = pl.BlockSpec(memory_space=pl.ANY)
    if has_prev:
        in_specs, ops = [cur, prev, cur, prev, cur], [q, k, k, v, v]
    else:
        in_specs, ops = [cur, cur, cur], [q, k, v]
    n_in = len(ops)
    o, l = pl.pallas_call(
        body, name=name, grid=(B, nb, 2),
        in_specs=in_specs + [anyspec, anyspec],
        out_specs=[cur, cur],
        out_shape=[jax.ShapeDtypeStruct(q.shape, F32)] * 2,
        input_output_aliases={n_in: 0, n_in + 1: 1},
        compiler_params=_cp(("parallel", "parallel", "parallel")),
    )(*ops, o_alias, l_alias)
    return o, l


def _dil_bwd_group(g, q, k, v, do, cb, lse, aliases, B, S, *, name):
    d = DIL[g]
    nb = S // (STEPS * d)
    has_prev = nb > 1
    n_out = 5 if has_prev else 3

    def body(*refs):
        if has_prev:
            q_ref, kp_ref, kc_ref, vp_ref, vc_ref, do_ref, c_ref, l_ref = refs[:8]
            dq_ref, dkc_ref, dvc_ref, dkp_ref, dvp_ref = refs[8 + n_out:]
        else:
            q_ref, kc_ref, vc_ref, do_ref, c_ref, l_ref = refs[:6]
            dq_ref, dkc_ref, dvc_ref = refs[6 + n_out:]
        n = pl.program_id(1)
        tdot = lambda a, b: lax.dot_general(a, b, (((0,), (0,)), ((), ())), preferred_element_type=F32)
        ndot = lambda a, b: lax.dot_general(a, b, (((1,), (1,)), ((), ())), preferred_element_type=F32)

        def residue(r):
            rows = _dil_rows(r, d)
            q = q_ref[rows, :]
            kc, vc = kc_ref[rows, :].astype(BF), vc_ref[rows, :].astype(BF)
            kp = kp_ref[rows, :].astype(BF) if has_prev else None
            vp = vp_ref[rows, :].astype(BF) if has_prev else None
            do = do_ref[rows, :]
            cbv = c_ref[rows, :]
            lse_b = l_ref[rows, :]
            z = jnp.zeros(q.shape, F32)
            dq, dkc, dkp, dvc, dvp = z, z, z, z, z
            for m in _head_masks(q.shape):
                qm = jnp.where(m, q, 0.0).astype(BF)
                dom = jnp.where(m, do, 0.0).astype(BF)
                sp, sc = _dil_scores(qm, kp, kc, n)
                lse = _row_of(lse_b, m)
                c = _row_of(cbv, m)
                pc = jnp.exp(sc - lse)
                dsc = (pc * (ndot(dom, vc) + c) * SCALE).astype(BF)
                dqh = jnp.dot(dsc, kc, preferred_element_type=F32)
                dkc = dkc + jnp.where(m, tdot(dsc, qm), 0.0)
                dvc = dvc + jnp.where(m, tdot(pc.astype(BF), dom), 0.0)
                if has_prev:
                    pp = jnp.exp(sp - lse)
                    dsp = (pp * (ndot(dom, vp) + c) * SCALE).astype(BF)
                    dqh = dqh + jnp.dot(dsp, kp, preferred_element_type=F32)
                    dkp = dkp + jnp.where(m, tdot(dsp, qm), 0.0)
                    dvp = dvp + jnp.where(m, tdot(pp.astype(BF), dom), 0.0)
                dq = dq + jnp.where(m, dqh, 0.0)
            dq_ref[rows, :] = dq
            dkc_ref[rows, :] = dkc
            dvc_ref[rows, :] = dvc
            if has_prev:
                dkp_ref[rows, :] = dkp
                dvp_ref[rows, :] = dvp

        _dil_loop(d, residue)

    cur, prev = _dil_specs(g, d, nb)
    anyspec = pl.BlockSpec(memory_space=pl.ANY)
    dq_a, dkc_a, dkp_a, dvc_a, dvp_a = aliases
    if has_prev:
        in_specs, ops = [cur, prev, cur, prev, cur, cur, cur, cur], [q, k, k, v, v, do, cb, lse]
        al = [dq_a, dkc_a, dvc_a, dkp_a, dvp_a]
    else:
        in_specs, ops = [cur, cur, cur, cur, cur, cur], [q, k, v, do, cb, lse]
        al = [dq_a, dkc_a, dvc_a]
    n_in = len(ops)
    outs = pl.pallas_call(
        body, name=name, grid=(B, nb, 2),
        in_specs=in_specs + [anyspec] * n_out,
        out_specs=[cur] * n_out,
        out_shape=[jax.ShapeDtypeStruct(q.shape, F32)] * n_out,
        input_output_aliases={n_in + i: i for i in range(n_out)},
        compiler_params=_cp(("parallel", "parallel", "parallel")),
    )(*ops, *al)
    if has_prev:
        dq_a, dkc_a, dvc_a, dkp_a, dvp_a = outs
    else:
        dq_a, dkc_a, dvc_a = outs
    return dq_a, dkc_a, dkp_a, dvc_a, dvp_a


N_UNITS = 16


def _unit_rows(g):
    d = DIL[g]
    nb = N_UNITS // d
    return [pl.ds(n * STEPS * d + r, STEPS, stride=d) if d > 1 else pl.ds(n * STEPS, STEPS)
            for n in range(nb) for r in range(d)]


def _load_units(ref, g):
    if DIL[g] == 1:
        return ref[...].reshape(N_UNITS, STEPS, 128)
    return jnp.stack([ref[rows, :] for rows in _unit_rows(g)])


def _store_units(ref, val, g):
    if DIL[g] == 1:
        ref[...] = val.reshape(N_UNITS * STEPS, 128)
    else:
        for u, rows in enumerate(_unit_rows(g)):
            ref[rows, :] = val[u]


def _shift_units(x, by):
    z = jnp.zeros((abs(by),) + x.shape[1:], x.dtype)
    return jnp.concatenate([z, x[:N_UNITS - by]], axis=0) if by > 0 else jnp.concatenate([x[-by:], z], axis=0)


def _bdot(a, b, ca, cb):
    return lax.dot_general(a, b, (((ca,), (cb,)), ((0,), (0,))), preferred_element_type=F32)


def _dil_masks(g):
    d = DIL[g]
    has_prev = N_UNITS // d > 1
    qi = lax.broadcasted_iota(jnp.int32, (1, STEPS, STEPS), 1)
    kj = lax.broadcasted_iota(jnp.int32, (1, STEPS, STEPS), 2)
    unit = lax.broadcasted_iota(jnp.int32, (N_UNITS, 1, 1), 0)
    cur = kj <= qi
    prev = jnp.logical_and(kj >= qi, unit >= d) if has_prev else None
    lane = lax.broadcasted_iota(jnp.int32, (1, 1, 128), 2)
    heads = [(lane // HEAD_DIM) == h for h in range(128 // HEAD_DIM)]
    return has_prev, cur, prev, heads


def _dil_fwd(g, q, k, v, o_alias, l_alias, B, S, *, name):
    assert S == N_UNITS * STEPS
    d = DIL[g]

    def body(q_ref, k_ref, v_ref, _, __, o_ref, l_ref):
        has_prev, cur, prev, heads = _dil_masks(g)
        q = _load_units(q_ref, g)
        kc = _load_units(k_ref, g).astype(BF)
        vc = _load_units(v_ref, g).astype(BF)
        if has_prev:
            kp, vp = _shift_units(kc, d), _shift_units(vc, d)
        o = jnp.zeros(q.shape, F32)
        lse_b = jnp.zeros(q.shape, F32)
        for m in heads:
            qm = jnp.where(m, q, 0.0).astype(BF)
            sc = jnp.where(cur, _bdot(qm, kc, 2, 2) * SCALE, NEG)
            mx = jnp.max(sc, axis=-1, keepdims=True)
            if has_prev:
                sp = jnp.where(prev, _bdot(qm, kp, 2, 2) * SCALE, NEG)
                mx = jnp.maximum(mx, jnp.max(sp, axis=-1, keepdims=True))
            l = jnp.sum(jnp.exp(sc - mx), axis=-1, keepdims=True)
            if has_prev:
                l = l + jnp.sum(jnp.exp(sp - mx), axis=-1, keepdims=True)
            lse = mx + jnp.log(l)
            oh = _bdot(jnp.exp(sc - lse).astype(BF), vc, 2, 1)
            if has_prev:
                oh = oh + _bdot(jnp.exp(sp - lse).astype(BF), vp, 2, 1)
            o = o + jnp.where(m, oh, 0.0)
            lse_b = lse_b + jnp.where(m, lse, 0.0)
        _store_units(o_ref, o, g)
        _store_units(l_ref, lse_b, g)

    blk = pl.BlockSpec((S, 128), lambda b, hf: (b, g * 2 + hf))
    anyspec = pl.BlockSpec(memory_space=pl.ANY)
    o, l = pl.pallas_call(
        body, name=name, grid=(B, 2),
        in_specs=[blk, blk, blk, anyspec, anyspec], out_specs=[blk, blk],
        out_shape=[jax.ShapeDtypeStruct(q.shape, F32)] * 2,
        input_output_aliases={3: 0, 4: 1},
        compiler_params=_cp(("parallel", "parallel")),
    )(q, k, v, o_alias, l_alias)
    return o, l


def _dil_bwd(g, q, k, v, do, cb, lse, aliases, B, S, *, name):
    assert S == N_UNITS * STEPS
    d = DIL[g]

    def body(q_ref, k_ref, v_ref, do_ref, c_ref, l_ref, _, __, ___, dq_ref, dk_ref, dv_ref):
        has_prev, cur, prev, heads = _dil_masks(g)
        q = _load_units(q_ref, g)
        kc = _load_units(k_ref, g).astype(BF)
        vc = _load_units(v_ref, g).astype(BF)
        do = _load_units(do_ref, g)
        cbv = _load_units(c_ref, g)
        lse_b = _load_units(l_ref, g)
        if has_prev:
            kp, vp = _shift_units(kc, d), _shift_units(vc, d)
        z = jnp.zeros(q.shape, F32)
        dq, dkc, dkp, dvc, dvp = z, z, z, z, z
        for m in heads:
            qm = jnp.where(m, q, 0.0).astype(BF)
            dom = jnp.where(m, do, 0.0).astype(BF)
            lse = jnp.max(jnp.where(m, lse_b, -jnp.inf), axis=-1, keepdims=True)
            c = jnp.max(jnp.where(m, cbv, -jnp.inf), axis=-1, keepdims=True)
            sc = jnp.where(cur, _bdot(qm, kc, 2, 2) * SCALE, NEG)
            pc = jnp.exp(sc - lse)
            dsc = (pc * (_bdot(dom, vc, 2, 2) + c) * SCALE).astype(BF)
            dqh = _bdot(dsc, kc, 2, 1)
            dkc = dkc + jnp.where(m, _bdot(dsc, qm, 1, 1), 0.0)
            dvc = dvc + jnp.where(m, _bdot(pc.astype(BF), dom, 1, 1), 0.0)
            if has_prev:
                sp = jnp.where(prev, _bdot(qm, kp, 2, 2) * SCALE, NEG)
                pp = jnp.exp(sp - lse)
                dsp = (pp * (_bdot(dom, vp, 2, 2) + c) * SCALE).astype(BF)
                dqh = dqh + _bdot(dsp, kp, 2, 1)
                dkp = dkp + jnp.where(m, _bdot(dsp, qm, 1, 1), 0.0)
                dvp = dvp + jnp.where(m, _bdot(pp.astype(BF), dom, 1, 1), 0.0)
            dq = dq + jnp.where(m, dqh, 0.0)
        if has_prev:
            dkc = dkc + _shift_units(dkp, -d)
            dvc = dvc + _shift_units(dvp, -d)
        _store_units(dq_ref, dq, g)
        _store_units(dk_ref, dkc, g)
        _store_units(dv_ref, dvc, g)

    blk = pl.BlockSpec((S, 128), lambda b, hf: (b, g * 2 + hf))
    anyspec = pl.BlockSpec(memory_space=pl.ANY)
    return tuple(pl.pallas_call(
        body, name=name, grid=(B, 2),
        in_specs=[blk] * 6 + [anyspec] * 3, out_specs=[blk] * 3,
        out_shape=[jax.ShapeDtypeStruct(q.shape, F32)] * 3,
        input_output_aliases={6: 0, 7: 1, 8: 2},
        compiler_params=_cp(("parallel", "parallel")),
    )(q, k, v, do, cb, lse, *aliases))


def _kv_grad_sum(parts, cos, sin, *, name, tm=512):
    T = parts[0][0].shape[0]
    n_l = len(parts)

    def body(*refs):
        c_ref, s_ref = refs[0], refs[1]
        dk_ref, dv_ref = refs[2 + 2 * n_l:]
        dk = refs[2][...]
        dv = refs[3][...]
        for li in range(1, n_l):
            dk = dk + refs[2 + 2 * li][...]
            dv = dv + refs[3 + 2 * li][...]
        dk_ref[...] = _rot(dk, c_ref[...], s_ref[...], -1.0).astype(dk_ref.dtype)
        dv_ref[...] = dv.astype(dv_ref.dtype)

    full = pl.BlockSpec((tm, MAIN_W), lambda i: (i, 0))
    tab = pl.BlockSpec((tm, 128), lambda i: (i, 0))
    ops = [cos, sin] + [t for part in parts for t in part]
    return pl.pallas_call(
        body, name=name, grid=(T // tm,), in_specs=[tab, tab] + [full] * (2 * n_l), out_specs=[full, full],
        out_shape=[jax.ShapeDtypeStruct((T, MAIN_W), BF)] * 2,
        compiler_params=_cp(("parallel",)),
    )(*ops)


def _group_softmax(lse):
    l0, l1, l2 = lse[:, 0:256], lse[:, 256:512], lse[:, 512:768]
    mx = jnp.maximum(jnp.maximum(l0, l1), l2)
    e0, e1, e2 = jnp.exp(l0 - mx), jnp.exp(l1 - mx), jnp.exp(l2 - mx)
    tot = e0 + e1 + e2
    return e0 / tot, e1 / tot, e2 / tot


def _dil_combine_fwd(o, lse, y_alias, *, name, tm=512):
    T = o.shape[0]

    def body(o_ref, l_ref, _, y_ref):
        a = jnp.concatenate(_group_softmax(l_ref[...]), axis=1)
        y_ref[...] = (o_ref[...] * a).astype(y_ref.dtype)

    blk = pl.BlockSpec((tm, MAIN_W), lambda i: (i, 0))
    return pl.pallas_call(
        body, name=name, grid=(T // tm,), in_specs=[blk, blk, pl.BlockSpec(memory_space=pl.ANY)], out_specs=blk,
        out_shape=jax.ShapeDtypeStruct(y_alias.shape, y_alias.dtype), input_output_aliases={2: 0},
        compiler_params=_cp(("parallel",)),
    )(o, lse, y_alias)


def _dil_combine_bwd(dy, o, lse, *, name, tm=256):
    T = o.shape[0]
    lane_r = lax.broadcasted_iota(jnp.int32, (256, 256), 0) // HEAD_DIM
    lane_c = lax.broadcasted_iota(jnp.int32, (256, 256), 1) // HEAD_DIM
    ones_bd = (lane_r == lane_c).astype(BF)

    def body(dy_ref, o_ref, l_ref, e_ref, do_ref, c_ref):
        dyv = dy_ref[...]
        alphas = _group_softmax(l_ref[...])
        prod = dyv * o_ref[...]
        e = e_ref[...]
        tot = jnp.zeros((tm, 256), F32)
        for gi in range(3):
            x = prod[:, gi * 256:(gi + 1) * 256]
            hi = x.astype(BF)
            lo = (x - hi.astype(F32)).astype(BF)
            dalpha = jnp.dot(hi, e, preferred_element_type=F32) + jnp.dot(lo, e, preferred_element_type=F32)
            tot = tot + alphas[gi] * dalpha
        a = jnp.concatenate(alphas, axis=1)
        do_ref[...] = (dyv * a).astype(do_ref.dtype)
        c_ref[...] = jnp.concatenate([-al * tot for al in alphas], axis=1)

    blk = pl.BlockSpec((tm, MAIN_W), lambda i: (i, 0))
    return pl.pallas_call(
        body, name=name, grid=(T // tm,),
        in_specs=[blk, blk, blk, pl.BlockSpec((256, 256), lambda i: (0, 0))], out_specs=[blk, blk],
        out_shape=[jax.ShapeDtypeStruct((T, MAIN_W), F32), jax.ShapeDtypeStruct((T, MAIN_W), F32)],
        compiler_params=_cp(("parallel",)),
    )(dy, o, lse, ones_bd)


def _kv_grad(parts, cos, sin, B, S, *, name):
    T = B * S
    tb = S // STEPS
    n_l = len(parts)

    def shifted(g):
        def f(b, t):
            return (b * tb + jnp.minimum(t + DIL[g], tb - 1), g)
        return f

    with_prev = [g for g in range(3) if DIL[g] < tb]
    n_p = len(with_prev)
    per_l = 2 + 2 * n_p

    def body(*refs):
        c_ref, s_ref = refs[0], refs[1]
        ins = refs[2:2 + n_l * per_l]
        dk_ref, dv_ref = refs[2 + n_l * per_l:]
        t = pl.program_id(1)
        dk = jnp.zeros((STEPS, MAIN_W), F32)
        dv = jnp.zeros((STEPS, MAIN_W), F32)
        zero = jnp.zeros((STEPS, 256), F32)
        for li in range(n_l):
            base = li * per_l
            dk = dk + ins[base][...]
            dv = dv + ins[base + 1][...]
            kparts, vparts = [zero] * 3, [zero] * 3
            for pi, g in enumerate(with_prev):
                ok = t + DIL[g] < tb
                kparts[g] = jnp.where(ok, ins[base + 2 + pi][...], 0.0)
                vparts[g] = jnp.where(ok, ins[base + 2 + n_p + pi][...], 0.0)
            dk = dk + jnp.concatenate(kparts, axis=1)
            dv = dv + jnp.concatenate(vparts, axis=1)
        dk_ref[...] = _rot(dk, c_ref[...], s_ref[...], -1.0).astype(dk_ref.dtype)
        dv_ref[...] = dv.astype(dv_ref.dtype)

    full = pl.BlockSpec((STEPS, MAIN_W), lambda b, t: (b * tb + t, 0))
    tab = pl.BlockSpec((STEPS, 128), lambda b, t: (b * tb + t, 0))
    in_specs, ops = [tab, tab], [cos, sin]
    for (kc, kp, vc, vp) in parts:
        in_specs += [full, full] + [pl.BlockSpec((STEPS, 256), shifted(g)) for g in with_prev] * 2
        ops += [kc, vc] + [kp] * n_p + [vp] * n_p
    return pl.pallas_call(
        body, name=name, grid=(B, tb), in_specs=in_specs, out_specs=[full, full],
        out_shape=[jax.ShapeDtypeStruct((T, MAIN_W), BF)] * 2,
        compiler_params=_cp(("parallel", "parallel")),
    )(*ops)


def _loss(y, target, *, name, tm=512):
    T, Dm = y.shape
    nt = T // tm

    def body(y_ref, t_ref, l_ref, d_ref, acc):
        i = pl.program_id(0)
        err = y_ref[...] - t_ref[...]
        d_ref[...] = err / Dm
        part = jnp.sum(jnp.mean(err * err, axis=-1, keepdims=True).reshape(tm // 8, 8, 1), axis=0)

        @pl.when(i == 0)
        def _():
            acc[...] = part

        @pl.when(i > 0)
        def _():
            acc[...] += part

        @pl.when(i == nt - 1)
        def _():
            l_ref[...] = 0.5 * jnp.sum(acc[...], axis=0, keepdims=True)

    row = pl.BlockSpec((tm, Dm), lambda i: (i, 0))
    return pl.pallas_call(
        body, name=name, grid=(nt,), in_specs=[row, row],
        out_specs=[pl.BlockSpec((1, 1), lambda i: (0, 0)), row],
        out_shape=[jax.ShapeDtypeStruct((1, 1), F32), jax.ShapeDtypeStruct((T, Dm), F32)],
        scratch_shapes=[pltpu.VMEM((8, 1), F32)],
        compiler_params=_cp(("arbitrary",)),
    )(y, target)


def _adamw(w, g, m, v, *, name):
    shape = w.shape
    cols = shape[-1]
    rows = w.size // cols
    tm = rows
    for cand in (512, 352, 256, 128):
        if rows > cand and rows % cand == 0 and cand * cols * 4 <= (1 << 20):
            tm = cand
            break

    def body(w_ref, g_ref, m_ref, v_ref, d_ref, mo_ref, vo_ref):
        gv = g_ref[...]
        mn = ADAM_B1 * m_ref[...] + (1.0 - ADAM_B1) * gv
        vn = ADAM_B2 * v_ref[...] + (1.0 - ADAM_B2) * (gv * gv)
        m_hat = mn / (1.0 - ADAM_B1 ** ADAM_STEP)
        v_hat = vn / (1.0 - ADAM_B2 ** ADAM_STEP)
        d_ref[...] = -ADAM_LR * (m_hat / (jnp.sqrt(v_hat) + ADAM_EPS) + ADAM_WD * w_ref[...])
        mo_ref[...] = mn
        vo_ref[...] = vn

    blk = pl.BlockSpec((tm, cols), lambda i: (i, 0))
    outs = pl.pallas_call(
        body, name=name, grid=(rows // tm,), in_specs=[blk] * 4, out_specs=[blk] * 3,
        out_shape=[jax.ShapeDtypeStruct((rows, cols), F32)] * 3,
        compiler_params=_cp(("parallel",)),
    )(*[t.reshape(rows, cols) for t in (w, g, m, v)])
    return tuple(t.reshape(shape) for t in outs)


def _adamw_layer(name, l, w, g, m, v, prev, after=None):
    L, rows, cols = w.shape
    tm = rows
    for cand in (512, 352, 256, 176, 128, 64):
        if rows % cand == 0 and cand * cols * 4 <= (1 << 20):
            tm = cand
            break
    if prev is None:
        prev = tuple(lax.empty(w.shape, F32) for _ in range(4))

    n_after = 0 if after is None else 1

    def body(w_ref, g_ref, m_ref, v_ref, *rest):
        d_ref, mo_ref, vo_ref, go_ref = rest[4 + n_after:]
        gv = g_ref[...]
        mn = ADAM_B1 * m_ref[...] + (1.0 - ADAM_B1) * gv
        vn = ADAM_B2 * v_ref[...] + (1.0 - ADAM_B2) * (gv * gv)
        m_hat = mn / (1.0 - ADAM_B1 ** ADAM_STEP)
        v_hat = vn / (1.0 - ADAM_B2 ** ADAM_STEP)
        d_ref[...] = -ADAM_LR * (m_hat / (jnp.sqrt(v_hat) + ADAM_EPS) + ADAM_WD * w_ref[...])
        mo_ref[...] = mn
        vo_ref[...] = vn
        go_ref[...] = gv

    lay = pl.BlockSpec((None, tm, cols), lambda i: (l, i, 0))
    one = pl.BlockSpec((None, tm, cols), lambda i: (0, i, 0))
    return tuple(pl.pallas_call(
        body, name=f"l{l}_adamw_{name}", grid=(rows // tm,),
        in_specs=[lay, one, lay, lay] + [pl.BlockSpec(memory_space=pl.ANY)] * (4 + n_after), out_specs=[lay] * 4,
        out_shape=[jax.ShapeDtypeStruct(w.shape, F32)] * 4,
        input_output_aliases={4 + i: i for i in range(4)},
        compiler_params=_cp(("parallel",)),
    )(w, g, m, v, *prev, *([] if after is None else [after])))


BIG = {
    'w_in': ((DEPTH, D_MODEL, D_MODEL), 'row'),
    'w_mem_kv': ((DEPTH, D_MODEL, 2 * MEM_W), 'row'),
    'w_out': ((DEPTH, D_MODEL, D_MODEL), 'row'),
    'w_kv': ((1, D_MODEL, 2 * MAIN_W), 'col'),
    'w_gate_up': ((DEPTH, D_MODEL, 2 * D_FF), 'col'),
    'w_down': ((DEPTH, D_FF, D_MODEL), 'row'),
}
BIG_NAMES = tuple(BIG)
N_CHIPS = 4
HBM_ANY = pl.BlockSpec(memory_space=pl.ANY)


def _geom(name):
    (L, R, C), kind = BIG[name]
    if kind == 'row':
        return L, R, C, kind, R // N_CHIPS, C, R // (2 * N_CHIPS)
    return L, R, C, kind, R, C // N_CHIPS, R // 2


def _shard_shape(name):
    L, R, C, kind, rs, cs, rh = _geom(name)
    return (L, rs, cs)


def _half_shape(name):
    L, R, C, kind, rs, cs, rh = _geom(name)
    return (L, rh, cs)


def _full_win(ref, name, s, h):
    L, R, C, kind, rs, cs, rh = _geom(name)
    if kind == 'row':
        rows = pl.ds(s * rs, rs) if h is None else pl.ds(s * rs + h * rh, rh)
        return ref.at[:, rows, :]
    rows = slice(None) if h is None else pl.ds(h * rh, rh)
    return ref.at[:, rows, pl.ds(s * cs, cs)]


def _shard_half(ref, name, h):
    L, R, C, kind, rs, cs, rh = _geom(name)
    return ref.at[:, pl.ds(h * rh, rh), :]


def _halves_win(ref, name, s):
    L, R, C, kind, rs, cs, rh = _geom(name)
    if kind == 'row':
        return ref.at[:, pl.ds(s * rh, rh), :]
    return ref.at[:, :, pl.ds(s * cs, cs)]


def _halves_shape(name):
    L, R, C, kind, rs, cs, rh = _geom(name)
    return (L, N_CHIPS * rh, cs) if kind == 'row' else (L, rh, C)


def _place():
    x, y, c = lax.axis_index("x"), lax.axis_index("y"), lax.axis_index("c")
    chips = [(1 - x, y), (x, 1 - y), (1 - x, 1 - y)]
    return x, y, c, chips


SMALL_ROWS = 24


def _all_gather(shards, small):
    names = BIG_NAMES
    nw = len(names)

    def body(*refs):
        src = dict(zip(names, refs[:nw]))
        small_ref = refs[nw]
        dst = dict(zip(names, refs[nw + 1:2 * nw + 1]))
        small_out = refs[2 * nw + 1]
        send_sems, recv_sems, local_sems = refs[2 * nw + 2:]
        x, y, c, chips = _place()
        s = 2 * x + y
        sib = (x, y, 1 - c)

        def remote(k, src_ref, dst_ref, to):
            return pltpu.make_async_remote_copy(src_ref=src_ref, dst_ref=dst_ref, send_sem=send_sems.at[k],
                                                recv_sem=recv_sems.at[k], device_id=to, device_id_type=MESH)

        local = []
        for wi, nm in enumerate(names):
            local.append(pltpu.make_async_copy(src[nm], _full_win(dst[nm], nm, s, None), local_sems.at[wi]))
        local.append(pltpu.make_async_copy(small_ref, small_out.at[s], local_sems.at[nw]))
        for cp in local:
            cp.start()
        sends = []
        for j, (px, py) in enumerate(chips):
            for wi, nm in enumerate(names):
                sends.append(remote(wi * 6 + j, _shard_half(src[nm], nm, c), _full_win(dst[nm], nm, s, c), (px, py, c)))
            sends.append(remote(nw * 6 + j, small_ref, small_out.at[s], (px, py, c)))
        for cp in sends:
            cp.start()
        for j, (px, py) in enumerate(chips):
            sp = 2 * px + py
            for wi, nm in enumerate(names):
                w = _full_win(dst[nm], nm, sp, c)
                remote(wi * 6 + j, w, w, sib).wait_recv()
                fwd = remote(wi * 6 + 3 + j, w, w, sib)
                fwd.start()
                sends.append(fwd)
            remote(nw * 6 + j, small_ref, small_out.at[sp], sib).wait_recv()
        for j, (px, py) in enumerate(chips):
            sp = 2 * px + py
            for wi, nm in enumerate(names):
                w = _full_win(dst[nm], nm, sp, 1 - c)
                remote(wi * 6 + 3 + j, w, w, sib).wait_recv()
        for cp in sends:
            cp.wait_send()
        for cp in local:
            cp.wait()

    n_sem = nw * 6 + 3
    outs = pl.pallas_call(
        body, name="all_gather_weights",
        in_specs=[HBM_ANY] * (nw + 1), out_specs=[HBM_ANY] * (nw + 1),
        out_shape=[jax.ShapeDtypeStruct(BIG[nm][0], BF) for nm in names]
        + [jax.ShapeDtypeStruct((N_CHIPS, SMALL_ROWS, 256), F32)],
        scratch_shapes=[pltpu.SemaphoreType.DMA((n_sem,)), pltpu.SemaphoreType.DMA((n_sem,)),
                        pltpu.SemaphoreType.DMA((nw + 1,))],
    )(*[shards[nm] for nm in names], small)
    return dict(zip(names, outs[:nw])), outs[nw]


SEM_SPEC = pl.BlockSpec(memory_space=pltpu.SEMAPHORE)
HBM_SPEC = pl.BlockSpec(memory_space=pltpu.HBM)
DATAFLOW = pltpu.SideEffectType.DATAFLOW_SIDE_EFFECTING


def _in_hbm(a):
    return pltpu.with_memory_space_constraint(a, pltpu.HBM)


def _remote(src, dst, send_sems, recv_sems, k, to):
    return pltpu.make_async_remote_copy(src_ref=src, dst_ref=dst, send_sem=send_sems.at[k], recv_sem=recv_sems.at[k],
                                        device_id=to, device_id_type=MESH)


def _split_start(name, bufs, n_copies, sends, after=None):
    nb = len(bufs)
    n_in = nb + (0 if after is None else 1)

    def body(*refs):
        in_refs = refs[:nb]
        send_sems, recv_sems = refs[n_in], refs[n_in + 1]
        token = refs[-1]
        for k, (src, dst, to) in enumerate(sends(in_refs)):
            _remote(src, dst, send_sems, recv_sems, k, to).start()
        token[...] = jnp.zeros_like(token)

    outs = pl.pallas_call(
        body, name=name,
        out_shape=(pltpu.SemaphoreType.DMA((n_copies,)), pltpu.SemaphoreType.DMA((n_copies,)),
                   *[pltpu.HBM(b.shape, b.dtype) for b in bufs], jax.ShapeDtypeStruct((8, 128), F32)),
        in_specs=[HBM_SPEC] * nb + [HBM_ANY] * (n_in - nb),
        out_specs=(SEM_SPEC, SEM_SPEC, *[HBM_SPEC] * nb, pl.BlockSpec(memory_space=pltpu.VMEM)),
        input_output_aliases={i: 2 + i for i in range(nb)},
        compiler_params=pltpu.CompilerParams(has_side_effects=DATAFLOW),
    )(*[_in_hbm(b) for b in bufs], *([] if after is None else [after]))
    return outs[0], outs[1], list(outs[2:2 + nb]), outs[-1]


def _split_wait(name, send_sems, recv_sems, bufs, after, sends, arrivals):
    nb = len(bufs)

    def body(*refs):
        in_refs = refs[:nb]
        s_sems, r_sems = refs[nb], refs[nb + 1]
        me = (lax.axis_index("x"), lax.axis_index("y"), lax.axis_index("c"))
        for k, (src, dst, to) in enumerate(sends(in_refs)):
            _remote(src, dst, s_sems, r_sems, k, to).wait_send()
        for k, win in enumerate(arrivals(in_refs)):
            _remote(win, win, s_sems, r_sems, k, me).wait_recv()

    outs = pl.pallas_call(
        body, name=name,
        out_shape=[pltpu.HBM(b.shape, b.dtype) for b in bufs],
        in_specs=[HBM_SPEC] * nb + [SEM_SPEC, SEM_SPEC, HBM_ANY],
        out_specs=[HBM_SPEC] * nb,
        input_output_aliases={i: i for i in range(nb)},
        compiler_params=pltpu.CompilerParams(has_side_effects=DATAFLOW),
    )(*bufs, send_sems, recv_sems, after)
    return list(outs)


MIX_W = ('w_in', 'w_mem_kv', 'w_out')
FFN_W = ('w_gate_up', 'w_down')
LAYER_W = MIX_W + FFN_W


def _place_own(l, names, shards, small, sc):
    nw = len(names)
    has_small = small is not None
    n_ops = nw + (1 if has_small else 0)

    def body(sc_ref, *refs):
        for src, dst in zip(refs[:n_ops], refs[n_ops:]):
            dst[...] = src[...]

    in_specs, out_specs, out_shape, ops = [], [], [], list(shards)
    for nm in names:
        L, R, C, kind, rs, cs, rh = _geom(nm)
        in_specs.append(pl.BlockSpec((1, rs, cs), lambda i, sc_ref: (0, 0, 0)))
        if kind == 'row':
            out_specs.append(pl.BlockSpec((1, rs, cs), lambda i, sc_ref: (0, sc_ref[0], 0)))
        else:
            out_specs.append(pl.BlockSpec((1, rs, cs), lambda i, sc_ref: (0, 0, sc_ref[0])))
        out_shape.append(jax.ShapeDtypeStruct((1, R, C), BF))
    if has_small:
        in_specs.append(pl.BlockSpec((SMALL_ROWS, 256), lambda i, sc_ref: (0, 0)))
        out_specs.append(pl.BlockSpec((None, SMALL_ROWS, 256), lambda i, sc_ref: (sc_ref[0], 0, 0)))
        out_shape.append(jax.ShapeDtypeStruct((N_CHIPS, SMALL_ROWS, 256), F32))
        ops.append(small)
    return pl.pallas_call(
        body, name=f"{l}_place_own_shard",
        grid_spec=pltpu.PrefetchScalarGridSpec(num_scalar_prefetch=1, grid=(1,), in_specs=in_specs, out_specs=out_specs),
        out_shape=out_shape,
        compiler_params=_cp(("arbitrary",)),
    )(sc, *ops)


def _gather_start(l, names, shards, small, sc, after=None):
    nw = len(names)
    has_small = small is not None
    fulls = _place_own(l, names, shards, small, sc)
    bufs = list(shards) + ([small] if has_small else []) + list(fulls)
    n_src = nw + (1 if has_small else 0)

    def sends(refs):
        x, y, c, chips = _place()
        s = 2 * x + y
        out = []
        for (px, py) in chips:
            for wi, nm in enumerate(names):
                out.append((_shard_half(refs[wi], nm, c), _full_win(refs[n_src + wi], nm, s, c), (px, py, c)))
            if has_small:
                out.append((refs[nw], refs[n_src + nw].at[s], (px, py, c)))
        return out

    def arrivals(refs):
        x, y, c, chips = _place()
        out = []
        for (px, py) in chips:
            sp = 2 * px + py
            for wi, nm in enumerate(names):
                out.append(_full_win(refs[n_src + wi], nm, sp, c))
            if has_small:
                out.append(refs[n_src + nw].at[sp])
        return out

    n_copies = 3 * n_src
    send_sems, recv_sems, bufs, token = _split_start(f"{l}_gather_ici_start", bufs, n_copies, sends, after)
    return dict(l=l, names=names, has_small=has_small, sems=(send_sems, recv_sems), bufs=bufs, sends=sends,
                arrivals=arrivals, token=token)


def _gather_forward(st, after):
    l, names = st['l'], st['names']
    nw = len(names)
    n_src = nw + (1 if st['has_small'] else 0)
    bufs = _split_wait(f"{l}_gather_ici_wait", *st['sems'], st['bufs'], after, st['sends'], st['arrivals'])
    fulls = bufs[n_src:n_src + nw]
    small_all = bufs[n_src + nw] if st['has_small'] else None

    def sends(refs):
        x, y, c, chips = _place()
        out = []
        for (px, py) in chips:
            sp = 2 * px + py
            for wi, nm in enumerate(names):
                w = _full_win(refs[wi], nm, sp, c)
                out.append((w, w, (x, y, 1 - c)))
        return out

    def arrivals(refs):
        x, y, c, chips = _place()
        out = []
        for (px, py) in chips:
            sp = 2 * px + py
            for wi, nm in enumerate(names):
                out.append(_full_win(refs[wi], nm, sp, 1 - c))
        return out

    send_sems, recv_sems, fulls, token = _split_start(f"{l}_gather_d2d_start", fulls, 3 * nw, sends)
    return dict(l=l, names=names, sems=(send_sems, recv_sems), bufs=fulls, sends=sends, arrivals=arrivals,
                small_all=small_all, token=token)


def _gather_finish(st, after):
    fulls = _split_wait(f"{st['l']}_gather_d2d_wait", *st['sems'], st['bufs'], after, st['sends'], st['arrivals'])
    return dict(zip(st['names'], fulls)), st['small_all']


def _reduce_start(tag, names, grads):
    nw = len(names)
    recv = [lax.empty((1,) + _halves_shape(nm)[1:], F32) for nm in names]
    bufs = [grads[nm] for nm in names] + recv

    def windows(refs, half_of):
        x, y, c, _ = _place()
        h = half_of(c)
        out = []
        for wi, nm in enumerate(names):
            L, R, C, kind, rs, cs, rh = _geom(nm)
            if kind == 'row':
                for sp in range(N_CHIPS):
                    out.append((_full_win(refs[wi], nm, sp, h), _halves_win(refs[nw + wi], nm, sp)))
            else:
                out.append((refs[wi].at[:, pl.ds(h * rh, rh), :], refs[nw + wi]))
        return out

    def sends(refs):
        x, y, c, _ = _place()
        return [(src, dst, (x, y, 1 - c)) for src, dst in windows(refs, lambda c: 1 - c)]

    def arrivals(refs):
        return [dst for _, dst in windows(refs, lambda c: c)]

    n_copies = sum(N_CHIPS if BIG[nm][1] == 'row' else 1 for nm in names)
    send_sems, recv_sems, bufs, token = _split_start(tag + "_halves_start", bufs, n_copies, sends)
    return dict(tag=tag, names=names, sems=(send_sems, recv_sems), bufs=bufs, sends=sends, arrivals=arrivals, token=token)


def _reduce_mid(st, after, sc):
    tag, names = st['tag'], st['names']
    nw = len(names)
    bufs = _split_wait(tag + "_halves_wait", *st['sems'], st['bufs'], after, st['sends'], st['arrivals'])
    halves, own = [], []
    for wi, nm in enumerate(names):
        hb, ow = _add_halves(nm, bufs[wi], bufs[nw + wi], sc, tag)
        halves.append(hb)
        own.append(ow)
    pieces = [lax.empty((3, 1) + _half_shape(nm)[1:], BF) for nm in names]

    def sends(refs):
        x, y, c, chips = _place()
        out = []
        for j, (px, py) in enumerate(chips):
            for wi, nm in enumerate(names):
                out.append((_halves_win(refs[wi], nm, 2 * px + py), refs[nw + wi].at[j], (px, py, c)))
        return out

    def arrivals(refs):
        return [refs[nw + wi].at[j] for j in range(3) for wi in range(nw)]

    send_sems, recv_sems, bufs, token = _split_start(tag + "_pieces_start", halves + pieces, 3 * nw, sends)
    return dict(tag=tag, names=names, sems=(send_sems, recv_sems), bufs=bufs, sends=sends, arrivals=arrivals, own=own,
                token=token)


def _reduce_late(st, after, sc):
    tag, names = st['tag'], st['names']
    nw = len(names)
    bufs = _split_wait(tag + "_pieces_wait", *st['sems'], st['bufs'], after, st['sends'], st['arrivals'])
    gsh = [_sum_pieces(nm, st['own'][wi], bufs[nw + wi], sc, tag) for wi, nm in enumerate(names)]

    def sends(refs):
        x, y, c, _ = _place()
        return [(_shard_half(refs[wi], nm, c), _shard_half(refs[wi], nm, c), (x, y, 1 - c)) for wi, nm in enumerate(names)]

    def arrivals(refs):
        x, y, c, _ = _place()
        return [_shard_half(refs[wi], nm, 1 - c) for wi, nm in enumerate(names)]

    send_sems, recv_sems, bufs, token = _split_start(tag + "_share_start", gsh, nw, sends)
    return dict(tag=tag, names=names, sems=(send_sems, recv_sems), bufs=bufs, sends=sends, arrivals=arrivals, token=token)


def _reduce_finish(st, after):
    gsh = _split_wait(st['tag'] + "_share_wait", *st['sems'], st['bufs'], after, st['sends'], st['arrivals'])
    return dict(zip(st['names'], gsh))


def _add_halves(name, g, r, sc, tag):
    _, R, C, kind, rs, cs, rh = _geom(name)
    L = g.shape[0]
    tr = rh if kind == 'row' else 256
    nr = rh // tr

    def body(sc_ref, g_ref, r_ref, hb_ref, own_ref):
        sp = pl.program_id(2)
        tot = g_ref[...] + r_ref[...]
        hb_ref[...] = tot.astype(hb_ref.dtype)

        @pl.when(sp == sc_ref[0])
        def _():
            own_ref[...] = tot

    if kind == 'row':
        g_map = lambda l, ri, sp, sc_ref: (l, sp * 2 + sc_ref[1], 0)
        h_map = lambda l, ri, sp, sc_ref: (l, sp, 0)
    else:
        g_map = lambda l, ri, sp, sc_ref: (l, sc_ref[1] * nr + ri, sp)
        h_map = lambda l, ri, sp, sc_ref: (l, ri, sp)
    own_map = lambda l, ri, sp, sc_ref: (l, ri, 0)
    blk = (None, tr, cs)
    return pl.pallas_call(
        body, name=tag + "_add_halves_" + name,
        grid_spec=pltpu.PrefetchScalarGridSpec(
            num_scalar_prefetch=1, grid=(L, nr, N_CHIPS),
            in_specs=[pl.BlockSpec(blk, g_map), pl.BlockSpec(blk, h_map)],
            out_specs=[pl.BlockSpec(blk, h_map), pl.BlockSpec(blk, own_map)]),
        out_shape=[jax.ShapeDtypeStruct((L,) + _halves_shape(name)[1:], BF),
                   jax.ShapeDtypeStruct((L,) + _half_shape(name)[1:], F32)],
        compiler_params=_cp(("parallel", "parallel", "arbitrary")),
    )(sc, g, r)


def _sum_pieces(name, own, pieces, sc, tag):
    _, R, C, kind, rs, cs, rh = _geom(name)
    L = own.shape[0]
    tr = rh if kind == 'row' else 256
    nr = rh // tr

    def body(sc_ref, o_ref, p_ref, out_ref):
        out_ref[...] = o_ref[...] + p_ref[0].astype(F32) + p_ref[1].astype(F32) + p_ref[2].astype(F32)

    blk = (None, tr, cs)
    return pl.pallas_call(
        body, name=tag + "_sum_pieces_" + name,
        grid_spec=pltpu.PrefetchScalarGridSpec(
            num_scalar_prefetch=1, grid=(L, nr),
            in_specs=[pl.BlockSpec(blk, lambda l, ri, sc_ref: (l, ri, 0)),
                      pl.BlockSpec((3, None, tr, cs), lambda l, ri, sc_ref: (0, l, ri, 0))],
            out_specs=pl.BlockSpec(blk, lambda l, ri, sc_ref: (l, sc_ref[1] * nr + ri, 0))),
        out_shape=jax.ShapeDtypeStruct((L,) + _shard_shape(name)[1:], F32),
        compiler_params=_cp(("parallel", "parallel")),
    )(sc, own, pieces)


def _small_gather_start(v, sc):
    rows = v.shape[0]

    def place(sc_ref, v_ref, o_ref):
        o_ref[...] = v_ref[...]

    slots = pl.pallas_call(
        place, name="small_grads_place_own",
        grid_spec=pltpu.PrefetchScalarGridSpec(
            num_scalar_prefetch=1, grid=(1,),
            in_specs=[pl.BlockSpec((rows, 128), lambda i, sc_ref: (0, 0))],
            out_specs=pl.BlockSpec((None, rows, 128), lambda i, sc_ref: (2 * sc_ref[0] + sc_ref[1], 0, 0))),
        out_shape=jax.ShapeDtypeStruct((8, rows, 128), v.dtype),
        compiler_params=_cp(("arbitrary",)),
    )(sc, v)

    def peers():
        x, y, c, _ = _place()
        flips = [(fx, fy, fc) for fx in (0, 1) for fy in (0, 1) for fc in (0, 1)][1:]
        return [((1 - x if fx else x), (1 - y if fy else y), (1 - c if fc else c)) for fx, fy, fc in flips]

    def sends(refs):
        x, y, c, _ = _place()
        return [(refs[0], refs[1].at[4 * x + 2 * y + c], p) for p in peers()]

    def arrivals(refs):
        return [refs[1].at[4 * px + 2 * py + pc] for px, py, pc in peers()]

    send_sems, recv_sems, bufs, token = _split_start("small_grads_gather_start", [v, slots], 7, sends)
    return dict(sems=(send_sems, recv_sems), bufs=bufs, sends=sends, arrivals=arrivals, token=token)


def _small_gather_finish(st, after):
    return _split_wait("small_grads_gather_wait", *st['sems'], st['bufs'], after, st['sends'], st['arrivals'])[1]


def _sum8(v8, *, name, tr=336):
    rows = v8.shape[1]
    tr = min(tr, rows)
    assert rows % tr == 0

    def body(v_ref, o_ref):
        tot = v_ref[0].astype(F32)
        for d in range(1, 8):
            tot = tot + v_ref[d].astype(F32)
        o_ref[...] = tot

    return pl.pallas_call(
        body, name=name, grid=(rows // tr,),
        in_specs=[pl.BlockSpec((8, tr, 128), lambda i: (0, i, 0))], out_specs=pl.BlockSpec((tr, 128), lambda i: (i, 0)),
        out_shape=jax.ShapeDtypeStruct((rows, 128), F32),
        compiler_params=_cp(("parallel",)),
    )(v8)


def _block_diag(w_pool_l):
    wbd = jnp.zeros((MAIN_W, MAIN_W), F32)
    for gi in range(len(POOL_WINDOWS)):
        wbd = lax.dynamic_update_slice(wbd, w_pool_l[gi], (gi * POOL_GROUP, gi * POOL_GROUP))
    return wbd.astype(BF)


def _unpack_small(small_all):
    ng = small_all[:, :16, :].reshape(N_CHIPS, DEPTH, 4, 256).transpose(1, 2, 0, 3).reshape(DEPTH, 4, D_MODEL)
    ps = small_all[:, 16:18, :POOL_GROUP].transpose(1, 0, 2).reshape(N_A, MAIN_W)
    return ng, ps


def _local_step(x, mem, positions, on_forward, on_backward, mem_norm, w_pool, kv_norm, target):
    B, S, _ = x.shape
    T = B * S
    xc = x.reshape(T, D_MODEL)
    memf = mem.reshape(B * N_MEM, D_MODEL)
    tgt = target.reshape(T, D_MODEL)
    cos, sin = _rope_tables(positions.reshape(T, 1), name="rope_tables")
    wbd = [_block_diag(w_pool[l]) for l in range(N_A)]
    nbo = D_FF // 256
    fw = []
    rk = rv = None
    kv_saved = None
    wts = []
    norm_gains = pool_scale = y2 = None

    def tied(vec, tok):
        return vec if tok is None else vec + tok

    for l in range(DEPTH):
        t = f"l{l}_"
        got = on_forward('start', l, y2)
        wts.append(dict(got[0]))
        if l == 0:
            norm_gains, pool_scale = _unpack_small(got[1])
        sv = {'x_in': xc}
        h0, sv['r0'] = _norm_fwd(xc, tied(norm_gains[l, 0], got[2]), name=t + "norm0", out_dtype=BF, tm=1024)
        z, = _mm(h0, wts[l]['w_in'], b_layer=0, name=t + "mm_in", tm=1024, tn=1024)
        memn, sv['rm'] = _norm_fwd(memf, mem_norm[l], name=t + "norm_mem", out_dtype=BF, tm=256)
        kvm, = _mm(memn, wts[l]['w_mem_kv'], b_layer=0, name=t + "mm_memkv", out_dtypes=(BF,))
        if l < N_A:
            ycat, sv['p'] = _pool_fwd(z, wbd[l], pool_scale[l], B, S, name=t + "pool_fwd")
        else:
            rq = _rope_apply(z, cos, sin, name=t + "rope_q", out_dtype=F32)
            o = lax.empty((T, MAIN_W), F32)
            lse = lax.empty((T, MAIN_W), F32)
            for g in range(3):
                o, lse = _dil_fwd(g, rq, rk, rv, o, lse, B, S, name=t + f"dil_fwd{g}")
            ycat = _dil_combine_fwd(o, lse, lax.empty((T, D_MODEL), BF), name=t + "dil_combine")
            sv.update(rq=rq, o=o, lse=lse)
        ycat, sv['lse_m'] = _memattn_fwd(z, kvm, ycat, B, S, name=t + "memattn_fwd")
        tok = on_forward('mid', l, ycat)
        y1, = _mm(ycat, wts[l]['w_out'], b_layer=0, name=t + "mm_out", tm=1024, tn=1024)
        wts[l].update(on_forward('ffn', l, y1)[0])
        x1, sv['r1'] = _norm_fwd(y1, tied(norm_gains[l, 1], tok), name=t + "norm1", res=xc)
        h2, sv['r2'] = _norm_fwd(x1, norm_gains[l, 2], name=t + "norm2", out_dtype=BF, tm=1024)
        gg, uu, aa = _mm(h2, wts[l]['w_gate_up'], b_layer=0, b_offsets=(0, nbo), out_n=D_FF, tm=4096, tn=256, name=t + "mm_gate_up",
                         epilogue=_swiglu_fwd_epilogue, out_dtypes=(BF, BF, BF))
        on_forward('post', l, gg)
        y2, = _mm(aa, wts[l]['w_down'], b_layer=0, tk=D_FF, name=t + "mm_down")
        x2, sv['r3'] = _norm_fwd(y2, norm_gains[l, 3], name=t + "norm3", res=x1)
        sv.update(h0=h0, z=z, memn=memn, kvm=kvm, ycat=ycat, y1=y1, x1=x1, h2=h2, gg=gg, uu=uu, aa=aa, y2=y2)
        fw.append(sv)
        xc = x2
        if l == N_A - 1:
            kvn, rkv = _norm_fwd(xc, kv_norm, name="norm_kv", out_dtype=BF)
            kv, = _mm(kvn, wts[N_A - 1]['w_kv'], b_layer=0, name="mm_kv")
            rk, rv = _rope_apply(kv, cos, sin, name="rope_k", passthrough=True, out_dtype=F32)
            kv_saved = (xc, kvn, rkv)

    loss, dx = _loss(xc, tgt, name="loss")

    d_ng = [[None] * 4 for _ in range(DEPTH)]
    d_memnorm = [None] * DEPTH
    d_wbd = [None] * N_A
    d_pscale = [None] * N_A
    d_kvnorm = None
    kv_parts = []
    tok = None

    def as3d(gl):
        return {nm: g.reshape((1,) + g.shape) for nm, g in gl.items()}

    for l in reversed(range(DEPTH)):
        t = f"l{l}_b_"
        sv = fw[l]
        gl = {}
        dy2, d_ng[l][3] = _norm_bwd(dx, sv['y2'], sv['r3'], tied(norm_gains[l, 3], tok), name=t + "norm3", out_dtype=BF, tm=1024)
        gl['w_down'], = _mm(sv['aa'], dy2, ta=True, tm=1408, tn=512, tk=4096, name=t + "dw_down")
        dg, du = _mm(dy2, wts[l]['w_down'], tb=True, b_layer=0, tm=1024, tn=1408, name=t + "d_act",
                     extras=((sv['gg'], 'tile'), (sv['uu'], 'tile')), epilogue=_swiglu_bwd_epilogue, out_dtypes=(BF, BF))
        gl['w_gate_up'], = _mm(sv['h2'], (dg, du), ta=True, tn=1408, tk=1024, name=t + "dw_gate_up")
        dh2, = _mm((dg, du), wts[l]['w_gate_up'], tb=True, b_layer=0, tn=1024, tk=1408, name=t + "d_h2", out_dtypes=(BF,))
        dx1, d_ng[l][2] = _norm_bwd(dh2, sv['x1'], sv['r2'], norm_gains[l, 2], name=t + "norm2", add=dx, tm=1024)
        tok = on_backward('ffn', l, dx1, as3d(gl))
        dy1, d_ng[l][1] = _norm_bwd(dx1, sv['y1'], sv['r1'], tied(norm_gains[l, 1], tok), name=t + "norm1", out_dtype=BF, tm=1024)
        gl['w_out'], = _mm(sv['ycat'], dy1, ta=True, name=t + "dw_out", tk=4096)
        dycat, = _mm(dy1, wts[l]['w_out'], tb=True, b_layer=0, name=t + "d_ycat", tm=1024, tn=1024)
        dz = lax.empty((T, D_MODEL), BF)
        dz, dkm, dvm = _memattn_bwd(dycat, sv['z'], sv['kvm'], sv['lse_m'], dz, B, S, name=t + "memattn")
        if l < N_A:
            dz, d_wbd[l], d_pscale[l] = _pool_bwd(dycat, sv['p'], wbd[l], pool_scale[l], dz, B, S, name=t + "pool")
        else:
            do, cb = _dil_combine_bwd(dycat, sv['o'], sv['lse'], name=t + "dil_combine")
            acc = tuple(lax.empty((T, MAIN_W), F32) for _ in range(3))
            for g in range(3):
                acc = _dil_bwd(g, sv['rq'], rk, rv, do, cb, sv['lse'], acc, B, S, name=t + f"dil{g}")
            dz = _rope_apply(acc[0], cos, sin, name=t + "rope_q", sign=-1.0, alias=dz)
            kv_parts.append(acc[1:])
        tok = on_backward('mix', l, dz, as3d(gl))
        gl['w_in'], = _mm(sv['h0'], dz, ta=True, name=t + "dw_in", tk=4096)
        dh0, = _mm(dz, wts[l]['w_in'], tb=True, b_layer=0, name=t + "d_h0", out_dtypes=(BF,), tm=1024, tn=1024)
        dx, d_ng[l][0] = _norm_bwd(dh0, sv['x_in'], sv['r0'], tied(norm_gains[l, 0], tok), name=t + "norm0", add=dx1, tm=1024)
        gl['w_mem_kv'], = _mm(sv['memn'], (dkm, dvm), ta=True, tn=256, name=t + "dw_memkv")
        dmemn, = _mm((dkm, dvm), wts[l]['w_mem_kv'], tb=True, b_layer=0, tk=256, name=t + "d_memn", out_dtypes=(BF,))
        _, d_memnorm[l] = _norm_bwd(dmemn, memf, sv['rm'], mem_norm[l], name=t + "norm_mem", out_dtype=BF, tm=256)
        if l == N_A:
            dk, dv = _kv_grad_sum(kv_parts, cos, sin, name="kv_grad")
            x_kv, kvn, rkv = kv_saved
            gl['w_kv'], = _mm(kvn, (dk, dv), ta=True, tn=768, tk=2048, name="dw_kv")
            dkvn, = _mm((dk, dv), wts[N_A - 1]['w_kv'], tb=True, b_layer=0, tn=1024, tk=768, name="d_kvn", out_dtypes=(BF,))
            dx, d_kvnorm = _norm_bwd(dkvn, x_kv, rkv, kv_norm, name="norm_kv_b", add=dx)
        tok = on_backward('end', l, dx, as3d(gl))

    small = {
        'norm_gains': jnp.stack([jnp.concatenate(d_ng[l], axis=0) for l in range(DEPTH)]),
        'mem_norm': jnp.concatenate(d_memnorm, axis=0),
        'kv_norm': d_kvnorm.reshape(D_MODEL),
        'pool_scale': jnp.concatenate(d_pscale, axis=0),
        'w_pool': jnp.stack([jnp.stack([d_wbd[l][gi * POOL_GROUP:(gi + 1) * POOL_GROUP, gi * POOL_GROUP:(gi + 1) * POOL_GROUP]
                                        for gi in range(len(POOL_WINDOWS))]) for l in range(N_A)]),
    }
    return loss, dx, small


SMALL_ORDER = ('norm_gains', 'mem_norm', 'kv_norm', 'pool_scale', 'w_pool')
SMALL_VEC_ROWS = 2560


def kernel(x, mem, positions, norm_gains, mem_norm, w_in, w_mem_kv, w_out, w_pool, pool_scale, kv_norm, w_kv, w_gate_up, w_down, loss_target, m_norm_gains, m_mem_norm, m_w_in, m_w_mem_kv, m_w_out, m_w_pool, m_pool_scale, m_kv_norm, m_w_kv, m_w_gate_up, m_w_down, v_norm_gains, v_mem_norm, v_w_in, v_w_mem_kv, v_w_out, v_w_pool, v_pool_scale, v_kv_norm, v_w_kv, v_w_gate_up, v_w_down):
    xi, yi, ci = lax.axis_index("x"), lax.axis_index("y"), lax.axis_index("c")
    s = 2 * xi + yi
    sc = jnp.stack([s, ci]).astype(jnp.int32)
    weights = dict(norm_gains=norm_gains, mem_norm=mem_norm, w_in=w_in, w_mem_kv=w_mem_kv, w_out=w_out, w_pool=w_pool,
                   pool_scale=pool_scale, kv_norm=kv_norm, w_kv=w_kv, w_gate_up=w_gate_up, w_down=w_down)
    moms = dict(norm_gains=m_norm_gains, mem_norm=m_mem_norm, w_in=m_w_in, w_mem_kv=m_w_mem_kv, w_out=m_w_out,
                w_pool=m_w_pool, pool_scale=m_pool_scale, kv_norm=m_kv_norm, w_kv=m_w_kv, w_gate_up=m_w_gate_up,
                w_down=m_w_down)
    vels = dict(norm_gains=v_norm_gains, mem_norm=v_mem_norm, w_in=v_w_in, w_mem_kv=v_w_mem_kv, w_out=v_w_out,
                w_pool=v_w_pool, pool_scale=v_pool_scale, kv_norm=v_kv_norm, w_kv=v_w_kv, w_gate_up=v_w_gate_up,
                w_down=v_w_down)

    small_w = jnp.zeros((SMALL_ROWS, 256), F32)
    small_w = lax.dynamic_update_slice(small_w, norm_gains.reshape(16, 256), (0, 0))
    small_w = lax.dynamic_update_slice(small_w, pool_scale, (16, 0))
    def shard_of(nm, l):
        return w_kv.astype(BF).reshape(_shard_shape('w_kv')) if nm == 'w_kv' else weights[nm][l:l + 1].astype(BF)

    groups = {'l0a': (0, MIX_W), 'l0b': (0, FFN_W)}
    groups.update({f"l{l}": (l, LAYER_W + (('w_kv',) if l == N_A - 1 else ())) for l in range(1, DEPTH)})
    on_ici, on_d2d, gathered = {}, {}, {}

    def start_group(tag, after):
        l, names = groups[tag]
        on_ici[tag] = _gather_start(tag, names, [shard_of(nm, l) for nm in names], small_w if tag == 'l0a' else None, sc,
                                    after)
        return on_ici[tag]['token'][0, 0]

    def on_forward(where, l, after):
        if where == 'start':
            if l == 0:
                start_group('l0a', None)
                st = on_ici.pop('l0a')
                fwd = _gather_forward(st, st['token'])
                w, small_all = _gather_finish(fwd, fwd['token'])
                return w, small_all, start_group('l0b', w['w_in'])
            if f"l{l}" not in on_d2d:
                on_d2d[f"l{l}"] = _gather_forward(on_ici.pop(f"l{l}"), after)
            gathered[l] = _gather_finish(on_d2d.pop(f"l{l}"), after)[0]
            tok = start_group(f"l{l + 1}", gathered[l]['w_in']) if l + 1 < DEPTH else None
            return {nm: w for nm, w in gathered[l].items() if nm not in FFN_W}, None, tok
        if where == 'mid' and l == 0:
            on_d2d['l0b'] = _gather_forward(on_ici.pop('l0b'), after)
            return start_group('l1', on_d2d['l0b']['token'])
        if where == 'ffn':
            if l == 0:
                return (_gather_finish(on_d2d.pop('l0b'), after)[0],)
            return ({nm: gathered[l][nm] for nm in FFN_W},)
        if where == 'post' and 0 < l < DEPTH - 1:
            on_d2d[f"l{l + 1}"] = _gather_forward(on_ici.pop(f"l{l + 1}"), after)
        return None

    hook_of = {'ffn': 0, 'mix': 1, 'end': 2}
    active, reduced = [], {l: {} for l in range(DEPTH)}
    advance = {'mid': lambda st, after: _reduce_mid(st, after, sc), 'late': lambda st, after: _reduce_late(st, after, sc)}

    def run_hook(idx, after):
        toks = []
        for grp in list(active):
            while grp['plan'] and grp['plan'][0][1] <= idx:
                step = grp['plan'].pop(0)[0]
                if step == 'finish':
                    reduced[grp['layer']].update(_reduce_finish(grp['st'], after))
                    active.remove(grp)
                else:
                    grp['st'] = advance[step](grp['st'], after)
                    toks.append(grp['st']['token'][0, 0])
        return toks

    def on_backward(where, l, after, grads):
        idx = 3 * (DEPTH - 1 - l) + hook_of[where]
        toks = run_hook(idx, after)
        if where == 'end' or (where == 'ffn' and l == 0):
            names = FFN_W if where == 'ffn' else tuple(nm for nm in grads if l > 0 or nm not in FFN_W)
            st = _reduce_start(f"l{l}_{where}_grads", names, {nm: grads[nm] for nm in names})
            plan = [('mid', idx + 1), ('late', idx + 3), ('finish', idx + 4)] if where == 'ffn' else \
                   [('mid', idx + 1), ('late', idx + 2), ('finish', idx + 3)]
            active.append(dict(layer=l, st=st, plan=plan))
            toks.append(st['token'][0, 0])
        return sum(toks) if toks else None

    loss, gx, gsmall = _local_step(x, mem, positions, on_forward, on_backward, mem_norm, w_pool, kv_norm, loss_target)
    loss = lax.psum(loss[0, 0], ("x", "y", "c"))

    vec = jnp.concatenate([gsmall[nm].reshape(-1) for nm in SMALL_ORDER])
    vec = jnp.pad(vec, (0, SMALL_VEC_ROWS * 128 - vec.shape[0])).reshape(SMALL_VEC_ROWS, 128)
    vec = vec + sum(grp['st']['token'][0, 0] for grp in active)
    small_st = _small_gather_start(vec.astype(BF), sc)
    outs = {nm: None for nm in LAYER_W}

    def adamw_layers(layers, names, after):
        for l in layers:
            for nm in names:
                outs[nm] = _adamw_layer(nm, l, weights[nm], reduced[l][nm], moms[nm], vels[nm], outs[nm], after)
                after = outs[nm][0]
        return after

    def zero_of(toks, st):
        return jnp.full((8, 128), sum(toks)) if toks else st['token']

    last = 3 * DEPTH
    toks = run_hook(last, small_st['token'])
    done = adamw_layers(range(DEPTH - 1, 0, -1), LAYER_W, zero_of(toks, small_st))
    toks = run_hook(last + 1, done)
    done = adamw_layers([0], FFN_W, zero_of(toks, small_st))
    tot = _sum8(_small_gather_finish(small_st, done), name="sum_small_grads", tr=512)
    run_hook(last + 2, tot)
    assert not active
    adamw_layers([0], MIX_W, None)
    tot = tot.reshape(-1)
    grads, off = {}, 0
    for nm in SMALL_ORDER:
        shape = (DEPTH, 4, D_MODEL) if nm == 'norm_gains' else (N_A, MAIN_W) if nm == 'pool_scale' else weights[nm].shape
        n = 1
        for dim in shape:
            n *= dim
        grads[nm] = tot[off:off + n].reshape(shape)
        off += n
    grads['norm_gains'] = lax.dynamic_slice(grads['norm_gains'], (0, 0, s * 256), (DEPTH, 4, 256))
    grads['pool_scale'] = lax.dynamic_slice(grads['pool_scale'], (0, s * POOL_GROUP), (N_A, POOL_GROUP))
    grads['w_kv'] = reduced[N_A]['w_kv'].reshape(w_kv.shape)

    order = ('norm_gains', 'mem_norm', 'w_in', 'w_mem_kv', 'w_out', 'w_pool', 'pool_scale', 'kv_norm', 'w_kv',
             'w_gate_up', 'w_down')
    deltas, new_m, new_v = {}, {}, {}
    for nm in order:
        if nm in LAYER_W:
            deltas[nm], new_m[nm], new_v[nm], grads[nm] = outs[nm]
        else:
            deltas[nm], new_m[nm], new_v[nm] = _adamw(weights[nm], grads[nm], moms[nm], vels[nm], name="adamw_" + nm)
    return (loss, gx.reshape(x.shape), *[grads[nm] for nm in order], *[deltas[nm] for nm in order],
            *[new_m[nm] for nm in order], *[new_v[nm] for nm in order])
```

```python
import functools

import jax
import jax.numpy as jnp
from jax import lax
from jax.experimental import pallas as pl
from jax.experimental.pallas import tpu as pltpu

F32 = jnp.float32
BF = jnp.bfloat16

D_MODEL = 1024
DEPTH = 4
N_A = 2
HEAD_DIM = 64
MEM_W = 256
MAIN_W = 768
D_FF = 2816
N_MEM = 256
POOL_WINDOWS = (2, 4, 8, 16)
POOL_GROUP = 192
DIL = (1, 4, 16)
STEPS = 128
ROPE_THETA = 10000.0
EPS = 1e-6
SCALE = HEAD_DIM ** -0.5
NEG = -1e30

ADAM_LR = 0.001
ADAM_B1 = 0.9
ADAM_B2 = 0.999
ADAM_EPS = 1e-08
ADAM_WD = 0.01
ADAM_STEP = 10

VMEM_LIMIT = 48 * 1024 * 1024
MESH = pl.DeviceIdType.MESH


def _cp(sem):
    return pltpu.CompilerParams(dimension_semantics=sem, vmem_limit_bytes=VMEM_LIMIT)


def _mm(a, b, *, name, ta=False, tb=False, tm=1024, tn=512, tk=1024, b_layer=None, b_offsets=(0,),
        extras=(), epilogue=None, out_dtypes=(F32,), out_n=None, stack=None):
    a_pair = isinstance(a, (tuple, list))
    b_pair = isinstance(b, (tuple, list))
    a0 = a[0] if a_pair else a
    b0 = b[0] if b_pair else b
    a_rows, a_cols = a0.shape
    if a_pair:
        a_cols *= 2
    b_rows, b_cols = b0.shape[-2:]
    if b_pair:
        b_cols *= 2
    M, K = (a_cols, a_rows) if ta else (a_rows, a_cols)
    N = b_rows if tb else b_cols
    if out_n is not None:
        N = out_n
    tm, tn, tk = min(tm, M), min(tn, N), min(tk, K)
    assert M % tm == 0 and N % tn == 0 and K % tk == 0, (name, M, N, K, tm, tn, tk)
    nk = K // tk
    n_acc = len(b_offsets)

    if a_pair:
        a_half = (a0.shape[1] // (tm if ta else tk))
    if b_pair:
        b_half = (b0.shape[1] // (tk if tb else tn))

    def a_map(sel):
        def f(i, j, k):
            r, c = (k, i) if ta else (i, k)
            if a_pair:
                c = jnp.clip(c - sel * a_half, 0, a_half - 1)
            return (r, c)
        return f

    def b_map(sel, off):
        def f(i, j, k):
            r, c = (j + off, k) if tb else (k, j + off)
            if b_pair:
                c = jnp.clip(c - sel * b_half, 0, b_half - 1)
            if b_layer is not None:
                return (b_layer, r, c)
            return (r, c)
        return f

    a_blk = (tk, tm) if ta else (tm, tk)
    b_blk = (tn, tk) if tb else (tk, tn)
    if b_layer is not None:
        b_blk = (None,) + b_blk
    in_specs, operands = [], []
    for sel in range(2 if a_pair else 1):
        in_specs.append(pl.BlockSpec(a_blk, a_map(sel)))
        operands.append(a[sel] if a_pair else a)
    n_a = len(operands)
    for off in b_offsets:
        for sel in range(2 if b_pair else 1):
            in_specs.append(pl.BlockSpec(b_blk, b_map(sel, off)))
            operands.append(b[sel] if b_pair else b)
    n_b = len(operands) - n_a
    for arr, kind in extras:
        if kind == 'tile':
            in_specs.append(pl.BlockSpec((tm, tn), lambda i, j, k: (i, j)))
        elif kind == 'row':
            in_specs.append(pl.BlockSpec((tm, 1), lambda i, j, k: (i, 0)))
        else:
            in_specs.append(pl.BlockSpec((1, tn), lambda i, j, k: (0, j)))
        operands.append(arr)
    n_e = len(extras)
    n_o = len(out_dtypes)
    dims = (((0,) if ta else (1,), (1,) if tb else (0,)), ((), ()))

    def body(*refs):
        a_refs = refs[:n_a]
        b_refs = refs[n_a:n_a + n_b]
        e_refs = refs[n_a + n_b:n_a + n_b + n_e]
        n_in = n_a + n_b + n_e + (1 if stack is not None else 0)
        o_refs = refs[n_in:n_in + n_o]
        acc_refs = refs[n_in + n_o:]
        i, j, k = pl.program_id(0), pl.program_id(1), pl.program_id(2)
        if a_pair:
            cidx = i if ta else k
            av = jnp.where(cidx < a_half, a_refs[0][...], a_refs[1][...])
        else:
            av = a_refs[0][...]
        av = av.astype(BF)
        prods = []
        for q in range(n_acc):
            if b_pair:
                cidx = (k if tb else j) + b_offsets[q]
                bv = jnp.where(cidx < b_half, b_refs[2 * q][...], b_refs[2 * q + 1][...])
            else:
                bv = b_refs[q][...]
            prods.append(lax.dot_general(av, bv.astype(BF), dims, preferred_element_type=F32))

        def finish(accs):
            outs = epilogue(accs, *[r[...] for r in e_refs]) if epilogue is not None else accs
            for o_ref, o in zip(o_refs, outs):
                o_ref[...] = o.astype(o_ref.dtype)

        if nk == 1:
            finish(prods)
        else:
            @pl.when(k == 0)
            def _():
                for r, p in zip(acc_refs, prods):
                    r[...] = p

            @pl.when(k > 0)
            def _():
                for r, p in zip(acc_refs, prods):
                    r[...] += p

            @pl.when(k == nk - 1)
            def _():
                finish([r[...] for r in acc_refs])

    if stack is not None:
        buf, layer = stack
        assert n_o == 1 and buf.shape[1:] == (M, N)
        return pl.pallas_call(
            body, name=name,
            grid=(M // tm, N // tn, nk),
            in_specs=in_specs + [pl.BlockSpec(memory_space=pl.ANY)],
            out_specs=[pl.BlockSpec((None, tm, tn), lambda i, j, k: (layer, i, j))],
            out_shape=[jax.ShapeDtypeStruct(buf.shape, buf.dtype)],
            scratch_shapes=[pltpu.VMEM((tm, tn), F32) for _ in range(n_acc if nk > 1 else 0)],
            input_output_aliases={len(operands): 0},
            compiler_params=_cp(("parallel", "parallel", "arbitrary")),
        )(*operands, buf)[0]
    return pl.pallas_call(
        body, name=name,
        grid=(M // tm, N // tn, nk),
        in_specs=in_specs,
        out_specs=[pl.BlockSpec((tm, tn), lambda i, j, k: (i, j)) for _ in range(n_o)],
        out_shape=[jax.ShapeDtypeStruct((M, N), dt) for dt in out_dtypes],
        scratch_shapes=[pltpu.VMEM((tm, tn), F32) for _ in range(n_acc if nk > 1 else 0)],
        compiler_params=_cp(("parallel", "parallel", "arbitrary")),
    )(*operands)


def _norm_fwd(x, g, *, name, res=None, out_dtype=F32, tm=512, after=None):
    T, Dm = x.shape
    has_res = res is not None
    after = list(after or [])

    def body(*refs):
        refs = refs[:len(refs) - 2 - len(after)] + refs[len(refs) - 2:]
        if has_res:
            x_ref, g_ref, r_ref, y_ref, s_ref = refs
        else:
            x_ref, g_ref, y_ref, s_ref = refs
        xv = x_ref[...]
        rstd = lax.rsqrt(jnp.mean(xv * xv, axis=-1, keepdims=True) + EPS)
        y = xv * rstd * g_ref[...]
        if has_res:
            y = r_ref[...] + y
        y_ref[...] = y.astype(y_ref.dtype)
        s_ref[...] = rstd

    row = pl.BlockSpec((tm, Dm), lambda i: (i, 0))
    in_specs = [row, pl.BlockSpec((1, Dm), lambda i: (0, 0))] + ([row] if has_res else [])
    in_specs += [pl.BlockSpec(memory_space=pl.ANY)] * len(after)
    ops = [x, g.reshape(1, Dm)] + ([res] if has_res else []) + after
    return pl.pallas_call(
        body, name=name, grid=(T // tm,), in_specs=in_specs,
        out_specs=[row, pl.BlockSpec((tm, 1), lambda i: (i, 0))],
        out_shape=[jax.ShapeDtypeStruct((T, Dm), out_dtype), jax.ShapeDtypeStruct((T, 1), F32)],
        compiler_params=_cp(("parallel",)),
    )(*ops)


def _norm_bwd(dout, x, rstd, g, *, name, add=None, out_dtype=F32, tm=512, after=None):
    T, Dm = x.shape
    has_add = add is not None
    nt = T // tm
    after = list(after or [])

    def body(*refs):
        refs = refs[:len(refs) - 3 - len(after)] + refs[len(refs) - 3:]
        if has_add:
            do_ref, x_ref, s_ref, g_ref, a_ref, dx_ref, dg_ref, acc = refs
        else:
            do_ref, x_ref, s_ref, g_ref, dx_ref, dg_ref, acc = refs
        i = pl.program_id(0)
        do = do_ref[...].astype(F32)
        xh = x_ref[...] * s_ref[...]
        gd = do * g_ref[...]
        dx = s_ref[...] * (gd - xh * jnp.mean(gd * xh, axis=-1, keepdims=True))
        if has_add:
            dx = dx + a_ref[...].astype(F32)
        dx_ref[...] = dx.astype(dx_ref.dtype)
        part = jnp.sum((do * xh).reshape(tm // 8, 8, Dm), axis=0)

        @pl.when(i == 0)
        def _():
            acc[...] = part

        @pl.when(i > 0)
        def _():
            acc[...] += part

        @pl.when(i == nt - 1)
        def _():
            dg_ref[...] = jnp.sum(acc[...], axis=0, keepdims=True)

    row = pl.BlockSpec((tm, Dm), lambda i: (i, 0))
    in_specs = [row, row, pl.BlockSpec((tm, 1), lambda i: (i, 0)), pl.BlockSpec((1, Dm), lambda i: (0, 0))]
    ops = [dout, x, rstd, g.reshape(1, Dm)]
    if has_add:
        in_specs.append(row)
        ops.append(add)
    in_specs += [pl.BlockSpec(memory_space=pl.ANY)] * len(after)
    ops += after
    return pl.pallas_call(
        body, name=name, grid=(nt,), in_specs=in_specs,
        out_specs=[row, pl.BlockSpec((1, Dm), lambda i: (0, 0))],
        out_shape=[jax.ShapeDtypeStruct((T, Dm), out_dtype), jax.ShapeDtypeStruct((1, Dm), F32)],
        scratch_shapes=[pltpu.VMEM((8, Dm), F32)],
        compiler_params=_cp(("arbitrary",)),
    )(*ops)


def _swiglu_fwd_epilogue(accs):
    g, u = accs
    return g, u, g * jax.nn.sigmoid(g) * u


def _swiglu_bwd_epilogue(accs, g, u):
    da = accs[0]
    g = g.astype(F32)
    u = u.astype(F32)
    sig = jax.nn.sigmoid(g)
    return da * u * (sig * (1.0 + g * (1.0 - sig))), da * (g * sig)


def _rope_tables(pos, *, name, tm=1024):
    T = pos.shape[0]
    half = HEAD_DIM // 2
    freqs = ROPE_THETA ** (-jnp.arange(half, dtype=F32) / half)
    freqs = jnp.tile(freqs, 4).reshape(1, 128)

    def body(p_ref, f_ref, c_ref, s_ref):
        ang = p_ref[...].astype(F32) * f_ref[...]
        lane = lax.broadcasted_iota(jnp.int32, ang.shape, 1)
        c_ref[...] = jnp.cos(ang)
        s_ref[...] = jnp.where(lane % HEAD_DIM < half, -1.0, 1.0) * jnp.sin(ang)

    tab = pl.BlockSpec((tm, 128), lambda i: (i, 0))
    return pl.pallas_call(
        body, name=name, grid=(T // tm,),
        in_specs=[pl.BlockSpec((tm, 1), lambda i: (i, 0)), pl.BlockSpec((1, 128), lambda i: (0, 0))],
        out_specs=[tab, tab],
        out_shape=[jax.ShapeDtypeStruct((T, 128), F32)] * 2,
        compiler_params=_cp(("parallel",)),
    )(pos, freqs)


def _rot(x, cos, sin, sign):
    W = x.shape[1]
    half = HEAD_DIM // 2
    reps = W // 128
    c = jnp.concatenate([cos] * reps, axis=1) if reps > 1 else cos
    s = jnp.concatenate([sin] * reps, axis=1) if reps > 1 else sin
    lane = lax.broadcasted_iota(jnp.int32, x.shape, 1)
    swapped = jnp.where(lane % HEAD_DIM < half, pltpu.roll(x, W - half, axis=1), pltpu.roll(x, half, axis=1))
    return x * c + (sign * s) * swapped


def _rope_apply(x, cos, sin, *, name, sign=1.0, width=MAIN_W, passthrough=False, out_dtype=BF, alias=None,
                out_cols=None, tm=512):
    T = x.shape[0]

    def body(*refs):
        if passthrough:
            x_ref, v_ref, c_ref, s_ref, o_ref, ov_ref = refs
            ov_ref[...] = v_ref[...].astype(ov_ref.dtype)
        elif alias is not None:
            x_ref, c_ref, s_ref, _, o_ref = refs
        else:
            x_ref, c_ref, s_ref, o_ref = refs
        o_ref[...] = _rot(x_ref[...].astype(F32), c_ref[...], s_ref[...], sign).astype(o_ref.dtype)

    blk0 = pl.BlockSpec((tm, width), lambda i: (i, 0))
    blk1 = pl.BlockSpec((tm, width), lambda i: (i, 1))
    tab = pl.BlockSpec((tm, 128), lambda i: (i, 0))
    if passthrough:
        return pl.pallas_call(
            body, name=name, grid=(T // tm,), in_specs=[blk0, blk1, tab, tab], out_specs=[blk0, blk0],
            out_shape=[jax.ShapeDtypeStruct((T, width), out_dtype)] * 2,
            compiler_params=_cp(("parallel",)),
        )(x, x, cos, sin)
    if alias is not None:
        return pl.pallas_call(
            body, name=name, grid=(T // tm,),
            in_specs=[blk0, tab, tab, pl.BlockSpec(memory_space=pl.ANY)], out_specs=blk0,
            out_shape=jax.ShapeDtypeStruct(alias.shape, alias.dtype),
            input_output_aliases={3: 0},
            compiler_params=_cp(("parallel",)),
        )(x, cos, sin, alias)
    return pl.pallas_call(
        body, name=name, grid=(T // tm,), in_specs=[blk0, tab, tab], out_specs=blk0,
        out_shape=jax.ShapeDtypeStruct((T, width), out_dtype),
        compiler_params=_cp(("parallel",)),
    )(x, cos, sin)


POOL_T = 256
POOL_HALO = 16


def _pool_lane_window(shape):
    lane = lax.broadcasted_iota(jnp.int32, shape, 1)
    w = jnp.full(shape, POOL_WINDOWS[0], jnp.int32)
    for gi in range(1, len(POOL_WINDOWS)):
        w = jnp.where(lane >= gi * POOL_GROUP, POOL_WINDOWS[gi], w)
    return w


POOL_PAD = 32
POOL_R = POOL_T + POOL_PAD


def _pool_window_sums(buf, tmp_a, tmp_b, win, back):
    src, dst, acc = buf, tmp_a, None
    for j, (w, sh) in enumerate(zip(POOL_WINDOWS, (1, 2, 4, 8)), start=1):
        n = POOL_R - 8 * j
        if back:
            dst[pl.ds(8 * j, n), :] = src[pl.ds(8 * j, n), :] + src[pl.ds(8 * j - sh, n), :]
            cur = dst[pl.ds(POOL_PAD, POOL_T), :]
        else:
            dst[pl.ds(0, n), :] = src[pl.ds(0, n), :] + src[pl.ds(sh, n), :]
            cur = dst[pl.ds(0, POOL_T), :]
        acc = cur if acc is None else jnp.where(win >= w, cur, acc)
        src, dst = dst, (tmp_b if dst is tmp_a else tmp_a)
    return acc


def _pool_fwd(z, wbd, scale, B, S, *, name):
    T = z.shape[0]
    nt = S // POOL_T
    hb = POOL_T // POOL_PAD

    def body(z_ref, h_ref, w_ref, sc_ref, y_ref, p_ref, ext, tmp_a, tmp_b):
        i = pl.program_id(1)
        u = z_ref[...]
        ext[pl.ds(POOL_PAD, POOL_T), :] = u
        ext[pl.ds(0, POOL_PAD), :] = jnp.where(i > 0, h_ref[...], 0.0)
        win = _pool_lane_window((POOL_T, MAIN_W))
        acc = _pool_window_sums(ext, tmp_a, tmp_b, win, True)
        t = i * POOL_T + lax.broadcasted_iota(jnp.int32, (POOL_T, MAIN_W), 0)
        cnt = jnp.minimum(t + 1, win).astype(F32)
        p = (acc / cnt - u).astype(BF)
        p_ref[...] = p
        y = jnp.dot(p, w_ref[...], preferred_element_type=F32) * sc_ref[...]
        y_ref[...] = y.astype(y_ref.dtype)

    return pl.pallas_call(
        body, name=name, grid=(B, nt),
        in_specs=[pl.BlockSpec((POOL_T, MAIN_W), lambda b, i: (b * nt + i, 0)),
                  pl.BlockSpec((POOL_PAD, MAIN_W), lambda b, i: (jnp.maximum((b * nt + i) * hb - 1, 0), 0)),
                  pl.BlockSpec((MAIN_W, MAIN_W), lambda b, i: (0, 0)),
                  pl.BlockSpec((1, MAIN_W), lambda b, i: (0, 0))],
        out_specs=[pl.BlockSpec((POOL_T, MAIN_W), lambda b, i: (b * nt + i, 0)),
                   pl.BlockSpec((POOL_T, MAIN_W), lambda b, i: (b * nt + i, 0))],
        out_shape=[jax.ShapeDtypeStruct((T, D_MODEL), BF), jax.ShapeDtypeStruct((T, MAIN_W), BF)],
        scratch_shapes=[pltpu.VMEM((POOL_R, MAIN_W), F32)] * 3,
        compiler_params=_cp(("parallel", "parallel")),
    )(z, z, wbd, scale.reshape(1, MAIN_W))


def _pool_bwd(dy, p, wbd, scale, dz_alias, B, S, *, name):
    T = dy.shape[0]
    nt = S // POOL_T
    hb = POOL_T // POOL_HALO
    last_halo = T // POOL_HALO - 1

    def body(dy_ref, dyn_ref, p_ref, pn_ref, w_ref, sc_ref, _, dz_ref, dw_ref, ds_ref, ext, tmp_a, tmp_b, dw_acc, ds_acc):
        b, i = pl.program_id(0), pl.program_id(1)
        first = jnp.logical_and(b == 0, i == 0)
        dyv = dy_ref[...]
        pv = p_ref[...]
        sc = sc_ref[...]
        w = w_ref[...]
        pw = jnp.dot(pv, w, preferred_element_type=F32)
        ds_part = jnp.sum((dyv * pw).reshape(POOL_T // 8, 8, MAIN_W), axis=0)
        dpw = (dyv * sc).astype(BF)
        dw_part = lax.dot_general(pv, dpw, (((0,), (0,)), ((), ())), preferred_element_type=F32)

        @pl.when(first)
        def _():
            dw_acc[...] = dw_part
            ds_acc[...] = ds_part

        @pl.when(jnp.logical_not(first))
        def _():
            dw_acc[...] += dw_part
            ds_acc[...] += ds_part

        @pl.when(jnp.logical_and(b == pl.num_programs(0) - 1, i == nt - 1))
        def _():
            dw_ref[...] = dw_acc[...]
            ds_ref[...] = jnp.sum(ds_acc[...], axis=0, keepdims=True)

        dp = lax.dot_general(dpw, w, (((1,), (1,)), ((), ())), preferred_element_type=F32)
        dpn = lax.dot_general((dyn_ref[...] * sc).astype(BF), w, (((1,), (1,)), ((), ())), preferred_element_type=F32)
        win = _pool_lane_window((POOL_T, MAIN_W))
        win_n = _pool_lane_window((POOL_HALO, MAIN_W))
        t = i * POOL_T + lax.broadcasted_iota(jnp.int32, (POOL_T, MAIN_W), 0)
        tn = (i + 1) * POOL_T + lax.broadcasted_iota(jnp.int32, (POOL_HALO, MAIN_W), 0)
        ext[pl.ds(0, POOL_T), :] = dp / jnp.minimum(t + 1, win).astype(F32)
        ext[pl.ds(POOL_T, POOL_HALO), :] = jnp.where(i < nt - 1, dpn / jnp.minimum(tn + 1, win_n).astype(F32), 0.0)
        ext[pl.ds(POOL_T + POOL_HALO, POOL_PAD - POOL_HALO), :] = jnp.zeros((POOL_PAD - POOL_HALO, MAIN_W), F32)
        acc = _pool_window_sums(ext, tmp_a, tmp_b, win, False) - dp
        dz_ref[...] = acc.astype(dz_ref.dtype)

    cur = lambda b, i: (b * nt + i, 0)
    nxt = lambda b, i: (jnp.minimum((b * nt + i + 1) * hb, last_halo), 0)
    return pl.pallas_call(
        body, name=name, grid=(B, nt),
        in_specs=[pl.BlockSpec((POOL_T, MAIN_W), cur), pl.BlockSpec((POOL_HALO, MAIN_W), nxt),
                  pl.BlockSpec((POOL_T, MAIN_W), cur), pl.BlockSpec((POOL_HALO, MAIN_W), nxt),
                  pl.BlockSpec((MAIN_W, MAIN_W), lambda b, i: (0, 0)),
                  pl.BlockSpec((1, MAIN_W), lambda b, i: (0, 0)),
                  pl.BlockSpec(memory_space=pl.ANY)],
        out_specs=[pl.BlockSpec((POOL_T, MAIN_W), cur),
                   pl.BlockSpec((MAIN_W, MAIN_W), lambda b, i: (0, 0)),
                   pl.BlockSpec((1, MAIN_W), lambda b, i: (0, 0))],
        out_shape=[jax.ShapeDtypeStruct(dz_alias.shape, dz_alias.dtype),
                   jax.ShapeDtypeStruct((MAIN_W, MAIN_W), F32), jax.ShapeDtypeStruct((1, MAIN_W), F32)],
        scratch_shapes=[pltpu.VMEM((POOL_R, MAIN_W), F32)] * 3 + [pltpu.VMEM((MAIN_W, MAIN_W), F32), pltpu.VMEM((8, MAIN_W), F32)],
        input_output_aliases={6: 0},
        compiler_params=_cp(("arbitrary", "arbitrary")),
    )(dy, dy, p, p, wbd, scale.reshape(1, MAIN_W), dz_alias)


def _head_masks(shape):
    lane = lax.broadcasted_iota(jnp.int32, shape, 1)
    return [(lane // HEAD_DIM) == h for h in range(shape[1] // HEAD_DIM)]


def _row_of(bcast, mask):
    return jnp.max(jnp.where(mask, bcast, -jnp.inf), axis=-1, keepdims=True)


MEM_TQ = 512


def _memattn_fwd(z, kv, y_alias, B, S, *, name):
    T = z.shape[0]
    nt = S // MEM_TQ

    def body(q_ref, k_ref, v_ref, _, y_ref, l_ref):
        q = q_ref[...]
        k = k_ref[...]
        v = v_ref[...]
        masks = _head_masks(q.shape)
        o = jnp.zeros(q.shape, F32)
        lse_b = jnp.zeros(q.shape, F32)
        for m in masks:
            qm = jnp.where(m, q, 0.0).astype(BF)
            s = lax.dot_general(qm, k, (((1,), (1,)), ((), ())), preferred_element_type=F32) * SCALE
            mx = jnp.max(s, axis=-1, keepdims=True)
            e = jnp.exp(s - mx)
            l = jnp.sum(e, axis=-1, keepdims=True)
            p = (e / l).astype(BF)
            o = o + jnp.where(m, jnp.dot(p, v, preferred_element_type=F32), 0.0)
            lse_b = lse_b + jnp.where(m, mx + jnp.log(l), 0.0)
        y_ref[...] = o.astype(y_ref.dtype)
        l_ref[...] = lse_b

    qblk = pl.BlockSpec((MEM_TQ, MEM_W), lambda b, i: (b * nt + i, 3))
    return pl.pallas_call(
        body, name=name, grid=(B, nt),
        in_specs=[qblk, pl.BlockSpec((N_MEM, MEM_W), lambda b, i: (b, 0)), pl.BlockSpec((N_MEM, MEM_W), lambda b, i: (b, 1)),
                  pl.BlockSpec(memory_space=pl.ANY)],
        out_specs=[qblk, pl.BlockSpec((MEM_TQ, MEM_W), lambda b, i: (b * nt + i, 0))],
        out_shape=[jax.ShapeDtypeStruct(y_alias.shape, y_alias.dtype), jax.ShapeDtypeStruct((T, MEM_W), F32)],
        input_output_aliases={3: 0},
        compiler_params=_cp(("parallel", "parallel")),
    )(z, kv, kv, y_alias)


def _memattn_bwd(dy, z, kv, lse, dz_alias, B, S, *, name):
    nt = S // MEM_TQ

    def body(do_ref, q_ref, k_ref, v_ref, l_ref, _, dz_ref, dk_ref, dv_ref, dk_acc, dv_acc):
        i = pl.program_id(1)
        do = do_ref[...]
        q = q_ref[...]
        k = k_ref[...]
        v = v_ref[...]
        lse_b = l_ref[...]
        masks = _head_masks(q.shape)
        dq = jnp.zeros(q.shape, F32)
        dk = jnp.zeros(k.shape, F32)
        dv = jnp.zeros(v.shape, F32)
        for m in masks:
            qm = jnp.where(m, q, 0.0).astype(BF)
            dom = jnp.where(m, do, 0.0).astype(BF)
            s = lax.dot_general(qm, k, (((1,), (1,)), ((), ())), preferred_element_type=F32) * SCALE
            p = jnp.exp(s - _row_of(lse_b, m))
            dp = lax.dot_general(dom, v, (((1,), (1,)), ((), ())), preferred_element_type=F32)
            delta = jnp.sum(p * dp, axis=-1, keepdims=True)
            ds = (p * (dp - delta) * SCALE).astype(BF)
            pb = p.astype(BF)
            dv = dv + jnp.where(m[:N_MEM], lax.dot_general(pb, dom, (((0,), (0,)), ((), ())), preferred_element_type=F32), 0.0)
            dk = dk + jnp.where(m[:N_MEM], lax.dot_general(ds, qm, (((0,), (0,)), ((), ())), preferred_element_type=F32), 0.0)
            dq = dq + jnp.where(m, jnp.dot(ds, k, preferred_element_type=F32), 0.0)
        dz_ref[...] = dq.astype(dz_ref.dtype)

        @pl.when(i == 0)
        def _():
            dk_acc[...] = dk
            dv_acc[...] = dv

        @pl.when(i > 0)
        def _():
            dk_acc[...] += dk
            dv_acc[...] += dv

        @pl.when(i == nt - 1)
        def _():
            dk_ref[...] = dk_acc[...]
            dv_ref[...] = dv_acc[...]

    qblk = pl.BlockSpec((MEM_TQ, MEM_W), lambda b, i: (b * nt + i, 3))
    kblk = pl.BlockSpec((N_MEM, MEM_W), lambda b, i: (b, 0))
    return pl.pallas_call(
        body, name=name, grid=(B, nt),
        in_specs=[qblk, qblk, kblk, pl.BlockSpec((N_MEM, MEM_W), lambda b, i: (b, 1)),
                  pl.BlockSpec((MEM_TQ, MEM_W), lambda b, i: (b * nt + i, 0)), pl.BlockSpec(memory_space=pl.ANY)],
        out_specs=[qblk, kblk, kblk],
        out_shape=[jax.ShapeDtypeStruct(dz_alias.shape, dz_alias.dtype),
                   jax.ShapeDtypeStruct((B * N_MEM, MEM_W), F32), jax.ShapeDtypeStruct((B * N_MEM, MEM_W), F32)],
        scratch_shapes=[pltpu.VMEM((N_MEM, MEM_W), F32), pltpu.VMEM((N_MEM, MEM_W), F32)],
        input_output_aliases={5: 0},
        compiler_params=_cp(("parallel", "arbitrary")),
    )(dy, z, kv, kv, lse, dz_alias)


def _dil_scores(qm, kp, kc, n):
    qi = lax.broadcasted_iota(jnp.int32, (STEPS, STEPS), 0)
    kj = lax.broadcasted_iota(jnp.int32, (STEPS, STEPS), 1)
    sc = lax.dot_general(qm, kc, (((1,), (1,)), ((), ())), preferred_element_type=F32) * SCALE
    sc = jnp.where(kj <= qi, sc, NEG)
    if kp is None:
        return None, sc
    sp = lax.dot_general(qm, kp, (((1,), (1,)), ((), ())), preferred_element_type=F32) * SCALE
    sp = jnp.where(jnp.logical_and(kj >= qi, n > 0), sp, NEG)
    return sp, sc


def _dil_specs(g, d, nb):
    chunk = STEPS * d
    cur = pl.BlockSpec((chunk, 128), lambda b, n, hf: (b * nb + n, g * 2 + hf))
    prev = pl.BlockSpec((chunk, 128), lambda b, n, hf: (b * nb + jnp.maximum(n - 1, 0), g * 2 + hf))
    return cur, prev


def _dil_rows(r, d):
    return pl.ds(r, STEPS, stride=d) if d > 1 else slice(None)


def _dil_loop(d, fn):
    if d <= 4:
        for r in range(d):
            fn(r)
    else:
        lax.fori_loop(0, d, lambda r, carry: (fn(r), carry)[1], 0)


def _dil_fwd_group(g, q, k, v, o_alias, l_alias, B, S, *, name):
    d = DIL[g]
    nb = S // (STEPS * d)
    has_prev = nb > 1

    def body(*refs):
        if has_prev:
            q_ref, kp_ref, kc_ref, vp_ref, vc_ref, _, __, o_ref, l_ref = refs
        else:
            q_ref, kc_ref, vc_ref, _, __, o_ref, l_ref = refs
        n = pl.program_id(1)

        def residue(r):
            rows = _dil_rows(r, d)
            q = q_ref[rows, :]
            kc, vc = kc_ref[rows, :].astype(BF), vc_ref[rows, :].astype(BF)
            kp = kp_ref[rows, :].astype(BF) if has_prev else None
            vp = vp_ref[rows, :].astype(BF) if has_prev else None
            o = jnp.zeros(q.shape, F32)
            lse_b = jnp.zeros(q.shape, F32)
            for m in _head_masks(q.shape):
                qm = jnp.where(m, q, 0.0).astype(BF)
                sp, sc = _dil_scores(qm, kp, kc, n)
                mx = jnp.max(sc, axis=-1, keepdims=True)
                if has_prev:
                    mx = jnp.maximum(mx, jnp.max(sp, axis=-1, keepdims=True))
                l = jnp.sum(jnp.exp(sc - mx), axis=-1, keepdims=True)
                if has_prev:
                    l = l + jnp.sum(jnp.exp(sp - mx), axis=-1, keepdims=True)
                lse = mx + jnp.log(l)
                oh = jnp.dot(jnp.exp(sc - lse).astype(BF), vc, preferred_element_type=F32)
                if has_prev:
                    oh = oh + jnp.dot(jnp.exp(sp - lse).astype(BF), vp, preferred_element_type=F32)
                o = o + jnp.where(m, oh, 0.0)
                lse_b = lse_b + jnp.where(m, lse, 0.0)
            o_ref[rows, :] = o
            l_ref[rows, :] = lse_b

        _dil_loop(d, residue)

    cur, prev = _dil_specs(g, d, nb)
    anyspec = pl.BlockSpec(memory_space=pl.ANY)
    if has_prev:
        in_specs, ops = [cur, prev, cur, prev, cur], [q, k, k, v, v]
    else:
        in_specs, ops = [cur, cur, cur], [q, k, v]
    n_in = len(ops)
    o, l = pl.pallas_call(
        body, name=name, grid=(B, nb, 2),
        in_specs=in_specs + [anyspec, anyspec],
        out_specs=[cur, cur],
        out_shape=[jax.ShapeDtypeStruct(q.shape, F32)] * 2,
        input_output_aliases={n_in: 0, n_in + 1: 1},
        compiler_params=_cp(("parallel", "parallel", "parallel")),
    )(*ops, o_alias, l_alias)
    return o, l


def _dil_bwd_group(g, q, k, v, do, cb, lse, aliases, B, S, *, name):
    d = DIL[g]
    nb = S // (STEPS * d)
    has_prev = nb > 1
    n_out = 5 if has_prev else 3

    def body(*refs):
        if has_prev:
            q_ref, kp_ref, kc_ref, vp_ref, vc_ref, do_ref, c_ref, l_ref = refs[:8]
            dq_ref, dkc_ref, dvc_ref, dkp_ref, dvp_ref = refs[8 + n_out:]
        else:
            q_ref, kc_ref, vc_ref, do_ref, c_ref, l_ref = refs[:6]
            dq_ref, dkc_ref, dvc_ref = refs[6 + n_out:]
        n = pl.program_id(1)
        tdot = lambda a, b: lax.dot_general(a, b, (((0,), (0,)), ((), ())), preferred_element_type=F32)
        ndot = lambda a, b: lax.dot_general(a, b, (((1,), (1,)), ((), ())), preferred_element_type=F32)

        def residue(r):
            rows = _dil_rows(r, d)
            q = q_ref[rows, :]
            kc, vc = kc_ref[rows, :].astype(BF), vc_ref[rows, :].astype(BF)
            kp = kp_ref[rows, :].astype(BF) if has_prev else None
            vp = vp_ref[rows, :].astype(BF) if has_prev else None
            do = do_ref[rows, :]
            cbv = c_ref[rows, :]
            lse_b = l_ref[rows, :]
            z = jnp.zeros(q.shape, F32)
            dq, dkc, dkp, dvc, dvp = z, z, z, z, z
            for m in _head_masks(q.shape):
                qm = jnp.where(m, q, 0.0).astype(BF)
                dom = jnp.where(m, do, 0.0).astype(BF)
                sp, sc = _dil_scores(qm, kp, kc, n)
                lse = _row_of(lse_b, m)
                c = _row_of(cbv, m)
                pc = jnp.exp(sc - lse)
                dsc = (pc * (ndot(dom, vc) + c) * SCALE).astype(BF)
                dqh = jnp.dot(dsc, kc, preferred_element_type=F32)
                dkc = dkc + jnp.where(m, tdot(dsc, qm), 0.0)
                dvc = dvc + jnp.where(m, tdot(pc.astype(BF), dom), 0.0)
                if has_prev:
                    pp = jnp.exp(sp - lse)
                    dsp = (pp * (ndot(dom, vp) + c) * SCALE).astype(BF)
                    dqh = dqh + jnp.dot(dsp, kp, preferred_element_type=F32)
                    dkp = dkp + jnp.where(m, tdot(dsp, qm), 0.0)
                    dvp = dvp + jnp.where(m, tdot(pp.astype(BF), dom), 0.0)
                dq = dq + jnp.where(m, dqh, 0.0)
            dq_ref[rows, :] = dq
            dkc_ref[rows, :] = dkc
            dvc_ref[rows, :] = dvc
            if has_prev:
                dkp_ref[rows, :] = dkp
                dvp_ref[rows, :] = dvp

        _dil_loop(d, residue)

    cur, prev = _dil_specs(g, d, nb)
    anyspec = pl.BlockSpec(memory_space=pl.ANY)
    dq_a, dkc_a, dkp_a, dvc_a, dvp_a = aliases
    if has_prev:
        in_specs, ops = [cur, prev, cur, prev, cur, cur, cur, cur], [q, k, k, v, v, do, cb, lse]
        al = [dq_a, dkc_a, dvc_a, dkp_a, dvp_a]
    else:
        in_specs, ops = [cur, cur, cur, cur, cur, cur], [q, k, v, do, cb, lse]
        al = [dq_a, dkc_a, dvc_a]
    n_in = len(ops)
    outs = pl.pallas_call(
        body, name=name, grid=(B, nb, 2),
        in_specs=in_specs + [anyspec] * n_out,
        out_specs=[cur] * n_out,
        out_shape=[jax.ShapeDtypeStruct(q.shape, F32)] * n_out,
        input_output_aliases={n_in + i: i for i in range(n_out)},
        compiler_params=_cp(("parallel", "parallel", "parallel")),
    )(*ops, *al)
    if has_prev:
        dq_a, dkc_a, dvc_a, dkp_a, dvp_a = outs
    else:
        dq_a, dkc_a, dvc_a = outs
    return dq_a, dkc_a, dkp_a, dvc_a, dvp_a


N_UNITS = 16


def _unit_rows(g):
    d = DIL[g]
    nb = N_UNITS // d
    return [pl.ds(n * STEPS * d + r, STEPS, stride=d) if d > 1 else pl.ds(n * STEPS, STEPS)
            for n in range(nb) for r in range(d)]


def _load_units(ref, g):
    if DIL[g] == 1:
        return ref[...].reshape(N_UNITS, STEPS, 128)
    return jnp.stack([ref[rows, :] for rows in _unit_rows(g)])


def _store_units(ref, val, g):
    if DIL[g] == 1:
        ref[...] = val.reshape(N_UNITS * STEPS, 128)
    else:
        for u, rows in enumerate(_unit_rows(g)):
            ref[rows, :] = val[u]


def _shift_units(x, by):
    z = jnp.zeros((abs(by),) + x.shape[1:], x.dtype)
    return jnp.concatenate([z, x[:N_UNITS - by]], axis=0) if by > 0 else jnp.concatenate([x[-by:], z], axis=0)


def _bdot(a, b, ca, cb):
    return lax.dot_general(a, b, (((ca,), (cb,)), ((0,), (0,))), preferred_element_type=F32)


def _dil_masks(g):
    d = DIL[g]
    has_prev = N_UNITS // d > 1
    qi = lax.broadcasted_iota(jnp.int32, (1, STEPS, STEPS), 1)
    kj = lax.broadcasted_iota(jnp.int32, (1, STEPS, STEPS), 2)
    unit = lax.broadcasted_iota(jnp.int32, (N_UNITS, 1, 1), 0)
    cur = kj <= qi
    prev = jnp.logical_and(kj >= qi, unit >= d) if has_prev else None
    lane = lax.broadcasted_iota(jnp.int32, (1, 1, 128), 2)
    heads = [(lane // HEAD_DIM) == h for h in range(128 // HEAD_DIM)]
    return has_prev, cur, prev, heads


def _dil_fwd(g, q, k, v, o_alias, l_alias, B, S, *, name):
    assert S == N_UNITS * STEPS
    d = DIL[g]

    def body(q_ref, k_ref, v_ref, _, __, o_ref, l_ref):
        has_prev, cur, prev, heads = _dil_masks(g)
        q = _load_units(q_ref, g)
        kc = _load_units(k_ref, g).astype(BF)
        vc = _load_units(v_ref, g).astype(BF)
        if has_prev:
            kp, vp = _shift_units(kc, d), _shift_units(vc, d)
        o = jnp.zeros(q.shape, F32)
        lse_b = jnp.zeros(q.shape, F32)
        for m in heads:
            qm = jnp.where(m, q, 0.0).astype(BF)
            sc = jnp.where(cur, _bdot(qm, kc, 2, 2) * SCALE, NEG)
            mx = jnp.max(sc, axis=-1, keepdims=True)
            if has_prev:
                sp = jnp.where(prev, _bdot(qm, kp, 2, 2) * SCALE, NEG)
                mx = jnp.maximum(mx, jnp.max(sp, axis=-1, keepdims=True))
            l = jnp.sum(jnp.exp(sc - mx), axis=-1, keepdims=True)
            if has_prev:
                l = l + jnp.sum(jnp.exp(sp - mx), axis=-1, keepdims=True)
            lse = mx + jnp.log(l)
            oh = _bdot(jnp.exp(sc - lse).astype(BF), vc, 2, 1)
            if has_prev:
                oh = oh + _bdot(jnp.exp(sp - lse).astype(BF), vp, 2, 1)
            o = o + jnp.where(m, oh, 0.0)
            lse_b = lse_b + jnp.where(m, lse, 0.0)
        _store_units(o_ref, o, g)
        _store_units(l_ref, lse_b, g)

    blk = pl.BlockSpec((S, 128), lambda b, hf: (b, g * 2 + hf))
    anyspec = pl.BlockSpec(memory_space=pl.ANY)
    o, l = pl.pallas_call(
        body, name=name, grid=(B, 2),
        in_specs=[blk, blk, blk, anyspec, anyspec], out_specs=[blk, blk],
        out_shape=[jax.ShapeDtypeStruct(q.shape, F32)] * 2,
        input_output_aliases={3: 0, 4: 1},
        compiler_params=_cp(("parallel", "parallel")),
    )(q, k, v, o_alias, l_alias)
    return o, l


def _dil_bwd(g, q, k, v, do, cb, lse, aliases, B, S, *, name):
    assert S == N_UNITS * STEPS
    d = DIL[g]

    def body(q_ref, k_ref, v_ref, do_ref, c_ref, l_ref, _, __, ___, dq_ref, dk_ref, dv_ref):
        has_prev, cur, prev, heads = _dil_masks(g)
        q = _load_units(q_ref, g)
        kc = _load_units(k_ref, g).astype(BF)
        vc = _load_units(v_ref, g).astype(BF)
        do = _load_units(do_ref, g)
        cbv = _load_units(c_ref, g)
        lse_b = _load_units(l_ref, g)
        if has_prev:
            kp, vp = _shift_units(kc, d), _shift_units(vc, d)
        z = jnp.zeros(q.shape, F32)
        dq, dkc, dkp, dvc, dvp = z, z, z, z, z
        for m in heads:
            qm = jnp.where(m, q, 0.0).astype(BF)
            dom = jnp.where(m, do, 0.0).astype(BF)
            lse = jnp.max(jnp.where(m, lse_b, -jnp.inf), axis=-1, keepdims=True)
            c = jnp.max(jnp.where(m, cbv, -jnp.inf), axis=-1, keepdims=True)
            sc = jnp.where(cur, _bdot(qm, kc, 2, 2) * SCALE, NEG)
            pc = jnp.exp(sc - lse)
            dsc = (pc * (_bdot(dom, vc, 2, 2) + c) * SCALE).astype(BF)
            dqh = _bdot(dsc, kc, 2, 1)
            dkc = dkc + jnp.where(m, _bdot(dsc, qm, 1, 1), 0.0)
            dvc = dvc + jnp.where(m, _bdot(pc.astype(BF), dom, 1, 1), 0.0)
            if has_prev:
                sp = jnp.where(prev, _bdot(qm, kp, 2, 2) * SCALE, NEG)
                pp = jnp.exp(sp - lse)
                dsp = (pp * (_bdot(dom, vp, 2, 2) + c) * SCALE).astype(BF)
                dqh = dqh + _bdot(dsp, kp, 2, 1)
                dkp = dkp + jnp.where(m, _bdot(dsp, qm, 1, 1), 0.0)
                dvp = dvp + jnp.where(m, _bdot(pp.astype(BF), dom, 1, 1), 0.0)
            dq = dq + jnp.where(m, dqh, 0.0)
        if has_prev:
            dkc = dkc + _shift_units(dkp, -d)
            dvc = dvc + _shift_units(dvp, -d)
        _store_units(dq_ref, dq, g)
        _store_units(dk_ref, dkc, g)
        _store_units(dv_ref, dvc, g)

    blk = pl.BlockSpec((S, 128), lambda b, hf: (b, g * 2 + hf))
    anyspec = pl.BlockSpec(memory_space=pl.ANY)
    return tuple(pl.pallas_call(
        body, name=name, grid=(B, 2),
        in_specs=[blk] * 6 + [anyspec] * 3, out_specs=[blk] * 3,
        out_shape=[jax.ShapeDtypeStruct(q.shape, F32)] * 3,
        input_output_aliases={6: 0, 7: 1, 8: 2},
        compiler_params=_cp(("parallel", "parallel")),
    )(q, k, v, do, cb, lse, *aliases))


def _kv_grad_sum(parts, cos, sin, *, name, tm=512):
    T = parts[0][0].shape[0]
    n_l = len(parts)

    def body(*refs):
        c_ref, s_ref = refs[0], refs[1]
        dk_ref, dv_ref = refs[2 + 2 * n_l:]
        dk = refs[2][...]
        dv = refs[3][...]
        for li in range(1, n_l):
            dk = dk + refs[2 + 2 * li][...]
            dv = dv + refs[3 + 2 * li][...]
        dk_ref[...] = _rot(dk, c_ref[...], s_ref[...], -1.0).astype(dk_ref.dtype)
        dv_ref[...] = dv.astype(dv_ref.dtype)

    full = pl.BlockSpec((tm, MAIN_W), lambda i: (i, 0))
    tab = pl.BlockSpec((tm, 128), lambda i: (i, 0))
    ops = [cos, sin] + [t for part in parts for t in part]
    return pl.pallas_call(
        body, name=name, grid=(T // tm,), in_specs=[tab, tab] + [full] * (2 * n_l), out_specs=[full, full],
        out_shape=[jax.ShapeDtypeStruct((T, MAIN_W), BF)] * 2,
        compiler_params=_cp(("parallel",)),
    )(*ops)


def _group_softmax(lse):
    l0, l1, l2 = lse[:, 0:256], lse[:, 256:512], lse[:, 512:768]
    mx = jnp.maximum(jnp.maximum(l0, l1), l2)
    e0, e1, e2 = jnp.exp(l0 - mx), jnp.exp(l1 - mx), jnp.exp(l2 - mx)
    tot = e0 + e1 + e2
    return e0 / tot, e1 / tot, e2 / tot


def _dil_combine_fwd(o, lse, y_alias, *, name, tm=512):
    T = o.shape[0]

    def body(o_ref, l_ref, _, y_ref):
        a = jnp.concatenate(_group_softmax(l_ref[...]), axis=1)
        y_ref[...] = (o_ref[...] * a).astype(y_ref.dtype)

    blk = pl.BlockSpec((tm, MAIN_W), lambda i: (i, 0))
    return pl.pallas_call(
        body, name=name, grid=(T // tm,), in_specs=[blk, blk, pl.BlockSpec(memory_space=pl.ANY)], out_specs=blk,
        out_shape=jax.ShapeDtypeStruct(y_alias.shape, y_alias.dtype), input_output_aliases={2: 0},
        compiler_params=_cp(("parallel",)),
    )(o, lse, y_alias)


def _dil_combine_bwd(dy, o, lse, *, name, tm=256):
    T = o.shape[0]
    lane_r = lax.broadcasted_iota(jnp.int32, (256, 256), 0) // HEAD_DIM
    lane_c = lax.broadcasted_iota(jnp.int32, (256, 256), 1) // HEAD_DIM
    ones_bd = (lane_r == lane_c).astype(BF)

    def body(dy_ref, o_ref, l_ref, e_ref, do_ref, c_ref):
        dyv = dy_ref[...]
        alphas = _group_softmax(l_ref[...])
        prod = dyv * o_ref[...]
        e = e_ref[...]
        tot = jnp.zeros((tm, 256), F32)
        for gi in range(3):
            x = prod[:, gi * 256:(gi + 1) * 256]
            hi = x.astype(BF)
            lo = (x - hi.astype(F32)).astype(BF)
            dalpha = jnp.dot(hi, e, preferred_element_type=F32) + jnp.dot(lo, e, preferred_element_type=F32)
            tot = tot + alphas[gi] * dalpha
        a = jnp.concatenate(alphas, axis=1)
        do_ref[...] = (dyv * a).astype(do_ref.dtype)
        c_ref[...] = jnp.concatenate([-al * tot for al in alphas], axis=1)

    blk = pl.BlockSpec((tm, MAIN_W), lambda i: (i, 0))
    return pl.pallas_call(
        body, name=name, grid=(T // tm,),
        in_specs=[blk, blk, blk, pl.BlockSpec((256, 256), lambda i: (0, 0))], out_specs=[blk, blk],
        out_shape=[jax.ShapeDtypeStruct((T, MAIN_W), F32), jax.ShapeDtypeStruct((T, MAIN_W), F32)],
        compiler_params=_cp(("parallel",)),
    )(dy, o, lse, ones_bd)


def _kv_grad(parts, cos, sin, B, S, *, name):
    T = B * S
    tb = S // STEPS
    n_l = len(parts)

    def shifted(g):
        def f(b, t):
            return (b * tb + jnp.minimum(t + DIL[g], tb - 1), g)
        return f

    with_prev = [g for g in range(3) if DIL[g] < tb]
    n_p = len(with_prev)
    per_l = 2 + 2 * n_p

    def body(*refs):
        c_ref, s_ref = refs[0], refs[1]
        ins = refs[2:2 + n_l * per_l]
        dk_ref, dv_ref = refs[2 + n_l * per_l:]
        t = pl.program_id(1)
        dk = jnp.zeros((STEPS, MAIN_W), F32)
        dv = jnp.zeros((STEPS, MAIN_W), F32)
        zero = jnp.zeros((STEPS, 256), F32)
        for li in range(n_l):
            base = li * per_l
            dk = dk + ins[base][...]
            dv = dv + ins[base + 1][...]
            kparts, vparts = [zero] * 3, [zero] * 3
            for pi, g in enumerate(with_prev):
                ok = t + DIL[g] < tb
                kparts[g] = jnp.where(ok, ins[base + 2 + pi][...], 0.0)
                vparts[g] = jnp.where(ok, ins[base + 2 + n_p + pi][...], 0.0)
            dk = dk + jnp.concatenate(kparts, axis=1)
            dv = dv + jnp.concatenate(vparts, axis=1)
        dk_ref[...] = _rot(dk, c_ref[...], s_ref[...], -1.0).astype(dk_ref.dtype)
        dv_ref[...] = dv.astype(dv_ref.dtype)

    full = pl.BlockSpec((STEPS, MAIN_W), lambda b, t: (b * tb + t, 0))
    tab = pl.BlockSpec((STEPS, 128), lambda b, t: (b * tb + t, 0))
    in_specs, ops = [tab, tab], [cos, sin]
    for (kc, kp, vc, vp) in parts:
        in_specs += [full, full] + [pl.BlockSpec((STEPS, 256), shifted(g)) for g in with_prev] * 2
        ops += [kc, vc] + [kp] * n_p + [vp] * n_p
    return pl.pallas_call(
        body, name=name, grid=(B, tb), in_specs=in_specs, out_specs=[full, full],
        out_shape=[jax.ShapeDtypeStruct((T, MAIN_W), BF)] * 2,
        compiler_params=_cp(("parallel", "parallel")),
    )(*ops)


def _loss(y, target, *, name, tm=512):
    T, Dm = y.shape
    nt = T // tm

    def body(y_ref, t_ref, l_ref, d_ref, acc):
        i = pl.program_id(0)
        err = y_ref[...] - t_ref[...]
        d_ref[...] = err / Dm
        part = jnp.sum(jnp.mean(err * err, axis=-1, keepdims=True).reshape(tm // 8, 8, 1), axis=0)

        @pl.when(i == 0)
        def _():
            acc[...] = part

        @pl.when(i > 0)
        def _():
            acc[...] += part

        @pl.when(i == nt - 1)
        def _():
            l_ref[...] = 0.5 * jnp.sum(acc[...], axis=0, keepdims=True)

    row = pl.BlockSpec((tm, Dm), lambda i: (i, 0))
    return pl.pallas_call(
        body, name=name, grid=(nt,), in_specs=[row, row],
        out_specs=[pl.BlockSpec((1, 1), lambda i: (0, 0)), row],
        out_shape=[jax.ShapeDtypeStruct((1, 1), F32), jax.ShapeDtypeStruct((T, Dm), F32)],
        scratch_shapes=[pltpu.VMEM((8, 1), F32)],
        compiler_params=_cp(("arbitrary",)),
    )(y, target)


def _adamw(w, g, m, v, *, name):
    shape = w.shape
    cols = shape[-1]
    rows = w.size // cols
    tm = rows
    for cand in (512, 352, 256, 128):
        if rows > cand and rows % cand == 0 and cand * cols * 4 <= (1 << 20):
            tm = cand
            break

    def body(w_ref, g_ref, m_ref, v_ref, d_ref, mo_ref, vo_ref):
        gv = g_ref[...]
        mn = ADAM_B1 * m_ref[...] + (1.0 - ADAM_B1) * gv
        vn = ADAM_B2 * v_ref[...] + (1.0 - ADAM_B2) * (gv * gv)
        m_hat = mn / (1.0 - ADAM_B1 ** ADAM_STEP)
        v_hat = vn / (1.0 - ADAM_B2 ** ADAM_STEP)
        d_ref[...] = -ADAM_LR * (m_hat / (jnp.sqrt(v_hat) + ADAM_EPS) + ADAM_WD * w_ref[...])
        mo_ref[...] = mn
        vo_ref[...] = vn

    blk = pl.BlockSpec((tm, cols), lambda i: (i, 0))
    outs = pl.pallas_call(
        body, name=name, grid=(rows // tm,), in_specs=[blk] * 4, out_specs=[blk] * 3,
        out_shape=[jax.ShapeDtypeStruct((rows, cols), F32)] * 3,
        compiler_params=_cp(("parallel",)),
    )(*[t.reshape(rows, cols) for t in (w, g, m, v)])
    return tuple(t.reshape(shape) for t in outs)


def _adamw_layer(name, l, w, g, m, v, prev, after=None):
    L, rows, cols = w.shape
    tm = rows
    for cand in (512, 352, 256, 176, 128, 64):
        if rows % cand == 0 and cand * cols * 4 <= (1 << 20):
            tm = cand
            break
    if prev is None:
        prev = tuple(lax.empty(w.shape, F32) for _ in range(4))

    n_after = 0 if after is None else 1

    def body(w_ref, g_ref, m_ref, v_ref, *rest):
        d_ref, mo_ref, vo_ref, go_ref = rest[4 + n_after:]
        gv = g_ref[...]
        mn = ADAM_B1 * m_ref[...] + (1.0 - ADAM_B1) * gv
        vn = ADAM_B2 * v_ref[...] + (1.0 - ADAM_B2) * (gv * gv)
        m_hat = mn / (1.0 - ADAM_B1 ** ADAM_STEP)
        v_hat = vn / (1.0 - ADAM_B2 ** ADAM_STEP)
        d_ref[...] = -ADAM_LR * (m_hat / (jnp.sqrt(v_hat) + ADAM_EPS) + ADAM_WD * w_ref[...])
        mo_ref[...] = mn
        vo_ref[...] = vn
        go_ref[...] = gv

    lay = pl.BlockSpec((None, tm, cols), lambda i: (l, i, 0))
    one = pl.BlockSpec((None, tm, cols), lambda i: (0, i, 0))
    return tuple(pl.pallas_call(
        body, name=f"l{l}_adamw_{name}", grid=(rows // tm,),
        in_specs=[lay, one, lay, lay] + [pl.BlockSpec(memory_space=pl.ANY)] * (4 + n_after), out_specs=[lay] * 4,
        out_shape=[jax.ShapeDtypeStruct(w.shape, F32)] * 4,
        input_output_aliases={4 + i: i for i in range(4)},
        compiler_params=_cp(("parallel",)),
    )(w, g, m, v, *prev, *([] if after is None else [after])))


BIG = {
    'w_in': ((DEPTH, D_MODEL, D_MODEL), 'row'),
    'w_mem_kv': ((DEPTH, D_MODEL, 2 * MEM_W), 'row'),
    'w_out': ((DEPTH, D_MODEL, D_MODEL), 'row'),
    'w_kv': ((1, D_MODEL, 2 * MAIN_W), 'col'),
    'w_gate_up': ((DEPTH, D_MODEL, 2 * D_FF), 'col'),
    'w_down': ((DEPTH, D_FF, D_MODEL), 'row'),
}
BIG_NAMES = tuple(BIG)
N_CHIPS = 4
HBM_ANY = pl.BlockSpec(memory_space=pl.ANY)


def _geom(name):
    (L, R, C), kind = BIG[name]
    if kind == 'row':
        return L, R, C, kind, R // N_CHIPS, C, R // (2 * N_CHIPS)
    return L, R, C, kind, R, C // N_CHIPS, R // 2


def _shard_shape(name):
    L, R, C, kind, rs, cs, rh = _geom(name)
    return (L, rs, cs)


def _half_shape(name):
    L, R, C, kind, rs, cs, rh = _geom(name)
    return (L, rh, cs)


def _full_win(ref, name, s, h):
    L, R, C, kind, rs, cs, rh = _geom(name)
    if kind == 'row':
        rows = pl.ds(s * rs, rs) if h is None else pl.ds(s * rs + h * rh, rh)
        return ref.at[:, rows, :]
    rows = slice(None) if h is None else pl.ds(h * rh, rh)
    return ref.at[:, rows, pl.ds(s * cs, cs)]


def _shard_half(ref, name, h):
    L, R, C, kind, rs, cs, rh = _geom(name)
    return ref.at[:, pl.ds(h * rh, rh), :]


def _halves_win(ref, name, s):
    L, R, C, kind, rs, cs, rh = _geom(name)
    if kind == 'row':
        return ref.at[:, pl.ds(s * rh, rh), :]
    return ref.at[:, :, pl.ds(s * cs, cs)]


def _halves_shape(name):
    L, R, C, kind, rs, cs, rh = _geom(name)
    return (L, N_CHIPS * rh, cs) if kind == 'row' else (L, rh, C)


def _place():
    x, y, c = lax.axis_index("x"), lax.axis_index("y"), lax.axis_index("c")
    chips = [(1 - x, y), (x, 1 - y), (1 - x, 1 - y)]
    return x, y, c, chips


SMALL_ROWS = 24


def _all_gather(shards, small):
    names = BIG_NAMES
    nw = len(names)

    def body(*refs):
        src = dict(zip(names, refs[:nw]))
        small_ref = refs[nw]
        dst = dict(zip(names, refs[nw + 1:2 * nw + 1]))
        small_out = refs[2 * nw + 1]
        send_sems, recv_sems, local_sems = refs[2 * nw + 2:]
        x, y, c, chips = _place()
        s = 2 * x + y
        sib = (x, y, 1 - c)

        def remote(k, src_ref, dst_ref, to):
            return pltpu.make_async_remote_copy(src_ref=src_ref, dst_ref=dst_ref, send_sem=send_sems.at[k],
                                                recv_sem=recv_sems.at[k], device_id=to, device_id_type=MESH)

        local = []
        for wi, nm in enumerate(names):
            local.append(pltpu.make_async_copy(src[nm], _full_win(dst[nm], nm, s, None), local_sems.at[wi]))
        local.append(pltpu.make_async_copy(small_ref, small_out.at[s], local_sems.at[nw]))
        for cp in local:
            cp.start()
        sends = []
        for j, (px, py) in enumerate(chips):
            for wi, nm in enumerate(names):
                sends.append(remote(wi * 6 + j, _shard_half(src[nm], nm, c), _full_win(dst[nm], nm, s, c), (px, py, c)))
            sends.append(remote(nw * 6 + j, small_ref, small_out.at[s], (px, py, c)))
        for cp in sends:
            cp.start()
        for j, (px, py) in enumerate(chips):
            sp = 2 * px + py
            for wi, nm in enumerate(names):
                w = _full_win(dst[nm], nm, sp, c)
                remote(wi * 6 + j, w, w, sib).wait_recv()
                fwd = remote(wi * 6 + 3 + j, w, w, sib)
                fwd.start()
                sends.append(fwd)
            remote(nw * 6 + j, small_ref, small_out.at[sp], sib).wait_recv()
        for j, (px, py) in enumerate(chips):
            sp = 2 * px + py
            for wi, nm in enumerate(names):
                w = _full_win(dst[nm], nm, sp, 1 - c)
                remote(wi * 6 + 3 + j, w, w, sib).wait_recv()
        for cp in sends:
            cp.wait_send()
        for cp in local:
            cp.wait()

    n_sem = nw * 6 + 3
    outs = pl.pallas_call(
        body, name="all_gather_weights",
        in_specs=[HBM_ANY] * (nw + 1), out_specs=[HBM_ANY] * (nw + 1),
        out_shape=[jax.ShapeDtypeStruct(BIG[nm][0], BF) for nm in names]
        + [jax.ShapeDtypeStruct((N_CHIPS, SMALL_ROWS, 256), F32)],
        scratch_shapes=[pltpu.SemaphoreType.DMA((n_sem,)), pltpu.SemaphoreType.DMA((n_sem,)),
                        pltpu.SemaphoreType.DMA((nw + 1,))],
    )(*[shards[nm] for nm in names], small)
    return dict(zip(names, outs[:nw])), outs[nw]


SEM_SPEC = pl.BlockSpec(memory_space=pltpu.SEMAPHORE)
HBM_SPEC = pl.BlockSpec(memory_space=pltpu.HBM)
DATAFLOW = pltpu.SideEffectType.DATAFLOW_SIDE_EFFECTING


def _in_hbm(a):
    return pltpu.with_memory_space_constraint(a, pltpu.HBM)


def _remote(src, dst, send_sems, recv_sems, k, to):
    return pltpu.make_async_remote_copy(src_ref=src, dst_ref=dst, send_sem=send_sems.at[k], recv_sem=recv_sems.at[k],
                                        device_id=to, device_id_type=MESH)


def _split_start(name, bufs, n_copies, sends, after=None):
    nb = len(bufs)
    n_in = nb + (0 if after is None else 1)

    def body(*refs):
        in_refs = refs[:nb]
        send_sems, recv_sems = refs[n_in], refs[n_in + 1]
        token = refs[-1]
        for k, (src, dst, to) in enumerate(sends(in_refs)):
            _remote(src, dst, send_sems, recv_sems, k, to).start()
        token[...] = jnp.zeros_like(token)

    outs = pl.pallas_call(
        body, name=name,
        out_shape=(pltpu.SemaphoreType.DMA((n_copies,)), pltpu.SemaphoreType.DMA((n_copies,)),
                   *[pltpu.HBM(b.shape, b.dtype) for b in bufs], jax.ShapeDtypeStruct((8, 128), F32)),
        in_specs=[HBM_SPEC] * nb + [HBM_ANY] * (n_in - nb),
        out_specs=(SEM_SPEC, SEM_SPEC, *[HBM_SPEC] * nb, pl.BlockSpec(memory_space=pltpu.VMEM)),
        input_output_aliases={i: 2 + i for i in range(nb)},
        compiler_params=pltpu.CompilerParams(has_side_effects=DATAFLOW),
    )(*[_in_hbm(b) for b in bufs], *([] if after is None else [after]))
    return outs[0], outs[1], list(outs[2:2 + nb]), outs[-1]


def _split_wait(name, send_sems, recv_sems, bufs, after, sends, arrivals):
    nb = len(bufs)

    def body(*refs):
        in_refs = refs[:nb]
        s_sems, r_sems = refs[nb], refs[nb + 1]
        me = (lax.axis_index("x"), lax.axis_index("y"), lax.axis_index("c"))
        for k, (src, dst, to) in enumerate(sends(in_refs)):
            _remote(src, dst, s_sems, r_sems, k, to).wait_send()
        for k, win in enumerate(arrivals(in_refs)):
            _remote(win, win, s_sems, r_sems, k, me).wait_recv()

    outs = pl.pallas_call(
        body, name=name,
        out_shape=[pltpu.HBM(b.shape, b.dtype) for b in bufs],
        in_specs=[HBM_SPEC] * nb + [SEM_SPEC, SEM_SPEC, HBM_ANY],
        out_specs=[HBM_SPEC] * nb,
        input_output_aliases={i: i for i in range(nb)},
        compiler_params=pltpu.CompilerParams(has_side_effects=DATAFLOW),
    )(*bufs, send_sems, recv_sems, after)
    return list(outs)


MIX_W = ('w_in', 'w_mem_kv', 'w_out')
FFN_W = ('w_gate_up', 'w_down')
LAYER_W = MIX_W + FFN_W


def _place_own(tag, names, sources, small, sc):
    nw = len(names)
    has_small = small is not None

    def body(sc_ref, *refs):
        srcs = refs[:nw]
        shard_out = refs[nw + has_small:2 * nw + has_small]
        full_out = refs[2 * nw + has_small:3 * nw + has_small]
        for src, sh, fu in zip(srcs, shard_out, full_out):
            v = src[...].astype(BF)
            sh[...] = v
            fu[...] = v
        if has_small:
            refs[-1][...] = refs[nw][...]

    in_specs, shard_specs, full_specs, shard_shape, full_shape, ops = [], [], [], [], [], []
    for nm, (arr, layer) in zip(names, sources):
        L, R, C, kind, rs, cs, rh = _geom(nm)
        in_specs.append(pl.BlockSpec((1, rs, cs), lambda i, sc_ref, layer=layer: (layer, 0, 0)))
        shard_specs.append(pl.BlockSpec((1, rs, cs), lambda i, sc_ref: (0, 0, 0)))
        if kind == 'row':
            full_specs.append(pl.BlockSpec((1, rs, cs), lambda i, sc_ref: (0, sc_ref[0], 0)))
        else:
            full_specs.append(pl.BlockSpec((1, rs, cs), lambda i, sc_ref: (0, 0, sc_ref[0])))
        shard_shape.append(jax.ShapeDtypeStruct((1, rs, cs), BF))
        full_shape.append(jax.ShapeDtypeStruct((1, R, C), BF))
        ops.append(arr)
    if has_small:
        in_specs.append(pl.BlockSpec((SMALL_ROWS, 256), lambda i, sc_ref: (0, 0)))
        full_specs.append(pl.BlockSpec((None, SMALL_ROWS, 256), lambda i, sc_ref: (sc_ref[0], 0, 0)))
        full_shape.append(jax.ShapeDtypeStruct((N_CHIPS, SMALL_ROWS, 256), F32))
        ops.append(small)
    outs = pl.pallas_call(
        body, name=f"{tag}_place_own_shard",
        grid_spec=pltpu.PrefetchScalarGridSpec(num_scalar_prefetch=1, grid=(1,), in_specs=in_specs,
                                               out_specs=shard_specs + full_specs),
        out_shape=shard_shape + full_shape,
        compiler_params=_cp(("arbitrary",)),
    )(sc, *ops)
    return list(outs[:nw]), list(outs[nw:])


def _gather_start(l, names, sources, small, sc, after=None):
    nw = len(names)
    has_small = small is not None
    shards, fulls = _place_own(l, names, sources, small, sc)
    bufs = list(shards) + ([small] if has_small else []) + list(fulls)
    n_src = nw + (1 if has_small else 0)

    def sends(refs):
        x, y, c, chips = _place()
        s = 2 * x + y
        out = []
        for (px, py) in chips:
            for wi, nm in enumerate(names):
                out.append((_shard_half(refs[wi], nm, c), _full_win(refs[n_src + wi], nm, s, c), (px, py, c)))
            if has_small:
                out.append((refs[nw], refs[n_src + nw].at[s], (px, py, c)))
        return out

    def arrivals(refs):
        x, y, c, chips = _place()
        out = []
        for (px, py) in chips:
            sp = 2 * px + py
            for wi, nm in enumerate(names):
                out.append(_full_win(refs[n_src + wi], nm, sp, c))
            if has_small:
                out.append(refs[n_src + nw].at[sp])
        return out

    n_copies = 3 * n_src
    send_sems, recv_sems, bufs, token = _split_start(f"{l}_gather_ici_start", bufs, n_copies, sends, after)
    return dict(l=l, names=names, has_small=has_small, sems=(send_sems, recv_sems), bufs=bufs, sends=sends,
                arrivals=arrivals, token=token)


def _gather_forward(st, after):
    l, names = st['l'], st['names']
    nw = len(names)
    n_src = nw + (1 if st['has_small'] else 0)
    bufs = _split_wait(f"{l}_gather_ici_wait", *st['sems'], st['bufs'], after, st['sends'], st['arrivals'])
    fulls = bufs[n_src:n_src + nw]
    small_all = bufs[n_src + nw] if st['has_small'] else None

    def sends(refs):
        x, y, c, chips = _place()
        out = []
        for (px, py) in chips:
            sp = 2 * px + py
            for wi, nm in enumerate(names):
                w = _full_win(refs[wi], nm, sp, c)
                out.append((w, w, (x, y, 1 - c)))
        return out

    def arrivals(refs):
        x, y, c, chips = _place()
        out = []
        for (px, py) in chips:
            sp = 2 * px + py
            for wi, nm in enumerate(names):
                out.append(_full_win(refs[wi], nm, sp, 1 - c))
        return out

    send_sems, recv_sems, fulls, token = _split_start(f"{l}_gather_d2d_start", fulls, 3 * nw, sends)
    return dict(l=l, names=names, sems=(send_sems, recv_sems), bufs=fulls, sends=sends, arrivals=arrivals,
                small_all=small_all, token=token)


def _gather_finish(st, after):
    fulls = _split_wait(f"{st['l']}_gather_d2d_wait", *st['sems'], st['bufs'], after, st['sends'], st['arrivals'])
    return dict(zip(st['names'], fulls)), st['small_all']


def _reduce_start(tag, names, grads):
    nw = len(names)
    recv = [lax.empty((1,) + _halves_shape(nm)[1:], F32) for nm in names]
    bufs = [grads[nm] for nm in names] + recv

    def windows(refs, half_of):
        x, y, c, _ = _place()
        h = half_of(c)
        out = []
        for wi, nm in enumerate(names):
            L, R, C, kind, rs, cs, rh = _geom(nm)
            if kind == 'row':
                for sp in range(N_CHIPS):
                    out.append((_full_win(refs[wi], nm, sp, h), _halves_win(refs[nw + wi], nm, sp)))
            else:
                out.append((refs[wi].at[:, pl.ds(h * rh, rh), :], refs[nw + wi]))
        return out

    def sends(refs):
        x, y, c, _ = _place()
        return [(src, dst, (x, y, 1 - c)) for src, dst in windows(refs, lambda c: 1 - c)]

    def arrivals(refs):
        return [dst for _, dst in windows(refs, lambda c: c)]

    n_copies = sum(N_CHIPS if BIG[nm][1] == 'row' else 1 for nm in names)
    send_sems, recv_sems, bufs, token = _split_start(tag + "_halves_start", bufs, n_copies, sends)
    return dict(tag=tag, names=names, sems=(send_sems, recv_sems), bufs=bufs, sends=sends, arrivals=arrivals, token=token)


def _reduce_mid(st, after, sc):
    tag, names = st['tag'], st['names']
    nw = len(names)
    bufs = _split_wait(tag + "_halves_wait", *st['sems'], st['bufs'], after, st['sends'], st['arrivals'])
    halves, own = [], []
    for wi, nm in enumerate(names):
        hb, ow = _add_halves(nm, bufs[wi], bufs[nw + wi], sc, tag)
        halves.append(hb)
        own.append(ow)
    pieces = [lax.empty((3, 1) + _half_shape(nm)[1:], BF) for nm in names]

    def sends(refs):
        x, y, c, chips = _place()
        out = []
        for j, (px, py) in enumerate(chips):
            for wi, nm in enumerate(names):
                out.append((_halves_win(refs[wi], nm, 2 * px + py), refs[nw + wi].at[j], (px, py, c)))
        return out

    def arrivals(refs):
        return [refs[nw + wi].at[j] for j in range(3) for wi in range(nw)]

    send_sems, recv_sems, bufs, token = _split_start(tag + "_pieces_start", halves + pieces, 3 * nw, sends)
    return dict(tag=tag, names=names, sems=(send_sems, recv_sems), bufs=bufs, sends=sends, arrivals=arrivals, own=own,
                token=token)


def _reduce_late(st, after, sc):
    tag, names = st['tag'], st['names']
    nw = len(names)
    bufs = _split_wait(tag + "_pieces_wait", *st['sems'], st['bufs'], after, st['sends'], st['arrivals'])
    gsh = [_sum_pieces(nm, st['own'][wi], bufs[nw + wi], sc, tag) for wi, nm in enumerate(names)]

    def sends(refs):
        x, y, c, _ = _place()
        return [(_shard_half(refs[wi], nm, c), _shard_half(refs[wi], nm, c), (x, y, 1 - c)) for wi, nm in enumerate(names)]

    def arrivals(refs):
        x, y, c, _ = _place()
        return [_shard_half(refs[wi], nm, 1 - c) for wi, nm in enumerate(names)]

    send_sems, recv_sems, bufs, token = _split_start(tag + "_share_start", gsh, nw, sends)
    return dict(tag=tag, names=names, sems=(send_sems, recv_sems), bufs=bufs, sends=sends, arrivals=arrivals, token=token)


def _reduce_finish(st, after):
    gsh = _split_wait(st['tag'] + "_share_wait", *st['sems'], st['bufs'], after, st['sends'], st['arrivals'])
    return dict(zip(st['names'], gsh))


def _add_halves(name, g, r, sc, tag):
    _, R, C, kind, rs, cs, rh = _geom(name)
    L = g.shape[0]
    tr = rh if kind == 'row' else 256
    nr = rh // tr

    def body(sc_ref, g_ref, r_ref, hb_ref, own_ref):
        sp = pl.program_id(2)
        tot = g_ref[...] + r_ref[...]
        hb_ref[...] = tot.astype(hb_ref.dtype)

        @pl.when(sp == sc_ref[0])
        def _():
            own_ref[...] = tot

    if kind == 'row':
        g_map = lambda l, ri, sp, sc_ref: (l, sp * 2 + sc_ref[1], 0)
        h_map = lambda l, ri, sp, sc_ref: (l, sp, 0)
    else:
        g_map = lambda l, ri, sp, sc_ref: (l, sc_ref[1] * nr + ri, sp)
        h_map = lambda l, ri, sp, sc_ref: (l, ri, sp)
    own_map = lambda l, ri, sp, sc_ref: (l, ri, 0)
    blk = (None, tr, cs)
    return pl.pallas_call(
        body, name=tag + "_add_halves_" + name,
        grid_spec=pltpu.PrefetchScalarGridSpec(
            num_scalar_prefetch=1, grid=(L, nr, N_CHIPS),
            in_specs=[pl.BlockSpec(blk, g_map), pl.BlockSpec(blk, h_map)],
            out_specs=[pl.BlockSpec(blk, h_map), pl.BlockSpec(blk, own_map)]),
        out_shape=[jax.ShapeDtypeStruct((L,) + _halves_shape(name)[1:], BF),
                   jax.ShapeDtypeStruct((L,) + _half_shape(name)[1:], F32)],
        compiler_params=_cp(("parallel", "parallel", "arbitrary")),
    )(sc, g, r)


def _sum_pieces(name, own, pieces, sc, tag):
    _, R, C, kind, rs, cs, rh = _geom(name)
    L = own.shape[0]
    tr = rh if kind == 'row' else 256
    nr = rh // tr

    def body(sc_ref, o_ref, p_ref, out_ref):
        out_ref[...] = o_ref[...] + p_ref[0].astype(F32) + p_ref[1].astype(F32) + p_ref[2].astype(F32)

    blk = (None, tr, cs)
    return pl.pallas_call(
        body, name=tag + "_sum_pieces_" + name,
        grid_spec=pltpu.PrefetchScalarGridSpec(
            num_scalar_prefetch=1, grid=(L, nr),
            in_specs=[pl.BlockSpec(blk, lambda l, ri, sc_ref: (l, ri, 0)),
                      pl.BlockSpec((3, None, tr, cs), lambda l, ri, sc_ref: (0, l, ri, 0))],
            out_specs=pl.BlockSpec(blk, lambda l, ri, sc_ref: (l, sc_ref[1] * nr + ri, 0))),
        out_shape=jax.ShapeDtypeStruct((L,) + _shard_shape(name)[1:], F32),
        compiler_params=_cp(("parallel", "parallel")),
    )(sc, own, pieces)


def _small_gather_start(v, sc):
    rows = v.shape[0]

    def place(sc_ref, v_ref, o_ref):
        o_ref[...] = v_ref[...]

    slots = pl.pallas_call(
        place, name="small_grads_place_own",
        grid_spec=pltpu.PrefetchScalarGridSpec(
            num_scalar_prefetch=1, grid=(1,),
            in_specs=[pl.BlockSpec((rows, 128), lambda i, sc_ref: (0, 0))],
            out_specs=pl.BlockSpec((None, rows, 128), lambda i, sc_ref: (2 * sc_ref[0] + sc_ref[1], 0, 0))),
        out_shape=jax.ShapeDtypeStruct((8, rows, 128), v.dtype),
        compiler_params=_cp(("arbitrary",)),
    )(sc, v)

    def peers():
        x, y, c, _ = _place()
        flips = [(fx, fy, fc) for fx in (0, 1) for fy in (0, 1) for fc in (0, 1)][1:]
        return [((1 - x if fx else x), (1 - y if fy else y), (1 - c if fc else c)) for fx, fy, fc in flips]

    def sends(refs):
        x, y, c, _ = _place()
        return [(refs[0], refs[1].at[4 * x + 2 * y + c], p) for p in peers()]

    def arrivals(refs):
        return [refs[1].at[4 * px + 2 * py + pc] for px, py, pc in peers()]

    send_sems, recv_sems, bufs, token = _split_start("small_grads_gather_start", [v, slots], 7, sends)
    return dict(sems=(send_sems, recv_sems), bufs=bufs, sends=sends, arrivals=arrivals, token=token)


def _small_gather_finish(st, after):
    return _split_wait("small_grads_gather_wait", *st['sems'], st['bufs'], after, st['sends'], st['arrivals'])[1]


def _sum8(v8, *, name, tr=336):
    rows = v8.shape[1]
    tr = min(tr, rows)
    assert rows % tr == 0

    def body(v_ref, o_ref):
        tot = v_ref[0].astype(F32)
        for d in range(1, 8):
            tot = tot + v_ref[d].astype(F32)
        o_ref[...] = tot

    return pl.pallas_call(
        body, name=name, grid=(rows // tr,),
        in_specs=[pl.BlockSpec((8, tr, 128), lambda i: (0, i, 0))], out_specs=pl.BlockSpec((tr, 128), lambda i: (i, 0)),
        out_shape=jax.ShapeDtypeStruct((rows, 128), F32),
        compiler_params=_cp(("parallel",)),
    )(v8)


def _block_diag(w_pool_l):
    wbd = jnp.zeros((MAIN_W, MAIN_W), F32)
    for gi in range(len(POOL_WINDOWS)):
        wbd = lax.dynamic_update_slice(wbd, w_pool_l[gi], (gi * POOL_GROUP, gi * POOL_GROUP))
    return wbd.astype(BF)


def _unpack_small(small_all):
    ng = small_all[:, :16, :].reshape(N_CHIPS, DEPTH, 4, 256).transpose(1, 2, 0, 3).reshape(DEPTH, 4, D_MODEL)
    ps = small_all[:, 16:18, :POOL_GROUP].transpose(1, 0, 2).reshape(N_A, MAIN_W)
    return ng, ps


def _local_step(x, mem, positions, on_forward, on_backward, mem_norm, w_pool, kv_norm, target):
    B, S, _ = x.shape
    T = B * S
    xc = x.reshape(T, D_MODEL)
    memf = mem.reshape(B * N_MEM, D_MODEL)
    tgt = target.reshape(T, D_MODEL)
    cos, sin = _rope_tables(positions.reshape(T, 1), name="rope_tables")
    wbd = [_block_diag(w_pool[l]) for l in range(N_A)]
    nbo = D_FF // 256
    fw = []
    rk = rv = None
    kv_saved = None
    wts = []
    norm_gains = pool_scale = y2 = None

    for l in range(DEPTH):
        t = f"l{l}_"
        got = on_forward('start', l, y2)
        wts.append(dict(got[0]))
        if l == 0:
            norm_gains, pool_scale = _unpack_small(got[1])
        sv = {'x_in': xc}
        h0, sv['r0'] = _norm_fwd(xc, norm_gains[l, 0], name=t + "norm0", out_dtype=BF, tm=1024, after=got[2])
        z, = _mm(h0, wts[l]['w_in'], b_layer=0, name=t + "mm_in", tm=1024, tn=1024)
        memn, sv['rm'] = _norm_fwd(memf, mem_norm[l], name=t + "norm_mem", out_dtype=BF, tm=256)
        kvm, = _mm(memn, wts[l]['w_mem_kv'], b_layer=0, name=t + "mm_memkv", out_dtypes=(BF,))
        if l < N_A:
            ycat, sv['p'] = _pool_fwd(z, wbd[l], pool_scale[l], B, S, name=t + "pool_fwd")
        else:
            rq = _rope_apply(z, cos, sin, name=t + "rope_q", out_dtype=F32)
            o = lax.empty((T, MAIN_W), F32)
            lse = lax.empty((T, MAIN_W), F32)
            for g in range(3):
                o, lse = _dil_fwd(g, rq, rk, rv, o, lse, B, S, name=t + f"dil_fwd{g}")
            ycat = _dil_combine_fwd(o, lse, lax.empty((T, D_MODEL), BF), name=t + "dil_combine")
            sv.update(rq=rq, o=o, lse=lse)
        ycat, sv['lse_m'] = _memattn_fwd(z, kvm, ycat, B, S, name=t + "memattn_fwd")
        tok = on_forward('mid', l, ycat)
        y1, = _mm(ycat, wts[l]['w_out'], b_layer=0, name=t + "mm_out", tm=1024, tn=1024)
        wts[l].update(on_forward('ffn', l, y1)[0])
        x1, sv['r1'] = _norm_fwd(y1, norm_gains[l, 1], name=t + "norm1", res=xc, after=tok)
        h2, sv['r2'] = _norm_fwd(x1, norm_gains[l, 2], name=t + "norm2", out_dtype=BF, tm=1024)
        gg, uu, aa = _mm(h2, wts[l]['w_gate_up'], b_layer=0, b_offsets=(0, nbo), out_n=D_FF, tm=4096, tn=256, name=t + "mm_gate_up",
                         epilogue=_swiglu_fwd_epilogue, out_dtypes=(BF, BF, BF))
        on_forward('post', l, gg)
        y2, = _mm(aa, wts[l]['w_down'], b_layer=0, tk=D_FF, name=t + "mm_down")
        x2, sv['r3'] = _norm_fwd(y2, norm_gains[l, 3], name=t + "norm3", res=x1)
        sv.update(h0=h0, z=z, memn=memn, kvm=kvm, ycat=ycat, y1=y1, x1=x1, h2=h2, gg=gg, uu=uu, aa=aa, y2=y2)
        fw.append(sv)
        xc = x2
        if l == N_A - 1:
            kvn, rkv = _norm_fwd(xc, kv_norm, name="norm_kv", out_dtype=BF)
            kv, = _mm(kvn, wts[N_A - 1]['w_kv'], b_layer=0, name="mm_kv")
            rk, rv = _rope_apply(kv, cos, sin, name="rope_k", passthrough=True, out_dtype=F32)
            kv_saved = (xc, kvn, rkv)

    loss, dx = _loss(xc, tgt, name="loss")

    d_ng = [[None] * 4 for _ in range(DEPTH)]
    d_memnorm = [None] * DEPTH
    d_wbd = [None] * N_A
    d_pscale = [None] * N_A
    d_kvnorm = None
    kv_parts = []
    tok = None

    def as3d(gl):
        return {nm: g.reshape((1,) + g.shape) for nm, g in gl.items()}

    for l in reversed(range(DEPTH)):
        t = f"l{l}_b_"
        sv = fw[l]
        gl = {}
        dy2, d_ng[l][3] = _norm_bwd(dx, sv['y2'], sv['r3'], norm_gains[l, 3], name=t + "norm3", out_dtype=BF, tm=1024, after=tok)
        gl['w_down'], = _mm(sv['aa'], dy2, ta=True, tm=1408, tn=512, tk=4096, name=t + "dw_down")
        dg, du = _mm(dy2, wts[l]['w_down'], tb=True, b_layer=0, tm=1024, tn=1408, name=t + "d_act",
                     extras=((sv['gg'], 'tile'), (sv['uu'], 'tile')), epilogue=_swiglu_bwd_epilogue, out_dtypes=(BF, BF))
        gl['w_gate_up'], = _mm(sv['h2'], (dg, du), ta=True, tn=1408, tk=1024, name=t + "dw_gate_up")
        dh2, = _mm((dg, du), wts[l]['w_gate_up'], tb=True, b_layer=0, tn=1024, tk=1408, name=t + "d_h2", out_dtypes=(BF,))
        dx1, d_ng[l][2] = _norm_bwd(dh2, sv['x1'], sv['r2'], norm_gains[l, 2], name=t + "norm2", add=dx, tm=1024)
        tok = on_backward('ffn', l, dx1, as3d(gl))
        dy1, d_ng[l][1] = _norm_bwd(dx1, sv['y1'], sv['r1'], norm_gains[l, 1], name=t + "norm1", out_dtype=BF, tm=1024, after=tok)
        gl['w_out'], = _mm(sv['ycat'], dy1, ta=True, name=t + "dw_out", tk=4096)
        dycat, = _mm(dy1, wts[l]['w_out'], tb=True, b_layer=0, name=t + "d_ycat", tm=1024, tn=1024)
        dz = lax.empty((T, D_MODEL), BF)
        dz, dkm, dvm = _memattn_bwd(dycat, sv['z'], sv['kvm'], sv['lse_m'], dz, B, S, name=t + "memattn")
        if l < N_A:
            dz, d_wbd[l], d_pscale[l] = _pool_bwd(dycat, sv['p'], wbd[l], pool_scale[l], dz, B, S, name=t + "pool")
        else:
            do, cb = _dil_combine_bwd(dycat, sv['o'], sv['lse'], name=t + "dil_combine")
            acc = tuple(lax.empty((T, MAIN_W), F32) for _ in range(3))
            for g in range(3):
                acc = _dil_bwd(g, sv['rq'], rk, rv, do, cb, sv['lse'], acc, B, S, name=t + f"dil{g}")
            dz = _rope_apply(acc[0], cos, sin, name=t + "rope_q", sign=-1.0, alias=dz)
            kv_parts.append(acc[1:])
        tok = on_backward('mix', l, dz, as3d(gl))
        gl['w_in'], = _mm(sv['h0'], dz, ta=True, name=t + "dw_in", tk=4096)
        dh0, = _mm(dz, wts[l]['w_in'], tb=True, b_layer=0, name=t + "d_h0", out_dtypes=(BF,), tm=1024, tn=1024)
        dx, d_ng[l][0] = _norm_bwd(dh0, sv['x_in'], sv['r0'], norm_gains[l, 0], name=t + "norm0", add=dx1, tm=1024, after=tok)
        gl['w_mem_kv'], = _mm(sv['memn'], (dkm, dvm), ta=True, tn=256, name=t + "dw_memkv")
        dmemn, = _mm((dkm, dvm), wts[l]['w_mem_kv'], tb=True, b_layer=0, tk=256, name=t + "d_memn", out_dtypes=(BF,))
        _, d_memnorm[l] = _norm_bwd(dmemn, memf, sv['rm'], mem_norm[l], name=t + "norm_mem", out_dtype=BF, tm=256)
        if l == N_A:
            dk, dv = _kv_grad_sum(kv_parts, cos, sin, name="kv_grad")
            x_kv, kvn, rkv = kv_saved
            gl['w_kv'], = _mm(kvn, (dk, dv), ta=True, tn=768, tk=2048, name="dw_kv")
            dkvn, = _mm((dk, dv), wts[N_A - 1]['w_kv'], tb=True, b_layer=0, tn=1024, tk=768, name="d_kvn", out_dtypes=(BF,))
            dx, d_kvnorm = _norm_bwd(dkvn, x_kv, rkv, kv_norm, name="norm_kv_b", add=dx)
        tok = on_backward('end', l, dx, as3d(gl))

    small = {
        'norm_gains': jnp.stack([jnp.concatenate(d_ng[l], axis=0) for l in range(DEPTH)]),
        'mem_norm': jnp.concatenate(d_memnorm, axis=0),
        'kv_norm': d_kvnorm.reshape(D_MODEL),
        'pool_scale': jnp.concatenate(d_pscale, axis=0),
        'w_pool': jnp.stack([jnp.stack([d_wbd[l][gi * POOL_GROUP:(gi + 1) * POOL_GROUP, gi * POOL_GROUP:(gi + 1) * POOL_GROUP]
                                        for gi in range(len(POOL_WINDOWS))]) for l in range(N_A)]),
    }
    return loss, dx, small


SMALL_ORDER = ('norm_gains', 'mem_norm', 'kv_norm', 'pool_scale', 'w_pool')
SMALL_VEC_ROWS = 2560


def kernel(x, mem, positions, norm_gains, mem_norm, w_in, w_mem_kv, w_out, w_pool, pool_scale, kv_norm, w_kv, w_gate_up, w_down, loss_target, m_norm_gains, m_mem_norm, m_w_in, m_w_mem_kv, m_w_out, m_w_pool, m_pool_scale, m_kv_norm, m_w_kv, m_w_gate_up, m_w_down, v_norm_gains, v_mem_norm, v_w_in, v_w_mem_kv, v_w_out, v_w_pool, v_pool_scale, v_kv_norm, v_w_kv, v_w_gate_up, v_w_down):
    xi, yi, ci = lax.axis_index("x"), lax.axis_index("y"), lax.axis_index("c")
    s = 2 * xi + yi
    sc = jnp.stack([s, ci]).astype(jnp.int32)
    weights = dict(norm_gains=norm_gains, mem_norm=mem_norm, w_in=w_in, w_mem_kv=w_mem_kv, w_out=w_out, w_pool=w_pool,
                   pool_scale=pool_scale, kv_norm=kv_norm, w_kv=w_kv, w_gate_up=w_gate_up, w_down=w_down)
    moms = dict(norm_gains=m_norm_gains, mem_norm=m_mem_norm, w_in=m_w_in, w_mem_kv=m_w_mem_kv, w_out=m_w_out,
                w_pool=m_w_pool, pool_scale=m_pool_scale, kv_norm=m_kv_norm, w_kv=m_w_kv, w_gate_up=m_w_gate_up,
                w_down=m_w_down)
    vels = dict(norm_gains=v_norm_gains, mem_norm=v_mem_norm, w_in=v_w_in, w_mem_kv=v_w_mem_kv, w_out=v_w_out,
                w_pool=v_w_pool, pool_scale=v_pool_scale, kv_norm=v_kv_norm, w_kv=v_w_kv, w_gate_up=v_w_gate_up,
                w_down=v_w_down)

    small_w = jnp.zeros((SMALL_ROWS, 256), F32)
    small_w = lax.dynamic_update_slice(small_w, norm_gains.reshape(16, 256), (0, 0))
    small_w = lax.dynamic_update_slice(small_w, pool_scale, (16, 0))
    def shard_of(nm, l):
        return (w_kv.reshape(_shard_shape('w_kv')), 0) if nm == 'w_kv' else (weights[nm], l)

    groups = {'l0a': (0, MIX_W), 'l0b': (0, FFN_W)}
    groups.update({f"l{l}": (l, LAYER_W + (('w_kv',) if l == N_A - 1 else ())) for l in range(1, DEPTH)})
    on_ici, on_d2d, gathered = {}, {}, {}

    def start_group(tag, after):
        l, names = groups[tag]
        on_ici[tag] = _gather_start(tag, names, [shard_of(nm, l) for nm in names], small_w if tag == 'l0a' else None, sc,
                                    after)
        return [on_ici[tag]['token']]

    def on_forward(where, l, after):
        if where == 'start':
            if l == 0:
                start_group('l0a', None)
                st = on_ici.pop('l0a')
                fwd = _gather_forward(st, st['token'])
                w, small_all = _gather_finish(fwd, fwd['token'])
                return w, small_all, start_group('l0b', w['w_in'])
            if f"l{l}" not in on_d2d:
                on_d2d[f"l{l}"] = _gather_forward(on_ici.pop(f"l{l}"), after)
            gathered[l] = _gather_finish(on_d2d.pop(f"l{l}"), after)[0]
            tok = start_group(f"l{l + 1}", gathered[l]['w_in']) if l + 1 < DEPTH else None
            return {nm: w for nm, w in gathered[l].items() if nm not in FFN_W}, None, tok
        if where == 'mid' and l == 0:
            on_d2d['l0b'] = _gather_forward(on_ici.pop('l0b'), after)
            return start_group('l1', on_d2d['l0b']['token'])
        if where == 'ffn':
            if l == 0:
                return (_gather_finish(on_d2d.pop('l0b'), after)[0],)
            return ({nm: gathered[l][nm] for nm in FFN_W},)
        if where == 'post' and 0 < l < DEPTH - 1:
            on_d2d[f"l{l + 1}"] = _gather_forward(on_ici.pop(f"l{l + 1}"), after)
        return None

    hook_of = {'ffn': 0, 'mix': 1, 'end': 2}
    active, reduced = [], {l: {} for l in range(DEPTH)}
    advance = {'mid': lambda st, after: _reduce_mid(st, after, sc), 'late': lambda st, after: _reduce_late(st, after, sc)}

    def run_hook(idx, after):
        toks = []
        for grp in list(active):
            while grp['plan'] and grp['plan'][0][1] <= idx:
                step = grp['plan'].pop(0)[0]
                if step == 'finish':
                    reduced[grp['layer']].update(_reduce_finish(grp['st'], after))
                    active.remove(grp)
                else:
                    grp['st'] = advance[step](grp['st'], after)
                    toks.append(grp['st']['token'])
        return toks

    def on_backward(where, l, after, grads):
        idx = 3 * (DEPTH - 1 - l) + hook_of[where]
        toks = run_hook(idx, after)
        if where == 'end' or (where == 'ffn' and l == 0):
            names = FFN_W if where == 'ffn' else tuple(nm for nm in grads if l > 0 or nm not in FFN_W)
            st = _reduce_start(f"l{l}_{where}_grads", names, {nm: grads[nm] for nm in names})
            plan = [('mid', idx + 1), ('late', idx + 3), ('finish', idx + 4)] if where == 'ffn' else \
                   [('mid', idx + 1), ('late', idx + 2), ('finish', idx + 3)]
            active.append(dict(layer=l, st=st, plan=plan))
            toks.append(st['token'])
        return toks

    loss, gx, gsmall = _local_step(x, mem, positions, on_forward, on_backward, mem_norm, w_pool, kv_norm, loss_target)
    loss = lax.psum(loss[0, 0], ("x", "y", "c"))

    vec = jnp.concatenate([gsmall[nm].reshape(-1) for nm in SMALL_ORDER])
    vec = jnp.pad(vec, (0, SMALL_VEC_ROWS * 128 - vec.shape[0])).reshape(SMALL_VEC_ROWS, 128)
    vec = vec + sum(grp['st']['token'][0, 0] for grp in active)
    small_st = _small_gather_start(vec.astype(BF), sc)
    outs = {nm: None for nm in LAYER_W}

    def adamw_layers(layers, names, after):
        for l in layers:
            for nm in names:
                outs[nm] = _adamw_layer(nm, l, weights[nm], reduced[l][nm], moms[nm], vels[nm], outs[nm], after)
                after = outs[nm][0]
        return after

    def zero_of(toks, st):
        return sum(toks) if toks else st['token']

    last = 3 * DEPTH
    toks = run_hook(last, small_st['token'])
    done = adamw_layers(range(DEPTH - 1, 0, -1), LAYER_W, zero_of(toks, small_st))
    toks = run_hook(last + 1, done)
    done = adamw_layers([0], FFN_W, zero_of(toks, small_st))
    tot = _sum8(_small_gather_finish(small_st, done), name="sum_small_grads", tr=512)
    run_hook(last + 2, tot)
    assert not active
    adamw_layers([0], MIX_W, None)
    tot = tot.reshape(-1)
    grads, off = {}, 0
    for nm in SMALL_ORDER:
        shape = (DEPTH, 4, D_MODEL) if nm == 'norm_gains' else (N_A, MAIN_W) if nm == 'pool_scale' else weights[nm].shape
        n = 1
        for dim in shape:
            n *= dim
        grads[nm] = tot[off:off + n].reshape(shape)
        off += n
    grads['norm_gains'] = lax.dynamic_slice(grads['norm_gains'], (0, 0, s * 256), (DEPTH, 4, 256))
    grads['pool_scale'] = lax.dynamic_slice(grads['pool_scale'], (0, s * POOL_GROUP), (N_A, POOL_GROUP))
    grads['w_kv'] = reduced[N_A]['w_kv'].reshape(w_kv.shape)

    order = ('norm_gains', 'mem_norm', 'w_in', 'w_mem_kv', 'w_out', 'w_pool', 'pool_scale', 'kv_norm', 'w_kv',
             'w_gate_up', 'w_down')
    deltas, new_m, new_v = {}, {}, {}
    for nm in order:
        if nm in LAYER_W:
            deltas[nm], new_m[nm], new_v[nm], grads[nm] = outs[nm]
        else:
            deltas[nm], new_m[nm], new_v[nm] = _adamw(weights[nm], grads[nm], moms[nm], vels[nm], name="adamw_" + nm)
    return (loss, gx.reshape(x.shape), *[grads[nm] for nm in order], *[deltas[nm] for nm in order],
            *[new_m[nm] for nm in order], *[new_v[nm] for nm in order])
```

```python
import functools

import jax
import jax.numpy as jnp
from jax import lax
from jax.experimental import pallas as pl
from jax.experimental.pallas import tpu as pltpu

F32 = jnp.float32
BF = jnp.bfloat16

D_MODEL = 1024
DEPTH = 4
N_A = 2
HEAD_DIM = 64
MEM_W = 256
MAIN_W = 768
D_FF = 2816
N_MEM = 256
POOL_WINDOWS = (2, 4, 8, 16)
POOL_GROUP = 192
DIL = (1, 4, 16)
STEPS = 128
ROPE_THETA = 10000.0
EPS = 1e-6
SCALE = HEAD_DIM ** -0.5
NEG = -1e30

ADAM_LR = 0.001
ADAM_B1 = 0.9
ADAM_B2 = 0.999
ADAM_EPS = 1e-08
ADAM_WD = 0.01
ADAM_STEP = 10

VMEM_LIMIT = 48 * 1024 * 1024
MESH = pl.DeviceIdType.MESH


def _cp(sem):
    return pltpu.CompilerParams(dimension_semantics=sem, vmem_limit_bytes=VMEM_LIMIT)


def _mm(a, b, *, name, ta=False, tb=False, tm=1024, tn=512, tk=1024, b_layer=None, b_offsets=(0,),
        extras=(), epilogue=None, out_dtypes=(F32,), out_n=None):
    a_pair = isinstance(a, (tuple, list))
    b_pair = isinstance(b, (tuple, list))
    a0 = a[0] if a_pair else a
    b0 = b[0] if b_pair else b
    a_rows, a_cols = a0.shape
    if a_pair:
        a_cols *= 2
    b_rows, b_cols = b0.shape[-2:]
    if b_pair:
        b_cols *= 2
    M, K = (a_cols, a_rows) if ta else (a_rows, a_cols)
    N = b_rows if tb else b_cols
    if out_n is not None:
        N = out_n
    tm, tn, tk = min(tm, M), min(tn, N), min(tk, K)
    assert M % tm == 0 and N % tn == 0 and K % tk == 0, (name, M, N, K, tm, tn, tk)
    nk = K // tk
    n_acc = len(b_offsets)

    if a_pair:
        a_half = (a0.shape[1] // (tm if ta else tk))
    if b_pair:
        b_half = (b0.shape[1] // (tk if tb else tn))

    def a_map(sel):
        def f(i, j, k):
            r, c = (k, i) if ta else (i, k)
            if a_pair:
                c = jnp.clip(c - sel * a_half, 0, a_half - 1)
            return (r, c)
        return f

    def b_map(sel, off):
        def f(i, j, k):
            r, c = (j + off, k) if tb else (k, j + off)
            if b_pair:
                c = jnp.clip(c - sel * b_half, 0, b_half - 1)
            if b_layer is not None:
                return (b_layer, r, c)
            return (r, c)
        return f

    a_blk = (tk, tm) if ta else (tm, tk)
    b_blk = (tn, tk) if tb else (tk, tn)
    if b_layer is not None:
        b_blk = (None,) + b_blk
    in_specs, operands = [], []
    for sel in range(2 if a_pair else 1):
        in_specs.append(pl.BlockSpec(a_blk, a_map(sel)))
        operands.append(a[sel] if a_pair else a)
    n_a = len(operands)
    for off in b_offsets:
        for sel in range(2 if b_pair else 1):
            in_specs.append(pl.BlockSpec(b_blk, b_map(sel, off)))
            operands.append(b[sel] if b_pair else b)
    n_b = len(operands) - n_a
    for arr, kind in extras:
        if kind == 'tile':
            in_specs.append(pl.BlockSpec((tm, tn), lambda i, j, k: (i, j)))
        elif kind == 'row':
            in_specs.append(pl.BlockSpec((tm, 1), lambda i, j, k: (i, 0)))
        else:
            in_specs.append(pl.BlockSpec((1, tn), lambda i, j, k: (0, j)))
        operands.append(arr)
    n_e = len(extras)
    n_o = len(out_dtypes)
    dims = (((0,) if ta else (1,), (1,) if tb else (0,)), ((), ()))

    def body(*refs):
        a_refs = refs[:n_a]
        b_refs = refs[n_a:n_a + n_b]
        e_refs = refs[n_a + n_b:n_a + n_b + n_e]
        n_in = n_a + n_b + n_e
        o_refs = refs[n_in:n_in + n_o]
        acc_refs = refs[n_in + n_o:]
        i, j, k = pl.program_id(0), pl.program_id(1), pl.program_id(2)
        if a_pair:
            cidx = i if ta else k
            av = jnp.where(cidx < a_half, a_refs[0][...], a_refs[1][...])
        else:
            av = a_refs[0][...]
        av = av.astype(BF)
        prods = []
        for q in range(n_acc):
            if b_pair:
                cidx = (k if tb else j) + b_offsets[q]
                bv = jnp.where(cidx < b_half, b_refs[2 * q][...], b_refs[2 * q + 1][...])
            else:
                bv = b_refs[q][...]
            prods.append(lax.dot_general(av, bv.astype(BF), dims, preferred_element_type=F32))

        def finish(accs):
            outs = epilogue(accs, *[r[...] for r in e_refs]) if epilogue is not None else accs
            for o_ref, o in zip(o_refs, outs):
                o_ref[...] = o.astype(o_ref.dtype)

        if nk == 1:
            finish(prods)
        else:
            @pl.when(k == 0)
            def _():
                for r, p in zip(acc_refs, prods):
                    r[...] = p

            @pl.when(k > 0)
            def _():
                for r, p in zip(acc_refs, prods):
                    r[...] += p

            @pl.when(k == nk - 1)
            def _():
                finish([r[...] for r in acc_refs])

    return pl.pallas_call(
        body, name=name,
        grid=(M // tm, N // tn, nk),
        in_specs=in_specs,
        out_specs=[pl.BlockSpec((tm, tn), lambda i, j, k: (i, j)) for _ in range(n_o)],
        out_shape=[jax.ShapeDtypeStruct((M, N), dt) for dt in out_dtypes],
        scratch_shapes=[pltpu.VMEM((tm, tn), F32) for _ in range(n_acc if nk > 1 else 0)],
        compiler_params=_cp(("parallel", "parallel", "arbitrary")),
    )(*operands)


def _norm_fwd(x, g, *, name, res=None, out_dtype=F32, tm=512, after=None):
    T, Dm = x.shape
    has_res = res is not None
    after = list(after or [])

    def body(*refs):
        refs = refs[:len(refs) - 1 - len(after)] + refs[len(refs) - 1:]
        if has_res:
            x_ref, g_ref, r_ref, y_ref = refs
        else:
            x_ref, g_ref, y_ref = refs
        xv = x_ref[...]
        rstd = lax.rsqrt(jnp.mean(xv * xv, axis=-1, keepdims=True) + EPS)
        y = xv * rstd * g_ref[...]
        if has_res:
            y = r_ref[...] + y
        y_ref[...] = y.astype(y_ref.dtype)

    row = pl.BlockSpec((tm, Dm), lambda i: (i, 0))
    in_specs = [row, pl.BlockSpec((1, Dm), lambda i: (0, 0))] + ([row] if has_res else [])
    in_specs += [pl.BlockSpec(memory_space=pl.ANY)] * len(after)
    ops = [x, g.reshape(1, Dm)] + ([res] if has_res else []) + after
    return pl.pallas_call(
        body, name=name, grid=(T // tm,), in_specs=in_specs,
        out_specs=row,
        out_shape=jax.ShapeDtypeStruct((T, Dm), out_dtype),
        compiler_params=_cp(("parallel",)),
    )(*ops)


def _norm_bwd(dout, x, g, *, name, add=None, out_dtype=F32, tm=512, after=None):
    T, Dm = x.shape
    has_add = add is not None
    nt = T // tm
    after = list(after or [])

    def body(*refs):
        refs = refs[:len(refs) - 3 - len(after)] + refs[len(refs) - 3:]
        if has_add:
            do_ref, x_ref, g_ref, a_ref, dx_ref, dg_ref, acc = refs
        else:
            do_ref, x_ref, g_ref, dx_ref, dg_ref, acc = refs
        i = pl.program_id(0)
        do = do_ref[...].astype(F32)
        xv = x_ref[...]
        rstd = lax.rsqrt(jnp.mean(xv * xv, axis=-1, keepdims=True) + EPS)
        xh = xv * rstd
        gd = do * g_ref[...]
        dx = rstd * (gd - xh * jnp.mean(gd * xh, axis=-1, keepdims=True))
        if has_add:
            dx = dx + a_ref[...].astype(F32)
        dx_ref[...] = dx.astype(dx_ref.dtype)
        part = jnp.sum((do * xh).reshape(tm // 8, 8, Dm), axis=0)

        @pl.when(i == 0)
        def _():
            acc[...] = part

        @pl.when(i > 0)
        def _():
            acc[...] += part

        @pl.when(i == nt - 1)
        def _():
            dg_ref[...] = jnp.sum(acc[...], axis=0, keepdims=True)

    row = pl.BlockSpec((tm, Dm), lambda i: (i, 0))
    in_specs = [row, row, pl.BlockSpec((1, Dm), lambda i: (0, 0))]
    ops = [dout, x, g.reshape(1, Dm)]
    if has_add:
        in_specs.append(row)
        ops.append(add)
    in_specs += [pl.BlockSpec(memory_space=pl.ANY)] * len(after)
    ops += after
    return pl.pallas_call(
        body, name=name, grid=(nt,), in_specs=in_specs,
        out_specs=[row, pl.BlockSpec((1, Dm), lambda i: (0, 0))],
        out_shape=[jax.ShapeDtypeStruct((T, Dm), out_dtype), jax.ShapeDtypeStruct((1, Dm), F32)],
        scratch_shapes=[pltpu.VMEM((8, Dm), F32)],
        compiler_params=_cp(("arbitrary",)),
    )(*ops)


def _swiglu_fwd_epilogue(accs):
    g, u = accs
    return g, u, g * jax.nn.sigmoid(g) * u


def _swiglu_bwd_epilogue(accs, g, u):
    da = accs[0]
    g = g.astype(F32)
    u = u.astype(F32)
    sig = jax.nn.sigmoid(g)
    return da * u * (sig * (1.0 + g * (1.0 - sig))), da * (g * sig)


def _rope_tables(pos, *, name, tm=1024):
    T = pos.shape[0]
    half = HEAD_DIM // 2
    freqs = ROPE_THETA ** (-jnp.arange(half, dtype=F32) / half)
    freqs = jnp.tile(freqs, 4).reshape(1, 128)

    def body(p_ref, f_ref, c_ref, s_ref):
        ang = p_ref[...].astype(F32) * f_ref[...]
        lane = lax.broadcasted_iota(jnp.int32, ang.shape, 1)
        c_ref[...] = jnp.cos(ang)
        s_ref[...] = jnp.where(lane % HEAD_DIM < half, -1.0, 1.0) * jnp.sin(ang)

    tab = pl.BlockSpec((tm, 128), lambda i: (i, 0))
    return pl.pallas_call(
        body, name=name, grid=(T // tm,),
        in_specs=[pl.BlockSpec((tm, 1), lambda i: (i, 0)), pl.BlockSpec((1, 128), lambda i: (0, 0))],
        out_specs=[tab, tab],
        out_shape=[jax.ShapeDtypeStruct((T, 128), F32)] * 2,
        compiler_params=_cp(("parallel",)),
    )(pos, freqs)


def _rot(x, cos, sin, sign):
    W = x.shape[1]
    half = HEAD_DIM // 2
    reps = W // 128
    c = jnp.concatenate([cos] * reps, axis=1) if reps > 1 else cos
    s = jnp.concatenate([sin] * reps, axis=1) if reps > 1 else sin
    lane = lax.broadcasted_iota(jnp.int32, x.shape, 1)
    swapped = jnp.where(lane % HEAD_DIM < half, pltpu.roll(x, W - half, axis=1), pltpu.roll(x, half, axis=1))
    return x * c + (sign * s) * swapped


def _rope_apply(x, cos, sin, *, name, sign=1.0, width=MAIN_W, passthrough=False, out_dtype=BF, alias=None,
                out_cols=None, tm=512):
    T = x.shape[0]

    def body(*refs):
        if passthrough:
            x_ref, v_ref, c_ref, s_ref, o_ref, ov_ref = refs
            ov_ref[...] = v_ref[...].astype(ov_ref.dtype)
        elif alias is not None:
            x_ref, c_ref, s_ref, _, o_ref = refs
        else:
            x_ref, c_ref, s_ref, o_ref = refs
        o_ref[...] = _rot(x_ref[...].astype(F32), c_ref[...], s_ref[...], sign).astype(o_ref.dtype)

    blk0 = pl.BlockSpec((tm, width), lambda i: (i, 0))
    blk1 = pl.BlockSpec((tm, width), lambda i: (i, 1))
    tab = pl.BlockSpec((tm, 128), lambda i: (i, 0))
    if passthrough:
        return pl.pallas_call(
            body, name=name, grid=(T // tm,), in_specs=[blk0, blk1, tab, tab], out_specs=[blk0, blk0],
            out_shape=[jax.ShapeDtypeStruct((T, width), out_dtype)] * 2,
            compiler_params=_cp(("parallel",)),
        )(x, x, cos, sin)
    if alias is not None:
        return pl.pallas_call(
            body, name=name, grid=(T // tm,),
            in_specs=[blk0, tab, tab, pl.BlockSpec(memory_space=pl.ANY)], out_specs=blk0,
            out_shape=jax.ShapeDtypeStruct(alias.shape, alias.dtype),
            input_output_aliases={3: 0},
            compiler_params=_cp(("parallel",)),
        )(x, cos, sin, alias)
    return pl.pallas_call(
        body, name=name, grid=(T // tm,), in_specs=[blk0, tab, tab], out_specs=blk0,
        out_shape=jax.ShapeDtypeStruct((T, width), out_dtype),
        compiler_params=_cp(("parallel",)),
    )(x, cos, sin)


POOL_T = 256
POOL_HALO = 16


def _pool_lane_window(shape):
    lane = lax.broadcasted_iota(jnp.int32, shape, 1)
    w = jnp.full(shape, POOL_WINDOWS[0], jnp.int32)
    for gi in range(1, len(POOL_WINDOWS)):
        w = jnp.where(lane >= gi * POOL_GROUP, POOL_WINDOWS[gi], w)
    return w


POOL_PAD = 32
POOL_R = POOL_T + POOL_PAD


def _pool_window_sums(buf, tmp_a, tmp_b, win, back):
    src, dst, acc = buf, tmp_a, None
    for j, (w, sh) in enumerate(zip(POOL_WINDOWS, (1, 2, 4, 8)), start=1):
        n = POOL_R - 8 * j
        if back:
            dst[pl.ds(8 * j, n), :] = src[pl.ds(8 * j, n), :] + src[pl.ds(8 * j - sh, n), :]
            cur = dst[pl.ds(POOL_PAD, POOL_T), :]
        else:
            dst[pl.ds(0, n), :] = src[pl.ds(0, n), :] + src[pl.ds(sh, n), :]
            cur = dst[pl.ds(0, POOL_T), :]
        acc = cur if acc is None else jnp.where(win >= w, cur, acc)
        src, dst = dst, (tmp_b if dst is tmp_a else tmp_a)
    return acc


def _pool_fwd(z, wbd, scale, B, S, *, name):
    T = z.shape[0]
    nt = S // POOL_T
    hb = POOL_T // POOL_PAD

    def body(z_ref, h_ref, w_ref, sc_ref, y_ref, p_ref, ext, tmp_a, tmp_b):
        i = pl.program_id(1)
        u = z_ref[...]
        ext[pl.ds(POOL_PAD, POOL_T), :] = u
        ext[pl.ds(0, POOL_PAD), :] = jnp.where(i > 0, h_ref[...], 0.0)
        win = _pool_lane_window((POOL_T, MAIN_W))
        acc = _pool_window_sums(ext, tmp_a, tmp_b, win, True)
        t = i * POOL_T + lax.broadcasted_iota(jnp.int32, (POOL_T, MAIN_W), 0)
        cnt = jnp.minimum(t + 1, win).astype(F32)
        p = (acc / cnt - u).astype(BF)
        p_ref[...] = p
        y = jnp.dot(p, w_ref[...], preferred_element_type=F32) * sc_ref[...]
        y_ref[...] = y.astype(y_ref.dtype)

    return pl.pallas_call(
        body, name=name, grid=(B, nt),
        in_specs=[pl.BlockSpec((POOL_T, MAIN_W), lambda b, i: (b * nt + i, 0)),
                  pl.BlockSpec((POOL_PAD, MAIN_W), lambda b, i: (jnp.maximum((b * nt + i) * hb - 1, 0), 0)),
                  pl.BlockSpec((MAIN_W, MAIN_W), lambda b, i: (0, 0)),
                  pl.BlockSpec((1, MAIN_W), lambda b, i: (0, 0))],
        out_specs=[pl.BlockSpec((POOL_T, MAIN_W), lambda b, i: (b * nt + i, 0)),
                   pl.BlockSpec((POOL_T, MAIN_W), lambda b, i: (b * nt + i, 0))],
        out_shape=[jax.ShapeDtypeStruct((T, D_MODEL), BF), jax.ShapeDtypeStruct((T, MAIN_W), BF)],
        scratch_shapes=[pltpu.VMEM((POOL_R, MAIN_W), F32)] * 3,
        compiler_params=_cp(("parallel", "parallel")),
    )(z, z, wbd, scale.reshape(1, MAIN_W))


def _pool_bwd(dy, p, wbd, scale, dz_alias, B, S, *, name):
    T = dy.shape[0]
    nt = S // POOL_T
    hb = POOL_T // POOL_HALO
    last_halo = T // POOL_HALO - 1

    def body(dy_ref, dyn_ref, p_ref, pn_ref, w_ref, sc_ref, _, dz_ref, dw_ref, ds_ref, ext, tmp_a, tmp_b, dw_acc, ds_acc):
        b, i = pl.program_id(0), pl.program_id(1)
        first = jnp.logical_and(b == 0, i == 0)
        dyv = dy_ref[...]
        pv = p_ref[...]
        sc = sc_ref[...]
        w = w_ref[...]
        pw = jnp.dot(pv, w, preferred_element_type=F32)
        ds_part = jnp.sum((dyv * pw).reshape(POOL_T // 8, 8, MAIN_W), axis=0)
        dpw = (dyv * sc).astype(BF)
        dw_part = lax.dot_general(pv, dpw, (((0,), (0,)), ((), ())), preferred_element_type=F32)

        @pl.when(first)
        def _():
            dw_acc[...] = dw_part
            ds_acc[...] = ds_part

        @pl.when(jnp.logical_not(first))
        def _():
            dw_acc[...] += dw_part
            ds_acc[...] += ds_part

        @pl.when(jnp.logical_and(b == pl.num_programs(0) - 1, i == nt - 1))
        def _():
            dw_ref[...] = dw_acc[...]
            ds_ref[...] = jnp.sum(ds_acc[...], axis=0, keepdims=True)

        dp = lax.dot_general(dpw, w, (((1,), (1,)), ((), ())), preferred_element_type=F32)
        dpn = lax.dot_general((dyn_ref[...] * sc).astype(BF), w, (((1,), (1,)), ((), ())), preferred_element_type=F32)
        win = _pool_lane_window((POOL_T, MAIN_W))
        win_n = _pool_lane_window((POOL_HALO, MAIN_W))
        t = i * POOL_T + lax.broadcasted_iota(jnp.int32, (POOL_T, MAIN_W), 0)
        tn = (i + 1) * POOL_T + lax.broadcasted_iota(jnp.int32, (POOL_HALO, MAIN_W), 0)
        ext[pl.ds(0, POOL_T), :] = dp / jnp.minimum(t + 1, win).astype(F32)
        ext[pl.ds(POOL_T, POOL_HALO), :] = jnp.where(i < nt - 1, dpn / jnp.minimum(tn + 1, win_n).astype(F32), 0.0)
        ext[pl.ds(POOL_T + POOL_HALO, POOL_PAD - POOL_HALO), :] = jnp.zeros((POOL_PAD - POOL_HALO, MAIN_W), F32)
        acc = _pool_window_sums(ext, tmp_a, tmp_b, win, False) - dp
        dz_ref[...] = acc.astype(dz_ref.dtype)

    cur = lambda b, i: (b * nt + i, 0)
    nxt = lambda b, i: (jnp.minimum((b * nt + i + 1) * hb, last_halo), 0)
    return pl.pallas_call(
        body, name=name, grid=(B, nt),
        in_specs=[pl.BlockSpec((POOL_T, MAIN_W), cur), pl.BlockSpec((POOL_HALO, MAIN_W), nxt),
                  pl.BlockSpec((POOL_T, MAIN_W), cur), pl.BlockSpec((POOL_HALO, MAIN_W), nxt),
                  pl.BlockSpec((MAIN_W, MAIN_W), lambda b, i: (0, 0)),
                  pl.BlockSpec((1, MAIN_W), lambda b, i: (0, 0)),
                  pl.BlockSpec(memory_space=pl.ANY)],
        out_specs=[pl.BlockSpec((POOL_T, MAIN_W), cur),
                   pl.BlockSpec((MAIN_W, MAIN_W), lambda b, i: (0, 0)),
                   pl.BlockSpec((1, MAIN_W), lambda b, i: (0, 0))],
        out_shape=[jax.ShapeDtypeStruct(dz_alias.shape, dz_alias.dtype),
                   jax.ShapeDtypeStruct((MAIN_W, MAIN_W), F32), jax.ShapeDtypeStruct((1, MAIN_W), F32)],
        scratch_shapes=[pltpu.VMEM((POOL_R, MAIN_W), F32)] * 3 + [pltpu.VMEM((MAIN_W, MAIN_W), F32), pltpu.VMEM((8, MAIN_W), F32)],
        input_output_aliases={6: 0},
        compiler_params=_cp(("arbitrary", "arbitrary")),
    )(dy, dy, p, p, wbd, scale.reshape(1, MAIN_W), dz_alias)


def _head_masks(shape):
    lane = lax.broadcasted_iota(jnp.int32, shape, 1)
    return [(lane // HEAD_DIM) == h for h in range(shape[1] // HEAD_DIM)]


def _row_of(bcast, mask):
    return jnp.max(jnp.where(mask, bcast, -jnp.inf), axis=-1, keepdims=True)


MEM_TQ = 512


def _memattn_fwd(z, kv, y_alias, B, S, *, name):
    T = z.shape[0]
    nt = S // MEM_TQ

    def body(q_ref, k_ref, v_ref, _, y_ref, l_ref):
        q = q_ref[...]
        k = k_ref[...]
        v = v_ref[...]
        masks = _head_masks(q.shape)
        o = jnp.zeros(q.shape, F32)
        lse_b = jnp.zeros(q.shape, F32)
        for m in masks:
            qm = jnp.where(m, q, 0.0).astype(BF)
            s = lax.dot_general(qm, k, (((1,), (1,)), ((), ())), preferred_element_type=F32) * SCALE
            mx = jnp.max(s, axis=-1, keepdims=True)
            e = jnp.exp(s - mx)
            l = jnp.sum(e, axis=-1, keepdims=True)
            p = (e / l).astype(BF)
            o = o + jnp.where(m, jnp.dot(p, v, preferred_element_type=F32), 0.0)
            lse_b = lse_b + jnp.where(m, mx + jnp.log(l), 0.0)
        y_ref[...] = o.astype(y_ref.dtype)
        l_ref[...] = lse_b

    qblk = pl.BlockSpec((MEM_TQ, MEM_W), lambda b, i: (b * nt + i, 3))
    return pl.pallas_call(
        body, name=name, grid=(B, nt),
        in_specs=[qblk, pl.BlockSpec((N_MEM, MEM_W), lambda b, i: (b, 0)), pl.BlockSpec((N_MEM, MEM_W), lambda b, i: (b, 1)),
                  pl.BlockSpec(memory_space=pl.ANY)],
        out_specs=[qblk, pl.BlockSpec((MEM_TQ, MEM_W), lambda b, i: (b * nt + i, 0))],
        out_shape=[jax.ShapeDtypeStruct(y_alias.shape, y_alias.dtype), jax.ShapeDtypeStruct((T, MEM_W), F32)],
        input_output_aliases={3: 0},
        compiler_params=_cp(("parallel", "parallel")),
    )(z, kv, kv, y_alias)


def _memattn_bwd(dy, z, kv, lse, dz_alias, B, S, *, name):
    nt = S // MEM_TQ

    def body(do_ref, q_ref, k_ref, v_ref, l_ref, _, dz_ref, dk_ref, dv_ref, dk_acc, dv_acc):
        i = pl.program_id(1)
        do = do_ref[...]
        q = q_ref[...]
        k = k_ref[...]
        v = v_ref[...]
        lse_b = l_ref[...]
        masks = _head_masks(q.shape)
        dq = jnp.zeros(q.shape, F32)
        dk = jnp.zeros(k.shape, F32)
        dv = jnp.zeros(v.shape, F32)
        for m in masks:
            qm = jnp.where(m, q, 0.0).astype(BF)
            dom = jnp.where(m, do, 0.0).astype(BF)
            s = lax.dot_general(qm, k, (((1,), (1,)), ((), ())), preferred_element_type=F32) * SCALE
            p = jnp.exp(s - _row_of(lse_b, m))
            dp = lax.dot_general(dom, v, (((1,), (1,)), ((), ())), preferred_element_type=F32)
            delta = jnp.sum(p * dp, axis=-1, keepdims=True)
            ds = (p * (dp - delta) * SCALE).astype(BF)
            pb = p.astype(BF)
            dv = dv + jnp.where(m[:N_MEM], lax.dot_general(pb, dom, (((0,), (0,)), ((), ())), preferred_element_type=F32), 0.0)
            dk = dk + jnp.where(m[:N_MEM], lax.dot_general(ds, qm, (((0,), (0,)), ((), ())), preferred_element_type=F32), 0.0)
            dq = dq + jnp.where(m, jnp.dot(ds, k, preferred_element_type=F32), 0.0)
        dz_ref[...] = dq.astype(dz_ref.dtype)

        @pl.when(i == 0)
        def _():
            dk_acc[...] = dk
            dv_acc[...] = dv

        @pl.when(i > 0)
        def _():
            dk_acc[...] += dk
            dv_acc[...] += dv

        @pl.when(i == nt - 1)
        def _():
            dk_ref[...] = dk_acc[...]
            dv_ref[...] = dv_acc[...]

    qblk = pl.BlockSpec((MEM_TQ, MEM_W), lambda b, i: (b * nt + i, 3))
    kblk = pl.BlockSpec((N_MEM, MEM_W), lambda b, i: (b, 0))
    return pl.pallas_call(
        body, name=name, grid=(B, nt),
        in_specs=[qblk, qblk, kblk, pl.BlockSpec((N_MEM, MEM_W), lambda b, i: (b, 1)),
                  pl.BlockSpec((MEM_TQ, MEM_W), lambda b, i: (b * nt + i, 0)), pl.BlockSpec(memory_space=pl.ANY)],
        out_specs=[qblk, kblk, kblk],
        out_shape=[jax.ShapeDtypeStruct(dz_alias.shape, dz_alias.dtype),
                   jax.ShapeDtypeStruct((B * N_MEM, MEM_W), F32), jax.ShapeDtypeStruct((B * N_MEM, MEM_W), F32)],
        scratch_shapes=[pltpu.VMEM((N_MEM, MEM_W), F32), pltpu.VMEM((N_MEM, MEM_W), F32)],
        input_output_aliases={5: 0},
        compiler_params=_cp(("parallel", "arbitrary")),
    )(dy, z, kv, kv, lse, dz_alias)


N_UNITS = 16


def _unit_rows(g):
    d = DIL[g]
    nb = N_UNITS // d
    return [pl.ds(n * STEPS * d + r, STEPS, stride=d) if d > 1 else pl.ds(n * STEPS, STEPS)
            for n in range(nb) for r in range(d)]


def _load_units(ref, g):
    if DIL[g] == 1:
        return ref[...].reshape(N_UNITS, STEPS, 128)
    return jnp.stack([ref[rows, :] for rows in _unit_rows(g)])


def _store_units(ref, val, g):
    if DIL[g] == 1:
        ref[...] = val.reshape(N_UNITS * STEPS, 128)
    else:
        for u, rows in enumerate(_unit_rows(g)):
            ref[rows, :] = val[u]


def _shift_units(x, by):
    z = jnp.zeros((abs(by),) + x.shape[1:], x.dtype)
    return jnp.concatenate([z, x[:N_UNITS - by]], axis=0) if by > 0 else jnp.concatenate([x[-by:], z], axis=0)


def _bdot(a, b, ca, cb):
    return lax.dot_general(a, b, (((ca,), (cb,)), ((0,), (0,))), preferred_element_type=F32)


def _dil_masks(g):
    d = DIL[g]
    has_prev = N_UNITS // d > 1
    qi = lax.broadcasted_iota(jnp.int32, (1, STEPS, STEPS), 1)
    kj = lax.broadcasted_iota(jnp.int32, (1, STEPS, STEPS), 2)
    unit = lax.broadcasted_iota(jnp.int32, (N_UNITS, 1, 1), 0)
    cur = kj <= qi
    prev = jnp.logical_and(kj >= qi, unit >= d) if has_prev else None
    lane = lax.broadcasted_iota(jnp.int32, (1, 1, 128), 2)
    heads = [(lane // HEAD_DIM) == h for h in range(128 // HEAD_DIM)]
    return has_prev, cur, prev, heads


def _dil_fwd(g, q, k, v, o_alias, l_alias, B, S, *, name):
    assert S == N_UNITS * STEPS
    d = DIL[g]

    def body(q_ref, k_ref, v_ref, _, __, o_ref, l_ref):
        has_prev, cur, prev, heads = _dil_masks(g)
        q = _load_units(q_ref, g)
        kc = _load_units(k_ref, g).astype(BF)
        vc = _load_units(v_ref, g).astype(BF)
        if has_prev:
            kp, vp = _shift_units(kc, d), _shift_units(vc, d)
        o = jnp.zeros(q.shape, F32)
        lse_b = jnp.zeros(q.shape, F32)
        for m in heads:
            qm = jnp.where(m, q, 0.0).astype(BF)
            sc = jnp.where(cur, _bdot(qm, kc, 2, 2) * SCALE, NEG)
            mx = jnp.max(sc, axis=-1, keepdims=True)
            if has_prev:
                sp = jnp.where(prev, _bdot(qm, kp, 2, 2) * SCALE, NEG)
                mx = jnp.maximum(mx, jnp.max(sp, axis=-1, keepdims=True))
            l = jnp.sum(jnp.exp(sc - mx), axis=-1, keepdims=True)
            if has_prev:
                l = l + jnp.sum(jnp.exp(sp - mx), axis=-1, keepdims=True)
            lse = mx + jnp.log(l)
            oh = _bdot(jnp.exp(sc - lse).astype(BF), vc, 2, 1)
            if has_prev:
                oh = oh + _bdot(jnp.exp(sp - lse).astype(BF), vp, 2, 1)
            o = o + jnp.where(m, oh, 0.0)
            lse_b = lse_b + jnp.where(m, lse, 0.0)
        _store_units(o_ref, o, g)
        _store_units(l_ref, lse_b, g)

    blk = pl.BlockSpec((S, 128), lambda b, hf: (b, g * 2 + hf))
    anyspec = pl.BlockSpec(memory_space=pl.ANY)
    o, l = pl.pallas_call(
        body, name=name, grid=(B, 2),
        in_specs=[blk, blk, blk, anyspec, anyspec], out_specs=[blk, blk],
        out_shape=[jax.ShapeDtypeStruct(q.shape, F32)] * 2,
        input_output_aliases={3: 0, 4: 1},
        compiler_params=_cp(("parallel", "parallel")),
    )(q, k, v, o_alias, l_alias)
    return o, l


def _dil_bwd(g, q, k, v, do, cb, lse, aliases, B, S, *, name):
    assert S == N_UNITS * STEPS
    d = DIL[g]

    def body(q_ref, k_ref, v_ref, do_ref, c_ref, l_ref, _, __, ___, dq_ref, dk_ref, dv_ref):
        has_prev, cur, prev, heads = _dil_masks(g)
        q = _load_units(q_ref, g)
        kc = _load_units(k_ref, g).astype(BF)
        vc = _load_units(v_ref, g).astype(BF)
        do = _load_units(do_ref, g)
        cbv = _load_units(c_ref, g)
        lse_b = _load_units(l_ref, g)
        if has_prev:
            kp, vp = _shift_units(kc, d), _shift_units(vc, d)
        z = jnp.zeros(q.shape, F32)
        dq, dkc, dkp, dvc, dvp = z, z, z, z, z
        for m in heads:
            qm = jnp.where(m, q, 0.0).astype(BF)
            dom = jnp.where(m, do, 0.0).astype(BF)
            lse = jnp.max(jnp.where(m, lse_b, -jnp.inf), axis=-1, keepdims=True)
            c = jnp.max(jnp.where(m, cbv, -jnp.inf), axis=-1, keepdims=True)
            sc = jnp.where(cur, _bdot(qm, kc, 2, 2) * SCALE, NEG)
            pc = jnp.exp(sc - lse)
            dsc = (pc * (_bdot(dom, vc, 2, 2) + c) * SCALE).astype(BF)
            dqh = _bdot(dsc, kc, 2, 1)
            dkc = dkc + jnp.where(m, _bdot(dsc, qm, 1, 1), 0.0)
            dvc = dvc + jnp.where(m, _bdot(pc.astype(BF), dom, 1, 1), 0.0)
            if has_prev:
                sp = jnp.where(prev, _bdot(qm, kp, 2, 2) * SCALE, NEG)
                pp = jnp.exp(sp - lse)
                dsp = (pp * (_bdot(dom, vp, 2, 2) + c) * SCALE).astype(BF)
                dqh = dqh + _bdot(dsp, kp, 2, 1)
                dkp = dkp + jnp.where(m, _bdot(dsp, qm, 1, 1), 0.0)
                dvp = dvp + jnp.where(m, _bdot(pp.astype(BF), dom, 1, 1), 0.0)
            dq = dq + jnp.where(m, dqh, 0.0)
        if has_prev:
            dkc = dkc + _shift_units(dkp, -d)
            dvc = dvc + _shift_units(dvp, -d)
        _store_units(dq_ref, dq, g)
        _store_units(dk_ref, dkc, g)
        _store_units(dv_ref, dvc, g)

    blk = pl.BlockSpec((S, 128), lambda b, hf: (b, g * 2 + hf))
    anyspec = pl.BlockSpec(memory_space=pl.ANY)
    return tuple(pl.pallas_call(
        body, name=name, grid=(B, 2),
        in_specs=[blk] * 6 + [anyspec] * 3, out_specs=[blk] * 3,
        out_shape=[jax.ShapeDtypeStruct(q.shape, F32)] * 3,
        input_output_aliases={6: 0, 7: 1, 8: 2},
        compiler_params=_cp(("parallel", "parallel")),
    )(q, k, v, do, cb, lse, *aliases))


def _kv_grad_sum(parts, cos, sin, *, name, tm=512):
    T = parts[0][0].shape[0]
    n_l = len(parts)

    def body(*refs):
        c_ref, s_ref = refs[0], refs[1]
        dk_ref, dv_ref = refs[2 + 2 * n_l:]
        dk = refs[2][...]
        dv = refs[3][...]
        for li in range(1, n_l):
            dk = dk + refs[2 + 2 * li][...]
            dv = dv + refs[3 + 2 * li][...]
        dk_ref[...] = _rot(dk, c_ref[...], s_ref[...], -1.0).astype(dk_ref.dtype)
        dv_ref[...] = dv.astype(dv_ref.dtype)

    full = pl.BlockSpec((tm, MAIN_W), lambda i: (i, 0))
    tab = pl.BlockSpec((tm, 128), lambda i: (i, 0))
    ops = [cos, sin] + [t for part in parts for t in part]
    return pl.pallas_call(
        body, name=name, grid=(T // tm,), in_specs=[tab, tab] + [full] * (2 * n_l), out_specs=[full, full],
        out_shape=[jax.ShapeDtypeStruct((T, MAIN_W), BF)] * 2,
        compiler_params=_cp(("parallel",)),
    )(*ops)


def _group_softmax(lse):
    l0, l1, l2 = lse[:, 0:256], lse[:, 256:512], lse[:, 512:768]
    mx = jnp.maximum(jnp.maximum(l0, l1), l2)
    e0, e1, e2 = jnp.exp(l0 - mx), jnp.exp(l1 - mx), jnp.exp(l2 - mx)
    tot = e0 + e1 + e2
    return e0 / tot, e1 / tot, e2 / tot


def _dil_combine_fwd(o, lse, y_alias, *, name, tm=512):
    T = o.shape[0]

    def body(o_ref, l_ref, _, y_ref):
        a = jnp.concatenate(_group_softmax(l_ref[...]), axis=1)
        y_ref[...] = (o_ref[...] * a).astype(y_ref.dtype)

    blk = pl.BlockSpec((tm, MAIN_W), lambda i: (i, 0))
    return pl.pallas_call(
        body, name=name, grid=(T // tm,), in_specs=[blk, blk, pl.BlockSpec(memory_space=pl.ANY)], out_specs=blk,
        out_shape=jax.ShapeDtypeStruct(y_alias.shape, y_alias.dtype), input_output_aliases={2: 0},
        compiler_params=_cp(("parallel",)),
    )(o, lse, y_alias)


def _dil_combine_bwd(dy, o, lse, *, name, tm=256):
    T = o.shape[0]
    lane_r = lax.broadcasted_iota(jnp.int32, (256, 256), 0) // HEAD_DIM
    lane_c = lax.broadcasted_iota(jnp.int32, (256, 256), 1) // HEAD_DIM
    ones_bd = (lane_r == lane_c).astype(BF)

    def body(dy_ref, o_ref, l_ref, e_ref, do_ref, c_ref):
        dyv = dy_ref[...]
        alphas = _group_softmax(l_ref[...])
        prod = dyv * o_ref[...]
        e = e_ref[...]
        tot = jnp.zeros((tm, 256), F32)
        for gi in range(3):
            x = prod[:, gi * 256:(gi + 1) * 256]
            hi = x.astype(BF)
            lo = (x - hi.astype(F32)).astype(BF)
            dalpha = jnp.dot(hi, e, preferred_element_type=F32) + jnp.dot(lo, e, preferred_element_type=F32)
            tot = tot + alphas[gi] * dalpha
        a = jnp.concatenate(alphas, axis=1)
        do_ref[...] = (dyv * a).astype(do_ref.dtype)
        c_ref[...] = jnp.concatenate([-al * tot for al in alphas], axis=1)

    blk = pl.BlockSpec((tm, MAIN_W), lambda i: (i, 0))
    return pl.pallas_call(
        body, name=name, grid=(T // tm,),
        in_specs=[blk, blk, blk, pl.BlockSpec((256, 256), lambda i: (0, 0))], out_specs=[blk, blk],
        out_shape=[jax.ShapeDtypeStruct((T, MAIN_W), F32), jax.ShapeDtypeStruct((T, MAIN_W), F32)],
        compiler_params=_cp(("parallel",)),
    )(dy, o, lse, ones_bd)


def _loss(y, target, *, name, tm=512):
    T, Dm = y.shape
    nt = T // tm

    def body(y_ref, t_ref, l_ref, d_ref, acc):
        i = pl.program_id(0)
        err = y_ref[...] - t_ref[...]
        d_ref[...] = err / Dm
        part = jnp.sum(jnp.mean(err * err, axis=-1, keepdims=True).reshape(tm // 8, 8, 1), axis=0)

        @pl.when(i == 0)
        def _():
            acc[...] = part

        @pl.when(i > 0)
        def _():
            acc[...] += part

        @pl.when(i == nt - 1)
        def _():
            l_ref[...] = 0.5 * jnp.sum(acc[...], axis=0, keepdims=True)

    row = pl.BlockSpec((tm, Dm), lambda i: (i, 0))
    return pl.pallas_call(
        body, name=name, grid=(nt,), in_specs=[row, row],
        out_specs=[pl.BlockSpec((1, 1), lambda i: (0, 0)), row],
        out_shape=[jax.ShapeDtypeStruct((1, 1), F32), jax.ShapeDtypeStruct((T, Dm), F32)],
        scratch_shapes=[pltpu.VMEM((8, 1), F32)],
        compiler_params=_cp(("arbitrary",)),
    )(y, target)


def _adamw(w, g, m, v, *, name):
    shape = w.shape
    cols = shape[-1]
    rows = w.size // cols
    tm = rows
    for cand in (512, 352, 256, 128):
        if rows > cand and rows % cand == 0 and cand * cols * 4 <= (1 << 20):
            tm = cand
            break

    def body(w_ref, g_ref, m_ref, v_ref, d_ref, mo_ref, vo_ref):
        gv = g_ref[...]
        mn = ADAM_B1 * m_ref[...] + (1.0 - ADAM_B1) * gv
        vn = ADAM_B2 * v_ref[...] + (1.0 - ADAM_B2) * (gv * gv)
        m_hat = mn / (1.0 - ADAM_B1 ** ADAM_STEP)
        v_hat = vn / (1.0 - ADAM_B2 ** ADAM_STEP)
        d_ref[...] = -ADAM_LR * (m_hat / (jnp.sqrt(v_hat) + ADAM_EPS) + ADAM_WD * w_ref[...])
        mo_ref[...] = mn
        vo_ref[...] = vn

    blk = pl.BlockSpec((tm, cols), lambda i: (i, 0))
    outs = pl.pallas_call(
        body, name=name, grid=(rows // tm,), in_specs=[blk] * 4, out_specs=[blk] * 3,
        out_shape=[jax.ShapeDtypeStruct((rows, cols), F32)] * 3,
        compiler_params=_cp(("parallel",)),
    )(*[t.reshape(rows, cols) for t in (w, g, m, v)])
    return tuple(t.reshape(shape) for t in outs)


def _adamw_layer(name, l, w, g, m, v, prev, after=None):
    L, rows, cols = w.shape
    tm = rows
    for cand in (512, 352, 256, 176, 128, 64):
        if rows % cand == 0 and cand * cols * 4 <= (1 << 20):
            tm = cand
            break
    if prev is None:
        prev = tuple(lax.empty(w.shape, F32) for _ in range(4))

    n_after = 0 if after is None else 1

    def body(w_ref, g_ref, m_ref, v_ref, *rest):
        d_ref, mo_ref, vo_ref, go_ref = rest[4 + n_after:]
        gv = g_ref[...]
        mn = ADAM_B1 * m_ref[...] + (1.0 - ADAM_B1) * gv
        vn = ADAM_B2 * v_ref[...] + (1.0 - ADAM_B2) * (gv * gv)
        m_hat = mn / (1.0 - ADAM_B1 ** ADAM_STEP)
        v_hat = vn / (1.0 - ADAM_B2 ** ADAM_STEP)
        d_ref[...] = -ADAM_LR * (m_hat / (jnp.sqrt(v_hat) + ADAM_EPS) + ADAM_WD * w_ref[...])
        mo_ref[...] = mn
        vo_ref[...] = vn
        go_ref[...] = gv

    lay = pl.BlockSpec((None, tm, cols), lambda i: (l, i, 0))
    one = pl.BlockSpec((None, tm, cols), lambda i: (0, i, 0))
    return tuple(pl.pallas_call(
        body, name=f"l{l}_adamw_{name}", grid=(rows // tm,),
        in_specs=[lay, one, lay, lay] + [pl.BlockSpec(memory_space=pl.ANY)] * (4 + n_after), out_specs=[lay] * 4,
        out_shape=[jax.ShapeDtypeStruct(w.shape, F32)] * 4,
        input_output_aliases={4 + i: i for i in range(4)},
        compiler_params=_cp(("parallel",)),
    )(w, g, m, v, *prev, *([] if after is None else [after])))


BIG = {
    'w_in': ((DEPTH, D_MODEL, D_MODEL), 'row'),
    'w_mem_kv': ((DEPTH, D_MODEL, 2 * MEM_W), 'row'),
    'w_out': ((DEPTH, D_MODEL, D_MODEL), 'row'),
    'w_kv': ((1, D_MODEL, 2 * MAIN_W), 'col'),
    'w_gate_up': ((DEPTH, D_MODEL, 2 * D_FF), 'col'),
    'w_down': ((DEPTH, D_FF, D_MODEL), 'row'),
}
BIG_NAMES = tuple(BIG)
N_CHIPS = 4
HBM_ANY = pl.BlockSpec(memory_space=pl.ANY)


def _geom(name):
    (L, R, C), kind = BIG[name]
    if kind == 'row':
        return L, R, C, kind, R // N_CHIPS, C, R // (2 * N_CHIPS)
    return L, R, C, kind, R, C // N_CHIPS, R // 2


def _shard_shape(name):
    L, R, C, kind, rs, cs, rh = _geom(name)
    return (L, rs, cs)


def _half_shape(name):
    L, R, C, kind, rs, cs, rh = _geom(name)
    return (L, rh, cs)


def _full_win(ref, name, s, h):
    L, R, C, kind, rs, cs, rh = _geom(name)
    if kind == 'row':
        rows = pl.ds(s * rs, rs) if h is None else pl.ds(s * rs + h * rh, rh)
        return ref.at[:, rows, :]
    rows = slice(None) if h is None else pl.ds(h * rh, rh)
    return ref.at[:, rows, pl.ds(s * cs, cs)]


def _shard_half(ref, name, h):
    L, R, C, kind, rs, cs, rh = _geom(name)
    return ref.at[:, pl.ds(h * rh, rh), :]


def _halves_win(ref, name, s):
    L, R, C, kind, rs, cs, rh = _geom(name)
    if kind == 'row':
        return ref.at[:, pl.ds(s * rh, rh), :]
    return ref.at[:, :, pl.ds(s * cs, cs)]


def _halves_shape(name):
    L, R, C, kind, rs, cs, rh = _geom(name)
    return (L, N_CHIPS * rh, cs) if kind == 'row' else (L, rh, C)


def _place():
    x, y, c = lax.axis_index("x"), lax.axis_index("y"), lax.axis_index("c")
    chips = [(1 - x, y), (x, 1 - y), (1 - x, 1 - y)]
    return x, y, c, chips


SMALL_ROWS = 24


SEM_SPEC = pl.BlockSpec(memory_space=pltpu.SEMAPHORE)
HBM_SPEC = pl.BlockSpec(memory_space=pltpu.HBM)
DATAFLOW = pltpu.SideEffectType.DATAFLOW_SIDE_EFFECTING


def _in_hbm(a):
    return pltpu.with_memory_space_constraint(a, pltpu.HBM)


def _remote(src, dst, send_sems, recv_sems, k, to):
    return pltpu.make_async_remote_copy(src_ref=src, dst_ref=dst, send_sem=send_sems.at[k], recv_sem=recv_sems.at[k],
                                        device_id=to, device_id_type=MESH)


def _split_start(name, bufs, n_copies, sends, after=None):
    nb = len(bufs)
    n_in = nb + (0 if after is None else 1)

    def body(*refs):
        in_refs = refs[:nb]
        send_sems, recv_sems = refs[n_in], refs[n_in + 1]
        token = refs[-1]
        for k, (src, dst, to) in enumerate(sends(in_refs)):
            _remote(src, dst, send_sems, recv_sems, k, to).start()
        token[...] = jnp.zeros_like(token)

    outs = pl.pallas_call(
        body, name=name,
        out_shape=(pltpu.SemaphoreType.DMA((n_copies,)), pltpu.SemaphoreType.DMA((n_copies,)),
                   *[pltpu.HBM(b.shape, b.dtype) for b in bufs], jax.ShapeDtypeStruct((8, 128), F32)),
        in_specs=[HBM_SPEC] * nb + [HBM_ANY] * (n_in - nb),
        out_specs=(SEM_SPEC, SEM_SPEC, *[HBM_SPEC] * nb, pl.BlockSpec(memory_space=pltpu.VMEM)),
        input_output_aliases={i: 2 + i for i in range(nb)},
        compiler_params=pltpu.CompilerParams(has_side_effects=DATAFLOW),
    )(*[_in_hbm(b) for b in bufs], *([] if after is None else [after]))
    return outs[0], outs[1], list(outs[2:2 + nb]), outs[-1]


def _split_wait(name, send_sems, recv_sems, bufs, after, sends, arrivals):
    nb = len(bufs)

    def body(*refs):
        in_refs = refs[:nb]
        s_sems, r_sems = refs[nb], refs[nb + 1]
        me = (lax.axis_index("x"), lax.axis_index("y"), lax.axis_index("c"))
        for k, (src, dst, to) in enumerate(sends(in_refs)):
            _remote(src, dst, s_sems, r_sems, k, to).wait_send()
        for k, win in enumerate(arrivals(in_refs)):
            _remote(win, win, s_sems, r_sems, k, me).wait_recv()

    outs = pl.pallas_call(
        body, name=name,
        out_shape=[pltpu.HBM(b.shape, b.dtype) for b in bufs],
        in_specs=[HBM_SPEC] * nb + [SEM_SPEC, SEM_SPEC, HBM_ANY],
        out_specs=[HBM_SPEC] * nb,
        input_output_aliases={i: i for i in range(nb)},
        compiler_params=pltpu.CompilerParams(has_side_effects=DATAFLOW),
    )(*bufs, send_sems, recv_sems, after)
    return list(outs)


MIX_W = ('w_in', 'w_mem_kv', 'w_out')
FFN_W = ('w_gate_up', 'w_down')
LAYER_W = MIX_W + FFN_W


def _place_own(tag, names, sources, small, sc):
    nw = len(names)
    has_small = small is not None

    def body(sc_ref, *refs):
        srcs = refs[:nw]
        shard_out = refs[nw + has_small:2 * nw + has_small]
        full_out = refs[2 * nw + has_small:3 * nw + has_small]
        for src, sh, fu in zip(srcs, shard_out, full_out):
            v = src[...].astype(BF)
            sh[...] = v
            fu[...] = v
        if has_small:
            refs[-1][...] = refs[nw][...]

    in_specs, shard_specs, full_specs, shard_shape, full_shape, ops = [], [], [], [], [], []
    for nm, (arr, layer) in zip(names, sources):
        L, R, C, kind, rs, cs, rh = _geom(nm)
        in_specs.append(pl.BlockSpec((1, rs, cs), lambda i, sc_ref, layer=layer: (layer, 0, 0)))
        shard_specs.append(pl.BlockSpec((1, rs, cs), lambda i, sc_ref: (0, 0, 0)))
        if kind == 'row':
            full_specs.append(pl.BlockSpec((1, rs, cs), lambda i, sc_ref: (0, sc_ref[0], 0)))
        else:
            full_specs.append(pl.BlockSpec((1, rs, cs), lambda i, sc_ref: (0, 0, sc_ref[0])))
        shard_shape.append(jax.ShapeDtypeStruct((1, rs, cs), BF))
        full_shape.append(jax.ShapeDtypeStruct((1, R, C), BF))
        ops.append(arr)
    if has_small:
        in_specs.append(pl.BlockSpec((SMALL_ROWS, 256), lambda i, sc_ref: (0, 0)))
        full_specs.append(pl.BlockSpec((None, SMALL_ROWS, 256), lambda i, sc_ref: (sc_ref[0], 0, 0)))
        full_shape.append(jax.ShapeDtypeStruct((N_CHIPS, SMALL_ROWS, 256), F32))
        ops.append(small)
    outs = pl.pallas_call(
        body, name=f"{tag}_place_own_shard",
        grid_spec=pltpu.PrefetchScalarGridSpec(num_scalar_prefetch=1, grid=(1,), in_specs=in_specs,
                                               out_specs=shard_specs + full_specs),
        out_shape=shard_shape + full_shape,
        compiler_params=_cp(("arbitrary",)),
    )(sc, *ops)
    return list(outs[:nw]), list(outs[nw:])


def _gather_start(l, names, sources, small, sc, after=None):
    nw = len(names)
    has_small = small is not None
    shards, fulls = _place_own(l, names, sources, small, sc)
    bufs = list(shards) + ([small] if has_small else []) + list(fulls)
    n_src = nw + (1 if has_small else 0)

    def sends(refs):
        x, y, c, chips = _place()
        s = 2 * x + y
        out = []
        for (px, py) in chips:
            for wi, nm in enumerate(names):
                out.append((_shard_half(refs[wi], nm, c), _full_win(refs[n_src + wi], nm, s, c), (px, py, c)))
            if has_small:
                out.append((refs[nw], refs[n_src + nw].at[s], (px, py, c)))
        return out

    def arrivals(refs):
        x, y, c, chips = _place()
        out = []
        for (px, py) in chips:
            sp = 2 * px + py
            for wi, nm in enumerate(names):
                out.append(_full_win(refs[n_src + wi], nm, sp, c))
            if has_small:
                out.append(refs[n_src + nw].at[sp])
        return out

    n_copies = 3 * n_src
    send_sems, recv_sems, bufs, token = _split_start(f"{l}_gather_ici_start", bufs, n_copies, sends, after)
    return dict(l=l, names=names, has_small=has_small, sems=(send_sems, recv_sems), bufs=bufs, sends=sends,
                arrivals=arrivals, token=token)


def _gather_forward(st, after):
    l, names = st['l'], st['names']
    nw = len(names)
    n_src = nw + (1 if st['has_small'] else 0)
    bufs = _split_wait(f"{l}_gather_ici_wait", *st['sems'], st['bufs'], after, st['sends'], st['arrivals'])
    fulls = bufs[n_src:n_src + nw]
    small_all = bufs[n_src + nw] if st['has_small'] else None

    def sends(refs):
        x, y, c, chips = _place()
        out = []
        for (px, py) in chips:
            sp = 2 * px + py
            for wi, nm in enumerate(names):
                w = _full_win(refs[wi], nm, sp, c)
                out.append((w, w, (x, y, 1 - c)))
        return out

    def arrivals(refs):
        x, y, c, chips = _place()
        out = []
        for (px, py) in chips:
            sp = 2 * px + py
            for wi, nm in enumerate(names):
                out.append(_full_win(refs[wi], nm, sp, 1 - c))
        return out

    send_sems, recv_sems, fulls, token = _split_start(f"{l}_gather_d2d_start", fulls, 3 * nw, sends)
    return dict(l=l, names=names, sems=(send_sems, recv_sems), bufs=fulls, sends=sends, arrivals=arrivals,
                small_all=small_all, token=token)


def _gather_finish(st, after):
    fulls = _split_wait(f"{st['l']}_gather_d2d_wait", *st['sems'], st['bufs'], after, st['sends'], st['arrivals'])
    return dict(zip(st['names'], fulls)), st['small_all']


def _reduce_start(tag, names, grads):
    nw = len(names)
    recv = [lax.empty((1,) + _halves_shape(nm)[1:], F32) for nm in names]
    bufs = [grads[nm] for nm in names] + recv

    def windows(refs, half_of):
        x, y, c, _ = _place()
        h = half_of(c)
        out = []
        for wi, nm in enumerate(names):
            L, R, C, kind, rs, cs, rh = _geom(nm)
            if kind == 'row':
                for sp in range(N_CHIPS):
                    out.append((_full_win(refs[wi], nm, sp, h), _halves_win(refs[nw + wi], nm, sp)))
            else:
                out.append((refs[wi].at[:, pl.ds(h * rh, rh), :], refs[nw + wi]))
        return out

    def sends(refs):
        x, y, c, _ = _place()
        return [(src, dst, (x, y, 1 - c)) for src, dst in windows(refs, lambda c: 1 - c)]

    def arrivals(refs):
        return [dst for _, dst in windows(refs, lambda c: c)]

    n_copies = sum(N_CHIPS if BIG[nm][1] == 'row' else 1 for nm in names)
    send_sems, recv_sems, bufs, token = _split_start(tag + "_halves_start", bufs, n_copies, sends)
    return dict(tag=tag, names=names, sems=(send_sems, recv_sems), bufs=bufs, sends=sends, arrivals=arrivals, token=token)


def _reduce_mid(st, after, sc):
    tag, names = st['tag'], st['names']
    nw = len(names)
    bufs = _split_wait(tag + "_halves_wait", *st['sems'], st['bufs'], after, st['sends'], st['arrivals'])
    halves, own = [], []
    for wi, nm in enumerate(names):
        hb, ow = _add_halves(nm, bufs[wi], bufs[nw + wi], sc, tag)
        halves.append(hb)
        own.append(ow)
    pieces = [lax.empty((3, 1) + _half_shape(nm)[1:], BF) for nm in names]

    def sends(refs):
        x, y, c, chips = _place()
        out = []
        for j, (px, py) in enumerate(chips):
            for wi, nm in enumerate(names):
                out.append((_halves_win(refs[wi], nm, 2 * px + py), refs[nw + wi].at[j], (px, py, c)))
        return out

    def arrivals(refs):
        return [refs[nw + wi].at[j] for j in range(3) for wi in range(nw)]

    send_sems, recv_sems, bufs, token = _split_start(tag + "_pieces_start", halves + pieces, 3 * nw, sends)
    return dict(tag=tag, names=names, sems=(send_sems, recv_sems), bufs=bufs, sends=sends, arrivals=arrivals, own=own,
                token=token)


def _reduce_late(st, after, sc):
    tag, names = st['tag'], st['names']
    nw = len(names)
    bufs = _split_wait(tag + "_pieces_wait", *st['sems'], st['bufs'], after, st['sends'], st['arrivals'])
    gsh = [_sum_pieces(nm, st['own'][wi], bufs[nw + wi], sc, tag) for wi, nm in enumerate(names)]

    def sends(refs):
        x, y, c, _ = _place()
        return [(_shard_half(refs[wi], nm, c), _shard_half(refs[wi], nm, c), (x, y, 1 - c)) for wi, nm in enumerate(names)]

    def arrivals(refs):
        x, y, c, _ = _place()
        return [_shard_half(refs[wi], nm, 1 - c) for wi, nm in enumerate(names)]

    send_sems, recv_sems, bufs, token = _split_start(tag + "_share_start", gsh, nw, sends)
    return dict(tag=tag, names=names, sems=(send_sems, recv_sems), bufs=bufs, sends=sends, arrivals=arrivals, token=token)


def _reduce_finish(st, after):
    gsh = _split_wait(st['tag'] + "_share_wait", *st['sems'], st['bufs'], after, st['sends'], st['arrivals'])
    return dict(zip(st['names'], gsh))


def _add_halves(name, g, r, sc, tag):
    _, R, C, kind, rs, cs, rh = _geom(name)
    L = g.shape[0]
    tr = rh if kind == 'row' else 256
    nr = rh // tr

    def body(sc_ref, g_ref, r_ref, hb_ref, own_ref):
        sp = pl.program_id(2)
        tot = g_ref[...] + r_ref[...]
        hb_ref[...] = tot.astype(hb_ref.dtype)

        @pl.when(sp == sc_ref[0])
        def _():
            own_ref[...] = tot

    if kind == 'row':
        g_map = lambda l, ri, sp, sc_ref: (l, sp * 2 + sc_ref[1], 0)
        h_map = lambda l, ri, sp, sc_ref: (l, sp, 0)
    else:
        g_map = lambda l, ri, sp, sc_ref: (l, sc_ref[1] * nr + ri, sp)
        h_map = lambda l, ri, sp, sc_ref: (l, ri, sp)
    own_map = lambda l, ri, sp, sc_ref: (l, ri, 0)
    blk = (None, tr, cs)
    return pl.pallas_call(
        body, name=tag + "_add_halves_" + name,
        grid_spec=pltpu.PrefetchScalarGridSpec(
            num_scalar_prefetch=1, grid=(L, nr, N_CHIPS),
            in_specs=[pl.BlockSpec(blk, g_map), pl.BlockSpec(blk, h_map)],
            out_specs=[pl.BlockSpec(blk, h_map), pl.BlockSpec(blk, own_map)]),
        out_shape=[jax.ShapeDtypeStruct((L,) + _halves_shape(name)[1:], BF),
                   jax.ShapeDtypeStruct((L,) + _half_shape(name)[1:], F32)],
        compiler_params=_cp(("parallel", "parallel", "arbitrary")),
    )(sc, g, r)


def _sum_pieces(name, own, pieces, sc, tag):
    _, R, C, kind, rs, cs, rh = _geom(name)
    L = own.shape[0]
    tr = rh if kind == 'row' else 256
    nr = rh // tr

    def body(sc_ref, o_ref, p_ref, out_ref):
        out_ref[...] = o_ref[...] + p_ref[0].astype(F32) + p_ref[1].astype(F32) + p_ref[2].astype(F32)

    blk = (None, tr, cs)
    return pl.pallas_call(
        body, name=tag + "_sum_pieces_" + name,
        grid_spec=pltpu.PrefetchScalarGridSpec(
            num_scalar_prefetch=1, grid=(L, nr),
            in_specs=[pl.BlockSpec(blk, lambda l, ri, sc_ref: (l, ri, 0)),
                      pl.BlockSpec((3, None, tr, cs), lambda l, ri, sc_ref: (0, l, ri, 0))],
            out_specs=pl.BlockSpec(blk, lambda l, ri, sc_ref: (l, sc_ref[1] * nr + ri, 0))),
        out_shape=jax.ShapeDtypeStruct((L,) + _shard_shape(name)[1:], F32),
        compiler_params=_cp(("parallel", "parallel")),
    )(sc, own, pieces)


def _small_gather_start(v, sc):
    rows = v.shape[0]

    def place(sc_ref, v_ref, o_ref):
        o_ref[...] = v_ref[...]

    slots = pl.pallas_call(
        place, name="small_grads_place_own",
        grid_spec=pltpu.PrefetchScalarGridSpec(
            num_scalar_prefetch=1, grid=(1,),
            in_specs=[pl.BlockSpec((rows, 128), lambda i, sc_ref: (0, 0))],
            out_specs=pl.BlockSpec((None, rows, 128), lambda i, sc_ref: (2 * sc_ref[0] + sc_ref[1], 0, 0))),
        out_shape=jax.ShapeDtypeStruct((8, rows, 128), v.dtype),
        compiler_params=_cp(("arbitrary",)),
    )(sc, v)

    def peers():
        x, y, c, _ = _place()
        flips = [(fx, fy, fc) for fx in (0, 1) for fy in (0, 1) for fc in (0, 1)][1:]
        return [((1 - x if fx else x), (1 - y if fy else y), (1 - c if fc else c)) for fx, fy, fc in flips]

    def sends(refs):
        x, y, c, _ = _place()
        return [(refs[0], refs[1].at[4 * x + 2 * y + c], p) for p in peers()]

    def arrivals(refs):
        return [refs[1].at[4 * px + 2 * py + pc] for px, py, pc in peers()]

    send_sems, recv_sems, bufs, token = _split_start("small_grads_gather_start", [v, slots], 7, sends)
    return dict(sems=(send_sems, recv_sems), bufs=bufs, sends=sends, arrivals=arrivals, token=token)


def _small_gather_finish(st, after):
    return _split_wait("small_grads_gather_wait", *st['sems'], st['bufs'], after, st['sends'], st['arrivals'])[1]


def _sum8(v8, *, name, tr=336):
    rows = v8.shape[1]
    tr = min(tr, rows)
    assert rows % tr == 0

    def body(v_ref, o_ref):
        tot = v_ref[0].astype(F32)
        for d in range(1, 8):
            tot = tot + v_ref[d].astype(F32)
        o_ref[...] = tot

    return pl.pallas_call(
        body, name=name, grid=(rows // tr,),
        in_specs=[pl.BlockSpec((8, tr, 128), lambda i: (0, i, 0))], out_specs=pl.BlockSpec((tr, 128), lambda i: (i, 0)),
        out_shape=jax.ShapeDtypeStruct((rows, 128), F32),
        compiler_params=_cp(("parallel",)),
    )(v8)


def _block_diag(w_pool_l):
    wbd = jnp.zeros((MAIN_W, MAIN_W), F32)
    for gi in range(len(POOL_WINDOWS)):
        wbd = lax.dynamic_update_slice(wbd, w_pool_l[gi], (gi * POOL_GROUP, gi * POOL_GROUP))
    return wbd.astype(BF)


def _unpack_small(small_all):
    ng = small_all[:, :16, :].reshape(N_CHIPS, DEPTH, 4, 256).transpose(1, 2, 0, 3).reshape(DEPTH, 4, D_MODEL)
    ps = small_all[:, 16:18, :POOL_GROUP].transpose(1, 0, 2).reshape(N_A, MAIN_W)
    return ng, ps


def _local_step(x, mem, positions, on_forward, on_backward, mem_norm, w_pool, kv_norm, target):
    B, S, _ = x.shape
    T = B * S
    xc = x.reshape(T, D_MODEL)
    memf = mem.reshape(B * N_MEM, D_MODEL)
    tgt = target.reshape(T, D_MODEL)
    cos, sin = _rope_tables(positions.reshape(T, 1), name="rope_tables")
    wbd = [_block_diag(w_pool[l]) for l in range(N_A)]
    nbo = D_FF // 256
    fw = []
    rk = rv = None
    kv_saved = None
    wts = []
    norm_gains = pool_scale = y2 = None

    for l in range(DEPTH):
        t = f"l{l}_"
        got = on_forward('start', l, y2)
        wts.append(dict(got[0]))
        if l == 0:
            norm_gains, pool_scale = _unpack_small(got[1])
        sv = {'x_in': xc}
        h0 = _norm_fwd(xc, norm_gains[l, 0], name=t + "norm0", out_dtype=BF, tm=1024, after=got[2])
        z, = _mm(h0, wts[l]['w_in'], b_layer=0, name=t + "mm_in", tm=1024, tn=1024)
        memn = _norm_fwd(memf, mem_norm[l], name=t + "norm_mem", out_dtype=BF, tm=256)
        kvm, = _mm(memn, wts[l]['w_mem_kv'], b_layer=0, name=t + "mm_memkv", out_dtypes=(BF,))
        if l < N_A:
            ycat, sv['p'] = _pool_fwd(z, wbd[l], pool_scale[l], B, S, name=t + "pool_fwd")
        else:
            rq = _rope_apply(z, cos, sin, name=t + "rope_q", out_dtype=F32)
            o = lax.empty((T, MAIN_W), F32)
            lse = lax.empty((T, MAIN_W), F32)
            for g in range(3):
                o, lse = _dil_fwd(g, rq, rk, rv, o, lse, B, S, name=t + f"dil_fwd{g}")
            ycat = _dil_combine_fwd(o, lse, lax.empty((T, D_MODEL), BF), name=t + "dil_combine")
            sv.update(rq=rq, o=o, lse=lse)
        ycat, sv['lse_m'] = _memattn_fwd(z, kvm, ycat, B, S, name=t + "memattn_fwd")
        tok = on_forward('mid', l, ycat)
        y1, = _mm(ycat, wts[l]['w_out'], b_layer=0, name=t + "mm_out", tm=1024, tn=1024)
        wts[l].update(on_forward('ffn', l, y1)[0])
        x1 = _norm_fwd(y1, norm_gains[l, 1], name=t + "norm1", res=xc, after=tok)
        h2 = _norm_fwd(x1, norm_gains[l, 2], name=t + "norm2", out_dtype=BF, tm=1024)
        gg, uu, aa = _mm(h2, wts[l]['w_gate_up'], b_layer=0, b_offsets=(0, nbo), out_n=D_FF, tm=4096, tn=256, name=t + "mm_gate_up",
                         epilogue=_swiglu_fwd_epilogue, out_dtypes=(BF, BF, BF))
        on_forward('post', l, gg)
        y2, = _mm(aa, wts[l]['w_down'], b_layer=0, tk=D_FF, name=t + "mm_down")
        x2 = _norm_fwd(y2, norm_gains[l, 3], name=t + "norm3", res=x1)
        sv.update(h0=h0, z=z, memn=memn, kvm=kvm, ycat=ycat, y1=y1, x1=x1, h2=h2, gg=gg, uu=uu, aa=aa, y2=y2)
        fw.append(sv)
        xc = x2
        if l == N_A - 1:
            kvn = _norm_fwd(xc, kv_norm, name="norm_kv", out_dtype=BF)
            kv, = _mm(kvn, wts[N_A - 1]['w_kv'], b_layer=0, name="mm_kv")
            rk, rv = _rope_apply(kv, cos, sin, name="rope_k", passthrough=True, out_dtype=F32)
            kv_saved = (xc, kvn)

    loss, dx = _loss(xc, tgt, name="loss")

    d_ng = [[None] * 4 for _ in range(DEPTH)]
    d_memnorm = [None] * DEPTH
    d_wbd = [None] * N_A
    d_pscale = [None] * N_A
    d_kvnorm = None
    kv_parts = []
    tok = None

    def as3d(gl):
        return {nm: g.reshape((1,) + g.shape) for nm, g in gl.items()}

    for l in reversed(range(DEPTH)):
        t = f"l{l}_b_"
        sv = fw[l]
        gl = {}
        dy2, d_ng[l][3] = _norm_bwd(dx, sv['y2'], norm_gains[l, 3], name=t + "norm3", out_dtype=BF, tm=1024, after=tok)
        gl['w_down'], = _mm(sv['aa'], dy2, ta=True, tm=1408, tn=512, tk=4096, name=t + "dw_down")
        dg, du = _mm(dy2, wts[l]['w_down'], tb=True, b_layer=0, tm=1024, tn=1408, name=t + "d_act",
                     extras=((sv['gg'], 'tile'), (sv['uu'], 'tile')), epilogue=_swiglu_bwd_epilogue, out_dtypes=(BF, BF))
        gl['w_gate_up'], = _mm(sv['h2'], (dg, du), ta=True, tn=1408, tk=1024, name=t + "dw_gate_up")
        dh2, = _mm((dg, du), wts[l]['w_gate_up'], tb=True, b_layer=0, tn=1024, tk=1408, name=t + "d_h2", out_dtypes=(BF,))
        dx1, d_ng[l][2] = _norm_bwd(dh2, sv['x1'], norm_gains[l, 2], name=t + "norm2", add=dx, tm=1024)
        tok = on_backward('ffn', l, dx1, as3d(gl))
        dy1, d_ng[l][1] = _norm_bwd(dx1, sv['y1'], norm_gains[l, 1], name=t + "norm1", out_dtype=BF, tm=1024, after=tok)
        gl['w_out'], = _mm(sv['ycat'], dy1, ta=True, name=t + "dw_out", tk=4096)
        dycat, = _mm(dy1, wts[l]['w_out'], tb=True, b_layer=0, name=t + "d_ycat", tm=1024, tn=1024)
        dz = lax.empty((T, D_MODEL), BF)
        dz, dkm, dvm = _memattn_bwd(dycat, sv['z'], sv['kvm'], sv['lse_m'], dz, B, S, name=t + "memattn")
        if l < N_A:
            dz, d_wbd[l], d_pscale[l] = _pool_bwd(dycat, sv['p'], wbd[l], pool_scale[l], dz, B, S, name=t + "pool")
        else:
            do, cb = _dil_combine_bwd(dycat, sv['o'], sv['lse'], name=t + "dil_combine")
            acc = tuple(lax.empty((T, MAIN_W), F32) for _ in range(3))
            for g in range(3):
                acc = _dil_bwd(g, sv['rq'], rk, rv, do, cb, sv['lse'], acc, B, S, name=t + f"dil{g}")
            dz = _rope_apply(acc[0], cos, sin, name=t + "rope_q", sign=-1.0, alias=dz)
            kv_parts.append(acc[1:])
        tok = on_backward('mix', l, dz, as3d(gl))
        gl['w_in'], = _mm(sv['h0'], dz, ta=True, name=t + "dw_in", tk=4096)
        dh0, = _mm(dz, wts[l]['w_in'], tb=True, b_layer=0, name=t + "d_h0", out_dtypes=(BF,), tm=1024, tn=1024)
        dx, d_ng[l][0] = _norm_bwd(dh0, sv['x_in'], norm_gains[l, 0], name=t + "norm0", add=dx1, tm=1024, after=tok)
        gl['w_mem_kv'], = _mm(sv['memn'], (dkm, dvm), ta=True, tn=256, name=t + "dw_memkv")
        dmemn, = _mm((dkm, dvm), wts[l]['w_mem_kv'], tb=True, b_layer=0, tk=256, name=t + "d_memn", out_dtypes=(BF,))
        _, d_memnorm[l] = _norm_bwd(dmemn, memf, mem_norm[l], name=t + "norm_mem", out_dtype=BF, tm=256)
        if l == N_A:
            dk, dv = _kv_grad_sum(kv_parts, cos, sin, name="kv_grad")
            x_kv, kvn = kv_saved
            gl['w_kv'], = _mm(kvn, (dk, dv), ta=True, tn=768, tk=2048, name="dw_kv")
            dkvn, = _mm((dk, dv), wts[N_A - 1]['w_kv'], tb=True, b_layer=0, tn=1024, tk=768, name="d_kvn", out_dtypes=(BF,))
            dx, d_kvnorm = _norm_bwd(dkvn, x_kv, kv_norm, name="norm_kv_b", add=dx)
        tok = on_backward('end', l, dx, as3d(gl))

    small = {
        'norm_gains': jnp.stack([jnp.concatenate(d_ng[l], axis=0) for l in range(DEPTH)]),
        'mem_norm': jnp.concatenate(d_memnorm, axis=0),
        'kv_norm': d_kvnorm.reshape(D_MODEL),
        'pool_scale': jnp.concatenate(d_pscale, axis=0),
        'w_pool': jnp.stack([jnp.stack([d_wbd[l][gi * POOL_GROUP:(gi + 1) * POOL_GROUP, gi * POOL_GROUP:(gi + 1) * POOL_GROUP]
                                        for gi in range(len(POOL_WINDOWS))]) for l in range(N_A)]),
    }
    return loss, dx, small


SMALL_ORDER = ('norm_gains', 'mem_norm', 'kv_norm', 'pool_scale', 'w_pool')
SMALL_VEC_ROWS = 2560


def kernel(x, mem, positions, norm_gains, mem_norm, w_in, w_mem_kv, w_out, w_pool, pool_scale, kv_norm, w_kv, w_gate_up, w_down, loss_target, m_norm_gains, m_mem_norm, m_w_in, m_w_mem_kv, m_w_out, m_w_pool, m_pool_scale, m_kv_norm, m_w_kv, m_w_gate_up, m_w_down, v_norm_gains, v_mem_norm, v_w_in, v_w_mem_kv, v_w_out, v_w_pool, v_pool_scale, v_kv_norm, v_w_kv, v_w_gate_up, v_w_down):
    xi, yi, ci = lax.axis_index("x"), lax.axis_index("y"), lax.axis_index("c")
    s = 2 * xi + yi
    sc = jnp.stack([s, ci]).astype(jnp.int32)
    weights = dict(norm_gains=norm_gains, mem_norm=mem_norm, w_in=w_in, w_mem_kv=w_mem_kv, w_out=w_out, w_pool=w_pool,
                   pool_scale=pool_scale, kv_norm=kv_norm, w_kv=w_kv, w_gate_up=w_gate_up, w_down=w_down)
    moms = dict(norm_gains=m_norm_gains, mem_norm=m_mem_norm, w_in=m_w_in, w_mem_kv=m_w_mem_kv, w_out=m_w_out,
                w_pool=m_w_pool, pool_scale=m_pool_scale, kv_norm=m_kv_norm, w_kv=m_w_kv, w_gate_up=m_w_gate_up,
                w_down=m_w_down)
    vels = dict(norm_gains=v_norm_gains, mem_norm=v_mem_norm, w_in=v_w_in, w_mem_kv=v_w_mem_kv, w_out=v_w_out,
                w_pool=v_w_pool, pool_scale=v_pool_scale, kv_norm=v_kv_norm, w_kv=v_w_kv, w_gate_up=v_w_gate_up,
                w_down=v_w_down)

    small_w = jnp.zeros((SMALL_ROWS, 256), F32)
    small_w = lax.dynamic_update_slice(small_w, norm_gains.reshape(16, 256), (0, 0))
    small_w = lax.dynamic_update_slice(small_w, pool_scale, (16, 0))
    def shard_of(nm, l):
        return (w_kv.reshape(_shard_shape('w_kv')), 0) if nm == 'w_kv' else (weights[nm], l)

    groups = {'l0a': (0, MIX_W), 'l0b': (0, FFN_W)}
    groups.update({f"l{l}": (l, LAYER_W + (('w_kv',) if l == N_A - 1 else ())) for l in range(1, DEPTH)})
    on_ici, on_d2d, gathered = {}, {}, {}

    def start_group(tag, after):
        l, names = groups[tag]
        on_ici[tag] = _gather_start(tag, names, [shard_of(nm, l) for nm in names], small_w if tag == 'l0a' else None, sc,
                                    after)
        return [on_ici[tag]['token']]

    def on_forward(where, l, after):
        if where == 'start':
            if l == 0:
                start_group('l0a', None)
                st = on_ici.pop('l0a')
                fwd = _gather_forward(st, st['token'])
                w, small_all = _gather_finish(fwd, fwd['token'])
                return w, small_all, start_group('l0b', w['w_in'])
            if f"l{l}" not in on_d2d:
                on_d2d[f"l{l}"] = _gather_forward(on_ici.pop(f"l{l}"), after)
            gathered[l] = _gather_finish(on_d2d.pop(f"l{l}"), after)[0]
            tok = start_group(f"l{l + 1}", gathered[l]['w_in']) if l + 1 < DEPTH else None
            return {nm: w for nm, w in gathered[l].items() if nm not in FFN_W}, None, tok
        if where == 'mid' and l == 0:
            on_d2d['l0b'] = _gather_forward(on_ici.pop('l0b'), after)
            return start_group('l1', on_d2d['l0b']['token'])
        if where == 'ffn':
            if l == 0:
                return (_gather_finish(on_d2d.pop('l0b'), after)[0],)
            return ({nm: gathered[l][nm] for nm in FFN_W},)
        if where == 'post' and 0 < l < DEPTH - 1:
            on_d2d[f"l{l + 1}"] = _gather_forward(on_ici.pop(f"l{l + 1}"), after)
        return None

    hook_of = {'ffn': 0, 'mix': 1, 'end': 2}
    active, reduced = [], {l: {} for l in range(DEPTH)}
    advance = {'mid': lambda st, after: _reduce_mid(st, after, sc), 'late': lambda st, after: _reduce_late(st, after, sc)}

    def run_hook(idx, after):
        toks = []
        for grp in list(active):
            while grp['plan'] and grp['plan'][0][1] <= idx:
                step = grp['plan'].pop(0)[0]
                if step == 'finish':
                    reduced[grp['layer']].update(_reduce_finish(grp['st'], after))
                    active.remove(grp)
                else:
                    grp['st'] = advance[step](grp['st'], after)
                    toks.append(grp['st']['token'])
        return toks

    def on_backward(where, l, after, grads):
        idx = 3 * (DEPTH - 1 - l) + hook_of[where]
        toks = run_hook(idx, after)
        if where == 'end' or (where == 'ffn' and l == 0):
            names = FFN_W if where == 'ffn' else tuple(nm for nm in grads if l > 0 or nm not in FFN_W)
            st = _reduce_start(f"l{l}_{where}_grads", names, {nm: grads[nm] for nm in names})
            plan = [('mid', idx + 1), ('late', idx + 3), ('finish', idx + 4)] if where == 'ffn' else \
                   [('mid', idx + 1), ('late', idx + 2), ('finish', idx + 3)]
            active.append(dict(layer=l, st=st, plan=plan))
            toks.append(st['token'])
        return toks

    loss, gx, gsmall = _local_step(x, mem, positions, on_forward, on_backward, mem_norm, w_pool, kv_norm, loss_target)
    loss = lax.psum(loss[0, 0], ("x", "y", "c"))

    vec = jnp.concatenate([gsmall[nm].reshape(-1) for nm in SMALL_ORDER])
    vec = jnp.pad(vec, (0, SMALL_VEC_ROWS * 128 - vec.shape[0])).reshape(SMALL_VEC_ROWS, 128)
    vec = vec + sum(grp['st']['token'][0, 0] for grp in active)
    small_st = _small_gather_start(vec.astype(BF), sc)
    outs = {nm: None for nm in LAYER_W}

    def adamw_layers(layers, names, after):
        for l in layers:
            for nm in names:
                outs[nm] = _adamw_layer(nm, l, weights[nm], reduced[l][nm], moms[nm], vels[nm], outs[nm], after)
                after = outs[nm][0]
        return after

    def zero_of(toks, st):
        return sum(toks) if toks else st['token']

    last = 3 * DEPTH
    toks = run_hook(last, small_st['token'])
    done = adamw_layers(range(DEPTH - 1, 0, -1), LAYER_W, zero_of(toks, small_st))
    toks = run_hook(last + 1, done)
    done = adamw_layers([0], FFN_W, zero_of(toks, small_st))
    tot = _sum8(_small_gather_finish(small_st, done), name="sum_small_grads", tr=512)
    run_hook(last + 2, tot)
    assert not active
    adamw_layers([0], MIX_W, None)
    tot = tot.reshape(-1)
    grads, off = {}, 0
    for nm in SMALL_ORDER:
        shape = (DEPTH, 4, D_MODEL) if nm == 'norm_gains' else (N_A, MAIN_W) if nm == 'pool_scale' else weights[nm].shape
        n = 1
        for dim in shape:
            n *= dim
        grads[nm] = tot[off:off + n].reshape(shape)
        off += n
    grads['norm_gains'] = lax.dynamic_slice(grads['norm_gains'], (0, 0, s * 256), (DEPTH, 4, 256))
    grads['pool_scale'] = lax.dynamic_slice(grads['pool_scale'], (0, s * POOL_GROUP), (N_A, POOL_GROUP))
    grads['w_kv'] = reduced[N_A]['w_kv'].reshape(w_kv.shape)

    order = ('norm_gains', 'mem_norm', 'w_in', 'w_mem_kv', 'w_out', 'w_pool', 'pool_scale', 'kv_norm', 'w_kv',
             'w_gate_up', 'w_down')
    deltas, new_m, new_v = {}, {}, {}
    for nm in order:
        if nm in LAYER_W:
            deltas[nm], new_m[nm], new_v[nm], grads[nm] = outs[nm]
        else:
            deltas[nm], new_m[nm], new_v[nm] = _adamw(weights[nm], grads[nm], moms[nm], vels[nm], name="adamw_" + nm)
    return (loss, gx.reshape(x.shape), *[grads[nm] for nm in order], *[deltas[nm] for nm in order],
            *[new_m[nm] for nm in order], *[new_v[nm] for nm in order])
```

```python
import jax
import jax.numpy as jnp
from jax import lax
from jax.experimental import pallas as pl
from jax.experimental.pallas import tpu as pltpu

F32 = jnp.float32
BF = jnp.bfloat16

D_MODEL = 1024
DEPTH = 4
N_A = 2
HEAD_DIM = 64
MEM_W = 256
MAIN_W = 768
D_FF = 2816
N_MEM = 256
POOL_WINDOWS = (2, 4, 8, 16)
POOL_GROUP = 192
DIL = (1, 4, 16)
STEPS = 128
ROPE_THETA = 10000.0
EPS = 1e-6
SCALE = HEAD_DIM ** -0.5
NEG = -1e30

ADAM_LR = 0.001
ADAM_B1 = 0.9
ADAM_B2 = 0.999
ADAM_EPS = 1e-08
ADAM_WD = 0.01
ADAM_STEP = 10

VMEM_LIMIT = 48 * 1024 * 1024
MESH = pl.DeviceIdType.MESH


def _cp(sem):
    return pltpu.CompilerParams(dimension_semantics=sem, vmem_limit_bytes=VMEM_LIMIT)


def _mm(a, b, *, name, ta=False, tb=False, tm=1024, tn=512, tk=1024, b_layer=None, b_offsets=(0,),
        extras=(), epilogue=None, out_dtypes=(F32,), out_n=None):
    a_pair = isinstance(a, (tuple, list))
    b_pair = isinstance(b, (tuple, list))
    a0 = a[0] if a_pair else a
    b0 = b[0] if b_pair else b
    a_rows, a_cols = a0.shape
    if a_pair:
        a_cols *= 2
    b_rows, b_cols = b0.shape[-2:]
    if b_pair:
        b_cols *= 2
    M, K = (a_cols, a_rows) if ta else (a_rows, a_cols)
    N = b_rows if tb else b_cols
    if out_n is not None:
        N = out_n
    tm, tn, tk = min(tm, M), min(tn, N), min(tk, K)
    assert M % tm == 0 and N % tn == 0 and K % tk == 0, (name, M, N, K, tm, tn, tk)
    nk = K // tk
    n_acc = len(b_offsets)

    if a_pair:
        a_half = (a0.shape[1] // (tm if ta else tk))
    if b_pair:
        b_half = (b0.shape[1] // (tk if tb else tn))

    def a_map(sel):
        def f(i, j, k):
            r, c = (k, i) if ta else (i, k)
            if a_pair:
                c = jnp.clip(c - sel * a_half, 0, a_half - 1)
            return (r, c)
        return f

    def b_map(sel, off):
        def f(i, j, k):
            r, c = (j + off, k) if tb else (k, j + off)
            if b_pair:
                c = jnp.clip(c - sel * b_half, 0, b_half - 1)
            if b_layer is not None:
                return (b_layer, r, c)
            return (r, c)
        return f

    a_blk = (tk, tm) if ta else (tm, tk)
    b_blk = (tn, tk) if tb else (tk, tn)
    if b_layer is not None:
        b_blk = (None,) + b_blk
    in_specs, operands = [], []
    for sel in range(2 if a_pair else 1):
        in_specs.append(pl.BlockSpec(a_blk, a_map(sel)))
        operands.append(a[sel] if a_pair else a)
    n_a = len(operands)
    for off in b_offsets:
        for sel in range(2 if b_pair else 1):
            in_specs.append(pl.BlockSpec(b_blk, b_map(sel, off)))
            operands.append(b[sel] if b_pair else b)
    n_b = len(operands) - n_a
    for arr, kind in extras:
        if kind == 'tile':
            in_specs.append(pl.BlockSpec((tm, tn), lambda i, j, k: (i, j)))
        elif kind == 'row':
            in_specs.append(pl.BlockSpec((tm, 1), lambda i, j, k: (i, 0)))
        else:
            in_specs.append(pl.BlockSpec((1, tn), lambda i, j, k: (0, j)))
        operands.append(arr)
    n_e = len(extras)
    n_o = len(out_dtypes)
    dims = (((0,) if ta else (1,), (1,) if tb else (0,)), ((), ()))

    def body(*refs):
        a_refs = refs[:n_a]
        b_refs = refs[n_a:n_a + n_b]
        e_refs = refs[n_a + n_b:n_a + n_b + n_e]
        n_in = n_a + n_b + n_e
        o_refs = refs[n_in:n_in + n_o]
        acc_refs = refs[n_in + n_o:]
        i, j, k = pl.program_id(0), pl.program_id(1), pl.program_id(2)
        if a_pair:
            cidx = i if ta else k
            av = jnp.where(cidx < a_half, a_refs[0][...], a_refs[1][...])
        else:
            av = a_refs[0][...]
        av = av.astype(BF)
        prods = []
        for q in range(n_acc):
            if b_pair:
                cidx = (k if tb else j) + b_offsets[q]
                bv = jnp.where(cidx < b_half, b_refs[2 * q][...], b_refs[2 * q + 1][...])
            else:
                bv = b_refs[q][...]
            prods.append(lax.dot_general(av, bv.astype(BF), dims, preferred_element_type=F32))

        def finish(accs):
            outs = epilogue(accs, *[r[...] for r in e_refs]) if epilogue is not None else accs
            for o_ref, o in zip(o_refs, outs):
                o_ref[...] = o.astype(o_ref.dtype)

        if nk == 1:
            finish(prods)
        else:
            @pl.when(k == 0)
            def _():
                for r, p in zip(acc_refs, prods):
                    r[...] = p

            @pl.when(k > 0)
            def _():
                for r, p in zip(acc_refs, prods):
                    r[...] += p

            @pl.when(k == nk - 1)
            def _():
                finish([r[...] for r in acc_refs])

    return pl.pallas_call(
        body, name=name,
        grid=(M // tm, N // tn, nk),
        in_specs=in_specs,
        out_specs=[pl.BlockSpec((tm, tn), lambda i, j, k: (i, j)) for _ in range(n_o)],
        out_shape=[jax.ShapeDtypeStruct((M, N), dt) for dt in out_dtypes],
        scratch_shapes=[pltpu.VMEM((tm, tn), F32) for _ in range(n_acc if nk > 1 else 0)],
        compiler_params=_cp(("parallel", "parallel", "arbitrary")),
    )(*operands)


def _norm_fwd(x, g, *, name, res=None, out_dtype=F32, tm=512, after=None):
    T, Dm = x.shape
    has_res = res is not None
    after = list(after or [])

    def body(*refs):
        refs = refs[:len(refs) - 1 - len(after)] + refs[len(refs) - 1:]
        if has_res:
            x_ref, g_ref, r_ref, y_ref = refs
        else:
            x_ref, g_ref, y_ref = refs
        xv = x_ref[...]
        rstd = lax.rsqrt(jnp.mean(xv * xv, axis=-1, keepdims=True) + EPS)
        y = xv * rstd * g_ref[...]
        if has_res:
            y = r_ref[...] + y
        y_ref[...] = y.astype(y_ref.dtype)

    row = pl.BlockSpec((tm, Dm), lambda i: (i, 0))
    in_specs = [row, pl.BlockSpec((1, Dm), lambda i: (0, 0))] + ([row] if has_res else [])
    in_specs += [pl.BlockSpec(memory_space=pl.ANY)] * len(after)
    ops = [x, g.reshape(1, Dm)] + ([res] if has_res else []) + after
    return pl.pallas_call(
        body, name=name, grid=(T // tm,), in_specs=in_specs,
        out_specs=row,
        out_shape=jax.ShapeDtypeStruct((T, Dm), out_dtype),
        compiler_params=_cp(("parallel",)),
    )(*ops)


def _norm_bwd(dout, x, g, *, name, add=None, out_dtype=F32, tm=512, after=None):
    T, Dm = x.shape
    has_add = add is not None
    nt = T // tm
    after = list(after or [])

    def body(*refs):
        refs = refs[:len(refs) - 3 - len(after)] + refs[len(refs) - 3:]
        if has_add:
            do_ref, x_ref, g_ref, a_ref, dx_ref, dg_ref, acc = refs
        else:
            do_ref, x_ref, g_ref, dx_ref, dg_ref, acc = refs
        i = pl.program_id(0)
        do = do_ref[...].astype(F32)
        xv = x_ref[...]
        rstd = lax.rsqrt(jnp.mean(xv * xv, axis=-1, keepdims=True) + EPS)
        xh = xv * rstd
        gd = do * g_ref[...]
        dx = rstd * (gd - xh * jnp.mean(gd * xh, axis=-1, keepdims=True))
        if has_add:
            dx = dx + a_ref[...].astype(F32)
        dx_ref[...] = dx.astype(dx_ref.dtype)
        part = jnp.sum((do * xh).reshape(tm // 8, 8, Dm), axis=0)

        @pl.when(i == 0)
        def _():
            acc[...] = part

        @pl.when(i > 0)
        def _():
            acc[...] += part

        @pl.when(i == nt - 1)
        def _():
            dg_ref[...] = jnp.sum(acc[...], axis=0, keepdims=True)

    row = pl.BlockSpec((tm, Dm), lambda i: (i, 0))
    in_specs = [row, row, pl.BlockSpec((1, Dm), lambda i: (0, 0))]
    ops = [dout, x, g.reshape(1, Dm)]
    if has_add:
        in_specs.append(row)
        ops.append(add)
    in_specs += [pl.BlockSpec(memory_space=pl.ANY)] * len(after)
    ops += after
    return pl.pallas_call(
        body, name=name, grid=(nt,), in_specs=in_specs,
        out_specs=[row, pl.BlockSpec((1, Dm), lambda i: (0, 0))],
        out_shape=[jax.ShapeDtypeStruct((T, Dm), out_dtype), jax.ShapeDtypeStruct((1, Dm), F32)],
        scratch_shapes=[pltpu.VMEM((8, Dm), F32)],
        compiler_params=_cp(("arbitrary",)),
    )(*ops)


def _swiglu_fwd_epilogue(accs):
    g, u = accs
    return g, u, g * jax.nn.sigmoid(g) * u


def _swiglu_bwd_epilogue(accs, g, u):
    da = accs[0]
    g = g.astype(F32)
    u = u.astype(F32)
    sig = jax.nn.sigmoid(g)
    return da * u * (sig * (1.0 + g * (1.0 - sig))), da * (g * sig)


def _rope_tables(pos, *, name, tm=1024):
    T = pos.shape[0]
    half = HEAD_DIM // 2
    freqs = ROPE_THETA ** (-jnp.arange(half, dtype=F32) / half)
    freqs = jnp.tile(freqs, 4).reshape(1, 128)

    def body(p_ref, f_ref, c_ref, s_ref):
        ang = p_ref[...].astype(F32) * f_ref[...]
        lane = lax.broadcasted_iota(jnp.int32, ang.shape, 1)
        c_ref[...] = jnp.cos(ang)
        s_ref[...] = jnp.where(lane % HEAD_DIM < half, -1.0, 1.0) * jnp.sin(ang)

    tab = pl.BlockSpec((tm, 128), lambda i: (i, 0))
    return pl.pallas_call(
        body, name=name, grid=(T // tm,),
        in_specs=[pl.BlockSpec((tm, 1), lambda i: (i, 0)), pl.BlockSpec((1, 128), lambda i: (0, 0))],
        out_specs=[tab, tab],
        out_shape=[jax.ShapeDtypeStruct((T, 128), F32)] * 2,
        compiler_params=_cp(("parallel",)),
    )(pos, freqs)


def _rot(x, cos, sin, sign):
    W = x.shape[1]
    half = HEAD_DIM // 2
    reps = W // 128
    c = jnp.concatenate([cos] * reps, axis=1) if reps > 1 else cos
    s = jnp.concatenate([sin] * reps, axis=1) if reps > 1 else sin
    lane = lax.broadcasted_iota(jnp.int32, x.shape, 1)
    swapped = jnp.where(lane % HEAD_DIM < half, pltpu.roll(x, W - half, axis=1), pltpu.roll(x, half, axis=1))
    return x * c + (sign * s) * swapped


def _rope_apply(x, cos, sin, *, name, sign=1.0, width=MAIN_W, passthrough=False, out_dtype=BF, alias=None,
                out_cols=None, tm=512):
    T = x.shape[0]

    def body(*refs):
        if passthrough:
            x_ref, v_ref, c_ref, s_ref, o_ref, ov_ref = refs
            ov_ref[...] = v_ref[...].astype(ov_ref.dtype)
        elif alias is not None:
            x_ref, c_ref, s_ref, _, o_ref = refs
        else:
            x_ref, c_ref, s_ref, o_ref = refs
        o_ref[...] = _rot(x_ref[...].astype(F32), c_ref[...], s_ref[...], sign).astype(o_ref.dtype)

    blk0 = pl.BlockSpec((tm, width), lambda i: (i, 0))
    blk1 = pl.BlockSpec((tm, width), lambda i: (i, 1))
    tab = pl.BlockSpec((tm, 128), lambda i: (i, 0))
    if passthrough:
        return pl.pallas_call(
            body, name=name, grid=(T // tm,), in_specs=[blk0, blk1, tab, tab], out_specs=[blk0, blk0],
            out_shape=[jax.ShapeDtypeStruct((T, width), out_dtype)] * 2,
            compiler_params=_cp(("parallel",)),
        )(x, x, cos, sin)
    if alias is not None:
        return pl.pallas_call(
            body, name=name, grid=(T // tm,),
            in_specs=[blk0, tab, tab, pl.BlockSpec(memory_space=pl.ANY)], out_specs=blk0,
            out_shape=jax.ShapeDtypeStruct(alias.shape, alias.dtype),
            input_output_aliases={3: 0},
            compiler_params=_cp(("parallel",)),
        )(x, cos, sin, alias)
    return pl.pallas_call(
        body, name=name, grid=(T // tm,), in_specs=[blk0, tab, tab], out_specs=blk0,
        out_shape=jax.ShapeDtypeStruct((T, width), out_dtype),
        compiler_params=_cp(("parallel",)),
    )(x, cos, sin)


POOL_T = 256
POOL_HALO = 16


def _pool_lane_window(shape):
    lane = lax.broadcasted_iota(jnp.int32, shape, 1)
    w = jnp.full(shape, POOL_WINDOWS[0], jnp.int32)
    for gi in range(1, len(POOL_WINDOWS)):
        w = jnp.where(lane >= gi * POOL_GROUP, POOL_WINDOWS[gi], w)
    return w


POOL_PAD = 32
POOL_R = POOL_T + POOL_PAD


def _pool_window_sums(buf, tmp_a, tmp_b, win, back):
    src, dst, acc = buf, tmp_a, None
    for j, (w, sh) in enumerate(zip(POOL_WINDOWS, (1, 2, 4, 8)), start=1):
        n = POOL_R - 8 * j
        if back:
            dst[pl.ds(8 * j, n), :] = src[pl.ds(8 * j, n), :] + src[pl.ds(8 * j - sh, n), :]
            cur = dst[pl.ds(POOL_PAD, POOL_T), :]
        else:
            dst[pl.ds(0, n), :] = src[pl.ds(0, n), :] + src[pl.ds(sh, n), :]
            cur = dst[pl.ds(0, POOL_T), :]
        acc = cur if acc is None else jnp.where(win >= w, cur, acc)
        src, dst = dst, (tmp_b if dst is tmp_a else tmp_a)
    return acc


def _pool_fwd(z, wbd, scale, B, S, *, name):
    T = z.shape[0]
    nt = S // POOL_T
    hb = POOL_T // POOL_PAD

    def body(z_ref, h_ref, w_ref, sc_ref, y_ref, p_ref, ext, tmp_a, tmp_b):
        i = pl.program_id(1)
        u = z_ref[...]
        ext[pl.ds(POOL_PAD, POOL_T), :] = u
        ext[pl.ds(0, POOL_PAD), :] = jnp.where(i > 0, h_ref[...], 0.0)
        win = _pool_lane_window((POOL_T, MAIN_W))
        acc = _pool_window_sums(ext, tmp_a, tmp_b, win, True)
        t = i * POOL_T + lax.broadcasted_iota(jnp.int32, (POOL_T, MAIN_W), 0)
        cnt = jnp.minimum(t + 1, win).astype(F32)
        p = (acc / cnt - u).astype(BF)
        p_ref[...] = p
        y = jnp.dot(p, w_ref[...], preferred_element_type=F32) * sc_ref[...]
        y_ref[...] = y.astype(y_ref.dtype)

    return pl.pallas_call(
        body, name=name, grid=(B, nt),
        in_specs=[pl.BlockSpec((POOL_T, MAIN_W), lambda b, i: (b * nt + i, 0)),
                  pl.BlockSpec((POOL_PAD, MAIN_W), lambda b, i: (jnp.maximum((b * nt + i) * hb - 1, 0), 0)),
                  pl.BlockSpec((MAIN_W, MAIN_W), lambda b, i: (0, 0)),
                  pl.BlockSpec((1, MAIN_W), lambda b, i: (0, 0))],
        out_specs=[pl.BlockSpec((POOL_T, MAIN_W), lambda b, i: (b * nt + i, 0)),
                   pl.BlockSpec((POOL_T, MAIN_W), lambda b, i: (b * nt + i, 0))],
        out_shape=[jax.ShapeDtypeStruct((T, D_MODEL), BF), jax.ShapeDtypeStruct((T, MAIN_W), BF)],
        scratch_shapes=[pltpu.VMEM((POOL_R, MAIN_W), F32)] * 3,
        compiler_params=_cp(("parallel", "parallel")),
    )(z, z, wbd, scale.reshape(1, MAIN_W))


def _pool_bwd(dy, p, wbd, scale, dz_alias, B, S, *, name):
    T = dy.shape[0]
    nt = S // POOL_T
    hb = POOL_T // POOL_HALO
    last_halo = T // POOL_HALO - 1

    def body(dy_ref, dyn_ref, p_ref, pn_ref, w_ref, sc_ref, _, dz_ref, dw_ref, ds_ref, ext, tmp_a, tmp_b, dw_acc, ds_acc):
        b, i = pl.program_id(0), pl.program_id(1)
        first = jnp.logical_and(b == 0, i == 0)
        dyv = dy_ref[...]
        pv = p_ref[...]
        sc = sc_ref[...]
        w = w_ref[...]
        pw = jnp.dot(pv, w, preferred_element_type=F32)
        ds_part = jnp.sum((dyv * pw).reshape(POOL_T // 8, 8, MAIN_W), axis=0)
        dpw = (dyv * sc).astype(BF)
        dw_part = lax.dot_general(pv, dpw, (((0,), (0,)), ((), ())), preferred_element_type=F32)

        @pl.when(first)
        def _():
            dw_acc[...] = dw_part
            ds_acc[...] = ds_part

        @pl.when(jnp.logical_not(first))
        def _():
            dw_acc[...] += dw_part
            ds_acc[...] += ds_part

        @pl.when(jnp.logical_and(b == pl.num_programs(0) - 1, i == nt - 1))
        def _():
            dw_ref[...] = dw_acc[...]
            ds_ref[...] = jnp.sum(ds_acc[...], axis=0, keepdims=True)

        dp = lax.dot_general(dpw, w, (((1,), (1,)), ((), ())), preferred_element_type=F32)
        dpn = lax.dot_general((dyn_ref[...] * sc).astype(BF), w, (((1,), (1,)), ((), ())), preferred_element_type=F32)
        win = _pool_lane_window((POOL_T, MAIN_W))
        win_n = _pool_lane_window((POOL_HALO, MAIN_W))
        t = i * POOL_T + lax.broadcasted_iota(jnp.int32, (POOL_T, MAIN_W), 0)
        tn = (i + 1) * POOL_T + lax.broadcasted_iota(jnp.int32, (POOL_HALO, MAIN_W), 0)
        ext[pl.ds(0, POOL_T), :] = dp / jnp.minimum(t + 1, win).astype(F32)
        ext[pl.ds(POOL_T, POOL_HALO), :] = jnp.where(i < nt - 1, dpn / jnp.minimum(tn + 1, win_n).astype(F32), 0.0)
        ext[pl.ds(POOL_T + POOL_HALO, POOL_PAD - POOL_HALO), :] = jnp.zeros((POOL_PAD - POOL_HALO, MAIN_W), F32)
        acc = _pool_window_sums(ext, tmp_a, tmp_b, win, False) - dp
        dz_ref[...] = acc.astype(dz_ref.dtype)

    cur = lambda b, i: (b * nt + i, 0)
    nxt = lambda b, i: (jnp.minimum((b * nt + i + 1) * hb, last_halo), 0)
    return pl.pallas_call(
        body, name=name, grid=(B, nt),
        in_specs=[pl.BlockSpec((POOL_T, MAIN_W), cur), pl.BlockSpec((POOL_HALO, MAIN_W), nxt),
                  pl.BlockSpec((POOL_T, MAIN_W), cur), pl.BlockSpec((POOL_HALO, MAIN_W), nxt),
                  pl.BlockSpec((MAIN_W, MAIN_W), lambda b, i: (0, 0)),
                  pl.BlockSpec((1, MAIN_W), lambda b, i: (0, 0)),
                  pl.BlockSpec(memory_space=pl.ANY)],
        out_specs=[pl.BlockSpec((POOL_T, MAIN_W), cur),
                   pl.BlockSpec((MAIN_W, MAIN_W), lambda b, i: (0, 0)),
                   pl.BlockSpec((1, MAIN_W), lambda b, i: (0, 0))],
        out_shape=[jax.ShapeDtypeStruct(dz_alias.shape, dz_alias.dtype),
                   jax.ShapeDtypeStruct((MAIN_W, MAIN_W), F32), jax.ShapeDtypeStruct((1, MAIN_W), F32)],
        scratch_shapes=[pltpu.VMEM((POOL_R, MAIN_W), F32)] * 3 + [pltpu.VMEM((MAIN_W, MAIN_W), F32), pltpu.VMEM((8, MAIN_W), F32)],
        input_output_aliases={6: 0},
        compiler_params=_cp(("arbitrary", "arbitrary")),
    )(dy, dy, p, p, wbd, scale.reshape(1, MAIN_W), dz_alias)


def _head_masks(shape):
    lane = lax.broadcasted_iota(jnp.int32, shape, 1)
    return [(lane // HEAD_DIM) == h for h in range(shape[1] // HEAD_DIM)]


def _row_of(bcast, mask):
    return jnp.max(jnp.where(mask, bcast, -jnp.inf), axis=-1, keepdims=True)


MEM_TQ = 2048


def _memattn_fwd(z, kv, y_alias, B, S, *, name, tq=MEM_TQ):
    T = z.shape[0]
    nt = S // tq

    def body(q_ref, k_ref, v_ref, _, y_ref, l_ref):
        q = q_ref[...]
        k = k_ref[...]
        v = v_ref[...]
        masks = _head_masks(q.shape)
        o = jnp.zeros(q.shape, F32)
        lse_b = jnp.zeros(q.shape, F32)
        for m in masks:
            qm = jnp.where(m, q, 0.0).astype(BF)
            s = lax.dot_general(qm, k, (((1,), (1,)), ((), ())), preferred_element_type=F32) * SCALE
            mx = jnp.max(s, axis=-1, keepdims=True)
            e = jnp.exp(s - mx)
            l = jnp.sum(e, axis=-1, keepdims=True)
            p = (e / l).astype(BF)
            o = o + jnp.where(m, jnp.dot(p, v, preferred_element_type=F32), 0.0)
            lse_b = lse_b + jnp.where(m, mx + jnp.log(l), 0.0)
        y_ref[...] = o.astype(y_ref.dtype)
        l_ref[...] = lse_b

    qblk = pl.BlockSpec((tq, MEM_W), lambda b, i: (b * nt + i, 3))
    return pl.pallas_call(
        body, name=name, grid=(B, nt),
        in_specs=[qblk, pl.BlockSpec((N_MEM, MEM_W), lambda b, i: (b, 0)), pl.BlockSpec((N_MEM, MEM_W), lambda b, i: (b, 1)),
                  pl.BlockSpec(memory_space=pl.ANY)],
        out_specs=[qblk, pl.BlockSpec((tq, MEM_W), lambda b, i: (b * nt + i, 0))],
        out_shape=[jax.ShapeDtypeStruct(y_alias.shape, y_alias.dtype), jax.ShapeDtypeStruct((T, MEM_W), F32)],
        input_output_aliases={3: 0},
        compiler_params=_cp(("parallel", "parallel")),
    )(z, kv, kv, y_alias)


def _memattn_bwd(dy, z, kv, lse, dz_alias, B, S, *, name, tq=MEM_TQ):
    nt = S // tq

    def body(do_ref, q_ref, k_ref, v_ref, l_ref, _, dz_ref, dk_ref, dv_ref, dk_acc, dv_acc):
        i = pl.program_id(1)
        do = do_ref[...]
        q = q_ref[...]
        k = k_ref[...]
        v = v_ref[...]
        lse_b = l_ref[...]
        masks = _head_masks(q.shape)
        dq = jnp.zeros(q.shape, F32)
        dk = jnp.zeros(k.shape, F32)
        dv = jnp.zeros(v.shape, F32)
        for m in masks:
            qm = jnp.where(m, q, 0.0).astype(BF)
            dom = jnp.where(m, do, 0.0).astype(BF)
            s = lax.dot_general(qm, k, (((1,), (1,)), ((), ())), preferred_element_type=F32) * SCALE
            p = jnp.exp(s - _row_of(lse_b, m))
            dp = lax.dot_general(dom, v, (((1,), (1,)), ((), ())), preferred_element_type=F32)
            delta = jnp.sum(p * dp, axis=-1, keepdims=True)
            ds = (p * (dp - delta) * SCALE).astype(BF)
            pb = p.astype(BF)
            dv = dv + jnp.where(m[:N_MEM], lax.dot_general(pb, dom, (((0,), (0,)), ((), ())), preferred_element_type=F32), 0.0)
            dk = dk + jnp.where(m[:N_MEM], lax.dot_general(ds, qm, (((0,), (0,)), ((), ())), preferred_element_type=F32), 0.0)
            dq = dq + jnp.where(m, jnp.dot(ds, k, preferred_element_type=F32), 0.0)
        dz_ref[...] = dq.astype(dz_ref.dtype)

        @pl.when(i == 0)
        def _():
            dk_acc[...] = dk
            dv_acc[...] = dv

        @pl.when(i > 0)
        def _():
            dk_acc[...] += dk
            dv_acc[...] += dv

        @pl.when(i == nt - 1)
        def _():
            dk_ref[...] = dk_acc[...]
            dv_ref[...] = dv_acc[...]

    qblk = pl.BlockSpec((tq, MEM_W), lambda b, i: (b * nt + i, 3))
    kblk = pl.BlockSpec((N_MEM, MEM_W), lambda b, i: (b, 0))
    return pl.pallas_call(
        body, name=name, grid=(B, nt),
        in_specs=[qblk, qblk, kblk, pl.BlockSpec((N_MEM, MEM_W), lambda b, i: (b, 1)),
                  pl.BlockSpec((tq, MEM_W), lambda b, i: (b * nt + i, 0)), pl.BlockSpec(memory_space=pl.ANY)],
        out_specs=[qblk, kblk, kblk],
        out_shape=[jax.ShapeDtypeStruct(dz_alias.shape, dz_alias.dtype),
                   jax.ShapeDtypeStruct((B * N_MEM, MEM_W), F32), jax.ShapeDtypeStruct((B * N_MEM, MEM_W), F32)],
        scratch_shapes=[pltpu.VMEM((N_MEM, MEM_W), F32), pltpu.VMEM((N_MEM, MEM_W), F32)],
        input_output_aliases={5: 0},
        compiler_params=_cp(("parallel", "arbitrary")),
    )(dy, z, kv, kv, lse, dz_alias)


N_UNITS = 16


def _unit_rows(g):
    d = DIL[g]
    nb = N_UNITS // d
    return [pl.ds(n * STEPS * d + r, STEPS, stride=d) if d > 1 else pl.ds(n * STEPS, STEPS)
            for n in range(nb) for r in range(d)]


def _load_units(ref, g):
    if DIL[g] == 1:
        return ref[...].reshape(N_UNITS, STEPS, 128)
    return jnp.stack([ref[rows, :] for rows in _unit_rows(g)])


def _store_units(ref, val, g):
    if DIL[g] == 1:
        ref[...] = val.reshape(N_UNITS * STEPS, 128)
    else:
        for u, rows in enumerate(_unit_rows(g)):
            ref[rows, :] = val[u]


def _shift_units(x, by):
    z = jnp.zeros((abs(by),) + x.shape[1:], x.dtype)
    return jnp.concatenate([z, x[:N_UNITS - by]], axis=0) if by > 0 else jnp.concatenate([x[-by:], z], axis=0)


def _bdot(a, b, ca, cb):
    return lax.dot_general(a, b, (((ca,), (cb,)), ((0,), (0,))), preferred_element_type=F32)


def _dil_masks(g):
    d = DIL[g]
    has_prev = N_UNITS // d > 1
    qi = lax.broadcasted_iota(jnp.int32, (1, STEPS, STEPS), 1)
    kj = lax.broadcasted_iota(jnp.int32, (1, STEPS, STEPS), 2)
    unit = lax.broadcasted_iota(jnp.int32, (N_UNITS, 1, 1), 0)
    cur = kj <= qi
    prev = jnp.logical_and(kj >= qi, unit >= d) if has_prev else None
    lane = lax.broadcasted_iota(jnp.int32, (1, 1, 128), 2)
    heads = [(lane // HEAD_DIM) == h for h in range(128 // HEAD_DIM)]
    return has_prev, cur, prev, heads


def _dil_fwd(g, q, k, v, o_alias, l_alias, B, S, *, name):
    assert S == N_UNITS * STEPS
    d = DIL[g]

    def body(q_ref, k_ref, v_ref, _, __, o_ref, l_ref):
        has_prev, cur, prev, heads = _dil_masks(g)
        q = _load_units(q_ref, g)
        kc = _load_units(k_ref, g).astype(BF)
        vc = _load_units(v_ref, g).astype(BF)
        if has_prev:
            kp, vp = _shift_units(kc, d), _shift_units(vc, d)
        o = jnp.zeros(q.shape, F32)
        lse_b = jnp.zeros(q.shape, F32)
        for m in heads:
            qm = jnp.where(m, q, 0.0).astype(BF)
            sc = jnp.where(cur, _bdot(qm, kc, 2, 2) * SCALE, NEG)
            mx = jnp.max(sc, axis=-1, keepdims=True)
            if has_prev:
                sp = jnp.where(prev, _bdot(qm, kp, 2, 2) * SCALE, NEG)
                mx = jnp.maximum(mx, jnp.max(sp, axis=-1, keepdims=True))
            l = jnp.sum(jnp.exp(sc - mx), axis=-1, keepdims=True)
            if has_prev:
                l = l + jnp.sum(jnp.exp(sp - mx), axis=-1, keepdims=True)
            lse = mx + jnp.log(l)
            oh = _bdot(jnp.exp(sc - lse).astype(BF), vc, 2, 1)
            if has_prev:
                oh = oh + _bdot(jnp.exp(sp - lse).astype(BF), vp, 2, 1)
            o = o + jnp.where(m, oh, 0.0)
            lse_b = lse_b + jnp.where(m, lse, 0.0)
        _store_units(o_ref, o, g)
        _store_units(l_ref, lse_b, g)

    blk = pl.BlockSpec((S, 128), lambda b, hf: (b, g * 2 + hf))
    anyspec = pl.BlockSpec(memory_space=pl.ANY)
    o, l = pl.pallas_call(
        body, name=name, grid=(B, 2),
        in_specs=[blk, blk, blk, anyspec, anyspec], out_specs=[blk, blk],
        out_shape=[jax.ShapeDtypeStruct(q.shape, F32)] * 2,
        input_output_aliases={3: 0, 4: 1},
        compiler_params=_cp(("parallel", "parallel")),
    )(q, k, v, o_alias, l_alias)
    return o, l


def _dil_bwd(g, q, k, v, do, cb, lse, aliases, B, S, *, name):
    assert S == N_UNITS * STEPS
    d = DIL[g]

    def body(q_ref, k_ref, v_ref, do_ref, c_ref, l_ref, _, __, ___, dq_ref, dk_ref, dv_ref):
        has_prev, cur, prev, heads = _dil_masks(g)
        q = _load_units(q_ref, g)
        kc = _load_units(k_ref, g).astype(BF)
        vc = _load_units(v_ref, g).astype(BF)
        do = _load_units(do_ref, g)
        cbv = _load_units(c_ref, g)
        lse_b = _load_units(l_ref, g)
        if has_prev:
            kp, vp = _shift_units(kc, d), _shift_units(vc, d)
        z = jnp.zeros(q.shape, F32)
        dq, dkc, dkp, dvc, dvp = z, z, z, z, z
        for m in heads:
            qm = jnp.where(m, q, 0.0).astype(BF)
            dom = jnp.where(m, do, 0.0).astype(BF)
            lse = jnp.max(jnp.where(m, lse_b, -jnp.inf), axis=-1, keepdims=True)
            c = jnp.max(jnp.where(m, cbv, -jnp.inf), axis=-1, keepdims=True)
            sc = jnp.where(cur, _bdot(qm, kc, 2, 2) * SCALE, NEG)
            pc = jnp.exp(sc - lse)
            dsc = (pc * (_bdot(dom, vc, 2, 2) + c) * SCALE).astype(BF)
            dqh = _bdot(dsc, kc, 2, 1)
            dkc = dkc + jnp.where(m, _bdot(dsc, qm, 1, 1), 0.0)
            dvc = dvc + jnp.where(m, _bdot(pc.astype(BF), dom, 1, 1), 0.0)
            if has_prev:
                sp = jnp.where(prev, _bdot(qm, kp, 2, 2) * SCALE, NEG)
                pp = jnp.exp(sp - lse)
                dsp = (pp * (_bdot(dom, vp, 2, 2) + c) * SCALE).astype(BF)
                dqh = dqh + _bdot(dsp, kp, 2, 1)
                dkp = dkp + jnp.where(m, _bdot(dsp, qm, 1, 1), 0.0)
                dvp = dvp + jnp.where(m, _bdot(pp.astype(BF), dom, 1, 1), 0.0)
            dq = dq + jnp.where(m, dqh, 0.0)
        if has_prev:
            dkc = dkc + _shift_units(dkp, -d)
            dvc = dvc + _shift_units(dvp, -d)
        _store_units(dq_ref, dq, g)
        _store_units(dk_ref, dkc, g)
        _store_units(dv_ref, dvc, g)

    blk = pl.BlockSpec((S, 128), lambda b, hf: (b, g * 2 + hf))
    anyspec = pl.BlockSpec(memory_space=pl.ANY)
    return tuple(pl.pallas_call(
        body, name=name, grid=(B, 2),
        in_specs=[blk] * 6 + [anyspec] * 3, out_specs=[blk] * 3,
        out_shape=[jax.ShapeDtypeStruct(q.shape, F32)] * 3,
        input_output_aliases={6: 0, 7: 1, 8: 2},
        compiler_params=_cp(("parallel", "parallel")),
    )(q, k, v, do, cb, lse, *aliases))


def _kv_grad_sum(parts, cos, sin, *, name, tm=512):
    T = parts[0][0].shape[0]
    n_l = len(parts)

    def body(*refs):
        c_ref, s_ref = refs[0], refs[1]
        dk_ref, dv_ref = refs[2 + 2 * n_l:]
        dk = refs[2][...]
        dv = refs[3][...]
        for li in range(1, n_l):
            dk = dk + refs[2 + 2 * li][...]
            dv = dv + refs[3 + 2 * li][...]
        dk_ref[...] = _rot(dk, c_ref[...], s_ref[...], -1.0).astype(dk_ref.dtype)
        dv_ref[...] = dv.astype(dv_ref.dtype)

    full = pl.BlockSpec((tm, MAIN_W), lambda i: (i, 0))
    tab = pl.BlockSpec((tm, 128), lambda i: (i, 0))
    ops = [cos, sin] + [t for part in parts for t in part]
    return pl.pallas_call(
        body, name=name, grid=(T // tm,), in_specs=[tab, tab] + [full] * (2 * n_l), out_specs=[full, full],
        out_shape=[jax.ShapeDtypeStruct((T, MAIN_W), BF)] * 2,
        compiler_params=_cp(("parallel",)),
    )(*ops)


def _group_softmax(lse):
    l0, l1, l2 = lse[:, 0:256], lse[:, 256:512], lse[:, 512:768]
    mx = jnp.maximum(jnp.maximum(l0, l1), l2)
    e0, e1, e2 = jnp.exp(l0 - mx), jnp.exp(l1 - mx), jnp.exp(l2 - mx)
    tot = e0 + e1 + e2
    return e0 / tot, e1 / tot, e2 / tot


def _dil_combine_fwd(o, lse, y_alias, *, name, tm=512):
    T = o.shape[0]

    def body(o_ref, l_ref, _, y_ref):
        a = jnp.concatenate(_group_softmax(l_ref[...]), axis=1)
        y_ref[...] = (o_ref[...] * a).astype(y_ref.dtype)

    blk = pl.BlockSpec((tm, MAIN_W), lambda i: (i, 0))
    return pl.pallas_call(
        body, name=name, grid=(T // tm,), in_specs=[blk, blk, pl.BlockSpec(memory_space=pl.ANY)], out_specs=blk,
        out_shape=jax.ShapeDtypeStruct(y_alias.shape, y_alias.dtype), input_output_aliases={2: 0},
        compiler_params=_cp(("parallel",)),
    )(o, lse, y_alias)


def _dil_combine_bwd(dy, o, lse, *, name, tm=256):
    T = o.shape[0]
    lane_r = lax.broadcasted_iota(jnp.int32, (256, 256), 0) // HEAD_DIM
    lane_c = lax.broadcasted_iota(jnp.int32, (256, 256), 1) // HEAD_DIM
    ones_bd = (lane_r == lane_c).astype(BF)

    def body(dy_ref, o_ref, l_ref, e_ref, do_ref, c_ref):
        dyv = dy_ref[...]
        alphas = _group_softmax(l_ref[...])
        prod = dyv * o_ref[...]
        e = e_ref[...]
        tot = jnp.zeros((tm, 256), F32)
        for gi in range(3):
            x = prod[:, gi * 256:(gi + 1) * 256]
            hi = x.astype(BF)
            lo = (x - hi.astype(F32)).astype(BF)
            dalpha = jnp.dot(hi, e, preferred_element_type=F32) + jnp.dot(lo, e, preferred_element_type=F32)
            tot = tot + alphas[gi] * dalpha
        a = jnp.concatenate(alphas, axis=1)
        do_ref[...] = (dyv * a).astype(do_ref.dtype)
        c_ref[...] = jnp.concatenate([-al * tot for al in alphas], axis=1)

    blk = pl.BlockSpec((tm, MAIN_W), lambda i: (i, 0))
    return pl.pallas_call(
        body, name=name, grid=(T // tm,),
        in_specs=[blk, blk, blk, pl.BlockSpec((256, 256), lambda i: (0, 0))], out_specs=[blk, blk],
        out_shape=[jax.ShapeDtypeStruct((T, MAIN_W), F32), jax.ShapeDtypeStruct((T, MAIN_W), F32)],
        compiler_params=_cp(("parallel",)),
    )(dy, o, lse, ones_bd)


def _loss(y, target, *, name, tm=512):
    T, Dm = y.shape
    nt = T // tm

    def body(y_ref, t_ref, l_ref, d_ref, acc):
        i = pl.program_id(0)
        err = y_ref[...] - t_ref[...]
        d_ref[...] = err / Dm
        part = jnp.sum(jnp.mean(err * err, axis=-1, keepdims=True).reshape(tm // 8, 8, 1), axis=0)

        @pl.when(i == 0)
        def _():
            acc[...] = part

        @pl.when(i > 0)
        def _():
            acc[...] += part

        @pl.when(i == nt - 1)
        def _():
            l_ref[...] = 0.5 * jnp.sum(acc[...], axis=0, keepdims=True)

    row = pl.BlockSpec((tm, Dm), lambda i: (i, 0))
    return pl.pallas_call(
        body, name=name, grid=(nt,), in_specs=[row, row],
        out_specs=[pl.BlockSpec((1, 1), lambda i: (0, 0)), row],
        out_shape=[jax.ShapeDtypeStruct((1, 1), F32), jax.ShapeDtypeStruct((T, Dm), F32)],
        scratch_shapes=[pltpu.VMEM((8, 1), F32)],
        compiler_params=_cp(("arbitrary",)),
    )(y, target)


def _adamw(w, g, m, v, *, name):
    shape = w.shape
    cols = shape[-1]
    rows = w.size // cols
    tm = rows
    for cand in (512, 352, 256, 128):
        if rows > cand and rows % cand == 0 and cand * cols * 4 <= (1 << 20):
            tm = cand
            break

    def body(w_ref, g_ref, m_ref, v_ref, d_ref, mo_ref, vo_ref):
        gv = g_ref[...]
        mn = ADAM_B1 * m_ref[...] + (1.0 - ADAM_B1) * gv
        vn = ADAM_B2 * v_ref[...] + (1.0 - ADAM_B2) * (gv * gv)
        m_hat = mn / (1.0 - ADAM_B1 ** ADAM_STEP)
        v_hat = vn / (1.0 - ADAM_B2 ** ADAM_STEP)
        d_ref[...] = -ADAM_LR * (m_hat / (jnp.sqrt(v_hat) + ADAM_EPS) + ADAM_WD * w_ref[...])
        mo_ref[...] = mn
        vo_ref[...] = vn

    blk = pl.BlockSpec((tm, cols), lambda i: (i, 0))
    outs = pl.pallas_call(
        body, name=name, grid=(rows // tm,), in_specs=[blk] * 4, out_specs=[blk] * 3,
        out_shape=[jax.ShapeDtypeStruct((rows, cols), F32)] * 3,
        compiler_params=_cp(("parallel",)),
    )(*[t.reshape(rows, cols) for t in (w, g, m, v)])
    return tuple(t.reshape(shape) for t in outs)


def _adamw_layer(name, l, w, g, m, v, prev, after=None):
    L, rows, cols = w.shape
    tm = rows
    for cand in (512, 352, 256, 176, 128, 64):
        if rows % cand == 0 and cand * cols * 4 <= (1 << 21):
            tm = cand
            break
    if prev is None:
        prev = tuple(lax.empty(w.shape, F32) for _ in range(4))

    n_after = 0 if after is None else 1

    def body(w_ref, g_ref, m_ref, v_ref, *rest):
        d_ref, mo_ref, vo_ref, go_ref = rest[4 + n_after:]
        gv = g_ref[...]
        mn = ADAM_B1 * m_ref[...] + (1.0 - ADAM_B1) * gv
        vn = ADAM_B2 * v_ref[...] + (1.0 - ADAM_B2) * (gv * gv)
        m_hat = mn / (1.0 - ADAM_B1 ** ADAM_STEP)
        v_hat = vn / (1.0 - ADAM_B2 ** ADAM_STEP)
        d_ref[...] = -ADAM_LR * (m_hat / (jnp.sqrt(v_hat) + ADAM_EPS) + ADAM_WD * w_ref[...])
        mo_ref[...] = mn
        vo_ref[...] = vn
        go_ref[...] = gv

    lay = pl.BlockSpec((None, tm, cols), lambda i: (l, i, 0))
    one = pl.BlockSpec((None, tm, cols), lambda i: (0, i, 0))
    return tuple(pl.pallas_call(
        body, name=f"l{l}_adamw_{name}", grid=(rows // tm,),
        in_specs=[lay, one, lay, lay] + [pl.BlockSpec(memory_space=pl.ANY)] * (4 + n_after), out_specs=[lay] * 4,
        out_shape=[jax.ShapeDtypeStruct(w.shape, F32)] * 4,
        input_output_aliases={4 + i: i for i in range(4)},
        compiler_params=_cp(("parallel",)),
    )(w, g, m, v, *prev, *([] if after is None else [after])))


BIG = {
    'w_in': ((DEPTH, D_MODEL, D_MODEL), 'row'),
    'w_mem_kv': ((DEPTH, D_MODEL, 2 * MEM_W), 'row'),
    'w_out': ((DEPTH, D_MODEL, D_MODEL), 'row'),
    'w_kv': ((1, D_MODEL, 2 * MAIN_W), 'col'),
    'w_gate_up': ((DEPTH, D_MODEL, 2 * D_FF), 'col'),
    'w_down': ((DEPTH, D_FF, D_MODEL), 'row'),
}
BIG_NAMES = tuple(BIG)
N_CHIPS = 4
HBM_ANY = pl.BlockSpec(memory_space=pl.ANY)


def _geom(name):
    (L, R, C), kind = BIG[name]
    if kind == 'row':
        return L, R, C, kind, R // N_CHIPS, C, R // (2 * N_CHIPS)
    return L, R, C, kind, R, C // N_CHIPS, R // 2


def _shard_shape(name):
    L, R, C, kind, rs, cs, rh = _geom(name)
    return (L, rs, cs)


def _half_shape(name):
    L, R, C, kind, rs, cs, rh = _geom(name)
    return (L, rh, cs)


def _full_win(ref, name, s, h):
    L, R, C, kind, rs, cs, rh = _geom(name)
    if kind == 'row':
        rows = pl.ds(s * rs, rs) if h is None else pl.ds(s * rs + h * rh, rh)
        return ref.at[:, rows, :]
    rows = slice(None) if h is None else pl.ds(h * rh, rh)
    return ref.at[:, rows, pl.ds(s * cs, cs)]


def _shard_half(ref, name, h):
    L, R, C, kind, rs, cs, rh = _geom(name)
    return ref.at[:, pl.ds(h * rh, rh), :]


def _halves_win(ref, name, s):
    L, R, C, kind, rs, cs, rh = _geom(name)
    if kind == 'row':
        return ref.at[:, pl.ds(s * rh, rh), :]
    return ref.at[:, :, pl.ds(s * cs, cs)]


def _halves_shape(name):
    L, R, C, kind, rs, cs, rh = _geom(name)
    return (L, N_CHIPS * rh, cs) if kind == 'row' else (L, rh, C)


def _place():
    x, y, c = lax.axis_index("x"), lax.axis_index("y"), lax.axis_index("c")
    chips = [(1 - x, y), (x, 1 - y), (1 - x, 1 - y)]
    return x, y, c, chips


SMALL_ROWS = 24


SEM_SPEC = pl.BlockSpec(memory_space=pltpu.SEMAPHORE)
HBM_SPEC = pl.BlockSpec(memory_space=pltpu.HBM)
DATAFLOW = pltpu.SideEffectType.DATAFLOW_SIDE_EFFECTING


def _in_hbm(a):
    return pltpu.with_memory_space_constraint(a, pltpu.HBM)


def _remote(src, dst, send_sems, recv_sems, k, to):
    return pltpu.make_async_remote_copy(src_ref=src, dst_ref=dst, send_sem=send_sems.at[k], recv_sem=recv_sems.at[k],
                                        device_id=to, device_id_type=MESH)


def _split_start(name, bufs, n_copies, sends, after=None):
    nb = len(bufs)
    n_in = nb + (0 if after is None else 1)

    def body(*refs):
        in_refs = refs[:nb]
        send_sems, recv_sems = refs[n_in], refs[n_in + 1]
        token = refs[-1]
        for k, (src, dst, to) in enumerate(sends(in_refs)):
            _remote(src, dst, send_sems, recv_sems, k, to).start()
        token[...] = jnp.zeros_like(token)

    outs = pl.pallas_call(
        body, name=name,
        out_shape=(pltpu.SemaphoreType.DMA((n_copies,)), pltpu.SemaphoreType.DMA((n_copies,)),
                   *[pltpu.HBM(b.shape, b.dtype) for b in bufs], jax.ShapeDtypeStruct((8, 128), F32)),
        in_specs=[HBM_SPEC] * nb + [HBM_ANY] * (n_in - nb),
        out_specs=(SEM_SPEC, SEM_SPEC, *[HBM_SPEC] * nb, pl.BlockSpec(memory_space=pltpu.VMEM)),
        input_output_aliases={i: 2 + i for i in range(nb)},
        compiler_params=pltpu.CompilerParams(has_side_effects=DATAFLOW),
    )(*[_in_hbm(b) for b in bufs], *([] if after is None else [after]))
    return outs[0], outs[1], list(outs[2:2 + nb]), outs[-1]


def _split_wait(name, send_sems, recv_sems, bufs, after, sends, arrivals):
    nb = len(bufs)

    def body(*refs):
        in_refs = refs[:nb]
        s_sems, r_sems = refs[nb], refs[nb + 1]
        me = (lax.axis_index("x"), lax.axis_index("y"), lax.axis_index("c"))
        for k, (src, dst, to) in enumerate(sends(in_refs)):
            _remote(src, dst, s_sems, r_sems, k, to).wait_send()
        for k, win in enumerate(arrivals(in_refs)):
            _remote(win, win, s_sems, r_sems, k, me).wait_recv()

    outs = pl.pallas_call(
        body, name=name,
        out_shape=[pltpu.HBM(b.shape, b.dtype) for b in bufs],
        in_specs=[HBM_SPEC] * nb + [SEM_SPEC, SEM_SPEC, HBM_ANY],
        out_specs=[HBM_SPEC] * nb,
        input_output_aliases={i: i for i in range(nb)},
        compiler_params=pltpu.CompilerParams(has_side_effects=DATAFLOW),
    )(*bufs, send_sems, recv_sems, after)
    return list(outs)


MIX_W = ('w_in', 'w_mem_kv', 'w_out')
FFN_W = ('w_gate_up', 'w_down')
LAYER_W = MIX_W + FFN_W


def _place_own(tag, names, sources, small, sc):
    nw = len(names)
    has_small = small is not None

    def body(sc_ref, *refs):
        srcs = refs[:nw]
        shard_out = refs[nw + has_small:2 * nw + has_small]
        full_out = refs[2 * nw + has_small:3 * nw + has_small]
        for src, sh, fu in zip(srcs, shard_out, full_out):
            v = src[...].astype(BF)
            sh[...] = v
            fu[...] = v
        if has_small:
            refs[-1][...] = refs[nw][...]

    in_specs, shard_specs, full_specs, shard_shape, full_shape, ops = [], [], [], [], [], []
    for nm, (arr, layer) in zip(names, sources):
        L, R, C, kind, rs, cs, rh = _geom(nm)
        in_specs.append(pl.BlockSpec((1, rs, cs), lambda i, sc_ref, layer=layer: (layer, 0, 0)))
        shard_specs.append(pl.BlockSpec((1, rs, cs), lambda i, sc_ref: (0, 0, 0)))
        if kind == 'row':
            full_specs.append(pl.BlockSpec((1, rs, cs), lambda i, sc_ref: (0, sc_ref[0], 0)))
        else:
            full_specs.append(pl.BlockSpec((1, rs, cs), lambda i, sc_ref: (0, 0, sc_ref[0])))
        shard_shape.append(jax.ShapeDtypeStruct((1, rs, cs), BF))
        full_shape.append(jax.ShapeDtypeStruct((1, R, C), BF))
        ops.append(arr)
    if has_small:
        in_specs.append(pl.BlockSpec((SMALL_ROWS, 256), lambda i, sc_ref: (0, 0)))
        full_specs.append(pl.BlockSpec((None, SMALL_ROWS, 256), lambda i, sc_ref: (sc_ref[0], 0, 0)))
        full_shape.append(jax.ShapeDtypeStruct((N_CHIPS, SMALL_ROWS, 256), F32))
        ops.append(small)
    outs = pl.pallas_call(
        body, name=f"{tag}_place_own_shard",
        grid_spec=pltpu.PrefetchScalarGridSpec(num_scalar_prefetch=1, grid=(1,), in_specs=in_specs,
                                               out_specs=shard_specs + full_specs),
        out_shape=shard_shape + full_shape,
        compiler_params=_cp(("arbitrary",)),
    )(sc, *ops)
    return list(outs[:nw]), list(outs[nw:])


def _gather_start(l, names, sources, small, sc, after=None):
    nw = len(names)
    has_small = small is not None
    shards, fulls = _place_own(l, names, sources, small, sc)
    bufs = list(shards) + ([small] if has_small else []) + list(fulls)
    n_src = nw + (1 if has_small else 0)

    def sends(refs):
        x, y, c, chips = _place()
        s = 2 * x + y
        out = []
        for (px, py) in chips:
            for wi, nm in enumerate(names):
                out.append((_shard_half(refs[wi], nm, c), _full_win(refs[n_src + wi], nm, s, c), (px, py, c)))
            if has_small:
                out.append((refs[nw], refs[n_src + nw].at[s], (px, py, c)))
        return out

    def arrivals(refs):
        x, y, c, chips = _place()
        out = []
        for (px, py) in chips:
            sp = 2 * px + py
            for wi, nm in enumerate(names):
                out.append(_full_win(refs[n_src + wi], nm, sp, c))
            if has_small:
                out.append(refs[n_src + nw].at[sp])
        return out

    n_copies = 3 * n_src
    send_sems, recv_sems, bufs, token = _split_start(f"{l}_gather_ici_start", bufs, n_copies, sends, after)
    return dict(l=l, names=names, has_small=has_small, sems=(send_sems, recv_sems), bufs=bufs, sends=sends,
                arrivals=arrivals, token=token)


def _gather_forward(st, after):
    l, names = st['l'], st['names']
    nw = len(names)
    n_src = nw + (1 if st['has_small'] else 0)
    bufs = _split_wait(f"{l}_gather_ici_wait", *st['sems'], st['bufs'], after, st['sends'], st['arrivals'])
    fulls = bufs[n_src:n_src + nw]
    small_all = bufs[n_src + nw] if st['has_small'] else None

    def sends(refs):
        x, y, c, chips = _place()
        out = []
        for (px, py) in chips:
            sp = 2 * px + py
            for wi, nm in enumerate(names):
                w = _full_win(refs[wi], nm, sp, c)
                out.append((w, w, (x, y, 1 - c)))
        return out

    def arrivals(refs):
        x, y, c, chips = _place()
        out = []
        for (px, py) in chips:
            sp = 2 * px + py
            for wi, nm in enumerate(names):
                out.append(_full_win(refs[wi], nm, sp, 1 - c))
        return out

    send_sems, recv_sems, fulls, token = _split_start(f"{l}_gather_d2d_start", fulls, 3 * nw, sends)
    return dict(l=l, names=names, sems=(send_sems, recv_sems), bufs=fulls, sends=sends, arrivals=arrivals,
                small_all=small_all, token=token)


def _gather_finish(st, after):
    fulls = _split_wait(f"{st['l']}_gather_d2d_wait", *st['sems'], st['bufs'], after, st['sends'], st['arrivals'])
    return dict(zip(st['names'], fulls)), st['small_all']


def _reduce_start(tag, names, grads):
    nw = len(names)
    recv = [lax.empty((1,) + _halves_shape(nm)[1:], F32) for nm in names]
    bufs = [grads[nm] for nm in names] + recv

    def windows(refs, half_of):
        x, y, c, _ = _place()
        h = half_of(c)
        out = []
        for wi, nm in enumerate(names):
            L, R, C, kind, rs, cs, rh = _geom(nm)
            if kind == 'row':
                for sp in range(N_CHIPS):
                    out.append((_full_win(refs[wi], nm, sp, h), _halves_win(refs[nw + wi], nm, sp)))
            else:
                out.append((refs[wi].at[:, pl.ds(h * rh, rh), :], refs[nw + wi]))
        return out

    def sends(refs):
        x, y, c, _ = _place()
        return [(src, dst, (x, y, 1 - c)) for src, dst in windows(refs, lambda c: 1 - c)]

    def arrivals(refs):
        return [dst for _, dst in windows(refs, lambda c: c)]

    n_copies = sum(N_CHIPS if BIG[nm][1] == 'row' else 1 for nm in names)
    send_sems, recv_sems, bufs, token = _split_start(tag + "_halves_start", bufs, n_copies, sends)
    return dict(tag=tag, names=names, sems=(send_sems, recv_sems), bufs=bufs, sends=sends, arrivals=arrivals, token=token)


def _reduce_mid(st, after, sc):
    tag, names = st['tag'], st['names']
    nw = len(names)
    bufs = _split_wait(tag + "_halves_wait", *st['sems'], st['bufs'], after, st['sends'], st['arrivals'])
    halves, own = [], []
    for wi, nm in enumerate(names):
        hb, ow = _add_halves(nm, bufs[wi], bufs[nw + wi], sc, tag)
        halves.append(hb)
        own.append(ow)
    pieces = [lax.empty((3, 1) + _half_shape(nm)[1:], BF) for nm in names]

    def sends(refs):
        x, y, c, chips = _place()
        out = []
        for j, (px, py) in enumerate(chips):
            for wi, nm in enumerate(names):
                out.append((_halves_win(refs[wi], nm, 2 * px + py), refs[nw + wi].at[j], (px, py, c)))
        return out

    def arrivals(refs):
        return [refs[nw + wi].at[j] for j in range(3) for wi in range(nw)]

    send_sems, recv_sems, bufs, token = _split_start(tag + "_pieces_start", halves + pieces, 3 * nw, sends)
    return dict(tag=tag, names=names, sems=(send_sems, recv_sems), bufs=bufs, sends=sends, arrivals=arrivals, own=own,
                token=token)


def _reduce_late(st, after, sc):
    tag, names = st['tag'], st['names']
    nw = len(names)
    bufs = _split_wait(tag + "_pieces_wait", *st['sems'], st['bufs'], after, st['sends'], st['arrivals'])
    gsh = [_sum_pieces(nm, st['own'][wi], bufs[nw + wi], sc, tag) for wi, nm in enumerate(names)]

    def sends(refs):
        x, y, c, _ = _place()
        return [(_shard_half(refs[wi], nm, c), _shard_half(refs[wi], nm, c), (x, y, 1 - c)) for wi, nm in enumerate(names)]

    def arrivals(refs):
        x, y, c, _ = _place()
        return [_shard_half(refs[wi], nm, 1 - c) for wi, nm in enumerate(names)]

    send_sems, recv_sems, bufs, token = _split_start(tag + "_share_start", gsh, nw, sends)
    return dict(tag=tag, names=names, sems=(send_sems, recv_sems), bufs=bufs, sends=sends, arrivals=arrivals, token=token)


def _reduce_finish(st, after):
    gsh = _split_wait(st['tag'] + "_share_wait", *st['sems'], st['bufs'], after, st['sends'], st['arrivals'])
    return dict(zip(st['names'], gsh))


def _add_halves(name, g, r, sc, tag):
    _, R, C, kind, rs, cs, rh = _geom(name)
    L = g.shape[0]
    tr = rh if kind == 'row' else 256
    nr = rh // tr

    def body(sc_ref, g_ref, r_ref, hb_ref, own_ref):
        sp = pl.program_id(2)
        tot = g_ref[...] + r_ref[...]
        hb_ref[...] = tot.astype(hb_ref.dtype)

        @pl.when(sp == sc_ref[0])
        def _():
            own_ref[...] = tot

    if kind == 'row':
        g_map = lambda l, ri, sp, sc_ref: (l, sp * 2 + sc_ref[1], 0)
        h_map = lambda l, ri, sp, sc_ref: (l, sp, 0)
    else:
        g_map = lambda l, ri, sp, sc_ref: (l, sc_ref[1] * nr + ri, sp)
        h_map = lambda l, ri, sp, sc_ref: (l, ri, sp)
    own_map = lambda l, ri, sp, sc_ref: (l, ri, 0)
    blk = (None, tr, cs)
    return pl.pallas_call(
        body, name=tag + "_add_halves_" + name,
        grid_spec=pltpu.PrefetchScalarGridSpec(
            num_scalar_prefetch=1, grid=(L, nr, N_CHIPS),
            in_specs=[pl.BlockSpec(blk, g_map), pl.BlockSpec(blk, h_map)],
            out_specs=[pl.BlockSpec(blk, h_map), pl.BlockSpec(blk, own_map)]),
        out_shape=[jax.ShapeDtypeStruct((L,) + _halves_shape(name)[1:], BF),
                   jax.ShapeDtypeStruct((L,) + _half_shape(name)[1:], F32)],
        compiler_params=_cp(("parallel", "parallel", "arbitrary")),
    )(sc, g, r)


def _sum_pieces(name, own, pieces, sc, tag):
    _, R, C, kind, rs, cs, rh = _geom(name)
    L = own.shape[0]
    tr = rh if kind == 'row' else 256
    nr = rh // tr

    def body(sc_ref, o_ref, p_ref, out_ref):
        out_ref[...] = o_ref[...] + p_ref[0].astype(F32) + p_ref[1].astype(F32) + p_ref[2].astype(F32)

    blk = (None, tr, cs)
    return pl.pallas_call(
        body, name=tag + "_sum_pieces_" + name,
        grid_spec=pltpu.PrefetchScalarGridSpec(
            num_scalar_prefetch=1, grid=(L, nr),
            in_specs=[pl.BlockSpec(blk, lambda l, ri, sc_ref: (l, ri, 0)),
                      pl.BlockSpec((3, None, tr, cs), lambda l, ri, sc_ref: (0, l, ri, 0))],
            out_specs=pl.BlockSpec(blk, lambda l, ri, sc_ref: (l, sc_ref[1] * nr + ri, 0))),
        out_shape=jax.ShapeDtypeStruct((L,) + _shard_shape(name)[1:], F32),
        compiler_params=_cp(("parallel", "parallel")),
    )(sc, own, pieces)


def _small_gather_start(v, sc):
    rows = v.shape[0]

    def place(sc_ref, v_ref, o_ref):
        o_ref[...] = v_ref[...]

    slots = pl.pallas_call(
        place, name="small_grads_place_own",
        grid_spec=pltpu.PrefetchScalarGridSpec(
            num_scalar_prefetch=1, grid=(1,),
            in_specs=[pl.BlockSpec((rows, 128), lambda i, sc_ref: (0, 0))],
            out_specs=pl.BlockSpec((None, rows, 128), lambda i, sc_ref: (2 * sc_ref[0] + sc_ref[1], 0, 0))),
        out_shape=jax.ShapeDtypeStruct((8, rows, 128), v.dtype),
        compiler_params=_cp(("arbitrary",)),
    )(sc, v)

    def peers():
        x, y, c, _ = _place()
        flips = [(fx, fy, fc) for fx in (0, 1) for fy in (0, 1) for fc in (0, 1)][1:]
        return [((1 - x if fx else x), (1 - y if fy else y), (1 - c if fc else c)) for fx, fy, fc in flips]

    def sends(refs):
        x, y, c, _ = _place()
        return [(refs[0], refs[1].at[4 * x + 2 * y + c], p) for p in peers()]

    def arrivals(refs):
        return [refs[1].at[4 * px + 2 * py + pc] for px, py, pc in peers()]

    send_sems, recv_sems, bufs, token = _split_start("small_grads_gather_start", [v, slots], 7, sends)
    return dict(sems=(send_sems, recv_sems), bufs=bufs, sends=sends, arrivals=arrivals, token=token)


def _small_gather_finish(st, after):
    return _split_wait("small_grads_gather_wait", *st['sems'], st['bufs'], after, st['sends'], st['arrivals'])[1]


def _sum8(v8, *, name, tr=336):
    rows = v8.shape[1]
    tr = min(tr, rows)
    assert rows % tr == 0

    def body(v_ref, o_ref):
        tot = v_ref[0].astype(F32)
        for d in range(1, 8):
            tot = tot + v_ref[d].astype(F32)
        o_ref[...] = tot

    return pl.pallas_call(
        body, name=name, grid=(rows // tr,),
        in_specs=[pl.BlockSpec((8, tr, 128), lambda i: (0, i, 0))], out_specs=pl.BlockSpec((tr, 128), lambda i: (i, 0)),
        out_shape=jax.ShapeDtypeStruct((rows, 128), F32),
        compiler_params=_cp(("parallel",)),
    )(v8)


def _block_diag(w_pool_l):
    wbd = jnp.zeros((MAIN_W, MAIN_W), F32)
    for gi in range(len(POOL_WINDOWS)):
        wbd = lax.dynamic_update_slice(wbd, w_pool_l[gi], (gi * POOL_GROUP, gi * POOL_GROUP))
    return wbd.astype(BF)


def _unpack_small(small_all):
    ng = small_all[:, :16, :].reshape(N_CHIPS, DEPTH, 4, 256).transpose(1, 2, 0, 3).reshape(DEPTH, 4, D_MODEL)
    ps = small_all[:, 16:18, :POOL_GROUP].transpose(1, 0, 2).reshape(N_A, MAIN_W)
    return ng, ps


def _local_step(x, mem, positions, on_forward, on_backward, mem_norm, w_pool, kv_norm, target):
    B, S, _ = x.shape
    T = B * S
    xc = x.reshape(T, D_MODEL)
    memf = mem.reshape(B * N_MEM, D_MODEL)
    tgt = target.reshape(T, D_MODEL)
    cos, sin = _rope_tables(positions.reshape(T, 1), name="rope_tables")
    wbd = [_block_diag(w_pool[l]) for l in range(N_A)]
    nbo = D_FF // 256
    fw = []
    rk = rv = None
    kv_saved = None
    wts = []
    norm_gains = pool_scale = y2 = None

    for l in range(DEPTH):
        t = f"l{l}_"
        got = on_forward('start', l, y2)
        wts.append(dict(got[0]))
        if l == 0:
            norm_gains, pool_scale = _unpack_small(got[1])
        sv = {'x_in': xc}
        h0 = _norm_fwd(xc, norm_gains[l, 0], name=t + "norm0", out_dtype=BF, tm=1024, after=got[2])
        z, = _mm(h0, wts[l]['w_in'], b_layer=0, name=t + "mm_in", tm=1024, tn=1024)
        memn = _norm_fwd(memf, mem_norm[l], name=t + "norm_mem", out_dtype=BF, tm=256)
        kvm, = _mm(memn, wts[l]['w_mem_kv'], b_layer=0, name=t + "mm_memkv", out_dtypes=(BF,))
        if l < N_A:
            ycat, sv['p'] = _pool_fwd(z, wbd[l], pool_scale[l], B, S, name=t + "pool_fwd")
        else:
            rq = _rope_apply(z, cos, sin, name=t + "rope_q", out_dtype=F32, tm=1024)
            o = lax.empty((T, MAIN_W), F32)
            lse = lax.empty((T, MAIN_W), F32)
            for g in range(3):
                o, lse = _dil_fwd(g, rq, rk, rv, o, lse, B, S, name=t + f"dil_fwd{g}")
            ycat = _dil_combine_fwd(o, lse, lax.empty((T, D_MODEL), BF), name=t + "dil_combine", tm=1024)
            sv.update(rq=rq, o=o, lse=lse)
        ycat, sv['lse_m'] = _memattn_fwd(z, kvm, ycat, B, S, name=t + "memattn_fwd")
        tok = on_forward('mid', l, ycat)
        y1, = _mm(ycat, wts[l]['w_out'], b_layer=0, name=t + "mm_out", tm=1024, tn=1024)
        wts[l].update(on_forward('ffn', l, y1)[0])
        x1 = _norm_fwd(y1, norm_gains[l, 1], name=t + "norm1", res=xc, after=tok)
        h2 = _norm_fwd(x1, norm_gains[l, 2], name=t + "norm2", out_dtype=BF, tm=1024)
        gg, uu, aa = _mm(h2, wts[l]['w_gate_up'], b_layer=0, b_offsets=(0, nbo), out_n=D_FF, tm=4096, tn=256, name=t + "mm_gate_up",
                         epilogue=_swiglu_fwd_epilogue, out_dtypes=(BF, BF, BF))
        on_forward('post', l, gg)
        y2, = _mm(aa, wts[l]['w_down'], b_layer=0, tk=D_FF, name=t + "mm_down")
        x2 = _norm_fwd(y2, norm_gains[l, 3], name=t + "norm3", res=x1)
        sv.update(h0=h0, z=z, memn=memn, kvm=kvm, ycat=ycat, y1=y1, x1=x1, h2=h2, gg=gg, uu=uu, aa=aa, y2=y2)
        fw.append(sv)
        xc = x2
        if l == N_A - 1:
            kvn = _norm_fwd(xc, kv_norm, name="norm_kv", out_dtype=BF)
            kv, = _mm(kvn, wts[N_A - 1]['w_kv'], b_layer=0, name="mm_kv")
            rk, rv = _rope_apply(kv, cos, sin, name="rope_k", passthrough=True, out_dtype=F32)
            kv_saved = (xc, kvn)

    loss, dx = _loss(xc, tgt, name="loss")

    d_ng = [[None] * 4 for _ in range(DEPTH)]
    d_memnorm = [None] * DEPTH
    d_wbd = [None] * N_A
    d_pscale = [None] * N_A
    d_kvnorm = None
    kv_parts = []
    tok = None

    def as3d(gl):
        return {nm: g.reshape((1,) + g.shape) for nm, g in gl.items()}

    for l in reversed(range(DEPTH)):
        t = f"l{l}_b_"
        sv = fw[l]
        gl = {}
        dy2, d_ng[l][3] = _norm_bwd(dx, sv['y2'], norm_gains[l, 3], name=t + "norm3", out_dtype=BF, tm=1024, after=tok)
        gl['w_down'], = _mm(sv['aa'], dy2, ta=True, tm=1408, tn=512, tk=4096, name=t + "dw_down")
        dg, du = _mm(dy2, wts[l]['w_down'], tb=True, b_layer=0, tm=1024, tn=1408, name=t + "d_act",
                     extras=((sv['gg'], 'tile'), (sv['uu'], 'tile')), epilogue=_swiglu_bwd_epilogue, out_dtypes=(BF, BF))
        gl['w_gate_up'], = _mm(sv['h2'], (dg, du), ta=True, tn=1408, tk=1024, name=t + "dw_gate_up")
        dh2, = _mm((dg, du), wts[l]['w_gate_up'], tb=True, b_layer=0, tn=1024, tk=1408, name=t + "d_h2", out_dtypes=(BF,))
        dx1, d_ng[l][2] = _norm_bwd(dh2, sv['x1'], norm_gains[l, 2], name=t + "norm2", add=dx, tm=1024)
        tok = on_backward('ffn', l, dx1, as3d(gl))
        dy1, d_ng[l][1] = _norm_bwd(dx1, sv['y1'], norm_gains[l, 1], name=t + "norm1", out_dtype=BF, tm=1024, after=tok)
        gl['w_out'], = _mm(sv['ycat'], dy1, ta=True, name=t + "dw_out", tk=4096)
        dycat, = _mm(dy1, wts[l]['w_out'], tb=True, b_layer=0, name=t + "d_ycat", tm=1024, tn=1024)
        dz = lax.empty((T, D_MODEL), BF)
        dz, dkm, dvm = _memattn_bwd(dycat, sv['z'], sv['kvm'], sv['lse_m'], dz, B, S, name=t + "memattn")
        if l < N_A:
            dz, d_wbd[l], d_pscale[l] = _pool_bwd(dycat, sv['p'], wbd[l], pool_scale[l], dz, B, S, name=t + "pool")
        else:
            do, cb = _dil_combine_bwd(dycat, sv['o'], sv['lse'], name=t + "dil_combine", tm=512)
            acc = tuple(lax.empty((T, MAIN_W), F32) for _ in range(3))
            for g in range(3):
                acc = _dil_bwd(g, sv['rq'], rk, rv, do, cb, sv['lse'], acc, B, S, name=t + f"dil{g}")
            dz = _rope_apply(acc[0], cos, sin, name=t + "rope_q", sign=-1.0, alias=dz, tm=1024)
            kv_parts.append(acc[1:])
        tok = on_backward('mix', l, dz, as3d(gl))
        gl['w_in'], = _mm(sv['h0'], dz, ta=True, name=t + "dw_in", tk=4096)
        dh0, = _mm(dz, wts[l]['w_in'], tb=True, b_layer=0, name=t + "d_h0", out_dtypes=(BF,), tm=1024, tn=1024)
        dx, d_ng[l][0] = _norm_bwd(dh0, sv['x_in'], norm_gains[l, 0], name=t + "norm0", add=dx1, tm=1024, after=tok)
        gl['w_mem_kv'], = _mm(sv['memn'], (dkm, dvm), ta=True, tn=256, name=t + "dw_memkv")
        dmemn, = _mm((dkm, dvm), wts[l]['w_mem_kv'], tb=True, b_layer=0, tk=256, name=t + "d_memn", out_dtypes=(BF,))
        _, d_memnorm[l] = _norm_bwd(dmemn, memf, mem_norm[l], name=t + "norm_mem", out_dtype=BF, tm=256)
        if l == N_A:
            dk, dv = _kv_grad_sum(kv_parts, cos, sin, name="kv_grad")
            x_kv, kvn = kv_saved
            gl['w_kv'], = _mm(kvn, (dk, dv), ta=True, tn=768, tk=2048, name="dw_kv")
            dkvn, = _mm((dk, dv), wts[N_A - 1]['w_kv'], tb=True, b_layer=0, tn=1024, tk=768, name="d_kvn", out_dtypes=(BF,))
            dx, d_kvnorm = _norm_bwd(dkvn, x_kv, kv_norm, name="norm_kv_b", add=dx)
        tok = on_backward('end', l, dx, as3d(gl))

    small = {
        'norm_gains': jnp.stack([jnp.concatenate(d_ng[l], axis=0) for l in range(DEPTH)]),
        'mem_norm': jnp.concatenate(d_memnorm, axis=0),
        'kv_norm': d_kvnorm.reshape(D_MODEL),
        'pool_scale': jnp.concatenate(d_pscale, axis=0),
        'w_pool': jnp.stack([jnp.stack([d_wbd[l][gi * POOL_GROUP:(gi + 1) * POOL_GROUP, gi * POOL_GROUP:(gi + 1) * POOL_GROUP]
                                        for gi in range(len(POOL_WINDOWS))]) for l in range(N_A)]),
    }
    return loss, dx, small


SMALL_ORDER = ('norm_gains', 'mem_norm', 'kv_norm', 'pool_scale', 'w_pool')
SMALL_VEC_ROWS = 2560


def kernel(x, mem, positions, norm_gains, mem_norm, w_in, w_mem_kv, w_out, w_pool, pool_scale, kv_norm, w_kv, w_gate_up, w_down, loss_target, m_norm_gains, m_mem_norm, m_w_in, m_w_mem_kv, m_w_out, m_w_pool, m_pool_scale, m_kv_norm, m_w_kv, m_w_gate_up, m_w_down, v_norm_gains, v_mem_norm, v_w_in, v_w_mem_kv, v_w_out, v_w_pool, v_pool_scale, v_kv_norm, v_w_kv, v_w_gate_up, v_w_down):
    xi, yi, ci = lax.axis_index("x"), lax.axis_index("y"), lax.axis_index("c")
    s = 2 * xi + yi
    sc = jnp.stack([s, ci]).astype(jnp.int32)
    weights = dict(norm_gains=norm_gains, mem_norm=mem_norm, w_in=w_in, w_mem_kv=w_mem_kv, w_out=w_out, w_pool=w_pool,
                   pool_scale=pool_scale, kv_norm=kv_norm, w_kv=w_kv, w_gate_up=w_gate_up, w_down=w_down)
    moms = dict(norm_gains=m_norm_gains, mem_norm=m_mem_norm, w_in=m_w_in, w_mem_kv=m_w_mem_kv, w_out=m_w_out,
                w_pool=m_w_pool, pool_scale=m_pool_scale, kv_norm=m_kv_norm, w_kv=m_w_kv, w_gate_up=m_w_gate_up,
                w_down=m_w_down)
    vels = dict(norm_gains=v_norm_gains, mem_norm=v_mem_norm, w_in=v_w_in, w_mem_kv=v_w_mem_kv, w_out=v_w_out,
                w_pool=v_w_pool, pool_scale=v_pool_scale, kv_norm=v_kv_norm, w_kv=v_w_kv, w_gate_up=v_w_gate_up,
                w_down=v_w_down)

    small_w = jnp.zeros((SMALL_ROWS, 256), F32)
    small_w = lax.dynamic_update_slice(small_w, norm_gains.reshape(16, 256), (0, 0))
    small_w = lax.dynamic_update_slice(small_w, pool_scale, (16, 0))
    def shard_of(nm, l):
        return (w_kv.reshape(_shard_shape('w_kv')), 0) if nm == 'w_kv' else (weights[nm], l)

    groups = {'l0a': (0, MIX_W), 'l0b': (0, FFN_W)}
    groups.update({f"l{l}": (l, LAYER_W + (('w_kv',) if l == N_A - 1 else ())) for l in range(1, DEPTH)})
    on_ici, on_d2d, gathered = {}, {}, {}

    def start_group(tag, after):
        l, names = groups[tag]
        on_ici[tag] = _gather_start(tag, names, [shard_of(nm, l) for nm in names], small_w if tag == 'l0a' else None, sc,
                                    after)
        return [on_ici[tag]['token']]

    def on_forward(where, l, after):
        if where == 'start':
            if l == 0:
                start_group('l0a', None)
                st = on_ici.pop('l0a')
                fwd = _gather_forward(st, st['token'])
                w, small_all = _gather_finish(fwd, fwd['token'])
                return w, small_all, start_group('l0b', w['w_in'])
            if f"l{l}" not in on_d2d:
                on_d2d[f"l{l}"] = _gather_forward(on_ici.pop(f"l{l}"), after)
            gathered[l] = _gather_finish(on_d2d.pop(f"l{l}"), after)[0]
            tok = start_group(f"l{l + 1}", gathered[l]['w_in']) if l + 1 < DEPTH else None
            return {nm: w for nm, w in gathered[l].items() if nm not in FFN_W}, None, tok
        if where == 'mid' and l == 0:
            on_d2d['l0b'] = _gather_forward(on_ici.pop('l0b'), after)
            return start_group('l1', on_d2d['l0b']['token'])
        if where == 'ffn':
            if l == 0:
                return (_gather_finish(on_d2d.pop('l0b'), after)[0],)
            return ({nm: gathered[l][nm] for nm in FFN_W},)
        if where == 'post' and 0 < l < DEPTH - 1:
            on_d2d[f"l{l + 1}"] = _gather_forward(on_ici.pop(f"l{l + 1}"), after)
        return None

    hook_of = {'ffn': 0, 'mix': 1, 'end': 2}
    active, reduced = [], {l: {} for l in range(DEPTH)}
    advance = {'mid': lambda st, after: _reduce_mid(st, after, sc), 'late': lambda st, after: _reduce_late(st, after, sc)}

    def run_hook(idx, after):
        toks = []
        for grp in list(active):
            while grp['plan'] and grp['plan'][0][1] <= idx:
                step = grp['plan'].pop(0)[0]
                if step == 'finish':
                    reduced[grp['layer']].update(_reduce_finish(grp['st'], after))
                    active.remove(grp)
                else:
                    grp['st'] = advance[step](grp['st'], after)
                    toks.append(grp['st']['token'])
        return toks

    def on_backward(where, l, after, grads):
        idx = 3 * (DEPTH - 1 - l) + hook_of[where]
        toks = run_hook(idx, after)
        if where == 'end' or (where == 'ffn' and l == 0):
            names = FFN_W if where == 'ffn' else tuple(nm for nm in grads if l > 0 or nm not in FFN_W)
            st = _reduce_start(f"l{l}_{where}_grads", names, {nm: grads[nm] for nm in names})
            plan = [('mid', idx + 1), ('late', idx + 3), ('finish', idx + 4)] if where == 'ffn' else \
                   [('mid', idx + 1), ('late', idx + 2), ('finish', idx + 3)]
            active.append(dict(layer=l, st=st, plan=plan))
            toks.append(st['token'])
        return toks

    loss, gx, gsmall = _local_step(x, mem, positions, on_forward, on_backward, mem_norm, w_pool, kv_norm, loss_target)
    loss = lax.psum(loss[0, 0], ("x", "y", "c"))

    vec = jnp.concatenate([gsmall[nm].reshape(-1) for nm in SMALL_ORDER])
    vec = jnp.pad(vec, (0, SMALL_VEC_ROWS * 128 - vec.shape[0])).reshape(SMALL_VEC_ROWS, 128)
    vec = vec + sum(grp['st']['token'][0, 0] for grp in active)
    small_st = _small_gather_start(vec.astype(BF), sc)
    outs = {nm: None for nm in LAYER_W}

    def adamw_layers(layers, names, after):
        for l in layers:
            for nm in names:
                outs[nm] = _adamw_layer(nm, l, weights[nm], reduced[l][nm], moms[nm], vels[nm], outs[nm], after)
                after = outs[nm][0]
        return after

    def zero_of(toks, st):
        return sum(toks) if toks else st['token']

    last = 3 * DEPTH
    toks = run_hook(last, small_st['token'])
    done = adamw_layers(range(DEPTH - 1, 0, -1), LAYER_W, zero_of(toks, small_st))
    toks = run_hook(last + 1, done)
    done = adamw_layers([0], FFN_W, zero_of(toks, small_st))
    tot = _sum8(_small_gather_finish(small_st, done), name="sum_small_grads", tr=512)
    run_hook(last + 2, tot)
    assert not active
    adamw_layers([0], MIX_W, None)
    tot = tot.reshape(-1)
    grads, off = {}, 0
    for nm in SMALL_ORDER:
        shape = (DEPTH, 4, D_MODEL) if nm == 'norm_gains' else (N_A, MAIN_W) if nm == 'pool_scale' else weights[nm].shape
        n = 1
        for dim in shape:
            n *= dim
        grads[nm] = tot[off:off + n].reshape(shape)
        off += n
    grads['norm_gains'] = lax.dynamic_slice(grads['norm_gains'], (0, 0, s * 256), (DEPTH, 4, 256))
    grads['pool_scale'] = lax.dynamic_slice(grads['pool_scale'], (0, s * POOL_GROUP), (N_A, POOL_GROUP))
    grads['w_kv'] = reduced[N_A]['w_kv'].reshape(w_kv.shape)

    order = ('norm_gains', 'mem_norm', 'w_in', 'w_mem_kv', 'w_out', 'w_pool', 'pool_scale', 'kv_norm', 'w_kv',
             'w_gate_up', 'w_down')
    deltas, new_m, new_v = {}, {}, {}
    for nm in order:
        if nm in LAYER_W:
            deltas[nm], new_m[nm], new_v[nm], grads[nm] = outs[nm]
        else:
            deltas[nm], new_m[nm], new_v[nm] = _adamw(weights[nm], grads[nm], moms[nm], vels[nm], name="adamw_" + nm)
    return (loss, gx.reshape(x.shape), *[grads[nm] for nm in order], *[deltas[nm] for nm in order],
            *[new_m[nm] for nm in order], *[new_v[nm] for nm in order])
```

```python
import jax
import jax.numpy as jnp
from jax import lax
from jax.experimental import pallas as pl
from jax.experimental.pallas import tpu as pltpu

F32 = jnp.float32
BF = jnp.bfloat16

D_MODEL = 1024
DEPTH = 4
N_A = 2
HEAD_DIM = 64
MEM_W = 256
MAIN_W = 768
D_FF = 2816
N_MEM = 256
POOL_WINDOWS = (2, 4, 8, 16)
POOL_GROUP = 192
DIL = (1, 4, 16)
STEPS = 128
ROPE_THETA = 10000.0
EPS = 1e-6
SCALE = HEAD_DIM ** -0.5
NEG = -1e30

ADAM_LR = 0.001
ADAM_B1 = 0.9
ADAM_B2 = 0.999
ADAM_EPS = 1e-08
ADAM_WD = 0.01
ADAM_STEP = 10

VMEM_LIMIT = 48 * 1024 * 1024
MESH = pl.DeviceIdType.MESH


def _cp(sem):
    return pltpu.CompilerParams(dimension_semantics=sem, vmem_limit_bytes=VMEM_LIMIT)


def _mm(a, b, *, name, ta=False, tb=False, tm=1024, tn=512, tk=1024, b_layer=None, b_offsets=(0,),
        extras=(), epilogue=None, out_dtypes=(F32,), out_n=None):
    a_pair = isinstance(a, (tuple, list))
    b_pair = isinstance(b, (tuple, list))
    a0 = a[0] if a_pair else a
    b0 = b[0] if b_pair else b
    a_rows, a_cols = a0.shape
    if a_pair:
        a_cols *= 2
    b_rows, b_cols = b0.shape[-2:]
    if b_pair:
        b_cols *= 2
    M, K = (a_cols, a_rows) if ta else (a_rows, a_cols)
    N = b_rows if tb else b_cols
    if out_n is not None:
        N = out_n
    tm, tn, tk = min(tm, M), min(tn, N), min(tk, K)
    assert M % tm == 0 and N % tn == 0 and K % tk == 0, (name, M, N, K, tm, tn, tk)
    nk = K // tk
    n_acc = len(b_offsets)

    if a_pair:
        a_half = (a0.shape[1] // (tm if ta else tk))
    if b_pair:
        b_half = (b0.shape[1] // (tk if tb else tn))

    def a_map(sel):
        def f(i, j, k):
            r, c = (k, i) if ta else (i, k)
            if a_pair:
                c = jnp.clip(c - sel * a_half, 0, a_half - 1)
            return (r, c)
        return f

    def b_map(sel, off):
        def f(i, j, k):
            r, c = (j + off, k) if tb else (k, j + off)
            if b_pair:
                c = jnp.clip(c - sel * b_half, 0, b_half - 1)
            if b_layer is not None:
                return (b_layer, r, c)
            return (r, c)
        return f

    a_blk = (tk, tm) if ta else (tm, tk)
    b_blk = (tn, tk) if tb else (tk, tn)
    if b_layer is not None:
        b_blk = (None,) + b_blk
    in_specs, operands = [], []
    for sel in range(2 if a_pair else 1):
        in_specs.append(pl.BlockSpec(a_blk, a_map(sel)))
        operands.append(a[sel] if a_pair else a)
    n_a = len(operands)
    for off in b_offsets:
        for sel in range(2 if b_pair else 1):
            in_specs.append(pl.BlockSpec(b_blk, b_map(sel, off)))
            operands.append(b[sel] if b_pair else b)
    n_b = len(operands) - n_a
    for arr, kind in extras:
        if kind == 'tile':
            in_specs.append(pl.BlockSpec((tm, tn), lambda i, j, k: (i, j)))
        elif kind == 'row':
            in_specs.append(pl.BlockSpec((tm, 1), lambda i, j, k: (i, 0)))
        else:
            in_specs.append(pl.BlockSpec((1, tn), lambda i, j, k: (0, j)))
        operands.append(arr)
    n_e = len(extras)
    n_o = len(out_dtypes)
    dims = (((0,) if ta else (1,), (1,) if tb else (0,)), ((), ()))

    def body(*refs):
        a_refs = refs[:n_a]
        b_refs = refs[n_a:n_a + n_b]
        e_refs = refs[n_a + n_b:n_a + n_b + n_e]
        n_in = n_a + n_b + n_e
        o_refs = refs[n_in:n_in + n_o]
        acc_refs = refs[n_in + n_o:]
        i, j, k = pl.program_id(0), pl.program_id(1), pl.program_id(2)
        if a_pair:
            cidx = i if ta else k
            av = jnp.where(cidx < a_half, a_refs[0][...], a_refs[1][...])
        else:
            av = a_refs[0][...]
        av = av.astype(BF)
        prods = []
        for q in range(n_acc):
            if b_pair:
                cidx = (k if tb else j) + b_offsets[q]
                bv = jnp.where(cidx < b_half, b_refs[2 * q][...], b_refs[2 * q + 1][...])
            else:
                bv = b_refs[q][...]
            prods.append(lax.dot_general(av, bv.astype(BF), dims, preferred_element_type=F32))

        def finish(accs):
            outs = epilogue(accs, *[r[...] for r in e_refs]) if epilogue is not None else accs
            for o_ref, o in zip(o_refs, outs):
                o_ref[...] = o.astype(o_ref.dtype)

        if nk == 1:
            finish(prods)
        else:
            @pl.when(k == 0)
            def _():
                for r, p in zip(acc_refs, prods):
                    r[...] = p

            @pl.when(k > 0)
            def _():
                for r, p in zip(acc_refs, prods):
                    r[...] += p

            @pl.when(k == nk - 1)
            def _():
                finish([r[...] for r in acc_refs])

    return pl.pallas_call(
        body, name=name,
        grid=(M // tm, N // tn, nk),
        in_specs=in_specs,
        out_specs=[pl.BlockSpec((tm, tn), lambda i, j, k: (i, j)) for _ in range(n_o)],
        out_shape=[jax.ShapeDtypeStruct((M, N), dt) for dt in out_dtypes],
        scratch_shapes=[pltpu.VMEM((tm, tn), F32) for _ in range(n_acc if nk > 1 else 0)],
        compiler_params=_cp(("parallel", "parallel", "arbitrary")),
    )(*operands)


def _norm_fwd(x, g, *, name, res=None, out_dtype=F32, tm=512, after=None):
    T, Dm = x.shape
    has_res = res is not None
    after = list(after or [])

    def body(*refs):
        refs = refs[:len(refs) - 1 - len(after)] + refs[len(refs) - 1:]
        if has_res:
            x_ref, g_ref, r_ref, y_ref = refs
        else:
            x_ref, g_ref, y_ref = refs
        xv = x_ref[...]
        rstd = lax.rsqrt(jnp.mean(xv * xv, axis=-1, keepdims=True) + EPS)
        y = xv * rstd * g_ref[...]
        if has_res:
            y = r_ref[...] + y
        y_ref[...] = y.astype(y_ref.dtype)

    row = pl.BlockSpec((tm, Dm), lambda i: (i, 0))
    in_specs = [row, pl.BlockSpec((1, Dm), lambda i: (0, 0))] + ([row] if has_res else [])
    in_specs += [pl.BlockSpec(memory_space=pl.ANY)] * len(after)
    ops = [x, g.reshape(1, Dm)] + ([res] if has_res else []) + after
    return pl.pallas_call(
        body, name=name, grid=(T // tm,), in_specs=in_specs,
        out_specs=row,
        out_shape=jax.ShapeDtypeStruct((T, Dm), out_dtype),
        compiler_params=_cp(("parallel",)),
    )(*ops)


def _norm_bwd(dout, x, g, *, name, add=None, out_dtype=F32, tm=512, after=None):
    T, Dm = x.shape
    has_add = add is not None
    nt = T // tm
    after = list(after or [])

    def body(*refs):
        refs = refs[:len(refs) - 3 - len(after)] + refs[len(refs) - 3:]
        if has_add:
            do_ref, x_ref, g_ref, a_ref, dx_ref, dg_ref, acc = refs
        else:
            do_ref, x_ref, g_ref, dx_ref, dg_ref, acc = refs
        i = pl.program_id(0)
        do = do_ref[...].astype(F32)
        xv = x_ref[...]
        rstd = lax.rsqrt(jnp.mean(xv * xv, axis=-1, keepdims=True) + EPS)
        xh = xv * rstd
        gd = do * g_ref[...]
        dx = rstd * (gd - xh * jnp.mean(gd * xh, axis=-1, keepdims=True))
        if has_add:
            dx = dx + a_ref[...].astype(F32)
        dx_ref[...] = dx.astype(dx_ref.dtype)
        part = jnp.sum((do * xh).reshape(tm // 8, 8, Dm), axis=0)

        @pl.when(i == 0)
        def _():
            acc[...] = part

        @pl.when(i > 0)
        def _():
            acc[...] += part

        @pl.when(i == nt - 1)
        def _():
            dg_ref[...] = jnp.sum(acc[...], axis=0, keepdims=True)

    row = pl.BlockSpec((tm, Dm), lambda i: (i, 0))
    in_specs = [row, row, pl.BlockSpec((1, Dm), lambda i: (0, 0))]
    ops = [dout, x, g.reshape(1, Dm)]
    if has_add:
        in_specs.append(row)
        ops.append(add)
    in_specs += [pl.BlockSpec(memory_space=pl.ANY)] * len(after)
    ops += after
    return pl.pallas_call(
        body, name=name, grid=(nt,), in_specs=in_specs,
        out_specs=[row, pl.BlockSpec((1, Dm), lambda i: (0, 0))],
        out_shape=[jax.ShapeDtypeStruct((T, Dm), out_dtype), jax.ShapeDtypeStruct((1, Dm), F32)],
        scratch_shapes=[pltpu.VMEM((8, Dm), F32)],
        compiler_params=_cp(("arbitrary",)),
    )(*ops)


def _swiglu_fwd_epilogue(accs):
    g, u = accs
    return g, u, g * jax.nn.sigmoid(g) * u


def _swiglu_bwd_epilogue(accs, g, u):
    da = accs[0]
    g = g.astype(F32)
    u = u.astype(F32)
    sig = jax.nn.sigmoid(g)
    return da * u * (sig * (1.0 + g * (1.0 - sig))), da * (g * sig)


def _rope_tables(pos, *, name, tm=1024):
    T = pos.shape[0]
    half = HEAD_DIM // 2
    freqs = ROPE_THETA ** (-jnp.arange(half, dtype=F32) / half)
    freqs = jnp.tile(freqs, 4).reshape(1, 128)

    def body(p_ref, f_ref, c_ref, s_ref):
        ang = p_ref[...].astype(F32) * f_ref[...]
        lane = lax.broadcasted_iota(jnp.int32, ang.shape, 1)
        c_ref[...] = jnp.cos(ang)
        s_ref[...] = jnp.where(lane % HEAD_DIM < half, -1.0, 1.0) * jnp.sin(ang)

    tab = pl.BlockSpec((tm, 128), lambda i: (i, 0))
    return pl.pallas_call(
        body, name=name, grid=(T // tm,),
        in_specs=[pl.BlockSpec((tm, 1), lambda i: (i, 0)), pl.BlockSpec((1, 128), lambda i: (0, 0))],
        out_specs=[tab, tab],
        out_shape=[jax.ShapeDtypeStruct((T, 128), F32)] * 2,
        compiler_params=_cp(("parallel",)),
    )(pos, freqs)


def _rot(x, cos, sin, sign):
    W = x.shape[1]
    half = HEAD_DIM // 2
    reps = W // 128
    c = jnp.concatenate([cos] * reps, axis=1) if reps > 1 else cos
    s = jnp.concatenate([sin] * reps, axis=1) if reps > 1 else sin
    lane = lax.broadcasted_iota(jnp.int32, x.shape, 1)
    swapped = jnp.where(lane % HEAD_DIM < half, pltpu.roll(x, W - half, axis=1), pltpu.roll(x, half, axis=1))
    return x * c + (sign * s) * swapped


def _rope_apply(x, cos, sin, *, name, sign=1.0, width=MAIN_W, passthrough=False, out_dtype=BF, alias=None,
                out_cols=None, tm=512):
    T = x.shape[0]

    def body(*refs):
        if passthrough:
            x_ref, v_ref, c_ref, s_ref, o_ref, ov_ref = refs
            ov_ref[...] = v_ref[...].astype(ov_ref.dtype)
        elif alias is not None:
            x_ref, c_ref, s_ref, _, o_ref = refs
        else:
            x_ref, c_ref, s_ref, o_ref = refs
        o_ref[...] = _rot(x_ref[...].astype(F32), c_ref[...], s_ref[...], sign).astype(o_ref.dtype)

    blk0 = pl.BlockSpec((tm, width), lambda i: (i, 0))
    blk1 = pl.BlockSpec((tm, width), lambda i: (i, 1))
    tab = pl.BlockSpec((tm, 128), lambda i: (i, 0))
    if passthrough:
        return pl.pallas_call(
            body, name=name, grid=(T // tm,), in_specs=[blk0, blk1, tab, tab], out_specs=[blk0, blk0],
            out_shape=[jax.ShapeDtypeStruct((T, width), out_dtype)] * 2,
            compiler_params=_cp(("parallel",)),
        )(x, x, cos, sin)
    if alias is not None:
        return pl.pallas_call(
            body, name=name, grid=(T // tm,),
            in_specs=[blk0, tab, tab, pl.BlockSpec(memory_space=pl.ANY)], out_specs=blk0,
            out_shape=jax.ShapeDtypeStruct(alias.shape, alias.dtype),
            input_output_aliases={3: 0},
            compiler_params=_cp(("parallel",)),
        )(x, cos, sin, alias)
    return pl.pallas_call(
        body, name=name, grid=(T // tm,), in_specs=[blk0, tab, tab], out_specs=blk0,
        out_shape=jax.ShapeDtypeStruct((T, width), out_dtype),
        compiler_params=_cp(("parallel",)),
    )(x, cos, sin)


POOL_T = 256
POOL_HALO = 16


def _pool_lane_window(shape):
    lane = lax.broadcasted_iota(jnp.int32, shape, 1)
    w = jnp.full(shape, POOL_WINDOWS[0], jnp.int32)
    for gi in range(1, len(POOL_WINDOWS)):
        w = jnp.where(lane >= gi * POOL_GROUP, POOL_WINDOWS[gi], w)
    return w


POOL_PAD = 32
POOL_R = POOL_T + POOL_PAD


def _pool_window_sums(buf, tmp_a, tmp_b, win, back):
    src, dst, acc = buf, tmp_a, None
    for j, (w, sh) in enumerate(zip(POOL_WINDOWS, (1, 2, 4, 8)), start=1):
        n = POOL_R - 8 * j
        if back:
            dst[pl.ds(8 * j, n), :] = src[pl.ds(8 * j, n), :] + src[pl.ds(8 * j - sh, n), :]
            cur = dst[pl.ds(POOL_PAD, POOL_T), :]
        else:
            dst[pl.ds(0, n), :] = src[pl.ds(0, n), :] + src[pl.ds(sh, n), :]
            cur = dst[pl.ds(0, POOL_T), :]
        acc = cur if acc is None else jnp.where(win >= w, cur, acc)
        src, dst = dst, (tmp_b if dst is tmp_a else tmp_a)
    return acc


def _pool_fwd(z, wbd, scale, B, S, *, name):
    T = z.shape[0]
    nt = S // POOL_T
    hb = POOL_T // POOL_PAD

    def body(z_ref, h_ref, w_ref, sc_ref, y_ref, p_ref, ext, tmp_a, tmp_b):
        i = pl.program_id(1)
        u = z_ref[...]
        ext[pl.ds(POOL_PAD, POOL_T), :] = u
        ext[pl.ds(0, POOL_PAD), :] = jnp.where(i > 0, h_ref[...], 0.0)
        win = _pool_lane_window((POOL_T, MAIN_W))
        acc = _pool_window_sums(ext, tmp_a, tmp_b, win, True)
        t = i * POOL_T + lax.broadcasted_iota(jnp.int32, (POOL_T, MAIN_W), 0)
        cnt = jnp.minimum(t + 1, win).astype(F32)
        p = (acc / cnt - u).astype(BF)
        p_ref[...] = p
        y = jnp.dot(p, w_ref[...], preferred_element_type=F32) * sc_ref[...]
        y_ref[...] = y.astype(y_ref.dtype)

    return pl.pallas_call(
        body, name=name, grid=(B, nt),
        in_specs=[pl.BlockSpec((POOL_T, MAIN_W), lambda b, i: (b * nt + i, 0)),
                  pl.BlockSpec((POOL_PAD, MAIN_W), lambda b, i: (jnp.maximum((b * nt + i) * hb - 1, 0), 0)),
                  pl.BlockSpec((MAIN_W, MAIN_W), lambda b, i: (0, 0)),
                  pl.BlockSpec((1, MAIN_W), lambda b, i: (0, 0))],
        out_specs=[pl.BlockSpec((POOL_T, MAIN_W), lambda b, i: (b * nt + i, 0)),
                   pl.BlockSpec((POOL_T, MAIN_W), lambda b, i: (b * nt + i, 0))],
        out_shape=[jax.ShapeDtypeStruct((T, D_MODEL), BF), jax.ShapeDtypeStruct((T, MAIN_W), BF)],
        scratch_shapes=[pltpu.VMEM((POOL_R, MAIN_W), F32)] * 3,
        compiler_params=_cp(("parallel", "parallel")),
    )(z, z, wbd, scale.reshape(1, MAIN_W))


def _pool_bwd(dy, p, wbd, scale, dz_alias, B, S, *, name):
    T = dy.shape[0]
    nt = S // POOL_T
    hb = POOL_T // POOL_HALO
    last_halo = T // POOL_HALO - 1

    def body(dy_ref, dyn_ref, p_ref, pn_ref, w_ref, sc_ref, _, dz_ref, dw_ref, ds_ref, ext, tmp_a, tmp_b, dw_acc, ds_acc):
        b, i = pl.program_id(0), pl.program_id(1)
        first = jnp.logical_and(b == 0, i == 0)
        dyv = dy_ref[...]
        pv = p_ref[...]
        sc = sc_ref[...]
        w = w_ref[...]
        pw = jnp.dot(pv, w, preferred_element_type=F32)
        ds_part = jnp.sum((dyv * pw).reshape(POOL_T // 8, 8, MAIN_W), axis=0)
        dpw = (dyv * sc).astype(BF)
        dw_part = lax.dot_general(pv, dpw, (((0,), (0,)), ((), ())), preferred_element_type=F32)

        @pl.when(first)
        def _():
            dw_acc[...] = dw_part
            ds_acc[...] = ds_part

        @pl.when(jnp.logical_not(first))
        def _():
            dw_acc[...] += dw_part
            ds_acc[...] += ds_part

        @pl.when(jnp.logical_and(b == pl.num_programs(0) - 1, i == nt - 1))
        def _():
            dw_ref[...] = dw_acc[...]
            ds_ref[...] = jnp.sum(ds_acc[...], axis=0, keepdims=True)

        dp = lax.dot_general(dpw, w, (((1,), (1,)), ((), ())), preferred_element_type=F32)
        dpn = lax.dot_general((dyn_ref[...] * sc).astype(BF), w, (((1,), (1,)), ((), ())), preferred_element_type=F32)
        win = _pool_lane_window((POOL_T, MAIN_W))
        win_n = _pool_lane_window((POOL_HALO, MAIN_W))
        t = i * POOL_T + lax.broadcasted_iota(jnp.int32, (POOL_T, MAIN_W), 0)
        tn = (i + 1) * POOL_T + lax.broadcasted_iota(jnp.int32, (POOL_HALO, MAIN_W), 0)
        ext[pl.ds(0, POOL_T), :] = dp / jnp.minimum(t + 1, win).astype(F32)
        ext[pl.ds(POOL_T, POOL_HALO), :] = jnp.where(i < nt - 1, dpn / jnp.minimum(tn + 1, win_n).astype(F32), 0.0)
        ext[pl.ds(POOL_T + POOL_HALO, POOL_PAD - POOL_HALO), :] = jnp.zeros((POOL_PAD - POOL_HALO, MAIN_W), F32)
        acc = _pool_window_sums(ext, tmp_a, tmp_b, win, False) - dp
        dz_ref[...] = acc.astype(dz_ref.dtype)

    cur = lambda b, i: (b * nt + i, 0)
    nxt = lambda b, i: (jnp.minimum((b * nt + i + 1) * hb, last_halo), 0)
    return pl.pallas_call(
        body, name=name, grid=(B, nt),
        in_specs=[pl.BlockSpec((POOL_T, MAIN_W), cur), pl.BlockSpec((POOL_HALO, MAIN_W), nxt),
                  pl.BlockSpec((POOL_T, MAIN_W), cur), pl.BlockSpec((POOL_HALO, MAIN_W), nxt),
                  pl.BlockSpec((MAIN_W, MAIN_W), lambda b, i: (0, 0)),
                  pl.BlockSpec((1, MAIN_W), lambda b, i: (0, 0)),
                  pl.BlockSpec(memory_space=pl.ANY)],
        out_specs=[pl.BlockSpec((POOL_T, MAIN_W), cur),
                   pl.BlockSpec((MAIN_W, MAIN_W), lambda b, i: (0, 0)),
                   pl.BlockSpec((1, MAIN_W), lambda b, i: (0, 0))],
        out_shape=[jax.ShapeDtypeStruct(dz_alias.shape, dz_alias.dtype),
                   jax.ShapeDtypeStruct((MAIN_W, MAIN_W), F32), jax.ShapeDtypeStruct((1, MAIN_W), F32)],
        scratch_shapes=[pltpu.VMEM((POOL_R, MAIN_W), F32)] * 3 + [pltpu.VMEM((MAIN_W, MAIN_W), F32), pltpu.VMEM((8, MAIN_W), F32)],
        input_output_aliases={6: 0},
        compiler_params=_cp(("arbitrary", "arbitrary")),
    )(dy, dy, p, p, wbd, scale.reshape(1, MAIN_W), dz_alias)


def _head_masks(shape):
    lane = lax.broadcasted_iota(jnp.int32, shape, 1)
    return [(lane // HEAD_DIM) == h for h in range(shape[1] // HEAD_DIM)]


def _row_of(bcast, mask):
    return jnp.max(jnp.where(mask, bcast, -jnp.inf), axis=-1, keepdims=True)


MEM_TQ = 2048


def _memattn_fwd(z, kv, y_alias, B, S, *, name, tq=MEM_TQ):
    T = z.shape[0]
    nt = S // tq

    def body(q_ref, k_ref, v_ref, _, y_ref, l_ref):
        q = q_ref[...]
        k = k_ref[...]
        v = v_ref[...]
        masks = _head_masks(q.shape)
        o = jnp.zeros(q.shape, F32)
        lse_b = jnp.zeros(q.shape, F32)
        for m in masks:
            qm = jnp.where(m, q, 0.0).astype(BF)
            s = lax.dot_general(qm, k, (((1,), (1,)), ((), ())), preferred_element_type=F32) * SCALE
            mx = jnp.max(s, axis=-1, keepdims=True)
            e = jnp.exp(s - mx)
            l = jnp.sum(e, axis=-1, keepdims=True)
            p = (e / l).astype(BF)
            o = o + jnp.where(m, jnp.dot(p, v, preferred_element_type=F32), 0.0)
            lse_b = lse_b + jnp.where(m, mx + jnp.log(l), 0.0)
        y_ref[...] = o.astype(y_ref.dtype)
        l_ref[...] = lse_b

    qblk = pl.BlockSpec((tq, MEM_W), lambda b, i: (b * nt + i, 3))
    return pl.pallas_call(
        body, name=name, grid=(B, nt),
        in_specs=[qblk, pl.BlockSpec((N_MEM, MEM_W), lambda b, i: (b, 0)), pl.BlockSpec((N_MEM, MEM_W), lambda b, i: (b, 1)),
                  pl.BlockSpec(memory_space=pl.ANY)],
        out_specs=[qblk, pl.BlockSpec((tq, MEM_W), lambda b, i: (b * nt + i, 0))],
        out_shape=[jax.ShapeDtypeStruct(y_alias.shape, y_alias.dtype), jax.ShapeDtypeStruct((T, MEM_W), F32)],
        input_output_aliases={3: 0},
        compiler_params=_cp(("parallel", "parallel")),
    )(z, kv, kv, y_alias)


def _memattn_bwd(dy, z, kv, lse, dz_alias, B, S, *, name, tq=MEM_TQ):
    nt = S // tq

    def body(do_ref, q_ref, k_ref, v_ref, l_ref, _, dz_ref, dk_ref, dv_ref, dk_acc, dv_acc):
        i = pl.program_id(1)
        do = do_ref[...]
        q = q_ref[...]
        k = k_ref[...]
        v = v_ref[...]
        lse_b = l_ref[...]
        masks = _head_masks(q.shape)
        dq = jnp.zeros(q.shape, F32)
        dk = jnp.zeros(k.shape, F32)
        dv = jnp.zeros(v.shape, F32)
        for m in masks:
            qm = jnp.where(m, q, 0.0).astype(BF)
            dom = jnp.where(m, do, 0.0).astype(BF)
            s = lax.dot_general(qm, k, (((1,), (1,)), ((), ())), preferred_element_type=F32) * SCALE
            p = jnp.exp(s - _row_of(lse_b, m))
            dp = lax.dot_general(dom, v, (((1,), (1,)), ((), ())), preferred_element_type=F32)
            delta = jnp.sum(p * dp, axis=-1, keepdims=True)
            ds = (p * (dp - delta) * SCALE).astype(BF)
            pb = p.astype(BF)
            dv = dv + jnp.where(m[:N_MEM], lax.dot_general(pb, dom, (((0,), (0,)), ((), ())), preferred_element_type=F32), 0.0)
            dk = dk + jnp.where(m[:N_MEM], lax.dot_general(ds, qm, (((0,), (0,)), ((), ())), preferred_element_type=F32), 0.0)
            dq = dq + jnp.where(m, jnp.dot(ds, k, preferred_element_type=F32), 0.0)
        dz_ref[...] = dq.astype(dz_ref.dtype)

        @pl.when(i == 0)
        def _():
            dk_acc[...] = dk
            dv_acc[...] = dv

        @pl.when(i > 0)
        def _():
            dk_acc[...] += dk
            dv_acc[...] += dv

        @pl.when(i == nt - 1)
        def _():
            dk_ref[...] = dk_acc[...]
            dv_ref[...] = dv_acc[...]

    qblk = pl.BlockSpec((tq, MEM_W), lambda b, i: (b * nt + i, 3))
    kblk = pl.BlockSpec((N_MEM, MEM_W), lambda b, i: (b, 0))
    return pl.pallas_call(
        body, name=name, grid=(B, nt),
        in_specs=[qblk, qblk, kblk, pl.BlockSpec((N_MEM, MEM_W), lambda b, i: (b, 1)),
                  pl.BlockSpec((tq, MEM_W), lambda b, i: (b * nt + i, 0)), pl.BlockSpec(memory_space=pl.ANY)],
        out_specs=[qblk, kblk, kblk],
        out_shape=[jax.ShapeDtypeStruct(dz_alias.shape, dz_alias.dtype),
                   jax.ShapeDtypeStruct((B * N_MEM, MEM_W), F32), jax.ShapeDtypeStruct((B * N_MEM, MEM_W), F32)],
        scratch_shapes=[pltpu.VMEM((N_MEM, MEM_W), F32), pltpu.VMEM((N_MEM, MEM_W), F32)],
        input_output_aliases={5: 0},
        compiler_params=_cp(("parallel", "arbitrary")),
    )(dy, z, kv, kv, lse, dz_alias)


N_UNITS = 16


def _unit_rows(g):
    d = DIL[g]
    nb = N_UNITS // d
    return [pl.ds(n * STEPS * d + r, STEPS, stride=d) if d > 1 else pl.ds(n * STEPS, STEPS)
            for n in range(nb) for r in range(d)]


def _load_units(ref, g):
    if DIL[g] == 1:
        return ref[...].reshape(N_UNITS, STEPS, 128)
    return jnp.stack([ref[rows, :] for rows in _unit_rows(g)])


def _store_units(ref, val, g):
    if DIL[g] == 1:
        ref[...] = val.reshape(N_UNITS * STEPS, 128)
    else:
        for u, rows in enumerate(_unit_rows(g)):
            ref[rows, :] = val[u]


def _shift_units(x, by):
    z = jnp.zeros((abs(by),) + x.shape[1:], x.dtype)
    return jnp.concatenate([z, x[:N_UNITS - by]], axis=0) if by > 0 else jnp.concatenate([x[-by:], z], axis=0)


def _bdot(a, b, ca, cb):
    return lax.dot_general(a, b, (((ca,), (cb,)), ((0,), (0,))), preferred_element_type=F32)


def _dil_masks(g):
    d = DIL[g]
    has_prev = N_UNITS // d > 1
    qi = lax.broadcasted_iota(jnp.int32, (1, STEPS, STEPS), 1)
    kj = lax.broadcasted_iota(jnp.int32, (1, STEPS, STEPS), 2)
    unit = lax.broadcasted_iota(jnp.int32, (N_UNITS, 1, 1), 0)
    cur = kj <= qi
    prev = jnp.logical_and(kj >= qi, unit >= d) if has_prev else None
    lane = lax.broadcasted_iota(jnp.int32, (1, 1, 128), 2)
    heads = [(lane // HEAD_DIM) == h for h in range(128 // HEAD_DIM)]
    return has_prev, cur, prev, heads


def _dil_fwd(g, q, k, v, o_alias, l_alias, B, S, *, name):
    assert S == N_UNITS * STEPS
    d = DIL[g]

    def body(q_ref, k_ref, v_ref, _, __, o_ref, l_ref):
        has_prev, cur, prev, heads = _dil_masks(g)
        q = _load_units(q_ref, g)
        kc = _load_units(k_ref, g).astype(BF)
        vc = _load_units(v_ref, g).astype(BF)
        if has_prev:
            kp, vp = _shift_units(kc, d), _shift_units(vc, d)
        o = jnp.zeros(q.shape, F32)
        lse_b = jnp.zeros(q.shape, F32)
        for m in heads:
            qm = jnp.where(m, q, 0.0).astype(BF)
            sc = jnp.where(cur, _bdot(qm, kc, 2, 2) * SCALE, NEG)
            mx = jnp.max(sc, axis=-1, keepdims=True)
            if has_prev:
                sp = jnp.where(prev, _bdot(qm, kp, 2, 2) * SCALE, NEG)
                mx = jnp.maximum(mx, jnp.max(sp, axis=-1, keepdims=True))
            l = jnp.sum(jnp.exp(sc - mx), axis=-1, keepdims=True)
            if has_prev:
                l = l + jnp.sum(jnp.exp(sp - mx), axis=-1, keepdims=True)
            lse = mx + jnp.log(l)
            oh = _bdot(jnp.exp(sc - lse).astype(BF), vc, 2, 1)
            if has_prev:
                oh = oh + _bdot(jnp.exp(sp - lse).astype(BF), vp, 2, 1)
            o = o + jnp.where(m, oh, 0.0)
            lse_b = lse_b + jnp.where(m, lse, 0.0)
        _store_units(o_ref, o, g)
        _store_units(l_ref, lse_b, g)

    blk = pl.BlockSpec((S, 128), lambda b, hf: (b, g * 2 + hf))
    anyspec = pl.BlockSpec(memory_space=pl.ANY)
    o, l = pl.pallas_call(
        body, name=name, grid=(B, 2),
        in_specs=[blk, blk, blk, anyspec, anyspec], out_specs=[blk, blk],
        out_shape=[jax.ShapeDtypeStruct(q.shape, F32)] * 2,
        input_output_aliases={3: 0, 4: 1},
        compiler_params=_cp(("parallel", "parallel")),
    )(q, k, v, o_alias, l_alias)
    return o, l


def _dil_bwd(g, q, k, v, do, cb, lse, aliases, B, S, *, name):
    assert S == N_UNITS * STEPS
    d = DIL[g]

    def body(q_ref, k_ref, v_ref, do_ref, c_ref, l_ref, _, __, ___, dq_ref, dk_ref, dv_ref):
        has_prev, cur, prev, heads = _dil_masks(g)
        q = _load_units(q_ref, g)
        kc = _load_units(k_ref, g).astype(BF)
        vc = _load_units(v_ref, g).astype(BF)
        do = _load_units(do_ref, g)
        cbv = _load_units(c_ref, g)
        lse_b = _load_units(l_ref, g)
        if has_prev:
            kp, vp = _shift_units(kc, d), _shift_units(vc, d)
        z = jnp.zeros(q.shape, F32)
        dq, dkc, dkp, dvc, dvp = z, z, z, z, z
        for m in heads:
            qm = jnp.where(m, q, 0.0).astype(BF)
            dom = jnp.where(m, do, 0.0).astype(BF)
            lse = jnp.max(jnp.where(m, lse_b, -jnp.inf), axis=-1, keepdims=True)
            c = jnp.max(jnp.where(m, cbv, -jnp.inf), axis=-1, keepdims=True)
            sc = jnp.where(cur, _bdot(qm, kc, 2, 2) * SCALE, NEG)
            pc = jnp.exp(sc - lse)
            dsc = (pc * (_bdot(dom, vc, 2, 2) + c) * SCALE).astype(BF)
            dqh = _bdot(dsc, kc, 2, 1)
            dkc = dkc + jnp.where(m, _bdot(dsc, qm, 1, 1), 0.0)
            dvc = dvc + jnp.where(m, _bdot(pc.astype(BF), dom, 1, 1), 0.0)
            if has_prev:
                sp = jnp.where(prev, _bdot(qm, kp, 2, 2) * SCALE, NEG)
                pp = jnp.exp(sp - lse)
                dsp = (pp * (_bdot(dom, vp, 2, 2) + c) * SCALE).astype(BF)
                dqh = dqh + _bdot(dsp, kp, 2, 1)
                dkp = dkp + jnp.where(m, _bdot(dsp, qm, 1, 1), 0.0)
                dvp = dvp + jnp.where(m, _bdot(pp.astype(BF), dom, 1, 1), 0.0)
            dq = dq + jnp.where(m, dqh, 0.0)
        if has_prev:
            dkc = dkc + _shift_units(dkp, -d)
            dvc = dvc + _shift_units(dvp, -d)
        _store_units(dq_ref, dq, g)
        _store_units(dk_ref, dkc, g)
        _store_units(dv_ref, dvc, g)

    blk = pl.BlockSpec((S, 128), lambda b, hf: (b, g * 2 + hf))
    anyspec = pl.BlockSpec(memory_space=pl.ANY)
    return tuple(pl.pallas_call(
        body, name=name, grid=(B, 2),
        in_specs=[blk] * 6 + [anyspec] * 3, out_specs=[blk] * 3,
        out_shape=[jax.ShapeDtypeStruct(q.shape, F32)] * 3,
        input_output_aliases={6: 0, 7: 1, 8: 2},
        compiler_params=_cp(("parallel", "parallel")),
    )(q, k, v, do, cb, lse, *aliases))


def _kv_grad_sum(parts, cos, sin, *, name, tm=512):
    T = parts[0][0].shape[0]
    n_l = len(parts)

    def body(*refs):
        c_ref, s_ref = refs[0], refs[1]
        dk_ref, dv_ref = refs[2 + 2 * n_l:]
        dk = refs[2][...]
        dv = refs[3][...]
        for li in range(1, n_l):
            dk = dk + refs[2 + 2 * li][...]
            dv = dv + refs[3 + 2 * li][...]
        dk_ref[...] = _rot(dk, c_ref[...], s_ref[...], -1.0).astype(dk_ref.dtype)
        dv_ref[...] = dv.astype(dv_ref.dtype)

    full = pl.BlockSpec((tm, MAIN_W), lambda i: (i, 0))
    tab = pl.BlockSpec((tm, 128), lambda i: (i, 0))
    ops = [cos, sin] + [t for part in parts for t in part]
    return pl.pallas_call(
        body, name=name, grid=(T // tm,), in_specs=[tab, tab] + [full] * (2 * n_l), out_specs=[full, full],
        out_shape=[jax.ShapeDtypeStruct((T, MAIN_W), BF)] * 2,
        compiler_params=_cp(("parallel",)),
    )(*ops)


def _group_softmax(lse):
    l0, l1, l2 = lse[:, 0:256], lse[:, 256:512], lse[:, 512:768]
    mx = jnp.maximum(jnp.maximum(l0, l1), l2)
    e0, e1, e2 = jnp.exp(l0 - mx), jnp.exp(l1 - mx), jnp.exp(l2 - mx)
    tot = e0 + e1 + e2
    return e0 / tot, e1 / tot, e2 / tot


def _dil_combine_fwd(o, lse, y_alias, *, name, tm=512):
    T = o.shape[0]

    def body(o_ref, l_ref, _, y_ref):
        a = jnp.concatenate(_group_softmax(l_ref[...]), axis=1)
        y_ref[...] = (o_ref[...] * a).astype(y_ref.dtype)

    blk = pl.BlockSpec((tm, MAIN_W), lambda i: (i, 0))
    return pl.pallas_call(
        body, name=name, grid=(T // tm,), in_specs=[blk, blk, pl.BlockSpec(memory_space=pl.ANY)], out_specs=blk,
        out_shape=jax.ShapeDtypeStruct(y_alias.shape, y_alias.dtype), input_output_aliases={2: 0},
        compiler_params=_cp(("parallel",)),
    )(o, lse, y_alias)


def _dil_combine_bwd(dy, o, lse, *, name, tm=256):
    T = o.shape[0]
    lane_r = lax.broadcasted_iota(jnp.int32, (256, 256), 0) // HEAD_DIM
    lane_c = lax.broadcasted_iota(jnp.int32, (256, 256), 1) // HEAD_DIM
    ones_bd = (lane_r == lane_c).astype(BF)

    def body(dy_ref, o_ref, l_ref, e_ref, do_ref, c_ref):
        dyv = dy_ref[...]
        alphas = _group_softmax(l_ref[...])
        prod = dyv * o_ref[...]
        e = e_ref[...]
        tot = jnp.zeros((tm, 256), F32)
        for gi in range(3):
            x = prod[:, gi * 256:(gi + 1) * 256]
            hi = x.astype(BF)
            lo = (x - hi.astype(F32)).astype(BF)
            dalpha = jnp.dot(hi, e, preferred_element_type=F32) + jnp.dot(lo, e, preferred_element_type=F32)
            tot = tot + alphas[gi] * dalpha
        a = jnp.concatenate(alphas, axis=1)
        do_ref[...] = (dyv * a).astype(do_ref.dtype)
        c_ref[...] = jnp.concatenate([-al * tot for al in alphas], axis=1)

    blk = pl.BlockSpec((tm, MAIN_W), lambda i: (i, 0))
    return pl.pallas_call(
        body, name=name, grid=(T // tm,),
        in_specs=[blk, blk, blk, pl.BlockSpec((256, 256), lambda i: (0, 0))], out_specs=[blk, blk],
        out_shape=[jax.ShapeDtypeStruct((T, MAIN_W), F32), jax.ShapeDtypeStruct((T, MAIN_W), F32)],
        compiler_params=_cp(("parallel",)),
    )(dy, o, lse, ones_bd)


def _loss(y, target, *, name, tm=512):
    T, Dm = y.shape
    nt = T // tm

    def body(y_ref, t_ref, l_ref, d_ref, acc):
        i = pl.program_id(0)
        err = y_ref[...] - t_ref[...]
        d_ref[...] = err / Dm
        part = jnp.sum(jnp.mean(err * err, axis=-1, keepdims=True).reshape(tm // 8, 8, 1), axis=0)

        @pl.when(i == 0)
        def _():
            acc[...] = part

        @pl.when(i > 0)
        def _():
            acc[...] += part

        @pl.when(i == nt - 1)
        def _():
            l_ref[...] = 0.5 * jnp.sum(acc[...], axis=0, keepdims=True)

    row = pl.BlockSpec((tm, Dm), lambda i: (i, 0))
    return pl.pallas_call(
        body, name=name, grid=(nt,), in_specs=[row, row],
        out_specs=[pl.BlockSpec((1, 1), lambda i: (0, 0)), row],
        out_shape=[jax.ShapeDtypeStruct((1, 1), F32), jax.ShapeDtypeStruct((T, Dm), F32)],
        scratch_shapes=[pltpu.VMEM((8, 1), F32)],
        compiler_params=_cp(("arbitrary",)),
    )(y, target)


def _adamw(w, g, m, v, *, name):
    shape = w.shape
    cols = shape[-1]
    rows = w.size // cols
    tm = rows
    for cand in (512, 352, 256, 128):
        if rows > cand and rows % cand == 0 and cand * cols * 4 <= (1 << 20):
            tm = cand
            break

    def body(w_ref, g_ref, m_ref, v_ref, d_ref, mo_ref, vo_ref):
        gv = g_ref[...]
        mn = ADAM_B1 * m_ref[...] + (1.0 - ADAM_B1) * gv
        vn = ADAM_B2 * v_ref[...] + (1.0 - ADAM_B2) * (gv * gv)
        m_hat = mn / (1.0 - ADAM_B1 ** ADAM_STEP)
        v_hat = vn / (1.0 - ADAM_B2 ** ADAM_STEP)
        d_ref[...] = -ADAM_LR * (m_hat / (jnp.sqrt(v_hat) + ADAM_EPS) + ADAM_WD * w_ref[...])
        mo_ref[...] = mn
        vo_ref[...] = vn

    blk = pl.BlockSpec((tm, cols), lambda i: (i, 0))
    outs = pl.pallas_call(
        body, name=name, grid=(rows // tm,), in_specs=[blk] * 4, out_specs=[blk] * 3,
        out_shape=[jax.ShapeDtypeStruct((rows, cols), F32)] * 3,
        compiler_params=_cp(("parallel",)),
    )(*[t.reshape(rows, cols) for t in (w, g, m, v)])
    return tuple(t.reshape(shape) for t in outs)


def _adamw_layer(name, l, w, g, m, v, prev, after=None):
    L, rows, cols = w.shape
    tm = rows
    for cand in (512, 352, 256, 176, 128, 64):
        if rows % cand == 0 and cand * cols * 4 <= (1 << 21):
            tm = cand
            break
    if prev is None:
        prev = tuple(lax.empty(w.shape, F32) for _ in range(4))

    n_after = 0 if after is None else 1

    def body(w_ref, g_ref, m_ref, v_ref, *rest):
        d_ref, mo_ref, vo_ref, go_ref = rest[4 + n_after:]
        gv = g_ref[...]
        mn = ADAM_B1 * m_ref[...] + (1.0 - ADAM_B1) * gv
        vn = ADAM_B2 * v_ref[...] + (1.0 - ADAM_B2) * (gv * gv)
        m_hat = mn / (1.0 - ADAM_B1 ** ADAM_STEP)
        v_hat = vn / (1.0 - ADAM_B2 ** ADAM_STEP)
        d_ref[...] = -ADAM_LR * (m_hat / (jnp.sqrt(v_hat) + ADAM_EPS) + ADAM_WD * w_ref[...])
        mo_ref[...] = mn
        vo_ref[...] = vn
        go_ref[...] = gv

    lay = pl.BlockSpec((None, tm, cols), lambda i: (l, i, 0))
    one = pl.BlockSpec((None, tm, cols), lambda i: (0, i, 0))
    return tuple(pl.pallas_call(
        body, name=f"l{l}_adamw_{name}", grid=(rows // tm,),
        in_specs=[lay, one, lay, lay] + [pl.BlockSpec(memory_space=pl.ANY)] * (4 + n_after), out_specs=[lay] * 4,
        out_shape=[jax.ShapeDtypeStruct(w.shape, F32)] * 4,
        input_output_aliases={4 + i: i for i in range(4)},
        compiler_params=_cp(("parallel",)),
    )(w, g, m, v, *prev, *([] if after is None else [after])))


BIG = {
    'w_in': ((DEPTH, D_MODEL, D_MODEL), 'row'),
    'w_mem_kv': ((DEPTH, D_MODEL, 2 * MEM_W), 'row'),
    'w_out': ((DEPTH, D_MODEL, D_MODEL), 'row'),
    'w_kv': ((1, D_MODEL, 2 * MAIN_W), 'col'),
    'w_gate_up': ((DEPTH, D_MODEL, 2 * D_FF), 'col'),
    'w_down': ((DEPTH, D_FF, D_MODEL), 'row'),
}
BIG_NAMES = tuple(BIG)
N_CHIPS = 4
HBM_ANY = pl.BlockSpec(memory_space=pl.ANY)


def _geom(name):
    (L, R, C), kind = BIG[name]
    if kind == 'row':
        return L, R, C, kind, R // N_CHIPS, C, R // (2 * N_CHIPS)
    return L, R, C, kind, R, C // N_CHIPS, R // 2


def _shard_shape(name):
    L, R, C, kind, rs, cs, rh = _geom(name)
    return (L, rs, cs)


def _half_shape(name):
    L, R, C, kind, rs, cs, rh = _geom(name)
    return (L, rh, cs)


def _full_win(ref, name, s, h):
    L, R, C, kind, rs, cs, rh = _geom(name)
    if kind == 'row':
        rows = pl.ds(s * rs, rs) if h is None else pl.ds(s * rs + h * rh, rh)
        return ref.at[:, rows, :]
    rows = slice(None) if h is None else pl.ds(h * rh, rh)
    return ref.at[:, rows, pl.ds(s * cs, cs)]


def _shard_half(ref, name, h):
    L, R, C, kind, rs, cs, rh = _geom(name)
    return ref.at[:, pl.ds(h * rh, rh), :]


def _halves_win(ref, name, s):
    L, R, C, kind, rs, cs, rh = _geom(name)
    if kind == 'row':
        return ref.at[:, pl.ds(s * rh, rh), :]
    return ref.at[:, :, pl.ds(s * cs, cs)]


def _halves_shape(name):
    L, R, C, kind, rs, cs, rh = _geom(name)
    return (L, N_CHIPS * rh, cs) if kind == 'row' else (L, rh, C)


def _place():
    x, y, c = lax.axis_index("x"), lax.axis_index("y"), lax.axis_index("c")
    chips = [(1 - x, y), (x, 1 - y), (1 - x, 1 - y)]
    return x, y, c, chips


SMALL_ROWS = 24


SEM_SPEC = pl.BlockSpec(memory_space=pltpu.SEMAPHORE)
HBM_SPEC = pl.BlockSpec(memory_space=pltpu.HBM)
DATAFLOW = pltpu.SideEffectType.DATAFLOW_SIDE_EFFECTING


def _in_hbm(a):
    return pltpu.with_memory_space_constraint(a, pltpu.HBM)


def _remote(src, dst, send_sems, recv_sems, k, to):
    return pltpu.make_async_remote_copy(src_ref=src, dst_ref=dst, send_sem=send_sems.at[k], recv_sem=recv_sems.at[k],
                                        device_id=to, device_id_type=MESH)


def _split_start(name, bufs, n_copies, sends, after=None):
    nb = len(bufs)
    n_in = nb + (0 if after is None else 1)

    def body(*refs):
        in_refs = refs[:nb]
        send_sems, recv_sems = refs[n_in], refs[n_in + 1]
        token = refs[-1]
        for k, (src, dst, to) in enumerate(sends(in_refs)):
            _remote(src, dst, send_sems, recv_sems, k, to).start()
        token[...] = jnp.zeros_like(token)

    outs = pl.pallas_call(
        body, name=name,
        out_shape=(pltpu.SemaphoreType.DMA((n_copies,)), pltpu.SemaphoreType.DMA((n_copies,)),
                   *[pltpu.HBM(b.shape, b.dtype) for b in bufs], jax.ShapeDtypeStruct((8, 128), F32)),
        in_specs=[HBM_SPEC] * nb + [HBM_ANY] * (n_in - nb),
        out_specs=(SEM_SPEC, SEM_SPEC, *[HBM_SPEC] * nb, pl.BlockSpec(memory_space=pltpu.VMEM)),
        input_output_aliases={i: 2 + i for i in range(nb)},
        compiler_params=pltpu.CompilerParams(has_side_effects=DATAFLOW),
    )(*[_in_hbm(b) for b in bufs], *([] if after is None else [after]))
    return outs[0], outs[1], list(outs[2:2 + nb]), outs[-1]


def _split_wait(name, send_sems, recv_sems, bufs, after, sends, arrivals):
    nb = len(bufs)

    def body(*refs):
        in_refs = refs[:nb]
        s_sems, r_sems = refs[nb], refs[nb + 1]
        me = (lax.axis_index("x"), lax.axis_index("y"), lax.axis_index("c"))
        for k, (src, dst, to) in enumerate(sends(in_refs)):
            _remote(src, dst, s_sems, r_sems, k, to).wait_send()
        for k, win in enumerate(arrivals(in_refs)):
            _remote(win, win, s_sems, r_sems, k, me).wait_recv()

    outs = pl.pallas_call(
        body, name=name,
        out_shape=[pltpu.HBM(b.shape, b.dtype) for b in bufs],
        in_specs=[HBM_SPEC] * nb + [SEM_SPEC, SEM_SPEC, HBM_ANY],
        out_specs=[HBM_SPEC] * nb,
        input_output_aliases={i: i for i in range(nb)},
        compiler_params=pltpu.CompilerParams(has_side_effects=DATAFLOW),
    )(*bufs, send_sems, recv_sems, after)
    return list(outs)


MIX_W = ('w_in', 'w_mem_kv', 'w_out')
FFN_W = ('w_gate_up', 'w_down')
LAYER_W = MIX_W + FFN_W


def _place_own(tag, names, sources, small, sc):
    nw = len(names)
    has_small = small is not None

    def body(sc_ref, *refs):
        srcs = refs[:nw]
        shard_out = refs[nw + has_small:2 * nw + has_small]
        full_out = refs[2 * nw + has_small:3 * nw + has_small]
        for src, sh, fu in zip(srcs, shard_out, full_out):
            v = src[...].astype(BF)
            sh[...] = v
            fu[...] = v
        if has_small:
            refs[-1][...] = refs[nw][...]

    in_specs, shard_specs, full_specs, shard_shape, full_shape, ops = [], [], [], [], [], []
    for nm, (arr, layer) in zip(names, sources):
        L, R, C, kind, rs, cs, rh = _geom(nm)
        in_specs.append(pl.BlockSpec((1, rs, cs), lambda i, sc_ref, layer=layer: (layer, 0, 0)))
        shard_specs.append(pl.BlockSpec((1, rs, cs), lambda i, sc_ref: (0, 0, 0)))
        if kind == 'row':
            full_specs.append(pl.BlockSpec((1, rs, cs), lambda i, sc_ref: (0, sc_ref[0], 0)))
        else:
            full_specs.append(pl.BlockSpec((1, rs, cs), lambda i, sc_ref: (0, 0, sc_ref[0])))
        shard_shape.append(jax.ShapeDtypeStruct((1, rs, cs), BF))
        full_shape.append(jax.ShapeDtypeStruct((1, R, C), BF))
        ops.append(arr)
    if has_small:
        in_specs.append(pl.BlockSpec((SMALL_ROWS, 256), lambda i, sc_ref: (0, 0)))
        full_specs.append(pl.BlockSpec((None, SMALL_ROWS, 256), lambda i, sc_ref: (sc_ref[0], 0, 0)))
        full_shape.append(jax.ShapeDtypeStruct((N_CHIPS, SMALL_ROWS, 256), F32))
        ops.append(small)
    outs = pl.pallas_call(
        body, name=f"{tag}_place_own_shard",
        grid_spec=pltpu.PrefetchScalarGridSpec(num_scalar_prefetch=1, grid=(1,), in_specs=in_specs,
                                               out_specs=shard_specs + full_specs),
        out_shape=shard_shape + full_shape,
        compiler_params=_cp(("arbitrary",)),
    )(sc, *ops)
    return list(outs[:nw]), list(outs[nw:])


def _gather_start(l, names, sources, small, sc, after=None):
    nw = len(names)
    has_small = small is not None
    shards, fulls = _place_own(l, names, sources, small, sc)
    bufs = list(shards) + ([small] if has_small else []) + list(fulls)
    n_src = nw + (1 if has_small else 0)

    def sends(refs):
        x, y, c, chips = _place()
        s = 2 * x + y
        out = []
        for (px, py) in chips:
            for wi, nm in enumerate(names):
                out.append((_shard_half(refs[wi], nm, c), _full_win(refs[n_src + wi], nm, s, c), (px, py, c)))
            if has_small:
                out.append((refs[nw], refs[n_src + nw].at[s], (px, py, c)))
        return out

    def arrivals(refs):
        x, y, c, chips = _place()
        out = []
        for (px, py) in chips:
            sp = 2 * px + py
            for wi, nm in enumerate(names):
                out.append(_full_win(refs[n_src + wi], nm, sp, c))
            if has_small:
                out.append(refs[n_src + nw].at[sp])
        return out

    n_copies = 3 * n_src
    send_sems, recv_sems, bufs, token = _split_start(f"{l}_gather_ici_start", bufs, n_copies, sends, after)
    return dict(l=l, names=names, has_small=has_small, sems=(send_sems, recv_sems), bufs=bufs, sends=sends,
                arrivals=arrivals, token=token)


def _gather_forward(st, after):
    l, names = st['l'], st['names']
    nw = len(names)
    n_src = nw + (1 if st['has_small'] else 0)
    bufs = _split_wait(f"{l}_gather_ici_wait", *st['sems'], st['bufs'], after, st['sends'], st['arrivals'])
    fulls = bufs[n_src:n_src + nw]
    small_all = bufs[n_src + nw] if st['has_small'] else None

    def sends(refs):
        x, y, c, chips = _place()
        out = []
        for (px, py) in chips:
            sp = 2 * px + py
            for wi, nm in enumerate(names):
                w = _full_win(refs[wi], nm, sp, c)
                out.append((w, w, (x, y, 1 - c)))
        return out

    def arrivals(refs):
        x, y, c, chips = _place()
        out = []
        for (px, py) in chips:
            sp = 2 * px + py
            for wi, nm in enumerate(names):
                out.append(_full_win(refs[wi], nm, sp, 1 - c))
        return out

    send_sems, recv_sems, fulls, token = _split_start(f"{l}_gather_d2d_start", fulls, 3 * nw, sends)
    return dict(l=l, names=names, sems=(send_sems, recv_sems), bufs=fulls, sends=sends, arrivals=arrivals,
                small_all=small_all, token=token)


def _gather_finish(st, after):
    fulls = _split_wait(f"{st['l']}_gather_d2d_wait", *st['sems'], st['bufs'], after, st['sends'], st['arrivals'])
    return dict(zip(st['names'], fulls)), st['small_all']


def _reduce_start(tag, names, grads):
    nw = len(names)
    recv = [lax.empty((1,) + _halves_shape(nm)[1:], F32) for nm in names]
    bufs = [grads[nm] for nm in names] + recv

    def windows(refs, half_of):
        x, y, c, _ = _place()
        h = half_of(c)
        out = []
        for wi, nm in enumerate(names):
            L, R, C, kind, rs, cs, rh = _geom(nm)
            if kind == 'row':
                for sp in range(N_CHIPS):
                    out.append((_full_win(refs[wi], nm, sp, h), _halves_win(refs[nw + wi], nm, sp)))
            else:
                out.append((refs[wi].at[:, pl.ds(h * rh, rh), :], refs[nw + wi]))
        return out

    def sends(refs):
        x, y, c, _ = _place()
        return [(src, dst, (x, y, 1 - c)) for src, dst in windows(refs, lambda c: 1 - c)]

    def arrivals(refs):
        return [dst for _, dst in windows(refs, lambda c: c)]

    n_copies = sum(N_CHIPS if BIG[nm][1] == 'row' else 1 for nm in names)
    send_sems, recv_sems, bufs, token = _split_start(tag + "_halves_start", bufs, n_copies, sends)
    return dict(tag=tag, names=names, sems=(send_sems, recv_sems), bufs=bufs, sends=sends, arrivals=arrivals, token=token)


def _reduce_mid(st, after, sc):
    tag, names = st['tag'], st['names']
    nw = len(names)
    bufs = _split_wait(tag + "_halves_wait", *st['sems'], st['bufs'], after, st['sends'], st['arrivals'])
    halves, own = [], []
    for wi, nm in enumerate(names):
        hb, ow = _add_halves(nm, bufs[wi], bufs[nw + wi], sc, tag)
        halves.append(hb)
        own.append(ow)
    pieces = [lax.empty((3, 1) + _half_shape(nm)[1:], BF) for nm in names]

    def sends(refs):
        x, y, c, chips = _place()
        out = []
        for j, (px, py) in enumerate(chips):
            for wi, nm in enumerate(names):
                out.append((_halves_win(refs[wi], nm, 2 * px + py), refs[nw + wi].at[j], (px, py, c)))
        return out

    def arrivals(refs):
        return [refs[nw + wi].at[j] for j in range(3) for wi in range(nw)]

    send_sems, recv_sems, bufs, token = _split_start(tag + "_pieces_start", halves + pieces, 3 * nw, sends)
    return dict(tag=tag, names=names, sems=(send_sems, recv_sems), bufs=bufs, sends=sends, arrivals=arrivals, own=own,
                token=token)


def _reduce_late(st, after, sc):
    tag, names = st['tag'], st['names']
    nw = len(names)
    bufs = _split_wait(tag + "_pieces_wait", *st['sems'], st['bufs'], after, st['sends'], st['arrivals'])
    gsh = [_sum_pieces(nm, st['own'][wi], bufs[nw + wi], sc, tag) for wi, nm in enumerate(names)]

    def sends(refs):
        x, y, c, _ = _place()
        return [(_shard_half(refs[wi], nm, c), _shard_half(refs[wi], nm, c), (x, y, 1 - c)) for wi, nm in enumerate(names)]

    def arrivals(refs):
        x, y, c, _ = _place()
        return [_shard_half(refs[wi], nm, 1 - c) for wi, nm in enumerate(names)]

    send_sems, recv_sems, bufs, token = _split_start(tag + "_share_start", gsh, nw, sends)
    return dict(tag=tag, names=names, sems=(send_sems, recv_sems), bufs=bufs, sends=sends, arrivals=arrivals, token=token)


def _reduce_finish(st, after):
    gsh = _split_wait(st['tag'] + "_share_wait", *st['sems'], st['bufs'], after, st['sends'], st['arrivals'])
    return dict(zip(st['names'], gsh))


def _add_halves(name, g, r, sc, tag):
    _, R, C, kind, rs, cs, rh = _geom(name)
    L = g.shape[0]
    tr = rh
    nr = rh // tr

    def body(sc_ref, g_ref, r_ref, hb_ref, own_ref):
        sp = pl.program_id(2)
        tot = g_ref[...] + r_ref[...]
        hb_ref[...] = tot.astype(hb_ref.dtype)

        @pl.when(sp == sc_ref[0])
        def _():
            own_ref[...] = tot

    if kind == 'row':
        g_map = lambda l, ri, sp, sc_ref: (l, sp * 2 + sc_ref[1], 0)
        h_map = lambda l, ri, sp, sc_ref: (l, sp, 0)
    else:
        g_map = lambda l, ri, sp, sc_ref: (l, sc_ref[1] * nr + ri, sp)
        h_map = lambda l, ri, sp, sc_ref: (l, ri, sp)
    own_map = lambda l, ri, sp, sc_ref: (l, ri, 0)
    blk = (None, tr, cs)
    return pl.pallas_call(
        body, name=tag + "_add_halves_" + name,
        grid_spec=pltpu.PrefetchScalarGridSpec(
            num_scalar_prefetch=1, grid=(L, nr, N_CHIPS),
            in_specs=[pl.BlockSpec(blk, g_map), pl.BlockSpec(blk, h_map)],
            out_specs=[pl.BlockSpec(blk, h_map), pl.BlockSpec(blk, own_map)]),
        out_shape=[jax.ShapeDtypeStruct((L,) + _halves_shape(name)[1:], BF),
                   jax.ShapeDtypeStruct((L,) + _half_shape(name)[1:], F32)],
        compiler_params=_cp(("parallel", "parallel", "arbitrary")),
    )(sc, g, r)


def _sum_pieces(name, own, pieces, sc, tag):
    _, R, C, kind, rs, cs, rh = _geom(name)
    L = own.shape[0]
    tr = rh
    nr = rh // tr

    def body(sc_ref, o_ref, p_ref, out_ref):
        out_ref[...] = o_ref[...] + p_ref[0].astype(F32) + p_ref[1].astype(F32) + p_ref[2].astype(F32)

    blk = (None, tr, cs)
    return pl.pallas_call(
        body, name=tag + "_sum_pieces_" + name,
        grid_spec=pltpu.PrefetchScalarGridSpec(
            num_scalar_prefetch=1, grid=(L, nr),
            in_specs=[pl.BlockSpec(blk, lambda l, ri, sc_ref: (l, ri, 0)),
                      pl.BlockSpec((3, None, tr, cs), lambda l, ri, sc_ref: (0, l, ri, 0))],
            out_specs=pl.BlockSpec(blk, lambda l, ri, sc_ref: (l, sc_ref[1] * nr + ri, 0))),
        out_shape=jax.ShapeDtypeStruct((L,) + _shard_shape(name)[1:], F32),
        compiler_params=_cp(("parallel", "parallel")),
    )(sc, own, pieces)


def _small_gather_start(v, sc):
    rows = v.shape[0]

    def place(sc_ref, v_ref, o_ref):
        o_ref[...] = v_ref[...]

    slots = pl.pallas_call(
        place, name="small_grads_place_own",
        grid_spec=pltpu.PrefetchScalarGridSpec(
            num_scalar_prefetch=1, grid=(1,),
            in_specs=[pl.BlockSpec((rows, 128), lambda i, sc_ref: (0, 0))],
            out_specs=pl.BlockSpec((None, rows, 128), lambda i, sc_ref: (2 * sc_ref[0] + sc_ref[1], 0, 0))),
        out_shape=jax.ShapeDtypeStruct((8, rows, 128), v.dtype),
        compiler_params=_cp(("arbitrary",)),
    )(sc, v)

    def peers():
        x, y, c, _ = _place()
        flips = [(fx, fy, fc) for fx in (0, 1) for fy in (0, 1) for fc in (0, 1)][1:]
        return [((1 - x if fx else x), (1 - y if fy else y), (1 - c if fc else c)) for fx, fy, fc in flips]

    def sends(refs):
        x, y, c, _ = _place()
        return [(refs[0], refs[1].at[4 * x + 2 * y + c], p) for p in peers()]

    def arrivals(refs):
        return [refs[1].at[4 * px + 2 * py + pc] for px, py, pc in peers()]

    send_sems, recv_sems, bufs, token = _split_start("small_grads_gather_start", [v, slots], 7, sends)
    return dict(sems=(send_sems, recv_sems), bufs=bufs, sends=sends, arrivals=arrivals, token=token)


def _small_gather_finish(st, after):
    return _split_wait("small_grads_gather_wait", *st['sems'], st['bufs'], after, st['sends'], st['arrivals'])[1]


def _sum8(v8, *, name, tr=336):
    rows = v8.shape[1]
    tr = min(tr, rows)
    assert rows % tr == 0

    def body(v_ref, o_ref):
        tot = v_ref[0].astype(F32)
        for d in range(1, 8):
            tot = tot + v_ref[d].astype(F32)
        o_ref[...] = tot

    return pl.pallas_call(
        body, name=name, grid=(rows // tr,),
        in_specs=[pl.BlockSpec((8, tr, 128), lambda i: (0, i, 0))], out_specs=pl.BlockSpec((tr, 128), lambda i: (i, 0)),
        out_shape=jax.ShapeDtypeStruct((rows, 128), F32),
        compiler_params=_cp(("parallel",)),
    )(v8)


def _block_diag(w_pool_l):
    wbd = jnp.zeros((MAIN_W, MAIN_W), F32)
    for gi in range(len(POOL_WINDOWS)):
        wbd = lax.dynamic_update_slice(wbd, w_pool_l[gi], (gi * POOL_GROUP, gi * POOL_GROUP))
    return wbd.astype(BF)


def _unpack_small(small_all):
    ng = small_all[:, :16, :].reshape(N_CHIPS, DEPTH, 4, 256).transpose(1, 2, 0, 3).reshape(DEPTH, 4, D_MODEL)
    ps = small_all[:, 16:18, :POOL_GROUP].transpose(1, 0, 2).reshape(N_A, MAIN_W)
    return ng, ps


def _local_step(x, mem, positions, on_forward, on_backward, mem_norm, w_pool, kv_norm, target):
    B, S, _ = x.shape
    T = B * S
    xc = x.reshape(T, D_MODEL)
    memf = mem.reshape(B * N_MEM, D_MODEL)
    tgt = target.reshape(T, D_MODEL)
    cos, sin = _rope_tables(positions.reshape(T, 1), name="rope_tables")
    wbd = [_block_diag(w_pool[l]) for l in range(N_A)]
    nbo = D_FF // 256
    fw = []
    rk = rv = None
    kv_saved = None
    wts = []
    norm_gains = pool_scale = y2 = None

    for l in range(DEPTH):
        t = f"l{l}_"
        got = on_forward('start', l, y2)
        wts.append(dict(got[0]))
        if l == 0:
            norm_gains, pool_scale = _unpack_small(got[1])
        sv = {'x_in': xc}
        h0 = _norm_fwd(xc, norm_gains[l, 0], name=t + "norm0", out_dtype=BF, tm=1024, after=got[2])
        z, = _mm(h0, wts[l]['w_in'], b_layer=0, name=t + "mm_in", tm=1024, tn=1024)
        memn = _norm_fwd(memf, mem_norm[l], name=t + "norm_mem", out_dtype=BF, tm=256)
        kvm, = _mm(memn, wts[l]['w_mem_kv'], b_layer=0, name=t + "mm_memkv", out_dtypes=(BF,))
        if l < N_A:
            ycat, sv['p'] = _pool_fwd(z, wbd[l], pool_scale[l], B, S, name=t + "pool_fwd")
        else:
            rq = _rope_apply(z, cos, sin, name=t + "rope_q", out_dtype=F32, tm=1024)
            o = lax.empty((T, MAIN_W), F32)
            lse = lax.empty((T, MAIN_W), F32)
            for g in range(3):
                o, lse = _dil_fwd(g, rq, rk, rv, o, lse, B, S, name=t + f"dil_fwd{g}")
            ycat = _dil_combine_fwd(o, lse, lax.empty((T, D_MODEL), BF), name=t + "dil_combine", tm=1024)
            sv.update(rq=rq, o=o, lse=lse)
        ycat, sv['lse_m'] = _memattn_fwd(z, kvm, ycat, B, S, name=t + "memattn_fwd")
        tok = on_forward('mid', l, ycat)
        y1, = _mm(ycat, wts[l]['w_out'], b_layer=0, name=t + "mm_out", tm=1024, tn=1024)
        wts[l].update(on_forward('ffn', l, y1)[0])
        x1 = _norm_fwd(y1, norm_gains[l, 1], name=t + "norm1", res=xc, after=tok)
        h2 = _norm_fwd(x1, norm_gains[l, 2], name=t + "norm2", out_dtype=BF, tm=1024)
        gg, uu, aa = _mm(h2, wts[l]['w_gate_up'], b_layer=0, b_offsets=(0, nbo), out_n=D_FF, tm=4096, tn=256, name=t + "mm_gate_up",
                         epilogue=_swiglu_fwd_epilogue, out_dtypes=(BF, BF, BF))
        on_forward('post', l, gg)
        y2, = _mm(aa, wts[l]['w_down'], b_layer=0, tk=D_FF, name=t + "mm_down")
        x2 = _norm_fwd(y2, norm_gains[l, 3], name=t + "norm3", res=x1)
        sv.update(h0=h0, z=z, memn=memn, kvm=kvm, ycat=ycat, y1=y1, x1=x1, h2=h2, gg=gg, uu=uu, aa=aa, y2=y2)
        fw.append(sv)
        xc = x2
        if l == N_A - 1:
            kvn = _norm_fwd(xc, kv_norm, name="norm_kv", out_dtype=BF, tm=1024)
            kv, = _mm(kvn, wts[N_A - 1]['w_kv'], b_layer=0, name="mm_kv")
            rk, rv = _rope_apply(kv, cos, sin, name="rope_k", passthrough=True, out_dtype=F32, tm=1024)
            kv_saved = (xc, kvn)

    loss, dx = _loss(xc, tgt, name="loss", tm=1024)

    d_ng = [[None] * 4 for _ in range(DEPTH)]
    d_memnorm = [None] * DEPTH
    d_wbd = [None] * N_A
    d_pscale = [None] * N_A
    d_kvnorm = None
    kv_parts = []
    tok = None

    def as3d(gl):
        return {nm: g.reshape((1,) + g.shape) for nm, g in gl.items()}

    for l in reversed(range(DEPTH)):
        t = f"l{l}_b_"
        sv = fw[l]
        gl = {}
        dy2, d_ng[l][3] = _norm_bwd(dx, sv['y2'], norm_gains[l, 3], name=t + "norm3", out_dtype=BF, tm=1024, after=tok)
        gl['w_down'], = _mm(sv['aa'], dy2, ta=True, tm=1408, tn=512, tk=4096, name=t + "dw_down")
        dg, du = _mm(dy2, wts[l]['w_down'], tb=True, b_layer=0, tm=1024, tn=1408, name=t + "d_act",
                     extras=((sv['gg'], 'tile'), (sv['uu'], 'tile')), epilogue=_swiglu_bwd_epilogue, out_dtypes=(BF, BF))
        gl['w_gate_up'], = _mm(sv['h2'], (dg, du), ta=True, tn=1408, tk=1024, name=t + "dw_gate_up")
        dh2, = _mm((dg, du), wts[l]['w_gate_up'], tb=True, b_layer=0, tn=1024, tk=1408, name=t + "d_h2", out_dtypes=(BF,))
        dx1, d_ng[l][2] = _norm_bwd(dh2, sv['x1'], norm_gains[l, 2], name=t + "norm2", add=dx, tm=1024)
        tok = on_backward('ffn', l, dx1, as3d(gl))
        dy1, d_ng[l][1] = _norm_bwd(dx1, sv['y1'], norm_gains[l, 1], name=t + "norm1", out_dtype=BF, tm=1024, after=tok)
        gl['w_out'], = _mm(sv['ycat'], dy1, ta=True, name=t + "dw_out", tk=4096)
        dycat, = _mm(dy1, wts[l]['w_out'], tb=True, b_layer=0, name=t + "d_ycat", tm=1024, tn=1024)
        dz = lax.empty((T, D_MODEL), BF)
        dz, dkm, dvm = _memattn_bwd(dycat, sv['z'], sv['kvm'], sv['lse_m'], dz, B, S, name=t + "memattn")
        if l < N_A:
            dz, d_wbd[l], d_pscale[l] = _pool_bwd(dycat, sv['p'], wbd[l], pool_scale[l], dz, B, S, name=t + "pool")
        else:
            do, cb = _dil_combine_bwd(dycat, sv['o'], sv['lse'], name=t + "dil_combine", tm=512)
            acc = tuple(lax.empty((T, MAIN_W), F32) for _ in range(3))
            for g in range(3):
                acc = _dil_bwd(g, sv['rq'], rk, rv, do, cb, sv['lse'], acc, B, S, name=t + f"dil{g}")
            dz = _rope_apply(acc[0], cos, sin, name=t + "rope_q", sign=-1.0, alias=dz, tm=1024)
            kv_parts.append(acc[1:])
        tok = on_backward('mix', l, dz, as3d(gl))
        gl['w_in'], = _mm(sv['h0'], dz, ta=True, name=t + "dw_in", tk=4096)
        dh0, = _mm(dz, wts[l]['w_in'], tb=True, b_layer=0, name=t + "d_h0", out_dtypes=(BF,), tm=1024, tn=1024)
        dx, d_ng[l][0] = _norm_bwd(dh0, sv['x_in'], norm_gains[l, 0], name=t + "norm0", add=dx1, tm=1024, after=tok)
        gl['w_mem_kv'], = _mm(sv['memn'], (dkm, dvm), ta=True, tn=256, name=t + "dw_memkv")
        dmemn, = _mm((dkm, dvm), wts[l]['w_mem_kv'], tb=True, b_layer=0, tk=256, name=t + "d_memn", out_dtypes=(BF,))
        _, d_memnorm[l] = _norm_bwd(dmemn, memf, mem_norm[l], name=t + "norm_mem", out_dtype=BF, tm=256)
        if l == N_A:
            dk, dv = _kv_grad_sum(kv_parts, cos, sin, name="kv_grad", tm=1024)
            x_kv, kvn = kv_saved
            gl['w_kv'], = _mm(kvn, (dk, dv), ta=True, tn=768, tk=2048, name="dw_kv")
            dkvn, = _mm((dk, dv), wts[N_A - 1]['w_kv'], tb=True, b_layer=0, tn=1024, tk=768, name="d_kvn", out_dtypes=(BF,))
            dx, d_kvnorm = _norm_bwd(dkvn, x_kv, kv_norm, name="norm_kv_b", add=dx, tm=1024)
        tok = on_backward('end', l, dx, as3d(gl))

    small = {
        'norm_gains': jnp.stack([jnp.concatenate(d_ng[l], axis=0) for l in range(DEPTH)]),
        'mem_norm': jnp.concatenate(d_memnorm, axis=0),
        'kv_norm': d_kvnorm.reshape(D_MODEL),
        'pool_scale': jnp.concatenate(d_pscale, axis=0),
        'w_pool': jnp.stack([jnp.stack([d_wbd[l][gi * POOL_GROUP:(gi + 1) * POOL_GROUP, gi * POOL_GROUP:(gi + 1) * POOL_GROUP]
                                        for gi in range(len(POOL_WINDOWS))]) for l in range(N_A)]),
    }
    return loss, dx, small


SMALL_ORDER = ('norm_gains', 'mem_norm', 'kv_norm', 'pool_scale', 'w_pool')
SMALL_VEC_ROWS = 2560


def kernel(x, mem, positions, norm_gains, mem_norm, w_in, w_mem_kv, w_out, w_pool, pool_scale, kv_norm, w_kv, w_gate_up, w_down, loss_target, m_norm_gains, m_mem_norm, m_w_in, m_w_mem_kv, m_w_out, m_w_pool, m_pool_scale, m_kv_norm, m_w_kv, m_w_gate_up, m_w_down, v_norm_gains, v_mem_norm, v_w_in, v_w_mem_kv, v_w_out, v_w_pool, v_pool_scale, v_kv_norm, v_w_kv, v_w_gate_up, v_w_down):
    xi, yi, ci = lax.axis_index("x"), lax.axis_index("y"), lax.axis_index("c")
    s = 2 * xi + yi
    sc = jnp.stack([s, ci]).astype(jnp.int32)
    weights = dict(norm_gains=norm_gains, mem_norm=mem_norm, w_in=w_in, w_mem_kv=w_mem_kv, w_out=w_out, w_pool=w_pool,
                   pool_scale=pool_scale, kv_norm=kv_norm, w_kv=w_kv, w_gate_up=w_gate_up, w_down=w_down)
    moms = dict(norm_gains=m_norm_gains, mem_norm=m_mem_norm, w_in=m_w_in, w_mem_kv=m_w_mem_kv, w_out=m_w_out,
                w_pool=m_w_pool, pool_scale=m_pool_scale, kv_norm=m_kv_norm, w_kv=m_w_kv, w_gate_up=m_w_gate_up,
                w_down=m_w_down)
    vels = dict(norm_gains=v_norm_gains, mem_norm=v_mem_norm, w_in=v_w_in, w_mem_kv=v_w_mem_kv, w_out=v_w_out,
                w_pool=v_w_pool, pool_scale=v_pool_scale, kv_norm=v_kv_norm, w_kv=v_w_kv, w_gate_up=v_w_gate_up,
                w_down=v_w_down)

    small_w = jnp.zeros((SMALL_ROWS, 256), F32)
    small_w = lax.dynamic_update_slice(small_w, norm_gains.reshape(16, 256), (0, 0))
    small_w = lax.dynamic_update_slice(small_w, pool_scale, (16, 0))

    def shard_of(nm, l):
        return (w_kv.reshape(_shard_shape('w_kv')), 0) if nm == 'w_kv' else (weights[nm], l)

    groups = {'l0a': (0, MIX_W), 'l0b': (0, FFN_W)}
    groups.update({f"l{l}": (l, LAYER_W + (('w_kv',) if l == N_A - 1 else ())) for l in range(1, DEPTH)})
    on_ici, on_d2d, gathered = {}, {}, {}

    def start_group(tag, after):
        l, names = groups[tag]
        on_ici[tag] = _gather_start(tag, names, [shard_of(nm, l) for nm in names], small_w if tag == 'l0a' else None, sc,
                                    after)
        return [on_ici[tag]['token']]

    def on_forward(where, l, after):
        if where == 'start':
            if l == 0:
                start_group('l0a', None)
                st = on_ici.pop('l0a')
                fwd = _gather_forward(st, st['token'])
                w, small_all = _gather_finish(fwd, fwd['token'])
                return w, small_all, start_group('l0b', w['w_in'])
            if f"l{l}" not in on_d2d:
                on_d2d[f"l{l}"] = _gather_forward(on_ici.pop(f"l{l}"), after)
            gathered[l] = _gather_finish(on_d2d.pop(f"l{l}"), after)[0]
            tok = start_group(f"l{l + 1}", gathered[l]['w_in']) if l + 1 < DEPTH else None
            return {nm: w for nm, w in gathered[l].items() if nm not in FFN_W}, None, tok
        if where == 'mid' and l == 0:
            on_d2d['l0b'] = _gather_forward(on_ici.pop('l0b'), after)
            return start_group('l1', on_d2d['l0b']['token'])
        if where == 'ffn':
            if l == 0:
                return (_gather_finish(on_d2d.pop('l0b'), after)[0],)
            return ({nm: gathered[l][nm] for nm in FFN_W},)
        if where == 'post' and 0 < l < DEPTH - 1:
            on_d2d[f"l{l + 1}"] = _gather_forward(on_ici.pop(f"l{l + 1}"), after)
        return None

    hook_of = {'ffn': 0, 'mix': 1, 'end': 2}
    active, reduced = [], {l: {} for l in range(DEPTH)}
    advance = {'mid': lambda st, after: _reduce_mid(st, after, sc), 'late': lambda st, after: _reduce_late(st, after, sc)}

    def run_hook(idx, after):
        toks = []
        for grp in list(active):
            while grp['plan'] and grp['plan'][0][1] <= idx:
                step = grp['plan'].pop(0)[0]
                if step == 'finish':
                    reduced[grp['layer']].update(_reduce_finish(grp['st'], after))
                    active.remove(grp)
                else:
                    grp['st'] = advance[step](grp['st'], after)
                    toks.append(grp['st']['token'])
        return toks

    def on_backward(where, l, after, grads):
        idx = 3 * (DEPTH - 1 - l) + hook_of[where]
        toks = run_hook(idx, after)
        if where == 'end' or (where == 'ffn' and l == 0):
            names = FFN_W if where == 'ffn' else tuple(nm for nm in grads if l > 0 or nm not in FFN_W)
            st = _reduce_start(f"l{l}_{where}_grads", names, {nm: grads[nm] for nm in names})
            plan = [('mid', idx + 1), ('late', idx + 3), ('finish', idx + 4)] if where == 'ffn' else \
                   [('mid', idx + 1), ('late', idx + 2), ('finish', idx + 3)]
            active.append(dict(layer=l, st=st, plan=plan))
            toks.append(st['token'])
        return toks

    loss, gx, gsmall = _local_step(x, mem, positions, on_forward, on_backward, mem_norm, w_pool, kv_norm, loss_target)
    loss = lax.psum(loss[0, 0], ("x", "y", "c"))

    vec = jnp.concatenate([gsmall[nm].reshape(-1) for nm in SMALL_ORDER])
    vec = jnp.pad(vec, (0, SMALL_VEC_ROWS * 128 - vec.shape[0])).reshape(SMALL_VEC_ROWS, 128)
    vec = vec + sum(grp['st']['token'][0, 0] for grp in active)
    small_st = _small_gather_start(vec.astype(BF), sc)
    outs = {nm: None for nm in LAYER_W}

    def adamw_layers(layers, names, after):
        for l in layers:
            for nm in names:
                outs[nm] = _adamw_layer(nm, l, weights[nm], reduced[l][nm], moms[nm], vels[nm], outs[nm], after)
                after = outs[nm][0]
        return after

    def zero_of(toks, st):
        return sum(toks) if toks else st['token']

    last = 3 * DEPTH
    toks = run_hook(last, small_st['token'])
    done = adamw_layers(range(DEPTH - 1, 0, -1), LAYER_W, zero_of(toks, small_st))
    toks = run_hook(last + 1, done)
    done = adamw_layers([0], FFN_W, zero_of(toks, small_st))
    tot = _sum8(_small_gather_finish(small_st, done), name="sum_small_grads", tr=512)
    run_hook(last + 2, tot)
    assert not active
    adamw_layers([0], MIX_W, None)
    tot = tot.reshape(-1)
    grads, off = {}, 0
    for nm in SMALL_ORDER:
        shape = (DEPTH, 4, D_MODEL) if nm == 'norm_gains' else (N_A, MAIN_W) if nm == 'pool_scale' else weights[nm].shape
        n = 1
        for dim in shape:
            n *= dim
        grads[nm] = tot[off:off + n].reshape(shape)
        off += n
    grads['norm_gains'] = lax.dynamic_slice(grads['norm_gains'], (0, 0, s * 256), (DEPTH, 4, 256))
    grads['pool_scale'] = lax.dynamic_slice(grads['pool_scale'], (0, s * POOL_GROUP), (N_A, POOL_GROUP))
    grads['w_kv'] = reduced[N_A]['w_kv'].reshape(w_kv.shape)

    order = ('norm_gains', 'mem_norm', 'w_in', 'w_mem_kv', 'w_out', 'w_pool', 'pool_scale', 'kv_norm', 'w_kv',
             'w_gate_up', 'w_down')
    deltas, new_m, new_v = {}, {}, {}
    for nm in order:
        if nm in LAYER_W:
            deltas[nm], new_m[nm], new_v[nm], grads[nm] = outs[nm]
        else:
            deltas[nm], new_m[nm], new_v[nm] = _adamw(weights[nm], grads[nm], moms[nm], vels[nm], name="adamw_" + nm)
    return (loss, gx.reshape(x.shape), *[grads[nm] for nm in order], *[deltas[nm] for nm in order],
            *[new_m[nm] for nm in order], *[new_v[nm] for nm in order])
```

```python
import jax
import jax.numpy as jnp
from jax import lax
from jax.experimental import pallas as pl
from jax.experimental.pallas import tpu as pltpu

F32 = jnp.float32
BF = jnp.bfloat16

D_MODEL = 1024
DEPTH = 4
N_A = 2
HEAD_DIM = 64
MEM_W = 256
MAIN_W = 768
D_FF = 2816
N_MEM = 256
POOL_WINDOWS = (2, 4, 8, 16)
POOL_GROUP = 192
DIL = (1, 4, 16)
STEPS = 128
ROPE_THETA = 10000.0
EPS = 1e-6
SCALE = HEAD_DIM ** -0.5
NEG = -1e30

ADAM_LR = 0.001
ADAM_B1 = 0.9
ADAM_B2 = 0.999
ADAM_EPS = 1e-08
ADAM_WD = 0.01
ADAM_STEP = 10

VMEM_LIMIT = 48 * 1024 * 1024
MESH = pl.DeviceIdType.MESH


def _cp(sem):
    return pltpu.CompilerParams(dimension_semantics=sem, vmem_limit_bytes=VMEM_LIMIT)


def _mm(a, b, *, name, ta=False, tb=False, tm=1024, tn=512, tk=1024, b_layer=None, b_offsets=(0,),
        extras=(), epilogue=None, out_dtypes=(F32,), out_n=None):
    a_pair = isinstance(a, (tuple, list))
    b_pair = isinstance(b, (tuple, list))
    a0 = a[0] if a_pair else a
    b0 = b[0] if b_pair else b
    a_rows, a_cols = a0.shape
    if a_pair:
        a_cols *= 2
    b_rows, b_cols = b0.shape[-2:]
    if b_pair:
        b_cols *= 2
    M, K = (a_cols, a_rows) if ta else (a_rows, a_cols)
    N = b_rows if tb else b_cols
    if out_n is not None:
        N = out_n
    tm, tn, tk = min(tm, M), min(tn, N), min(tk, K)
    assert M % tm == 0 and N % tn == 0 and K % tk == 0, (name, M, N, K, tm, tn, tk)
    nk = K // tk
    n_acc = len(b_offsets)

    if a_pair:
        a_half = (a0.shape[1] // (tm if ta else tk))
    if b_pair:
        b_half = (b0.shape[1] // (tk if tb else tn))

    def a_map(sel):
        def f(i, j, k):
            r, c = (k, i) if ta else (i, k)
            if a_pair:
                c = jnp.clip(c - sel * a_half, 0, a_half - 1)
            return (r, c)
        return f

    def b_map(sel, off):
        def f(i, j, k):
            r, c = (j + off, k) if tb else (k, j + off)
            if b_pair:
                c = jnp.clip(c - sel * b_half, 0, b_half - 1)
            if b_layer is not None:
                return (b_layer, r, c)
            return (r, c)
        return f

    a_blk = (tk, tm) if ta else (tm, tk)
    b_blk = (tn, tk) if tb else (tk, tn)
    if b_layer is not None:
        b_blk = (None,) + b_blk
    in_specs, operands = [], []
    for sel in range(2 if a_pair else 1):
        in_specs.append(pl.BlockSpec(a_blk, a_map(sel)))
        operands.append(a[sel] if a_pair else a)
    n_a = len(operands)
    for off in b_offsets:
        for sel in range(2 if b_pair else 1):
            in_specs.append(pl.BlockSpec(b_blk, b_map(sel, off)))
            operands.append(b[sel] if b_pair else b)
    n_b = len(operands) - n_a
    for arr, kind in extras:
        if kind == 'tile':
            in_specs.append(pl.BlockSpec((tm, tn), lambda i, j, k: (i, j)))
        elif kind == 'row':
            in_specs.append(pl.BlockSpec((tm, 1), lambda i, j, k: (i, 0)))
        else:
            in_specs.append(pl.BlockSpec((1, tn), lambda i, j, k: (0, j)))
        operands.append(arr)
    n_e = len(extras)
    n_o = len(out_dtypes)
    dims = (((0,) if ta else (1,), (1,) if tb else (0,)), ((), ()))
    in_place = nk > 1 and epilogue is None and n_acc == 1 and tuple(out_dtypes) == (F32,)

    def body(*refs):
        a_refs = refs[:n_a]
        b_refs = refs[n_a:n_a + n_b]
        e_refs = refs[n_a + n_b:n_a + n_b + n_e]
        n_in = n_a + n_b + n_e
        o_refs = refs[n_in:n_in + n_o]
        acc_refs = refs[n_in + n_o:]
        i, j, k = pl.program_id(0), pl.program_id(1), pl.program_id(2)
        if a_pair:
            cidx = i if ta else k
            av = jnp.where(cidx < a_half, a_refs[0][...], a_refs[1][...])
        else:
            av = a_refs[0][...]
        av = av.astype(BF)
        prods = []
        for q in range(n_acc):
            if b_pair:
                cidx = (k if tb else j) + b_offsets[q]
                bv = jnp.where(cidx < b_half, b_refs[2 * q][...], b_refs[2 * q + 1][...])
            else:
                bv = b_refs[q][...]
            prods.append(lax.dot_general(av, bv.astype(BF), dims, preferred_element_type=F32))

        def finish(accs):
            outs = epilogue(accs, *[r[...] for r in e_refs]) if epilogue is not None else accs
            for o_ref, o in zip(o_refs, outs):
                o_ref[...] = o.astype(o_ref.dtype)

        if nk == 1:
            finish(prods)
        elif in_place:
            @pl.when(k == 0)
            def _():
                o_refs[0][...] = prods[0]

            @pl.when(k > 0)
            def _():
                o_refs[0][...] += prods[0]
        else:
            @pl.when(k == 0)
            def _():
                for r, p in zip(acc_refs, prods):
                    r[...] = p

            @pl.when(k > 0)
            def _():
                for r, p in zip(acc_refs, prods):
                    r[...] += p

            @pl.when(k == nk - 1)
            def _():
                finish([r[...] for r in acc_refs])

    return pl.pallas_call(
        body, name=name,
        grid=(M // tm, N // tn, nk),
        in_specs=in_specs,
        out_specs=[pl.BlockSpec((tm, tn), lambda i, j, k: (i, j)) for _ in range(n_o)],
        out_shape=[jax.ShapeDtypeStruct((M, N), dt) for dt in out_dtypes],
        scratch_shapes=[pltpu.VMEM((tm, tn), F32) for _ in range(n_acc if nk > 1 and not in_place else 0)],
        compiler_params=_cp(("parallel", "parallel", "arbitrary")),
    )(*operands)


def _norm_fwd(x, g, *, name, res=None, out_dtype=F32, tm=512, after=None):
    T, Dm = x.shape
    has_res = res is not None
    after = list(after or [])

    def body(*refs):
        refs = refs[:len(refs) - 1 - len(after)] + refs[len(refs) - 1:]
        if has_res:
            x_ref, g_ref, r_ref, y_ref = refs
        else:
            x_ref, g_ref, y_ref = refs
        xv = x_ref[...]
        rstd = lax.rsqrt(jnp.mean(xv * xv, axis=-1, keepdims=True) + EPS)
        y = xv * rstd * g_ref[...]
        if has_res:
            y = r_ref[...] + y
        y_ref[...] = y.astype(y_ref.dtype)

    row = pl.BlockSpec((tm, Dm), lambda i: (i, 0))
    in_specs = [row, pl.BlockSpec((1, Dm), lambda i: (0, 0))] + ([row] if has_res else [])
    in_specs += [pl.BlockSpec(memory_space=pl.ANY)] * len(after)
    ops = [x, g.reshape(1, Dm)] + ([res] if has_res else []) + after
    return pl.pallas_call(
        body, name=name, grid=(T // tm,), in_specs=in_specs,
        out_specs=row,
        out_shape=jax.ShapeDtypeStruct((T, Dm), out_dtype),
        compiler_params=_cp(("parallel",)),
    )(*ops)


def _norm_bwd(dout, x, g, *, name, add=None, out_dtype=F32, tm=512, after=None):
    T, Dm = x.shape
    has_add = add is not None
    nt = T // tm
    after = list(after or [])

    def body(*refs):
        refs = refs[:len(refs) - 3 - len(after)] + refs[len(refs) - 3:]
        if has_add:
            do_ref, x_ref, g_ref, a_ref, dx_ref, dg_ref, acc = refs
        else:
            do_ref, x_ref, g_ref, dx_ref, dg_ref, acc = refs
        i = pl.program_id(0)
        do = do_ref[...].astype(F32)
        xv = x_ref[...]
        rstd = lax.rsqrt(jnp.mean(xv * xv, axis=-1, keepdims=True) + EPS)
        xh = xv * rstd
        gd = do * g_ref[...]
        dx = rstd * (gd - xh * jnp.mean(gd * xh, axis=-1, keepdims=True))
        if has_add:
            dx = dx + a_ref[...].astype(F32)
        dx_ref[...] = dx.astype(dx_ref.dtype)
        part = jnp.sum((do * xh).reshape(tm // 8, 8, Dm), axis=0)

        @pl.when(i == 0)
        def _():
            acc[...] = part

        @pl.when(i > 0)
        def _():
            acc[...] += part

        @pl.when(i == nt - 1)
        def _():
            dg_ref[...] = jnp.sum(acc[...], axis=0, keepdims=True)

    row = pl.BlockSpec((tm, Dm), lambda i: (i, 0))
    in_specs = [row, row, pl.BlockSpec((1, Dm), lambda i: (0, 0))]
    ops = [dout, x, g.reshape(1, Dm)]
    if has_add:
        in_specs.append(row)
        ops.append(add)
    in_specs += [pl.BlockSpec(memory_space=pl.ANY)] * len(after)
    ops += after
    return pl.pallas_call(
        body, name=name, grid=(nt,), in_specs=in_specs,
        out_specs=[row, pl.BlockSpec((1, Dm), lambda i: (0, 0))],
        out_shape=[jax.ShapeDtypeStruct((T, Dm), out_dtype), jax.ShapeDtypeStruct((1, Dm), F32)],
        scratch_shapes=[pltpu.VMEM((8, Dm), F32)],
        compiler_params=_cp(("arbitrary",)),
    )(*ops)


def _swiglu_fwd_epilogue(accs):
    g, u = accs
    return g, u, g * jax.nn.sigmoid(g) * u


def _swiglu_bwd_epilogue(accs, g, u):
    da = accs[0]
    g = g.astype(F32)
    u = u.astype(F32)
    sig = jax.nn.sigmoid(g)
    return da * u * (sig * (1.0 + g * (1.0 - sig))), da * (g * sig)


def _rope_tables(pos, *, name, tm=1024):
    T = pos.shape[0]
    half = HEAD_DIM // 2
    freqs = ROPE_THETA ** (-jnp.arange(half, dtype=F32) / half)
    freqs = jnp.tile(freqs, 4).reshape(1, 128)

    def body(p_ref, f_ref, c_ref, s_ref):
        ang = p_ref[...].astype(F32) * f_ref[...]
        lane = lax.broadcasted_iota(jnp.int32, ang.shape, 1)
        c_ref[...] = jnp.cos(ang)
        s_ref[...] = jnp.where(lane % HEAD_DIM < half, -1.0, 1.0) * jnp.sin(ang)

    tab = pl.BlockSpec((tm, 128), lambda i: (i, 0))
    return pl.pallas_call(
        body, name=name, grid=(T // tm,),
        in_specs=[pl.BlockSpec((tm, 1), lambda i: (i, 0)), pl.BlockSpec((1, 128), lambda i: (0, 0))],
        out_specs=[tab, tab],
        out_shape=[jax.ShapeDtypeStruct((T, 128), F32)] * 2,
        compiler_params=_cp(("parallel",)),
    )(pos, freqs)


def _rot(x, cos, sin, sign):
    W = x.shape[1]
    half = HEAD_DIM // 2
    reps = W // 128
    c = jnp.concatenate([cos] * reps, axis=1) if reps > 1 else cos
    s = jnp.concatenate([sin] * reps, axis=1) if reps > 1 else sin
    lane = lax.broadcasted_iota(jnp.int32, x.shape, 1)
    swapped = jnp.where(lane % HEAD_DIM < half, pltpu.roll(x, W - half, axis=1), pltpu.roll(x, half, axis=1))
    return x * c + (sign * s) * swapped


def _rope_apply(x, cos, sin, *, name, sign=1.0, width=MAIN_W, passthrough=False, out_dtype=BF, alias=None,
                out_cols=None, tm=512):
    T = x.shape[0]

    def body(*refs):
        if passthrough:
            x_ref, v_ref, c_ref, s_ref, o_ref, ov_ref = refs
            ov_ref[...] = v_ref[...].astype(ov_ref.dtype)
        elif alias is not None:
            x_ref, c_ref, s_ref, _, o_ref = refs
        else:
            x_ref, c_ref, s_ref, o_ref = refs
        o_ref[...] = _rot(x_ref[...].astype(F32), c_ref[...], s_ref[...], sign).astype(o_ref.dtype)

    blk0 = pl.BlockSpec((tm, width), lambda i: (i, 0))
    blk1 = pl.BlockSpec((tm, width), lambda i: (i, 1))
    tab = pl.BlockSpec((tm, 128), lambda i: (i, 0))
    if passthrough:
        return pl.pallas_call(
            body, name=name, grid=(T // tm,), in_specs=[blk0, blk1, tab, tab], out_specs=[blk0, blk0],
            out_shape=[jax.ShapeDtypeStruct((T, width), out_dtype)] * 2,
            compiler_params=_cp(("parallel",)),
        )(x, x, cos, sin)
    if alias is not None:
        return pl.pallas_call(
            body, name=name, grid=(T // tm,),
            in_specs=[blk0, tab, tab, pl.BlockSpec(memory_space=pl.ANY)], out_specs=blk0,
            out_shape=jax.ShapeDtypeStruct(alias.shape, alias.dtype),
            input_output_aliases={3: 0},
            compiler_params=_cp(("parallel",)),
        )(x, cos, sin, alias)
    return pl.pallas_call(
        body, name=name, grid=(T // tm,), in_specs=[blk0, tab, tab], out_specs=blk0,
        out_shape=jax.ShapeDtypeStruct((T, width), out_dtype),
        compiler_params=_cp(("parallel",)),
    )(x, cos, sin)


POOL_T = 256
POOL_HALO = 16


def _pool_lane_window(shape):
    lane = lax.broadcasted_iota(jnp.int32, shape, 1)
    w = jnp.full(shape, POOL_WINDOWS[0], jnp.int32)
    for gi in range(1, len(POOL_WINDOWS)):
        w = jnp.where(lane >= gi * POOL_GROUP, POOL_WINDOWS[gi], w)
    return w


POOL_PAD = 32
POOL_R = POOL_T + POOL_PAD


def _pool_window_sums(buf, tmp_a, tmp_b, win, back):
    src, dst, acc = buf, tmp_a, None
    for j, (w, sh) in enumerate(zip(POOL_WINDOWS, (1, 2, 4, 8)), start=1):
        n = POOL_R - 8 * j
        if back:
            dst[pl.ds(8 * j, n), :] = src[pl.ds(8 * j, n), :] + src[pl.ds(8 * j - sh, n), :]
            cur = dst[pl.ds(POOL_PAD, POOL_T), :]
        else:
            dst[pl.ds(0, n), :] = src[pl.ds(0, n), :] + src[pl.ds(sh, n), :]
            cur = dst[pl.ds(0, POOL_T), :]
        acc = cur if acc is None else jnp.where(win >= w, cur, acc)
        src, dst = dst, (tmp_b if dst is tmp_a else tmp_a)
    return acc


def _pool_fwd(z, wbd, scale, B, S, *, name):
    T = z.shape[0]
    nt = S // POOL_T
    hb = POOL_T // POOL_PAD

    def body(z_ref, h_ref, w_ref, sc_ref, y_ref, p_ref, ext, tmp_a, tmp_b):
        i = pl.program_id(1)
        u = z_ref[...]
        ext[pl.ds(POOL_PAD, POOL_T), :] = u
        ext[pl.ds(0, POOL_PAD), :] = jnp.where(i > 0, h_ref[...], 0.0)
        win = _pool_lane_window((POOL_T, MAIN_W))
        acc = _pool_window_sums(ext, tmp_a, tmp_b, win, True)
        t = i * POOL_T + lax.broadcasted_iota(jnp.int32, (POOL_T, MAIN_W), 0)
        cnt = jnp.minimum(t + 1, win).astype(F32)
        p = (acc / cnt - u).astype(BF)
        p_ref[...] = p
        y = jnp.dot(p, w_ref[...], preferred_element_type=F32) * sc_ref[...]
        y_ref[...] = y.astype(y_ref.dtype)

    return pl.pallas_call(
        body, name=name, grid=(B, nt),
        in_specs=[pl.BlockSpec((POOL_T, MAIN_W), lambda b, i: (b * nt + i, 0)),
                  pl.BlockSpec((POOL_PAD, MAIN_W), lambda b, i: (jnp.maximum((b * nt + i) * hb - 1, 0), 0)),
                  pl.BlockSpec((MAIN_W, MAIN_W), lambda b, i: (0, 0)),
                  pl.BlockSpec((1, MAIN_W), lambda b, i: (0, 0))],
        out_specs=[pl.BlockSpec((POOL_T, MAIN_W), lambda b, i: (b * nt + i, 0)),
                   pl.BlockSpec((POOL_T, MAIN_W), lambda b, i: (b * nt + i, 0))],
        out_shape=[jax.ShapeDtypeStruct((T, D_MODEL), BF), jax.ShapeDtypeStruct((T, MAIN_W), BF)],
        scratch_shapes=[pltpu.VMEM((POOL_R, MAIN_W), F32)] * 3,
        compiler_params=_cp(("parallel", "parallel")),
    )(z, z, wbd, scale.reshape(1, MAIN_W))


def _pool_bwd(dy, p, wbd, scale, dz_alias, B, S, *, name):
    T = dy.shape[0]
    nt = S // POOL_T
    hb = POOL_T // POOL_HALO
    last_halo = T // POOL_HALO - 1

    def body(dy_ref, dyn_ref, p_ref, pn_ref, w_ref, sc_ref, _, dz_ref, dw_ref, ds_ref, ext, tmp_a, tmp_b, dw_acc, ds_acc):
        b, i = pl.program_id(0), pl.program_id(1)
        first = jnp.logical_and(b == 0, i == 0)
        dyv = dy_ref[...]
        pv = p_ref[...]
        sc = sc_ref[...]
        w = w_ref[...]
        pw = jnp.dot(pv, w, preferred_element_type=F32)
        ds_part = jnp.sum((dyv * pw).reshape(POOL_T // 8, 8, MAIN_W), axis=0)
        dpw = (dyv * sc).astype(BF)
        dw_part = lax.dot_general(pv, dpw, (((0,), (0,)), ((), ())), preferred_element_type=F32)

        @pl.when(first)
        def _():
            dw_acc[...] = dw_part
            ds_acc[...] = ds_part

        @pl.when(jnp.logical_not(first))
        def _():
            dw_acc[...] += dw_part
            ds_acc[...] += ds_part

        @pl.when(jnp.logical_and(b == pl.num_programs(0) - 1, i == nt - 1))
        def _():
            dw_ref[...] = dw_acc[...]
            ds_ref[...] = jnp.sum(ds_acc[...], axis=0, keepdims=True)

        dp = lax.dot_general(dpw, w, (((1,), (1,)), ((), ())), preferred_element_type=F32)
        dpn = lax.dot_general((dyn_ref[...] * sc).astype(BF), w, (((1,), (1,)), ((), ())), preferred_element_type=F32)
        win = _pool_lane_window((POOL_T, MAIN_W))
        win_n = _pool_lane_window((POOL_HALO, MAIN_W))
        t = i * POOL_T + lax.broadcasted_iota(jnp.int32, (POOL_T, MAIN_W), 0)
        tn = (i + 1) * POOL_T + lax.broadcasted_iota(jnp.int32, (POOL_HALO, MAIN_W), 0)
        ext[pl.ds(0, POOL_T), :] = dp / jnp.minimum(t + 1, win).astype(F32)
        ext[pl.ds(POOL_T, POOL_HALO), :] = jnp.where(i < nt - 1, dpn / jnp.minimum(tn + 1, win_n).astype(F32), 0.0)
        ext[pl.ds(POOL_T + POOL_HALO, POOL_PAD - POOL_HALO), :] = jnp.zeros((POOL_PAD - POOL_HALO, MAIN_W), F32)
        acc = _pool_window_sums(ext, tmp_a, tmp_b, win, False) - dp
        dz_ref[...] = acc.astype(dz_ref.dtype)

    cur = lambda b, i: (b * nt + i, 0)
    nxt = lambda b, i: (jnp.minimum((b * nt + i + 1) * hb, last_halo), 0)
    return pl.pallas_call(
        body, name=name, grid=(B, nt),
        in_specs=[pl.BlockSpec((POOL_T, MAIN_W), cur), pl.BlockSpec((POOL_HALO, MAIN_W), nxt),
                  pl.BlockSpec((POOL_T, MAIN_W), cur), pl.BlockSpec((POOL_HALO, MAIN_W), nxt),
                  pl.BlockSpec((MAIN_W, MAIN_W), lambda b, i: (0, 0)),
                  pl.BlockSpec((1, MAIN_W), lambda b, i: (0, 0)),
                  pl.BlockSpec(memory_space=pl.ANY)],
        out_specs=[pl.BlockSpec((POOL_T, MAIN_W), cur),
                   pl.BlockSpec((MAIN_W, MAIN_W), lambda b, i: (0, 0)),
                   pl.BlockSpec((1, MAIN_W), lambda b, i: (0, 0))],
        out_shape=[jax.ShapeDtypeStruct(dz_alias.shape, dz_alias.dtype),
                   jax.ShapeDtypeStruct((MAIN_W, MAIN_W), F32), jax.ShapeDtypeStruct((1, MAIN_W), F32)],
        scratch_shapes=[pltpu.VMEM((POOL_R, MAIN_W), F32)] * 3 + [pltpu.VMEM((MAIN_W, MAIN_W), F32), pltpu.VMEM((8, MAIN_W), F32)],
        input_output_aliases={6: 0},
        compiler_params=_cp(("arbitrary", "arbitrary")),
    )(dy, dy, p, p, wbd, scale.reshape(1, MAIN_W), dz_alias)


def _head_masks(shape):
    lane = lax.broadcasted_iota(jnp.int32, shape, 1)
    return [(lane // HEAD_DIM) == h for h in range(shape[1] // HEAD_DIM)]


def _row_of(bcast, mask):
    return jnp.max(jnp.where(mask, bcast, -jnp.inf), axis=-1, keepdims=True)


MEM_TQ = 2048


def _memattn_fwd(z, kv, y_alias, B, S, *, name, tq=MEM_TQ):
    T = z.shape[0]
    nt = S // tq

    def body(q_ref, k_ref, v_ref, _, y_ref, l_ref):
        q = q_ref[...]
        k = k_ref[...]
        v = v_ref[...]
        masks = _head_masks(q.shape)
        o = jnp.zeros(q.shape, F32)
        lse_b = jnp.zeros(q.shape, F32)
        for m in masks:
            qm = jnp.where(m, q, 0.0).astype(BF)
            s = lax.dot_general(qm, k, (((1,), (1,)), ((), ())), preferred_element_type=F32) * SCALE
            mx = jnp.max(s, axis=-1, keepdims=True)
            e = jnp.exp(s - mx)
            l = jnp.sum(e, axis=-1, keepdims=True)
            p = (e / l).astype(BF)
            o = o + jnp.where(m, jnp.dot(p, v, preferred_element_type=F32), 0.0)
            lse_b = lse_b + jnp.where(m, mx + jnp.log(l), 0.0)
        y_ref[...] = o.astype(y_ref.dtype)
        l_ref[...] = lse_b

    qblk = pl.BlockSpec((tq, MEM_W), lambda b, i: (b * nt + i, 3))
    return pl.pallas_call(
        body, name=name, grid=(B, nt),
        in_specs=[qblk, pl.BlockSpec((N_MEM, MEM_W), lambda b, i: (b, 0)), pl.BlockSpec((N_MEM, MEM_W), lambda b, i: (b, 1)),
                  pl.BlockSpec(memory_space=pl.ANY)],
        out_specs=[qblk, pl.BlockSpec((tq, MEM_W), lambda b, i: (b * nt + i, 0))],
        out_shape=[jax.ShapeDtypeStruct(y_alias.shape, y_alias.dtype), jax.ShapeDtypeStruct((T, MEM_W), F32)],
        input_output_aliases={3: 0},
        compiler_params=_cp(("parallel", "parallel")),
    )(z, kv, kv, y_alias)


def _memattn_bwd(dy, z, kv, lse, dz_alias, B, S, *, name, tq=MEM_TQ):
    nt = S // tq

    def body(do_ref, q_ref, k_ref, v_ref, l_ref, _, dz_ref, dk_ref, dv_ref, dk_acc, dv_acc):
        i = pl.program_id(1)
        do = do_ref[...]
        q = q_ref[...]
        k = k_ref[...]
        v = v_ref[...]
        lse_b = l_ref[...]
        masks = _head_masks(q.shape)
        dq = jnp.zeros(q.shape, F32)
        dk = jnp.zeros(k.shape, F32)
        dv = jnp.zeros(v.shape, F32)
        for m in masks:
            qm = jnp.where(m, q, 0.0).astype(BF)
            dom = jnp.where(m, do, 0.0).astype(BF)
            s = lax.dot_general(qm, k, (((1,), (1,)), ((), ())), preferred_element_type=F32) * SCALE
            p = jnp.exp(s - _row_of(lse_b, m))
            dp = lax.dot_general(dom, v, (((1,), (1,)), ((), ())), preferred_element_type=F32)
            delta = jnp.sum(p * dp, axis=-1, keepdims=True)
            ds = (p * (dp - delta) * SCALE).astype(BF)
            pb = p.astype(BF)
            dv = dv + jnp.where(m[:N_MEM], lax.dot_general(pb, dom, (((0,), (0,)), ((), ())), preferred_element_type=F32), 0.0)
            dk = dk + jnp.where(m[:N_MEM], lax.dot_general(ds, qm, (((0,), (0,)), ((), ())), preferred_element_type=F32), 0.0)
            dq = dq + jnp.where(m, jnp.dot(ds, k, preferred_element_type=F32), 0.0)
        dz_ref[...] = dq.astype(dz_ref.dtype)

        @pl.when(i == 0)
        def _():
            dk_acc[...] = dk
            dv_acc[...] = dv

        @pl.when(i > 0)
        def _():
            dk_acc[...] += dk
            dv_acc[...] += dv

        @pl.when(i == nt - 1)
        def _():
            dk_ref[...] = dk_acc[...]
            dv_ref[...] = dv_acc[...]

    qblk = pl.BlockSpec((tq, MEM_W), lambda b, i: (b * nt + i, 3))
    kblk = pl.BlockSpec((N_MEM, MEM_W), lambda b, i: (b, 0))
    return pl.pallas_call(
        body, name=name, grid=(B, nt),
        in_specs=[qblk, qblk, kblk, pl.BlockSpec((N_MEM, MEM_W), lambda b, i: (b, 1)),
                  pl.BlockSpec((tq, MEM_W), lambda b, i: (b * nt + i, 0)), pl.BlockSpec(memory_space=pl.ANY)],
        out_specs=[qblk, kblk, kblk],
        out_shape=[jax.ShapeDtypeStruct(dz_alias.shape, dz_alias.dtype),
                   jax.ShapeDtypeStruct((B * N_MEM, MEM_W), F32), jax.ShapeDtypeStruct((B * N_MEM, MEM_W), F32)],
        scratch_shapes=[pltpu.VMEM((N_MEM, MEM_W), F32), pltpu.VMEM((N_MEM, MEM_W), F32)],
        input_output_aliases={5: 0},
        compiler_params=_cp(("parallel", "arbitrary")),
    )(dy, z, kv, kv, lse, dz_alias)


N_UNITS = 16


def _unit_rows(g):
    d = DIL[g]
    nb = N_UNITS // d
    return [pl.ds(n * STEPS * d + r, STEPS, stride=d) if d > 1 else pl.ds(n * STEPS, STEPS)
            for n in range(nb) for r in range(d)]


def _load_units(ref, g):
    if DIL[g] == 1:
        return ref[...].reshape(N_UNITS, STEPS, 128)
    return jnp.stack([ref[rows, :] for rows in _unit_rows(g)])


def _store_units(ref, val, g):
    if DIL[g] == 1:
        ref[...] = val.reshape(N_UNITS * STEPS, 128)
    else:
        for u, rows in enumerate(_unit_rows(g)):
            ref[rows, :] = val[u]


def _shift_units(x, by):
    z = jnp.zeros((abs(by),) + x.shape[1:], x.dtype)
    return jnp.concatenate([z, x[:N_UNITS - by]], axis=0) if by > 0 else jnp.concatenate([x[-by:], z], axis=0)


def _bdot(a, b, ca, cb):
    return lax.dot_general(a, b, (((ca,), (cb,)), ((0,), (0,))), preferred_element_type=F32)


def _dil_masks(g):
    d = DIL[g]
    has_prev = N_UNITS // d > 1
    qi = lax.broadcasted_iota(jnp.int32, (1, STEPS, STEPS), 1)
    kj = lax.broadcasted_iota(jnp.int32, (1, STEPS, STEPS), 2)
    unit = lax.broadcasted_iota(jnp.int32, (N_UNITS, 1, 1), 0)
    cur = kj <= qi
    prev = jnp.logical_and(kj >= qi, unit >= d) if has_prev else None
    lane = lax.broadcasted_iota(jnp.int32, (1, 1, 128), 2)
    heads = [(lane // HEAD_DIM) == h for h in range(128 // HEAD_DIM)]
    return has_prev, cur, prev, heads


def _dil_fwd(g, q, k, v, o_alias, l_alias, B, S, *, name):
    assert S == N_UNITS * STEPS
    d = DIL[g]

    def body(q_ref, k_ref, v_ref, _, __, o_ref, l_ref):
        has_prev, cur, prev, heads = _dil_masks(g)
        q = _load_units(q_ref, g)
        kc = _load_units(k_ref, g).astype(BF)
        vc = _load_units(v_ref, g).astype(BF)
        if has_prev:
            kp, vp = _shift_units(kc, d), _shift_units(vc, d)
        o = jnp.zeros(q.shape, F32)
        lse_b = jnp.zeros(q.shape, F32)
        for m in heads:
            qm = jnp.where(m, q, 0.0).astype(BF)
            sc = jnp.where(cur, _bdot(qm, kc, 2, 2) * SCALE, NEG)
            mx = jnp.max(sc, axis=-1, keepdims=True)
            if has_prev:
                sp = jnp.where(prev, _bdot(qm, kp, 2, 2) * SCALE, NEG)
                mx = jnp.maximum(mx, jnp.max(sp, axis=-1, keepdims=True))
            l = jnp.sum(jnp.exp(sc - mx), axis=-1, keepdims=True)
            if has_prev:
                l = l + jnp.sum(jnp.exp(sp - mx), axis=-1, keepdims=True)
            lse = mx + jnp.log(l)
            oh = _bdot(jnp.exp(sc - lse).astype(BF), vc, 2, 1)
            if has_prev:
                oh = oh + _bdot(jnp.exp(sp - lse).astype(BF), vp, 2, 1)
            o = o + jnp.where(m, oh, 0.0)
            lse_b = lse_b + jnp.where(m, lse, 0.0)
        _store_units(o_ref, o, g)
        _store_units(l_ref, lse_b, g)

    blk = pl.BlockSpec((S, 128), lambda b, hf: (b, g * 2 + hf))
    anyspec = pl.BlockSpec(memory_space=pl.ANY)
    o, l = pl.pallas_call(
        body, name=name, grid=(B, 2),
        in_specs=[blk, blk, blk, anyspec, anyspec], out_specs=[blk, blk],
        out_shape=[jax.ShapeDtypeStruct(q.shape, F32)] * 2,
        input_output_aliases={3: 0, 4: 1},
        compiler_params=_cp(("parallel", "parallel")),
    )(q, k, v, o_alias, l_alias)
    return o, l


def _dil_bwd(g, q, k, v, do, cb, lse, aliases, B, S, *, name):
    assert S == N_UNITS * STEPS
    d = DIL[g]

    def body(q_ref, k_ref, v_ref, do_ref, c_ref, l_ref, _, __, ___, dq_ref, dk_ref, dv_ref):
        has_prev, cur, prev, heads = _dil_masks(g)
        q = _load_units(q_ref, g)
        kc = _load_units(k_ref, g).astype(BF)
        vc = _load_units(v_ref, g).astype(BF)
        do = _load_units(do_ref, g)
        cbv = _load_units(c_ref, g)
        lse_b = _load_units(l_ref, g)
        if has_prev:
            kp, vp = _shift_units(kc, d), _shift_units(vc, d)
        z = jnp.zeros(q.shape, F32)
        dq, dkc, dkp, dvc, dvp = z, z, z, z, z
        for m in heads:
            qm = jnp.where(m, q, 0.0).astype(BF)
            dom = jnp.where(m, do, 0.0).astype(BF)
            lse = jnp.max(jnp.where(m, lse_b, -jnp.inf), axis=-1, keepdims=True)
            c = jnp.max(jnp.where(m, cbv, -jnp.inf), axis=-1, keepdims=True)
            sc = jnp.where(cur, _bdot(qm, kc, 2, 2) * SCALE, NEG)
            pc = jnp.exp(sc - lse)
            dsc = (pc * (_bdot(dom, vc, 2, 2) + c) * SCALE).astype(BF)
            dqh = _bdot(dsc, kc, 2, 1)
            dkc = dkc + jnp.where(m, _bdot(dsc, qm, 1, 1), 0.0)
            dvc = dvc + jnp.where(m, _bdot(pc.astype(BF), dom, 1, 1), 0.0)
            if has_prev:
                sp = jnp.where(prev, _bdot(qm, kp, 2, 2) * SCALE, NEG)
                pp = jnp.exp(sp - lse)
                dsp = (pp * (_bdot(dom, vp, 2, 2) + c) * SCALE).astype(BF)
                dqh = dqh + _bdot(dsp, kp, 2, 1)
                dkp = dkp + jnp.where(m, _bdot(dsp, qm, 1, 1), 0.0)
                dvp = dvp + jnp.where(m, _bdot(pp.astype(BF), dom, 1, 1), 0.0)
            dq = dq + jnp.where(m, dqh, 0.0)
        if has_prev:
            dkc = dkc + _shift_units(dkp, -d)
            dvc = dvc + _shift_units(dvp, -d)
        _store_units(dq_ref, dq, g)
        _store_units(dk_ref, dkc, g)
        _store_units(dv_ref, dvc, g)

    blk = pl.BlockSpec((S, 128), lambda b, hf: (b, g * 2 + hf))
    anyspec = pl.BlockSpec(memory_space=pl.ANY)
    return tuple(pl.pallas_call(
        body, name=name, grid=(B, 2),
        in_specs=[blk] * 6 + [anyspec] * 3, out_specs=[blk] * 3,
        out_shape=[jax.ShapeDtypeStruct(q.shape, F32)] * 3,
        input_output_aliases={6: 0, 7: 1, 8: 2},
        compiler_params=_cp(("parallel", "parallel")),
    )(q, k, v, do, cb, lse, *aliases))


def _kv_grad_sum(parts, cos, sin, *, name, tm=512):
    T = parts[0][0].shape[0]
    n_l = len(parts)

    def body(*refs):
        c_ref, s_ref = refs[0], refs[1]
        dk_ref, dv_ref = refs[2 + 2 * n_l:]
        dk = refs[2][...]
        dv = refs[3][...]
        for li in range(1, n_l):
            dk = dk + refs[2 + 2 * li][...]
            dv = dv + refs[3 + 2 * li][...]
        dk_ref[...] = _rot(dk, c_ref[...], s_ref[...], -1.0).astype(dk_ref.dtype)
        dv_ref[...] = dv.astype(dv_ref.dtype)

    full = pl.BlockSpec((tm, MAIN_W), lambda i: (i, 0))
    tab = pl.BlockSpec((tm, 128), lambda i: (i, 0))
    ops = [cos, sin] + [t for part in parts for t in part]
    return pl.pallas_call(
        body, name=name, grid=(T // tm,), in_specs=[tab, tab] + [full] * (2 * n_l), out_specs=[full, full],
        out_shape=[jax.ShapeDtypeStruct((T, MAIN_W), BF)] * 2,
        compiler_params=_cp(("parallel",)),
    )(*ops)


def _group_softmax(lse):
    l0, l1, l2 = lse[:, 0:256], lse[:, 256:512], lse[:, 512:768]
    mx = jnp.maximum(jnp.maximum(l0, l1), l2)
    e0, e1, e2 = jnp.exp(l0 - mx), jnp.exp(l1 - mx), jnp.exp(l2 - mx)
    tot = e0 + e1 + e2
    return e0 / tot, e1 / tot, e2 / tot


def _dil_combine_fwd(o, lse, y_alias, *, name, tm=512):
    T = o.shape[0]

    def body(o_ref, l_ref, _, y_ref):
        a = jnp.concatenate(_group_softmax(l_ref[...]), axis=1)
        y_ref[...] = (o_ref[...] * a).astype(y_ref.dtype)

    blk = pl.BlockSpec((tm, MAIN_W), lambda i: (i, 0))
    return pl.pallas_call(
        body, name=name, grid=(T // tm,), in_specs=[blk, blk, pl.BlockSpec(memory_space=pl.ANY)], out_specs=blk,
        out_shape=jax.ShapeDtypeStruct(y_alias.shape, y_alias.dtype), input_output_aliases={2: 0},
        compiler_params=_cp(("parallel",)),
    )(o, lse, y_alias)


def _dil_combine_bwd(dy, o, lse, *, name, tm=256):
    T = o.shape[0]
    lane_r = lax.broadcasted_iota(jnp.int32, (256, 256), 0) // HEAD_DIM
    lane_c = lax.broadcasted_iota(jnp.int32, (256, 256), 1) // HEAD_DIM
    ones_bd = (lane_r == lane_c).astype(BF)

    def body(dy_ref, o_ref, l_ref, e_ref, do_ref, c_ref):
        dyv = dy_ref[...]
        alphas = _group_softmax(l_ref[...])
        prod = dyv * o_ref[...]
        e = e_ref[...]
        tot = jnp.zeros((tm, 256), F32)
        for gi in range(3):
            x = prod[:, gi * 256:(gi + 1) * 256]
            hi = x.astype(BF)
            lo = (x - hi.astype(F32)).astype(BF)
            dalpha = jnp.dot(hi, e, preferred_element_type=F32) + jnp.dot(lo, e, preferred_element_type=F32)
            tot = tot + alphas[gi] * dalpha
        a = jnp.concatenate(alphas, axis=1)
        do_ref[...] = (dyv * a).astype(do_ref.dtype)
        c_ref[...] = jnp.concatenate([-al * tot for al in alphas], axis=1)

    blk = pl.BlockSpec((tm, MAIN_W), lambda i: (i, 0))
    return pl.pallas_call(
        body, name=name, grid=(T // tm,),
        in_specs=[blk, blk, blk, pl.BlockSpec((256, 256), lambda i: (0, 0))], out_specs=[blk, blk],
        out_shape=[jax.ShapeDtypeStruct((T, MAIN_W), F32), jax.ShapeDtypeStruct((T, MAIN_W), F32)],
        compiler_params=_cp(("parallel",)),
    )(dy, o, lse, ones_bd)


def _loss(y, target, *, name, tm=512):
    T, Dm = y.shape
    nt = T // tm

    def body(y_ref, t_ref, l_ref, d_ref, acc):
        i = pl.program_id(0)
        err = y_ref[...] - t_ref[...]
        d_ref[...] = err / Dm
        part = jnp.sum(jnp.mean(err * err, axis=-1, keepdims=True).reshape(tm // 8, 8, 1), axis=0)

        @pl.when(i == 0)
        def _():
            acc[...] = part

        @pl.when(i > 0)
        def _():
            acc[...] += part

        @pl.when(i == nt - 1)
        def _():
            l_ref[...] = 0.5 * jnp.sum(acc[...], axis=0, keepdims=True)

    row = pl.BlockSpec((tm, Dm), lambda i: (i, 0))
    return pl.pallas_call(
        body, name=name, grid=(nt,), in_specs=[row, row],
        out_specs=[pl.BlockSpec((1, 1), lambda i: (0, 0)), row],
        out_shape=[jax.ShapeDtypeStruct((1, 1), F32), jax.ShapeDtypeStruct((T, Dm), F32)],
        scratch_shapes=[pltpu.VMEM((8, 1), F32)],
        compiler_params=_cp(("arbitrary",)),
    )(y, target)


def _adamw(w, g, m, v, *, name):
    shape = w.shape
    cols = shape[-1]
    rows = w.size // cols
    tm = rows
    for cand in (512, 352, 256, 128):
        if rows > cand and rows % cand == 0 and cand * cols * 4 <= (1 << 20):
            tm = cand
            break

    def body(w_ref, g_ref, m_ref, v_ref, d_ref, mo_ref, vo_ref):
        gv = g_ref[...]
        mn = ADAM_B1 * m_ref[...] + (1.0 - ADAM_B1) * gv
        vn = ADAM_B2 * v_ref[...] + (1.0 - ADAM_B2) * (gv * gv)
        m_hat = mn / (1.0 - ADAM_B1 ** ADAM_STEP)
        v_hat = vn / (1.0 - ADAM_B2 ** ADAM_STEP)
        d_ref[...] = -ADAM_LR * (m_hat / (jnp.sqrt(v_hat) + ADAM_EPS) + ADAM_WD * w_ref[...])
        mo_ref[...] = mn
        vo_ref[...] = vn

    blk = pl.BlockSpec((tm, cols), lambda i: (i, 0))
    outs = pl.pallas_call(
        body, name=name, grid=(rows // tm,), in_specs=[blk] * 4, out_specs=[blk] * 3,
        out_shape=[jax.ShapeDtypeStruct((rows, cols), F32)] * 3,
        compiler_params=_cp(("parallel",)),
    )(*[t.reshape(rows, cols) for t in (w, g, m, v)])
    return tuple(t.reshape(shape) for t in outs)


def _adamw_layer(name, l, w, g, m, v, prev, after=None):
    L, rows, cols = w.shape
    tm = rows
    for cand in (512, 352, 256, 176, 128, 64):
        if rows % cand == 0 and cand * cols * 4 <= (1 << 21):
            tm = cand
            break
    if prev is None:
        prev = tuple(lax.empty(w.shape, F32) for _ in range(4))

    n_after = 0 if after is None else 1

    def body(w_ref, g_ref, m_ref, v_ref, *rest):
        d_ref, mo_ref, vo_ref, go_ref = rest[4 + n_after:]
        gv = g_ref[...]
        mn = ADAM_B1 * m_ref[...] + (1.0 - ADAM_B1) * gv
        vn = ADAM_B2 * v_ref[...] + (1.0 - ADAM_B2) * (gv * gv)
        m_hat = mn / (1.0 - ADAM_B1 ** ADAM_STEP)
        v_hat = vn / (1.0 - ADAM_B2 ** ADAM_STEP)
        d_ref[...] = -ADAM_LR * (m_hat / (jnp.sqrt(v_hat) + ADAM_EPS) + ADAM_WD * w_ref[...])
        mo_ref[...] = mn
        vo_ref[...] = vn
        go_ref[...] = gv

    lay = pl.BlockSpec((None, tm, cols), lambda i: (l, i, 0))
    one = pl.BlockSpec((None, tm, cols), lambda i: (0, i, 0))
    return tuple(pl.pallas_call(
        body, name=f"l{l}_adamw_{name}", grid=(rows // tm,),
        in_specs=[lay, one, lay, lay] + [pl.BlockSpec(memory_space=pl.ANY)] * (4 + n_after), out_specs=[lay] * 4,
        out_shape=[jax.ShapeDtypeStruct(w.shape, F32)] * 4,
        input_output_aliases={4 + i: i for i in range(4)},
        compiler_params=_cp(("parallel",)),
    )(w, g, m, v, *prev, *([] if after is None else [after])))


BIG = {
    'w_in': ((DEPTH, D_MODEL, D_MODEL), 'row'),
    'w_mem_kv': ((DEPTH, D_MODEL, 2 * MEM_W), 'row'),
    'w_out': ((DEPTH, D_MODEL, D_MODEL), 'row'),
    'w_kv': ((1, D_MODEL, 2 * MAIN_W), 'col'),
    'w_gate_up': ((DEPTH, D_MODEL, 2 * D_FF), 'col'),
    'w_down': ((DEPTH, D_FF, D_MODEL), 'row'),
}
BIG_NAMES = tuple(BIG)
N_CHIPS = 4
HBM_ANY = pl.BlockSpec(memory_space=pl.ANY)


def _geom(name):
    (L, R, C), kind = BIG[name]
    if kind == 'row':
        return L, R, C, kind, R // N_CHIPS, C, R // (2 * N_CHIPS)
    return L, R, C, kind, R, C // N_CHIPS, R // 2


def _shard_shape(name):
    L, R, C, kind, rs, cs, rh = _geom(name)
    return (L, rs, cs)


def _half_shape(name):
    L, R, C, kind, rs, cs, rh = _geom(name)
    return (L, rh, cs)


def _full_win(ref, name, s, h):
    L, R, C, kind, rs, cs, rh = _geom(name)
    if kind == 'row':
        rows = pl.ds(s * rs, rs) if h is None else pl.ds(s * rs + h * rh, rh)
        return ref.at[:, rows, :]
    rows = slice(None) if h is None else pl.ds(h * rh, rh)
    return ref.at[:, rows, pl.ds(s * cs, cs)]


def _shard_half(ref, name, h):
    L, R, C, kind, rs, cs, rh = _geom(name)
    return ref.at[:, pl.ds(h * rh, rh), :]


def _halves_win(ref, name, s):
    L, R, C, kind, rs, cs, rh = _geom(name)
    if kind == 'row':
        return ref.at[:, pl.ds(s * rh, rh), :]
    return ref.at[:, :, pl.ds(s * cs, cs)]


def _halves_shape(name):
    L, R, C, kind, rs, cs, rh = _geom(name)
    return (L, N_CHIPS * rh, cs) if kind == 'row' else (L, rh, C)


def _place():
    x, y, c = lax.axis_index("x"), lax.axis_index("y"), lax.axis_index("c")
    chips = [(1 - x, y), (x, 1 - y), (1 - x, 1 - y)]
    return x, y, c, chips


SMALL_ROWS = 24


SEM_SPEC = pl.BlockSpec(memory_space=pltpu.SEMAPHORE)
HBM_SPEC = pl.BlockSpec(memory_space=pltpu.HBM)
DATAFLOW = pltpu.SideEffectType.DATAFLOW_SIDE_EFFECTING


def _in_hbm(a):
    return pltpu.with_memory_space_constraint(a, pltpu.HBM)


def _remote(src, dst, send_sems, recv_sems, k, to):
    return pltpu.make_async_remote_copy(src_ref=src, dst_ref=dst, send_sem=send_sems.at[k], recv_sem=recv_sems.at[k],
                                        device_id=to, device_id_type=MESH)


def _split_start(name, bufs, n_copies, sends, after=None):
    nb = len(bufs)
    n_in = nb + (0 if after is None else 1)

    def body(*refs):
        in_refs = refs[:nb]
        send_sems, recv_sems = refs[n_in], refs[n_in + 1]
        token = refs[-1]
        for k, (src, dst, to) in enumerate(sends(in_refs)):
            _remote(src, dst, send_sems, recv_sems, k, to).start()
        token[...] = jnp.zeros_like(token)

    outs = pl.pallas_call(
        body, name=name,
        out_shape=(pltpu.SemaphoreType.DMA((n_copies,)), pltpu.SemaphoreType.DMA((n_copies,)),
                   *[pltpu.HBM(b.shape, b.dtype) for b in bufs], jax.ShapeDtypeStruct((8, 128), F32)),
        in_specs=[HBM_SPEC] * nb + [HBM_ANY] * (n_in - nb),
        out_specs=(SEM_SPEC, SEM_SPEC, *[HBM_SPEC] * nb, pl.BlockSpec(memory_space=pltpu.VMEM)),
        input_output_aliases={i: 2 + i for i in range(nb)},
        compiler_params=pltpu.CompilerParams(has_side_effects=DATAFLOW),
    )(*[_in_hbm(b) for b in bufs], *([] if after is None else [after]))
    return outs[0], outs[1], list(outs[2:2 + nb]), outs[-1]


def _split_wait(name, send_sems, recv_sems, bufs, after, sends, arrivals):
    nb = len(bufs)

    def body(*refs):
        in_refs = refs[:nb]
        s_sems, r_sems = refs[nb], refs[nb + 1]
        me = (lax.axis_index("x"), lax.axis_index("y"), lax.axis_index("c"))
        for k, (src, dst, to) in enumerate(sends(in_refs)):
            _remote(src, dst, s_sems, r_sems, k, to).wait_send()
        for k, win in enumerate(arrivals(in_refs)):
            _remote(win, win, s_sems, r_sems, k, me).wait_recv()

    outs = pl.pallas_call(
        body, name=name,
        out_shape=[pltpu.HBM(b.shape, b.dtype) for b in bufs],
        in_specs=[HBM_SPEC] * nb + [SEM_SPEC, SEM_SPEC, HBM_ANY],
        out_specs=[HBM_SPEC] * nb,
        input_output_aliases={i: i for i in range(nb)},
        compiler_params=pltpu.CompilerParams(has_side_effects=DATAFLOW),
    )(*bufs, send_sems, recv_sems, after)
    return list(outs)


MIX_W = ('w_in', 'w_mem_kv', 'w_out')
FFN_W = ('w_gate_up', 'w_down')
LAYER_W = MIX_W + FFN_W


def _place_own(tag, names, sources, small, sc):
    nw = len(names)
    has_small = small is not None

    def body(sc_ref, *refs):
        srcs = refs[:nw]
        shard_out = refs[nw + has_small:2 * nw + has_small]
        full_out = refs[2 * nw + has_small:3 * nw + has_small]
        for src, sh, fu in zip(srcs, shard_out, full_out):
            v = src[...].astype(BF)
            sh[...] = v
            fu[...] = v
        if has_small:
            refs[-1][...] = refs[nw][...]

    in_specs, shard_specs, full_specs, shard_shape, full_shape, ops = [], [], [], [], [], []
    for nm, (arr, layer) in zip(names, sources):
        L, R, C, kind, rs, cs, rh = _geom(nm)
        in_specs.append(pl.BlockSpec((1, rs, cs), lambda i, sc_ref, layer=layer: (layer, 0, 0)))
        shard_specs.append(pl.BlockSpec((1, rs, cs), lambda i, sc_ref: (0, 0, 0)))
        if kind == 'row':
            full_specs.append(pl.BlockSpec((1, rs, cs), lambda i, sc_ref: (0, sc_ref[0], 0)))
        else:
            full_specs.append(pl.BlockSpec((1, rs, cs), lambda i, sc_ref: (0, 0, sc_ref[0])))
        shard_shape.append(jax.ShapeDtypeStruct((1, rs, cs), BF))
        full_shape.append(jax.ShapeDtypeStruct((1, R, C), BF))
        ops.append(arr)
    if has_small:
        in_specs.append(pl.BlockSpec((SMALL_ROWS, 256), lambda i, sc_ref: (0, 0)))
        full_specs.append(pl.BlockSpec((None, SMALL_ROWS, 256), lambda i, sc_ref: (sc_ref[0], 0, 0)))
        full_shape.append(jax.ShapeDtypeStruct((N_CHIPS, SMALL_ROWS, 256), F32))
        ops.append(small)
    outs = pl.pallas_call(
        body, name=f"{tag}_place_own_shard",
        grid_spec=pltpu.PrefetchScalarGridSpec(num_scalar_prefetch=1, grid=(1,), in_specs=in_specs,
                                               out_specs=shard_specs + full_specs),
        out_shape=shard_shape + full_shape,
        compiler_params=_cp(("arbitrary",)),
    )(sc, *ops)
    return list(outs[:nw]), list(outs[nw:])


def _gather_start(l, names, sources, small, sc, after=None):
    nw = len(names)
    has_small = small is not None
    shards, fulls = _place_own(l, names, sources, small, sc)
    bufs = list(shards) + ([small] if has_small else []) + list(fulls)
    n_src = nw + (1 if has_small else 0)

    def sends(refs):
        x, y, c, chips = _place()
        s = 2 * x + y
        out = []
        for (px, py) in chips:
            for wi, nm in enumerate(names):
                out.append((_shard_half(refs[wi], nm, c), _full_win(refs[n_src + wi], nm, s, c), (px, py, c)))
            if has_small:
                out.append((refs[nw], refs[n_src + nw].at[s], (px, py, c)))
        return out

    def arrivals(refs):
        x, y, c, chips = _place()
        out = []
        for (px, py) in chips:
            sp = 2 * px + py
            for wi, nm in enumerate(names):
                out.append(_full_win(refs[n_src + wi], nm, sp, c))
            if has_small:
                out.append(refs[n_src + nw].at[sp])
        return out

    n_copies = 3 * n_src
    send_sems, recv_sems, bufs, token = _split_start(f"{l}_gather_ici_start", bufs, n_copies, sends, after)
    return dict(l=l, names=names, has_small=has_small, sems=(send_sems, recv_sems), bufs=bufs, sends=sends,
                arrivals=arrivals, token=token)


def _gather_forward(st, after):
    l, names = st['l'], st['names']
    nw = len(names)
    n_src = nw + (1 if st['has_small'] else 0)
    bufs = _split_wait(f"{l}_gather_ici_wait", *st['sems'], st['bufs'], after, st['sends'], st['arrivals'])
    fulls = bufs[n_src:n_src + nw]
    small_all = bufs[n_src + nw] if st['has_small'] else None

    def sends(refs):
        x, y, c, chips = _place()
        out = []
        for (px, py) in chips:
            sp = 2 * px + py
            for wi, nm in enumerate(names):
                w = _full_win(refs[wi], nm, sp, c)
                out.append((w, w, (x, y, 1 - c)))
        return out

    def arrivals(refs):
        x, y, c, chips = _place()
        out = []
        for (px, py) in chips:
            sp = 2 * px + py
            for wi, nm in enumerate(names):
                out.append(_full_win(refs[wi], nm, sp, 1 - c))
        return out

    send_sems, recv_sems, fulls, token = _split_start(f"{l}_gather_d2d_start", fulls, 3 * nw, sends)
    return dict(l=l, names=names, sems=(send_sems, recv_sems), bufs=fulls, sends=sends, arrivals=arrivals,
                small_all=small_all, token=token)


def _gather_finish(st, after):
    fulls = _split_wait(f"{st['l']}_gather_d2d_wait", *st['sems'], st['bufs'], after, st['sends'], st['arrivals'])
    return dict(zip(st['names'], fulls)), st['small_all']


def _reduce_start(tag, names, grads):
    nw = len(names)
    recv = [lax.empty((1,) + _halves_shape(nm)[1:], F32) for nm in names]
    bufs = [grads[nm] for nm in names] + recv

    def windows(refs, half_of):
        x, y, c, _ = _place()
        h = half_of(c)
        out = []
        for wi, nm in enumerate(names):
            L, R, C, kind, rs, cs, rh = _geom(nm)
            if kind == 'row':
                for sp in range(N_CHIPS):
                    out.append((_full_win(refs[wi], nm, sp, h), _halves_win(refs[nw + wi], nm, sp)))
            else:
                out.append((refs[wi].at[:, pl.ds(h * rh, rh), :], refs[nw + wi]))
        return out

    def sends(refs):
        x, y, c, _ = _place()
        return [(src, dst, (x, y, 1 - c)) for src, dst in windows(refs, lambda c: 1 - c)]

    def arrivals(refs):
        return [dst for _, dst in windows(refs, lambda c: c)]

    n_copies = sum(N_CHIPS if BIG[nm][1] == 'row' else 1 for nm in names)
    send_sems, recv_sems, bufs, token = _split_start(tag + "_halves_start", bufs, n_copies, sends)
    return dict(tag=tag, names=names, sems=(send_sems, recv_sems), bufs=bufs, sends=sends, arrivals=arrivals, token=token)


def _reduce_mid(st, after, sc):
    tag, names = st['tag'], st['names']
    nw = len(names)
    bufs = _split_wait(tag + "_halves_wait", *st['sems'], st['bufs'], after, st['sends'], st['arrivals'])
    halves, own = [], []
    for wi, nm in enumerate(names):
        hb, ow = _add_halves(nm, bufs[wi], bufs[nw + wi], sc, tag)
        halves.append(hb)
        own.append(ow)
    pieces = [lax.empty((3, 1) + _half_shape(nm)[1:], BF) for nm in names]

    def sends(refs):
        x, y, c, chips = _place()
        out = []
        for j, (px, py) in enumerate(chips):
            for wi, nm in enumerate(names):
                out.append((_halves_win(refs[wi], nm, 2 * px + py), refs[nw + wi].at[j], (px, py, c)))
        return out

    def arrivals(refs):
        return [refs[nw + wi].at[j] for j in range(3) for wi in range(nw)]

    send_sems, recv_sems, bufs, token = _split_start(tag + "_pieces_start", halves + pieces, 3 * nw, sends)
    return dict(tag=tag, names=names, sems=(send_sems, recv_sems), bufs=bufs, sends=sends, arrivals=arrivals, own=own,
                token=token)


def _reduce_late(st, after, sc):
    tag, names = st['tag'], st['names']
    nw = len(names)
    bufs = _split_wait(tag + "_pieces_wait", *st['sems'], st['bufs'], after, st['sends'], st['arrivals'])
    gsh = [_sum_pieces(nm, st['own'][wi], bufs[nw + wi], sc, tag) for wi, nm in enumerate(names)]

    def sends(refs):
        x, y, c, _ = _place()
        return [(_shard_half(refs[wi], nm, c), _shard_half(refs[wi], nm, c), (x, y, 1 - c)) for wi, nm in enumerate(names)]

    def arrivals(refs):
        x, y, c, _ = _place()
        return [_shard_half(refs[wi], nm, 1 - c) for wi, nm in enumerate(names)]

    send_sems, recv_sems, bufs, token = _split_start(tag + "_share_start", gsh, nw, sends)
    return dict(tag=tag, names=names, sems=(send_sems, recv_sems), bufs=bufs, sends=sends, arrivals=arrivals, token=token)


def _reduce_finish(st, after):
    gsh = _split_wait(st['tag'] + "_share_wait", *st['sems'], st['bufs'], after, st['sends'], st['arrivals'])
    return dict(zip(st['names'], gsh))


def _add_halves(name, g, r, sc, tag):
    _, R, C, kind, rs, cs, rh = _geom(name)
    L = g.shape[0]
    tr = rh
    nr = rh // tr

    def body(sc_ref, g_ref, r_ref, hb_ref, own_ref):
        sp = pl.program_id(2)
        tot = g_ref[...] + r_ref[...]
        hb_ref[...] = tot.astype(hb_ref.dtype)

        @pl.when(sp == sc_ref[0])
        def _():
            own_ref[...] = tot

    if kind == 'row':
        g_map = lambda l, ri, sp, sc_ref: (l, sp * 2 + sc_ref[1], 0)
        h_map = lambda l, ri, sp, sc_ref: (l, sp, 0)
    else:
        g_map = lambda l, ri, sp, sc_ref: (l, sc_ref[1] * nr + ri, sp)
        h_map = lambda l, ri, sp, sc_ref: (l, ri, sp)
    own_map = lambda l, ri, sp, sc_ref: (l, ri, 0)
    blk = (None, tr, cs)
    return pl.pallas_call(
        body, name=tag + "_add_halves_" + name,
        grid_spec=pltpu.PrefetchScalarGridSpec(
            num_scalar_prefetch=1, grid=(L, nr, N_CHIPS),
            in_specs=[pl.BlockSpec(blk, g_map), pl.BlockSpec(blk, h_map)],
            out_specs=[pl.BlockSpec(blk, h_map), pl.BlockSpec(blk, own_map)]),
        out_shape=[jax.ShapeDtypeStruct((L,) + _halves_shape(name)[1:], BF),
                   jax.ShapeDtypeStruct((L,) + _half_shape(name)[1:], F32)],
        compiler_params=_cp(("parallel", "parallel", "arbitrary")),
    )(sc, g, r)


def _sum_pieces(name, own, pieces, sc, tag):
    _, R, C, kind, rs, cs, rh = _geom(name)
    L = own.shape[0]
    tr = rh
    nr = rh // tr

    def body(sc_ref, o_ref, p_ref, out_ref):
        out_ref[...] = o_ref[...] + p_ref[0].astype(F32) + p_ref[1].astype(F32) + p_ref[2].astype(F32)

    blk = (None, tr, cs)
    return pl.pallas_call(
        body, name=tag + "_sum_pieces_" + name,
        grid_spec=pltpu.PrefetchScalarGridSpec(
            num_scalar_prefetch=1, grid=(L, nr),
            in_specs=[pl.BlockSpec(blk, lambda l, ri, sc_ref: (l, ri, 0)),
                      pl.BlockSpec((3, None, tr, cs), lambda l, ri, sc_ref: (0, l, ri, 0))],
            out_specs=pl.BlockSpec(blk, lambda l, ri, sc_ref: (l, sc_ref[1] * nr + ri, 0))),
        out_shape=jax.ShapeDtypeStruct((L,) + _shard_shape(name)[1:], F32),
        compiler_params=_cp(("parallel", "parallel")),
    )(sc, own, pieces)


def _small_gather_start(v, sc):
    rows = v.shape[0]

    def place(sc_ref, v_ref, o_ref):
        o_ref[...] = v_ref[...]

    slots = pl.pallas_call(
        place, name="small_grads_place_own",
        grid_spec=pltpu.PrefetchScalarGridSpec(
            num_scalar_prefetch=1, grid=(1,),
            in_specs=[pl.BlockSpec((rows, 128), lambda i, sc_ref: (0, 0))],
            out_specs=pl.BlockSpec((None, rows, 128), lambda i, sc_ref: (2 * sc_ref[0] + sc_ref[1], 0, 0))),
        out_shape=jax.ShapeDtypeStruct((8, rows, 128), v.dtype),
        compiler_params=_cp(("arbitrary",)),
    )(sc, v)

    def peers():
        x, y, c, _ = _place()
        flips = [(fx, fy, fc) for fx in (0, 1) for fy in (0, 1) for fc in (0, 1)][1:]
        return [((1 - x if fx else x), (1 - y if fy else y), (1 - c if fc else c)) for fx, fy, fc in flips]

    def sends(refs):
        x, y, c, _ = _place()
        return [(refs[0], refs[1].at[4 * x + 2 * y + c], p) for p in peers()]

    def arrivals(refs):
        return [refs[1].at[4 * px + 2 * py + pc] for px, py, pc in peers()]

    send_sems, recv_sems, bufs, token = _split_start("small_grads_gather_start", [v, slots], 7, sends)
    return dict(sems=(send_sems, recv_sems), bufs=bufs, sends=sends, arrivals=arrivals, token=token)


def _small_gather_finish(st, after):
    return _split_wait("small_grads_gather_wait", *st['sems'], st['bufs'], after, st['sends'], st['arrivals'])[1]


def _sum8(v8, *, name, tr=336):
    rows = v8.shape[1]
    tr = min(tr, rows)
    assert rows % tr == 0

    def body(v_ref, o_ref):
        tot = v_ref[0].astype(F32)
        for d in range(1, 8):
            tot = tot + v_ref[d].astype(F32)
        o_ref[...] = tot

    return pl.pallas_call(
        body, name=name, grid=(rows // tr,),
        in_specs=[pl.BlockSpec((8, tr, 128), lambda i: (0, i, 0))], out_specs=pl.BlockSpec((tr, 128), lambda i: (i, 0)),
        out_shape=jax.ShapeDtypeStruct((rows, 128), F32),
        compiler_params=_cp(("parallel",)),
    )(v8)


def _block_diag(w_pool_l):
    wbd = jnp.zeros((MAIN_W, MAIN_W), F32)
    for gi in range(len(POOL_WINDOWS)):
        wbd = lax.dynamic_update_slice(wbd, w_pool_l[gi], (gi * POOL_GROUP, gi * POOL_GROUP))
    return wbd.astype(BF)


def _unpack_small(small_all):
    ng = small_all[:, :16, :].reshape(N_CHIPS, DEPTH, 4, 256).transpose(1, 2, 0, 3).reshape(DEPTH, 4, D_MODEL)
    ps = small_all[:, 16:18, :POOL_GROUP].transpose(1, 0, 2).reshape(N_A, MAIN_W)
    return ng, ps


def _local_step(x, mem, positions, on_forward, on_backward, mem_norm, w_pool, kv_norm, target):
    B, S, _ = x.shape
    T = B * S
    xc = x.reshape(T, D_MODEL)
    memf = mem.reshape(B * N_MEM, D_MODEL)
    tgt = target.reshape(T, D_MODEL)
    cos, sin = _rope_tables(positions.reshape(T, 1), name="rope_tables")
    wbd = [_block_diag(w_pool[l]) for l in range(N_A)]
    nbo = D_FF // 256
    fw = []
    rk = rv = None
    kv_saved = None
    wts = []
    norm_gains = pool_scale = y2 = None

    for l in range(DEPTH):
        t = f"l{l}_"
        got = on_forward('start', l, y2)
        wts.append(dict(got[0]))
        if l == 0:
            norm_gains, pool_scale = _unpack_small(got[1])
        sv = {'x_in': xc}
        h0 = _norm_fwd(xc, norm_gains[l, 0], name=t + "norm0", out_dtype=BF, tm=1024, after=got[2])
        z, = _mm(h0, wts[l]['w_in'], b_layer=0, name=t + "mm_in", tm=1024, tn=1024)
        memn = _norm_fwd(memf, mem_norm[l], name=t + "norm_mem", out_dtype=BF, tm=256)
        kvm, = _mm(memn, wts[l]['w_mem_kv'], b_layer=0, name=t + "mm_memkv", out_dtypes=(BF,))
        if l < N_A:
            ycat, sv['p'] = _pool_fwd(z, wbd[l], pool_scale[l], B, S, name=t + "pool_fwd")
        else:
            rq = _rope_apply(z, cos, sin, name=t + "rope_q", out_dtype=F32, tm=1024)
            o = lax.empty((T, MAIN_W), F32)
            lse = lax.empty((T, MAIN_W), F32)
            for g in range(3):
                o, lse = _dil_fwd(g, rq, rk, rv, o, lse, B, S, name=t + f"dil_fwd{g}")
            ycat = _dil_combine_fwd(o, lse, lax.empty((T, D_MODEL), BF), name=t + "dil_combine", tm=1024)
            sv.update(rq=rq, o=o, lse=lse)
        ycat, sv['lse_m'] = _memattn_fwd(z, kvm, ycat, B, S, name=t + "memattn_fwd")
        tok = on_forward('mid', l, ycat)
        y1, = _mm(ycat, wts[l]['w_out'], b_layer=0, name=t + "mm_out", tm=1024, tn=1024)
        wts[l].update(on_forward('ffn', l, y1)[0])
        x1 = _norm_fwd(y1, norm_gains[l, 1], name=t + "norm1", res=xc, after=tok)
        h2 = _norm_fwd(x1, norm_gains[l, 2], name=t + "norm2", out_dtype=BF, tm=1024)
        gg, uu, aa = _mm(h2, wts[l]['w_gate_up'], b_layer=0, b_offsets=(0, nbo), out_n=D_FF, tm=4096, tn=256, name=t + "mm_gate_up",
                         epilogue=_swiglu_fwd_epilogue, out_dtypes=(BF, BF, BF))
        on_forward('post', l, gg)
        y2, = _mm(aa, wts[l]['w_down'], b_layer=0, tk=D_FF, name=t + "mm_down")
        x2 = _norm_fwd(y2, norm_gains[l, 3], name=t + "norm3", res=x1)
        sv.update(h0=h0, z=z, memn=memn, kvm=kvm, ycat=ycat, y1=y1, x1=x1, h2=h2, gg=gg, uu=uu, aa=aa, y2=y2)
        fw.append(sv)
        xc = x2
        if l == N_A - 1:
            kvn = _norm_fwd(xc, kv_norm, name="norm_kv", out_dtype=BF, tm=1024)
            kv, = _mm(kvn, wts[N_A - 1]['w_kv'], b_layer=0, name="mm_kv")
            rk, rv = _rope_apply(kv, cos, sin, name="rope_k", passthrough=True, out_dtype=F32, tm=1024)
            kv_saved = (xc, kvn)

    loss, dx = _loss(xc, tgt, name="loss", tm=1024)

    d_ng = [[None] * 4 for _ in range(DEPTH)]
    d_memnorm = [None] * DEPTH
    d_wbd = [None] * N_A
    d_pscale = [None] * N_A
    d_kvnorm = None
    kv_parts = []
    tok = None

    def as3d(gl):
        return {nm: g.reshape((1,) + g.shape) for nm, g in gl.items()}

    for l in reversed(range(DEPTH)):
        t = f"l{l}_b_"
        sv = fw[l]
        gl = {}
        dy2, d_ng[l][3] = _norm_bwd(dx, sv['y2'], norm_gains[l, 3], name=t + "norm3", out_dtype=BF, tm=1024, after=tok)
        gl['w_down'], = _mm(sv['aa'], dy2, ta=True, tm=1408, tn=512, tk=4096, name=t + "dw_down")
        dg, du = _mm(dy2, wts[l]['w_down'], tb=True, b_layer=0, tm=1024, tn=1408, name=t + "d_act",
                     extras=((sv['gg'], 'tile'), (sv['uu'], 'tile')), epilogue=_swiglu_bwd_epilogue, out_dtypes=(BF, BF))
        gl['w_gate_up'], = _mm(sv['h2'], (dg, du), ta=True, tn=1408, tk=1024, name=t + "dw_gate_up")
        dh2, = _mm((dg, du), wts[l]['w_gate_up'], tb=True, b_layer=0, tn=1024, tk=1408, name=t + "d_h2", out_dtypes=(BF,))
        dx1, d_ng[l][2] = _norm_bwd(dh2, sv['x1'], norm_gains[l, 2], name=t + "norm2", add=dx, tm=1024)
        tok = on_backward('ffn', l, dx1, as3d(gl))
        dy1, d_ng[l][1] = _norm_bwd(dx1, sv['y1'], norm_gains[l, 1], name=t + "norm1", out_dtype=BF, tm=1024, after=tok)
        gl['w_out'], = _mm(sv['ycat'], dy1, ta=True, name=t + "dw_out", tk=4096)
        dycat, = _mm(dy1, wts[l]['w_out'], tb=True, b_layer=0, name=t + "d_ycat", tm=1024, tn=1024)
        dz = lax.empty((T, D_MODEL), BF)
        dz, dkm, dvm = _memattn_bwd(dycat, sv['z'], sv['kvm'], sv['lse_m'], dz, B, S, name=t + "memattn")
        if l < N_A:
            dz, d_wbd[l], d_pscale[l] = _pool_bwd(dycat, sv['p'], wbd[l], pool_scale[l], dz, B, S, name=t + "pool")
        else:
            do, cb = _dil_combine_bwd(dycat, sv['o'], sv['lse'], name=t + "dil_combine", tm=512)
            acc = tuple(lax.empty((T, MAIN_W), F32) for _ in range(3))
            for g in range(3):
                acc = _dil_bwd(g, sv['rq'], rk, rv, do, cb, sv['lse'], acc, B, S, name=t + f"dil{g}")
            dz = _rope_apply(acc[0], cos, sin, name=t + "rope_q", sign=-1.0, alias=dz, tm=1024)
            kv_parts.append(acc[1:])
        tok = on_backward('mix', l, dz, as3d(gl))
        gl['w_in'], = _mm(sv['h0'], dz, ta=True, name=t + "dw_in", tk=4096)
        dh0, = _mm(dz, wts[l]['w_in'], tb=True, b_layer=0, name=t + "d_h0", out_dtypes=(BF,), tm=1024, tn=1024)
        dx, d_ng[l][0] = _norm_bwd(dh0, sv['x_in'], norm_gains[l, 0], name=t + "norm0", add=dx1, tm=1024, after=tok)
        gl['w_mem_kv'], = _mm(sv['memn'], (dkm, dvm), ta=True, tn=256, name=t + "dw_memkv")
        dmemn, = _mm((dkm, dvm), wts[l]['w_mem_kv'], tb=True, b_layer=0, tk=256, name=t + "d_memn", out_dtypes=(BF,))
        _, d_memnorm[l] = _norm_bwd(dmemn, memf, mem_norm[l], name=t + "norm_mem", out_dtype=BF, tm=256)
        if l == N_A:
            dk, dv = _kv_grad_sum(kv_parts, cos, sin, name="kv_grad", tm=1024)
            x_kv, kvn = kv_saved
            gl['w_kv'], = _mm(kvn, (dk, dv), ta=True, tn=768, tk=2048, name="dw_kv")
            dkvn, = _mm((dk, dv), wts[N_A - 1]['w_kv'], tb=True, b_layer=0, tn=1024, tk=768, name="d_kvn", out_dtypes=(BF,))
            dx, d_kvnorm = _norm_bwd(dkvn, x_kv, kv_norm, name="norm_kv_b", add=dx, tm=1024)
        tok = on_backward('end', l, dx, as3d(gl))

    small = {
        'norm_gains': jnp.stack([jnp.concatenate(d_ng[l], axis=0) for l in range(DEPTH)]),
        'mem_norm': jnp.concatenate(d_memnorm, axis=0),
        'kv_norm': d_kvnorm.reshape(D_MODEL),
        'pool_scale': jnp.concatenate(d_pscale, axis=0),
        'w_pool': jnp.stack([jnp.stack([d_wbd[l][gi * POOL_GROUP:(gi + 1) * POOL_GROUP, gi * POOL_GROUP:(gi + 1) * POOL_GROUP]
                                        for gi in range(len(POOL_WINDOWS))]) for l in range(N_A)]),
    }
    return loss, dx, small


SMALL_ORDER = ('norm_gains', 'mem_norm', 'kv_norm', 'pool_scale', 'w_pool')
SMALL_VEC_ROWS = 2560


def kernel(x, mem, positions, norm_gains, mem_norm, w_in, w_mem_kv, w_out, w_pool, pool_scale, kv_norm, w_kv, w_gate_up, w_down, loss_target, m_norm_gains, m_mem_norm, m_w_in, m_w_mem_kv, m_w_out, m_w_pool, m_pool_scale, m_kv_norm, m_w_kv, m_w_gate_up, m_w_down, v_norm_gains, v_mem_norm, v_w_in, v_w_mem_kv, v_w_out, v_w_pool, v_pool_scale, v_kv_norm, v_w_kv, v_w_gate_up, v_w_down):
    xi, yi, ci = lax.axis_index("x"), lax.axis_index("y"), lax.axis_index("c")
    s = 2 * xi + yi
    sc = jnp.stack([s, ci]).astype(jnp.int32)
    weights = dict(norm_gains=norm_gains, mem_norm=mem_norm, w_in=w_in, w_mem_kv=w_mem_kv, w_out=w_out, w_pool=w_pool,
                   pool_scale=pool_scale, kv_norm=kv_norm, w_kv=w_kv, w_gate_up=w_gate_up, w_down=w_down)
    moms = dict(norm_gains=m_norm_gains, mem_norm=m_mem_norm, w_in=m_w_in, w_mem_kv=m_w_mem_kv, w_out=m_w_out,
                w_pool=m_w_pool, pool_scale=m_pool_scale, kv_norm=m_kv_norm, w_kv=m_w_kv, w_gate_up=m_w_gate_up,
                w_down=m_w_down)
    vels = dict(norm_gains=v_norm_gains, mem_norm=v_mem_norm, w_in=v_w_in, w_mem_kv=v_w_mem_kv, w_out=v_w_out,
                w_pool=v_w_pool, pool_scale=v_pool_scale, kv_norm=v_kv_norm, w_kv=v_w_kv, w_gate_up=v_w_gate_up,
                w_down=v_w_down)

    small_w = jnp.zeros((SMALL_ROWS, 256), F32)
    small_w = lax.dynamic_update_slice(small_w, norm_gains.reshape(16, 256), (0, 0))
    small_w = lax.dynamic_update_slice(small_w, pool_scale, (16, 0))

    def shard_of(nm, l):
        return (w_kv.reshape(_shard_shape('w_kv')), 0) if nm == 'w_kv' else (weights[nm], l)

    groups = {'l0a': (0, MIX_W), 'l0b': (0, FFN_W)}
    groups.update({f"l{l}": (l, LAYER_W + (('w_kv',) if l == N_A - 1 else ())) for l in range(1, DEPTH)})
    on_ici, on_d2d, gathered = {}, {}, {}

    def start_group(tag, after):
        l, names = groups[tag]
        on_ici[tag] = _gather_start(tag, names, [shard_of(nm, l) for nm in names], small_w if tag == 'l0a' else None, sc,
                                    after)
        return [on_ici[tag]['token']]

    def on_forward(where, l, after):
        if where == 'start':
            if l == 0:
                start_group('l0a', None)
                st = on_ici.pop('l0a')
                fwd = _gather_forward(st, st['token'])
                w, small_all = _gather_finish(fwd, fwd['token'])
                return w, small_all, start_group('l0b', w['w_in'])
            if f"l{l}" not in on_d2d:
                on_d2d[f"l{l}"] = _gather_forward(on_ici.pop(f"l{l}"), after)
            gathered[l] = _gather_finish(on_d2d.pop(f"l{l}"), after)[0]
            tok = start_group(f"l{l + 1}", gathered[l]['w_in']) if l + 1 < DEPTH else None
            return {nm: w for nm, w in gathered[l].items() if nm not in FFN_W}, None, tok
        if where == 'mid' and l == 0:
            on_d2d['l0b'] = _gather_forward(on_ici.pop('l0b'), after)
            return start_group('l1', on_d2d['l0b']['token'])
        if where == 'ffn':
            if l == 0:
                return (_gather_finish(on_d2d.pop('l0b'), after)[0],)
            return ({nm: gathered[l][nm] for nm in FFN_W},)
        if where == 'post' and 0 < l < DEPTH - 1:
            on_d2d[f"l{l + 1}"] = _gather_forward(on_ici.pop(f"l{l + 1}"), after)
        return None

    hook_of = {'ffn': 0, 'mix': 1, 'end': 2}
    active, reduced = [], {l: {} for l in range(DEPTH)}
    advance = {'mid': lambda st, after: _reduce_mid(st, after, sc), 'late': lambda st, after: _reduce_late(st, after, sc)}

    def run_hook(idx, after):
        toks = []
        for grp in list(active):
            while grp['plan'] and grp['plan'][0][1] <= idx:
                step = grp['plan'].pop(0)[0]
                if step == 'finish':
                    reduced[grp['layer']].update(_reduce_finish(grp['st'], after))
                    active.remove(grp)
                else:
                    grp['st'] = advance[step](grp['st'], after)
                    toks.append(grp['st']['token'])
        return toks

    def on_backward(where, l, after, grads):
        idx = 3 * (DEPTH - 1 - l) + hook_of[where]
        toks = run_hook(idx, after)
        if where == 'end' or (where == 'ffn' and l == 0):
            names = FFN_W if where == 'ffn' else tuple(nm for nm in grads if l > 0 or nm not in FFN_W)
            st = _reduce_start(f"l{l}_{where}_grads", names, {nm: grads[nm] for nm in names})
            plan = [('mid', idx + 1), ('late', idx + 3), ('finish', idx + 4)] if where == 'ffn' else \
                   [('mid', idx + 1), ('late', idx + 2), ('finish', idx + 3)]
            active.append(dict(layer=l, st=st, plan=plan))
            toks.append(st['token'])
        return toks

    loss, gx, gsmall = _local_step(x, mem, positions, on_forward, on_backward, mem_norm, w_pool, kv_norm, loss_target)
    loss = lax.psum(loss[0, 0], ("x", "y", "c"))

    vec = jnp.concatenate([gsmall[nm].reshape(-1) for nm in SMALL_ORDER])
    vec = jnp.pad(vec, (0, SMALL_VEC_ROWS * 128 - vec.shape[0])).reshape(SMALL_VEC_ROWS, 128)
    vec = vec + sum(grp['st']['token'][0, 0] for grp in active)
    small_st = _small_gather_start(vec.astype(BF), sc)
    outs = {nm: None for nm in LAYER_W}

    def adamw_layers(layers, names, after):
        for l in layers:
            for nm in names:
                outs[nm] = _adamw_layer(nm, l, weights[nm], reduced[l][nm], moms[nm], vels[nm], outs[nm], after)
                after = outs[nm][0]
        return after

    def zero_of(toks, st):
        return sum(toks) if toks else st['token']

    last = 3 * DEPTH
    toks = run_hook(last, small_st['token'])
    done = adamw_layers(range(DEPTH - 1, 0, -1), LAYER_W, zero_of(toks, small_st))
    toks = run_hook(last + 1, done)
    done = adamw_layers([0], FFN_W, zero_of(toks, small_st))
    tot = _sum8(_small_gather_finish(small_st, done), name="sum_small_grads", tr=512)
    run_hook(last + 2, tot)
    assert not active
    adamw_layers([0], MIX_W, None)
    tot = tot.reshape(-1)
    grads, off = {}, 0
    for nm in SMALL_ORDER:
        shape = (DEPTH, 4, D_MODEL) if nm == 'norm_gains' else (N_A, MAIN_W) if nm == 'pool_scale' else weights[nm].shape
        n = 1
        for dim in shape:
            n *= dim
        grads[nm] = tot[off:off + n].reshape(shape)
        off += n
    grads['norm_gains'] = lax.dynamic_slice(grads['norm_gains'], (0, 0, s * 256), (DEPTH, 4, 256))
    grads['pool_scale'] = lax.dynamic_slice(grads['pool_scale'], (0, s * POOL_GROUP), (N_A, POOL_GROUP))
    grads['w_kv'] = reduced[N_A]['w_kv'].reshape(w_kv.shape)

    order = ('norm_gains', 'mem_norm', 'w_in', 'w_mem_kv', 'w_out', 'w_pool', 'pool_scale', 'kv_norm', 'w_kv',
             'w_gate_up', 'w_down')
    deltas, new_m, new_v = {}, {}, {}
    for nm in order:
        if nm in LAYER_W:
            deltas[nm], new_m[nm], new_v[nm], grads[nm] = outs[nm]
        else:
            deltas[nm], new_m[nm], new_v[nm] = _adamw(weights[nm], grads[nm], moms[nm], vels[nm], name="adamw_" + nm)
    return (loss, gx.reshape(x.shape), *[grads[nm] for nm in order], *[deltas[nm] for nm in order],
            *[new_m[nm] for nm in order], *[new_v[nm] for nm in order])
```

```python
import jax
import jax.numpy as jnp
from jax import lax
from jax.experimental import pallas as pl
from jax.experimental.pallas import tpu as pltpu

F32 = jnp.float32
BF = jnp.bfloat16

D_MODEL = 1024
DEPTH = 4
N_A = 2
HEAD_DIM = 64
MEM_W = 256
MAIN_W = 768
D_FF = 2816
N_MEM = 256
POOL_WINDOWS = (2, 4, 8, 16)
POOL_GROUP = 192
DIL = (1, 4, 16)
STEPS = 128
ROPE_THETA = 10000.0
EPS = 1e-6
SCALE = HEAD_DIM ** -0.5
NEG = -1e30

ADAM_LR = 0.001
ADAM_B1 = 0.9
ADAM_B2 = 0.999
ADAM_EPS = 1e-08
ADAM_WD = 0.01
ADAM_STEP = 10

VMEM_LIMIT = 48 * 1024 * 1024
MESH = pl.DeviceIdType.MESH


def _cp(sem):
    return pltpu.CompilerParams(dimension_semantics=sem, vmem_limit_bytes=VMEM_LIMIT)


def _mm(a, b, *, name, ta=False, tb=False, tm=1024, tn=512, tk=1024, b_layer=None, b_offsets=(0,),
        extras=(), epilogue=None, out_dtypes=(F32,), out_n=None):
    a_pair = isinstance(a, (tuple, list))
    b_pair = isinstance(b, (tuple, list))
    a0 = a[0] if a_pair else a
    b0 = b[0] if b_pair else b
    a_rows, a_cols = a0.shape
    if a_pair:
        a_cols *= 2
    b_rows, b_cols = b0.shape[-2:]
    if b_pair:
        b_cols *= 2
    M, K = (a_cols, a_rows) if ta else (a_rows, a_cols)
    N = b_rows if tb else b_cols
    if out_n is not None:
        N = out_n
    tm, tn, tk = min(tm, M), min(tn, N), min(tk, K)
    assert M % tm == 0 and N % tn == 0 and K % tk == 0, (name, M, N, K, tm, tn, tk)
    nk = K // tk
    n_acc = len(b_offsets)

    if a_pair:
        a_half = (a0.shape[1] // (tm if ta else tk))
    if b_pair:
        b_half = (b0.shape[1] // (tk if tb else tn))

    def a_map(sel):
        def f(i, j, k):
            r, c = (k, i) if ta else (i, k)
            if a_pair:
                c = jnp.clip(c - sel * a_half, 0, a_half - 1)
            return (r, c)
        return f

    def b_map(sel, off):
        def f(i, j, k):
            r, c = (j + off, k) if tb else (k, j + off)
            if b_pair:
                c = jnp.clip(c - sel * b_half, 0, b_half - 1)
            if b_layer is not None:
                return (b_layer, r, c)
            return (r, c)
        return f

    a_blk = (tk, tm) if ta else (tm, tk)
    b_blk = (tn, tk) if tb else (tk, tn)
    if b_layer is not None:
        b_blk = (None,) + b_blk
    in_specs, operands = [], []
    for sel in range(2 if a_pair else 1):
        in_specs.append(pl.BlockSpec(a_blk, a_map(sel)))
        operands.append(a[sel] if a_pair else a)
    n_a = len(operands)
    for off in b_offsets:
        for sel in range(2 if b_pair else 1):
            in_specs.append(pl.BlockSpec(b_blk, b_map(sel, off)))
            operands.append(b[sel] if b_pair else b)
    n_b = len(operands) - n_a
    for arr, kind in extras:
        if kind == 'tile':
            in_specs.append(pl.BlockSpec((tm, tn), lambda i, j, k: (i, j)))
        elif kind == 'row':
            in_specs.append(pl.BlockSpec((tm, 1), lambda i, j, k: (i, 0)))
        else:
            in_specs.append(pl.BlockSpec((1, tn), lambda i, j, k: (0, j)))
        operands.append(arr)
    n_e = len(extras)
    n_o = len(out_dtypes)
    dims = (((0,) if ta else (1,), (1,) if tb else (0,)), ((), ()))
    in_place = nk > 1 and epilogue is None and n_acc == 1 and tuple(out_dtypes) == (F32,)

    def body(*refs):
        a_refs = refs[:n_a]
        b_refs = refs[n_a:n_a + n_b]
        e_refs = refs[n_a + n_b:n_a + n_b + n_e]
        n_in = n_a + n_b + n_e
        o_refs = refs[n_in:n_in + n_o]
        acc_refs = refs[n_in + n_o:]
        i, j, k = pl.program_id(0), pl.program_id(1), pl.program_id(2)
        if a_pair:
            cidx = i if ta else k
            av = jnp.where(cidx < a_half, a_refs[0][...], a_refs[1][...])
        else:
            av = a_refs[0][...]
        av = av.astype(BF)
        prods = []
        for q in range(n_acc):
            if b_pair:
                cidx = (k if tb else j) + b_offsets[q]
                bv = jnp.where(cidx < b_half, b_refs[2 * q][...], b_refs[2 * q + 1][...])
            else:
                bv = b_refs[q][...]
            prods.append(lax.dot_general(av, bv.astype(BF), dims, preferred_element_type=F32))

        def finish(accs):
            outs = epilogue(accs, *[r[...] for r in e_refs]) if epilogue is not None else accs
            for o_ref, o in zip(o_refs, outs):
                o_ref[...] = o.astype(o_ref.dtype)

        if nk == 1:
            finish(prods)
        elif in_place:
            @pl.when(k == 0)
            def _():
                o_refs[0][...] = prods[0]

            @pl.when(k > 0)
            def _():
                o_refs[0][...] += prods[0]
        else:
            @pl.when(k == 0)
            def _():
                for r, p in zip(acc_refs, prods):
                    r[...] = p

            @pl.when(k > 0)
            def _():
                for r, p in zip(acc_refs, prods):
                    r[...] += p

            @pl.when(k == nk - 1)
            def _():
                finish([r[...] for r in acc_refs])

    return pl.pallas_call(
        body, name=name,
        grid=(M // tm, N // tn, nk),
        in_specs=in_specs,
        out_specs=[pl.BlockSpec((tm, tn), lambda i, j, k: (i, j)) for _ in range(n_o)],
        out_shape=[jax.ShapeDtypeStruct((M, N), dt) for dt in out_dtypes],
        scratch_shapes=[pltpu.VMEM((tm, tn), F32) for _ in range(n_acc if nk > 1 and not in_place else 0)],
        compiler_params=_cp(("parallel", "parallel", "arbitrary")),
    )(*operands)


def _norm_fwd(x, g, *, name, res=None, out_dtype=F32, tm=512, after=None):
    T, Dm = x.shape
    has_res = res is not None
    after = list(after or [])

    def body(*refs):
        refs = refs[:len(refs) - 1 - len(after)] + refs[len(refs) - 1:]
        if has_res:
            x_ref, g_ref, r_ref, y_ref = refs
        else:
            x_ref, g_ref, y_ref = refs
        xv = x_ref[...]
        rstd = lax.rsqrt(jnp.mean(xv * xv, axis=-1, keepdims=True) + EPS)
        y = xv * rstd * g_ref[...]
        if has_res:
            y = r_ref[...] + y
        y_ref[...] = y.astype(y_ref.dtype)

    row = pl.BlockSpec((tm, Dm), lambda i: (i, 0))
    in_specs = [row, pl.BlockSpec((1, Dm), lambda i: (0, 0))] + ([row] if has_res else [])
    in_specs += [pl.BlockSpec(memory_space=pl.ANY)] * len(after)
    ops = [x, g.reshape(1, Dm)] + ([res] if has_res else []) + after
    return pl.pallas_call(
        body, name=name, grid=(T // tm,), in_specs=in_specs,
        out_specs=row,
        out_shape=jax.ShapeDtypeStruct((T, Dm), out_dtype),
        compiler_params=_cp(("parallel",)),
    )(*ops)


def _norm_bwd(dout, x, g, *, name, add=None, out_dtype=F32, tm=512, after=None):
    T, Dm = x.shape
    has_add = add is not None
    nt = T // tm
    after = list(after or [])

    def body(*refs):
        refs = refs[:len(refs) - 3 - len(after)] + refs[len(refs) - 3:]
        if has_add:
            do_ref, x_ref, g_ref, a_ref, dx_ref, dg_ref, acc = refs
        else:
            do_ref, x_ref, g_ref, dx_ref, dg_ref, acc = refs
        i = pl.program_id(0)
        do = do_ref[...].astype(F32)
        xv = x_ref[...]
        rstd = lax.rsqrt(jnp.mean(xv * xv, axis=-1, keepdims=True) + EPS)
        xh = xv * rstd
        gd = do * g_ref[...]
        dx = rstd * (gd - xh * jnp.mean(gd * xh, axis=-1, keepdims=True))
        if has_add:
            dx = dx + a_ref[...].astype(F32)
        dx_ref[...] = dx.astype(dx_ref.dtype)
        part = jnp.sum((do * xh).reshape(tm // 8, 8, Dm), axis=0)

        @pl.when(i == 0)
        def _():
            acc[...] = part

        @pl.when(i > 0)
        def _():
            acc[...] += part

        @pl.when(i == nt - 1)
        def _():
            dg_ref[...] = jnp.sum(acc[...], axis=0, keepdims=True)

    row = pl.BlockSpec((tm, Dm), lambda i: (i, 0))
    in_specs = [row, row, pl.BlockSpec((1, Dm), lambda i: (0, 0))]
    ops = [dout, x, g.reshape(1, Dm)]
    if has_add:
        in_specs.append(row)
        ops.append(add)
    in_specs += [pl.BlockSpec(memory_space=pl.ANY)] * len(after)
    ops += after
    return pl.pallas_call(
        body, name=name, grid=(nt,), in_specs=in_specs,
        out_specs=[row, pl.BlockSpec((1, Dm), lambda i: (0, 0))],
        out_shape=[jax.ShapeDtypeStruct((T, Dm), out_dtype), jax.ShapeDtypeStruct((1, Dm), F32)],
        scratch_shapes=[pltpu.VMEM((8, Dm), F32)],
        compiler_params=_cp(("arbitrary",)),
    )(*ops)


def _swiglu_fwd_epilogue(accs):
    g, u = accs
    return g, u, g * jax.nn.sigmoid(g) * u


def _swiglu_bwd_epilogue(accs, g, u):
    da = accs[0]
    g = g.astype(F32)
    u = u.astype(F32)
    sig = jax.nn.sigmoid(g)
    return da * u * (sig * (1.0 + g * (1.0 - sig))), da * (g * sig)


def _rope_tables(pos, *, name, tm=1024):
    T = pos.shape[0]
    half = HEAD_DIM // 2
    freqs = ROPE_THETA ** (-jnp.arange(half, dtype=F32) / half)
    freqs = jnp.tile(freqs, 4).reshape(1, 128)

    def body(p_ref, f_ref, c_ref, s_ref):
        ang = p_ref[...].astype(F32) * f_ref[...]
        lane = lax.broadcasted_iota(jnp.int32, ang.shape, 1)
        c_ref[...] = jnp.cos(ang)
        s_ref[...] = jnp.where(lane % HEAD_DIM < half, -1.0, 1.0) * jnp.sin(ang)

    tab = pl.BlockSpec((tm, 128), lambda i: (i, 0))
    return pl.pallas_call(
        body, name=name, grid=(T // tm,),
        in_specs=[pl.BlockSpec((tm, 1), lambda i: (i, 0)), pl.BlockSpec((1, 128), lambda i: (0, 0))],
        out_specs=[tab, tab],
        out_shape=[jax.ShapeDtypeStruct((T, 128), F32)] * 2,
        compiler_params=_cp(("parallel",)),
    )(pos, freqs)


def _rot(x, cos, sin, sign):
    W = x.shape[1]
    half = HEAD_DIM // 2
    reps = W // 128
    c = jnp.concatenate([cos] * reps, axis=1) if reps > 1 else cos
    s = jnp.concatenate([sin] * reps, axis=1) if reps > 1 else sin
    lane = lax.broadcasted_iota(jnp.int32, x.shape, 1)
    swapped = jnp.where(lane % HEAD_DIM < half, pltpu.roll(x, W - half, axis=1), pltpu.roll(x, half, axis=1))
    return x * c + (sign * s) * swapped


def _rope_apply(x, cos, sin, *, name, sign=1.0, width=MAIN_W, passthrough=False, out_dtype=BF, alias=None,
                out_cols=None, tm=512):
    T = x.shape[0]

    def body(*refs):
        if passthrough:
            x_ref, v_ref, c_ref, s_ref, o_ref, ov_ref = refs
            ov_ref[...] = v_ref[...].astype(ov_ref.dtype)
        elif alias is not None:
            x_ref, c_ref, s_ref, _, o_ref = refs
        else:
            x_ref, c_ref, s_ref, o_ref = refs
        o_ref[...] = _rot(x_ref[...].astype(F32), c_ref[...], s_ref[...], sign).astype(o_ref.dtype)

    blk0 = pl.BlockSpec((tm, width), lambda i: (i, 0))
    blk1 = pl.BlockSpec((tm, width), lambda i: (i, 1))
    tab = pl.BlockSpec((tm, 128), lambda i: (i, 0))
    if passthrough:
        return pl.pallas_call(
            body, name=name, grid=(T // tm,), in_specs=[blk0, blk1, tab, tab], out_specs=[blk0, blk0],
            out_shape=[jax.ShapeDtypeStruct((T, width), out_dtype)] * 2,
            compiler_params=_cp(("parallel",)),
        )(x, x, cos, sin)
    if alias is not None:
        return pl.pallas_call(
            body, name=name, grid=(T // tm,),
            in_specs=[blk0, tab, tab, pl.BlockSpec(memory_space=pl.ANY)], out_specs=blk0,
            out_shape=jax.ShapeDtypeStruct(alias.shape, alias.dtype),
            input_output_aliases={3: 0},
            compiler_params=_cp(("parallel",)),
        )(x, cos, sin, alias)
    return pl.pallas_call(
        body, name=name, grid=(T // tm,), in_specs=[blk0, tab, tab], out_specs=blk0,
        out_shape=jax.ShapeDtypeStruct((T, width), out_dtype),
        compiler_params=_cp(("parallel",)),
    )(x, cos, sin)


POOL_T = 256
POOL_HALO = 16


def _pool_lane_window(shape):
    lane = lax.broadcasted_iota(jnp.int32, shape, 1)
    w = jnp.full(shape, POOL_WINDOWS[0], jnp.int32)
    for gi in range(1, len(POOL_WINDOWS)):
        w = jnp.where(lane >= gi * POOL_GROUP, POOL_WINDOWS[gi], w)
    return w


POOL_PAD = 32
POOL_R = POOL_T + POOL_PAD


def _pool_window_sums(buf, tmp_a, tmp_b, win, back):
    src, dst, acc = buf, tmp_a, None
    for j, (w, sh) in enumerate(zip(POOL_WINDOWS, (1, 2, 4, 8)), start=1):
        n = POOL_R - 8 * j
        if back:
            dst[pl.ds(8 * j, n), :] = src[pl.ds(8 * j, n), :] + src[pl.ds(8 * j - sh, n), :]
            cur = dst[pl.ds(POOL_PAD, POOL_T), :]
        else:
            dst[pl.ds(0, n), :] = src[pl.ds(0, n), :] + src[pl.ds(sh, n), :]
            cur = dst[pl.ds(0, POOL_T), :]
        acc = cur if acc is None else jnp.where(win >= w, cur, acc)
        src, dst = dst, (tmp_b if dst is tmp_a else tmp_a)
    return acc


def _pool_fwd(z, wbd, scale, B, S, *, name):
    T = z.shape[0]
    nt = S // POOL_T
    hb = POOL_T // POOL_PAD

    def body(z_ref, h_ref, w_ref, sc_ref, y_ref, p_ref, ext, tmp_a, tmp_b):
        i = pl.program_id(1)
        u = z_ref[...]
        ext[pl.ds(POOL_PAD, POOL_T), :] = u
        ext[pl.ds(0, POOL_PAD), :] = jnp.where(i > 0, h_ref[...], 0.0)
        win = _pool_lane_window((POOL_T, MAIN_W))
        acc = _pool_window_sums(ext, tmp_a, tmp_b, win, True)
        t = i * POOL_T + lax.broadcasted_iota(jnp.int32, (POOL_T, MAIN_W), 0)
        cnt = jnp.minimum(t + 1, win).astype(F32)
        p = (acc / cnt - u).astype(BF)
        p_ref[...] = p
        y = jnp.dot(p, w_ref[...], preferred_element_type=F32) * sc_ref[...]
        y_ref[...] = y.astype(y_ref.dtype)

    return pl.pallas_call(
        body, name=name, grid=(B, nt),
        in_specs=[pl.BlockSpec((POOL_T, MAIN_W), lambda b, i: (b * nt + i, 0)),
                  pl.BlockSpec((POOL_PAD, MAIN_W), lambda b, i: (jnp.maximum((b * nt + i) * hb - 1, 0), 0)),
                  pl.BlockSpec((MAIN_W, MAIN_W), lambda b, i: (0, 0)),
                  pl.BlockSpec((1, MAIN_W), lambda b, i: (0, 0))],
        out_specs=[pl.BlockSpec((POOL_T, MAIN_W), lambda b, i: (b * nt + i, 0)),
                   pl.BlockSpec((POOL_T, MAIN_W), lambda b, i: (b * nt + i, 0))],
        out_shape=[jax.ShapeDtypeStruct((T, D_MODEL), BF), jax.ShapeDtypeStruct((T, MAIN_W), BF)],
        scratch_shapes=[pltpu.VMEM((POOL_R, MAIN_W), F32)] * 3,
        compiler_params=_cp(("parallel", "parallel")),
    )(z, z, wbd, scale.reshape(1, MAIN_W))


def _pool_bwd(dy, p, wbd, scale, dz_alias, B, S, *, name):
    T = dy.shape[0]
    nt = S // POOL_T
    hb = POOL_T // POOL_HALO
    last_halo = T // POOL_HALO - 1

    def body(dy_ref, dyn_ref, p_ref, pn_ref, w_ref, sc_ref, _, dz_ref, dw_ref, ds_ref, ext, tmp_a, tmp_b, dw_acc, ds_acc):
        b, i = pl.program_id(0), pl.program_id(1)
        first = jnp.logical_and(b == 0, i == 0)
        dyv = dy_ref[...]
        pv = p_ref[...]
        sc = sc_ref[...]
        w = w_ref[...]
        pw = jnp.dot(pv, w, preferred_element_type=F32)
        ds_part = jnp.sum((dyv * pw).reshape(POOL_T // 8, 8, MAIN_W), axis=0)
        dpw = (dyv * sc).astype(BF)
        dw_part = lax.dot_general(pv, dpw, (((0,), (0,)), ((), ())), preferred_element_type=F32)

        @pl.when(first)
        def _():
            dw_acc[...] = dw_part
            ds_acc[...] = ds_part

        @pl.when(jnp.logical_not(first))
        def _():
            dw_acc[...] += dw_part
            ds_acc[...] += ds_part

        @pl.when(jnp.logical_and(b == pl.num_programs(0) - 1, i == nt - 1))
        def _():
            dw_ref[...] = dw_acc[...]
            ds_ref[...] = jnp.sum(ds_acc[...], axis=0, keepdims=True)

        dp = lax.dot_general(dpw, w, (((1,), (1,)), ((), ())), preferred_element_type=F32)
        dpn = lax.dot_general((dyn_ref[...] * sc).astype(BF), w, (((1,), (1,)), ((), ())), preferred_element_type=F32)
        win = _pool_lane_window((POOL_T, MAIN_W))
        win_n = _pool_lane_window((POOL_HALO, MAIN_W))
        t = i * POOL_T + lax.broadcasted_iota(jnp.int32, (POOL_T, MAIN_W), 0)
        tn = (i + 1) * POOL_T + lax.broadcasted_iota(jnp.int32, (POOL_HALO, MAIN_W), 0)
        ext[pl.ds(0, POOL_T), :] = dp / jnp.minimum(t + 1, win).astype(F32)
        ext[pl.ds(POOL_T, POOL_HALO), :] = jnp.where(i < nt - 1, dpn / jnp.minimum(tn + 1, win_n).astype(F32), 0.0)
        ext[pl.ds(POOL_T + POOL_HALO, POOL_PAD - POOL_HALO), :] = jnp.zeros((POOL_PAD - POOL_HALO, MAIN_W), F32)
        acc = _pool_window_sums(ext, tmp_a, tmp_b, win, False) - dp
        dz_ref[...] = acc.astype(dz_ref.dtype)

    cur = lambda b, i: (b * nt + i, 0)
    nxt = lambda b, i: (jnp.minimum((b * nt + i + 1) * hb, last_halo), 0)
    return pl.pallas_call(
        body, name=name, grid=(B, nt),
        in_specs=[pl.BlockSpec((POOL_T, MAIN_W), cur), pl.BlockSpec((POOL_HALO, MAIN_W), nxt),
                  pl.BlockSpec((POOL_T, MAIN_W), cur), pl.BlockSpec((POOL_HALO, MAIN_W), nxt),
                  pl.BlockSpec((MAIN_W, MAIN_W), lambda b, i: (0, 0)),
                  pl.BlockSpec((1, MAIN_W), lambda b, i: (0, 0)),
                  pl.BlockSpec(memory_space=pl.ANY)],
        out_specs=[pl.BlockSpec((POOL_T, MAIN_W), cur),
                   pl.BlockSpec((MAIN_W, MAIN_W), lambda b, i: (0, 0)),
                   pl.BlockSpec((1, MAIN_W), lambda b, i: (0, 0))],
        out_shape=[jax.ShapeDtypeStruct(dz_alias.shape, dz_alias.dtype),
                   jax.ShapeDtypeStruct((MAIN_W, MAIN_W), F32), jax.ShapeDtypeStruct((1, MAIN_W), F32)],
        scratch_shapes=[pltpu.VMEM((POOL_R, MAIN_W), F32)] * 3 + [pltpu.VMEM((MAIN_W, MAIN_W), F32), pltpu.VMEM((8, MAIN_W), F32)],
        input_output_aliases={6: 0},
        compiler_params=_cp(("arbitrary", "arbitrary")),
    )(dy, dy, p, p, wbd, scale.reshape(1, MAIN_W), dz_alias)


def _head_masks(shape):
    lane = lax.broadcasted_iota(jnp.int32, shape, 1)
    return [(lane // HEAD_DIM) == h for h in range(shape[1] // HEAD_DIM)]


def _row_of(bcast, mask):
    return jnp.max(jnp.where(mask, bcast, -jnp.inf), axis=-1, keepdims=True)


MEM_TQ = 2048


def _memattn_fwd(z, kv, y_alias, B, S, *, name, tq=MEM_TQ):
    T = z.shape[0]
    nt = S // tq

    def body(q_ref, k_ref, v_ref, _, y_ref, l_ref):
        q = q_ref[...]
        k = k_ref[...]
        v = v_ref[...]
        masks = _head_masks(q.shape)
        o = jnp.zeros(q.shape, F32)
        lse_b = jnp.zeros(q.shape, F32)
        for m in masks:
            qm = jnp.where(m, q, 0.0).astype(BF)
            s = lax.dot_general(qm, k, (((1,), (1,)), ((), ())), preferred_element_type=F32) * SCALE
            mx = jnp.max(s, axis=-1, keepdims=True)
            e = jnp.exp(s - mx)
            l = jnp.sum(e, axis=-1, keepdims=True)
            p = (e / l).astype(BF)
            o = o + jnp.where(m, jnp.dot(p, v, preferred_element_type=F32), 0.0)
            lse_b = lse_b + jnp.where(m, mx + jnp.log(l), 0.0)
        y_ref[...] = o.astype(y_ref.dtype)
        l_ref[...] = lse_b

    qblk = pl.BlockSpec((tq, MEM_W), lambda b, i: (b * nt + i, 3))
    return pl.pallas_call(
        body, name=name, grid=(B, nt),
        in_specs=[qblk, pl.BlockSpec((N_MEM, MEM_W), lambda b, i: (b, 0)), pl.BlockSpec((N_MEM, MEM_W), lambda b, i: (b, 1)),
                  pl.BlockSpec(memory_space=pl.ANY)],
        out_specs=[qblk, pl.BlockSpec((tq, MEM_W), lambda b, i: (b * nt + i, 0))],
        out_shape=[jax.ShapeDtypeStruct(y_alias.shape, y_alias.dtype), jax.ShapeDtypeStruct((T, MEM_W), F32)],
        input_output_aliases={3: 0},
        compiler_params=_cp(("parallel", "parallel")),
    )(z, kv, kv, y_alias)


def _memattn_bwd(dy, z, kv, lse, dz_alias, B, S, *, name, tq=MEM_TQ):
    nt = S // tq

    def body(do_ref, q_ref, k_ref, v_ref, l_ref, _, dz_ref, dk_ref, dv_ref, dk_acc, dv_acc):
        i = pl.program_id(1)
        do = do_ref[...]
        q = q_ref[...]
        k = k_ref[...]
        v = v_ref[...]
        lse_b = l_ref[...]
        masks = _head_masks(q.shape)
        dq = jnp.zeros(q.shape, F32)
        dk = jnp.zeros(k.shape, F32)
        dv = jnp.zeros(v.shape, F32)
        for m in masks:
            qm = jnp.where(m, q, 0.0).astype(BF)
            dom = jnp.where(m, do, 0.0).astype(BF)
            s = lax.dot_general(qm, k, (((1,), (1,)), ((), ())), preferred_element_type=F32) * SCALE
            p = jnp.exp(s - _row_of(lse_b, m))
            dp = lax.dot_general(dom, v, (((1,), (1,)), ((), ())), preferred_element_type=F32)
            delta = jnp.sum(p * dp, axis=-1, keepdims=True)
            ds = (p * (dp - delta) * SCALE).astype(BF)
            pb = p.astype(BF)
            dv = dv + jnp.where(m[:N_MEM], lax.dot_general(pb, dom, (((0,), (0,)), ((), ())), preferred_element_type=F32), 0.0)
            dk = dk + jnp.where(m[:N_MEM], lax.dot_general(ds, qm, (((0,), (0,)), ((), ())), preferred_element_type=F32), 0.0)
            dq = dq + jnp.where(m, jnp.dot(ds, k, preferred_element_type=F32), 0.0)
        dz_ref[...] = dq.astype(dz_ref.dtype)

        @pl.when(i == 0)
        def _():
            dk_acc[...] = dk
            dv_acc[...] = dv

        @pl.when(i > 0)
        def _():
            dk_acc[...] += dk
            dv_acc[...] += dv

        @pl.when(i == nt - 1)
        def _():
            dk_ref[...] = dk_acc[...]
            dv_ref[...] = dv_acc[...]

    qblk = pl.BlockSpec((tq, MEM_W), lambda b, i: (b * nt + i, 3))
    kblk = pl.BlockSpec((N_MEM, MEM_W), lambda b, i: (b, 0))
    return pl.pallas_call(
        body, name=name, grid=(B, nt),
        in_specs=[qblk, qblk, kblk, pl.BlockSpec((N_MEM, MEM_W), lambda b, i: (b, 1)),
                  pl.BlockSpec((tq, MEM_W), lambda b, i: (b * nt + i, 0)), pl.BlockSpec(memory_space=pl.ANY)],
        out_specs=[qblk, kblk, kblk],
        out_shape=[jax.ShapeDtypeStruct(dz_alias.shape, dz_alias.dtype),
                   jax.ShapeDtypeStruct((B * N_MEM, MEM_W), F32), jax.ShapeDtypeStruct((B * N_MEM, MEM_W), F32)],
        scratch_shapes=[pltpu.VMEM((N_MEM, MEM_W), F32), pltpu.VMEM((N_MEM, MEM_W), F32)],
        input_output_aliases={5: 0},
        compiler_params=_cp(("parallel", "arbitrary")),
    )(dy, z, kv, kv, lse, dz_alias)


N_UNITS = 16


def _unit_rows(g):
    d = DIL[g]
    nb = N_UNITS // d
    return [pl.ds(n * STEPS * d + r, STEPS, stride=d) if d > 1 else pl.ds(n * STEPS, STEPS)
            for n in range(nb) for r in range(d)]


def _load_units(ref, g):
    if DIL[g] == 1:
        return ref[...].reshape(N_UNITS, STEPS, 128)
    return jnp.stack([ref[rows, :] for rows in _unit_rows(g)])


def _store_units(ref, val, g):
    if DIL[g] == 1:
        ref[...] = val.reshape(N_UNITS * STEPS, 128)
    else:
        for u, rows in enumerate(_unit_rows(g)):
            ref[rows, :] = val[u]


def _shift_units(x, by):
    z = jnp.zeros((abs(by),) + x.shape[1:], x.dtype)
    return jnp.concatenate([z, x[:N_UNITS - by]], axis=0) if by > 0 else jnp.concatenate([x[-by:], z], axis=0)


def _bdot(a, b, ca, cb):
    return lax.dot_general(a, b, (((ca,), (cb,)), ((0,), (0,))), preferred_element_type=F32)


def _dil_masks(g):
    d = DIL[g]
    has_prev = N_UNITS // d > 1
    qi = lax.broadcasted_iota(jnp.int32, (1, STEPS, STEPS), 1)
    kj = lax.broadcasted_iota(jnp.int32, (1, STEPS, STEPS), 2)
    unit = lax.broadcasted_iota(jnp.int32, (N_UNITS, 1, 1), 0)
    cur = kj <= qi
    prev = jnp.logical_and(kj >= qi, unit >= d) if has_prev else None
    lane = lax.broadcasted_iota(jnp.int32, (1, 1, 128), 2)
    heads = [(lane // HEAD_DIM) == h for h in range(128 // HEAD_DIM)]
    return has_prev, cur, prev, heads


def _dil_fwd(g, q, k, v, o_alias, l_alias, B, S, *, name):
    assert S == N_UNITS * STEPS
    d = DIL[g]

    def body(q_ref, k_ref, v_ref, _, __, o_ref, l_ref):
        has_prev, cur, prev, heads = _dil_masks(g)
        q = _load_units(q_ref, g)
        kc = _load_units(k_ref, g).astype(BF)
        vc = _load_units(v_ref, g).astype(BF)
        if has_prev:
            kp, vp = _shift_units(kc, d), _shift_units(vc, d)
        o = jnp.zeros(q.shape, F32)
        lse_b = jnp.zeros(q.shape, F32)
        for m in heads:
            qm = jnp.where(m, q, 0.0).astype(BF)
            sc = jnp.where(cur, _bdot(qm, kc, 2, 2) * SCALE, NEG)
            mx = jnp.max(sc, axis=-1, keepdims=True)
            if has_prev:
                sp = jnp.where(prev, _bdot(qm, kp, 2, 2) * SCALE, NEG)
                mx = jnp.maximum(mx, jnp.max(sp, axis=-1, keepdims=True))
            l = jnp.sum(jnp.exp(sc - mx), axis=-1, keepdims=True)
            if has_prev:
                l = l + jnp.sum(jnp.exp(sp - mx), axis=-1, keepdims=True)
            lse = mx + jnp.log(l)
            oh = _bdot(jnp.exp(sc - lse).astype(BF), vc, 2, 1)
            if has_prev:
                oh = oh + _bdot(jnp.exp(sp - lse).astype(BF), vp, 2, 1)
            o = o + jnp.where(m, oh, 0.0)
            lse_b = lse_b + jnp.where(m, lse, 0.0)
        _store_units(o_ref, o, g)
        _store_units(l_ref, lse_b, g)

    blk = pl.BlockSpec((S, 128), lambda b, hf: (b, g * 2 + hf))
    anyspec = pl.BlockSpec(memory_space=pl.ANY)
    o, l = pl.pallas_call(
        body, name=name, grid=(B, 2),
        in_specs=[blk, blk, blk, anyspec, anyspec], out_specs=[blk, blk],
        out_shape=[jax.ShapeDtypeStruct(q.shape, F32)] * 2,
        input_output_aliases={3: 0, 4: 1},
        compiler_params=_cp(("parallel", "parallel")),
    )(q, k, v, o_alias, l_alias)
    return o, l


def _dil_bwd(g, q, k, v, do, cb, lse, aliases, B, S, *, name):
    assert S == N_UNITS * STEPS
    d = DIL[g]

    def body(q_ref, k_ref, v_ref, do_ref, c_ref, l_ref, _, __, ___, dq_ref, dk_ref, dv_ref):
        has_prev, cur, prev, heads = _dil_masks(g)
        q = _load_units(q_ref, g)
        kc = _load_units(k_ref, g).astype(BF)
        vc = _load_units(v_ref, g).astype(BF)
        do = _load_units(do_ref, g)
        cbv = _load_units(c_ref, g)
        lse_b = _load_units(l_ref, g)
        if has_prev:
            kp, vp = _shift_units(kc, d), _shift_units(vc, d)
        z = jnp.zeros(q.shape, F32)
        dq, dkc, dkp, dvc, dvp = z, z, z, z, z
        for m in heads:
            qm = jnp.where(m, q, 0.0).astype(BF)
            dom = jnp.where(m, do, 0.0).astype(BF)
            lse = jnp.max(jnp.where(m, lse_b, -jnp.inf), axis=-1, keepdims=True)
            c = jnp.max(jnp.where(m, cbv, -jnp.inf), axis=-1, keepdims=True)
            sc = jnp.where(cur, _bdot(qm, kc, 2, 2) * SCALE, NEG)
            pc = jnp.exp(sc - lse)
            dsc = (pc * (_bdot(dom, vc, 2, 2) + c) * SCALE).astype(BF)
            dqh = _bdot(dsc, kc, 2, 1)
            dkc = dkc + jnp.where(m, _bdot(dsc, qm, 1, 1), 0.0)
            dvc = dvc + jnp.where(m, _bdot(pc.astype(BF), dom, 1, 1), 0.0)
            if has_prev:
                sp = jnp.where(prev, _bdot(qm, kp, 2, 2) * SCALE, NEG)
                pp = jnp.exp(sp - lse)
                dsp = (pp * (_bdot(dom, vp, 2, 2) + c) * SCALE).astype(BF)
                dqh = dqh + _bdot(dsp, kp, 2, 1)
                dkp = dkp + jnp.where(m, _bdot(dsp, qm, 1, 1), 0.0)
                dvp = dvp + jnp.where(m, _bdot(pp.astype(BF), dom, 1, 1), 0.0)
            dq = dq + jnp.where(m, dqh, 0.0)
        if has_prev:
            dkc = dkc + _shift_units(dkp, -d)
            dvc = dvc + _shift_units(dvp, -d)
        _store_units(dq_ref, dq, g)
        _store_units(dk_ref, dkc, g)
        _store_units(dv_ref, dvc, g)

    blk = pl.BlockSpec((S, 128), lambda b, hf: (b, g * 2 + hf))
    anyspec = pl.BlockSpec(memory_space=pl.ANY)
    return tuple(pl.pallas_call(
        body, name=name, grid=(B, 2),
        in_specs=[blk] * 6 + [anyspec] * 3, out_specs=[blk] * 3,
        out_shape=[jax.ShapeDtypeStruct(q.shape, F32)] * 3,
        input_output_aliases={6: 0, 7: 1, 8: 2},
        compiler_params=_cp(("parallel", "parallel")),
    )(q, k, v, do, cb, lse, *aliases))


def _kv_grad_sum(parts, cos, sin, *, name, tm=512):
    T = parts[0][0].shape[0]
    n_l = len(parts)

    def body(*refs):
        c_ref, s_ref = refs[0], refs[1]
        dk_ref, dv_ref = refs[2 + 2 * n_l:]
        dk = refs[2][...]
        dv = refs[3][...]
        for li in range(1, n_l):
            dk = dk + refs[2 + 2 * li][...]
            dv = dv + refs[3 + 2 * li][...]
        dk_ref[...] = _rot(dk, c_ref[...], s_ref[...], -1.0).astype(dk_ref.dtype)
        dv_ref[...] = dv.astype(dv_ref.dtype)

    full = pl.BlockSpec((tm, MAIN_W), lambda i: (i, 0))
    tab = pl.BlockSpec((tm, 128), lambda i: (i, 0))
    ops = [cos, sin] + [t for part in parts for t in part]
    return pl.pallas_call(
        body, name=name, grid=(T // tm,), in_specs=[tab, tab] + [full] * (2 * n_l), out_specs=[full, full],
        out_shape=[jax.ShapeDtypeStruct((T, MAIN_W), BF)] * 2,
        compiler_params=_cp(("parallel",)),
    )(*ops)


def _group_softmax(lse):
    l0, l1, l2 = lse[:, 0:256], lse[:, 256:512], lse[:, 512:768]
    mx = jnp.maximum(jnp.maximum(l0, l1), l2)
    e0, e1, e2 = jnp.exp(l0 - mx), jnp.exp(l1 - mx), jnp.exp(l2 - mx)
    tot = e0 + e1 + e2
    return e0 / tot, e1 / tot, e2 / tot


def _dil_combine_fwd(o, lse, y_alias, *, name, tm=512):
    T = o.shape[0]

    def body(o_ref, l_ref, _, y_ref):
        a = jnp.concatenate(_group_softmax(l_ref[...]), axis=1)
        y_ref[...] = (o_ref[...] * a).astype(y_ref.dtype)

    blk = pl.BlockSpec((tm, MAIN_W), lambda i: (i, 0))
    return pl.pallas_call(
        body, name=name, grid=(T // tm,), in_specs=[blk, blk, pl.BlockSpec(memory_space=pl.ANY)], out_specs=blk,
        out_shape=jax.ShapeDtypeStruct(y_alias.shape, y_alias.dtype), input_output_aliases={2: 0},
        compiler_params=_cp(("parallel",)),
    )(o, lse, y_alias)


def _dil_combine_bwd(dy, o, lse, *, name, tm=256):
    T = o.shape[0]
    lane_r = lax.broadcasted_iota(jnp.int32, (256, 256), 0) // HEAD_DIM
    lane_c = lax.broadcasted_iota(jnp.int32, (256, 256), 1) // HEAD_DIM
    ones_bd = (lane_r == lane_c).astype(BF)

    def body(dy_ref, o_ref, l_ref, e_ref, do_ref, c_ref):
        dyv = dy_ref[...]
        alphas = _group_softmax(l_ref[...])
        prod = dyv * o_ref[...]
        e = e_ref[...]
        tot = jnp.zeros((tm, 256), F32)
        for gi in range(3):
            x = prod[:, gi * 256:(gi + 1) * 256]
            hi = x.astype(BF)
            lo = (x - hi.astype(F32)).astype(BF)
            dalpha = jnp.dot(hi, e, preferred_element_type=F32) + jnp.dot(lo, e, preferred_element_type=F32)
            tot = tot + alphas[gi] * dalpha
        a = jnp.concatenate(alphas, axis=1)
        do_ref[...] = (dyv * a).astype(do_ref.dtype)
        c_ref[...] = jnp.concatenate([-al * tot for al in alphas], axis=1)

    blk = pl.BlockSpec((tm, MAIN_W), lambda i: (i, 0))
    return pl.pallas_call(
        body, name=name, grid=(T // tm,),
        in_specs=[blk, blk, blk, pl.BlockSpec((256, 256), lambda i: (0, 0))], out_specs=[blk, blk],
        out_shape=[jax.ShapeDtypeStruct((T, MAIN_W), F32), jax.ShapeDtypeStruct((T, MAIN_W), F32)],
        compiler_params=_cp(("parallel",)),
    )(dy, o, lse, ones_bd)


def _loss(y, target, *, name, tm=512):
    T, Dm = y.shape
    nt = T // tm

    def body(y_ref, t_ref, l_ref, d_ref, acc):
        i = pl.program_id(0)
        err = y_ref[...] - t_ref[...]
        d_ref[...] = err / Dm
        part = jnp.sum(jnp.mean(err * err, axis=-1, keepdims=True).reshape(tm // 8, 8, 1), axis=0)

        @pl.when(i == 0)
        def _():
            acc[...] = part

        @pl.when(i > 0)
        def _():
            acc[...] += part

        @pl.when(i == nt - 1)
        def _():
            l_ref[...] = 0.5 * jnp.sum(acc[...], axis=0, keepdims=True)

    row = pl.BlockSpec((tm, Dm), lambda i: (i, 0))
    return pl.pallas_call(
        body, name=name, grid=(nt,), in_specs=[row, row],
        out_specs=[pl.BlockSpec((1, 1), lambda i: (0, 0)), row],
        out_shape=[jax.ShapeDtypeStruct((1, 1), F32), jax.ShapeDtypeStruct((T, Dm), F32)],
        scratch_shapes=[pltpu.VMEM((8, 1), F32)],
        compiler_params=_cp(("arbitrary",)),
    )(y, target)


def _adamw(w, g, m, v, *, name):
    shape = w.shape
    cols = shape[-1]
    rows = w.size // cols
    tm = rows
    for cand in (512, 352, 256, 128):
        if rows > cand and rows % cand == 0 and cand * cols * 4 <= (1 << 20):
            tm = cand
            break

    def body(w_ref, g_ref, m_ref, v_ref, d_ref, mo_ref, vo_ref):
        gv = g_ref[...]
        mn = ADAM_B1 * m_ref[...] + (1.0 - ADAM_B1) * gv
        vn = ADAM_B2 * v_ref[...] + (1.0 - ADAM_B2) * (gv * gv)
        m_hat = mn / (1.0 - ADAM_B1 ** ADAM_STEP)
        v_hat = vn / (1.0 - ADAM_B2 ** ADAM_STEP)
        d_ref[...] = -ADAM_LR * (m_hat / (jnp.sqrt(v_hat) + ADAM_EPS) + ADAM_WD * w_ref[...])
        mo_ref[...] = mn
        vo_ref[...] = vn

    blk = pl.BlockSpec((tm, cols), lambda i: (i, 0))
    outs = pl.pallas_call(
        body, name=name, grid=(rows // tm,), in_specs=[blk] * 4, out_specs=[blk] * 3,
        out_shape=[jax.ShapeDtypeStruct((rows, cols), F32)] * 3,
        compiler_params=_cp(("parallel",)),
    )(*[t.reshape(rows, cols) for t in (w, g, m, v)])
    return tuple(t.reshape(shape) for t in outs)


def _adamw_layer(name, l, w, g, m, v, prev, after=None):
    L, rows, cols = w.shape
    tm = rows
    for cand in (512, 352, 256, 176, 128, 64):
        if rows % cand == 0 and cand * cols * 4 <= (1 << 21):
            tm = cand
            break
    if prev is None:
        prev = tuple(lax.empty(w.shape, F32) for _ in range(4))

    n_after = 0 if after is None else 1

    def body(w_ref, g_ref, m_ref, v_ref, *rest):
        d_ref, mo_ref, vo_ref, go_ref = rest[4 + n_after:]
        gv = g_ref[...]
        mn = ADAM_B1 * m_ref[...] + (1.0 - ADAM_B1) * gv
        vn = ADAM_B2 * v_ref[...] + (1.0 - ADAM_B2) * (gv * gv)
        m_hat = mn / (1.0 - ADAM_B1 ** ADAM_STEP)
        v_hat = vn / (1.0 - ADAM_B2 ** ADAM_STEP)
        d_ref[...] = -ADAM_LR * (m_hat / (jnp.sqrt(v_hat) + ADAM_EPS) + ADAM_WD * w_ref[...])
        mo_ref[...] = mn
        vo_ref[...] = vn
        go_ref[...] = gv

    lay = pl.BlockSpec((None, tm, cols), lambda i: (l, i, 0))
    one = pl.BlockSpec((None, tm, cols), lambda i: (0, i, 0))
    return tuple(pl.pallas_call(
        body, name=f"l{l}_adamw_{name}", grid=(rows // tm,),
        in_specs=[lay, one, lay, lay] + [pl.BlockSpec(memory_space=pl.ANY)] * (4 + n_after), out_specs=[lay] * 4,
        out_shape=[jax.ShapeDtypeStruct(w.shape, F32)] * 4,
        input_output_aliases={4 + i: i for i in range(4)},
        compiler_params=_cp(("parallel",)),
    )(w, g, m, v, *prev, *([] if after is None else [after])))


BIG = {
    'w_in': ((DEPTH, D_MODEL, D_MODEL), 'row'),
    'w_mem_kv': ((DEPTH, D_MODEL, 2 * MEM_W), 'row'),
    'w_out': ((DEPTH, D_MODEL, D_MODEL), 'row'),
    'w_kv': ((1, D_MODEL, 2 * MAIN_W), 'col'),
    'w_gate_up': ((DEPTH, D_MODEL, 2 * D_FF), 'col'),
    'w_down': ((DEPTH, D_FF, D_MODEL), 'row'),
}
BIG_NAMES = tuple(BIG)
N_CHIPS = 4
HBM_ANY = pl.BlockSpec(memory_space=pl.ANY)


def _geom(name):
    (L, R, C), kind = BIG[name]
    if kind == 'row':
        return L, R, C, kind, R // N_CHIPS, C, R // (2 * N_CHIPS)
    return L, R, C, kind, R, C // N_CHIPS, R // 2


def _shard_shape(name):
    L, R, C, kind, rs, cs, rh = _geom(name)
    return (L, rs, cs)


def _half_shape(name):
    L, R, C, kind, rs, cs, rh = _geom(name)
    return (L, rh, cs)


def _full_win(ref, name, s, h):
    L, R, C, kind, rs, cs, rh = _geom(name)
    if kind == 'row':
        rows = pl.ds(s * rs, rs) if h is None else pl.ds(s * rs + h * rh, rh)
        return ref.at[:, rows, :]
    rows = slice(None) if h is None else pl.ds(h * rh, rh)
    return ref.at[:, rows, pl.ds(s * cs, cs)]


def _shard_half(ref, name, h):
    L, R, C, kind, rs, cs, rh = _geom(name)
    return ref.at[:, pl.ds(h * rh, rh), :]


def _halves_win(ref, name, s):
    L, R, C, kind, rs, cs, rh = _geom(name)
    if kind == 'row':
        return ref.at[:, pl.ds(s * rh, rh), :]
    return ref.at[:, :, pl.ds(s * cs, cs)]


def _halves_shape(name):
    L, R, C, kind, rs, cs, rh = _geom(name)
    return (L, N_CHIPS * rh, cs) if kind == 'row' else (L, rh, C)


def _place():
    x, y, c = lax.axis_index("x"), lax.axis_index("y"), lax.axis_index("c")
    chips = [(1 - x, y), (x, 1 - y), (1 - x, 1 - y)]
    return x, y, c, chips


SMALL_ROWS = 24


SEM_SPEC = pl.BlockSpec(memory_space=pltpu.SEMAPHORE)
HBM_SPEC = pl.BlockSpec(memory_space=pltpu.HBM)
DATAFLOW = pltpu.SideEffectType.DATAFLOW_SIDE_EFFECTING


def _in_hbm(a):
    return pltpu.with_memory_space_constraint(a, pltpu.HBM)


def _remote(src, dst, send_sems, recv_sems, k, to):
    return pltpu.make_async_remote_copy(src_ref=src, dst_ref=dst, send_sem=send_sems.at[k], recv_sem=recv_sems.at[k],
                                        device_id=to, device_id_type=MESH)


def _split_start(name, bufs, n_copies, sends, after=None):
    nb = len(bufs)
    n_in = nb + (0 if after is None else 1)

    def body(*refs):
        in_refs = refs[:nb]
        send_sems, recv_sems = refs[n_in], refs[n_in + 1]
        token = refs[-1]
        for k, (src, dst, to) in enumerate(sends(in_refs)):
            _remote(src, dst, send_sems, recv_sems, k, to).start()
        token[...] = jnp.zeros_like(token)

    outs = pl.pallas_call(
        body, name=name,
        out_shape=(pltpu.SemaphoreType.DMA((n_copies,)), pltpu.SemaphoreType.DMA((n_copies,)),
                   *[pltpu.HBM(b.shape, b.dtype) for b in bufs], jax.ShapeDtypeStruct((8, 128), F32)),
        in_specs=[HBM_SPEC] * nb + [HBM_ANY] * (n_in - nb),
        out_specs=(SEM_SPEC, SEM_SPEC, *[HBM_SPEC] * nb, pl.BlockSpec(memory_space=pltpu.VMEM)),
        input_output_aliases={i: 2 + i for i in range(nb)},
        compiler_params=pltpu.CompilerParams(has_side_effects=DATAFLOW),
    )(*[_in_hbm(b) for b in bufs], *([] if after is None else [after]))
    return outs[0], outs[1], list(outs[2:2 + nb]), outs[-1]


def _split_wait(name, send_sems, recv_sems, bufs, after, sends, arrivals):
    nb = len(bufs)

    def body(*refs):
        in_refs = refs[:nb]
        s_sems, r_sems = refs[nb], refs[nb + 1]
        me = (lax.axis_index("x"), lax.axis_index("y"), lax.axis_index("c"))
        for k, (src, dst, to) in enumerate(sends(in_refs)):
            _remote(src, dst, s_sems, r_sems, k, to).wait_send()
        for k, win in enumerate(arrivals(in_refs)):
            _remote(win, win, s_sems, r_sems, k, me).wait_recv()

    outs = pl.pallas_call(
        body, name=name,
        out_shape=[pltpu.HBM(b.shape, b.dtype) for b in bufs],
        in_specs=[HBM_SPEC] * nb + [SEM_SPEC, SEM_SPEC, HBM_ANY],
        out_specs=[HBM_SPEC] * nb,
        input_output_aliases={i: i for i in range(nb)},
        compiler_params=pltpu.CompilerParams(has_side_effects=DATAFLOW),
    )(*bufs, send_sems, recv_sems, after)
    return list(outs)


MIX_W = ('w_in', 'w_mem_kv', 'w_out')
FFN_W = ('w_gate_up', 'w_down')
LAYER_W = MIX_W + FFN_W


def _place_own(tag, names, sources, small, sc):
    nw = len(names)
    has_small = small is not None

    def body(sc_ref, *refs):
        srcs = refs[:nw]
        shard_out = refs[nw + has_small:2 * nw + has_small]
        full_out = refs[2 * nw + has_small:3 * nw + has_small]
        for src, sh, fu in zip(srcs, shard_out, full_out):
            v = src[...].astype(BF)
            sh[...] = v
            fu[...] = v
        if has_small:
            refs[-1][...] = refs[nw][...]

    in_specs, shard_specs, full_specs, shard_shape, full_shape, ops = [], [], [], [], [], []
    for nm, (arr, layer) in zip(names, sources):
        L, R, C, kind, rs, cs, rh = _geom(nm)
        in_specs.append(pl.BlockSpec((1, rs, cs), lambda i, sc_ref, layer=layer: (layer, 0, 0)))
        shard_specs.append(pl.BlockSpec((1, rs, cs), lambda i, sc_ref: (0, 0, 0)))
        if kind == 'row':
            full_specs.append(pl.BlockSpec((1, rs, cs), lambda i, sc_ref: (0, sc_ref[0], 0)))
        else:
            full_specs.append(pl.BlockSpec((1, rs, cs), lambda i, sc_ref: (0, 0, sc_ref[0])))
        shard_shape.append(jax.ShapeDtypeStruct((1, rs, cs), BF))
        full_shape.append(jax.ShapeDtypeStruct((1, R, C), BF))
        ops.append(arr)
    if has_small:
        in_specs.append(pl.BlockSpec((SMALL_ROWS, 256), lambda i, sc_ref: (0, 0)))
        full_specs.append(pl.BlockSpec((None, SMALL_ROWS, 256), lambda i, sc_ref: (sc_ref[0], 0, 0)))
        full_shape.append(jax.ShapeDtypeStruct((N_CHIPS, SMALL_ROWS, 256), F32))
        ops.append(small)
    outs = pl.pallas_call(
        body, name=f"{tag}_place_own_shard",
        grid_spec=pltpu.PrefetchScalarGridSpec(num_scalar_prefetch=1, grid=(1,), in_specs=in_specs,
                                               out_specs=shard_specs + full_specs),
        out_shape=shard_shape + full_shape,
        compiler_params=_cp(("arbitrary",)),
    )(sc, *ops)
    return list(outs[:nw]), list(outs[nw:])


def _gather_start(l, names, sources, small, sc, after=None):
    nw = len(names)
    has_small = small is not None
    shards, fulls = _place_own(l, names, sources, small, sc)
    bufs = list(shards) + ([small] if has_small else []) + list(fulls)
    n_src = nw + (1 if has_small else 0)

    def sends(refs):
        x, y, c, chips = _place()
        s = 2 * x + y
        out = []
        for (px, py) in chips:
            for wi, nm in enumerate(names):
                out.append((_shard_half(refs[wi], nm, c), _full_win(refs[n_src + wi], nm, s, c), (px, py, c)))
            if has_small:
                out.append((refs[nw], refs[n_src + nw].at[s], (px, py, c)))
        return out

    def arrivals(refs):
        x, y, c, chips = _place()
        out = []
        for (px, py) in chips:
            sp = 2 * px + py
            for wi, nm in enumerate(names):
                out.append(_full_win(refs[n_src + wi], nm, sp, c))
            if has_small:
                out.append(refs[n_src + nw].at[sp])
        return out

    n_copies = 3 * n_src
    send_sems, recv_sems, bufs, token = _split_start(f"{l}_gather_ici_start", bufs, n_copies, sends, after)
    return dict(l=l, names=names, has_small=has_small, sems=(send_sems, recv_sems), bufs=bufs, sends=sends,
                arrivals=arrivals, token=token)


def _gather_forward(st, after):
    l, names = st['l'], st['names']
    nw = len(names)
    n_src = nw + (1 if st['has_small'] else 0)
    bufs = _split_wait(f"{l}_gather_ici_wait", *st['sems'], st['bufs'], after, st['sends'], st['arrivals'])
    fulls = bufs[n_src:n_src + nw]
    small_all = bufs[n_src + nw] if st['has_small'] else None

    def sends(refs):
        x, y, c, chips = _place()
        out = []
        for (px, py) in chips:
            sp = 2 * px + py
            for wi, nm in enumerate(names):
                w = _full_win(refs[wi], nm, sp, c)
                out.append((w, w, (x, y, 1 - c)))
        return out

    def arrivals(refs):
        x, y, c, chips = _place()
        out = []
        for (px, py) in chips:
            sp = 2 * px + py
            for wi, nm in enumerate(names):
                out.append(_full_win(refs[wi], nm, sp, 1 - c))
        return out

    send_sems, recv_sems, fulls, token = _split_start(f"{l}_gather_d2d_start", fulls, 3 * nw, sends)
    return dict(l=l, names=names, sems=(send_sems, recv_sems), bufs=fulls, sends=sends, arrivals=arrivals,
                small_all=small_all, token=token)


def _gather_finish(st, after):
    fulls = _split_wait(f"{st['l']}_gather_d2d_wait", *st['sems'], st['bufs'], after, st['sends'], st['arrivals'])
    return dict(zip(st['names'], fulls)), st['small_all']


def _reduce_start(tag, names, grads):
    nw = len(names)
    recv = [lax.empty((1,) + _halves_shape(nm)[1:], F32) for nm in names]
    bufs = [grads[nm] for nm in names] + recv

    def windows(refs, half_of):
        x, y, c, _ = _place()
        h = half_of(c)
        out = []
        for wi, nm in enumerate(names):
            L, R, C, kind, rs, cs, rh = _geom(nm)
            if kind == 'row':
                for sp in range(N_CHIPS):
                    out.append((_full_win(refs[wi], nm, sp, h), _halves_win(refs[nw + wi], nm, sp)))
            else:
                out.append((refs[wi].at[:, pl.ds(h * rh, rh), :], refs[nw + wi]))
        return out

    def sends(refs):
        x, y, c, _ = _place()
        return [(src, dst, (x, y, 1 - c)) for src, dst in windows(refs, lambda c: 1 - c)]

    def arrivals(refs):
        return [dst for _, dst in windows(refs, lambda c: c)]

    n_copies = sum(N_CHIPS if BIG[nm][1] == 'row' else 1 for nm in names)
    send_sems, recv_sems, bufs, token = _split_start(tag + "_halves_start", bufs, n_copies, sends)
    return dict(tag=tag, names=names, sems=(send_sems, recv_sems), bufs=bufs, sends=sends, arrivals=arrivals, token=token)


def _reduce_mid(st, after, sc):
    tag, names = st['tag'], st['names']
    nw = len(names)
    bufs = _split_wait(tag + "_halves_wait", *st['sems'], st['bufs'], after, st['sends'], st['arrivals'])
    halves, own = [], []
    for wi, nm in enumerate(names):
        hb, ow = _add_halves(nm, bufs[wi], bufs[nw + wi], sc, tag)
        halves.append(hb)
        own.append(ow)
    pieces = [lax.empty((3, 1) + _half_shape(nm)[1:], BF) for nm in names]

    def sends(refs):
        x, y, c, chips = _place()
        out = []
        for j, (px, py) in enumerate(chips):
            for wi, nm in enumerate(names):
                out.append((_halves_win(refs[wi], nm, 2 * px + py), refs[nw + wi].at[j], (px, py, c)))
        return out

    def arrivals(refs):
        return [refs[nw + wi].at[j] for j in range(3) for wi in range(nw)]

    send_sems, recv_sems, bufs, token = _split_start(tag + "_pieces_start", halves + pieces, 3 * nw, sends)
    return dict(tag=tag, names=names, sems=(send_sems, recv_sems), bufs=bufs, sends=sends, arrivals=arrivals, own=own,
                token=token)


def _reduce_late(st, after, sc):
    tag, names = st['tag'], st['names']
    nw = len(names)
    bufs = _split_wait(tag + "_pieces_wait", *st['sems'], st['bufs'], after, st['sends'], st['arrivals'])
    gsh = [_sum_pieces(nm, st['own'][wi], bufs[nw + wi], sc, tag) for wi, nm in enumerate(names)]

    def sends(refs):
        x, y, c, _ = _place()
        return [(_shard_half(refs[wi], nm, c), _shard_half(refs[wi], nm, c), (x, y, 1 - c)) for wi, nm in enumerate(names)]

    def arrivals(refs):
        x, y, c, _ = _place()
        return [_shard_half(refs[wi], nm, 1 - c) for wi, nm in enumerate(names)]

    send_sems, recv_sems, bufs, token = _split_start(tag + "_share_start", gsh, nw, sends)
    return dict(tag=tag, names=names, sems=(send_sems, recv_sems), bufs=bufs, sends=sends, arrivals=arrivals, token=token)


def _reduce_finish(st, after):
    gsh = _split_wait(st['tag'] + "_share_wait", *st['sems'], st['bufs'], after, st['sends'], st['arrivals'])
    return dict(zip(st['names'], gsh))


def _add_halves(name, g, r, sc, tag):
    _, R, C, kind, rs, cs, rh = _geom(name)
    L = g.shape[0]
    tr = rh
    nr = rh // tr

    def body(sc_ref, g_ref, r_ref, hb_ref, own_ref):
        sp = pl.program_id(2)
        tot = g_ref[...] + r_ref[...]
        hb_ref[...] = tot.astype(hb_ref.dtype)

        @pl.when(sp == sc_ref[0])
        def _():
            own_ref[...] = tot

    if kind == 'row':
        g_map = lambda l, ri, sp, sc_ref: (l, sp * 2 + sc_ref[1], 0)
        h_map = lambda l, ri, sp, sc_ref: (l, sp, 0)
    else:
        g_map = lambda l, ri, sp, sc_ref: (l, sc_ref[1] * nr + ri, sp)
        h_map = lambda l, ri, sp, sc_ref: (l, ri, sp)
    own_map = lambda l, ri, sp, sc_ref: (l, ri, 0)
    blk = (None, tr, cs)
    return pl.pallas_call(
        body, name=tag + "_add_halves_" + name,
        grid_spec=pltpu.PrefetchScalarGridSpec(
            num_scalar_prefetch=1, grid=(L, nr, N_CHIPS),
            in_specs=[pl.BlockSpec(blk, g_map), pl.BlockSpec(blk, h_map)],
            out_specs=[pl.BlockSpec(blk, h_map), pl.BlockSpec(blk, own_map)]),
        out_shape=[jax.ShapeDtypeStruct((L,) + _halves_shape(name)[1:], BF),
                   jax.ShapeDtypeStruct((L,) + _half_shape(name)[1:], F32)],
        compiler_params=_cp(("parallel", "parallel", "arbitrary")),
    )(sc, g, r)


def _sum_pieces(name, own, pieces, sc, tag):
    _, R, C, kind, rs, cs, rh = _geom(name)
    L = own.shape[0]
    tr = rh
    nr = rh // tr

    def body(sc_ref, o_ref, p_ref, out_ref):
        out_ref[...] = o_ref[...] + p_ref[0].astype(F32) + p_ref[1].astype(F32) + p_ref[2].astype(F32)

    blk = (None, tr, cs)
    return pl.pallas_call(
        body, name=tag + "_sum_pieces_" + name,
        grid_spec=pltpu.PrefetchScalarGridSpec(
            num_scalar_prefetch=1, grid=(L, nr),
            in_specs=[pl.BlockSpec(blk, lambda l, ri, sc_ref: (l, ri, 0)),
                      pl.BlockSpec((3, None, tr, cs), lambda l, ri, sc_ref: (0, l, ri, 0))],
            out_specs=pl.BlockSpec(blk, lambda l, ri, sc_ref: (l, sc_ref[1] * nr + ri, 0))),
        out_shape=jax.ShapeDtypeStruct((L,) + _shard_shape(name)[1:], F32),
        compiler_params=_cp(("parallel", "parallel")),
    )(sc, own, pieces)


def _small_gather_start(v, sc):
    rows = v.shape[0]

    def place(sc_ref, v_ref, o_ref):
        o_ref[...] = v_ref[...]

    slots = pl.pallas_call(
        place, name="small_grads_place_own",
        grid_spec=pltpu.PrefetchScalarGridSpec(
            num_scalar_prefetch=1, grid=(1,),
            in_specs=[pl.BlockSpec((rows, 128), lambda i, sc_ref: (0, 0))],
            out_specs=pl.BlockSpec((None, rows, 128), lambda i, sc_ref: (2 * sc_ref[0] + sc_ref[1], 0, 0))),
        out_shape=jax.ShapeDtypeStruct((8, rows, 128), v.dtype),
        compiler_params=_cp(("arbitrary",)),
    )(sc, v)

    def peers():
        x, y, c, _ = _place()
        flips = [(fx, fy, fc) for fx in (0, 1) for fy in (0, 1) for fc in (0, 1)][1:]
        return [((1 - x if fx else x), (1 - y if fy else y), (1 - c if fc else c)) for fx, fy, fc in flips]

    def sends(refs):
        x, y, c, _ = _place()
        return [(refs[0], refs[1].at[4 * x + 2 * y + c], p) for p in peers()]

    def arrivals(refs):
        return [refs[1].at[4 * px + 2 * py + pc] for px, py, pc in peers()]

    send_sems, recv_sems, bufs, token = _split_start("small_grads_gather_start", [v, slots], 7, sends)
    return dict(sems=(send_sems, recv_sems), bufs=bufs, sends=sends, arrivals=arrivals, token=token)


def _small_gather_finish(st, after):
    return _split_wait("small_grads_gather_wait", *st['sems'], st['bufs'], after, st['sends'], st['arrivals'])[1]


def _sum8(v8, *, name, tr=336):
    rows = v8.shape[1]
    tr = min(tr, rows)
    assert rows % tr == 0

    def body(v_ref, o_ref):
        tot = v_ref[0].astype(F32)
        for d in range(1, 8):
            tot = tot + v_ref[d].astype(F32)
        o_ref[...] = tot

    return pl.pallas_call(
        body, name=name, grid=(rows // tr,),
        in_specs=[pl.BlockSpec((8, tr, 128), lambda i: (0, i, 0))], out_specs=pl.BlockSpec((tr, 128), lambda i: (i, 0)),
        out_shape=jax.ShapeDtypeStruct((rows, 128), F32),
        compiler_params=_cp(("parallel",)),
    )(v8)


def _block_diag(w_pool_l):
    wbd = jnp.zeros((MAIN_W, MAIN_W), F32)
    for gi in range(len(POOL_WINDOWS)):
        wbd = lax.dynamic_update_slice(wbd, w_pool_l[gi], (gi * POOL_GROUP, gi * POOL_GROUP))
    return wbd.astype(BF)


def _unpack_small(small_all):
    ng = small_all[:, :16, :].reshape(N_CHIPS, DEPTH, 4, 256).transpose(1, 2, 0, 3).reshape(DEPTH, 4, D_MODEL)
    ps = small_all[:, 16:18, :POOL_GROUP].transpose(1, 0, 2).reshape(N_A, MAIN_W)
    return ng, ps


def _local_step(x, mem, positions, on_forward, on_backward, mem_norm, w_pool, kv_norm, target):
    B, S, _ = x.shape
    T = B * S
    xc = x.reshape(T, D_MODEL)
    memf = mem.reshape(B * N_MEM, D_MODEL)
    tgt = target.reshape(T, D_MODEL)
    cos, sin = _rope_tables(positions.reshape(T, 1), name="rope_tables")
    wbd = [_block_diag(w_pool[l]) for l in range(N_A)]
    nbo = D_FF // 256
    fw = []
    rk = rv = None
    kv_saved = None
    wts = []
    norm_gains = pool_scale = y2 = None

    for l in range(DEPTH):
        t = f"l{l}_"
        got = on_forward('start', l, y2)
        wts.append(dict(got[0]))
        if l == 0:
            norm_gains, pool_scale = _unpack_small(got[1])
        sv = {'x_in': xc}
        h0 = _norm_fwd(xc, norm_gains[l, 0], name=t + "norm0", out_dtype=BF, tm=1024, after=got[2])
        z, = _mm(h0, wts[l]['w_in'], b_layer=0, name=t + "mm_in", tm=1024, tn=1024)
        memn = _norm_fwd(memf, mem_norm[l], name=t + "norm_mem", out_dtype=BF, tm=256)
        kvm, = _mm(memn, wts[l]['w_mem_kv'], b_layer=0, name=t + "mm_memkv", out_dtypes=(BF,))
        if l < N_A:
            ycat, sv['p'] = _pool_fwd(z, wbd[l], pool_scale[l], B, S, name=t + "pool_fwd")
        else:
            rq = _rope_apply(z, cos, sin, name=t + "rope_q", out_dtype=F32, tm=1024)
            o = lax.empty((T, MAIN_W), F32)
            lse = lax.empty((T, MAIN_W), F32)
            for g in range(3):
                o, lse = _dil_fwd(g, rq, rk, rv, o, lse, B, S, name=t + f"dil_fwd{g}")
            ycat = _dil_combine_fwd(o, lse, lax.empty((T, D_MODEL), BF), name=t + "dil_combine", tm=1024)
            sv.update(rq=rq, o=o, lse=lse)
        ycat, sv['lse_m'] = _memattn_fwd(z, kvm, ycat, B, S, name=t + "memattn_fwd")
        tok = on_forward('mid', l, ycat)
        y1, = _mm(ycat, wts[l]['w_out'], b_layer=0, name=t + "mm_out", tm=1024, tn=1024)
        wts[l].update(on_forward('ffn', l, y1)[0])
        x1 = _norm_fwd(y1, norm_gains[l, 1], name=t + "norm1", res=xc, after=tok)
        h2 = _norm_fwd(x1, norm_gains[l, 2], name=t + "norm2", out_dtype=BF, tm=1024)
        gg, uu, aa = _mm(h2, wts[l]['w_gate_up'], b_layer=0, b_offsets=(0, nbo), out_n=D_FF, tm=4096, tn=256, name=t + "mm_gate_up",
                         epilogue=_swiglu_fwd_epilogue, out_dtypes=(BF, BF, BF))
        on_forward('post', l, gg)
        y2, = _mm(aa, wts[l]['w_down'], b_layer=0, tk=D_FF, name=t + "mm_down")
        x2 = _norm_fwd(y2, norm_gains[l, 3], name=t + "norm3", res=x1)
        sv.update(h0=h0, z=z, memn=memn, kvm=kvm, ycat=ycat, y1=y1, x1=x1, h2=h2, gg=gg, uu=uu, aa=aa, y2=y2)
        fw.append(sv)
        xc = x2
        if l == N_A - 1:
            kvn = _norm_fwd(xc, kv_norm, name="norm_kv", out_dtype=BF, tm=1024)
            kv, = _mm(kvn, wts[N_A - 1]['w_kv'], b_layer=0, name="mm_kv")
            rk, rv = _rope_apply(kv, cos, sin, name="rope_k", passthrough=True, out_dtype=F32, tm=1024)
            kv_saved = (xc, kvn)

    loss, dx = _loss(xc, tgt, name="loss", tm=1024)

    d_ng = [[None] * 4 for _ in range(DEPTH)]
    d_memnorm = [None] * DEPTH
    d_wbd = [None] * N_A
    d_pscale = [None] * N_A
    d_kvnorm = None
    kv_parts = []
    tok = None

    def as3d(gl):
        return {nm: g.reshape((1,) + g.shape) for nm, g in gl.items()}

    for l in reversed(range(DEPTH)):
        t = f"l{l}_b_"
        sv = fw[l]
        gl = {}
        dy2, d_ng[l][3] = _norm_bwd(dx, sv['y2'], norm_gains[l, 3], name=t + "norm3", out_dtype=BF, tm=1024, after=tok)
        gl['w_down'], = _mm(sv['aa'], dy2, ta=True, tm=1408, tn=512, tk=4096, name=t + "dw_down")
        dg, du = _mm(dy2, wts[l]['w_down'], tb=True, b_layer=0, tm=1024, tn=1408, name=t + "d_act",
                     extras=((sv['gg'], 'tile'), (sv['uu'], 'tile')), epilogue=_swiglu_bwd_epilogue, out_dtypes=(BF, BF))
        gl['w_gate_up'], = _mm(sv['h2'], (dg, du), ta=True, tn=1408, tk=1024, name=t + "dw_gate_up")
        dh2, = _mm((dg, du), wts[l]['w_gate_up'], tb=True, b_layer=0, tn=1024, tk=1408, name=t + "d_h2")
        dx1, d_ng[l][2] = _norm_bwd(dh2, sv['x1'], norm_gains[l, 2], name=t + "norm2", add=dx, tm=1024)
        tok = on_backward('ffn', l, dx1, as3d(gl))
        dy1, d_ng[l][1] = _norm_bwd(dx1, sv['y1'], norm_gains[l, 1], name=t + "norm1", out_dtype=BF, tm=1024, after=tok)
        gl['w_out'], = _mm(sv['ycat'], dy1, ta=True, name=t + "dw_out", tk=4096)
        dycat, = _mm(dy1, wts[l]['w_out'], tb=True, b_layer=0, name=t + "d_ycat", tm=1024, tn=1024)
        dz = lax.empty((T, D_MODEL), BF)
        dz, dkm, dvm = _memattn_bwd(dycat, sv['z'], sv['kvm'], sv['lse_m'], dz, B, S, name=t + "memattn")
        if l < N_A:
            dz, d_wbd[l], d_pscale[l] = _pool_bwd(dycat, sv['p'], wbd[l], pool_scale[l], dz, B, S, name=t + "pool")
        else:
            do, cb = _dil_combine_bwd(dycat, sv['o'], sv['lse'], name=t + "dil_combine", tm=512)
            acc = tuple(lax.empty((T, MAIN_W), F32) for _ in range(3))
            for g in range(3):
                acc = _dil_bwd(g, sv['rq'], rk, rv, do, cb, sv['lse'], acc, B, S, name=t + f"dil{g}")
            dz = _rope_apply(acc[0], cos, sin, name=t + "rope_q", sign=-1.0, alias=dz, tm=1024)
            kv_parts.append(acc[1:])
        tok = on_backward('mix', l, dz, as3d(gl))
        gl['w_in'], = _mm(sv['h0'], dz, ta=True, name=t + "dw_in", tk=4096)
        dh0, = _mm(dz, wts[l]['w_in'], tb=True, b_layer=0, name=t + "d_h0", out_dtypes=(BF,), tm=1024, tn=1024)
        dx, d_ng[l][0] = _norm_bwd(dh0, sv['x_in'], norm_gains[l, 0], name=t + "norm0", add=dx1, tm=1024, after=tok)
        gl['w_mem_kv'], = _mm(sv['memn'], (dkm, dvm), ta=True, tn=256, name=t + "dw_memkv")
        dmemn, = _mm((dkm, dvm), wts[l]['w_mem_kv'], tb=True, b_layer=0, tk=256, name=t + "d_memn", out_dtypes=(BF,))
        _, d_memnorm[l] = _norm_bwd(dmemn, memf, mem_norm[l], name=t + "norm_mem", out_dtype=BF, tm=256)
        if l == N_A:
            dk, dv = _kv_grad_sum(kv_parts, cos, sin, name="kv_grad", tm=1024)
            x_kv, kvn = kv_saved
            gl['w_kv'], = _mm(kvn, (dk, dv), ta=True, tn=768, tk=2048, name="dw_kv")
            dkvn, = _mm((dk, dv), wts[N_A - 1]['w_kv'], tb=True, b_layer=0, tn=1024, tk=768, name="d_kvn", out_dtypes=(BF,))
            dx, d_kvnorm = _norm_bwd(dkvn, x_kv, kv_norm, name="norm_kv_b", add=dx, tm=1024)
        tok = on_backward('end', l, dx, as3d(gl))

    small = {
        'norm_gains': jnp.stack([jnp.concatenate(d_ng[l], axis=0) for l in range(DEPTH)]),
        'mem_norm': jnp.concatenate(d_memnorm, axis=0),
        'kv_norm': d_kvnorm.reshape(D_MODEL),
        'pool_scale': jnp.concatenate(d_pscale, axis=0),
        'w_pool': jnp.stack([jnp.stack([d_wbd[l][gi * POOL_GROUP:(gi + 1) * POOL_GROUP, gi * POOL_GROUP:(gi + 1) * POOL_GROUP]
                                        for gi in range(len(POOL_WINDOWS))]) for l in range(N_A)]),
    }
    return loss, dx, small


SMALL_ORDER = ('norm_gains', 'mem_norm', 'kv_norm', 'pool_scale', 'w_pool')
SMALL_VEC_ROWS = 2560


def kernel(x, mem, positions, norm_gains, mem_norm, w_in, w_mem_kv, w_out, w_pool, pool_scale, kv_norm, w_kv, w_gate_up, w_down, loss_target, m_norm_gains, m_mem_norm, m_w_in, m_w_mem_kv, m_w_out, m_w_pool, m_pool_scale, m_kv_norm, m_w_kv, m_w_gate_up, m_w_down, v_norm_gains, v_mem_norm, v_w_in, v_w_mem_kv, v_w_out, v_w_pool, v_pool_scale, v_kv_norm, v_w_kv, v_w_gate_up, v_w_down):
    xi, yi, ci = lax.axis_index("x"), lax.axis_index("y"), lax.axis_index("c")
    s = 2 * xi + yi
    sc = jnp.stack([s, ci]).astype(jnp.int32)
    weights = dict(norm_gains=norm_gains, mem_norm=mem_norm, w_in=w_in, w_mem_kv=w_mem_kv, w_out=w_out, w_pool=w_pool,
                   pool_scale=pool_scale, kv_norm=kv_norm, w_kv=w_kv, w_gate_up=w_gate_up, w_down=w_down)
    moms = dict(norm_gains=m_norm_gains, mem_norm=m_mem_norm, w_in=m_w_in, w_mem_kv=m_w_mem_kv, w_out=m_w_out,
                w_pool=m_w_pool, pool_scale=m_pool_scale, kv_norm=m_kv_norm, w_kv=m_w_kv, w_gate_up=m_w_gate_up,
                w_down=m_w_down)
    vels = dict(norm_gains=v_norm_gains, mem_norm=v_mem_norm, w_in=v_w_in, w_mem_kv=v_w_mem_kv, w_out=v_w_out,
                w_pool=v_w_pool, pool_scale=v_pool_scale, kv_norm=v_kv_norm, w_kv=v_w_kv, w_gate_up=v_w_gate_up,
                w_down=v_w_down)

    small_w = jnp.zeros((SMALL_ROWS, 256), F32)
    small_w = lax.dynamic_update_slice(small_w, norm_gains.reshape(16, 256), (0, 0))
    small_w = lax.dynamic_update_slice(small_w, pool_scale, (16, 0))

    def shard_of(nm, l):
        return (w_kv.reshape(_shard_shape('w_kv')), 0) if nm == 'w_kv' else (weights[nm], l)

    groups = {'l0a': (0, MIX_W), 'l0b': (0, FFN_W)}
    groups.update({f"l{l}": (l, LAYER_W + (('w_kv',) if l == N_A - 1 else ())) for l in range(1, DEPTH)})
    on_ici, on_d2d, gathered = {}, {}, {}

    def start_group(tag, after):
        l, names = groups[tag]
        on_ici[tag] = _gather_start(tag, names, [shard_of(nm, l) for nm in names], small_w if tag == 'l0a' else None, sc,
                                    after)
        return [on_ici[tag]['token']]

    def on_forward(where, l, after):
        if where == 'start':
            if l == 0:
                start_group('l0a', None)
                st = on_ici.pop('l0a')
                fwd = _gather_forward(st, st['token'])
                w, small_all = _gather_finish(fwd, fwd['token'])
                return w, small_all, start_group('l0b', w['w_in'])
            if f"l{l}" not in on_d2d:
                on_d2d[f"l{l}"] = _gather_forward(on_ici.pop(f"l{l}"), after)
            gathered[l] = _gather_finish(on_d2d.pop(f"l{l}"), after)[0]
            tok = start_group(f"l{l + 1}", gathered[l]['w_in']) if l + 1 < DEPTH else None
            return {nm: w for nm, w in gathered[l].items() if nm not in FFN_W}, None, tok
        if where == 'mid' and l == 0:
            on_d2d['l0b'] = _gather_forward(on_ici.pop('l0b'), after)
            return start_group('l1', on_d2d['l0b']['token'])
        if where == 'ffn':
            if l == 0:
                return (_gather_finish(on_d2d.pop('l0b'), after)[0],)
            return ({nm: gathered[l][nm] for nm in FFN_W},)
        if where == 'post' and 0 < l < DEPTH - 1:
            on_d2d[f"l{l + 1}"] = _gather_forward(on_ici.pop(f"l{l + 1}"), after)
        return None

    hook_of = {'ffn': 0, 'mix': 1, 'end': 2}
    active, reduced = [], {l: {} for l in range(DEPTH)}
    advance = {'mid': lambda st, after: _reduce_mid(st, after, sc), 'late': lambda st, after: _reduce_late(st, after, sc)}

    def run_hook(idx, after):
        toks = []
        for grp in list(active):
            while grp['plan'] and grp['plan'][0][1] <= idx:
                step = grp['plan'].pop(0)[0]
                if step == 'finish':
                    reduced[grp['layer']].update(_reduce_finish(grp['st'], after))
                    active.remove(grp)
                else:
                    grp['st'] = advance[step](grp['st'], after)
                    toks.append(grp['st']['token'])
        return toks

    def on_backward(where, l, after, grads):
        idx = 3 * (DEPTH - 1 - l) + hook_of[where]
        toks = run_hook(idx, after)
        if where == 'end' or (where == 'ffn' and l == 0):
            names = FFN_W if where == 'ffn' else tuple(nm for nm in grads if l > 0 or nm not in FFN_W)
            st = _reduce_start(f"l{l}_{where}_grads", names, {nm: grads[nm] for nm in names})
            plan = [('mid', idx + 1), ('late', idx + 3), ('finish', idx + 4)] if where == 'ffn' else \
                   [('mid', idx + 1), ('late', idx + 2), ('finish', idx + 3)]
            active.append(dict(layer=l, st=st, plan=plan))
            toks.append(st['token'])
        return toks

    loss, gx, gsmall = _local_step(x, mem, positions, on_forward, on_backward, mem_norm, w_pool, kv_norm, loss_target)
    loss = lax.psum(loss[0, 0], ("x", "y", "c"))

    vec = jnp.concatenate([gsmall[nm].reshape(-1) for nm in SMALL_ORDER])
    vec = jnp.pad(vec, (0, SMALL_VEC_ROWS * 128 - vec.shape[0])).reshape(SMALL_VEC_ROWS, 128)
    vec = vec + sum(grp['st']['token'][0, 0] for grp in active)
    small_st = _small_gather_start(vec.astype(BF), sc)
    outs = {nm: None for nm in LAYER_W}

    def adamw_layers(layers, names, after):
        for l in layers:
            for nm in names:
                outs[nm] = _adamw_layer(nm, l, weights[nm], reduced[l][nm], moms[nm], vels[nm], outs[nm], after)
                after = outs[nm][0]
        return after

    def zero_of(toks, st):
        return sum(toks) if toks else st['token']

    last = 3 * DEPTH
    toks = run_hook(last, small_st['token'])
    done = adamw_layers(range(DEPTH - 1, 0, -1), LAYER_W, zero_of(toks, small_st))
    toks = run_hook(last + 1, done)
    done = adamw_layers([0], FFN_W, zero_of(toks, small_st))
    tot = _sum8(_small_gather_finish(small_st, done), name="sum_small_grads", tr=512)
    run_hook(last + 2, tot)
    assert not active
    adamw_layers([0], MIX_W, None)
    tot = tot.reshape(-1)
    grads, off = {}, 0
    for nm in SMALL_ORDER:
        shape = (DEPTH, 4, D_MODEL) if nm == 'norm_gains' else (N_A, MAIN_W) if nm == 'pool_scale' else weights[nm].shape
        n = 1
        for dim in shape:
            n *= dim
        grads[nm] = tot[off:off + n].reshape(shape)
        off += n
    grads['norm_gains'] = lax.dynamic_slice(grads['norm_gains'], (0, 0, s * 256), (DEPTH, 4, 256))
    grads['pool_scale'] = lax.dynamic_slice(grads['pool_scale'], (0, s * POOL_GROUP), (N_A, POOL_GROUP))
    grads['w_kv'] = reduced[N_A]['w_kv'].reshape(w_kv.shape)

    order = ('norm_gains', 'mem_norm', 'w_in', 'w_mem_kv', 'w_out', 'w_pool', 'pool_scale', 'kv_norm', 'w_kv',
             'w_gate_up', 'w_down')
    deltas, new_m, new_v = {}, {}, {}
    for nm in order:
        if nm in LAYER_W:
            deltas[nm], new_m[nm], new_v[nm], grads[nm] = outs[nm]
        else:
            deltas[nm], new_m[nm], new_v[nm] = _adamw(weights[nm], grads[nm], moms[nm], vels[nm], name="adamw_" + nm)
    return (loss, gx.reshape(x.shape), *[grads[nm] for nm in order], *[deltas[nm] for nm in order],
            *[new_m[nm] for nm in order], *[new_v[nm] for nm in order])
```

```python
import jax
import jax.numpy as jnp
from jax import lax
from jax.experimental import pallas as pl
from jax.experimental.pallas import tpu as pltpu

F32 = jnp.float32
BF = jnp.bfloat16

D_MODEL = 1024
DEPTH = 4
N_A = 2
HEAD_DIM = 64
MEM_W = 256
MAIN_W = 768
D_FF = 2816
N_MEM = 256
POOL_WINDOWS = (2, 4, 8, 16)
POOL_GROUP = 192
DIL = (1, 4, 16)
STEPS = 128
ROPE_THETA = 10000.0
EPS = 1e-6
SCALE = HEAD_DIM ** -0.5
NEG = -1e30

ADAM_LR = 0.001
ADAM_B1 = 0.9
ADAM_B2 = 0.999
ADAM_EPS = 1e-08
ADAM_WD = 0.01
ADAM_STEP = 10

VMEM_LIMIT = 48 * 1024 * 1024
MESH = pl.DeviceIdType.MESH


def _cp(sem):
    return pltpu.CompilerParams(dimension_semantics=sem, vmem_limit_bytes=VMEM_LIMIT)


def _mm(a, b, *, name, ta=False, tb=False, tm=1024, tn=512, tk=1024, b_layer=None, b_offsets=(0,),
        extras=(), epilogue=None, out_dtypes=(F32,), out_n=None):
    a_pair = isinstance(a, (tuple, list))
    b_pair = isinstance(b, (tuple, list))
    a0 = a[0] if a_pair else a
    b0 = b[0] if b_pair else b
    a_rows, a_cols = a0.shape
    if a_pair:
        a_cols *= 2
    b_rows, b_cols = b0.shape[-2:]
    if b_pair:
        b_cols *= 2
    M, K = (a_cols, a_rows) if ta else (a_rows, a_cols)
    N = b_rows if tb else b_cols
    if out_n is not None:
        N = out_n
    tm, tn, tk = min(tm, M), min(tn, N), min(tk, K)
    assert M % tm == 0 and N % tn == 0 and K % tk == 0, (name, M, N, K, tm, tn, tk)
    nk = K // tk
    n_acc = len(b_offsets)

    if a_pair:
        a_half = (a0.shape[1] // (tm if ta else tk))
    if b_pair:
        b_half = (b0.shape[1] // (tk if tb else tn))

    def a_map(sel):
        def f(i, j, k):
            r, c = (k, i) if ta else (i, k)
            if a_pair:
                c = jnp.clip(c - sel * a_half, 0, a_half - 1)
            return (r, c)
        return f

    def b_map(sel, off):
        def f(i, j, k):
            r, c = (j + off, k) if tb else (k, j + off)
            if b_pair:
                c = jnp.clip(c - sel * b_half, 0, b_half - 1)
            if b_layer is not None:
                return (b_layer, r, c)
            return (r, c)
        return f

    a_blk = (tk, tm) if ta else (tm, tk)
    b_blk = (tn, tk) if tb else (tk, tn)
    if b_layer is not None:
        b_blk = (None,) + b_blk
    in_specs, operands = [], []
    for sel in range(2 if a_pair else 1):
        in_specs.append(pl.BlockSpec(a_blk, a_map(sel)))
        operands.append(a[sel] if a_pair else a)
    n_a = len(operands)
    for off in b_offsets:
        for sel in range(2 if b_pair else 1):
            in_specs.append(pl.BlockSpec(b_blk, b_map(sel, off)))
            operands.append(b[sel] if b_pair else b)
    n_b = len(operands) - n_a
    for arr, kind in extras:
        if kind == 'tile':
            in_specs.append(pl.BlockSpec((tm, tn), lambda i, j, k: (i, j)))
        elif kind == 'row':
            in_specs.append(pl.BlockSpec((tm, 1), lambda i, j, k: (i, 0)))
        else:
            in_specs.append(pl.BlockSpec((1, tn), lambda i, j, k: (0, j)))
        operands.append(arr)
    n_e = len(extras)
    n_o = len(out_dtypes)
    dims = (((0,) if ta else (1,), (1,) if tb else (0,)), ((), ()))
    in_place = nk > 1 and epilogue is None and n_acc == 1 and tuple(out_dtypes) == (F32,)

    def body(*refs):
        a_refs = refs[:n_a]
        b_refs = refs[n_a:n_a + n_b]
        e_refs = refs[n_a + n_b:n_a + n_b + n_e]
        n_in = n_a + n_b + n_e
        o_refs = refs[n_in:n_in + n_o]
        acc_refs = refs[n_in + n_o:]
        i, j, k = pl.program_id(0), pl.program_id(1), pl.program_id(2)
        if a_pair:
            cidx = i if ta else k
            av = jnp.where(cidx < a_half, a_refs[0][...], a_refs[1][...])
        else:
            av = a_refs[0][...]
        av = av.astype(BF)
        prods = []
        for q in range(n_acc):
            if b_pair:
                cidx = (k if tb else j) + b_offsets[q]
                bv = jnp.where(cidx < b_half, b_refs[2 * q][...], b_refs[2 * q + 1][...])
            else:
                bv = b_refs[q][...]
            prods.append(lax.dot_general(av, bv.astype(BF), dims, preferred_element_type=F32))

        def finish(accs):
            outs = epilogue(accs, *[r[...] for r in e_refs]) if epilogue is not None else accs
            for o_ref, o in zip(o_refs, outs):
                o_ref[...] = o.astype(o_ref.dtype)

        if nk == 1:
            finish(prods)
        elif in_place:
            @pl.when(k == 0)
            def _():
                o_refs[0][...] = prods[0]

            @pl.when(k > 0)
            def _():
                o_refs[0][...] += prods[0]
        else:
            @pl.when(k == 0)
            def _():
                for r, p in zip(acc_refs, prods):
                    r[...] = p

            @pl.when(k > 0)
            def _():
                for r, p in zip(acc_refs, prods):
                    r[...] += p

            @pl.when(k == nk - 1)
            def _():
                finish([r[...] for r in acc_refs])

    return pl.pallas_call(
        body, name=name,
        grid=(M // tm, N // tn, nk),
        in_specs=in_specs,
        out_specs=[pl.BlockSpec((tm, tn), lambda i, j, k: (i, j)) for _ in range(n_o)],
        out_shape=[jax.ShapeDtypeStruct((M, N), dt) for dt in out_dtypes],
        scratch_shapes=[pltpu.VMEM((tm, tn), F32) for _ in range(n_acc if nk > 1 and not in_place else 0)],
        compiler_params=_cp(("parallel", "parallel", "arbitrary")),
    )(*operands)


def _norm_fwd(x, g, *, name, res=None, out_dtype=F32, tm=512, after=None):
    T, Dm = x.shape
    has_res = res is not None
    after = list(after or [])

    def body(*refs):
        refs = refs[:len(refs) - 1 - len(after)] + refs[len(refs) - 1:]
        if has_res:
            x_ref, g_ref, r_ref, y_ref = refs
        else:
            x_ref, g_ref, y_ref = refs
        xv = x_ref[...]
        rstd = lax.rsqrt(jnp.mean(xv * xv, axis=-1, keepdims=True) + EPS)
        y = xv * rstd * g_ref[...]
        if has_res:
            y = r_ref[...] + y
        y_ref[...] = y.astype(y_ref.dtype)

    row = pl.BlockSpec((tm, Dm), lambda i: (i, 0))
    in_specs = [row, pl.BlockSpec((1, Dm), lambda i: (0, 0))] + ([row] if has_res else [])
    in_specs += [pl.BlockSpec(memory_space=pl.ANY)] * len(after)
    ops = [x, g.reshape(1, Dm)] + ([res] if has_res else []) + after
    return pl.pallas_call(
        body, name=name, grid=(T // tm,), in_specs=in_specs,
        out_specs=row,
        out_shape=jax.ShapeDtypeStruct((T, Dm), out_dtype),
        compiler_params=_cp(("parallel",)),
    )(*ops)


def _norm_bwd(dout, x, g, *, name, add=None, out_dtype=F32, tm=512, after=None):
    T, Dm = x.shape
    has_add = add is not None
    nt = T // tm
    after = list(after or [])

    def body(*refs):
        refs = refs[:len(refs) - 3 - len(after)] + refs[len(refs) - 3:]
        if has_add:
            do_ref, x_ref, g_ref, a_ref, dx_ref, dg_ref, acc = refs
        else:
            do_ref, x_ref, g_ref, dx_ref, dg_ref, acc = refs
        i = pl.program_id(0)
        do = do_ref[...].astype(F32)
        xv = x_ref[...]
        rstd = lax.rsqrt(jnp.mean(xv * xv, axis=-1, keepdims=True) + EPS)
        xh = xv * rstd
        gd = do * g_ref[...]
        dx = rstd * (gd - xh * jnp.mean(gd * xh, axis=-1, keepdims=True))
        if has_add:
            dx = dx + a_ref[...].astype(F32)
        dx_ref[...] = dx.astype(dx_ref.dtype)
        part = jnp.sum((do * xh).reshape(tm // 8, 8, Dm), axis=0)

        @pl.when(i == 0)
        def _():
            acc[...] = part

        @pl.when(i > 0)
        def _():
            acc[...] += part

        @pl.when(i == nt - 1)
        def _():
            dg_ref[...] = jnp.sum(acc[...], axis=0, keepdims=True)

    row = pl.BlockSpec((tm, Dm), lambda i: (i, 0))
    in_specs = [row, row, pl.BlockSpec((1, Dm), lambda i: (0, 0))]
    ops = [dout, x, g.reshape(1, Dm)]
    if has_add:
        in_specs.append(row)
        ops.append(add)
    in_specs += [pl.BlockSpec(memory_space=pl.ANY)] * len(after)
    ops += after
    return pl.pallas_call(
        body, name=name, grid=(nt,), in_specs=in_specs,
        out_specs=[row, pl.BlockSpec((1, Dm), lambda i: (0, 0))],
        out_shape=[jax.ShapeDtypeStruct((T, Dm), out_dtype), jax.ShapeDtypeStruct((1, Dm), F32)],
        scratch_shapes=[pltpu.VMEM((8, Dm), F32)],
        compiler_params=_cp(("arbitrary",)),
    )(*ops)


def _swiglu_fwd_epilogue(accs):
    g, u = accs
    return g, u, g * jax.nn.sigmoid(g) * u


def _swiglu_bwd_epilogue(accs, g, u):
    da = accs[0]
    g = g.astype(F32)
    u = u.astype(F32)
    sig = jax.nn.sigmoid(g)
    return da * u * (sig * (1.0 + g * (1.0 - sig))), da * (g * sig)


def _rope_tables(pos, *, name, tm=1024):
    T = pos.shape[0]
    half = HEAD_DIM // 2
    freqs = ROPE_THETA ** (-jnp.arange(half, dtype=F32) / half)
    freqs = jnp.tile(freqs, 4).reshape(1, 128)

    def body(p_ref, f_ref, c_ref, s_ref):
        ang = p_ref[...].astype(F32) * f_ref[...]
        lane = lax.broadcasted_iota(jnp.int32, ang.shape, 1)
        c_ref[...] = jnp.cos(ang)
        s_ref[...] = jnp.where(lane % HEAD_DIM < half, -1.0, 1.0) * jnp.sin(ang)

    tab = pl.BlockSpec((tm, 128), lambda i: (i, 0))
    return pl.pallas_call(
        body, name=name, grid=(T // tm,),
        in_specs=[pl.BlockSpec((tm, 1), lambda i: (i, 0)), pl.BlockSpec((1, 128), lambda i: (0, 0))],
        out_specs=[tab, tab],
        out_shape=[jax.ShapeDtypeStruct((T, 128), F32)] * 2,
        compiler_params=_cp(("parallel",)),
    )(pos, freqs)


def _rot(x, cos, sin, sign):
    W = x.shape[1]
    half = HEAD_DIM // 2
    reps = W // 128
    c = jnp.concatenate([cos] * reps, axis=1) if reps > 1 else cos
    s = jnp.concatenate([sin] * reps, axis=1) if reps > 1 else sin
    lane = lax.broadcasted_iota(jnp.int32, x.shape, 1)
    swapped = jnp.where(lane % HEAD_DIM < half, pltpu.roll(x, W - half, axis=1), pltpu.roll(x, half, axis=1))
    return x * c + (sign * s) * swapped


def _rope_apply(x, cos, sin, *, name, sign=1.0, width=MAIN_W, passthrough=False, out_dtype=BF, alias=None,
                out_cols=None, tm=512):
    T = x.shape[0]

    def body(*refs):
        if passthrough:
            x_ref, v_ref, c_ref, s_ref, o_ref, ov_ref = refs
            ov_ref[...] = v_ref[...].astype(ov_ref.dtype)
        elif alias is not None:
            x_ref, c_ref, s_ref, _, o_ref = refs
        else:
            x_ref, c_ref, s_ref, o_ref = refs
        o_ref[...] = _rot(x_ref[...].astype(F32), c_ref[...], s_ref[...], sign).astype(o_ref.dtype)

    blk0 = pl.BlockSpec((tm, width), lambda i: (i, 0))
    blk1 = pl.BlockSpec((tm, width), lambda i: (i, 1))
    tab = pl.BlockSpec((tm, 128), lambda i: (i, 0))
    if passthrough:
        return pl.pallas_call(
            body, name=name, grid=(T // tm,), in_specs=[blk0, blk1, tab, tab], out_specs=[blk0, blk0],
            out_shape=[jax.ShapeDtypeStruct((T, width), out_dtype)] * 2,
            compiler_params=_cp(("parallel",)),
        )(x, x, cos, sin)
    if alias is not None:
        return pl.pallas_call(
            body, name=name, grid=(T // tm,),
            in_specs=[blk0, tab, tab, pl.BlockSpec(memory_space=pl.ANY)], out_specs=blk0,
            out_shape=jax.ShapeDtypeStruct(alias.shape, alias.dtype),
            input_output_aliases={3: 0},
            compiler_params=_cp(("parallel",)),
        )(x, cos, sin, alias)
    return pl.pallas_call(
        body, name=name, grid=(T // tm,), in_specs=[blk0, tab, tab], out_specs=blk0,
        out_shape=jax.ShapeDtypeStruct((T, width), out_dtype),
        compiler_params=_cp(("parallel",)),
    )(x, cos, sin)


POOL_T = 512
POOL_HALO = 16


def _pool_lane_window(shape):
    lane = lax.broadcasted_iota(jnp.int32, shape, 1)
    w = jnp.full(shape, POOL_WINDOWS[0], jnp.int32)
    for gi in range(1, len(POOL_WINDOWS)):
        w = jnp.where(lane >= gi * POOL_GROUP, POOL_WINDOWS[gi], w)
    return w


POOL_PAD = 32
POOL_R = POOL_T + POOL_PAD


def _pool_window_sums(buf, tmp_a, tmp_b, win, back):
    src, dst, acc = buf, tmp_a, None
    for j, (w, sh) in enumerate(zip(POOL_WINDOWS, (1, 2, 4, 8)), start=1):
        n = POOL_R - 8 * j
        if back:
            dst[pl.ds(8 * j, n), :] = src[pl.ds(8 * j, n), :] + src[pl.ds(8 * j - sh, n), :]
            cur = dst[pl.ds(POOL_PAD, POOL_T), :]
        else:
            dst[pl.ds(0, n), :] = src[pl.ds(0, n), :] + src[pl.ds(sh, n), :]
            cur = dst[pl.ds(0, POOL_T), :]
        acc = cur if acc is None else jnp.where(win >= w, cur, acc)
        src, dst = dst, (tmp_b if dst is tmp_a else tmp_a)
    return acc


def _pool_fwd(z, wbd, scale, B, S, *, name):
    T = z.shape[0]
    nt = S // POOL_T
    hb = POOL_T // POOL_PAD

    def body(z_ref, h_ref, w_ref, sc_ref, y_ref, p_ref, ext, tmp_a, tmp_b):
        i = pl.program_id(1)
        u = z_ref[...]
        ext[pl.ds(POOL_PAD, POOL_T), :] = u
        ext[pl.ds(0, POOL_PAD), :] = jnp.where(i > 0, h_ref[...], 0.0)
        win = _pool_lane_window((POOL_T, MAIN_W))
        acc = _pool_window_sums(ext, tmp_a, tmp_b, win, True)
        t = i * POOL_T + lax.broadcasted_iota(jnp.int32, (POOL_T, MAIN_W), 0)
        cnt = jnp.minimum(t + 1, win).astype(F32)
        p = (acc / cnt - u).astype(BF)
        p_ref[...] = p
        y = jnp.dot(p, w_ref[...], preferred_element_type=F32) * sc_ref[...]
        y_ref[...] = y.astype(y_ref.dtype)

    return pl.pallas_call(
        body, name=name, grid=(B, nt),
        in_specs=[pl.BlockSpec((POOL_T, MAIN_W), lambda b, i: (b * nt + i, 0)),
                  pl.BlockSpec((POOL_PAD, MAIN_W), lambda b, i: (jnp.maximum((b * nt + i) * hb - 1, 0), 0)),
                  pl.BlockSpec((MAIN_W, MAIN_W), lambda b, i: (0, 0)),
                  pl.BlockSpec((1, MAIN_W), lambda b, i: (0, 0))],
        out_specs=[pl.BlockSpec((POOL_T, MAIN_W), lambda b, i: (b * nt + i, 0)),
                   pl.BlockSpec((POOL_T, MAIN_W), lambda b, i: (b * nt + i, 0))],
        out_shape=[jax.ShapeDtypeStruct((T, D_MODEL), BF), jax.ShapeDtypeStruct((T, MAIN_W), BF)],
        scratch_shapes=[pltpu.VMEM((POOL_R, MAIN_W), F32)] * 3,
        compiler_params=_cp(("parallel", "parallel")),
    )(z, z, wbd, scale.reshape(1, MAIN_W))


def _pool_bwd(dy, p, wbd, scale, dz_alias, B, S, *, name):
    T = dy.shape[0]
    nt = S // POOL_T
    hb = POOL_T // POOL_HALO
    last_halo = T // POOL_HALO - 1

    def body(dy_ref, dyn_ref, p_ref, pn_ref, w_ref, sc_ref, _, dz_ref, dw_ref, ds_ref, ext, tmp_a, tmp_b, dw_acc, ds_acc):
        b, i = pl.program_id(0), pl.program_id(1)
        first = jnp.logical_and(b == 0, i == 0)
        dyv = dy_ref[...]
        pv = p_ref[...]
        sc = sc_ref[...]
        w = w_ref[...]
        pw = jnp.dot(pv, w, preferred_element_type=F32)
        ds_part = jnp.sum((dyv * pw).reshape(POOL_T // 8, 8, MAIN_W), axis=0)
        dpw = (dyv * sc).astype(BF)
        dw_part = lax.dot_general(pv, dpw, (((0,), (0,)), ((), ())), preferred_element_type=F32)

        @pl.when(first)
        def _():
            dw_acc[...] = dw_part
            ds_acc[...] = ds_part

        @pl.when(jnp.logical_not(first))
        def _():
            dw_acc[...] += dw_part
            ds_acc[...] += ds_part

        @pl.when(jnp.logical_and(b == pl.num_programs(0) - 1, i == nt - 1))
        def _():
            dw_ref[...] = dw_acc[...]
            ds_ref[...] = jnp.sum(ds_acc[...], axis=0, keepdims=True)

        dp = lax.dot_general(dpw, w, (((1,), (1,)), ((), ())), preferred_element_type=F32)
        dpn = lax.dot_general((dyn_ref[...] * sc).astype(BF), w, (((1,), (1,)), ((), ())), preferred_element_type=F32)
        win = _pool_lane_window((POOL_T, MAIN_W))
        win_n = _pool_lane_window((POOL_HALO, MAIN_W))
        t = i * POOL_T + lax.broadcasted_iota(jnp.int32, (POOL_T, MAIN_W), 0)
        tn = (i + 1) * POOL_T + lax.broadcasted_iota(jnp.int32, (POOL_HALO, MAIN_W), 0)
        ext[pl.ds(0, POOL_T), :] = dp / jnp.minimum(t + 1, win).astype(F32)
        ext[pl.ds(POOL_T, POOL_HALO), :] = jnp.where(i < nt - 1, dpn / jnp.minimum(tn + 1, win_n).astype(F32), 0.0)
        ext[pl.ds(POOL_T + POOL_HALO, POOL_PAD - POOL_HALO), :] = jnp.zeros((POOL_PAD - POOL_HALO, MAIN_W), F32)
        acc = _pool_window_sums(ext, tmp_a, tmp_b, win, False) - dp
        dz_ref[...] = acc.astype(dz_ref.dtype)

    cur = lambda b, i: (b * nt + i, 0)
    nxt = lambda b, i: (jnp.minimum((b * nt + i + 1) * hb, last_halo), 0)
    return pl.pallas_call(
        body, name=name, grid=(B, nt),
        in_specs=[pl.BlockSpec((POOL_T, MAIN_W), cur), pl.BlockSpec((POOL_HALO, MAIN_W), nxt),
                  pl.BlockSpec((POOL_T, MAIN_W), cur), pl.BlockSpec((POOL_HALO, MAIN_W), nxt),
                  pl.BlockSpec((MAIN_W, MAIN_W), lambda b, i: (0, 0)),
                  pl.BlockSpec((1, MAIN_W), lambda b, i: (0, 0)),
                  pl.BlockSpec(memory_space=pl.ANY)],
        out_specs=[pl.BlockSpec((POOL_T, MAIN_W), cur),
                   pl.BlockSpec((MAIN_W, MAIN_W), lambda b, i: (0, 0)),
                   pl.BlockSpec((1, MAIN_W), lambda b, i: (0, 0))],
        out_shape=[jax.ShapeDtypeStruct(dz_alias.shape, dz_alias.dtype),
                   jax.ShapeDtypeStruct((MAIN_W, MAIN_W), F32), jax.ShapeDtypeStruct((1, MAIN_W), F32)],
        scratch_shapes=[pltpu.VMEM((POOL_R, MAIN_W), F32)] * 3 + [pltpu.VMEM((MAIN_W, MAIN_W), F32), pltpu.VMEM((8, MAIN_W), F32)],
        input_output_aliases={6: 0},
        compiler_params=_cp(("arbitrary", "arbitrary")),
    )(dy, dy, p, p, wbd, scale.reshape(1, MAIN_W), dz_alias)


def _head_masks(shape):
    lane = lax.broadcasted_iota(jnp.int32, shape, 1)
    return [(lane // HEAD_DIM) == h for h in range(shape[1] // HEAD_DIM)]


def _row_of(bcast, mask):
    return jnp.max(jnp.where(mask, bcast, -jnp.inf), axis=-1, keepdims=True)


MEM_TQ = 2048


def _memattn_fwd(z, kv, y_alias, B, S, *, name, tq=MEM_TQ):
    T = z.shape[0]
    nt = S // tq

    def body(q_ref, k_ref, v_ref, _, y_ref, l_ref):
        q = q_ref[...]
        k = k_ref[...]
        v = v_ref[...]
        masks = _head_masks(q.shape)
        o = jnp.zeros(q.shape, F32)
        lse_b = jnp.zeros(q.shape, F32)
        for m in masks:
            qm = jnp.where(m, q, 0.0).astype(BF)
            s = lax.dot_general(qm, k, (((1,), (1,)), ((), ())), preferred_element_type=F32) * SCALE
            mx = jnp.max(s, axis=-1, keepdims=True)
            e = jnp.exp(s - mx)
            l = jnp.sum(e, axis=-1, keepdims=True)
            p = (e / l).astype(BF)
            o = o + jnp.where(m, jnp.dot(p, v, preferred_element_type=F32), 0.0)
            lse_b = lse_b + jnp.where(m, mx + jnp.log(l), 0.0)
        y_ref[...] = o.astype(y_ref.dtype)
        l_ref[...] = lse_b

    qblk = pl.BlockSpec((tq, MEM_W), lambda b, i: (b * nt + i, 3))
    return pl.pallas_call(
        body, name=name, grid=(B, nt),
        in_specs=[qblk, pl.BlockSpec((N_MEM, MEM_W), lambda b, i: (b, 0)), pl.BlockSpec((N_MEM, MEM_W), lambda b, i: (b, 1)),
                  pl.BlockSpec(memory_space=pl.ANY)],
        out_specs=[qblk, pl.BlockSpec((tq, MEM_W), lambda b, i: (b * nt + i, 0))],
        out_shape=[jax.ShapeDtypeStruct(y_alias.shape, y_alias.dtype), jax.ShapeDtypeStruct((T, MEM_W), F32)],
        input_output_aliases={3: 0},
        compiler_params=_cp(("parallel", "parallel")),
    )(z, kv, kv, y_alias)


def _memattn_bwd(dy, z, kv, lse, dz_alias, B, S, *, name, tq=MEM_TQ):
    nt = S // tq

    def body(do_ref, q_ref, k_ref, v_ref, l_ref, _, dz_ref, dk_ref, dv_ref, dk_acc, dv_acc):
        i = pl.program_id(1)
        do = do_ref[...]
        q = q_ref[...]
        k = k_ref[...]
        v = v_ref[...]
        lse_b = l_ref[...]
        masks = _head_masks(q.shape)
        dq = jnp.zeros(q.shape, F32)
        dk = jnp.zeros(k.shape, F32)
        dv = jnp.zeros(v.shape, F32)
        for m in masks:
            qm = jnp.where(m, q, 0.0).astype(BF)
            dom = jnp.where(m, do, 0.0).astype(BF)
            s = lax.dot_general(qm, k, (((1,), (1,)), ((), ())), preferred_element_type=F32) * SCALE
            p = jnp.exp(s - _row_of(lse_b, m))
            dp = lax.dot_general(dom, v, (((1,), (1,)), ((), ())), preferred_element_type=F32)
            delta = jnp.sum(p * dp, axis=-1, keepdims=True)
            ds = (p * (dp - delta) * SCALE).astype(BF)
            pb = p.astype(BF)
            dv = dv + jnp.where(m[:N_MEM], lax.dot_general(pb, dom, (((0,), (0,)), ((), ())), preferred_element_type=F32), 0.0)
            dk = dk + jnp.where(m[:N_MEM], lax.dot_general(ds, qm, (((0,), (0,)), ((), ())), preferred_element_type=F32), 0.0)
            dq = dq + jnp.where(m, jnp.dot(ds, k, preferred_element_type=F32), 0.0)
        dz_ref[...] = dq.astype(dz_ref.dtype)

        @pl.when(i == 0)
        def _():
            dk_acc[...] = dk
            dv_acc[...] = dv

        @pl.when(i > 0)
        def _():
            dk_acc[...] += dk
            dv_acc[...] += dv

        @pl.when(i == nt - 1)
        def _():
            dk_ref[...] = dk_acc[...]
            dv_ref[...] = dv_acc[...]

    qblk = pl.BlockSpec((tq, MEM_W), lambda b, i: (b * nt + i, 3))
    kblk = pl.BlockSpec((N_MEM, MEM_W), lambda b, i: (b, 0))
    return pl.pallas_call(
        body, name=name, grid=(B, nt),
        in_specs=[qblk, qblk, kblk, pl.BlockSpec((N_MEM, MEM_W), lambda b, i: (b, 1)),
                  pl.BlockSpec((tq, MEM_W), lambda b, i: (b * nt + i, 0)), pl.BlockSpec(memory_space=pl.ANY)],
        out_specs=[qblk, kblk, kblk],
        out_shape=[jax.ShapeDtypeStruct(dz_alias.shape, dz_alias.dtype),
                   jax.ShapeDtypeStruct((B * N_MEM, MEM_W), F32), jax.ShapeDtypeStruct((B * N_MEM, MEM_W), F32)],
        scratch_shapes=[pltpu.VMEM((N_MEM, MEM_W), F32), pltpu.VMEM((N_MEM, MEM_W), F32)],
        input_output_aliases={5: 0},
        compiler_params=_cp(("parallel", "arbitrary")),
    )(dy, z, kv, kv, lse, dz_alias)


N_UNITS = 16


def _unit_rows(g):
    d = DIL[g]
    nb = N_UNITS // d
    return [pl.ds(n * STEPS * d + r, STEPS, stride=d) if d > 1 else pl.ds(n * STEPS, STEPS)
            for n in range(nb) for r in range(d)]


def _load_units(ref, g):
    if DIL[g] == 1:
        return ref[...].reshape(N_UNITS, STEPS, 128)
    return jnp.stack([ref[rows, :] for rows in _unit_rows(g)])


def _store_units(ref, val, g):
    if DIL[g] == 1:
        ref[...] = val.reshape(N_UNITS * STEPS, 128)
    else:
        for u, rows in enumerate(_unit_rows(g)):
            ref[rows, :] = val[u]


def _shift_units(x, by):
    z = jnp.zeros((abs(by),) + x.shape[1:], x.dtype)
    return jnp.concatenate([z, x[:N_UNITS - by]], axis=0) if by > 0 else jnp.concatenate([x[-by:], z], axis=0)


def _bdot(a, b, ca, cb):
    return lax.dot_general(a, b, (((ca,), (cb,)), ((0,), (0,))), preferred_element_type=F32)


def _dil_masks(g):
    d = DIL[g]
    has_prev = N_UNITS // d > 1
    qi = lax.broadcasted_iota(jnp.int32, (1, STEPS, STEPS), 1)
    kj = lax.broadcasted_iota(jnp.int32, (1, STEPS, STEPS), 2)
    unit = lax.broadcasted_iota(jnp.int32, (N_UNITS, 1, 1), 0)
    cur = kj <= qi
    prev = jnp.logical_and(kj >= qi, unit >= d) if has_prev else None
    lane = lax.broadcasted_iota(jnp.int32, (1, 1, 128), 2)
    heads = [(lane // HEAD_DIM) == h for h in range(128 // HEAD_DIM)]
    return has_prev, cur, prev, heads


def _dil_fwd(g, q, k, v, o_alias, l_alias, B, S, *, name):
    assert S == N_UNITS * STEPS
    d = DIL[g]

    def body(q_ref, k_ref, v_ref, _, __, o_ref, l_ref):
        has_prev, cur, prev, heads = _dil_masks(g)
        q = _load_units(q_ref, g)
        kc = _load_units(k_ref, g).astype(BF)
        vc = _load_units(v_ref, g).astype(BF)
        if has_prev:
            kp, vp = _shift_units(kc, d), _shift_units(vc, d)
        o = jnp.zeros(q.shape, F32)
        lse_b = jnp.zeros(q.shape, F32)
        for m in heads:
            qm = jnp.where(m, q, 0.0).astype(BF)
            sc = jnp.where(cur, _bdot(qm, kc, 2, 2) * SCALE, NEG)
            mx = jnp.max(sc, axis=-1, keepdims=True)
            if has_prev:
                sp = jnp.where(prev, _bdot(qm, kp, 2, 2) * SCALE, NEG)
                mx = jnp.maximum(mx, jnp.max(sp, axis=-1, keepdims=True))
            l = jnp.sum(jnp.exp(sc - mx), axis=-1, keepdims=True)
            if has_prev:
                l = l + jnp.sum(jnp.exp(sp - mx), axis=-1, keepdims=True)
            lse = mx + jnp.log(l)
            oh = _bdot(jnp.exp(sc - lse).astype(BF), vc, 2, 1)
            if has_prev:
                oh = oh + _bdot(jnp.exp(sp - lse).astype(BF), vp, 2, 1)
            o = o + jnp.where(m, oh, 0.0)
            lse_b = lse_b + jnp.where(m, lse, 0.0)
        _store_units(o_ref, o, g)
        _store_units(l_ref, lse_b, g)

    blk = pl.BlockSpec((S, 128), lambda b, hf: (b, g * 2 + hf))
    anyspec = pl.BlockSpec(memory_space=pl.ANY)
    o, l = pl.pallas_call(
        body, name=name, grid=(B, 2),
        in_specs=[blk, blk, blk, anyspec, anyspec], out_specs=[blk, blk],
        out_shape=[jax.ShapeDtypeStruct(q.shape, F32)] * 2,
        input_output_aliases={3: 0, 4: 1},
        compiler_params=_cp(("parallel", "parallel")),
    )(q, k, v, o_alias, l_alias)
    return o, l


def _dil_bwd(g, q, k, v, do, cb, lse, aliases, B, S, *, name):
    assert S == N_UNITS * STEPS
    d = DIL[g]

    def body(q_ref, k_ref, v_ref, do_ref, c_ref, l_ref, _, __, ___, dq_ref, dk_ref, dv_ref):
        has_prev, cur, prev, heads = _dil_masks(g)
        q = _load_units(q_ref, g)
        kc = _load_units(k_ref, g).astype(BF)
        vc = _load_units(v_ref, g).astype(BF)
        do = _load_units(do_ref, g)
        cbv = _load_units(c_ref, g)
        lse_b = _load_units(l_ref, g)
        if has_prev:
            kp, vp = _shift_units(kc, d), _shift_units(vc, d)
        z = jnp.zeros(q.shape, F32)
        dq, dkc, dkp, dvc, dvp = z, z, z, z, z
        for m in heads:
            qm = jnp.where(m, q, 0.0).astype(BF)
            dom = jnp.where(m, do, 0.0).astype(BF)
            lse = jnp.max(jnp.where(m, lse_b, -jnp.inf), axis=-1, keepdims=True)
            c = jnp.max(jnp.where(m, cbv, -jnp.inf), axis=-1, keepdims=True)
            sc = jnp.where(cur, _bdot(qm, kc, 2, 2) * SCALE, NEG)
            pc = jnp.exp(sc - lse)
            dsc = (pc * (_bdot(dom, vc, 2, 2) + c) * SCALE).astype(BF)
            dqh = _bdot(dsc, kc, 2, 1)
            dkc = dkc + jnp.where(m, _bdot(dsc, qm, 1, 1), 0.0)
            dvc = dvc + jnp.where(m, _bdot(pc.astype(BF), dom, 1, 1), 0.0)
            if has_prev:
                sp = jnp.where(prev, _bdot(qm, kp, 2, 2) * SCALE, NEG)
                pp = jnp.exp(sp - lse)
                dsp = (pp * (_bdot(dom, vp, 2, 2) + c) * SCALE).astype(BF)
                dqh = dqh + _bdot(dsp, kp, 2, 1)
                dkp = dkp + jnp.where(m, _bdot(dsp, qm, 1, 1), 0.0)
                dvp = dvp + jnp.where(m, _bdot(pp.astype(BF), dom, 1, 1), 0.0)
            dq = dq + jnp.where(m, dqh, 0.0)
        if has_prev:
            dkc = dkc + _shift_units(dkp, -d)
            dvc = dvc + _shift_units(dvp, -d)
        _store_units(dq_ref, dq, g)
        _store_units(dk_ref, dkc, g)
        _store_units(dv_ref, dvc, g)

    blk = pl.BlockSpec((S, 128), lambda b, hf: (b, g * 2 + hf))
    anyspec = pl.BlockSpec(memory_space=pl.ANY)
    return tuple(pl.pallas_call(
        body, name=name, grid=(B, 2),
        in_specs=[blk] * 6 + [anyspec] * 3, out_specs=[blk] * 3,
        out_shape=[jax.ShapeDtypeStruct(q.shape, F32)] * 3,
        input_output_aliases={6: 0, 7: 1, 8: 2},
        compiler_params=_cp(("parallel", "parallel")),
    )(q, k, v, do, cb, lse, *aliases))


def _kv_grad_sum(parts, cos, sin, *, name, tm=512):
    T = parts[0][0].shape[0]
    n_l = len(parts)

    def body(*refs):
        c_ref, s_ref = refs[0], refs[1]
        dk_ref, dv_ref = refs[2 + 2 * n_l:]
        dk = refs[2][...]
        dv = refs[3][...]
        for li in range(1, n_l):
            dk = dk + refs[2 + 2 * li][...]
            dv = dv + refs[3 + 2 * li][...]
        dk_ref[...] = _rot(dk, c_ref[...], s_ref[...], -1.0).astype(dk_ref.dtype)
        dv_ref[...] = dv.astype(dv_ref.dtype)

    full = pl.BlockSpec((tm, MAIN_W), lambda i: (i, 0))
    tab = pl.BlockSpec((tm, 128), lambda i: (i, 0))
    ops = [cos, sin] + [t for part in parts for t in part]
    return pl.pallas_call(
        body, name=name, grid=(T // tm,), in_specs=[tab, tab] + [full] * (2 * n_l), out_specs=[full, full],
        out_shape=[jax.ShapeDtypeStruct((T, MAIN_W), BF)] * 2,
        compiler_params=_cp(("parallel",)),
    )(*ops)


def _group_softmax(lse):
    l0, l1, l2 = lse[:, 0:256], lse[:, 256:512], lse[:, 512:768]
    mx = jnp.maximum(jnp.maximum(l0, l1), l2)
    e0, e1, e2 = jnp.exp(l0 - mx), jnp.exp(l1 - mx), jnp.exp(l2 - mx)
    tot = e0 + e1 + e2
    return e0 / tot, e1 / tot, e2 / tot


def _dil_combine_fwd(o, lse, y_alias, *, name, tm=512):
    T = o.shape[0]

    def body(o_ref, l_ref, _, y_ref):
        a = jnp.concatenate(_group_softmax(l_ref[...]), axis=1)
        y_ref[...] = (o_ref[...] * a).astype(y_ref.dtype)

    blk = pl.BlockSpec((tm, MAIN_W), lambda i: (i, 0))
    return pl.pallas_call(
        body, name=name, grid=(T // tm,), in_specs=[blk, blk, pl.BlockSpec(memory_space=pl.ANY)], out_specs=blk,
        out_shape=jax.ShapeDtypeStruct(y_alias.shape, y_alias.dtype), input_output_aliases={2: 0},
        compiler_params=_cp(("parallel",)),
    )(o, lse, y_alias)


def _dil_combine_bwd(dy, o, lse, *, name, tm=256):
    T = o.shape[0]
    lane_r = lax.broadcasted_iota(jnp.int32, (256, 256), 0) // HEAD_DIM
    lane_c = lax.broadcasted_iota(jnp.int32, (256, 256), 1) // HEAD_DIM
    ones_bd = (lane_r == lane_c).astype(BF)

    def body(dy_ref, o_ref, l_ref, e_ref, do_ref, c_ref):
        dyv = dy_ref[...]
        alphas = _group_softmax(l_ref[...])
        prod = dyv * o_ref[...]
        e = e_ref[...]
        tot = jnp.zeros((tm, 256), F32)
        for gi in range(3):
            x = prod[:, gi * 256:(gi + 1) * 256]
            hi = x.astype(BF)
            lo = (x - hi.astype(F32)).astype(BF)
            dalpha = jnp.dot(hi, e, preferred_element_type=F32) + jnp.dot(lo, e, preferred_element_type=F32)
            tot = tot + alphas[gi] * dalpha
        a = jnp.concatenate(alphas, axis=1)
        do_ref[...] = (dyv * a).astype(do_ref.dtype)
        c_ref[...] = jnp.concatenate([-al * tot for al in alphas], axis=1)

    blk = pl.BlockSpec((tm, MAIN_W), lambda i: (i, 0))
    return pl.pallas_call(
        body, name=name, grid=(T // tm,),
        in_specs=[blk, blk, blk, pl.BlockSpec((256, 256), lambda i: (0, 0))], out_specs=[blk, blk],
        out_shape=[jax.ShapeDtypeStruct((T, MAIN_W), F32), jax.ShapeDtypeStruct((T, MAIN_W), F32)],
        compiler_params=_cp(("parallel",)),
    )(dy, o, lse, ones_bd)


def _loss(y, target, *, name, tm=512):
    T, Dm = y.shape
    nt = T // tm

    def body(y_ref, t_ref, l_ref, d_ref, acc):
        i = pl.program_id(0)
        err = y_ref[...] - t_ref[...]
        d_ref[...] = err / Dm
        part = jnp.sum(jnp.mean(err * err, axis=-1, keepdims=True).reshape(tm // 8, 8, 1), axis=0)

        @pl.when(i == 0)
        def _():
            acc[...] = part

        @pl.when(i > 0)
        def _():
            acc[...] += part

        @pl.when(i == nt - 1)
        def _():
            l_ref[...] = 0.5 * jnp.sum(acc[...], axis=0, keepdims=True)

    row = pl.BlockSpec((tm, Dm), lambda i: (i, 0))
    return pl.pallas_call(
        body, name=name, grid=(nt,), in_specs=[row, row],
        out_specs=[pl.BlockSpec((1, 1), lambda i: (0, 0)), row],
        out_shape=[jax.ShapeDtypeStruct((1, 1), F32), jax.ShapeDtypeStruct((T, Dm), F32)],
        scratch_shapes=[pltpu.VMEM((8, 1), F32)],
        compiler_params=_cp(("arbitrary",)),
    )(y, target)


def _adamw(w, g, m, v, *, name):
    shape = w.shape
    cols = shape[-1]
    rows = w.size // cols
    tm = rows
    for cand in (512, 352, 256, 128):
        if rows > cand and rows % cand == 0 and cand * cols * 4 <= (1 << 20):
            tm = cand
            break

    def body(w_ref, g_ref, m_ref, v_ref, d_ref, mo_ref, vo_ref):
        gv = g_ref[...]
        mn = ADAM_B1 * m_ref[...] + (1.0 - ADAM_B1) * gv
        vn = ADAM_B2 * v_ref[...] + (1.0 - ADAM_B2) * (gv * gv)
        m_hat = mn / (1.0 - ADAM_B1 ** ADAM_STEP)
        v_hat = vn / (1.0 - ADAM_B2 ** ADAM_STEP)
        d_ref[...] = -ADAM_LR * (m_hat / (jnp.sqrt(v_hat) + ADAM_EPS) + ADAM_WD * w_ref[...])
        mo_ref[...] = mn
        vo_ref[...] = vn

    blk = pl.BlockSpec((tm, cols), lambda i: (i, 0))
    outs = pl.pallas_call(
        body, name=name, grid=(rows // tm,), in_specs=[blk] * 4, out_specs=[blk] * 3,
        out_shape=[jax.ShapeDtypeStruct((rows, cols), F32)] * 3,
        compiler_params=_cp(("parallel",)),
    )(*[t.reshape(rows, cols) for t in (w, g, m, v)])
    return tuple(t.reshape(shape) for t in outs)


def _adamw_layer(name, l, w, g, m, v, prev, after=None):
    L, rows, cols = w.shape
    tm = rows
    for cand in (512, 352, 256, 176, 128, 64):
        if rows % cand == 0 and cand * cols * 4 <= (1 << 21):
            tm = cand
            break
    if prev is None:
        prev = tuple(lax.empty(w.shape, F32) for _ in range(4))

    n_after = 0 if after is None else 1

    def body(w_ref, g_ref, m_ref, v_ref, *rest):
        d_ref, mo_ref, vo_ref, go_ref = rest[4 + n_after:]
        gv = g_ref[...]
        mn = ADAM_B1 * m_ref[...] + (1.0 - ADAM_B1) * gv
        vn = ADAM_B2 * v_ref[...] + (1.0 - ADAM_B2) * (gv * gv)
        m_hat = mn / (1.0 - ADAM_B1 ** ADAM_STEP)
        v_hat = vn / (1.0 - ADAM_B2 ** ADAM_STEP)
        d_ref[...] = -ADAM_LR * (m_hat / (jnp.sqrt(v_hat) + ADAM_EPS) + ADAM_WD * w_ref[...])
        mo_ref[...] = mn
        vo_ref[...] = vn
        go_ref[...] = gv

    lay = pl.BlockSpec((None, tm, cols), lambda i: (l, i, 0))
    one = pl.BlockSpec((None, tm, cols), lambda i: (0, i, 0))
    return tuple(pl.pallas_call(
        body, name=f"l{l}_adamw_{name}", grid=(rows // tm,),
        in_specs=[lay, one, lay, lay] + [pl.BlockSpec(memory_space=pl.ANY)] * (4 + n_after), out_specs=[lay] * 4,
        out_shape=[jax.ShapeDtypeStruct(w.shape, F32)] * 4,
        input_output_aliases={4 + i: i for i in range(4)},
        compiler_params=_cp(("parallel",)),
    )(w, g, m, v, *prev, *([] if after is None else [after])))


BIG = {
    'w_in': ((DEPTH, D_MODEL, D_MODEL), 'row'),
    'w_mem_kv': ((DEPTH, D_MODEL, 2 * MEM_W), 'row'),
    'w_out': ((DEPTH, D_MODEL, D_MODEL), 'row'),
    'w_kv': ((1, D_MODEL, 2 * MAIN_W), 'col'),
    'w_gate_up': ((DEPTH, D_MODEL, 2 * D_FF), 'col'),
    'w_down': ((DEPTH, D_FF, D_MODEL), 'row'),
}
BIG_NAMES = tuple(BIG)
N_CHIPS = 4
HBM_ANY = pl.BlockSpec(memory_space=pl.ANY)


def _geom(name):
    (L, R, C), kind = BIG[name]
    if kind == 'row':
        return L, R, C, kind, R // N_CHIPS, C, R // (2 * N_CHIPS)
    return L, R, C, kind, R, C // N_CHIPS, R // 2


def _shard_shape(name):
    L, R, C, kind, rs, cs, rh = _geom(name)
    return (L, rs, cs)


def _half_shape(name):
    L, R, C, kind, rs, cs, rh = _geom(name)
    return (L, rh, cs)


def _full_win(ref, name, s, h):
    L, R, C, kind, rs, cs, rh = _geom(name)
    if kind == 'row':
        rows = pl.ds(s * rs, rs) if h is None else pl.ds(s * rs + h * rh, rh)
        return ref.at[:, rows, :]
    rows = slice(None) if h is None else pl.ds(h * rh, rh)
    return ref.at[:, rows, pl.ds(s * cs, cs)]


def _shard_half(ref, name, h):
    L, R, C, kind, rs, cs, rh = _geom(name)
    return ref.at[:, pl.ds(h * rh, rh), :]


def _halves_win(ref, name, s):
    L, R, C, kind, rs, cs, rh = _geom(name)
    if kind == 'row':
        return ref.at[:, pl.ds(s * rh, rh), :]
    return ref.at[:, :, pl.ds(s * cs, cs)]


def _halves_shape(name):
    L, R, C, kind, rs, cs, rh = _geom(name)
    return (L, N_CHIPS * rh, cs) if kind == 'row' else (L, rh, C)


def _place():
    x, y, c = lax.axis_index("x"), lax.axis_index("y"), lax.axis_index("c")
    chips = [(1 - x, y), (x, 1 - y), (1 - x, 1 - y)]
    return x, y, c, chips


SMALL_ROWS = 24


SEM_SPEC = pl.BlockSpec(memory_space=pltpu.SEMAPHORE)
HBM_SPEC = pl.BlockSpec(memory_space=pltpu.HBM)
DATAFLOW = pltpu.SideEffectType.DATAFLOW_SIDE_EFFECTING


def _in_hbm(a):
    return pltpu.with_memory_space_constraint(a, pltpu.HBM)


def _remote(src, dst, send_sems, recv_sems, k, to):
    return pltpu.make_async_remote_copy(src_ref=src, dst_ref=dst, send_sem=send_sems.at[k], recv_sem=recv_sems.at[k],
                                        device_id=to, device_id_type=MESH)


def _split_start(name, bufs, n_copies, sends, after=None):
    nb = len(bufs)
    n_in = nb + (0 if after is None else 1)

    def body(*refs):
        in_refs = refs[:nb]
        send_sems, recv_sems = refs[n_in], refs[n_in + 1]
        token = refs[-1]
        for k, (src, dst, to) in enumerate(sends(in_refs)):
            _remote(src, dst, send_sems, recv_sems, k, to).start()
        token[...] = jnp.zeros_like(token)

    outs = pl.pallas_call(
        body, name=name,
        out_shape=(pltpu.SemaphoreType.DMA((n_copies,)), pltpu.SemaphoreType.DMA((n_copies,)),
                   *[pltpu.HBM(b.shape, b.dtype) for b in bufs], jax.ShapeDtypeStruct((8, 128), F32)),
        in_specs=[HBM_SPEC] * nb + [HBM_ANY] * (n_in - nb),
        out_specs=(SEM_SPEC, SEM_SPEC, *[HBM_SPEC] * nb, pl.BlockSpec(memory_space=pltpu.VMEM)),
        input_output_aliases={i: 2 + i for i in range(nb)},
        compiler_params=pltpu.CompilerParams(has_side_effects=DATAFLOW),
    )(*[_in_hbm(b) for b in bufs], *([] if after is None else [after]))
    return outs[0], outs[1], list(outs[2:2 + nb]), outs[-1]


def _split_wait(name, send_sems, recv_sems, bufs, after, sends, arrivals):
    nb = len(bufs)

    def body(*refs):
        in_refs = refs[:nb]
        s_sems, r_sems = refs[nb], refs[nb + 1]
        me = (lax.axis_index("x"), lax.axis_index("y"), lax.axis_index("c"))
        for k, (src, dst, to) in enumerate(sends(in_refs)):
            _remote(src, dst, s_sems, r_sems, k, to).wait_send()
        for k, win in enumerate(arrivals(in_refs)):
            _remote(win, win, s_sems, r_sems, k, me).wait_recv()

    outs = pl.pallas_call(
        body, name=name,
        out_shape=[pltpu.HBM(b.shape, b.dtype) for b in bufs],
        in_specs=[HBM_SPEC] * nb + [SEM_SPEC, SEM_SPEC, HBM_ANY],
        out_specs=[HBM_SPEC] * nb,
        input_output_aliases={i: i for i in range(nb)},
        compiler_params=pltpu.CompilerParams(has_side_effects=DATAFLOW),
    )(*bufs, send_sems, recv_sems, after)
    return list(outs)


MIX_W = ('w_in', 'w_mem_kv', 'w_out')
FFN_W = ('w_gate_up', 'w_down')
LAYER_W = MIX_W + FFN_W


def _place_own(tag, names, sources, small, sc):
    nw = len(names)
    has_small = small is not None

    def body(sc_ref, *refs):
        srcs = refs[:nw]
        shard_out = refs[nw + has_small:2 * nw + has_small]
        full_out = refs[2 * nw + has_small:3 * nw + has_small]
        for src, sh, fu in zip(srcs, shard_out, full_out):
            v = src[...].astype(BF)
            sh[...] = v
            fu[...] = v
        if has_small:
            refs[-1][...] = refs[nw][...]

    in_specs, shard_specs, full_specs, shard_shape, full_shape, ops = [], [], [], [], [], []
    for nm, (arr, layer) in zip(names, sources):
        L, R, C, kind, rs, cs, rh = _geom(nm)
        in_specs.append(pl.BlockSpec((1, rs, cs), lambda i, sc_ref, layer=layer: (layer, 0, 0)))
        shard_specs.append(pl.BlockSpec((1, rs, cs), lambda i, sc_ref: (0, 0, 0)))
        if kind == 'row':
            full_specs.append(pl.BlockSpec((1, rs, cs), lambda i, sc_ref: (0, sc_ref[0], 0)))
        else:
            full_specs.append(pl.BlockSpec((1, rs, cs), lambda i, sc_ref: (0, 0, sc_ref[0])))
        shard_shape.append(jax.ShapeDtypeStruct((1, rs, cs), BF))
        full_shape.append(jax.ShapeDtypeStruct((1, R, C), BF))
        ops.append(arr)
    if has_small:
        in_specs.append(pl.BlockSpec((SMALL_ROWS, 256), lambda i, sc_ref: (0, 0)))
        full_specs.append(pl.BlockSpec((None, SMALL_ROWS, 256), lambda i, sc_ref: (sc_ref[0], 0, 0)))
        full_shape.append(jax.ShapeDtypeStruct((N_CHIPS, SMALL_ROWS, 256), F32))
        ops.append(small)
    outs = pl.pallas_call(
        body, name=f"{tag}_place_own_shard",
        grid_spec=pltpu.PrefetchScalarGridSpec(num_scalar_prefetch=1, grid=(1,), in_specs=in_specs,
                                               out_specs=shard_specs + full_specs),
        out_shape=shard_shape + full_shape,
        compiler_params=_cp(("arbitrary",)),
    )(sc, *ops)
    return list(outs[:nw]), list(outs[nw:])


def _gather_start(l, names, sources, small, sc, after=None):
    nw = len(names)
    has_small = small is not None
    shards, fulls = _place_own(l, names, sources, small, sc)
    bufs = list(shards) + ([small] if has_small else []) + list(fulls)
    n_src = nw + (1 if has_small else 0)

    def sends(refs):
        x, y, c, chips = _place()
        s = 2 * x + y
        out = []
        for (px, py) in chips:
            for wi, nm in enumerate(names):
                out.append((_shard_half(refs[wi], nm, c), _full_win(refs[n_src + wi], nm, s, c), (px, py, c)))
            if has_small:
                out.append((refs[nw], refs[n_src + nw].at[s], (px, py, c)))
        return out

    def arrivals(refs):
        x, y, c, chips = _place()
        out = []
        for (px, py) in chips:
            sp = 2 * px + py
            for wi, nm in enumerate(names):
                out.append(_full_win(refs[n_src + wi], nm, sp, c))
            if has_small:
                out.append(refs[n_src + nw].at[sp])
        return out

    n_copies = 3 * n_src
    send_sems, recv_sems, bufs, token = _split_start(f"{l}_gather_ici_start", bufs, n_copies, sends, after)
    return dict(l=l, names=names, has_small=has_small, sems=(send_sems, recv_sems), bufs=bufs, sends=sends,
                arrivals=arrivals, token=token)


def _gather_forward(st, after):
    l, names = st['l'], st['names']
    nw = len(names)
    n_src = nw + (1 if st['has_small'] else 0)
    bufs = _split_wait(f"{l}_gather_ici_wait", *st['sems'], st['bufs'], after, st['sends'], st['arrivals'])
    fulls = bufs[n_src:n_src + nw]
    small_all = bufs[n_src + nw] if st['has_small'] else None

    def sends(refs):
        x, y, c, chips = _place()
        out = []
        for (px, py) in chips:
            sp = 2 * px + py
            for wi, nm in enumerate(names):
                w = _full_win(refs[wi], nm, sp, c)
                out.append((w, w, (x, y, 1 - c)))
        return out

    def arrivals(refs):
        x, y, c, chips = _place()
        out = []
        for (px, py) in chips:
            sp = 2 * px + py
            for wi, nm in enumerate(names):
                out.append(_full_win(refs[wi], nm, sp, 1 - c))
        return out

    send_sems, recv_sems, fulls, token = _split_start(f"{l}_gather_d2d_start", fulls, 3 * nw, sends)
    return dict(l=l, names=names, sems=(send_sems, recv_sems), bufs=fulls, sends=sends, arrivals=arrivals,
                small_all=small_all, token=token)


def _gather_finish(st, after):
    fulls = _split_wait(f"{st['l']}_gather_d2d_wait", *st['sems'], st['bufs'], after, st['sends'], st['arrivals'])
    return dict(zip(st['names'], fulls)), st['small_all']


def _reduce_start(tag, names, grads):
    nw = len(names)
    recv = [lax.empty((1,) + _halves_shape(nm)[1:], F32) for nm in names]
    bufs = [grads[nm] for nm in names] + recv

    def windows(refs, half_of):
        x, y, c, _ = _place()
        h = half_of(c)
        out = []
        for wi, nm in enumerate(names):
            L, R, C, kind, rs, cs, rh = _geom(nm)
            if kind == 'row':
                for sp in range(N_CHIPS):
                    out.append((_full_win(refs[wi], nm, sp, h), _halves_win(refs[nw + wi], nm, sp)))
            else:
                out.append((refs[wi].at[:, pl.ds(h * rh, rh), :], refs[nw + wi]))
        return out

    def sends(refs):
        x, y, c, _ = _place()
        return [(src, dst, (x, y, 1 - c)) for src, dst in windows(refs, lambda c: 1 - c)]

    def arrivals(refs):
        return [dst for _, dst in windows(refs, lambda c: c)]

    n_copies = sum(N_CHIPS if BIG[nm][1] == 'row' else 1 for nm in names)
    send_sems, recv_sems, bufs, token = _split_start(tag + "_halves_start", bufs, n_copies, sends)
    return dict(tag=tag, names=names, sems=(send_sems, recv_sems), bufs=bufs, sends=sends, arrivals=arrivals, token=token)


def _reduce_mid(st, after, sc):
    tag, names = st['tag'], st['names']
    nw = len(names)
    bufs = _split_wait(tag + "_halves_wait", *st['sems'], st['bufs'], after, st['sends'], st['arrivals'])
    halves, own = [], []
    for wi, nm in enumerate(names):
        hb, ow = _add_halves(nm, bufs[wi], bufs[nw + wi], sc, tag)
        halves.append(hb)
        own.append(ow)
    pieces = [lax.empty((3, 1) + _half_shape(nm)[1:], BF) for nm in names]

    def sends(refs):
        x, y, c, chips = _place()
        out = []
        for j, (px, py) in enumerate(chips):
            for wi, nm in enumerate(names):
                out.append((_halves_win(refs[wi], nm, 2 * px + py), refs[nw + wi].at[j], (px, py, c)))
        return out

    def arrivals(refs):
        return [refs[nw + wi].at[j] for j in range(3) for wi in range(nw)]

    send_sems, recv_sems, bufs, token = _split_start(tag + "_pieces_start", halves + pieces, 3 * nw, sends)
    return dict(tag=tag, names=names, sems=(send_sems, recv_sems), bufs=bufs, sends=sends, arrivals=arrivals, own=own,
                token=token)


def _reduce_late(st, after, sc):
    tag, names = st['tag'], st['names']
    nw = len(names)
    bufs = _split_wait(tag + "_pieces_wait", *st['sems'], st['bufs'], after, st['sends'], st['arrivals'])
    gsh = [_sum_pieces(nm, st['own'][wi], bufs[nw + wi], sc, tag) for wi, nm in enumerate(names)]

    def sends(refs):
        x, y, c, _ = _place()
        return [(_shard_half(refs[wi], nm, c), _shard_half(refs[wi], nm, c), (x, y, 1 - c)) for wi, nm in enumerate(names)]

    def arrivals(refs):
        x, y, c, _ = _place()
        return [_shard_half(refs[wi], nm, 1 - c) for wi, nm in enumerate(names)]

    send_sems, recv_sems, bufs, token = _split_start(tag + "_share_start", gsh, nw, sends)
    return dict(tag=tag, names=names, sems=(send_sems, recv_sems), bufs=bufs, sends=sends, arrivals=arrivals, token=token)


def _reduce_finish(st, after):
    gsh = _split_wait(st['tag'] + "_share_wait", *st['sems'], st['bufs'], after, st['sends'], st['arrivals'])
    return dict(zip(st['names'], gsh))


def _add_halves(name, g, r, sc, tag):
    _, R, C, kind, rs, cs, rh = _geom(name)
    L = g.shape[0]
    tr = rh
    nr = rh // tr

    def body(sc_ref, g_ref, r_ref, hb_ref, own_ref):
        sp = pl.program_id(2)
        tot = g_ref[...] + r_ref[...]
        hb_ref[...] = tot.astype(hb_ref.dtype)

        @pl.when(sp == sc_ref[0])
        def _():
            own_ref[...] = tot

    if kind == 'row':
        g_map = lambda l, ri, sp, sc_ref: (l, sp * 2 + sc_ref[1], 0)
        h_map = lambda l, ri, sp, sc_ref: (l, sp, 0)
    else:
        g_map = lambda l, ri, sp, sc_ref: (l, sc_ref[1] * nr + ri, sp)
        h_map = lambda l, ri, sp, sc_ref: (l, ri, sp)
    own_map = lambda l, ri, sp, sc_ref: (l, ri, 0)
    blk = (None, tr, cs)
    return pl.pallas_call(
        body, name=tag + "_add_halves_" + name,
        grid_spec=pltpu.PrefetchScalarGridSpec(
            num_scalar_prefetch=1, grid=(L, nr, N_CHIPS),
            in_specs=[pl.BlockSpec(blk, g_map), pl.BlockSpec(blk, h_map)],
            out_specs=[pl.BlockSpec(blk, h_map), pl.BlockSpec(blk, own_map)]),
        out_shape=[jax.ShapeDtypeStruct((L,) + _halves_shape(name)[1:], BF),
                   jax.ShapeDtypeStruct((L,) + _half_shape(name)[1:], F32)],
        compiler_params=_cp(("parallel", "parallel", "arbitrary")),
    )(sc, g, r)


def _sum_pieces(name, own, pieces, sc, tag):
    _, R, C, kind, rs, cs, rh = _geom(name)
    L = own.shape[0]
    tr = rh
    nr = rh // tr

    def body(sc_ref, o_ref, p_ref, out_ref):
        out_ref[...] = o_ref[...] + p_ref[0].astype(F32) + p_ref[1].astype(F32) + p_ref[2].astype(F32)

    blk = (None, tr, cs)
    return pl.pallas_call(
        body, name=tag + "_sum_pieces_" + name,
        grid_spec=pltpu.PrefetchScalarGridSpec(
            num_scalar_prefetch=1, grid=(L, nr),
            in_specs=[pl.BlockSpec(blk, lambda l, ri, sc_ref: (l, ri, 0)),
                      pl.BlockSpec((3, None, tr, cs), lambda l, ri, sc_ref: (0, l, ri, 0))],
            out_specs=pl.BlockSpec(blk, lambda l, ri, sc_ref: (l, sc_ref[1] * nr + ri, 0))),
        out_shape=jax.ShapeDtypeStruct((L,) + _shard_shape(name)[1:], F32),
        compiler_params=_cp(("parallel", "parallel")),
    )(sc, own, pieces)


def _small_gather_start(v, sc):
    rows = v.shape[0]

    def place(sc_ref, v_ref, o_ref):
        o_ref[...] = v_ref[...]

    slots = pl.pallas_call(
        place, name="small_grads_place_own",
        grid_spec=pltpu.PrefetchScalarGridSpec(
            num_scalar_prefetch=1, grid=(1,),
            in_specs=[pl.BlockSpec((rows, 128), lambda i, sc_ref: (0, 0))],
            out_specs=pl.BlockSpec((None, rows, 128), lambda i, sc_ref: (2 * sc_ref[0] + sc_ref[1], 0, 0))),
        out_shape=jax.ShapeDtypeStruct((8, rows, 128), v.dtype),
        compiler_params=_cp(("arbitrary",)),
    )(sc, v)

    def peers():
        x, y, c, _ = _place()
        flips = [(fx, fy, fc) for fx in (0, 1) for fy in (0, 1) for fc in (0, 1)][1:]
        return [((1 - x if fx else x), (1 - y if fy else y), (1 - c if fc else c)) for fx, fy, fc in flips]

    def sends(refs):
        x, y, c, _ = _place()
        return [(refs[0], refs[1].at[4 * x + 2 * y + c], p) for p in peers()]

    def arrivals(refs):
        return [refs[1].at[4 * px + 2 * py + pc] for px, py, pc in peers()]

    send_sems, recv_sems, bufs, token = _split_start("small_grads_gather_start", [v, slots], 7, sends)
    return dict(sems=(send_sems, recv_sems), bufs=bufs, sends=sends, arrivals=arrivals, token=token)


def _small_gather_finish(st, after):
    return _split_wait("small_grads_gather_wait", *st['sems'], st['bufs'], after, st['sends'], st['arrivals'])[1]


def _sum8(v8, *, name, tr=336):
    rows = v8.shape[1]
    tr = min(tr, rows)
    assert rows % tr == 0

    def body(v_ref, o_ref):
        tot = v_ref[0].astype(F32)
        for d in range(1, 8):
            tot = tot + v_ref[d].astype(F32)
        o_ref[...] = tot

    return pl.pallas_call(
        body, name=name, grid=(rows // tr,),
        in_specs=[pl.BlockSpec((8, tr, 128), lambda i: (0, i, 0))], out_specs=pl.BlockSpec((tr, 128), lambda i: (i, 0)),
        out_shape=jax.ShapeDtypeStruct((rows, 128), F32),
        compiler_params=_cp(("parallel",)),
    )(v8)


def _block_diag(w_pool_l):
    wbd = jnp.zeros((MAIN_W, MAIN_W), F32)
    for gi in range(len(POOL_WINDOWS)):
        wbd = lax.dynamic_update_slice(wbd, w_pool_l[gi], (gi * POOL_GROUP, gi * POOL_GROUP))
    return wbd.astype(BF)


def _unpack_small(small_all):
    ng = small_all[:, :16, :].reshape(N_CHIPS, DEPTH, 4, 256).transpose(1, 2, 0, 3).reshape(DEPTH, 4, D_MODEL)
    ps = small_all[:, 16:18, :POOL_GROUP].transpose(1, 0, 2).reshape(N_A, MAIN_W)
    return ng, ps


def _local_step(x, mem, positions, on_forward, on_backward, mem_norm, w_pool, kv_norm, target):
    B, S, _ = x.shape
    T = B * S
    xc = x.reshape(T, D_MODEL)
    memf = mem.reshape(B * N_MEM, D_MODEL)
    tgt = target.reshape(T, D_MODEL)
    cos, sin = _rope_tables(positions.reshape(T, 1), name="rope_tables")
    wbd = [_block_diag(w_pool[l]) for l in range(N_A)]
    nbo = D_FF // 256
    fw = []
    rk = rv = None
    kv_saved = None
    wts = []
    norm_gains = pool_scale = y2 = None

    for l in range(DEPTH):
        t = f"l{l}_"
        got = on_forward('start', l, y2)
        wts.append(dict(got[0]))
        if l == 0:
            norm_gains, pool_scale = _unpack_small(got[1])
        sv = {'x_in': xc}
        h0 = _norm_fwd(xc, norm_gains[l, 0], name=t + "norm0", out_dtype=BF, tm=1024, after=got[2])
        z, = _mm(h0, wts[l]['w_in'], b_layer=0, name=t + "mm_in", tm=1024, tn=1024)
        memn = _norm_fwd(memf, mem_norm[l], name=t + "norm_mem", out_dtype=BF, tm=256)
        kvm, = _mm(memn, wts[l]['w_mem_kv'], b_layer=0, name=t + "mm_memkv", out_dtypes=(BF,))
        if l < N_A:
            ycat, sv['p'] = _pool_fwd(z, wbd[l], pool_scale[l], B, S, name=t + "pool_fwd")
        else:
            rq = _rope_apply(z, cos, sin, name=t + "rope_q", out_dtype=F32, tm=1024)
            o = lax.empty((T, MAIN_W), F32)
            lse = lax.empty((T, MAIN_W), F32)
            for g in range(3):
                o, lse = _dil_fwd(g, rq, rk, rv, o, lse, B, S, name=t + f"dil_fwd{g}")
            ycat = _dil_combine_fwd(o, lse, lax.empty((T, D_MODEL), BF), name=t + "dil_combine", tm=1024)
            sv.update(rq=rq, o=o, lse=lse)
        ycat, sv['lse_m'] = _memattn_fwd(z, kvm, ycat, B, S, name=t + "memattn_fwd")
        tok = on_forward('mid', l, ycat)
        y1, = _mm(ycat, wts[l]['w_out'], b_layer=0, name=t + "mm_out", tm=1024, tn=1024)
        wts[l].update(on_forward('ffn', l, y1)[0])
        x1 = _norm_fwd(y1, norm_gains[l, 1], name=t + "norm1", res=xc, after=tok)
        h2 = _norm_fwd(x1, norm_gains[l, 2], name=t + "norm2", out_dtype=BF, tm=1024)
        gg, uu, aa = _mm(h2, wts[l]['w_gate_up'], b_layer=0, b_offsets=(0, nbo), out_n=D_FF, tm=4096, tn=256, name=t + "mm_gate_up",
                         epilogue=_swiglu_fwd_epilogue, out_dtypes=(BF, BF, BF))
        on_forward('post', l, gg)
        y2, = _mm(aa, wts[l]['w_down'], b_layer=0, tk=D_FF, name=t + "mm_down")
        x2 = _norm_fwd(y2, norm_gains[l, 3], name=t + "norm3", res=x1)
        sv.update(h0=h0, z=z, memn=memn, kvm=kvm, ycat=ycat, y1=y1, x1=x1, h2=h2, gg=gg, uu=uu, aa=aa, y2=y2)
        fw.append(sv)
        xc = x2
        if l == N_A - 1:
            kvn = _norm_fwd(xc, kv_norm, name="norm_kv", out_dtype=BF, tm=1024)
            kv, = _mm(kvn, wts[N_A - 1]['w_kv'], b_layer=0, name="mm_kv")
            rk, rv = _rope_apply(kv, cos, sin, name="rope_k", passthrough=True, out_dtype=F32, tm=1024)
            kv_saved = (xc, kvn)

    loss, dx = _loss(xc, tgt, name="loss", tm=1024)

    d_ng = [[None] * 4 for _ in range(DEPTH)]
    d_memnorm = [None] * DEPTH
    d_wbd = [None] * N_A
    d_pscale = [None] * N_A
    d_kvnorm = None
    kv_parts = []
    tok = None

    def as3d(gl):
        return {nm: g.reshape((1,) + g.shape) for nm, g in gl.items()}

    for l in reversed(range(DEPTH)):
        t = f"l{l}_b_"
        sv = fw[l]
        gl = {}
        dy2, d_ng[l][3] = _norm_bwd(dx, sv['y2'], norm_gains[l, 3], name=t + "norm3", out_dtype=BF, tm=1024, after=tok)
        gl['w_down'], = _mm(sv['aa'], dy2, ta=True, tm=1408, tn=512, tk=4096, name=t + "dw_down")
        dg, du = _mm(dy2, wts[l]['w_down'], tb=True, b_layer=0, tm=1024, tn=1408, name=t + "d_act",
                     extras=((sv['gg'], 'tile'), (sv['uu'], 'tile')), epilogue=_swiglu_bwd_epilogue, out_dtypes=(BF, BF))
        gl['w_gate_up'], = _mm(sv['h2'], (dg, du), ta=True, tn=1408, tk=1024, name=t + "dw_gate_up")
        dh2, = _mm((dg, du), wts[l]['w_gate_up'], tb=True, b_layer=0, tn=1024, tk=1408, name=t + "d_h2", out_dtypes=(BF,))
        dx1, d_ng[l][2] = _norm_bwd(dh2, sv['x1'], norm_gains[l, 2], name=t + "norm2", add=dx, tm=1024)
        tok = on_backward('ffn', l, dx1, as3d(gl))
        dy1, d_ng[l][1] = _norm_bwd(dx1, sv['y1'], norm_gains[l, 1], name=t + "norm1", out_dtype=BF, tm=1024, after=tok)
        gl['w_out'], = _mm(sv['ycat'], dy1, ta=True, name=t + "dw_out", tk=4096)
        dycat, = _mm(dy1, wts[l]['w_out'], tb=True, b_layer=0, name=t + "d_ycat", tm=1024, tn=1024)
        dz = lax.empty((T, D_MODEL), BF)
        dz, dkm, dvm = _memattn_bwd(dycat, sv['z'], sv['kvm'], sv['lse_m'], dz, B, S, name=t + "memattn")
        if l < N_A:
            dz, d_wbd[l], d_pscale[l] = _pool_bwd(dycat, sv['p'], wbd[l], pool_scale[l], dz, B, S, name=t + "pool")
        else:
            do, cb = _dil_combine_bwd(dycat, sv['o'], sv['lse'], name=t + "dil_combine", tm=512)
            acc = tuple(lax.empty((T, MAIN_W), F32) for _ in range(3))
            for g in range(3):
                acc = _dil_bwd(g, sv['rq'], rk, rv, do, cb, sv['lse'], acc, B, S, name=t + f"dil{g}")
            dz = _rope_apply(acc[0], cos, sin, name=t + "rope_q", sign=-1.0, alias=dz, tm=1024)
            kv_parts.append(acc[1:])
        tok = on_backward('mix', l, dz, as3d(gl))
        gl['w_in'], = _mm(sv['h0'], dz, ta=True, name=t + "dw_in", tk=4096)
        dh0, = _mm(dz, wts[l]['w_in'], tb=True, b_layer=0, name=t + "d_h0", out_dtypes=(BF,), tm=1024, tn=1024)
        dx, d_ng[l][0] = _norm_bwd(dh0, sv['x_in'], norm_gains[l, 0], name=t + "norm0", add=dx1, tm=1024, after=tok)
        gl['w_mem_kv'], = _mm(sv['memn'], (dkm, dvm), ta=True, tn=256, name=t + "dw_memkv")
        dmemn, = _mm((dkm, dvm), wts[l]['w_mem_kv'], tb=True, b_layer=0, tk=256, name=t + "d_memn", out_dtypes=(BF,))
        _, d_memnorm[l] = _norm_bwd(dmemn, memf, mem_norm[l], name=t + "norm_mem", out_dtype=BF, tm=256)
        if l == N_A:
            dk, dv = _kv_grad_sum(kv_parts, cos, sin, name="kv_grad", tm=1024)
            x_kv, kvn = kv_saved
            gl['w_kv'], = _mm(kvn, (dk, dv), ta=True, tn=768, tk=2048, name="dw_kv")
            dkvn, = _mm((dk, dv), wts[N_A - 1]['w_kv'], tb=True, b_layer=0, tn=1024, tk=768, name="d_kvn", out_dtypes=(BF,))
            dx, d_kvnorm = _norm_bwd(dkvn, x_kv, kv_norm, name="norm_kv_b", add=dx, tm=1024)
        tok = on_backward('end', l, dx, as3d(gl))

    small = {
        'norm_gains': jnp.stack([jnp.concatenate(d_ng[l], axis=0) for l in range(DEPTH)]),
        'mem_norm': jnp.concatenate(d_memnorm, axis=0),
        'kv_norm': d_kvnorm.reshape(D_MODEL),
        'pool_scale': jnp.concatenate(d_pscale, axis=0),
        'w_pool': jnp.stack([jnp.stack([d_wbd[l][gi * POOL_GROUP:(gi + 1) * POOL_GROUP, gi * POOL_GROUP:(gi + 1) * POOL_GROUP]
                                        for gi in range(len(POOL_WINDOWS))]) for l in range(N_A)]),
    }
    return loss, dx, small


SMALL_ORDER = ('norm_gains', 'mem_norm', 'kv_norm', 'pool_scale', 'w_pool')
SMALL_VEC_ROWS = 2560


def kernel(x, mem, positions, norm_gains, mem_norm, w_in, w_mem_kv, w_out, w_pool, pool_scale, kv_norm, w_kv, w_gate_up, w_down, loss_target, m_norm_gains, m_mem_norm, m_w_in, m_w_mem_kv, m_w_out, m_w_pool, m_pool_scale, m_kv_norm, m_w_kv, m_w_gate_up, m_w_down, v_norm_gains, v_mem_norm, v_w_in, v_w_mem_kv, v_w_out, v_w_pool, v_pool_scale, v_kv_norm, v_w_kv, v_w_gate_up, v_w_down):
    xi, yi, ci = lax.axis_index("x"), lax.axis_index("y"), lax.axis_index("c")
    s = 2 * xi + yi
    sc = jnp.stack([s, ci]).astype(jnp.int32)
    weights = dict(norm_gains=norm_gains, mem_norm=mem_norm, w_in=w_in, w_mem_kv=w_mem_kv, w_out=w_out, w_pool=w_pool,
                   pool_scale=pool_scale, kv_norm=kv_norm, w_kv=w_kv, w_gate_up=w_gate_up, w_down=w_down)
    moms = dict(norm_gains=m_norm_gains, mem_norm=m_mem_norm, w_in=m_w_in, w_mem_kv=m_w_mem_kv, w_out=m_w_out,
                w_pool=m_w_pool, pool_scale=m_pool_scale, kv_norm=m_kv_norm, w_kv=m_w_kv, w_gate_up=m_w_gate_up,
                w_down=m_w_down)
    vels = dict(norm_gains=v_norm_gains, mem_norm=v_mem_norm, w_in=v_w_in, w_mem_kv=v_w_mem_kv, w_out=v_w_out,
                w_pool=v_w_pool, pool_scale=v_pool_scale, kv_norm=v_kv_norm, w_kv=v_w_kv, w_gate_up=v_w_gate_up,
                w_down=v_w_down)

    small_w = jnp.zeros((SMALL_ROWS, 256), F32)
    small_w = lax.dynamic_update_slice(small_w, norm_gains.reshape(16, 256), (0, 0))
    small_w = lax.dynamic_update_slice(small_w, pool_scale, (16, 0))

    def shard_of(nm, l):
        return (w_kv.reshape(_shard_shape('w_kv')), 0) if nm == 'w_kv' else (weights[nm], l)

    groups = {'l0a': (0, MIX_W), 'l0b': (0, FFN_W)}
    groups.update({f"l{l}": (l, LAYER_W + (('w_kv',) if l == N_A - 1 else ())) for l in range(1, DEPTH)})
    on_ici, on_d2d, gathered = {}, {}, {}

    def start_group(tag, after):
        l, names = groups[tag]
        on_ici[tag] = _gather_start(tag, names, [shard_of(nm, l) for nm in names], small_w if tag == 'l0a' else None, sc,
                                    after)
        return [on_ici[tag]['token']]

    def on_forward(where, l, after):
        if where == 'start':
            if l == 0:
                start_group('l0a', None)
                st = on_ici.pop('l0a')
                fwd = _gather_forward(st, st['token'])
                w, small_all = _gather_finish(fwd, fwd['token'])
                return w, small_all, start_group('l0b', w['w_in'])
            if f"l{l}" not in on_d2d:
                on_d2d[f"l{l}"] = _gather_forward(on_ici.pop(f"l{l}"), after)
            gathered[l] = _gather_finish(on_d2d.pop(f"l{l}"), after)[0]
            tok = start_group(f"l{l + 1}", gathered[l]['w_in']) if l + 1 < DEPTH else None
            return {nm: w for nm, w in gathered[l].items() if nm not in FFN_W}, None, tok
        if where == 'mid' and l == 0:
            on_d2d['l0b'] = _gather_forward(on_ici.pop('l0b'), after)
            return start_group('l1', on_d2d['l0b']['token'])
        if where == 'ffn':
            if l == 0:
                return (_gather_finish(on_d2d.pop('l0b'), after)[0],)
            return ({nm: gathered[l][nm] for nm in FFN_W},)
        if where == 'post' and 0 < l < DEPTH - 1:
            on_d2d[f"l{l + 1}"] = _gather_forward(on_ici.pop(f"l{l + 1}"), after)
        return None

    hook_of = {'ffn': 0, 'mix': 1, 'end': 2}
    active, reduced = [], {l: {} for l in range(DEPTH)}
    advance = {'mid': lambda st, after: _reduce_mid(st, after, sc), 'late': lambda st, after: _reduce_late(st, after, sc)}

    def run_hook(idx, after):
        toks = []
        for grp in list(active):
            while grp['plan'] and grp['plan'][0][1] <= idx:
                step = grp['plan'].pop(0)[0]
                if step == 'finish':
                    reduced[grp['layer']].update(_reduce_finish(grp['st'], after))
                    active.remove(grp)
                else:
                    grp['st'] = advance[step](grp['st'], after)
                    toks.append(grp['st']['token'])
        return toks

    def on_backward(where, l, after, grads):
        idx = 3 * (DEPTH - 1 - l) + hook_of[where]
        toks = run_hook(idx, after)
        if where == 'end' or (where == 'ffn' and l == 0):
            names = FFN_W if where == 'ffn' else tuple(nm for nm in grads if l > 0 or nm not in FFN_W)
            st = _reduce_start(f"l{l}_{where}_grads", names, {nm: grads[nm] for nm in names})
            plan = [('mid', idx + 1), ('late', idx + 3), ('finish', idx + 4)] if where == 'ffn' else \
                   [('mid', idx + 1), ('late', idx + 2), ('finish', idx + 3)]
            active.append(dict(layer=l, st=st, plan=plan))
            toks.append(st['token'])
        return toks

    loss, gx, gsmall = _local_step(x, mem, positions, on_forward, on_backward, mem_norm, w_pool, kv_norm, loss_target)
    loss = lax.psum(loss[0, 0], ("x", "y", "c"))

    vec = jnp.concatenate([gsmall[nm].reshape(-1) for nm in SMALL_ORDER])
    vec = jnp.pad(vec, (0, SMALL_VEC_ROWS * 128 - vec.shape[0])).reshape(SMALL_VEC_ROWS, 128)
    vec = vec + sum(grp['st']['token'][0, 0] for grp in active)
    small_st = _small_gather_start(vec.astype(BF), sc)
    outs = {nm: None for nm in LAYER_W}

    def adamw_layers(layers, names, after):
        for l in layers:
            for nm in names:
                outs[nm] = _adamw_layer(nm, l, weights[nm], reduced[l][nm], moms[nm], vels[nm], outs[nm], after)
                after = outs[nm][0]
        return after

    def zero_of(toks, st):
        return sum(toks) if toks else st['token']

    last = 3 * DEPTH
    toks = run_hook(last, small_st['token'])
    done = adamw_layers(range(DEPTH - 1, 0, -1), LAYER_W, zero_of(toks, small_st))
    toks = run_hook(last + 1, done)
    done = adamw_layers([0], FFN_W, zero_of(toks, small_st))
    tot = _sum8(_small_gather_finish(small_st, done), name="sum_small_grads", tr=512)
    run_hook(last + 2, tot)
    assert not active
    adamw_layers([0], MIX_W, None)
    tot = tot.reshape(-1)
    grads, off = {}, 0
    for nm in SMALL_ORDER:
        shape = (DEPTH, 4, D_MODEL) if nm == 'norm_gains' else (N_A, MAIN_W) if nm == 'pool_scale' else weights[nm].shape
        n = 1
        for dim in shape:
            n *= dim
        grads[nm] = tot[off:off + n].reshape(shape)
        off += n
    grads['norm_gains'] = lax.dynamic_slice(grads['norm_gains'], (0, 0, s * 256), (DEPTH, 4, 256))
    grads['pool_scale'] = lax.dynamic_slice(grads['pool_scale'], (0, s * POOL_GROUP), (N_A, POOL_GROUP))
    grads['w_kv'] = reduced[N_A]['w_kv'].reshape(w_kv.shape)

    order = ('norm_gains', 'mem_norm', 'w_in', 'w_mem_kv', 'w_out', 'w_pool', 'pool_scale', 'kv_norm', 'w_kv',
             'w_gate_up', 'w_down')
    deltas, new_m, new_v = {}, {}, {}
    for nm in order:
        if nm in LAYER_W:
            deltas[nm], new_m[nm], new_v[nm], grads[nm] = outs[nm]
        else:
            deltas[nm], new_m[nm], new_v[nm] = _adamw(weights[nm], grads[nm], moms[nm], vels[nm], name="adamw_" + nm)
    return (loss, gx.reshape(x.shape), *[grads[nm] for nm in order], *[deltas[nm] for nm in order],
            *[new_m[nm] for nm in order], *[new_v[nm] for nm in order])
```

```python
import jax
import jax.numpy as jnp
from jax import lax
from jax.experimental import pallas as pl
from jax.experimental.pallas import tpu as pltpu

F32 = jnp.float32
BF = jnp.bfloat16

D_MODEL = 1024
DEPTH = 4
N_A = 2
HEAD_DIM = 64
MEM_W = 256
MAIN_W = 768
D_FF = 2816
N_MEM = 256
POOL_WINDOWS = (2, 4, 8, 16)
POOL_GROUP = 192
DIL = (1, 4, 16)
STEPS = 128
ROPE_THETA = 10000.0
EPS = 1e-6
SCALE = HEAD_DIM ** -0.5
NEG = -1e30

ADAM_LR = 0.001
ADAM_B1 = 0.9
ADAM_B2 = 0.999
ADAM_EPS = 1e-08
ADAM_WD = 0.01
ADAM_STEP = 10

VMEM_LIMIT = 48 * 1024 * 1024
MESH = pl.DeviceIdType.MESH


def _cp(sem):
    return pltpu.CompilerParams(dimension_semantics=sem, vmem_limit_bytes=VMEM_LIMIT)


def _mm(a, b, *, name, ta=False, tb=False, tm=1024, tn=512, tk=1024, b_layer=None, b_offsets=(0,),
        extras=(), epilogue=None, out_dtypes=(F32,), out_n=None):
    a_pair = isinstance(a, (tuple, list))
    b_pair = isinstance(b, (tuple, list))
    a0 = a[0] if a_pair else a
    b0 = b[0] if b_pair else b
    a_rows, a_cols = a0.shape
    if a_pair:
        a_cols *= 2
    b_rows, b_cols = b0.shape[-2:]
    if b_pair:
        b_cols *= 2
    M, K = (a_cols, a_rows) if ta else (a_rows, a_cols)
    N = b_rows if tb else b_cols
    if out_n is not None:
        N = out_n
    tm, tn, tk = min(tm, M), min(tn, N), min(tk, K)
    assert M % tm == 0 and N % tn == 0 and K % tk == 0, (name, M, N, K, tm, tn, tk)
    nk = K // tk
    n_acc = len(b_offsets)

    if a_pair:
        a_half = (a0.shape[1] // (tm if ta else tk))
    if b_pair:
        b_half = (b0.shape[1] // (tk if tb else tn))

    def a_map(sel):
        def f(i, j, k):
            r, c = (k, i) if ta else (i, k)
            if a_pair:
                c = jnp.clip(c - sel * a_half, 0, a_half - 1)
            return (r, c)
        return f

    def b_map(sel, off):
        def f(i, j, k):
            r, c = (j + off, k) if tb else (k, j + off)
            if b_pair:
                c = jnp.clip(c - sel * b_half, 0, b_half - 1)
            if b_layer is not None:
                return (b_layer, r, c)
            return (r, c)
        return f

    a_blk = (tk, tm) if ta else (tm, tk)
    b_blk = (tn, tk) if tb else (tk, tn)
    if b_layer is not None:
        b_blk = (None,) + b_blk
    in_specs, operands = [], []
    for sel in range(2 if a_pair else 1):
        in_specs.append(pl.BlockSpec(a_blk, a_map(sel)))
        operands.append(a[sel] if a_pair else a)
    n_a = len(operands)
    for off in b_offsets:
        for sel in range(2 if b_pair else 1):
            in_specs.append(pl.BlockSpec(b_blk, b_map(sel, off)))
            operands.append(b[sel] if b_pair else b)
    n_b = len(operands) - n_a
    for arr, kind in extras:
        if kind == 'tile':
            in_specs.append(pl.BlockSpec((tm, tn), lambda i, j, k: (i, j)))
        elif kind == 'row':
            in_specs.append(pl.BlockSpec((tm, 1), lambda i, j, k: (i, 0)))
        else:
            in_specs.append(pl.BlockSpec((1, tn), lambda i, j, k: (0, j)))
        operands.append(arr)
    n_e = len(extras)
    n_o = len(out_dtypes)
    dims = (((0,) if ta else (1,), (1,) if tb else (0,)), ((), ()))
    in_place = nk > 1 and epilogue is None and n_acc == 1 and tuple(out_dtypes) == (F32,)

    def body(*refs):
        a_refs = refs[:n_a]
        b_refs = refs[n_a:n_a + n_b]
        e_refs = refs[n_a + n_b:n_a + n_b + n_e]
        n_in = n_a + n_b + n_e
        o_refs = refs[n_in:n_in + n_o]
        acc_refs = refs[n_in + n_o:]
        i, j, k = pl.program_id(0), pl.program_id(1), pl.program_id(2)
        if a_pair:
            cidx = i if ta else k
            av = jnp.where(cidx < a_half, a_refs[0][...], a_refs[1][...])
        else:
            av = a_refs[0][...]
        av = av.astype(BF)
        prods = []
        for q in range(n_acc):
            if b_pair:
                cidx = (k if tb else j) + b_offsets[q]
                bv = jnp.where(cidx < b_half, b_refs[2 * q][...], b_refs[2 * q + 1][...])
            else:
                bv = b_refs[q][...]
            prods.append(lax.dot_general(av, bv.astype(BF), dims, preferred_element_type=F32))

        def finish(accs):
            outs = epilogue(accs, *[r[...] for r in e_refs]) if epilogue is not None else accs
            for o_ref, o in zip(o_refs, outs):
                o_ref[...] = o.astype(o_ref.dtype)

        if nk == 1:
            finish(prods)
        elif in_place:
            @pl.when(k == 0)
            def _():
                o_refs[0][...] = prods[0]

            @pl.when(k > 0)
            def _():
                o_refs[0][...] += prods[0]
        else:
            @pl.when(k == 0)
            def _():
                for r, p in zip(acc_refs, prods):
                    r[...] = p

            @pl.when(k > 0)
            def _():
                for r, p in zip(acc_refs, prods):
                    r[...] += p

            @pl.when(k == nk - 1)
            def _():
                finish([r[...] for r in acc_refs])

    return pl.pallas_call(
        body, name=name,
        grid=(M // tm, N // tn, nk),
        in_specs=in_specs,
        out_specs=[pl.BlockSpec((tm, tn), lambda i, j, k: (i, j)) for _ in range(n_o)],
        out_shape=[jax.ShapeDtypeStruct((M, N), dt) for dt in out_dtypes],
        scratch_shapes=[pltpu.VMEM((tm, tn), F32) for _ in range(n_acc if nk > 1 and not in_place else 0)],
        compiler_params=_cp(("parallel", "parallel", "arbitrary")),
    )(*operands)


def _norm_fwd(x, g, *, name, res=None, out_dtype=F32, tm=512, after=None):
    T, Dm = x.shape
    has_res = res is not None
    after = list(after or [])

    def body(*refs):
        refs = refs[:len(refs) - 1 - len(after)] + refs[len(refs) - 1:]
        if has_res:
            x_ref, g_ref, r_ref, y_ref = refs
        else:
            x_ref, g_ref, y_ref = refs
        xv = x_ref[...]
        rstd = lax.rsqrt(jnp.mean(xv * xv, axis=-1, keepdims=True) + EPS)
        y = xv * rstd * g_ref[...]
        if has_res:
            y = r_ref[...] + y
        y_ref[...] = y.astype(y_ref.dtype)

    row = pl.BlockSpec((tm, Dm), lambda i: (i, 0))
    in_specs = [row, pl.BlockSpec((1, Dm), lambda i: (0, 0))] + ([row] if has_res else [])
    in_specs += [pl.BlockSpec(memory_space=pl.ANY)] * len(after)
    ops = [x, g.reshape(1, Dm)] + ([res] if has_res else []) + after
    return pl.pallas_call(
        body, name=name, grid=(T // tm,), in_specs=in_specs,
        out_specs=row,
        out_shape=jax.ShapeDtypeStruct((T, Dm), out_dtype),
        compiler_params=_cp(("parallel",)),
    )(*ops)


def _norm_bwd(dout, x, g, *, name, add=None, out_dtype=F32, tm=512, after=None):
    T, Dm = x.shape
    has_add = add is not None
    nt = T // tm
    after = list(after or [])

    def body(*refs):
        refs = refs[:len(refs) - 3 - len(after)] + refs[len(refs) - 3:]
        if has_add:
            do_ref, x_ref, g_ref, a_ref, dx_ref, dg_ref, acc = refs
        else:
            do_ref, x_ref, g_ref, dx_ref, dg_ref, acc = refs
        i = pl.program_id(0)
        do = do_ref[...].astype(F32)
        xv = x_ref[...]
        rstd = lax.rsqrt(jnp.mean(xv * xv, axis=-1, keepdims=True) + EPS)
        xh = xv * rstd
        gd = do * g_ref[...]
        dx = rstd * (gd - xh * jnp.mean(gd * xh, axis=-1, keepdims=True))
        if has_add:
            dx = dx + a_ref[...].astype(F32)
        dx_ref[...] = dx.astype(dx_ref.dtype)
        part = jnp.sum((do * xh).reshape(tm // 8, 8, Dm), axis=0)

        @pl.when(i == 0)
        def _():
            acc[...] = part

        @pl.when(i > 0)
        def _():
            acc[...] += part

        @pl.when(i == nt - 1)
        def _():
            dg_ref[...] = jnp.sum(acc[...], axis=0, keepdims=True)

    row = pl.BlockSpec((tm, Dm), lambda i: (i, 0))
    in_specs = [row, row, pl.BlockSpec((1, Dm), lambda i: (0, 0))]
    ops = [dout, x, g.reshape(1, Dm)]
    if has_add:
        in_specs.append(row)
        ops.append(add)
    in_specs += [pl.BlockSpec(memory_space=pl.ANY)] * len(after)
    ops += after
    return pl.pallas_call(
        body, name=name, grid=(nt,), in_specs=in_specs,
        out_specs=[row, pl.BlockSpec((1, Dm), lambda i: (0, 0))],
        out_shape=[jax.ShapeDtypeStruct((T, Dm), out_dtype), jax.ShapeDtypeStruct((1, Dm), F32)],
        scratch_shapes=[pltpu.VMEM((8, Dm), F32)],
        compiler_params=_cp(("arbitrary",)),
    )(*ops)


def _swiglu_fwd_epilogue(accs):
    g, u = accs
    return g, u, g * jax.nn.sigmoid(g) * u


def _swiglu_bwd_epilogue(accs, g, u):
    da = accs[0]
    g = g.astype(F32)
    u = u.astype(F32)
    sig = jax.nn.sigmoid(g)
    return da * u * (sig * (1.0 + g * (1.0 - sig))), da * (g * sig)


def _rope_tables(pos, *, name, tm=1024):
    T = pos.shape[0]
    half = HEAD_DIM // 2
    freqs = ROPE_THETA ** (-jnp.arange(half, dtype=F32) / half)
    freqs = jnp.tile(freqs, 4).reshape(1, 128)

    def body(p_ref, f_ref, c_ref, s_ref):
        ang = p_ref[...].astype(F32) * f_ref[...]
        lane = lax.broadcasted_iota(jnp.int32, ang.shape, 1)
        c_ref[...] = jnp.cos(ang)
        s_ref[...] = jnp.where(lane % HEAD_DIM < half, -1.0, 1.0) * jnp.sin(ang)

    tab = pl.BlockSpec((tm, 128), lambda i: (i, 0))
    return pl.pallas_call(
        body, name=name, grid=(T // tm,),
        in_specs=[pl.BlockSpec((tm, 1), lambda i: (i, 0)), pl.BlockSpec((1, 128), lambda i: (0, 0))],
        out_specs=[tab, tab],
        out_shape=[jax.ShapeDtypeStruct((T, 128), F32)] * 2,
        compiler_params=_cp(("parallel",)),
    )(pos, freqs)


def _rot(x, cos, sin, sign):
    W = x.shape[1]
    half = HEAD_DIM // 2
    reps = W // 128
    c = jnp.concatenate([cos] * reps, axis=1) if reps > 1 else cos
    s = jnp.concatenate([sin] * reps, axis=1) if reps > 1 else sin
    lane = lax.broadcasted_iota(jnp.int32, x.shape, 1)
    swapped = jnp.where(lane % HEAD_DIM < half, pltpu.roll(x, W - half, axis=1), pltpu.roll(x, half, axis=1))
    return x * c + (sign * s) * swapped


def _rope_apply(x, cos, sin, *, name, sign=1.0, width=MAIN_W, passthrough=False, out_dtype=BF, alias=None,
                out_cols=None, tm=512):
    T = x.shape[0]

    def body(*refs):
        if passthrough:
            x_ref, v_ref, c_ref, s_ref, o_ref, ov_ref = refs
            ov_ref[...] = v_ref[...].astype(ov_ref.dtype)
        elif alias is not None:
            x_ref, c_ref, s_ref, _, o_ref = refs
        else:
            x_ref, c_ref, s_ref, o_ref = refs
        o_ref[...] = _rot(x_ref[...].astype(F32), c_ref[...], s_ref[...], sign).astype(o_ref.dtype)

    blk0 = pl.BlockSpec((tm, width), lambda i: (i, 0))
    blk1 = pl.BlockSpec((tm, width), lambda i: (i, 1))
    tab = pl.BlockSpec((tm, 128), lambda i: (i, 0))
    if passthrough:
        return pl.pallas_call(
            body, name=name, grid=(T // tm,), in_specs=[blk0, blk1, tab, tab], out_specs=[blk0, blk0],
            out_shape=[jax.ShapeDtypeStruct((T, width), out_dtype)] * 2,
            compiler_params=_cp(("parallel",)),
        )(x, x, cos, sin)
    if alias is not None:
        return pl.pallas_call(
            body, name=name, grid=(T // tm,),
            in_specs=[blk0, tab, tab, pl.BlockSpec(memory_space=pl.ANY)], out_specs=blk0,
            out_shape=jax.ShapeDtypeStruct(alias.shape, alias.dtype),
            input_output_aliases={3: 0},
            compiler_params=_cp(("parallel",)),
        )(x, cos, sin, alias)
    return pl.pallas_call(
        body, name=name, grid=(T // tm,), in_specs=[blk0, tab, tab], out_specs=blk0,
        out_shape=jax.ShapeDtypeStruct((T, width), out_dtype),
        compiler_params=_cp(("parallel",)),
    )(x, cos, sin)


POOL_T = 1024
POOL_HALO = 16


def _pool_lane_window(shape):
    lane = lax.broadcasted_iota(jnp.int32, shape, 1)
    w = jnp.full(shape, POOL_WINDOWS[0], jnp.int32)
    for gi in range(1, len(POOL_WINDOWS)):
        w = jnp.where(lane >= gi * POOL_GROUP, POOL_WINDOWS[gi], w)
    return w


POOL_PAD = 32
POOL_R = POOL_T + POOL_PAD


def _pool_window_sums(buf, tmp_a, tmp_b, win, back):
    src, dst, acc = buf, tmp_a, None
    for j, (w, sh) in enumerate(zip(POOL_WINDOWS, (1, 2, 4, 8)), start=1):
        n = POOL_R - 8 * j
        if back:
            dst[pl.ds(8 * j, n), :] = src[pl.ds(8 * j, n), :] + src[pl.ds(8 * j - sh, n), :]
            cur = dst[pl.ds(POOL_PAD, POOL_T), :]
        else:
            dst[pl.ds(0, n), :] = src[pl.ds(0, n), :] + src[pl.ds(sh, n), :]
            cur = dst[pl.ds(0, POOL_T), :]
        acc = cur if acc is None else jnp.where(win >= w, cur, acc)
        src, dst = dst, (tmp_b if dst is tmp_a else tmp_a)
    return acc


def _pool_fwd(z, wbd, scale, B, S, *, name):
    T = z.shape[0]
    nt = S // POOL_T
    hb = POOL_T // POOL_PAD

    def body(z_ref, h_ref, w_ref, sc_ref, y_ref, p_ref, ext, tmp_a, tmp_b):
        i = pl.program_id(1)
        u = z_ref[...]
        ext[pl.ds(POOL_PAD, POOL_T), :] = u
        ext[pl.ds(0, POOL_PAD), :] = jnp.where(i > 0, h_ref[...], 0.0)
        win = _pool_lane_window((POOL_T, MAIN_W))
        acc = _pool_window_sums(ext, tmp_a, tmp_b, win, True)
        t = i * POOL_T + lax.broadcasted_iota(jnp.int32, (POOL_T, MAIN_W), 0)
        cnt = jnp.minimum(t + 1, win).astype(F32)
        p = (acc / cnt - u).astype(BF)
        p_ref[...] = p
        y = jnp.dot(p, w_ref[...], preferred_element_type=F32) * sc_ref[...]
        y_ref[...] = y.astype(y_ref.dtype)

    return pl.pallas_call(
        body, name=name, grid=(B, nt),
        in_specs=[pl.BlockSpec((POOL_T, MAIN_W), lambda b, i: (b * nt + i, 0)),
                  pl.BlockSpec((POOL_PAD, MAIN_W), lambda b, i: (jnp.maximum((b * nt + i) * hb - 1, 0), 0)),
                  pl.BlockSpec((MAIN_W, MAIN_W), lambda b, i: (0, 0)),
                  pl.BlockSpec((1, MAIN_W), lambda b, i: (0, 0))],
        out_specs=[pl.BlockSpec((POOL_T, MAIN_W), lambda b, i: (b * nt + i, 0)),
                   pl.BlockSpec((POOL_T, MAIN_W), lambda b, i: (b * nt + i, 0))],
        out_shape=[jax.ShapeDtypeStruct((T, D_MODEL), BF), jax.ShapeDtypeStruct((T, MAIN_W), BF)],
        scratch_shapes=[pltpu.VMEM((POOL_R, MAIN_W), F32)] * 3,
        compiler_params=_cp(("parallel", "parallel")),
    )(z, z, wbd, scale.reshape(1, MAIN_W))


def _pool_bwd(dy, p, wbd, scale, dz_alias, B, S, *, name):
    T = dy.shape[0]
    nt = S // POOL_T
    hb = POOL_T // POOL_HALO
    last_halo = T // POOL_HALO - 1

    def body(dy_ref, dyn_ref, p_ref, pn_ref, w_ref, sc_ref, _, dz_ref, dw_ref, ds_ref, ext, tmp_a, tmp_b, dw_acc, ds_acc):
        b, i = pl.program_id(0), pl.program_id(1)
        first = jnp.logical_and(b == 0, i == 0)
        dyv = dy_ref[...]
        pv = p_ref[...]
        sc = sc_ref[...]
        w = w_ref[...]
        pw = jnp.dot(pv, w, preferred_element_type=F32)
        ds_part = jnp.sum((dyv * pw).reshape(POOL_T // 8, 8, MAIN_W), axis=0)
        dpw = (dyv * sc).astype(BF)
        dw_part = lax.dot_general(pv, dpw, (((0,), (0,)), ((), ())), preferred_element_type=F32)

        @pl.when(first)
        def _():
            dw_acc[...] = dw_part
            ds_acc[...] = ds_part

        @pl.when(jnp.logical_not(first))
        def _():
            dw_acc[...] += dw_part
            ds_acc[...] += ds_part

        @pl.when(jnp.logical_and(b == pl.num_programs(0) - 1, i == nt - 1))
        def _():
            dw_ref[...] = dw_acc[...]
            ds_ref[...] = jnp.sum(ds_acc[...], axis=0, keepdims=True)

        dp = lax.dot_general(dpw, w, (((1,), (1,)), ((), ())), preferred_element_type=F32)
        dpn = lax.dot_general((dyn_ref[...] * sc).astype(BF), w, (((1,), (1,)), ((), ())), preferred_element_type=F32)
        win = _pool_lane_window((POOL_T, MAIN_W))
        win_n = _pool_lane_window((POOL_HALO, MAIN_W))
        t = i * POOL_T + lax.broadcasted_iota(jnp.int32, (POOL_T, MAIN_W), 0)
        tn = (i + 1) * POOL_T + lax.broadcasted_iota(jnp.int32, (POOL_HALO, MAIN_W), 0)
        ext[pl.ds(0, POOL_T), :] = dp / jnp.minimum(t + 1, win).astype(F32)
        ext[pl.ds(POOL_T, POOL_HALO), :] = jnp.where(i < nt - 1, dpn / jnp.minimum(tn + 1, win_n).astype(F32), 0.0)
        ext[pl.ds(POOL_T + POOL_HALO, POOL_PAD - POOL_HALO), :] = jnp.zeros((POOL_PAD - POOL_HALO, MAIN_W), F32)
        acc = _pool_window_sums(ext, tmp_a, tmp_b, win, False) - dp
        dz_ref[...] = acc.astype(dz_ref.dtype)

    cur = lambda b, i: (b * nt + i, 0)
    nxt = lambda b, i: (jnp.minimum((b * nt + i + 1) * hb, last_halo), 0)
    return pl.pallas_call(
        body, name=name, grid=(B, nt),
        in_specs=[pl.BlockSpec((POOL_T, MAIN_W), cur), pl.BlockSpec((POOL_HALO, MAIN_W), nxt),
                  pl.BlockSpec((POOL_T, MAIN_W), cur), pl.BlockSpec((POOL_HALO, MAIN_W), nxt),
                  pl.BlockSpec((MAIN_W, MAIN_W), lambda b, i: (0, 0)),
                  pl.BlockSpec((1, MAIN_W), lambda b, i: (0, 0)),
                  pl.BlockSpec(memory_space=pl.ANY)],
        out_specs=[pl.BlockSpec((POOL_T, MAIN_W), cur),
                   pl.BlockSpec((MAIN_W, MAIN_W), lambda b, i: (0, 0)),
                   pl.BlockSpec((1, MAIN_W), lambda b, i: (0, 0))],
        out_shape=[jax.ShapeDtypeStruct(dz_alias.shape, dz_alias.dtype),
                   jax.ShapeDtypeStruct((MAIN_W, MAIN_W), F32), jax.ShapeDtypeStruct((1, MAIN_W), F32)],
        scratch_shapes=[pltpu.VMEM((POOL_R, MAIN_W), F32)] * 3 + [pltpu.VMEM((MAIN_W, MAIN_W), F32), pltpu.VMEM((8, MAIN_W), F32)],
        input_output_aliases={6: 0},
        compiler_params=_cp(("arbitrary", "arbitrary")),
    )(dy, dy, p, p, wbd, scale.reshape(1, MAIN_W), dz_alias)


def _head_masks(shape):
    lane = lax.broadcasted_iota(jnp.int32, shape, 1)
    return [(lane // HEAD_DIM) == h for h in range(shape[1] // HEAD_DIM)]


def _row_of(bcast, mask):
    return jnp.max(jnp.where(mask, bcast, -jnp.inf), axis=-1, keepdims=True)


MEM_TQ = 2048


def _memattn_fwd(z, kv, y_alias, B, S, *, name, tq=MEM_TQ):
    T = z.shape[0]
    nt = S // tq

    def body(q_ref, k_ref, v_ref, _, y_ref, l_ref):
        q = q_ref[...]
        k = k_ref[...]
        v = v_ref[...]
        masks = _head_masks(q.shape)
        o = jnp.zeros(q.shape, F32)
        lse_b = jnp.zeros(q.shape, F32)
        for m in masks:
            qm = jnp.where(m, q, 0.0).astype(BF)
            s = lax.dot_general(qm, k, (((1,), (1,)), ((), ())), preferred_element_type=F32) * SCALE
            mx = jnp.max(s, axis=-1, keepdims=True)
            e = jnp.exp(s - mx)
            l = jnp.sum(e, axis=-1, keepdims=True)
            p = (e / l).astype(BF)
            o = o + jnp.where(m, jnp.dot(p, v, preferred_element_type=F32), 0.0)
            lse_b = lse_b + jnp.where(m, mx + jnp.log(l), 0.0)
        y_ref[...] = o.astype(y_ref.dtype)
        l_ref[...] = lse_b

    qblk = pl.BlockSpec((tq, MEM_W), lambda b, i: (b * nt + i, 3))
    return pl.pallas_call(
        body, name=name, grid=(B, nt),
        in_specs=[qblk, pl.BlockSpec((N_MEM, MEM_W), lambda b, i: (b, 0)), pl.BlockSpec((N_MEM, MEM_W), lambda b, i: (b, 1)),
                  pl.BlockSpec(memory_space=pl.ANY)],
        out_specs=[qblk, pl.BlockSpec((tq, MEM_W), lambda b, i: (b * nt + i, 0))],
        out_shape=[jax.ShapeDtypeStruct(y_alias.shape, y_alias.dtype), jax.ShapeDtypeStruct((T, MEM_W), F32)],
        input_output_aliases={3: 0},
        compiler_params=_cp(("parallel", "parallel")),
    )(z, kv, kv, y_alias)


def _memattn_bwd(dy, z, kv, lse, dz_alias, B, S, *, name, tq=MEM_TQ):
    nt = S // tq

    def body(do_ref, q_ref, k_ref, v_ref, l_ref, _, dz_ref, dk_ref, dv_ref, dk_acc, dv_acc):
        i = pl.program_id(1)
        do = do_ref[...]
        q = q_ref[...]
        k = k_ref[...]
        v = v_ref[...]
        lse_b = l_ref[...]
        masks = _head_masks(q.shape)
        dq = jnp.zeros(q.shape, F32)
        dk = jnp.zeros(k.shape, F32)
        dv = jnp.zeros(v.shape, F32)
        for m in masks:
            qm = jnp.where(m, q, 0.0).astype(BF)
            dom = jnp.where(m, do, 0.0).astype(BF)
            s = lax.dot_general(qm, k, (((1,), (1,)), ((), ())), preferred_element_type=F32) * SCALE
            p = jnp.exp(s - _row_of(lse_b, m))
            dp = lax.dot_general(dom, v, (((1,), (1,)), ((), ())), preferred_element_type=F32)
            delta = jnp.sum(p * dp, axis=-1, keepdims=True)
            ds = (p * (dp - delta) * SCALE).astype(BF)
            pb = p.astype(BF)
            dv = dv + jnp.where(m[:N_MEM], lax.dot_general(pb, dom, (((0,), (0,)), ((), ())), preferred_element_type=F32), 0.0)
            dk = dk + jnp.where(m[:N_MEM], lax.dot_general(ds, qm, (((0,), (0,)), ((), ())), preferred_element_type=F32), 0.0)
            dq = dq + jnp.where(m, jnp.dot(ds, k, preferred_element_type=F32), 0.0)
        dz_ref[...] = dq.astype(dz_ref.dtype)

        @pl.when(i == 0)
        def _():
            dk_acc[...] = dk
            dv_acc[...] = dv

        @pl.when(i > 0)
        def _():
            dk_acc[...] += dk
            dv_acc[...] += dv

        @pl.when(i == nt - 1)
        def _():
            dk_ref[...] = dk_acc[...]
            dv_ref[...] = dv_acc[...]

    qblk = pl.BlockSpec((tq, MEM_W), lambda b, i: (b * nt + i, 3))
    kblk = pl.BlockSpec((N_MEM, MEM_W), lambda b, i: (b, 0))
    return pl.pallas_call(
        body, name=name, grid=(B, nt),
        in_specs=[qblk, qblk, kblk, pl.BlockSpec((N_MEM, MEM_W), lambda b, i: (b, 1)),
                  pl.BlockSpec((tq, MEM_W), lambda b, i: (b * nt + i, 0)), pl.BlockSpec(memory_space=pl.ANY)],
        out_specs=[qblk, kblk, kblk],
        out_shape=[jax.ShapeDtypeStruct(dz_alias.shape, dz_alias.dtype),
                   jax.ShapeDtypeStruct((B * N_MEM, MEM_W), F32), jax.ShapeDtypeStruct((B * N_MEM, MEM_W), F32)],
        scratch_shapes=[pltpu.VMEM((N_MEM, MEM_W), F32), pltpu.VMEM((N_MEM, MEM_W), F32)],
        input_output_aliases={5: 0},
        compiler_params=_cp(("parallel", "arbitrary")),
    )(dy, z, kv, kv, lse, dz_alias)


N_UNITS = 16


def _unit_rows(g):
    d = DIL[g]
    nb = N_UNITS // d
    return [pl.ds(n * STEPS * d + r, STEPS, stride=d) if d > 1 else pl.ds(n * STEPS, STEPS)
            for n in range(nb) for r in range(d)]


def _load_units(ref, g):
    if DIL[g] == 1:
        return ref[...].reshape(N_UNITS, STEPS, 128)
    return jnp.stack([ref[rows, :] for rows in _unit_rows(g)])


def _store_units(ref, val, g):
    if DIL[g] == 1:
        ref[...] = val.reshape(N_UNITS * STEPS, 128)
    else:
        for u, rows in enumerate(_unit_rows(g)):
            ref[rows, :] = val[u]


def _shift_units(x, by):
    z = jnp.zeros((abs(by),) + x.shape[1:], x.dtype)
    return jnp.concatenate([z, x[:N_UNITS - by]], axis=0) if by > 0 else jnp.concatenate([x[-by:], z], axis=0)


def _bdot(a, b, ca, cb):
    return lax.dot_general(a, b, (((ca,), (cb,)), ((0,), (0,))), preferred_element_type=F32)


def _dil_masks(g):
    d = DIL[g]
    has_prev = N_UNITS // d > 1
    qi = lax.broadcasted_iota(jnp.int32, (1, STEPS, STEPS), 1)
    kj = lax.broadcasted_iota(jnp.int32, (1, STEPS, STEPS), 2)
    unit = lax.broadcasted_iota(jnp.int32, (N_UNITS, 1, 1), 0)
    cur = kj <= qi
    prev = jnp.logical_and(kj >= qi, unit >= d) if has_prev else None
    lane = lax.broadcasted_iota(jnp.int32, (1, 1, 128), 2)
    heads = [(lane // HEAD_DIM) == h for h in range(128 // HEAD_DIM)]
    return has_prev, cur, prev, heads


def _dil_fwd(g, q, k, v, o_alias, l_alias, B, S, *, name):
    assert S == N_UNITS * STEPS
    d = DIL[g]

    def body(q_ref, k_ref, v_ref, _, __, o_ref, l_ref):
        has_prev, cur, prev, heads = _dil_masks(g)
        q = _load_units(q_ref, g)
        kc = _load_units(k_ref, g).astype(BF)
        vc = _load_units(v_ref, g).astype(BF)
        if has_prev:
            kp, vp = _shift_units(kc, d), _shift_units(vc, d)
        o = jnp.zeros(q.shape, F32)
        lse_b = jnp.zeros(q.shape, F32)
        for m in heads:
            qm = jnp.where(m, q, 0.0).astype(BF)
            sc = jnp.where(cur, _bdot(qm, kc, 2, 2) * SCALE, NEG)
            mx = jnp.max(sc, axis=-1, keepdims=True)
            if has_prev:
                sp = jnp.where(prev, _bdot(qm, kp, 2, 2) * SCALE, NEG)
                mx = jnp.maximum(mx, jnp.max(sp, axis=-1, keepdims=True))
            l = jnp.sum(jnp.exp(sc - mx), axis=-1, keepdims=True)
            if has_prev:
                l = l + jnp.sum(jnp.exp(sp - mx), axis=-1, keepdims=True)
            lse = mx + jnp.log(l)
            oh = _bdot(jnp.exp(sc - lse).astype(BF), vc, 2, 1)
            if has_prev:
                oh = oh + _bdot(jnp.exp(sp - lse).astype(BF), vp, 2, 1)
            o = o + jnp.where(m, oh, 0.0)
            lse_b = lse_b + jnp.where(m, lse, 0.0)
        _store_units(o_ref, o, g)
        _store_units(l_ref, lse_b, g)

    blk = pl.BlockSpec((S, 128), lambda b, hf: (b, g * 2 + hf))
    anyspec = pl.BlockSpec(memory_space=pl.ANY)
    o, l = pl.pallas_call(
        body, name=name, grid=(B, 2),
        in_specs=[blk, blk, blk, anyspec, anyspec], out_specs=[blk, blk],
        out_shape=[jax.ShapeDtypeStruct(q.shape, F32)] * 2,
        input_output_aliases={3: 0, 4: 1},
        compiler_params=_cp(("parallel", "parallel")),
    )(q, k, v, o_alias, l_alias)
    return o, l


def _dil_bwd(g, q, k, v, do, cb, lse, aliases, B, S, *, name):
    assert S == N_UNITS * STEPS
    d = DIL[g]

    def body(q_ref, k_ref, v_ref, do_ref, c_ref, l_ref, _, __, ___, dq_ref, dk_ref, dv_ref):
        has_prev, cur, prev, heads = _dil_masks(g)
        q = _load_units(q_ref, g)
        kc = _load_units(k_ref, g).astype(BF)
        vc = _load_units(v_ref, g).astype(BF)
        do = _load_units(do_ref, g)
        cbv = _load_units(c_ref, g)
        lse_b = _load_units(l_ref, g)
        if has_prev:
            kp, vp = _shift_units(kc, d), _shift_units(vc, d)
        z = jnp.zeros(q.shape, F32)
        dq, dkc, dkp, dvc, dvp = z, z, z, z, z
        for m in heads:
            qm = jnp.where(m, q, 0.0).astype(BF)
            dom = jnp.where(m, do, 0.0).astype(BF)
            lse = jnp.max(jnp.where(m, lse_b, -jnp.inf), axis=-1, keepdims=True)
            c = jnp.max(jnp.where(m, cbv, -jnp.inf), axis=-1, keepdims=True)
            sc = jnp.where(cur, _bdot(qm, kc, 2, 2) * SCALE, NEG)
            pc = jnp.exp(sc - lse)
            dsc = (pc * (_bdot(dom, vc, 2, 2) + c) * SCALE).astype(BF)
            dqh = _bdot(dsc, kc, 2, 1)
            dkc = dkc + jnp.where(m, _bdot(dsc, qm, 1, 1), 0.0)
            dvc = dvc + jnp.where(m, _bdot(pc.astype(BF), dom, 1, 1), 0.0)
            if has_prev:
                sp = jnp.where(prev, _bdot(qm, kp, 2, 2) * SCALE, NEG)
                pp = jnp.exp(sp - lse)
                dsp = (pp * (_bdot(dom, vp, 2, 2) + c) * SCALE).astype(BF)
                dqh = dqh + _bdot(dsp, kp, 2, 1)
                dkp = dkp + jnp.where(m, _bdot(dsp, qm, 1, 1), 0.0)
                dvp = dvp + jnp.where(m, _bdot(pp.astype(BF), dom, 1, 1), 0.0)
            dq = dq + jnp.where(m, dqh, 0.0)
        if has_prev:
            dkc = dkc + _shift_units(dkp, -d)
            dvc = dvc + _shift_units(dvp, -d)
        _store_units(dq_ref, dq, g)
        _store_units(dk_ref, dkc, g)
        _store_units(dv_ref, dvc, g)

    blk = pl.BlockSpec((S, 128), lambda b, hf: (b, g * 2 + hf))
    anyspec = pl.BlockSpec(memory_space=pl.ANY)
    return tuple(pl.pallas_call(
        body, name=name, grid=(B, 2),
        in_specs=[blk] * 6 + [anyspec] * 3, out_specs=[blk] * 3,
        out_shape=[jax.ShapeDtypeStruct(q.shape, F32)] * 3,
        input_output_aliases={6: 0, 7: 1, 8: 2},
        compiler_params=_cp(("parallel", "parallel")),
    )(q, k, v, do, cb, lse, *aliases))


def _kv_grad_sum(parts, cos, sin, *, name, tm=512):
    T = parts[0][0].shape[0]
    n_l = len(parts)

    def body(*refs):
        c_ref, s_ref = refs[0], refs[1]
        dk_ref, dv_ref = refs[2 + 2 * n_l:]
        dk = refs[2][...]
        dv = refs[3][...]
        for li in range(1, n_l):
            dk = dk + refs[2 + 2 * li][...]
            dv = dv + refs[3 + 2 * li][...]
        dk_ref[...] = _rot(dk, c_ref[...], s_ref[...], -1.0).astype(dk_ref.dtype)
        dv_ref[...] = dv.astype(dv_ref.dtype)

    full = pl.BlockSpec((tm, MAIN_W), lambda i: (i, 0))
    tab = pl.BlockSpec((tm, 128), lambda i: (i, 0))
    ops = [cos, sin] + [t for part in parts for t in part]
    return pl.pallas_call(
        body, name=name, grid=(T // tm,), in_specs=[tab, tab] + [full] * (2 * n_l), out_specs=[full, full],
        out_shape=[jax.ShapeDtypeStruct((T, MAIN_W), BF)] * 2,
        compiler_params=_cp(("parallel",)),
    )(*ops)


def _group_softmax(lse):
    l0, l1, l2 = lse[:, 0:256], lse[:, 256:512], lse[:, 512:768]
    mx = jnp.maximum(jnp.maximum(l0, l1), l2)
    e0, e1, e2 = jnp.exp(l0 - mx), jnp.exp(l1 - mx), jnp.exp(l2 - mx)
    tot = e0 + e1 + e2
    return e0 / tot, e1 / tot, e2 / tot


def _dil_combine_fwd(o, lse, y_alias, *, name, tm=512):
    T = o.shape[0]

    def body(o_ref, l_ref, _, y_ref):
        a = jnp.concatenate(_group_softmax(l_ref[...]), axis=1)
        y_ref[...] = (o_ref[...] * a).astype(y_ref.dtype)

    blk = pl.BlockSpec((tm, MAIN_W), lambda i: (i, 0))
    return pl.pallas_call(
        body, name=name, grid=(T // tm,), in_specs=[blk, blk, pl.BlockSpec(memory_space=pl.ANY)], out_specs=blk,
        out_shape=jax.ShapeDtypeStruct(y_alias.shape, y_alias.dtype), input_output_aliases={2: 0},
        compiler_params=_cp(("parallel",)),
    )(o, lse, y_alias)


def _dil_combine_bwd(dy, o, lse, *, name, tm=256):
    T = o.shape[0]
    lane_r = lax.broadcasted_iota(jnp.int32, (256, 256), 0) // HEAD_DIM
    lane_c = lax.broadcasted_iota(jnp.int32, (256, 256), 1) // HEAD_DIM
    ones_bd = (lane_r == lane_c).astype(BF)

    def body(dy_ref, o_ref, l_ref, e_ref, do_ref, c_ref):
        dyv = dy_ref[...]
        alphas = _group_softmax(l_ref[...])
        prod = dyv * o_ref[...]
        e = e_ref[...]
        tot = jnp.zeros((tm, 256), F32)
        for gi in range(3):
            x = prod[:, gi * 256:(gi + 1) * 256]
            hi = x.astype(BF)
            lo = (x - hi.astype(F32)).astype(BF)
            dalpha = jnp.dot(hi, e, preferred_element_type=F32) + jnp.dot(lo, e, preferred_element_type=F32)
            tot = tot + alphas[gi] * dalpha
        a = jnp.concatenate(alphas, axis=1)
        do_ref[...] = (dyv * a).astype(do_ref.dtype)
        c_ref[...] = jnp.concatenate([-al * tot for al in alphas], axis=1)

    blk = pl.BlockSpec((tm, MAIN_W), lambda i: (i, 0))
    return pl.pallas_call(
        body, name=name, grid=(T // tm,),
        in_specs=[blk, blk, blk, pl.BlockSpec((256, 256), lambda i: (0, 0))], out_specs=[blk, blk],
        out_shape=[jax.ShapeDtypeStruct((T, MAIN_W), F32), jax.ShapeDtypeStruct((T, MAIN_W), F32)],
        compiler_params=_cp(("parallel",)),
    )(dy, o, lse, ones_bd)


def _loss(y, target, *, name, tm=512):
    T, Dm = y.shape
    nt = T // tm

    def body(y_ref, t_ref, l_ref, d_ref, acc):
        i = pl.program_id(0)
        err = y_ref[...] - t_ref[...]
        d_ref[...] = err / Dm
        part = jnp.sum(jnp.mean(err * err, axis=-1, keepdims=True).reshape(tm // 8, 8, 1), axis=0)

        @pl.when(i == 0)
        def _():
            acc[...] = part

        @pl.when(i > 0)
        def _():
            acc[...] += part

        @pl.when(i == nt - 1)
        def _():
            l_ref[...] = 0.5 * jnp.sum(acc[...], axis=0, keepdims=True)

    row = pl.BlockSpec((tm, Dm), lambda i: (i, 0))
    return pl.pallas_call(
        body, name=name, grid=(nt,), in_specs=[row, row],
        out_specs=[pl.BlockSpec((1, 1), lambda i: (0, 0)), row],
        out_shape=[jax.ShapeDtypeStruct((1, 1), F32), jax.ShapeDtypeStruct((T, Dm), F32)],
        scratch_shapes=[pltpu.VMEM((8, 1), F32)],
        compiler_params=_cp(("arbitrary",)),
    )(y, target)


def _adamw(w, g, m, v, *, name):
    shape = w.shape
    cols = shape[-1]
    rows = w.size // cols
    tm = rows
    for cand in (512, 352, 256, 128):
        if rows > cand and rows % cand == 0 and cand * cols * 4 <= (1 << 20):
            tm = cand
            break

    def body(w_ref, g_ref, m_ref, v_ref, d_ref, mo_ref, vo_ref):
        gv = g_ref[...]
        mn = ADAM_B1 * m_ref[...] + (1.0 - ADAM_B1) * gv
        vn = ADAM_B2 * v_ref[...] + (1.0 - ADAM_B2) * (gv * gv)
        m_hat = mn / (1.0 - ADAM_B1 ** ADAM_STEP)
        v_hat = vn / (1.0 - ADAM_B2 ** ADAM_STEP)
        d_ref[...] = -ADAM_LR * (m_hat / (jnp.sqrt(v_hat) + ADAM_EPS) + ADAM_WD * w_ref[...])
        mo_ref[...] = mn
        vo_ref[...] = vn

    blk = pl.BlockSpec((tm, cols), lambda i: (i, 0))
    outs = pl.pallas_call(
        body, name=name, grid=(rows // tm,), in_specs=[blk] * 4, out_specs=[blk] * 3,
        out_shape=[jax.ShapeDtypeStruct((rows, cols), F32)] * 3,
        compiler_params=_cp(("parallel",)),
    )(*[t.reshape(rows, cols) for t in (w, g, m, v)])
    return tuple(t.reshape(shape) for t in outs)


def _adamw_layer(name, l, w, g, m, v, prev, after=None):
    L, rows, cols = w.shape
    tm = rows
    for cand in (512, 352, 256, 176, 128, 64):
        if rows % cand == 0 and cand * cols * 4 <= (1 << 21):
            tm = cand
            break
    if prev is None:
        prev = tuple(lax.empty(w.shape, F32) for _ in range(4))

    n_after = 0 if after is None else 1

    def body(w_ref, g_ref, m_ref, v_ref, *rest):
        d_ref, mo_ref, vo_ref, go_ref = rest[4 + n_after:]
        gv = g_ref[...]
        mn = ADAM_B1 * m_ref[...] + (1.0 - ADAM_B1) * gv
        vn = ADAM_B2 * v_ref[...] + (1.0 - ADAM_B2) * (gv * gv)
        m_hat = mn / (1.0 - ADAM_B1 ** ADAM_STEP)
        v_hat = vn / (1.0 - ADAM_B2 ** ADAM_STEP)
        d_ref[...] = -ADAM_LR * (m_hat / (jnp.sqrt(v_hat) + ADAM_EPS) + ADAM_WD * w_ref[...])
        mo_ref[...] = mn
        vo_ref[...] = vn
        go_ref[...] = gv

    lay = pl.BlockSpec((None, tm, cols), lambda i: (l, i, 0))
    one = pl.BlockSpec((None, tm, cols), lambda i: (0, i, 0))
    return tuple(pl.pallas_call(
        body, name=f"l{l}_adamw_{name}", grid=(rows // tm,),
        in_specs=[lay, one, lay, lay] + [pl.BlockSpec(memory_space=pl.ANY)] * (4 + n_after), out_specs=[lay] * 4,
        out_shape=[jax.ShapeDtypeStruct(w.shape, F32)] * 4,
        input_output_aliases={4 + i: i for i in range(4)},
        compiler_params=_cp(("parallel",)),
    )(w, g, m, v, *prev, *([] if after is None else [after])))


BIG = {
    'w_in': ((DEPTH, D_MODEL, D_MODEL), 'row'),
    'w_mem_kv': ((DEPTH, D_MODEL, 2 * MEM_W), 'row'),
    'w_out': ((DEPTH, D_MODEL, D_MODEL), 'row'),
    'w_kv': ((1, D_MODEL, 2 * MAIN_W), 'col'),
    'w_gate_up': ((DEPTH, D_MODEL, 2 * D_FF), 'col'),
    'w_down': ((DEPTH, D_FF, D_MODEL), 'row'),
}
BIG_NAMES = tuple(BIG)
N_CHIPS = 4
HBM_ANY = pl.BlockSpec(memory_space=pl.ANY)


def _geom(name):
    (L, R, C), kind = BIG[name]
    if kind == 'row':
        return L, R, C, kind, R // N_CHIPS, C, R // (2 * N_CHIPS)
    return L, R, C, kind, R, C // N_CHIPS, R // 2


def _shard_shape(name):
    L, R, C, kind, rs, cs, rh = _geom(name)
    return (L, rs, cs)


def _half_shape(name):
    L, R, C, kind, rs, cs, rh = _geom(name)
    return (L, rh, cs)


def _full_win(ref, name, s, h):
    L, R, C, kind, rs, cs, rh = _geom(name)
    if kind == 'row':
        rows = pl.ds(s * rs, rs) if h is None else pl.ds(s * rs + h * rh, rh)
        return ref.at[:, rows, :]
    rows = slice(None) if h is None else pl.ds(h * rh, rh)
    return ref.at[:, rows, pl.ds(s * cs, cs)]


def _shard_half(ref, name, h):
    L, R, C, kind, rs, cs, rh = _geom(name)
    return ref.at[:, pl.ds(h * rh, rh), :]


def _halves_win(ref, name, s):
    L, R, C, kind, rs, cs, rh = _geom(name)
    if kind == 'row':
        return ref.at[:, pl.ds(s * rh, rh), :]
    return ref.at[:, :, pl.ds(s * cs, cs)]


def _halves_shape(name):
    L, R, C, kind, rs, cs, rh = _geom(name)
    return (L, N_CHIPS * rh, cs) if kind == 'row' else (L, rh, C)


def _place():
    x, y, c = lax.axis_index("x"), lax.axis_index("y"), lax.axis_index("c")
    chips = [(1 - x, y), (x, 1 - y), (1 - x, 1 - y)]
    return x, y, c, chips


SMALL_ROWS = 24


SEM_SPEC = pl.BlockSpec(memory_space=pltpu.SEMAPHORE)
HBM_SPEC = pl.BlockSpec(memory_space=pltpu.HBM)
DATAFLOW = pltpu.SideEffectType.DATAFLOW_SIDE_EFFECTING


def _in_hbm(a):
    return pltpu.with_memory_space_constraint(a, pltpu.HBM)


def _remote(src, dst, send_sems, recv_sems, k, to):
    return pltpu.make_async_remote_copy(src_ref=src, dst_ref=dst, send_sem=send_sems.at[k], recv_sem=recv_sems.at[k],
                                        device_id=to, device_id_type=MESH)


def _split_start(name, bufs, n_copies, sends, after=None):
    nb = len(bufs)
    n_in = nb + (0 if after is None else 1)

    def body(*refs):
        in_refs = refs[:nb]
        send_sems, recv_sems = refs[n_in], refs[n_in + 1]
        token = refs[-1]
        for k, (src, dst, to) in enumerate(sends(in_refs)):
            _remote(src, dst, send_sems, recv_sems, k, to).start()
        token[...] = jnp.zeros_like(token)

    outs = pl.pallas_call(
        body, name=name,
        out_shape=(pltpu.SemaphoreType.DMA((n_copies,)), pltpu.SemaphoreType.DMA((n_copies,)),
                   *[pltpu.HBM(b.shape, b.dtype) for b in bufs], jax.ShapeDtypeStruct((8, 128), F32)),
        in_specs=[HBM_SPEC] * nb + [HBM_ANY] * (n_in - nb),
        out_specs=(SEM_SPEC, SEM_SPEC, *[HBM_SPEC] * nb, pl.BlockSpec(memory_space=pltpu.VMEM)),
        input_output_aliases={i: 2 + i for i in range(nb)},
        compiler_params=pltpu.CompilerParams(has_side_effects=DATAFLOW),
    )(*[_in_hbm(b) for b in bufs], *([] if after is None else [after]))
    return outs[0], outs[1], list(outs[2:2 + nb]), outs[-1]


def _split_wait(name, send_sems, recv_sems, bufs, after, sends, arrivals):
    nb = len(bufs)

    def body(*refs):
        in_refs = refs[:nb]
        s_sems, r_sems = refs[nb], refs[nb + 1]
        me = (lax.axis_index("x"), lax.axis_index("y"), lax.axis_index("c"))
        for k, (src, dst, to) in enumerate(sends(in_refs)):
            _remote(src, dst, s_sems, r_sems, k, to).wait_send()
        for k, win in enumerate(arrivals(in_refs)):
            _remote(win, win, s_sems, r_sems, k, me).wait_recv()

    outs = pl.pallas_call(
        body, name=name,
        out_shape=[pltpu.HBM(b.shape, b.dtype) for b in bufs],
        in_specs=[HBM_SPEC] * nb + [SEM_SPEC, SEM_SPEC, HBM_ANY],
        out_specs=[HBM_SPEC] * nb,
        input_output_aliases={i: i for i in range(nb)},
        compiler_params=pltpu.CompilerParams(has_side_effects=DATAFLOW),
    )(*bufs, send_sems, recv_sems, after)
    return list(outs)


MIX_W = ('w_in', 'w_mem_kv', 'w_out')
FFN_W = ('w_gate_up', 'w_down')
LAYER_W = MIX_W + FFN_W


def _place_own(tag, names, sources, small, sc):
    nw = len(names)
    has_small = small is not None

    def body(sc_ref, *refs):
        srcs = refs[:nw]
        shard_out = refs[nw + has_small:2 * nw + has_small]
        full_out = refs[2 * nw + has_small:3 * nw + has_small]
        for src, sh, fu in zip(srcs, shard_out, full_out):
            v = src[...].astype(BF)
            sh[...] = v
            fu[...] = v
        if has_small:
            refs[-1][...] = refs[nw][...]

    in_specs, shard_specs, full_specs, shard_shape, full_shape, ops = [], [], [], [], [], []
    for nm, (arr, layer) in zip(names, sources):
        L, R, C, kind, rs, cs, rh = _geom(nm)
        in_specs.append(pl.BlockSpec((1, rs, cs), lambda i, sc_ref, layer=layer: (layer, 0, 0)))
        shard_specs.append(pl.BlockSpec((1, rs, cs), lambda i, sc_ref: (0, 0, 0)))
        if kind == 'row':
            full_specs.append(pl.BlockSpec((1, rs, cs), lambda i, sc_ref: (0, sc_ref[0], 0)))
        else:
            full_specs.append(pl.BlockSpec((1, rs, cs), lambda i, sc_ref: (0, 0, sc_ref[0])))
        shard_shape.append(jax.ShapeDtypeStruct((1, rs, cs), BF))
        full_shape.append(jax.ShapeDtypeStruct((1, R, C), BF))
        ops.append(arr)
    if has_small:
        in_specs.append(pl.BlockSpec((SMALL_ROWS, 256), lambda i, sc_ref: (0, 0)))
        full_specs.append(pl.BlockSpec((None, SMALL_ROWS, 256), lambda i, sc_ref: (sc_ref[0], 0, 0)))
        full_shape.append(jax.ShapeDtypeStruct((N_CHIPS, SMALL_ROWS, 256), F32))
        ops.append(small)
    outs = pl.pallas_call(
        body, name=f"{tag}_place_own_shard",
        grid_spec=pltpu.PrefetchScalarGridSpec(num_scalar_prefetch=1, grid=(1,), in_specs=in_specs,
                                               out_specs=shard_specs + full_specs),
        out_shape=shard_shape + full_shape,
        compiler_params=_cp(("arbitrary",)),
    )(sc, *ops)
    return list(outs[:nw]), list(outs[nw:])


def _gather_start(l, names, sources, small, sc, after=None):
    nw = len(names)
    has_small = small is not None
    shards, fulls = _place_own(l, names, sources, small, sc)
    bufs = list(shards) + ([small] if has_small else []) + list(fulls)
    n_src = nw + (1 if has_small else 0)

    def sends(refs):
        x, y, c, chips = _place()
        s = 2 * x + y
        out = []
        for (px, py) in chips:
            for wi, nm in enumerate(names):
                out.append((_shard_half(refs[wi], nm, c), _full_win(refs[n_src + wi], nm, s, c), (px, py, c)))
            if has_small:
                out.append((refs[nw], refs[n_src + nw].at[s], (px, py, c)))
        return out

    def arrivals(refs):
        x, y, c, chips = _place()
        out = []
        for (px, py) in chips:
            sp = 2 * px + py
            for wi, nm in enumerate(names):
                out.append(_full_win(refs[n_src + wi], nm, sp, c))
            if has_small:
                out.append(refs[n_src + nw].at[sp])
        return out

    n_copies = 3 * n_src
    send_sems, recv_sems, bufs, token = _split_start(f"{l}_gather_ici_start", bufs, n_copies, sends, after)
    return dict(l=l, names=names, has_small=has_small, sems=(send_sems, recv_sems), bufs=bufs, sends=sends,
                arrivals=arrivals, token=token)


def _gather_forward(st, after):
    l, names = st['l'], st['names']
    nw = len(names)
    n_src = nw + (1 if st['has_small'] else 0)
    bufs = _split_wait(f"{l}_gather_ici_wait", *st['sems'], st['bufs'], after, st['sends'], st['arrivals'])
    fulls = bufs[n_src:n_src + nw]
    small_all = bufs[n_src + nw] if st['has_small'] else None

    def sends(refs):
        x, y, c, chips = _place()
        out = []
        for (px, py) in chips:
            sp = 2 * px + py
            for wi, nm in enumerate(names):
                w = _full_win(refs[wi], nm, sp, c)
                out.append((w, w, (x, y, 1 - c)))
        return out

    def arrivals(refs):
        x, y, c, chips = _place()
        out = []
        for (px, py) in chips:
            sp = 2 * px + py
            for wi, nm in enumerate(names):
                out.append(_full_win(refs[wi], nm, sp, 1 - c))
        return out

    send_sems, recv_sems, fulls, token = _split_start(f"{l}_gather_d2d_start", fulls, 3 * nw, sends)
    return dict(l=l, names=names, sems=(send_sems, recv_sems), bufs=fulls, sends=sends, arrivals=arrivals,
                small_all=small_all, token=token)


def _gather_finish(st, after):
    fulls = _split_wait(f"{st['l']}_gather_d2d_wait", *st['sems'], st['bufs'], after, st['sends'], st['arrivals'])
    return dict(zip(st['names'], fulls)), st['small_all']


def _reduce_start(tag, names, grads):
    nw = len(names)
    recv = [lax.empty((1,) + _halves_shape(nm)[1:], F32) for nm in names]
    bufs = [grads[nm] for nm in names] + recv

    def windows(refs, half_of):
        x, y, c, _ = _place()
        h = half_of(c)
        out = []
        for wi, nm in enumerate(names):
            L, R, C, kind, rs, cs, rh = _geom(nm)
            if kind == 'row':
                for sp in range(N_CHIPS):
                    out.append((_full_win(refs[wi], nm, sp, h), _halves_win(refs[nw + wi], nm, sp)))
            else:
                out.append((refs[wi].at[:, pl.ds(h * rh, rh), :], refs[nw + wi]))
        return out

    def sends(refs):
        x, y, c, _ = _place()
        return [(src, dst, (x, y, 1 - c)) for src, dst in windows(refs, lambda c: 1 - c)]

    def arrivals(refs):
        return [dst for _, dst in windows(refs, lambda c: c)]

    n_copies = sum(N_CHIPS if BIG[nm][1] == 'row' else 1 for nm in names)
    send_sems, recv_sems, bufs, token = _split_start(tag + "_halves_start", bufs, n_copies, sends)
    return dict(tag=tag, names=names, sems=(send_sems, recv_sems), bufs=bufs, sends=sends, arrivals=arrivals, token=token)


def _reduce_mid(st, after, sc):
    tag, names = st['tag'], st['names']
    nw = len(names)
    bufs = _split_wait(tag + "_halves_wait", *st['sems'], st['bufs'], after, st['sends'], st['arrivals'])
    halves, own = [], []
    for wi, nm in enumerate(names):
        hb, ow = _add_halves(nm, bufs[wi], bufs[nw + wi], sc, tag)
        halves.append(hb)
        own.append(ow)
    pieces = [lax.empty((3, 1) + _half_shape(nm)[1:], BF) for nm in names]

    def sends(refs):
        x, y, c, chips = _place()
        out = []
        for j, (px, py) in enumerate(chips):
            for wi, nm in enumerate(names):
                out.append((_halves_win(refs[wi], nm, 2 * px + py), refs[nw + wi].at[j], (px, py, c)))
        return out

    def arrivals(refs):
        return [refs[nw + wi].at[j] for j in range(3) for wi in range(nw)]

    send_sems, recv_sems, bufs, token = _split_start(tag + "_pieces_start", halves + pieces, 3 * nw, sends)
    return dict(tag=tag, names=names, sems=(send_sems, recv_sems), bufs=bufs, sends=sends, arrivals=arrivals, own=own,
                token=token)


def _reduce_late(st, after, sc):
    tag, names = st['tag'], st['names']
    nw = len(names)
    bufs = _split_wait(tag + "_pieces_wait", *st['sems'], st['bufs'], after, st['sends'], st['arrivals'])
    gsh = [_sum_pieces(nm, st['own'][wi], bufs[nw + wi], sc, tag) for wi, nm in enumerate(names)]

    def sends(refs):
        x, y, c, _ = _place()
        return [(_shard_half(refs[wi], nm, c), _shard_half(refs[wi], nm, c), (x, y, 1 - c)) for wi, nm in enumerate(names)]

    def arrivals(refs):
        x, y, c, _ = _place()
        return [_shard_half(refs[wi], nm, 1 - c) for wi, nm in enumerate(names)]

    send_sems, recv_sems, bufs, token = _split_start(tag + "_share_start", gsh, nw, sends)
    return dict(tag=tag, names=names, sems=(send_sems, recv_sems), bufs=bufs, sends=sends, arrivals=arrivals, token=token)


def _reduce_finish(st, after):
    gsh = _split_wait(st['tag'] + "_share_wait", *st['sems'], st['bufs'], after, st['sends'], st['arrivals'])
    return dict(zip(st['names'], gsh))


def _add_halves(name, g, r, sc, tag):
    _, R, C, kind, rs, cs, rh = _geom(name)
    L = g.shape[0]
    tr = rh
    nr = rh // tr

    def body(sc_ref, g_ref, r_ref, hb_ref, own_ref):
        sp = pl.program_id(2)
        tot = g_ref[...] + r_ref[...]
        hb_ref[...] = tot.astype(hb_ref.dtype)

        @pl.when(sp == sc_ref[0])
        def _():
            own_ref[...] = tot

    if kind == 'row':
        g_map = lambda l, ri, sp, sc_ref: (l, sp * 2 + sc_ref[1], 0)
        h_map = lambda l, ri, sp, sc_ref: (l, sp, 0)
    else:
        g_map = lambda l, ri, sp, sc_ref: (l, sc_ref[1] * nr + ri, sp)
        h_map = lambda l, ri, sp, sc_ref: (l, ri, sp)
    own_map = lambda l, ri, sp, sc_ref: (l, ri, 0)
    blk = (None, tr, cs)
    return pl.pallas_call(
        body, name=tag + "_add_halves_" + name,
        grid_spec=pltpu.PrefetchScalarGridSpec(
            num_scalar_prefetch=1, grid=(L, nr, N_CHIPS),
            in_specs=[pl.BlockSpec(blk, g_map), pl.BlockSpec(blk, h_map)],
            out_specs=[pl.BlockSpec(blk, h_map), pl.BlockSpec(blk, own_map)]),
        out_shape=[jax.ShapeDtypeStruct((L,) + _halves_shape(name)[1:], BF),
                   jax.ShapeDtypeStruct((L,) + _half_shape(name)[1:], F32)],
        compiler_params=_cp(("parallel", "parallel", "arbitrary")),
    )(sc, g, r)


def _sum_pieces(name, own, pieces, sc, tag):
    _, R, C, kind, rs, cs, rh = _geom(name)
    L = own.shape[0]
    tr = rh
    nr = rh // tr

    def body(sc_ref, o_ref, p_ref, out_ref):
        out_ref[...] = o_ref[...] + p_ref[0].astype(F32) + p_ref[1].astype(F32) + p_ref[2].astype(F32)

    blk = (None, tr, cs)
    return pl.pallas_call(
        body, name=tag + "_sum_pieces_" + name,
        grid_spec=pltpu.PrefetchScalarGridSpec(
            num_scalar_prefetch=1, grid=(L, nr),
            in_specs=[pl.BlockSpec(blk, lambda l, ri, sc_ref: (l, ri, 0)),
                      pl.BlockSpec((3, None, tr, cs), lambda l, ri, sc_ref: (0, l, ri, 0))],
            out_specs=pl.BlockSpec(blk, lambda l, ri, sc_ref: (l, sc_ref[1] * nr + ri, 0))),
        out_shape=jax.ShapeDtypeStruct((L,) + _shard_shape(name)[1:], F32),
        compiler_params=_cp(("parallel", "parallel")),
    )(sc, own, pieces)


def _small_gather_start(v, sc):
    rows = v.shape[0]

    def place(sc_ref, v_ref, o_ref):
        o_ref[...] = v_ref[...]

    slots = pl.pallas_call(
        place, name="small_grads_place_own",
        grid_spec=pltpu.PrefetchScalarGridSpec(
            num_scalar_prefetch=1, grid=(1,),
            in_specs=[pl.BlockSpec((rows, 128), lambda i, sc_ref: (0, 0))],
            out_specs=pl.BlockSpec((None, rows, 128), lambda i, sc_ref: (2 * sc_ref[0] + sc_ref[1], 0, 0))),
        out_shape=jax.ShapeDtypeStruct((8, rows, 128), v.dtype),
        compiler_params=_cp(("arbitrary",)),
    )(sc, v)

    def peers():
        x, y, c, _ = _place()
        flips = [(fx, fy, fc) for fx in (0, 1) for fy in (0, 1) for fc in (0, 1)][1:]
        return [((1 - x if fx else x), (1 - y if fy else y), (1 - c if fc else c)) for fx, fy, fc in flips]

    def sends(refs):
        x, y, c, _ = _place()
        return [(refs[0], refs[1].at[4 * x + 2 * y + c], p) for p in peers()]

    def arrivals(refs):
        return [refs[1].at[4 * px + 2 * py + pc] for px, py, pc in peers()]

    send_sems, recv_sems, bufs, token = _split_start("small_grads_gather_start", [v, slots], 7, sends)
    return dict(sems=(send_sems, recv_sems), bufs=bufs, sends=sends, arrivals=arrivals, token=token)


def _small_gather_finish(st, after):
    return _split_wait("small_grads_gather_wait", *st['sems'], st['bufs'], after, st['sends'], st['arrivals'])[1]


def _sum8(v8, *, name, tr=336):
    rows = v8.shape[1]
    tr = min(tr, rows)
    assert rows % tr == 0

    def body(v_ref, o_ref):
        tot = v_ref[0].astype(F32)
        for d in range(1, 8):
            tot = tot + v_ref[d].astype(F32)
        o_ref[...] = tot

    return pl.pallas_call(
        body, name=name, grid=(rows // tr,),
        in_specs=[pl.BlockSpec((8, tr, 128), lambda i: (0, i, 0))], out_specs=pl.BlockSpec((tr, 128), lambda i: (i, 0)),
        out_shape=jax.ShapeDtypeStruct((rows, 128), F32),
        compiler_params=_cp(("parallel",)),
    )(v8)


def _block_diag(w_pool_l):
    wbd = jnp.zeros((MAIN_W, MAIN_W), F32)
    for gi in range(len(POOL_WINDOWS)):
        wbd = lax.dynamic_update_slice(wbd, w_pool_l[gi], (gi * POOL_GROUP, gi * POOL_GROUP))
    return wbd.astype(BF)


def _unpack_small(small_all):
    ng = small_all[:, :16, :].reshape(N_CHIPS, DEPTH, 4, 256).transpose(1, 2, 0, 3).reshape(DEPTH, 4, D_MODEL)
    ps = small_all[:, 16:18, :POOL_GROUP].transpose(1, 0, 2).reshape(N_A, MAIN_W)
    return ng, ps


def _local_step(x, mem, positions, on_forward, on_backward, mem_norm, w_pool, kv_norm, target):
    B, S, _ = x.shape
    T = B * S
    xc = x.reshape(T, D_MODEL)
    memf = mem.reshape(B * N_MEM, D_MODEL)
    tgt = target.reshape(T, D_MODEL)
    cos, sin = _rope_tables(positions.reshape(T, 1), name="rope_tables")
    wbd = [_block_diag(w_pool[l]) for l in range(N_A)]
    nbo = D_FF // 256
    fw = []
    rk = rv = None
    kv_saved = None
    wts = []
    norm_gains = pool_scale = y2 = None

    for l in range(DEPTH):
        t = f"l{l}_"
        got = on_forward('start', l, y2)
        wts.append(dict(got[0]))
        if l == 0:
            norm_gains, pool_scale = _unpack_small(got[1])
        sv = {'x_in': xc}
        h0 = _norm_fwd(xc, norm_gains[l, 0], name=t + "norm0", out_dtype=BF, tm=1024, after=got[2])
        z, = _mm(h0, wts[l]['w_in'], b_layer=0, name=t + "mm_in", tm=1024, tn=1024)
        memn = _norm_fwd(memf, mem_norm[l], name=t + "norm_mem", out_dtype=BF, tm=256)
        kvm, = _mm(memn, wts[l]['w_mem_kv'], b_layer=0, name=t + "mm_memkv", out_dtypes=(BF,))
        if l < N_A:
            ycat, sv['p'] = _pool_fwd(z, wbd[l], pool_scale[l], B, S, name=t + "pool_fwd")
        else:
            rq = _rope_apply(z, cos, sin, name=t + "rope_q", out_dtype=F32, tm=1024)
            o = lax.empty((T, MAIN_W), F32)
            lse = lax.empty((T, MAIN_W), F32)
            for g in range(3):
                o, lse = _dil_fwd(g, rq, rk, rv, o, lse, B, S, name=t + f"dil_fwd{g}")
            ycat = _dil_combine_fwd(o, lse, lax.empty((T, D_MODEL), BF), name=t + "dil_combine", tm=1024)
            sv.update(rq=rq, o=o, lse=lse)
        ycat, sv['lse_m'] = _memattn_fwd(z, kvm, ycat, B, S, name=t + "memattn_fwd")
        tok = on_forward('mid', l, ycat)
        y1, = _mm(ycat, wts[l]['w_out'], b_layer=0, name=t + "mm_out", tm=1024, tn=1024)
        wts[l].update(on_forward('ffn', l, y1)[0])
        x1 = _norm_fwd(y1, norm_gains[l, 1], name=t + "norm1", res=xc, after=tok)
        h2 = _norm_fwd(x1, norm_gains[l, 2], name=t + "norm2", out_dtype=BF, tm=1024)
        gg, uu, aa = _mm(h2, wts[l]['w_gate_up'], b_layer=0, b_offsets=(0, nbo), out_n=D_FF, tm=4096, tn=256, name=t + "mm_gate_up",
                         epilogue=_swiglu_fwd_epilogue, out_dtypes=(BF, BF, BF))
        on_forward('post', l, gg)
        y2, = _mm(aa, wts[l]['w_down'], b_layer=0, tk=D_FF, name=t + "mm_down")
        x2 = _norm_fwd(y2, norm_gains[l, 3], name=t + "norm3", res=x1)
        sv.update(h0=h0, z=z, memn=memn, kvm=kvm, ycat=ycat, y1=y1, x1=x1, h2=h2, gg=gg, uu=uu, aa=aa, y2=y2)
        fw.append(sv)
        xc = x2
        if l == N_A - 1:
            kvn = _norm_fwd(xc, kv_norm, name="norm_kv", out_dtype=BF, tm=1024)
            kv, = _mm(kvn, wts[N_A - 1]['w_kv'], b_layer=0, name="mm_kv")
            rk, rv = _rope_apply(kv, cos, sin, name="rope_k", passthrough=True, out_dtype=F32, tm=1024)
            kv_saved = (xc, kvn)

    loss, dx = _loss(xc, tgt, name="loss", tm=1024)

    d_ng = [[None] * 4 for _ in range(DEPTH)]
    d_memnorm = [None] * DEPTH
    d_wbd = [None] * N_A
    d_pscale = [None] * N_A
    d_kvnorm = None
    kv_parts = []
    tok = None

    def as3d(gl):
        return {nm: g.reshape((1,) + g.shape) for nm, g in gl.items()}

    for l in reversed(range(DEPTH)):
        t = f"l{l}_b_"
        sv = fw[l]
        gl = {}
        dy2, d_ng[l][3] = _norm_bwd(dx, sv['y2'], norm_gains[l, 3], name=t + "norm3", out_dtype=BF, tm=1024, after=tok)
        gl['w_down'], = _mm(sv['aa'], dy2, ta=True, tm=1408, tn=512, tk=4096, name=t + "dw_down")
        dg, du = _mm(dy2, wts[l]['w_down'], tb=True, b_layer=0, tm=1024, tn=1408, name=t + "d_act",
                     extras=((sv['gg'], 'tile'), (sv['uu'], 'tile')), epilogue=_swiglu_bwd_epilogue, out_dtypes=(BF, BF))
        gl['w_gate_up'], = _mm(sv['h2'], (dg, du), ta=True, tn=1408, tk=1024, name=t + "dw_gate_up")
        dh2, = _mm((dg, du), wts[l]['w_gate_up'], tb=True, b_layer=0, tn=1024, tk=1408, name=t + "d_h2", out_dtypes=(BF,))
        dx1, d_ng[l][2] = _norm_bwd(dh2, sv['x1'], norm_gains[l, 2], name=t + "norm2", add=dx, tm=1024)
        tok = on_backward('ffn', l, dx1, as3d(gl))
        dy1, d_ng[l][1] = _norm_bwd(dx1, sv['y1'], norm_gains[l, 1], name=t + "norm1", out_dtype=BF, tm=1024, after=tok)
        gl['w_out'], = _mm(sv['ycat'], dy1, ta=True, name=t + "dw_out", tk=4096)
        dycat, = _mm(dy1, wts[l]['w_out'], tb=True, b_layer=0, name=t + "d_ycat", tm=1024, tn=1024)
        dz = lax.empty((T, D_MODEL), BF)
        dz, dkm, dvm = _memattn_bwd(dycat, sv['z'], sv['kvm'], sv['lse_m'], dz, B, S, name=t + "memattn")
        if l < N_A:
            dz, d_wbd[l], d_pscale[l] = _pool_bwd(dycat, sv['p'], wbd[l], pool_scale[l], dz, B, S, name=t + "pool")
        else:
            do, cb = _dil_combine_bwd(dycat, sv['o'], sv['lse'], name=t + "dil_combine", tm=512)
            acc = tuple(lax.empty((T, MAIN_W), F32) for _ in range(3))
            for g in range(3):
                acc = _dil_bwd(g, sv['rq'], rk, rv, do, cb, sv['lse'], acc, B, S, name=t + f"dil{g}")
            dz = _rope_apply(acc[0], cos, sin, name=t + "rope_q", sign=-1.0, alias=dz, tm=1024)
            kv_parts.append(acc[1:])
        tok = on_backward('mix', l, dz, as3d(gl))
        gl['w_in'], = _mm(sv['h0'], dz, ta=True, name=t + "dw_in", tk=4096)
        dh0, = _mm(dz, wts[l]['w_in'], tb=True, b_layer=0, name=t + "d_h0", out_dtypes=(BF,), tm=1024, tn=1024)
        dx, d_ng[l][0] = _norm_bwd(dh0, sv['x_in'], norm_gains[l, 0], name=t + "norm0", add=dx1, tm=1024, after=tok)
        gl['w_mem_kv'], = _mm(sv['memn'], (dkm, dvm), ta=True, tn=256, name=t + "dw_memkv")
        dmemn, = _mm((dkm, dvm), wts[l]['w_mem_kv'], tb=True, b_layer=0, tk=256, name=t + "d_memn", out_dtypes=(BF,))
        _, d_memnorm[l] = _norm_bwd(dmemn, memf, mem_norm[l], name=t + "norm_mem", out_dtype=BF, tm=256)
        if l == N_A:
            dk, dv = _kv_grad_sum(kv_parts, cos, sin, name="kv_grad", tm=1024)
            x_kv, kvn = kv_saved
            gl['w_kv'], = _mm(kvn, (dk, dv), ta=True, tn=768, tk=2048, name="dw_kv")
            dkvn, = _mm((dk, dv), wts[N_A - 1]['w_kv'], tb=True, b_layer=0, tn=1024, tk=768, name="d_kvn", out_dtypes=(BF,))
            dx, d_kvnorm = _norm_bwd(dkvn, x_kv, kv_norm, name="norm_kv_b", add=dx, tm=1024)
        tok = on_backward('end', l, dx, as3d(gl))

    small = {
        'norm_gains': jnp.stack([jnp.concatenate(d_ng[l], axis=0) for l in range(DEPTH)]),
        'mem_norm': jnp.concatenate(d_memnorm, axis=0),
        'kv_norm': d_kvnorm.reshape(D_MODEL),
        'pool_scale': jnp.concatenate(d_pscale, axis=0),
        'w_pool': jnp.stack([jnp.stack([d_wbd[l][gi * POOL_GROUP:(gi + 1) * POOL_GROUP, gi * POOL_GROUP:(gi + 1) * POOL_GROUP]
                                        for gi in range(len(POOL_WINDOWS))]) for l in range(N_A)]),
    }
    return loss, dx, small


SMALL_ORDER = ('norm_gains', 'mem_norm', 'kv_norm', 'pool_scale', 'w_pool')
SMALL_VEC_ROWS = 2560


def kernel(x, mem, positions, norm_gains, mem_norm, w_in, w_mem_kv, w_out, w_pool, pool_scale, kv_norm, w_kv, w_gate_up, w_down, loss_target, m_norm_gains, m_mem_norm, m_w_in, m_w_mem_kv, m_w_out, m_w_pool, m_pool_scale, m_kv_norm, m_w_kv, m_w_gate_up, m_w_down, v_norm_gains, v_mem_norm, v_w_in, v_w_mem_kv, v_w_out, v_w_pool, v_pool_scale, v_kv_norm, v_w_kv, v_w_gate_up, v_w_down):
    xi, yi, ci = lax.axis_index("x"), lax.axis_index("y"), lax.axis_index("c")
    s = 2 * xi + yi
    sc = jnp.stack([s, ci]).astype(jnp.int32)
    weights = dict(norm_gains=norm_gains, mem_norm=mem_norm, w_in=w_in, w_mem_kv=w_mem_kv, w_out=w_out, w_pool=w_pool,
                   pool_scale=pool_scale, kv_norm=kv_norm, w_kv=w_kv, w_gate_up=w_gate_up, w_down=w_down)
    moms = dict(norm_gains=m_norm_gains, mem_norm=m_mem_norm, w_in=m_w_in, w_mem_kv=m_w_mem_kv, w_out=m_w_out,
                w_pool=m_w_pool, pool_scale=m_pool_scale, kv_norm=m_kv_norm, w_kv=m_w_kv, w_gate_up=m_w_gate_up,
                w_down=m_w_down)
    vels = dict(norm_gains=v_norm_gains, mem_norm=v_mem_norm, w_in=v_w_in, w_mem_kv=v_w_mem_kv, w_out=v_w_out,
                w_pool=v_w_pool, pool_scale=v_pool_scale, kv_norm=v_kv_norm, w_kv=v_w_kv, w_gate_up=v_w_gate_up,
                w_down=v_w_down)

    small_w = jnp.zeros((SMALL_ROWS, 256), F32)
    small_w = lax.dynamic_update_slice(small_w, norm_gains.reshape(16, 256), (0, 0))
    small_w = lax.dynamic_update_slice(small_w, pool_scale, (16, 0))

    def shard_of(nm, l):
        return (w_kv.reshape(_shard_shape('w_kv')), 0) if nm == 'w_kv' else (weights[nm], l)

    groups = {'l0a': (0, MIX_W), 'l0b': (0, FFN_W)}
    groups.update({f"l{l}": (l, LAYER_W + (('w_kv',) if l == N_A - 1 else ())) for l in range(1, DEPTH)})
    on_ici, on_d2d, gathered = {}, {}, {}

    def start_group(tag, after):
        l, names = groups[tag]
        on_ici[tag] = _gather_start(tag, names, [shard_of(nm, l) for nm in names], small_w if tag == 'l0a' else None, sc,
                                    after)
        return [on_ici[tag]['token']]

    def on_forward(where, l, after):
        if where == 'start':
            if l == 0:
                start_group('l0a', None)
                st = on_ici.pop('l0a')
                fwd = _gather_forward(st, st['token'])
                w, small_all = _gather_finish(fwd, fwd['token'])
                return w, small_all, start_group('l0b', w['w_in'])
            if f"l{l}" not in on_d2d:
                on_d2d[f"l{l}"] = _gather_forward(on_ici.pop(f"l{l}"), after)
            gathered[l] = _gather_finish(on_d2d.pop(f"l{l}"), after)[0]
            tok = start_group(f"l{l + 1}", gathered[l]['w_in']) if l + 1 < DEPTH else None
            return {nm: w for nm, w in gathered[l].items() if nm not in FFN_W}, None, tok
        if where == 'mid' and l == 0:
            on_d2d['l0b'] = _gather_forward(on_ici.pop('l0b'), after)
            return start_group('l1', on_d2d['l0b']['token'])
        if where == 'ffn':
            if l == 0:
                return (_gather_finish(on_d2d.pop('l0b'), after)[0],)
            return ({nm: gathered[l][nm] for nm in FFN_W},)
        if where == 'post' and 0 < l < DEPTH - 1:
            on_d2d[f"l{l + 1}"] = _gather_forward(on_ici.pop(f"l{l + 1}"), after)
        return None

    hook_of = {'ffn': 0, 'mix': 1, 'end': 2}
    active, reduced = [], {l: {} for l in range(DEPTH)}
    advance = {'mid': lambda st, after: _reduce_mid(st, after, sc), 'late': lambda st, after: _reduce_late(st, after, sc)}

    def run_hook(idx, after):
        toks = []
        for grp in list(active):
            while grp['plan'] and grp['plan'][0][1] <= idx:
                step = grp['plan'].pop(0)[0]
                if step == 'finish':
                    reduced[grp['layer']].update(_reduce_finish(grp['st'], after))
                    active.remove(grp)
                else:
                    grp['st'] = advance[step](grp['st'], after)
                    toks.append(grp['st']['token'])
        return toks

    def on_backward(where, l, after, grads):
        idx = 3 * (DEPTH - 1 - l) + hook_of[where]
        toks = run_hook(idx, after)
        if where == 'end' or (where == 'ffn' and l == 0):
            names = FFN_W if where == 'ffn' else tuple(nm for nm in grads if l > 0 or nm not in FFN_W)
            st = _reduce_start(f"l{l}_{where}_grads", names, {nm: grads[nm] for nm in names})
            plan = [('mid', idx + 1), ('late', idx + 3), ('finish', idx + 4)] if where == 'ffn' else \
                   [('mid', idx + 1), ('late', idx + 2), ('finish', idx + 3)]
            active.append(dict(layer=l, st=st, plan=plan))
            toks.append(st['token'])
        return toks

    loss, gx, gsmall = _local_step(x, mem, positions, on_forward, on_backward, mem_norm, w_pool, kv_norm, loss_target)
    loss = lax.psum(loss[0, 0], ("x", "y", "c"))

    vec = jnp.concatenate([gsmall[nm].reshape(-1) for nm in SMALL_ORDER])
    vec = jnp.pad(vec, (0, SMALL_VEC_ROWS * 128 - vec.shape[0])).reshape(SMALL_VEC_ROWS, 128)
    vec = vec + sum(grp['st']['token'][0, 0] for grp in active)
    small_st = _small_gather_start(vec.astype(BF), sc)
    outs = {nm: None for nm in LAYER_W}

    def adamw_layers(layers, names, after):
        for l in layers:
            for nm in names:
                outs[nm] = _adamw_layer(nm, l, weights[nm], reduced[l][nm], moms[nm], vels[nm], outs[nm], after)
                after = outs[nm][0]
        return after

    def zero_of(toks, st):
        return sum(toks) if toks else st['token']

    last = 3 * DEPTH
    toks = run_hook(last, small_st['token'])
    done = adamw_layers(range(DEPTH - 1, 0, -1), LAYER_W, zero_of(toks, small_st))
    toks = run_hook(last + 1, done)
    done = adamw_layers([0], FFN_W, zero_of(toks, small_st))
    tot = _sum8(_small_gather_finish(small_st, done), name="sum_small_grads", tr=512)
    run_hook(last + 2, tot)
    assert not active
    adamw_layers([0], MIX_W, None)
    tot = tot.reshape(-1)
    grads, off = {}, 0
    for nm in SMALL_ORDER:
        shape = (DEPTH, 4, D_MODEL) if nm == 'norm_gains' else (N_A, MAIN_W) if nm == 'pool_scale' else weights[nm].shape
        n = 1
        for dim in shape:
            n *= dim
        grads[nm] = tot[off:off + n].reshape(shape)
        off += n
    grads['norm_gains'] = lax.dynamic_slice(grads['norm_gains'], (0, 0, s * 256), (DEPTH, 4, 256))
    grads['pool_scale'] = lax.dynamic_slice(grads['pool_scale'], (0, s * POOL_GROUP), (N_A, POOL_GROUP))
    grads['w_kv'] = reduced[N_A]['w_kv'].reshape(w_kv.shape)

    order = ('norm_gains', 'mem_norm', 'w_in', 'w_mem_kv', 'w_out', 'w_pool', 'pool_scale', 'kv_norm', 'w_kv',
             'w_gate_up', 'w_down')
    deltas, new_m, new_v = {}, {}, {}
    for nm in order:
        if nm in LAYER_W:
            deltas[nm], new_m[nm], new_v[nm], grads[nm] = outs[nm]
        else:
            deltas[nm], new_m[nm], new_v[nm] = _adamw(weights[nm], grads[nm], moms[nm], vels[nm], name="adamw_" + nm)
    return (loss, gx.reshape(x.shape), *[grads[nm] for nm in order], *[deltas[nm] for nm in order],
            *[new_m[nm] for nm in order], *[new_v[nm] for nm in order])
```
